```python
import math
import jax, jax.numpy as jnp
from jax import lax
import numpy as np

D_MODEL = 1024
BATCH = 8
SEQ = 8192
DEPTH = 1

HEAD_DIM = 64
FOX_HEADS = (D_MODEL // 2) // HEAD_DIM
SWA_HEADS = (D_MODEL // 2) // HEAD_DIM
SWA_KV_HEADS = max(1, SWA_HEADS // 4)
SWA_GROUP = SWA_HEADS // SWA_KV_HEADS
FOX_WIDTH = FOX_HEADS * HEAD_DIM
SWA_WIDTH = SWA_HEADS * HEAD_DIM
SWA_KV_WIDTH = SWA_KV_HEADS * HEAD_DIM
D_MIX = FOX_WIDTH + SWA_WIDTH
BLOCK = 128
WINDOW = 128
NUM_BUCKETS = 32
MAX_DISTANCE = 128
LN_EPS = 1e-5
NEG_INF = -1e30

_SPLIT_SIZES = (
    FOX_WIDTH,
    FOX_WIDTH,
    FOX_WIDTH,
    FOX_HEADS,
    FOX_WIDTH,
    SWA_WIDTH,
    SWA_KV_WIDTH,
    SWA_KV_WIDTH,
    SWA_WIDTH,
)
D_IN = sum(_SPLIT_SIZES)
SPLIT_POINTS = [int(v) for v in np.cumsum(_SPLIT_SIZES)[:-1]]

kernel_name = "hymba_fox_swa_sink_deepnorm"


def layer_norm(x, g, b):
    xf = x.astype(jnp.float32)
    mu = jnp.mean(xf, axis=-1, keepdims=True)
    var = jnp.mean(jnp.square(xf - mu), axis=-1, keepdims=True)
    y = (xf - mu) * lax.rsqrt(var + LN_EPS)
    return (y * g.astype(jnp.float32) + b.astype(jnp.float32)).astype(x.dtype)


def t5_causal_bucket(rel):
    max_exact = NUM_BUCKETS // 2
    is_small = rel < max_exact
    relf = jnp.maximum(rel, 1).astype(jnp.float32)
    large = max_exact + (jnp.log(relf / max_exact) / math.log(MAX_DISTANCE / max_exact)
                         * (NUM_BUCKETS - max_exact)).astype(jnp.int32)
    large = jnp.minimum(large, NUM_BUCKETS - 1)
    return jnp.where(is_small, rel, large)


def forgetting_attention(q, k, v, log_f):
    B, S, H, dh = q.shape
    nb = S // BLOCK
    scale = 1.0 / math.sqrt(dh)
    cum = lax.cumsum(log_f, axis=1)
    cum_k = jnp.transpose(cum, (0, 2, 1))
    q_blocks = jnp.transpose(q.reshape(B, nb, BLOCK, H, dh), (1, 0, 2, 3, 4))
    c_blocks = jnp.transpose(cum.reshape(B, nb, BLOCK, H), (1, 0, 3, 2))
    kpos = jnp.arange(S)

    def one_block(args):
        q_blk, c_blk, i = args
        s = jnp.einsum('bqhd,bkhd->bhqk', q_blk, k,
                       preferred_element_type=jnp.float32) * scale
        s = s + (c_blk[..., :, None] - cum_k[..., None, :])
        qpos = i * BLOCK + jnp.arange(BLOCK)
        causal = kpos[None, :] <= qpos[:, None]
        s = jnp.where(causal[None, None], s, NEG_INF)
        p = jax.nn.softmax(s, axis=-1)
        return jnp.einsum('bhqk,bkhd->bqhd', p.astype(v.dtype), v)

    out = lax.map(one_block, (q_blocks, c_blocks, jnp.arange(nb)))
    return jnp.transpose(out, (1, 0, 2, 3, 4)).reshape(B, S, H * dh)


def swa_sink_attention(q, k, v, rel_bias, sink):
    B, S, Hq, dh = q.shape
    nb = S // BLOCK
    scale = 1.0 / math.sqrt(dh)
    qb = q.reshape(B, nb, BLOCK, SWA_KV_HEADS, SWA_GROUP, dh)
    pad = jnp.zeros((B, BLOCK, SWA_KV_HEADS, dh), k.dtype)
    kp = jnp.concatenate([pad, k], axis=1).reshape(B, nb + 1, BLOCK, SWA_KV_HEADS, dh)
    vp = jnp.concatenate([pad, v], axis=1).reshape(B, nb + 1, BLOCK, SWA_KV_HEADS, dh)
    kb = jnp.concatenate([kp[:, :-1], kp[:, 1:]], axis=2)
    vb = jnp.concatenate([vp[:, :-1], vp[:, 1:]], axis=2)

    scores = jnp.einsum('bnqkgd,bnskd->bnkgqs', qb, kb,
                        preferred_element_type=jnp.float32) * scale

    qi = jnp.arange(BLOCK)[:, None]
    kj = jnp.arange(2 * BLOCK)[None, :]
    rel = qi + BLOCK - kj
    band = (rel >= 0) & (rel < WINDOW)
    s_abs = (jnp.arange(nb)[:, None] - 1) * BLOCK + jnp.arange(2 * BLOCK)[None, :]
    mask = band[None] & (s_abs >= 0)[:, None, :]

    bucket = t5_causal_bucket(jnp.maximum(rel, 0))
    bias = jnp.transpose(rel_bias.astype(jnp.float32)[bucket], (2, 0, 1))
    bias = bias.reshape(SWA_KV_HEADS, SWA_GROUP, BLOCK, 2 * BLOCK)
    scores = scores + bias[None, None]
    scores = jnp.where(mask[None, :, None, None], scores, NEG_INF)

    sink_logit = jnp.broadcast_to(
        sink.astype(jnp.float32).reshape(1, 1, SWA_KV_HEADS, SWA_GROUP, 1, 1),
        scores.shape[:-1] + (1,))
    probs = jax.nn.softmax(jnp.concatenate([scores, sink_logit], axis=-1), axis=-1)[..., :-1]
    out = jnp.einsum('bnkgqs,bnskd->bnqkgd', probs.astype(v.dtype), vb)
    return out.reshape(B, S, Hq * dh)


def _fwd_setup_inputs(seed: int = 0) -> dict:
    key = jax.random.key(seed)
    ks = jax.random.split(key, 8)
    beta = (8.0 * DEPTH) ** -0.25
    x = jax.random.normal(ks[0], (BATCH, SEQ, D_MODEL), jnp.float32)
    w_in = jax.random.normal(ks[1], (DEPTH, D_MODEL, D_IN), jnp.float32) * D_MODEL ** -0.5
    b_f = jax.random.uniform(ks[2], (DEPTH, FOX_HEADS), jnp.float32, 1.0, 5.0)
    rel_bias = jax.random.normal(ks[3], (NUM_BUCKETS, SWA_HEADS), jnp.float32) * 0.5
    sink = jax.random.normal(ks[4], (DEPTH, SWA_HEADS), jnp.float32) * 0.5
    w_o = jax.random.normal(ks[5], (DEPTH, D_MIX, D_MODEL), jnp.float32) * (D_MIX ** -0.5) * beta
    ln_g = 1.0 + 0.02 * jax.random.normal(ks[6], (DEPTH, D_MODEL), jnp.float32)
    ln_b = 0.02 * jax.random.normal(ks[7], (DEPTH, D_MODEL), jnp.float32)
    return {"x": x, "w_in": w_in, "b_f": b_f, "rel_bias": rel_bias, "sink": sink,
            "w_o": w_o, "ln_g": ln_g, "ln_b": ln_b}


def _fwd_reference(x, w_in, b_f, rel_bias, sink, w_o, ln_g, ln_b):
    alpha = (2.0 * DEPTH) ** 0.25
    B, S, _ = x.shape
    for l in range(DEPTH):
        proj = jnp.einsum('bsd,de->bse', x, w_in[l])
        fq, fk, fv, ff, fz, sq, sk, sv, sz = jnp.split(proj, SPLIT_POINTS, axis=-1)
        log_f = jax.nn.log_sigmoid(ff.astype(jnp.float32) + b_f[l].astype(jnp.float32))
        fox = forgetting_attention(
            fq.reshape(B, S, FOX_HEADS, HEAD_DIM),
            fk.reshape(B, S, FOX_HEADS, HEAD_DIM),
            fv.reshape(B, S, FOX_HEADS, HEAD_DIM),
            log_f)
        swa = swa_sink_attention(
            sq.reshape(B, S, SWA_HEADS, HEAD_DIM),
            sk.reshape(B, S, SWA_KV_HEADS, HEAD_DIM),
            sv.reshape(B, S, SWA_KV_HEADS, HEAD_DIM),
            rel_bias, sink[l])
        mixed = jnp.concatenate([fox * jax.nn.silu(fz), swa * jax.nn.silu(sz)], axis=-1)
        y = jnp.einsum('bse,ed->bsd', mixed, w_o[l])
        x = layer_norm(alpha * x + y, ln_g[l], ln_b[l])
    return x


import jax as _jax
import jax.numpy as _jnp

TWIN_FORMAT = 'train_step'
FWD_PARAMS = ['x', 'w_in', 'b_f', 'rel_bias', 'sink', 'w_o', 'ln_g', 'ln_b']
TWIN_WEIGHTS = ['w_in', 'b_f', 'rel_bias', 'sink', 'w_o', 'ln_g', 'ln_b']
TWIN_DIFF_INPUT = 'x'
TWIN_INPUTS = ['x', 'w_in', 'b_f', 'rel_bias', 'sink', 'w_o', 'ln_g', 'ln_b', 'loss_target', 'm_w_in', 'm_b_f', 'm_rel_bias', 'm_sink', 'm_w_o', 'm_ln_g', 'm_ln_b', 'v_w_in', 'v_b_f', 'v_rel_bias', 'v_sink', 'v_w_o', 'v_ln_g', 'v_ln_b']
TWIN_OUTPUTS = ['loss', 'grad_x', 'grad_w_in', 'grad_b_f', 'grad_rel_bias', 'grad_sink', 'grad_w_o', 'grad_ln_g', 'grad_ln_b', 'delta_w_in', 'delta_b_f', 'delta_rel_bias', 'delta_sink', 'delta_w_o', 'delta_ln_g', 'delta_ln_b', 'new_m_w_in', 'new_m_b_f', 'new_m_rel_bias', 'new_m_sink', 'new_m_w_o', 'new_m_ln_g', 'new_m_ln_b', 'new_v_w_in', 'new_v_b_f', 'new_v_rel_bias', 'new_v_sink', 'new_v_w_o', 'new_v_ln_g', 'new_v_ln_b']
TWIN_LEAF_KINDS = {'loss': 'loss', 'grad_x': 'grad_x', 'grad_w_in': 'grad_w', 'grad_b_f': 'grad_w', 'grad_rel_bias': 'grad_w', 'grad_sink': 'grad_w', 'grad_w_o': 'grad_w', 'grad_ln_g': 'grad_w', 'grad_ln_b': 'grad_w', 'delta_w_in': 'delta_w', 'delta_b_f': 'delta_w', 'delta_rel_bias': 'delta_w', 'delta_sink': 'delta_w', 'delta_w_o': 'delta_w', 'delta_ln_g': 'delta_w', 'delta_ln_b': 'delta_w', 'new_m_w_in': 'new_m', 'new_m_b_f': 'new_m', 'new_m_rel_bias': 'new_m', 'new_m_sink': 'new_m', 'new_m_w_o': 'new_m', 'new_m_ln_g': 'new_m', 'new_m_ln_b': 'new_m', 'new_v_w_in': 'new_v', 'new_v_b_f': 'new_v', 'new_v_rel_bias': 'new_v', 'new_v_sink': 'new_v', 'new_v_w_o': 'new_v', 'new_v_ln_g': 'new_v', 'new_v_ln_b': 'new_v'}


def _forward(args):
    return _fwd_reference(*[args[k] for k in FWD_PARAMS])


def _output_shape():
    def fwd():
        inp = _fwd_setup_inputs(0)
        return _fwd_reference(*[inp[k] for k in FWD_PARAMS])
    out = _jax.eval_shape(fwd)
    return out.shape, out.dtype

N_MICROBATCH = 1
ADAM_LR = 0.001
ADAM_B1 = 0.9
ADAM_B2 = 0.999
ADAM_EPS = 1e-08
ADAM_WD = 0.01
ADAM_STEP = 10
PER_EXAMPLE_BATCH_AXIS = {'x': 0, 'loss_target': 0}
SHARED_INPUTS = []
_WEIGHT_DTYPES = {'w_in': _jnp.float32, 'b_f': _jnp.float32, 'rel_bias': _jnp.float32, 'sink': _jnp.float32, 'w_o': _jnp.float32, 'ln_g': _jnp.float32, 'ln_b': _jnp.float32}
MOMENT_SCALE = {'w_in': 2.103967e-02, 'b_f': 8.050606e-02, 'rel_bias': 1.988650e-02, 'sink': 1.270819e-02, 'w_o': 3.458507e-02, 'ln_g': 6.397727e+01, 'ln_b': 9.251637e-01}


def _to_microbatches(a, axis):
    t = _jnp.moveaxis(a, axis, 0)
    t = t.reshape((N_MICROBATCH, t.shape[0] // N_MICROBATCH) + t.shape[1:])
    return _jnp.moveaxis(t, 1, axis + 1)


def setup_inputs(seed: int = 0) -> dict:
    inp = _fwd_setup_inputs(seed)
    key = _jax.random.fold_in(_jax.random.key(seed), 7919)
    shape, _ = _output_shape()
    out = dict(inp)
    out["loss_target"] = _jax.random.normal(_jax.random.fold_in(key, 0), shape, _jnp.float32)
    for i, name in enumerate(TWIN_WEIGHTS):
        w = inp[name].astype(_jnp.float32)
        if MOMENT_SCALE is None:
            s = _jnp.sqrt(_jnp.mean(_jnp.square(w)) + 1e-30)
        else:
            s = MOMENT_SCALE[name]
        km, kv = _jax.random.split(_jax.random.fold_in(key, i + 1))
        out[name] = w
        out["m_" + name] = s * _jax.random.normal(km, w.shape, _jnp.float32)
        out["v_" + name] = (s * s) * _jax.random.uniform(kv, w.shape, _jnp.float32, 0.5, 1.5)
    if N_MICROBATCH > 1:
        for name, axis in PER_EXAMPLE_BATCH_AXIS.items():
            out[name] = _to_microbatches(out[name], axis)
    return {'x': out['x'], 'w_in': out['w_in'], 'b_f': out['b_f'], 'rel_bias': out['rel_bias'], 'sink': out['sink'], 'w_o': out['w_o'], 'ln_g': out['ln_g'], 'ln_b': out['ln_b'], 'loss_target': out['loss_target'], 'm_w_in': out['m_w_in'], 'm_b_f': out['m_b_f'], 'm_rel_bias': out['m_rel_bias'], 'm_sink': out['m_sink'], 'm_w_o': out['m_w_o'], 'm_ln_g': out['m_ln_g'], 'm_ln_b': out['m_ln_b'], 'v_w_in': out['v_w_in'], 'v_b_f': out['v_b_f'], 'v_rel_bias': out['v_rel_bias'], 'v_sink': out['v_sink'], 'v_w_o': out['v_w_o'], 'v_ln_g': out['v_ln_g'], 'v_ln_b': out['v_ln_b']}


def _loss(weights, diff, rest, loss_target):
    with _jax.named_scope("forward"):
        args = {**rest, TWIN_DIFF_INPUT: diff, **{k: w.astype(_WEIGHT_DTYPES[k]) for k, w in weights.items()}}
        y = _forward(args)
    with _jax.named_scope("loss_head"):
        err = _jnp.square(y.astype(_jnp.float32) - loss_target)
        return 0.5 * _jnp.sum(_jnp.mean(err, axis=-1)) if err.ndim else 0.5 * err


def _adamw(w, g, m, v):
    m = ADAM_B1 * m + (1.0 - ADAM_B1) * g
    v = ADAM_B2 * v + (1.0 - ADAM_B2) * _jnp.square(g)
    m_hat = m / (1.0 - ADAM_B1 ** ADAM_STEP)
    v_hat = v / (1.0 - ADAM_B2 ** ADAM_STEP)
    delta = -ADAM_LR * (m_hat / (_jnp.sqrt(v_hat) + ADAM_EPS) + ADAM_WD * w)
    return delta, m, v


def reference(x, w_in, b_f, rel_bias, sink, w_o, ln_g, ln_b, loss_target, m_w_in, m_b_f, m_rel_bias, m_sink, m_w_o, m_ln_g, m_ln_b, v_w_in, v_b_f, v_rel_bias, v_sink, v_w_o, v_ln_g, v_ln_b):
    given = dict(x=x, w_in=w_in, b_f=b_f, rel_bias=rel_bias, sink=sink, w_o=w_o, ln_g=ln_g, ln_b=ln_b, loss_target=loss_target, m_w_in=m_w_in, m_b_f=m_b_f, m_rel_bias=m_rel_bias, m_sink=m_sink, m_w_o=m_w_o, m_ln_g=m_ln_g, m_ln_b=m_ln_b, v_w_in=v_w_in, v_b_f=v_b_f, v_rel_bias=v_rel_bias, v_sink=v_sink, v_w_o=v_w_o, v_ln_g=v_ln_g, v_ln_b=v_ln_b)
    weights = {n: given[n] for n in TWIN_WEIGHTS}
    shared = {n: given[n] for n in SHARED_INPUTS}
    per_example = {n: given[n] for n in ['x']}
    grad_fn = _jax.value_and_grad(_loss, argnums=(0, 1))

    def one_microbatch(ex, loss_target):
        ex = dict(ex)
        diff = ex.pop(TWIN_DIFF_INPUT)
        return grad_fn(weights, diff, {**shared, **ex}, loss_target)

    if N_MICROBATCH == 1:
        loss, (grad_w, grad_x) = one_microbatch(per_example, given["loss_target"])
    else:
        def body(carry, xs):
            loss_sum, grad_sum = carry
            l_k, (gw_k, gx_k) = one_microbatch(xs[0], xs[1])
            with _jax.named_scope("update"):
                return (loss_sum + l_k, _jax.tree.map(_jnp.add, grad_sum, gw_k)), gx_k

        init = (_jnp.zeros((), _jnp.float32), _jax.tree.map(_jnp.zeros_like, weights))
        (loss, grad_w), grad_x = _jax.lax.scan(body, init, (per_example, given["loss_target"]))
    with _jax.named_scope("update"):
        delta_w, new_m, new_v = {}, {}, {}
        for n in TWIN_WEIGHTS:
            delta_w[n], new_m[n], new_v[n] = _adamw(weights[n], grad_w[n], given["m_" + n], given["v_" + n])
    return (loss, grad_x, *[grad_w[n] for n in TWIN_WEIGHTS], *[delta_w[n] for n in TWIN_WEIGHTS],
            *[new_m[n] for n in TWIN_WEIGHTS], *[new_v[n] for n in TWIN_WEIGHTS])
```

```python
import functools
import math

import numpy as np
import jax
import jax.numpy as jnp
from jax import lax
from jax.experimental import pallas as pl
from jax.experimental.pallas import tpu as pltpu

F32 = jnp.float32
BF16 = jnp.bfloat16

D_MODEL = 1024
HEAD_DIM = 64
FOX_HEADS = 8
SWA_HEADS = 8
SWA_KV_HEADS = 2
SWA_GROUP = 4
FOX_W = 512
SWA_W = 512
SWA_KV_W = 128
BLOCK = 128
NUM_BUCKETS = 32
MAX_DISTANCE = 128
LN_EPS = 1e-5
NEG = -1e30
ALPHA = 2.0 ** 0.25
QK_SCALE = 0.125

ADAM_LR = 0.001
ADAM_B1 = 0.9
ADAM_B2 = 0.999
ADAM_EPS = 1e-08
ADAM_WD = 0.01
ADAM_STEP = 10

D_IN = 3336
N_A = 2304
N_C = 256
N_B = 1024
OFF_C = N_A
OFF_B = N_A + N_C
N_PAD = N_A + N_C + N_B
COL_FK, COL_FV, COL_SQ, COL_SK, COL_SV = 512, 1024, 1536, 2048, 2176

LANES = 128
FOX_T = 256
VMEM_LIMIT = 56 * 1024 * 1024

MESH = pl.DeviceIdType.MESH
N_CHIPS = 4
N_DEV = 8
SMALL_ROWS = 24


def _cparams(sem=None):
    return pltpu.CompilerParams(dimension_semantics=sem, vmem_limit_bytes=VMEM_LIMIT)


def _split3(x):
    hi = x.astype(BF16)
    r = x - hi.astype(F32)
    mid = r.astype(BF16)
    lo = (r - mid.astype(F32)).astype(BF16)
    return hi, mid, lo


def _dot(a, b):
    return jnp.dot(a, b, preferred_element_type=F32)


def _dot_nt(a, b):
    return lax.dot_general(a, b, (((1,), (1,)), ((), ())), preferred_element_type=F32)


def _dot_tn(a, b):
    return lax.dot_general(a, b, (((0,), (0,)), ((), ())), preferred_element_type=F32)


def _matmul_nn(a, b, *, n_off, n_out, tm, tn, out_dtype, name):
    m, k = a.shape
    joff = n_off // tn

    def kern(a_ref, b_ref, o_ref):
        o_ref[...] = _dot(a_ref[...], b_ref[...]).astype(o_ref.dtype)

    return pl.pallas_call(
        kern, name=name,
        grid=(n_out // tn, m // tm),
        in_specs=[pl.BlockSpec((tm, k), lambda j, i: (i, 0)),
                  pl.BlockSpec((k, tn), lambda j, i: (0, j + joff))],
        out_specs=pl.BlockSpec((tm, tn), lambda j, i: (i, j)),
        out_shape=jax.ShapeDtypeStruct((m, n_out), out_dtype),
        compiler_params=_cparams(("parallel", "parallel")),
    )(a, b)


def _grad_x_matmul(dproj, w_pad, dh, *, tm, tn, name):
    m, k = dproj.shape
    n = w_pad.shape[0]

    def kern(a_ref, b_ref, dh_ref, o_ref):
        o_ref[...] = ALPHA * dh_ref[...] + _dot_nt(a_ref[...], b_ref[...])

    return pl.pallas_call(
        kern, name=name,
        grid=(n // tn, m // tm),
        in_specs=[pl.BlockSpec((tm, k), lambda j, i: (i, 0)),
                  pl.BlockSpec((tn, k), lambda j, i: (j, 0)),
                  pl.BlockSpec((tm, tn), lambda j, i: (i, j))],
        out_specs=pl.BlockSpec((tm, tn), lambda j, i: (i, j)),
        out_shape=jax.ShapeDtypeStruct((m, n), F32),
        compiler_params=_cparams(("parallel", "parallel")),
    )(dproj, w_pad, dh)


def _matmul_tn(a, b, *, tm, tn, tk, name):
    s, m = a.shape
    n = b.shape[1]

    def kern(a_ref, b_ref, o_ref):
        @pl.when(pl.program_id(2) == 0)
        def _():
            o_ref[...] = jnp.zeros_like(o_ref)
        o_ref[...] += _dot_tn(a_ref[...], b_ref[...])

    return pl.pallas_call(
        kern, name=name,
        grid=(m // tm, n // tn, s // tk),
        in_specs=[pl.BlockSpec((tk, tm), lambda i, j, k: (k, i)),
                  pl.BlockSpec((tk, tn), lambda i, j, k: (k, j))],
        out_specs=pl.BlockSpec((tm, tn), lambda i, j, k: (i, j)),
        out_shape=jax.ShapeDtypeStruct((m, n), F32),
        compiler_params=_cparams(("parallel", "parallel", "arbitrary")),
    )(a, b)


def _tri(n, lower):
    r = lax.broadcasted_iota(jnp.int32, (n, n), 0)
    c = lax.broadcasted_iota(jnp.int32, (n, n), 1)
    keep = (c <= r) if lower else (c >= r)
    return jnp.where(keep, 1.0, 0.0).astype(BF16)


def _expand_matrix():
    r = lax.broadcasted_iota(jnp.int32, (LANES, FOX_W), 0)
    c = lax.broadcasted_iota(jnp.int32, (LANES, FOX_W), 1)
    return jnp.where(r == c // HEAD_DIM, 1.0, 0.0).astype(BF16)


def _exact_dot(mat_bf16, x_f32, left):
    out = None
    for piece in _split3(x_f32):
        t = _dot(mat_bf16, piece) if left else _dot(piece, mat_bf16)
        out = t if out is None else out + t
    return out


def _log_sigmoid(z):
    return jnp.minimum(z, 0.0) - jnp.log(1.0 + jnp.exp(-jnp.abs(z)))


def _cum_fwd(ffp, bfp):
    s = ffp.shape[0]
    t = min(256, s)

    def kern(ff_ref, b_ref, cum_ref, cumb_ref, carry_ref):
        @pl.when(pl.program_id(0) == 0)
        def _():
            carry_ref[...] = jnp.zeros_like(carry_ref)
        lane = lax.broadcasted_iota(jnp.int32, (1, LANES), 1)
        lf = _log_sigmoid(ff_ref[...] + b_ref[...])
        lf = jnp.where(lane < FOX_HEADS, lf, 0.0)
        cum = _exact_dot(_tri(t, True), lf, True) + carry_ref[0:1, :]
        cum_ref[...] = cum
        carry_ref[...] = jnp.broadcast_to(cum[t - 1:t, :], carry_ref.shape)
        cumb_ref[...] = _exact_dot(_expand_matrix(), cum, False)

    return pl.pallas_call(
        kern, name="cum_fwd",
        grid=(s // t,),
        in_specs=[pl.BlockSpec((t, LANES), lambda i: (i, 0)),
                  pl.BlockSpec((1, LANES), lambda i: (0, 0))],
        out_specs=[pl.BlockSpec((t, LANES), lambda i: (i, 0)),
                   pl.BlockSpec((t, FOX_W), lambda i: (i, 0))],
        out_shape=[jax.ShapeDtypeStruct((s, LANES), F32),
                   jax.ShapeDtypeStruct((s, FOX_W), F32)],
        scratch_shapes=[pltpu.VMEM((8, LANES), F32)],
        compiler_params=_cparams(("arbitrary",)),
    )(ffp, bfp)


def _cum_bwd(dcum_k, dcum_q, ffp, bfp):
    s = dcum_k.shape[0]
    t = min(256, s)
    nb = s // t

    def kern(dck_ref, dcq_ref, ff_ref, b_ref, dff_ref, gb_ref, carry_ref):
        @pl.when(pl.program_id(0) == 0)
        def _():
            carry_ref[...] = jnp.zeros_like(carry_ref)
            gb_ref[...] = jnp.zeros_like(gb_ref)
        lane = lax.broadcasted_iota(jnp.int32, (1, LANES), 1)
        dlf = _exact_dot(_tri(t, False), dck_ref[...] + dcq_ref[...], True) + carry_ref[0:1, :]
        carry_ref[...] = jnp.broadcast_to(dlf[0:1, :], carry_ref.shape)
        z = ff_ref[...] + b_ref[...]
        dff = jnp.where(lane < FOX_HEADS, dlf / (1.0 + jnp.exp(z)), 0.0)
        gb_ref[...] += jnp.broadcast_to(jnp.sum(dff, axis=0, keepdims=True), gb_ref.shape)
        dff_ref[...] = jnp.concatenate([dff, jnp.zeros_like(dff)], axis=1).astype(BF16)

    return pl.pallas_call(
        kern, name="cum_bwd",
        grid=(nb,),
        in_specs=[pl.BlockSpec((t, LANES), lambda i: (nb - 1 - i, 0)),
                  pl.BlockSpec((t, LANES), lambda i: (nb - 1 - i, 0)),
                  pl.BlockSpec((t, LANES), lambda i: (nb - 1 - i, 0)),
                  pl.BlockSpec((1, LANES), lambda i: (0, 0))],
        out_specs=[pl.BlockSpec((t, N_C), lambda i: (nb - 1 - i, 0)),
                   pl.BlockSpec((8, LANES), lambda i: (0, 0))],
        out_shape=[jax.ShapeDtypeStruct((s, N_C), BF16),
                   jax.ShapeDtypeStruct((8, LANES), F32)],
        scratch_shapes=[pltpu.VMEM((8, LANES), F32)],
        compiler_params=_cparams(("arbitrary",)),
    )(dcum_k, dcum_q, ffp, bfp)


def _fox_fwd(qkv, cum_t3):
    s = qkv.shape[0]
    t = min(FOX_T, s)
    nq = s // t

    def kern(q_ref, k_ref, v_ref, ct_ref, o_ref, lse_ref):
        i = pl.program_id(1)
        lane = lax.broadcasted_iota(jnp.int32, (1, LANES), 1)
        row = lax.broadcasted_iota(jnp.int32, (t, t), 0)
        col = lax.broadcasted_iota(jnp.int32, (t, t), 1)
        causal = col <= row
        q2 = q_ref[...] * jnp.asarray(QK_SCALE, BF16)
        q0 = pl.multiple_of(i * t, t)
        res = []
        for a in range(2):
            sel = (lane < HEAD_DIM) if a == 0 else (lane >= HEAD_DIM)
            qa = jnp.where(sel, q2, jnp.zeros_like(q2))
            cref = ct_ref[a, :, pl.ds(q0, LANES)][:, 0:1]

            def tile(j, carry, masked, a=a, qa=qa, cref=cref):
                m, l, acc = carry
                k0 = pl.multiple_of(j * t, t)
                k2 = k_ref[pl.ds(k0, t), :]
                v2 = v_ref[pl.ds(k0, t), :]
                u = _dot_nt(qa, k2) - (ct_ref[a, :, pl.ds(k0, t)] - cref)
                if masked:
                    u = jnp.where(causal, u, NEG)
                m_new = jnp.maximum(m, jnp.max(u, axis=1, keepdims=True))
                scale = jnp.exp(m - m_new)
                p = jnp.exp(u - m_new)
                l = scale * l + jnp.sum(p, axis=1, keepdims=True)
                acc = scale * acc + _dot(p.astype(BF16), v2)
                return m_new, l, acc

            init = (jnp.full((t, 1), NEG, F32), jnp.zeros((t, 1), F32), jnp.zeros((t, LANES), F32))
            carry = lax.fori_loop(0, i, functools.partial(tile, masked=False), init)
            m, l, acc = tile(i, carry, True)
            res.append((acc / l, m + jnp.log(l)))
        o_ref[...] = jnp.where(lane < HEAD_DIM, res[0][0], res[1][0])
        lse_ref[...] = jnp.where(lane < HEAD_DIM, res[0][1], res[1][1])

    return pl.pallas_call(
        kern, name="fox_fwd",
        grid=(FOX_HEADS // 2, nq),
        in_specs=[pl.BlockSpec((t, LANES), lambda hp, i: (i, hp)),
                  pl.BlockSpec((s, LANES), lambda hp, i: (0, COL_FK // LANES + hp)),
                  pl.BlockSpec((s, LANES), lambda hp, i: (0, COL_FV // LANES + hp)),
                  pl.BlockSpec((2, 1, s), lambda hp, i: (hp, 0, 0))],
        out_specs=[pl.BlockSpec((t, LANES), lambda hp, i: (i, hp)),
                   pl.BlockSpec((t, LANES), lambda hp, i: (i, hp))],
        out_shape=[jax.ShapeDtypeStruct((s, FOX_W), F32),
                   jax.ShapeDtypeStruct((s, FOX_W), F32)],
        compiler_params=_cparams(("parallel", "parallel")),
    )(qkv, qkv, qkv, cum_t3)


def _fox_bwd(qkv, do_bf, cum_t3, cum_b, lse_t3, delta_t3):
    s = qkv.shape[0]
    t = min(FOX_T, s)
    nq = s // t

    def kern(q_ref, do_ref, k_ref, v_ref, ct_ref, cb_ref, lse_ref, dl_ref,
             dq_ref, dk_ref, dv_ref, dc_ref, dcq_ref, dqt_ref, accv_ref, acck_ref, accd_ref):
        kj = pl.program_id(1)
        lane = lax.broadcasted_iota(jnp.int32, (1, LANES), 1)
        sub = lax.broadcasted_iota(jnp.int32, (LANES, 1), 0)
        krow = lax.broadcasted_iota(jnp.int32, (t, t), 0)
        qcol = lax.broadcasted_iota(jnp.int32, (t, t), 1)
        causal = krow <= qcol
        sels = [lane < HEAD_DIM, lane >= HEAD_DIM]
        subsels = [sub < HEAD_DIM, sub >= HEAD_DIM]

        @pl.when(kj == 0)
        def _():
            dqt_ref[...] = jnp.zeros_like(dqt_ref)
            dcq_ref[...] = jnp.zeros_like(dcq_ref)

        accv_ref[...] = jnp.zeros_like(accv_ref)
        acck_ref[...] = jnp.zeros_like(acck_ref)
        accd_ref[...] = jnp.zeros_like(accd_ref)
        k2 = k_ref[...]
        v2 = v_ref[...]
        kt = k2.astype(F32).T
        kts = [jnp.where(subsels[a], kt, 0.0).astype(BF16) for a in range(2)]
        cbs = [cb_ref[:, a * HEAD_DIM:a * HEAD_DIM + 1] for a in range(2)]

        def tile(i, masked):
            q0 = pl.multiple_of(i * t, t)
            qi = q_ref[pl.ds(q0, t), :] * jnp.asarray(QK_SCALE, BF16)
            doi = do_ref[pl.ds(q0, t), :]
            dqt = None
            for a in range(2):
                qm = jnp.where(sels[a], qi, jnp.zeros_like(qi))
                dom = jnp.where(sels[a], doi, jnp.zeros_like(doi))
                cref = ct_ref[a, :, pl.ds(q0, LANES)][:, 0:1]
                st = _dot_nt(k2, qm)
                dpt = _dot_nt(v2, dom)
                pt = jnp.exp(st - (cbs[a] - cref) - lse_ref[a, :, pl.ds(q0, t)])
                if masked:
                    pt = jnp.where(causal, pt, 0.0)
                ds32 = pt * (dpt - dl_ref[a, :, pl.ds(q0, t)])
                part = ds32[:, 0:LANES]
                for c in range(1, t // LANES):
                    part = part + ds32[:, c * LANES:(c + 1) * LANES]
                accd_ref[a] += part
                dcq_ref[a, :, pl.ds(q0, t)] += jnp.sum(ds32, axis=0, keepdims=True)
                dst = ds32.astype(BF16)
                accv_ref[...] += _dot(pt.astype(BF16), dom)
                acck_ref[...] += _dot(dst, qm)
                d = _dot(kts[a], dst)
                dqt = d if dqt is None else dqt + d
            dqt_ref[:, pl.ds(q0, t)] += dqt

        tile(kj, True)

        def body(i, c):
            tile(i, False)
            return c
        lax.fori_loop(kj + 1, nq, body, 0)

        dv_ref[...] = accv_ref[...].astype(BF16)
        dk_ref[...] = acck_ref[...].astype(BF16)
        dc_a = jnp.sum(accd_ref[0], axis=1, keepdims=True)
        dc_b = jnp.sum(accd_ref[1], axis=1, keepdims=True)
        dc_ref[...] = -jnp.where(sels[0], dc_a, dc_b)

        @pl.when(kj == nq - 1)
        def _():
            for c in range(nq):
                dq_ref[c * t:(c + 1) * t, :] = dqt_ref[:, c * t:(c + 1) * t].T * QK_SCALE

    hp_blk = lambda hp, kj: (hp, 0, 0)
    return pl.pallas_call(
        kern, name="fox_bwd",
        grid=(FOX_HEADS // 2, nq),
        in_specs=[pl.BlockSpec((s, LANES), lambda hp, kj: (0, hp)),
                  pl.BlockSpec((s, LANES), lambda hp, kj: (0, hp)),
                  pl.BlockSpec((t, LANES), lambda hp, kj: (kj, COL_FK // LANES + hp)),
                  pl.BlockSpec((t, LANES), lambda hp, kj: (kj, COL_FV // LANES + hp)),
                  pl.BlockSpec((2, 1, s), hp_blk),
                  pl.BlockSpec((t, LANES), lambda hp, kj: (kj, hp)),
                  pl.BlockSpec((2, 1, s), hp_blk),
                  pl.BlockSpec((2, 1, s), hp_blk)],
        out_specs=[pl.BlockSpec((s, LANES), lambda hp, kj: (0, hp)),
                   pl.BlockSpec((t, LANES), lambda hp, kj: (kj, hp)),
                   pl.BlockSpec((t, LANES), lambda hp, kj: (kj, hp)),
                   pl.BlockSpec((t, LANES), lambda hp, kj: (kj, hp)),
                   pl.BlockSpec((2, 1, s), hp_blk)],
        out_shape=[jax.ShapeDtypeStruct((s, FOX_W), F32),
                   jax.ShapeDtypeStruct((s, FOX_W), BF16),
                   jax.ShapeDtypeStruct((s, FOX_W), BF16),
                   jax.ShapeDtypeStruct((s, FOX_W), F32),
                   jax.ShapeDtypeStruct((FOX_HEADS, 1, s), F32)],
        scratch_shapes=[pltpu.VMEM((LANES, s), F32),
                        pltpu.VMEM((t, LANES), F32),
                        pltpu.VMEM((t, LANES), F32),
                        pltpu.VMEM((2, t, LANES), F32)],
        compiler_params=_cparams(("parallel", "arbitrary")),
    )(qkv, do_bf, qkv, qkv, cum_t3, cum_b, lse_t3, delta_t3)


def _bucket_table():
    qi = np.arange(BLOCK)[:, None]
    kj = np.arange(2 * BLOCK)[None, :]
    rel = np.maximum(qi + BLOCK - kj, 0).astype(np.int32)
    max_exact = NUM_BUCKETS // 2
    relf = np.maximum(rel, 1).astype(np.float32)
    large = max_exact + (np.log(relf / np.float32(max_exact)) / np.float32(math.log(MAX_DISTANCE / max_exact))
                         * np.float32(NUM_BUCKETS - max_exact)).astype(np.int32)
    large = np.minimum(large, NUM_BUCKETS - 1)
    return np.where(rel < max_exact, rel, large).astype(np.int32)


def _swa_bias(rel_bias, bucket):
    def kern(rb_ref, bk_ref, o_ref):
        bk = bk_ref[...]
        for h in range(SWA_HEADS):
            acc = jnp.zeros((BLOCK, 2 * BLOCK), F32)
            for b in range(NUM_BUCKETS):
                acc = jnp.where(bk == b, rb_ref[b, h], acc)
            o_ref[h] = acc

    return pl.pallas_call(
        kern, name="swa_bias",
        in_specs=[pl.BlockSpec(memory_space=pltpu.SMEM),
                  pl.BlockSpec(memory_space=pltpu.VMEM)],
        out_specs=pl.BlockSpec(memory_space=pltpu.VMEM),
        out_shape=jax.ShapeDtypeStruct((SWA_HEADS, BLOCK, 2 * BLOCK), F32),
        compiler_params=_cparams(),
    )(rel_bias, bucket)


def _swa_mask(n):
    qi = lax.broadcasted_iota(jnp.int32, (BLOCK, 2 * BLOCK), 0)
    kj = lax.broadcasted_iota(jnp.int32, (BLOCK, 2 * BLOCK), 1)
    rel = qi + BLOCK - kj
    band = (rel >= 0) & (rel < BLOCK)
    return band & ((kj >= BLOCK) | (n > 0))


def _swa_fwd(qkv, bias, sink):
    s = qkv.shape[0]
    nb = s // BLOCK

    def kern(q_ref, kp_ref, kc_ref, vp_ref, vc_ref, bias_ref, sink_ref, o_ref, lse_ref):
        n = pl.program_id(0)
        mask = _swa_mask(n)
        lane = lax.broadcasted_iota(jnp.int32, (1, LANES), 1)
        q = q_ref[...] * jnp.asarray(QK_SCALE, BF16)
        k = jnp.concatenate([kp_ref[...], kc_ref[...]], axis=0)
        v = jnp.concatenate([vp_ref[...], vc_ref[...]], axis=0)
        outs = []
        lse_all = jnp.zeros((BLOCK, LANES), F32)
        for h in range(SWA_HEADS):
            g = h // SWA_GROUP
            qh = q[:, h * HEAD_DIM:(h + 1) * HEAD_DIM]
            kg = k[:, g * HEAD_DIM:(g + 1) * HEAD_DIM]
            vg = v[:, g * HEAD_DIM:(g + 1) * HEAD_DIM]
            sc = jnp.where(mask, _dot_nt(qh, kg) + bias_ref[h], NEG)
            sk = sink_ref[0, h]
            m = jnp.maximum(jnp.max(sc, axis=1, keepdims=True), sk)
            p = jnp.exp(sc - m)
            l = jnp.sum(p, axis=1, keepdims=True) + jnp.exp(sk - m)
            probs = p * (1.0 / l)
            outs.append(_dot(probs.astype(BF16), vg))
            lse_all = jnp.where(lane == h, m + jnp.log(l), lse_all)
        o_ref[...] = jnp.concatenate(outs, axis=1)
        lse_ref[...] = lse_all

    cq, ck, cv = COL_SQ // SWA_W, COL_SK // LANES, COL_SV // LANES
    prev = lambda n: jnp.maximum(n - 1, 0)
    return pl.pallas_call(
        kern, name="swa_fwd",
        grid=(nb,),
        in_specs=[pl.BlockSpec((BLOCK, SWA_W), lambda n: (n, cq)),
                  pl.BlockSpec((BLOCK, LANES), lambda n: (prev(n), ck)),
                  pl.BlockSpec((BLOCK, LANES), lambda n: (n, ck)),
                  pl.BlockSpec((BLOCK, LANES), lambda n: (prev(n), cv)),
                  pl.BlockSpec((BLOCK, LANES), lambda n: (n, cv)),
                  pl.BlockSpec((SWA_HEADS, BLOCK, 2 * BLOCK), lambda n: (0, 0, 0)),
                  pl.BlockSpec(memory_space=pltpu.SMEM)],
        out_specs=[pl.BlockSpec((BLOCK, SWA_W), lambda n: (n, 0)),
                   pl.BlockSpec((BLOCK, LANES), lambda n: (n, 0))],
        out_shape=[jax.ShapeDtypeStruct((s, SWA_W), F32),
                   jax.ShapeDtypeStruct((s, LANES), F32)],
        compiler_params=_cparams(("parallel",)),
    )(qkv, qkv, qkv, qkv, qkv, bias, sink)


def _swa_bwd(qkv, do_bf, delta_b, lse, bias, sink, bucket):
    s = qkv.shape[0]
    nb = s // BLOCK

    def kern(q_ref, kp_ref, kc_ref, vp_ref, vc_ref, do_ref, dl_ref, lse_ref, bias_ref, sink_ref, bk_ref,
             dq_ref, dk_ref, dv_ref, grb_ref, gsk_ref, dbias_ref, ck_ref, cv_ref, sk_ref):
        n = pl.program_id(0)
        lane = lax.broadcasted_iota(jnp.int32, (1, LANES), 1)

        @pl.when(n == 0)
        def _():
            dbias_ref[...] = jnp.zeros_like(dbias_ref)
            ck_ref[...] = jnp.zeros_like(ck_ref)
            cv_ref[...] = jnp.zeros_like(cv_ref)
            sk_ref[...] = jnp.zeros_like(sk_ref)

        @pl.when(n < nb)
        def _():
            mask = _swa_mask(n)
            q = q_ref[...] * jnp.asarray(QK_SCALE, BF16)
            k = jnp.concatenate([kp_ref[...], kc_ref[...]], axis=0)
            v = jnp.concatenate([vp_ref[...], vc_ref[...]], axis=0)
            do = do_ref[...]
            dl = dl_ref[...]
            lse_all = lse_ref[...]
            dqs = []
            dks = [None] * SWA_KV_HEADS
            dvs = [None] * SWA_KV_HEADS
            gsk = jnp.zeros((1, LANES), F32)
            for h in range(SWA_HEADS):
                g = h // SWA_GROUP
                hs = slice(h * HEAD_DIM, (h + 1) * HEAD_DIM)
                gs = slice(g * HEAD_DIM, (g + 1) * HEAD_DIM)
                qh, doh = q[:, hs], do[:, hs]
                kg, vg = k[:, gs], v[:, gs]
                lse_h = lse_all[:, h:h + 1]
                dlt = dl[:, h * HEAD_DIM:h * HEAD_DIM + 1]
                sc = jnp.where(mask, _dot_nt(qh, kg) + bias_ref[h], NEG)
                p = jnp.exp(sc - lse_h)
                dp = _dot_nt(doh, vg)
                ds = p * (dp - dlt)
                dbias_ref[h] += ds
                p_sink = jnp.exp(sink_ref[0, h] - lse_h)
                gsk = gsk + jnp.where(lane == h, -jnp.sum(p_sink * dlt), 0.0)
                ds_bf = ds.astype(BF16)
                dqs.append(_dot(ds_bf, kg) * QK_SCALE)
                dk_h = _dot_tn(ds_bf, qh)
                dv_h = _dot_tn(p.astype(BF16), doh)
                dks[g] = dk_h if dks[g] is None else dks[g] + dk_h
                dvs[g] = dv_h if dvs[g] is None else dvs[g] + dv_h
            dq_ref[...] = jnp.concatenate(dqs, axis=1).astype(BF16)
            sk_ref[...] += jnp.broadcast_to(gsk, sk_ref.shape)
            dk2 = jnp.concatenate(dks, axis=1)
            dv2 = jnp.concatenate(dvs, axis=1)
            dk_ref[...] = (ck_ref[...] + dk2[:BLOCK]).astype(BF16)
            dv_ref[...] = (cv_ref[...] + dv2[:BLOCK]).astype(BF16)
            ck_ref[...] = dk2[BLOCK:]
            cv_ref[...] = dv2[BLOCK:]

        @pl.when(n == nb)
        def _():
            dk_ref[...] = ck_ref[...].astype(BF16)
            dv_ref[...] = cv_ref[...].astype(BF16)
            gsk_ref[...] = sk_ref[...]
            bk = bk_ref[...]
            rowi = lax.broadcasted_iota(jnp.int32, (NUM_BUCKETS, LANES), 0)
            lanei = lax.broadcasted_iota(jnp.int32, (NUM_BUCKETS, LANES), 1)
            out = jnp.zeros((NUM_BUCKETS, LANES), F32)
            for h in range(SWA_HEADS):
                db = dbias_ref[h]
                for b in range(NUM_BUCKETS):
                    val = jnp.sum(jnp.where(bk == b, db, 0.0))
                    out = jnp.where((rowi == b) & (lanei == h), val, out)
            grb_ref[...] = out

    cq, ck, cv = COL_SQ // SWA_W, COL_SK // LANES, COL_SV // LANES
    cur = lambda n: jnp.minimum(n, nb - 1)
    prev = lambda n: jnp.maximum(jnp.minimum(n, nb - 1) - 1, 0)
    kout = lambda n: jnp.maximum(n - 1, 0)
    return pl.pallas_call(
        kern, name="swa_bwd",
        grid=(nb + 1,),
        in_specs=[pl.BlockSpec((BLOCK, SWA_W), lambda n: (cur(n), cq)),
                  pl.BlockSpec((BLOCK, LANES), lambda n: (prev(n), ck)),
                  pl.BlockSpec((BLOCK, LANES), lambda n: (cur(n), ck)),
                  pl.BlockSpec((BLOCK, LANES), lambda n: (prev(n), cv)),
                  pl.BlockSpec((BLOCK, LANES), lambda n: (cur(n), cv)),
                  pl.BlockSpec((BLOCK, SWA_W), lambda n: (cur(n), 1)),
                  pl.BlockSpec((BLOCK, SWA_W), lambda n: (cur(n), 1)),
                  pl.BlockSpec((BLOCK, LANES), lambda n: (cur(n), 0)),
                  pl.BlockSpec((SWA_HEADS, BLOCK, 2 * BLOCK), lambda n: (0, 0, 0)),
                  pl.BlockSpec(memory_space=pltpu.SMEM),
                  pl.BlockSpec((BLOCK, 2 * BLOCK), lambda n: (0, 0))],
        out_specs=[pl.BlockSpec((BLOCK, SWA_W), lambda n: (cur(n), 0)),
                   pl.BlockSpec((BLOCK, LANES), lambda n: (kout(n), 0)),
                   pl.BlockSpec((BLOCK, LANES), lambda n: (kout(n), 0)),
                   pl.BlockSpec((NUM_BUCKETS, LANES), lambda n: (0, 0)),
                   pl.BlockSpec((8, LANES), lambda n: (0, 0))],
        out_shape=[jax.ShapeDtypeStruct((s, SWA_W), BF16),
                   jax.ShapeDtypeStruct((s, LANES), BF16),
                   jax.ShapeDtypeStruct((s, LANES), BF16),
                   jax.ShapeDtypeStruct((NUM_BUCKETS, LANES), F32),
                   jax.ShapeDtypeStruct((8, LANES), F32)],
        scratch_shapes=[pltpu.VMEM((SWA_HEADS, BLOCK, 2 * BLOCK), F32),
                        pltpu.VMEM((BLOCK, LANES), F32),
                        pltpu.VMEM((BLOCK, LANES), F32),
                        pltpu.VMEM((8, LANES), F32)],
        compiler_params=_cparams(("arbitrary",)),
    )(qkv, qkv, qkv, qkv, qkv, do_bf, delta_b, lse, bias, sink, bucket)


def _post(x, target, o_fox, o_swa, z, w_o, ln_g, ln_b):
    s = x.shape[0]
    tm = min(256, s)
    nt = s // tm
    seg = 256

    def kern(x_ref, t_ref, of_ref, os_ref, z_ref, w_ref, g_ref, b_ref,
             loss_ref, dh_ref, dy_ref, mix_ref, do_ref, dz_ref, dl_ref, gg_ref, gb_ref, lacc_ref):
        step = pl.program_id(0)

        @pl.when(step == 0)
        def _():
            lacc_ref[...] = jnp.zeros_like(lacc_ref)
            gg_ref[...] = jnp.zeros_like(gg_ref)
            gb_ref[...] = jnp.zeros_like(gb_ref)

        o = jnp.concatenate([of_ref[...], os_ref[...]], axis=1)
        zz = z_ref[...]
        sig = 1.0 / (1.0 + jnp.exp(-zz))
        silu = zz * sig
        mixed = (o * silu).astype(BF16)
        mix_ref[...] = mixed
        w = w_ref[...]
        h = ALPHA * x_ref[...] + _dot(mixed, w)
        mu = jnp.mean(h, axis=1, keepdims=True)
        hc = h - mu
        var = jnp.mean(hc * hc, axis=1, keepdims=True)
        rstd = lax.rsqrt(var + LN_EPS)
        xhat = hc * rstd
        g = g_ref[...]
        err = xhat * g + b_ref[...] - t_ref[...]
        lacc_ref[...] += jnp.broadcast_to(jnp.sum(err * err, axis=0, keepdims=True), lacc_ref.shape)
        dout = err * (1.0 / D_MODEL)
        gg_ref[...] += jnp.broadcast_to(jnp.sum(dout * xhat, axis=0, keepdims=True), gg_ref.shape)
        gb_ref[...] += jnp.broadcast_to(jnp.sum(dout, axis=0, keepdims=True), gb_ref.shape)
        dxh = dout * g
        m1 = jnp.mean(dxh, axis=1, keepdims=True)
        m2 = jnp.mean(dxh * xhat, axis=1, keepdims=True)
        dh = rstd * (dxh - m1 - xhat * m2)
        dh_ref[...] = dh
        dy = dh.astype(BF16)
        dy_ref[...] = dy
        dmix = _dot_nt(dy, w)
        do = dmix * silu
        do_ref[...] = do.astype(BF16)
        dz_ref[...] = (dmix * o * (sig * (1.0 + zz * (1.0 - sig)))).astype(BF16)
        r = lax.broadcasted_iota(jnp.int32, (seg, seg), 0) // HEAD_DIM
        c = lax.broadcasted_iota(jnp.int32, (seg, seg), 1) // HEAD_DIM
        bd = jnp.where(r == c, 1.0, 0.0).astype(BF16)
        prod = do * o
        parts = [_exact_dot(bd, prod[:, j * seg:(j + 1) * seg], False) for j in range(D_MODEL // seg)]
        dl_ref[...] = jnp.concatenate(parts, axis=1)

        @pl.when(step == nt - 1)
        def _():
            tot = jnp.sum(lacc_ref[0:1, :]) * (0.5 / D_MODEL)
            loss_ref[...] = jnp.broadcast_to(tot, loss_ref.shape)

    row = lambda i: (i, 0)
    fixed = lambda i: (0, 0)
    wide = pl.BlockSpec((tm, D_MODEL), row)
    half = pl.BlockSpec((tm, FOX_W), row)
    return pl.pallas_call(
        kern, name="post",
        grid=(nt,),
        in_specs=[wide, wide, half, half, wide,
                  pl.BlockSpec((D_MODEL, D_MODEL), fixed),
                  pl.BlockSpec((1, D_MODEL), fixed),
                  pl.BlockSpec((1, D_MODEL), fixed)],
        out_specs=[pl.BlockSpec((8, LANES), fixed), wide, wide, wide, wide, wide, wide,
                   pl.BlockSpec((8, D_MODEL), fixed), pl.BlockSpec((8, D_MODEL), fixed)],
        out_shape=[jax.ShapeDtypeStruct((8, LANES), F32),
                   jax.ShapeDtypeStruct((s, D_MODEL), F32),
                   jax.ShapeDtypeStruct((s, D_MODEL), BF16),
                   jax.ShapeDtypeStruct((s, D_MODEL), BF16),
                   jax.ShapeDtypeStruct((s, D_MODEL), BF16),
                   jax.ShapeDtypeStruct((s, D_MODEL), BF16),
                   jax.ShapeDtypeStruct((s, D_MODEL), F32),
                   jax.ShapeDtypeStruct((8, D_MODEL), F32),
                   jax.ShapeDtypeStruct((8, D_MODEL), F32)],
        scratch_shapes=[pltpu.VMEM((8, D_MODEL), F32)],
        compiler_params=_cparams(("arbitrary",)),
    )(x, target, o_fox, o_swa, z, w_o, ln_g, ln_b)


def _adamw_math(w, g, m, v):
    m = ADAM_B1 * m + (1.0 - ADAM_B1) * g
    v = ADAM_B2 * v + (1.0 - ADAM_B2) * (g * g)
    m_hat = m / (1.0 - ADAM_B1 ** ADAM_STEP)
    v_hat = v / (1.0 - ADAM_B2 ** ADAM_STEP)
    delta = -ADAM_LR * (m_hat / (jnp.sqrt(v_hat) + ADAM_EPS) + ADAM_WD * w)
    return delta, m, v


def _adamw(w, g, m, v, *, name):
    r, c = w.shape
    tr = min(256, r)

    def kern(w_ref, g_ref, m_ref, v_ref, d_ref, mo_ref, vo_ref):
        d, mn, vn = _adamw_math(w_ref[...], g_ref[...], m_ref[...], v_ref[...])
        d_ref[...] = d
        mo_ref[...] = mn
        vo_ref[...] = vn

    blk = pl.BlockSpec((tr, c), lambda i: (i, 0))
    sds = jax.ShapeDtypeStruct((r, c), F32)
    return pl.pallas_call(
        kern, name=name,
        grid=(r // tr,),
        in_specs=[blk, blk, blk, blk],
        out_specs=[blk, blk, blk],
        out_shape=[sds, sds, sds],
        compiler_params=_cparams(("parallel",)),
    )(w, g, m, v)


def _position():
    x, y, c = lax.axis_index("x"), lax.axis_index("y"), lax.axis_index("c")
    chips = [(1 - x, y), (x, 1 - y), (1 - x, 1 - y)]
    return x, y, c, chips


def _chip_index(cx, cy):
    return 2 * cx + cy


def _gather_weights(w_in_bf, w_o_bf):
    shards = (w_in_bf, w_o_bf)
    n_arr = len(shards)

    def kern(*refs):
        ins, outs = refs[:n_arr], refs[n_arr:2 * n_arr]
        send_sems, recv_sems, local_sems = refs[2 * n_arr:]
        x, y, c, chips = _position()
        me = _chip_index(x, y)
        sibling = (x, y, 1 - c)

        local = [pltpu.make_async_copy(ins[a], outs[a].at[me], local_sems.at[a]) for a in range(n_arr)]
        for cp in local:
            cp.start()

        def half(ref, a):
            rows = shards[a].shape[0] // 2
            return ref.at[pl.ds(c * rows, rows), :]

        def copy(a, k, src, slot, to):
            return pltpu.make_async_remote_copy(
                src_ref=src, dst_ref=half(outs[a].at[slot], a),
                send_sem=send_sems.at[a * 6 + k], recv_sem=recv_sems.at[a * 6 + k],
                device_id=to, device_id_type=MESH)

        first = [copy(a, j, half(ins[a], a), me, (*chip, c)) for a in range(n_arr) for j, chip in enumerate(chips)]
        for cp in first:
            cp.start()
        passed = []
        for a in range(n_arr):
            for j, chip in enumerate(chips):
                slot = _chip_index(*chip)
                copy(a, j, half(ins[a], a), slot, (*chip, c)).wait_recv()
                fwd = copy(a, 3 + j, half(outs[a].at[slot], a), slot, sibling)
                fwd.start()
                passed.append(fwd)
        for a in range(n_arr):
            for j, chip in enumerate(chips):
                slot = _chip_index(*chip)
                rows = shards[a].shape[0] // 2
                dst = outs[a].at[slot].at[pl.ds((1 - c) * rows, rows), :]
                pltpu.make_async_remote_copy(
                    src_ref=dst, dst_ref=dst, send_sem=send_sems.at[a * 6 + 3 + j],
                    recv_sem=recv_sems.at[a * 6 + 3 + j], device_id=sibling, device_id_type=MESH).wait_recv()
        for cp in first + passed:
            cp.wait_send()
        for cp in local:
            cp.wait()

    hbm = pl.BlockSpec(memory_space=pl.ANY)
    return pl.pallas_call(
        kern, name="gather_weights",
        in_specs=[hbm] * n_arr,
        out_specs=[hbm] * n_arr,
        out_shape=[jax.ShapeDtypeStruct((N_CHIPS,) + w.shape, w.dtype) for w in shards],
        scratch_shapes=[pltpu.SemaphoreType.DMA((6 * n_arr,)),
                        pltpu.SemaphoreType.DMA((6 * n_arr,)),
                        pltpu.SemaphoreType.DMA((n_arr,))],
    )(*shards)


def _swap_halves(grads):
    n_arr = len(grads)

    def kern(*refs):
        ins = refs[:n_arr]
        owns = refs[n_arr:2 * n_arr]
        gots = refs[2 * n_arr:3 * n_arr]
        send_sems, recv_sems, local_sems = refs[3 * n_arr:]
        x, y, c, _ = _position()
        sibling = (x, y, 1 - c)
        local, remote = [], []
        for a in range(n_arr):
            rows = grads[a].shape[1] // 2
            local.append(pltpu.make_async_copy(ins[a].at[:, pl.ds(c * rows, rows), :], owns[a], local_sems.at[a]))
            remote.append(pltpu.make_async_remote_copy(
                src_ref=ins[a].at[:, pl.ds((1 - c) * rows, rows), :], dst_ref=gots[a],
                send_sem=send_sems.at[a], recv_sem=recv_sems.at[a], device_id=sibling, device_id_type=MESH))
        for cp in local + remote:
            cp.start()
        for cp in remote:
            cp.wait()
        for cp in local:
            cp.wait()

    hbm = pl.BlockSpec(memory_space=pl.ANY)
    half = [jax.ShapeDtypeStruct((N_CHIPS, g.shape[1] // 2, g.shape[2]), F32) for g in grads]
    outs = pl.pallas_call(
        kern, name="swap_halves",
        in_specs=[hbm] * n_arr,
        out_specs=[hbm] * (2 * n_arr),
        out_shape=half + half,
        scratch_shapes=[pltpu.SemaphoreType.DMA((n_arr,)),
                        pltpu.SemaphoreType.DMA((n_arr,)),
                        pltpu.SemaphoreType.DMA((n_arr,))],
    )(*grads)
    return outs[:n_arr], outs[n_arr:]


def _scatter_to_owners(parts):
    n_arr = len(parts)

    def kern(*refs):
        ins = refs[:n_arr]
        outs = refs[n_arr:2 * n_arr]
        send_sems, recv_sems, local_sems = refs[2 * n_arr:]
        x, y, c, chips = _position()
        me = _chip_index(x, y)
        local = [pltpu.make_async_copy(ins[a].at[me], outs[a].at[me], local_sems.at[a]) for a in range(n_arr)]
        for cp in local:
            cp.start()
        sends = []
        for a in range(n_arr):
            for j, chip in enumerate(chips):
                sends.append(pltpu.make_async_remote_copy(
                    src_ref=ins[a].at[_chip_index(*chip)], dst_ref=outs[a].at[me],
                    send_sem=send_sems.at[a * 3 + j], recv_sem=recv_sems.at[a * 3 + j],
                    device_id=(*chip, c), device_id_type=MESH))
        for cp in sends:
            cp.start()
        for a in range(n_arr):
            for j, chip in enumerate(chips):
                slot = outs[a].at[_chip_index(*chip)]
                pltpu.make_async_remote_copy(
                    src_ref=slot, dst_ref=slot, send_sem=send_sems.at[a * 3 + j],
                    recv_sem=recv_sems.at[a * 3 + j], device_id=(*chip, c), device_id_type=MESH).wait_recv()
        for cp in sends:
            cp.wait_send()
        for cp in local:
            cp.wait()

    hbm = pl.BlockSpec(memory_space=pl.ANY)
    return pl.pallas_call(
        kern, name="scatter_to_owners",
        in_specs=[hbm] * n_arr,
        out_specs=[hbm] * n_arr,
        out_shape=[jax.ShapeDtypeStruct(p.shape, F32) for p in parts],
        scratch_shapes=[pltpu.SemaphoreType.DMA((3 * n_arr,)),
                        pltpu.SemaphoreType.DMA((3 * n_arr,)),
                        pltpu.SemaphoreType.DMA((n_arr,))],
    )(*parts)


def _join_halves(halves):
    n_arr = len(halves)

    def kern(*refs):
        ins = refs[:n_arr]
        outs = refs[n_arr:2 * n_arr]
        send_sems, recv_sems, local_sems = refs[2 * n_arr:]
        x, y, c, _ = _position()
        sibling = (x, y, 1 - c)
        local, remote = [], []
        for a in range(n_arr):
            rows = halves[a].shape[0]
            mine = outs[a].at[pl.ds(c * rows, rows), :]
            local.append(pltpu.make_async_copy(ins[a], mine, local_sems.at[a]))
            remote.append(pltpu.make_async_remote_copy(
                src_ref=ins[a], dst_ref=mine, send_sem=send_sems.at[a], recv_sem=recv_sems.at[a],
                device_id=sibling, device_id_type=MESH))
        for cp in local + remote:
            cp.start()
        for a in range(n_arr):
            rows = halves[a].shape[0]
            theirs = outs[a].at[pl.ds((1 - c) * rows, rows), :]
            pltpu.make_async_remote_copy(
                src_ref=theirs, dst_ref=theirs, send_sem=send_sems.at[a], recv_sem=recv_sems.at[a],
                device_id=sibling, device_id_type=MESH).wait_recv()
        for cp in remote:
            cp.wait_send()
        for cp in local:
            cp.wait()

    hbm = pl.BlockSpec(memory_space=pl.ANY)
    return pl.pallas_call(
        kern, name="join_halves",
        in_specs=[hbm] * n_arr,
        out_specs=[hbm] * n_arr,
        out_shape=[jax.ShapeDtypeStruct((2 * h.shape[0], h.shape[1]), F32) for h in halves],
        scratch_shapes=[pltpu.SemaphoreType.DMA((n_arr,)),
                        pltpu.SemaphoreType.DMA((n_arr,)),
                        pltpu.SemaphoreType.DMA((n_arr,))],
    )(*halves)


def _add2(a, b, *, name):
    n, r, c = a.shape
    tr = min(256, r)

    def kern(a_ref, b_ref, o_ref):
        o_ref[...] = a_ref[...] + b_ref[...]

    blk = pl.BlockSpec((1, tr, c), lambda j, i: (j, i, 0))
    return pl.pallas_call(
        kern, name=name,
        grid=(n, r // tr),
        in_specs=[blk, blk],
        out_specs=blk,
        out_shape=jax.ShapeDtypeStruct(a.shape, F32),
        compiler_params=_cparams(("parallel", "parallel")),
    )(a, b)


def _sum4(a, *, name):
    n, r, c = a.shape
    tr = min(256, r)

    def kern(a_ref, o_ref):
        o_ref[...] = ((a_ref[0] + a_ref[1]) + a_ref[2]) + a_ref[3]

    return pl.pallas_call(
        kern, name=name,
        grid=(r // tr,),
        in_specs=[pl.BlockSpec((n, tr, c), lambda i: (0, i, 0))],
        out_specs=pl.BlockSpec((tr, c), lambda i: (i, 0)),
        out_shape=jax.ShapeDtypeStruct((r, c), F32),
        compiler_params=_cparams(("parallel",)),
    )(a)


def _small_allreduce_adamw(g, w, m, v):
    def kern(g_ref, w_ref, m_ref, v_ref, gs_ref, d_ref, mo_ref, vo_ref, buf_ref, send_sems, recv_sems):
        x, y, c, _ = _position()
        me = 4 * x + 2 * y + c
        buf_ref[me] = g_ref[...]
        peers = [(x, y, 1 - c)] + [(px, py, pc) for px, py in _position()[3] for pc in (c, 1 - c)]
        sends = []
        for k, peer in enumerate(peers):
            sends.append(pltpu.make_async_remote_copy(
                src_ref=g_ref, dst_ref=buf_ref.at[me], send_sem=send_sems.at[k], recv_sem=recv_sems.at[k],
                device_id=peer, device_id_type=MESH))
        for cp in sends:
            cp.start()
        for k, (px, py, pc) in enumerate(peers):
            slot = buf_ref.at[4 * px + 2 * py + pc]
            pltpu.make_async_remote_copy(
                src_ref=slot, dst_ref=slot, send_sem=send_sems.at[k], recv_sem=recv_sems.at[k],
                device_id=(px, py, pc), device_id_type=MESH).wait_recv()
        for cp in sends:
            cp.wait_send()
        tot = buf_ref[0]
        for d in range(1, N_DEV):
            tot = tot + buf_ref[d]
        gs_ref[...] = tot
        delta, mn, vn = _adamw_math(w_ref[...], tot, m_ref[...], v_ref[...])
        d_ref[...] = delta
        mo_ref[...] = mn
        vo_ref[...] = vn

    vm = pl.BlockSpec(memory_space=pltpu.VMEM)
    sds = jax.ShapeDtypeStruct((SMALL_ROWS, LANES), F32)
    return pl.pallas_call(
        kern, name="small_allreduce_adamw",
        in_specs=[vm] * 4,
        out_specs=[vm] * 4,
        out_shape=[sds] * 4,
        scratch_shapes=[pltpu.VMEM((N_DEV, SMALL_ROWS, LANES), F32),
                        pltpu.SemaphoreType.DMA((N_DEV - 1,)),
                        pltpu.SemaphoreType.DMA((N_DEV - 1,))],
    )(g, w, m, v)


def _to_padded_cols(w):
    pad = jnp.zeros((w.shape[0], N_C - FOX_HEADS), w.dtype)
    return jnp.concatenate([w[:, 0:1536], w[:, 2056:2824], w[:, 1536:1544], pad,
                            w[:, 1544:2056], w[:, 2824:3336]], axis=1)


def _from_padded_cols(g):
    return jnp.concatenate([g[:, 0:1536], g[:, OFF_C:OFF_C + FOX_HEADS], g[:, OFF_B:OFF_B + FOX_W],
                            g[:, 1536:N_A], g[:, OFF_B + FOX_W:N_PAD]], axis=1)


def _pack_small(b_f, rel_bias, sink, ln_g, ln_b):
    row = lambda v: jnp.pad(v.reshape(1, -1), ((0, 0), (0, LANES - v.size)))
    return jnp.concatenate([ln_g.reshape(8, LANES), ln_b.reshape(8, LANES), rel_bias.reshape(2, LANES),
                            row(b_f), row(sink), jnp.zeros((4, LANES), F32)], axis=0)


def _unpack_small(p):
    ln_g = p[0:8].reshape(1, D_MODEL)
    ln_b = p[8:16].reshape(1, D_MODEL)
    rel_bias = p[16:18].reshape(NUM_BUCKETS, SWA_HEADS)
    b_f = p[18:19, :FOX_HEADS]
    sink = p[19:20, :SWA_HEADS]
    return b_f, rel_bias, sink, ln_g, ln_b


def _heads_to_rows(a_b):
    return a_b[:, ::HEAD_DIM].T.reshape(FOX_HEADS, 1, a_b.shape[0])


def kernel(x, w_in, b_f, rel_bias, sink, w_o, ln_g, ln_b, loss_target, m_w_in, m_b_f, m_rel_bias, m_sink, m_w_o, m_ln_g, m_ln_b, v_w_in, v_b_f, v_rel_bias, v_sink, v_w_o, v_ln_g, v_ln_b):
    x2 = x[0]
    tgt = loss_target[0]
    s = x2.shape[0]
    w_in2, w_o2 = w_in[0], w_o[0]

    w_in_all, w_o_all = _gather_weights(w_in2.astype(BF16), w_o2.astype(BF16))
    w_full = jnp.concatenate([w_in_all[j] for j in range(N_CHIPS)], axis=1)
    w_pad = _to_padded_cols(w_full)
    w_o_full = w_o_all.reshape(D_MODEL, D_MODEL)

    x_bf = x2.astype(BF16)
    qkv = _matmul_nn(x_bf, w_pad, n_off=0, n_out=N_A, tm=512, tn=768, out_dtype=BF16, name="proj_qkv")
    z = _matmul_nn(x_bf, w_pad, n_off=OFF_B, n_out=N_B, tm=512, tn=512, out_dtype=F32, name="proj_gate")
    ffp = _matmul_nn(x_bf, w_pad, n_off=OFF_C, n_out=N_C, tm=512, tn=N_C, out_dtype=F32, name="proj_forget")
    bfp = jnp.pad(b_f, ((0, 0), (0, LANES - FOX_HEADS)))
    cum, cum_b = _cum_fwd(ffp, bfp)
    cum_t3 = cum[:, :FOX_HEADS].T.reshape(FOX_HEADS, 1, s)
    o_fox, lse_b = _fox_fwd(qkv, cum_t3)
    bucket = jnp.asarray(_bucket_table())
    bias = _swa_bias(rel_bias, bucket)
    o_swa, lse_swa = _swa_fwd(qkv, bias, sink)

    loss8, dh, dy, mixed, do_bf, dz, delta_b, gg8, gb8 = _post(
        x2, tgt, o_fox, o_swa, z, w_o_full, ln_g, ln_b)
    loss = lax.psum(loss8[0, 0], ("x", "y", "c"))
    grad_w_o_full = _matmul_tn(mixed, dy, tm=512, tn=512, tk=512, name="grad_w_o")

    lse_t3 = _heads_to_rows(lse_b)
    delta_t3 = _heads_to_rows(delta_b[:, :FOX_W])
    dq_fox, dk_fox, dv_fox, dcum_b, dcum_q = _fox_bwd(qkv, do_bf, cum_t3, cum_b, lse_t3, delta_t3)
    lane_pad = ((0, 0), (0, LANES - FOX_HEADS))
    dcum_k = jnp.pad(dcum_b[:, ::HEAD_DIM], lane_pad)
    dcum_q = jnp.pad(dcum_q.reshape(FOX_HEADS, s).T, lane_pad)
    dff, gbf8 = _cum_bwd(dcum_k, dcum_q, ffp, bfp)
    dq_swa, dk_swa, dv_swa, grb, gsk8 = _swa_bwd(qkv, do_bf, delta_b, lse_swa, bias, sink, bucket)

    dproj = jnp.concatenate([dq_fox.astype(BF16), dk_fox, dv_fox, dq_swa, dk_swa, dv_swa, dff, dz], axis=1)
    grad_x = _grad_x_matmul(dproj, w_pad, dh, tm=512, tn=512, name="grad_x")
    grad_w_pad = _matmul_tn(x_bf, dproj, tm=512, tn=512, tk=512, name="grad_w_in")
    grad_w_in_full = _from_padded_cols(grad_w_pad)

    shard_cols = D_IN // N_CHIPS
    g_in4 = jnp.stack([grad_w_in_full[:, j * shard_cols:(j + 1) * shard_cols] for j in range(N_CHIPS)])
    g_o4 = grad_w_o_full.reshape(N_CHIPS, D_MODEL // N_CHIPS, D_MODEL)
    owns, gots = _swap_halves([g_in4, g_o4])
    parts = [_add2(owns[0], gots[0], name="pair_sum_w_in"), _add2(owns[1], gots[1], name="pair_sum_w_o")]
    slabs = _scatter_to_owners(parts)
    halves = [_sum4(slabs[0], name="chip_sum_w_in"), _sum4(slabs[1], name="chip_sum_w_o")]
    g_w_in, g_w_o = _join_halves(halves)

    d_w_in, nm_w_in, nv_w_in = _adamw(w_in2, g_w_in, m_w_in[0], v_w_in[0], name="adamw_w_in")
    d_w_o, nm_w_o, nv_w_o = _adamw(w_o2, g_w_o, m_w_o[0], v_w_o[0], name="adamw_w_o")

    g_small = _pack_small(gbf8[0:1, :FOX_HEADS], grb[:, :SWA_HEADS], gsk8[0:1, :SWA_HEADS], gg8[0:1], gb8[0:1])
    w_small = _pack_small(b_f, rel_bias, sink, ln_g, ln_b)
    m_small = _pack_small(m_b_f, m_rel_bias, m_sink, m_ln_g, m_ln_b)
    v_small = _pack_small(v_b_f, v_rel_bias, v_sink, v_ln_g, v_ln_b)
    gs, ds, ms, vs = _small_allreduce_adamw(g_small, w_small, m_small, v_small)
    g_bf, g_rb, g_sk, g_lg, g_lb = _unpack_small(gs)
    d_bf, d_rb, d_sk, d_lg, d_lb = _unpack_small(ds)
    m_bf, m_rb, m_sk, m_lg, m_lb = _unpack_small(ms)
    v_bf, v_rb, v_sk, v_lg, v_lb = _unpack_small(vs)

    e = lambda a: a[None]
    return (loss, e(grad_x),
            e(g_w_in), g_bf, g_rb, g_sk, e(g_w_o), g_lg, g_lb,
            e(d_w_in), d_bf, d_rb, d_sk, e(d_w_o), d_lg, d_lb,
            e(nm_w_in), m_bf, m_rb, m_sk, e(nm_w_o), m_lg, m_lb,
            e(nv_w_in), v_bf, v_rb, v_sk, e(nv_w_o), v_lg, v_lb)
```

```python
import functools
import math

import numpy as np
import jax
import jax.numpy as jnp
from jax import lax
from jax.experimental import pallas as pl
from jax.experimental.pallas import tpu as pltpu

F32 = jnp.float32
BF16 = jnp.bfloat16

D_MODEL = 1024
HEAD_DIM = 64
FOX_HEADS = 8
SWA_HEADS = 8
SWA_KV_HEADS = 2
SWA_GROUP = 4
FOX_W = 512
SWA_W = 512
SWA_KV_W = 128
BLOCK = 128
NUM_BUCKETS = 32
MAX_DISTANCE = 128
LN_EPS = 1e-5
NEG = -1e30
ALPHA = 2.0 ** 0.25
QK_SCALE = 0.125

ADAM_LR = 0.001
ADAM_B1 = 0.9
ADAM_B2 = 0.999
ADAM_EPS = 1e-08
ADAM_WD = 0.01
ADAM_STEP = 10

D_IN = 3336
N_A = 2304
N_C = 256
N_B = 1024
OFF_C = N_A
OFF_B = N_A + N_C
N_PAD = N_A + N_C + N_B
COL_FK, COL_FV, COL_SQ, COL_SK, COL_SV = 512, 1024, 1536, 2048, 2176

LANES = 128
FOX_T = 256
VMEM_LIMIT = 56 * 1024 * 1024

MESH = pl.DeviceIdType.MESH
N_CHIPS = 4
N_DEV = 8
SMALL_ROWS = 24


def _cparams(sem=None):
    return pltpu.CompilerParams(dimension_semantics=sem, vmem_limit_bytes=VMEM_LIMIT)


def _split3(x):
    hi = x.astype(BF16)
    r = x - hi.astype(F32)
    mid = r.astype(BF16)
    lo = (r - mid.astype(F32)).astype(BF16)
    return hi, mid, lo


def _dot(a, b):
    return jnp.dot(a, b, preferred_element_type=F32)


def _dot_nt(a, b):
    return lax.dot_general(a, b, (((1,), (1,)), ((), ())), preferred_element_type=F32)


def _dot_tn(a, b):
    return lax.dot_general(a, b, (((0,), (0,)), ((), ())), preferred_element_type=F32)


def _matmul_nn(a, b, *, n_off, n_out, tm, tn, out_dtype, name):
    m, k = a.shape
    joff = n_off // tn

    def kern(a_ref, b_ref, o_ref):
        o_ref[...] = _dot(a_ref[...], b_ref[...]).astype(o_ref.dtype)

    return pl.pallas_call(
        kern, name=name,
        grid=(n_out // tn, m // tm),
        in_specs=[pl.BlockSpec((tm, k), lambda j, i: (i, 0)),
                  pl.BlockSpec((k, tn), lambda j, i: (0, j + joff))],
        out_specs=pl.BlockSpec((tm, tn), lambda j, i: (i, j)),
        out_shape=jax.ShapeDtypeStruct((m, n_out), out_dtype),
        compiler_params=_cparams(("parallel", "parallel")),
    )(a, b)


def _grad_x_matmul(dproj, w_pad, dh, *, tm, tn, name):
    m, k = dproj.shape
    n = w_pad.shape[0]

    def kern(a_ref, b_ref, dh_ref, o_ref):
        o_ref[...] = ALPHA * dh_ref[...] + _dot_nt(a_ref[...], b_ref[...])

    return pl.pallas_call(
        kern, name=name,
        grid=(n // tn, m // tm),
        in_specs=[pl.BlockSpec((tm, k), lambda j, i: (i, 0)),
                  pl.BlockSpec((tn, k), lambda j, i: (j, 0)),
                  pl.BlockSpec((tm, tn), lambda j, i: (i, j))],
        out_specs=pl.BlockSpec((tm, tn), lambda j, i: (i, j)),
        out_shape=jax.ShapeDtypeStruct((m, n), F32),
        compiler_params=_cparams(("parallel", "parallel")),
    )(dproj, w_pad, dh)


def _matmul_tn(a, b, *, tm, tn, tk, name):
    s, m = a.shape
    n = b.shape[1]

    def kern(a_ref, b_ref, o_ref):
        @pl.when(pl.program_id(2) == 0)
        def _():
            o_ref[...] = jnp.zeros_like(o_ref)
        o_ref[...] += _dot_tn(a_ref[...], b_ref[...])

    return pl.pallas_call(
        kern, name=name,
        grid=(m // tm, n // tn, s // tk),
        in_specs=[pl.BlockSpec((tk, tm), lambda i, j, k: (k, i)),
                  pl.BlockSpec((tk, tn), lambda i, j, k: (k, j))],
        out_specs=pl.BlockSpec((tm, tn), lambda i, j, k: (i, j)),
        out_shape=jax.ShapeDtypeStruct((m, n), F32),
        compiler_params=_cparams(("parallel", "parallel", "arbitrary")),
    )(a, b)


def _tri(n, lower):
    r = lax.broadcasted_iota(jnp.int32, (n, n), 0)
    c = lax.broadcasted_iota(jnp.int32, (n, n), 1)
    keep = (c <= r) if lower else (c >= r)
    return jnp.where(keep, 1.0, 0.0).astype(BF16)


def _exact_dot(mat_bf16, x_f32, left):
    out = None
    for piece in _split3(x_f32):
        t = _dot(mat_bf16, piece) if left else _dot(piece, mat_bf16)
        out = t if out is None else out + t
    return out


def _log_sigmoid(z):
    return jnp.minimum(z, 0.0) - jnp.log(1.0 + jnp.exp(-jnp.abs(z)))


def _cum_fwd(ffp, bfp):
    s = ffp.shape[0]
    t = min(256, s)

    def kern(ff_ref, b_ref, cum_ref, carry_ref):
        @pl.when(pl.program_id(0) == 0)
        def _():
            carry_ref[...] = jnp.zeros_like(carry_ref)
        lane = lax.broadcasted_iota(jnp.int32, (1, LANES), 1)
        lf = _log_sigmoid(ff_ref[...] + b_ref[...])
        lf = jnp.where(lane < FOX_HEADS, lf, 0.0)
        cum = _exact_dot(_tri(t, True), lf, True) + carry_ref[0:1, :]
        cum_ref[...] = cum
        carry_ref[...] = jnp.broadcast_to(cum[t - 1:t, :], carry_ref.shape)

    return pl.pallas_call(
        kern, name="cum_fwd",
        grid=(s // t,),
        in_specs=[pl.BlockSpec((t, LANES), lambda i: (i, 0)),
                  pl.BlockSpec((1, LANES), lambda i: (0, 0))],
        out_specs=pl.BlockSpec((t, LANES), lambda i: (i, 0)),
        out_shape=jax.ShapeDtypeStruct((s, LANES), F32),
        scratch_shapes=[pltpu.VMEM((8, LANES), F32)],
        compiler_params=_cparams(("arbitrary",)),
    )(ffp, bfp)


def _cum_bwd(dcum_k, dcum_q, ffp, bfp):
    s = dcum_k.shape[0]
    t = min(256, s)
    nb = s // t

    def kern(dck_ref, dcq_ref, ff_ref, b_ref, dff_ref, gb_ref, carry_ref):
        @pl.when(pl.program_id(0) == 0)
        def _():
            carry_ref[...] = jnp.zeros_like(carry_ref)
            gb_ref[...] = jnp.zeros_like(gb_ref)
        lane = lax.broadcasted_iota(jnp.int32, (1, LANES), 1)
        dlf = _exact_dot(_tri(t, False), dck_ref[...] + dcq_ref[...], True) + carry_ref[0:1, :]
        carry_ref[...] = jnp.broadcast_to(dlf[0:1, :], carry_ref.shape)
        z = ff_ref[...] + b_ref[...]
        dff = jnp.where(lane < FOX_HEADS, dlf / (1.0 + jnp.exp(z)), 0.0)
        gb_ref[...] += jnp.broadcast_to(jnp.sum(dff, axis=0, keepdims=True), gb_ref.shape)
        dff_ref[...] = jnp.concatenate([dff, jnp.zeros_like(dff)], axis=1).astype(BF16)

    return pl.pallas_call(
        kern, name="cum_bwd",
        grid=(nb,),
        in_specs=[pl.BlockSpec((t, LANES), lambda i: (nb - 1 - i, 0)),
                  pl.BlockSpec((t, LANES), lambda i: (nb - 1 - i, 0)),
                  pl.BlockSpec((t, LANES), lambda i: (nb - 1 - i, 0)),
                  pl.BlockSpec((1, LANES), lambda i: (0, 0))],
        out_specs=[pl.BlockSpec((t, N_C), lambda i: (nb - 1 - i, 0)),
                   pl.BlockSpec((8, LANES), lambda i: (0, 0))],
        out_shape=[jax.ShapeDtypeStruct((s, N_C), BF16),
                   jax.ShapeDtypeStruct((8, LANES), F32)],
        scratch_shapes=[pltpu.VMEM((8, LANES), F32)],
        compiler_params=_cparams(("arbitrary",)),
    )(dcum_k, dcum_q, ffp, bfp)


def _resident(shape, index_map):
    return pl.BlockSpec(shape, index_map, pipeline_mode=pl.Buffered(1))


def _fox_fwd(qkv, vt, cum_t3, cum):
    s = qkv.shape[0]
    t = min(FOX_T, s)
    nq = s // t
    nh = FOX_HEADS

    def kern(q_ref, k_ref, vt_ref, ct_ref, c_ref, o_ref, lse_ref, m_ref, l_ref, acc_ref):
        i = pl.program_id(0)
        lane = lax.broadcasted_iota(jnp.int32, (1, LANES), 1)
        krow = lax.broadcasted_iota(jnp.int32, (t, t), 0)
        qcol = lax.broadcasted_iota(jnp.int32, (t, t), 1)
        causal = krow <= qcol
        q0 = pl.multiple_of(i * t, t)
        qts, crefs = [], []
        for h in range(nh):
            p, a = divmod(h, 2)
            q2 = q_ref[:, p * LANES:(p + 1) * LANES] * jnp.asarray(QK_SCALE, BF16)
            sel = (lane < HEAD_DIM) if a == 0 else (lane >= HEAD_DIM)
            qts.append(jnp.where(sel, q2, jnp.zeros_like(q2)).astype(F32).T.astype(BF16))
            crefs.append(ct_ref[h, :, pl.ds(q0, LANES)][:, 0:1])
        m_ref[...] = jnp.full(m_ref.shape, NEG, F32)
        l_ref[...] = jnp.zeros_like(l_ref)
        acc_ref[...] = jnp.zeros_like(acc_ref)

        def tile(j, masked):
            k0 = pl.multiple_of(j * t, t)
            cb = c_ref[pl.ds(k0, t), :]
            sts = [_dot(k_ref[pl.ds(k0, t), (h // 2) * LANES:(h // 2 + 1) * LANES], qts[h]) for h in range(nh)]
            pts, scales = [], []
            for h in range(nh):
                u = sts[h] - (cb[:, h:h + 1] - crefs[h])
                if masked:
                    u = jnp.where(causal, u, NEG)
                m_old = m_ref[h]
                m_new = jnp.maximum(m_old, jnp.max(u, axis=0, keepdims=True))
                scale = jnp.exp(m_old - m_new)
                p = jnp.exp(u - m_new)
                l_ref[h] = scale * l_ref[h] + jnp.sum(p, axis=0, keepdims=True)
                m_ref[h] = m_new
                pts.append(p.astype(BF16))
                scales.append(scale)
            for h in range(nh):
                vth = vt_ref[h * HEAD_DIM:(h + 1) * HEAD_DIM, pl.ds(k0, t)]
                acc_ref[h] = scales[h] * acc_ref[h] + _dot(vth, pts[h])

        def body(j, c):
            tile(j, False)
            return c
        lax.fori_loop(0, i, body, 0)
        tile(i, True)

        for p in range(nh // 2):
            ot = jnp.concatenate([acc_ref[2 * p + a] * (1.0 / l_ref[2 * p + a]) for a in range(2)], axis=0)
            o_ref[:, p * LANES:(p + 1) * LANES] = ot.T
        for h in range(nh):
            lse_ref[h, :, pl.ds(q0, t)] = m_ref[h] + jnp.log(l_ref[h])

    return pl.pallas_call(
        kern, name="fox_fwd",
        grid=(nq,),
        in_specs=[pl.BlockSpec((t, FOX_W), lambda i: (i, 0)),
                  _resident((s, FOX_W), lambda i: (0, COL_FK // FOX_W)),
                  _resident((FOX_W, s), lambda i: (0, 0)),
                  _resident((nh, 1, s), lambda i: (0, 0, 0)),
                  _resident((s, LANES), lambda i: (0, 0))],
        out_specs=[pl.BlockSpec((t, FOX_W), lambda i: (i, 0)),
                   pl.BlockSpec((nh, 1, s), lambda i: (0, 0, 0))],
        out_shape=[jax.ShapeDtypeStruct((s, FOX_W), F32),
                   jax.ShapeDtypeStruct((nh, 1, s), F32)],
        scratch_shapes=[pltpu.VMEM((nh, 1, t), F32),
                        pltpu.VMEM((nh, 1, t), F32),
                        pltpu.VMEM((nh, HEAD_DIM, t), F32)],
        compiler_params=_cparams(("arbitrary",)),
    )(qkv, qkv, vt, cum_t3, cum)


def _fox_bwd(qkv, do_bf, cum_t3, cum, lse_t3, delta_t3):
    s = qkv.shape[0]
    t = min(FOX_T, s)
    nq = s // t
    nh = FOX_HEADS
    npair = nh // 2

    def kern(q_ref, do_ref, k_ref, v_ref, ct_ref, c_ref, lse_ref, dl_ref,
             dqt_ref, dk_ref, dv_ref, dc_ref, dcq_ref, accv_ref, acck_ref, accd_ref):
        kj = pl.program_id(0)
        lane = lax.broadcasted_iota(jnp.int32, (1, LANES), 1)
        sub = lax.broadcasted_iota(jnp.int32, (LANES, 1), 0)
        krow = lax.broadcasted_iota(jnp.int32, (t, t), 0)
        qcol = lax.broadcasted_iota(jnp.int32, (t, t), 1)
        causal = krow <= qcol
        sels = [lane < HEAD_DIM, lane >= HEAD_DIM]
        subsels = [sub < HEAD_DIM, sub >= HEAD_DIM]

        @pl.when(kj == 0)
        def _():
            dqt_ref[...] = jnp.zeros_like(dqt_ref)
            dcq_ref[...] = jnp.zeros_like(dcq_ref)

        accv_ref[...] = jnp.zeros_like(accv_ref)
        acck_ref[...] = jnp.zeros_like(acck_ref)
        accd_ref[...] = jnp.zeros_like(accd_ref)
        cb = c_ref[...]
        k2s, v2s, kts = [], [], []
        for p in range(npair):
            k2 = k_ref[:, p * LANES:(p + 1) * LANES]
            k2s.append(k2)
            v2s.append(v_ref[:, p * LANES:(p + 1) * LANES])
            kt = k2.astype(F32).T * QK_SCALE
            kts.append([jnp.where(subsels[a], kt, 0.0).astype(BF16) for a in range(2)])
        css = [cb[:, h:h + 1] for h in range(nh)]

        def tile(i, masked):
            q0 = pl.multiple_of(i * t, t)
            sts, dpts, qms, doms = [], [], [], []
            for h in range(nh):
                p, a = divmod(h, 2)
                qi = q_ref[pl.ds(q0, t), p * LANES:(p + 1) * LANES] * jnp.asarray(QK_SCALE, BF16)
                doi = do_ref[pl.ds(q0, t), p * LANES:(p + 1) * LANES]
                qm = jnp.where(sels[a], qi, jnp.zeros_like(qi))
                dom = jnp.where(sels[a], doi, jnp.zeros_like(doi))
                qms.append(qm)
                doms.append(dom)
                sts.append(_dot_nt(k2s[p], qm))
                dpts.append(_dot_nt(v2s[p], dom))
            pts, dsts = [], []
            for h in range(nh):
                cref = ct_ref[h, :, pl.ds(q0, LANES)][:, 0:1]
                pt = jnp.exp(sts[h] - (css[h] - cref) - lse_ref[h, :, pl.ds(q0, t)])
                if masked:
                    pt = jnp.where(causal, pt, 0.0)
                ds32 = pt * (dpts[h] - dl_ref[h, :, pl.ds(q0, t)])
                part = ds32[:, 0:LANES]
                for c in range(1, t // LANES):
                    part = part + ds32[:, c * LANES:(c + 1) * LANES]
                accd_ref[h] += part
                dcq_ref[h, :, pl.ds(q0, t)] += jnp.sum(ds32, axis=0, keepdims=True)
                pts.append(pt.astype(BF16))
                dsts.append(ds32.astype(BF16))
            for p in range(npair):
                ha, hb = 2 * p, 2 * p + 1
                accv_ref[p] += _dot(pts[ha], doms[ha]) + _dot(pts[hb], doms[hb])
                acck_ref[p] += _dot(dsts[ha], qms[ha]) + _dot(dsts[hb], qms[hb])
                dqt_ref[p * LANES:(p + 1) * LANES, pl.ds(q0, t)] += (
                    _dot(kts[p][0], dsts[ha]) + _dot(kts[p][1], dsts[hb]))

        tile(kj, True)

        def body(i, c):
            tile(i, False)
            return c
        lax.fori_loop(kj + 1, nq, body, 0)

        dc = jnp.zeros((t, LANES), F32)
        for h in range(nh):
            dc = jnp.where(lane == h, -jnp.sum(accd_ref[h], axis=1, keepdims=True), dc)
        dc_ref[...] = dc
        for p in range(npair):
            dv_ref[:, p * LANES:(p + 1) * LANES] = accv_ref[p].astype(BF16)
            dk_ref[:, p * LANES:(p + 1) * LANES] = acck_ref[p].astype(BF16)

    whole = lambda kj: (0, 0, 0)
    return pl.pallas_call(
        kern, name="fox_bwd",
        grid=(nq,),
        in_specs=[_resident((s, FOX_W), lambda kj: (0, 0)),
                  _resident((s, FOX_W), lambda kj: (0, 0)),
                  pl.BlockSpec((t, FOX_W), lambda kj: (kj, COL_FK // FOX_W)),
                  pl.BlockSpec((t, FOX_W), lambda kj: (kj, COL_FV // FOX_W)),
                  _resident((nh, 1, s), whole),
                  pl.BlockSpec((t, LANES), lambda kj: (kj, 0)),
                  _resident((nh, 1, s), whole),
                  _resident((nh, 1, s), whole)],
        out_specs=[_resident((FOX_W, s), lambda kj: (0, 0)),
                   pl.BlockSpec((t, FOX_W), lambda kj: (kj, 0)),
                   pl.BlockSpec((t, FOX_W), lambda kj: (kj, 0)),
                   pl.BlockSpec((t, LANES), lambda kj: (kj, 0)),
                   _resident((nh, 1, s), whole)],
        out_shape=[jax.ShapeDtypeStruct((FOX_W, s), F32),
                   jax.ShapeDtypeStruct((s, FOX_W), BF16),
                   jax.ShapeDtypeStruct((s, FOX_W), BF16),
                   jax.ShapeDtypeStruct((s, LANES), F32),
                   jax.ShapeDtypeStruct((nh, 1, s), F32)],
        scratch_shapes=[pltpu.VMEM((npair, t, LANES), F32),
                        pltpu.VMEM((npair, t, LANES), F32),
                        pltpu.VMEM((nh, t, LANES), F32)],
        compiler_params=_cparams(("arbitrary",)),
    )(qkv, do_bf, qkv, qkv, cum_t3, cum, lse_t3, delta_t3)


def _bucket_table():
    qi = np.arange(BLOCK)[:, None]
    kj = np.arange(2 * BLOCK)[None, :]
    rel = np.maximum(qi + BLOCK - kj, 0).astype(np.int32)
    max_exact = NUM_BUCKETS // 2
    relf = np.maximum(rel, 1).astype(np.float32)
    large = max_exact + (np.log(relf / np.float32(max_exact)) / np.float32(math.log(MAX_DISTANCE / max_exact))
                         * np.float32(NUM_BUCKETS - max_exact)).astype(np.int32)
    large = np.minimum(large, NUM_BUCKETS - 1)
    return np.where(rel < max_exact, rel, large).astype(np.int32)


def _swa_bias(rel_bias, bucket):
    def kern(rb_ref, bk_ref, o_ref):
        bk = bk_ref[...]
        for h in range(SWA_HEADS):
            acc = jnp.zeros((BLOCK, 2 * BLOCK), F32)
            for b in range(NUM_BUCKETS):
                acc = jnp.where(bk == b, rb_ref[b, h], acc)
            o_ref[h] = acc

    return pl.pallas_call(
        kern, name="swa_bias",
        in_specs=[pl.BlockSpec(memory_space=pltpu.SMEM),
                  pl.BlockSpec(memory_space=pltpu.VMEM)],
        out_specs=pl.BlockSpec(memory_space=pltpu.VMEM),
        out_shape=jax.ShapeDtypeStruct((SWA_HEADS, BLOCK, 2 * BLOCK), F32),
        compiler_params=_cparams(),
    )(rel_bias, bucket)


def _swa_mask(n):
    qi = lax.broadcasted_iota(jnp.int32, (BLOCK, 2 * BLOCK), 0)
    kj = lax.broadcasted_iota(jnp.int32, (BLOCK, 2 * BLOCK), 1)
    rel = qi + BLOCK - kj
    band = (rel >= 0) & (rel < BLOCK)
    return band & ((kj >= BLOCK) | (n > 0))


def _swa_fwd(qkv, bias, sink):
    s = qkv.shape[0]
    nb = s // BLOCK

    def kern(q_ref, kp_ref, kc_ref, vp_ref, vc_ref, bias_ref, sink_ref, o_ref, lse_ref):
        n = pl.program_id(0)
        mask = _swa_mask(n)
        lane = lax.broadcasted_iota(jnp.int32, (1, LANES), 1)
        q = q_ref[...] * jnp.asarray(QK_SCALE, BF16)
        k = jnp.concatenate([kp_ref[...], kc_ref[...]], axis=0)
        v = jnp.concatenate([vp_ref[...], vc_ref[...]], axis=0)
        outs = []
        lse_all = jnp.zeros((BLOCK, LANES), F32)
        for h in range(SWA_HEADS):
            g = h // SWA_GROUP
            qh = q[:, h * HEAD_DIM:(h + 1) * HEAD_DIM]
            kg = k[:, g * HEAD_DIM:(g + 1) * HEAD_DIM]
            vg = v[:, g * HEAD_DIM:(g + 1) * HEAD_DIM]
            sc = jnp.where(mask, _dot_nt(qh, kg) + bias_ref[h], NEG)
            sk = sink_ref[0, h]
            m = jnp.maximum(jnp.max(sc, axis=1, keepdims=True), sk)
            p = jnp.exp(sc - m)
            l = jnp.sum(p, axis=1, keepdims=True) + jnp.exp(sk - m)
            probs = p * (1.0 / l)
            outs.append(_dot(probs.astype(BF16), vg))
            lse_all = jnp.where(lane == h, m + jnp.log(l), lse_all)
        o_ref[...] = jnp.concatenate(outs, axis=1)
        lse_ref[...] = lse_all

    cq, ck, cv = COL_SQ // SWA_W, COL_SK // LANES, COL_SV // LANES
    prev = lambda n: jnp.maximum(n - 1, 0)
    return pl.pallas_call(
        kern, name="swa_fwd",
        grid=(nb,),
        in_specs=[pl.BlockSpec((BLOCK, SWA_W), lambda n: (n, cq)),
                  pl.BlockSpec((BLOCK, LANES), lambda n: (prev(n), ck)),
                  pl.BlockSpec((BLOCK, LANES), lambda n: (n, ck)),
                  pl.BlockSpec((BLOCK, LANES), lambda n: (prev(n), cv)),
                  pl.BlockSpec((BLOCK, LANES), lambda n: (n, cv)),
                  pl.BlockSpec((SWA_HEADS, BLOCK, 2 * BLOCK), lambda n: (0, 0, 0)),
                  pl.BlockSpec(memory_space=pltpu.SMEM)],
        out_specs=[pl.BlockSpec((BLOCK, SWA_W), lambda n: (n, 0)),
                   pl.BlockSpec((BLOCK, LANES), lambda n: (n, 0))],
        out_shape=[jax.ShapeDtypeStruct((s, SWA_W), F32),
                   jax.ShapeDtypeStruct((s, LANES), F32)],
        compiler_params=_cparams(("parallel",)),
    )(qkv, qkv, qkv, qkv, qkv, bias, sink)


def _swa_bwd(qkv, do_bf, delta_b, lse, bias, sink, bucket):
    s = qkv.shape[0]
    nb = s // BLOCK

    def kern(q_ref, kp_ref, kc_ref, vp_ref, vc_ref, do_ref, dl_ref, lse_ref, bias_ref, sink_ref, bk_ref,
             dq_ref, dk_ref, dv_ref, grb_ref, gsk_ref, dbias_ref, ck_ref, cv_ref, sk_ref):
        n = pl.program_id(0)
        lane = lax.broadcasted_iota(jnp.int32, (1, LANES), 1)

        @pl.when(n == 0)
        def _():
            dbias_ref[...] = jnp.zeros_like(dbias_ref)
            ck_ref[...] = jnp.zeros_like(ck_ref)
            cv_ref[...] = jnp.zeros_like(cv_ref)
            sk_ref[...] = jnp.zeros_like(sk_ref)

        @pl.when(n < nb)
        def _():
            mask = _swa_mask(n)
            q = q_ref[...] * jnp.asarray(QK_SCALE, BF16)
            k = jnp.concatenate([kp_ref[...], kc_ref[...]], axis=0)
            v = jnp.concatenate([vp_ref[...], vc_ref[...]], axis=0)
            do = do_ref[...]
            dl = dl_ref[...]
            lse_all = lse_ref[...]
            dqs = []
            dks = [None] * SWA_KV_HEADS
            dvs = [None] * SWA_KV_HEADS
            gsk = jnp.zeros((1, LANES), F32)
            for h in range(SWA_HEADS):
                g = h // SWA_GROUP
                hs = slice(h * HEAD_DIM, (h + 1) * HEAD_DIM)
                gs = slice(g * HEAD_DIM, (g + 1) * HEAD_DIM)
                qh, doh = q[:, hs], do[:, hs]
                kg, vg = k[:, gs], v[:, gs]
                lse_h = lse_all[:, h:h + 1]
                dlt = dl[:, h * HEAD_DIM:h * HEAD_DIM + 1]
                sc = jnp.where(mask, _dot_nt(qh, kg) + bias_ref[h], NEG)
                p = jnp.exp(sc - lse_h)
                dp = _dot_nt(doh, vg)
                ds = p * (dp - dlt)
                dbias_ref[h] += ds
                p_sink = jnp.exp(sink_ref[0, h] - lse_h)
                gsk = gsk + jnp.where(lane == h, -jnp.sum(p_sink * dlt), 0.0)
                ds_bf = ds.astype(BF16)
                dqs.append(_dot(ds_bf, kg) * QK_SCALE)
                dk_h = _dot_tn(ds_bf, qh)
                dv_h = _dot_tn(p.astype(BF16), doh)
                dks[g] = dk_h if dks[g] is None else dks[g] + dk_h
                dvs[g] = dv_h if dvs[g] is None else dvs[g] + dv_h
            dq_ref[...] = jnp.concatenate(dqs, axis=1).astype(BF16)
            sk_ref[...] += jnp.broadcast_to(gsk, sk_ref.shape)
            dk2 = jnp.concatenate(dks, axis=1)
            dv2 = jnp.concatenate(dvs, axis=1)
            dk_ref[...] = (ck_ref[...] + dk2[:BLOCK]).astype(BF16)
            dv_ref[...] = (cv_ref[...] + dv2[:BLOCK]).astype(BF16)
            ck_ref[...] = dk2[BLOCK:]
            cv_ref[...] = dv2[BLOCK:]

        @pl.when(n == nb)
        def _():
            dk_ref[...] = ck_ref[...].astype(BF16)
            dv_ref[...] = cv_ref[...].astype(BF16)
            gsk_ref[...] = sk_ref[...]
            bk = bk_ref[...]
            rowi = lax.broadcasted_iota(jnp.int32, (NUM_BUCKETS, LANES), 0)
            lanei = lax.broadcasted_iota(jnp.int32, (NUM_BUCKETS, LANES), 1)
            out = jnp.zeros((NUM_BUCKETS, LANES), F32)
            for h in range(SWA_HEADS):
                db = dbias_ref[h]
                for b in range(NUM_BUCKETS):
                    val = jnp.sum(jnp.where(bk == b, db, 0.0))
                    out = jnp.where((rowi == b) & (lanei == h), val, out)
            grb_ref[...] = out

    cq, ck, cv = COL_SQ // SWA_W, COL_SK // LANES, COL_SV // LANES
    cur = lambda n: jnp.minimum(n, nb - 1)
    prev = lambda n: jnp.maximum(jnp.minimum(n, nb - 1) - 1, 0)
    kout = lambda n: jnp.maximum(n - 1, 0)
    return pl.pallas_call(
        kern, name="swa_bwd",
        grid=(nb + 1,),
        in_specs=[pl.BlockSpec((BLOCK, SWA_W), lambda n: (cur(n), cq)),
                  pl.BlockSpec((BLOCK, LANES), lambda n: (prev(n), ck)),
                  pl.BlockSpec((BLOCK, LANES), lambda n: (cur(n), ck)),
                  pl.BlockSpec((BLOCK, LANES), lambda n: (prev(n), cv)),
                  pl.BlockSpec((BLOCK, LANES), lambda n: (cur(n), cv)),
                  pl.BlockSpec((BLOCK, SWA_W), lambda n: (cur(n), 1)),
                  pl.BlockSpec((BLOCK, SWA_W), lambda n: (cur(n), 1)),
                  pl.BlockSpec((BLOCK, LANES), lambda n: (cur(n), 0)),
                  pl.BlockSpec((SWA_HEADS, BLOCK, 2 * BLOCK), lambda n: (0, 0, 0)),
                  pl.BlockSpec(memory_space=pltpu.SMEM),
                  pl.BlockSpec((BLOCK, 2 * BLOCK), lambda n: (0, 0))],
        out_specs=[pl.BlockSpec((BLOCK, SWA_W), lambda n: (cur(n), 0)),
                   pl.BlockSpec((BLOCK, LANES), lambda n: (kout(n), 0)),
                   pl.BlockSpec((BLOCK, LANES), lambda n: (kout(n), 0)),
                   pl.BlockSpec((NUM_BUCKETS, LANES), lambda n: (0, 0)),
                   pl.BlockSpec((8, LANES), lambda n: (0, 0))],
        out_shape=[jax.ShapeDtypeStruct((s, SWA_W), BF16),
                   jax.ShapeDtypeStruct((s, LANES), BF16),
                   jax.ShapeDtypeStruct((s, LANES), BF16),
                   jax.ShapeDtypeStruct((NUM_BUCKETS, LANES), F32),
                   jax.ShapeDtypeStruct((8, LANES), F32)],
        scratch_shapes=[pltpu.VMEM((SWA_HEADS, BLOCK, 2 * BLOCK), F32),
                        pltpu.VMEM((BLOCK, LANES), F32),
                        pltpu.VMEM((BLOCK, LANES), F32),
                        pltpu.VMEM((8, LANES), F32)],
        compiler_params=_cparams(("arbitrary",)),
    )(qkv, qkv, qkv, qkv, qkv, do_bf, delta_b, lse, bias, sink, bucket)


def _post(x, target, o_fox, o_swa, z, w_o, ln_g, ln_b):
    s = x.shape[0]
    tm = min(256, s)
    nt = s // tm
    seg = 256

    def kern(x_ref, t_ref, of_ref, os_ref, z_ref, w_ref, g_ref, b_ref,
             loss_ref, dh_ref, dy_ref, mix_ref, do_ref, dz_ref, dl_ref, gg_ref, gb_ref, lacc_ref):
        step = pl.program_id(0)

        @pl.when(step == 0)
        def _():
            lacc_ref[...] = jnp.zeros_like(lacc_ref)
            gg_ref[...] = jnp.zeros_like(gg_ref)
            gb_ref[...] = jnp.zeros_like(gb_ref)

        o = jnp.concatenate([of_ref[...], os_ref[...]], axis=1)
        zz = z_ref[...]
        sig = 1.0 / (1.0 + jnp.exp(-zz))
        silu = zz * sig
        mixed = (o * silu).astype(BF16)
        mix_ref[...] = mixed
        w = w_ref[...]
        h = ALPHA * x_ref[...] + _dot(mixed, w)
        mu = jnp.mean(h, axis=1, keepdims=True)
        hc = h - mu
        var = jnp.mean(hc * hc, axis=1, keepdims=True)
        rstd = lax.rsqrt(var + LN_EPS)
        xhat = hc * rstd
        g = g_ref[...]
        err = xhat * g + b_ref[...] - t_ref[...]
        lacc_ref[...] += jnp.broadcast_to(jnp.sum(err * err, axis=0, keepdims=True), lacc_ref.shape)
        dout = err * (1.0 / D_MODEL)
        gg_ref[...] += jnp.broadcast_to(jnp.sum(dout * xhat, axis=0, keepdims=True), gg_ref.shape)
        gb_ref[...] += jnp.broadcast_to(jnp.sum(dout, axis=0, keepdims=True), gb_ref.shape)
        dxh = dout * g
        m1 = jnp.mean(dxh, axis=1, keepdims=True)
        m2 = jnp.mean(dxh * xhat, axis=1, keepdims=True)
        dh = rstd * (dxh - m1 - xhat * m2)
        dh_ref[...] = dh
        dy = dh.astype(BF16)
        dy_ref[...] = dy
        dmix = _dot_nt(dy, w)
        do = dmix * silu
        do_ref[...] = do.astype(BF16)
        dz_ref[...] = (dmix * o * (sig * (1.0 + zz * (1.0 - sig)))).astype(BF16)
        r = lax.broadcasted_iota(jnp.int32, (seg, seg), 0) // HEAD_DIM
        c = lax.broadcasted_iota(jnp.int32, (seg, seg), 1) // HEAD_DIM
        bd = jnp.where(r == c, 1.0, 0.0).astype(BF16)
        prod = do * o
        parts = [_exact_dot(bd, prod[:, j * seg:(j + 1) * seg], False) for j in range(D_MODEL // seg)]
        dl_ref[...] = jnp.concatenate(parts, axis=1)

        @pl.when(step == nt - 1)
        def _():
            tot = jnp.sum(lacc_ref[0:1, :]) * (0.5 / D_MODEL)
            loss_ref[...] = jnp.broadcast_to(tot, loss_ref.shape)

    row = lambda i: (i, 0)
    fixed = lambda i: (0, 0)
    wide = pl.BlockSpec((tm, D_MODEL), row)
    half = pl.BlockSpec((tm, FOX_W), row)
    return pl.pallas_call(
        kern, name="post",
        grid=(nt,),
        in_specs=[wide, wide, half, half, wide,
                  pl.BlockSpec((D_MODEL, D_MODEL), fixed),
                  pl.BlockSpec((1, D_MODEL), fixed),
                  pl.BlockSpec((1, D_MODEL), fixed)],
        out_specs=[pl.BlockSpec((8, LANES), fixed), wide, wide, wide, wide, wide, wide,
                   pl.BlockSpec((8, D_MODEL), fixed), pl.BlockSpec((8, D_MODEL), fixed)],
        out_shape=[jax.ShapeDtypeStruct((8, LANES), F32),
                   jax.ShapeDtypeStruct((s, D_MODEL), F32),
                   jax.ShapeDtypeStruct((s, D_MODEL), BF16),
                   jax.ShapeDtypeStruct((s, D_MODEL), BF16),
                   jax.ShapeDtypeStruct((s, D_MODEL), BF16),
                   jax.ShapeDtypeStruct((s, D_MODEL), BF16),
                   jax.ShapeDtypeStruct((s, D_MODEL), F32),
                   jax.ShapeDtypeStruct((8, D_MODEL), F32),
                   jax.ShapeDtypeStruct((8, D_MODEL), F32)],
        scratch_shapes=[pltpu.VMEM((8, D_MODEL), F32)],
        compiler_params=_cparams(("arbitrary",)),
    )(x, target, o_fox, o_swa, z, w_o, ln_g, ln_b)


def _adamw_math(w, g, m, v):
    m = ADAM_B1 * m + (1.0 - ADAM_B1) * g
    v = ADAM_B2 * v + (1.0 - ADAM_B2) * (g * g)
    m_hat = m / (1.0 - ADAM_B1 ** ADAM_STEP)
    v_hat = v / (1.0 - ADAM_B2 ** ADAM_STEP)
    delta = -ADAM_LR * (m_hat / (jnp.sqrt(v_hat) + ADAM_EPS) + ADAM_WD * w)
    return delta, m, v


def _adamw(w, g, m, v, *, name):
    r, c = w.shape
    tr = min(256, r)

    def kern(w_ref, g_ref, m_ref, v_ref, d_ref, mo_ref, vo_ref):
        d, mn, vn = _adamw_math(w_ref[...], g_ref[...], m_ref[...], v_ref[...])
        d_ref[...] = d
        mo_ref[...] = mn
        vo_ref[...] = vn

    blk = pl.BlockSpec((tr, c), lambda i: (i, 0))
    sds = jax.ShapeDtypeStruct((r, c), F32)
    return pl.pallas_call(
        kern, name=name,
        grid=(r // tr,),
        in_specs=[blk, blk, blk, blk],
        out_specs=[blk, blk, blk],
        out_shape=[sds, sds, sds],
        compiler_params=_cparams(("parallel",)),
    )(w, g, m, v)


def _position():
    x, y, c = lax.axis_index("x"), lax.axis_index("y"), lax.axis_index("c")
    chips = [(1 - x, y), (x, 1 - y), (1 - x, 1 - y)]
    return x, y, c, chips


def _chip_index(cx, cy):
    return 2 * cx + cy


def _gather_weights(w_in_bf, w_o_bf):
    shards = (w_in_bf, w_o_bf)
    n_arr = len(shards)

    def kern(*refs):
        ins, outs = refs[:n_arr], refs[n_arr:2 * n_arr]
        send_sems, recv_sems, local_sems = refs[2 * n_arr:]
        x, y, c, chips = _position()
        me = _chip_index(x, y)
        sibling = (x, y, 1 - c)

        local = [pltpu.make_async_copy(ins[a], outs[a].at[me], local_sems.at[a]) for a in range(n_arr)]
        for cp in local:
            cp.start()

        def half(ref, a):
            rows = shards[a].shape[0] // 2
            return ref.at[pl.ds(c * rows, rows), :]

        def copy(a, k, src, slot, to):
            return pltpu.make_async_remote_copy(
                src_ref=src, dst_ref=half(outs[a].at[slot], a),
                send_sem=send_sems.at[a * 6 + k], recv_sem=recv_sems.at[a * 6 + k],
                device_id=to, device_id_type=MESH)

        first = [copy(a, j, half(ins[a], a), me, (*chip, c)) for a in range(n_arr) for j, chip in enumerate(chips)]
        for cp in first:
            cp.start()
        passed = []
        for a in range(n_arr):
            for j, chip in enumerate(chips):
                slot = _chip_index(*chip)
                copy(a, j, half(ins[a], a), slot, (*chip, c)).wait_recv()
                fwd = copy(a, 3 + j, half(outs[a].at[slot], a), slot, sibling)
                fwd.start()
                passed.append(fwd)
        for a in range(n_arr):
            for j, chip in enumerate(chips):
                slot = _chip_index(*chip)
                rows = shards[a].shape[0] // 2
                dst = outs[a].at[slot].at[pl.ds((1 - c) * rows, rows), :]
                pltpu.make_async_remote_copy(
                    src_ref=dst, dst_ref=dst, send_sem=send_sems.at[a * 6 + 3 + j],
                    recv_sem=recv_sems.at[a * 6 + 3 + j], device_id=sibling, device_id_type=MESH).wait_recv()
        for cp in first + passed:
            cp.wait_send()
        for cp in local:
            cp.wait()

    hbm = pl.BlockSpec(memory_space=pl.ANY)
    return pl.pallas_call(
        kern, name="gather_weights",
        in_specs=[hbm] * n_arr,
        out_specs=[hbm] * n_arr,
        out_shape=[jax.ShapeDtypeStruct((N_CHIPS,) + w.shape, w.dtype) for w in shards],
        scratch_shapes=[pltpu.SemaphoreType.DMA((6 * n_arr,)),
                        pltpu.SemaphoreType.DMA((6 * n_arr,)),
                        pltpu.SemaphoreType.DMA((n_arr,))],
    )(*shards)


def _swap_halves(grads):
    n_arr = len(grads)

    def kern(*refs):
        ins = refs[:n_arr]
        owns = refs[n_arr:2 * n_arr]
        gots = refs[2 * n_arr:3 * n_arr]
        send_sems, recv_sems, local_sems = refs[3 * n_arr:]
        x, y, c, _ = _position()
        sibling = (x, y, 1 - c)
        local, remote = [], []
        for a in range(n_arr):
            rows = grads[a].shape[1] // 2
            local.append(pltpu.make_async_copy(ins[a].at[:, pl.ds(c * rows, rows), :], owns[a], local_sems.at[a]))
            remote.append(pltpu.make_async_remote_copy(
                src_ref=ins[a].at[:, pl.ds((1 - c) * rows, rows), :], dst_ref=gots[a],
                send_sem=send_sems.at[a], recv_sem=recv_sems.at[a], device_id=sibling, device_id_type=MESH))
        for cp in local + remote:
            cp.start()
        for cp in remote:
            cp.wait()
        for cp in local:
            cp.wait()

    hbm = pl.BlockSpec(memory_space=pl.ANY)
    half = [jax.ShapeDtypeStruct((N_CHIPS, g.shape[1] // 2, g.shape[2]), F32) for g in grads]
    outs = pl.pallas_call(
        kern, name="swap_halves",
        in_specs=[hbm] * n_arr,
        out_specs=[hbm] * (2 * n_arr),
        out_shape=half + half,
        scratch_shapes=[pltpu.SemaphoreType.DMA((n_arr,)),
                        pltpu.SemaphoreType.DMA((n_arr,)),
                        pltpu.SemaphoreType.DMA((n_arr,))],
    )(*grads)
    return outs[:n_arr], outs[n_arr:]


def _scatter_to_owners(parts):
    n_arr = len(parts)

    def kern(*refs):
        ins = refs[:n_arr]
        outs = refs[n_arr:2 * n_arr]
        send_sems, recv_sems, local_sems = refs[2 * n_arr:]
        x, y, c, chips = _position()
        me = _chip_index(x, y)
        local = [pltpu.make_async_copy(ins[a].at[me], outs[a].at[me], local_sems.at[a]) for a in range(n_arr)]
        for cp in local:
            cp.start()
        sends = []
        for a in range(n_arr):
            for j, chip in enumerate(chips):
                sends.append(pltpu.make_async_remote_copy(
                    src_ref=ins[a].at[_chip_index(*chip)], dst_ref=outs[a].at[me],
                    send_sem=send_sems.at[a * 3 + j], recv_sem=recv_sems.at[a * 3 + j],
                    device_id=(*chip, c), device_id_type=MESH))
        for cp in sends:
            cp.start()
        for a in range(n_arr):
            for j, chip in enumerate(chips):
                slot = outs[a].at[_chip_index(*chip)]
                pltpu.make_async_remote_copy(
                    src_ref=slot, dst_ref=slot, send_sem=send_sems.at[a * 3 + j],
                    recv_sem=recv_sems.at[a * 3 + j], device_id=(*chip, c), device_id_type=MESH).wait_recv()
        for cp in sends:
            cp.wait_send()
        for cp in local:
            cp.wait()

    hbm = pl.BlockSpec(memory_space=pl.ANY)
    return pl.pallas_call(
        kern, name="scatter_to_owners",
        in_specs=[hbm] * n_arr,
        out_specs=[hbm] * n_arr,
        out_shape=[jax.ShapeDtypeStruct(p.shape, F32) for p in parts],
        scratch_shapes=[pltpu.SemaphoreType.DMA((3 * n_arr,)),
                        pltpu.SemaphoreType.DMA((3 * n_arr,)),
                        pltpu.SemaphoreType.DMA((n_arr,))],
    )(*parts)


def _join_halves(halves):
    n_arr = len(halves)

    def kern(*refs):
        ins = refs[:n_arr]
        outs = refs[n_arr:2 * n_arr]
        send_sems, recv_sems, local_sems = refs[2 * n_arr:]
        x, y, c, _ = _position()
        sibling = (x, y, 1 - c)
        local, remote = [], []
        for a in range(n_arr):
            rows = halves[a].shape[0]
            mine = outs[a].at[pl.ds(c * rows, rows), :]
            local.append(pltpu.make_async_copy(ins[a], mine, local_sems.at[a]))
            remote.append(pltpu.make_async_remote_copy(
                src_ref=ins[a], dst_ref=mine, send_sem=send_sems.at[a], recv_sem=recv_sems.at[a],
                device_id=sibling, device_id_type=MESH))
        for cp in local + remote:
            cp.start()
        for a in range(n_arr):
            rows = halves[a].shape[0]
            theirs = outs[a].at[pl.ds((1 - c) * rows, rows), :]
            pltpu.make_async_remote_copy(
                src_ref=theirs, dst_ref=theirs, send_sem=send_sems.at[a], recv_sem=recv_sems.at[a],
                device_id=sibling, device_id_type=MESH).wait_recv()
        for cp in remote:
            cp.wait_send()
        for cp in local:
            cp.wait()

    hbm = pl.BlockSpec(memory_space=pl.ANY)
    return pl.pallas_call(
        kern, name="join_halves",
        in_specs=[hbm] * n_arr,
        out_specs=[hbm] * n_arr,
        out_shape=[jax.ShapeDtypeStruct((2 * h.shape[0], h.shape[1]), F32) for h in halves],
        scratch_shapes=[pltpu.SemaphoreType.DMA((n_arr,)),
                        pltpu.SemaphoreType.DMA((n_arr,)),
                        pltpu.SemaphoreType.DMA((n_arr,))],
    )(*halves)


def _add2(a, b, *, name):
    n, r, c = a.shape
    tr = min(256, r)

    def kern(a_ref, b_ref, o_ref):
        o_ref[...] = a_ref[...] + b_ref[...]

    blk = pl.BlockSpec((1, tr, c), lambda j, i: (j, i, 0))
    return pl.pallas_call(
        kern, name=name,
        grid=(n, r // tr),
        in_specs=[blk, blk],
        out_specs=blk,
        out_shape=jax.ShapeDtypeStruct(a.shape, F32),
        compiler_params=_cparams(("parallel", "parallel")),
    )(a, b)


def _sum4(a, *, name):
    n, r, c = a.shape
    tr = min(256, r)

    def kern(a_ref, o_ref):
        o_ref[...] = ((a_ref[0] + a_ref[1]) + a_ref[2]) + a_ref[3]

    return pl.pallas_call(
        kern, name=name,
        grid=(r // tr,),
        in_specs=[pl.BlockSpec((n, tr, c), lambda i: (0, i, 0))],
        out_specs=pl.BlockSpec((tr, c), lambda i: (i, 0)),
        out_shape=jax.ShapeDtypeStruct((r, c), F32),
        compiler_params=_cparams(("parallel",)),
    )(a)


def _small_allreduce_adamw(g, w, m, v):
    def kern(g_ref, w_ref, m_ref, v_ref, gs_ref, d_ref, mo_ref, vo_ref, buf_ref, send_sems, recv_sems):
        x, y, c, _ = _position()
        me = 4 * x + 2 * y + c
        buf_ref[me] = g_ref[...]
        peers = [(x, y, 1 - c)] + [(px, py, pc) for px, py in _position()[3] for pc in (c, 1 - c)]
        sends = []
        for k, peer in enumerate(peers):
            sends.append(pltpu.make_async_remote_copy(
                src_ref=g_ref, dst_ref=buf_ref.at[me], send_sem=send_sems.at[k], recv_sem=recv_sems.at[k],
                device_id=peer, device_id_type=MESH))
        for cp in sends:
            cp.start()
        for k, (px, py, pc) in enumerate(peers):
            slot = buf_ref.at[4 * px + 2 * py + pc]
            pltpu.make_async_remote_copy(
                src_ref=slot, dst_ref=slot, send_sem=send_sems.at[k], recv_sem=recv_sems.at[k],
                device_id=(px, py, pc), device_id_type=MESH).wait_recv()
        for cp in sends:
            cp.wait_send()
        tot = buf_ref[0]
        for d in range(1, N_DEV):
            tot = tot + buf_ref[d]
        gs_ref[...] = tot
        delta, mn, vn = _adamw_math(w_ref[...], tot, m_ref[...], v_ref[...])
        d_ref[...] = delta
        mo_ref[...] = mn
        vo_ref[...] = vn

    vm = pl.BlockSpec(memory_space=pltpu.VMEM)
    sds = jax.ShapeDtypeStruct((SMALL_ROWS, LANES), F32)
    return pl.pallas_call(
        kern, name="small_allreduce_adamw",
        in_specs=[vm] * 4,
        out_specs=[vm] * 4,
        out_shape=[sds] * 4,
        scratch_shapes=[pltpu.VMEM((N_DEV, SMALL_ROWS, LANES), F32),
                        pltpu.SemaphoreType.DMA((N_DEV - 1,)),
                        pltpu.SemaphoreType.DMA((N_DEV - 1,))],
    )(g, w, m, v)


def _to_padded_cols(w):
    pad = jnp.zeros((w.shape[0], N_C - FOX_HEADS), w.dtype)
    return jnp.concatenate([w[:, 0:1536], w[:, 2056:2824], w[:, 1536:1544], pad,
                            w[:, 1544:2056], w[:, 2824:3336]], axis=1)


def _from_padded_cols(g):
    return jnp.concatenate([g[:, 0:1536], g[:, OFF_C:OFF_C + FOX_HEADS], g[:, OFF_B:OFF_B + FOX_W],
                            g[:, 1536:N_A], g[:, OFF_B + FOX_W:N_PAD]], axis=1)


def _pack_small(b_f, rel_bias, sink, ln_g, ln_b):
    row = lambda v: jnp.pad(v.reshape(1, -1), ((0, 0), (0, LANES - v.size)))
    return jnp.concatenate([ln_g.reshape(8, LANES), ln_b.reshape(8, LANES), rel_bias.reshape(2, LANES),
                            row(b_f), row(sink), jnp.zeros((4, LANES), F32)], axis=0)


def _unpack_small(p):
    ln_g = p[0:8].reshape(1, D_MODEL)
    ln_b = p[8:16].reshape(1, D_MODEL)
    rel_bias = p[16:18].reshape(NUM_BUCKETS, SWA_HEADS)
    b_f = p[18:19, :FOX_HEADS]
    sink = p[19:20, :SWA_HEADS]
    return b_f, rel_bias, sink, ln_g, ln_b


def _heads_to_rows(a_b):
    return a_b[:, ::HEAD_DIM].T.reshape(FOX_HEADS, 1, a_b.shape[0])


def kernel(x, w_in, b_f, rel_bias, sink, w_o, ln_g, ln_b, loss_target, m_w_in, m_b_f, m_rel_bias, m_sink, m_w_o, m_ln_g, m_ln_b, v_w_in, v_b_f, v_rel_bias, v_sink, v_w_o, v_ln_g, v_ln_b):
    x2 = x[0]
    tgt = loss_target[0]
    s = x2.shape[0]
    w_in2, w_o2 = w_in[0], w_o[0]

    w_in_all, w_o_all = _gather_weights(w_in2.astype(BF16), w_o2.astype(BF16))
    w_full = jnp.concatenate([w_in_all[j] for j in range(N_CHIPS)], axis=1)
    w_pad = _to_padded_cols(w_full)
    w_o_full = w_o_all.reshape(D_MODEL, D_MODEL)

    x_bf = x2.astype(BF16)
    qkv = _matmul_nn(x_bf, w_pad, n_off=0, n_out=N_A, tm=512, tn=768, out_dtype=BF16, name="proj_qkv")
    z = _matmul_nn(x_bf, w_pad, n_off=OFF_B, n_out=N_B, tm=512, tn=512, out_dtype=F32, name="proj_gate")
    ffp = _matmul_nn(x_bf, w_pad, n_off=OFF_C, n_out=N_C, tm=512, tn=N_C, out_dtype=F32, name="proj_forget")
    bfp = jnp.pad(b_f, ((0, 0), (0, LANES - FOX_HEADS)))
    cum = _cum_fwd(ffp, bfp)
    cum_t3 = cum[:, :FOX_HEADS].T.reshape(FOX_HEADS, 1, s)
    vt = qkv[:, COL_FV:COL_FV + FOX_W].T
    o_fox, lse_t3 = _fox_fwd(qkv, vt, cum_t3, cum)
    bucket = jnp.asarray(_bucket_table())
    bias = _swa_bias(rel_bias, bucket)
    o_swa, lse_swa = _swa_fwd(qkv, bias, sink)

    loss8, dh, dy, mixed, do_bf, dz, delta_b, gg8, gb8 = _post(
        x2, tgt, o_fox, o_swa, z, w_o_full, ln_g, ln_b)
    loss = lax.psum(loss8[0, 0], ("x", "y", "c"))
    grad_w_o_full = _matmul_tn(mixed, dy, tm=512, tn=512, tk=512, name="grad_w_o")

    delta_t3 = _heads_to_rows(delta_b[:, :FOX_W])
    dqt_fox, dk_fox, dv_fox, dcum_k, dcum_q = _fox_bwd(qkv, do_bf, cum_t3, cum, lse_t3, delta_t3)
    dcum_q = jnp.pad(dcum_q.reshape(FOX_HEADS, s).T, ((0, 0), (0, LANES - FOX_HEADS)))
    dff, gbf8 = _cum_bwd(dcum_k, dcum_q, ffp, bfp)
    dq_swa, dk_swa, dv_swa, grb, gsk8 = _swa_bwd(qkv, do_bf, delta_b, lse_swa, bias, sink, bucket)

    dproj = jnp.concatenate([dqt_fox.T.astype(BF16), dk_fox, dv_fox, dq_swa, dk_swa, dv_swa, dff, dz], axis=1)
    grad_x = _grad_x_matmul(dproj, w_pad, dh, tm=512, tn=512, name="grad_x")
    grad_w_pad = _matmul_tn(x_bf, dproj, tm=512, tn=512, tk=512, name="grad_w_in")
    grad_w_in_full = _from_padded_cols(grad_w_pad)

    shard_cols = D_IN // N_CHIPS
    g_in4 = jnp.stack([grad_w_in_full[:, j * shard_cols:(j + 1) * shard_cols] for j in range(N_CHIPS)])
    g_o4 = grad_w_o_full.reshape(N_CHIPS, D_MODEL // N_CHIPS, D_MODEL)
    owns, gots = _swap_halves([g_in4, g_o4])
    parts = [_add2(owns[0], gots[0], name="pair_sum_w_in"), _add2(owns[1], gots[1], name="pair_sum_w_o")]
    slabs = _scatter_to_owners(parts)
    halves = [_sum4(slabs[0], name="chip_sum_w_in"), _sum4(slabs[1], name="chip_sum_w_o")]
    g_w_in, g_w_o = _join_halves(halves)

    d_w_in, nm_w_in, nv_w_in = _adamw(w_in2, g_w_in, m_w_in[0], v_w_in[0], name="adamw_w_in")
    d_w_o, nm_w_o, nv_w_o = _adamw(w_o2, g_w_o, m_w_o[0], v_w_o[0], name="adamw_w_o")

    g_small = _pack_small(gbf8[0:1, :FOX_HEADS], grb[:, :SWA_HEADS], gsk8[0:1, :SWA_HEADS], gg8[0:1], gb8[0:1])
    w_small = _pack_small(b_f, rel_bias, sink, ln_g, ln_b)
    m_small = _pack_small(m_b_f, m_rel_bias, m_sink, m_ln_g, m_ln_b)
    v_small = _pack_small(v_b_f, v_rel_bias, v_sink, v_ln_g, v_ln_b)
    gs, ds, ms, vs = _small_allreduce_adamw(g_small, w_small, m_small, v_small)
    g_bf, g_rb, g_sk, g_lg, g_lb = _unpack_small(gs)
    d_bf, d_rb, d_sk, d_lg, d_lb = _unpack_small(ds)
    m_bf, m_rb, m_sk, m_lg, m_lb = _unpack_small(ms)
    v_bf, v_rb, v_sk, v_lg, v_lb = _unpack_small(vs)

    e = lambda a: a[None]
    return (loss, e(grad_x),
            e(g_w_in), g_bf, g_rb, g_sk, e(g_w_o), g_lg, g_lb,
            e(d_w_in), d_bf, d_rb, d_sk, e(d_w_o), d_lg, d_lb,
            e(nm_w_in), m_bf, m_rb, m_sk, e(nm_w_o), m_lg, m_lb,
            e(nv_w_in), v_bf, v_rb, v_sk, e(nv_w_o), v_lg, v_lb)
```

```python
import functools
import math

import numpy as np
import jax
import jax.numpy as jnp
from jax import lax
from jax.experimental import pallas as pl
from jax.experimental.pallas import tpu as pltpu

F32 = jnp.float32
BF16 = jnp.bfloat16

D_MODEL = 1024
HEAD_DIM = 64
FOX_HEADS = 8
SWA_HEADS = 8
SWA_KV_HEADS = 2
SWA_GROUP = 4
FOX_W = 512
SWA_W = 512
SWA_KV_W = 128
BLOCK = 128
NUM_BUCKETS = 32
MAX_DISTANCE = 128
LN_EPS = 1e-5
NEG = -1e30
ALPHA = 2.0 ** 0.25
QK_SCALE = 0.125

ADAM_LR = 0.001
ADAM_B1 = 0.9
ADAM_B2 = 0.999
ADAM_EPS = 1e-08
ADAM_WD = 0.01
ADAM_STEP = 10

D_IN = 3336
SHARD_PAD = 896
N_A = 2304
N_C = 256
N_B = 1024
OFF_C = N_A
OFF_B = N_A + N_C
N_PAD = N_A + N_C + N_B
COL_FK, COL_FV, COL_SQ, COL_SK, COL_SV = 512, 1024, 1536, 2048, 2176

LANES = 128
FOX_T = 256
VMEM_LIMIT = 56 * 1024 * 1024

MESH = pl.DeviceIdType.MESH
N_CHIPS = 4
N_DEV = 8
SMALL_ROWS = 24


def _cparams(sem=None):
    return pltpu.CompilerParams(dimension_semantics=sem, vmem_limit_bytes=VMEM_LIMIT)


def _split3(x):
    hi = x.astype(BF16)
    r = x - hi.astype(F32)
    mid = r.astype(BF16)
    lo = (r - mid.astype(F32)).astype(BF16)
    return hi, mid, lo


def _dot(a, b):
    return jnp.dot(a, b, preferred_element_type=F32)


def _dot_nt(a, b):
    return lax.dot_general(a, b, (((1,), (1,)), ((), ())), preferred_element_type=F32)


def _dot_tn(a, b):
    return lax.dot_general(a, b, (((0,), (0,)), ((), ())), preferred_element_type=F32)


def _matmul_nn(a, b, *, n_off, n_out, tm, tn, out_dtype, name):
    m, k = a.shape
    joff = n_off // tn

    def kern(a_ref, b_ref, o_ref):
        o_ref[...] = _dot(a_ref[...], b_ref[...]).astype(o_ref.dtype)

    return pl.pallas_call(
        kern, name=name,
        grid=(n_out // tn, m // tm),
        in_specs=[pl.BlockSpec((tm, k), lambda j, i: (i, 0)),
                  pl.BlockSpec((k, tn), lambda j, i: (0, j + joff))],
        out_specs=pl.BlockSpec((tm, tn), lambda j, i: (i, j)),
        out_shape=jax.ShapeDtypeStruct((m, n_out), out_dtype),
        compiler_params=_cparams(("parallel", "parallel")),
    )(a, b)


def _grad_x_matmul(dproj, w_pad, dh, *, tm, tn, name):
    m, k = dproj.shape
    n = w_pad.shape[0]

    def kern(a_ref, b_ref, dh_ref, o_ref):
        o_ref[...] = ALPHA * dh_ref[...] + _dot_nt(a_ref[...], b_ref[...])

    return pl.pallas_call(
        kern, name=name,
        grid=(n // tn, m // tm),
        in_specs=[pl.BlockSpec((tm, k), lambda j, i: (i, 0)),
                  pl.BlockSpec((tn, k), lambda j, i: (j, 0)),
                  pl.BlockSpec((tm, tn), lambda j, i: (i, j))],
        out_specs=pl.BlockSpec((tm, tn), lambda j, i: (i, j)),
        out_shape=jax.ShapeDtypeStruct((m, n), F32),
        compiler_params=_cparams(("parallel", "parallel")),
    )(dproj, w_pad, dh)


def _matmul_acc(at, b, *, tm, tn, tk, name):
    m, s = at.shape
    n = b.shape[1]

    def kern(a_ref, b_ref, o_ref):
        @pl.when(pl.program_id(2) == 0)
        def _():
            o_ref[...] = jnp.zeros_like(o_ref)
        o_ref[...] += _dot(a_ref[...], b_ref[...])

    return pl.pallas_call(
        kern, name=name,
        grid=(m // tm, n // tn, s // tk),
        in_specs=[pl.BlockSpec((tm, tk), lambda i, j, k: (i, k)),
                  pl.BlockSpec((tk, tn), lambda i, j, k: (k, j))],
        out_specs=pl.BlockSpec((tm, tn), lambda i, j, k: (i, j)),
        out_shape=jax.ShapeDtypeStruct((m, n), F32),
        compiler_params=_cparams(("parallel", "parallel", "arbitrary")),
    )(at, b)


def _tri(n, lower):
    r = lax.broadcasted_iota(jnp.int32, (n, n), 0)
    c = lax.broadcasted_iota(jnp.int32, (n, n), 1)
    keep = (c <= r) if lower else (c >= r)
    return jnp.where(keep, 1.0, 0.0).astype(BF16)


def _exact_dot(mat_bf16, x_f32, left):
    out = None
    for piece in _split3(x_f32):
        t = _dot(mat_bf16, piece) if left else _dot(piece, mat_bf16)
        out = t if out is None else out + t
    return out


def _log_sigmoid(z):
    return jnp.minimum(z, 0.0) - jnp.log(1.0 + jnp.exp(-jnp.abs(z)))


def _cum_fwd(ffp, bfp):
    s = ffp.shape[0]
    t = min(256, s)

    def kern(ff_ref, b_ref, cum_ref, carry_ref):
        @pl.when(pl.program_id(0) == 0)
        def _():
            carry_ref[...] = jnp.zeros_like(carry_ref)
        lane = lax.broadcasted_iota(jnp.int32, (1, LANES), 1)
        lf = _log_sigmoid(ff_ref[...] + b_ref[...])
        lf = jnp.where(lane < FOX_HEADS, lf, 0.0)
        cum = _exact_dot(_tri(t, True), lf, True) + carry_ref[0:1, :]
        cum_ref[...] = cum
        carry_ref[...] = jnp.broadcast_to(cum[t - 1:t, :], carry_ref.shape)

    return pl.pallas_call(
        kern, name="cum_fwd",
        grid=(s // t,),
        in_specs=[pl.BlockSpec((t, LANES), lambda i: (i, 0)),
                  pl.BlockSpec((1, LANES), lambda i: (0, 0))],
        out_specs=pl.BlockSpec((t, LANES), lambda i: (i, 0)),
        out_shape=jax.ShapeDtypeStruct((s, LANES), F32),
        scratch_shapes=[pltpu.VMEM((8, LANES), F32)],
        compiler_params=_cparams(("arbitrary",)),
    )(ffp, bfp)


def _cum_bwd(dcum_k, dcum_q, ffp, bfp):
    s = dcum_k.shape[0]
    t = min(256, s)
    nb = s // t

    def kern(dck_ref, dcq_ref, ff_ref, b_ref, dff_ref, gb_ref, carry_ref):
        @pl.when(pl.program_id(0) == 0)
        def _():
            carry_ref[...] = jnp.zeros_like(carry_ref)
            gb_ref[...] = jnp.zeros_like(gb_ref)
        lane = lax.broadcasted_iota(jnp.int32, (1, LANES), 1)
        dlf = _exact_dot(_tri(t, False), dck_ref[...] + dcq_ref[...], True) + carry_ref[0:1, :]
        carry_ref[...] = jnp.broadcast_to(dlf[0:1, :], carry_ref.shape)
        z = ff_ref[...] + b_ref[...]
        dff = jnp.where(lane < FOX_HEADS, dlf / (1.0 + jnp.exp(z)), 0.0)
        gb_ref[...] += jnp.broadcast_to(jnp.sum(dff, axis=0, keepdims=True), gb_ref.shape)
        dff_ref[...] = jnp.concatenate([dff, jnp.zeros_like(dff)], axis=1).astype(BF16)

    return pl.pallas_call(
        kern, name="cum_bwd",
        grid=(nb,),
        in_specs=[pl.BlockSpec((t, LANES), lambda i: (nb - 1 - i, 0)),
                  pl.BlockSpec((t, LANES), lambda i: (nb - 1 - i, 0)),
                  pl.BlockSpec((t, LANES), lambda i: (nb - 1 - i, 0)),
                  pl.BlockSpec((1, LANES), lambda i: (0, 0))],
        out_specs=[pl.BlockSpec((t, N_C), lambda i: (nb - 1 - i, 0)),
                   pl.BlockSpec((8, LANES), lambda i: (0, 0))],
        out_shape=[jax.ShapeDtypeStruct((s, N_C), BF16),
                   jax.ShapeDtypeStruct((8, LANES), F32)],
        scratch_shapes=[pltpu.VMEM((8, LANES), F32)],
        compiler_params=_cparams(("arbitrary",)),
    )(dcum_k, dcum_q, ffp, bfp)


def _resident(shape, index_map):
    return pl.BlockSpec(shape, index_map, pipeline_mode=pl.Buffered(1))


def _fox_fwd(qkv, vt, cum_t3, cum):
    s = qkv.shape[0]
    t = min(FOX_T, s)
    nq = s // t
    nh = FOX_HEADS

    def kern(q_ref, k_ref, vt_ref, ct_ref, c_ref, o_ref, lse_ref, m_ref, l_ref, acc_ref):
        i = pl.program_id(0)
        lane = lax.broadcasted_iota(jnp.int32, (1, LANES), 1)
        krow = lax.broadcasted_iota(jnp.int32, (t, t), 0)
        qcol = lax.broadcasted_iota(jnp.int32, (t, t), 1)
        causal = krow <= qcol
        q0 = pl.multiple_of(i * t, t)
        qts, crefs = [], []
        for h in range(nh):
            p, a = divmod(h, 2)
            q2 = q_ref[:, p * LANES:(p + 1) * LANES] * jnp.asarray(QK_SCALE, BF16)
            sel = (lane < HEAD_DIM) if a == 0 else (lane >= HEAD_DIM)
            qts.append(jnp.where(sel, q2, jnp.zeros_like(q2)).astype(F32).T.astype(BF16))
            crefs.append(ct_ref[h, :, pl.ds(q0, LANES)][:, 0:1])
        m_ref[...] = jnp.full(m_ref.shape, NEG, F32)
        l_ref[...] = jnp.zeros_like(l_ref)
        acc_ref[...] = jnp.zeros_like(acc_ref)

        def tile(j, masked):
            k0 = pl.multiple_of(j * t, t)
            cb = c_ref[pl.ds(k0, t), :]
            sts = [_dot(k_ref[pl.ds(k0, t), (h // 2) * LANES:(h // 2 + 1) * LANES], qts[h]) for h in range(nh)]
            pts, scales = [], []
            for h in range(nh):
                u = sts[h] - (cb[:, h:h + 1] - crefs[h])
                if masked:
                    u = jnp.where(causal, u, NEG)
                m_old = m_ref[h]
                m_new = jnp.maximum(m_old, jnp.max(u, axis=0, keepdims=True))
                scale = jnp.exp(m_old - m_new)
                p = jnp.exp(u - m_new)
                l_ref[h] = scale * l_ref[h] + jnp.sum(p, axis=0, keepdims=True)
                m_ref[h] = m_new
                pts.append(p.astype(BF16))
                scales.append(scale)
            for h in range(nh):
                vth = vt_ref[h * HEAD_DIM:(h + 1) * HEAD_DIM, pl.ds(k0, t)]
                acc_ref[h] = scales[h] * acc_ref[h] + _dot(vth, pts[h])

        def body(j, c):
            tile(j, False)
            return c
        lax.fori_loop(0, i, body, 0)
        tile(i, True)

        for p in range(nh // 2):
            ot = jnp.concatenate([acc_ref[2 * p + a] * (1.0 / l_ref[2 * p + a]) for a in range(2)], axis=0)
            o_ref[:, p * LANES:(p + 1) * LANES] = ot.T
        for h in range(nh):
            lse_ref[h, :, pl.ds(q0, t)] = m_ref[h] + jnp.log(l_ref[h])

    return pl.pallas_call(
        kern, name="fox_fwd",
        grid=(nq,),
        in_specs=[pl.BlockSpec((t, FOX_W), lambda i: (i, 0)),
                  _resident((s, FOX_W), lambda i: (0, COL_FK // FOX_W)),
                  _resident((FOX_W, s), lambda i: (0, 0)),
                  _resident((nh, 1, s), lambda i: (0, 0, 0)),
                  _resident((s, LANES), lambda i: (0, 0))],
        out_specs=[pl.BlockSpec((t, FOX_W), lambda i: (i, 0)),
                   pl.BlockSpec((nh, 1, s), lambda i: (0, 0, 0))],
        out_shape=[jax.ShapeDtypeStruct((s, FOX_W), F32),
                   jax.ShapeDtypeStruct((nh, 1, s), F32)],
        scratch_shapes=[pltpu.VMEM((nh, 1, t), F32),
                        pltpu.VMEM((nh, 1, t), F32),
                        pltpu.VMEM((nh, HEAD_DIM, t), F32)],
        compiler_params=_cparams(("arbitrary",)),
    )(qkv, qkv, vt, cum_t3, cum)


def _fox_bwd(qkv, do_bf, cum_t3, cum, lse_t3, delta_t3):
    s = qkv.shape[0]
    t = min(FOX_T, s)
    nq = s // t
    nh = FOX_HEADS
    npair = nh // 2

    def kern(q_ref, do_ref, k_ref, v_ref, ct_ref, c_ref, lse_ref, dl_ref,
             dqt_ref, dk_ref, dv_ref, dc_ref, dcq_ref, accv_ref, acck_ref, accd_ref):
        kj = pl.program_id(0)
        lane = lax.broadcasted_iota(jnp.int32, (1, LANES), 1)
        sub = lax.broadcasted_iota(jnp.int32, (LANES, 1), 0)
        krow = lax.broadcasted_iota(jnp.int32, (t, t), 0)
        qcol = lax.broadcasted_iota(jnp.int32, (t, t), 1)
        causal = krow <= qcol
        sels = [lane < HEAD_DIM, lane >= HEAD_DIM]
        subsels = [sub < HEAD_DIM, sub >= HEAD_DIM]

        @pl.when(kj == 0)
        def _():
            dqt_ref[...] = jnp.zeros_like(dqt_ref)
            dcq_ref[...] = jnp.zeros_like(dcq_ref)

        accv_ref[...] = jnp.zeros_like(accv_ref)
        acck_ref[...] = jnp.zeros_like(acck_ref)
        accd_ref[...] = jnp.zeros_like(accd_ref)
        cb = c_ref[...]
        k2s, v2s, kts = [], [], []
        for p in range(npair):
            k2 = k_ref[:, p * LANES:(p + 1) * LANES]
            k2s.append(k2)
            v2s.append(v_ref[:, p * LANES:(p + 1) * LANES])
            kt = k2.astype(F32).T * QK_SCALE
            kts.append([jnp.where(subsels[a], kt, 0.0).astype(BF16) for a in range(2)])
        css = [cb[:, h:h + 1] for h in range(nh)]

        def tile(i, masked):
            q0 = pl.multiple_of(i * t, t)
            sts, dpts, qms, doms = [], [], [], []
            for h in range(nh):
                p, a = divmod(h, 2)
                qi = q_ref[pl.ds(q0, t), p * LANES:(p + 1) * LANES] * jnp.asarray(QK_SCALE, BF16)
                doi = do_ref[pl.ds(q0, t), p * LANES:(p + 1) * LANES]
                qm = jnp.where(sels[a], qi, jnp.zeros_like(qi))
                dom = jnp.where(sels[a], doi, jnp.zeros_like(doi))
                qms.append(qm)
                doms.append(dom)
                sts.append(_dot_nt(k2s[p], qm))
                dpts.append(_dot_nt(v2s[p], dom))
            pts, dsts = [], []
            for h in range(nh):
                cref = ct_ref[h, :, pl.ds(q0, LANES)][:, 0:1]
                pt = jnp.exp(sts[h] - (css[h] - cref) - lse_ref[h, :, pl.ds(q0, t)])
                if masked:
                    pt = jnp.where(causal, pt, 0.0)
                ds32 = pt * (dpts[h] - dl_ref[h, :, pl.ds(q0, t)])
                part = ds32[:, 0:LANES]
                for c in range(1, t // LANES):
                    part = part + ds32[:, c * LANES:(c + 1) * LANES]
                accd_ref[h] += part
                dcq_ref[h, :, pl.ds(q0, t)] += jnp.sum(ds32, axis=0, keepdims=True)
                pts.append(pt.astype(BF16))
                dsts.append(ds32.astype(BF16))
            for p in range(npair):
                ha, hb = 2 * p, 2 * p + 1
                accv_ref[p] += _dot(pts[ha], doms[ha]) + _dot(pts[hb], doms[hb])
                acck_ref[p] += _dot(dsts[ha], qms[ha]) + _dot(dsts[hb], qms[hb])
                dqt_ref[p * LANES:(p + 1) * LANES, pl.ds(q0, t)] += (
                    _dot(kts[p][0], dsts[ha]) + _dot(kts[p][1], dsts[hb]))

        tile(kj, True)

        def body(i, c):
            tile(i, False)
            return c
        lax.fori_loop(kj + 1, nq, body, 0)

        dc = jnp.zeros((t, LANES), F32)
        for h in range(nh):
            dc = jnp.where(lane == h, -jnp.sum(accd_ref[h], axis=1, keepdims=True), dc)
        dc_ref[...] = dc
        for p in range(npair):
            dv_ref[:, p * LANES:(p + 1) * LANES] = accv_ref[p].astype(BF16)
            dk_ref[:, p * LANES:(p + 1) * LANES] = acck_ref[p].astype(BF16)

    whole = lambda kj: (0, 0, 0)
    return pl.pallas_call(
        kern, name="fox_bwd",
        grid=(nq,),
        in_specs=[_resident((s, FOX_W), lambda kj: (0, 0)),
                  _resident((s, FOX_W), lambda kj: (0, 0)),
                  pl.BlockSpec((t, FOX_W), lambda kj: (kj, COL_FK // FOX_W)),
                  pl.BlockSpec((t, FOX_W), lambda kj: (kj, COL_FV // FOX_W)),
                  _resident((nh, 1, s), whole),
                  pl.BlockSpec((t, LANES), lambda kj: (kj, 0)),
                  _resident((nh, 1, s), whole),
                  _resident((nh, 1, s), whole)],
        out_specs=[_resident((FOX_W, s), lambda kj: (0, 0)),
                   pl.BlockSpec((t, FOX_W), lambda kj: (kj, 0)),
                   pl.BlockSpec((t, FOX_W), lambda kj: (kj, 0)),
                   pl.BlockSpec((t, LANES), lambda kj: (kj, 0)),
                   _resident((nh, 1, s), whole)],
        out_shape=[jax.ShapeDtypeStruct((FOX_W, s), F32),
                   jax.ShapeDtypeStruct((s, FOX_W), BF16),
                   jax.ShapeDtypeStruct((s, FOX_W), BF16),
                   jax.ShapeDtypeStruct((s, LANES), F32),
                   jax.ShapeDtypeStruct((nh, 1, s), F32)],
        scratch_shapes=[pltpu.VMEM((npair, t, LANES), F32),
                        pltpu.VMEM((npair, t, LANES), F32),
                        pltpu.VMEM((nh, t, LANES), F32)],
        compiler_params=_cparams(("arbitrary",)),
    )(qkv, do_bf, qkv, qkv, cum_t3, cum, lse_t3, delta_t3)


def _bucket_table():
    qi = np.arange(BLOCK)[:, None]
    kj = np.arange(2 * BLOCK)[None, :]
    rel = np.maximum(qi + BLOCK - kj, 0).astype(np.int32)
    max_exact = NUM_BUCKETS // 2
    relf = np.maximum(rel, 1).astype(np.float32)
    large = max_exact + (np.log(relf / np.float32(max_exact)) / np.float32(math.log(MAX_DISTANCE / max_exact))
                         * np.float32(NUM_BUCKETS - max_exact)).astype(np.int32)
    large = np.minimum(large, NUM_BUCKETS - 1)
    return np.where(rel < max_exact, rel, large).astype(np.int32)


def _swa_bias(rel_bias, bucket):
    def kern(rb_ref, bk_ref, o_ref):
        bk = bk_ref[...]
        for h in range(SWA_HEADS):
            acc = jnp.zeros((BLOCK, 2 * BLOCK), F32)
            for b in range(NUM_BUCKETS):
                acc = jnp.where(bk == b, rb_ref[b, h], acc)
            o_ref[h] = acc

    return pl.pallas_call(
        kern, name="swa_bias",
        in_specs=[pl.BlockSpec(memory_space=pltpu.SMEM),
                  pl.BlockSpec(memory_space=pltpu.VMEM)],
        out_specs=pl.BlockSpec(memory_space=pltpu.VMEM),
        out_shape=jax.ShapeDtypeStruct((SWA_HEADS, BLOCK, 2 * BLOCK), F32),
        compiler_params=_cparams(),
    )(rel_bias, bucket)


def _swa_mask(n):
    qi = lax.broadcasted_iota(jnp.int32, (BLOCK, 2 * BLOCK), 0)
    kj = lax.broadcasted_iota(jnp.int32, (BLOCK, 2 * BLOCK), 1)
    rel = qi + BLOCK - kj
    band = (rel >= 0) & (rel < BLOCK)
    return band & ((kj >= BLOCK) | (n > 0))


def _swa_fwd(qkv, bias, sink):
    s = qkv.shape[0]
    nb = s // BLOCK

    def kern(q_ref, kp_ref, kc_ref, vp_ref, vc_ref, bias_ref, sink_ref, o_ref, lse_ref):
        n = pl.program_id(0)
        mask = _swa_mask(n)
        lane = lax.broadcasted_iota(jnp.int32, (1, LANES), 1)
        q = q_ref[...] * jnp.asarray(QK_SCALE, BF16)
        k = jnp.concatenate([kp_ref[...], kc_ref[...]], axis=0)
        v = jnp.concatenate([vp_ref[...], vc_ref[...]], axis=0)
        kgs = [k[:, g * HEAD_DIM:(g + 1) * HEAD_DIM] for g in range(SWA_KV_HEADS)]
        vgs = [v[:, g * HEAD_DIM:(g + 1) * HEAD_DIM] for g in range(SWA_KV_HEADS)]
        raw = [_dot_nt(q[:, h * HEAD_DIM:(h + 1) * HEAD_DIM], kgs[h // SWA_GROUP]) for h in range(SWA_HEADS)]
        probs = []
        lse_all = jnp.zeros((BLOCK, LANES), F32)
        for h in range(SWA_HEADS):
            sc = jnp.where(mask, raw[h] + bias_ref[h], NEG)
            sk = sink_ref[0, h]
            m = jnp.maximum(jnp.max(sc, axis=1, keepdims=True), sk)
            p = jnp.exp(sc - m)
            l = jnp.sum(p, axis=1, keepdims=True) + jnp.exp(sk - m)
            probs.append((p * (1.0 / l)).astype(BF16))
            lse_all = jnp.where(lane == h, m + jnp.log(l), lse_all)
        outs = [_dot(probs[h], vgs[h // SWA_GROUP]) for h in range(SWA_HEADS)]
        o_ref[...] = jnp.concatenate(outs, axis=1)
        lse_ref[...] = lse_all

    cq, ck, cv = COL_SQ // SWA_W, COL_SK // LANES, COL_SV // LANES
    prev = lambda n: jnp.maximum(n - 1, 0)
    return pl.pallas_call(
        kern, name="swa_fwd",
        grid=(nb,),
        in_specs=[pl.BlockSpec((BLOCK, SWA_W), lambda n: (n, cq)),
                  pl.BlockSpec((BLOCK, LANES), lambda n: (prev(n), ck)),
                  pl.BlockSpec((BLOCK, LANES), lambda n: (n, ck)),
                  pl.BlockSpec((BLOCK, LANES), lambda n: (prev(n), cv)),
                  pl.BlockSpec((BLOCK, LANES), lambda n: (n, cv)),
                  pl.BlockSpec((SWA_HEADS, BLOCK, 2 * BLOCK), lambda n: (0, 0, 0)),
                  pl.BlockSpec(memory_space=pltpu.SMEM)],
        out_specs=[pl.BlockSpec((BLOCK, SWA_W), lambda n: (n, 0)),
                   pl.BlockSpec((BLOCK, LANES), lambda n: (n, 0))],
        out_shape=[jax.ShapeDtypeStruct((s, SWA_W), F32),
                   jax.ShapeDtypeStruct((s, LANES), F32)],
        compiler_params=_cparams(("parallel",)),
    )(qkv, qkv, qkv, qkv, qkv, bias, sink)


def _swa_bwd(qkv, do_bf, delta_b, lse, bias, sink, bucket):
    s = qkv.shape[0]
    nb = s // BLOCK

    def kern(q_ref, kp_ref, kc_ref, vp_ref, vc_ref, do_ref, dl_ref, lse_ref, bias_ref, sink_ref, bk_ref,
             dq_ref, dk_ref, dv_ref, grb_ref, gsk_ref, dbias_ref, ck_ref, cv_ref, sk_ref):
        n = pl.program_id(0)
        lane = lax.broadcasted_iota(jnp.int32, (1, LANES), 1)

        @pl.when(n == 0)
        def _():
            dbias_ref[...] = jnp.zeros_like(dbias_ref)
            ck_ref[...] = jnp.zeros_like(ck_ref)
            cv_ref[...] = jnp.zeros_like(cv_ref)
            sk_ref[...] = jnp.zeros_like(sk_ref)

        @pl.when(n < nb)
        def _():
            mask = _swa_mask(n)
            q = q_ref[...] * jnp.asarray(QK_SCALE, BF16)
            k = jnp.concatenate([kp_ref[...], kc_ref[...]], axis=0)
            v = jnp.concatenate([vp_ref[...], vc_ref[...]], axis=0)
            do = do_ref[...]
            dl = dl_ref[...]
            lse_all = lse_ref[...]
            dks = [None] * SWA_KV_HEADS
            dvs = [None] * SWA_KV_HEADS
            gsk = jnp.zeros((1, LANES), F32)
            kgs = [k[:, g * HEAD_DIM:(g + 1) * HEAD_DIM] for g in range(SWA_KV_HEADS)]
            vgs = [v[:, g * HEAD_DIM:(g + 1) * HEAD_DIM] for g in range(SWA_KV_HEADS)]
            qhs = [q[:, h * HEAD_DIM:(h + 1) * HEAD_DIM] for h in range(SWA_HEADS)]
            dohs = [do[:, h * HEAD_DIM:(h + 1) * HEAD_DIM] for h in range(SWA_HEADS)]
            raw = [_dot_nt(qhs[h], kgs[h // SWA_GROUP]) for h in range(SWA_HEADS)]
            dps = [_dot_nt(dohs[h], vgs[h // SWA_GROUP]) for h in range(SWA_HEADS)]
            ps, dss = [], []
            for h in range(SWA_HEADS):
                lse_h = lse_all[:, h:h + 1]
                dlt = dl[:, h * HEAD_DIM:h * HEAD_DIM + 1]
                sc = jnp.where(mask, raw[h] + bias_ref[h], NEG)
                p = jnp.exp(sc - lse_h)
                ds = p * (dps[h] - dlt)
                dbias_ref[h] += ds
                p_sink = jnp.exp(sink_ref[0, h] - lse_h)
                gsk = gsk + jnp.where(lane == h, -jnp.sum(p_sink * dlt), 0.0)
                ps.append(p.astype(BF16))
                dss.append(ds.astype(BF16))
            dqs = [_dot(dss[h], kgs[h // SWA_GROUP]) * QK_SCALE for h in range(SWA_HEADS)]
            for h in range(SWA_HEADS):
                g = h // SWA_GROUP
                dk_h = _dot_tn(dss[h], qhs[h])
                dv_h = _dot_tn(ps[h], dohs[h])
                dks[g] = dk_h if dks[g] is None else dks[g] + dk_h
                dvs[g] = dv_h if dvs[g] is None else dvs[g] + dv_h
            dq_ref[...] = jnp.concatenate(dqs, axis=1).astype(BF16)
            sk_ref[...] += jnp.broadcast_to(gsk, sk_ref.shape)
            dk2 = jnp.concatenate(dks, axis=1)
            dv2 = jnp.concatenate(dvs, axis=1)
            dk_ref[...] = (ck_ref[...] + dk2[:BLOCK]).astype(BF16)
            dv_ref[...] = (cv_ref[...] + dv2[:BLOCK]).astype(BF16)
            ck_ref[...] = dk2[BLOCK:]
            cv_ref[...] = dv2[BLOCK:]

        @pl.when(n == nb)
        def _():
            dk_ref[...] = ck_ref[...].astype(BF16)
            dv_ref[...] = cv_ref[...].astype(BF16)
            gsk_ref[...] = sk_ref[...]
            bk = bk_ref[...]
            rowi = lax.broadcasted_iota(jnp.int32, (NUM_BUCKETS, LANES), 0)
            lanei = lax.broadcasted_iota(jnp.int32, (NUM_BUCKETS, LANES), 1)
            out = jnp.zeros((NUM_BUCKETS, LANES), F32)
            for h in range(SWA_HEADS):
                db = dbias_ref[h]
                for b in range(NUM_BUCKETS):
                    val = jnp.sum(jnp.where(bk == b, db, 0.0))
                    out = jnp.where((rowi == b) & (lanei == h), val, out)
            grb_ref[...] = out

    cq, ck, cv = COL_SQ // SWA_W, COL_SK // LANES, COL_SV // LANES
    cur = lambda n: jnp.minimum(n, nb - 1)
    prev = lambda n: jnp.maximum(jnp.minimum(n, nb - 1) - 1, 0)
    kout = lambda n: jnp.maximum(n - 1, 0)
    return pl.pallas_call(
        kern, name="swa_bwd",
        grid=(nb + 1,),
        in_specs=[pl.BlockSpec((BLOCK, SWA_W), lambda n: (cur(n), cq)),
                  pl.BlockSpec((BLOCK, LANES), lambda n: (prev(n), ck)),
                  pl.BlockSpec((BLOCK, LANES), lambda n: (cur(n), ck)),
                  pl.BlockSpec((BLOCK, LANES), lambda n: (prev(n), cv)),
                  pl.BlockSpec((BLOCK, LANES), lambda n: (cur(n), cv)),
                  pl.BlockSpec((BLOCK, SWA_W), lambda n: (cur(n), 1)),
                  pl.BlockSpec((BLOCK, SWA_W), lambda n: (cur(n), 1)),
                  pl.BlockSpec((BLOCK, LANES), lambda n: (cur(n), 0)),
                  pl.BlockSpec((SWA_HEADS, BLOCK, 2 * BLOCK), lambda n: (0, 0, 0)),
                  pl.BlockSpec(memory_space=pltpu.SMEM),
                  pl.BlockSpec((BLOCK, 2 * BLOCK), lambda n: (0, 0))],
        out_specs=[pl.BlockSpec((BLOCK, SWA_W), lambda n: (cur(n), 0)),
                   pl.BlockSpec((BLOCK, LANES), lambda n: (kout(n), 0)),
                   pl.BlockSpec((BLOCK, LANES), lambda n: (kout(n), 0)),
                   pl.BlockSpec((NUM_BUCKETS, LANES), lambda n: (0, 0)),
                   pl.BlockSpec((8, LANES), lambda n: (0, 0))],
        out_shape=[jax.ShapeDtypeStruct((s, SWA_W), BF16),
                   jax.ShapeDtypeStruct((s, LANES), BF16),
                   jax.ShapeDtypeStruct((s, LANES), BF16),
                   jax.ShapeDtypeStruct((NUM_BUCKETS, LANES), F32),
                   jax.ShapeDtypeStruct((8, LANES), F32)],
        scratch_shapes=[pltpu.VMEM((SWA_HEADS, BLOCK, 2 * BLOCK), F32),
                        pltpu.VMEM((BLOCK, LANES), F32),
                        pltpu.VMEM((BLOCK, LANES), F32),
                        pltpu.VMEM((8, LANES), F32)],
        compiler_params=_cparams(("arbitrary",)),
    )(qkv, qkv, qkv, qkv, qkv, do_bf, delta_b, lse, bias, sink, bucket)


def _post(x, target, o_fox, o_swa, z, w_o, ln_g, ln_b):
    s = x.shape[0]
    tm = min(256, s)
    nt = s // tm
    seg = 256

    def kern(x_ref, t_ref, of_ref, os_ref, z_ref, w_ref, g_ref, b_ref,
             loss_ref, dh_ref, dy_ref, mix_ref, do_ref, dz_ref, dl_ref, gg_ref, gb_ref, lacc_ref):
        step = pl.program_id(0)

        @pl.when(step == 0)
        def _():
            lacc_ref[...] = jnp.zeros_like(lacc_ref)
            gg_ref[...] = jnp.zeros_like(gg_ref)
            gb_ref[...] = jnp.zeros_like(gb_ref)

        o = jnp.concatenate([of_ref[...], os_ref[...]], axis=1)
        zz = z_ref[...]
        sig = 1.0 / (1.0 + jnp.exp(-zz))
        silu = zz * sig
        mixed = (o * silu).astype(BF16)
        mix_ref[...] = mixed
        w = w_ref[...]
        h = ALPHA * x_ref[...] + _dot(mixed, w)
        mu = jnp.mean(h, axis=1, keepdims=True)
        hc = h - mu
        var = jnp.mean(hc * hc, axis=1, keepdims=True)
        rstd = lax.rsqrt(var + LN_EPS)
        xhat = hc * rstd
        g = g_ref[...]
        err = xhat * g + b_ref[...] - t_ref[...]
        lacc_ref[...] += jnp.broadcast_to(jnp.sum(err * err, axis=0, keepdims=True), lacc_ref.shape)
        dout = err * (1.0 / D_MODEL)
        gg_ref[...] += jnp.broadcast_to(jnp.sum(dout * xhat, axis=0, keepdims=True), gg_ref.shape)
        gb_ref[...] += jnp.broadcast_to(jnp.sum(dout, axis=0, keepdims=True), gb_ref.shape)
        dxh = dout * g
        m1 = jnp.mean(dxh, axis=1, keepdims=True)
        m2 = jnp.mean(dxh * xhat, axis=1, keepdims=True)
        dh = rstd * (dxh - m1 - xhat * m2)
        dh_ref[...] = dh
        dy = dh.astype(BF16)
        dy_ref[...] = dy
        dmix = _dot_nt(dy, w)
        do = dmix * silu
        do_ref[...] = do.astype(BF16)
        dz_ref[...] = (dmix * o * (sig * (1.0 + zz * (1.0 - sig)))).astype(BF16)
        r = lax.broadcasted_iota(jnp.int32, (seg, seg), 0) // HEAD_DIM
        c = lax.broadcasted_iota(jnp.int32, (seg, seg), 1) // HEAD_DIM
        bd = jnp.where(r == c, 1.0, 0.0).astype(BF16)
        prod = do * o
        parts = [_exact_dot(bd, prod[:, j * seg:(j + 1) * seg], False) for j in range(D_MODEL // seg)]
        dl_ref[...] = jnp.concatenate(parts, axis=1)

        @pl.when(step == nt - 1)
        def _():
            tot = jnp.sum(lacc_ref[0:1, :]) * (0.5 / D_MODEL)
            loss_ref[...] = jnp.broadcast_to(tot, loss_ref.shape)

    row = lambda i: (i, 0)
    fixed = lambda i: (0, 0)
    wide = pl.BlockSpec((tm, D_MODEL), row)
    half = pl.BlockSpec((tm, FOX_W), row)
    return pl.pallas_call(
        kern, name="post",
        grid=(nt,),
        in_specs=[wide, wide, half, half, wide,
                  pl.BlockSpec((D_MODEL, D_MODEL), fixed),
                  pl.BlockSpec((1, D_MODEL), fixed),
                  pl.BlockSpec((1, D_MODEL), fixed)],
        out_specs=[pl.BlockSpec((8, LANES), fixed), wide, wide, wide, wide, wide, wide,
                   pl.BlockSpec((8, D_MODEL), fixed), pl.BlockSpec((8, D_MODEL), fixed)],
        out_shape=[jax.ShapeDtypeStruct((8, LANES), F32),
                   jax.ShapeDtypeStruct((s, D_MODEL), F32),
                   jax.ShapeDtypeStruct((s, D_MODEL), BF16),
                   jax.ShapeDtypeStruct((s, D_MODEL), BF16),
                   jax.ShapeDtypeStruct((s, D_MODEL), BF16),
                   jax.ShapeDtypeStruct((s, D_MODEL), BF16),
                   jax.ShapeDtypeStruct((s, D_MODEL), F32),
                   jax.ShapeDtypeStruct((8, D_MODEL), F32),
                   jax.ShapeDtypeStruct((8, D_MODEL), F32)],
        scratch_shapes=[pltpu.VMEM((8, D_MODEL), F32)],
        compiler_params=_cparams(("arbitrary",)),
    )(x, target, o_fox, o_swa, z, w_o, ln_g, ln_b)


def _adamw_math(w, g, m, v):
    m = ADAM_B1 * m + (1.0 - ADAM_B1) * g
    v = ADAM_B2 * v + (1.0 - ADAM_B2) * (g * g)
    m_hat = m / (1.0 - ADAM_B1 ** ADAM_STEP)
    v_hat = v / (1.0 - ADAM_B2 ** ADAM_STEP)
    delta = -ADAM_LR * (m_hat / (jnp.sqrt(v_hat) + ADAM_EPS) + ADAM_WD * w)
    return delta, m, v


def _adamw(w, g, m, v, *, name):
    r, c = w.shape
    tr = min(256, r)

    def kern(w_ref, g_ref, m_ref, v_ref, d_ref, mo_ref, vo_ref):
        d, mn, vn = _adamw_math(w_ref[...], g_ref[...], m_ref[...], v_ref[...])
        d_ref[...] = d
        mo_ref[...] = mn
        vo_ref[...] = vn

    blk = pl.BlockSpec((tr, c), lambda i: (i, 0))
    sds = jax.ShapeDtypeStruct((r, c), F32)
    return pl.pallas_call(
        kern, name=name,
        grid=(r // tr,),
        in_specs=[blk, blk, blk, blk],
        out_specs=[blk, blk, blk],
        out_shape=[sds, sds, sds],
        compiler_params=_cparams(("parallel",)),
    )(w, g, m, v)


def _position():
    x, y, c = lax.axis_index("x"), lax.axis_index("y"), lax.axis_index("c")
    chips = [(1 - x, y), (x, 1 - y), (1 - x, 1 - y)]
    return x, y, c, chips


def _chip_index(cx, cy):
    return 2 * cx + cy


def _gather_weights(w_in_bf, w_o_bf):
    shards = (w_in_bf, w_o_bf)
    n_arr = len(shards)

    def kern(*refs):
        ins, outs = refs[:n_arr], refs[n_arr:2 * n_arr]
        send_sems, recv_sems, local_sems = refs[2 * n_arr:]
        x, y, c, chips = _position()
        me = _chip_index(x, y)
        sibling = (x, y, 1 - c)

        local = [pltpu.make_async_copy(ins[a], outs[a].at[me], local_sems.at[a]) for a in range(n_arr)]
        for cp in local:
            cp.start()

        def half(ref, a):
            rows = shards[a].shape[0] // 2
            return ref.at[pl.ds(c * rows, rows), :]

        def copy(a, k, src, slot, to):
            return pltpu.make_async_remote_copy(
                src_ref=src, dst_ref=half(outs[a].at[slot], a),
                send_sem=send_sems.at[a * 6 + k], recv_sem=recv_sems.at[a * 6 + k],
                device_id=to, device_id_type=MESH)

        first = [copy(a, j, half(ins[a], a), me, (*chip, c)) for a in range(n_arr) for j, chip in enumerate(chips)]
        for cp in first:
            cp.start()
        passed = []
        for a in range(n_arr):
            for j, chip in enumerate(chips):
                slot = _chip_index(*chip)
                copy(a, j, half(ins[a], a), slot, (*chip, c)).wait_recv()
                fwd = copy(a, 3 + j, half(outs[a].at[slot], a), slot, sibling)
                fwd.start()
                passed.append(fwd)
        for a in range(n_arr):
            for j, chip in enumerate(chips):
                slot = _chip_index(*chip)
                rows = shards[a].shape[0] // 2
                dst = outs[a].at[slot].at[pl.ds((1 - c) * rows, rows), :]
                pltpu.make_async_remote_copy(
                    src_ref=dst, dst_ref=dst, send_sem=send_sems.at[a * 6 + 3 + j],
                    recv_sem=recv_sems.at[a * 6 + 3 + j], device_id=sibling, device_id_type=MESH).wait_recv()
        for cp in first + passed:
            cp.wait_send()
        for cp in local:
            cp.wait()

    hbm = pl.BlockSpec(memory_space=pl.ANY)
    return pl.pallas_call(
        kern, name="gather_weights",
        in_specs=[hbm] * n_arr,
        out_specs=[hbm] * n_arr,
        out_shape=[jax.ShapeDtypeStruct((N_CHIPS,) + w.shape, w.dtype) for w in shards],
        scratch_shapes=[pltpu.SemaphoreType.DMA((6 * n_arr,)),
                        pltpu.SemaphoreType.DMA((6 * n_arr,)),
                        pltpu.SemaphoreType.DMA((n_arr,))],
    )(*shards)


def _swap_halves(grads):
    n_arr = len(grads)

    def kern(*refs):
        ins = refs[:n_arr]
        owns = refs[n_arr:2 * n_arr]
        gots = refs[2 * n_arr:3 * n_arr]
        send_sems, recv_sems, local_sems = refs[3 * n_arr:]
        x, y, c, _ = _position()
        sibling = (x, y, 1 - c)
        local, remote = [], []
        for a in range(n_arr):
            rows = grads[a].shape[1] // 2
            local.append(pltpu.make_async_copy(ins[a].at[:, pl.ds(c * rows, rows), :], owns[a], local_sems.at[a]))
            remote.append(pltpu.make_async_remote_copy(
                src_ref=ins[a].at[:, pl.ds((1 - c) * rows, rows), :], dst_ref=gots[a],
                send_sem=send_sems.at[a], recv_sem=recv_sems.at[a], device_id=sibling, device_id_type=MESH))
        for cp in local + remote:
            cp.start()
        for cp in remote:
            cp.wait()
        for cp in local:
            cp.wait()

    hbm = pl.BlockSpec(memory_space=pl.ANY)
    half = [jax.ShapeDtypeStruct((N_CHIPS, g.shape[1] // 2, g.shape[2]), F32) for g in grads]
    outs = pl.pallas_call(
        kern, name="swap_halves",
        in_specs=[hbm] * n_arr,
        out_specs=[hbm] * (2 * n_arr),
        out_shape=half + half,
        scratch_shapes=[pltpu.SemaphoreType.DMA((n_arr,)),
                        pltpu.SemaphoreType.DMA((n_arr,)),
                        pltpu.SemaphoreType.DMA((n_arr,))],
    )(*grads)
    return outs[:n_arr], outs[n_arr:]


def _scatter_to_owners(parts):
    n_arr = len(parts)

    def kern(*refs):
        ins = refs[:n_arr]
        outs = refs[n_arr:2 * n_arr]
        send_sems, recv_sems, local_sems = refs[2 * n_arr:]
        x, y, c, chips = _position()
        me = _chip_index(x, y)
        local = [pltpu.make_async_copy(ins[a].at[me], outs[a].at[me], local_sems.at[a]) for a in range(n_arr)]
        for cp in local:
            cp.start()
        sends = []
        for a in range(n_arr):
            for j, chip in enumerate(chips):
                sends.append(pltpu.make_async_remote_copy(
                    src_ref=ins[a].at[_chip_index(*chip)], dst_ref=outs[a].at[me],
                    send_sem=send_sems.at[a * 3 + j], recv_sem=recv_sems.at[a * 3 + j],
                    device_id=(*chip, c), device_id_type=MESH))
        for cp in sends:
            cp.start()
        for a in range(n_arr):
            for j, chip in enumerate(chips):
                slot = outs[a].at[_chip_index(*chip)]
                pltpu.make_async_remote_copy(
                    src_ref=slot, dst_ref=slot, send_sem=send_sems.at[a * 3 + j],
                    recv_sem=recv_sems.at[a * 3 + j], device_id=(*chip, c), device_id_type=MESH).wait_recv()
        for cp in sends:
            cp.wait_send()
        for cp in local:
            cp.wait()

    hbm = pl.BlockSpec(memory_space=pl.ANY)
    return pl.pallas_call(
        kern, name="scatter_to_owners",
        in_specs=[hbm] * n_arr,
        out_specs=[hbm] * n_arr,
        out_shape=[jax.ShapeDtypeStruct(p.shape, F32) for p in parts],
        scratch_shapes=[pltpu.SemaphoreType.DMA((3 * n_arr,)),
                        pltpu.SemaphoreType.DMA((3 * n_arr,)),
                        pltpu.SemaphoreType.DMA((n_arr,))],
    )(*parts)


def _join_halves(halves):
    n_arr = len(halves)

    def kern(*refs):
        ins = refs[:n_arr]
        outs = refs[n_arr:2 * n_arr]
        send_sems, recv_sems, local_sems = refs[2 * n_arr:]
        x, y, c, _ = _position()
        sibling = (x, y, 1 - c)
        local, remote = [], []
        for a in range(n_arr):
            rows = halves[a].shape[0]
            mine = outs[a].at[pl.ds(c * rows, rows), :]
            local.append(pltpu.make_async_copy(ins[a], mine, local_sems.at[a]))
            remote.append(pltpu.make_async_remote_copy(
                src_ref=ins[a], dst_ref=mine, send_sem=send_sems.at[a], recv_sem=recv_sems.at[a],
                device_id=sibling, device_id_type=MESH))
        for cp in local + remote:
            cp.start()
        for a in range(n_arr):
            rows = halves[a].shape[0]
            theirs = outs[a].at[pl.ds((1 - c) * rows, rows), :]
            pltpu.make_async_remote_copy(
                src_ref=theirs, dst_ref=theirs, send_sem=send_sems.at[a], recv_sem=recv_sems.at[a],
                device_id=sibling, device_id_type=MESH).wait_recv()
        for cp in remote:
            cp.wait_send()
        for cp in local:
            cp.wait()

    hbm = pl.BlockSpec(memory_space=pl.ANY)
    return pl.pallas_call(
        kern, name="join_halves",
        in_specs=[hbm] * n_arr,
        out_specs=[hbm] * n_arr,
        out_shape=[jax.ShapeDtypeStruct((2 * h.shape[0], h.shape[1]), F32) for h in halves],
        scratch_shapes=[pltpu.SemaphoreType.DMA((n_arr,)),
                        pltpu.SemaphoreType.DMA((n_arr,)),
                        pltpu.SemaphoreType.DMA((n_arr,))],
    )(*halves)


def _add2(a, b, *, name):
    n, r, c = a.shape
    tr = min(256, r)

    def kern(a_ref, b_ref, o_ref):
        o_ref[...] = a_ref[...] + b_ref[...]

    blk = pl.BlockSpec((1, tr, c), lambda j, i: (j, i, 0))
    return pl.pallas_call(
        kern, name=name,
        grid=(n, r // tr),
        in_specs=[blk, blk],
        out_specs=blk,
        out_shape=jax.ShapeDtypeStruct(a.shape, F32),
        compiler_params=_cparams(("parallel", "parallel")),
    )(a, b)


def _sum4(a, *, name):
    n, r, c = a.shape
    tr = min(256, r)

    def kern(a_ref, o_ref):
        o_ref[...] = ((a_ref[0] + a_ref[1]) + a_ref[2]) + a_ref[3]

    return pl.pallas_call(
        kern, name=name,
        grid=(r // tr,),
        in_specs=[pl.BlockSpec((n, tr, c), lambda i: (0, i, 0))],
        out_specs=pl.BlockSpec((tr, c), lambda i: (i, 0)),
        out_shape=jax.ShapeDtypeStruct((r, c), F32),
        compiler_params=_cparams(("parallel",)),
    )(a)


def _small_allreduce_adamw(g, w, m, v):
    def kern(g_ref, w_ref, m_ref, v_ref, gs_ref, d_ref, mo_ref, vo_ref, buf_ref, send_sems, recv_sems):
        x, y, c, _ = _position()
        me = 4 * x + 2 * y + c
        buf_ref[me] = g_ref[...]
        peers = [(x, y, 1 - c)] + [(px, py, pc) for px, py in _position()[3] for pc in (c, 1 - c)]
        sends = []
        for k, peer in enumerate(peers):
            sends.append(pltpu.make_async_remote_copy(
                src_ref=g_ref, dst_ref=buf_ref.at[me], send_sem=send_sems.at[k], recv_sem=recv_sems.at[k],
                device_id=peer, device_id_type=MESH))
        for cp in sends:
            cp.start()
        for k, (px, py, pc) in enumerate(peers):
            slot = buf_ref.at[4 * px + 2 * py + pc]
            pltpu.make_async_remote_copy(
                src_ref=slot, dst_ref=slot, send_sem=send_sems.at[k], recv_sem=recv_sems.at[k],
                device_id=(px, py, pc), device_id_type=MESH).wait_recv()
        for cp in sends:
            cp.wait_send()
        tot = buf_ref[0]
        for d in range(1, N_DEV):
            tot = tot + buf_ref[d]
        gs_ref[...] = tot
        delta, mn, vn = _adamw_math(w_ref[...], tot, m_ref[...], v_ref[...])
        d_ref[...] = delta
        mo_ref[...] = mn
        vo_ref[...] = vn

    vm = pl.BlockSpec(memory_space=pltpu.VMEM)
    sds = jax.ShapeDtypeStruct((SMALL_ROWS, LANES), F32)
    return pl.pallas_call(
        kern, name="small_allreduce_adamw",
        in_specs=[vm] * 4,
        out_specs=[vm] * 4,
        out_shape=[sds] * 4,
        scratch_shapes=[pltpu.VMEM((N_DEV, SMALL_ROWS, LANES), F32),
                        pltpu.SemaphoreType.DMA((N_DEV - 1,)),
                        pltpu.SemaphoreType.DMA((N_DEV - 1,))],
    )(g, w, m, v)


def _to_padded_cols(w):
    pad = jnp.zeros((w.shape[0], N_C - FOX_HEADS), w.dtype)
    return jnp.concatenate([w[:, 0:1536], w[:, 2056:2824], w[:, 1536:1544], pad,
                            w[:, 1544:2056], w[:, 2824:3336]], axis=1)


def _from_padded_cols(g):
    return jnp.concatenate([g[:, 0:1536], g[:, OFF_C:OFF_C + FOX_HEADS], g[:, OFF_B:OFF_B + FOX_W],
                            g[:, 1536:N_A], g[:, OFF_B + FOX_W:N_PAD]], axis=1)


def _pack_small(b_f, rel_bias, sink, ln_g, ln_b):
    row = lambda v: jnp.pad(v.reshape(1, -1), ((0, 0), (0, LANES - v.size)))
    return jnp.concatenate([ln_g.reshape(8, LANES), ln_b.reshape(8, LANES), rel_bias.reshape(2, LANES),
                            row(b_f), row(sink), jnp.zeros((4, LANES), F32)], axis=0)


def _unpack_small(p):
    ln_g = p[0:8].reshape(1, D_MODEL)
    ln_b = p[8:16].reshape(1, D_MODEL)
    rel_bias = p[16:18].reshape(NUM_BUCKETS, SWA_HEADS)
    b_f = p[18:19, :FOX_HEADS]
    sink = p[19:20, :SWA_HEADS]
    return b_f, rel_bias, sink, ln_g, ln_b


def _heads_to_rows(a_b):
    return a_b[:, ::HEAD_DIM].T.reshape(FOX_HEADS, 1, a_b.shape[0])


def kernel(x, w_in, b_f, rel_bias, sink, w_o, ln_g, ln_b, loss_target, m_w_in, m_b_f, m_rel_bias, m_sink, m_w_o, m_ln_g, m_ln_b, v_w_in, v_b_f, v_rel_bias, v_sink, v_w_o, v_ln_g, v_ln_b):
    x2 = x[0]
    tgt = loss_target[0]
    s = x2.shape[0]
    w_in2, w_o2 = w_in[0], w_o[0]

    shard_cols = D_IN // N_CHIPS
    col_pad = ((0, 0), (0, SHARD_PAD - shard_cols))
    w_in_all, w_o_all = _gather_weights(jnp.pad(w_in2.astype(BF16), col_pad), w_o2.astype(BF16))
    w_full = jnp.concatenate([w_in_all[j, :, :shard_cols] for j in range(N_CHIPS)], axis=1)
    w_pad = _to_padded_cols(w_full)
    w_o_full = w_o_all.reshape(D_MODEL, D_MODEL)

    x_bf = x2.astype(BF16)
    qkv = _matmul_nn(x_bf, w_pad, n_off=0, n_out=N_A, tm=512, tn=768, out_dtype=BF16, name="proj_qkv")
    z = _matmul_nn(x_bf, w_pad, n_off=OFF_B, n_out=N_B, tm=512, tn=512, out_dtype=F32, name="proj_gate")
    ffp = _matmul_nn(x_bf, w_pad, n_off=OFF_C, n_out=N_C, tm=512, tn=N_C, out_dtype=F32, name="proj_forget")
    bfp = jnp.pad(b_f, ((0, 0), (0, LANES - FOX_HEADS)))
    cum = _cum_fwd(ffp, bfp)
    cum_t3 = cum[:, :FOX_HEADS].T.reshape(FOX_HEADS, 1, s)
    vt = qkv[:, COL_FV:COL_FV + FOX_W].T
    o_fox, lse_t3 = _fox_fwd(qkv, vt, cum_t3, cum)
    bucket = jnp.asarray(_bucket_table())
    bias = _swa_bias(rel_bias, bucket)
    o_swa, lse_swa = _swa_fwd(qkv, bias, sink)

    loss8, dh, dy, mixed, do_bf, dz, delta_b, gg8, gb8 = _post(
        x2, tgt, o_fox, o_swa, z, w_o_full, ln_g, ln_b)
    loss = lax.psum(loss8[0, 0], ("x", "y", "c"))
    grad_w_o_full = _matmul_acc(mixed.T, dy, tm=1024, tn=512, tk=1024, name="grad_w_o")

    delta_t3 = _heads_to_rows(delta_b[:, :FOX_W])
    dqt_fox, dk_fox, dv_fox, dcum_k, dcum_q = _fox_bwd(qkv, do_bf, cum_t3, cum, lse_t3, delta_t3)
    dcum_q = jnp.pad(dcum_q.reshape(FOX_HEADS, s).T, ((0, 0), (0, LANES - FOX_HEADS)))
    dff, gbf8 = _cum_bwd(dcum_k, dcum_q, ffp, bfp)
    dq_swa, dk_swa, dv_swa, grb, gsk8 = _swa_bwd(qkv, do_bf, delta_b, lse_swa, bias, sink, bucket)

    dproj = jnp.concatenate([dqt_fox.T.astype(BF16), dk_fox, dv_fox, dq_swa, dk_swa, dv_swa, dff, dz], axis=1)
    grad_x = _grad_x_matmul(dproj, w_pad, dh, tm=512, tn=512, name="grad_x")
    grad_w_pad = _matmul_acc(x_bf.T, dproj, tm=1024, tn=512, tk=1024, name="grad_w_in")
    grad_w_in_full = _from_padded_cols(grad_w_pad)

    g_in4 = jnp.stack([jnp.pad(grad_w_in_full[:, j * shard_cols:(j + 1) * shard_cols], col_pad)
                       for j in range(N_CHIPS)])
    g_o4 = grad_w_o_full.reshape(N_CHIPS, D_MODEL // N_CHIPS, D_MODEL)
    owns, gots = _swap_halves([g_in4, g_o4])
    parts = [_add2(owns[0], gots[0], name="pair_sum_w_in"), _add2(owns[1], gots[1], name="pair_sum_w_o")]
    slabs = _scatter_to_owners(parts)
    halves = [_sum4(slabs[0], name="chip_sum_w_in"), _sum4(slabs[1], name="chip_sum_w_o")]
    g_w_in, g_w_o = _join_halves(halves)
    g_w_in = g_w_in[:, :shard_cols]

    d_w_in, nm_w_in, nv_w_in = _adamw(w_in2, g_w_in, m_w_in[0], v_w_in[0], name="adamw_w_in")
    d_w_o, nm_w_o, nv_w_o = _adamw(w_o2, g_w_o, m_w_o[0], v_w_o[0], name="adamw_w_o")

    g_small = _pack_small(gbf8[0:1, :FOX_HEADS], grb[:, :SWA_HEADS], gsk8[0:1, :SWA_HEADS], gg8[0:1], gb8[0:1])
    w_small = _pack_small(b_f, rel_bias, sink, ln_g, ln_b)
    m_small = _pack_small(m_b_f, m_rel_bias, m_sink, m_ln_g, m_ln_b)
    v_small = _pack_small(v_b_f, v_rel_bias, v_sink, v_ln_g, v_ln_b)
    gs, ds, ms, vs = _small_allreduce_adamw(g_small, w_small, m_small, v_small)
    g_bf, g_rb, g_sk, g_lg, g_lb = _unpack_small(gs)
    d_bf, d_rb, d_sk, d_lg, d_lb = _unpack_small(ds)
    m_bf, m_rb, m_sk, m_lg, m_lb = _unpack_small(ms)
    v_bf, v_rb, v_sk, v_lg, v_lb = _unpack_small(vs)

    e = lambda a: a[None]
    return (loss, e(grad_x),
            e(g_w_in), g_bf, g_rb, g_sk, e(g_w_o), g_lg, g_lb,
            e(d_w_in), d_bf, d_rb, d_sk, e(d_w_o), d_lg, d_lb,
            e(nm_w_in), m_bf, m_rb, m_sk, e(nm_w_o), m_lg, m_lb,
            e(nv_w_in), v_bf, v_rb, v_sk, e(nv_w_o), v_lg, v_lb)
```

```python
import functools
import math

import numpy as np
import jax
import jax.numpy as jnp
from jax import lax
from jax.experimental import pallas as pl
from jax.experimental.pallas import tpu as pltpu

F32 = jnp.float32
BF16 = jnp.bfloat16

D_MODEL = 1024
HEAD_DIM = 64
FOX_HEADS = 8
SWA_HEADS = 8
SWA_KV_HEADS = 2
SWA_GROUP = 4
FOX_W = 512
SWA_W = 512
SWA_KV_W = 128
BLOCK = 128
NUM_BUCKETS = 32
MAX_DISTANCE = 128
LN_EPS = 1e-5
NEG = -1e30
ALPHA = 2.0 ** 0.25
QK_SCALE = 0.125

ADAM_LR = 0.001
ADAM_B1 = 0.9
ADAM_B2 = 0.999
ADAM_EPS = 1e-08
ADAM_WD = 0.01
ADAM_STEP = 10

D_IN = 3336
SHARD_PAD = 896
N_A = 2304
N_C = 256
N_B = 1024
OFF_C = N_A
OFF_B = N_A + N_C
N_PAD = N_A + N_C + N_B
COL_FK, COL_FV, COL_SQ, COL_SK, COL_SV = 512, 1024, 1536, 2048, 2176

LANES = 128
FOX_T = 256
VMEM_LIMIT = 56 * 1024 * 1024

MESH = pl.DeviceIdType.MESH
N_CHIPS = 4
N_DEV = 8
SMALL_ROWS = 24
COPY_PIECES = 4


def _cparams(sem=None):
    return pltpu.CompilerParams(dimension_semantics=sem, vmem_limit_bytes=VMEM_LIMIT)


def _split3(x):
    hi = x.astype(BF16)
    r = x - hi.astype(F32)
    mid = r.astype(BF16)
    lo = (r - mid.astype(F32)).astype(BF16)
    return hi, mid, lo


def _dot(a, b):
    return jnp.dot(a, b, preferred_element_type=F32)


def _dot_nt(a, b):
    return lax.dot_general(a, b, (((1,), (1,)), ((), ())), preferred_element_type=F32)


def _dot_tn(a, b):
    return lax.dot_general(a, b, (((0,), (0,)), ((), ())), preferred_element_type=F32)


def _matmul_nn(a, b, *, n_off, n_out, tm, tn, out_dtype, name):
    m, k = a.shape
    joff = n_off // tn

    def kern(a_ref, b_ref, o_ref):
        o_ref[...] = _dot(a_ref[...], b_ref[...]).astype(o_ref.dtype)

    return pl.pallas_call(
        kern, name=name,
        grid=(n_out // tn, m // tm),
        in_specs=[pl.BlockSpec((tm, k), lambda j, i: (i, 0)),
                  pl.BlockSpec((k, tn), lambda j, i: (0, j + joff))],
        out_specs=pl.BlockSpec((tm, tn), lambda j, i: (i, j)),
        out_shape=jax.ShapeDtypeStruct((m, n_out), out_dtype),
        compiler_params=_cparams(("parallel", "parallel")),
    )(a, b)


def _grad_x_matmul(dproj, w_pad, dh, *, tm, tn, name):
    m, k = dproj.shape
    n = w_pad.shape[0]

    def kern(a_ref, b_ref, dh_ref, o_ref):
        o_ref[...] = ALPHA * dh_ref[...] + _dot_nt(a_ref[...], b_ref[...])

    return pl.pallas_call(
        kern, name=name,
        grid=(n // tn, m // tm),
        in_specs=[pl.BlockSpec((tm, k), lambda j, i: (i, 0)),
                  pl.BlockSpec((tn, k), lambda j, i: (j, 0)),
                  pl.BlockSpec((tm, tn), lambda j, i: (i, j))],
        out_specs=pl.BlockSpec((tm, tn), lambda j, i: (i, j)),
        out_shape=jax.ShapeDtypeStruct((m, n), F32),
        compiler_params=_cparams(("parallel", "parallel")),
    )(dproj, w_pad, dh)


def _matmul_acc(at, b, *, tm, tn, tk, name):
    m, s = at.shape
    n = b.shape[1]

    def kern(a_ref, b_ref, o_ref):
        @pl.when(pl.program_id(2) == 0)
        def _():
            o_ref[...] = jnp.zeros_like(o_ref)
        o_ref[...] += _dot(a_ref[...], b_ref[...])

    return pl.pallas_call(
        kern, name=name,
        grid=(m // tm, n // tn, s // tk),
        in_specs=[pl.BlockSpec((tm, tk), lambda i, j, k: (i, k)),
                  pl.BlockSpec((tk, tn), lambda i, j, k: (k, j))],
        out_specs=pl.BlockSpec((tm, tn), lambda i, j, k: (i, j)),
        out_shape=jax.ShapeDtypeStruct((m, n), F32),
        compiler_params=_cparams(("parallel", "parallel", "arbitrary")),
    )(at, b)


def _tri(n, lower):
    r = lax.broadcasted_iota(jnp.int32, (n, n), 0)
    c = lax.broadcasted_iota(jnp.int32, (n, n), 1)
    keep = (c <= r) if lower else (c >= r)
    return jnp.where(keep, 1.0, 0.0).astype(BF16)


def _exact_dot(mat_bf16, x_f32, left):
    out = None
    for piece in _split3(x_f32):
        t = _dot(mat_bf16, piece) if left else _dot(piece, mat_bf16)
        out = t if out is None else out + t
    return out


def _log_sigmoid(z):
    return jnp.minimum(z, 0.0) - jnp.log(1.0 + jnp.exp(-jnp.abs(z)))


def _cum_fwd(ffp, bfp):
    s = ffp.shape[0]
    t = min(256, s)

    def kern(ff_ref, b_ref, cum_ref, carry_ref):
        @pl.when(pl.program_id(0) == 0)
        def _():
            carry_ref[...] = jnp.zeros_like(carry_ref)
        lane = lax.broadcasted_iota(jnp.int32, (1, LANES), 1)
        lf = _log_sigmoid(ff_ref[...] + b_ref[...])
        lf = jnp.where(lane < FOX_HEADS, lf, 0.0)
        cum = _exact_dot(_tri(t, True), lf, True) + carry_ref[0:1, :]
        cum_ref[...] = cum
        carry_ref[...] = jnp.broadcast_to(cum[t - 1:t, :], carry_ref.shape)

    return pl.pallas_call(
        kern, name="cum_fwd",
        grid=(s // t,),
        in_specs=[pl.BlockSpec((t, LANES), lambda i: (i, 0)),
                  pl.BlockSpec((1, LANES), lambda i: (0, 0))],
        out_specs=pl.BlockSpec((t, LANES), lambda i: (i, 0)),
        out_shape=jax.ShapeDtypeStruct((s, LANES), F32),
        scratch_shapes=[pltpu.VMEM((8, LANES), F32)],
        compiler_params=_cparams(("arbitrary",)),
    )(ffp, bfp)


def _cum_bwd(dcum_k, dcum_q, ffp, bfp):
    s = dcum_k.shape[0]
    t = min(256, s)
    nb = s // t

    def kern(dck_ref, dcq_ref, ff_ref, b_ref, dff_ref, gb_ref, carry_ref):
        @pl.when(pl.program_id(0) == 0)
        def _():
            carry_ref[...] = jnp.zeros_like(carry_ref)
            gb_ref[...] = jnp.zeros_like(gb_ref)
        lane = lax.broadcasted_iota(jnp.int32, (1, LANES), 1)
        dlf = _exact_dot(_tri(t, False), dck_ref[...] + dcq_ref[...], True) + carry_ref[0:1, :]
        carry_ref[...] = jnp.broadcast_to(dlf[0:1, :], carry_ref.shape)
        z = ff_ref[...] + b_ref[...]
        dff = jnp.where(lane < FOX_HEADS, dlf / (1.0 + jnp.exp(z)), 0.0)
        gb_ref[...] += jnp.broadcast_to(jnp.sum(dff, axis=0, keepdims=True), gb_ref.shape)
        dff_ref[...] = jnp.concatenate([dff, jnp.zeros_like(dff)], axis=1).astype(BF16)

    return pl.pallas_call(
        kern, name="cum_bwd",
        grid=(nb,),
        in_specs=[pl.BlockSpec((t, LANES), lambda i: (nb - 1 - i, 0)),
                  pl.BlockSpec((t, LANES), lambda i: (nb - 1 - i, 0)),
                  pl.BlockSpec((t, LANES), lambda i: (nb - 1 - i, 0)),
                  pl.BlockSpec((1, LANES), lambda i: (0, 0))],
        out_specs=[pl.BlockSpec((t, N_C), lambda i: (nb - 1 - i, 0)),
                   pl.BlockSpec((8, LANES), lambda i: (0, 0))],
        out_shape=[jax.ShapeDtypeStruct((s, N_C), BF16),
                   jax.ShapeDtypeStruct((8, LANES), F32)],
        scratch_shapes=[pltpu.VMEM((8, LANES), F32)],
        compiler_params=_cparams(("arbitrary",)),
    )(dcum_k, dcum_q, ffp, bfp)


def _resident(shape, index_map):
    return pl.BlockSpec(shape, index_map, pipeline_mode=pl.Buffered(1))


def _fox_fwd(qkv, vt, cum_t3, cum):
    s = qkv.shape[0]
    t = min(FOX_T, s)
    nq = s // t
    nh = FOX_HEADS

    def kern(q_ref, k_ref, vt_ref, ct_ref, c_ref, o_ref, lse_ref, m_ref, l_ref, acc_ref):
        i = pl.program_id(0)
        lane = lax.broadcasted_iota(jnp.int32, (1, LANES), 1)
        krow = lax.broadcasted_iota(jnp.int32, (t, t), 0)
        qcol = lax.broadcasted_iota(jnp.int32, (t, t), 1)
        causal = krow <= qcol
        q0 = pl.multiple_of(i * t, t)
        qts, crefs = [], []
        for h in range(nh):
            p, a = divmod(h, 2)
            q2 = q_ref[:, p * LANES:(p + 1) * LANES] * jnp.asarray(QK_SCALE, BF16)
            sel = (lane < HEAD_DIM) if a == 0 else (lane >= HEAD_DIM)
            qts.append(jnp.where(sel, q2, jnp.zeros_like(q2)).astype(F32).T.astype(BF16))
            crefs.append(ct_ref[h, :, pl.ds(q0, LANES)][:, 0:1])
        m_ref[...] = jnp.full(m_ref.shape, NEG, F32)
        l_ref[...] = jnp.zeros_like(l_ref)
        acc_ref[...] = jnp.zeros_like(acc_ref)

        def tile(j, masked):
            k0 = pl.multiple_of(j * t, t)
            cb = c_ref[pl.ds(k0, t), :]
            sts = [_dot(k_ref[pl.ds(k0, t), (h // 2) * LANES:(h // 2 + 1) * LANES], qts[h]) for h in range(nh)]
            pts, scales = [], []
            for h in range(nh):
                u = sts[h] - (cb[:, h:h + 1] - crefs[h])
                if masked:
                    u = jnp.where(causal, u, NEG)
                m_old = m_ref[h]
                m_new = jnp.maximum(m_old, jnp.max(u, axis=0, keepdims=True))
                scale = jnp.exp(m_old - m_new)
                p = jnp.exp(u - m_new)
                l_ref[h] = scale * l_ref[h] + jnp.sum(p, axis=0, keepdims=True)
                m_ref[h] = m_new
                pts.append(p.astype(BF16))
                scales.append(scale)
            for h in range(nh):
                vth = vt_ref[h * HEAD_DIM:(h + 1) * HEAD_DIM, pl.ds(k0, t)]
                acc_ref[h] = scales[h] * acc_ref[h] + _dot(vth, pts[h])

        def body(j, c):
            tile(j, False)
            return c
        lax.fori_loop(0, i, body, 0)
        tile(i, True)

        for p in range(nh // 2):
            ot = jnp.concatenate([acc_ref[2 * p + a] * (1.0 / l_ref[2 * p + a]) for a in range(2)], axis=0)
            o_ref[:, p * LANES:(p + 1) * LANES] = ot.T
        for h in range(nh):
            lse_ref[h, :, pl.ds(q0, t)] = m_ref[h] + jnp.log(l_ref[h])

    return pl.pallas_call(
        kern, name="fox_fwd",
        grid=(nq,),
        in_specs=[pl.BlockSpec((t, FOX_W), lambda i: (i, 0)),
                  _resident((s, FOX_W), lambda i: (0, COL_FK // FOX_W)),
                  _resident((FOX_W, s), lambda i: (0, 0)),
                  _resident((nh, 1, s), lambda i: (0, 0, 0)),
                  _resident((s, LANES), lambda i: (0, 0))],
        out_specs=[pl.BlockSpec((t, FOX_W), lambda i: (i, 0)),
                   pl.BlockSpec((nh, 1, s), lambda i: (0, 0, 0))],
        out_shape=[jax.ShapeDtypeStruct((s, FOX_W), F32),
                   jax.ShapeDtypeStruct((nh, 1, s), F32)],
        scratch_shapes=[pltpu.VMEM((nh, 1, t), F32),
                        pltpu.VMEM((nh, 1, t), F32),
                        pltpu.VMEM((nh, HEAD_DIM, t), F32)],
        compiler_params=_cparams(("arbitrary",)),
    )(qkv, qkv, vt, cum_t3, cum)


def _fox_bwd(qkv, do_bf, cum_t3, cum, lse_t3, delta_t3):
    s = qkv.shape[0]
    t = min(FOX_T, s)
    nq = s // t
    nh = FOX_HEADS
    npair = nh // 2

    def kern(q_ref, do_ref, k_ref, v_ref, ct_ref, c_ref, lse_ref, dl_ref,
             dqt_ref, dk_ref, dv_ref, dc_ref, dcq_ref, accv_ref, acck_ref, accd_ref):
        kj = pl.program_id(0)
        lane = lax.broadcasted_iota(jnp.int32, (1, LANES), 1)
        sub = lax.broadcasted_iota(jnp.int32, (LANES, 1), 0)
        krow = lax.broadcasted_iota(jnp.int32, (t, t), 0)
        qcol = lax.broadcasted_iota(jnp.int32, (t, t), 1)
        causal = krow <= qcol
        sels = [lane < HEAD_DIM, lane >= HEAD_DIM]
        subsels = [sub < HEAD_DIM, sub >= HEAD_DIM]

        @pl.when(kj == 0)
        def _():
            dqt_ref[...] = jnp.zeros_like(dqt_ref)
            dcq_ref[...] = jnp.zeros_like(dcq_ref)

        accv_ref[...] = jnp.zeros_like(accv_ref)
        acck_ref[...] = jnp.zeros_like(acck_ref)
        accd_ref[...] = jnp.zeros_like(accd_ref)
        cb = c_ref[...]
        k2s, v2s, kts = [], [], []
        for p in range(npair):
            k2 = k_ref[:, p * LANES:(p + 1) * LANES]
            k2s.append(k2)
            v2s.append(v_ref[:, p * LANES:(p + 1) * LANES])
            kt = k2.astype(F32).T * QK_SCALE
            kts.append([jnp.where(subsels[a], kt, 0.0).astype(BF16) for a in range(2)])
        css = [cb[:, h:h + 1] for h in range(nh)]

        def tile(i, masked):
            q0 = pl.multiple_of(i * t, t)
            sts, dpts, qms, doms = [], [], [], []
            for h in range(nh):
                p, a = divmod(h, 2)
                qi = q_ref[pl.ds(q0, t), p * LANES:(p + 1) * LANES] * jnp.asarray(QK_SCALE, BF16)
                doi = do_ref[pl.ds(q0, t), p * LANES:(p + 1) * LANES]
                qm = jnp.where(sels[a], qi, jnp.zeros_like(qi))
                dom = jnp.where(sels[a], doi, jnp.zeros_like(doi))
                qms.append(qm)
                doms.append(dom)
                sts.append(_dot_nt(k2s[p], qm))
                dpts.append(_dot_nt(v2s[p], dom))
            pts, dsts = [], []
            for h in range(nh):
                cref = ct_ref[h, :, pl.ds(q0, LANES)][:, 0:1]
                pt = jnp.exp(sts[h] - (css[h] - cref) - lse_ref[h, :, pl.ds(q0, t)])
                if masked:
                    pt = jnp.where(causal, pt, 0.0)
                ds32 = pt * (dpts[h] - dl_ref[h, :, pl.ds(q0, t)])
                part = ds32[:, 0:LANES]
                for c in range(1, t // LANES):
                    part = part + ds32[:, c * LANES:(c + 1) * LANES]
                accd_ref[h] += part
                dcq_ref[h, :, pl.ds(q0, t)] += jnp.sum(ds32, axis=0, keepdims=True)
                pts.append(pt.astype(BF16))
                dsts.append(ds32.astype(BF16))
            for p in range(npair):
                ha, hb = 2 * p, 2 * p + 1
                accv_ref[p] += _dot(pts[ha], doms[ha]) + _dot(pts[hb], doms[hb])
                acck_ref[p] += _dot(dsts[ha], qms[ha]) + _dot(dsts[hb], qms[hb])
                dqt_ref[p * LANES:(p + 1) * LANES, pl.ds(q0, t)] += (
                    _dot(kts[p][0], dsts[ha]) + _dot(kts[p][1], dsts[hb]))

        tile(kj, True)

        def body(i, c):
            tile(i, False)
            return c
        lax.fori_loop(kj + 1, nq, body, 0)

        dc = jnp.zeros((t, LANES), F32)
        for h in range(nh):
            dc = jnp.where(lane == h, -jnp.sum(accd_ref[h], axis=1, keepdims=True), dc)
        dc_ref[...] = dc
        for p in range(npair):
            dv_ref[:, p * LANES:(p + 1) * LANES] = accv_ref[p].astype(BF16)
            dk_ref[:, p * LANES:(p + 1) * LANES] = acck_ref[p].astype(BF16)

    whole = lambda kj: (0, 0, 0)
    return pl.pallas_call(
        kern, name="fox_bwd",
        grid=(nq,),
        in_specs=[_resident((s, FOX_W), lambda kj: (0, 0)),
                  _resident((s, FOX_W), lambda kj: (0, 0)),
                  pl.BlockSpec((t, FOX_W), lambda kj: (kj, COL_FK // FOX_W)),
                  pl.BlockSpec((t, FOX_W), lambda kj: (kj, COL_FV // FOX_W)),
                  _resident((nh, 1, s), whole),
                  pl.BlockSpec((t, LANES), lambda kj: (kj, 0)),
                  _resident((nh, 1, s), whole),
                  _resident((nh, 1, s), whole)],
        out_specs=[_resident((FOX_W, s), lambda kj: (0, 0)),
                   pl.BlockSpec((t, FOX_W), lambda kj: (kj, 0)),
                   pl.BlockSpec((t, FOX_W), lambda kj: (kj, 0)),
                   pl.BlockSpec((t, LANES), lambda kj: (kj, 0)),
                   _resident((nh, 1, s), whole)],
        out_shape=[jax.ShapeDtypeStruct((FOX_W, s), F32),
                   jax.ShapeDtypeStruct((s, FOX_W), BF16),
                   jax.ShapeDtypeStruct((s, FOX_W), BF16),
                   jax.ShapeDtypeStruct((s, LANES), F32),
                   jax.ShapeDtypeStruct((nh, 1, s), F32)],
        scratch_shapes=[pltpu.VMEM((npair, t, LANES), F32),
                        pltpu.VMEM((npair, t, LANES), F32),
                        pltpu.VMEM((nh, t, LANES), F32)],
        compiler_params=_cparams(("arbitrary",)),
    )(qkv, do_bf, qkv, qkv, cum_t3, cum, lse_t3, delta_t3)


def _bucket_table():
    qi = np.arange(BLOCK)[:, None]
    kj = np.arange(2 * BLOCK)[None, :]
    rel = np.maximum(qi + BLOCK - kj, 0).astype(np.int32)
    max_exact = NUM_BUCKETS // 2
    relf = np.maximum(rel, 1).astype(np.float32)
    large = max_exact + (np.log(relf / np.float32(max_exact)) / np.float32(math.log(MAX_DISTANCE / max_exact))
                         * np.float32(NUM_BUCKETS - max_exact)).astype(np.int32)
    large = np.minimum(large, NUM_BUCKETS - 1)
    return np.where(rel < max_exact, rel, large).astype(np.int32)


def _swa_bias(rel_bias, bucket):
    def kern(rb_ref, bk_ref, o_ref):
        bk = bk_ref[...]
        for h in range(SWA_HEADS):
            acc = jnp.zeros((BLOCK, 2 * BLOCK), F32)
            for b in range(NUM_BUCKETS):
                acc = jnp.where(bk == b, rb_ref[b, h], acc)
            o_ref[h] = acc

    return pl.pallas_call(
        kern, name="swa_bias",
        in_specs=[pl.BlockSpec(memory_space=pltpu.SMEM),
                  pl.BlockSpec(memory_space=pltpu.VMEM)],
        out_specs=pl.BlockSpec(memory_space=pltpu.VMEM),
        out_shape=jax.ShapeDtypeStruct((SWA_HEADS, BLOCK, 2 * BLOCK), F32),
        compiler_params=_cparams(),
    )(rel_bias, bucket)


def _swa_mask(n):
    qi = lax.broadcasted_iota(jnp.int32, (BLOCK, 2 * BLOCK), 0)
    kj = lax.broadcasted_iota(jnp.int32, (BLOCK, 2 * BLOCK), 1)
    rel = qi + BLOCK - kj
    band = (rel >= 0) & (rel < BLOCK)
    return band & ((kj >= BLOCK) | (n > 0))


def _swa_fwd(qkv, bias, sink):
    s = qkv.shape[0]
    nb = s // BLOCK

    def kern(q_ref, kp_ref, kc_ref, vp_ref, vc_ref, bias_ref, sink_ref, o_ref, lse_ref):
        n = pl.program_id(0)
        mask = _swa_mask(n)
        lane = lax.broadcasted_iota(jnp.int32, (1, LANES), 1)
        q = q_ref[...] * jnp.asarray(QK_SCALE, BF16)
        k = jnp.concatenate([kp_ref[...], kc_ref[...]], axis=0)
        v = jnp.concatenate([vp_ref[...], vc_ref[...]], axis=0)
        kgs = [k[:, g * HEAD_DIM:(g + 1) * HEAD_DIM] for g in range(SWA_KV_HEADS)]
        vgs = [v[:, g * HEAD_DIM:(g + 1) * HEAD_DIM] for g in range(SWA_KV_HEADS)]
        raw = [_dot_nt(q[:, h * HEAD_DIM:(h + 1) * HEAD_DIM], kgs[h // SWA_GROUP]) for h in range(SWA_HEADS)]
        probs = []
        lse_all = jnp.zeros((BLOCK, LANES), F32)
        for h in range(SWA_HEADS):
            sc = jnp.where(mask, raw[h] + bias_ref[h], NEG)
            sk = sink_ref[0, h]
            m = jnp.maximum(jnp.max(sc, axis=1, keepdims=True), sk)
            p = jnp.exp(sc - m)
            l = jnp.sum(p, axis=1, keepdims=True) + jnp.exp(sk - m)
            probs.append((p * (1.0 / l)).astype(BF16))
            lse_all = jnp.where(lane == h, m + jnp.log(l), lse_all)
        outs = [_dot(probs[h], vgs[h // SWA_GROUP]) for h in range(SWA_HEADS)]
        o_ref[...] = jnp.concatenate(outs, axis=1)
        lse_ref[...] = lse_all

    cq, ck, cv = COL_SQ // SWA_W, COL_SK // LANES, COL_SV // LANES
    prev = lambda n: jnp.maximum(n - 1, 0)
    return pl.pallas_call(
        kern, name="swa_fwd",
        grid=(nb,),
        in_specs=[pl.BlockSpec((BLOCK, SWA_W), lambda n: (n, cq)),
                  pl.BlockSpec((BLOCK, LANES), lambda n: (prev(n), ck)),
                  pl.BlockSpec((BLOCK, LANES), lambda n: (n, ck)),
                  pl.BlockSpec((BLOCK, LANES), lambda n: (prev(n), cv)),
                  pl.BlockSpec((BLOCK, LANES), lambda n: (n, cv)),
                  pl.BlockSpec((SWA_HEADS, BLOCK, 2 * BLOCK), lambda n: (0, 0, 0)),
                  pl.BlockSpec(memory_space=pltpu.SMEM)],
        out_specs=[pl.BlockSpec((BLOCK, SWA_W), lambda n: (n, 0)),
                   pl.BlockSpec((BLOCK, LANES), lambda n: (n, 0))],
        out_shape=[jax.ShapeDtypeStruct((s, SWA_W), F32),
                   jax.ShapeDtypeStruct((s, LANES), F32)],
        compiler_params=_cparams(("parallel",)),
    )(qkv, qkv, qkv, qkv, qkv, bias, sink)


def _swa_bwd(qkv, do_bf, delta_b, lse, bias, sink, bucket):
    s = qkv.shape[0]
    nb = s // BLOCK

    def kern(q_ref, kp_ref, kc_ref, vp_ref, vc_ref, do_ref, dl_ref, lse_ref, bias_ref, sink_ref, bk_ref,
             dq_ref, dk_ref, dv_ref, grb_ref, gsk_ref, dbias_ref, ck_ref, cv_ref, sk_ref):
        n = pl.program_id(0)
        lane = lax.broadcasted_iota(jnp.int32, (1, LANES), 1)

        @pl.when(n == 0)
        def _():
            dbias_ref[...] = jnp.zeros_like(dbias_ref)
            ck_ref[...] = jnp.zeros_like(ck_ref)
            cv_ref[...] = jnp.zeros_like(cv_ref)
            sk_ref[...] = jnp.zeros_like(sk_ref)

        @pl.when(n < nb)
        def _():
            mask = _swa_mask(n)
            q = q_ref[...] * jnp.asarray(QK_SCALE, BF16)
            k = jnp.concatenate([kp_ref[...], kc_ref[...]], axis=0)
            v = jnp.concatenate([vp_ref[...], vc_ref[...]], axis=0)
            do = do_ref[...]
            dl = dl_ref[...]
            lse_all = lse_ref[...]
            dks = [None] * SWA_KV_HEADS
            dvs = [None] * SWA_KV_HEADS
            gsk = jnp.zeros((1, LANES), F32)
            kgs = [k[:, g * HEAD_DIM:(g + 1) * HEAD_DIM] for g in range(SWA_KV_HEADS)]
            vgs = [v[:, g * HEAD_DIM:(g + 1) * HEAD_DIM] for g in range(SWA_KV_HEADS)]
            qhs = [q[:, h * HEAD_DIM:(h + 1) * HEAD_DIM] for h in range(SWA_HEADS)]
            dohs = [do[:, h * HEAD_DIM:(h + 1) * HEAD_DIM] for h in range(SWA_HEADS)]
            raw = [_dot_nt(qhs[h], kgs[h // SWA_GROUP]) for h in range(SWA_HEADS)]
            dps = [_dot_nt(dohs[h], vgs[h // SWA_GROUP]) for h in range(SWA_HEADS)]
            ps, dss = [], []
            for h in range(SWA_HEADS):
                lse_h = lse_all[:, h:h + 1]
                dlt = dl[:, h * HEAD_DIM:h * HEAD_DIM + 1]
                sc = jnp.where(mask, raw[h] + bias_ref[h], NEG)
                p = jnp.exp(sc - lse_h)
                ds = p * (dps[h] - dlt)
                dbias_ref[h] += ds
                p_sink = jnp.exp(sink_ref[0, h] - lse_h)
                gsk = gsk + jnp.where(lane == h, -jnp.sum(p_sink * dlt), 0.0)
                ps.append(p.astype(BF16))
                dss.append(ds.astype(BF16))
            dqs = [_dot(dss[h], kgs[h // SWA_GROUP]) * QK_SCALE for h in range(SWA_HEADS)]
            for h in range(SWA_HEADS):
                g = h // SWA_GROUP
                dk_h = _dot_tn(dss[h], qhs[h])
                dv_h = _dot_tn(ps[h], dohs[h])
                dks[g] = dk_h if dks[g] is None else dks[g] + dk_h
                dvs[g] = dv_h if dvs[g] is None else dvs[g] + dv_h
            dq_ref[...] = jnp.concatenate(dqs, axis=1).astype(BF16)
            sk_ref[...] += jnp.broadcast_to(gsk, sk_ref.shape)
            dk2 = jnp.concatenate(dks, axis=1)
            dv2 = jnp.concatenate(dvs, axis=1)
            dk_ref[...] = (ck_ref[...] + dk2[:BLOCK]).astype(BF16)
            dv_ref[...] = (cv_ref[...] + dv2[:BLOCK]).astype(BF16)
            ck_ref[...] = dk2[BLOCK:]
            cv_ref[...] = dv2[BLOCK:]

        @pl.when(n == nb)
        def _():
            dk_ref[...] = ck_ref[...].astype(BF16)
            dv_ref[...] = cv_ref[...].astype(BF16)
            gsk_ref[...] = sk_ref[...]
            bk = bk_ref[...]
            rowi = lax.broadcasted_iota(jnp.int32, (NUM_BUCKETS, LANES), 0)
            lanei = lax.broadcasted_iota(jnp.int32, (NUM_BUCKETS, LANES), 1)
            out = jnp.zeros((NUM_BUCKETS, LANES), F32)
            for h in range(SWA_HEADS):
                db = dbias_ref[h]
                for b in range(NUM_BUCKETS):
                    val = jnp.sum(jnp.where(bk == b, db, 0.0))
                    out = jnp.where((rowi == b) & (lanei == h), val, out)
            grb_ref[...] = out

    cq, ck, cv = COL_SQ // SWA_W, COL_SK // LANES, COL_SV // LANES
    cur = lambda n: jnp.minimum(n, nb - 1)
    prev = lambda n: jnp.maximum(jnp.minimum(n, nb - 1) - 1, 0)
    kout = lambda n: jnp.maximum(n - 1, 0)
    return pl.pallas_call(
        kern, name="swa_bwd",
        grid=(nb + 1,),
        in_specs=[pl.BlockSpec((BLOCK, SWA_W), lambda n: (cur(n), cq)),
                  pl.BlockSpec((BLOCK, LANES), lambda n: (prev(n), ck)),
                  pl.BlockSpec((BLOCK, LANES), lambda n: (cur(n), ck)),
                  pl.BlockSpec((BLOCK, LANES), lambda n: (prev(n), cv)),
                  pl.BlockSpec((BLOCK, LANES), lambda n: (cur(n), cv)),
                  pl.BlockSpec((BLOCK, SWA_W), lambda n: (cur(n), 1)),
                  pl.BlockSpec((BLOCK, SWA_W), lambda n: (cur(n), 1)),
                  pl.BlockSpec((BLOCK, LANES), lambda n: (cur(n), 0)),
                  pl.BlockSpec((SWA_HEADS, BLOCK, 2 * BLOCK), lambda n: (0, 0, 0)),
                  pl.BlockSpec(memory_space=pltpu.SMEM),
                  pl.BlockSpec((BLOCK, 2 * BLOCK), lambda n: (0, 0))],
        out_specs=[pl.BlockSpec((BLOCK, SWA_W), lambda n: (cur(n), 0)),
                   pl.BlockSpec((BLOCK, LANES), lambda n: (kout(n), 0)),
                   pl.BlockSpec((BLOCK, LANES), lambda n: (kout(n), 0)),
                   pl.BlockSpec((NUM_BUCKETS, LANES), lambda n: (0, 0)),
                   pl.BlockSpec((8, LANES), lambda n: (0, 0))],
        out_shape=[jax.ShapeDtypeStruct((s, SWA_W), BF16),
                   jax.ShapeDtypeStruct((s, LANES), BF16),
                   jax.ShapeDtypeStruct((s, LANES), BF16),
                   jax.ShapeDtypeStruct((NUM_BUCKETS, LANES), F32),
                   jax.ShapeDtypeStruct((8, LANES), F32)],
        scratch_shapes=[pltpu.VMEM((SWA_HEADS, BLOCK, 2 * BLOCK), F32),
                        pltpu.VMEM((BLOCK, LANES), F32),
                        pltpu.VMEM((BLOCK, LANES), F32),
                        pltpu.VMEM((8, LANES), F32)],
        compiler_params=_cparams(("arbitrary",)),
    )(qkv, qkv, qkv, qkv, qkv, do_bf, delta_b, lse, bias, sink, bucket)


def _post(x, target, o_fox, o_swa, z, w_o, ln_g, ln_b):
    s = x.shape[0]
    tm = min(256, s)
    nt = s // tm
    seg = 256

    def kern(x_ref, t_ref, of_ref, os_ref, z_ref, w_ref, g_ref, b_ref,
             loss_ref, dh_ref, dy_ref, mix_ref, do_ref, dz_ref, dl_ref, gg_ref, gb_ref, lacc_ref):
        step = pl.program_id(0)

        @pl.when(step == 0)
        def _():
            lacc_ref[...] = jnp.zeros_like(lacc_ref)
            gg_ref[...] = jnp.zeros_like(gg_ref)
            gb_ref[...] = jnp.zeros_like(gb_ref)

        o = jnp.concatenate([of_ref[...], os_ref[...]], axis=1)
        zz = z_ref[...]
        sig = 1.0 / (1.0 + jnp.exp(-zz))
        silu = zz * sig
        mixed = (o * silu).astype(BF16)
        mix_ref[...] = mixed
        w = w_ref[...]
        h = ALPHA * x_ref[...] + _dot(mixed, w)
        mu = jnp.mean(h, axis=1, keepdims=True)
        hc = h - mu
        var = jnp.mean(hc * hc, axis=1, keepdims=True)
        rstd = lax.rsqrt(var + LN_EPS)
        xhat = hc * rstd
        g = g_ref[...]
        err = xhat * g + b_ref[...] - t_ref[...]
        lacc_ref[...] += jnp.broadcast_to(jnp.sum(err * err, axis=0, keepdims=True), lacc_ref.shape)
        dout = err * (1.0 / D_MODEL)
        gg_ref[...] += jnp.broadcast_to(jnp.sum(dout * xhat, axis=0, keepdims=True), gg_ref.shape)
        gb_ref[...] += jnp.broadcast_to(jnp.sum(dout, axis=0, keepdims=True), gb_ref.shape)
        dxh = dout * g
        m1 = jnp.mean(dxh, axis=1, keepdims=True)
        m2 = jnp.mean(dxh * xhat, axis=1, keepdims=True)
        dh = rstd * (dxh - m1 - xhat * m2)
        dh_ref[...] = dh
        dy = dh.astype(BF16)
        dy_ref[...] = dy
        dmix = _dot_nt(dy, w)
        do = dmix * silu
        do_ref[...] = do.astype(BF16)
        dz_ref[...] = (dmix * o * (sig * (1.0 + zz * (1.0 - sig)))).astype(BF16)
        r = lax.broadcasted_iota(jnp.int32, (seg, seg), 0) // HEAD_DIM
        c = lax.broadcasted_iota(jnp.int32, (seg, seg), 1) // HEAD_DIM
        bd = jnp.where(r == c, 1.0, 0.0).astype(BF16)
        prod = do * o
        parts = [_exact_dot(bd, prod[:, j * seg:(j + 1) * seg], False) for j in range(D_MODEL // seg)]
        dl_ref[...] = jnp.concatenate(parts, axis=1)

        @pl.when(step == nt - 1)
        def _():
            tot = jnp.sum(lacc_ref[0:1, :]) * (0.5 / D_MODEL)
            loss_ref[...] = jnp.broadcast_to(tot, loss_ref.shape)

    row = lambda i: (i, 0)
    fixed = lambda i: (0, 0)
    wide = pl.BlockSpec((tm, D_MODEL), row)
    half = pl.BlockSpec((tm, FOX_W), row)
    return pl.pallas_call(
        kern, name="post",
        grid=(nt,),
        in_specs=[wide, wide, half, half, wide,
                  pl.BlockSpec((D_MODEL, D_MODEL), fixed),
                  pl.BlockSpec((1, D_MODEL), fixed),
                  pl.BlockSpec((1, D_MODEL), fixed)],
        out_specs=[pl.BlockSpec((8, LANES), fixed), wide, wide, wide, wide, wide, wide,
                   pl.BlockSpec((8, D_MODEL), fixed), pl.BlockSpec((8, D_MODEL), fixed)],
        out_shape=[jax.ShapeDtypeStruct((8, LANES), F32),
                   jax.ShapeDtypeStruct((s, D_MODEL), F32),
                   jax.ShapeDtypeStruct((s, D_MODEL), BF16),
                   jax.ShapeDtypeStruct((s, D_MODEL), BF16),
                   jax.ShapeDtypeStruct((s, D_MODEL), BF16),
                   jax.ShapeDtypeStruct((s, D_MODEL), BF16),
                   jax.ShapeDtypeStruct((s, D_MODEL), F32),
                   jax.ShapeDtypeStruct((8, D_MODEL), F32),
                   jax.ShapeDtypeStruct((8, D_MODEL), F32)],
        scratch_shapes=[pltpu.VMEM((8, D_MODEL), F32)],
        compiler_params=_cparams(("arbitrary",)),
    )(x, target, o_fox, o_swa, z, w_o, ln_g, ln_b)


def _adamw_math(w, g, m, v):
    m = ADAM_B1 * m + (1.0 - ADAM_B1) * g
    v = ADAM_B2 * v + (1.0 - ADAM_B2) * (g * g)
    m_hat = m / (1.0 - ADAM_B1 ** ADAM_STEP)
    v_hat = v / (1.0 - ADAM_B2 ** ADAM_STEP)
    delta = -ADAM_LR * (m_hat / (jnp.sqrt(v_hat) + ADAM_EPS) + ADAM_WD * w)
    return delta, m, v


def _adamw(w, g, m, v, *, name):
    r, c = w.shape
    tr = min(256, r)

    def kern(w_ref, g_ref, m_ref, v_ref, d_ref, mo_ref, vo_ref):
        d, mn, vn = _adamw_math(w_ref[...], g_ref[...], m_ref[...], v_ref[...])
        d_ref[...] = d
        mo_ref[...] = mn
        vo_ref[...] = vn

    blk = pl.BlockSpec((tr, c), lambda i: (i, 0))
    sds = jax.ShapeDtypeStruct((r, c), F32)
    return pl.pallas_call(
        kern, name=name,
        grid=(r // tr,),
        in_specs=[blk, blk, blk, blk],
        out_specs=[blk, blk, blk],
        out_shape=[sds, sds, sds],
        compiler_params=_cparams(("parallel",)),
    )(w, g, m, v)


def _position():
    x, y, c = lax.axis_index("x"), lax.axis_index("y"), lax.axis_index("c")
    chips = [(1 - x, y), (x, 1 - y), (1 - x, 1 - y)]
    return x, y, c, chips


def _chip_index(cx, cy):
    return 2 * cx + cy


def _gather_weights(w_in_bf, w_o_bf):
    shards = (w_in_bf, w_o_bf)
    n_arr = len(shards)

    def kern(*refs):
        ins, outs = refs[:n_arr], refs[n_arr:2 * n_arr]
        send_sems, recv_sems, local_sems = refs[2 * n_arr:]
        x, y, c, chips = _position()
        me = _chip_index(x, y)
        sibling = (x, y, 1 - c)

        local = [pltpu.make_async_copy(ins[a], outs[a].at[me], local_sems.at[a]) for a in range(n_arr)]
        for cp in local:
            cp.start()

        def half(ref, a):
            rows = shards[a].shape[0] // 2
            return ref.at[pl.ds(c * rows, rows), :]

        def copy(a, k, src, slot, to):
            return pltpu.make_async_remote_copy(
                src_ref=src, dst_ref=half(outs[a].at[slot], a),
                send_sem=send_sems.at[a * 6 + k], recv_sem=recv_sems.at[a * 6 + k],
                device_id=to, device_id_type=MESH)

        first = [copy(a, j, half(ins[a], a), me, (*chip, c)) for a in range(n_arr) for j, chip in enumerate(chips)]
        for cp in first:
            cp.start()
        passed = []
        for a in range(n_arr):
            for j, chip in enumerate(chips):
                slot = _chip_index(*chip)
                copy(a, j, half(ins[a], a), slot, (*chip, c)).wait_recv()
                fwd = copy(a, 3 + j, half(outs[a].at[slot], a), slot, sibling)
                fwd.start()
                passed.append(fwd)
        for a in range(n_arr):
            for j, chip in enumerate(chips):
                slot = _chip_index(*chip)
                rows = shards[a].shape[0] // 2
                dst = outs[a].at[slot].at[pl.ds((1 - c) * rows, rows), :]
                pltpu.make_async_remote_copy(
                    src_ref=dst, dst_ref=dst, send_sem=send_sems.at[a * 6 + 3 + j],
                    recv_sem=recv_sems.at[a * 6 + 3 + j], device_id=sibling, device_id_type=MESH).wait_recv()
        for cp in first + passed:
            cp.wait_send()
        for cp in local:
            cp.wait()

    hbm = pl.BlockSpec(memory_space=pl.ANY)
    return pl.pallas_call(
        kern, name="gather_weights",
        in_specs=[hbm] * n_arr,
        out_specs=[hbm] * n_arr,
        out_shape=[jax.ShapeDtypeStruct((N_CHIPS,) + w.shape, w.dtype) for w in shards],
        scratch_shapes=[pltpu.SemaphoreType.DMA((6 * n_arr,)),
                        pltpu.SemaphoreType.DMA((6 * n_arr,)),
                        pltpu.SemaphoreType.DMA((n_arr,))],
    )(*shards)


def _swap_halves(grads):
    n_arr = len(grads)

    def kern(*refs):
        ins = refs[:n_arr]
        owns = refs[n_arr:2 * n_arr]
        gots = refs[2 * n_arr:3 * n_arr]
        send_sems, recv_sems, local_sems = refs[3 * n_arr:]
        x, y, c, _ = _position()
        sibling = (x, y, 1 - c)
        local, remote = [], []
        for a in range(n_arr):
            rows = grads[a].shape[1] // 2
            piece = rows // COPY_PIECES
            for j in range(N_CHIPS):
                for r in range(COPY_PIECES):
                    k = (a * N_CHIPS + j) * COPY_PIECES + r
                    dst_rows = pl.ds(r * piece, piece)
                    local.append(pltpu.make_async_copy(
                        ins[a].at[j, pl.ds(c * rows + r * piece, piece), :],
                        owns[a].at[j, dst_rows, :], local_sems.at[k]))
                    remote.append(pltpu.make_async_remote_copy(
                        src_ref=ins[a].at[j, pl.ds((1 - c) * rows + r * piece, piece), :],
                        dst_ref=gots[a].at[j, dst_rows, :], send_sem=send_sems.at[k], recv_sem=recv_sems.at[k],
                        device_id=sibling, device_id_type=MESH))
        for cp in local + remote:
            cp.start()
        for cp in remote:
            cp.wait()
        for cp in local:
            cp.wait()

    hbm = pl.BlockSpec(memory_space=pl.ANY)
    half = [jax.ShapeDtypeStruct((N_CHIPS, g.shape[1] // 2, g.shape[2]), F32) for g in grads]
    outs = pl.pallas_call(
        kern, name="swap_halves",
        in_specs=[hbm] * n_arr,
        out_specs=[hbm] * (2 * n_arr),
        out_shape=half + half,
        scratch_shapes=[pltpu.SemaphoreType.DMA((n_arr * N_CHIPS * COPY_PIECES,)),
                        pltpu.SemaphoreType.DMA((n_arr * N_CHIPS * COPY_PIECES,)),
                        pltpu.SemaphoreType.DMA((n_arr * N_CHIPS * COPY_PIECES,))],
    )(*grads)
    return outs[:n_arr], outs[n_arr:]


def _scatter_to_owners(parts):
    n_arr = len(parts)

    def kern(*refs):
        ins = refs[:n_arr]
        outs = refs[n_arr:2 * n_arr]
        send_sems, recv_sems, local_sems = refs[2 * n_arr:]
        x, y, c, chips = _position()
        me = _chip_index(x, y)
        local = [pltpu.make_async_copy(ins[a].at[me], outs[a].at[me], local_sems.at[a]) for a in range(n_arr)]
        for cp in local:
            cp.start()
        sends = []
        for a in range(n_arr):
            for j, chip in enumerate(chips):
                sends.append(pltpu.make_async_remote_copy(
                    src_ref=ins[a].at[_chip_index(*chip)], dst_ref=outs[a].at[me],
                    send_sem=send_sems.at[a * 3 + j], recv_sem=recv_sems.at[a * 3 + j],
                    device_id=(*chip, c), device_id_type=MESH))
        for cp in sends:
            cp.start()
        for a in range(n_arr):
            for j, chip in enumerate(chips):
                slot = outs[a].at[_chip_index(*chip)]
                pltpu.make_async_remote_copy(
                    src_ref=slot, dst_ref=slot, send_sem=send_sems.at[a * 3 + j],
                    recv_sem=recv_sems.at[a * 3 + j], device_id=(*chip, c), device_id_type=MESH).wait_recv()
        for cp in sends:
            cp.wait_send()
        for cp in local:
            cp.wait()

    hbm = pl.BlockSpec(memory_space=pl.ANY)
    return pl.pallas_call(
        kern, name="scatter_to_owners",
        in_specs=[hbm] * n_arr,
        out_specs=[hbm] * n_arr,
        out_shape=[jax.ShapeDtypeStruct(p.shape, F32) for p in parts],
        scratch_shapes=[pltpu.SemaphoreType.DMA((3 * n_arr,)),
                        pltpu.SemaphoreType.DMA((3 * n_arr,)),
                        pltpu.SemaphoreType.DMA((n_arr,))],
    )(*parts)


def _join_halves(halves):
    n_arr = len(halves)

    def kern(*refs):
        ins = refs[:n_arr]
        outs = refs[n_arr:2 * n_arr]
        send_sems, recv_sems, local_sems = refs[2 * n_arr:]
        x, y, c, _ = _position()
        sibling = (x, y, 1 - c)
        local, remote = [], []
        for a in range(n_arr):
            rows = halves[a].shape[0]
            piece = rows // COPY_PIECES
            for r in range(COPY_PIECES):
                k = a * COPY_PIECES + r
                src = ins[a].at[pl.ds(r * piece, piece), :]
                dst = outs[a].at[pl.ds(c * rows + r * piece, piece), :]
                local.append(pltpu.make_async_copy(src, dst, local_sems.at[k]))
                remote.append(pltpu.make_async_remote_copy(
                    src_ref=src, dst_ref=dst, send_sem=send_sems.at[k], recv_sem=recv_sems.at[k],
                    device_id=sibling, device_id_type=MESH))
        for cp in local + remote:
            cp.start()
        for a in range(n_arr):
            rows = halves[a].shape[0]
            piece = rows // COPY_PIECES
            for r in range(COPY_PIECES):
                k = a * COPY_PIECES + r
                theirs = outs[a].at[pl.ds((1 - c) * rows + r * piece, piece), :]
                pltpu.make_async_remote_copy(
                    src_ref=theirs, dst_ref=theirs, send_sem=send_sems.at[k], recv_sem=recv_sems.at[k],
                    device_id=sibling, device_id_type=MESH).wait_recv()
        for cp in remote:
            cp.wait_send()
        for cp in local:
            cp.wait()

    hbm = pl.BlockSpec(memory_space=pl.ANY)
    return pl.pallas_call(
        kern, name="join_halves",
        in_specs=[hbm] * n_arr,
        out_specs=[hbm] * n_arr,
        out_shape=[jax.ShapeDtypeStruct((2 * h.shape[0], h.shape[1]), F32) for h in halves],
        scratch_shapes=[pltpu.SemaphoreType.DMA((n_arr * COPY_PIECES,)),
                        pltpu.SemaphoreType.DMA((n_arr * COPY_PIECES,)),
                        pltpu.SemaphoreType.DMA((n_arr * COPY_PIECES,))],
    )(*halves)


def _add2(a, b, *, name):
    n, r, c = a.shape
    tr = min(256, r)

    def kern(a_ref, b_ref, o_ref):
        o_ref[...] = a_ref[...] + b_ref[...]

    blk = pl.BlockSpec((1, tr, c), lambda j, i: (j, i, 0))
    return pl.pallas_call(
        kern, name=name,
        grid=(n, r // tr),
        in_specs=[blk, blk],
        out_specs=blk,
        out_shape=jax.ShapeDtypeStruct(a.shape, F32),
        compiler_params=_cparams(("parallel", "parallel")),
    )(a, b)


def _sum4(a, *, name):
    n, r, c = a.shape
    tr = min(256, r)

    def kern(a_ref, o_ref):
        o_ref[...] = ((a_ref[0] + a_ref[1]) + a_ref[2]) + a_ref[3]

    return pl.pallas_call(
        kern, name=name,
        grid=(r // tr,),
        in_specs=[pl.BlockSpec((n, tr, c), lambda i: (0, i, 0))],
        out_specs=pl.BlockSpec((tr, c), lambda i: (i, 0)),
        out_shape=jax.ShapeDtypeStruct((r, c), F32),
        compiler_params=_cparams(("parallel",)),
    )(a)


def _small_allreduce_adamw(g, w, m, v):
    def kern(g_ref, w_ref, m_ref, v_ref, gs_ref, d_ref, mo_ref, vo_ref, buf_ref, send_sems, recv_sems):
        x, y, c, _ = _position()
        me = 4 * x + 2 * y + c
        buf_ref[me] = g_ref[...]
        peers = [(x, y, 1 - c)] + [(px, py, pc) for px, py in _position()[3] for pc in (c, 1 - c)]
        sends = []
        for k, peer in enumerate(peers):
            sends.append(pltpu.make_async_remote_copy(
                src_ref=g_ref, dst_ref=buf_ref.at[me], send_sem=send_sems.at[k], recv_sem=recv_sems.at[k],
                device_id=peer, device_id_type=MESH))
        for cp in sends:
            cp.start()
        for k, (px, py, pc) in enumerate(peers):
            slot = buf_ref.at[4 * px + 2 * py + pc]
            pltpu.make_async_remote_copy(
                src_ref=slot, dst_ref=slot, send_sem=send_sems.at[k], recv_sem=recv_sems.at[k],
                device_id=(px, py, pc), device_id_type=MESH).wait_recv()
        for cp in sends:
            cp.wait_send()
        tot = buf_ref[0]
        for d in range(1, N_DEV):
            tot = tot + buf_ref[d]
        gs_ref[...] = tot
        delta, mn, vn = _adamw_math(w_ref[...], tot, m_ref[...], v_ref[...])
        d_ref[...] = delta
        mo_ref[...] = mn
        vo_ref[...] = vn

    vm = pl.BlockSpec(memory_space=pltpu.VMEM)
    sds = jax.ShapeDtypeStruct((SMALL_ROWS, LANES), F32)
    return pl.pallas_call(
        kern, name="small_allreduce_adamw",
        in_specs=[vm] * 4,
        out_specs=[vm] * 4,
        out_shape=[sds] * 4,
        scratch_shapes=[pltpu.VMEM((N_DEV, SMALL_ROWS, LANES), F32),
                        pltpu.SemaphoreType.DMA((N_DEV - 1,)),
                        pltpu.SemaphoreType.DMA((N_DEV - 1,))],
    )(g, w, m, v)


def _to_padded_cols(w):
    pad = jnp.zeros((w.shape[0], N_C - FOX_HEADS), w.dtype)
    return jnp.concatenate([w[:, 0:1536], w[:, 2056:2824], w[:, 1536:1544], pad,
                            w[:, 1544:2056], w[:, 2824:3336]], axis=1)


def _from_padded_cols(g):
    return jnp.concatenate([g[:, 0:1536], g[:, OFF_C:OFF_C + FOX_HEADS], g[:, OFF_B:OFF_B + FOX_W],
                            g[:, 1536:N_A], g[:, OFF_B + FOX_W:N_PAD]], axis=1)


def _pack_small(b_f, rel_bias, sink, ln_g, ln_b):
    row = lambda v: jnp.pad(v.reshape(1, -1), ((0, 0), (0, LANES - v.size)))
    return jnp.concatenate([ln_g.reshape(8, LANES), ln_b.reshape(8, LANES), rel_bias.reshape(2, LANES),
                            row(b_f), row(sink), jnp.zeros((4, LANES), F32)], axis=0)


def _unpack_small(p):
    ln_g = p[0:8].reshape(1, D_MODEL)
    ln_b = p[8:16].reshape(1, D_MODEL)
    rel_bias = p[16:18].reshape(NUM_BUCKETS, SWA_HEADS)
    b_f = p[18:19, :FOX_HEADS]
    sink = p[19:20, :SWA_HEADS]
    return b_f, rel_bias, sink, ln_g, ln_b


def _heads_to_rows(a_b):
    return a_b[:, ::HEAD_DIM].T.reshape(FOX_HEADS, 1, a_b.shape[0])


def kernel(x, w_in, b_f, rel_bias, sink, w_o, ln_g, ln_b, loss_target, m_w_in, m_b_f, m_rel_bias, m_sink, m_w_o, m_ln_g, m_ln_b, v_w_in, v_b_f, v_rel_bias, v_sink, v_w_o, v_ln_g, v_ln_b):
    x2 = x[0]
    tgt = loss_target[0]
    s = x2.shape[0]
    w_in2, w_o2 = w_in[0], w_o[0]

    shard_cols = D_IN // N_CHIPS
    col_pad = ((0, 0), (0, SHARD_PAD - shard_cols))
    w_in_all, w_o_all = _gather_weights(jnp.pad(w_in2.astype(BF16), col_pad), w_o2.astype(BF16))
    w_full = jnp.concatenate([w_in_all[j, :, :shard_cols] for j in range(N_CHIPS)], axis=1)
    w_pad = _to_padded_cols(w_full)
    w_o_full = w_o_all.reshape(D_MODEL, D_MODEL)

    x_bf = x2.astype(BF16)
    qkv = _matmul_nn(x_bf, w_pad, n_off=0, n_out=N_A, tm=512, tn=768, out_dtype=BF16, name="proj_qkv")
    z = _matmul_nn(x_bf, w_pad, n_off=OFF_B, n_out=N_B, tm=512, tn=512, out_dtype=F32, name="proj_gate")
    ffp = _matmul_nn(x_bf, w_pad, n_off=OFF_C, n_out=N_C, tm=512, tn=N_C, out_dtype=F32, name="proj_forget")
    bfp = jnp.pad(b_f, ((0, 0), (0, LANES - FOX_HEADS)))
    cum = _cum_fwd(ffp, bfp)
    cum_t3 = cum[:, :FOX_HEADS].T.reshape(FOX_HEADS, 1, s)
    vt = qkv[:, COL_FV:COL_FV + FOX_W].T
    o_fox, lse_t3 = _fox_fwd(qkv, vt, cum_t3, cum)
    bucket = jnp.asarray(_bucket_table())
    bias = _swa_bias(rel_bias, bucket)
    o_swa, lse_swa = _swa_fwd(qkv, bias, sink)

    loss8, dh, dy, mixed, do_bf, dz, delta_b, gg8, gb8 = _post(
        x2, tgt, o_fox, o_swa, z, w_o_full, ln_g, ln_b)
    loss = lax.psum(loss8[0, 0], ("x", "y", "c"))
    grad_w_o_full = _matmul_acc(mixed.T, dy, tm=1024, tn=512, tk=1024, name="grad_w_o")

    delta_t3 = _heads_to_rows(delta_b[:, :FOX_W])
    dqt_fox, dk_fox, dv_fox, dcum_k, dcum_q = _fox_bwd(qkv, do_bf, cum_t3, cum, lse_t3, delta_t3)
    dcum_q = jnp.pad(dcum_q.reshape(FOX_HEADS, s).T, ((0, 0), (0, LANES - FOX_HEADS)))
    dff, gbf8 = _cum_bwd(dcum_k, dcum_q, ffp, bfp)
    dq_swa, dk_swa, dv_swa, grb, gsk8 = _swa_bwd(qkv, do_bf, delta_b, lse_swa, bias, sink, bucket)

    dproj = jnp.concatenate([dqt_fox.T.astype(BF16), dk_fox, dv_fox, dq_swa, dk_swa, dv_swa, dff, dz], axis=1)
    grad_x = _grad_x_matmul(dproj, w_pad, dh, tm=512, tn=512, name="grad_x")
    grad_w_pad = _matmul_acc(x_bf.T, dproj, tm=1024, tn=512, tk=1024, name="grad_w_in")
    grad_w_in_full = _from_padded_cols(grad_w_pad)

    g_in4 = jnp.stack([jnp.pad(grad_w_in_full[:, j * shard_cols:(j + 1) * shard_cols], col_pad)
                       for j in range(N_CHIPS)])
    g_o4 = grad_w_o_full.reshape(N_CHIPS, D_MODEL // N_CHIPS, D_MODEL)
    owns, gots = _swap_halves([g_in4, g_o4])
    parts = [_add2(owns[0], gots[0], name="pair_sum_w_in"), _add2(owns[1], gots[1], name="pair_sum_w_o")]
    slabs = _scatter_to_owners(parts)
    halves = [_sum4(slabs[0], name="chip_sum_w_in"), _sum4(slabs[1], name="chip_sum_w_o")]
    g_w_in, g_w_o = _join_halves(halves)
    g_w_in = g_w_in[:, :shard_cols]

    d_w_in, nm_w_in, nv_w_in = _adamw(w_in2, g_w_in, m_w_in[0], v_w_in[0], name="adamw_w_in")
    d_w_o, nm_w_o, nv_w_o = _adamw(w_o2, g_w_o, m_w_o[0], v_w_o[0], name="adamw_w_o")

    g_small = _pack_small(gbf8[0:1, :FOX_HEADS], grb[:, :SWA_HEADS], gsk8[0:1, :SWA_HEADS], gg8[0:1], gb8[0:1])
    w_small = _pack_small(b_f, rel_bias, sink, ln_g, ln_b)
    m_small = _pack_small(m_b_f, m_rel_bias, m_sink, m_ln_g, m_ln_b)
    v_small = _pack_small(v_b_f, v_rel_bias, v_sink, v_ln_g, v_ln_b)
    gs, ds, ms, vs = _small_allreduce_adamw(g_small, w_small, m_small, v_small)
    g_bf, g_rb, g_sk, g_lg, g_lb = _unpack_small(gs)
    d_bf, d_rb, d_sk, d_lg, d_lb = _unpack_small(ds)
    m_bf, m_rb, m_sk, m_lg, m_lb = _unpack_small(ms)
    v_bf, v_rb, v_sk, v_lg, v_lb = _unpack_small(vs)

    e = lambda a: a[None]
    return (loss, e(grad_x),
            e(g_w_in), g_bf, g_rb, g_sk, e(g_w_o), g_lg, g_lb,
            e(d_w_in), d_bf, d_rb, d_sk, e(d_w_o), d_lg, d_lb,
            e(nm_w_in), m_bf, m_rb, m_sk, e(nm_w_o), m_lg, m_lb,
            e(nv_w_in), v_bf, v_rb, v_sk, e(nv_w_o), v_lg, v_lb)
```

```python
import functools
import math

import numpy as np
import jax
import jax.numpy as jnp
from jax import lax
from jax.experimental import pallas as pl
from jax.experimental.pallas import tpu as pltpu

F32 = jnp.float32
BF16 = jnp.bfloat16

D_MODEL = 1024
HEAD_DIM = 64
FOX_HEADS = 8
SWA_HEADS = 8
SWA_KV_HEADS = 2
SWA_GROUP = 4
FOX_W = 512
SWA_W = 512
SWA_KV_W = 128
BLOCK = 128
NUM_BUCKETS = 32
MAX_DISTANCE = 128
LN_EPS = 1e-5
NEG = -1e30
ALPHA = 2.0 ** 0.25
QK_SCALE = 0.125

ADAM_LR = 0.001
ADAM_B1 = 0.9
ADAM_B2 = 0.999
ADAM_EPS = 1e-08
ADAM_WD = 0.01
ADAM_STEP = 10

D_IN = 3336
SHARD_PAD = 896
N_A = 2304
N_C = 256
N_B = 1024
OFF_C = N_A
OFF_B = N_A + N_C
N_PAD = N_A + N_C + N_B
COL_FK, COL_FV, COL_SQ, COL_SK, COL_SV = 512, 1024, 1536, 2048, 2176

LANES = 128
FOX_T = 256
VMEM_LIMIT = 56 * 1024 * 1024

MESH = pl.DeviceIdType.MESH
N_CHIPS = 4
N_DEV = 8
SMALL_ROWS = 24
COPY_PIECES = 4


def _cparams(sem=None):
    return pltpu.CompilerParams(dimension_semantics=sem, vmem_limit_bytes=VMEM_LIMIT)


def _split3(x):
    hi = x.astype(BF16)
    r = x - hi.astype(F32)
    mid = r.astype(BF16)
    lo = (r - mid.astype(F32)).astype(BF16)
    return hi, mid, lo


def _dot(a, b):
    return jnp.dot(a, b, preferred_element_type=F32)


def _dot_nt(a, b):
    return lax.dot_general(a, b, (((1,), (1,)), ((), ())), preferred_element_type=F32)


def _dot_tn(a, b):
    return lax.dot_general(a, b, (((0,), (0,)), ((), ())), preferred_element_type=F32)


def _matmul_nn(a, b, *, n_off, n_out, tm, tn, out_dtype, name):
    m, k = a.shape
    joff = n_off // tn

    def kern(a_ref, b_ref, o_ref):
        o_ref[...] = _dot(a_ref[...], b_ref[...]).astype(o_ref.dtype)

    return pl.pallas_call(
        kern, name=name,
        grid=(n_out // tn, m // tm),
        in_specs=[pl.BlockSpec((tm, k), lambda j, i: (i, 0)),
                  pl.BlockSpec((k, tn), lambda j, i: (0, j + joff))],
        out_specs=pl.BlockSpec((tm, tn), lambda j, i: (i, j)),
        out_shape=jax.ShapeDtypeStruct((m, n_out), out_dtype),
        compiler_params=_cparams(("parallel", "parallel")),
    )(a, b)


def _grad_x_matmul(dproj, w_pad, dh, *, tm, tn, name):
    m, k = dproj.shape
    n = w_pad.shape[0]

    def kern(a_ref, b_ref, dh_ref, o_ref):
        o_ref[...] = ALPHA * dh_ref[...] + _dot_nt(a_ref[...], b_ref[...])

    return pl.pallas_call(
        kern, name=name,
        grid=(n // tn, m // tm),
        in_specs=[pl.BlockSpec((tm, k), lambda j, i: (i, 0)),
                  pl.BlockSpec((tn, k), lambda j, i: (j, 0)),
                  pl.BlockSpec((tm, tn), lambda j, i: (i, j))],
        out_specs=pl.BlockSpec((tm, tn), lambda j, i: (i, j)),
        out_shape=jax.ShapeDtypeStruct((m, n), F32),
        compiler_params=_cparams(("parallel", "parallel")),
    )(dproj, w_pad, dh)


def _matmul_acc(at, b, *, tm, tn, tk, name):
    m, s = at.shape
    n = b.shape[1]

    def kern(a_ref, b_ref, o_ref):
        @pl.when(pl.program_id(2) == 0)
        def _():
            o_ref[...] = jnp.zeros_like(o_ref)
        o_ref[...] += _dot(a_ref[...], b_ref[...])

    return pl.pallas_call(
        kern, name=name,
        grid=(m // tm, n // tn, s // tk),
        in_specs=[pl.BlockSpec((tm, tk), lambda i, j, k: (i, k)),
                  pl.BlockSpec((tk, tn), lambda i, j, k: (k, j))],
        out_specs=pl.BlockSpec((tm, tn), lambda i, j, k: (i, j)),
        out_shape=jax.ShapeDtypeStruct((m, n), F32),
        compiler_params=_cparams(("parallel", "parallel", "arbitrary")),
    )(at, b)


def _tri(n, lower):
    r = lax.broadcasted_iota(jnp.int32, (n, n), 0)
    c = lax.broadcasted_iota(jnp.int32, (n, n), 1)
    keep = (c <= r) if lower else (c >= r)
    return jnp.where(keep, 1.0, 0.0).astype(BF16)


def _exact_dot(mat_bf16, x_f32, left):
    out = None
    for piece in _split3(x_f32):
        t = _dot(mat_bf16, piece) if left else _dot(piece, mat_bf16)
        out = t if out is None else out + t
    return out


def _log_sigmoid(z):
    return jnp.minimum(z, 0.0) - jnp.log(1.0 + jnp.exp(-jnp.abs(z)))


def _cum_fwd(ffp, bfp):
    s = ffp.shape[0]
    t = min(256, s)

    def kern(ff_ref, b_ref, cum_ref, carry_ref):
        @pl.when(pl.program_id(0) == 0)
        def _():
            carry_ref[...] = jnp.zeros_like(carry_ref)
        lane = lax.broadcasted_iota(jnp.int32, (1, LANES), 1)
        lf = _log_sigmoid(ff_ref[...] + b_ref[...])
        lf = jnp.where(lane < FOX_HEADS, lf, 0.0)
        cum = _exact_dot(_tri(t, True), lf, True) + carry_ref[0:1, :]
        cum_ref[...] = cum
        carry_ref[...] = jnp.broadcast_to(cum[t - 1:t, :], carry_ref.shape)

    return pl.pallas_call(
        kern, name="cum_fwd",
        grid=(s // t,),
        in_specs=[pl.BlockSpec((t, LANES), lambda i: (i, 0)),
                  pl.BlockSpec((1, LANES), lambda i: (0, 0))],
        out_specs=pl.BlockSpec((t, LANES), lambda i: (i, 0)),
        out_shape=jax.ShapeDtypeStruct((s, LANES), F32),
        scratch_shapes=[pltpu.VMEM((8, LANES), F32)],
        compiler_params=_cparams(("arbitrary",)),
    )(ffp, bfp)


def _cum_bwd(dcum_k, dcum_q, ffp, bfp):
    s = dcum_k.shape[0]
    t = min(256, s)
    nb = s // t

    def kern(dck_ref, dcq_ref, ff_ref, b_ref, dff_ref, gb_ref, carry_ref):
        @pl.when(pl.program_id(0) == 0)
        def _():
            carry_ref[...] = jnp.zeros_like(carry_ref)
            gb_ref[...] = jnp.zeros_like(gb_ref)
        lane = lax.broadcasted_iota(jnp.int32, (1, LANES), 1)
        dlf = _exact_dot(_tri(t, False), dck_ref[...] + dcq_ref[...], True) + carry_ref[0:1, :]
        carry_ref[...] = jnp.broadcast_to(dlf[0:1, :], carry_ref.shape)
        z = ff_ref[...] + b_ref[...]
        dff = jnp.where(lane < FOX_HEADS, dlf / (1.0 + jnp.exp(z)), 0.0)
        gb_ref[...] += jnp.broadcast_to(jnp.sum(dff, axis=0, keepdims=True), gb_ref.shape)
        dff_ref[...] = jnp.concatenate([dff, jnp.zeros_like(dff)], axis=1).astype(BF16)

    return pl.pallas_call(
        kern, name="cum_bwd",
        grid=(nb,),
        in_specs=[pl.BlockSpec((t, LANES), lambda i: (nb - 1 - i, 0)),
                  pl.BlockSpec((t, LANES), lambda i: (nb - 1 - i, 0)),
                  pl.BlockSpec((t, LANES), lambda i: (nb - 1 - i, 0)),
                  pl.BlockSpec((1, LANES), lambda i: (0, 0))],
        out_specs=[pl.BlockSpec((t, N_C), lambda i: (nb - 1 - i, 0)),
                   pl.BlockSpec((8, LANES), lambda i: (0, 0))],
        out_shape=[jax.ShapeDtypeStruct((s, N_C), BF16),
                   jax.ShapeDtypeStruct((8, LANES), F32)],
        scratch_shapes=[pltpu.VMEM((8, LANES), F32)],
        compiler_params=_cparams(("arbitrary",)),
    )(dcum_k, dcum_q, ffp, bfp)


def _resident(shape, index_map):
    return pl.BlockSpec(shape, index_map, pipeline_mode=pl.Buffered(1))


def _fox_fwd(qkv, vt, cum_t3, cum):
    s = qkv.shape[0]
    t = min(FOX_T, s)
    nq = s // t
    nh = FOX_HEADS

    def kern(q_ref, k_ref, vt_ref, ct_ref, c_ref, o_ref, lse_ref, m_ref, l_ref, acc_ref):
        i = pl.program_id(0)
        lane = lax.broadcasted_iota(jnp.int32, (1, LANES), 1)
        krow = lax.broadcasted_iota(jnp.int32, (t, t), 0)
        qcol = lax.broadcasted_iota(jnp.int32, (t, t), 1)
        causal = krow <= qcol
        q0 = pl.multiple_of(i * t, t)
        qts, crefs = [], []
        for h in range(nh):
            p, a = divmod(h, 2)
            q2 = q_ref[:, p * LANES:(p + 1) * LANES] * jnp.asarray(QK_SCALE, BF16)
            sel = (lane < HEAD_DIM) if a == 0 else (lane >= HEAD_DIM)
            qts.append(jnp.where(sel, q2, jnp.zeros_like(q2)).astype(F32).T.astype(BF16))
            crefs.append(ct_ref[h, :, pl.ds(q0, LANES)][:, 0:1])
        m_ref[...] = jnp.full(m_ref.shape, NEG, F32)
        l_ref[...] = jnp.zeros_like(l_ref)
        acc_ref[...] = jnp.zeros_like(acc_ref)

        def tile(j, masked):
            k0 = pl.multiple_of(j * t, t)
            cb = c_ref[pl.ds(k0, t), :]
            sts = [_dot(k_ref[pl.ds(k0, t), (h // 2) * LANES:(h // 2 + 1) * LANES], qts[h]) for h in range(nh)]
            pts, scales = [], []
            for h in range(nh):
                u = sts[h] - (cb[:, h:h + 1] - crefs[h])
                if masked:
                    u = jnp.where(causal, u, NEG)
                m_old = m_ref[h]
                m_new = jnp.maximum(m_old, jnp.max(u, axis=0, keepdims=True))
                scale = jnp.exp(m_old - m_new)
                p = jnp.exp(u - m_new)
                l_ref[h] = scale * l_ref[h] + jnp.sum(p, axis=0, keepdims=True)
                m_ref[h] = m_new
                pts.append(p.astype(BF16))
                scales.append(scale)
            for h in range(nh):
                vth = vt_ref[h * HEAD_DIM:(h + 1) * HEAD_DIM, pl.ds(k0, t)]
                acc_ref[h] = scales[h] * acc_ref[h] + _dot(vth, pts[h])

        def body(j, c):
            tile(j, False)
            return c
        lax.fori_loop(0, i, body, 0)
        tile(i, True)

        for p in range(nh // 2):
            ot = jnp.concatenate([acc_ref[2 * p + a] * (1.0 / l_ref[2 * p + a]) for a in range(2)], axis=0)
            o_ref[:, p * LANES:(p + 1) * LANES] = ot.T
        for h in range(nh):
            lse_ref[h, :, pl.ds(q0, t)] = m_ref[h] + jnp.log(l_ref[h])

    return pl.pallas_call(
        kern, name="fox_fwd",
        grid=(nq,),
        in_specs=[pl.BlockSpec((t, FOX_W), lambda i: (i, 0)),
                  _resident((s, FOX_W), lambda i: (0, COL_FK // FOX_W)),
                  _resident((FOX_W, s), lambda i: (0, 0)),
                  _resident((nh, 1, s), lambda i: (0, 0, 0)),
                  _resident((s, LANES), lambda i: (0, 0))],
        out_specs=[pl.BlockSpec((t, FOX_W), lambda i: (i, 0)),
                   pl.BlockSpec((nh, 1, s), lambda i: (0, 0, 0))],
        out_shape=[jax.ShapeDtypeStruct((s, FOX_W), F32),
                   jax.ShapeDtypeStruct((nh, 1, s), F32)],
        scratch_shapes=[pltpu.VMEM((nh, 1, t), F32),
                        pltpu.VMEM((nh, 1, t), F32),
                        pltpu.VMEM((nh, HEAD_DIM, t), F32)],
        compiler_params=_cparams(("arbitrary",)),
    )(qkv, qkv, vt, cum_t3, cum)


def _fox_bwd(qkv, do_bf, cum_t3, cum, lse_t3, delta_t3):
    s = qkv.shape[0]
    t = min(FOX_T, s)
    nq = s // t
    nh = FOX_HEADS
    npair = nh // 2

    def kern(q_ref, do_ref, k_ref, v_ref, ct_ref, c_ref, lse_ref, dl_ref,
             dqt_ref, dk_ref, dv_ref, dc_ref, dcq_ref, accv_ref, acck_ref, accd_ref):
        kj = pl.program_id(0)
        lane = lax.broadcasted_iota(jnp.int32, (1, LANES), 1)
        sub = lax.broadcasted_iota(jnp.int32, (LANES, 1), 0)
        krow = lax.broadcasted_iota(jnp.int32, (t, t), 0)
        qcol = lax.broadcasted_iota(jnp.int32, (t, t), 1)
        causal = krow <= qcol
        sels = [lane < HEAD_DIM, lane >= HEAD_DIM]
        subsels = [sub < HEAD_DIM, sub >= HEAD_DIM]

        @pl.when(kj == 0)
        def _():
            dqt_ref[...] = jnp.zeros_like(dqt_ref)
            dcq_ref[...] = jnp.zeros_like(dcq_ref)

        accv_ref[...] = jnp.zeros_like(accv_ref)
        acck_ref[...] = jnp.zeros_like(acck_ref)
        accd_ref[...] = jnp.zeros_like(accd_ref)
        cb = c_ref[...]
        k2s, v2s, kts = [], [], []
        for p in range(npair):
            k2 = k_ref[:, p * LANES:(p + 1) * LANES]
            k2s.append(k2)
            v2s.append(v_ref[:, p * LANES:(p + 1) * LANES])
            kt = k2.astype(F32).T * QK_SCALE
            kts.append([jnp.where(subsels[a], kt, 0.0).astype(BF16) for a in range(2)])
        css = [cb[:, h:h + 1] for h in range(nh)]

        def tile(i, masked):
            q0 = pl.multiple_of(i * t, t)
            sts, dpts, qms, doms = [], [], [], []
            for h in range(nh):
                p, a = divmod(h, 2)
                qi = q_ref[pl.ds(q0, t), p * LANES:(p + 1) * LANES] * jnp.asarray(QK_SCALE, BF16)
                doi = do_ref[pl.ds(q0, t), p * LANES:(p + 1) * LANES]
                qm = jnp.where(sels[a], qi, jnp.zeros_like(qi))
                dom = jnp.where(sels[a], doi, jnp.zeros_like(doi))
                qms.append(qm)
                doms.append(dom)
                sts.append(_dot_nt(k2s[p], qm))
                dpts.append(_dot_nt(v2s[p], dom))
            pts, dsts = [], []
            for h in range(nh):
                cref = ct_ref[h, :, pl.ds(q0, LANES)][:, 0:1]
                pt = jnp.exp(sts[h] - (css[h] - cref) - lse_ref[h, :, pl.ds(q0, t)])
                if masked:
                    pt = jnp.where(causal, pt, 0.0)
                ds32 = pt * (dpts[h] - dl_ref[h, :, pl.ds(q0, t)])
                part = ds32[:, 0:LANES]
                for c in range(1, t // LANES):
                    part = part + ds32[:, c * LANES:(c + 1) * LANES]
                accd_ref[h] += part
                dcq_ref[h, :, pl.ds(q0, t)] += jnp.sum(ds32, axis=0, keepdims=True)
                pts.append(pt.astype(BF16))
                dsts.append(ds32.astype(BF16))
            for p in range(npair):
                ha, hb = 2 * p, 2 * p + 1
                accv_ref[p] += _dot(pts[ha], doms[ha]) + _dot(pts[hb], doms[hb])
                acck_ref[p] += _dot(dsts[ha], qms[ha]) + _dot(dsts[hb], qms[hb])
                dqt_ref[p * LANES:(p + 1) * LANES, pl.ds(q0, t)] += (
                    _dot(kts[p][0], dsts[ha]) + _dot(kts[p][1], dsts[hb]))

        tile(kj, True)

        def body(i, c):
            tile(i, False)
            return c
        lax.fori_loop(kj + 1, nq, body, 0)

        dc = jnp.zeros((t, LANES), F32)
        for h in range(nh):
            dc = jnp.where(lane == h, -jnp.sum(accd_ref[h], axis=1, keepdims=True), dc)
        dc_ref[...] = dc
        for p in range(npair):
            dv_ref[:, p * LANES:(p + 1) * LANES] = accv_ref[p].astype(BF16)
            dk_ref[:, p * LANES:(p + 1) * LANES] = acck_ref[p].astype(BF16)

    whole = lambda kj: (0, 0, 0)
    return pl.pallas_call(
        kern, name="fox_bwd",
        grid=(nq,),
        in_specs=[_resident((s, FOX_W), lambda kj: (0, 0)),
                  _resident((s, FOX_W), lambda kj: (0, 0)),
                  pl.BlockSpec((t, FOX_W), lambda kj: (kj, COL_FK // FOX_W)),
                  pl.BlockSpec((t, FOX_W), lambda kj: (kj, COL_FV // FOX_W)),
                  _resident((nh, 1, s), whole),
                  pl.BlockSpec((t, LANES), lambda kj: (kj, 0)),
                  _resident((nh, 1, s), whole),
                  _resident((nh, 1, s), whole)],
        out_specs=[_resident((FOX_W, s), lambda kj: (0, 0)),
                   pl.BlockSpec((t, FOX_W), lambda kj: (kj, 0)),
                   pl.BlockSpec((t, FOX_W), lambda kj: (kj, 0)),
                   pl.BlockSpec((t, LANES), lambda kj: (kj, 0)),
                   _resident((nh, 1, s), whole)],
        out_shape=[jax.ShapeDtypeStruct((FOX_W, s), F32),
                   jax.ShapeDtypeStruct((s, FOX_W), BF16),
                   jax.ShapeDtypeStruct((s, FOX_W), BF16),
                   jax.ShapeDtypeStruct((s, LANES), F32),
                   jax.ShapeDtypeStruct((nh, 1, s), F32)],
        scratch_shapes=[pltpu.VMEM((npair, t, LANES), F32),
                        pltpu.VMEM((npair, t, LANES), F32),
                        pltpu.VMEM((nh, t, LANES), F32)],
        compiler_params=_cparams(("arbitrary",)),
    )(qkv, do_bf, qkv, qkv, cum_t3, cum, lse_t3, delta_t3)


def _bucket_table():
    qi = np.arange(BLOCK)[:, None]
    kj = np.arange(2 * BLOCK)[None, :]
    rel = np.maximum(qi + BLOCK - kj, 0).astype(np.int32)
    max_exact = NUM_BUCKETS // 2
    relf = np.maximum(rel, 1).astype(np.float32)
    large = max_exact + (np.log(relf / np.float32(max_exact)) / np.float32(math.log(MAX_DISTANCE / max_exact))
                         * np.float32(NUM_BUCKETS - max_exact)).astype(np.int32)
    large = np.minimum(large, NUM_BUCKETS - 1)
    return np.where(rel < max_exact, rel, large).astype(np.int32)


def _swa_bias(rel_bias, bucket):
    def kern(rb_ref, bk_ref, o_ref):
        bk = bk_ref[...]
        for h in range(SWA_HEADS):
            acc = jnp.zeros((BLOCK, 2 * BLOCK), F32)
            for b in range(NUM_BUCKETS):
                acc = jnp.where(bk == b, rb_ref[b, h], acc)
            o_ref[h] = acc

    return pl.pallas_call(
        kern, name="swa_bias",
        in_specs=[pl.BlockSpec(memory_space=pltpu.SMEM),
                  pl.BlockSpec(memory_space=pltpu.VMEM)],
        out_specs=pl.BlockSpec(memory_space=pltpu.VMEM),
        out_shape=jax.ShapeDtypeStruct((SWA_HEADS, BLOCK, 2 * BLOCK), F32),
        compiler_params=_cparams(),
    )(rel_bias, bucket)


def _swa_mask(n):
    qi = lax.broadcasted_iota(jnp.int32, (BLOCK, 2 * BLOCK), 0)
    kj = lax.broadcasted_iota(jnp.int32, (BLOCK, 2 * BLOCK), 1)
    rel = qi + BLOCK - kj
    band = (rel >= 0) & (rel < BLOCK)
    return band & ((kj >= BLOCK) | (n > 0))


def _swa_fwd(qkv, bias, sink):
    s = qkv.shape[0]
    nb = s // BLOCK

    def kern(q_ref, kp_ref, kc_ref, vp_ref, vc_ref, bias_ref, sink_ref, o_ref, lse_ref):
        n = pl.program_id(0)
        mask = _swa_mask(n)
        lane = lax.broadcasted_iota(jnp.int32, (1, LANES), 1)
        q = q_ref[...] * jnp.asarray(QK_SCALE, BF16)
        k = jnp.concatenate([kp_ref[...], kc_ref[...]], axis=0)
        v = jnp.concatenate([vp_ref[...], vc_ref[...]], axis=0)
        kgs = [k[:, g * HEAD_DIM:(g + 1) * HEAD_DIM] for g in range(SWA_KV_HEADS)]
        vgs = [v[:, g * HEAD_DIM:(g + 1) * HEAD_DIM] for g in range(SWA_KV_HEADS)]
        raw = [_dot_nt(q[:, h * HEAD_DIM:(h + 1) * HEAD_DIM], kgs[h // SWA_GROUP]) for h in range(SWA_HEADS)]
        probs = []
        lse_all = jnp.zeros((BLOCK, LANES), F32)
        for h in range(SWA_HEADS):
            sc = jnp.where(mask, raw[h] + bias_ref[h], NEG)
            sk = sink_ref[0, h]
            m = jnp.maximum(jnp.max(sc, axis=1, keepdims=True), sk)
            p = jnp.exp(sc - m)
            l = jnp.sum(p, axis=1, keepdims=True) + jnp.exp(sk - m)
            probs.append((p * (1.0 / l)).astype(BF16))
            lse_all = jnp.where(lane == h, m + jnp.log(l), lse_all)
        outs = [_dot(probs[h], vgs[h // SWA_GROUP]) for h in range(SWA_HEADS)]
        o_ref[...] = jnp.concatenate(outs, axis=1)
        lse_ref[...] = lse_all

    cq, ck, cv = COL_SQ // SWA_W, COL_SK // LANES, COL_SV // LANES
    prev = lambda n: jnp.maximum(n - 1, 0)
    return pl.pallas_call(
        kern, name="swa_fwd",
        grid=(nb,),
        in_specs=[pl.BlockSpec((BLOCK, SWA_W), lambda n: (n, cq)),
                  pl.BlockSpec((BLOCK, LANES), lambda n: (prev(n), ck)),
                  pl.BlockSpec((BLOCK, LANES), lambda n: (n, ck)),
                  pl.BlockSpec((BLOCK, LANES), lambda n: (prev(n), cv)),
                  pl.BlockSpec((BLOCK, LANES), lambda n: (n, cv)),
                  pl.BlockSpec((SWA_HEADS, BLOCK, 2 * BLOCK), lambda n: (0, 0, 0)),
                  pl.BlockSpec(memory_space=pltpu.SMEM)],
        out_specs=[pl.BlockSpec((BLOCK, SWA_W), lambda n: (n, 0)),
                   pl.BlockSpec((BLOCK, LANES), lambda n: (n, 0))],
        out_shape=[jax.ShapeDtypeStruct((s, SWA_W), F32),
                   jax.ShapeDtypeStruct((s, LANES), F32)],
        compiler_params=_cparams(("parallel",)),
    )(qkv, qkv, qkv, qkv, qkv, bias, sink)


def _swa_bwd(qkv, do_bf, delta_b, lse, bias, sink, bucket):
    s = qkv.shape[0]
    nb = s // BLOCK

    def kern(q_ref, kp_ref, kc_ref, vp_ref, vc_ref, do_ref, dl_ref, lse_ref, bias_ref, sink_ref, bk_ref,
             dq_ref, dk_ref, dv_ref, grb_ref, gsk_ref, dbias_ref, ck_ref, cv_ref, sk_ref):
        n = pl.program_id(0)
        lane = lax.broadcasted_iota(jnp.int32, (1, LANES), 1)

        @pl.when(n == 0)
        def _():
            dbias_ref[...] = jnp.zeros_like(dbias_ref)
            ck_ref[...] = jnp.zeros_like(ck_ref)
            cv_ref[...] = jnp.zeros_like(cv_ref)
            sk_ref[...] = jnp.zeros_like(sk_ref)

        @pl.when(n < nb)
        def _():
            mask = _swa_mask(n)
            q = q_ref[...] * jnp.asarray(QK_SCALE, BF16)
            k = jnp.concatenate([kp_ref[...], kc_ref[...]], axis=0)
            v = jnp.concatenate([vp_ref[...], vc_ref[...]], axis=0)
            do = do_ref[...]
            dl = dl_ref[...]
            lse_all = lse_ref[...]
            dks = [None] * SWA_KV_HEADS
            dvs = [None] * SWA_KV_HEADS
            gsk = jnp.zeros((1, LANES), F32)
            kgs = [k[:, g * HEAD_DIM:(g + 1) * HEAD_DIM] for g in range(SWA_KV_HEADS)]
            vgs = [v[:, g * HEAD_DIM:(g + 1) * HEAD_DIM] for g in range(SWA_KV_HEADS)]
            qhs = [q[:, h * HEAD_DIM:(h + 1) * HEAD_DIM] for h in range(SWA_HEADS)]
            dohs = [do[:, h * HEAD_DIM:(h + 1) * HEAD_DIM] for h in range(SWA_HEADS)]
            raw = [_dot_nt(qhs[h], kgs[h // SWA_GROUP]) for h in range(SWA_HEADS)]
            dps = [_dot_nt(dohs[h], vgs[h // SWA_GROUP]) for h in range(SWA_HEADS)]
            ps, dss = [], []
            for h in range(SWA_HEADS):
                lse_h = lse_all[:, h:h + 1]
                dlt = dl[:, h * HEAD_DIM:h * HEAD_DIM + 1]
                sc = jnp.where(mask, raw[h] + bias_ref[h], NEG)
                p = jnp.exp(sc - lse_h)
                ds = p * (dps[h] - dlt)
                dbias_ref[h] += ds
                p_sink = jnp.exp(sink_ref[0, h] - lse_h)
                gsk = gsk + jnp.where(lane == h, -jnp.sum(p_sink * dlt), 0.0)
                ps.append(p.astype(BF16))
                dss.append(ds.astype(BF16))
            dqs = [_dot(dss[h], kgs[h // SWA_GROUP]) * QK_SCALE for h in range(SWA_HEADS)]
            for h in range(SWA_HEADS):
                g = h // SWA_GROUP
                dk_h = _dot_tn(dss[h], qhs[h])
                dv_h = _dot_tn(ps[h], dohs[h])
                dks[g] = dk_h if dks[g] is None else dks[g] + dk_h
                dvs[g] = dv_h if dvs[g] is None else dvs[g] + dv_h
            dq_ref[...] = jnp.concatenate(dqs, axis=1).astype(BF16)
            sk_ref[...] += jnp.broadcast_to(gsk, sk_ref.shape)
            dk2 = jnp.concatenate(dks, axis=1)
            dv2 = jnp.concatenate(dvs, axis=1)
            dk_ref[...] = (ck_ref[...] + dk2[:BLOCK]).astype(BF16)
            dv_ref[...] = (cv_ref[...] + dv2[:BLOCK]).astype(BF16)
            ck_ref[...] = dk2[BLOCK:]
            cv_ref[...] = dv2[BLOCK:]

        @pl.when(n == nb)
        def _():
            dk_ref[...] = ck_ref[...].astype(BF16)
            dv_ref[...] = cv_ref[...].astype(BF16)
            gsk_ref[...] = sk_ref[...]
            bk = bk_ref[...]
            rowi = lax.broadcasted_iota(jnp.int32, (NUM_BUCKETS, LANES), 0)
            lanei = lax.broadcasted_iota(jnp.int32, (NUM_BUCKETS, LANES), 1)
            out = jnp.zeros((NUM_BUCKETS, LANES), F32)
            for h in range(SWA_HEADS):
                db = dbias_ref[h]
                for b in range(NUM_BUCKETS):
                    val = jnp.sum(jnp.where(bk == b, db, 0.0))
                    out = jnp.where((rowi == b) & (lanei == h), val, out)
            grb_ref[...] = out

    cq, ck, cv = COL_SQ // SWA_W, COL_SK // LANES, COL_SV // LANES
    cur = lambda n: jnp.minimum(n, nb - 1)
    prev = lambda n: jnp.maximum(jnp.minimum(n, nb - 1) - 1, 0)
    kout = lambda n: jnp.maximum(n - 1, 0)
    return pl.pallas_call(
        kern, name="swa_bwd",
        grid=(nb + 1,),
        in_specs=[pl.BlockSpec((BLOCK, SWA_W), lambda n: (cur(n), cq)),
                  pl.BlockSpec((BLOCK, LANES), lambda n: (prev(n), ck)),
                  pl.BlockSpec((BLOCK, LANES), lambda n: (cur(n), ck)),
                  pl.BlockSpec((BLOCK, LANES), lambda n: (prev(n), cv)),
                  pl.BlockSpec((BLOCK, LANES), lambda n: (cur(n), cv)),
                  pl.BlockSpec((BLOCK, SWA_W), lambda n: (cur(n), 1)),
                  pl.BlockSpec((BLOCK, SWA_W), lambda n: (cur(n), 1)),
                  pl.BlockSpec((BLOCK, LANES), lambda n: (cur(n), 0)),
                  pl.BlockSpec((SWA_HEADS, BLOCK, 2 * BLOCK), lambda n: (0, 0, 0)),
                  pl.BlockSpec(memory_space=pltpu.SMEM),
                  pl.BlockSpec((BLOCK, 2 * BLOCK), lambda n: (0, 0))],
        out_specs=[pl.BlockSpec((BLOCK, SWA_W), lambda n: (cur(n), 0)),
                   pl.BlockSpec((BLOCK, LANES), lambda n: (kout(n), 0)),
                   pl.BlockSpec((BLOCK, LANES), lambda n: (kout(n), 0)),
                   pl.BlockSpec((NUM_BUCKETS, LANES), lambda n: (0, 0)),
                   pl.BlockSpec((8, LANES), lambda n: (0, 0))],
        out_shape=[jax.ShapeDtypeStruct((s, SWA_W), BF16),
                   jax.ShapeDtypeStruct((s, LANES), BF16),
                   jax.ShapeDtypeStruct((s, LANES), BF16),
                   jax.ShapeDtypeStruct((NUM_BUCKETS, LANES), F32),
                   jax.ShapeDtypeStruct((8, LANES), F32)],
        scratch_shapes=[pltpu.VMEM((SWA_HEADS, BLOCK, 2 * BLOCK), F32),
                        pltpu.VMEM((BLOCK, LANES), F32),
                        pltpu.VMEM((BLOCK, LANES), F32),
                        pltpu.VMEM((8, LANES), F32)],
        compiler_params=_cparams(("arbitrary",)),
    )(qkv, qkv, qkv, qkv, qkv, do_bf, delta_b, lse, bias, sink, bucket)


def _post(x, target, o_fox, o_swa, z, w_o, ln_g, ln_b):
    s = x.shape[0]
    tm = min(256, s)
    nt = s // tm
    seg = 256

    def kern(x_ref, t_ref, of_ref, os_ref, z_ref, w_ref, g_ref, b_ref,
             loss_ref, dh_ref, dy_ref, mix_ref, do_ref, dz_ref, dl_ref, gg_ref, gb_ref, lacc_ref):
        step = pl.program_id(0)

        @pl.when(step == 0)
        def _():
            lacc_ref[...] = jnp.zeros_like(lacc_ref)
            gg_ref[...] = jnp.zeros_like(gg_ref)
            gb_ref[...] = jnp.zeros_like(gb_ref)

        o = jnp.concatenate([of_ref[...], os_ref[...]], axis=1)
        zz = z_ref[...]
        sig = 1.0 / (1.0 + jnp.exp(-zz))
        silu = zz * sig
        mixed = (o * silu).astype(BF16)
        mix_ref[...] = mixed
        w = w_ref[...]
        h = ALPHA * x_ref[...] + _dot(mixed, w)
        mu = jnp.mean(h, axis=1, keepdims=True)
        hc = h - mu
        var = jnp.mean(hc * hc, axis=1, keepdims=True)
        rstd = lax.rsqrt(var + LN_EPS)
        xhat = hc * rstd
        g = g_ref[...]
        err = xhat * g + b_ref[...] - t_ref[...]
        lacc_ref[...] += jnp.broadcast_to(jnp.sum(err * err, axis=0, keepdims=True), lacc_ref.shape)
        dout = err * (1.0 / D_MODEL)
        gg_ref[...] += jnp.broadcast_to(jnp.sum(dout * xhat, axis=0, keepdims=True), gg_ref.shape)
        gb_ref[...] += jnp.broadcast_to(jnp.sum(dout, axis=0, keepdims=True), gb_ref.shape)
        dxh = dout * g
        m1 = jnp.mean(dxh, axis=1, keepdims=True)
        m2 = jnp.mean(dxh * xhat, axis=1, keepdims=True)
        dh = rstd * (dxh - m1 - xhat * m2)
        dh_ref[...] = dh
        dy = dh.astype(BF16)
        dy_ref[...] = dy
        dmix = _dot_nt(dy, w)
        do = dmix * silu
        do_ref[...] = do.astype(BF16)
        dz_ref[...] = (dmix * o * (sig * (1.0 + zz * (1.0 - sig)))).astype(BF16)
        r = lax.broadcasted_iota(jnp.int32, (seg, seg), 0) // HEAD_DIM
        c = lax.broadcasted_iota(jnp.int32, (seg, seg), 1) // HEAD_DIM
        bd = jnp.where(r == c, 1.0, 0.0).astype(BF16)
        prod = do * o
        parts = [_exact_dot(bd, prod[:, j * seg:(j + 1) * seg], False) for j in range(D_MODEL // seg)]
        dl_ref[...] = jnp.concatenate(parts, axis=1)

        @pl.when(step == nt - 1)
        def _():
            tot = jnp.sum(lacc_ref[0:1, :]) * (0.5 / D_MODEL)
            loss_ref[...] = jnp.broadcast_to(tot, loss_ref.shape)

    row = lambda i: (i, 0)
    fixed = lambda i: (0, 0)
    wide = pl.BlockSpec((tm, D_MODEL), row)
    half = pl.BlockSpec((tm, FOX_W), row)
    return pl.pallas_call(
        kern, name="post",
        grid=(nt,),
        in_specs=[wide, wide, half, half, wide,
                  pl.BlockSpec((D_MODEL, D_MODEL), fixed),
                  pl.BlockSpec((1, D_MODEL), fixed),
                  pl.BlockSpec((1, D_MODEL), fixed)],
        out_specs=[pl.BlockSpec((8, LANES), fixed), wide, wide, wide, wide, wide, wide,
                   pl.BlockSpec((8, D_MODEL), fixed), pl.BlockSpec((8, D_MODEL), fixed)],
        out_shape=[jax.ShapeDtypeStruct((8, LANES), F32),
                   jax.ShapeDtypeStruct((s, D_MODEL), F32),
                   jax.ShapeDtypeStruct((s, D_MODEL), BF16),
                   jax.ShapeDtypeStruct((s, D_MODEL), BF16),
                   jax.ShapeDtypeStruct((s, D_MODEL), BF16),
                   jax.ShapeDtypeStruct((s, D_MODEL), BF16),
                   jax.ShapeDtypeStruct((s, D_MODEL), F32),
                   jax.ShapeDtypeStruct((8, D_MODEL), F32),
                   jax.ShapeDtypeStruct((8, D_MODEL), F32)],
        scratch_shapes=[pltpu.VMEM((8, D_MODEL), F32)],
        compiler_params=_cparams(("arbitrary",)),
    )(x, target, o_fox, o_swa, z, w_o, ln_g, ln_b)


def _adamw_math(w, g, m, v):
    m = ADAM_B1 * m + (1.0 - ADAM_B1) * g
    v = ADAM_B2 * v + (1.0 - ADAM_B2) * (g * g)
    m_hat = m / (1.0 - ADAM_B1 ** ADAM_STEP)
    v_hat = v / (1.0 - ADAM_B2 ** ADAM_STEP)
    delta = -ADAM_LR * (m_hat / (jnp.sqrt(v_hat) + ADAM_EPS) + ADAM_WD * w)
    return delta, m, v


def _adamw(w, g, m, v, *, name):
    r, c = w.shape
    tr = min(256, r)

    def kern(w_ref, g_ref, m_ref, v_ref, d_ref, mo_ref, vo_ref):
        d, mn, vn = _adamw_math(w_ref[...], g_ref[...], m_ref[...], v_ref[...])
        d_ref[...] = d
        mo_ref[...] = mn
        vo_ref[...] = vn

    blk = pl.BlockSpec((tr, c), lambda i: (i, 0))
    sds = jax.ShapeDtypeStruct((r, c), F32)
    return pl.pallas_call(
        kern, name=name,
        grid=(r // tr,),
        in_specs=[blk, blk, blk, blk],
        out_specs=[blk, blk, blk],
        out_shape=[sds, sds, sds],
        compiler_params=_cparams(("parallel",)),
    )(w, g, m, v)


def _position():
    x, y, c = lax.axis_index("x"), lax.axis_index("y"), lax.axis_index("c")
    chips = [(1 - x, y), (x, 1 - y), (1 - x, 1 - y)]
    return x, y, c, chips


def _chip_index(cx, cy):
    return 2 * cx + cy


def _gather_weights(w_in_bf, w_o_bf):
    shards = (w_in_bf, w_o_bf)
    n_arr = len(shards)

    def kern(*refs):
        ins, outs = refs[:n_arr], refs[n_arr:2 * n_arr]
        send_sems, recv_sems, local_sems = refs[2 * n_arr:]
        x, y, c, chips = _position()
        me = _chip_index(x, y)
        sibling = (x, y, 1 - c)

        local = [pltpu.make_async_copy(ins[a], outs[a].at[me], local_sems.at[a]) for a in range(n_arr)]
        for cp in local:
            cp.start()

        def half(ref, a):
            rows = shards[a].shape[0] // 2
            return ref.at[pl.ds(c * rows, rows), :]

        def copy(a, k, src, slot, to):
            return pltpu.make_async_remote_copy(
                src_ref=src, dst_ref=half(outs[a].at[slot], a),
                send_sem=send_sems.at[a * 6 + k], recv_sem=recv_sems.at[a * 6 + k],
                device_id=to, device_id_type=MESH)

        first = [copy(a, j, half(ins[a], a), me, (*chip, c)) for a in range(n_arr) for j, chip in enumerate(chips)]
        for cp in first:
            cp.start()
        passed = []
        for a in range(n_arr):
            for j, chip in enumerate(chips):
                slot = _chip_index(*chip)
                copy(a, j, half(ins[a], a), slot, (*chip, c)).wait_recv()
                fwd = copy(a, 3 + j, half(outs[a].at[slot], a), slot, sibling)
                fwd.start()
                passed.append(fwd)
        for a in range(n_arr):
            for j, chip in enumerate(chips):
                slot = _chip_index(*chip)
                rows = shards[a].shape[0] // 2
                dst = outs[a].at[slot].at[pl.ds((1 - c) * rows, rows), :]
                pltpu.make_async_remote_copy(
                    src_ref=dst, dst_ref=dst, send_sem=send_sems.at[a * 6 + 3 + j],
                    recv_sem=recv_sems.at[a * 6 + 3 + j], device_id=sibling, device_id_type=MESH).wait_recv()
        for cp in first + passed:
            cp.wait_send()
        for cp in local:
            cp.wait()

    vmem = pl.BlockSpec(memory_space=pltpu.VMEM)
    return pl.pallas_call(
        kern, name="gather_weights",
        in_specs=[vmem] * n_arr,
        out_specs=[vmem] * n_arr,
        out_shape=[jax.ShapeDtypeStruct((N_CHIPS,) + w.shape, w.dtype) for w in shards],
        scratch_shapes=[pltpu.SemaphoreType.DMA((6 * n_arr,)),
                        pltpu.SemaphoreType.DMA((6 * n_arr,)),
                        pltpu.SemaphoreType.DMA((n_arr,))],
        compiler_params=_cparams(),
    )(*shards)


def _swap_halves(grads):
    n_arr = len(grads)

    def kern(*refs):
        ins = refs[:n_arr]
        owns = refs[n_arr:2 * n_arr]
        gots = refs[2 * n_arr:3 * n_arr]
        send_sems, recv_sems, local_sems = refs[3 * n_arr:]
        x, y, c, _ = _position()
        sibling = (x, y, 1 - c)
        local, remote = [], []
        for a in range(n_arr):
            rows = grads[a].shape[1] // 2
            piece = rows // COPY_PIECES
            for j in range(N_CHIPS):
                for r in range(COPY_PIECES):
                    k = (a * N_CHIPS + j) * COPY_PIECES + r
                    dst_rows = pl.ds(r * piece, piece)
                    local.append(pltpu.make_async_copy(
                        ins[a].at[j, pl.ds(c * rows + r * piece, piece), :],
                        owns[a].at[j, dst_rows, :], local_sems.at[k]))
                    remote.append(pltpu.make_async_remote_copy(
                        src_ref=ins[a].at[j, pl.ds((1 - c) * rows + r * piece, piece), :],
                        dst_ref=gots[a].at[j, dst_rows, :], send_sem=send_sems.at[k], recv_sem=recv_sems.at[k],
                        device_id=sibling, device_id_type=MESH))
        for cp in local + remote:
            cp.start()
        for cp in remote:
            cp.wait()
        for cp in local:
            cp.wait()

    hbm = pl.BlockSpec(memory_space=pltpu.VMEM)
    half = [jax.ShapeDtypeStruct((N_CHIPS, g.shape[1] // 2, g.shape[2]), F32) for g in grads]
    outs = pl.pallas_call(
        kern, name="swap_halves",
        in_specs=[hbm] * n_arr,
        out_specs=[hbm] * (2 * n_arr),
        out_shape=half + half,
        scratch_shapes=[pltpu.SemaphoreType.DMA((n_arr * N_CHIPS * COPY_PIECES,)),
                        pltpu.SemaphoreType.DMA((n_arr * N_CHIPS * COPY_PIECES,)),
                        pltpu.SemaphoreType.DMA((n_arr * N_CHIPS * COPY_PIECES,))],
        compiler_params=_cparams(),
    )(*grads)
    return outs[:n_arr], outs[n_arr:]


def _scatter_to_owners(parts):
    n_arr = len(parts)

    def kern(*refs):
        ins = refs[:n_arr]
        outs = refs[n_arr:2 * n_arr]
        send_sems, recv_sems, local_sems = refs[2 * n_arr:]
        x, y, c, chips = _position()
        me = _chip_index(x, y)
        local = [pltpu.make_async_copy(ins[a].at[me], outs[a].at[me], local_sems.at[a]) for a in range(n_arr)]
        for cp in local:
            cp.start()
        sends = []
        for a in range(n_arr):
            for j, chip in enumerate(chips):
                sends.append(pltpu.make_async_remote_copy(
                    src_ref=ins[a].at[_chip_index(*chip)], dst_ref=outs[a].at[me],
                    send_sem=send_sems.at[a * 3 + j], recv_sem=recv_sems.at[a * 3 + j],
                    device_id=(*chip, c), device_id_type=MESH))
        for cp in sends:
            cp.start()
        for a in range(n_arr):
            for j, chip in enumerate(chips):
                slot = outs[a].at[_chip_index(*chip)]
                pltpu.make_async_remote_copy(
                    src_ref=slot, dst_ref=slot, send_sem=send_sems.at[a * 3 + j],
                    recv_sem=recv_sems.at[a * 3 + j], device_id=(*chip, c), device_id_type=MESH).wait_recv()
        for cp in sends:
            cp.wait_send()
        for cp in local:
            cp.wait()

    hbm = pl.BlockSpec(memory_space=pltpu.VMEM)
    return pl.pallas_call(
        kern, name="scatter_to_owners",
        in_specs=[hbm] * n_arr,
        out_specs=[hbm] * n_arr,
        out_shape=[jax.ShapeDtypeStruct(p.shape, F32) for p in parts],
        scratch_shapes=[pltpu.SemaphoreType.DMA((3 * n_arr,)),
                        pltpu.SemaphoreType.DMA((3 * n_arr,)),
                        pltpu.SemaphoreType.DMA((n_arr,))],
        compiler_params=_cparams(),
    )(*parts)


def _join_halves(halves):
    n_arr = len(halves)

    def kern(*refs):
        ins = refs[:n_arr]
        outs = refs[n_arr:2 * n_arr]
        send_sems, recv_sems, local_sems = refs[2 * n_arr:]
        x, y, c, _ = _position()
        sibling = (x, y, 1 - c)
        local, remote = [], []
        for a in range(n_arr):
            rows = halves[a].shape[0]
            piece = rows // COPY_PIECES
            for r in range(COPY_PIECES):
                k = a * COPY_PIECES + r
                src = ins[a].at[pl.ds(r * piece, piece), :]
                dst = outs[a].at[pl.ds(c * rows + r * piece, piece), :]
                local.append(pltpu.make_async_copy(src, dst, local_sems.at[k]))
                remote.append(pltpu.make_async_remote_copy(
                    src_ref=src, dst_ref=dst, send_sem=send_sems.at[k], recv_sem=recv_sems.at[k],
                    device_id=sibling, device_id_type=MESH))
        for cp in local + remote:
            cp.start()
        for a in range(n_arr):
            rows = halves[a].shape[0]
            piece = rows // COPY_PIECES
            for r in range(COPY_PIECES):
                k = a * COPY_PIECES + r
                theirs = outs[a].at[pl.ds((1 - c) * rows + r * piece, piece), :]
                pltpu.make_async_remote_copy(
                    src_ref=theirs, dst_ref=theirs, send_sem=send_sems.at[k], recv_sem=recv_sems.at[k],
                    device_id=sibling, device_id_type=MESH).wait_recv()
        for cp in remote:
            cp.wait_send()
        for cp in local:
            cp.wait()

    hbm = pl.BlockSpec(memory_space=pltpu.VMEM)
    return pl.pallas_call(
        kern, name="join_halves",
        in_specs=[hbm] * n_arr,
        out_specs=[hbm] * n_arr,
        out_shape=[jax.ShapeDtypeStruct((2 * h.shape[0], h.shape[1]), F32) for h in halves],
        scratch_shapes=[pltpu.SemaphoreType.DMA((n_arr * COPY_PIECES,)),
                        pltpu.SemaphoreType.DMA((n_arr * COPY_PIECES,)),
                        pltpu.SemaphoreType.DMA((n_arr * COPY_PIECES,))],
        compiler_params=_cparams(),
    )(*halves)


def _add2(a, b, *, name):
    n, r, c = a.shape
    tr = min(256, r)

    def kern(a_ref, b_ref, o_ref):
        o_ref[...] = a_ref[...] + b_ref[...]

    blk = pl.BlockSpec((1, tr, c), lambda j, i: (j, i, 0))
    return pl.pallas_call(
        kern, name=name,
        grid=(n, r // tr),
        in_specs=[blk, blk],
        out_specs=blk,
        out_shape=jax.ShapeDtypeStruct(a.shape, F32),
        compiler_params=_cparams(("parallel", "parallel")),
    )(a, b)


def _sum4(a, *, name):
    n, r, c = a.shape
    tr = min(256, r)

    def kern(a_ref, o_ref):
        o_ref[...] = ((a_ref[0] + a_ref[1]) + a_ref[2]) + a_ref[3]

    return pl.pallas_call(
        kern, name=name,
        grid=(r // tr,),
        in_specs=[pl.BlockSpec((n, tr, c), lambda i: (0, i, 0))],
        out_specs=pl.BlockSpec((tr, c), lambda i: (i, 0)),
        out_shape=jax.ShapeDtypeStruct((r, c), F32),
        compiler_params=_cparams(("parallel",)),
    )(a)


def _small_allreduce_adamw(g, w, m, v):
    def kern(g_ref, w_ref, m_ref, v_ref, gs_ref, d_ref, mo_ref, vo_ref, buf_ref, send_sems, recv_sems):
        x, y, c, _ = _position()
        me = 4 * x + 2 * y + c
        buf_ref[me] = g_ref[...]
        peers = [(x, y, 1 - c)] + [(px, py, pc) for px, py in _position()[3] for pc in (c, 1 - c)]
        sends = []
        for k, peer in enumerate(peers):
            sends.append(pltpu.make_async_remote_copy(
                src_ref=g_ref, dst_ref=buf_ref.at[me], send_sem=send_sems.at[k], recv_sem=recv_sems.at[k],
                device_id=peer, device_id_type=MESH))
        for cp in sends:
            cp.start()
        for k, (px, py, pc) in enumerate(peers):
            slot = buf_ref.at[4 * px + 2 * py + pc]
            pltpu.make_async_remote_copy(
                src_ref=slot, dst_ref=slot, send_sem=send_sems.at[k], recv_sem=recv_sems.at[k],
                device_id=(px, py, pc), device_id_type=MESH).wait_recv()
        for cp in sends:
            cp.wait_send()
        tot = buf_ref[0]
        for d in range(1, N_DEV):
            tot = tot + buf_ref[d]
        gs_ref[...] = tot
        delta, mn, vn = _adamw_math(w_ref[...], tot, m_ref[...], v_ref[...])
        d_ref[...] = delta
        mo_ref[...] = mn
        vo_ref[...] = vn

    vm = pl.BlockSpec(memory_space=pltpu.VMEM)
    sds = jax.ShapeDtypeStruct((SMALL_ROWS, LANES), F32)
    return pl.pallas_call(
        kern, name="small_allreduce_adamw",
        in_specs=[vm] * 4,
        out_specs=[vm] * 4,
        out_shape=[sds] * 4,
        scratch_shapes=[pltpu.VMEM((N_DEV, SMALL_ROWS, LANES), F32),
                        pltpu.SemaphoreType.DMA((N_DEV - 1,)),
                        pltpu.SemaphoreType.DMA((N_DEV - 1,))],
    )(g, w, m, v)


def _to_padded_cols(w):
    pad = jnp.zeros((w.shape[0], N_C - FOX_HEADS), w.dtype)
    return jnp.concatenate([w[:, 0:1536], w[:, 2056:2824], w[:, 1536:1544], pad,
                            w[:, 1544:2056], w[:, 2824:3336]], axis=1)


def _from_padded_cols(g):
    return jnp.concatenate([g[:, 0:1536], g[:, OFF_C:OFF_C + FOX_HEADS], g[:, OFF_B:OFF_B + FOX_W],
                            g[:, 1536:N_A], g[:, OFF_B + FOX_W:N_PAD]], axis=1)


def _pack_small(b_f, rel_bias, sink, ln_g, ln_b):
    row = lambda v: jnp.pad(v.reshape(1, -1), ((0, 0), (0, LANES - v.size)))
    return jnp.concatenate([ln_g.reshape(8, LANES), ln_b.reshape(8, LANES), rel_bias.reshape(2, LANES),
                            row(b_f), row(sink), jnp.zeros((4, LANES), F32)], axis=0)


def _unpack_small(p):
    ln_g = p[0:8].reshape(1, D_MODEL)
    ln_b = p[8:16].reshape(1, D_MODEL)
    rel_bias = p[16:18].reshape(NUM_BUCKETS, SWA_HEADS)
    b_f = p[18:19, :FOX_HEADS]
    sink = p[19:20, :SWA_HEADS]
    return b_f, rel_bias, sink, ln_g, ln_b


def _heads_to_rows(a_b):
    return a_b[:, ::HEAD_DIM].T.reshape(FOX_HEADS, 1, a_b.shape[0])


def kernel(x, w_in, b_f, rel_bias, sink, w_o, ln_g, ln_b, loss_target, m_w_in, m_b_f, m_rel_bias, m_sink, m_w_o, m_ln_g, m_ln_b, v_w_in, v_b_f, v_rel_bias, v_sink, v_w_o, v_ln_g, v_ln_b):
    x2 = x[0]
    tgt = loss_target[0]
    s = x2.shape[0]
    w_in2, w_o2 = w_in[0], w_o[0]

    shard_cols = D_IN // N_CHIPS
    col_pad = ((0, 0), (0, SHARD_PAD - shard_cols))
    w_in_all, w_o_all = _gather_weights(jnp.pad(w_in2.astype(BF16), col_pad), w_o2.astype(BF16))
    w_full = jnp.concatenate([w_in_all[j, :, :shard_cols] for j in range(N_CHIPS)], axis=1)
    w_pad = _to_padded_cols(w_full)
    w_o_full = w_o_all.reshape(D_MODEL, D_MODEL)

    x_bf = x2.astype(BF16)
    qkv = _matmul_nn(x_bf, w_pad, n_off=0, n_out=N_A, tm=512, tn=768, out_dtype=BF16, name="proj_qkv")
    z = _matmul_nn(x_bf, w_pad, n_off=OFF_B, n_out=N_B, tm=512, tn=512, out_dtype=F32, name="proj_gate")
    ffp = _matmul_nn(x_bf, w_pad, n_off=OFF_C, n_out=N_C, tm=512, tn=N_C, out_dtype=F32, name="proj_forget")
    bfp = jnp.pad(b_f, ((0, 0), (0, LANES - FOX_HEADS)))
    cum = _cum_fwd(ffp, bfp)
    cum_t3 = cum[:, :FOX_HEADS].T.reshape(FOX_HEADS, 1, s)
    vt = qkv[:, COL_FV:COL_FV + FOX_W].T
    o_fox, lse_t3 = _fox_fwd(qkv, vt, cum_t3, cum)
    bucket = jnp.asarray(_bucket_table())
    bias = _swa_bias(rel_bias, bucket)
    o_swa, lse_swa = _swa_fwd(qkv, bias, sink)

    loss8, dh, dy, mixed, do_bf, dz, delta_b, gg8, gb8 = _post(
        x2, tgt, o_fox, o_swa, z, w_o_full, ln_g, ln_b)
    loss = lax.psum(loss8[0, 0], ("x", "y", "c"))
    grad_w_o_full = _matmul_acc(mixed.T, dy, tm=1024, tn=512, tk=1024, name="grad_w_o")

    delta_t3 = _heads_to_rows(delta_b[:, :FOX_W])
    dqt_fox, dk_fox, dv_fox, dcum_k, dcum_q = _fox_bwd(qkv, do_bf, cum_t3, cum, lse_t3, delta_t3)
    dcum_q = jnp.pad(dcum_q.reshape(FOX_HEADS, s).T, ((0, 0), (0, LANES - FOX_HEADS)))
    dff, gbf8 = _cum_bwd(dcum_k, dcum_q, ffp, bfp)
    dq_swa, dk_swa, dv_swa, grb, gsk8 = _swa_bwd(qkv, do_bf, delta_b, lse_swa, bias, sink, bucket)

    dproj = jnp.concatenate([dqt_fox.T.astype(BF16), dk_fox, dv_fox, dq_swa, dk_swa, dv_swa, dff, dz], axis=1)
    grad_x = _grad_x_matmul(dproj, w_pad, dh, tm=512, tn=512, name="grad_x")
    grad_w_pad = _matmul_acc(x_bf.T, dproj, tm=1024, tn=512, tk=1024, name="grad_w_in")
    grad_w_in_full = _from_padded_cols(grad_w_pad)

    g_in4 = jnp.stack([jnp.pad(grad_w_in_full[:, j * shard_cols:(j + 1) * shard_cols], col_pad)
                       for j in range(N_CHIPS)])
    g_o4 = grad_w_o_full.reshape(N_CHIPS, D_MODEL // N_CHIPS, D_MODEL)
    owns, gots = _swap_halves([g_in4, g_o4])
    parts = [_add2(owns[0], gots[0], name="pair_sum_w_in"), _add2(owns[1], gots[1], name="pair_sum_w_o")]
    slabs = _scatter_to_owners(parts)
    halves = [_sum4(slabs[0], name="chip_sum_w_in"), _sum4(slabs[1], name="chip_sum_w_o")]
    g_w_in, g_w_o = _join_halves(halves)
    g_w_in = g_w_in[:, :shard_cols]

    d_w_in, nm_w_in, nv_w_in = _adamw(w_in2, g_w_in, m_w_in[0], v_w_in[0], name="adamw_w_in")
    d_w_o, nm_w_o, nv_w_o = _adamw(w_o2, g_w_o, m_w_o[0], v_w_o[0], name="adamw_w_o")

    g_small = _pack_small(gbf8[0:1, :FOX_HEADS], grb[:, :SWA_HEADS], gsk8[0:1, :SWA_HEADS], gg8[0:1], gb8[0:1])
    w_small = _pack_small(b_f, rel_bias, sink, ln_g, ln_b)
    m_small = _pack_small(m_b_f, m_rel_bias, m_sink, m_ln_g, m_ln_b)
    v_small = _pack_small(v_b_f, v_rel_bias, v_sink, v_ln_g, v_ln_b)
    gs, ds, ms, vs = _small_allreduce_adamw(g_small, w_small, m_small, v_small)
    g_bf, g_rb, g_sk, g_lg, g_lb = _unpack_small(gs)
    d_bf, d_rb, d_sk, d_lg, d_lb = _unpack_small(ds)
    m_bf, m_rb, m_sk, m_lg, m_lb = _unpack_small(ms)
    v_bf, v_rb, v_sk, v_lg, v_lb = _unpack_small(vs)

    e = lambda a: a[None]
    return (loss, e(grad_x),
            e(g_w_in), g_bf, g_rb, g_sk, e(g_w_o), g_lg, g_lb,
            e(d_w_in), d_bf, d_rb, d_sk, e(d_w_o), d_lg, d_lb,
            e(nm_w_in), m_bf, m_rb, m_sk, e(nm_w_o), m_lg, m_lb,
            e(nv_w_in), v_bf, v_rb, v_sk, e(nv_w_o), v_lg, v_lb)
```

```python
import functools
import math

import numpy as np
import jax
import jax.numpy as jnp
from jax import lax
from jax.experimental import pallas as pl
from jax.experimental.pallas import tpu as pltpu

F32 = jnp.float32
BF16 = jnp.bfloat16

D_MODEL = 1024
HEAD_DIM = 64
FOX_HEADS = 8
SWA_HEADS = 8
SWA_KV_HEADS = 2
SWA_GROUP = 4
FOX_W = 512
SWA_W = 512
SWA_KV_W = 128
BLOCK = 128
NUM_BUCKETS = 32
MAX_DISTANCE = 128
LN_EPS = 1e-5
NEG = -1e30
ALPHA = 2.0 ** 0.25
QK_SCALE = 0.125

ADAM_LR = 0.001
ADAM_B1 = 0.9
ADAM_B2 = 0.999
ADAM_EPS = 1e-08
ADAM_WD = 0.01
ADAM_STEP = 10

D_IN = 3336
SHARD_PAD = 896
N_A = 2304
N_C = 256
N_B = 1024
OFF_C = N_A
OFF_B = N_A + N_C
N_PAD = N_A + N_C + N_B
COL_FK, COL_FV, COL_SQ, COL_SK, COL_SV = 512, 1024, 1536, 2048, 2176

LANES = 128
FOX_T = 256
VMEM_LIMIT = 56 * 1024 * 1024

MESH = pl.DeviceIdType.MESH
N_CHIPS = 4
N_DEV = 8
SMALL_ROWS = 24
COPY_PIECES = 4


def _cparams(sem=None):
    return pltpu.CompilerParams(dimension_semantics=sem, vmem_limit_bytes=VMEM_LIMIT)


def _split3(x):
    hi = x.astype(BF16)
    r = x - hi.astype(F32)
    mid = r.astype(BF16)
    lo = (r - mid.astype(F32)).astype(BF16)
    return hi, mid, lo


def _dot(a, b):
    return jnp.dot(a, b, preferred_element_type=F32)


def _dot_nt(a, b):
    return lax.dot_general(a, b, (((1,), (1,)), ((), ())), preferred_element_type=F32)


def _dot_tn(a, b):
    return lax.dot_general(a, b, (((0,), (0,)), ((), ())), preferred_element_type=F32)


def _matmul_nn(a, b, *, n_off, n_out, tm, tn, out_dtype, name):
    m, k = a.shape
    joff = n_off // tn

    def kern(a_ref, b_ref, o_ref):
        o_ref[...] = _dot(a_ref[...], b_ref[...]).astype(o_ref.dtype)

    return pl.pallas_call(
        kern, name=name,
        grid=(n_out // tn, m // tm),
        in_specs=[pl.BlockSpec((tm, k), lambda j, i: (i, 0)),
                  pl.BlockSpec((k, tn), lambda j, i: (0, j + joff))],
        out_specs=pl.BlockSpec((tm, tn), lambda j, i: (i, j)),
        out_shape=jax.ShapeDtypeStruct((m, n_out), out_dtype),
        compiler_params=_cparams(("parallel", "parallel")),
    )(a, b)


def _grad_x_matmul(dproj, w_pad, dh, *, tm, tn, name):
    m, k = dproj.shape
    n = w_pad.shape[0]

    def kern(a_ref, b_ref, dh_ref, o_ref):
        o_ref[...] = ALPHA * dh_ref[...] + _dot_nt(a_ref[...], b_ref[...])

    return pl.pallas_call(
        kern, name=name,
        grid=(n // tn, m // tm),
        in_specs=[pl.BlockSpec((tm, k), lambda j, i: (i, 0)),
                  pl.BlockSpec((tn, k), lambda j, i: (j, 0)),
                  pl.BlockSpec((tm, tn), lambda j, i: (i, j))],
        out_specs=pl.BlockSpec((tm, tn), lambda j, i: (i, j)),
        out_shape=jax.ShapeDtypeStruct((m, n), F32),
        compiler_params=_cparams(("parallel", "parallel")),
    )(dproj, w_pad, dh)


def _matmul_acc(at, b, *, tm, tn, tk, name):
    m, s = at.shape
    n = b.shape[1]

    def kern(a_ref, b_ref, o_ref):
        @pl.when(pl.program_id(2) == 0)
        def _():
            o_ref[...] = jnp.zeros_like(o_ref)
        o_ref[...] += _dot(a_ref[...], b_ref[...])

    return pl.pallas_call(
        kern, name=name,
        grid=(m // tm, n // tn, s // tk),
        in_specs=[pl.BlockSpec((tm, tk), lambda i, j, k: (i, k)),
                  pl.BlockSpec((tk, tn), lambda i, j, k: (k, j))],
        out_specs=pl.BlockSpec((tm, tn), lambda i, j, k: (i, j)),
        out_shape=jax.ShapeDtypeStruct((m, n), F32),
        compiler_params=_cparams(("parallel", "parallel", "arbitrary")),
    )(at, b)


def _tri(n, lower):
    r = lax.broadcasted_iota(jnp.int32, (n, n), 0)
    c = lax.broadcasted_iota(jnp.int32, (n, n), 1)
    keep = (c <= r) if lower else (c >= r)
    return jnp.where(keep, 1.0, 0.0).astype(BF16)


def _exact_dot(mat_bf16, x_f32, left):
    out = None
    for piece in _split3(x_f32):
        t = _dot(mat_bf16, piece) if left else _dot(piece, mat_bf16)
        out = t if out is None else out + t
    return out


def _log_sigmoid(z):
    return jnp.minimum(z, 0.0) - jnp.log(1.0 + jnp.exp(-jnp.abs(z)))


def _cum_fwd(ffp, bfp):
    s = ffp.shape[0]
    t = min(256, s)

    def kern(ff_ref, b_ref, cum_ref, carry_ref):
        @pl.when(pl.program_id(0) == 0)
        def _():
            carry_ref[...] = jnp.zeros_like(carry_ref)
        lane = lax.broadcasted_iota(jnp.int32, (1, LANES), 1)
        lf = _log_sigmoid(ff_ref[...] + b_ref[...])
        lf = jnp.where(lane < FOX_HEADS, lf, 0.0)
        cum = _exact_dot(_tri(t, True), lf, True) + carry_ref[0:1, :]
        cum_ref[...] = cum
        carry_ref[...] = jnp.broadcast_to(cum[t - 1:t, :], carry_ref.shape)

    return pl.pallas_call(
        kern, name="cum_fwd",
        grid=(s // t,),
        in_specs=[pl.BlockSpec((t, LANES), lambda i: (i, 0)),
                  pl.BlockSpec((1, LANES), lambda i: (0, 0))],
        out_specs=pl.BlockSpec((t, LANES), lambda i: (i, 0)),
        out_shape=jax.ShapeDtypeStruct((s, LANES), F32),
        scratch_shapes=[pltpu.VMEM((8, LANES), F32)],
        compiler_params=_cparams(("arbitrary",)),
    )(ffp, bfp)


def _cum_bwd(dcum_k, dcum_q, ffp, bfp):
    s = dcum_k.shape[0]
    t = min(256, s)
    nb = s // t

    def kern(dck_ref, dcq_ref, ff_ref, b_ref, dff_ref, gb_ref, carry_ref):
        @pl.when(pl.program_id(0) == 0)
        def _():
            carry_ref[...] = jnp.zeros_like(carry_ref)
            gb_ref[...] = jnp.zeros_like(gb_ref)
        lane = lax.broadcasted_iota(jnp.int32, (1, LANES), 1)
        dlf = _exact_dot(_tri(t, False), dck_ref[...] + dcq_ref[...], True) + carry_ref[0:1, :]
        carry_ref[...] = jnp.broadcast_to(dlf[0:1, :], carry_ref.shape)
        z = ff_ref[...] + b_ref[...]
        dff = jnp.where(lane < FOX_HEADS, dlf / (1.0 + jnp.exp(z)), 0.0)
        gb_ref[...] += jnp.broadcast_to(jnp.sum(dff, axis=0, keepdims=True), gb_ref.shape)
        dff_ref[...] = jnp.concatenate([dff, jnp.zeros_like(dff)], axis=1).astype(BF16)

    return pl.pallas_call(
        kern, name="cum_bwd",
        grid=(nb,),
        in_specs=[pl.BlockSpec((t, LANES), lambda i: (nb - 1 - i, 0)),
                  pl.BlockSpec((t, LANES), lambda i: (nb - 1 - i, 0)),
                  pl.BlockSpec((t, LANES), lambda i: (nb - 1 - i, 0)),
                  pl.BlockSpec((1, LANES), lambda i: (0, 0))],
        out_specs=[pl.BlockSpec((t, N_C), lambda i: (nb - 1 - i, 0)),
                   pl.BlockSpec((8, LANES), lambda i: (0, 0))],
        out_shape=[jax.ShapeDtypeStruct((s, N_C), BF16),
                   jax.ShapeDtypeStruct((8, LANES), F32)],
        scratch_shapes=[pltpu.VMEM((8, LANES), F32)],
        compiler_params=_cparams(("arbitrary",)),
    )(dcum_k, dcum_q, ffp, bfp)


def _resident(shape, index_map):
    return pl.BlockSpec(shape, index_map, pipeline_mode=pl.Buffered(1))


def _fox_fwd(qkv, vt, cum_t3, cum):
    s = qkv.shape[0]
    t = min(FOX_T, s)
    nq = s // t
    nh = FOX_HEADS

    def kern(q_ref, k_ref, vt_ref, ct_ref, c_ref, o_ref, lse_ref, m_ref, l_ref, acc_ref):
        i = pl.program_id(0)
        lane = lax.broadcasted_iota(jnp.int32, (1, LANES), 1)
        krow = lax.broadcasted_iota(jnp.int32, (t, t), 0)
        qcol = lax.broadcasted_iota(jnp.int32, (t, t), 1)
        causal = krow <= qcol
        q0 = pl.multiple_of(i * t, t)
        qts, crefs = [], []
        for h in range(nh):
            p, a = divmod(h, 2)
            q2 = q_ref[:, p * LANES:(p + 1) * LANES] * jnp.asarray(QK_SCALE, BF16)
            sel = (lane < HEAD_DIM) if a == 0 else (lane >= HEAD_DIM)
            qts.append(jnp.where(sel, q2, jnp.zeros_like(q2)).astype(F32).T.astype(BF16))
            crefs.append(ct_ref[h, :, pl.ds(q0, LANES)][:, 0:1])
        m_ref[...] = jnp.full(m_ref.shape, NEG, F32)
        l_ref[...] = jnp.zeros_like(l_ref)
        acc_ref[...] = jnp.zeros_like(acc_ref)

        def tile(j, masked):
            k0 = pl.multiple_of(j * t, t)
            cb = c_ref[pl.ds(k0, t), :]
            sts = [_dot(k_ref[pl.ds(k0, t), (h // 2) * LANES:(h // 2 + 1) * LANES], qts[h]) for h in range(nh)]
            pts, scales = [], []
            for h in range(nh):
                u = sts[h] - (cb[:, h:h + 1] - crefs[h])
                if masked:
                    u = jnp.where(causal, u, NEG)
                m_old = m_ref[h]
                m_new = jnp.maximum(m_old, jnp.max(u, axis=0, keepdims=True))
                scale = jnp.exp(m_old - m_new)
                p = jnp.exp(u - m_new)
                l_ref[h] = scale * l_ref[h] + jnp.sum(p, axis=0, keepdims=True)
                m_ref[h] = m_new
                pts.append(p.astype(BF16))
                scales.append(scale)
            for h in range(nh):
                vth = vt_ref[h * HEAD_DIM:(h + 1) * HEAD_DIM, pl.ds(k0, t)]
                acc_ref[h] = scales[h] * acc_ref[h] + _dot(vth, pts[h])

        def body(j, c):
            tile(j, False)
            return c
        lax.fori_loop(0, i, body, 0)
        tile(i, True)

        for p in range(nh // 2):
            ot = jnp.concatenate([acc_ref[2 * p + a] * (1.0 / l_ref[2 * p + a]) for a in range(2)], axis=0)
            o_ref[:, p * LANES:(p + 1) * LANES] = ot.T
        for h in range(nh):
            lse_ref[h, :, pl.ds(q0, t)] = m_ref[h] + jnp.log(l_ref[h])

    return pl.pallas_call(
        kern, name="fox_fwd",
        grid=(nq,),
        in_specs=[pl.BlockSpec((t, FOX_W), lambda i: (i, 0)),
                  _resident((s, FOX_W), lambda i: (0, COL_FK // FOX_W)),
                  _resident((FOX_W, s), lambda i: (0, 0)),
                  _resident((nh, 1, s), lambda i: (0, 0, 0)),
                  _resident((s, LANES), lambda i: (0, 0))],
        out_specs=[pl.BlockSpec((t, FOX_W), lambda i: (i, 0)),
                   pl.BlockSpec((nh, 1, s), lambda i: (0, 0, 0))],
        out_shape=[jax.ShapeDtypeStruct((s, FOX_W), F32),
                   jax.ShapeDtypeStruct((nh, 1, s), F32)],
        scratch_shapes=[pltpu.VMEM((nh, 1, t), F32),
                        pltpu.VMEM((nh, 1, t), F32),
                        pltpu.VMEM((nh, HEAD_DIM, t), F32)],
        compiler_params=_cparams(("arbitrary",)),
    )(qkv, qkv, vt, cum_t3, cum)


def _fox_bwd(qkv, do_bf, cum_t3, cum, lse_t3, delta_t3):
    s = qkv.shape[0]
    t = min(FOX_T, s)
    nq = s // t
    nh = FOX_HEADS
    npair = nh // 2

    def kern(q_ref, do_ref, k_ref, v_ref, ct_ref, c_ref, lse_ref, dl_ref,
             dqt_ref, dk_ref, dv_ref, dc_ref, dcq_ref, accv_ref, acck_ref, accd_ref):
        kj = pl.program_id(0)
        lane = lax.broadcasted_iota(jnp.int32, (1, LANES), 1)
        krow = lax.broadcasted_iota(jnp.int32, (t, t), 0)
        qcol = lax.broadcasted_iota(jnp.int32, (t, t), 1)
        causal = krow <= qcol
        sels = [lane < HEAD_DIM, lane >= HEAD_DIM]

        @pl.when(kj == 0)
        def _():
            dqt_ref[...] = jnp.zeros_like(dqt_ref)
            dcq_ref[...] = jnp.zeros_like(dcq_ref)

        accv_ref[...] = jnp.zeros_like(accv_ref)
        acck_ref[...] = jnp.zeros_like(acck_ref)
        accd_ref[...] = jnp.zeros_like(accd_ref)
        cb = c_ref[...]
        k2s, v2s, kts = [], [], []
        for p in range(npair):
            k2 = k_ref[:, p * LANES:(p + 1) * LANES]
            k2s.append(k2)
            v2s.append(v_ref[:, p * LANES:(p + 1) * LANES])
            kt = k2.astype(F32).T * QK_SCALE
            kts.append(kt[:HEAD_DIM].astype(BF16))
            kts.append(kt[HEAD_DIM:].astype(BF16))
        css = [cb[:, h:h + 1] for h in range(nh)]

        def tile(i, masked):
            q0 = pl.multiple_of(i * t, t)
            sts, dpts, qms, doms = [], [], [], []
            for h in range(nh):
                p, a = divmod(h, 2)
                qi = q_ref[pl.ds(q0, t), p * LANES:(p + 1) * LANES] * jnp.asarray(QK_SCALE, BF16)
                doi = do_ref[pl.ds(q0, t), p * LANES:(p + 1) * LANES]
                qm = jnp.where(sels[a], qi, jnp.zeros_like(qi))
                dom = jnp.where(sels[a], doi, jnp.zeros_like(doi))
                qms.append(qm)
                doms.append(dom)
                sts.append(_dot_nt(k2s[p], qm))
                dpts.append(_dot_nt(v2s[p], dom))
            pts, dsts = [], []
            for h in range(nh):
                cref = ct_ref[h, :, pl.ds(q0, LANES)][:, 0:1]
                pt = jnp.exp(sts[h] - (css[h] - cref) - lse_ref[h, :, pl.ds(q0, t)])
                if masked:
                    pt = jnp.where(causal, pt, 0.0)
                ds32 = pt * (dpts[h] - dl_ref[h, :, pl.ds(q0, t)])
                part = ds32[:, 0:LANES]
                for c in range(1, t // LANES):
                    part = part + ds32[:, c * LANES:(c + 1) * LANES]
                accd_ref[h] += part
                dcq_ref[h, :, pl.ds(q0, t)] += jnp.sum(ds32, axis=0, keepdims=True)
                pts.append(pt.astype(BF16))
                dsts.append(ds32.astype(BF16))
            for p in range(npair):
                ha, hb = 2 * p, 2 * p + 1
                accv_ref[p] += _dot(pts[ha], doms[ha]) + _dot(pts[hb], doms[hb])
                acck_ref[p] += _dot(dsts[ha], qms[ha]) + _dot(dsts[hb], qms[hb])
            for h in range(nh):
                dqt_ref[h * HEAD_DIM:(h + 1) * HEAD_DIM, pl.ds(q0, t)] += _dot(kts[h], dsts[h])

        tile(kj, True)

        def body(i, c):
            tile(i, False)
            return c
        lax.fori_loop(kj + 1, nq, body, 0)

        dc = jnp.zeros((t, LANES), F32)
        for h in range(nh):
            dc = jnp.where(lane == h, -jnp.sum(accd_ref[h], axis=1, keepdims=True), dc)
        dc_ref[...] = dc
        for p in range(npair):
            dv_ref[:, p * LANES:(p + 1) * LANES] = accv_ref[p].astype(BF16)
            dk_ref[:, p * LANES:(p + 1) * LANES] = acck_ref[p].astype(BF16)

    whole = lambda kj: (0, 0, 0)
    return pl.pallas_call(
        kern, name="fox_bwd",
        grid=(nq,),
        in_specs=[_resident((s, FOX_W), lambda kj: (0, 0)),
                  _resident((s, FOX_W), lambda kj: (0, 0)),
                  pl.BlockSpec((t, FOX_W), lambda kj: (kj, COL_FK // FOX_W)),
                  pl.BlockSpec((t, FOX_W), lambda kj: (kj, COL_FV // FOX_W)),
                  _resident((nh, 1, s), whole),
                  pl.BlockSpec((t, LANES), lambda kj: (kj, 0)),
                  _resident((nh, 1, s), whole),
                  _resident((nh, 1, s), whole)],
        out_specs=[_resident((FOX_W, s), lambda kj: (0, 0)),
                   pl.BlockSpec((t, FOX_W), lambda kj: (kj, 0)),
                   pl.BlockSpec((t, FOX_W), lambda kj: (kj, 0)),
                   pl.BlockSpec((t, LANES), lambda kj: (kj, 0)),
                   _resident((nh, 1, s), whole)],
        out_shape=[jax.ShapeDtypeStruct((FOX_W, s), F32),
                   jax.ShapeDtypeStruct((s, FOX_W), BF16),
                   jax.ShapeDtypeStruct((s, FOX_W), BF16),
                   jax.ShapeDtypeStruct((s, LANES), F32),
                   jax.ShapeDtypeStruct((nh, 1, s), F32)],
        scratch_shapes=[pltpu.VMEM((npair, t, LANES), F32),
                        pltpu.VMEM((npair, t, LANES), F32),
                        pltpu.VMEM((nh, t, LANES), F32)],
        compiler_params=_cparams(("arbitrary",)),
    )(qkv, do_bf, qkv, qkv, cum_t3, cum, lse_t3, delta_t3)


def _bucket_table():
    qi = np.arange(BLOCK)[:, None]
    kj = np.arange(2 * BLOCK)[None, :]
    rel = np.maximum(qi + BLOCK - kj, 0).astype(np.int32)
    max_exact = NUM_BUCKETS // 2
    relf = np.maximum(rel, 1).astype(np.float32)
    large = max_exact + (np.log(relf / np.float32(max_exact)) / np.float32(math.log(MAX_DISTANCE / max_exact))
                         * np.float32(NUM_BUCKETS - max_exact)).astype(np.int32)
    large = np.minimum(large, NUM_BUCKETS - 1)
    return np.where(rel < max_exact, rel, large).astype(np.int32)


def _swa_bias(rel_bias, bucket):
    def kern(rb_ref, bk_ref, o_ref):
        bk = bk_ref[...]
        for h in range(SWA_HEADS):
            acc = jnp.zeros((BLOCK, 2 * BLOCK), F32)
            for b in range(NUM_BUCKETS):
                acc = jnp.where(bk == b, rb_ref[b, h], acc)
            o_ref[h] = acc

    return pl.pallas_call(
        kern, name="swa_bias",
        in_specs=[pl.BlockSpec(memory_space=pltpu.SMEM),
                  pl.BlockSpec(memory_space=pltpu.VMEM)],
        out_specs=pl.BlockSpec(memory_space=pltpu.VMEM),
        out_shape=jax.ShapeDtypeStruct((SWA_HEADS, BLOCK, 2 * BLOCK), F32),
        compiler_params=_cparams(),
    )(rel_bias, bucket)


def _swa_mask(n):
    qi = lax.broadcasted_iota(jnp.int32, (BLOCK, 2 * BLOCK), 0)
    kj = lax.broadcasted_iota(jnp.int32, (BLOCK, 2 * BLOCK), 1)
    rel = qi + BLOCK - kj
    band = (rel >= 0) & (rel < BLOCK)
    return band & ((kj >= BLOCK) | (n > 0))


def _swa_fwd(qkv, bias, sink):
    s = qkv.shape[0]
    nb = s // BLOCK

    def kern(q_ref, kp_ref, kc_ref, vp_ref, vc_ref, bias_ref, sink_ref, o_ref, lse_ref):
        n = pl.program_id(0)
        mask = _swa_mask(n)
        lane = lax.broadcasted_iota(jnp.int32, (1, LANES), 1)
        q = q_ref[...] * jnp.asarray(QK_SCALE, BF16)
        k = jnp.concatenate([kp_ref[...], kc_ref[...]], axis=0)
        v = jnp.concatenate([vp_ref[...], vc_ref[...]], axis=0)
        kgs = [k[:, g * HEAD_DIM:(g + 1) * HEAD_DIM] for g in range(SWA_KV_HEADS)]
        vgs = [v[:, g * HEAD_DIM:(g + 1) * HEAD_DIM] for g in range(SWA_KV_HEADS)]
        raw = [_dot_nt(q[:, h * HEAD_DIM:(h + 1) * HEAD_DIM], kgs[h // SWA_GROUP]) for h in range(SWA_HEADS)]
        probs = []
        lse_all = jnp.zeros((BLOCK, LANES), F32)
        for h in range(SWA_HEADS):
            sc = jnp.where(mask, raw[h] + bias_ref[h], NEG)
            sk = sink_ref[0, h]
            m = jnp.maximum(jnp.max(sc, axis=1, keepdims=True), sk)
            p = jnp.exp(sc - m)
            l = jnp.sum(p, axis=1, keepdims=True) + jnp.exp(sk - m)
            probs.append((p * (1.0 / l)).astype(BF16))
            lse_all = jnp.where(lane == h, m + jnp.log(l), lse_all)
        outs = [_dot(probs[h], vgs[h // SWA_GROUP]) for h in range(SWA_HEADS)]
        o_ref[...] = jnp.concatenate(outs, axis=1)
        lse_ref[...] = lse_all

    cq, ck, cv = COL_SQ // SWA_W, COL_SK // LANES, COL_SV // LANES
    prev = lambda n: jnp.maximum(n - 1, 0)
    return pl.pallas_call(
        kern, name="swa_fwd",
        grid=(nb,),
        in_specs=[pl.BlockSpec((BLOCK, SWA_W), lambda n: (n, cq)),
                  pl.BlockSpec((BLOCK, LANES), lambda n: (prev(n), ck)),
                  pl.BlockSpec((BLOCK, LANES), lambda n: (n, ck)),
                  pl.BlockSpec((BLOCK, LANES), lambda n: (prev(n), cv)),
                  pl.BlockSpec((BLOCK, LANES), lambda n: (n, cv)),
                  pl.BlockSpec((SWA_HEADS, BLOCK, 2 * BLOCK), lambda n: (0, 0, 0)),
                  pl.BlockSpec(memory_space=pltpu.SMEM)],
        out_specs=[pl.BlockSpec((BLOCK, SWA_W), lambda n: (n, 0)),
                   pl.BlockSpec((BLOCK, LANES), lambda n: (n, 0))],
        out_shape=[jax.ShapeDtypeStruct((s, SWA_W), F32),
                   jax.ShapeDtypeStruct((s, LANES), F32)],
        compiler_params=_cparams(("parallel",)),
    )(qkv, qkv, qkv, qkv, qkv, bias, sink)


def _swa_bwd(qkv, do_bf, delta, lse, bias, sink, bucket):
    s = qkv.shape[0]
    nb = s // BLOCK

    def kern(q_ref, kp_ref, kc_ref, vp_ref, vc_ref, do_ref, dl_ref, lse_ref, bias_ref, sink_ref, bk_ref,
             dq_ref, dk_ref, dv_ref, grb_ref, gsk_ref, dbias_ref, ck_ref, cv_ref, sk_ref):
        n = pl.program_id(0)
        lane = lax.broadcasted_iota(jnp.int32, (1, LANES), 1)

        @pl.when(n == 0)
        def _():
            dbias_ref[...] = jnp.zeros_like(dbias_ref)
            ck_ref[...] = jnp.zeros_like(ck_ref)
            cv_ref[...] = jnp.zeros_like(cv_ref)
            sk_ref[...] = jnp.zeros_like(sk_ref)

        @pl.when(n < nb)
        def _():
            mask = _swa_mask(n)
            q = q_ref[...] * jnp.asarray(QK_SCALE, BF16)
            k = jnp.concatenate([kp_ref[...], kc_ref[...]], axis=0)
            v = jnp.concatenate([vp_ref[...], vc_ref[...]], axis=0)
            do = do_ref[...]
            dl = dl_ref[...]
            lse_all = lse_ref[...]
            dks = [None] * SWA_KV_HEADS
            dvs = [None] * SWA_KV_HEADS
            gsk = jnp.zeros((1, LANES), F32)
            kgs = [k[:, g * HEAD_DIM:(g + 1) * HEAD_DIM] for g in range(SWA_KV_HEADS)]
            vgs = [v[:, g * HEAD_DIM:(g + 1) * HEAD_DIM] for g in range(SWA_KV_HEADS)]
            qhs = [q[:, h * HEAD_DIM:(h + 1) * HEAD_DIM] for h in range(SWA_HEADS)]
            dohs = [do[:, h * HEAD_DIM:(h + 1) * HEAD_DIM] for h in range(SWA_HEADS)]
            raw = [_dot_nt(qhs[h], kgs[h // SWA_GROUP]) for h in range(SWA_HEADS)]
            dps = [_dot_nt(dohs[h], vgs[h // SWA_GROUP]) for h in range(SWA_HEADS)]
            ps, dss = [], []
            for h in range(SWA_HEADS):
                lse_h = lse_all[:, h:h + 1]
                dlt = dl[:, FOX_HEADS + h:FOX_HEADS + h + 1]
                sc = jnp.where(mask, raw[h] + bias_ref[h], NEG)
                p = jnp.exp(sc - lse_h)
                ds = p * (dps[h] - dlt)
                dbias_ref[h] += ds
                p_sink = jnp.exp(sink_ref[0, h] - lse_h)
                gsk = gsk + jnp.where(lane == h, -jnp.sum(p_sink * dlt), 0.0)
                ps.append(p.astype(BF16))
                dss.append(ds.astype(BF16))
            dqs = [_dot(dss[h], kgs[h // SWA_GROUP]) * QK_SCALE for h in range(SWA_HEADS)]
            for h in range(SWA_HEADS):
                g = h // SWA_GROUP
                dk_h = _dot_tn(dss[h], qhs[h])
                dv_h = _dot_tn(ps[h], dohs[h])
                dks[g] = dk_h if dks[g] is None else dks[g] + dk_h
                dvs[g] = dv_h if dvs[g] is None else dvs[g] + dv_h
            dq_ref[...] = jnp.concatenate(dqs, axis=1).astype(BF16)
            sk_ref[...] += jnp.broadcast_to(gsk, sk_ref.shape)
            dk2 = jnp.concatenate(dks, axis=1)
            dv2 = jnp.concatenate(dvs, axis=1)
            dk_ref[...] = (ck_ref[...] + dk2[:BLOCK]).astype(BF16)
            dv_ref[...] = (cv_ref[...] + dv2[:BLOCK]).astype(BF16)
            ck_ref[...] = dk2[BLOCK:]
            cv_ref[...] = dv2[BLOCK:]

        @pl.when(n == nb)
        def _():
            dk_ref[...] = ck_ref[...].astype(BF16)
            dv_ref[...] = cv_ref[...].astype(BF16)
            gsk_ref[...] = sk_ref[...]
            bk = bk_ref[...]
            rowi = lax.broadcasted_iota(jnp.int32, (NUM_BUCKETS, LANES), 0)
            lanei = lax.broadcasted_iota(jnp.int32, (NUM_BUCKETS, LANES), 1)
            out = jnp.zeros((NUM_BUCKETS, LANES), F32)
            for h in range(SWA_HEADS):
                db = dbias_ref[h]
                for b in range(NUM_BUCKETS):
                    val = jnp.sum(jnp.where(bk == b, db, 0.0))
                    out = jnp.where((rowi == b) & (lanei == h), val, out)
            grb_ref[...] = out

    cq, ck, cv = COL_SQ // SWA_W, COL_SK // LANES, COL_SV // LANES
    cur = lambda n: jnp.minimum(n, nb - 1)
    prev = lambda n: jnp.maximum(jnp.minimum(n, nb - 1) - 1, 0)
    kout = lambda n: jnp.maximum(n - 1, 0)
    return pl.pallas_call(
        kern, name="swa_bwd",
        grid=(nb + 1,),
        in_specs=[pl.BlockSpec((BLOCK, SWA_W), lambda n: (cur(n), cq)),
                  pl.BlockSpec((BLOCK, LANES), lambda n: (prev(n), ck)),
                  pl.BlockSpec((BLOCK, LANES), lambda n: (cur(n), ck)),
                  pl.BlockSpec((BLOCK, LANES), lambda n: (prev(n), cv)),
                  pl.BlockSpec((BLOCK, LANES), lambda n: (cur(n), cv)),
                  pl.BlockSpec((BLOCK, SWA_W), lambda n: (cur(n), 1)),
                  pl.BlockSpec((BLOCK, LANES), lambda n: (cur(n), 0)),
                  pl.BlockSpec((BLOCK, LANES), lambda n: (cur(n), 0)),
                  pl.BlockSpec((SWA_HEADS, BLOCK, 2 * BLOCK), lambda n: (0, 0, 0)),
                  pl.BlockSpec(memory_space=pltpu.SMEM),
                  pl.BlockSpec((BLOCK, 2 * BLOCK), lambda n: (0, 0))],
        out_specs=[pl.BlockSpec((BLOCK, SWA_W), lambda n: (cur(n), 0)),
                   pl.BlockSpec((BLOCK, LANES), lambda n: (kout(n), 0)),
                   pl.BlockSpec((BLOCK, LANES), lambda n: (kout(n), 0)),
                   pl.BlockSpec((NUM_BUCKETS, LANES), lambda n: (0, 0)),
                   pl.BlockSpec((8, LANES), lambda n: (0, 0))],
        out_shape=[jax.ShapeDtypeStruct((s, SWA_W), BF16),
                   jax.ShapeDtypeStruct((s, LANES), BF16),
                   jax.ShapeDtypeStruct((s, LANES), BF16),
                   jax.ShapeDtypeStruct((NUM_BUCKETS, LANES), F32),
                   jax.ShapeDtypeStruct((8, LANES), F32)],
        scratch_shapes=[pltpu.VMEM((SWA_HEADS, BLOCK, 2 * BLOCK), F32),
                        pltpu.VMEM((BLOCK, LANES), F32),
                        pltpu.VMEM((BLOCK, LANES), F32),
                        pltpu.VMEM((8, LANES), F32)],
        compiler_params=_cparams(("arbitrary",)),
    )(qkv, qkv, qkv, qkv, qkv, do_bf, delta, lse, bias, sink, bucket)


def _post(x, target, o_fox, o_swa, z, w_o, ln_g, ln_b):
    s = x.shape[0]
    tm = min(256, s)
    nt = s // tm

    def kern(x_ref, t_ref, of_ref, os_ref, z_ref, w_ref, g_ref, b_ref,
             loss_ref, dh_ref, dy_ref, mix_ref, do_ref, dz_ref, dl_ref, gg_ref, gb_ref, lacc_ref):
        step = pl.program_id(0)

        @pl.when(step == 0)
        def _():
            lacc_ref[...] = jnp.zeros_like(lacc_ref)
            gg_ref[...] = jnp.zeros_like(gg_ref)
            gb_ref[...] = jnp.zeros_like(gb_ref)

        o = jnp.concatenate([of_ref[...], os_ref[...]], axis=1)
        zz = z_ref[...]
        sig = 1.0 / (1.0 + jnp.exp(-zz))
        silu = zz * sig
        mixed = (o * silu).astype(BF16)
        mix_ref[...] = mixed
        w = w_ref[...]
        h = ALPHA * x_ref[...] + _dot(mixed, w)
        mu = jnp.mean(h, axis=1, keepdims=True)
        hc = h - mu
        var = jnp.mean(hc * hc, axis=1, keepdims=True)
        rstd = lax.rsqrt(var + LN_EPS)
        xhat = hc * rstd
        g = g_ref[...]
        err = xhat * g + b_ref[...] - t_ref[...]
        lacc_ref[...] += jnp.broadcast_to(jnp.sum(err * err, axis=0, keepdims=True), lacc_ref.shape)
        dout = err * (1.0 / D_MODEL)
        gg_ref[...] += jnp.broadcast_to(jnp.sum(dout * xhat, axis=0, keepdims=True), gg_ref.shape)
        gb_ref[...] += jnp.broadcast_to(jnp.sum(dout, axis=0, keepdims=True), gb_ref.shape)
        dxh = dout * g
        m1 = jnp.mean(dxh, axis=1, keepdims=True)
        m2 = jnp.mean(dxh * xhat, axis=1, keepdims=True)
        dh = rstd * (dxh - m1 - xhat * m2)
        dh_ref[...] = dh
        dy = dh.astype(BF16)
        dy_ref[...] = dy
        dmix = _dot_nt(dy, w)
        do = dmix * silu
        do_ref[...] = do.astype(BF16)
        dz_ref[...] = (dmix * o * (sig * (1.0 + zz * (1.0 - sig)))).astype(BF16)
        r = lax.broadcasted_iota(jnp.int32, (D_MODEL, LANES), 0) // HEAD_DIM
        c = lax.broadcasted_iota(jnp.int32, (D_MODEL, LANES), 1)
        pick = jnp.where(r == c, 1.0, 0.0).astype(BF16)
        dl_ref[...] = _exact_dot(pick, do * o, False)

        @pl.when(step == nt - 1)
        def _():
            tot = jnp.sum(lacc_ref[0:1, :]) * (0.5 / D_MODEL)
            loss_ref[...] = jnp.broadcast_to(tot, loss_ref.shape)

    row = lambda i: (i, 0)
    fixed = lambda i: (0, 0)
    wide = pl.BlockSpec((tm, D_MODEL), row)
    half = pl.BlockSpec((tm, FOX_W), row)
    return pl.pallas_call(
        kern, name="post",
        grid=(nt,),
        in_specs=[wide, wide, half, half, wide,
                  pl.BlockSpec((D_MODEL, D_MODEL), fixed),
                  pl.BlockSpec((1, D_MODEL), fixed),
                  pl.BlockSpec((1, D_MODEL), fixed)],
        out_specs=[pl.BlockSpec((8, LANES), fixed), wide, wide, wide, wide, wide,
                   pl.BlockSpec((tm, LANES), row),
                   pl.BlockSpec((8, D_MODEL), fixed), pl.BlockSpec((8, D_MODEL), fixed)],
        out_shape=[jax.ShapeDtypeStruct((8, LANES), F32),
                   jax.ShapeDtypeStruct((s, D_MODEL), F32),
                   jax.ShapeDtypeStruct((s, D_MODEL), BF16),
                   jax.ShapeDtypeStruct((s, D_MODEL), BF16),
                   jax.ShapeDtypeStruct((s, D_MODEL), BF16),
                   jax.ShapeDtypeStruct((s, D_MODEL), BF16),
                   jax.ShapeDtypeStruct((s, LANES), F32),
                   jax.ShapeDtypeStruct((8, D_MODEL), F32),
                   jax.ShapeDtypeStruct((8, D_MODEL), F32)],
        scratch_shapes=[pltpu.VMEM((8, D_MODEL), F32)],
        compiler_params=_cparams(("arbitrary",)),
    )(x, target, o_fox, o_swa, z, w_o, ln_g, ln_b)


def _adamw_math(w, g, m, v):
    m = ADAM_B1 * m + (1.0 - ADAM_B1) * g
    v = ADAM_B2 * v + (1.0 - ADAM_B2) * (g * g)
    m_hat = m / (1.0 - ADAM_B1 ** ADAM_STEP)
    v_hat = v / (1.0 - ADAM_B2 ** ADAM_STEP)
    delta = -ADAM_LR * (m_hat / (jnp.sqrt(v_hat) + ADAM_EPS) + ADAM_WD * w)
    return delta, m, v


def _adamw(w, g, m, v, *, name):
    r, c = w.shape
    tr = min(256, r)

    def kern(w_ref, g_ref, m_ref, v_ref, d_ref, mo_ref, vo_ref):
        d, mn, vn = _adamw_math(w_ref[...], g_ref[...], m_ref[...], v_ref[...])
        d_ref[...] = d
        mo_ref[...] = mn
        vo_ref[...] = vn

    blk = pl.BlockSpec((tr, c), lambda i: (i, 0))
    sds = jax.ShapeDtypeStruct((r, c), F32)
    return pl.pallas_call(
        kern, name=name,
        grid=(r // tr,),
        in_specs=[blk, blk, blk, blk],
        out_specs=[blk, blk, blk],
        out_shape=[sds, sds, sds],
        compiler_params=_cparams(("parallel",)),
    )(w, g, m, v)


def _position():
    x, y, c = lax.axis_index("x"), lax.axis_index("y"), lax.axis_index("c")
    chips = [(1 - x, y), (x, 1 - y), (1 - x, 1 - y)]
    return x, y, c, chips


def _chip_index(cx, cy):
    return 2 * cx + cy


def _gather_weights(w_in_bf, w_o_bf):
    shards = (w_in_bf, w_o_bf)
    n_arr = len(shards)

    def kern(*refs):
        ins, outs = refs[:n_arr], refs[n_arr:2 * n_arr]
        send_sems, recv_sems, local_sems = refs[2 * n_arr:]
        x, y, c, chips = _position()
        me = _chip_index(x, y)
        sibling = (x, y, 1 - c)

        local = [pltpu.make_async_copy(ins[a], outs[a].at[me], local_sems.at[a]) for a in range(n_arr)]
        for cp in local:
            cp.start()

        def half(ref, a):
            rows = shards[a].shape[0] // 2
            return ref.at[pl.ds(c * rows, rows), :]

        def copy(a, k, src, slot, to):
            return pltpu.make_async_remote_copy(
                src_ref=src, dst_ref=half(outs[a].at[slot], a),
                send_sem=send_sems.at[a * 6 + k], recv_sem=recv_sems.at[a * 6 + k],
                device_id=to, device_id_type=MESH)

        first = [copy(a, j, half(ins[a], a), me, (*chip, c)) for a in range(n_arr) for j, chip in enumerate(chips)]
        for cp in first:
            cp.start()
        passed = []
        for a in range(n_arr):
            for j, chip in enumerate(chips):
                slot = _chip_index(*chip)
                copy(a, j, half(ins[a], a), slot, (*chip, c)).wait_recv()
                fwd = copy(a, 3 + j, half(outs[a].at[slot], a), slot, sibling)
                fwd.start()
                passed.append(fwd)
        for a in range(n_arr):
            for j, chip in enumerate(chips):
                slot = _chip_index(*chip)
                rows = shards[a].shape[0] // 2
                dst = outs[a].at[slot].at[pl.ds((1 - c) * rows, rows), :]
                pltpu.make_async_remote_copy(
                    src_ref=dst, dst_ref=dst, send_sem=send_sems.at[a * 6 + 3 + j],
                    recv_sem=recv_sems.at[a * 6 + 3 + j], device_id=sibling, device_id_type=MESH).wait_recv()
        for cp in first + passed:
            cp.wait_send()
        for cp in local:
            cp.wait()

    vmem = pl.BlockSpec(memory_space=pltpu.VMEM)
    return pl.pallas_call(
        kern, name="gather_weights",
        in_specs=[vmem] * n_arr,
        out_specs=[vmem] * n_arr,
        out_shape=[jax.ShapeDtypeStruct((N_CHIPS,) + w.shape, w.dtype) for w in shards],
        scratch_shapes=[pltpu.SemaphoreType.DMA((6 * n_arr,)),
                        pltpu.SemaphoreType.DMA((6 * n_arr,)),
                        pltpu.SemaphoreType.DMA((n_arr,))],
        compiler_params=_cparams(),
    )(*shards)


def _swap_halves(grads):
    n_arr = len(grads)

    def kern(*refs):
        ins = refs[:n_arr]
        owns = refs[n_arr:2 * n_arr]
        gots = refs[2 * n_arr:3 * n_arr]
        send_sems, recv_sems, local_sems = refs[3 * n_arr:]
        x, y, c, _ = _position()
        sibling = (x, y, 1 - c)
        local, remote = [], []
        for a in range(n_arr):
            rows = grads[a].shape[1] // 2
            piece = rows // COPY_PIECES
            for j in range(N_CHIPS):
                for r in range(COPY_PIECES):
                    k = (a * N_CHIPS + j) * COPY_PIECES + r
                    dst_rows = pl.ds(r * piece, piece)
                    local.append(pltpu.make_async_copy(
                        ins[a].at[j, pl.ds(c * rows + r * piece, piece), :],
                        owns[a].at[j, dst_rows, :], local_sems.at[k]))
                    remote.append(pltpu.make_async_remote_copy(
                        src_ref=ins[a].at[j, pl.ds((1 - c) * rows + r * piece, piece), :],
                        dst_ref=gots[a].at[j, dst_rows, :], send_sem=send_sems.at[k], recv_sem=recv_sems.at[k],
                        device_id=sibling, device_id_type=MESH))
        for cp in local + remote:
            cp.start()
        for cp in remote:
            cp.wait()
        for cp in local:
            cp.wait()

    hbm = pl.BlockSpec(memory_space=pltpu.VMEM)
    half = [jax.ShapeDtypeStruct((N_CHIPS, g.shape[1] // 2, g.shape[2]), F32) for g in grads]
    outs = pl.pallas_call(
        kern, name="swap_halves",
        in_specs=[hbm] * n_arr,
        out_specs=[hbm] * (2 * n_arr),
        out_shape=half + half,
        scratch_shapes=[pltpu.SemaphoreType.DMA((n_arr * N_CHIPS * COPY_PIECES,)),
                        pltpu.SemaphoreType.DMA((n_arr * N_CHIPS * COPY_PIECES,)),
                        pltpu.SemaphoreType.DMA((n_arr * N_CHIPS * COPY_PIECES,))],
        compiler_params=_cparams(),
    )(*grads)
    return outs[:n_arr], outs[n_arr:]


def _scatter_to_owners(parts):
    n_arr = len(parts)

    def kern(*refs):
        ins = refs[:n_arr]
        outs = refs[n_arr:2 * n_arr]
        send_sems, recv_sems, local_sems = refs[2 * n_arr:]
        x, y, c, chips = _position()
        me = _chip_index(x, y)
        local = [pltpu.make_async_copy(ins[a].at[me], outs[a].at[me], local_sems.at[a]) for a in range(n_arr)]
        for cp in local:
            cp.start()
        sends = []
        for a in range(n_arr):
            for j, chip in enumerate(chips):
                sends.append(pltpu.make_async_remote_copy(
                    src_ref=ins[a].at[_chip_index(*chip)], dst_ref=outs[a].at[me],
                    send_sem=send_sems.at[a * 3 + j], recv_sem=recv_sems.at[a * 3 + j],
                    device_id=(*chip, c), device_id_type=MESH))
        for cp in sends:
            cp.start()
        for a in range(n_arr):
            for j, chip in enumerate(chips):
                slot = outs[a].at[_chip_index(*chip)]
                pltpu.make_async_remote_copy(
                    src_ref=slot, dst_ref=slot, send_sem=send_sems.at[a * 3 + j],
                    recv_sem=recv_sems.at[a * 3 + j], device_id=(*chip, c), device_id_type=MESH).wait_recv()
        for cp in sends:
            cp.wait_send()
        for cp in local:
            cp.wait()

    hbm = pl.BlockSpec(memory_space=pltpu.VMEM)
    return pl.pallas_call(
        kern, name="scatter_to_owners",
        in_specs=[hbm] * n_arr,
        out_specs=[hbm] * n_arr,
        out_shape=[jax.ShapeDtypeStruct(p.shape, p.dtype) for p in parts],
        scratch_shapes=[pltpu.SemaphoreType.DMA((3 * n_arr,)),
                        pltpu.SemaphoreType.DMA((3 * n_arr,)),
                        pltpu.SemaphoreType.DMA((n_arr,))],
        compiler_params=_cparams(),
    )(*parts)


def _join_halves(halves):
    n_arr = len(halves)

    def kern(*refs):
        ins = refs[:n_arr]
        outs = refs[n_arr:2 * n_arr]
        send_sems, recv_sems, local_sems = refs[2 * n_arr:]
        x, y, c, _ = _position()
        sibling = (x, y, 1 - c)
        local, remote = [], []
        for a in range(n_arr):
            rows = halves[a].shape[0]
            piece = rows // COPY_PIECES
            for r in range(COPY_PIECES):
                k = a * COPY_PIECES + r
                src = ins[a].at[pl.ds(r * piece, piece), :]
                dst = outs[a].at[pl.ds(c * rows + r * piece, piece), :]
                local.append(pltpu.make_async_copy(src, dst, local_sems.at[k]))
                remote.append(pltpu.make_async_remote_copy(
                    src_ref=src, dst_ref=dst, send_sem=send_sems.at[k], recv_sem=recv_sems.at[k],
                    device_id=sibling, device_id_type=MESH))
        for cp in local + remote:
            cp.start()
        for a in range(n_arr):
            rows = halves[a].shape[0]
            piece = rows // COPY_PIECES
            for r in range(COPY_PIECES):
                k = a * COPY_PIECES + r
                theirs = outs[a].at[pl.ds((1 - c) * rows + r * piece, piece), :]
                pltpu.make_async_remote_copy(
                    src_ref=theirs, dst_ref=theirs, send_sem=send_sems.at[k], recv_sem=recv_sems.at[k],
                    device_id=sibling, device_id_type=MESH).wait_recv()
        for cp in remote:
            cp.wait_send()
        for cp in local:
            cp.wait()

    hbm = pl.BlockSpec(memory_space=pltpu.VMEM)
    return pl.pallas_call(
        kern, name="join_halves",
        in_specs=[hbm] * n_arr,
        out_specs=[hbm] * n_arr,
        out_shape=[jax.ShapeDtypeStruct((2 * h.shape[0], h.shape[1]), F32) for h in halves],
        scratch_shapes=[pltpu.SemaphoreType.DMA((n_arr * COPY_PIECES,)),
                        pltpu.SemaphoreType.DMA((n_arr * COPY_PIECES,)),
                        pltpu.SemaphoreType.DMA((n_arr * COPY_PIECES,))],
        compiler_params=_cparams(),
    )(*halves)


def _add2(a, b, *, name):
    n, r, c = a.shape
    tr = min(256, r)

    def kern(a_ref, b_ref, o_ref):
        o_ref[...] = (a_ref[...] + b_ref[...]).astype(BF16)

    blk = pl.BlockSpec((1, tr, c), lambda j, i: (j, i, 0))
    return pl.pallas_call(
        kern, name=name,
        grid=(n, r // tr),
        in_specs=[blk, blk],
        out_specs=blk,
        out_shape=jax.ShapeDtypeStruct(a.shape, BF16),
        compiler_params=_cparams(("parallel", "parallel")),
    )(a, b)


def _sum4(a, *, name):
    n, r, c = a.shape
    tr = min(256, r)

    def kern(a_ref, o_ref):
        f = lambda j: a_ref[j].astype(F32)
        o_ref[...] = ((f(0) + f(1)) + f(2)) + f(3)

    return pl.pallas_call(
        kern, name=name,
        grid=(r // tr,),
        in_specs=[pl.BlockSpec((n, tr, c), lambda i: (0, i, 0))],
        out_specs=pl.BlockSpec((tr, c), lambda i: (i, 0)),
        out_shape=jax.ShapeDtypeStruct((r, c), F32),
        compiler_params=_cparams(("parallel",)),
    )(a)


def _small_allreduce_adamw(g, w, m, v):
    def kern(g_ref, w_ref, m_ref, v_ref, gs_ref, d_ref, mo_ref, vo_ref, buf_ref, send_sems, recv_sems):
        x, y, c, _ = _position()
        me = 4 * x + 2 * y + c
        buf_ref[me] = g_ref[...]
        peers = [(x, y, 1 - c)] + [(px, py, pc) for px, py in _position()[3] for pc in (c, 1 - c)]
        sends = []
        for k, peer in enumerate(peers):
            sends.append(pltpu.make_async_remote_copy(
                src_ref=g_ref, dst_ref=buf_ref.at[me], send_sem=send_sems.at[k], recv_sem=recv_sems.at[k],
                device_id=peer, device_id_type=MESH))
        for cp in sends:
            cp.start()
        for k, (px, py, pc) in enumerate(peers):
            slot = buf_ref.at[4 * px + 2 * py + pc]
            pltpu.make_async_remote_copy(
                src_ref=slot, dst_ref=slot, send_sem=send_sems.at[k], recv_sem=recv_sems.at[k],
                device_id=(px, py, pc), device_id_type=MESH).wait_recv()
        for cp in sends:
            cp.wait_send()
        tot = buf_ref[0]
        for d in range(1, N_DEV):
            tot = tot + buf_ref[d]
        gs_ref[...] = tot
        delta, mn, vn = _adamw_math(w_ref[...], tot, m_ref[...], v_ref[...])
        d_ref[...] = delta
        mo_ref[...] = mn
        vo_ref[...] = vn

    vm = pl.BlockSpec(memory_space=pltpu.VMEM)
    sds = jax.ShapeDtypeStruct((SMALL_ROWS, LANES), F32)
    return pl.pallas_call(
        kern, name="small_allreduce_adamw",
        in_specs=[vm] * 4,
        out_specs=[vm] * 4,
        out_shape=[sds] * 4,
        scratch_shapes=[pltpu.VMEM((N_DEV, SMALL_ROWS, LANES), F32),
                        pltpu.SemaphoreType.DMA((N_DEV - 1,)),
                        pltpu.SemaphoreType.DMA((N_DEV - 1,))],
    )(g, w, m, v)


def _to_padded_cols(w):
    pad = jnp.zeros((w.shape[0], N_C - FOX_HEADS), w.dtype)
    return jnp.concatenate([w[:, 0:1536], w[:, 2056:2824], w[:, 1536:1544], pad,
                            w[:, 1544:2056], w[:, 2824:3336]], axis=1)


def _from_padded_cols(g):
    return jnp.concatenate([g[:, 0:1536], g[:, OFF_C:OFF_C + FOX_HEADS], g[:, OFF_B:OFF_B + FOX_W],
                            g[:, 1536:N_A], g[:, OFF_B + FOX_W:N_PAD]], axis=1)


def _pack_small(b_f, rel_bias, sink, ln_g, ln_b):
    row = lambda v: jnp.pad(v.reshape(1, -1), ((0, 0), (0, LANES - v.size)))
    return jnp.concatenate([ln_g.reshape(8, LANES), ln_b.reshape(8, LANES), rel_bias.reshape(2, LANES),
                            row(b_f), row(sink), jnp.zeros((4, LANES), F32)], axis=0)


def _unpack_small(p):
    ln_g = p[0:8].reshape(1, D_MODEL)
    ln_b = p[8:16].reshape(1, D_MODEL)
    rel_bias = p[16:18].reshape(NUM_BUCKETS, SWA_HEADS)
    b_f = p[18:19, :FOX_HEADS]
    sink = p[19:20, :SWA_HEADS]
    return b_f, rel_bias, sink, ln_g, ln_b


def _fox_rows(a):
    return a[:, :FOX_HEADS].T.reshape(FOX_HEADS, 1, a.shape[0])


def kernel(x, w_in, b_f, rel_bias, sink, w_o, ln_g, ln_b, loss_target, m_w_in, m_b_f, m_rel_bias, m_sink, m_w_o, m_ln_g, m_ln_b, v_w_in, v_b_f, v_rel_bias, v_sink, v_w_o, v_ln_g, v_ln_b):
    x2 = x[0]
    tgt = loss_target[0]
    s = x2.shape[0]
    w_in2, w_o2 = w_in[0], w_o[0]

    shard_cols = D_IN // N_CHIPS
    col_pad = ((0, 0), (0, SHARD_PAD - shard_cols))
    w_in_all, w_o_all = _gather_weights(jnp.pad(w_in2.astype(BF16), col_pad), w_o2.astype(BF16))
    w_full = jnp.concatenate([w_in_all[j, :, :shard_cols] for j in range(N_CHIPS)], axis=1)
    w_pad = _to_padded_cols(w_full)
    w_o_full = w_o_all.reshape(D_MODEL, D_MODEL)

    x_bf = x2.astype(BF16)
    qkv = _matmul_nn(x_bf, w_pad, n_off=0, n_out=N_A, tm=512, tn=768, out_dtype=BF16, name="proj_qkv")
    z = _matmul_nn(x_bf, w_pad, n_off=OFF_B, n_out=N_B, tm=512, tn=512, out_dtype=F32, name="proj_gate")
    ffp = _matmul_nn(x_bf, w_pad, n_off=OFF_C, n_out=N_C, tm=512, tn=N_C, out_dtype=F32, name="proj_forget")
    bfp = jnp.pad(b_f, ((0, 0), (0, LANES - FOX_HEADS)))
    cum = _cum_fwd(ffp, bfp)
    cum_t3 = _fox_rows(cum)
    vt = qkv[:, COL_FV:COL_FV + FOX_W].T
    o_fox, lse_t3 = _fox_fwd(qkv, vt, cum_t3, cum)
    bucket = jnp.asarray(_bucket_table())
    bias = _swa_bias(rel_bias, bucket)
    o_swa, lse_swa = _swa_fwd(qkv, bias, sink)

    loss8, dh, dy, mixed, do_bf, dz, delta, gg8, gb8 = _post(
        x2, tgt, o_fox, o_swa, z, w_o_full, ln_g, ln_b)
    loss = lax.psum(loss8[0, 0], ("x", "y", "c"))
    grad_w_o_full = _matmul_acc(mixed.T, dy, tm=1024, tn=512, tk=1024, name="grad_w_o")

    delta_t3 = _fox_rows(delta)
    dqt_fox, dk_fox, dv_fox, dcum_k, dcum_q = _fox_bwd(qkv, do_bf, cum_t3, cum, lse_t3, delta_t3)
    dcum_q = jnp.pad(dcum_q.reshape(FOX_HEADS, s).T, ((0, 0), (0, LANES - FOX_HEADS)))
    dff, gbf8 = _cum_bwd(dcum_k, dcum_q, ffp, bfp)
    dq_swa, dk_swa, dv_swa, grb, gsk8 = _swa_bwd(qkv, do_bf, delta, lse_swa, bias, sink, bucket)

    dproj = jnp.concatenate([dqt_fox.T.astype(BF16), dk_fox, dv_fox, dq_swa, dk_swa, dv_swa, dff, dz], axis=1)
    grad_x = _grad_x_matmul(dproj, w_pad, dh, tm=512, tn=512, name="grad_x")
    grad_w_pad = _matmul_acc(x_bf.T, dproj, tm=1024, tn=512, tk=1024, name="grad_w_in")
    grad_w_in_full = _from_padded_cols(grad_w_pad)

    g_in4 = jnp.stack([jnp.pad(grad_w_in_full[:, j * shard_cols:(j + 1) * shard_cols], col_pad)
                       for j in range(N_CHIPS)])
    g_o4 = grad_w_o_full.reshape(N_CHIPS, D_MODEL // N_CHIPS, D_MODEL)
    owns, gots = _swap_halves([g_in4, g_o4])
    parts = [_add2(owns[0], gots[0], name="pair_sum_w_in"), _add2(owns[1], gots[1], name="pair_sum_w_o")]
    slabs = _scatter_to_owners(parts)
    halves = [_sum4(slabs[0], name="chip_sum_w_in"), _sum4(slabs[1], name="chip_sum_w_o")]
    g_w_in, g_w_o = _join_halves(halves)
    g_w_in = g_w_in[:, :shard_cols]

    d_w_in, nm_w_in, nv_w_in = _adamw(w_in2, g_w_in, m_w_in[0], v_w_in[0], name="adamw_w_in")
    d_w_o, nm_w_o, nv_w_o = _adamw(w_o2, g_w_o, m_w_o[0], v_w_o[0], name="adamw_w_o")

    g_small = _pack_small(gbf8[0:1, :FOX_HEADS], grb[:, :SWA_HEADS], gsk8[0:1, :SWA_HEADS], gg8[0:1], gb8[0:1])
    w_small = _pack_small(b_f, rel_bias, sink, ln_g, ln_b)
    m_small = _pack_small(m_b_f, m_rel_bias, m_sink, m_ln_g, m_ln_b)
    v_small = _pack_small(v_b_f, v_rel_bias, v_sink, v_ln_g, v_ln_b)
    gs, ds, ms, vs = _small_allreduce_adamw(g_small, w_small, m_small, v_small)
    g_bf, g_rb, g_sk, g_lg, g_lb = _unpack_small(gs)
    d_bf, d_rb, d_sk, d_lg, d_lb = _unpack_small(ds)
    m_bf, m_rb, m_sk, m_lg, m_lb = _unpack_small(ms)
    v_bf, v_rb, v_sk, v_lg, v_lb = _unpack_small(vs)

    e = lambda a: a[None]
    return (loss, e(grad_x),
            e(g_w_in), g_bf, g_rb, g_sk, e(g_w_o), g_lg, g_lb,
            e(d_w_in), d_bf, d_rb, d_sk, e(d_w_o), d_lg, d_lb,
            e(nm_w_in), m_bf, m_rb, m_sk, e(nm_w_o), m_lg, m_lb,
            e(nv_w_in), v_bf, v_rb, v_sk, e(nv_w_o), v_lg, v_lb)
```

```python
import functools
import math

import numpy as np
import jax
import jax.numpy as jnp
from jax import lax
from jax.experimental import pallas as pl
from jax.experimental.pallas import tpu as pltpu

F32 = jnp.float32
BF16 = jnp.bfloat16

D_MODEL = 1024
HEAD_DIM = 64
FOX_HEADS = 8
SWA_HEADS = 8
SWA_KV_HEADS = 2
SWA_GROUP = 4
FOX_W = 512
SWA_W = 512
SWA_KV_W = 128
BLOCK = 128
NUM_BUCKETS = 32
MAX_DISTANCE = 128
LN_EPS = 1e-5
NEG = -1e30
ALPHA = 2.0 ** 0.25
QK_SCALE = 0.125

ADAM_LR = 0.001
ADAM_B1 = 0.9
ADAM_B2 = 0.999
ADAM_EPS = 1e-08
ADAM_WD = 0.01
ADAM_STEP = 10

D_IN = 3336
SHARD_PAD = 896
N_A = 2304
N_C = 256
N_B = 1024
OFF_C = N_A
OFF_B = N_A + N_C
N_PAD = N_A + N_C + N_B
COL_FK, COL_FV, COL_SQ, COL_SK, COL_SV = 512, 1024, 1536, 2048, 2176

LANES = 128
FOX_T = 256
FOX_REF = 512
VMEM_LIMIT = 56 * 1024 * 1024

MESH = pl.DeviceIdType.MESH
N_CHIPS = 4
N_DEV = 8
SMALL_ROWS = 24
COPY_PIECES = 4


def _cparams(sem=None):
    return pltpu.CompilerParams(dimension_semantics=sem, vmem_limit_bytes=VMEM_LIMIT)


def _split3(x):
    hi = x.astype(BF16)
    r = x - hi.astype(F32)
    mid = r.astype(BF16)
    lo = (r - mid.astype(F32)).astype(BF16)
    return hi, mid, lo


def _dot(a, b):
    return jnp.dot(a, b, preferred_element_type=F32)


def _dot_nt(a, b):
    return lax.dot_general(a, b, (((1,), (1,)), ((), ())), preferred_element_type=F32)


def _dot_tn(a, b):
    return lax.dot_general(a, b, (((0,), (0,)), ((), ())), preferred_element_type=F32)


def _matmul_nn(a, b, *, n_off, n_out, tm, tn, out_dtype, name):
    m, k = a.shape
    joff = n_off // tn

    def kern(a_ref, b_ref, o_ref):
        o_ref[...] = _dot(a_ref[...], b_ref[...]).astype(o_ref.dtype)

    return pl.pallas_call(
        kern, name=name,
        grid=(n_out // tn, m // tm),
        in_specs=[pl.BlockSpec((tm, k), lambda j, i: (i, 0)),
                  pl.BlockSpec((k, tn), lambda j, i: (0, j + joff))],
        out_specs=pl.BlockSpec((tm, tn), lambda j, i: (i, j)),
        out_shape=jax.ShapeDtypeStruct((m, n_out), out_dtype),
        compiler_params=_cparams(("parallel", "parallel")),
    )(a, b)


def _grad_x_matmul(dproj, w_pad, dh, *, tm, tn, name):
    m, k = dproj.shape
    n = w_pad.shape[0]

    def kern(a_ref, b_ref, dh_ref, o_ref):
        o_ref[...] = ALPHA * dh_ref[...] + _dot_nt(a_ref[...], b_ref[...])

    return pl.pallas_call(
        kern, name=name,
        grid=(n // tn, m // tm),
        in_specs=[pl.BlockSpec((tm, k), lambda j, i: (i, 0)),
                  pl.BlockSpec((tn, k), lambda j, i: (j, 0)),
                  pl.BlockSpec((tm, tn), lambda j, i: (i, j))],
        out_specs=pl.BlockSpec((tm, tn), lambda j, i: (i, j)),
        out_shape=jax.ShapeDtypeStruct((m, n), F32),
        compiler_params=_cparams(("parallel", "parallel")),
    )(dproj, w_pad, dh)


def _matmul_acc(at, b, *, tm, tn, tk, name):
    m, s = at.shape
    n = b.shape[1]

    def kern(a_ref, b_ref, o_ref):
        @pl.when(pl.program_id(2) == 0)
        def _():
            o_ref[...] = jnp.zeros_like(o_ref)
        o_ref[...] += _dot(a_ref[...], b_ref[...])

    return pl.pallas_call(
        kern, name=name,
        grid=(m // tm, n // tn, s // tk),
        in_specs=[pl.BlockSpec((tm, tk), lambda i, j, k: (i, k)),
                  pl.BlockSpec((tk, tn), lambda i, j, k: (k, j))],
        out_specs=pl.BlockSpec((tm, tn), lambda i, j, k: (i, j)),
        out_shape=jax.ShapeDtypeStruct((m, n), F32),
        compiler_params=_cparams(("parallel", "parallel", "arbitrary")),
    )(at, b)


def _tri(n, lower):
    r = lax.broadcasted_iota(jnp.int32, (n, n), 0)
    c = lax.broadcasted_iota(jnp.int32, (n, n), 1)
    keep = (c <= r) if lower else (c >= r)
    return jnp.where(keep, 1.0, 0.0).astype(BF16)


def _exact_dot(mat_bf16, x_f32, left):
    out = None
    for piece in _split3(x_f32):
        t = _dot(mat_bf16, piece) if left else _dot(piece, mat_bf16)
        out = t if out is None else out + t
    return out


def _log_sigmoid(z):
    return jnp.minimum(z, 0.0) - jnp.log(1.0 + jnp.exp(-jnp.abs(z)))


def _cum_fwd(ffp, bfp):
    s = ffp.shape[0]
    t = min(256, s)

    def kern(ff_ref, b_ref, cum_ref, carry_ref):
        @pl.when(pl.program_id(0) == 0)
        def _():
            carry_ref[...] = jnp.zeros_like(carry_ref)
        lane = lax.broadcasted_iota(jnp.int32, (1, LANES), 1)
        lf = _log_sigmoid(ff_ref[...] + b_ref[...])
        lf = jnp.where(lane < FOX_HEADS, lf, 0.0)
        cum = _exact_dot(_tri(t, True), lf, True) + carry_ref[0:1, :]
        cum_ref[...] = cum
        carry_ref[...] = jnp.broadcast_to(cum[t - 1:t, :], carry_ref.shape)

    return pl.pallas_call(
        kern, name="cum_fwd",
        grid=(s // t,),
        in_specs=[pl.BlockSpec((t, LANES), lambda i: (i, 0)),
                  pl.BlockSpec((1, LANES), lambda i: (0, 0))],
        out_specs=pl.BlockSpec((t, LANES), lambda i: (i, 0)),
        out_shape=jax.ShapeDtypeStruct((s, LANES), F32),
        scratch_shapes=[pltpu.VMEM((8, LANES), F32)],
        compiler_params=_cparams(("arbitrary",)),
    )(ffp, bfp)


def _cum_bwd(dcum_k, dcum_q, ffp, bfp):
    s = dcum_k.shape[0]
    t = min(256, s)
    nb = s // t

    def kern(dck_ref, dcq_ref, ff_ref, b_ref, dff_ref, gb_ref, carry_ref):
        @pl.when(pl.program_id(0) == 0)
        def _():
            carry_ref[...] = jnp.zeros_like(carry_ref)
            gb_ref[...] = jnp.zeros_like(gb_ref)
        lane = lax.broadcasted_iota(jnp.int32, (1, LANES), 1)
        dlf = _exact_dot(_tri(t, False), dck_ref[...] + dcq_ref[...], True) + carry_ref[0:1, :]
        carry_ref[...] = jnp.broadcast_to(dlf[0:1, :], carry_ref.shape)
        z = ff_ref[...] + b_ref[...]
        dff = jnp.where(lane < FOX_HEADS, dlf / (1.0 + jnp.exp(z)), 0.0)
        gb_ref[...] += jnp.broadcast_to(jnp.sum(dff, axis=0, keepdims=True), gb_ref.shape)
        dff_ref[...] = jnp.concatenate([dff, jnp.zeros_like(dff)], axis=1).astype(BF16)

    return pl.pallas_call(
        kern, name="cum_bwd",
        grid=(nb,),
        in_specs=[pl.BlockSpec((t, LANES), lambda i: (nb - 1 - i, 0)),
                  pl.BlockSpec((t, LANES), lambda i: (nb - 1 - i, 0)),
                  pl.BlockSpec((t, LANES), lambda i: (nb - 1 - i, 0)),
                  pl.BlockSpec((1, LANES), lambda i: (0, 0))],
        out_specs=[pl.BlockSpec((t, N_C), lambda i: (nb - 1 - i, 0)),
                   pl.BlockSpec((8, LANES), lambda i: (0, 0))],
        out_shape=[jax.ShapeDtypeStruct((s, N_C), BF16),
                   jax.ShapeDtypeStruct((8, LANES), F32)],
        scratch_shapes=[pltpu.VMEM((8, LANES), F32)],
        compiler_params=_cparams(("arbitrary",)),
    )(dcum_k, dcum_q, ffp, bfp)


def _resident(shape, index_map):
    return pl.BlockSpec(shape, index_map, pipeline_mode=pl.Buffered(1))


def _fox_fwd(qkv, vt, cum_t3, cum):
    s = qkv.shape[0]
    tk = min(FOX_T, s)
    tq = FOX_REF
    nq = s // tq
    nh = FOX_HEADS
    diag_tiles = tq // tk

    def kern(q_ref, k_ref, vt_ref, ct_ref, c_ref, o_ref, lse_ref, m_ref, l_ref, acc_ref, u_ref):
        i = pl.program_id(0)
        lane = lax.broadcasted_iota(jnp.int32, (1, LANES), 1)
        krow = lax.broadcasted_iota(jnp.int32, (tk, tq), 0)
        qcol = lax.broadcasted_iota(jnp.int32, (tk, tq), 1)
        q0 = pl.multiple_of(i * tq, tq)
        qts, crefs = [], []
        for h in range(nh):
            p, a = divmod(h, 2)
            q2 = q_ref[:, p * LANES:(p + 1) * LANES] * jnp.asarray(QK_SCALE, BF16)
            sel = (lane < HEAD_DIM) if a == 0 else (lane >= HEAD_DIM)
            qts.append(jnp.where(sel, q2, jnp.zeros_like(q2)).astype(F32).T.astype(BF16))
            crefs.append(ct_ref[h, :, pl.ds(q0, LANES)][:, 0:1])
        m_ref[...] = jnp.full(m_ref.shape, NEG, F32)
        l_ref[...] = jnp.zeros_like(l_ref)
        acc_ref[...] = jnp.zeros_like(acc_ref)

        def tile(j, diag):
            k0 = pl.multiple_of(j * tk, tk)
            cb = c_ref[pl.ds(k0, tk), :]
            sts = [_dot(k_ref[pl.ds(k0, tk), (h // 2) * LANES:(h // 2 + 1) * LANES], qts[h]) for h in range(nh)]
            tile_max = []
            for h in range(nh):
                u = sts[h] - (cb[:, h:h + 1] - crefs[h])
                if diag is not None:
                    u = jnp.where(krow + diag * tk <= qcol, u, NEG)
                u_ref[h] = u
                tile_max.append(jnp.max(u, axis=0, keepdims=True))
            pts, scales = [], []
            for h in range(nh):
                m_old = m_ref[h]
                m_new = jnp.maximum(m_old, tile_max[h])
                scale = jnp.exp(m_old - m_new)
                p = jnp.exp(u_ref[h] - m_new)
                l_ref[h] = scale * l_ref[h] + jnp.sum(p, axis=0, keepdims=True)
                m_ref[h] = m_new
                pts.append(p.astype(BF16))
                scales.append(scale)
            for h in range(nh):
                vth = vt_ref[h * HEAD_DIM:(h + 1) * HEAD_DIM, pl.ds(k0, tk)]
                acc_ref[h] = scales[h] * acc_ref[h] + _dot(vth, pts[h])

        def body(j, c):
            tile(j, None)
            return c
        lax.fori_loop(0, i * diag_tiles, body, 0)
        for d in range(diag_tiles):
            tile(i * diag_tiles + d, d)

        for p in range(nh // 2):
            ot = jnp.concatenate([acc_ref[2 * p + a] * (1.0 / l_ref[2 * p + a]) for a in range(2)], axis=0)
            o_ref[:, p * LANES:(p + 1) * LANES] = ot.T
        for h in range(nh):
            lse_ref[h, :, pl.ds(q0, tq)] = m_ref[h] + jnp.log(l_ref[h])

    return pl.pallas_call(
        kern, name="fox_fwd",
        grid=(nq,),
        in_specs=[pl.BlockSpec((tq, FOX_W), lambda i: (i, 0)),
                  _resident((s, FOX_W), lambda i: (0, COL_FK // FOX_W)),
                  _resident((FOX_W, s), lambda i: (0, 0)),
                  _resident((nh, 1, s), lambda i: (0, 0, 0)),
                  _resident((s, LANES), lambda i: (0, 0))],
        out_specs=[pl.BlockSpec((tq, FOX_W), lambda i: (i, 0)),
                   pl.BlockSpec((nh, 1, s), lambda i: (0, 0, 0))],
        out_shape=[jax.ShapeDtypeStruct((s, FOX_W), F32),
                   jax.ShapeDtypeStruct((nh, 1, s), F32)],
        scratch_shapes=[pltpu.VMEM((nh, 1, tq), F32),
                        pltpu.VMEM((nh, 1, tq), F32),
                        pltpu.VMEM((nh, HEAD_DIM, tq), F32),
                        pltpu.VMEM((nh, tk, tq), F32)],
        compiler_params=_cparams(("arbitrary",)),
    )(qkv, qkv, vt, cum_t3, cum)


def _fox_bwd(qkv, do_bf, cum_t3, cum, lse_t3, delta_t3):
    s = qkv.shape[0]
    t = min(FOX_T, s)
    nq = s // t
    nh = FOX_HEADS
    npair = nh // 2

    def kern(q_ref, do_ref, k_ref, v_ref, ct_ref, c_ref, lse_ref, dl_ref,
             dqt_ref, dk_ref, dv_ref, dc_ref, dcq_ref, accv_ref, acck_ref, accd_ref):
        kj = pl.program_id(0)
        lane = lax.broadcasted_iota(jnp.int32, (1, LANES), 1)
        krow = lax.broadcasted_iota(jnp.int32, (t, t), 0)
        qcol = lax.broadcasted_iota(jnp.int32, (t, t), 1)
        causal = krow <= qcol
        sels = [lane < HEAD_DIM, lane >= HEAD_DIM]

        @pl.when(kj == 0)
        def _():
            dqt_ref[...] = jnp.zeros_like(dqt_ref)
            dcq_ref[...] = jnp.zeros_like(dcq_ref)

        accv_ref[...] = jnp.zeros_like(accv_ref)
        acck_ref[...] = jnp.zeros_like(acck_ref)
        accd_ref[...] = jnp.zeros_like(accd_ref)
        cb = c_ref[...]
        k2s, v2s, kts = [], [], []
        for p in range(npair):
            k2 = k_ref[:, p * LANES:(p + 1) * LANES]
            k2s.append(k2)
            v2s.append(v_ref[:, p * LANES:(p + 1) * LANES])
            kt = k2.astype(F32).T * QK_SCALE
            kts.append(kt[:HEAD_DIM].astype(BF16))
            kts.append(kt[HEAD_DIM:].astype(BF16))
        css = [cb[:, h:h + 1] for h in range(nh)]

        def tile(i, masked):
            q0 = pl.multiple_of(i * t, t)
            r0 = pl.multiple_of((i // (FOX_REF // t)) * FOX_REF, FOX_REF)
            sts, dpts, qms, doms = [], [], [], []
            for h in range(nh):
                p, a = divmod(h, 2)
                qi = q_ref[pl.ds(q0, t), p * LANES:(p + 1) * LANES] * jnp.asarray(QK_SCALE, BF16)
                doi = do_ref[pl.ds(q0, t), p * LANES:(p + 1) * LANES]
                qm = jnp.where(sels[a], qi, jnp.zeros_like(qi))
                dom = jnp.where(sels[a], doi, jnp.zeros_like(doi))
                qms.append(qm)
                doms.append(dom)
                sts.append(_dot_nt(k2s[p], qm))
                dpts.append(_dot_nt(v2s[p], dom))
            pts, dsts = [], []
            for h in range(nh):
                cref = ct_ref[h, :, pl.ds(r0, LANES)][:, 0:1]
                pt = jnp.exp(sts[h] - (css[h] - cref) - lse_ref[h, :, pl.ds(q0, t)])
                if masked:
                    pt = jnp.where(causal, pt, 0.0)
                ds32 = pt * (dpts[h] - dl_ref[h, :, pl.ds(q0, t)])
                part = ds32[:, 0:LANES]
                for c in range(1, t // LANES):
                    part = part + ds32[:, c * LANES:(c + 1) * LANES]
                accd_ref[h] += part
                dcq_ref[h, :, pl.ds(q0, t)] += jnp.sum(ds32, axis=0, keepdims=True)
                pts.append(pt.astype(BF16))
                dsts.append(ds32.astype(BF16))
            for p in range(npair):
                ha, hb = 2 * p, 2 * p + 1
                accv_ref[p] += _dot(pts[ha], doms[ha]) + _dot(pts[hb], doms[hb])
                acck_ref[p] += _dot(dsts[ha], qms[ha]) + _dot(dsts[hb], qms[hb])
            for h in range(nh):
                dqt_ref[h * HEAD_DIM:(h + 1) * HEAD_DIM, pl.ds(q0, t)] += _dot(kts[h], dsts[h])

        tile(kj, True)

        def body(i, c):
            tile(i, False)
            return c
        lax.fori_loop(kj + 1, nq, body, 0)

        dc = jnp.zeros((t, LANES), F32)
        for h in range(nh):
            dc = jnp.where(lane == h, -jnp.sum(accd_ref[h], axis=1, keepdims=True), dc)
        dc_ref[...] = dc
        for p in range(npair):
            dv_ref[:, p * LANES:(p + 1) * LANES] = accv_ref[p].astype(BF16)
            dk_ref[:, p * LANES:(p + 1) * LANES] = acck_ref[p].astype(BF16)

    whole = lambda kj: (0, 0, 0)
    return pl.pallas_call(
        kern, name="fox_bwd",
        grid=(nq,),
        in_specs=[_resident((s, FOX_W), lambda kj: (0, 0)),
                  _resident((s, FOX_W), lambda kj: (0, 0)),
                  pl.BlockSpec((t, FOX_W), lambda kj: (kj, COL_FK // FOX_W)),
                  pl.BlockSpec((t, FOX_W), lambda kj: (kj, COL_FV // FOX_W)),
                  _resident((nh, 1, s), whole),
                  pl.BlockSpec((t, LANES), lambda kj: (kj, 0)),
                  _resident((nh, 1, s), whole),
                  _resident((nh, 1, s), whole)],
        out_specs=[_resident((FOX_W, s), lambda kj: (0, 0)),
                   pl.BlockSpec((t, FOX_W), lambda kj: (kj, 0)),
                   pl.BlockSpec((t, FOX_W), lambda kj: (kj, 0)),
                   pl.BlockSpec((t, LANES), lambda kj: (kj, 0)),
                   _resident((nh, 1, s), whole)],
        out_shape=[jax.ShapeDtypeStruct((FOX_W, s), F32),
                   jax.ShapeDtypeStruct((s, FOX_W), BF16),
                   jax.ShapeDtypeStruct((s, FOX_W), BF16),
                   jax.ShapeDtypeStruct((s, LANES), F32),
                   jax.ShapeDtypeStruct((nh, 1, s), F32)],
        scratch_shapes=[pltpu.VMEM((npair, t, LANES), F32),
                        pltpu.VMEM((npair, t, LANES), F32),
                        pltpu.VMEM((nh, t, LANES), F32)],
        compiler_params=_cparams(("arbitrary",)),
    )(qkv, do_bf, qkv, qkv, cum_t3, cum, lse_t3, delta_t3)


def _bucket_table():
    qi = np.arange(BLOCK)[:, None]
    kj = np.arange(2 * BLOCK)[None, :]
    rel = np.maximum(qi + BLOCK - kj, 0).astype(np.int32)
    max_exact = NUM_BUCKETS // 2
    relf = np.maximum(rel, 1).astype(np.float32)
    large = max_exact + (np.log(relf / np.float32(max_exact)) / np.float32(math.log(MAX_DISTANCE / max_exact))
                         * np.float32(NUM_BUCKETS - max_exact)).astype(np.int32)
    large = np.minimum(large, NUM_BUCKETS - 1)
    return np.where(rel < max_exact, rel, large).astype(np.int32)


def _swa_bias(rel_bias, bucket):
    def kern(rb_ref, bk_ref, o_ref):
        bk = bk_ref[...]
        for h in range(SWA_HEADS):
            acc = jnp.zeros((BLOCK, 2 * BLOCK), F32)
            for b in range(NUM_BUCKETS):
                acc = jnp.where(bk == b, rb_ref[b, h], acc)
            o_ref[h] = acc

    return pl.pallas_call(
        kern, name="swa_bias",
        in_specs=[pl.BlockSpec(memory_space=pltpu.SMEM),
                  pl.BlockSpec(memory_space=pltpu.VMEM)],
        out_specs=pl.BlockSpec(memory_space=pltpu.VMEM),
        out_shape=jax.ShapeDtypeStruct((SWA_HEADS, BLOCK, 2 * BLOCK), F32),
        compiler_params=_cparams(),
    )(rel_bias, bucket)


def _swa_mask(n):
    qi = lax.broadcasted_iota(jnp.int32, (BLOCK, 2 * BLOCK), 0)
    kj = lax.broadcasted_iota(jnp.int32, (BLOCK, 2 * BLOCK), 1)
    rel = qi + BLOCK - kj
    band = (rel >= 0) & (rel < BLOCK)
    return band & ((kj >= BLOCK) | (n > 0))


def _swa_fwd(qkv, bias, sink):
    s = qkv.shape[0]
    nb = s // BLOCK

    def kern(q_ref, kp_ref, kc_ref, vp_ref, vc_ref, bias_ref, sink_ref, o_ref, lse_ref):
        n = pl.program_id(0)
        mask = _swa_mask(n)
        lane = lax.broadcasted_iota(jnp.int32, (1, LANES), 1)
        q = q_ref[...] * jnp.asarray(QK_SCALE, BF16)
        k = jnp.concatenate([kp_ref[...], kc_ref[...]], axis=0)
        v = jnp.concatenate([vp_ref[...], vc_ref[...]], axis=0)
        kgs = [k[:, g * HEAD_DIM:(g + 1) * HEAD_DIM] for g in range(SWA_KV_HEADS)]
        vgs = [v[:, g * HEAD_DIM:(g + 1) * HEAD_DIM] for g in range(SWA_KV_HEADS)]
        raw = [_dot_nt(q[:, h * HEAD_DIM:(h + 1) * HEAD_DIM], kgs[h // SWA_GROUP]) for h in range(SWA_HEADS)]
        probs = []
        lse_all = jnp.zeros((BLOCK, LANES), F32)
        for h in range(SWA_HEADS):
            sc = jnp.where(mask, raw[h] + bias_ref[h], NEG)
            sk = sink_ref[0, h]
            m = jnp.maximum(jnp.max(sc, axis=1, keepdims=True), sk)
            p = jnp.exp(sc - m)
            l = jnp.sum(p, axis=1, keepdims=True) + jnp.exp(sk - m)
            probs.append((p * (1.0 / l)).astype(BF16))
            lse_all = jnp.where(lane == h, m + jnp.log(l), lse_all)
        outs = [_dot(probs[h], vgs[h // SWA_GROUP]) for h in range(SWA_HEADS)]
        o_ref[...] = jnp.concatenate(outs, axis=1)
        lse_ref[...] = lse_all

    cq, ck, cv = COL_SQ // SWA_W, COL_SK // LANES, COL_SV // LANES
    prev = lambda n: jnp.maximum(n - 1, 0)
    return pl.pallas_call(
        kern, name="swa_fwd",
        grid=(nb,),
        in_specs=[pl.BlockSpec((BLOCK, SWA_W), lambda n: (n, cq)),
                  pl.BlockSpec((BLOCK, LANES), lambda n: (prev(n), ck)),
                  pl.BlockSpec((BLOCK, LANES), lambda n: (n, ck)),
                  pl.BlockSpec((BLOCK, LANES), lambda n: (prev(n), cv)),
                  pl.BlockSpec((BLOCK, LANES), lambda n: (n, cv)),
                  pl.BlockSpec((SWA_HEADS, BLOCK, 2 * BLOCK), lambda n: (0, 0, 0)),
                  pl.BlockSpec(memory_space=pltpu.SMEM)],
        out_specs=[pl.BlockSpec((BLOCK, SWA_W), lambda n: (n, 0)),
                   pl.BlockSpec((BLOCK, LANES), lambda n: (n, 0))],
        out_shape=[jax.ShapeDtypeStruct((s, SWA_W), F32),
                   jax.ShapeDtypeStruct((s, LANES), F32)],
        compiler_params=_cparams(("parallel",)),
    )(qkv, qkv, qkv, qkv, qkv, bias, sink)


def _swa_bwd(qkv, do_bf, delta, lse, bias, sink, bucket):
    s = qkv.shape[0]
    nb = s // BLOCK

    def kern(q_ref, kp_ref, kc_ref, vp_ref, vc_ref, do_ref, dl_ref, lse_ref, bias_ref, sink_ref, bk_ref,
             dq_ref, dk_ref, dv_ref, grb_ref, gsk_ref, dbias_ref, ck_ref, cv_ref, sk_ref):
        n = pl.program_id(0)
        lane = lax.broadcasted_iota(jnp.int32, (1, LANES), 1)

        @pl.when(n == 0)
        def _():
            dbias_ref[...] = jnp.zeros_like(dbias_ref)
            ck_ref[...] = jnp.zeros_like(ck_ref)
            cv_ref[...] = jnp.zeros_like(cv_ref)
            sk_ref[...] = jnp.zeros_like(sk_ref)

        @pl.when(n < nb)
        def _():
            mask = _swa_mask(n)
            q = q_ref[...] * jnp.asarray(QK_SCALE, BF16)
            k = jnp.concatenate([kp_ref[...], kc_ref[...]], axis=0)
            v = jnp.concatenate([vp_ref[...], vc_ref[...]], axis=0)
            do = do_ref[...]
            dl = dl_ref[...]
            lse_all = lse_ref[...]
            dks = [None] * SWA_KV_HEADS
            dvs = [None] * SWA_KV_HEADS
            gsk = jnp.zeros((1, LANES), F32)
            kgs = [k[:, g * HEAD_DIM:(g + 1) * HEAD_DIM] for g in range(SWA_KV_HEADS)]
            vgs = [v[:, g * HEAD_DIM:(g + 1) * HEAD_DIM] for g in range(SWA_KV_HEADS)]
            qhs = [q[:, h * HEAD_DIM:(h + 1) * HEAD_DIM] for h in range(SWA_HEADS)]
            dohs = [do[:, h * HEAD_DIM:(h + 1) * HEAD_DIM] for h in range(SWA_HEADS)]
            raw = [_dot_nt(qhs[h], kgs[h // SWA_GROUP]) for h in range(SWA_HEADS)]
            dps = [_dot_nt(dohs[h], vgs[h // SWA_GROUP]) for h in range(SWA_HEADS)]
            ps, dss = [], []
            for h in range(SWA_HEADS):
                lse_h = lse_all[:, h:h + 1]
                dlt = dl[:, FOX_HEADS + h:FOX_HEADS + h + 1]
                sc = jnp.where(mask, raw[h] + bias_ref[h], NEG)
                p = jnp.exp(sc - lse_h)
                ds = p * (dps[h] - dlt)
                dbias_ref[h] += ds
                p_sink = jnp.exp(sink_ref[0, h] - lse_h)
                gsk = gsk + jnp.where(lane == h, -jnp.sum(p_sink * dlt), 0.0)
                ps.append(p.astype(BF16))
                dss.append(ds.astype(BF16))
            dqs = [_dot(dss[h], kgs[h // SWA_GROUP]) * QK_SCALE for h in range(SWA_HEADS)]
            for h in range(SWA_HEADS):
                g = h // SWA_GROUP
                dk_h = _dot_tn(dss[h], qhs[h])
                dv_h = _dot_tn(ps[h], dohs[h])
                dks[g] = dk_h if dks[g] is None else dks[g] + dk_h
                dvs[g] = dv_h if dvs[g] is None else dvs[g] + dv_h
            dq_ref[...] = jnp.concatenate(dqs, axis=1).astype(BF16)
            sk_ref[...] += jnp.broadcast_to(gsk, sk_ref.shape)
            dk2 = jnp.concatenate(dks, axis=1)
            dv2 = jnp.concatenate(dvs, axis=1)
            dk_ref[...] = (ck_ref[...] + dk2[:BLOCK]).astype(BF16)
            dv_ref[...] = (cv_ref[...] + dv2[:BLOCK]).astype(BF16)
            ck_ref[...] = dk2[BLOCK:]
            cv_ref[...] = dv2[BLOCK:]

        @pl.when(n == nb)
        def _():
            dk_ref[...] = ck_ref[...].astype(BF16)
            dv_ref[...] = cv_ref[...].astype(BF16)
            gsk_ref[...] = sk_ref[...]
            bk = bk_ref[...]
            rowi = lax.broadcasted_iota(jnp.int32, (NUM_BUCKETS, LANES), 0)
            lanei = lax.broadcasted_iota(jnp.int32, (NUM_BUCKETS, LANES), 1)
            out = jnp.zeros((NUM_BUCKETS, LANES), F32)
            for h in range(SWA_HEADS):
                db = dbias_ref[h]
                for b in range(NUM_BUCKETS):
                    val = jnp.sum(jnp.where(bk == b, db, 0.0))
                    out = jnp.where((rowi == b) & (lanei == h), val, out)
            grb_ref[...] = out

    cq, ck, cv = COL_SQ // SWA_W, COL_SK // LANES, COL_SV // LANES
    cur = lambda n: jnp.minimum(n, nb - 1)
    prev = lambda n: jnp.maximum(jnp.minimum(n, nb - 1) - 1, 0)
    kout = lambda n: jnp.maximum(n - 1, 0)
    return pl.pallas_call(
        kern, name="swa_bwd",
        grid=(nb + 1,),
        in_specs=[pl.BlockSpec((BLOCK, SWA_W), lambda n: (cur(n), cq)),
                  pl.BlockSpec((BLOCK, LANES), lambda n: (prev(n), ck)),
                  pl.BlockSpec((BLOCK, LANES), lambda n: (cur(n), ck)),
                  pl.BlockSpec((BLOCK, LANES), lambda n: (prev(n), cv)),
                  pl.BlockSpec((BLOCK, LANES), lambda n: (cur(n), cv)),
                  pl.BlockSpec((BLOCK, SWA_W), lambda n: (cur(n), 1)),
                  pl.BlockSpec((BLOCK, LANES), lambda n: (cur(n), 0)),
                  pl.BlockSpec((BLOCK, LANES), lambda n: (cur(n), 0)),
                  pl.BlockSpec((SWA_HEADS, BLOCK, 2 * BLOCK), lambda n: (0, 0, 0)),
                  pl.BlockSpec(memory_space=pltpu.SMEM),
                  pl.BlockSpec((BLOCK, 2 * BLOCK), lambda n: (0, 0))],
        out_specs=[pl.BlockSpec((BLOCK, SWA_W), lambda n: (cur(n), 0)),
                   pl.BlockSpec((BLOCK, LANES), lambda n: (kout(n), 0)),
                   pl.BlockSpec((BLOCK, LANES), lambda n: (kout(n), 0)),
                   pl.BlockSpec((NUM_BUCKETS, LANES), lambda n: (0, 0)),
                   pl.BlockSpec((8, LANES), lambda n: (0, 0))],
        out_shape=[jax.ShapeDtypeStruct((s, SWA_W), BF16),
                   jax.ShapeDtypeStruct((s, LANES), BF16),
                   jax.ShapeDtypeStruct((s, LANES), BF16),
                   jax.ShapeDtypeStruct((NUM_BUCKETS, LANES), F32),
                   jax.ShapeDtypeStruct((8, LANES), F32)],
        scratch_shapes=[pltpu.VMEM((SWA_HEADS, BLOCK, 2 * BLOCK), F32),
                        pltpu.VMEM((BLOCK, LANES), F32),
                        pltpu.VMEM((BLOCK, LANES), F32),
                        pltpu.VMEM((8, LANES), F32)],
        compiler_params=_cparams(("arbitrary",)),
    )(qkv, qkv, qkv, qkv, qkv, do_bf, delta, lse, bias, sink, bucket)


def _post(x, target, o_fox, o_swa, z, w_o, ln_g, ln_b):
    s = x.shape[0]
    tm = min(256, s)
    nt = s // tm

    def kern(x_ref, t_ref, of_ref, os_ref, z_ref, w_ref, g_ref, b_ref,
             loss_ref, dh_ref, dy_ref, mix_ref, do_ref, dz_ref, dl_ref, gg_ref, gb_ref, lacc_ref):
        step = pl.program_id(0)

        @pl.when(step == 0)
        def _():
            lacc_ref[...] = jnp.zeros_like(lacc_ref)
            gg_ref[...] = jnp.zeros_like(gg_ref)
            gb_ref[...] = jnp.zeros_like(gb_ref)

        o = jnp.concatenate([of_ref[...], os_ref[...]], axis=1)
        zz = z_ref[...]
        sig = 1.0 / (1.0 + jnp.exp(-zz))
        silu = zz * sig
        mixed = (o * silu).astype(BF16)
        mix_ref[...] = mixed
        w = w_ref[...]
        h = ALPHA * x_ref[...] + _dot(mixed, w)
        mu = jnp.mean(h, axis=1, keepdims=True)
        hc = h - mu
        var = jnp.mean(hc * hc, axis=1, keepdims=True)
        rstd = lax.rsqrt(var + LN_EPS)
        xhat = hc * rstd
        g = g_ref[...]
        err = xhat * g + b_ref[...] - t_ref[...]
        lacc_ref[...] += jnp.broadcast_to(jnp.sum(err * err, axis=0, keepdims=True), lacc_ref.shape)
        dout = err * (1.0 / D_MODEL)
        gg_ref[...] += jnp.broadcast_to(jnp.sum(dout * xhat, axis=0, keepdims=True), gg_ref.shape)
        gb_ref[...] += jnp.broadcast_to(jnp.sum(dout, axis=0, keepdims=True), gb_ref.shape)
        dxh = dout * g
        m1 = jnp.mean(dxh, axis=1, keepdims=True)
        m2 = jnp.mean(dxh * xhat, axis=1, keepdims=True)
        dh = rstd * (dxh - m1 - xhat * m2)
        dh_ref[...] = dh
        dy = dh.astype(BF16)
        dy_ref[...] = dy
        dmix = _dot_nt(dy, w)
        do = dmix * silu
        do_ref[...] = do.astype(BF16)
        dz_ref[...] = (dmix * o * (sig * (1.0 + zz * (1.0 - sig)))).astype(BF16)
        r = lax.broadcasted_iota(jnp.int32, (D_MODEL, LANES), 0) // HEAD_DIM
        c = lax.broadcasted_iota(jnp.int32, (D_MODEL, LANES), 1)
        pick = jnp.where(r == c, 1.0, 0.0).astype(BF16)
        dl_ref[...] = _exact_dot(pick, do * o, False)

        @pl.when(step == nt - 1)
        def _():
            tot = jnp.sum(lacc_ref[0:1, :]) * (0.5 / D_MODEL)
            loss_ref[...] = jnp.broadcast_to(tot, loss_ref.shape)

    row = lambda i: (i, 0)
    fixed = lambda i: (0, 0)
    wide = pl.BlockSpec((tm, D_MODEL), row)
    half = pl.BlockSpec((tm, FOX_W), row)
    return pl.pallas_call(
        kern, name="post",
        grid=(nt,),
        in_specs=[wide, wide, half, half, wide,
                  pl.BlockSpec((D_MODEL, D_MODEL), fixed),
                  pl.BlockSpec((1, D_MODEL), fixed),
                  pl.BlockSpec((1, D_MODEL), fixed)],
        out_specs=[pl.BlockSpec((8, LANES), fixed), wide, wide, wide, wide, wide,
                   pl.BlockSpec((tm, LANES), row),
                   pl.BlockSpec((8, D_MODEL), fixed), pl.BlockSpec((8, D_MODEL), fixed)],
        out_shape=[jax.ShapeDtypeStruct((8, LANES), F32),
                   jax.ShapeDtypeStruct((s, D_MODEL), F32),
                   jax.ShapeDtypeStruct((s, D_MODEL), BF16),
                   jax.ShapeDtypeStruct((s, D_MODEL), BF16),
                   jax.ShapeDtypeStruct((s, D_MODEL), BF16),
                   jax.ShapeDtypeStruct((s, D_MODEL), BF16),
                   jax.ShapeDtypeStruct((s, LANES), F32),
                   jax.ShapeDtypeStruct((8, D_MODEL), F32),
                   jax.ShapeDtypeStruct((8, D_MODEL), F32)],
        scratch_shapes=[pltpu.VMEM((8, D_MODEL), F32)],
        compiler_params=_cparams(("arbitrary",)),
    )(x, target, o_fox, o_swa, z, w_o, ln_g, ln_b)


def _adamw_math(w, g, m, v):
    m = ADAM_B1 * m + (1.0 - ADAM_B1) * g
    v = ADAM_B2 * v + (1.0 - ADAM_B2) * (g * g)
    m_hat = m / (1.0 - ADAM_B1 ** ADAM_STEP)
    v_hat = v / (1.0 - ADAM_B2 ** ADAM_STEP)
    delta = -ADAM_LR * (m_hat / (jnp.sqrt(v_hat) + ADAM_EPS) + ADAM_WD * w)
    return delta, m, v


def _adamw(w, g, m, v, *, name):
    r, c = w.shape
    tr = min(256, r)

    def kern(w_ref, g_ref, m_ref, v_ref, d_ref, mo_ref, vo_ref):
        d, mn, vn = _adamw_math(w_ref[...], g_ref[...], m_ref[...], v_ref[...])
        d_ref[...] = d
        mo_ref[...] = mn
        vo_ref[...] = vn

    blk = pl.BlockSpec((tr, c), lambda i: (i, 0))
    sds = jax.ShapeDtypeStruct((r, c), F32)
    return pl.pallas_call(
        kern, name=name,
        grid=(r // tr,),
        in_specs=[blk, blk, blk, blk],
        out_specs=[blk, blk, blk],
        out_shape=[sds, sds, sds],
        compiler_params=_cparams(("parallel",)),
    )(w, g, m, v)


def _position():
    x, y, c = lax.axis_index("x"), lax.axis_index("y"), lax.axis_index("c")
    chips = [(1 - x, y), (x, 1 - y), (1 - x, 1 - y)]
    return x, y, c, chips


def _chip_index(cx, cy):
    return 2 * cx + cy


def _gather_weights(w_in_bf, w_o_bf):
    shards = (w_in_bf, w_o_bf)
    n_arr = len(shards)

    def kern(*refs):
        ins, outs = refs[:n_arr], refs[n_arr:2 * n_arr]
        send_sems, recv_sems, local_sems = refs[2 * n_arr:]
        x, y, c, chips = _position()
        me = _chip_index(x, y)
        sibling = (x, y, 1 - c)

        local = [pltpu.make_async_copy(ins[a], outs[a].at[me], local_sems.at[a]) for a in range(n_arr)]
        for cp in local:
            cp.start()

        def half(ref, a):
            rows = shards[a].shape[0] // 2
            return ref.at[pl.ds(c * rows, rows), :]

        def copy(a, k, src, slot, to):
            return pltpu.make_async_remote_copy(
                src_ref=src, dst_ref=half(outs[a].at[slot], a),
                send_sem=send_sems.at[a * 6 + k], recv_sem=recv_sems.at[a * 6 + k],
                device_id=to, device_id_type=MESH)

        first = [copy(a, j, half(ins[a], a), me, (*chip, c)) for a in range(n_arr) for j, chip in enumerate(chips)]
        for cp in first:
            cp.start()
        passed = []
        for a in range(n_arr):
            for j, chip in enumerate(chips):
                slot = _chip_index(*chip)
                copy(a, j, half(ins[a], a), slot, (*chip, c)).wait_recv()
                fwd = copy(a, 3 + j, half(outs[a].at[slot], a), slot, sibling)
                fwd.start()
                passed.append(fwd)
        for a in range(n_arr):
            for j, chip in enumerate(chips):
                slot = _chip_index(*chip)
                rows = shards[a].shape[0] // 2
                dst = outs[a].at[slot].at[pl.ds((1 - c) * rows, rows), :]
                pltpu.make_async_remote_copy(
                    src_ref=dst, dst_ref=dst, send_sem=send_sems.at[a * 6 + 3 + j],
                    recv_sem=recv_sems.at[a * 6 + 3 + j], device_id=sibling, device_id_type=MESH).wait_recv()
        for cp in first + passed:
            cp.wait_send()
        for cp in local:
            cp.wait()

    vmem = pl.BlockSpec(memory_space=pltpu.VMEM)
    return pl.pallas_call(
        kern, name="gather_weights",
        in_specs=[vmem] * n_arr,
        out_specs=[vmem] * n_arr,
        out_shape=[jax.ShapeDtypeStruct((N_CHIPS,) + w.shape, w.dtype) for w in shards],
        scratch_shapes=[pltpu.SemaphoreType.DMA((6 * n_arr,)),
                        pltpu.SemaphoreType.DMA((6 * n_arr,)),
                        pltpu.SemaphoreType.DMA((n_arr,))],
        compiler_params=_cparams(),
    )(*shards)


def _swap_halves(grads):
    n_arr = len(grads)

    def kern(*refs):
        ins = refs[:n_arr]
        owns = refs[n_arr:2 * n_arr]
        gots = refs[2 * n_arr:3 * n_arr]
        send_sems, recv_sems, local_sems = refs[3 * n_arr:]
        x, y, c, _ = _position()
        sibling = (x, y, 1 - c)
        local, remote = [], []
        for a in range(n_arr):
            rows = grads[a].shape[1] // 2
            piece = rows // COPY_PIECES
            for j in range(N_CHIPS):
                for r in range(COPY_PIECES):
                    k = (a * N_CHIPS + j) * COPY_PIECES + r
                    dst_rows = pl.ds(r * piece, piece)
                    local.append(pltpu.make_async_copy(
                        ins[a].at[j, pl.ds(c * rows + r * piece, piece), :],
                        owns[a].at[j, dst_rows, :], local_sems.at[k]))
                    remote.append(pltpu.make_async_remote_copy(
                        src_ref=ins[a].at[j, pl.ds((1 - c) * rows + r * piece, piece), :],
                        dst_ref=gots[a].at[j, dst_rows, :], send_sem=send_sems.at[k], recv_sem=recv_sems.at[k],
                        device_id=sibling, device_id_type=MESH))
        for cp in local + remote:
            cp.start()
        for cp in remote:
            cp.wait()
        for cp in local:
            cp.wait()

    hbm = pl.BlockSpec(memory_space=pltpu.VMEM)
    half = [jax.ShapeDtypeStruct((N_CHIPS, g.shape[1] // 2, g.shape[2]), F32) for g in grads]
    outs = pl.pallas_call(
        kern, name="swap_halves",
        in_specs=[hbm] * n_arr,
        out_specs=[hbm] * (2 * n_arr),
        out_shape=half + half,
        scratch_shapes=[pltpu.SemaphoreType.DMA((n_arr * N_CHIPS * COPY_PIECES,)),
                        pltpu.SemaphoreType.DMA((n_arr * N_CHIPS * COPY_PIECES,)),
                        pltpu.SemaphoreType.DMA((n_arr * N_CHIPS * COPY_PIECES,))],
        compiler_params=_cparams(),
    )(*grads)
    return outs[:n_arr], outs[n_arr:]


def _scatter_to_owners(parts):
    n_arr = len(parts)

    def kern(*refs):
        ins = refs[:n_arr]
        outs = refs[n_arr:2 * n_arr]
        send_sems, recv_sems, local_sems = refs[2 * n_arr:]
        x, y, c, chips = _position()
        me = _chip_index(x, y)
        local = [pltpu.make_async_copy(ins[a].at[me], outs[a].at[me], local_sems.at[a]) for a in range(n_arr)]
        for cp in local:
            cp.start()
        sends = []
        for a in range(n_arr):
            for j, chip in enumerate(chips):
                sends.append(pltpu.make_async_remote_copy(
                    src_ref=ins[a].at[_chip_index(*chip)], dst_ref=outs[a].at[me],
                    send_sem=send_sems.at[a * 3 + j], recv_sem=recv_sems.at[a * 3 + j],
                    device_id=(*chip, c), device_id_type=MESH))
        for cp in sends:
            cp.start()
        for a in range(n_arr):
            for j, chip in enumerate(chips):
                slot = outs[a].at[_chip_index(*chip)]
                pltpu.make_async_remote_copy(
                    src_ref=slot, dst_ref=slot, send_sem=send_sems.at[a * 3 + j],
                    recv_sem=recv_sems.at[a * 3 + j], device_id=(*chip, c), device_id_type=MESH).wait_recv()
        for cp in sends:
            cp.wait_send()
        for cp in local:
            cp.wait()

    hbm = pl.BlockSpec(memory_space=pltpu.VMEM)
    return pl.pallas_call(
        kern, name="scatter_to_owners",
        in_specs=[hbm] * n_arr,
        out_specs=[hbm] * n_arr,
        out_shape=[jax.ShapeDtypeStruct(p.shape, p.dtype) for p in parts],
        scratch_shapes=[pltpu.SemaphoreType.DMA((3 * n_arr,)),
                        pltpu.SemaphoreType.DMA((3 * n_arr,)),
                        pltpu.SemaphoreType.DMA((n_arr,))],
        compiler_params=_cparams(),
    )(*parts)


def _join_halves(halves):
    n_arr = len(halves)

    def kern(*refs):
        ins = refs[:n_arr]
        outs = refs[n_arr:2 * n_arr]
        send_sems, recv_sems, local_sems = refs[2 * n_arr:]
        x, y, c, _ = _position()
        sibling = (x, y, 1 - c)
        local, remote = [], []
        for a in range(n_arr):
            rows = halves[a].shape[0]
            piece = rows // COPY_PIECES
            for r in range(COPY_PIECES):
                k = a * COPY_PIECES + r
                src = ins[a].at[pl.ds(r * piece, piece), :]
                dst = outs[a].at[pl.ds(c * rows + r * piece, piece), :]
                local.append(pltpu.make_async_copy(src, dst, local_sems.at[k]))
                remote.append(pltpu.make_async_remote_copy(
                    src_ref=src, dst_ref=dst, send_sem=send_sems.at[k], recv_sem=recv_sems.at[k],
                    device_id=sibling, device_id_type=MESH))
        for cp in local + remote:
            cp.start()
        for a in range(n_arr):
            rows = halves[a].shape[0]
            piece = rows // COPY_PIECES
            for r in range(COPY_PIECES):
                k = a * COPY_PIECES + r
                theirs = outs[a].at[pl.ds((1 - c) * rows + r * piece, piece), :]
                pltpu.make_async_remote_copy(
                    src_ref=theirs, dst_ref=theirs, send_sem=send_sems.at[k], recv_sem=recv_sems.at[k],
                    device_id=sibling, device_id_type=MESH).wait_recv()
        for cp in remote:
            cp.wait_send()
        for cp in local:
            cp.wait()

    hbm = pl.BlockSpec(memory_space=pltpu.VMEM)
    return pl.pallas_call(
        kern, name="join_halves",
        in_specs=[hbm] * n_arr,
        out_specs=[hbm] * n_arr,
        out_shape=[jax.ShapeDtypeStruct((2 * h.shape[0], h.shape[1]), F32) for h in halves],
        scratch_shapes=[pltpu.SemaphoreType.DMA((n_arr * COPY_PIECES,)),
                        pltpu.SemaphoreType.DMA((n_arr * COPY_PIECES,)),
                        pltpu.SemaphoreType.DMA((n_arr * COPY_PIECES,))],
        compiler_params=_cparams(),
    )(*halves)


def _add2(a, b, *, name):
    n, r, c = a.shape
    tr = min(256, r)

    def kern(a_ref, b_ref, o_ref):
        o_ref[...] = (a_ref[...] + b_ref[...]).astype(BF16)

    blk = pl.BlockSpec((1, tr, c), lambda j, i: (j, i, 0))
    return pl.pallas_call(
        kern, name=name,
        grid=(n, r // tr),
        in_specs=[blk, blk],
        out_specs=blk,
        out_shape=jax.ShapeDtypeStruct(a.shape, BF16),
        compiler_params=_cparams(("parallel", "parallel")),
    )(a, b)


def _sum4(a, *, name):
    n, r, c = a.shape
    tr = min(256, r)

    def kern(a_ref, o_ref):
        f = lambda j: a_ref[j].astype(F32)
        o_ref[...] = ((f(0) + f(1)) + f(2)) + f(3)

    return pl.pallas_call(
        kern, name=name,
        grid=(r // tr,),
        in_specs=[pl.BlockSpec((n, tr, c), lambda i: (0, i, 0))],
        out_specs=pl.BlockSpec((tr, c), lambda i: (i, 0)),
        out_shape=jax.ShapeDtypeStruct((r, c), F32),
        compiler_params=_cparams(("parallel",)),
    )(a)


def _small_allreduce_adamw(g, w, m, v):
    def kern(g_ref, w_ref, m_ref, v_ref, gs_ref, d_ref, mo_ref, vo_ref, buf_ref, send_sems, recv_sems):
        x, y, c, _ = _position()
        me = 4 * x + 2 * y + c
        buf_ref[me] = g_ref[...]
        peers = [(x, y, 1 - c)] + [(px, py, pc) for px, py in _position()[3] for pc in (c, 1 - c)]
        sends = []
        for k, peer in enumerate(peers):
            sends.append(pltpu.make_async_remote_copy(
                src_ref=g_ref, dst_ref=buf_ref.at[me], send_sem=send_sems.at[k], recv_sem=recv_sems.at[k],
                device_id=peer, device_id_type=MESH))
        for cp in sends:
            cp.start()
        for k, (px, py, pc) in enumerate(peers):
            slot = buf_ref.at[4 * px + 2 * py + pc]
            pltpu.make_async_remote_copy(
                src_ref=slot, dst_ref=slot, send_sem=send_sems.at[k], recv_sem=recv_sems.at[k],
                device_id=(px, py, pc), device_id_type=MESH).wait_recv()
        for cp in sends:
            cp.wait_send()
        tot = buf_ref[0]
        for d in range(1, N_DEV):
            tot = tot + buf_ref[d]
        gs_ref[...] = tot
        delta, mn, vn = _adamw_math(w_ref[...], tot, m_ref[...], v_ref[...])
        d_ref[...] = delta
        mo_ref[...] = mn
        vo_ref[...] = vn

    vm = pl.BlockSpec(memory_space=pltpu.VMEM)
    sds = jax.ShapeDtypeStruct((SMALL_ROWS, LANES), F32)
    return pl.pallas_call(
        kern, name="small_allreduce_adamw",
        in_specs=[vm] * 4,
        out_specs=[vm] * 4,
        out_shape=[sds] * 4,
        scratch_shapes=[pltpu.VMEM((N_DEV, SMALL_ROWS, LANES), F32),
                        pltpu.SemaphoreType.DMA((N_DEV - 1,)),
                        pltpu.SemaphoreType.DMA((N_DEV - 1,))],
    )(g, w, m, v)


def _to_padded_cols(w):
    pad = jnp.zeros((w.shape[0], N_C - FOX_HEADS), w.dtype)
    return jnp.concatenate([w[:, 0:1536], w[:, 2056:2824], w[:, 1536:1544], pad,
                            w[:, 1544:2056], w[:, 2824:3336]], axis=1)


def _from_padded_cols(g):
    return jnp.concatenate([g[:, 0:1536], g[:, OFF_C:OFF_C + FOX_HEADS], g[:, OFF_B:OFF_B + FOX_W],
                            g[:, 1536:N_A], g[:, OFF_B + FOX_W:N_PAD]], axis=1)


def _pack_small(b_f, rel_bias, sink, ln_g, ln_b):
    row = lambda v: jnp.pad(v.reshape(1, -1), ((0, 0), (0, LANES - v.size)))
    return jnp.concatenate([ln_g.reshape(8, LANES), ln_b.reshape(8, LANES), rel_bias.reshape(2, LANES),
                            row(b_f), row(sink), jnp.zeros((4, LANES), F32)], axis=0)


def _unpack_small(p):
    ln_g = p[0:8].reshape(1, D_MODEL)
    ln_b = p[8:16].reshape(1, D_MODEL)
    rel_bias = p[16:18].reshape(NUM_BUCKETS, SWA_HEADS)
    b_f = p[18:19, :FOX_HEADS]
    sink = p[19:20, :SWA_HEADS]
    return b_f, rel_bias, sink, ln_g, ln_b


def _fox_rows(a):
    return a[:, :FOX_HEADS].T.reshape(FOX_HEADS, 1, a.shape[0])


def kernel(x, w_in, b_f, rel_bias, sink, w_o, ln_g, ln_b, loss_target, m_w_in, m_b_f, m_rel_bias, m_sink, m_w_o, m_ln_g, m_ln_b, v_w_in, v_b_f, v_rel_bias, v_sink, v_w_o, v_ln_g, v_ln_b):
    x2 = x[0]
    tgt = loss_target[0]
    s = x2.shape[0]
    w_in2, w_o2 = w_in[0], w_o[0]

    shard_cols = D_IN // N_CHIPS
    col_pad = ((0, 0), (0, SHARD_PAD - shard_cols))
    w_in_all, w_o_all = _gather_weights(jnp.pad(w_in2.astype(BF16), col_pad), w_o2.astype(BF16))
    w_full = jnp.concatenate([w_in_all[j, :, :shard_cols] for j in range(N_CHIPS)], axis=1)
    w_pad = _to_padded_cols(w_full)
    w_o_full = w_o_all.reshape(D_MODEL, D_MODEL)

    x_bf = x2.astype(BF16)
    qkv = _matmul_nn(x_bf, w_pad, n_off=0, n_out=N_A, tm=512, tn=768, out_dtype=BF16, name="proj_qkv")
    z = _matmul_nn(x_bf, w_pad, n_off=OFF_B, n_out=N_B, tm=512, tn=512, out_dtype=F32, name="proj_gate")
    ffp = _matmul_nn(x_bf, w_pad, n_off=OFF_C, n_out=N_C, tm=512, tn=N_C, out_dtype=F32, name="proj_forget")
    bfp = jnp.pad(b_f, ((0, 0), (0, LANES - FOX_HEADS)))
    cum = _cum_fwd(ffp, bfp)
    cum_t3 = _fox_rows(cum)
    vt = qkv[:, COL_FV:COL_FV + FOX_W].T
    o_fox, lse_t3 = _fox_fwd(qkv, vt, cum_t3, cum)
    bucket = jnp.asarray(_bucket_table())
    bias = _swa_bias(rel_bias, bucket)
    o_swa, lse_swa = _swa_fwd(qkv, bias, sink)

    loss8, dh, dy, mixed, do_bf, dz, delta, gg8, gb8 = _post(
        x2, tgt, o_fox, o_swa, z, w_o_full, ln_g, ln_b)
    loss = lax.psum(loss8[0, 0], ("x", "y", "c"))
    grad_w_o_full = _matmul_acc(mixed.T, dy, tm=1024, tn=512, tk=1024, name="grad_w_o")

    delta_t3 = _fox_rows(delta)
    dqt_fox, dk_fox, dv_fox, dcum_k, dcum_q = _fox_bwd(qkv, do_bf, cum_t3, cum, lse_t3, delta_t3)
    dcum_q = jnp.pad(dcum_q.reshape(FOX_HEADS, s).T, ((0, 0), (0, LANES - FOX_HEADS)))
    dff, gbf8 = _cum_bwd(dcum_k, dcum_q, ffp, bfp)
    dq_swa, dk_swa, dv_swa, grb, gsk8 = _swa_bwd(qkv, do_bf, delta, lse_swa, bias, sink, bucket)

    dproj = jnp.concatenate([dqt_fox.T.astype(BF16), dk_fox, dv_fox, dq_swa, dk_swa, dv_swa, dff, dz], axis=1)
    grad_x = _grad_x_matmul(dproj, w_pad, dh, tm=512, tn=512, name="grad_x")
    grad_w_pad = _matmul_acc(x_bf.T, dproj, tm=1024, tn=512, tk=1024, name="grad_w_in")
    grad_w_in_full = _from_padded_cols(grad_w_pad)

    g_in4 = jnp.stack([jnp.pad(grad_w_in_full[:, j * shard_cols:(j + 1) * shard_cols], col_pad)
                       for j in range(N_CHIPS)])
    g_o4 = grad_w_o_full.reshape(N_CHIPS, D_MODEL // N_CHIPS, D_MODEL)
    owns, gots = _swap_halves([g_in4, g_o4])
    parts = [_add2(owns[0], gots[0], name="pair_sum_w_in"), _add2(owns[1], gots[1], name="pair_sum_w_o")]
    slabs = _scatter_to_owners(parts)
    halves = [_sum4(slabs[0], name="chip_sum_w_in"), _sum4(slabs[1], name="chip_sum_w_o")]
    g_w_in, g_w_o = _join_halves(halves)
    g_w_in = g_w_in[:, :shard_cols]

    d_w_in, nm_w_in, nv_w_in = _adamw(w_in2, g_w_in, m_w_in[0], v_w_in[0], name="adamw_w_in")
    d_w_o, nm_w_o, nv_w_o = _adamw(w_o2, g_w_o, m_w_o[0], v_w_o[0], name="adamw_w_o")

    g_small = _pack_small(gbf8[0:1, :FOX_HEADS], grb[:, :SWA_HEADS], gsk8[0:1, :SWA_HEADS], gg8[0:1], gb8[0:1])
    w_small = _pack_small(b_f, rel_bias, sink, ln_g, ln_b)
    m_small = _pack_small(m_b_f, m_rel_bias, m_sink, m_ln_g, m_ln_b)
    v_small = _pack_small(v_b_f, v_rel_bias, v_sink, v_ln_g, v_ln_b)
    gs, ds, ms, vs = _small_allreduce_adamw(g_small, w_small, m_small, v_small)
    g_bf, g_rb, g_sk, g_lg, g_lb = _unpack_small(gs)
    d_bf, d_rb, d_sk, d_lg, d_lb = _unpack_small(ds)
    m_bf, m_rb, m_sk, m_lg, m_lb = _unpack_small(ms)
    v_bf, v_rb, v_sk, v_lg, v_lb = _unpack_small(vs)

    e = lambda a: a[None]
    return (loss, e(grad_x),
            e(g_w_in), g_bf, g_rb, g_sk, e(g_w_o), g_lg, g_lb,
            e(d_w_in), d_bf, d_rb, d_sk, e(d_w_o), d_lg, d_lb,
            e(nm_w_in), m_bf, m_rb, m_sk, e(nm_w_o), m_lg, m_lb,
            e(nv_w_in), v_bf, v_rb, v_sk, e(nv_w_o), v_lg, v_lb)
```

```python
import functools
import math

import numpy as np
import jax
import jax.numpy as jnp
from jax import lax
from jax.experimental import pallas as pl
from jax.experimental.pallas import tpu as pltpu

F32 = jnp.float32
BF16 = jnp.bfloat16

D_MODEL = 1024
HEAD_DIM = 64
FOX_HEADS = 8
SWA_HEADS = 8
SWA_KV_HEADS = 2
SWA_GROUP = 4
FOX_W = 512
SWA_W = 512
SWA_KV_W = 128
BLOCK = 128
NUM_BUCKETS = 32
MAX_DISTANCE = 128
LN_EPS = 1e-5
NEG = -1e30
ALPHA = 2.0 ** 0.25
QK_SCALE = 0.125

ADAM_LR = 0.001
ADAM_B1 = 0.9
ADAM_B2 = 0.999
ADAM_EPS = 1e-08
ADAM_WD = 0.01
ADAM_STEP = 10

D_IN = 3336
SHARD_PAD = 896
N_A = 2304
N_C = 256
N_B = 1024
OFF_C = N_A
OFF_B = N_A + N_C
N_PAD = N_A + N_C + N_B
COL_FK, COL_FV, COL_SQ, COL_SK, COL_SV = 512, 1024, 1536, 2048, 2176

LANES = 128
FOX_T = 256
FOX_REF = 512
VMEM_LIMIT = 56 * 1024 * 1024

MESH = pl.DeviceIdType.MESH
N_CHIPS = 4
N_DEV = 8
SMALL_ROWS = 24
COPY_PIECES = 4


def _cparams(sem=None):
    return pltpu.CompilerParams(dimension_semantics=sem, vmem_limit_bytes=VMEM_LIMIT)


def _split3(x):
    hi = x.astype(BF16)
    r = x - hi.astype(F32)
    mid = r.astype(BF16)
    lo = (r - mid.astype(F32)).astype(BF16)
    return hi, mid, lo


def _dot(a, b):
    return jnp.dot(a, b, preferred_element_type=F32)


def _dot_nt(a, b):
    return lax.dot_general(a, b, (((1,), (1,)), ((), ())), preferred_element_type=F32)


def _dot_tn(a, b):
    return lax.dot_general(a, b, (((0,), (0,)), ((), ())), preferred_element_type=F32)


def _matmul_nn(a, b, *, n_off, n_out, tm, tn, out_dtype, name):
    m, k = a.shape
    joff = n_off // tn

    def kern(a_ref, b_ref, o_ref):
        o_ref[...] = _dot(a_ref[...].astype(BF16), b_ref[...]).astype(o_ref.dtype)

    return pl.pallas_call(
        kern, name=name,
        grid=(n_out // tn, m // tm),
        in_specs=[pl.BlockSpec((tm, k), lambda j, i: (i, 0)),
                  pl.BlockSpec((k, tn), lambda j, i: (0, j + joff))],
        out_specs=pl.BlockSpec((tm, tn), lambda j, i: (i, j)),
        out_shape=jax.ShapeDtypeStruct((m, n_out), out_dtype),
        compiler_params=_cparams(("parallel", "parallel")),
    )(a, b)


def _grad_x_matmul(pieces, w_pad, dh, *, tm, tn, name):
    m = dh.shape[0]
    n, k = w_pad.shape
    widths = [p.shape[1] for p in pieces]
    offs = [sum(widths[:i]) for i in range(len(pieces))]
    assert sum(widths) == k

    def kern(*refs):
        p_refs, (b_ref, dh_ref, o_ref) = refs[:len(pieces)], refs[len(pieces):]
        acc = ALPHA * dh_ref[...]
        for p_ref, off, width in zip(p_refs, offs, widths):
            acc = acc + _dot_nt(p_ref[...], b_ref[:, off:off + width])
        o_ref[...] = acc

    return pl.pallas_call(
        kern, name=name,
        grid=(n // tn, m // tm),
        in_specs=[pl.BlockSpec((tm, w), lambda j, i: (i, 0)) for w in widths]
        + [pl.BlockSpec((tn, k), lambda j, i: (j, 0)),
           pl.BlockSpec((tm, tn), lambda j, i: (i, j))],
        out_specs=pl.BlockSpec((tm, tn), lambda j, i: (i, j)),
        out_shape=jax.ShapeDtypeStruct((m, n), F32),
        compiler_params=_cparams(("parallel", "parallel")),
    )(*pieces, w_pad, dh)


def _grad_w_matmul(xt, blocks, *, tk, name):
    m, s = xt.shape
    tn = 512
    nb = len(blocks)

    def kern(a_ref, *refs):
        b_refs, o_ref = refs[:nb], refs[nb]
        j = pl.program_id(0)

        @pl.when(pl.program_id(1) == 0)
        def _():
            o_ref[...] = jnp.zeros_like(o_ref)
        for blk in range(nb):
            @pl.when(j == blk)
            def _(blk=blk):
                o_ref[...] += _dot(a_ref[...], b_refs[blk][...])

    def b_spec(blk, col):
        return pl.BlockSpec((tk, tn), lambda j, k: (jnp.where(j == blk, k, 0), col))

    return pl.pallas_call(
        kern, name=name,
        grid=(nb, s // tk),
        in_specs=[pl.BlockSpec((m, tk), lambda j, k: (0, k))]
        + [b_spec(blk, col) for blk, (_, col) in enumerate(blocks)],
        out_specs=pl.BlockSpec((m, tn), lambda j, k: (0, j)),
        out_shape=jax.ShapeDtypeStruct((m, nb * tn), F32),
        compiler_params=_cparams(("parallel", "arbitrary")),
    )(xt, *[arr for arr, _ in blocks])


def _matmul_acc(at, b, *, tm, tn, tk, name):
    m, s = at.shape
    n = b.shape[1]

    def kern(a_ref, b_ref, o_ref):
        @pl.when(pl.program_id(2) == 0)
        def _():
            o_ref[...] = jnp.zeros_like(o_ref)
        o_ref[...] += _dot(a_ref[...], b_ref[...])

    return pl.pallas_call(
        kern, name=name,
        grid=(m // tm, n // tn, s // tk),
        in_specs=[pl.BlockSpec((tm, tk), lambda i, j, k: (i, k)),
                  pl.BlockSpec((tk, tn), lambda i, j, k: (k, j))],
        out_specs=pl.BlockSpec((tm, tn), lambda i, j, k: (i, j)),
        out_shape=jax.ShapeDtypeStruct((m, n), F32),
        compiler_params=_cparams(("parallel", "parallel", "arbitrary")),
    )(at, b)


def _tri(n, lower):
    r = lax.broadcasted_iota(jnp.int32, (n, n), 0)
    c = lax.broadcasted_iota(jnp.int32, (n, n), 1)
    keep = (c <= r) if lower else (c >= r)
    return jnp.where(keep, 1.0, 0.0).astype(BF16)


def _exact_dot(mat_bf16, x_f32, left):
    out = None
    for piece in _split3(x_f32):
        t = _dot(mat_bf16, piece) if left else _dot(piece, mat_bf16)
        out = t if out is None else out + t
    return out


def _log_sigmoid(z):
    return jnp.minimum(z, 0.0) - jnp.log(1.0 + jnp.exp(-jnp.abs(z)))


def _cum_fwd(ffp, bfp):
    s = ffp.shape[0]
    t = min(256, s)

    def kern(ff_ref, b_ref, cum_ref, carry_ref):
        @pl.when(pl.program_id(0) == 0)
        def _():
            carry_ref[...] = jnp.zeros_like(carry_ref)
        lane = lax.broadcasted_iota(jnp.int32, (1, LANES), 1)
        lf = _log_sigmoid(ff_ref[...] + b_ref[...])
        lf = jnp.where(lane < FOX_HEADS, lf, 0.0)
        cum = _exact_dot(_tri(t, True), lf, True) + carry_ref[0:1, :]
        cum_ref[...] = cum
        carry_ref[...] = jnp.broadcast_to(cum[t - 1:t, :], carry_ref.shape)

    return pl.pallas_call(
        kern, name="cum_fwd",
        grid=(s // t,),
        in_specs=[pl.BlockSpec((t, LANES), lambda i: (i, 0)),
                  pl.BlockSpec((1, LANES), lambda i: (0, 0))],
        out_specs=pl.BlockSpec((t, LANES), lambda i: (i, 0)),
        out_shape=jax.ShapeDtypeStruct((s, LANES), F32),
        scratch_shapes=[pltpu.VMEM((8, LANES), F32)],
        compiler_params=_cparams(("arbitrary",)),
    )(ffp, bfp)


def _cum_bwd(dcum_k, dcum_q, ffp, bfp):
    s = dcum_k.shape[0]
    t = min(256, s)
    nb = s // t

    def kern(dck_ref, dcq_ref, ff_ref, b_ref, dff_ref, gb_ref, carry_ref):
        @pl.when(pl.program_id(0) == 0)
        def _():
            carry_ref[...] = jnp.zeros_like(carry_ref)
            gb_ref[...] = jnp.zeros_like(gb_ref)
        lane = lax.broadcasted_iota(jnp.int32, (1, LANES), 1)
        dlf = _exact_dot(_tri(t, False), dck_ref[...] + dcq_ref[...], True) + carry_ref[0:1, :]
        carry_ref[...] = jnp.broadcast_to(dlf[0:1, :], carry_ref.shape)
        z = ff_ref[...] + b_ref[...]
        dff = jnp.where(lane < FOX_HEADS, dlf / (1.0 + jnp.exp(z)), 0.0)
        gb_ref[...] += jnp.broadcast_to(jnp.sum(dff, axis=0, keepdims=True), gb_ref.shape)
        dff_ref[...] = jnp.concatenate([dff, jnp.zeros_like(dff)], axis=1).astype(BF16)

    return pl.pallas_call(
        kern, name="cum_bwd",
        grid=(nb,),
        in_specs=[pl.BlockSpec((t, LANES), lambda i: (nb - 1 - i, 0)),
                  pl.BlockSpec((t, LANES), lambda i: (nb - 1 - i, 0)),
                  pl.BlockSpec((t, LANES), lambda i: (nb - 1 - i, 0)),
                  pl.BlockSpec((1, LANES), lambda i: (0, 0))],
        out_specs=[pl.BlockSpec((t, N_C), lambda i: (nb - 1 - i, 0)),
                   pl.BlockSpec((8, LANES), lambda i: (0, 0))],
        out_shape=[jax.ShapeDtypeStruct((s, N_C), BF16),
                   jax.ShapeDtypeStruct((8, LANES), F32)],
        scratch_shapes=[pltpu.VMEM((8, LANES), F32)],
        compiler_params=_cparams(("arbitrary",)),
    )(dcum_k, dcum_q, ffp, bfp)


def _resident(shape, index_map):
    return pl.BlockSpec(shape, index_map, pipeline_mode=pl.Buffered(1))


def _fox_fwd(qkv, vt, cum_t3, cum):
    s = qkv.shape[0]
    tk = min(FOX_T, s)
    tq = FOX_REF
    nq = s // tq
    nh = FOX_HEADS
    diag_tiles = tq // tk

    def kern(q_ref, k_ref, vt_ref, ct_ref, c_ref, o_ref, lse_ref, m_ref, l_ref, acc_ref, u_ref):
        i = pl.program_id(0)
        lane = lax.broadcasted_iota(jnp.int32, (1, LANES), 1)
        krow = lax.broadcasted_iota(jnp.int32, (tk, tq), 0)
        qcol = lax.broadcasted_iota(jnp.int32, (tk, tq), 1)
        q0 = pl.multiple_of(i * tq, tq)
        qts, crefs = [], []
        for h in range(nh):
            p, a = divmod(h, 2)
            q2 = q_ref[:, p * LANES:(p + 1) * LANES] * jnp.asarray(QK_SCALE, BF16)
            sel = (lane < HEAD_DIM) if a == 0 else (lane >= HEAD_DIM)
            qts.append(jnp.where(sel, q2, jnp.zeros_like(q2)).astype(F32).T.astype(BF16))
            crefs.append(ct_ref[h, :, pl.ds(q0, LANES)][:, 0:1])
        m_ref[...] = jnp.full(m_ref.shape, NEG, F32)
        l_ref[...] = jnp.zeros_like(l_ref)
        acc_ref[...] = jnp.zeros_like(acc_ref)

        def tile(j, diag):
            k0 = pl.multiple_of(j * tk, tk)
            cb = c_ref[pl.ds(k0, tk), :]
            sts = [_dot(k_ref[pl.ds(k0, tk), (h // 2) * LANES:(h // 2 + 1) * LANES], qts[h]) for h in range(nh)]
            tile_max = []
            for h in range(nh):
                u = sts[h] - (cb[:, h:h + 1] - crefs[h])
                if diag is not None:
                    u = jnp.where(krow + diag * tk <= qcol, u, NEG)
                u_ref[h] = u
                tile_max.append(jnp.max(u, axis=0, keepdims=True))
            pts, scales = [], []
            for h in range(nh):
                m_old = m_ref[h]
                m_new = jnp.maximum(m_old, tile_max[h])
                scale = jnp.exp(m_old - m_new)
                p = jnp.exp(u_ref[h] - m_new)
                l_ref[h] = scale * l_ref[h] + jnp.sum(p, axis=0, keepdims=True)
                m_ref[h] = m_new
                pts.append(p.astype(BF16))
                scales.append(scale)
            for h in range(nh):
                vth = vt_ref[h * HEAD_DIM:(h + 1) * HEAD_DIM, pl.ds(k0, tk)]
                acc_ref[h] = scales[h] * acc_ref[h] + _dot(vth, pts[h])

        def body(j, c):
            tile(j, None)
            return c
        lax.fori_loop(0, i * diag_tiles, body, 0)
        for d in range(diag_tiles):
            tile(i * diag_tiles + d, d)

        for p in range(nh // 2):
            ot = jnp.concatenate([acc_ref[2 * p + a] * (1.0 / l_ref[2 * p + a]) for a in range(2)], axis=0)
            o_ref[:, p * LANES:(p + 1) * LANES] = ot.T
        for h in range(nh):
            lse_ref[h, :, pl.ds(q0, tq)] = m_ref[h] + jnp.log(l_ref[h])

    return pl.pallas_call(
        kern, name="fox_fwd",
        grid=(nq,),
        in_specs=[pl.BlockSpec((tq, FOX_W), lambda i: (i, 0)),
                  _resident((s, FOX_W), lambda i: (0, COL_FK // FOX_W)),
                  _resident((FOX_W, s), lambda i: (0, 0)),
                  _resident((nh, 1, s), lambda i: (0, 0, 0)),
                  _resident((s, LANES), lambda i: (0, 0))],
        out_specs=[pl.BlockSpec((tq, FOX_W), lambda i: (i, 0)),
                   pl.BlockSpec((nh, 1, s), lambda i: (0, 0, 0))],
        out_shape=[jax.ShapeDtypeStruct((s, FOX_W), F32),
                   jax.ShapeDtypeStruct((nh, 1, s), F32)],
        scratch_shapes=[pltpu.VMEM((nh, 1, tq), F32),
                        pltpu.VMEM((nh, 1, tq), F32),
                        pltpu.VMEM((nh, HEAD_DIM, tq), F32),
                        pltpu.VMEM((nh, tk, tq), F32)],
        compiler_params=_cparams(("arbitrary",)),
    )(qkv, qkv, vt, cum_t3, cum)


def _fox_bwd(qkv, do_bf, cum_t3, cum, lse_t3, delta_t3):
    s = qkv.shape[0]
    t = min(FOX_T, s)
    nq = s // t
    nh = FOX_HEADS
    npair = nh // 2

    def kern(q_ref, do_ref, k_ref, v_ref, ct_ref, c_ref, lse_ref, dl_ref,
             dqt_ref, dk_ref, dv_ref, dc_ref, dcq_ref, accv_ref, acck_ref, accd_ref):
        kj = pl.program_id(0)
        lane = lax.broadcasted_iota(jnp.int32, (1, LANES), 1)
        krow = lax.broadcasted_iota(jnp.int32, (t, t), 0)
        qcol = lax.broadcasted_iota(jnp.int32, (t, t), 1)
        causal = krow <= qcol
        sels = [lane < HEAD_DIM, lane >= HEAD_DIM]

        @pl.when(kj == 0)
        def _():
            dqt_ref[...] = jnp.zeros_like(dqt_ref)
            dcq_ref[...] = jnp.zeros_like(dcq_ref)

        accv_ref[...] = jnp.zeros_like(accv_ref)
        acck_ref[...] = jnp.zeros_like(acck_ref)
        accd_ref[...] = jnp.zeros_like(accd_ref)
        cb = c_ref[...]
        k2s, v2s, kts = [], [], []
        for p in range(npair):
            k2 = k_ref[:, p * LANES:(p + 1) * LANES]
            k2s.append(k2)
            v2s.append(v_ref[:, p * LANES:(p + 1) * LANES])
            kt = k2.astype(F32).T * QK_SCALE
            kts.append(kt[:HEAD_DIM].astype(BF16))
            kts.append(kt[HEAD_DIM:].astype(BF16))
        css = [cb[:, h:h + 1] for h in range(nh)]

        def tile(i, masked):
            q0 = pl.multiple_of(i * t, t)
            r0 = pl.multiple_of((i // (FOX_REF // t)) * FOX_REF, FOX_REF)
            sts, dpts, qms, doms = [], [], [], []
            for h in range(nh):
                p, a = divmod(h, 2)
                qi = q_ref[pl.ds(q0, t), p * LANES:(p + 1) * LANES] * jnp.asarray(QK_SCALE, BF16)
                doi = do_ref[pl.ds(q0, t), p * LANES:(p + 1) * LANES]
                qm = jnp.where(sels[a], qi, jnp.zeros_like(qi))
                dom = jnp.where(sels[a], doi, jnp.zeros_like(doi))
                qms.append(qm)
                doms.append(dom)
                sts.append(_dot_nt(k2s[p], qm))
                dpts.append(_dot_nt(v2s[p], dom))
            pts, dsts = [], []
            for h in range(nh):
                cref = ct_ref[h, :, pl.ds(r0, LANES)][:, 0:1]
                pt = jnp.exp(sts[h] - (css[h] - cref) - lse_ref[h, :, pl.ds(q0, t)])
                if masked:
                    pt = jnp.where(causal, pt, 0.0)
                ds32 = pt * (dpts[h] - dl_ref[h, :, pl.ds(q0, t)])
                part = ds32[:, 0:LANES]
                for c in range(1, t // LANES):
                    part = part + ds32[:, c * LANES:(c + 1) * LANES]
                accd_ref[h] += part
                dcq_ref[h, :, pl.ds(q0, t)] += jnp.sum(ds32, axis=0, keepdims=True)
                pts.append(pt.astype(BF16))
                dsts.append(ds32.astype(BF16))
            for p in range(npair):
                ha, hb = 2 * p, 2 * p + 1
                accv_ref[p] += _dot(pts[ha], doms[ha]) + _dot(pts[hb], doms[hb])
                acck_ref[p] += _dot(dsts[ha], qms[ha]) + _dot(dsts[hb], qms[hb])
            for h in range(nh):
                dqt_ref[h * HEAD_DIM:(h + 1) * HEAD_DIM, pl.ds(q0, t)] += _dot(kts[h], dsts[h])

        tile(kj, True)

        def body(i, c):
            tile(i, False)
            return c
        lax.fori_loop(kj + 1, nq, body, 0)

        dc = jnp.zeros((t, LANES), F32)
        for h in range(nh):
            dc = jnp.where(lane == h, -jnp.sum(accd_ref[h], axis=1, keepdims=True), dc)
        dc_ref[...] = dc
        for p in range(npair):
            dv_ref[:, p * LANES:(p + 1) * LANES] = accv_ref[p].astype(BF16)
            dk_ref[:, p * LANES:(p + 1) * LANES] = acck_ref[p].astype(BF16)

    whole = lambda kj: (0, 0, 0)
    return pl.pallas_call(
        kern, name="fox_bwd",
        grid=(nq,),
        in_specs=[_resident((s, FOX_W), lambda kj: (0, 0)),
                  _resident((s, FOX_W), lambda kj: (0, 0)),
                  pl.BlockSpec((t, FOX_W), lambda kj: (kj, COL_FK // FOX_W)),
                  pl.BlockSpec((t, FOX_W), lambda kj: (kj, COL_FV // FOX_W)),
                  _resident((nh, 1, s), whole),
                  pl.BlockSpec((t, LANES), lambda kj: (kj, 0)),
                  _resident((nh, 1, s), whole),
                  _resident((nh, 1, s), whole)],
        out_specs=[_resident((FOX_W, s), lambda kj: (0, 0)),
                   pl.BlockSpec((t, FOX_W), lambda kj: (kj, 0)),
                   pl.BlockSpec((t, FOX_W), lambda kj: (kj, 0)),
                   pl.BlockSpec((t, LANES), lambda kj: (kj, 0)),
                   _resident((nh, 1, s), whole)],
        out_shape=[jax.ShapeDtypeStruct((FOX_W, s), F32),
                   jax.ShapeDtypeStruct((s, FOX_W), BF16),
                   jax.ShapeDtypeStruct((s, FOX_W), BF16),
                   jax.ShapeDtypeStruct((s, LANES), F32),
                   jax.ShapeDtypeStruct((nh, 1, s), F32)],
        scratch_shapes=[pltpu.VMEM((npair, t, LANES), F32),
                        pltpu.VMEM((npair, t, LANES), F32),
                        pltpu.VMEM((nh, t, LANES), F32)],
        compiler_params=_cparams(("arbitrary",)),
    )(qkv, do_bf, qkv, qkv, cum_t3, cum, lse_t3, delta_t3)


def _bucket_table():
    qi = np.arange(BLOCK)[:, None]
    kj = np.arange(2 * BLOCK)[None, :]
    rel = np.maximum(qi + BLOCK - kj, 0).astype(np.int32)
    max_exact = NUM_BUCKETS // 2
    relf = np.maximum(rel, 1).astype(np.float32)
    large = max_exact + (np.log(relf / np.float32(max_exact)) / np.float32(math.log(MAX_DISTANCE / max_exact))
                         * np.float32(NUM_BUCKETS - max_exact)).astype(np.int32)
    large = np.minimum(large, NUM_BUCKETS - 1)
    return np.where(rel < max_exact, rel, large).astype(np.int32)


def _swa_bias(rel_bias, bucket):
    def kern(rb_ref, bk_ref, o_ref):
        bk = bk_ref[...]
        for h in range(SWA_HEADS):
            acc = jnp.zeros((BLOCK, 2 * BLOCK), F32)
            for b in range(NUM_BUCKETS):
                acc = jnp.where(bk == b, rb_ref[b, h], acc)
            o_ref[h] = acc

    return pl.pallas_call(
        kern, name="swa_bias",
        in_specs=[pl.BlockSpec(memory_space=pltpu.SMEM),
                  pl.BlockSpec(memory_space=pltpu.VMEM)],
        out_specs=pl.BlockSpec(memory_space=pltpu.VMEM),
        out_shape=jax.ShapeDtypeStruct((SWA_HEADS, BLOCK, 2 * BLOCK), F32),
        compiler_params=_cparams(),
    )(rel_bias, bucket)


def _swa_mask(n):
    qi = lax.broadcasted_iota(jnp.int32, (BLOCK, 2 * BLOCK), 0)
    kj = lax.broadcasted_iota(jnp.int32, (BLOCK, 2 * BLOCK), 1)
    rel = qi + BLOCK - kj
    band = (rel >= 0) & (rel < BLOCK)
    return band & ((kj >= BLOCK) | (n > 0))


def _swa_fwd(qkv, bias, sink):
    s = qkv.shape[0]
    nb = s // BLOCK

    def kern(q_ref, kp_ref, kc_ref, vp_ref, vc_ref, bias_ref, sink_ref, o_ref, lse_ref):
        n = pl.program_id(0)
        mask = _swa_mask(n)
        lane = lax.broadcasted_iota(jnp.int32, (1, LANES), 1)
        q = q_ref[...] * jnp.asarray(QK_SCALE, BF16)
        k = jnp.concatenate([kp_ref[...], kc_ref[...]], axis=0)
        v = jnp.concatenate([vp_ref[...], vc_ref[...]], axis=0)
        kgs = [k[:, g * HEAD_DIM:(g + 1) * HEAD_DIM] for g in range(SWA_KV_HEADS)]
        vgs = [v[:, g * HEAD_DIM:(g + 1) * HEAD_DIM] for g in range(SWA_KV_HEADS)]
        raw = [_dot_nt(q[:, h * HEAD_DIM:(h + 1) * HEAD_DIM], kgs[h // SWA_GROUP]) for h in range(SWA_HEADS)]
        probs = []
        lse_all = jnp.zeros((BLOCK, LANES), F32)
        for h in range(SWA_HEADS):
            sc = jnp.where(mask, raw[h] + bias_ref[h], NEG)
            sk = sink_ref[0, h]
            m = jnp.maximum(jnp.max(sc, axis=1, keepdims=True), sk)
            p = jnp.exp(sc - m)
            l = jnp.sum(p, axis=1, keepdims=True) + jnp.exp(sk - m)
            probs.append((p * (1.0 / l)).astype(BF16))
            lse_all = jnp.where(lane == h, m + jnp.log(l), lse_all)
        outs = [_dot(probs[h], vgs[h // SWA_GROUP]) for h in range(SWA_HEADS)]
        o_ref[...] = jnp.concatenate(outs, axis=1)
        lse_ref[...] = lse_all

    cq, ck, cv = COL_SQ // SWA_W, COL_SK // LANES, COL_SV // LANES
    prev = lambda n: jnp.maximum(n - 1, 0)
    return pl.pallas_call(
        kern, name="swa_fwd",
        grid=(nb,),
        in_specs=[pl.BlockSpec((BLOCK, SWA_W), lambda n: (n, cq)),
                  pl.BlockSpec((BLOCK, LANES), lambda n: (prev(n), ck)),
                  pl.BlockSpec((BLOCK, LANES), lambda n: (n, ck)),
                  pl.BlockSpec((BLOCK, LANES), lambda n: (prev(n), cv)),
                  pl.BlockSpec((BLOCK, LANES), lambda n: (n, cv)),
                  pl.BlockSpec((SWA_HEADS, BLOCK, 2 * BLOCK), lambda n: (0, 0, 0)),
                  pl.BlockSpec(memory_space=pltpu.SMEM)],
        out_specs=[pl.BlockSpec((BLOCK, SWA_W), lambda n: (n, 0)),
                   pl.BlockSpec((BLOCK, LANES), lambda n: (n, 0))],
        out_shape=[jax.ShapeDtypeStruct((s, SWA_W), F32),
                   jax.ShapeDtypeStruct((s, LANES), F32)],
        compiler_params=_cparams(("parallel",)),
    )(qkv, qkv, qkv, qkv, qkv, bias, sink)


def _swa_bwd(qkv, do_bf, delta, lse, bias, sink, bucket):
    s = qkv.shape[0]
    nb = s // BLOCK

    def kern(q_ref, kp_ref, kc_ref, vp_ref, vc_ref, do_ref, dl_ref, lse_ref, bias_ref, sink_ref, bk_ref,
             dq_ref, dk_ref, dv_ref, grb_ref, gsk_ref, dbias_ref, ck_ref, cv_ref, sk_ref):
        n = pl.program_id(0)
        lane = lax.broadcasted_iota(jnp.int32, (1, LANES), 1)

        @pl.when(n == 0)
        def _():
            dbias_ref[...] = jnp.zeros_like(dbias_ref)
            ck_ref[...] = jnp.zeros_like(ck_ref)
            cv_ref[...] = jnp.zeros_like(cv_ref)
            sk_ref[...] = jnp.zeros_like(sk_ref)

        @pl.when(n < nb)
        def _():
            mask = _swa_mask(n)
            q = q_ref[...] * jnp.asarray(QK_SCALE, BF16)
            k = jnp.concatenate([kp_ref[...], kc_ref[...]], axis=0)
            v = jnp.concatenate([vp_ref[...], vc_ref[...]], axis=0)
            do = do_ref[...]
            dl = dl_ref[...]
            lse_all = lse_ref[...]
            dks = [None] * SWA_KV_HEADS
            dvs = [None] * SWA_KV_HEADS
            gsk = jnp.zeros((1, LANES), F32)
            kgs = [k[:, g * HEAD_DIM:(g + 1) * HEAD_DIM] for g in range(SWA_KV_HEADS)]
            vgs = [v[:, g * HEAD_DIM:(g + 1) * HEAD_DIM] for g in range(SWA_KV_HEADS)]
            qhs = [q[:, h * HEAD_DIM:(h + 1) * HEAD_DIM] for h in range(SWA_HEADS)]
            dohs = [do[:, h * HEAD_DIM:(h + 1) * HEAD_DIM] for h in range(SWA_HEADS)]
            raw = [_dot_nt(qhs[h], kgs[h // SWA_GROUP]) for h in range(SWA_HEADS)]
            dps = [_dot_nt(dohs[h], vgs[h // SWA_GROUP]) for h in range(SWA_HEADS)]
            ps, dss = [], []
            for h in range(SWA_HEADS):
                lse_h = lse_all[:, h:h + 1]
                dlt = dl[:, FOX_HEADS + h:FOX_HEADS + h + 1]
                sc = jnp.where(mask, raw[h] + bias_ref[h], NEG)
                p = jnp.exp(sc - lse_h)
                ds = p * (dps[h] - dlt)
                dbias_ref[h] += ds
                p_sink = jnp.exp(sink_ref[0, h] - lse_h)
                gsk = gsk + jnp.where(lane == h, -jnp.sum(p_sink * dlt), 0.0)
                ps.append(p.astype(BF16))
                dss.append(ds.astype(BF16))
            dqs = [_dot(dss[h], kgs[h // SWA_GROUP]) * QK_SCALE for h in range(SWA_HEADS)]
            for h in range(SWA_HEADS):
                g = h // SWA_GROUP
                dk_h = _dot_tn(dss[h], qhs[h])
                dv_h = _dot_tn(ps[h], dohs[h])
                dks[g] = dk_h if dks[g] is None else dks[g] + dk_h
                dvs[g] = dv_h if dvs[g] is None else dvs[g] + dv_h
            dq_ref[...] = jnp.concatenate(dqs, axis=1).astype(BF16)
            sk_ref[...] += jnp.broadcast_to(gsk, sk_ref.shape)
            dk2 = jnp.concatenate(dks, axis=1)
            dv2 = jnp.concatenate(dvs, axis=1)
            dk_ref[...] = (ck_ref[...] + dk2[:BLOCK]).astype(BF16)
            dv_ref[...] = (cv_ref[...] + dv2[:BLOCK]).astype(BF16)
            ck_ref[...] = dk2[BLOCK:]
            cv_ref[...] = dv2[BLOCK:]

        @pl.when(n == nb)
        def _():
            dk_ref[...] = ck_ref[...].astype(BF16)
            dv_ref[...] = cv_ref[...].astype(BF16)
            gsk_ref[...] = sk_ref[...]
            bk = bk_ref[...]
            rowi = lax.broadcasted_iota(jnp.int32, (NUM_BUCKETS, LANES), 0)
            lanei = lax.broadcasted_iota(jnp.int32, (NUM_BUCKETS, LANES), 1)
            out = jnp.zeros((NUM_BUCKETS, LANES), F32)
            for h in range(SWA_HEADS):
                db = dbias_ref[h]
                for b in range(NUM_BUCKETS):
                    val = jnp.sum(jnp.where(bk == b, db, 0.0))
                    out = jnp.where((rowi == b) & (lanei == h), val, out)
            grb_ref[...] = out

    cq, ck, cv = COL_SQ // SWA_W, COL_SK // LANES, COL_SV // LANES
    cur = lambda n: jnp.minimum(n, nb - 1)
    prev = lambda n: jnp.maximum(jnp.minimum(n, nb - 1) - 1, 0)
    kout = lambda n: jnp.maximum(n - 1, 0)
    return pl.pallas_call(
        kern, name="swa_bwd",
        grid=(nb + 1,),
        in_specs=[pl.BlockSpec((BLOCK, SWA_W), lambda n: (cur(n), cq)),
                  pl.BlockSpec((BLOCK, LANES), lambda n: (prev(n), ck)),
                  pl.BlockSpec((BLOCK, LANES), lambda n: (cur(n), ck)),
                  pl.BlockSpec((BLOCK, LANES), lambda n: (prev(n), cv)),
                  pl.BlockSpec((BLOCK, LANES), lambda n: (cur(n), cv)),
                  pl.BlockSpec((BLOCK, SWA_W), lambda n: (cur(n), 1)),
                  pl.BlockSpec((BLOCK, LANES), lambda n: (cur(n), 0)),
                  pl.BlockSpec((BLOCK, LANES), lambda n: (cur(n), 0)),
                  pl.BlockSpec((SWA_HEADS, BLOCK, 2 * BLOCK), lambda n: (0, 0, 0)),
                  pl.BlockSpec(memory_space=pltpu.SMEM),
                  pl.BlockSpec((BLOCK, 2 * BLOCK), lambda n: (0, 0))],
        out_specs=[pl.BlockSpec((BLOCK, SWA_W), lambda n: (cur(n), 0)),
                   pl.BlockSpec((BLOCK, LANES), lambda n: (kout(n), 0)),
                   pl.BlockSpec((BLOCK, LANES), lambda n: (kout(n), 0)),
                   pl.BlockSpec((NUM_BUCKETS, LANES), lambda n: (0, 0)),
                   pl.BlockSpec((8, LANES), lambda n: (0, 0))],
        out_shape=[jax.ShapeDtypeStruct((s, SWA_W), BF16),
                   jax.ShapeDtypeStruct((s, LANES), BF16),
                   jax.ShapeDtypeStruct((s, LANES), BF16),
                   jax.ShapeDtypeStruct((NUM_BUCKETS, LANES), F32),
                   jax.ShapeDtypeStruct((8, LANES), F32)],
        scratch_shapes=[pltpu.VMEM((SWA_HEADS, BLOCK, 2 * BLOCK), F32),
                        pltpu.VMEM((BLOCK, LANES), F32),
                        pltpu.VMEM((BLOCK, LANES), F32),
                        pltpu.VMEM((8, LANES), F32)],
        compiler_params=_cparams(("arbitrary",)),
    )(qkv, qkv, qkv, qkv, qkv, do_bf, delta, lse, bias, sink, bucket)


def _post(x, target, o_fox, o_swa, z, w_o, ln_g, ln_b):
    s = x.shape[0]
    tm = min(256, s)
    nt = s // tm

    def kern(x_ref, t_ref, of_ref, os_ref, z_ref, w_ref, g_ref, b_ref,
             loss_ref, dh_ref, dy_ref, mix_ref, do_ref, dz_ref, dl_ref, gg_ref, gb_ref, lacc_ref):
        step = pl.program_id(0)

        @pl.when(step == 0)
        def _():
            lacc_ref[...] = jnp.zeros_like(lacc_ref)
            gg_ref[...] = jnp.zeros_like(gg_ref)
            gb_ref[...] = jnp.zeros_like(gb_ref)

        o = jnp.concatenate([of_ref[...], os_ref[...]], axis=1)
        zz = z_ref[...]
        sig = 1.0 / (1.0 + jnp.exp(-zz))
        silu = zz * sig
        mixed32 = o * silu
        mixed = mixed32.astype(BF16)
        mix_ref[...] = mixed32.T.astype(BF16)
        w = w_ref[...]
        h = ALPHA * x_ref[...] + _dot(mixed, w)
        mu = jnp.mean(h, axis=1, keepdims=True)
        hc = h - mu
        var = jnp.mean(hc * hc, axis=1, keepdims=True)
        rstd = lax.rsqrt(var + LN_EPS)
        xhat = hc * rstd
        g = g_ref[...]
        err = xhat * g + b_ref[...] - t_ref[...]
        lacc_ref[...] += jnp.broadcast_to(jnp.sum(err * err, axis=0, keepdims=True), lacc_ref.shape)
        dout = err * (1.0 / D_MODEL)
        gg_ref[...] += jnp.broadcast_to(jnp.sum(dout * xhat, axis=0, keepdims=True), gg_ref.shape)
        gb_ref[...] += jnp.broadcast_to(jnp.sum(dout, axis=0, keepdims=True), gb_ref.shape)
        dxh = dout * g
        m1 = jnp.mean(dxh, axis=1, keepdims=True)
        m2 = jnp.mean(dxh * xhat, axis=1, keepdims=True)
        dh = rstd * (dxh - m1 - xhat * m2)
        dh_ref[...] = dh
        dy = dh.astype(BF16)
        dy_ref[...] = dy
        dmix = _dot_nt(dy, w)
        do = dmix * silu
        do_ref[...] = do.astype(BF16)
        dz_ref[...] = (dmix * o * (sig * (1.0 + zz * (1.0 - sig)))).astype(BF16)
        r = lax.broadcasted_iota(jnp.int32, (D_MODEL, LANES), 0) // HEAD_DIM
        c = lax.broadcasted_iota(jnp.int32, (D_MODEL, LANES), 1)
        pick = jnp.where(r == c, 1.0, 0.0).astype(BF16)
        dl_ref[...] = _exact_dot(pick, do * o, False)

        @pl.when(step == nt - 1)
        def _():
            tot = jnp.sum(lacc_ref[0:1, :]) * (0.5 / D_MODEL)
            loss_ref[...] = jnp.broadcast_to(tot, loss_ref.shape)

    row = lambda i: (i, 0)
    fixed = lambda i: (0, 0)
    wide = pl.BlockSpec((tm, D_MODEL), row)
    half = pl.BlockSpec((tm, FOX_W), row)
    return pl.pallas_call(
        kern, name="post",
        grid=(nt,),
        in_specs=[wide, wide, half, half, wide,
                  pl.BlockSpec((D_MODEL, D_MODEL), fixed),
                  pl.BlockSpec((1, D_MODEL), fixed),
                  pl.BlockSpec((1, D_MODEL), fixed)],
        out_specs=[pl.BlockSpec((8, LANES), fixed), wide, wide,
                   pl.BlockSpec((D_MODEL, tm), lambda i: (0, i)), wide, wide,
                   pl.BlockSpec((tm, LANES), row),
                   pl.BlockSpec((8, D_MODEL), fixed), pl.BlockSpec((8, D_MODEL), fixed)],
        out_shape=[jax.ShapeDtypeStruct((8, LANES), F32),
                   jax.ShapeDtypeStruct((s, D_MODEL), F32),
                   jax.ShapeDtypeStruct((s, D_MODEL), BF16),
                   jax.ShapeDtypeStruct((D_MODEL, s), BF16),
                   jax.ShapeDtypeStruct((s, D_MODEL), BF16),
                   jax.ShapeDtypeStruct((s, D_MODEL), BF16),
                   jax.ShapeDtypeStruct((s, LANES), F32),
                   jax.ShapeDtypeStruct((8, D_MODEL), F32),
                   jax.ShapeDtypeStruct((8, D_MODEL), F32)],
        scratch_shapes=[pltpu.VMEM((8, D_MODEL), F32)],
        compiler_params=_cparams(("arbitrary",)),
    )(x, target, o_fox, o_swa, z, w_o, ln_g, ln_b)


def _adamw_math(w, g, m, v):
    m = ADAM_B1 * m + (1.0 - ADAM_B1) * g
    v = ADAM_B2 * v + (1.0 - ADAM_B2) * (g * g)
    m_hat = m / (1.0 - ADAM_B1 ** ADAM_STEP)
    v_hat = v / (1.0 - ADAM_B2 ** ADAM_STEP)
    delta = -ADAM_LR * (m_hat / (jnp.sqrt(v_hat) + ADAM_EPS) + ADAM_WD * w)
    return delta, m, v


def _adamw(w, g, m, v, *, name):
    r, c = w.shape
    tr = min(256, r)

    def kern(w_ref, g_ref, m_ref, v_ref, d_ref, mo_ref, vo_ref):
        d, mn, vn = _adamw_math(w_ref[...], g_ref[...], m_ref[...], v_ref[...])
        d_ref[...] = d
        mo_ref[...] = mn
        vo_ref[...] = vn

    blk = pl.BlockSpec((tr, c), lambda i: (i, 0))
    sds = jax.ShapeDtypeStruct((r, c), F32)
    return pl.pallas_call(
        kern, name=name,
        grid=(r // tr,),
        in_specs=[blk, blk, blk, blk],
        out_specs=[blk, blk, blk],
        out_shape=[sds, sds, sds],
        compiler_params=_cparams(("parallel",)),
    )(w, g, m, v)


def _position():
    x, y, c = lax.axis_index("x"), lax.axis_index("y"), lax.axis_index("c")
    chips = [(1 - x, y), (x, 1 - y), (1 - x, 1 - y)]
    return x, y, c, chips


def _chip_index(cx, cy):
    return 2 * cx + cy


def _gather_weights(w_in_bf, w_o_bf):
    shards = (w_in_bf, w_o_bf)
    n_arr = len(shards)

    def kern(*refs):
        ins, outs = refs[:n_arr], refs[n_arr:2 * n_arr]
        send_sems, recv_sems, local_sems = refs[2 * n_arr:]
        x, y, c, chips = _position()
        me = _chip_index(x, y)
        sibling = (x, y, 1 - c)

        local = [pltpu.make_async_copy(ins[a], outs[a].at[me], local_sems.at[a]) for a in range(n_arr)]
        for cp in local:
            cp.start()

        def half(ref, a):
            rows = shards[a].shape[0] // 2
            return ref.at[pl.ds(c * rows, rows), :]

        def copy(a, k, src, slot, to):
            return pltpu.make_async_remote_copy(
                src_ref=src, dst_ref=half(outs[a].at[slot], a),
                send_sem=send_sems.at[a * 6 + k], recv_sem=recv_sems.at[a * 6 + k],
                device_id=to, device_id_type=MESH)

        first = [copy(a, j, half(ins[a], a), me, (*chip, c)) for a in range(n_arr) for j, chip in enumerate(chips)]
        for cp in first:
            cp.start()
        passed = []
        for a in range(n_arr):
            for j, chip in enumerate(chips):
                slot = _chip_index(*chip)
                copy(a, j, half(ins[a], a), slot, (*chip, c)).wait_recv()
                fwd = copy(a, 3 + j, half(outs[a].at[slot], a), slot, sibling)
                fwd.start()
                passed.append(fwd)
        for a in range(n_arr):
            for j, chip in enumerate(chips):
                slot = _chip_index(*chip)
                rows = shards[a].shape[0] // 2
                dst = outs[a].at[slot].at[pl.ds((1 - c) * rows, rows), :]
                pltpu.make_async_remote_copy(
                    src_ref=dst, dst_ref=dst, send_sem=send_sems.at[a * 6 + 3 + j],
                    recv_sem=recv_sems.at[a * 6 + 3 + j], device_id=sibling, device_id_type=MESH).wait_recv()
        for cp in first + passed:
            cp.wait_send()
        for cp in local:
            cp.wait()

    vmem = pl.BlockSpec(memory_space=pltpu.VMEM)
    return pl.pallas_call(
        kern, name="gather_weights",
        in_specs=[vmem] * n_arr,
        out_specs=[vmem] * n_arr,
        out_shape=[jax.ShapeDtypeStruct((N_CHIPS,) + w.shape, w.dtype) for w in shards],
        scratch_shapes=[pltpu.SemaphoreType.DMA((6 * n_arr,)),
                        pltpu.SemaphoreType.DMA((6 * n_arr,)),
                        pltpu.SemaphoreType.DMA((n_arr,))],
        compiler_params=_cparams(),
    )(*shards)


def _swap_halves(grads):
    n_arr = len(grads)

    def kern(*refs):
        ins = refs[:n_arr]
        owns = refs[n_arr:2 * n_arr]
        gots = refs[2 * n_arr:3 * n_arr]
        send_sems, recv_sems, local_sems = refs[3 * n_arr:]
        x, y, c, _ = _position()
        sibling = (x, y, 1 - c)
        local, remote = [], []
        for a in range(n_arr):
            rows = grads[a].shape[1] // 2
            piece = rows // COPY_PIECES
            for j in range(N_CHIPS):
                for r in range(COPY_PIECES):
                    k = (a * N_CHIPS + j) * COPY_PIECES + r
                    dst_rows = pl.ds(r * piece, piece)
                    local.append(pltpu.make_async_copy(
                        ins[a].at[j, pl.ds(c * rows + r * piece, piece), :],
                        owns[a].at[j, dst_rows, :], local_sems.at[k]))
                    remote.append(pltpu.make_async_remote_copy(
                        src_ref=ins[a].at[j, pl.ds((1 - c) * rows + r * piece, piece), :],
                        dst_ref=gots[a].at[j, dst_rows, :], send_sem=send_sems.at[k], recv_sem=recv_sems.at[k],
                        device_id=sibling, device_id_type=MESH))
        for cp in local + remote:
            cp.start()
        for cp in remote:
            cp.wait()
        for cp in local:
            cp.wait()

    hbm = pl.BlockSpec(memory_space=pltpu.VMEM)
    half = [jax.ShapeDtypeStruct((N_CHIPS, g.shape[1] // 2, g.shape[2]), F32) for g in grads]
    outs = pl.pallas_call(
        kern, name="swap_halves",
        in_specs=[hbm] * n_arr,
        out_specs=[hbm] * (2 * n_arr),
        out_shape=half + half,
        scratch_shapes=[pltpu.SemaphoreType.DMA((n_arr * N_CHIPS * COPY_PIECES,)),
                        pltpu.SemaphoreType.DMA((n_arr * N_CHIPS * COPY_PIECES,)),
                        pltpu.SemaphoreType.DMA((n_arr * N_CHIPS * COPY_PIECES,))],
        compiler_params=_cparams(),
    )(*grads)
    return outs[:n_arr], outs[n_arr:]


def _scatter_to_owners(parts):
    n_arr = len(parts)

    def kern(*refs):
        ins = refs[:n_arr]
        outs = refs[n_arr:2 * n_arr]
        send_sems, recv_sems, local_sems = refs[2 * n_arr:]
        x, y, c, chips = _position()
        me = _chip_index(x, y)
        local = [pltpu.make_async_copy(ins[a].at[me], outs[a].at[me], local_sems.at[a]) for a in range(n_arr)]
        for cp in local:
            cp.start()
        sends = []
        for a in range(n_arr):
            for j, chip in enumerate(chips):
                sends.append(pltpu.make_async_remote_copy(
                    src_ref=ins[a].at[_chip_index(*chip)], dst_ref=outs[a].at[me],
                    send_sem=send_sems.at[a * 3 + j], recv_sem=recv_sems.at[a * 3 + j],
                    device_id=(*chip, c), device_id_type=MESH))
        for cp in sends:
            cp.start()
        for a in range(n_arr):
            for j, chip in enumerate(chips):
                slot = outs[a].at[_chip_index(*chip)]
                pltpu.make_async_remote_copy(
                    src_ref=slot, dst_ref=slot, send_sem=send_sems.at[a * 3 + j],
                    recv_sem=recv_sems.at[a * 3 + j], device_id=(*chip, c), device_id_type=MESH).wait_recv()
        for cp in sends:
            cp.wait_send()
        for cp in local:
            cp.wait()

    hbm = pl.BlockSpec(memory_space=pltpu.VMEM)
    return pl.pallas_call(
        kern, name="scatter_to_owners",
        in_specs=[hbm] * n_arr,
        out_specs=[hbm] * n_arr,
        out_shape=[jax.ShapeDtypeStruct(p.shape, p.dtype) for p in parts],
        scratch_shapes=[pltpu.SemaphoreType.DMA((3 * n_arr,)),
                        pltpu.SemaphoreType.DMA((3 * n_arr,)),
                        pltpu.SemaphoreType.DMA((n_arr,))],
        compiler_params=_cparams(),
    )(*parts)


def _join_halves(halves):
    n_arr = len(halves)

    def kern(*refs):
        ins = refs[:n_arr]
        outs = refs[n_arr:2 * n_arr]
        send_sems, recv_sems, local_sems = refs[2 * n_arr:]
        x, y, c, _ = _position()
        sibling = (x, y, 1 - c)
        local, remote = [], []
        for a in range(n_arr):
            rows = halves[a].shape[0]
            piece = rows // COPY_PIECES
            for r in range(COPY_PIECES):
                k = a * COPY_PIECES + r
                src = ins[a].at[pl.ds(r * piece, piece), :]
                dst = outs[a].at[pl.ds(c * rows + r * piece, piece), :]
                local.append(pltpu.make_async_copy(src, dst, local_sems.at[k]))
                remote.append(pltpu.make_async_remote_copy(
                    src_ref=src, dst_ref=dst, send_sem=send_sems.at[k], recv_sem=recv_sems.at[k],
                    device_id=sibling, device_id_type=MESH))
        for cp in local + remote:
            cp.start()
        for a in range(n_arr):
            rows = halves[a].shape[0]
            piece = rows // COPY_PIECES
            for r in range(COPY_PIECES):
                k = a * COPY_PIECES + r
                theirs = outs[a].at[pl.ds((1 - c) * rows + r * piece, piece), :]
                pltpu.make_async_remote_copy(
                    src_ref=theirs, dst_ref=theirs, send_sem=send_sems.at[k], recv_sem=recv_sems.at[k],
                    device_id=sibling, device_id_type=MESH).wait_recv()
        for cp in remote:
            cp.wait_send()
        for cp in local:
            cp.wait()

    hbm = pl.BlockSpec(memory_space=pltpu.VMEM)
    return pl.pallas_call(
        kern, name="join_halves",
        in_specs=[hbm] * n_arr,
        out_specs=[hbm] * n_arr,
        out_shape=[jax.ShapeDtypeStruct((2 * h.shape[0], h.shape[1]), F32) for h in halves],
        scratch_shapes=[pltpu.SemaphoreType.DMA((n_arr * COPY_PIECES,)),
                        pltpu.SemaphoreType.DMA((n_arr * COPY_PIECES,)),
                        pltpu.SemaphoreType.DMA((n_arr * COPY_PIECES,))],
        compiler_params=_cparams(),
    )(*halves)


def _add2(a, b, *, name):
    n, r, c = a.shape
    tr = min(256, r)

    def kern(a_ref, b_ref, o_ref):
        o_ref[...] = (a_ref[...] + b_ref[...]).astype(BF16)

    blk = pl.BlockSpec((1, tr, c), lambda j, i: (j, i, 0))
    return pl.pallas_call(
        kern, name=name,
        grid=(n, r // tr),
        in_specs=[blk, blk],
        out_specs=blk,
        out_shape=jax.ShapeDtypeStruct(a.shape, BF16),
        compiler_params=_cparams(("parallel", "parallel")),
    )(a, b)


def _sum4(a, *, name):
    n, r, c = a.shape
    tr = min(256, r)

    def kern(a_ref, o_ref):
        f = lambda j: a_ref[j].astype(F32)
        o_ref[...] = ((f(0) + f(1)) + f(2)) + f(3)

    return pl.pallas_call(
        kern, name=name,
        grid=(r // tr,),
        in_specs=[pl.BlockSpec((n, tr, c), lambda i: (0, i, 0))],
        out_specs=pl.BlockSpec((tr, c), lambda i: (i, 0)),
        out_shape=jax.ShapeDtypeStruct((r, c), F32),
        compiler_params=_cparams(("parallel",)),
    )(a)


def _small_allreduce_adamw(g, w, m, v):
    def kern(g_ref, w_ref, m_ref, v_ref, gs_ref, d_ref, mo_ref, vo_ref, buf_ref, send_sems, recv_sems):
        x, y, c, _ = _position()
        me = 4 * x + 2 * y + c
        buf_ref[me] = g_ref[...]
        peers = [(x, y, 1 - c)] + [(px, py, pc) for px, py in _position()[3] for pc in (c, 1 - c)]
        sends = []
        for k, peer in enumerate(peers):
            sends.append(pltpu.make_async_remote_copy(
                src_ref=g_ref, dst_ref=buf_ref.at[me], send_sem=send_sems.at[k], recv_sem=recv_sems.at[k],
                device_id=peer, device_id_type=MESH))
        for cp in sends:
            cp.start()
        for k, (px, py, pc) in enumerate(peers):
            slot = buf_ref.at[4 * px + 2 * py + pc]
            pltpu.make_async_remote_copy(
                src_ref=slot, dst_ref=slot, send_sem=send_sems.at[k], recv_sem=recv_sems.at[k],
                device_id=(px, py, pc), device_id_type=MESH).wait_recv()
        for cp in sends:
            cp.wait_send()
        tot = buf_ref[0]
        for d in range(1, N_DEV):
            tot = tot + buf_ref[d]
        gs_ref[...] = tot
        delta, mn, vn = _adamw_math(w_ref[...], tot, m_ref[...], v_ref[...])
        d_ref[...] = delta
        mo_ref[...] = mn
        vo_ref[...] = vn

    vm = pl.BlockSpec(memory_space=pltpu.VMEM)
    sds = jax.ShapeDtypeStruct((SMALL_ROWS, LANES), F32)
    return pl.pallas_call(
        kern, name="small_allreduce_adamw",
        in_specs=[vm] * 4,
        out_specs=[vm] * 4,
        out_shape=[sds] * 4,
        scratch_shapes=[pltpu.VMEM((N_DEV, SMALL_ROWS, LANES), F32),
                        pltpu.SemaphoreType.DMA((N_DEV - 1,)),
                        pltpu.SemaphoreType.DMA((N_DEV - 1,))],
    )(g, w, m, v)


def _to_padded_cols(w):
    pad = jnp.zeros((w.shape[0], N_C - FOX_HEADS), w.dtype)
    return jnp.concatenate([w[:, 0:1536], w[:, 2056:2824], w[:, 1536:1544], pad,
                            w[:, 1544:2056], w[:, 2824:3336]], axis=1)


def _from_padded_cols(g):
    return jnp.concatenate([g[:, 0:1536], g[:, OFF_C:OFF_C + FOX_HEADS], g[:, OFF_B:OFF_B + FOX_W],
                            g[:, 1536:N_A], g[:, OFF_B + FOX_W:N_PAD]], axis=1)


def _pack_small(b_f, rel_bias, sink, ln_g, ln_b):
    row = lambda v: jnp.pad(v.reshape(1, -1), ((0, 0), (0, LANES - v.size)))
    return jnp.concatenate([ln_g.reshape(8, LANES), ln_b.reshape(8, LANES), rel_bias.reshape(2, LANES),
                            row(b_f), row(sink), jnp.zeros((4, LANES), F32)], axis=0)


def _unpack_small(p):
    ln_g = p[0:8].reshape(1, D_MODEL)
    ln_b = p[8:16].reshape(1, D_MODEL)
    rel_bias = p[16:18].reshape(NUM_BUCKETS, SWA_HEADS)
    b_f = p[18:19, :FOX_HEADS]
    sink = p[19:20, :SWA_HEADS]
    return b_f, rel_bias, sink, ln_g, ln_b


def _fox_rows(a):
    return a[:, :FOX_HEADS].T.reshape(FOX_HEADS, 1, a.shape[0])


def kernel(x, w_in, b_f, rel_bias, sink, w_o, ln_g, ln_b, loss_target, m_w_in, m_b_f, m_rel_bias, m_sink, m_w_o, m_ln_g, m_ln_b, v_w_in, v_b_f, v_rel_bias, v_sink, v_w_o, v_ln_g, v_ln_b):
    x2 = x[0]
    tgt = loss_target[0]
    s = x2.shape[0]
    w_in2, w_o2 = w_in[0], w_o[0]

    shard_cols = D_IN // N_CHIPS
    col_pad = ((0, 0), (0, SHARD_PAD - shard_cols))
    w_in_all, w_o_all = _gather_weights(jnp.pad(w_in2.astype(BF16), col_pad), w_o2.astype(BF16))
    w_full = jnp.concatenate([w_in_all[j, :, :shard_cols] for j in range(N_CHIPS)], axis=1)
    w_pad = _to_padded_cols(w_full)
    w_o_full = w_o_all.reshape(D_MODEL, D_MODEL)

    qkv = _matmul_nn(x2, w_pad, n_off=0, n_out=N_A, tm=512, tn=768, out_dtype=BF16, name="proj_qkv")
    z = _matmul_nn(x2, w_pad, n_off=OFF_B, n_out=N_B, tm=512, tn=512, out_dtype=F32, name="proj_gate")
    ffp = _matmul_nn(x2, w_pad, n_off=OFF_C, n_out=N_C, tm=512, tn=N_C, out_dtype=F32, name="proj_forget")
    bfp = jnp.pad(b_f, ((0, 0), (0, LANES - FOX_HEADS)))
    cum = _cum_fwd(ffp, bfp)
    cum_t3 = _fox_rows(cum)
    vt = qkv[:, COL_FV:COL_FV + FOX_W].T
    o_fox, lse_t3 = _fox_fwd(qkv, vt, cum_t3, cum)
    bucket = jnp.asarray(_bucket_table())
    bias = _swa_bias(rel_bias, bucket)
    o_swa, lse_swa = _swa_fwd(qkv, bias, sink)

    loss8, dh, dy, mixed_t, do_bf, dz, delta, gg8, gb8 = _post(
        x2, tgt, o_fox, o_swa, z, w_o_full, ln_g, ln_b)
    loss = lax.psum(loss8[0, 0], ("x", "y", "c"))
    grad_w_o_full = _matmul_acc(mixed_t, dy, tm=1024, tn=512, tk=1024, name="grad_w_o")

    delta_t3 = _fox_rows(delta)
    dqt_fox, dk_fox, dv_fox, dcum_k, dcum_q = _fox_bwd(qkv, do_bf, cum_t3, cum, lse_t3, delta_t3)
    dcum_q = jnp.pad(dcum_q.reshape(FOX_HEADS, s).T, ((0, 0), (0, LANES - FOX_HEADS)))
    dff, gbf8 = _cum_bwd(dcum_k, dcum_q, ffp, bfp)
    dq_swa, dk_swa, dv_swa, grb, gsk8 = _swa_bwd(qkv, do_bf, delta, lse_swa, bias, sink, bucket)

    dq_fox = dqt_fox.T.astype(BF16)
    d_misc = jnp.concatenate([dk_swa, dv_swa, dff], axis=1)
    pieces = [dq_fox, dk_fox, dv_fox, dq_swa, d_misc, dz]
    grad_x = _grad_x_matmul(pieces, w_pad, dh, tm=512, tn=512, name="grad_x")
    blocks = [(p, 0) for p in pieces[:-1]] + [(dz, 0), (dz, 1)]
    grad_w_pad = _grad_w_matmul(x2.T.astype(BF16), blocks, tk=1024, name="grad_w_in")
    grad_w_in_full = _from_padded_cols(grad_w_pad)

    g_in4 = jnp.stack([jnp.pad(grad_w_in_full[:, j * shard_cols:(j + 1) * shard_cols], col_pad)
                       for j in range(N_CHIPS)])
    g_o4 = grad_w_o_full.reshape(N_CHIPS, D_MODEL // N_CHIPS, D_MODEL)
    owns, gots = _swap_halves([g_in4, g_o4])
    parts = [_add2(owns[0], gots[0], name="pair_sum_w_in"), _add2(owns[1], gots[1], name="pair_sum_w_o")]
    slabs = _scatter_to_owners(parts)
    halves = [_sum4(slabs[0], name="chip_sum_w_in"), _sum4(slabs[1], name="chip_sum_w_o")]
    g_w_in, g_w_o = _join_halves(halves)
    g_w_in = g_w_in[:, :shard_cols]

    d_w_in, nm_w_in, nv_w_in = _adamw(w_in2, g_w_in, m_w_in[0], v_w_in[0], name="adamw_w_in")
    d_w_o, nm_w_o, nv_w_o = _adamw(w_o2, g_w_o, m_w_o[0], v_w_o[0], name="adamw_w_o")

    g_small = _pack_small(gbf8[0:1, :FOX_HEADS], grb[:, :SWA_HEADS], gsk8[0:1, :SWA_HEADS], gg8[0:1], gb8[0:1])
    w_small = _pack_small(b_f, rel_bias, sink, ln_g, ln_b)
    m_small = _pack_small(m_b_f, m_rel_bias, m_sink, m_ln_g, m_ln_b)
    v_small = _pack_small(v_b_f, v_rel_bias, v_sink, v_ln_g, v_ln_b)
    gs, ds, ms, vs = _small_allreduce_adamw(g_small, w_small, m_small, v_small)
    g_bf, g_rb, g_sk, g_lg, g_lb = _unpack_small(gs)
    d_bf, d_rb, d_sk, d_lg, d_lb = _unpack_small(ds)
    m_bf, m_rb, m_sk, m_lg, m_lb = _unpack_small(ms)
    v_bf, v_rb, v_sk, v_lg, v_lb = _unpack_small(vs)

    e = lambda a: a[None]
    return (loss, e(grad_x),
            e(g_w_in), g_bf, g_rb, g_sk, e(g_w_o), g_lg, g_lb,
            e(d_w_in), d_bf, d_rb, d_sk, e(d_w_o), d_lg, d_lb,
            e(nm_w_in), m_bf, m_rb, m_sk, e(nm_w_o), m_lg, m_lb,
            e(nv_w_in), v_bf, v_rb, v_sk, e(nv_w_o), v_lg, v_lb)
```

```python
import functools
import math

import numpy as np
import jax
import jax.numpy as jnp
from jax import lax
from jax.experimental import pallas as pl
from jax.experimental.pallas import tpu as pltpu

F32 = jnp.float32
BF16 = jnp.bfloat16

D_MODEL = 1024
HEAD_DIM = 64
FOX_HEADS = 8
SWA_HEADS = 8
SWA_KV_HEADS = 2
SWA_GROUP = 4
FOX_W = 512
SWA_W = 512
SWA_KV_W = 128
BLOCK = 128
NUM_BUCKETS = 32
MAX_DISTANCE = 128
LN_EPS = 1e-5
NEG = -1e30
ALPHA = 2.0 ** 0.25
QK_SCALE = 0.125

ADAM_LR = 0.001
ADAM_B1 = 0.9
ADAM_B2 = 0.999
ADAM_EPS = 1e-08
ADAM_WD = 0.01
ADAM_STEP = 10

D_IN = 3336
SHARD_PAD = 896
N_A = 2304
N_C = 256
N_B = 1024
OFF_C = N_A
OFF_B = N_A + N_C
N_PAD = N_A + N_C + N_B
COL_FK, COL_FV, COL_SQ, COL_SK, COL_SV = 512, 1024, 1536, 2048, 2176

LANES = 128
FOX_T = 256
FOX_REF = 512
VMEM_LIMIT = 56 * 1024 * 1024

MESH = pl.DeviceIdType.MESH
N_CHIPS = 4
N_DEV = 8
SMALL_ROWS = 24
COPY_PIECES = 4


def _cparams(sem=None):
    return pltpu.CompilerParams(dimension_semantics=sem, vmem_limit_bytes=VMEM_LIMIT)


def _split3(x):
    hi = x.astype(BF16)
    r = x - hi.astype(F32)
    mid = r.astype(BF16)
    lo = (r - mid.astype(F32)).astype(BF16)
    return hi, mid, lo


def _dot(a, b):
    return jnp.dot(a, b, preferred_element_type=F32)


def _dot_nt(a, b):
    return lax.dot_general(a, b, (((1,), (1,)), ((), ())), preferred_element_type=F32)


def _dot_tn(a, b):
    return lax.dot_general(a, b, (((0,), (0,)), ((), ())), preferred_element_type=F32)


def _project(x, w_pad):
    s, k = x.shape
    tm = 512
    chunk = 768

    def kern(x_ref, w_ref, qkv_ref, ff_ref, z_ref, xt_ref):
        xf = x_ref[...]
        xb = xf.astype(BF16)
        xt_ref[...] = xf.T.astype(BF16)
        for c0 in range(0, N_A, chunk):
            qkv_ref[:, c0:c0 + chunk] = _dot(xb, w_ref[:, c0:c0 + chunk]).astype(BF16)
        ff_ref[...] = _dot(xb, w_ref[:, OFF_C:OFF_C + N_C])
        for c0 in range(0, N_B, 512):
            z_ref[:, c0:c0 + 512] = _dot(xb, w_ref[:, OFF_B + c0:OFF_B + c0 + 512])

    row = lambda i: (i, 0)
    return pl.pallas_call(
        kern, name="project",
        grid=(s // tm,),
        in_specs=[pl.BlockSpec((tm, k), row),
                  _resident((k, N_PAD), lambda i: (0, 0))],
        out_specs=[pl.BlockSpec((tm, N_A), row),
                   pl.BlockSpec((tm, N_C), row),
                   pl.BlockSpec((tm, N_B), row),
                   pl.BlockSpec((k, tm), lambda i: (0, i))],
        out_shape=[jax.ShapeDtypeStruct((s, N_A), BF16),
                   jax.ShapeDtypeStruct((s, N_C), F32),
                   jax.ShapeDtypeStruct((s, N_B), F32),
                   jax.ShapeDtypeStruct((k, s), BF16)],
        compiler_params=_cparams(("parallel",)),
    )(x, w_pad)


def _grad_x_matmul(pieces, w_pad, dh, *, tm, tn, name):
    m = dh.shape[0]
    n, k = w_pad.shape
    widths = [p.shape[1] for p in pieces]
    offs = [sum(widths[:i]) for i in range(len(pieces))]
    assert sum(widths) == k

    def kern(*refs):
        p_refs, (b_ref, dh_ref, o_ref) = refs[:len(pieces)], refs[len(pieces):]
        acc = ALPHA * dh_ref[...]
        for p_ref, off, width in zip(p_refs, offs, widths):
            acc = acc + _dot_nt(p_ref[...], b_ref[:, off:off + width])
        o_ref[...] = acc

    return pl.pallas_call(
        kern, name=name,
        grid=(n // tn, m // tm),
        in_specs=[pl.BlockSpec((tm, w), lambda j, i: (i, 0)) for w in widths]
        + [pl.BlockSpec((tn, k), lambda j, i: (j, 0)),
           pl.BlockSpec((tm, tn), lambda j, i: (i, j))],
        out_specs=pl.BlockSpec((tm, tn), lambda j, i: (i, j)),
        out_shape=jax.ShapeDtypeStruct((m, n), F32),
        compiler_params=_cparams(("parallel", "parallel")),
    )(*pieces, w_pad, dh)


def _grad_w_matmul(xt, blocks, *, tk, name):
    m, s = xt.shape
    tn = 512
    nb = len(blocks)

    def kern(a_ref, *refs):
        b_refs, o_ref = refs[:nb], refs[nb]
        j = pl.program_id(0)

        @pl.when(pl.program_id(1) == 0)
        def _():
            o_ref[...] = jnp.zeros_like(o_ref)
        for blk in range(nb):
            @pl.when(j == blk)
            def _(blk=blk):
                o_ref[...] += _dot(a_ref[...], b_refs[blk][...])

    def b_spec(blk, col):
        return pl.BlockSpec((tk, tn), lambda j, k: (jnp.where(j == blk, k, 0), col))

    return pl.pallas_call(
        kern, name=name,
        grid=(nb, s // tk),
        in_specs=[pl.BlockSpec((m, tk), lambda j, k: (0, k))]
        + [b_spec(blk, col) for blk, (_, col) in enumerate(blocks)],
        out_specs=pl.BlockSpec((m, tn), lambda j, k: (0, j)),
        out_shape=jax.ShapeDtypeStruct((m, nb * tn), F32),
        compiler_params=_cparams(("parallel", "arbitrary")),
    )(xt, *[arr for arr, _ in blocks])


def _matmul_acc(at, b, *, tm, tn, tk, name):
    m, s = at.shape
    n = b.shape[1]

    def kern(a_ref, b_ref, o_ref):
        @pl.when(pl.program_id(2) == 0)
        def _():
            o_ref[...] = jnp.zeros_like(o_ref)
        o_ref[...] += _dot(a_ref[...], b_ref[...])

    return pl.pallas_call(
        kern, name=name,
        grid=(m // tm, n // tn, s // tk),
        in_specs=[pl.BlockSpec((tm, tk), lambda i, j, k: (i, k)),
                  pl.BlockSpec((tk, tn), lambda i, j, k: (k, j))],
        out_specs=pl.BlockSpec((tm, tn), lambda i, j, k: (i, j)),
        out_shape=jax.ShapeDtypeStruct((m, n), F32),
        compiler_params=_cparams(("parallel", "parallel", "arbitrary")),
    )(at, b)


def _tri(n, lower):
    r = lax.broadcasted_iota(jnp.int32, (n, n), 0)
    c = lax.broadcasted_iota(jnp.int32, (n, n), 1)
    keep = (c <= r) if lower else (c >= r)
    return jnp.where(keep, 1.0, 0.0).astype(BF16)


def _exact_dot(mat_bf16, x_f32, left):
    out = None
    for piece in _split3(x_f32):
        t = _dot(mat_bf16, piece) if left else _dot(piece, mat_bf16)
        out = t if out is None else out + t
    return out


def _log_sigmoid(z):
    return jnp.minimum(z, 0.0) - jnp.log(1.0 + jnp.exp(-jnp.abs(z)))


def _cum_fwd(ffp, bfp):
    s = ffp.shape[0]
    t = min(512, s)

    def kern(ff_ref, b_ref, cum_ref, carry_ref):
        @pl.when(pl.program_id(0) == 0)
        def _():
            carry_ref[...] = jnp.zeros_like(carry_ref)
        lane = lax.broadcasted_iota(jnp.int32, (1, LANES), 1)
        lf = _log_sigmoid(ff_ref[...] + b_ref[...])
        lf = jnp.where(lane < FOX_HEADS, lf, 0.0)
        cum = _exact_dot(_tri(t, True), lf, True) + carry_ref[0:1, :]
        cum_ref[...] = cum
        carry_ref[...] = jnp.broadcast_to(cum[t - 1:t, :], carry_ref.shape)

    return pl.pallas_call(
        kern, name="cum_fwd",
        grid=(s // t,),
        in_specs=[pl.BlockSpec((t, LANES), lambda i: (i, 0)),
                  pl.BlockSpec((1, LANES), lambda i: (0, 0))],
        out_specs=pl.BlockSpec((t, LANES), lambda i: (i, 0)),
        out_shape=jax.ShapeDtypeStruct((s, LANES), F32),
        scratch_shapes=[pltpu.VMEM((8, LANES), F32)],
        compiler_params=_cparams(("arbitrary",)),
    )(ffp, bfp)


def _cum_bwd(dcum_k, dcum_q, ffp, bfp):
    s = dcum_k.shape[0]
    t = min(512, s)
    nb = s // t

    def kern(dck_ref, dcq_ref, ff_ref, b_ref, dff_ref, gb_ref, carry_ref):
        @pl.when(pl.program_id(0) == 0)
        def _():
            carry_ref[...] = jnp.zeros_like(carry_ref)
            gb_ref[...] = jnp.zeros_like(gb_ref)
        lane = lax.broadcasted_iota(jnp.int32, (1, LANES), 1)
        dlf = _exact_dot(_tri(t, False), dck_ref[...] + dcq_ref[...], True) + carry_ref[0:1, :]
        carry_ref[...] = jnp.broadcast_to(dlf[0:1, :], carry_ref.shape)
        z = ff_ref[...] + b_ref[...]
        dff = jnp.where(lane < FOX_HEADS, dlf / (1.0 + jnp.exp(z)), 0.0)
        gb_ref[...] += jnp.broadcast_to(jnp.sum(dff, axis=0, keepdims=True), gb_ref.shape)
        dff_ref[...] = jnp.concatenate([dff, jnp.zeros_like(dff)], axis=1).astype(BF16)

    return pl.pallas_call(
        kern, name="cum_bwd",
        grid=(nb,),
        in_specs=[pl.BlockSpec((t, LANES), lambda i: (nb - 1 - i, 0)),
                  pl.BlockSpec((t, LANES), lambda i: (nb - 1 - i, 0)),
                  pl.BlockSpec((t, LANES), lambda i: (nb - 1 - i, 0)),
                  pl.BlockSpec((1, LANES), lambda i: (0, 0))],
        out_specs=[pl.BlockSpec((t, N_C), lambda i: (nb - 1 - i, 0)),
                   pl.BlockSpec((8, LANES), lambda i: (0, 0))],
        out_shape=[jax.ShapeDtypeStruct((s, N_C), BF16),
                   jax.ShapeDtypeStruct((8, LANES), F32)],
        scratch_shapes=[pltpu.VMEM((8, LANES), F32)],
        compiler_params=_cparams(("arbitrary",)),
    )(dcum_k, dcum_q, ffp, bfp)


def _resident(shape, index_map):
    return pl.BlockSpec(shape, index_map, pipeline_mode=pl.Buffered(1))


def _fox_fwd(qkv, vt, cum_t3, cum):
    s = qkv.shape[0]
    tk = min(FOX_T, s)
    tq = FOX_REF
    nq = s // tq
    nh = FOX_HEADS
    diag_tiles = tq // tk

    def kern(q_ref, k_ref, vt_ref, ct_ref, c_ref, o_ref, lse_ref, m_ref, l_ref, acc_ref, u_ref):
        i = pl.program_id(0)
        lane = lax.broadcasted_iota(jnp.int32, (1, LANES), 1)
        krow = lax.broadcasted_iota(jnp.int32, (tk, tq), 0)
        qcol = lax.broadcasted_iota(jnp.int32, (tk, tq), 1)
        q0 = pl.multiple_of(i * tq, tq)
        qts, crefs = [], []
        for h in range(nh):
            p, a = divmod(h, 2)
            q2 = q_ref[:, p * LANES:(p + 1) * LANES] * jnp.asarray(QK_SCALE, BF16)
            sel = (lane < HEAD_DIM) if a == 0 else (lane >= HEAD_DIM)
            qts.append(jnp.where(sel, q2, jnp.zeros_like(q2)).astype(F32).T.astype(BF16))
            crefs.append(ct_ref[h, :, pl.ds(q0, LANES)][:, 0:1])
        m_ref[...] = jnp.full(m_ref.shape, NEG, F32)
        l_ref[...] = jnp.zeros_like(l_ref)
        acc_ref[...] = jnp.zeros_like(acc_ref)

        def tile(j, diag):
            k0 = pl.multiple_of(j * tk, tk)
            cb = c_ref[pl.ds(k0, tk), :]
            sts = [_dot(k_ref[pl.ds(k0, tk), (h // 2) * LANES:(h // 2 + 1) * LANES], qts[h]) for h in range(nh)]
            tile_max = []
            for h in range(nh):
                u = sts[h] - (cb[:, h:h + 1] - crefs[h])
                if diag is not None:
                    u = jnp.where(krow + diag * tk <= qcol, u, NEG)
                u_ref[h] = u
                tile_max.append(jnp.max(u, axis=0, keepdims=True))
            pts, scales = [], []
            for h in range(nh):
                m_old = m_ref[h]
                m_new = jnp.maximum(m_old, tile_max[h])
                scale = jnp.exp(m_old - m_new)
                p = jnp.exp(u_ref[h] - m_new)
                l_ref[h] = scale * l_ref[h] + jnp.sum(p, axis=0, keepdims=True)
                m_ref[h] = m_new
                pts.append(p.astype(BF16))
                scales.append(scale)
            for h in range(nh):
                vth = vt_ref[h * HEAD_DIM:(h + 1) * HEAD_DIM, pl.ds(k0, tk)]
                acc_ref[h] = scales[h] * acc_ref[h] + _dot(vth, pts[h])

        def body(j, c):
            tile(j, None)
            return c
        lax.fori_loop(0, i * diag_tiles, body, 0)
        for d in range(diag_tiles):
            tile(i * diag_tiles + d, d)

        for p in range(nh // 2):
            ot = jnp.concatenate([acc_ref[2 * p + a] * (1.0 / l_ref[2 * p + a]) for a in range(2)], axis=0)
            o_ref[:, p * LANES:(p + 1) * LANES] = ot.T
        for h in range(nh):
            lse_ref[h, :, pl.ds(q0, tq)] = m_ref[h] + jnp.log(l_ref[h])

    return pl.pallas_call(
        kern, name="fox_fwd",
        grid=(nq,),
        in_specs=[pl.BlockSpec((tq, FOX_W), lambda i: (i, 0)),
                  _resident((s, FOX_W), lambda i: (0, COL_FK // FOX_W)),
                  _resident((FOX_W, s), lambda i: (0, 0)),
                  _resident((nh, 1, s), lambda i: (0, 0, 0)),
                  _resident((s, LANES), lambda i: (0, 0))],
        out_specs=[pl.BlockSpec((tq, FOX_W), lambda i: (i, 0)),
                   pl.BlockSpec((nh, 1, s), lambda i: (0, 0, 0))],
        out_shape=[jax.ShapeDtypeStruct((s, FOX_W), F32),
                   jax.ShapeDtypeStruct((nh, 1, s), F32)],
        scratch_shapes=[pltpu.VMEM((nh, 1, tq), F32),
                        pltpu.VMEM((nh, 1, tq), F32),
                        pltpu.VMEM((nh, HEAD_DIM, tq), F32),
                        pltpu.VMEM((nh, tk, tq), F32)],
        compiler_params=_cparams(("arbitrary",)),
    )(qkv, qkv, vt, cum_t3, cum)


def _fox_bwd(qkv, do_bf, cum_t3, cum, lse_t3, delta_t3):
    s = qkv.shape[0]
    t = min(FOX_T, s)
    nq = s // t
    nh = FOX_HEADS
    npair = nh // 2

    def kern(q_ref, do_ref, k_ref, v_ref, ct_ref, c_ref, lse_ref, dl_ref,
             dqt_ref, dk_ref, dv_ref, dc_ref, dcq_ref, accv_ref, acck_ref, accd_ref):
        kj = pl.program_id(0)
        lane = lax.broadcasted_iota(jnp.int32, (1, LANES), 1)
        krow = lax.broadcasted_iota(jnp.int32, (t, t), 0)
        qcol = lax.broadcasted_iota(jnp.int32, (t, t), 1)
        causal = krow <= qcol
        sels = [lane < HEAD_DIM, lane >= HEAD_DIM]

        @pl.when(kj == 0)
        def _():
            dqt_ref[...] = jnp.zeros_like(dqt_ref)
            dcq_ref[...] = jnp.zeros_like(dcq_ref)

        accv_ref[...] = jnp.zeros_like(accv_ref)
        acck_ref[...] = jnp.zeros_like(acck_ref)
        accd_ref[...] = jnp.zeros_like(accd_ref)
        cb = c_ref[...]
        k2s, v2s, kts = [], [], []
        for p in range(npair):
            k2 = k_ref[:, p * LANES:(p + 1) * LANES]
            k2s.append(k2)
            v2s.append(v_ref[:, p * LANES:(p + 1) * LANES])
            kt = k2.astype(F32).T * QK_SCALE
            kts.append(kt[:HEAD_DIM].astype(BF16))
            kts.append(kt[HEAD_DIM:].astype(BF16))
        css = [cb[:, h:h + 1] for h in range(nh)]

        def tile(i, masked):
            q0 = pl.multiple_of(i * t, t)
            r0 = pl.multiple_of((i // (FOX_REF // t)) * FOX_REF, FOX_REF)
            sts, dpts, qms, doms = [], [], [], []
            for h in range(nh):
                p, a = divmod(h, 2)
                qi = q_ref[pl.ds(q0, t), p * LANES:(p + 1) * LANES] * jnp.asarray(QK_SCALE, BF16)
                doi = do_ref[pl.ds(q0, t), p * LANES:(p + 1) * LANES]
                qm = jnp.where(sels[a], qi, jnp.zeros_like(qi))
                dom = jnp.where(sels[a], doi, jnp.zeros_like(doi))
                qms.append(qm)
                doms.append(dom)
                sts.append(_dot_nt(k2s[p], qm))
                dpts.append(_dot_nt(v2s[p], dom))
            pts, dsts = [], []
            for h in range(nh):
                cref = ct_ref[h, :, pl.ds(r0, LANES)][:, 0:1]
                pt = jnp.exp(sts[h] - (css[h] - cref) - lse_ref[h, :, pl.ds(q0, t)])
                if masked:
                    pt = jnp.where(causal, pt, 0.0)
                ds32 = pt * (dpts[h] - dl_ref[h, :, pl.ds(q0, t)])
                part = ds32[:, 0:LANES]
                for c in range(1, t // LANES):
                    part = part + ds32[:, c * LANES:(c + 1) * LANES]
                accd_ref[h] += part
                dcq_ref[h, :, pl.ds(q0, t)] += jnp.sum(ds32, axis=0, keepdims=True)
                pts.append(pt.astype(BF16))
                dsts.append(ds32.astype(BF16))
            for p in range(npair):
                ha, hb = 2 * p, 2 * p + 1
                accv_ref[p] += _dot(pts[ha], doms[ha]) + _dot(pts[hb], doms[hb])
                acck_ref[p] += _dot(dsts[ha], qms[ha]) + _dot(dsts[hb], qms[hb])
            for h in range(nh):
                dqt_ref[h * HEAD_DIM:(h + 1) * HEAD_DIM, pl.ds(q0, t)] += _dot(kts[h], dsts[h])

        tile(kj, True)

        def body(i, c):
            tile(i, False)
            return c
        lax.fori_loop(kj + 1, nq, body, 0)

        dc = jnp.zeros((t, LANES), F32)
        for h in range(nh):
            dc = jnp.where(lane == h, -jnp.sum(accd_ref[h], axis=1, keepdims=True), dc)
        dc_ref[...] = dc
        for p in range(npair):
            dv_ref[:, p * LANES:(p + 1) * LANES] = accv_ref[p].astype(BF16)
            dk_ref[:, p * LANES:(p + 1) * LANES] = acck_ref[p].astype(BF16)

    whole = lambda kj: (0, 0, 0)
    return pl.pallas_call(
        kern, name="fox_bwd",
        grid=(nq,),
        in_specs=[_resident((s, FOX_W), lambda kj: (0, 0)),
                  _resident((s, FOX_W), lambda kj: (0, 0)),
                  pl.BlockSpec((t, FOX_W), lambda kj: (kj, COL_FK // FOX_W)),
                  pl.BlockSpec((t, FOX_W), lambda kj: (kj, COL_FV // FOX_W)),
                  _resident((nh, 1, s), whole),
                  pl.BlockSpec((t, LANES), lambda kj: (kj, 0)),
                  _resident((nh, 1, s), whole),
                  _resident((nh, 1, s), whole)],
        out_specs=[_resident((FOX_W, s), lambda kj: (0, 0)),
                   pl.BlockSpec((t, FOX_W), lambda kj: (kj, 0)),
                   pl.BlockSpec((t, FOX_W), lambda kj: (kj, 0)),
                   pl.BlockSpec((t, LANES), lambda kj: (kj, 0)),
                   _resident((nh, 1, s), whole)],
        out_shape=[jax.ShapeDtypeStruct((FOX_W, s), F32),
                   jax.ShapeDtypeStruct((s, FOX_W), BF16),
                   jax.ShapeDtypeStruct((s, FOX_W), BF16),
                   jax.ShapeDtypeStruct((s, LANES), F32),
                   jax.ShapeDtypeStruct((nh, 1, s), F32)],
        scratch_shapes=[pltpu.VMEM((npair, t, LANES), F32),
                        pltpu.VMEM((npair, t, LANES), F32),
                        pltpu.VMEM((nh, t, LANES), F32)],
        compiler_params=_cparams(("arbitrary",)),
    )(qkv, do_bf, qkv, qkv, cum_t3, cum, lse_t3, delta_t3)


def _bucket_table():
    qi = np.arange(BLOCK)[:, None]
    kj = np.arange(2 * BLOCK)[None, :]
    rel = np.maximum(qi + BLOCK - kj, 0).astype(np.int32)
    max_exact = NUM_BUCKETS // 2
    relf = np.maximum(rel, 1).astype(np.float32)
    large = max_exact + (np.log(relf / np.float32(max_exact)) / np.float32(math.log(MAX_DISTANCE / max_exact))
                         * np.float32(NUM_BUCKETS - max_exact)).astype(np.int32)
    large = np.minimum(large, NUM_BUCKETS - 1)
    return np.where(rel < max_exact, rel, large).astype(np.int32)


def _swa_bias(rel_bias, bucket):
    def kern(rb_ref, bk_ref, o_ref):
        bk = bk_ref[...]
        for h in range(SWA_HEADS):
            acc = jnp.zeros((BLOCK, 2 * BLOCK), F32)
            for b in range(NUM_BUCKETS):
                acc = jnp.where(bk == b, rb_ref[b, h], acc)
            o_ref[h] = acc

    return pl.pallas_call(
        kern, name="swa_bias",
        in_specs=[pl.BlockSpec(memory_space=pltpu.SMEM),
                  pl.BlockSpec(memory_space=pltpu.VMEM)],
        out_specs=pl.BlockSpec(memory_space=pltpu.VMEM),
        out_shape=jax.ShapeDtypeStruct((SWA_HEADS, BLOCK, 2 * BLOCK), F32),
        compiler_params=_cparams(),
    )(rel_bias, bucket)


def _swa_mask(n):
    qi = lax.broadcasted_iota(jnp.int32, (BLOCK, 2 * BLOCK), 0)
    kj = lax.broadcasted_iota(jnp.int32, (BLOCK, 2 * BLOCK), 1)
    rel = qi + BLOCK - kj
    band = (rel >= 0) & (rel < BLOCK)
    return band & ((kj >= BLOCK) | (n > 0))


def _swa_fwd(qkv, bias, sink):
    s = qkv.shape[0]
    nb = s // BLOCK

    def kern(q_ref, kp_ref, kc_ref, vp_ref, vc_ref, bias_ref, sink_ref, o_ref, lse_ref):
        n = pl.program_id(0)
        mask = _swa_mask(n)
        lane = lax.broadcasted_iota(jnp.int32, (1, LANES), 1)
        q = q_ref[...] * jnp.asarray(QK_SCALE, BF16)
        k = jnp.concatenate([kp_ref[...], kc_ref[...]], axis=0)
        v = jnp.concatenate([vp_ref[...], vc_ref[...]], axis=0)
        kgs = [k[:, g * HEAD_DIM:(g + 1) * HEAD_DIM] for g in range(SWA_KV_HEADS)]
        vgs = [v[:, g * HEAD_DIM:(g + 1) * HEAD_DIM] for g in range(SWA_KV_HEADS)]
        raw = [_dot_nt(q[:, h * HEAD_DIM:(h + 1) * HEAD_DIM], kgs[h // SWA_GROUP]) for h in range(SWA_HEADS)]
        probs = []
        lse_all = jnp.zeros((BLOCK, LANES), F32)
        for h in range(SWA_HEADS):
            sc = jnp.where(mask, raw[h] + bias_ref[h], NEG)
            sk = sink_ref[0, h]
            m = jnp.maximum(jnp.max(sc, axis=1, keepdims=True), sk)
            p = jnp.exp(sc - m)
            l = jnp.sum(p, axis=1, keepdims=True) + jnp.exp(sk - m)
            probs.append((p * (1.0 / l)).astype(BF16))
            lse_all = jnp.where(lane == h, m + jnp.log(l), lse_all)
        outs = [_dot(probs[h], vgs[h // SWA_GROUP]) for h in range(SWA_HEADS)]
        o_ref[...] = jnp.concatenate(outs, axis=1)
        lse_ref[...] = lse_all

    cq, ck, cv = COL_SQ // SWA_W, COL_SK // LANES, COL_SV // LANES
    prev = lambda n: jnp.maximum(n - 1, 0)
    return pl.pallas_call(
        kern, name="swa_fwd",
        grid=(nb,),
        in_specs=[pl.BlockSpec((BLOCK, SWA_W), lambda n: (n, cq)),
                  pl.BlockSpec((BLOCK, LANES), lambda n: (prev(n), ck)),
                  pl.BlockSpec((BLOCK, LANES), lambda n: (n, ck)),
                  pl.BlockSpec((BLOCK, LANES), lambda n: (prev(n), cv)),
                  pl.BlockSpec((BLOCK, LANES), lambda n: (n, cv)),
                  pl.BlockSpec((SWA_HEADS, BLOCK, 2 * BLOCK), lambda n: (0, 0, 0)),
                  pl.BlockSpec(memory_space=pltpu.SMEM)],
        out_specs=[pl.BlockSpec((BLOCK, SWA_W), lambda n: (n, 0)),
                   pl.BlockSpec((BLOCK, LANES), lambda n: (n, 0))],
        out_shape=[jax.ShapeDtypeStruct((s, SWA_W), F32),
                   jax.ShapeDtypeStruct((s, LANES), F32)],
        compiler_params=_cparams(("parallel",)),
    )(qkv, qkv, qkv, qkv, qkv, bias, sink)


def _swa_bwd(qkv, do_bf, delta, lse, bias, sink, bucket):
    s = qkv.shape[0]
    nb = s // BLOCK

    def kern(q_ref, kp_ref, kc_ref, vp_ref, vc_ref, do_ref, dl_ref, lse_ref, bias_ref, sink_ref, bk_ref,
             dq_ref, dk_ref, dv_ref, grb_ref, gsk_ref, dbias_ref, ck_ref, cv_ref, sk_ref):
        n = pl.program_id(0)
        lane = lax.broadcasted_iota(jnp.int32, (1, LANES), 1)

        @pl.when(n == 0)
        def _():
            dbias_ref[...] = jnp.zeros_like(dbias_ref)
            ck_ref[...] = jnp.zeros_like(ck_ref)
            cv_ref[...] = jnp.zeros_like(cv_ref)
            sk_ref[...] = jnp.zeros_like(sk_ref)

        @pl.when(n < nb)
        def _():
            mask = _swa_mask(n)
            q = q_ref[...] * jnp.asarray(QK_SCALE, BF16)
            k = jnp.concatenate([kp_ref[...], kc_ref[...]], axis=0)
            v = jnp.concatenate([vp_ref[...], vc_ref[...]], axis=0)
            do = do_ref[...]
            dl = dl_ref[...]
            lse_all = lse_ref[...]
            dks = [None] * SWA_KV_HEADS
            dvs = [None] * SWA_KV_HEADS
            gsk = jnp.zeros((1, LANES), F32)
            kgs = [k[:, g * HEAD_DIM:(g + 1) * HEAD_DIM] for g in range(SWA_KV_HEADS)]
            vgs = [v[:, g * HEAD_DIM:(g + 1) * HEAD_DIM] for g in range(SWA_KV_HEADS)]
            qhs = [q[:, h * HEAD_DIM:(h + 1) * HEAD_DIM] for h in range(SWA_HEADS)]
            dohs = [do[:, h * HEAD_DIM:(h + 1) * HEAD_DIM] for h in range(SWA_HEADS)]
            raw = [_dot_nt(qhs[h], kgs[h // SWA_GROUP]) for h in range(SWA_HEADS)]
            dps = [_dot_nt(dohs[h], vgs[h // SWA_GROUP]) for h in range(SWA_HEADS)]
            ps, dss = [], []
            for h in range(SWA_HEADS):
                lse_h = lse_all[:, h:h + 1]
                dlt = dl[:, FOX_HEADS + h:FOX_HEADS + h + 1]
                sc = jnp.where(mask, raw[h] + bias_ref[h], NEG)
                p = jnp.exp(sc - lse_h)
                ds = p * (dps[h] - dlt)
                dbias_ref[h] += ds
                p_sink = jnp.exp(sink_ref[0, h] - lse_h)
                gsk = gsk + jnp.where(lane == h, -jnp.sum(p_sink * dlt), 0.0)
                ps.append(p.astype(BF16))
                dss.append(ds.astype(BF16))
            dqs = [_dot(dss[h], kgs[h // SWA_GROUP]) * QK_SCALE for h in range(SWA_HEADS)]
            for h in range(SWA_HEADS):
                g = h // SWA_GROUP
                dk_h = _dot_tn(dss[h], qhs[h])
                dv_h = _dot_tn(ps[h], dohs[h])
                dks[g] = dk_h if dks[g] is None else dks[g] + dk_h
                dvs[g] = dv_h if dvs[g] is None else dvs[g] + dv_h
            dq_ref[...] = jnp.concatenate(dqs, axis=1).astype(BF16)
            sk_ref[...] += jnp.broadcast_to(gsk, sk_ref.shape)
            dk2 = jnp.concatenate(dks, axis=1)
            dv2 = jnp.concatenate(dvs, axis=1)
            dk_ref[...] = (ck_ref[...] + dk2[:BLOCK]).astype(BF16)
            dv_ref[...] = (cv_ref[...] + dv2[:BLOCK]).astype(BF16)
            ck_ref[...] = dk2[BLOCK:]
            cv_ref[...] = dv2[BLOCK:]

        @pl.when(n == nb)
        def _():
            dk_ref[...] = ck_ref[...].astype(BF16)
            dv_ref[...] = cv_ref[...].astype(BF16)
            gsk_ref[...] = sk_ref[...]
            bk = bk_ref[...]
            rowi = lax.broadcasted_iota(jnp.int32, (NUM_BUCKETS, LANES), 0)
            lanei = lax.broadcasted_iota(jnp.int32, (NUM_BUCKETS, LANES), 1)
            out = jnp.zeros((NUM_BUCKETS, LANES), F32)
            for h in range(SWA_HEADS):
                db = dbias_ref[h]
                for b in range(NUM_BUCKETS):
                    val = jnp.sum(jnp.where(bk == b, db, 0.0))
                    out = jnp.where((rowi == b) & (lanei == h), val, out)
            grb_ref[...] = out

    cq, ck, cv = COL_SQ // SWA_W, COL_SK // LANES, COL_SV // LANES
    cur = lambda n: jnp.minimum(n, nb - 1)
    prev = lambda n: jnp.maximum(jnp.minimum(n, nb - 1) - 1, 0)
    kout = lambda n: jnp.maximum(n - 1, 0)
    return pl.pallas_call(
        kern, name="swa_bwd",
        grid=(nb + 1,),
        in_specs=[pl.BlockSpec((BLOCK, SWA_W), lambda n: (cur(n), cq)),
                  pl.BlockSpec((BLOCK, LANES), lambda n: (prev(n), ck)),
                  pl.BlockSpec((BLOCK, LANES), lambda n: (cur(n), ck)),
                  pl.BlockSpec((BLOCK, LANES), lambda n: (prev(n), cv)),
                  pl.BlockSpec((BLOCK, LANES), lambda n: (cur(n), cv)),
                  pl.BlockSpec((BLOCK, SWA_W), lambda n: (cur(n), 1)),
                  pl.BlockSpec((BLOCK, LANES), lambda n: (cur(n), 0)),
                  pl.BlockSpec((BLOCK, LANES), lambda n: (cur(n), 0)),
                  pl.BlockSpec((SWA_HEADS, BLOCK, 2 * BLOCK), lambda n: (0, 0, 0)),
                  pl.BlockSpec(memory_space=pltpu.SMEM),
                  pl.BlockSpec((BLOCK, 2 * BLOCK), lambda n: (0, 0))],
        out_specs=[pl.BlockSpec((BLOCK, SWA_W), lambda n: (cur(n), 0)),
                   pl.BlockSpec((BLOCK, LANES), lambda n: (kout(n), 0)),
                   pl.BlockSpec((BLOCK, LANES), lambda n: (kout(n), 0)),
                   pl.BlockSpec((NUM_BUCKETS, LANES), lambda n: (0, 0)),
                   pl.BlockSpec((8, LANES), lambda n: (0, 0))],
        out_shape=[jax.ShapeDtypeStruct((s, SWA_W), BF16),
                   jax.ShapeDtypeStruct((s, LANES), BF16),
                   jax.ShapeDtypeStruct((s, LANES), BF16),
                   jax.ShapeDtypeStruct((NUM_BUCKETS, LANES), F32),
                   jax.ShapeDtypeStruct((8, LANES), F32)],
        scratch_shapes=[pltpu.VMEM((SWA_HEADS, BLOCK, 2 * BLOCK), F32),
                        pltpu.VMEM((BLOCK, LANES), F32),
                        pltpu.VMEM((BLOCK, LANES), F32),
                        pltpu.VMEM((8, LANES), F32)],
        compiler_params=_cparams(("arbitrary",)),
    )(qkv, qkv, qkv, qkv, qkv, do_bf, delta, lse, bias, sink, bucket)


def _post(x, target, o_fox, o_swa, z, w_o, ln_g, ln_b):
    s = x.shape[0]
    tm = min(256, s)
    nt = s // tm

    def kern(x_ref, t_ref, of_ref, os_ref, z_ref, w_ref, g_ref, b_ref,
             loss_ref, dh_ref, dy_ref, mix_ref, do_ref, dz_ref, dl_ref, gg_ref, gb_ref, lacc_ref):
        step = pl.program_id(0)

        @pl.when(step == 0)
        def _():
            lacc_ref[...] = jnp.zeros_like(lacc_ref)
            gg_ref[...] = jnp.zeros_like(gg_ref)
            gb_ref[...] = jnp.zeros_like(gb_ref)

        o = jnp.concatenate([of_ref[...], os_ref[...]], axis=1)
        zz = z_ref[...]
        sig = 1.0 / (1.0 + jnp.exp(-zz))
        silu = zz * sig
        mixed32 = o * silu
        mixed = mixed32.astype(BF16)
        mix_ref[...] = mixed32.T.astype(BF16)
        w = w_ref[...]
        h = ALPHA * x_ref[...] + _dot(mixed, w)
        mu = jnp.mean(h, axis=1, keepdims=True)
        hc = h - mu
        var = jnp.mean(hc * hc, axis=1, keepdims=True)
        rstd = lax.rsqrt(var + LN_EPS)
        xhat = hc * rstd
        g = g_ref[...]
        err = xhat * g + b_ref[...] - t_ref[...]
        lacc_ref[...] += jnp.broadcast_to(jnp.sum(err * err, axis=0, keepdims=True), lacc_ref.shape)
        dout = err * (1.0 / D_MODEL)
        gg_ref[...] += jnp.broadcast_to(jnp.sum(dout * xhat, axis=0, keepdims=True), gg_ref.shape)
        gb_ref[...] += jnp.broadcast_to(jnp.sum(dout, axis=0, keepdims=True), gb_ref.shape)
        dxh = dout * g
        m1 = jnp.mean(dxh, axis=1, keepdims=True)
        m2 = jnp.mean(dxh * xhat, axis=1, keepdims=True)
        dh = rstd * (dxh - m1 - xhat * m2)
        dh_ref[...] = dh
        dy = dh.astype(BF16)
        dy_ref[...] = dy
        dmix = _dot_nt(dy, w)
        do = dmix * silu
        do_ref[...] = do.astype(BF16)
        dz_ref[...] = (dmix * o * (sig * (1.0 + zz * (1.0 - sig)))).astype(BF16)
        r = lax.broadcasted_iota(jnp.int32, (D_MODEL, LANES), 0) // HEAD_DIM
        c = lax.broadcasted_iota(jnp.int32, (D_MODEL, LANES), 1)
        pick = jnp.where(r == c, 1.0, 0.0).astype(BF16)
        dl_ref[...] = _exact_dot(pick, do * o, False)

        @pl.when(step == nt - 1)
        def _():
            tot = jnp.sum(lacc_ref[0:1, :]) * (0.5 / D_MODEL)
            loss_ref[...] = jnp.broadcast_to(tot, loss_ref.shape)

    row = lambda i: (i, 0)
    fixed = lambda i: (0, 0)
    wide = pl.BlockSpec((tm, D_MODEL), row)
    half = pl.BlockSpec((tm, FOX_W), row)
    return pl.pallas_call(
        kern, name="post",
        grid=(nt,),
        in_specs=[wide, wide, half, half, wide,
                  pl.BlockSpec((D_MODEL, D_MODEL), fixed),
                  pl.BlockSpec((1, D_MODEL), fixed),
                  pl.BlockSpec((1, D_MODEL), fixed)],
        out_specs=[pl.BlockSpec((8, LANES), fixed), wide, wide,
                   pl.BlockSpec((D_MODEL, tm), lambda i: (0, i)), wide, wide,
                   pl.BlockSpec((tm, LANES), row),
                   pl.BlockSpec((8, D_MODEL), fixed), pl.BlockSpec((8, D_MODEL), fixed)],
        out_shape=[jax.ShapeDtypeStruct((8, LANES), F32),
                   jax.ShapeDtypeStruct((s, D_MODEL), F32),
                   jax.ShapeDtypeStruct((s, D_MODEL), BF16),
                   jax.ShapeDtypeStruct((D_MODEL, s), BF16),
                   jax.ShapeDtypeStruct((s, D_MODEL), BF16),
                   jax.ShapeDtypeStruct((s, D_MODEL), BF16),
                   jax.ShapeDtypeStruct((s, LANES), F32),
                   jax.ShapeDtypeStruct((8, D_MODEL), F32),
                   jax.ShapeDtypeStruct((8, D_MODEL), F32)],
        scratch_shapes=[pltpu.VMEM((8, D_MODEL), F32)],
        compiler_params=_cparams(("arbitrary",)),
    )(x, target, o_fox, o_swa, z, w_o, ln_g, ln_b)


def _adamw_math(w, g, m, v):
    m = ADAM_B1 * m + (1.0 - ADAM_B1) * g
    v = ADAM_B2 * v + (1.0 - ADAM_B2) * (g * g)
    m_hat = m / (1.0 - ADAM_B1 ** ADAM_STEP)
    v_hat = v / (1.0 - ADAM_B2 ** ADAM_STEP)
    delta = -ADAM_LR * (m_hat / (jnp.sqrt(v_hat) + ADAM_EPS) + ADAM_WD * w)
    return delta, m, v


def _adamw(w, g, m, v, *, name):
    r, c = w.shape
    tr = min(256, r)

    def kern(w_ref, g_ref, m_ref, v_ref, d_ref, mo_ref, vo_ref):
        d, mn, vn = _adamw_math(w_ref[...], g_ref[...], m_ref[...], v_ref[...])
        d_ref[...] = d
        mo_ref[...] = mn
        vo_ref[...] = vn

    blk = pl.BlockSpec((tr, c), lambda i: (i, 0))
    sds = jax.ShapeDtypeStruct((r, c), F32)
    return pl.pallas_call(
        kern, name=name,
        grid=(r // tr,),
        in_specs=[blk, blk, blk, blk],
        out_specs=[blk, blk, blk],
        out_shape=[sds, sds, sds],
        compiler_params=_cparams(("parallel",)),
    )(w, g, m, v)


def _position():
    x, y, c = lax.axis_index("x"), lax.axis_index("y"), lax.axis_index("c")
    chips = [(1 - x, y), (x, 1 - y), (1 - x, 1 - y)]
    return x, y, c, chips


def _chip_index(cx, cy):
    return 2 * cx + cy


def _gather_weights(w_in_bf, w_o_bf):
    shards = (w_in_bf, w_o_bf)
    n_arr = len(shards)

    def kern(*refs):
        ins, outs = refs[:n_arr], refs[n_arr:2 * n_arr]
        send_sems, recv_sems, local_sems = refs[2 * n_arr:]
        x, y, c, chips = _position()
        me = _chip_index(x, y)
        sibling = (x, y, 1 - c)

        local = [pltpu.make_async_copy(ins[a], outs[a].at[me], local_sems.at[a]) for a in range(n_arr)]
        for cp in local:
            cp.start()

        def half(ref, a):
            rows = shards[a].shape[0] // 2
            return ref.at[pl.ds(c * rows, rows), :]

        def copy(a, k, src, slot, to):
            return pltpu.make_async_remote_copy(
                src_ref=src, dst_ref=half(outs[a].at[slot], a),
                send_sem=send_sems.at[a * 6 + k], recv_sem=recv_sems.at[a * 6 + k],
                device_id=to, device_id_type=MESH)

        first = [copy(a, j, half(ins[a], a), me, (*chip, c)) for a in range(n_arr) for j, chip in enumerate(chips)]
        for cp in first:
            cp.start()
        passed = []
        for a in range(n_arr):
            for j, chip in enumerate(chips):
                slot = _chip_index(*chip)
                copy(a, j, half(ins[a], a), slot, (*chip, c)).wait_recv()
                fwd = copy(a, 3 + j, half(outs[a].at[slot], a), slot, sibling)
                fwd.start()
                passed.append(fwd)
        for a in range(n_arr):
            for j, chip in enumerate(chips):
                slot = _chip_index(*chip)
                rows = shards[a].shape[0] // 2
                dst = outs[a].at[slot].at[pl.ds((1 - c) * rows, rows), :]
                pltpu.make_async_remote_copy(
                    src_ref=dst, dst_ref=dst, send_sem=send_sems.at[a * 6 + 3 + j],
                    recv_sem=recv_sems.at[a * 6 + 3 + j], device_id=sibling, device_id_type=MESH).wait_recv()
        for cp in first + passed:
            cp.wait_send()
        for cp in local:
            cp.wait()

    vmem = pl.BlockSpec(memory_space=pltpu.VMEM)
    return pl.pallas_call(
        kern, name="gather_weights",
        in_specs=[vmem] * n_arr,
        out_specs=[vmem] * n_arr,
        out_shape=[jax.ShapeDtypeStruct((N_CHIPS,) + w.shape, w.dtype) for w in shards],
        scratch_shapes=[pltpu.SemaphoreType.DMA((6 * n_arr,)),
                        pltpu.SemaphoreType.DMA((6 * n_arr,)),
                        pltpu.SemaphoreType.DMA((n_arr,))],
        compiler_params=_cparams(),
    )(*shards)


def _swap_halves(grads):
    n_arr = len(grads)

    def kern(*refs):
        ins = refs[:n_arr]
        owns = refs[n_arr:2 * n_arr]
        gots = refs[2 * n_arr:3 * n_arr]
        send_sems, recv_sems, local_sems = refs[3 * n_arr:]
        x, y, c, _ = _position()
        sibling = (x, y, 1 - c)
        local, remote = [], []
        for a in range(n_arr):
            rows = grads[a].shape[1] // 2
            piece = rows // COPY_PIECES
            for j in range(N_CHIPS):
                for r in range(COPY_PIECES):
                    k = (a * N_CHIPS + j) * COPY_PIECES + r
                    dst_rows = pl.ds(r * piece, piece)
                    local.append(pltpu.make_async_copy(
                        ins[a].at[j, pl.ds(c * rows + r * piece, piece), :],
                        owns[a].at[j, dst_rows, :], local_sems.at[k]))
                    remote.append(pltpu.make_async_remote_copy(
                        src_ref=ins[a].at[j, pl.ds((1 - c) * rows + r * piece, piece), :],
                        dst_ref=gots[a].at[j, dst_rows, :], send_sem=send_sems.at[k], recv_sem=recv_sems.at[k],
                        device_id=sibling, device_id_type=MESH))
        for cp in local + remote:
            cp.start()
        for cp in remote:
            cp.wait()
        for cp in local:
            cp.wait()

    hbm = pl.BlockSpec(memory_space=pltpu.VMEM)
    half = [jax.ShapeDtypeStruct((N_CHIPS, g.shape[1] // 2, g.shape[2]), F32) for g in grads]
    outs = pl.pallas_call(
        kern, name="swap_halves",
        in_specs=[hbm] * n_arr,
        out_specs=[hbm] * (2 * n_arr),
        out_shape=half + half,
        scratch_shapes=[pltpu.SemaphoreType.DMA((n_arr * N_CHIPS * COPY_PIECES,)),
                        pltpu.SemaphoreType.DMA((n_arr * N_CHIPS * COPY_PIECES,)),
                        pltpu.SemaphoreType.DMA((n_arr * N_CHIPS * COPY_PIECES,))],
        compiler_params=_cparams(),
    )(*grads)
    return outs[:n_arr], outs[n_arr:]


def _scatter_to_owners(parts):
    n_arr = len(parts)

    def kern(*refs):
        ins = refs[:n_arr]
        outs = refs[n_arr:2 * n_arr]
        send_sems, recv_sems, local_sems = refs[2 * n_arr:]
        x, y, c, chips = _position()
        me = _chip_index(x, y)
        local = [pltpu.make_async_copy(ins[a].at[me], outs[a].at[me], local_sems.at[a]) for a in range(n_arr)]
        for cp in local:
            cp.start()
        sends = []
        for a in range(n_arr):
            for j, chip in enumerate(chips):
                sends.append(pltpu.make_async_remote_copy(
                    src_ref=ins[a].at[_chip_index(*chip)], dst_ref=outs[a].at[me],
                    send_sem=send_sems.at[a * 3 + j], recv_sem=recv_sems.at[a * 3 + j],
                    device_id=(*chip, c), device_id_type=MESH))
        for cp in sends:
            cp.start()
        for a in range(n_arr):
            for j, chip in enumerate(chips):
                slot = outs[a].at[_chip_index(*chip)]
                pltpu.make_async_remote_copy(
                    src_ref=slot, dst_ref=slot, send_sem=send_sems.at[a * 3 + j],
                    recv_sem=recv_sems.at[a * 3 + j], device_id=(*chip, c), device_id_type=MESH).wait_recv()
        for cp in sends:
            cp.wait_send()
        for cp in local:
            cp.wait()

    hbm = pl.BlockSpec(memory_space=pltpu.VMEM)
    return pl.pallas_call(
        kern, name="scatter_to_owners",
        in_specs=[hbm] * n_arr,
        out_specs=[hbm] * n_arr,
        out_shape=[jax.ShapeDtypeStruct(p.shape, p.dtype) for p in parts],
        scratch_shapes=[pltpu.SemaphoreType.DMA((3 * n_arr,)),
                        pltpu.SemaphoreType.DMA((3 * n_arr,)),
                        pltpu.SemaphoreType.DMA((n_arr,))],
        compiler_params=_cparams(),
    )(*parts)


def _join_halves(halves):
    n_arr = len(halves)

    def kern(*refs):
        ins = refs[:n_arr]
        outs = refs[n_arr:2 * n_arr]
        send_sems, recv_sems, local_sems = refs[2 * n_arr:]
        x, y, c, _ = _position()
        sibling = (x, y, 1 - c)
        local, remote = [], []
        for a in range(n_arr):
            rows = halves[a].shape[0]
            piece = rows // COPY_PIECES
            for r in range(COPY_PIECES):
                k = a * COPY_PIECES + r
                src = ins[a].at[pl.ds(r * piece, piece), :]
                dst = outs[a].at[pl.ds(c * rows + r * piece, piece), :]
                local.append(pltpu.make_async_copy(src, dst, local_sems.at[k]))
                remote.append(pltpu.make_async_remote_copy(
                    src_ref=src, dst_ref=dst, send_sem=send_sems.at[k], recv_sem=recv_sems.at[k],
                    device_id=sibling, device_id_type=MESH))
        for cp in local + remote:
            cp.start()
        for a in range(n_arr):
            rows = halves[a].shape[0]
            piece = rows // COPY_PIECES
            for r in range(COPY_PIECES):
                k = a * COPY_PIECES + r
                theirs = outs[a].at[pl.ds((1 - c) * rows + r * piece, piece), :]
                pltpu.make_async_remote_copy(
                    src_ref=theirs, dst_ref=theirs, send_sem=send_sems.at[k], recv_sem=recv_sems.at[k],
                    device_id=sibling, device_id_type=MESH).wait_recv()
        for cp in remote:
            cp.wait_send()
        for cp in local:
            cp.wait()

    hbm = pl.BlockSpec(memory_space=pltpu.VMEM)
    return pl.pallas_call(
        kern, name="join_halves",
        in_specs=[hbm] * n_arr,
        out_specs=[hbm] * n_arr,
        out_shape=[jax.ShapeDtypeStruct((2 * h.shape[0], h.shape[1]), F32) for h in halves],
        scratch_shapes=[pltpu.SemaphoreType.DMA((n_arr * COPY_PIECES,)),
                        pltpu.SemaphoreType.DMA((n_arr * COPY_PIECES,)),
                        pltpu.SemaphoreType.DMA((n_arr * COPY_PIECES,))],
        compiler_params=_cparams(),
    )(*halves)


def _add2(a, b, *, name):
    n, r, c = a.shape
    tr = min(256, r)

    def kern(a_ref, b_ref, o_ref):
        o_ref[...] = (a_ref[...] + b_ref[...]).astype(BF16)

    blk = pl.BlockSpec((1, tr, c), lambda j, i: (j, i, 0))
    return pl.pallas_call(
        kern, name=name,
        grid=(n, r // tr),
        in_specs=[blk, blk],
        out_specs=blk,
        out_shape=jax.ShapeDtypeStruct(a.shape, BF16),
        compiler_params=_cparams(("parallel", "parallel")),
    )(a, b)


def _sum4(a, *, name):
    n, r, c = a.shape
    tr = min(256, r)

    def kern(a_ref, o_ref):
        f = lambda j: a_ref[j].astype(F32)
        o_ref[...] = ((f(0) + f(1)) + f(2)) + f(3)

    return pl.pallas_call(
        kern, name=name,
        grid=(r // tr,),
        in_specs=[pl.BlockSpec((n, tr, c), lambda i: (0, i, 0))],
        out_specs=pl.BlockSpec((tr, c), lambda i: (i, 0)),
        out_shape=jax.ShapeDtypeStruct((r, c), F32),
        compiler_params=_cparams(("parallel",)),
    )(a)


def _small_allreduce_adamw(g, w, m, v):
    def kern(g_ref, w_ref, m_ref, v_ref, gs_ref, d_ref, mo_ref, vo_ref, buf_ref, send_sems, recv_sems):
        x, y, c, _ = _position()
        me = 4 * x + 2 * y + c
        buf_ref[me] = g_ref[...]
        peers = [(x, y, 1 - c)] + [(px, py, pc) for px, py in _position()[3] for pc in (c, 1 - c)]
        sends = []
        for k, peer in enumerate(peers):
            sends.append(pltpu.make_async_remote_copy(
                src_ref=g_ref, dst_ref=buf_ref.at[me], send_sem=send_sems.at[k], recv_sem=recv_sems.at[k],
                device_id=peer, device_id_type=MESH))
        for cp in sends:
            cp.start()
        for k, (px, py, pc) in enumerate(peers):
            slot = buf_ref.at[4 * px + 2 * py + pc]
            pltpu.make_async_remote_copy(
                src_ref=slot, dst_ref=slot, send_sem=send_sems.at[k], recv_sem=recv_sems.at[k],
                device_id=(px, py, pc), device_id_type=MESH).wait_recv()
        for cp in sends:
            cp.wait_send()
        tot = buf_ref[0]
        for d in range(1, N_DEV):
            tot = tot + buf_ref[d]
        gs_ref[...] = tot
        delta, mn, vn = _adamw_math(w_ref[...], tot, m_ref[...], v_ref[...])
        d_ref[...] = delta
        mo_ref[...] = mn
        vo_ref[...] = vn

    vm = pl.BlockSpec(memory_space=pltpu.VMEM)
    sds = jax.ShapeDtypeStruct((SMALL_ROWS, LANES), F32)
    return pl.pallas_call(
        kern, name="small_allreduce_adamw",
        in_specs=[vm] * 4,
        out_specs=[vm] * 4,
        out_shape=[sds] * 4,
        scratch_shapes=[pltpu.VMEM((N_DEV, SMALL_ROWS, LANES), F32),
                        pltpu.SemaphoreType.DMA((N_DEV - 1,)),
                        pltpu.SemaphoreType.DMA((N_DEV - 1,))],
    )(g, w, m, v)


def _to_padded_cols(w):
    pad = jnp.zeros((w.shape[0], N_C - FOX_HEADS), w.dtype)
    return jnp.concatenate([w[:, 0:1536], w[:, 2056:2824], w[:, 1536:1544], pad,
                            w[:, 1544:2056], w[:, 2824:3336]], axis=1)


def _from_padded_cols(g):
    return jnp.concatenate([g[:, 0:1536], g[:, OFF_C:OFF_C + FOX_HEADS], g[:, OFF_B:OFF_B + FOX_W],
                            g[:, 1536:N_A], g[:, OFF_B + FOX_W:N_PAD]], axis=1)


def _pack_small(b_f, rel_bias, sink, ln_g, ln_b):
    row = lambda v: jnp.pad(v.reshape(1, -1), ((0, 0), (0, LANES - v.size)))
    return jnp.concatenate([ln_g.reshape(8, LANES), ln_b.reshape(8, LANES), rel_bias.reshape(2, LANES),
                            row(b_f), row(sink), jnp.zeros((4, LANES), F32)], axis=0)


def _unpack_small(p):
    ln_g = p[0:8].reshape(1, D_MODEL)
    ln_b = p[8:16].reshape(1, D_MODEL)
    rel_bias = p[16:18].reshape(NUM_BUCKETS, SWA_HEADS)
    b_f = p[18:19, :FOX_HEADS]
    sink = p[19:20, :SWA_HEADS]
    return b_f, rel_bias, sink, ln_g, ln_b


def _fox_rows(a):
    return a[:, :FOX_HEADS].T.reshape(FOX_HEADS, 1, a.shape[0])


def kernel(x, w_in, b_f, rel_bias, sink, w_o, ln_g, ln_b, loss_target, m_w_in, m_b_f, m_rel_bias, m_sink, m_w_o, m_ln_g, m_ln_b, v_w_in, v_b_f, v_rel_bias, v_sink, v_w_o, v_ln_g, v_ln_b):
    x2 = x[0]
    tgt = loss_target[0]
    s = x2.shape[0]
    w_in2, w_o2 = w_in[0], w_o[0]

    shard_cols = D_IN // N_CHIPS
    col_pad = ((0, 0), (0, SHARD_PAD - shard_cols))
    w_in_all, w_o_all = _gather_weights(jnp.pad(w_in2.astype(BF16), col_pad), w_o2.astype(BF16))
    w_full = jnp.concatenate([w_in_all[j, :, :shard_cols] for j in range(N_CHIPS)], axis=1)
    w_pad = _to_padded_cols(w_full)
    w_o_full = w_o_all.reshape(D_MODEL, D_MODEL)

    qkv, ffp, z, xt = _project(x2, w_pad)
    bfp = jnp.pad(b_f, ((0, 0), (0, LANES - FOX_HEADS)))
    cum = _cum_fwd(ffp, bfp)
    cum_t3 = _fox_rows(cum)
    vt = qkv[:, COL_FV:COL_FV + FOX_W].T
    o_fox, lse_t3 = _fox_fwd(qkv, vt, cum_t3, cum)
    bucket = jnp.asarray(_bucket_table())
    bias = _swa_bias(rel_bias, bucket)
    o_swa, lse_swa = _swa_fwd(qkv, bias, sink)

    loss8, dh, dy, mixed_t, do_bf, dz, delta, gg8, gb8 = _post(
        x2, tgt, o_fox, o_swa, z, w_o_full, ln_g, ln_b)
    loss = lax.psum(loss8[0, 0], ("x", "y", "c"))
    grad_w_o_full = _matmul_acc(mixed_t, dy, tm=1024, tn=512, tk=1024, name="grad_w_o")

    delta_t3 = _fox_rows(delta)
    dqt_fox, dk_fox, dv_fox, dcum_k, dcum_q = _fox_bwd(qkv, do_bf, cum_t3, cum, lse_t3, delta_t3)
    dcum_q = jnp.pad(dcum_q.reshape(FOX_HEADS, s).T, ((0, 0), (0, LANES - FOX_HEADS)))
    dff, gbf8 = _cum_bwd(dcum_k, dcum_q, ffp, bfp)
    dq_swa, dk_swa, dv_swa, grb, gsk8 = _swa_bwd(qkv, do_bf, delta, lse_swa, bias, sink, bucket)

    dq_fox = dqt_fox.T.astype(BF16)
    d_misc = jnp.concatenate([dk_swa, dv_swa, dff], axis=1)
    pieces = [dq_fox, dk_fox, dv_fox, dq_swa, d_misc, dz]
    grad_x = _grad_x_matmul(pieces, w_pad, dh, tm=512, tn=512, name="grad_x")
    blocks = [(p, 0) for p in pieces[:-1]] + [(dz, 0), (dz, 1)]
    grad_w_pad = _grad_w_matmul(xt, blocks, tk=1024, name="grad_w_in")
    grad_w_in_full = _from_padded_cols(grad_w_pad)

    g_in4 = jnp.stack([jnp.pad(grad_w_in_full[:, j * shard_cols:(j + 1) * shard_cols], col_pad)
                       for j in range(N_CHIPS)])
    g_o4 = grad_w_o_full.reshape(N_CHIPS, D_MODEL // N_CHIPS, D_MODEL)
    owns, gots = _swap_halves([g_in4, g_o4])
    parts = [_add2(owns[0], gots[0], name="pair_sum_w_in"), _add2(owns[1], gots[1], name="pair_sum_w_o")]
    slabs = _scatter_to_owners(parts)
    halves = [_sum4(slabs[0], name="chip_sum_w_in"), _sum4(slabs[1], name="chip_sum_w_o")]
    g_w_in, g_w_o = _join_halves(halves)
    g_w_in = g_w_in[:, :shard_cols]

    d_w_in, nm_w_in, nv_w_in = _adamw(w_in2, g_w_in, m_w_in[0], v_w_in[0], name="adamw_w_in")
    d_w_o, nm_w_o, nv_w_o = _adamw(w_o2, g_w_o, m_w_o[0], v_w_o[0], name="adamw_w_o")

    g_small = _pack_small(gbf8[0:1, :FOX_HEADS], grb[:, :SWA_HEADS], gsk8[0:1, :SWA_HEADS], gg8[0:1], gb8[0:1])
    w_small = _pack_small(b_f, rel_bias, sink, ln_g, ln_b)
    m_small = _pack_small(m_b_f, m_rel_bias, m_sink, m_ln_g, m_ln_b)
    v_small = _pack_small(v_b_f, v_rel_bias, v_sink, v_ln_g, v_ln_b)
    gs, ds, ms, vs = _small_allreduce_adamw(g_small, w_small, m_small, v_small)
    g_bf, g_rb, g_sk, g_lg, g_lb = _unpack_small(gs)
    d_bf, d_rb, d_sk, d_lg, d_lb = _unpack_small(ds)
    m_bf, m_rb, m_sk, m_lg, m_lb = _unpack_small(ms)
    v_bf, v_rb, v_sk, v_lg, v_lb = _unpack_small(vs)

    e = lambda a: a[None]
    return (loss, e(grad_x),
            e(g_w_in), g_bf, g_rb, g_sk, e(g_w_o), g_lg, g_lb,
            e(d_w_in), d_bf, d_rb, d_sk, e(d_w_o), d_lg, d_lb,
            e(nm_w_in), m_bf, m_rb, m_sk, e(nm_w_o), m_lg, m_lb,
            e(nv_w_in), v_bf, v_rb, v_sk, e(nv_w_o), v_lg, v_lb)
```

```python
import functools
import math

import numpy as np
import jax
import jax.numpy as jnp
from jax import lax
from jax.experimental import pallas as pl
from jax.experimental.pallas import tpu as pltpu

F32 = jnp.float32
BF16 = jnp.bfloat16

D_MODEL = 1024
HEAD_DIM = 64
FOX_HEADS = 8
SWA_HEADS = 8
SWA_KV_HEADS = 2
SWA_GROUP = 4
FOX_W = 512
SWA_W = 512
SWA_KV_W = 128
BLOCK = 128
NUM_BUCKETS = 32
MAX_DISTANCE = 128
LN_EPS = 1e-5
NEG = -1e30
ALPHA = 2.0 ** 0.25
QK_SCALE = 0.125

ADAM_LR = 0.001
ADAM_B1 = 0.9
ADAM_B2 = 0.999
ADAM_EPS = 1e-08
ADAM_WD = 0.01
ADAM_STEP = 10

D_IN = 3336
SHARD_PAD = 896
N_A = 2304
N_C = 256
N_B = 1024
OFF_C = N_A
OFF_B = N_A + N_C
N_PAD = N_A + N_C + N_B
COL_FK, COL_FV, COL_SQ, COL_SK, COL_SV = 512, 1024, 1536, 2048, 2176

LANES = 128
FOX_T = 256
FOX_REF = 512
VMEM_LIMIT = 56 * 1024 * 1024

MESH = pl.DeviceIdType.MESH
N_CHIPS = 4
N_DEV = 8
SMALL_ROWS = 24
LOSS_ROW = 20
COPY_PIECES = 4


def _cparams(sem=None):
    return pltpu.CompilerParams(dimension_semantics=sem, vmem_limit_bytes=VMEM_LIMIT)


def _split3(x):
    hi = x.astype(BF16)
    r = x - hi.astype(F32)
    mid = r.astype(BF16)
    lo = (r - mid.astype(F32)).astype(BF16)
    return hi, mid, lo


def _dot(a, b):
    return jnp.dot(a, b, preferred_element_type=F32)


def _dot_nt(a, b):
    return lax.dot_general(a, b, (((1,), (1,)), ((), ())), preferred_element_type=F32)


def _dot_tn(a, b):
    return lax.dot_general(a, b, (((0,), (0,)), ((), ())), preferred_element_type=F32)


def _project(x, w_pad):
    s, k = x.shape
    tm = 512
    chunk = 512

    def kern(x_ref, w_ref, qkv_ref, ff_ref, z_ref, xt_ref, vt_ref):
        xf = x_ref[...]
        xb = xf.astype(BF16)
        xt_ref[...] = xf.T.astype(BF16)
        for c0 in range(0, N_A, chunk):
            width = min(chunk, N_A - c0)
            res = _dot(xb, w_ref[:, c0:c0 + width])
            qkv_ref[:, c0:c0 + width] = res.astype(BF16)
            if c0 == COL_FV:
                vt_ref[...] = res.T.astype(BF16)
        ff_ref[...] = _dot(xb, w_ref[:, OFF_C:OFF_C + N_C])
        for c0 in range(0, N_B, 512):
            z_ref[:, c0:c0 + 512] = _dot(xb, w_ref[:, OFF_B + c0:OFF_B + c0 + 512])

    row = lambda i: (i, 0)
    return pl.pallas_call(
        kern, name="project",
        grid=(s // tm,),
        in_specs=[pl.BlockSpec((tm, k), row),
                  _resident((k, N_PAD), lambda i: (0, 0))],
        out_specs=[pl.BlockSpec((tm, N_A), row),
                   pl.BlockSpec((tm, N_C), row),
                   pl.BlockSpec((tm, N_B), row),
                   pl.BlockSpec((k, tm), lambda i: (0, i)),
                   pl.BlockSpec((FOX_W, tm), lambda i: (0, i))],
        out_shape=[jax.ShapeDtypeStruct((s, N_A), BF16),
                   jax.ShapeDtypeStruct((s, N_C), F32),
                   jax.ShapeDtypeStruct((s, N_B), F32),
                   jax.ShapeDtypeStruct((k, s), BF16),
                   jax.ShapeDtypeStruct((FOX_W, s), BF16)],
        compiler_params=_cparams(("parallel",)),
    )(x, w_pad)


def _grad_x_matmul(pieces, w_pad, dh, *, tm, tn, name):
    m = dh.shape[0]
    n, k = w_pad.shape
    widths = [p.shape[1] for p in pieces]
    offs = [sum(widths[:i]) for i in range(len(pieces))]
    assert sum(widths) == k

    def kern(*refs):
        p_refs, (b_ref, dh_ref, o_ref) = refs[:len(pieces)], refs[len(pieces):]
        acc = ALPHA * dh_ref[...]
        for p_ref, off, width in zip(p_refs, offs, widths):
            acc = acc + _dot_nt(p_ref[...], b_ref[:, off:off + width])
        o_ref[...] = acc

    return pl.pallas_call(
        kern, name=name,
        grid=(n // tn, m // tm),
        in_specs=[pl.BlockSpec((tm, w), lambda j, i: (i, 0)) for w in widths]
        + [pl.BlockSpec((tn, k), lambda j, i: (j, 0)),
           pl.BlockSpec((tm, tn), lambda j, i: (i, j))],
        out_specs=pl.BlockSpec((tm, tn), lambda j, i: (i, j)),
        out_shape=jax.ShapeDtypeStruct((m, n), F32),
        compiler_params=_cparams(("parallel", "parallel")),
    )(*pieces, w_pad, dh)


def _grad_w_matmul(xt, blocks, *, tk, name):
    m, s = xt.shape
    tn = 512
    nb = len(blocks)

    def kern(a_ref, *refs):
        b_refs, o_ref = refs[:nb], refs[nb]
        j = pl.program_id(0)

        @pl.when(pl.program_id(1) == 0)
        def _():
            o_ref[...] = jnp.zeros_like(o_ref)
        for blk in range(nb):
            @pl.when(j == blk)
            def _(blk=blk):
                o_ref[...] += _dot(a_ref[...], b_refs[blk][...])

    def b_spec(blk, col):
        return pl.BlockSpec((tk, tn), lambda j, k: (jnp.where(j == blk, k, 0), col))

    return pl.pallas_call(
        kern, name=name,
        grid=(nb, s // tk),
        in_specs=[pl.BlockSpec((m, tk), lambda j, k: (0, k))]
        + [b_spec(blk, col) for blk, (_, col) in enumerate(blocks)],
        out_specs=pl.BlockSpec((m, tn), lambda j, k: (0, j)),
        out_shape=jax.ShapeDtypeStruct((m, nb * tn), F32),
        compiler_params=_cparams(("parallel", "arbitrary")),
    )(xt, *[arr for arr, _ in blocks])


def _matmul_acc(at, b, *, tm, tn, tk, name):
    m, s = at.shape
    n = b.shape[1]

    def kern(a_ref, b_ref, o_ref):
        @pl.when(pl.program_id(2) == 0)
        def _():
            o_ref[...] = jnp.zeros_like(o_ref)
        o_ref[...] += _dot(a_ref[...], b_ref[...])

    return pl.pallas_call(
        kern, name=name,
        grid=(m // tm, n // tn, s // tk),
        in_specs=[pl.BlockSpec((tm, tk), lambda i, j, k: (i, k)),
                  pl.BlockSpec((tk, tn), lambda i, j, k: (k, j))],
        out_specs=pl.BlockSpec((tm, tn), lambda i, j, k: (i, j)),
        out_shape=jax.ShapeDtypeStruct((m, n), F32),
        compiler_params=_cparams(("parallel", "parallel", "arbitrary")),
    )(at, b)


def _tri(n, lower):
    r = lax.broadcasted_iota(jnp.int32, (n, n), 0)
    c = lax.broadcasted_iota(jnp.int32, (n, n), 1)
    keep = (c <= r) if lower else (c >= r)
    return jnp.where(keep, 1.0, 0.0).astype(BF16)


def _exact_dot(mat_bf16, x_f32, left):
    out = None
    for piece in _split3(x_f32):
        t = _dot(mat_bf16, piece) if left else _dot(piece, mat_bf16)
        out = t if out is None else out + t
    return out


def _log_sigmoid(z):
    return jnp.minimum(z, 0.0) - jnp.log(1.0 + jnp.exp(-jnp.abs(z)))


def _cum_fwd(ffp, bfp):
    s = ffp.shape[0]
    t = min(512, s)

    def kern(ff_ref, b_ref, cum_ref, carry_ref):
        @pl.when(pl.program_id(0) == 0)
        def _():
            carry_ref[...] = jnp.zeros_like(carry_ref)
        lane = lax.broadcasted_iota(jnp.int32, (1, LANES), 1)
        lf = _log_sigmoid(ff_ref[...] + b_ref[...])
        lf = jnp.where(lane < FOX_HEADS, lf, 0.0)
        cum = _exact_dot(_tri(t, True), lf, True) + carry_ref[0:1, :]
        cum_ref[...] = cum
        carry_ref[...] = jnp.broadcast_to(cum[t - 1:t, :], carry_ref.shape)

    return pl.pallas_call(
        kern, name="cum_fwd",
        grid=(s // t,),
        in_specs=[pl.BlockSpec((t, LANES), lambda i: (i, 0)),
                  pl.BlockSpec((1, LANES), lambda i: (0, 0))],
        out_specs=pl.BlockSpec((t, LANES), lambda i: (i, 0)),
        out_shape=jax.ShapeDtypeStruct((s, LANES), F32),
        scratch_shapes=[pltpu.VMEM((8, LANES), F32)],
        compiler_params=_cparams(("arbitrary",)),
    )(ffp, bfp)


def _cum_bwd(dcum_k, dcum_q, ffp, bfp):
    s = dcum_k.shape[0]
    t = min(512, s)
    nb = s // t

    def kern(dck_ref, dcq_ref, ff_ref, b_ref, dff_ref, gb_ref, carry_ref):
        @pl.when(pl.program_id(0) == 0)
        def _():
            carry_ref[...] = jnp.zeros_like(carry_ref)
            gb_ref[...] = jnp.zeros_like(gb_ref)
        lane = lax.broadcasted_iota(jnp.int32, (1, LANES), 1)
        dlf = _exact_dot(_tri(t, False), dck_ref[...] + dcq_ref[...], True) + carry_ref[0:1, :]
        carry_ref[...] = jnp.broadcast_to(dlf[0:1, :], carry_ref.shape)
        z = ff_ref[...] + b_ref[...]
        dff = jnp.where(lane < FOX_HEADS, dlf / (1.0 + jnp.exp(z)), 0.0)
        gb_ref[...] += jnp.broadcast_to(jnp.sum(dff, axis=0, keepdims=True), gb_ref.shape)
        dff_ref[...] = jnp.concatenate([dff, jnp.zeros_like(dff)], axis=1).astype(BF16)

    return pl.pallas_call(
        kern, name="cum_bwd",
        grid=(nb,),
        in_specs=[pl.BlockSpec((t, LANES), lambda i: (nb - 1 - i, 0)),
                  pl.BlockSpec((t, LANES), lambda i: (nb - 1 - i, 0)),
                  pl.BlockSpec((t, LANES), lambda i: (nb - 1 - i, 0)),
                  pl.BlockSpec((1, LANES), lambda i: (0, 0))],
        out_specs=[pl.BlockSpec((t, N_C), lambda i: (nb - 1 - i, 0)),
                   pl.BlockSpec((8, LANES), lambda i: (0, 0))],
        out_shape=[jax.ShapeDtypeStruct((s, N_C), BF16),
                   jax.ShapeDtypeStruct((8, LANES), F32)],
        scratch_shapes=[pltpu.VMEM((8, LANES), F32)],
        compiler_params=_cparams(("arbitrary",)),
    )(dcum_k, dcum_q, ffp, bfp)


def _resident(shape, index_map):
    return pl.BlockSpec(shape, index_map, pipeline_mode=pl.Buffered(1))


def _fox_fwd(qkv, vt, cum_t3, cum):
    s = qkv.shape[0]
    tk = min(FOX_T, s)
    tq = FOX_REF
    nq = s // tq
    nh = FOX_HEADS
    diag_tiles = tq // tk

    def kern(q_ref, k_ref, vt_ref, ct_ref, c_ref, o_ref, lse_ref, m_ref, l_ref, acc_ref, u_ref):
        i = pl.program_id(0)
        lane = lax.broadcasted_iota(jnp.int32, (1, LANES), 1)
        krow = lax.broadcasted_iota(jnp.int32, (tk, tq), 0)
        qcol = lax.broadcasted_iota(jnp.int32, (tk, tq), 1)
        q0 = pl.multiple_of(i * tq, tq)
        qts, crefs = [], []
        for h in range(nh):
            p, a = divmod(h, 2)
            q2 = q_ref[:, p * LANES:(p + 1) * LANES] * jnp.asarray(QK_SCALE, BF16)
            sel = (lane < HEAD_DIM) if a == 0 else (lane >= HEAD_DIM)
            qts.append(jnp.where(sel, q2, jnp.zeros_like(q2)).astype(F32).T.astype(BF16))
            crefs.append(ct_ref[h, :, pl.ds(q0, LANES)][:, 0:1])
        m_ref[...] = jnp.full(m_ref.shape, NEG, F32)
        l_ref[...] = jnp.zeros_like(l_ref)
        acc_ref[...] = jnp.zeros_like(acc_ref)

        def tile(j, diag):
            k0 = pl.multiple_of(j * tk, tk)
            cb = c_ref[pl.ds(k0, tk), :]
            sts = [_dot(k_ref[pl.ds(k0, tk), (h // 2) * LANES:(h // 2 + 1) * LANES], qts[h]) for h in range(nh)]
            tile_max = []
            for h in range(nh):
                u = sts[h] - (cb[:, h:h + 1] - crefs[h])
                if diag is not None:
                    u = jnp.where(krow + diag * tk <= qcol, u, NEG)
                u_ref[h] = u
                tile_max.append(jnp.max(u, axis=0, keepdims=True))
            pts, scales = [], []
            for h in range(nh):
                m_old = m_ref[h]
                m_new = jnp.maximum(m_old, tile_max[h])
                scale = jnp.exp(m_old - m_new)
                p = jnp.exp(u_ref[h] - m_new)
                l_ref[h] = scale * l_ref[h] + jnp.sum(p, axis=0, keepdims=True)
                m_ref[h] = m_new
                pts.append(p.astype(BF16))
                scales.append(scale)
            for h in range(nh):
                vth = vt_ref[h * HEAD_DIM:(h + 1) * HEAD_DIM, pl.ds(k0, tk)]
                acc_ref[h] = scales[h] * acc_ref[h] + _dot(vth, pts[h])

        def body(j, c):
            tile(j, None)
            return c
        lax.fori_loop(0, i * diag_tiles, body, 0)
        for d in range(diag_tiles):
            tile(i * diag_tiles + d, d)

        for p in range(nh // 2):
            ot = jnp.concatenate([acc_ref[2 * p + a] * (1.0 / l_ref[2 * p + a]) for a in range(2)], axis=0)
            o_ref[:, p * LANES:(p + 1) * LANES] = ot.T
        for h in range(nh):
            lse_ref[h, :, pl.ds(q0, tq)] = m_ref[h] + jnp.log(l_ref[h])

    return pl.pallas_call(
        kern, name="fox_fwd",
        grid=(nq,),
        in_specs=[pl.BlockSpec((tq, FOX_W), lambda i: (i, 0)),
                  _resident((s, FOX_W), lambda i: (0, COL_FK // FOX_W)),
                  _resident((FOX_W, s), lambda i: (0, 0)),
                  _resident((nh, 1, s), lambda i: (0, 0, 0)),
                  _resident((s, LANES), lambda i: (0, 0))],
        out_specs=[pl.BlockSpec((tq, FOX_W), lambda i: (i, 0)),
                   pl.BlockSpec((nh, 1, s), lambda i: (0, 0, 0))],
        out_shape=[jax.ShapeDtypeStruct((s, FOX_W), F32),
                   jax.ShapeDtypeStruct((nh, 1, s), F32)],
        scratch_shapes=[pltpu.VMEM((nh, 1, tq), F32),
                        pltpu.VMEM((nh, 1, tq), F32),
                        pltpu.VMEM((nh, HEAD_DIM, tq), F32),
                        pltpu.VMEM((nh, tk, tq), F32)],
        compiler_params=_cparams(("arbitrary",)),
    )(qkv, qkv, vt, cum_t3, cum)


def _fox_bwd(qkv, do_bf, cum_t3, cum, lse_t3, delta_t3):
    s = qkv.shape[0]
    t = min(FOX_T, s)
    nq = s // t
    nh = FOX_HEADS
    npair = nh // 2

    def kern(q_ref, do_ref, k_ref, v_ref, ct_ref, c_ref, lse_ref, dl_ref,
             dqt_ref, dk_ref, dv_ref, dc_ref, dcq_ref, accv_ref, acck_ref, accd_ref):
        kj = pl.program_id(0)
        lane = lax.broadcasted_iota(jnp.int32, (1, LANES), 1)
        krow = lax.broadcasted_iota(jnp.int32, (t, t), 0)
        qcol = lax.broadcasted_iota(jnp.int32, (t, t), 1)
        causal = krow <= qcol
        sels = [lane < HEAD_DIM, lane >= HEAD_DIM]

        @pl.when(kj == 0)
        def _():
            dqt_ref[...] = jnp.zeros_like(dqt_ref)
            dcq_ref[...] = jnp.zeros_like(dcq_ref)

        accv_ref[...] = jnp.zeros_like(accv_ref)
        acck_ref[...] = jnp.zeros_like(acck_ref)
        accd_ref[...] = jnp.zeros_like(accd_ref)
        cb = c_ref[...]
        k2s, v2s, kts = [], [], []
        for p in range(npair):
            k2 = k_ref[:, p * LANES:(p + 1) * LANES]
            k2s.append(k2)
            v2s.append(v_ref[:, p * LANES:(p + 1) * LANES])
            kt = k2.astype(F32).T * QK_SCALE
            kts.append(kt[:HEAD_DIM].astype(BF16))
            kts.append(kt[HEAD_DIM:].astype(BF16))
        css = [cb[:, h:h + 1] for h in range(nh)]

        def tile(i, masked):
            q0 = pl.multiple_of(i * t, t)
            r0 = pl.multiple_of((i // (FOX_REF // t)) * FOX_REF, FOX_REF)
            sts, dpts, qms, doms = [], [], [], []
            for h in range(nh):
                p, a = divmod(h, 2)
                qi = q_ref[pl.ds(q0, t), p * LANES:(p + 1) * LANES] * jnp.asarray(QK_SCALE, BF16)
                doi = do_ref[pl.ds(q0, t), p * LANES:(p + 1) * LANES]
                qm = jnp.where(sels[a], qi, jnp.zeros_like(qi))
                dom = jnp.where(sels[a], doi, jnp.zeros_like(doi))
                qms.append(qm)
                doms.append(dom)
                sts.append(_dot_nt(k2s[p], qm))
                dpts.append(_dot_nt(v2s[p], dom))
            pts, dsts = [], []
            for h in range(nh):
                cref = ct_ref[h, :, pl.ds(r0, LANES)][:, 0:1]
                pt = jnp.exp(sts[h] - (css[h] - cref) - lse_ref[h, :, pl.ds(q0, t)])
                if masked:
                    pt = jnp.where(causal, pt, 0.0)
                ds32 = pt * (dpts[h] - dl_ref[h, :, pl.ds(q0, t)])
                part = ds32[:, 0:LANES]
                for c in range(1, t // LANES):
                    part = part + ds32[:, c * LANES:(c + 1) * LANES]
                accd_ref[h] += part
                dcq_ref[h, :, pl.ds(q0, t)] += jnp.sum(ds32, axis=0, keepdims=True)
                pts.append(pt.astype(BF16))
                dsts.append(ds32.astype(BF16))
            for p in range(npair):
                ha, hb = 2 * p, 2 * p + 1
                accv_ref[p] += _dot(pts[ha], doms[ha]) + _dot(pts[hb], doms[hb])
                acck_ref[p] += _dot(dsts[ha], qms[ha]) + _dot(dsts[hb], qms[hb])
            for h in range(nh):
                dqt_ref[h * HEAD_DIM:(h + 1) * HEAD_DIM, pl.ds(q0, t)] += _dot(kts[h], dsts[h])

        tile(kj, True)

        def body(i, c):
            tile(i, False)
            return c
        lax.fori_loop(kj + 1, nq, body, 0)

        dc = jnp.zeros((t, LANES), F32)
        for h in range(nh):
            dc = jnp.where(lane == h, -jnp.sum(accd_ref[h], axis=1, keepdims=True), dc)
        dc_ref[...] = dc
        for p in range(npair):
            dv_ref[:, p * LANES:(p + 1) * LANES] = accv_ref[p].astype(BF16)
            dk_ref[:, p * LANES:(p + 1) * LANES] = acck_ref[p].astype(BF16)

    whole = lambda kj: (0, 0, 0)
    return pl.pallas_call(
        kern, name="fox_bwd",
        grid=(nq,),
        in_specs=[_resident((s, FOX_W), lambda kj: (0, 0)),
                  _resident((s, FOX_W), lambda kj: (0, 0)),
                  pl.BlockSpec((t, FOX_W), lambda kj: (kj, COL_FK // FOX_W)),
                  pl.BlockSpec((t, FOX_W), lambda kj: (kj, COL_FV // FOX_W)),
                  _resident((nh, 1, s), whole),
                  pl.BlockSpec((t, LANES), lambda kj: (kj, 0)),
                  _resident((nh, 1, s), whole),
                  _resident((nh, 1, s), whole)],
        out_specs=[_resident((FOX_W, s), lambda kj: (0, 0)),
                   pl.BlockSpec((t, FOX_W), lambda kj: (kj, 0)),
                   pl.BlockSpec((t, FOX_W), lambda kj: (kj, 0)),
                   pl.BlockSpec((t, LANES), lambda kj: (kj, 0)),
                   _resident((nh, 1, s), whole)],
        out_shape=[jax.ShapeDtypeStruct((FOX_W, s), F32),
                   jax.ShapeDtypeStruct((s, FOX_W), BF16),
                   jax.ShapeDtypeStruct((s, FOX_W), BF16),
                   jax.ShapeDtypeStruct((s, LANES), F32),
                   jax.ShapeDtypeStruct((nh, 1, s), F32)],
        scratch_shapes=[pltpu.VMEM((npair, t, LANES), F32),
                        pltpu.VMEM((npair, t, LANES), F32),
                        pltpu.VMEM((nh, t, LANES), F32)],
        compiler_params=_cparams(("arbitrary",)),
    )(qkv, do_bf, qkv, qkv, cum_t3, cum, lse_t3, delta_t3)


def _bucket_table():
    qi = np.arange(BLOCK)[:, None]
    kj = np.arange(2 * BLOCK)[None, :]
    rel = np.maximum(qi + BLOCK - kj, 0).astype(np.int32)
    max_exact = NUM_BUCKETS // 2
    relf = np.maximum(rel, 1).astype(np.float32)
    large = max_exact + (np.log(relf / np.float32(max_exact)) / np.float32(math.log(MAX_DISTANCE / max_exact))
                         * np.float32(NUM_BUCKETS - max_exact)).astype(np.int32)
    large = np.minimum(large, NUM_BUCKETS - 1)
    return np.where(rel < max_exact, rel, large).astype(np.int32)


def _swa_bias(rel_bias, bucket):
    def kern(rb_ref, bk_ref, o_ref):
        bk = bk_ref[...]
        for h in range(SWA_HEADS):
            acc = jnp.zeros((BLOCK, 2 * BLOCK), F32)
            for b in range(NUM_BUCKETS):
                acc = jnp.where(bk == b, rb_ref[b, h], acc)
            o_ref[h] = acc

    return pl.pallas_call(
        kern, name="swa_bias",
        in_specs=[pl.BlockSpec(memory_space=pltpu.SMEM),
                  pl.BlockSpec(memory_space=pltpu.VMEM)],
        out_specs=pl.BlockSpec(memory_space=pltpu.VMEM),
        out_shape=jax.ShapeDtypeStruct((SWA_HEADS, BLOCK, 2 * BLOCK), F32),
        compiler_params=_cparams(),
    )(rel_bias, bucket)


def _swa_mask(n):
    qi = lax.broadcasted_iota(jnp.int32, (BLOCK, 2 * BLOCK), 0)
    kj = lax.broadcasted_iota(jnp.int32, (BLOCK, 2 * BLOCK), 1)
    rel = qi + BLOCK - kj
    band = (rel >= 0) & (rel < BLOCK)
    return band & ((kj >= BLOCK) | (n > 0))


def _swa_fwd(qkv, bias, sink):
    s = qkv.shape[0]
    nb = s // BLOCK

    def kern(q_ref, kp_ref, kc_ref, vp_ref, vc_ref, bias_ref, sink_ref, o_ref, lse_ref):
        n = pl.program_id(0)
        mask = _swa_mask(n)
        lane = lax.broadcasted_iota(jnp.int32, (1, LANES), 1)
        q = q_ref[...] * jnp.asarray(QK_SCALE, BF16)
        k = jnp.concatenate([kp_ref[...], kc_ref[...]], axis=0)
        v = jnp.concatenate([vp_ref[...], vc_ref[...]], axis=0)
        kgs = [k[:, g * HEAD_DIM:(g + 1) * HEAD_DIM] for g in range(SWA_KV_HEADS)]
        vgs = [v[:, g * HEAD_DIM:(g + 1) * HEAD_DIM] for g in range(SWA_KV_HEADS)]
        raw = [_dot_nt(q[:, h * HEAD_DIM:(h + 1) * HEAD_DIM], kgs[h // SWA_GROUP]) for h in range(SWA_HEADS)]
        probs = []
        lse_all = jnp.zeros((BLOCK, LANES), F32)
        for h in range(SWA_HEADS):
            sc = jnp.where(mask, raw[h] + bias_ref[h], NEG)
            sk = sink_ref[0, h]
            m = jnp.maximum(jnp.max(sc, axis=1, keepdims=True), sk)
            p = jnp.exp(sc - m)
            l = jnp.sum(p, axis=1, keepdims=True) + jnp.exp(sk - m)
            probs.append((p * (1.0 / l)).astype(BF16))
            lse_all = jnp.where(lane == h, m + jnp.log(l), lse_all)
        outs = [_dot(probs[h], vgs[h // SWA_GROUP]) for h in range(SWA_HEADS)]
        o_ref[...] = jnp.concatenate(outs, axis=1)
        lse_ref[...] = lse_all

    cq, ck, cv = COL_SQ // SWA_W, COL_SK // LANES, COL_SV // LANES
    prev = lambda n: jnp.maximum(n - 1, 0)
    return pl.pallas_call(
        kern, name="swa_fwd",
        grid=(nb,),
        in_specs=[pl.BlockSpec((BLOCK, SWA_W), lambda n: (n, cq)),
                  pl.BlockSpec((BLOCK, LANES), lambda n: (prev(n), ck)),
                  pl.BlockSpec((BLOCK, LANES), lambda n: (n, ck)),
                  pl.BlockSpec((BLOCK, LANES), lambda n: (prev(n), cv)),
                  pl.BlockSpec((BLOCK, LANES), lambda n: (n, cv)),
                  pl.BlockSpec((SWA_HEADS, BLOCK, 2 * BLOCK), lambda n: (0, 0, 0)),
                  pl.BlockSpec(memory_space=pltpu.SMEM)],
        out_specs=[pl.BlockSpec((BLOCK, SWA_W), lambda n: (n, 0)),
                   pl.BlockSpec((BLOCK, LANES), lambda n: (n, 0))],
        out_shape=[jax.ShapeDtypeStruct((s, SWA_W), F32),
                   jax.ShapeDtypeStruct((s, LANES), F32)],
        compiler_params=_cparams(("parallel",)),
    )(qkv, qkv, qkv, qkv, qkv, bias, sink)


def _swa_bwd(qkv, do_bf, delta, lse, bias, sink, bucket):
    s = qkv.shape[0]
    nb = s // BLOCK

    def kern(q_ref, kp_ref, kc_ref, vp_ref, vc_ref, do_ref, dl_ref, lse_ref, bias_ref, sink_ref, bk_ref,
             dq_ref, dk_ref, dv_ref, grb_ref, gsk_ref, dbias_ref, ck_ref, cv_ref, sk_ref):
        n = pl.program_id(0)
        lane = lax.broadcasted_iota(jnp.int32, (1, LANES), 1)

        @pl.when(n == 0)
        def _():
            dbias_ref[...] = jnp.zeros_like(dbias_ref)
            ck_ref[...] = jnp.zeros_like(ck_ref)
            cv_ref[...] = jnp.zeros_like(cv_ref)
            sk_ref[...] = jnp.zeros_like(sk_ref)

        @pl.when(n < nb)
        def _():
            mask = _swa_mask(n)
            q = q_ref[...] * jnp.asarray(QK_SCALE, BF16)
            k = jnp.concatenate([kp_ref[...], kc_ref[...]], axis=0)
            v = jnp.concatenate([vp_ref[...], vc_ref[...]], axis=0)
            do = do_ref[...]
            dl = dl_ref[...]
            lse_all = lse_ref[...]
            dks = [None] * SWA_KV_HEADS
            dvs = [None] * SWA_KV_HEADS
            gsk = jnp.zeros((1, LANES), F32)
            kgs = [k[:, g * HEAD_DIM:(g + 1) * HEAD_DIM] for g in range(SWA_KV_HEADS)]
            vgs = [v[:, g * HEAD_DIM:(g + 1) * HEAD_DIM] for g in range(SWA_KV_HEADS)]
            qhs = [q[:, h * HEAD_DIM:(h + 1) * HEAD_DIM] for h in range(SWA_HEADS)]
            dohs = [do[:, h * HEAD_DIM:(h + 1) * HEAD_DIM] for h in range(SWA_HEADS)]
            raw = [_dot_nt(qhs[h], kgs[h // SWA_GROUP]) for h in range(SWA_HEADS)]
            dps = [_dot_nt(dohs[h], vgs[h // SWA_GROUP]) for h in range(SWA_HEADS)]
            ps, dss = [], []
            for h in range(SWA_HEADS):
                lse_h = lse_all[:, h:h + 1]
                dlt = dl[:, FOX_HEADS + h:FOX_HEADS + h + 1]
                sc = jnp.where(mask, raw[h] + bias_ref[h], NEG)
                p = jnp.exp(sc - lse_h)
                ds = p * (dps[h] - dlt)
                dbias_ref[h] += ds
                p_sink = jnp.exp(sink_ref[0, h] - lse_h)
                gsk = gsk + jnp.where(lane == h, -jnp.sum(p_sink * dlt), 0.0)
                ps.append(p.astype(BF16))
                dss.append(ds.astype(BF16))
            dqs = [_dot(dss[h], kgs[h // SWA_GROUP]) * QK_SCALE for h in range(SWA_HEADS)]
            for h in range(SWA_HEADS):
                g = h // SWA_GROUP
                dk_h = _dot_tn(dss[h], qhs[h])
                dv_h = _dot_tn(ps[h], dohs[h])
                dks[g] = dk_h if dks[g] is None else dks[g] + dk_h
                dvs[g] = dv_h if dvs[g] is None else dvs[g] + dv_h
            dq_ref[...] = jnp.concatenate(dqs, axis=1).astype(BF16)
            sk_ref[...] += jnp.broadcast_to(gsk, sk_ref.shape)
            dk2 = jnp.concatenate(dks, axis=1)
            dv2 = jnp.concatenate(dvs, axis=1)
            dk_ref[...] = (ck_ref[...] + dk2[:BLOCK]).astype(BF16)
            dv_ref[...] = (cv_ref[...] + dv2[:BLOCK]).astype(BF16)
            ck_ref[...] = dk2[BLOCK:]
            cv_ref[...] = dv2[BLOCK:]

        @pl.when(n == nb)
        def _():
            dk_ref[...] = ck_ref[...].astype(BF16)
            dv_ref[...] = cv_ref[...].astype(BF16)
            gsk_ref[...] = sk_ref[...]
            bk = bk_ref[...]
            rowi = lax.broadcasted_iota(jnp.int32, (NUM_BUCKETS, LANES), 0)
            lanei = lax.broadcasted_iota(jnp.int32, (NUM_BUCKETS, LANES), 1)
            out = jnp.zeros((NUM_BUCKETS, LANES), F32)
            for h in range(SWA_HEADS):
                db = dbias_ref[h]
                for b in range(NUM_BUCKETS):
                    val = jnp.sum(jnp.where(bk == b, db, 0.0))
                    out = jnp.where((rowi == b) & (lanei == h), val, out)
            grb_ref[...] = out

    cq, ck, cv = COL_SQ // SWA_W, COL_SK // LANES, COL_SV // LANES
    cur = lambda n: jnp.minimum(n, nb - 1)
    prev = lambda n: jnp.maximum(jnp.minimum(n, nb - 1) - 1, 0)
    kout = lambda n: jnp.maximum(n - 1, 0)
    return pl.pallas_call(
        kern, name="swa_bwd",
        grid=(nb + 1,),
        in_specs=[pl.BlockSpec((BLOCK, SWA_W), lambda n: (cur(n), cq)),
                  pl.BlockSpec((BLOCK, LANES), lambda n: (prev(n), ck)),
                  pl.BlockSpec((BLOCK, LANES), lambda n: (cur(n), ck)),
                  pl.BlockSpec((BLOCK, LANES), lambda n: (prev(n), cv)),
                  pl.BlockSpec((BLOCK, LANES), lambda n: (cur(n), cv)),
                  pl.BlockSpec((BLOCK, SWA_W), lambda n: (cur(n), 1)),
                  pl.BlockSpec((BLOCK, LANES), lambda n: (cur(n), 0)),
                  pl.BlockSpec((BLOCK, LANES), lambda n: (cur(n), 0)),
                  pl.BlockSpec((SWA_HEADS, BLOCK, 2 * BLOCK), lambda n: (0, 0, 0)),
                  pl.BlockSpec(memory_space=pltpu.SMEM),
                  pl.BlockSpec((BLOCK, 2 * BLOCK), lambda n: (0, 0))],
        out_specs=[pl.BlockSpec((BLOCK, SWA_W), lambda n: (cur(n), 0)),
                   pl.BlockSpec((BLOCK, LANES), lambda n: (kout(n), 0)),
                   pl.BlockSpec((BLOCK, LANES), lambda n: (kout(n), 0)),
                   pl.BlockSpec((NUM_BUCKETS, LANES), lambda n: (0, 0)),
                   pl.BlockSpec((8, LANES), lambda n: (0, 0))],
        out_shape=[jax.ShapeDtypeStruct((s, SWA_W), BF16),
                   jax.ShapeDtypeStruct((s, LANES), BF16),
                   jax.ShapeDtypeStruct((s, LANES), BF16),
                   jax.ShapeDtypeStruct((NUM_BUCKETS, LANES), F32),
                   jax.ShapeDtypeStruct((8, LANES), F32)],
        scratch_shapes=[pltpu.VMEM((SWA_HEADS, BLOCK, 2 * BLOCK), F32),
                        pltpu.VMEM((BLOCK, LANES), F32),
                        pltpu.VMEM((BLOCK, LANES), F32),
                        pltpu.VMEM((8, LANES), F32)],
        compiler_params=_cparams(("arbitrary",)),
    )(qkv, qkv, qkv, qkv, qkv, do_bf, delta, lse, bias, sink, bucket)


def _post(x, target, o_fox, o_swa, z, w_o, ln_g, ln_b):
    s = x.shape[0]
    tm = min(256, s)
    nt = s // tm

    def kern(x_ref, t_ref, of_ref, os_ref, z_ref, w_ref, g_ref, b_ref,
             loss_ref, dh_ref, dy_ref, mix_ref, do_ref, dz_ref, dl_ref, gg_ref, gb_ref, lacc_ref):
        step = pl.program_id(0)

        @pl.when(step == 0)
        def _():
            lacc_ref[...] = jnp.zeros_like(lacc_ref)
            gg_ref[...] = jnp.zeros_like(gg_ref)
            gb_ref[...] = jnp.zeros_like(gb_ref)

        o = jnp.concatenate([of_ref[...], os_ref[...]], axis=1)
        zz = z_ref[...]
        sig = 1.0 / (1.0 + jnp.exp(-zz))
        silu = zz * sig
        mixed32 = o * silu
        mixed = mixed32.astype(BF16)
        mix_ref[...] = mixed32.T.astype(BF16)
        w = w_ref[...]
        h = ALPHA * x_ref[...] + _dot(mixed, w)
        mu = jnp.mean(h, axis=1, keepdims=True)
        hc = h - mu
        var = jnp.mean(hc * hc, axis=1, keepdims=True)
        rstd = lax.rsqrt(var + LN_EPS)
        xhat = hc * rstd
        g = g_ref[...]
        err = xhat * g + b_ref[...] - t_ref[...]
        lacc_ref[...] += jnp.broadcast_to(jnp.sum(err * err, axis=0, keepdims=True), lacc_ref.shape)
        dout = err * (1.0 / D_MODEL)
        gg_ref[...] += jnp.broadcast_to(jnp.sum(dout * xhat, axis=0, keepdims=True), gg_ref.shape)
        gb_ref[...] += jnp.broadcast_to(jnp.sum(dout, axis=0, keepdims=True), gb_ref.shape)
        dxh = dout * g
        m1 = jnp.mean(dxh, axis=1, keepdims=True)
        m2 = jnp.mean(dxh * xhat, axis=1, keepdims=True)
        dh = rstd * (dxh - m1 - xhat * m2)
        dh_ref[...] = dh
        dy = dh.astype(BF16)
        dy_ref[...] = dy
        dmix = _dot_nt(dy, w)
        do = dmix * silu
        do_ref[...] = do.astype(BF16)
        dz_ref[...] = (dmix * o * (sig * (1.0 + zz * (1.0 - sig)))).astype(BF16)
        r = lax.broadcasted_iota(jnp.int32, (D_MODEL, LANES), 0) // HEAD_DIM
        c = lax.broadcasted_iota(jnp.int32, (D_MODEL, LANES), 1)
        pick = jnp.where(r == c, 1.0, 0.0).astype(BF16)
        dl_ref[...] = _exact_dot(pick, do * o, False)

        @pl.when(step == nt - 1)
        def _():
            tot = jnp.sum(lacc_ref[0:1, :]) * (0.5 / D_MODEL)
            loss_ref[...] = jnp.broadcast_to(tot, loss_ref.shape)

    row = lambda i: (i, 0)
    fixed = lambda i: (0, 0)
    wide = pl.BlockSpec((tm, D_MODEL), row)
    half = pl.BlockSpec((tm, FOX_W), row)
    return pl.pallas_call(
        kern, name="post",
        grid=(nt,),
        in_specs=[wide, wide, half, half, wide,
                  pl.BlockSpec((D_MODEL, D_MODEL), fixed),
                  pl.BlockSpec((1, D_MODEL), fixed),
                  pl.BlockSpec((1, D_MODEL), fixed)],
        out_specs=[pl.BlockSpec((8, LANES), fixed), wide, wide,
                   pl.BlockSpec((D_MODEL, tm), lambda i: (0, i)), wide, wide,
                   pl.BlockSpec((tm, LANES), row),
                   pl.BlockSpec((8, D_MODEL), fixed), pl.BlockSpec((8, D_MODEL), fixed)],
        out_shape=[jax.ShapeDtypeStruct((8, LANES), F32),
                   jax.ShapeDtypeStruct((s, D_MODEL), F32),
                   jax.ShapeDtypeStruct((s, D_MODEL), BF16),
                   jax.ShapeDtypeStruct((D_MODEL, s), BF16),
                   jax.ShapeDtypeStruct((s, D_MODEL), BF16),
                   jax.ShapeDtypeStruct((s, D_MODEL), BF16),
                   jax.ShapeDtypeStruct((s, LANES), F32),
                   jax.ShapeDtypeStruct((8, D_MODEL), F32),
                   jax.ShapeDtypeStruct((8, D_MODEL), F32)],
        scratch_shapes=[pltpu.VMEM((8, D_MODEL), F32)],
        compiler_params=_cparams(("arbitrary",)),
    )(x, target, o_fox, o_swa, z, w_o, ln_g, ln_b)


def _adamw_math(w, g, m, v):
    m = ADAM_B1 * m + (1.0 - ADAM_B1) * g
    v = ADAM_B2 * v + (1.0 - ADAM_B2) * (g * g)
    m_hat = m / (1.0 - ADAM_B1 ** ADAM_STEP)
    v_hat = v / (1.0 - ADAM_B2 ** ADAM_STEP)
    delta = -ADAM_LR * (m_hat / (jnp.sqrt(v_hat) + ADAM_EPS) + ADAM_WD * w)
    return delta, m, v


def _adamw(w, g, m, v, *, name):
    r, c = w.shape
    tr = min(256, r)

    def kern(w_ref, g_ref, m_ref, v_ref, d_ref, mo_ref, vo_ref):
        d, mn, vn = _adamw_math(w_ref[...], g_ref[...], m_ref[...], v_ref[...])
        d_ref[...] = d
        mo_ref[...] = mn
        vo_ref[...] = vn

    blk = pl.BlockSpec((tr, c), lambda i: (i, 0))
    sds = jax.ShapeDtypeStruct((r, c), F32)
    return pl.pallas_call(
        kern, name=name,
        grid=(r // tr,),
        in_specs=[blk, blk, blk, blk],
        out_specs=[blk, blk, blk],
        out_shape=[sds, sds, sds],
        compiler_params=_cparams(("parallel",)),
    )(w, g, m, v)


def _adamw_cols(w, g, m, v, *, name):
    c, _, r = w.shape
    tc = 139
    assert c % tc == 0

    def kern(w_ref, g_ref, m_ref, v_ref, d_ref, mo_ref, vo_ref):
        d, mn, vn = _adamw_math(w_ref[...], g_ref[...], m_ref[...], v_ref[...])
        d_ref[...] = d
        mo_ref[...] = mn
        vo_ref[...] = vn

    blk = pl.BlockSpec((tc, 1, r), lambda i: (i, 0, 0))
    sds = jax.ShapeDtypeStruct((c, 1, r), F32)
    return pl.pallas_call(
        kern, name=name,
        grid=(c // tc,),
        in_specs=[blk, blk, blk, blk],
        out_specs=[blk, blk, blk],
        out_shape=[sds, sds, sds],
        compiler_params=_cparams(("parallel",)),
    )(w, g, m, v)


def _position():
    x, y, c = lax.axis_index("x"), lax.axis_index("y"), lax.axis_index("c")
    chips = [(1 - x, y), (x, 1 - y), (1 - x, 1 - y)]
    return x, y, c, chips


def _chip_index(cx, cy):
    return 2 * cx + cy


def _gather_weights(w_in_bf, w_o_bf):
    shards = (w_in_bf, w_o_bf)
    n_arr = len(shards)

    def kern(*refs):
        ins, outs = refs[:n_arr], refs[n_arr:2 * n_arr]
        send_sems, recv_sems, local_sems = refs[2 * n_arr:]
        x, y, c, chips = _position()
        me = _chip_index(x, y)
        sibling = (x, y, 1 - c)

        local = [pltpu.make_async_copy(ins[a], outs[a].at[me], local_sems.at[a]) for a in range(n_arr)]
        for cp in local:
            cp.start()

        def half(ref, a):
            rows = shards[a].shape[0] // 2
            return ref.at[pl.ds(c * rows, rows), :]

        def copy(a, k, src, slot, to):
            return pltpu.make_async_remote_copy(
                src_ref=src, dst_ref=half(outs[a].at[slot], a),
                send_sem=send_sems.at[a * 6 + k], recv_sem=recv_sems.at[a * 6 + k],
                device_id=to, device_id_type=MESH)

        first = [copy(a, j, half(ins[a], a), me, (*chip, c)) for a in range(n_arr) for j, chip in enumerate(chips)]
        for cp in first:
            cp.start()
        passed = []
        for a in range(n_arr):
            for j, chip in enumerate(chips):
                slot = _chip_index(*chip)
                copy(a, j, half(ins[a], a), slot, (*chip, c)).wait_recv()
                fwd = copy(a, 3 + j, half(outs[a].at[slot], a), slot, sibling)
                fwd.start()
                passed.append(fwd)
        for a in range(n_arr):
            for j, chip in enumerate(chips):
                slot = _chip_index(*chip)
                rows = shards[a].shape[0] // 2
                dst = outs[a].at[slot].at[pl.ds((1 - c) * rows, rows), :]
                pltpu.make_async_remote_copy(
                    src_ref=dst, dst_ref=dst, send_sem=send_sems.at[a * 6 + 3 + j],
                    recv_sem=recv_sems.at[a * 6 + 3 + j], device_id=sibling, device_id_type=MESH).wait_recv()
        for cp in first + passed:
            cp.wait_send()
        for cp in local:
            cp.wait()

    vmem = pl.BlockSpec(memory_space=pltpu.VMEM)
    return pl.pallas_call(
        kern, name="gather_weights",
        in_specs=[vmem] * n_arr,
        out_specs=[vmem] * n_arr,
        out_shape=[jax.ShapeDtypeStruct((N_CHIPS,) + w.shape, w.dtype) for w in shards],
        scratch_shapes=[pltpu.SemaphoreType.DMA((6 * n_arr,)),
                        pltpu.SemaphoreType.DMA((6 * n_arr,)),
                        pltpu.SemaphoreType.DMA((n_arr,))],
        compiler_params=_cparams(),
    )(*shards)


def _swap_halves(grads):
    n_arr = len(grads)

    def kern(*refs):
        ins = refs[:n_arr]
        owns = refs[n_arr:2 * n_arr]
        gots = refs[2 * n_arr:3 * n_arr]
        send_sems, recv_sems, local_sems = refs[3 * n_arr:]
        x, y, c, _ = _position()
        sibling = (x, y, 1 - c)
        local, remote = [], []
        for a in range(n_arr):
            rows = grads[a].shape[1] // 2
            piece = rows // COPY_PIECES
            for j in range(N_CHIPS):
                for r in range(COPY_PIECES):
                    k = (a * N_CHIPS + j) * COPY_PIECES + r
                    dst_rows = pl.ds(r * piece, piece)
                    local.append(pltpu.make_async_copy(
                        ins[a].at[j, pl.ds(c * rows + r * piece, piece), :],
                        owns[a].at[j, dst_rows, :], local_sems.at[k]))
                    remote.append(pltpu.make_async_remote_copy(
                        src_ref=ins[a].at[j, pl.ds((1 - c) * rows + r * piece, piece), :],
                        dst_ref=gots[a].at[j, dst_rows, :], send_sem=send_sems.at[k], recv_sem=recv_sems.at[k],
                        device_id=sibling, device_id_type=MESH))
        for cp in local + remote:
            cp.start()
        for cp in remote:
            cp.wait()
        for cp in local:
            cp.wait()

    hbm = pl.BlockSpec(memory_space=pltpu.VMEM)
    half = [jax.ShapeDtypeStruct((N_CHIPS, g.shape[1] // 2, g.shape[2]), F32) for g in grads]
    outs = pl.pallas_call(
        kern, name="swap_halves",
        in_specs=[hbm] * n_arr,
        out_specs=[hbm] * (2 * n_arr),
        out_shape=half + half,
        scratch_shapes=[pltpu.SemaphoreType.DMA((n_arr * N_CHIPS * COPY_PIECES,)),
                        pltpu.SemaphoreType.DMA((n_arr * N_CHIPS * COPY_PIECES,)),
                        pltpu.SemaphoreType.DMA((n_arr * N_CHIPS * COPY_PIECES,))],
        compiler_params=_cparams(),
    )(*grads)
    return outs[:n_arr], outs[n_arr:]


def _scatter_to_owners(parts):
    n_arr = len(parts)

    def kern(*refs):
        ins = refs[:n_arr]
        outs = refs[n_arr:2 * n_arr]
        send_sems, recv_sems, local_sems = refs[2 * n_arr:]
        x, y, c, chips = _position()
        me = _chip_index(x, y)
        local = [pltpu.make_async_copy(ins[a].at[me], outs[a].at[me], local_sems.at[a]) for a in range(n_arr)]
        for cp in local:
            cp.start()
        sends = []
        for a in range(n_arr):
            for j, chip in enumerate(chips):
                sends.append(pltpu.make_async_remote_copy(
                    src_ref=ins[a].at[_chip_index(*chip)], dst_ref=outs[a].at[me],
                    send_sem=send_sems.at[a * 3 + j], recv_sem=recv_sems.at[a * 3 + j],
                    device_id=(*chip, c), device_id_type=MESH))
        for cp in sends:
            cp.start()
        for a in range(n_arr):
            for j, chip in enumerate(chips):
                slot = outs[a].at[_chip_index(*chip)]
                pltpu.make_async_remote_copy(
                    src_ref=slot, dst_ref=slot, send_sem=send_sems.at[a * 3 + j],
                    recv_sem=recv_sems.at[a * 3 + j], device_id=(*chip, c), device_id_type=MESH).wait_recv()
        for cp in sends:
            cp.wait_send()
        for cp in local:
            cp.wait()

    hbm = pl.BlockSpec(memory_space=pltpu.VMEM)
    return pl.pallas_call(
        kern, name="scatter_to_owners",
        in_specs=[hbm] * n_arr,
        out_specs=[hbm] * n_arr,
        out_shape=[jax.ShapeDtypeStruct(p.shape, p.dtype) for p in parts],
        scratch_shapes=[pltpu.SemaphoreType.DMA((3 * n_arr,)),
                        pltpu.SemaphoreType.DMA((3 * n_arr,)),
                        pltpu.SemaphoreType.DMA((n_arr,))],
        compiler_params=_cparams(),
    )(*parts)


def _join_halves(halves):
    n_arr = len(halves)

    def kern(*refs):
        ins = refs[:n_arr]
        outs = refs[n_arr:2 * n_arr]
        send_sems, recv_sems, local_sems = refs[2 * n_arr:]
        x, y, c, _ = _position()
        sibling = (x, y, 1 - c)
        local, remote = [], []
        for a in range(n_arr):
            rows = halves[a].shape[0]
            piece = rows // COPY_PIECES
            for r in range(COPY_PIECES):
                k = a * COPY_PIECES + r
                src = ins[a].at[pl.ds(r * piece, piece), :]
                dst = outs[a].at[pl.ds(c * rows + r * piece, piece), :]
                local.append(pltpu.make_async_copy(src, dst, local_sems.at[k]))
                remote.append(pltpu.make_async_remote_copy(
                    src_ref=src, dst_ref=dst, send_sem=send_sems.at[k], recv_sem=recv_sems.at[k],
                    device_id=sibling, device_id_type=MESH))
        for cp in local + remote:
            cp.start()
        for a in range(n_arr):
            rows = halves[a].shape[0]
            piece = rows // COPY_PIECES
            for r in range(COPY_PIECES):
                k = a * COPY_PIECES + r
                theirs = outs[a].at[pl.ds((1 - c) * rows + r * piece, piece), :]
                pltpu.make_async_remote_copy(
                    src_ref=theirs, dst_ref=theirs, send_sem=send_sems.at[k], recv_sem=recv_sems.at[k],
                    device_id=sibling, device_id_type=MESH).wait_recv()
        for cp in remote:
            cp.wait_send()
        for cp in local:
            cp.wait()

    hbm = pl.BlockSpec(memory_space=pltpu.VMEM)
    return pl.pallas_call(
        kern, name="join_halves",
        in_specs=[hbm] * n_arr,
        out_specs=[hbm] * n_arr,
        out_shape=[jax.ShapeDtypeStruct((2 * h.shape[0], h.shape[1]), F32) for h in halves],
        scratch_shapes=[pltpu.SemaphoreType.DMA((n_arr * COPY_PIECES,)),
                        pltpu.SemaphoreType.DMA((n_arr * COPY_PIECES,)),
                        pltpu.SemaphoreType.DMA((n_arr * COPY_PIECES,))],
        compiler_params=_cparams(),
    )(*halves)


def _add2(a, b, *, name):
    n, r, c = a.shape
    tr = min(256, r)

    def kern(a_ref, b_ref, o_ref):
        o_ref[...] = (a_ref[...] + b_ref[...]).astype(BF16)

    blk = pl.BlockSpec((1, tr, c), lambda j, i: (j, i, 0))
    return pl.pallas_call(
        kern, name=name,
        grid=(n, r // tr),
        in_specs=[blk, blk],
        out_specs=blk,
        out_shape=jax.ShapeDtypeStruct(a.shape, BF16),
        compiler_params=_cparams(("parallel", "parallel")),
    )(a, b)


def _sum4(a, *, name):
    n, r, c = a.shape
    tr = min(256, r)

    def kern(a_ref, o_ref):
        f = lambda j: a_ref[j].astype(F32)
        o_ref[...] = ((f(0) + f(1)) + f(2)) + f(3)

    return pl.pallas_call(
        kern, name=name,
        grid=(r // tr,),
        in_specs=[pl.BlockSpec((n, tr, c), lambda i: (0, i, 0))],
        out_specs=pl.BlockSpec((tr, c), lambda i: (i, 0)),
        out_shape=jax.ShapeDtypeStruct((r, c), F32),
        compiler_params=_cparams(("parallel",)),
    )(a)


def _small_allreduce_adamw(g, w, m, v):
    def kern(g_ref, w_ref, m_ref, v_ref, gs_ref, d_ref, mo_ref, vo_ref, buf_ref, send_sems, recv_sems):
        x, y, c, _ = _position()
        me = 4 * x + 2 * y + c
        buf_ref[me] = g_ref[...]
        peers = [(x, y, 1 - c)] + [(px, py, pc) for px, py in _position()[3] for pc in (c, 1 - c)]
        sends = []
        for k, peer in enumerate(peers):
            sends.append(pltpu.make_async_remote_copy(
                src_ref=g_ref, dst_ref=buf_ref.at[me], send_sem=send_sems.at[k], recv_sem=recv_sems.at[k],
                device_id=peer, device_id_type=MESH))
        for cp in sends:
            cp.start()
        for k, (px, py, pc) in enumerate(peers):
            slot = buf_ref.at[4 * px + 2 * py + pc]
            pltpu.make_async_remote_copy(
                src_ref=slot, dst_ref=slot, send_sem=send_sems.at[k], recv_sem=recv_sems.at[k],
                device_id=(px, py, pc), device_id_type=MESH).wait_recv()
        for cp in sends:
            cp.wait_send()
        tot = buf_ref[0]
        for d in range(1, N_DEV):
            tot = tot + buf_ref[d]
        gs_ref[...] = tot
        delta, mn, vn = _adamw_math(w_ref[...], tot, m_ref[...], v_ref[...])
        d_ref[...] = delta
        mo_ref[...] = mn
        vo_ref[...] = vn

    vm = pl.BlockSpec(memory_space=pltpu.VMEM)
    sds = jax.ShapeDtypeStruct((SMALL_ROWS, LANES), F32)
    return pl.pallas_call(
        kern, name="small_allreduce_adamw",
        in_specs=[vm] * 4,
        out_specs=[vm] * 4,
        out_shape=[sds] * 4,
        scratch_shapes=[pltpu.VMEM((N_DEV, SMALL_ROWS, LANES), F32),
                        pltpu.SemaphoreType.DMA((N_DEV - 1,)),
                        pltpu.SemaphoreType.DMA((N_DEV - 1,))],
    )(g, w, m, v)


def _to_padded_cols(w):
    pad = jnp.zeros((w.shape[0], N_C - FOX_HEADS), w.dtype)
    return jnp.concatenate([w[:, 0:1536], w[:, 2056:2824], w[:, 1536:1544], pad,
                            w[:, 1544:2056], w[:, 2824:3336]], axis=1)


def _from_padded_cols(g):
    return jnp.concatenate([g[:, 0:1536], g[:, OFF_C:OFF_C + FOX_HEADS], g[:, OFF_B:OFF_B + FOX_W],
                            g[:, 1536:N_A], g[:, OFF_B + FOX_W:N_PAD]], axis=1)


def _pack_small(b_f, rel_bias, sink, ln_g, ln_b):
    row = lambda v: jnp.pad(v.reshape(1, -1), ((0, 0), (0, LANES - v.size)))
    return jnp.concatenate([ln_g.reshape(8, LANES), ln_b.reshape(8, LANES), rel_bias.reshape(2, LANES),
                            row(b_f), row(sink), jnp.zeros((4, LANES), F32)], axis=0)


def _unpack_small(p):
    ln_g = p[0:8].reshape(1, D_MODEL)
    ln_b = p[8:16].reshape(1, D_MODEL)
    rel_bias = p[16:18].reshape(NUM_BUCKETS, SWA_HEADS)
    b_f = p[18:19, :FOX_HEADS]
    sink = p[19:20, :SWA_HEADS]
    return b_f, rel_bias, sink, ln_g, ln_b


def _fox_rows(a):
    return a[:, :FOX_HEADS].T.reshape(FOX_HEADS, 1, a.shape[0])


def kernel(x, w_in, b_f, rel_bias, sink, w_o, ln_g, ln_b, loss_target, m_w_in, m_b_f, m_rel_bias, m_sink, m_w_o, m_ln_g, m_ln_b, v_w_in, v_b_f, v_rel_bias, v_sink, v_w_o, v_ln_g, v_ln_b):
    x2 = x[0]
    tgt = loss_target[0]
    s = x2.shape[0]
    w_in2, w_o2 = w_in[0], w_o[0]

    shard_cols = D_IN // N_CHIPS
    col_pad = ((0, 0), (0, SHARD_PAD - shard_cols))
    w_in_all, w_o_all = _gather_weights(jnp.pad(w_in2.astype(BF16), col_pad), w_o2.astype(BF16))
    w_full = jnp.concatenate([w_in_all[j, :, :shard_cols] for j in range(N_CHIPS)], axis=1)
    w_pad = _to_padded_cols(w_full)
    w_o_full = w_o_all.reshape(D_MODEL, D_MODEL)

    qkv, ffp, z, xt, vt = _project(x2, w_pad)
    bfp = jnp.pad(b_f, ((0, 0), (0, LANES - FOX_HEADS)))
    cum = _cum_fwd(ffp, bfp)
    cum_t3 = _fox_rows(cum)
    o_fox, lse_t3 = _fox_fwd(qkv, vt, cum_t3, cum)
    bucket = jnp.asarray(_bucket_table())
    bias = _swa_bias(rel_bias, bucket)
    o_swa, lse_swa = _swa_fwd(qkv, bias, sink)

    loss8, dh, dy, mixed_t, do_bf, dz, delta, gg8, gb8 = _post(
        x2, tgt, o_fox, o_swa, z, w_o_full, ln_g, ln_b)
    grad_w_o_full = _matmul_acc(mixed_t, dy, tm=1024, tn=512, tk=1024, name="grad_w_o")

    delta_t3 = _fox_rows(delta)
    dqt_fox, dk_fox, dv_fox, dcum_k, dcum_q = _fox_bwd(qkv, do_bf, cum_t3, cum, lse_t3, delta_t3)
    dcum_q = jnp.pad(dcum_q.reshape(FOX_HEADS, s).T, ((0, 0), (0, LANES - FOX_HEADS)))
    dff, gbf8 = _cum_bwd(dcum_k, dcum_q, ffp, bfp)
    dq_swa, dk_swa, dv_swa, grb, gsk8 = _swa_bwd(qkv, do_bf, delta, lse_swa, bias, sink, bucket)

    dq_fox = dqt_fox.T.astype(BF16)
    d_misc = jnp.concatenate([dk_swa, dv_swa, dff], axis=1)
    pieces = [dq_fox, dk_fox, dv_fox, dq_swa, d_misc, dz]
    grad_x = _grad_x_matmul(pieces, w_pad, dh, tm=512, tn=512, name="grad_x")
    blocks = [(p, 0) for p in pieces[:-1]] + [(dz, 0), (dz, 1)]
    grad_w_pad = _grad_w_matmul(xt, blocks, tk=1024, name="grad_w_in")
    grad_w_in_full = _from_padded_cols(grad_w_pad)

    g_in4 = jnp.stack([jnp.pad(grad_w_in_full[:, j * shard_cols:(j + 1) * shard_cols], col_pad)
                       for j in range(N_CHIPS)])
    g_o4 = grad_w_o_full.reshape(N_CHIPS, D_MODEL // N_CHIPS, D_MODEL)
    owns, gots = _swap_halves([g_in4, g_o4])
    parts = [_add2(owns[0], gots[0], name="pair_sum_w_in"), _add2(owns[1], gots[1], name="pair_sum_w_o")]
    slabs = _scatter_to_owners(parts)
    halves = [_sum4(slabs[0], name="chip_sum_w_in"), _sum4(slabs[1], name="chip_sum_w_o")]
    g_w_in, g_w_o = _join_halves(halves)
    g_w_in = g_w_in[:, :shard_cols]

    cols_first = lambda a: jnp.transpose(a, (2, 0, 1))
    rows_first = lambda a: jnp.transpose(a, (1, 2, 0))
    d_w_in, nm_w_in, nv_w_in = [rows_first(a) for a in _adamw_cols(
        cols_first(w_in), cols_first(g_w_in[None]), cols_first(m_w_in), cols_first(v_w_in), name="adamw_w_in")]
    d_w_o, nm_w_o, nv_w_o = _adamw(w_o2, g_w_o, m_w_o[0], v_w_o[0], name="adamw_w_o")

    g_small = _pack_small(gbf8[0:1, :FOX_HEADS], grb[:, :SWA_HEADS], gsk8[0:1, :SWA_HEADS], gg8[0:1], gb8[0:1])
    g_small = g_small.at[LOSS_ROW, 0].set(loss8[0, 0])
    w_small = _pack_small(b_f, rel_bias, sink, ln_g, ln_b)
    m_small = _pack_small(m_b_f, m_rel_bias, m_sink, m_ln_g, m_ln_b)
    v_small = _pack_small(v_b_f, v_rel_bias, v_sink, v_ln_g, v_ln_b)
    gs, ds, ms, vs = _small_allreduce_adamw(g_small, w_small, m_small, v_small)
    loss = gs[LOSS_ROW, 0]
    g_bf, g_rb, g_sk, g_lg, g_lb = _unpack_small(gs)
    d_bf, d_rb, d_sk, d_lg, d_lb = _unpack_small(ds)
    m_bf, m_rb, m_sk, m_lg, m_lb = _unpack_small(ms)
    v_bf, v_rb, v_sk, v_lg, v_lb = _unpack_small(vs)

    e = lambda a: a[None]
    return (loss, e(grad_x),
            e(g_w_in), g_bf, g_rb, g_sk, e(g_w_o), g_lg, g_lb,
            d_w_in, d_bf, d_rb, d_sk, e(d_w_o), d_lg, d_lb,
            nm_w_in, m_bf, m_rb, m_sk, e(nm_w_o), m_lg, m_lb,
            nv_w_in, v_bf, v_rb, v_sk, e(nv_w_o), v_lg, v_lb)
```

```python
import functools
import math

import numpy as np
import jax
import jax.numpy as jnp
from jax import lax
from jax.experimental import pallas as pl
from jax.experimental.pallas import tpu as pltpu

F32 = jnp.float32
BF16 = jnp.bfloat16

D_MODEL = 1024
HEAD_DIM = 64
FOX_HEADS = 8
SWA_HEADS = 8
SWA_KV_HEADS = 2
SWA_GROUP = 4
FOX_W = 512
SWA_W = 512
SWA_KV_W = 128
BLOCK = 128
NUM_BUCKETS = 32
MAX_DISTANCE = 128
LN_EPS = 1e-5
NEG = -1e30
ALPHA = 2.0 ** 0.25
QK_SCALE = 0.125

ADAM_LR = 0.001
ADAM_B1 = 0.9
ADAM_B2 = 0.999
ADAM_EPS = 1e-08
ADAM_WD = 0.01
ADAM_STEP = 10

D_IN = 3336
SHARD_PAD = 896
N_A = 2304
N_C = 256
N_B = 1024
OFF_C = N_A
OFF_B = N_A + N_C
N_PAD = N_A + N_C + N_B
COL_FK, COL_FV, COL_SQ, COL_SK, COL_SV = 512, 1024, 1536, 2048, 2176

LANES = 128
FOX_T = 256
FOX_REF = 512
SUM_ROWS = 16
VMEM_LIMIT = 56 * 1024 * 1024

MESH = pl.DeviceIdType.MESH
N_CHIPS = 4
N_DEV = 8
SMALL_ROWS = 24
LOSS_ROW = 20
COPY_PIECES = 4


def _cparams(sem=None):
    return pltpu.CompilerParams(dimension_semantics=sem, vmem_limit_bytes=VMEM_LIMIT)


def _split3(x):
    hi = x.astype(BF16)
    r = x - hi.astype(F32)
    mid = r.astype(BF16)
    lo = (r - mid.astype(F32)).astype(BF16)
    return hi, mid, lo


def _dot(a, b):
    return jnp.dot(a, b, preferred_element_type=F32)


def _dot_nt(a, b):
    return lax.dot_general(a, b, (((1,), (1,)), ((), ())), preferred_element_type=F32)


def _dot_tn(a, b):
    return lax.dot_general(a, b, (((0,), (0,)), ((), ())), preferred_element_type=F32)


def _project(x, w_pad):
    s, k = x.shape
    tm = 512
    chunk = 512

    def kern(x_ref, w_ref, qkv_ref, ff_ref, z_ref, xt_ref, vt_ref):
        xf = x_ref[...]
        xb = xf.astype(BF16)
        xt_ref[...] = xf.T.astype(BF16)
        for c0 in range(0, N_A, chunk):
            width = min(chunk, N_A - c0)
            res = _dot(xb, w_ref[:, c0:c0 + width])
            qkv_ref[:, c0:c0 + width] = res.astype(BF16)
            if c0 == COL_FV:
                vt_ref[...] = res.T.astype(BF16)
        ff_ref[...] = _dot(xb, w_ref[:, OFF_C:OFF_C + N_C])
        for c0 in range(0, N_B, 512):
            z_ref[:, c0:c0 + 512] = _dot(xb, w_ref[:, OFF_B + c0:OFF_B + c0 + 512])

    row = lambda i: (i, 0)
    return pl.pallas_call(
        kern, name="project",
        grid=(s // tm,),
        in_specs=[pl.BlockSpec((tm, k), row),
                  _resident((k, N_PAD), lambda i: (0, 0))],
        out_specs=[pl.BlockSpec((tm, N_A), row),
                   pl.BlockSpec((tm, N_C), row),
                   pl.BlockSpec((tm, N_B), row),
                   pl.BlockSpec((k, tm), lambda i: (0, i)),
                   pl.BlockSpec((FOX_W, tm), lambda i: (0, i))],
        out_shape=[jax.ShapeDtypeStruct((s, N_A), BF16),
                   jax.ShapeDtypeStruct((s, N_C), F32),
                   jax.ShapeDtypeStruct((s, N_B), F32),
                   jax.ShapeDtypeStruct((k, s), BF16),
                   jax.ShapeDtypeStruct((FOX_W, s), BF16)],
        compiler_params=_cparams(("parallel",)),
    )(x, w_pad)


def _grad_x_matmul(pieces, w_pad, dh, *, tm, tn, name):
    m = dh.shape[0]
    n, k = w_pad.shape
    widths = [p.shape[1] for p in pieces]
    offs = [sum(widths[:i]) for i in range(len(pieces))]
    assert sum(widths) == k

    def kern(*refs):
        p_refs, (b_ref, dh_ref, o_ref) = refs[:len(pieces)], refs[len(pieces):]
        acc = ALPHA * dh_ref[...]
        for p_ref, off, width in zip(p_refs, offs, widths):
            acc = acc + _dot_nt(p_ref[...], b_ref[:, off:off + width])
        o_ref[...] = acc

    return pl.pallas_call(
        kern, name=name,
        grid=(n // tn, m // tm),
        in_specs=[pl.BlockSpec((tm, w), lambda j, i: (i, 0)) for w in widths]
        + [pl.BlockSpec((tn, k), lambda j, i: (j, 0)),
           pl.BlockSpec((tm, tn), lambda j, i: (i, j))],
        out_specs=pl.BlockSpec((tm, tn), lambda j, i: (i, j)),
        out_shape=jax.ShapeDtypeStruct((m, n), F32),
        compiler_params=_cparams(("parallel", "parallel")),
    )(*pieces, w_pad, dh)


def _grad_w_matmul(xt, blocks, *, tk, name):
    m, s = xt.shape
    tn = 512
    nb = len(blocks)

    def kern(a_ref, *refs):
        b_refs, o_ref = refs[:nb], refs[nb]

        @pl.when(pl.program_id(0) == 0)
        def _():
            o_ref[...] = jnp.zeros_like(o_ref)
        a = a_ref[...]
        for blk in range(nb):
            o_ref[:, blk * tn:(blk + 1) * tn] += _dot(a, b_refs[blk][...])

    return pl.pallas_call(
        kern, name=name,
        grid=(s // tk,),
        in_specs=[pl.BlockSpec((m, tk), lambda k: (0, k))]
        + [pl.BlockSpec((tk, tn), functools.partial(lambda k, col: (k, col), col=col)) for _, col in blocks],
        out_specs=_resident((m, nb * tn), lambda k: (0, 0)),
        out_shape=jax.ShapeDtypeStruct((m, nb * tn), F32),
        compiler_params=_cparams(("arbitrary",)),
    )(xt, *[arr for arr, _ in blocks])


def _matmul_acc(at, b, *, tm, tn, tk, name):
    m, s = at.shape
    n = b.shape[1]

    def kern(a_ref, b_ref, o_ref):
        @pl.when(pl.program_id(2) == 0)
        def _():
            o_ref[...] = jnp.zeros_like(o_ref)
        o_ref[...] += _dot(a_ref[...], b_ref[...])

    return pl.pallas_call(
        kern, name=name,
        grid=(m // tm, n // tn, s // tk),
        in_specs=[pl.BlockSpec((tm, tk), lambda i, j, k: (i, k)),
                  pl.BlockSpec((tk, tn), lambda i, j, k: (k, j))],
        out_specs=pl.BlockSpec((tm, tn), lambda i, j, k: (i, j)),
        out_shape=jax.ShapeDtypeStruct((m, n), F32),
        compiler_params=_cparams(("parallel", "parallel", "arbitrary")),
    )(at, b)


def _tri(n, lower):
    r = lax.broadcasted_iota(jnp.int32, (n, n), 0)
    c = lax.broadcasted_iota(jnp.int32, (n, n), 1)
    keep = (c <= r) if lower else (c >= r)
    return jnp.where(keep, 1.0, 0.0).astype(BF16)


def _exact_dot(mat_bf16, x_f32, left):
    out = None
    for piece in _split3(x_f32):
        t = _dot(mat_bf16, piece) if left else _dot(piece, mat_bf16)
        out = t if out is None else out + t
    return out


def _log_sigmoid(z):
    return jnp.minimum(z, 0.0) - jnp.log(1.0 + jnp.exp(-jnp.abs(z)))


def _cum_fwd(ffp, bfp):
    s = ffp.shape[0]
    t = min(512, s)

    def kern(ff_ref, b_ref, cum_ref, carry_ref):
        @pl.when(pl.program_id(0) == 0)
        def _():
            carry_ref[...] = jnp.zeros_like(carry_ref)
        lane = lax.broadcasted_iota(jnp.int32, (1, LANES), 1)
        lf = _log_sigmoid(ff_ref[...] + b_ref[...])
        lf = jnp.where(lane < FOX_HEADS, lf, 0.0)
        cum = _exact_dot(_tri(t, True), lf, True) + carry_ref[0:1, :]
        cum_ref[...] = cum
        carry_ref[...] = jnp.broadcast_to(cum[t - 1:t, :], carry_ref.shape)

    return pl.pallas_call(
        kern, name="cum_fwd",
        grid=(s // t,),
        in_specs=[pl.BlockSpec((t, LANES), lambda i: (i, 0)),
                  pl.BlockSpec((1, LANES), lambda i: (0, 0))],
        out_specs=pl.BlockSpec((t, LANES), lambda i: (i, 0)),
        out_shape=jax.ShapeDtypeStruct((s, LANES), F32),
        scratch_shapes=[pltpu.VMEM((8, LANES), F32)],
        compiler_params=_cparams(("arbitrary",)),
    )(ffp, bfp)


def _cum_bwd(dcum_k, dcum_q, ffp, bfp):
    s = dcum_k.shape[0]
    t = min(512, s)
    nb = s // t

    def kern(dck_ref, dcq_ref, ff_ref, b_ref, dff_ref, gb_ref, carry_ref):
        @pl.when(pl.program_id(0) == 0)
        def _():
            carry_ref[...] = jnp.zeros_like(carry_ref)
            gb_ref[...] = jnp.zeros_like(gb_ref)
        lane = lax.broadcasted_iota(jnp.int32, (1, LANES), 1)
        dlf = _exact_dot(_tri(t, False), dck_ref[...] + dcq_ref[...], True) + carry_ref[0:1, :]
        carry_ref[...] = jnp.broadcast_to(dlf[0:1, :], carry_ref.shape)
        z = ff_ref[...] + b_ref[...]
        dff = jnp.where(lane < FOX_HEADS, dlf / (1.0 + jnp.exp(z)), 0.0)
        gb_ref[...] += jnp.broadcast_to(jnp.sum(dff, axis=0, keepdims=True), gb_ref.shape)
        dff_ref[...] = jnp.concatenate([dff, jnp.zeros_like(dff)], axis=1).astype(BF16)

    return pl.pallas_call(
        kern, name="cum_bwd",
        grid=(nb,),
        in_specs=[pl.BlockSpec((t, LANES), lambda i: (nb - 1 - i, 0)),
                  pl.BlockSpec((t, LANES), lambda i: (nb - 1 - i, 0)),
                  pl.BlockSpec((t, LANES), lambda i: (nb - 1 - i, 0)),
                  pl.BlockSpec((1, LANES), lambda i: (0, 0))],
        out_specs=[pl.BlockSpec((t, N_C), lambda i: (nb - 1 - i, 0)),
                   pl.BlockSpec((8, LANES), lambda i: (0, 0))],
        out_shape=[jax.ShapeDtypeStruct((s, N_C), BF16),
                   jax.ShapeDtypeStruct((8, LANES), F32)],
        scratch_shapes=[pltpu.VMEM((8, LANES), F32)],
        compiler_params=_cparams(("arbitrary",)),
    )(dcum_k, dcum_q, ffp, bfp)


def _resident(shape, index_map):
    return pl.BlockSpec(shape, index_map, pipeline_mode=pl.Buffered(1))


def _fox_fwd(qkv, vt, cum_t3, cum):
    s = qkv.shape[0]
    tk = min(FOX_T, s)
    tq = FOX_REF
    nq = s // tq
    nh = FOX_HEADS
    diag_tiles = tq // tk

    def kern(q_ref, k_ref, vt_ref, ct_ref, c_ref, o_ref, lse_ref, m_ref, acc_ref, u_ref):
        i = pl.program_id(0)
        lane = lax.broadcasted_iota(jnp.int32, (1, LANES), 1)
        krow = lax.broadcasted_iota(jnp.int32, (tk, tq), 0)
        qcol = lax.broadcasted_iota(jnp.int32, (tk, tq), 1)
        q0 = pl.multiple_of(i * tq, tq)
        qts, crefs = [], []
        for h in range(nh):
            p, a = divmod(h, 2)
            q2 = q_ref[:, p * LANES:(p + 1) * LANES] * jnp.asarray(QK_SCALE, BF16)
            sel = (lane < HEAD_DIM) if a == 0 else (lane >= HEAD_DIM)
            qts.append(jnp.where(sel, q2, jnp.zeros_like(q2)).astype(F32).T.astype(BF16))
            crefs.append(ct_ref[h, :, pl.ds(q0, LANES)][:, 0:1])
        m_ref[...] = jnp.full(m_ref.shape, NEG, F32)
        acc_ref[...] = jnp.zeros_like(acc_ref)
        ones = jnp.ones((SUM_ROWS, tk), BF16)

        def tile(j, diag):
            k0 = pl.multiple_of(j * tk, tk)
            cb = c_ref[pl.ds(k0, tk), :]
            sts = [_dot(k_ref[pl.ds(k0, tk), (h // 2) * LANES:(h // 2 + 1) * LANES], qts[h]) for h in range(nh)]
            tile_max = []
            for h in range(nh):
                u = sts[h] - (cb[:, h:h + 1] - crefs[h])
                if diag is not None:
                    u = jnp.where(krow + diag * tk <= qcol, u, NEG)
                u_ref[h] = u
                tile_max.append(jnp.max(u, axis=0, keepdims=True))
            pts, scales = [], []
            for h in range(nh):
                m_old = m_ref[h]
                m_new = jnp.maximum(m_old, tile_max[h])
                scales.append(jnp.exp(m_old - m_new))
                pts.append(jnp.exp(u_ref[h] - m_new).astype(BF16))
                m_ref[h] = m_new
            for h in range(nh):
                vth = jnp.concatenate([vt_ref[h * HEAD_DIM:(h + 1) * HEAD_DIM, pl.ds(k0, tk)], ones], axis=0)
                acc_ref[h] = scales[h] * acc_ref[h] + _dot(vth, pts[h])

        def body(j, c):
            tile(j, None)
            return c
        lax.fori_loop(0, i * diag_tiles, body, 0)
        for d in range(diag_tiles):
            tile(i * diag_tiles + d, d)

        ls = [acc_ref[h][HEAD_DIM:HEAD_DIM + 1] for h in range(nh)]
        for p in range(nh // 2):
            ot = jnp.concatenate([acc_ref[2 * p + a][:HEAD_DIM] * (1.0 / ls[2 * p + a]) for a in range(2)], axis=0)
            o_ref[:, p * LANES:(p + 1) * LANES] = ot.T
        for h in range(nh):
            lse_ref[h, :, pl.ds(q0, tq)] = m_ref[h] + jnp.log(ls[h])

    return pl.pallas_call(
        kern, name="fox_fwd",
        grid=(nq,),
        in_specs=[pl.BlockSpec((tq, FOX_W), lambda i: (i, 0)),
                  _resident((s, FOX_W), lambda i: (0, COL_FK // FOX_W)),
                  _resident((FOX_W, s), lambda i: (0, 0)),
                  _resident((nh, 1, s), lambda i: (0, 0, 0)),
                  _resident((s, LANES), lambda i: (0, 0))],
        out_specs=[pl.BlockSpec((tq, FOX_W), lambda i: (i, 0)),
                   pl.BlockSpec((nh, 1, s), lambda i: (0, 0, 0))],
        out_shape=[jax.ShapeDtypeStruct((s, FOX_W), F32),
                   jax.ShapeDtypeStruct((nh, 1, s), F32)],
        scratch_shapes=[pltpu.VMEM((nh, 1, tq), F32),
                        pltpu.VMEM((nh, HEAD_DIM + SUM_ROWS, tq), F32),
                        pltpu.VMEM((nh, tk, tq), F32)],
        compiler_params=_cparams(("arbitrary",)),
    )(qkv, qkv, vt, cum_t3, cum)


def _fox_bwd(qkv, do_bf, cum_t3, cum, lse_t3, delta_t3):
    s = qkv.shape[0]
    t = min(FOX_T, s)
    nq = s // t
    nh = FOX_HEADS
    npair = nh // 2

    def kern(q_ref, do_ref, k_ref, v_ref, ct_ref, c_ref, lse_ref, dl_ref,
             dqt_ref, dk_ref, dv_ref, dc_ref, dcq_ref, accv_ref, acck_ref, accd_ref):
        kj = pl.program_id(0)
        lane = lax.broadcasted_iota(jnp.int32, (1, LANES), 1)
        krow = lax.broadcasted_iota(jnp.int32, (t, t), 0)
        qcol = lax.broadcasted_iota(jnp.int32, (t, t), 1)
        causal = krow <= qcol
        sels = [lane < HEAD_DIM, lane >= HEAD_DIM]

        @pl.when(kj == 0)
        def _():
            dqt_ref[...] = jnp.zeros_like(dqt_ref)
            dcq_ref[...] = jnp.zeros_like(dcq_ref)

        accv_ref[...] = jnp.zeros_like(accv_ref)
        acck_ref[...] = jnp.zeros_like(acck_ref)
        accd_ref[...] = jnp.zeros_like(accd_ref)
        cb = c_ref[...]
        k2s, v2s, kts = [], [], []
        for p in range(npair):
            k2 = k_ref[:, p * LANES:(p + 1) * LANES]
            k2s.append(k2)
            v2s.append(v_ref[:, p * LANES:(p + 1) * LANES])
            kt = k2.astype(F32).T * QK_SCALE
            kts.append(kt[:HEAD_DIM].astype(BF16))
            kts.append(kt[HEAD_DIM:].astype(BF16))
        css = [cb[:, h:h + 1] for h in range(nh)]

        def tile(i, masked):
            q0 = pl.multiple_of(i * t, t)
            r0 = pl.multiple_of((i // (FOX_REF // t)) * FOX_REF, FOX_REF)
            sts, dpts, qms, doms = [], [], [], []
            for h in range(nh):
                p, a = divmod(h, 2)
                qi = q_ref[pl.ds(q0, t), p * LANES:(p + 1) * LANES] * jnp.asarray(QK_SCALE, BF16)
                doi = do_ref[pl.ds(q0, t), p * LANES:(p + 1) * LANES]
                qm = jnp.where(sels[a], qi, jnp.zeros_like(qi))
                dom = jnp.where(sels[a], doi, jnp.zeros_like(doi))
                qms.append(qm)
                doms.append(dom)
                sts.append(_dot_nt(k2s[p], qm))
                dpts.append(_dot_nt(v2s[p], dom))
            pts, dsts = [], []
            for h in range(nh):
                cref = ct_ref[h, :, pl.ds(r0, LANES)][:, 0:1]
                pt = jnp.exp(sts[h] - (css[h] - cref) - lse_ref[h, :, pl.ds(q0, t)])
                if masked:
                    pt = jnp.where(causal, pt, 0.0)
                ds32 = pt * (dpts[h] - dl_ref[h, :, pl.ds(q0, t)])
                part = ds32[:, 0:LANES]
                for c in range(1, t // LANES):
                    part = part + ds32[:, c * LANES:(c + 1) * LANES]
                accd_ref[h] += part
                dcq_ref[h, :, pl.ds(q0, t)] += jnp.sum(ds32, axis=0, keepdims=True)
                pts.append(pt.astype(BF16))
                dsts.append(ds32.astype(BF16))
            for p in range(npair):
                ha, hb = 2 * p, 2 * p + 1
                accv_ref[p] += _dot(pts[ha], doms[ha]) + _dot(pts[hb], doms[hb])
                acck_ref[p] += _dot(dsts[ha], qms[ha]) + _dot(dsts[hb], qms[hb])
            for h in range(nh):
                dqt_ref[h * HEAD_DIM:(h + 1) * HEAD_DIM, pl.ds(q0, t)] += _dot(kts[h], dsts[h])

        tile(kj, True)

        def body(i, c):
            tile(i, False)
            return c
        lax.fori_loop(kj + 1, nq, body, 0)

        dc = jnp.zeros((t, LANES), F32)
        for h in range(nh):
            dc = jnp.where(lane == h, -jnp.sum(accd_ref[h], axis=1, keepdims=True), dc)
        dc_ref[...] = dc
        for p in range(npair):
            dv_ref[:, p * LANES:(p + 1) * LANES] = accv_ref[p].astype(BF16)
            dk_ref[:, p * LANES:(p + 1) * LANES] = acck_ref[p].astype(BF16)

    whole = lambda kj: (0, 0, 0)
    return pl.pallas_call(
        kern, name="fox_bwd",
        grid=(nq,),
        in_specs=[_resident((s, FOX_W), lambda kj: (0, 0)),
                  _resident((s, FOX_W), lambda kj: (0, 0)),
                  pl.BlockSpec((t, FOX_W), lambda kj: (kj, COL_FK // FOX_W)),
                  pl.BlockSpec((t, FOX_W), lambda kj: (kj, COL_FV // FOX_W)),
                  _resident((nh, 1, s), whole),
                  pl.BlockSpec((t, LANES), lambda kj: (kj, 0)),
                  _resident((nh, 1, s), whole),
                  _resident((nh, 1, s), whole)],
        out_specs=[_resident((FOX_W, s), lambda kj: (0, 0)),
                   pl.BlockSpec((t, FOX_W), lambda kj: (kj, 0)),
                   pl.BlockSpec((t, FOX_W), lambda kj: (kj, 0)),
                   pl.BlockSpec((t, LANES), lambda kj: (kj, 0)),
                   _resident((nh, 1, s), whole)],
        out_shape=[jax.ShapeDtypeStruct((FOX_W, s), F32),
                   jax.ShapeDtypeStruct((s, FOX_W), BF16),
                   jax.ShapeDtypeStruct((s, FOX_W), BF16),
                   jax.ShapeDtypeStruct((s, LANES), F32),
                   jax.ShapeDtypeStruct((nh, 1, s), F32)],
        scratch_shapes=[pltpu.VMEM((npair, t, LANES), F32),
                        pltpu.VMEM((npair, t, LANES), F32),
                        pltpu.VMEM((nh, t, LANES), F32)],
        compiler_params=_cparams(("arbitrary",)),
    )(qkv, do_bf, qkv, qkv, cum_t3, cum, lse_t3, delta_t3)


def _bucket_table():
    qi = np.arange(BLOCK)[:, None]
    kj = np.arange(2 * BLOCK)[None, :]
    rel = np.maximum(qi + BLOCK - kj, 0).astype(np.int32)
    max_exact = NUM_BUCKETS // 2
    relf = np.maximum(rel, 1).astype(np.float32)
    large = max_exact + (np.log(relf / np.float32(max_exact)) / np.float32(math.log(MAX_DISTANCE / max_exact))
                         * np.float32(NUM_BUCKETS - max_exact)).astype(np.int32)
    large = np.minimum(large, NUM_BUCKETS - 1)
    return np.where(rel < max_exact, rel, large).astype(np.int32)


def _swa_bias(rel_bias, bucket):
    def kern(rb_ref, bk_ref, o_ref):
        bk = bk_ref[...]
        for h in range(SWA_HEADS):
            acc = jnp.zeros((BLOCK, 2 * BLOCK), F32)
            for b in range(NUM_BUCKETS):
                acc = jnp.where(bk == b, rb_ref[b, h], acc)
            o_ref[h] = acc

    return pl.pallas_call(
        kern, name="swa_bias",
        in_specs=[pl.BlockSpec(memory_space=pltpu.SMEM),
                  pl.BlockSpec(memory_space=pltpu.VMEM)],
        out_specs=pl.BlockSpec(memory_space=pltpu.VMEM),
        out_shape=jax.ShapeDtypeStruct((SWA_HEADS, BLOCK, 2 * BLOCK), F32),
        compiler_params=_cparams(),
    )(rel_bias, bucket)


def _swa_mask(n):
    qi = lax.broadcasted_iota(jnp.int32, (BLOCK, 2 * BLOCK), 0)
    kj = lax.broadcasted_iota(jnp.int32, (BLOCK, 2 * BLOCK), 1)
    rel = qi + BLOCK - kj
    band = (rel >= 0) & (rel < BLOCK)
    return band & ((kj >= BLOCK) | (n > 0))


def _swa_fwd(qkv, bias, sink):
    s = qkv.shape[0]
    nb = s // BLOCK

    def kern(q_ref, kp_ref, kc_ref, vp_ref, vc_ref, bias_ref, sink_ref, o_ref, lse_ref):
        n = pl.program_id(0)
        mask = _swa_mask(n)
        lane = lax.broadcasted_iota(jnp.int32, (1, LANES), 1)
        q = q_ref[...] * jnp.asarray(QK_SCALE, BF16)
        k = jnp.concatenate([kp_ref[...], kc_ref[...]], axis=0)
        v = jnp.concatenate([vp_ref[...], vc_ref[...]], axis=0)
        kgs = [k[:, g * HEAD_DIM:(g + 1) * HEAD_DIM] for g in range(SWA_KV_HEADS)]
        vgs = [v[:, g * HEAD_DIM:(g + 1) * HEAD_DIM] for g in range(SWA_KV_HEADS)]
        raw = [_dot_nt(q[:, h * HEAD_DIM:(h + 1) * HEAD_DIM], kgs[h // SWA_GROUP]) for h in range(SWA_HEADS)]
        probs = []
        lse_all = jnp.zeros((BLOCK, LANES), F32)
        for h in range(SWA_HEADS):
            sc = jnp.where(mask, raw[h] + bias_ref[h], NEG)
            sk = sink_ref[0, h]
            m = jnp.maximum(jnp.max(sc, axis=1, keepdims=True), sk)
            p = jnp.exp(sc - m)
            l = jnp.sum(p, axis=1, keepdims=True) + jnp.exp(sk - m)
            probs.append((p * (1.0 / l)).astype(BF16))
            lse_all = jnp.where(lane == h, m + jnp.log(l), lse_all)
        outs = [_dot(probs[h], vgs[h // SWA_GROUP]) for h in range(SWA_HEADS)]
        o_ref[...] = jnp.concatenate(outs, axis=1)
        lse_ref[...] = lse_all

    cq, ck, cv = COL_SQ // SWA_W, COL_SK // LANES, COL_SV // LANES
    prev = lambda n: jnp.maximum(n - 1, 0)
    return pl.pallas_call(
        kern, name="swa_fwd",
        grid=(nb,),
        in_specs=[pl.BlockSpec((BLOCK, SWA_W), lambda n: (n, cq)),
                  pl.BlockSpec((BLOCK, LANES), lambda n: (prev(n), ck)),
                  pl.BlockSpec((BLOCK, LANES), lambda n: (n, ck)),
                  pl.BlockSpec((BLOCK, LANES), lambda n: (prev(n), cv)),
                  pl.BlockSpec((BLOCK, LANES), lambda n: (n, cv)),
                  pl.BlockSpec((SWA_HEADS, BLOCK, 2 * BLOCK), lambda n: (0, 0, 0)),
                  pl.BlockSpec(memory_space=pltpu.SMEM)],
        out_specs=[pl.BlockSpec((BLOCK, SWA_W), lambda n: (n, 0)),
                   pl.BlockSpec((BLOCK, LANES), lambda n: (n, 0))],
        out_shape=[jax.ShapeDtypeStruct((s, SWA_W), F32),
                   jax.ShapeDtypeStruct((s, LANES), F32)],
        compiler_params=_cparams(("parallel",)),
    )(qkv, qkv, qkv, qkv, qkv, bias, sink)


def _swa_bwd(qkv, do_bf, delta, lse, bias, sink, bucket):
    s = qkv.shape[0]
    nb = s // BLOCK

    def kern(q_ref, kp_ref, kc_ref, vp_ref, vc_ref, do_ref, dl_ref, lse_ref, bias_ref, sink_ref, bk_ref,
             dq_ref, dk_ref, dv_ref, grb_ref, gsk_ref, dbias_ref, ck_ref, cv_ref, sk_ref):
        n = pl.program_id(0)
        lane = lax.broadcasted_iota(jnp.int32, (1, LANES), 1)

        @pl.when(n == 0)
        def _():
            dbias_ref[...] = jnp.zeros_like(dbias_ref)
            ck_ref[...] = jnp.zeros_like(ck_ref)
            cv_ref[...] = jnp.zeros_like(cv_ref)
            sk_ref[...] = jnp.zeros_like(sk_ref)

        @pl.when(n < nb)
        def _():
            mask = _swa_mask(n)
            q = q_ref[...] * jnp.asarray(QK_SCALE, BF16)
            k = jnp.concatenate([kp_ref[...], kc_ref[...]], axis=0)
            v = jnp.concatenate([vp_ref[...], vc_ref[...]], axis=0)
            do = do_ref[...]
            dl = dl_ref[...]
            lse_all = lse_ref[...]
            dks = [None] * SWA_KV_HEADS
            dvs = [None] * SWA_KV_HEADS
            gsk = jnp.zeros((1, LANES), F32)
            kgs = [k[:, g * HEAD_DIM:(g + 1) * HEAD_DIM] for g in range(SWA_KV_HEADS)]
            vgs = [v[:, g * HEAD_DIM:(g + 1) * HEAD_DIM] for g in range(SWA_KV_HEADS)]
            qhs = [q[:, h * HEAD_DIM:(h + 1) * HEAD_DIM] for h in range(SWA_HEADS)]
            dohs = [do[:, h * HEAD_DIM:(h + 1) * HEAD_DIM] for h in range(SWA_HEADS)]
            raw = [_dot_nt(qhs[h], kgs[h // SWA_GROUP]) for h in range(SWA_HEADS)]
            dps = [_dot_nt(dohs[h], vgs[h // SWA_GROUP]) for h in range(SWA_HEADS)]
            ps, dss = [], []
            for h in range(SWA_HEADS):
                lse_h = lse_all[:, h:h + 1]
                dlt = dl[:, FOX_HEADS + h:FOX_HEADS + h + 1]
                sc = jnp.where(mask, raw[h] + bias_ref[h], NEG)
                p = jnp.exp(sc - lse_h)
                ds = p * (dps[h] - dlt)
                dbias_ref[h] += ds
                p_sink = jnp.exp(sink_ref[0, h] - lse_h)
                gsk = gsk + jnp.where(lane == h, -jnp.sum(p_sink * dlt), 0.0)
                ps.append(p.astype(BF16))
                dss.append(ds.astype(BF16))
            dqs = [_dot(dss[h], kgs[h // SWA_GROUP]) * QK_SCALE for h in range(SWA_HEADS)]
            for h in range(SWA_HEADS):
                g = h // SWA_GROUP
                dk_h = _dot_tn(dss[h], qhs[h])
                dv_h = _dot_tn(ps[h], dohs[h])
                dks[g] = dk_h if dks[g] is None else dks[g] + dk_h
                dvs[g] = dv_h if dvs[g] is None else dvs[g] + dv_h
            dq_ref[...] = jnp.concatenate(dqs, axis=1).astype(BF16)
            sk_ref[...] += jnp.broadcast_to(gsk, sk_ref.shape)
            dk2 = jnp.concatenate(dks, axis=1)
            dv2 = jnp.concatenate(dvs, axis=1)
            dk_ref[...] = (ck_ref[...] + dk2[:BLOCK]).astype(BF16)
            dv_ref[...] = (cv_ref[...] + dv2[:BLOCK]).astype(BF16)
            ck_ref[...] = dk2[BLOCK:]
            cv_ref[...] = dv2[BLOCK:]

        @pl.when(n == nb)
        def _():
            dk_ref[...] = ck_ref[...].astype(BF16)
            dv_ref[...] = cv_ref[...].astype(BF16)
            gsk_ref[...] = sk_ref[...]
            bk = bk_ref[...]
            rowi = lax.broadcasted_iota(jnp.int32, (NUM_BUCKETS, LANES), 0)
            lanei = lax.broadcasted_iota(jnp.int32, (NUM_BUCKETS, LANES), 1)
            out = jnp.zeros((NUM_BUCKETS, LANES), F32)
            for h in range(SWA_HEADS):
                db = dbias_ref[h]
                for b in range(NUM_BUCKETS):
                    val = jnp.sum(jnp.where(bk == b, db, 0.0))
                    out = jnp.where((rowi == b) & (lanei == h), val, out)
            grb_ref[...] = out

    cq, ck, cv = COL_SQ // SWA_W, COL_SK // LANES, COL_SV // LANES
    cur = lambda n: jnp.minimum(n, nb - 1)
    prev = lambda n: jnp.maximum(jnp.minimum(n, nb - 1) - 1, 0)
    kout = lambda n: jnp.maximum(n - 1, 0)
    return pl.pallas_call(
        kern, name="swa_bwd",
        grid=(nb + 1,),
        in_specs=[pl.BlockSpec((BLOCK, SWA_W), lambda n: (cur(n), cq)),
                  pl.BlockSpec((BLOCK, LANES), lambda n: (prev(n), ck)),
                  pl.BlockSpec((BLOCK, LANES), lambda n: (cur(n), ck)),
                  pl.BlockSpec((BLOCK, LANES), lambda n: (prev(n), cv)),
                  pl.BlockSpec((BLOCK, LANES), lambda n: (cur(n), cv)),
                  pl.BlockSpec((BLOCK, SWA_W), lambda n: (cur(n), 1)),
                  pl.BlockSpec((BLOCK, LANES), lambda n: (cur(n), 0)),
                  pl.BlockSpec((BLOCK, LANES), lambda n: (cur(n), 0)),
                  pl.BlockSpec((SWA_HEADS, BLOCK, 2 * BLOCK), lambda n: (0, 0, 0)),
                  pl.BlockSpec(memory_space=pltpu.SMEM),
                  pl.BlockSpec((BLOCK, 2 * BLOCK), lambda n: (0, 0))],
        out_specs=[pl.BlockSpec((BLOCK, SWA_W), lambda n: (cur(n), 0)),
                   pl.BlockSpec((BLOCK, LANES), lambda n: (kout(n), 0)),
                   pl.BlockSpec((BLOCK, LANES), lambda n: (kout(n), 0)),
                   pl.BlockSpec((NUM_BUCKETS, LANES), lambda n: (0, 0)),
                   pl.BlockSpec((8, LANES), lambda n: (0, 0))],
        out_shape=[jax.ShapeDtypeStruct((s, SWA_W), BF16),
                   jax.ShapeDtypeStruct((s, LANES), BF16),
                   jax.ShapeDtypeStruct((s, LANES), BF16),
                   jax.ShapeDtypeStruct((NUM_BUCKETS, LANES), F32),
                   jax.ShapeDtypeStruct((8, LANES), F32)],
        scratch_shapes=[pltpu.VMEM((SWA_HEADS, BLOCK, 2 * BLOCK), F32),
                        pltpu.VMEM((BLOCK, LANES), F32),
                        pltpu.VMEM((BLOCK, LANES), F32),
                        pltpu.VMEM((8, LANES), F32)],
        compiler_params=_cparams(("arbitrary",)),
    )(qkv, qkv, qkv, qkv, qkv, do_bf, delta, lse, bias, sink, bucket)


def _post(x, target, o_fox, o_swa, z, w_o, ln_g, ln_b):
    s = x.shape[0]
    tm = min(256, s)
    nt = s // tm

    def kern(x_ref, t_ref, of_ref, os_ref, z_ref, w_ref, g_ref, b_ref,
             loss_ref, dh_ref, dy_ref, mix_ref, do_ref, dz_ref, dl_ref, gg_ref, gb_ref, lacc_ref):
        step = pl.program_id(0)

        @pl.when(step == 0)
        def _():
            lacc_ref[...] = jnp.zeros_like(lacc_ref)
            gg_ref[...] = jnp.zeros_like(gg_ref)
            gb_ref[...] = jnp.zeros_like(gb_ref)

        o = jnp.concatenate([of_ref[...], os_ref[...]], axis=1)
        zz = z_ref[...]
        sig = 1.0 / (1.0 + jnp.exp(-zz))
        silu = zz * sig
        mixed32 = o * silu
        mixed = mixed32.astype(BF16)
        mix_ref[...] = mixed32.T.astype(BF16)
        w = w_ref[...]
        h = ALPHA * x_ref[...] + _dot(mixed, w)
        mu = jnp.mean(h, axis=1, keepdims=True)
        hc = h - mu
        var = jnp.mean(hc * hc, axis=1, keepdims=True)
        rstd = lax.rsqrt(var + LN_EPS)
        xhat = hc * rstd
        g = g_ref[...]
        err = xhat * g + b_ref[...] - t_ref[...]
        lacc_ref[...] += jnp.broadcast_to(jnp.sum(err * err, axis=0, keepdims=True), lacc_ref.shape)
        dout = err * (1.0 / D_MODEL)
        gg_ref[...] += jnp.broadcast_to(jnp.sum(dout * xhat, axis=0, keepdims=True), gg_ref.shape)
        gb_ref[...] += jnp.broadcast_to(jnp.sum(dout, axis=0, keepdims=True), gb_ref.shape)
        dxh = dout * g
        m1 = jnp.mean(dxh, axis=1, keepdims=True)
        m2 = jnp.mean(dxh * xhat, axis=1, keepdims=True)
        dh = rstd * (dxh - m1 - xhat * m2)
        dh_ref[...] = dh
        dy = dh.astype(BF16)
        dy_ref[...] = dy
        dmix = _dot_nt(dy, w)
        do = dmix * silu
        do_ref[...] = do.astype(BF16)
        dz_ref[...] = (dmix * o * (sig * (1.0 + zz * (1.0 - sig)))).astype(BF16)
        r = lax.broadcasted_iota(jnp.int32, (D_MODEL, LANES), 0) // HEAD_DIM
        c = lax.broadcasted_iota(jnp.int32, (D_MODEL, LANES), 1)
        pick = jnp.where(r == c, 1.0, 0.0).astype(BF16)
        dl_ref[...] = _exact_dot(pick, do * o, False)

        @pl.when(step == nt - 1)
        def _():
            tot = jnp.sum(lacc_ref[0:1, :]) * (0.5 / D_MODEL)
            loss_ref[...] = jnp.broadcast_to(tot, loss_ref.shape)

    row = lambda i: (i, 0)
    fixed = lambda i: (0, 0)
    wide = pl.BlockSpec((tm, D_MODEL), row)
    half = pl.BlockSpec((tm, FOX_W), row)
    return pl.pallas_call(
        kern, name="post",
        grid=(nt,),
        in_specs=[wide, wide, half, half, wide,
                  pl.BlockSpec((D_MODEL, D_MODEL), fixed),
                  pl.BlockSpec((1, D_MODEL), fixed),
                  pl.BlockSpec((1, D_MODEL), fixed)],
        out_specs=[pl.BlockSpec((8, LANES), fixed), wide, wide,
                   pl.BlockSpec((D_MODEL, tm), lambda i: (0, i)), wide, wide,
                   pl.BlockSpec((tm, LANES), row),
                   pl.BlockSpec((8, D_MODEL), fixed), pl.BlockSpec((8, D_MODEL), fixed)],
        out_shape=[jax.ShapeDtypeStruct((8, LANES), F32),
                   jax.ShapeDtypeStruct((s, D_MODEL), F32),
                   jax.ShapeDtypeStruct((s, D_MODEL), BF16),
                   jax.ShapeDtypeStruct((D_MODEL, s), BF16),
                   jax.ShapeDtypeStruct((s, D_MODEL), BF16),
                   jax.ShapeDtypeStruct((s, D_MODEL), BF16),
                   jax.ShapeDtypeStruct((s, LANES), F32),
                   jax.ShapeDtypeStruct((8, D_MODEL), F32),
                   jax.ShapeDtypeStruct((8, D_MODEL), F32)],
        scratch_shapes=[pltpu.VMEM((8, D_MODEL), F32)],
        compiler_params=_cparams(("arbitrary",)),
    )(x, target, o_fox, o_swa, z, w_o, ln_g, ln_b)


def _adamw_math(w, g, m, v):
    m = ADAM_B1 * m + (1.0 - ADAM_B1) * g
    v = ADAM_B2 * v + (1.0 - ADAM_B2) * (g * g)
    m_hat = m / (1.0 - ADAM_B1 ** ADAM_STEP)
    v_hat = v / (1.0 - ADAM_B2 ** ADAM_STEP)
    delta = -ADAM_LR * (m_hat / (jnp.sqrt(v_hat) + ADAM_EPS) + ADAM_WD * w)
    return delta, m, v


def _adamw(w, g, m, v, *, name):
    r, c = w.shape
    tr = min(256, r)

    def kern(w_ref, g_ref, m_ref, v_ref, d_ref, mo_ref, vo_ref):
        d, mn, vn = _adamw_math(w_ref[...], g_ref[...], m_ref[...], v_ref[...])
        d_ref[...] = d
        mo_ref[...] = mn
        vo_ref[...] = vn

    blk = pl.BlockSpec((tr, c), lambda i: (i, 0))
    sds = jax.ShapeDtypeStruct((r, c), F32)
    return pl.pallas_call(
        kern, name=name,
        grid=(r // tr,),
        in_specs=[blk, blk, blk, blk],
        out_specs=[blk, blk, blk],
        out_shape=[sds, sds, sds],
        compiler_params=_cparams(("parallel",)),
    )(w, g, m, v)


def _adamw_cols(w, g, m, v, *, name):
    c, _, r = w.shape
    tc = 139
    assert c % tc == 0

    def kern(w_ref, g_ref, m_ref, v_ref, d_ref, mo_ref, vo_ref):
        d, mn, vn = _adamw_math(w_ref[...], g_ref[...], m_ref[...], v_ref[...])
        d_ref[...] = d
        mo_ref[...] = mn
        vo_ref[...] = vn

    blk = pl.BlockSpec((tc, 1, r), lambda i: (i, 0, 0))
    sds = jax.ShapeDtypeStruct((c, 1, r), F32)
    return pl.pallas_call(
        kern, name=name,
        grid=(c // tc,),
        in_specs=[blk, blk, blk, blk],
        out_specs=[blk, blk, blk],
        out_shape=[sds, sds, sds],
        compiler_params=_cparams(("parallel",)),
    )(w, g, m, v)


def _position():
    x, y, c = lax.axis_index("x"), lax.axis_index("y"), lax.axis_index("c")
    chips = [(1 - x, y), (x, 1 - y), (1 - x, 1 - y)]
    return x, y, c, chips


def _chip_index(cx, cy):
    return 2 * cx + cy


def _gather_weights(w_in_bf, w_o_bf):
    shards = (w_in_bf, w_o_bf)
    n_arr = len(shards)

    def kern(*refs):
        ins, outs = refs[:n_arr], refs[n_arr:2 * n_arr]
        send_sems, recv_sems, local_sems = refs[2 * n_arr:]
        x, y, c, chips = _position()
        me = _chip_index(x, y)
        sibling = (x, y, 1 - c)

        local = [pltpu.make_async_copy(ins[a], outs[a].at[me], local_sems.at[a]) for a in range(n_arr)]
        for cp in local:
            cp.start()

        def half(ref, a):
            rows = shards[a].shape[0] // 2
            return ref.at[pl.ds(c * rows, rows), :]

        def copy(a, k, src, slot, to):
            return pltpu.make_async_remote_copy(
                src_ref=src, dst_ref=half(outs[a].at[slot], a),
                send_sem=send_sems.at[a * 6 + k], recv_sem=recv_sems.at[a * 6 + k],
                device_id=to, device_id_type=MESH)

        first = [copy(a, j, half(ins[a], a), me, (*chip, c)) for a in range(n_arr) for j, chip in enumerate(chips)]
        for cp in first:
            cp.start()
        passed = []
        for a in range(n_arr):
            for j, chip in enumerate(chips):
                slot = _chip_index(*chip)
                copy(a, j, half(ins[a], a), slot, (*chip, c)).wait_recv()
                fwd = copy(a, 3 + j, half(outs[a].at[slot], a), slot, sibling)
                fwd.start()
                passed.append(fwd)
        for a in range(n_arr):
            for j, chip in enumerate(chips):
                slot = _chip_index(*chip)
                rows = shards[a].shape[0] // 2
                dst = outs[a].at[slot].at[pl.ds((1 - c) * rows, rows), :]
                pltpu.make_async_remote_copy(
                    src_ref=dst, dst_ref=dst, send_sem=send_sems.at[a * 6 + 3 + j],
                    recv_sem=recv_sems.at[a * 6 + 3 + j], device_id=sibling, device_id_type=MESH).wait_recv()
        for cp in first + passed:
            cp.wait_send()
        for cp in local:
            cp.wait()

    vmem = pl.BlockSpec(memory_space=pltpu.VMEM)
    return pl.pallas_call(
        kern, name="gather_weights",
        in_specs=[vmem] * n_arr,
        out_specs=[vmem] * n_arr,
        out_shape=[jax.ShapeDtypeStruct((N_CHIPS,) + w.shape, w.dtype) for w in shards],
        scratch_shapes=[pltpu.SemaphoreType.DMA((6 * n_arr,)),
                        pltpu.SemaphoreType.DMA((6 * n_arr,)),
                        pltpu.SemaphoreType.DMA((n_arr,))],
        compiler_params=_cparams(),
    )(*shards)


def _swap_halves(grads):
    n_arr = len(grads)

    def kern(*refs):
        ins = refs[:n_arr]
        owns = refs[n_arr:2 * n_arr]
        gots = refs[2 * n_arr:3 * n_arr]
        send_sems, recv_sems, local_sems = refs[3 * n_arr:]
        x, y, c, _ = _position()
        sibling = (x, y, 1 - c)
        local, remote = [], []
        for a in range(n_arr):
            rows = grads[a].shape[1] // 2
            piece = rows // COPY_PIECES
            for j in range(N_CHIPS):
                for r in range(COPY_PIECES):
                    k = (a * N_CHIPS + j) * COPY_PIECES + r
                    dst_rows = pl.ds(r * piece, piece)
                    local.append(pltpu.make_async_copy(
                        ins[a].at[j, pl.ds(c * rows + r * piece, piece), :],
                        owns[a].at[j, dst_rows, :], local_sems.at[k]))
                    remote.append(pltpu.make_async_remote_copy(
                        src_ref=ins[a].at[j, pl.ds((1 - c) * rows + r * piece, piece), :],
                        dst_ref=gots[a].at[j, dst_rows, :], send_sem=send_sems.at[k], recv_sem=recv_sems.at[k],
                        device_id=sibling, device_id_type=MESH))
        for cp in local + remote:
            cp.start()
        for cp in remote:
            cp.wait()
        for cp in local:
            cp.wait()

    hbm = pl.BlockSpec(memory_space=pltpu.VMEM)
    half = [jax.ShapeDtypeStruct((N_CHIPS, g.shape[1] // 2, g.shape[2]), F32) for g in grads]
    outs = pl.pallas_call(
        kern, name="swap_halves",
        in_specs=[hbm] * n_arr,
        out_specs=[hbm] * (2 * n_arr),
        out_shape=half + half,
        scratch_shapes=[pltpu.SemaphoreType.DMA((n_arr * N_CHIPS * COPY_PIECES,)),
                        pltpu.SemaphoreType.DMA((n_arr * N_CHIPS * COPY_PIECES,)),
                        pltpu.SemaphoreType.DMA((n_arr * N_CHIPS * COPY_PIECES,))],
        compiler_params=_cparams(),
    )(*grads)
    return outs[:n_arr], outs[n_arr:]


def _scatter_to_owners(parts):
    n_arr = len(parts)

    def kern(*refs):
        ins = refs[:n_arr]
        outs = refs[n_arr:2 * n_arr]
        send_sems, recv_sems, local_sems = refs[2 * n_arr:]
        x, y, c, chips = _position()
        me = _chip_index(x, y)
        local = [pltpu.make_async_copy(ins[a].at[me], outs[a].at[me], local_sems.at[a]) for a in range(n_arr)]
        for cp in local:
            cp.start()
        sends = []
        for a in range(n_arr):
            for j, chip in enumerate(chips):
                sends.append(pltpu.make_async_remote_copy(
                    src_ref=ins[a].at[_chip_index(*chip)], dst_ref=outs[a].at[me],
                    send_sem=send_sems.at[a * 3 + j], recv_sem=recv_sems.at[a * 3 + j],
                    device_id=(*chip, c), device_id_type=MESH))
        for cp in sends:
            cp.start()
        for a in range(n_arr):
            for j, chip in enumerate(chips):
                slot = outs[a].at[_chip_index(*chip)]
                pltpu.make_async_remote_copy(
                    src_ref=slot, dst_ref=slot, send_sem=send_sems.at[a * 3 + j],
                    recv_sem=recv_sems.at[a * 3 + j], device_id=(*chip, c), device_id_type=MESH).wait_recv()
        for cp in sends:
            cp.wait_send()
        for cp in local:
            cp.wait()

    hbm = pl.BlockSpec(memory_space=pltpu.VMEM)
    return pl.pallas_call(
        kern, name="scatter_to_owners",
        in_specs=[hbm] * n_arr,
        out_specs=[hbm] * n_arr,
        out_shape=[jax.ShapeDtypeStruct(p.shape, p.dtype) for p in parts],
        scratch_shapes=[pltpu.SemaphoreType.DMA((3 * n_arr,)),
                        pltpu.SemaphoreType.DMA((3 * n_arr,)),
                        pltpu.SemaphoreType.DMA((n_arr,))],
        compiler_params=_cparams(),
    )(*parts)


def _join_halves(halves):
    n_arr = len(halves)

    def kern(*refs):
        ins = refs[:n_arr]
        outs = refs[n_arr:2 * n_arr]
        send_sems, recv_sems, local_sems = refs[2 * n_arr:]
        x, y, c, _ = _position()
        sibling = (x, y, 1 - c)
        local, remote = [], []
        for a in range(n_arr):
            rows = halves[a].shape[0]
            piece = rows // COPY_PIECES
            for r in range(COPY_PIECES):
                k = a * COPY_PIECES + r
                src = ins[a].at[pl.ds(r * piece, piece), :]
                dst = outs[a].at[pl.ds(c * rows + r * piece, piece), :]
                local.append(pltpu.make_async_copy(src, dst, local_sems.at[k]))
                remote.append(pltpu.make_async_remote_copy(
                    src_ref=src, dst_ref=dst, send_sem=send_sems.at[k], recv_sem=recv_sems.at[k],
                    device_id=sibling, device_id_type=MESH))
        for cp in local + remote:
            cp.start()
        for a in range(n_arr):
            rows = halves[a].shape[0]
            piece = rows // COPY_PIECES
            for r in range(COPY_PIECES):
                k = a * COPY_PIECES + r
                theirs = outs[a].at[pl.ds((1 - c) * rows + r * piece, piece), :]
                pltpu.make_async_remote_copy(
                    src_ref=theirs, dst_ref=theirs, send_sem=send_sems.at[k], recv_sem=recv_sems.at[k],
                    device_id=sibling, device_id_type=MESH).wait_recv()
        for cp in remote:
            cp.wait_send()
        for cp in local:
            cp.wait()

    hbm = pl.BlockSpec(memory_space=pltpu.VMEM)
    return pl.pallas_call(
        kern, name="join_halves",
        in_specs=[hbm] * n_arr,
        out_specs=[hbm] * n_arr,
        out_shape=[jax.ShapeDtypeStruct((2 * h.shape[0], h.shape[1]), F32) for h in halves],
        scratch_shapes=[pltpu.SemaphoreType.DMA((n_arr * COPY_PIECES,)),
                        pltpu.SemaphoreType.DMA((n_arr * COPY_PIECES,)),
                        pltpu.SemaphoreType.DMA((n_arr * COPY_PIECES,))],
        compiler_params=_cparams(),
    )(*halves)


def _add2(a, b, *, name):
    n, r, c = a.shape
    tr = min(256, r)

    def kern(a_ref, b_ref, o_ref):
        o_ref[...] = (a_ref[...] + b_ref[...]).astype(BF16)

    blk = pl.BlockSpec((1, tr, c), lambda j, i: (j, i, 0))
    return pl.pallas_call(
        kern, name=name,
        grid=(n, r // tr),
        in_specs=[blk, blk],
        out_specs=blk,
        out_shape=jax.ShapeDtypeStruct(a.shape, BF16),
        compiler_params=_cparams(("parallel", "parallel")),
    )(a, b)


def _sum4(a, *, name):
    n, r, c = a.shape
    tr = min(256, r)

    def kern(a_ref, o_ref):
        f = lambda j: a_ref[j].astype(F32)
        o_ref[...] = ((f(0) + f(1)) + f(2)) + f(3)

    return pl.pallas_call(
        kern, name=name,
        grid=(r // tr,),
        in_specs=[pl.BlockSpec((n, tr, c), lambda i: (0, i, 0))],
        out_specs=pl.BlockSpec((tr, c), lambda i: (i, 0)),
        out_shape=jax.ShapeDtypeStruct((r, c), F32),
        compiler_params=_cparams(("parallel",)),
    )(a)


def _small_allreduce_adamw(g, w, m, v):
    def kern(g_ref, w_ref, m_ref, v_ref, gs_ref, d_ref, mo_ref, vo_ref, buf_ref, send_sems, recv_sems):
        x, y, c, _ = _position()
        me = 4 * x + 2 * y + c
        buf_ref[me] = g_ref[...]
        peers = [(x, y, 1 - c)] + [(px, py, pc) for px, py in _position()[3] for pc in (c, 1 - c)]
        sends = []
        for k, peer in enumerate(peers):
            sends.append(pltpu.make_async_remote_copy(
                src_ref=g_ref, dst_ref=buf_ref.at[me], send_sem=send_sems.at[k], recv_sem=recv_sems.at[k],
                device_id=peer, device_id_type=MESH))
        for cp in sends:
            cp.start()
        for k, (px, py, pc) in enumerate(peers):
            slot = buf_ref.at[4 * px + 2 * py + pc]
            pltpu.make_async_remote_copy(
                src_ref=slot, dst_ref=slot, send_sem=send_sems.at[k], recv_sem=recv_sems.at[k],
                device_id=(px, py, pc), device_id_type=MESH).wait_recv()
        for cp in sends:
            cp.wait_send()
        tot = buf_ref[0]
        for d in range(1, N_DEV):
            tot = tot + buf_ref[d]
        gs_ref[...] = tot
        delta, mn, vn = _adamw_math(w_ref[...], tot, m_ref[...], v_ref[...])
        d_ref[...] = delta
        mo_ref[...] = mn
        vo_ref[...] = vn

    vm = pl.BlockSpec(memory_space=pltpu.VMEM)
    sds = jax.ShapeDtypeStruct((SMALL_ROWS, LANES), F32)
    return pl.pallas_call(
        kern, name="small_allreduce_adamw",
        in_specs=[vm] * 4,
        out_specs=[vm] * 4,
        out_shape=[sds] * 4,
        scratch_shapes=[pltpu.VMEM((N_DEV, SMALL_ROWS, LANES), F32),
                        pltpu.SemaphoreType.DMA((N_DEV - 1,)),
                        pltpu.SemaphoreType.DMA((N_DEV - 1,))],
    )(g, w, m, v)


def _to_padded_cols(w):
    pad = jnp.zeros((w.shape[0], N_C - FOX_HEADS), w.dtype)
    return jnp.concatenate([w[:, 0:1536], w[:, 2056:2824], w[:, 1536:1544], pad,
                            w[:, 1544:2056], w[:, 2824:3336]], axis=1)


def _from_padded_cols(g):
    return jnp.concatenate([g[:, 0:1536], g[:, OFF_C:OFF_C + FOX_HEADS], g[:, OFF_B:OFF_B + FOX_W],
                            g[:, 1536:N_A], g[:, OFF_B + FOX_W:N_PAD]], axis=1)


def _pack_small(b_f, rel_bias, sink, ln_g, ln_b):
    row = lambda v: jnp.pad(v.reshape(1, -1), ((0, 0), (0, LANES - v.size)))
    return jnp.concatenate([ln_g.reshape(8, LANES), ln_b.reshape(8, LANES), rel_bias.reshape(2, LANES),
                            row(b_f), row(sink), jnp.zeros((4, LANES), F32)], axis=0)


def _unpack_small(p):
    ln_g = p[0:8].reshape(1, D_MODEL)
    ln_b = p[8:16].reshape(1, D_MODEL)
    rel_bias = p[16:18].reshape(NUM_BUCKETS, SWA_HEADS)
    b_f = p[18:19, :FOX_HEADS]
    sink = p[19:20, :SWA_HEADS]
    return b_f, rel_bias, sink, ln_g, ln_b


def _fox_rows(a):
    return a[:, :FOX_HEADS].T.reshape(FOX_HEADS, 1, a.shape[0])


def kernel(x, w_in, b_f, rel_bias, sink, w_o, ln_g, ln_b, loss_target, m_w_in, m_b_f, m_rel_bias, m_sink, m_w_o, m_ln_g, m_ln_b, v_w_in, v_b_f, v_rel_bias, v_sink, v_w_o, v_ln_g, v_ln_b):
    x2 = x[0]
    tgt = loss_target[0]
    s = x2.shape[0]
    w_in2, w_o2 = w_in[0], w_o[0]

    shard_cols = D_IN // N_CHIPS
    col_pad = ((0, 0), (0, SHARD_PAD - shard_cols))
    w_in_all, w_o_all = _gather_weights(jnp.pad(w_in2.astype(BF16), col_pad), w_o2.astype(BF16))
    w_full = jnp.concatenate([w_in_all[j, :, :shard_cols] for j in range(N_CHIPS)], axis=1)
    w_pad = _to_padded_cols(w_full)
    w_o_full = w_o_all.reshape(D_MODEL, D_MODEL)

    qkv, ffp, z, xt, vt = _project(x2, w_pad)
    bfp = jnp.pad(b_f, ((0, 0), (0, LANES - FOX_HEADS)))
    cum = _cum_fwd(ffp, bfp)
    cum_t3 = _fox_rows(cum)
    o_fox, lse_t3 = _fox_fwd(qkv, vt, cum_t3, cum)
    bucket = jnp.asarray(_bucket_table())
    bias = _swa_bias(rel_bias, bucket)
    o_swa, lse_swa = _swa_fwd(qkv, bias, sink)

    loss8, dh, dy, mixed_t, do_bf, dz, delta, gg8, gb8 = _post(
        x2, tgt, o_fox, o_swa, z, w_o_full, ln_g, ln_b)
    grad_w_o_full = _matmul_acc(mixed_t, dy, tm=1024, tn=512, tk=1024, name="grad_w_o")

    delta_t3 = _fox_rows(delta)
    dqt_fox, dk_fox, dv_fox, dcum_k, dcum_q = _fox_bwd(qkv, do_bf, cum_t3, cum, lse_t3, delta_t3)
    dcum_q = jnp.pad(dcum_q.reshape(FOX_HEADS, s).T, ((0, 0), (0, LANES - FOX_HEADS)))
    dff, gbf8 = _cum_bwd(dcum_k, dcum_q, ffp, bfp)
    dq_swa, dk_swa, dv_swa, grb, gsk8 = _swa_bwd(qkv, do_bf, delta, lse_swa, bias, sink, bucket)

    dq_fox = dqt_fox.T.astype(BF16)
    d_misc = jnp.concatenate([dk_swa, dv_swa, dff], axis=1)
    pieces = [dq_fox, dk_fox, dv_fox, dq_swa, d_misc, dz]
    grad_x = _grad_x_matmul(pieces, w_pad, dh, tm=512, tn=512, name="grad_x")
    blocks = [(p, 0) for p in pieces[:-1]] + [(dz, 0), (dz, 1)]
    grad_w_pad = _grad_w_matmul(xt, blocks, tk=1024, name="grad_w_in")
    grad_w_in_full = _from_padded_cols(grad_w_pad)

    g_in4 = jnp.stack([jnp.pad(grad_w_in_full[:, j * shard_cols:(j + 1) * shard_cols], col_pad)
                       for j in range(N_CHIPS)])
    g_o4 = grad_w_o_full.reshape(N_CHIPS, D_MODEL // N_CHIPS, D_MODEL)
    owns, gots = _swap_halves([g_in4, g_o4])
    parts = [_add2(owns[0], gots[0], name="pair_sum_w_in"), _add2(owns[1], gots[1], name="pair_sum_w_o")]
    slabs = _scatter_to_owners(parts)
    halves = [_sum4(slabs[0], name="chip_sum_w_in"), _sum4(slabs[1], name="chip_sum_w_o")]
    g_w_in, g_w_o = _join_halves(halves)
    g_w_in = g_w_in[:, :shard_cols]

    cols_first = lambda a: jnp.transpose(a, (2, 0, 1))
    rows_first = lambda a: jnp.transpose(a, (1, 2, 0))
    d_w_in, nm_w_in, nv_w_in = [rows_first(a) for a in _adamw_cols(
        cols_first(w_in), cols_first(g_w_in[None]), cols_first(m_w_in), cols_first(v_w_in), name="adamw_w_in")]
    d_w_o, nm_w_o, nv_w_o = _adamw(w_o2, g_w_o, m_w_o[0], v_w_o[0], name="adamw_w_o")

    g_small = _pack_small(gbf8[0:1, :FOX_HEADS], grb[:, :SWA_HEADS], gsk8[0:1, :SWA_HEADS], gg8[0:1], gb8[0:1])
    g_small = g_small.at[LOSS_ROW, 0].set(loss8[0, 0])
    w_small = _pack_small(b_f, rel_bias, sink, ln_g, ln_b)
    m_small = _pack_small(m_b_f, m_rel_bias, m_sink, m_ln_g, m_ln_b)
    v_small = _pack_small(v_b_f, v_rel_bias, v_sink, v_ln_g, v_ln_b)
    gs, ds, ms, vs = _small_allreduce_adamw(g_small, w_small, m_small, v_small)
    loss = gs[LOSS_ROW, 0]
    g_bf, g_rb, g_sk, g_lg, g_lb = _unpack_small(gs)
    d_bf, d_rb, d_sk, d_lg, d_lb = _unpack_small(ds)
    m_bf, m_rb, m_sk, m_lg, m_lb = _unpack_small(ms)
    v_bf, v_rb, v_sk, v_lg, v_lb = _unpack_small(vs)

    e = lambda a: a[None]
    return (loss, e(grad_x),
            e(g_w_in), g_bf, g_rb, g_sk, e(g_w_o), g_lg, g_lb,
            d_w_in, d_bf, d_rb, d_sk, e(d_w_o), d_lg, d_lb,
            nm_w_in, m_bf, m_rb, m_sk, e(nm_w_o), m_lg, m_lb,
            nv_w_in, v_bf, v_rb, v_sk, e(nv_w_o), v_lg, v_lb)
```

```python
import functools
import math

import numpy as np
import jax
import jax.numpy as jnp
from jax import lax
from jax.experimental import pallas as pl
from jax.experimental.pallas import tpu as pltpu

F32 = jnp.float32
BF16 = jnp.bfloat16

D_MODEL = 1024
HEAD_DIM = 64
FOX_HEADS = 8
SWA_HEADS = 8
SWA_KV_HEADS = 2
SWA_GROUP = 4
FOX_W = 512
SWA_W = 512
SWA_KV_W = 128
BLOCK = 128
NUM_BUCKETS = 32
MAX_DISTANCE = 128
LN_EPS = 1e-5
NEG = -1e30
ALPHA = 2.0 ** 0.25
QK_SCALE = 0.125

ADAM_LR = 0.001
ADAM_B1 = 0.9
ADAM_B2 = 0.999
ADAM_EPS = 1e-08
ADAM_WD = 0.01
ADAM_STEP = 10

D_IN = 3336
SHARD_PAD = 896
N_A = 2304
N_C = 256
N_B = 1024
OFF_C = N_A
OFF_B = N_A + N_C
N_PAD = N_A + N_C + N_B
COL_FK, COL_FV, COL_SQ, COL_SK, COL_SV = 512, 1024, 1536, 2048, 2176

LANES = 128
FOX_T = 256
FOX_REF = 512
SUM_ROWS = 16
VMEM_LIMIT = 56 * 1024 * 1024

MESH = pl.DeviceIdType.MESH
N_CHIPS = 4
N_DEV = 8
SMALL_ROWS = 24
LOSS_ROW = 20
COPY_PIECES = 4


def _cparams(sem=None):
    return pltpu.CompilerParams(dimension_semantics=sem, vmem_limit_bytes=VMEM_LIMIT)


def _split3(x):
    hi = x.astype(BF16)
    r = x - hi.astype(F32)
    mid = r.astype(BF16)
    lo = (r - mid.astype(F32)).astype(BF16)
    return hi, mid, lo


def _dot(a, b):
    return jnp.dot(a, b, preferred_element_type=F32)


def _dot_nt(a, b):
    return lax.dot_general(a, b, (((1,), (1,)), ((), ())), preferred_element_type=F32)


def _dot_tn(a, b):
    return lax.dot_general(a, b, (((0,), (0,)), ((), ())), preferred_element_type=F32)


def _project(x, w_pad):
    s, k = x.shape
    tm = 512
    chunk = 512

    def kern(x_ref, w_ref, qkv_ref, ff_ref, z_ref, xt_ref, vt_ref):
        xf = x_ref[...]
        xb = xf.astype(BF16)
        xt_ref[...] = xf.T.astype(BF16)
        for c0 in range(0, N_A, chunk):
            width = min(chunk, N_A - c0)
            res = _dot(xb, w_ref[:, c0:c0 + width])
            qkv_ref[:, c0:c0 + width] = res.astype(BF16)
            if c0 == COL_FV:
                vt_ref[...] = res.T.astype(BF16)
        ff_ref[...] = _dot(xb, w_ref[:, OFF_C:OFF_C + N_C])
        for c0 in range(0, N_B, 512):
            z_ref[:, c0:c0 + 512] = _dot(xb, w_ref[:, OFF_B + c0:OFF_B + c0 + 512])

    row = lambda i: (i, 0)
    return pl.pallas_call(
        kern, name="project",
        grid=(s // tm,),
        in_specs=[pl.BlockSpec((tm, k), row),
                  _resident((k, N_PAD), lambda i: (0, 0))],
        out_specs=[pl.BlockSpec((tm, N_A), row),
                   pl.BlockSpec((tm, N_C), row),
                   pl.BlockSpec((tm, N_B), row),
                   pl.BlockSpec((k, tm), lambda i: (0, i)),
                   pl.BlockSpec((FOX_W, tm), lambda i: (0, i))],
        out_shape=[jax.ShapeDtypeStruct((s, N_A), BF16),
                   jax.ShapeDtypeStruct((s, N_C), F32),
                   jax.ShapeDtypeStruct((s, N_B), F32),
                   jax.ShapeDtypeStruct((k, s), BF16),
                   jax.ShapeDtypeStruct((FOX_W, s), BF16)],
        compiler_params=_cparams(("parallel",)),
    )(x, w_pad)


def _grad_x_matmul(pieces, w_pad, dh, *, tm, tn, name):
    m = dh.shape[0]
    n, k = w_pad.shape
    widths = [p.shape[1] for p in pieces]
    offs = [sum(widths[:i]) for i in range(len(pieces))]
    assert sum(widths) == k

    def kern(*refs):
        p_refs, (b_ref, dh_ref, o_ref) = refs[:len(pieces)], refs[len(pieces):]
        acc = ALPHA * dh_ref[...]
        for p_ref, off, width in zip(p_refs, offs, widths):
            acc = acc + _dot_nt(p_ref[...], b_ref[:, off:off + width])
        o_ref[...] = acc

    assert tn == n
    return pl.pallas_call(
        kern, name=name,
        grid=(m // tm,),
        in_specs=[pl.BlockSpec((tm, w), lambda i: (i, 0)) for w in widths]
        + [_resident((n, k), lambda i: (0, 0)),
           pl.BlockSpec((tm, n), lambda i: (i, 0))],
        out_specs=pl.BlockSpec((tm, n), lambda i: (i, 0)),
        out_shape=jax.ShapeDtypeStruct((m, n), F32),
        compiler_params=_cparams(("parallel",)),
    )(*pieces, w_pad, dh)


def _grad_w_matmul(xt, blocks, *, tk, name):
    m, s = xt.shape
    tn = 512
    nb = len(blocks)

    def kern(a_ref, *refs):
        b_refs, o_ref = refs[:nb], refs[nb]

        @pl.when(pl.program_id(0) == 0)
        def _():
            o_ref[...] = jnp.zeros_like(o_ref)
        a = a_ref[...]
        for blk in range(nb):
            o_ref[:, blk * tn:(blk + 1) * tn] += _dot(a, b_refs[blk][...])

    return pl.pallas_call(
        kern, name=name,
        grid=(s // tk,),
        in_specs=[pl.BlockSpec((m, tk), lambda k: (0, k))]
        + [pl.BlockSpec((tk, tn), functools.partial(lambda k, col: (k, col), col=col)) for _, col in blocks],
        out_specs=_resident((m, nb * tn), lambda k: (0, 0)),
        out_shape=jax.ShapeDtypeStruct((m, nb * tn), F32),
        compiler_params=_cparams(("arbitrary",)),
    )(xt, *[arr for arr, _ in blocks])


def _tri(n, lower):
    r = lax.broadcasted_iota(jnp.int32, (n, n), 0)
    c = lax.broadcasted_iota(jnp.int32, (n, n), 1)
    keep = (c <= r) if lower else (c >= r)
    return jnp.where(keep, 1.0, 0.0).astype(BF16)


def _exact_dot(mat_bf16, x_f32, left):
    out = None
    for piece in _split3(x_f32):
        t = _dot(mat_bf16, piece) if left else _dot(piece, mat_bf16)
        out = t if out is None else out + t
    return out


def _log_sigmoid(z):
    return jnp.minimum(z, 0.0) - jnp.log(1.0 + jnp.exp(-jnp.abs(z)))


def _cum_fwd(ffp, bfp):
    s = ffp.shape[0]
    t = min(512, s)

    def kern(ff_ref, b_ref, cum_ref, carry_ref):
        @pl.when(pl.program_id(0) == 0)
        def _():
            carry_ref[...] = jnp.zeros_like(carry_ref)
        lane = lax.broadcasted_iota(jnp.int32, (1, LANES), 1)
        lf = _log_sigmoid(ff_ref[...] + b_ref[...])
        lf = jnp.where(lane < FOX_HEADS, lf, 0.0)
        cum = _exact_dot(_tri(t, True), lf, True) + carry_ref[0:1, :]
        cum_ref[...] = cum
        carry_ref[...] = jnp.broadcast_to(cum[t - 1:t, :], carry_ref.shape)

    return pl.pallas_call(
        kern, name="cum_fwd",
        grid=(s // t,),
        in_specs=[pl.BlockSpec((t, LANES), lambda i: (i, 0)),
                  pl.BlockSpec((1, LANES), lambda i: (0, 0))],
        out_specs=pl.BlockSpec((t, LANES), lambda i: (i, 0)),
        out_shape=jax.ShapeDtypeStruct((s, LANES), F32),
        scratch_shapes=[pltpu.VMEM((8, LANES), F32)],
        compiler_params=_cparams(("arbitrary",)),
    )(ffp, bfp)


def _cum_bwd(dcum_k, dcum_q, ffp, bfp):
    s = dcum_k.shape[0]
    t = min(512, s)
    nb = s // t

    def kern(dck_ref, dcq_ref, ff_ref, b_ref, dff_ref, gb_ref, carry_ref):
        @pl.when(pl.program_id(0) == 0)
        def _():
            carry_ref[...] = jnp.zeros_like(carry_ref)
            gb_ref[...] = jnp.zeros_like(gb_ref)
        lane = lax.broadcasted_iota(jnp.int32, (1, LANES), 1)
        dlf = _exact_dot(_tri(t, False), dck_ref[...] + dcq_ref[...], True) + carry_ref[0:1, :]
        carry_ref[...] = jnp.broadcast_to(dlf[0:1, :], carry_ref.shape)
        z = ff_ref[...] + b_ref[...]
        dff = jnp.where(lane < FOX_HEADS, dlf / (1.0 + jnp.exp(z)), 0.0)
        gb_ref[...] += jnp.broadcast_to(jnp.sum(dff, axis=0, keepdims=True), gb_ref.shape)
        dff_ref[...] = jnp.concatenate([dff, jnp.zeros_like(dff)], axis=1).astype(BF16)

    return pl.pallas_call(
        kern, name="cum_bwd",
        grid=(nb,),
        in_specs=[pl.BlockSpec((t, LANES), lambda i: (nb - 1 - i, 0)),
                  pl.BlockSpec((t, LANES), lambda i: (nb - 1 - i, 0)),
                  pl.BlockSpec((t, LANES), lambda i: (nb - 1 - i, 0)),
                  pl.BlockSpec((1, LANES), lambda i: (0, 0))],
        out_specs=[pl.BlockSpec((t, N_C), lambda i: (nb - 1 - i, 0)),
                   pl.BlockSpec((8, LANES), lambda i: (0, 0))],
        out_shape=[jax.ShapeDtypeStruct((s, N_C), BF16),
                   jax.ShapeDtypeStruct((8, LANES), F32)],
        scratch_shapes=[pltpu.VMEM((8, LANES), F32)],
        compiler_params=_cparams(("arbitrary",)),
    )(dcum_k, dcum_q, ffp, bfp)


def _resident(shape, index_map):
    return pl.BlockSpec(shape, index_map, pipeline_mode=pl.Buffered(1))


def _fox_fwd(qkv, vt, cum_t3, cum):
    s = qkv.shape[0]
    tk = min(FOX_T, s)
    tq = FOX_REF
    nq = s // tq
    nh = FOX_HEADS
    diag_tiles = tq // tk

    def kern(q_ref, k_ref, vt_ref, ct_ref, c_ref, o_ref, lse_ref, m_ref, acc_ref, u_ref):
        i = pl.program_id(0)
        lane = lax.broadcasted_iota(jnp.int32, (1, LANES), 1)
        krow = lax.broadcasted_iota(jnp.int32, (tk, tq), 0)
        qcol = lax.broadcasted_iota(jnp.int32, (tk, tq), 1)
        q0 = pl.multiple_of(i * tq, tq)
        qts, crefs = [], []
        for h in range(nh):
            p, a = divmod(h, 2)
            q2 = q_ref[:, p * LANES:(p + 1) * LANES] * jnp.asarray(QK_SCALE, BF16)
            sel = (lane < HEAD_DIM) if a == 0 else (lane >= HEAD_DIM)
            qts.append(jnp.where(sel, q2, jnp.zeros_like(q2)).astype(F32).T.astype(BF16))
            crefs.append(ct_ref[h, :, pl.ds(q0, LANES)][:, 0:1])
        m_ref[...] = jnp.full(m_ref.shape, NEG, F32)
        acc_ref[...] = jnp.zeros_like(acc_ref)
        ones = jnp.ones((SUM_ROWS, tk), BF16)

        def tile(j, diag):
            k0 = pl.multiple_of(j * tk, tk)
            cb = c_ref[pl.ds(k0, tk), :]
            sts = [_dot(k_ref[pl.ds(k0, tk), (h // 2) * LANES:(h // 2 + 1) * LANES], qts[h]) for h in range(nh)]
            tile_max = []
            for h in range(nh):
                u = sts[h] - (cb[:, h:h + 1] - crefs[h])
                if diag is not None:
                    u = jnp.where(krow + diag * tk <= qcol, u, NEG)
                u_ref[h] = u
                tile_max.append(jnp.max(u, axis=0, keepdims=True))
            pts, scales = [], []
            for h in range(nh):
                m_old = m_ref[h]
                m_new = jnp.maximum(m_old, tile_max[h])
                scales.append(jnp.exp(m_old - m_new))
                pts.append(jnp.exp(u_ref[h] - m_new).astype(BF16))
                m_ref[h] = m_new
            for h in range(nh):
                vth = jnp.concatenate([vt_ref[h * HEAD_DIM:(h + 1) * HEAD_DIM, pl.ds(k0, tk)], ones], axis=0)
                acc_ref[h] = scales[h] * acc_ref[h] + _dot(vth, pts[h])

        def body(j, c):
            tile(j, None)
            return c
        lax.fori_loop(0, i * diag_tiles, body, 0)
        for d in range(diag_tiles):
            tile(i * diag_tiles + d, d)

        ls = [acc_ref[h][HEAD_DIM:HEAD_DIM + 1] for h in range(nh)]
        for p in range(nh // 2):
            ot = jnp.concatenate([acc_ref[2 * p + a][:HEAD_DIM] * (1.0 / ls[2 * p + a]) for a in range(2)], axis=0)
            o_ref[:, p * LANES:(p + 1) * LANES] = ot.T
        for h in range(nh):
            lse_ref[h, :, pl.ds(q0, tq)] = m_ref[h] + jnp.log(ls[h])

    return pl.pallas_call(
        kern, name="fox_fwd",
        grid=(nq,),
        in_specs=[pl.BlockSpec((tq, FOX_W), lambda i: (i, 0)),
                  _resident((s, FOX_W), lambda i: (0, COL_FK // FOX_W)),
                  _resident((FOX_W, s), lambda i: (0, 0)),
                  _resident((nh, 1, s), lambda i: (0, 0, 0)),
                  _resident((s, LANES), lambda i: (0, 0))],
        out_specs=[pl.BlockSpec((tq, FOX_W), lambda i: (i, 0)),
                   pl.BlockSpec((nh, 1, s), lambda i: (0, 0, 0))],
        out_shape=[jax.ShapeDtypeStruct((s, FOX_W), F32),
                   jax.ShapeDtypeStruct((nh, 1, s), F32)],
        scratch_shapes=[pltpu.VMEM((nh, 1, tq), F32),
                        pltpu.VMEM((nh, HEAD_DIM + SUM_ROWS, tq), F32),
                        pltpu.VMEM((nh, tk, tq), F32)],
        compiler_params=_cparams(("arbitrary",)),
    )(qkv, qkv, vt, cum_t3, cum)


def _fox_bwd(qkv, do_bf, cum_t3, cum, lse_t3, delta_t3):
    s = qkv.shape[0]
    t = min(FOX_T, s)
    nq = s // t
    nh = FOX_HEADS
    npair = nh // 2

    def kern(q_ref, do_ref, k_ref, v_ref, ct_ref, c_ref, lse_ref, dl_ref,
             dqt_ref, dk_ref, dv_ref, dc_ref, dcq_ref, accv_ref, acck_ref, accd_ref):
        kj = pl.program_id(0)
        lane = lax.broadcasted_iota(jnp.int32, (1, LANES), 1)
        krow = lax.broadcasted_iota(jnp.int32, (t, t), 0)
        qcol = lax.broadcasted_iota(jnp.int32, (t, t), 1)
        causal = krow <= qcol
        sels = [lane < HEAD_DIM, lane >= HEAD_DIM]

        @pl.when(kj == 0)
        def _():
            dqt_ref[...] = jnp.zeros_like(dqt_ref)
            dcq_ref[...] = jnp.zeros_like(dcq_ref)

        accv_ref[...] = jnp.zeros_like(accv_ref)
        acck_ref[...] = jnp.zeros_like(acck_ref)
        accd_ref[...] = jnp.zeros_like(accd_ref)
        cb = c_ref[...]
        k2s, v2s, kts = [], [], []
        for p in range(npair):
            k2 = k_ref[:, p * LANES:(p + 1) * LANES]
            k2s.append(k2)
            v2s.append(v_ref[:, p * LANES:(p + 1) * LANES])
            kt = k2.astype(F32).T * QK_SCALE
            kts.append(kt[:HEAD_DIM].astype(BF16))
            kts.append(kt[HEAD_DIM:].astype(BF16))
        css = [cb[:, h:h + 1] for h in range(nh)]

        def tile(i, masked):
            q0 = pl.multiple_of(i * t, t)
            r0 = pl.multiple_of((i // (FOX_REF // t)) * FOX_REF, FOX_REF)
            sts, dpts, qms, doms = [], [], [], []
            for h in range(nh):
                p, a = divmod(h, 2)
                qi = q_ref[pl.ds(q0, t), p * LANES:(p + 1) * LANES] * jnp.asarray(QK_SCALE, BF16)
                doi = do_ref[pl.ds(q0, t), p * LANES:(p + 1) * LANES]
                qm = jnp.where(sels[a], qi, jnp.zeros_like(qi))
                dom = jnp.where(sels[a], doi, jnp.zeros_like(doi))
                qms.append(qm)
                doms.append(dom)
                sts.append(_dot_nt(k2s[p], qm))
                dpts.append(_dot_nt(v2s[p], dom))
            pts, dsts = [], []
            for h in range(nh):
                cref = ct_ref[h, :, pl.ds(r0, LANES)][:, 0:1]
                pt = jnp.exp(sts[h] - (css[h] - cref) - lse_ref[h, :, pl.ds(q0, t)])
                if masked:
                    pt = jnp.where(causal, pt, 0.0)
                ds32 = pt * (dpts[h] - dl_ref[h, :, pl.ds(q0, t)])
                part = ds32[:, 0:LANES]
                for c in range(1, t // LANES):
                    part = part + ds32[:, c * LANES:(c + 1) * LANES]
                accd_ref[h] += part
                dcq_ref[h, :, pl.ds(q0, t)] += jnp.sum(ds32, axis=0, keepdims=True)
                pts.append(pt.astype(BF16))
                dsts.append(ds32.astype(BF16))
            for p in range(npair):
                ha, hb = 2 * p, 2 * p + 1
                accv_ref[p] += _dot(pts[ha], doms[ha]) + _dot(pts[hb], doms[hb])
                acck_ref[p] += _dot(dsts[ha], qms[ha]) + _dot(dsts[hb], qms[hb])
            for h in range(nh):
                dqt_ref[h * HEAD_DIM:(h + 1) * HEAD_DIM, pl.ds(q0, t)] += _dot(kts[h], dsts[h])

        tile(kj, True)

        def body(i, c):
            tile(i, False)
            return c
        lax.fori_loop(kj + 1, nq, body, 0)

        dc = jnp.zeros((t, LANES), F32)
        for h in range(nh):
            dc = jnp.where(lane == h, -jnp.sum(accd_ref[h], axis=1, keepdims=True), dc)
        dc_ref[...] = dc
        for p in range(npair):
            dv_ref[:, p * LANES:(p + 1) * LANES] = accv_ref[p].astype(BF16)
            dk_ref[:, p * LANES:(p + 1) * LANES] = acck_ref[p].astype(BF16)

    whole = lambda kj: (0, 0, 0)
    return pl.pallas_call(
        kern, name="fox_bwd",
        grid=(nq,),
        in_specs=[_resident((s, FOX_W), lambda kj: (0, 0)),
                  _resident((s, FOX_W), lambda kj: (0, 0)),
                  pl.BlockSpec((t, FOX_W), lambda kj: (kj, COL_FK // FOX_W)),
                  pl.BlockSpec((t, FOX_W), lambda kj: (kj, COL_FV // FOX_W)),
                  _resident((nh, 1, s), whole),
                  pl.BlockSpec((t, LANES), lambda kj: (kj, 0)),
                  _resident((nh, 1, s), whole),
                  _resident((nh, 1, s), whole)],
        out_specs=[_resident((FOX_W, s), lambda kj: (0, 0)),
                   pl.BlockSpec((t, FOX_W), lambda kj: (kj, 0)),
                   pl.BlockSpec((t, FOX_W), lambda kj: (kj, 0)),
                   pl.BlockSpec((t, LANES), lambda kj: (kj, 0)),
                   _resident((nh, 1, s), whole)],
        out_shape=[jax.ShapeDtypeStruct((FOX_W, s), F32),
                   jax.ShapeDtypeStruct((s, FOX_W), BF16),
                   jax.ShapeDtypeStruct((s, FOX_W), BF16),
                   jax.ShapeDtypeStruct((s, LANES), F32),
                   jax.ShapeDtypeStruct((nh, 1, s), F32)],
        scratch_shapes=[pltpu.VMEM((npair, t, LANES), F32),
                        pltpu.VMEM((npair, t, LANES), F32),
                        pltpu.VMEM((nh, t, LANES), F32)],
        compiler_params=_cparams(("arbitrary",)),
    )(qkv, do_bf, qkv, qkv, cum_t3, cum, lse_t3, delta_t3)


def _bucket_table():
    qi = np.arange(BLOCK)[:, None]
    kj = np.arange(2 * BLOCK)[None, :]
    rel = np.maximum(qi + BLOCK - kj, 0).astype(np.int32)
    max_exact = NUM_BUCKETS // 2
    relf = np.maximum(rel, 1).astype(np.float32)
    large = max_exact + (np.log(relf / np.float32(max_exact)) / np.float32(math.log(MAX_DISTANCE / max_exact))
                         * np.float32(NUM_BUCKETS - max_exact)).astype(np.int32)
    large = np.minimum(large, NUM_BUCKETS - 1)
    return np.where(rel < max_exact, rel, large).astype(np.int32)


def _swa_bias(rel_bias, bucket):
    def kern(rb_ref, bk_ref, o_ref):
        bk = bk_ref[...]
        for h in range(SWA_HEADS):
            acc = jnp.zeros((BLOCK, 2 * BLOCK), F32)
            for b in range(NUM_BUCKETS):
                acc = jnp.where(bk == b, rb_ref[b, h], acc)
            o_ref[h] = acc

    return pl.pallas_call(
        kern, name="swa_bias",
        in_specs=[pl.BlockSpec(memory_space=pltpu.SMEM),
                  pl.BlockSpec(memory_space=pltpu.VMEM)],
        out_specs=pl.BlockSpec(memory_space=pltpu.VMEM),
        out_shape=jax.ShapeDtypeStruct((SWA_HEADS, BLOCK, 2 * BLOCK), F32),
        compiler_params=_cparams(),
    )(rel_bias, bucket)


def _swa_mask(n):
    qi = lax.broadcasted_iota(jnp.int32, (BLOCK, 2 * BLOCK), 0)
    kj = lax.broadcasted_iota(jnp.int32, (BLOCK, 2 * BLOCK), 1)
    rel = qi + BLOCK - kj
    band = (rel >= 0) & (rel < BLOCK)
    return band & ((kj >= BLOCK) | (n > 0))


def _swa_fwd(qkv, bias, sink):
    s = qkv.shape[0]
    nb = s // BLOCK

    def kern(q_ref, kp_ref, kc_ref, vp_ref, vc_ref, bias_ref, sink_ref, o_ref, lse_ref):
        n = pl.program_id(0)
        mask = _swa_mask(n)
        lane = lax.broadcasted_iota(jnp.int32, (1, LANES), 1)
        q = q_ref[...] * jnp.asarray(QK_SCALE, BF16)
        k = jnp.concatenate([kp_ref[...], kc_ref[...]], axis=0)
        v = jnp.concatenate([vp_ref[...], vc_ref[...]], axis=0)
        kgs = [k[:, g * HEAD_DIM:(g + 1) * HEAD_DIM] for g in range(SWA_KV_HEADS)]
        vgs = [v[:, g * HEAD_DIM:(g + 1) * HEAD_DIM] for g in range(SWA_KV_HEADS)]
        raw = [_dot_nt(q[:, h * HEAD_DIM:(h + 1) * HEAD_DIM], kgs[h // SWA_GROUP]) for h in range(SWA_HEADS)]
        probs = []
        lse_all = jnp.zeros((BLOCK, LANES), F32)
        for h in range(SWA_HEADS):
            sc = jnp.where(mask, raw[h] + bias_ref[h], NEG)
            sk = sink_ref[0, h]
            m = jnp.maximum(jnp.max(sc, axis=1, keepdims=True), sk)
            p = jnp.exp(sc - m)
            l = jnp.sum(p, axis=1, keepdims=True) + jnp.exp(sk - m)
            probs.append((p * (1.0 / l)).astype(BF16))
            lse_all = jnp.where(lane == h, m + jnp.log(l), lse_all)
        outs = [_dot(probs[h], vgs[h // SWA_GROUP]) for h in range(SWA_HEADS)]
        o_ref[...] = jnp.concatenate(outs, axis=1)
        lse_ref[...] = lse_all

    cq, ck, cv = COL_SQ // SWA_W, COL_SK // LANES, COL_SV // LANES
    prev = lambda n: jnp.maximum(n - 1, 0)
    return pl.pallas_call(
        kern, name="swa_fwd",
        grid=(nb,),
        in_specs=[pl.BlockSpec((BLOCK, SWA_W), lambda n: (n, cq)),
                  pl.BlockSpec((BLOCK, LANES), lambda n: (prev(n), ck)),
                  pl.BlockSpec((BLOCK, LANES), lambda n: (n, ck)),
                  pl.BlockSpec((BLOCK, LANES), lambda n: (prev(n), cv)),
                  pl.BlockSpec((BLOCK, LANES), lambda n: (n, cv)),
                  pl.BlockSpec((SWA_HEADS, BLOCK, 2 * BLOCK), lambda n: (0, 0, 0)),
                  pl.BlockSpec(memory_space=pltpu.SMEM)],
        out_specs=[pl.BlockSpec((BLOCK, SWA_W), lambda n: (n, 0)),
                   pl.BlockSpec((BLOCK, LANES), lambda n: (n, 0))],
        out_shape=[jax.ShapeDtypeStruct((s, SWA_W), F32),
                   jax.ShapeDtypeStruct((s, LANES), F32)],
        compiler_params=_cparams(("parallel",)),
    )(qkv, qkv, qkv, qkv, qkv, bias, sink)


def _swa_bwd(qkv, do_bf, delta, lse, bias, sink, bucket):
    s = qkv.shape[0]
    nb = s // BLOCK

    def kern(q_ref, kp_ref, kc_ref, vp_ref, vc_ref, do_ref, dl_ref, lse_ref, bias_ref, sink_ref, bk_ref,
             dq_ref, dk_ref, dv_ref, grb_ref, gsk_ref, dbias_ref, ck_ref, cv_ref, sk_ref):
        n = pl.program_id(0)
        lane = lax.broadcasted_iota(jnp.int32, (1, LANES), 1)

        @pl.when(n == 0)
        def _():
            dbias_ref[...] = jnp.zeros_like(dbias_ref)
            ck_ref[...] = jnp.zeros_like(ck_ref)
            cv_ref[...] = jnp.zeros_like(cv_ref)
            sk_ref[...] = jnp.zeros_like(sk_ref)

        @pl.when(n < nb)
        def _():
            mask = _swa_mask(n)
            q = q_ref[...] * jnp.asarray(QK_SCALE, BF16)
            k = jnp.concatenate([kp_ref[...], kc_ref[...]], axis=0)
            v = jnp.concatenate([vp_ref[...], vc_ref[...]], axis=0)
            do = do_ref[...]
            dl = dl_ref[...]
            lse_all = lse_ref[...]
            dks = [None] * SWA_KV_HEADS
            dvs = [None] * SWA_KV_HEADS
            gsk = jnp.zeros((1, LANES), F32)
            kgs = [k[:, g * HEAD_DIM:(g + 1) * HEAD_DIM] for g in range(SWA_KV_HEADS)]
            vgs = [v[:, g * HEAD_DIM:(g + 1) * HEAD_DIM] for g in range(SWA_KV_HEADS)]
            qhs = [q[:, h * HEAD_DIM:(h + 1) * HEAD_DIM] for h in range(SWA_HEADS)]
            dohs = [do[:, h * HEAD_DIM:(h + 1) * HEAD_DIM] for h in range(SWA_HEADS)]
            raw = [_dot_nt(qhs[h], kgs[h // SWA_GROUP]) for h in range(SWA_HEADS)]
            dps = [_dot_nt(dohs[h], vgs[h // SWA_GROUP]) for h in range(SWA_HEADS)]
            ps, dss = [], []
            for h in range(SWA_HEADS):
                lse_h = lse_all[:, h:h + 1]
                dlt = dl[:, FOX_HEADS + h:FOX_HEADS + h + 1]
                sc = jnp.where(mask, raw[h] + bias_ref[h], NEG)
                p = jnp.exp(sc - lse_h)
                ds = p * (dps[h] - dlt)
                dbias_ref[h] += ds
                p_sink = jnp.exp(sink_ref[0, h] - lse_h)
                gsk = gsk + jnp.where(lane == h, -jnp.sum(p_sink * dlt), 0.0)
                ps.append(p.astype(BF16))
                dss.append(ds.astype(BF16))
            dqs = [_dot(dss[h], kgs[h // SWA_GROUP]) * QK_SCALE for h in range(SWA_HEADS)]
            for h in range(SWA_HEADS):
                g = h // SWA_GROUP
                dk_h = _dot_tn(dss[h], qhs[h])
                dv_h = _dot_tn(ps[h], dohs[h])
                dks[g] = dk_h if dks[g] is None else dks[g] + dk_h
                dvs[g] = dv_h if dvs[g] is None else dvs[g] + dv_h
            dq_ref[...] = jnp.concatenate(dqs, axis=1).astype(BF16)
            sk_ref[...] += jnp.broadcast_to(gsk, sk_ref.shape)
            dk2 = jnp.concatenate(dks, axis=1)
            dv2 = jnp.concatenate(dvs, axis=1)
            dk_ref[...] = (ck_ref[...] + dk2[:BLOCK]).astype(BF16)
            dv_ref[...] = (cv_ref[...] + dv2[:BLOCK]).astype(BF16)
            ck_ref[...] = dk2[BLOCK:]
            cv_ref[...] = dv2[BLOCK:]

        @pl.when(n == nb)
        def _():
            dk_ref[...] = ck_ref[...].astype(BF16)
            dv_ref[...] = cv_ref[...].astype(BF16)
            gsk_ref[...] = sk_ref[...]
            bk = bk_ref[...]
            rowi = lax.broadcasted_iota(jnp.int32, (NUM_BUCKETS, LANES), 0)
            lanei = lax.broadcasted_iota(jnp.int32, (NUM_BUCKETS, LANES), 1)
            out = jnp.zeros((NUM_BUCKETS, LANES), F32)
            for h in range(SWA_HEADS):
                db = dbias_ref[h]
                for b in range(NUM_BUCKETS):
                    val = jnp.sum(jnp.where(bk == b, db, 0.0))
                    out = jnp.where((rowi == b) & (lanei == h), val, out)
            grb_ref[...] = out

    cq, ck, cv = COL_SQ // SWA_W, COL_SK // LANES, COL_SV // LANES
    cur = lambda n: jnp.minimum(n, nb - 1)
    prev = lambda n: jnp.maximum(jnp.minimum(n, nb - 1) - 1, 0)
    kout = lambda n: jnp.maximum(n - 1, 0)
    return pl.pallas_call(
        kern, name="swa_bwd",
        grid=(nb + 1,),
        in_specs=[pl.BlockSpec((BLOCK, SWA_W), lambda n: (cur(n), cq)),
                  pl.BlockSpec((BLOCK, LANES), lambda n: (prev(n), ck)),
                  pl.BlockSpec((BLOCK, LANES), lambda n: (cur(n), ck)),
                  pl.BlockSpec((BLOCK, LANES), lambda n: (prev(n), cv)),
                  pl.BlockSpec((BLOCK, LANES), lambda n: (cur(n), cv)),
                  pl.BlockSpec((BLOCK, SWA_W), lambda n: (cur(n), 1)),
                  pl.BlockSpec((BLOCK, LANES), lambda n: (cur(n), 0)),
                  pl.BlockSpec((BLOCK, LANES), lambda n: (cur(n), 0)),
                  pl.BlockSpec((SWA_HEADS, BLOCK, 2 * BLOCK), lambda n: (0, 0, 0)),
                  pl.BlockSpec(memory_space=pltpu.SMEM),
                  pl.BlockSpec((BLOCK, 2 * BLOCK), lambda n: (0, 0))],
        out_specs=[pl.BlockSpec((BLOCK, SWA_W), lambda n: (cur(n), 0)),
                   pl.BlockSpec((BLOCK, LANES), lambda n: (kout(n), 0)),
                   pl.BlockSpec((BLOCK, LANES), lambda n: (kout(n), 0)),
                   pl.BlockSpec((NUM_BUCKETS, LANES), lambda n: (0, 0)),
                   pl.BlockSpec((8, LANES), lambda n: (0, 0))],
        out_shape=[jax.ShapeDtypeStruct((s, SWA_W), BF16),
                   jax.ShapeDtypeStruct((s, LANES), BF16),
                   jax.ShapeDtypeStruct((s, LANES), BF16),
                   jax.ShapeDtypeStruct((NUM_BUCKETS, LANES), F32),
                   jax.ShapeDtypeStruct((8, LANES), F32)],
        scratch_shapes=[pltpu.VMEM((SWA_HEADS, BLOCK, 2 * BLOCK), F32),
                        pltpu.VMEM((BLOCK, LANES), F32),
                        pltpu.VMEM((BLOCK, LANES), F32),
                        pltpu.VMEM((8, LANES), F32)],
        compiler_params=_cparams(("arbitrary",)),
    )(qkv, qkv, qkv, qkv, qkv, do_bf, delta, lse, bias, sink, bucket)


def _post(x, target, o_fox, o_swa, z, w_o, ln_g, ln_b):
    s = x.shape[0]
    tm = min(256, s)
    nt = s // tm

    def kern(x_ref, t_ref, of_ref, os_ref, z_ref, w_ref, g_ref, b_ref,
             loss_ref, dh_ref, gwo_ref, do_ref, dz_ref, dl_ref, gg_ref, gb_ref, lacc_ref):
        step = pl.program_id(0)

        @pl.when(step == 0)
        def _():
            lacc_ref[...] = jnp.zeros_like(lacc_ref)
            gg_ref[...] = jnp.zeros_like(gg_ref)
            gwo_ref[...] = jnp.zeros_like(gwo_ref)
            gb_ref[...] = jnp.zeros_like(gb_ref)

        o = jnp.concatenate([of_ref[...], os_ref[...]], axis=1)
        zz = z_ref[...]
        sig = 1.0 / (1.0 + jnp.exp(-zz))
        silu = zz * sig
        mixed32 = o * silu
        mixed = mixed32.astype(BF16)
        w = w_ref[...]
        h = ALPHA * x_ref[...] + _dot(mixed, w)
        mu = jnp.mean(h, axis=1, keepdims=True)
        hc = h - mu
        var = jnp.mean(hc * hc, axis=1, keepdims=True)
        rstd = lax.rsqrt(var + LN_EPS)
        xhat = hc * rstd
        g = g_ref[...]
        err = xhat * g + b_ref[...] - t_ref[...]
        lacc_ref[...] += jnp.broadcast_to(jnp.sum(err * err, axis=0, keepdims=True), lacc_ref.shape)
        dout = err * (1.0 / D_MODEL)
        gg_ref[...] += jnp.broadcast_to(jnp.sum(dout * xhat, axis=0, keepdims=True), gg_ref.shape)
        gb_ref[...] += jnp.broadcast_to(jnp.sum(dout, axis=0, keepdims=True), gb_ref.shape)
        dxh = dout * g
        m1 = jnp.mean(dxh, axis=1, keepdims=True)
        m2 = jnp.mean(dxh * xhat, axis=1, keepdims=True)
        dh = rstd * (dxh - m1 - xhat * m2)
        dh_ref[...] = dh
        dy = dh.astype(BF16)
        gwo_ref[...] += _dot(mixed32.T.astype(BF16), dy)
        dmix = _dot_nt(dy, w)
        do = dmix * silu
        do_ref[...] = do.astype(BF16)
        dz_ref[...] = (dmix * o * (sig * (1.0 + zz * (1.0 - sig)))).astype(BF16)
        r = lax.broadcasted_iota(jnp.int32, (D_MODEL, LANES), 0) // HEAD_DIM
        c = lax.broadcasted_iota(jnp.int32, (D_MODEL, LANES), 1)
        pick = jnp.where(r == c, 1.0, 0.0).astype(BF16)
        dl_ref[...] = _exact_dot(pick, do * o, False)

        @pl.when(step == nt - 1)
        def _():
            tot = jnp.sum(lacc_ref[0:1, :]) * (0.5 / D_MODEL)
            loss_ref[...] = jnp.broadcast_to(tot, loss_ref.shape)

    row = lambda i: (i, 0)
    fixed = lambda i: (0, 0)
    wide = pl.BlockSpec((tm, D_MODEL), row)
    half = pl.BlockSpec((tm, FOX_W), row)
    return pl.pallas_call(
        kern, name="post",
        grid=(nt,),
        in_specs=[wide, wide, half, half, wide,
                  pl.BlockSpec((D_MODEL, D_MODEL), fixed),
                  pl.BlockSpec((1, D_MODEL), fixed),
                  pl.BlockSpec((1, D_MODEL), fixed)],
        out_specs=[pl.BlockSpec((8, LANES), fixed), wide,
                   _resident((D_MODEL, D_MODEL), fixed), wide, wide,
                   pl.BlockSpec((tm, LANES), row),
                   pl.BlockSpec((8, D_MODEL), fixed), pl.BlockSpec((8, D_MODEL), fixed)],
        out_shape=[jax.ShapeDtypeStruct((8, LANES), F32),
                   jax.ShapeDtypeStruct((s, D_MODEL), F32),
                   jax.ShapeDtypeStruct((D_MODEL, D_MODEL), F32),
                   jax.ShapeDtypeStruct((s, D_MODEL), BF16),
                   jax.ShapeDtypeStruct((s, D_MODEL), BF16),
                   jax.ShapeDtypeStruct((s, LANES), F32),
                   jax.ShapeDtypeStruct((8, D_MODEL), F32),
                   jax.ShapeDtypeStruct((8, D_MODEL), F32)],
        scratch_shapes=[pltpu.VMEM((8, D_MODEL), F32)],
        compiler_params=_cparams(("arbitrary",)),
    )(x, target, o_fox, o_swa, z, w_o, ln_g, ln_b)


def _adamw_math(w, g, m, v):
    m = ADAM_B1 * m + (1.0 - ADAM_B1) * g
    v = ADAM_B2 * v + (1.0 - ADAM_B2) * (g * g)
    m_hat = m / (1.0 - ADAM_B1 ** ADAM_STEP)
    v_hat = v / (1.0 - ADAM_B2 ** ADAM_STEP)
    delta = -ADAM_LR * (m_hat / (jnp.sqrt(v_hat) + ADAM_EPS) + ADAM_WD * w)
    return delta, m, v


def _adamw(w, g, m, v, *, name):
    r, c = w.shape
    tr = min(256, r)

    def kern(w_ref, g_ref, m_ref, v_ref, d_ref, mo_ref, vo_ref):
        d, mn, vn = _adamw_math(w_ref[...], g_ref[...], m_ref[...], v_ref[...])
        d_ref[...] = d
        mo_ref[...] = mn
        vo_ref[...] = vn

    blk = pl.BlockSpec((tr, c), lambda i: (i, 0))
    sds = jax.ShapeDtypeStruct((r, c), F32)
    return pl.pallas_call(
        kern, name=name,
        grid=(r // tr,),
        in_specs=[blk, blk, blk, blk],
        out_specs=[blk, blk, blk],
        out_shape=[sds, sds, sds],
        compiler_params=_cparams(("parallel",)),
    )(w, g, m, v)


def _adamw_cols(w, g, m, v, *, name):
    c, _, r = w.shape
    tc = 139
    assert c % tc == 0

    def kern(w_ref, g_ref, m_ref, v_ref, d_ref, mo_ref, vo_ref):
        d, mn, vn = _adamw_math(w_ref[...], g_ref[...], m_ref[...], v_ref[...])
        d_ref[...] = d
        mo_ref[...] = mn
        vo_ref[...] = vn

    blk = pl.BlockSpec((tc, 1, r), lambda i: (i, 0, 0))
    sds = jax.ShapeDtypeStruct((c, 1, r), F32)
    return pl.pallas_call(
        kern, name=name,
        grid=(c // tc,),
        in_specs=[blk, blk, blk, blk],
        out_specs=[blk, blk, blk],
        out_shape=[sds, sds, sds],
        compiler_params=_cparams(("parallel",)),
    )(w, g, m, v)


def _position():
    x, y, c = lax.axis_index("x"), lax.axis_index("y"), lax.axis_index("c")
    chips = [(1 - x, y), (x, 1 - y), (1 - x, 1 - y)]
    return x, y, c, chips


def _chip_index(cx, cy):
    return 2 * cx + cy


def _gather_weights(w_in_bf, w_o_bf):
    shards = (w_in_bf, w_o_bf)
    n_arr = len(shards)

    def kern(*refs):
        ins, outs = refs[:n_arr], refs[n_arr:2 * n_arr]
        send_sems, recv_sems, local_sems = refs[2 * n_arr:]
        x, y, c, chips = _position()
        me = _chip_index(x, y)
        sibling = (x, y, 1 - c)

        local = [pltpu.make_async_copy(ins[a], outs[a].at[me], local_sems.at[a]) for a in range(n_arr)]
        for cp in local:
            cp.start()

        def half(ref, a):
            rows = shards[a].shape[0] // 2
            return ref.at[pl.ds(c * rows, rows), :]

        def copy(a, k, src, slot, to):
            return pltpu.make_async_remote_copy(
                src_ref=src, dst_ref=half(outs[a].at[slot], a),
                send_sem=send_sems.at[a * 6 + k], recv_sem=recv_sems.at[a * 6 + k],
                device_id=to, device_id_type=MESH)

        first = [copy(a, j, half(ins[a], a), me, (*chip, c)) for a in range(n_arr) for j, chip in enumerate(chips)]
        for cp in first:
            cp.start()
        passed = []
        for a in range(n_arr):
            for j, chip in enumerate(chips):
                slot = _chip_index(*chip)
                copy(a, j, half(ins[a], a), slot, (*chip, c)).wait_recv()
                fwd = copy(a, 3 + j, half(outs[a].at[slot], a), slot, sibling)
                fwd.start()
                passed.append(fwd)
        for a in range(n_arr):
            for j, chip in enumerate(chips):
                slot = _chip_index(*chip)
                rows = shards[a].shape[0] // 2
                dst = outs[a].at[slot].at[pl.ds((1 - c) * rows, rows), :]
                pltpu.make_async_remote_copy(
                    src_ref=dst, dst_ref=dst, send_sem=send_sems.at[a * 6 + 3 + j],
                    recv_sem=recv_sems.at[a * 6 + 3 + j], device_id=sibling, device_id_type=MESH).wait_recv()
        for cp in first + passed:
            cp.wait_send()
        for cp in local:
            cp.wait()

    vmem = pl.BlockSpec(memory_space=pltpu.VMEM)
    return pl.pallas_call(
        kern, name="gather_weights",
        in_specs=[vmem] * n_arr,
        out_specs=[vmem] * n_arr,
        out_shape=[jax.ShapeDtypeStruct((N_CHIPS,) + w.shape, w.dtype) for w in shards],
        scratch_shapes=[pltpu.SemaphoreType.DMA((6 * n_arr,)),
                        pltpu.SemaphoreType.DMA((6 * n_arr,)),
                        pltpu.SemaphoreType.DMA((n_arr,))],
        compiler_params=_cparams(),
    )(*shards)


def _swap_halves(grads):
    n_arr = len(grads)

    def kern(*refs):
        ins = refs[:n_arr]
        owns = refs[n_arr:2 * n_arr]
        gots = refs[2 * n_arr:3 * n_arr]
        send_sems, recv_sems, local_sems = refs[3 * n_arr:]
        x, y, c, _ = _position()
        sibling = (x, y, 1 - c)
        local, remote = [], []
        for a in range(n_arr):
            rows = grads[a].shape[1] // 2
            piece = rows // COPY_PIECES
            for j in range(N_CHIPS):
                for r in range(COPY_PIECES):
                    k = (a * N_CHIPS + j) * COPY_PIECES + r
                    dst_rows = pl.ds(r * piece, piece)
                    local.append(pltpu.make_async_copy(
                        ins[a].at[j, pl.ds(c * rows + r * piece, piece), :],
                        owns[a].at[j, dst_rows, :], local_sems.at[k]))
                    remote.append(pltpu.make_async_remote_copy(
                        src_ref=ins[a].at[j, pl.ds((1 - c) * rows + r * piece, piece), :],
                        dst_ref=gots[a].at[j, dst_rows, :], send_sem=send_sems.at[k], recv_sem=recv_sems.at[k],
                        device_id=sibling, device_id_type=MESH))
        for cp in local + remote:
            cp.start()
        for cp in remote:
            cp.wait()
        for cp in local:
            cp.wait()

    hbm = pl.BlockSpec(memory_space=pltpu.VMEM)
    half = [jax.ShapeDtypeStruct((N_CHIPS, g.shape[1] // 2, g.shape[2]), F32) for g in grads]
    outs = pl.pallas_call(
        kern, name="swap_halves",
        in_specs=[hbm] * n_arr,
        out_specs=[hbm] * (2 * n_arr),
        out_shape=half + half,
        scratch_shapes=[pltpu.SemaphoreType.DMA((n_arr * N_CHIPS * COPY_PIECES,)),
                        pltpu.SemaphoreType.DMA((n_arr * N_CHIPS * COPY_PIECES,)),
                        pltpu.SemaphoreType.DMA((n_arr * N_CHIPS * COPY_PIECES,))],
        compiler_params=_cparams(),
    )(*grads)
    return outs[:n_arr], outs[n_arr:]


def _scatter_to_owners(parts):
    n_arr = len(parts)

    def kern(*refs):
        ins = refs[:n_arr]
        outs = refs[n_arr:2 * n_arr]
        send_sems, recv_sems, local_sems = refs[2 * n_arr:]
        x, y, c, chips = _position()
        me = _chip_index(x, y)
        local = [pltpu.make_async_copy(ins[a].at[me], outs[a].at[me], local_sems.at[a]) for a in range(n_arr)]
        for cp in local:
            cp.start()
        sends = []
        for a in range(n_arr):
            for j, chip in enumerate(chips):
                sends.append(pltpu.make_async_remote_copy(
                    src_ref=ins[a].at[_chip_index(*chip)], dst_ref=outs[a].at[me],
                    send_sem=send_sems.at[a * 3 + j], recv_sem=recv_sems.at[a * 3 + j],
                    device_id=(*chip, c), device_id_type=MESH))
        for cp in sends:
            cp.start()
        for a in range(n_arr):
            for j, chip in enumerate(chips):
                slot = outs[a].at[_chip_index(*chip)]
                pltpu.make_async_remote_copy(
                    src_ref=slot, dst_ref=slot, send_sem=send_sems.at[a * 3 + j],
                    recv_sem=recv_sems.at[a * 3 + j], device_id=(*chip, c), device_id_type=MESH).wait_recv()
        for cp in sends:
            cp.wait_send()
        for cp in local:
            cp.wait()

    hbm = pl.BlockSpec(memory_space=pltpu.VMEM)
    return pl.pallas_call(
        kern, name="scatter_to_owners",
        in_specs=[hbm] * n_arr,
        out_specs=[hbm] * n_arr,
        out_shape=[jax.ShapeDtypeStruct(p.shape, p.dtype) for p in parts],
        scratch_shapes=[pltpu.SemaphoreType.DMA((3 * n_arr,)),
                        pltpu.SemaphoreType.DMA((3 * n_arr,)),
                        pltpu.SemaphoreType.DMA((n_arr,))],
        compiler_params=_cparams(),
    )(*parts)


def _join_halves(halves):
    n_arr = len(halves)

    def kern(*refs):
        ins = refs[:n_arr]
        outs = refs[n_arr:2 * n_arr]
        send_sems, recv_sems, local_sems = refs[2 * n_arr:]
        x, y, c, _ = _position()
        sibling = (x, y, 1 - c)
        local, remote = [], []
        for a in range(n_arr):
            rows = halves[a].shape[0]
            piece = rows // COPY_PIECES
            for r in range(COPY_PIECES):
                k = a * COPY_PIECES + r
                src = ins[a].at[pl.ds(r * piece, piece), :]
                dst = outs[a].at[pl.ds(c * rows + r * piece, piece), :]
                local.append(pltpu.make_async_copy(src, dst, local_sems.at[k]))
                remote.append(pltpu.make_async_remote_copy(
                    src_ref=src, dst_ref=dst, send_sem=send_sems.at[k], recv_sem=recv_sems.at[k],
                    device_id=sibling, device_id_type=MESH))
        for cp in local + remote:
            cp.start()
        for a in range(n_arr):
            rows = halves[a].shape[0]
            piece = rows // COPY_PIECES
            for r in range(COPY_PIECES):
                k = a * COPY_PIECES + r
                theirs = outs[a].at[pl.ds((1 - c) * rows + r * piece, piece), :]
                pltpu.make_async_remote_copy(
                    src_ref=theirs, dst_ref=theirs, send_sem=send_sems.at[k], recv_sem=recv_sems.at[k],
                    device_id=sibling, device_id_type=MESH).wait_recv()
        for cp in remote:
            cp.wait_send()
        for cp in local:
            cp.wait()

    hbm = pl.BlockSpec(memory_space=pltpu.VMEM)
    return pl.pallas_call(
        kern, name="join_halves",
        in_specs=[hbm] * n_arr,
        out_specs=[hbm] * n_arr,
        out_shape=[jax.ShapeDtypeStruct((2 * h.shape[0], h.shape[1]), F32) for h in halves],
        scratch_shapes=[pltpu.SemaphoreType.DMA((n_arr * COPY_PIECES,)),
                        pltpu.SemaphoreType.DMA((n_arr * COPY_PIECES,)),
                        pltpu.SemaphoreType.DMA((n_arr * COPY_PIECES,))],
        compiler_params=_cparams(),
    )(*halves)


def _add2(a, b, *, name):
    n, r, c = a.shape
    tr = min(256, r)

    def kern(a_ref, b_ref, o_ref):
        o_ref[...] = (a_ref[...] + b_ref[...]).astype(BF16)

    blk = pl.BlockSpec((1, tr, c), lambda j, i: (j, i, 0))
    return pl.pallas_call(
        kern, name=name,
        grid=(n, r // tr),
        in_specs=[blk, blk],
        out_specs=blk,
        out_shape=jax.ShapeDtypeStruct(a.shape, BF16),
        compiler_params=_cparams(("parallel", "parallel")),
    )(a, b)


def _sum4(a, *, name):
    n, r, c = a.shape
    tr = min(256, r)

    def kern(a_ref, o_ref):
        f = lambda j: a_ref[j].astype(F32)
        o_ref[...] = ((f(0) + f(1)) + f(2)) + f(3)

    return pl.pallas_call(
        kern, name=name,
        grid=(r // tr,),
        in_specs=[pl.BlockSpec((n, tr, c), lambda i: (0, i, 0))],
        out_specs=pl.BlockSpec((tr, c), lambda i: (i, 0)),
        out_shape=jax.ShapeDtypeStruct((r, c), F32),
        compiler_params=_cparams(("parallel",)),
    )(a)


def _small_allreduce_adamw(g, w, m, v):
    def kern(g_ref, w_ref, m_ref, v_ref, gs_ref, d_ref, mo_ref, vo_ref, buf_ref, send_sems, recv_sems):
        x, y, c, _ = _position()
        me = 4 * x + 2 * y + c
        buf_ref[me] = g_ref[...]
        peers = [(x, y, 1 - c)] + [(px, py, pc) for px, py in _position()[3] for pc in (c, 1 - c)]
        sends = []
        for k, peer in enumerate(peers):
            sends.append(pltpu.make_async_remote_copy(
                src_ref=g_ref, dst_ref=buf_ref.at[me], send_sem=send_sems.at[k], recv_sem=recv_sems.at[k],
                device_id=peer, device_id_type=MESH))
        for cp in sends:
            cp.start()
        for k, (px, py, pc) in enumerate(peers):
            slot = buf_ref.at[4 * px + 2 * py + pc]
            pltpu.make_async_remote_copy(
                src_ref=slot, dst_ref=slot, send_sem=send_sems.at[k], recv_sem=recv_sems.at[k],
                device_id=(px, py, pc), device_id_type=MESH).wait_recv()
        for cp in sends:
            cp.wait_send()
        tot = buf_ref[0]
        for d in range(1, N_DEV):
            tot = tot + buf_ref[d]
        gs_ref[...] = tot
        delta, mn, vn = _adamw_math(w_ref[...], tot, m_ref[...], v_ref[...])
        d_ref[...] = delta
        mo_ref[...] = mn
        vo_ref[...] = vn

    vm = pl.BlockSpec(memory_space=pltpu.VMEM)
    sds = jax.ShapeDtypeStruct((SMALL_ROWS, LANES), F32)
    return pl.pallas_call(
        kern, name="small_allreduce_adamw",
        in_specs=[vm] * 4,
        out_specs=[vm] * 4,
        out_shape=[sds] * 4,
        scratch_shapes=[pltpu.VMEM((N_DEV, SMALL_ROWS, LANES), F32),
                        pltpu.SemaphoreType.DMA((N_DEV - 1,)),
                        pltpu.SemaphoreType.DMA((N_DEV - 1,))],
    )(g, w, m, v)


def _to_padded_cols(w):
    pad = jnp.zeros((w.shape[0], N_C - FOX_HEADS), w.dtype)
    return jnp.concatenate([w[:, 0:1536], w[:, 2056:2824], w[:, 1536:1544], pad,
                            w[:, 1544:2056], w[:, 2824:3336]], axis=1)


def _from_padded_cols(g):
    return jnp.concatenate([g[:, 0:1536], g[:, OFF_C:OFF_C + FOX_HEADS], g[:, OFF_B:OFF_B + FOX_W],
                            g[:, 1536:N_A], g[:, OFF_B + FOX_W:N_PAD]], axis=1)


def _pack_small(b_f, rel_bias, sink, ln_g, ln_b):
    row = lambda v: jnp.pad(v.reshape(1, -1), ((0, 0), (0, LANES - v.size)))
    return jnp.concatenate([ln_g.reshape(8, LANES), ln_b.reshape(8, LANES), rel_bias.reshape(2, LANES),
                            row(b_f), row(sink), jnp.zeros((4, LANES), F32)], axis=0)


def _unpack_small(p):
    ln_g = p[0:8].reshape(1, D_MODEL)
    ln_b = p[8:16].reshape(1, D_MODEL)
    rel_bias = p[16:18].reshape(NUM_BUCKETS, SWA_HEADS)
    b_f = p[18:19, :FOX_HEADS]
    sink = p[19:20, :SWA_HEADS]
    return b_f, rel_bias, sink, ln_g, ln_b


def _fox_rows(a):
    return a[:, :FOX_HEADS].T.reshape(FOX_HEADS, 1, a.shape[0])


def kernel(x, w_in, b_f, rel_bias, sink, w_o, ln_g, ln_b, loss_target, m_w_in, m_b_f, m_rel_bias, m_sink, m_w_o, m_ln_g, m_ln_b, v_w_in, v_b_f, v_rel_bias, v_sink, v_w_o, v_ln_g, v_ln_b):
    x2 = x[0]
    tgt = loss_target[0]
    s = x2.shape[0]
    w_in2, w_o2 = w_in[0], w_o[0]

    shard_cols = D_IN // N_CHIPS
    col_pad = ((0, 0), (0, SHARD_PAD - shard_cols))
    w_in_all, w_o_all = _gather_weights(jnp.pad(w_in2.astype(BF16), col_pad), w_o2.astype(BF16))
    w_full = jnp.concatenate([w_in_all[j, :, :shard_cols] for j in range(N_CHIPS)], axis=1)
    w_pad = _to_padded_cols(w_full)
    w_o_full = w_o_all.reshape(D_MODEL, D_MODEL)

    qkv, ffp, z, xt, vt = _project(x2, w_pad)
    bfp = jnp.pad(b_f, ((0, 0), (0, LANES - FOX_HEADS)))
    cum = _cum_fwd(ffp, bfp)
    cum_t3 = _fox_rows(cum)
    o_fox, lse_t3 = _fox_fwd(qkv, vt, cum_t3, cum)
    bucket = jnp.asarray(_bucket_table())
    bias = _swa_bias(rel_bias, bucket)
    o_swa, lse_swa = _swa_fwd(qkv, bias, sink)

    loss8, dh, grad_w_o_full, do_bf, dz, delta, gg8, gb8 = _post(
        x2, tgt, o_fox, o_swa, z, w_o_full, ln_g, ln_b)

    delta_t3 = _fox_rows(delta)
    dqt_fox, dk_fox, dv_fox, dcum_k, dcum_q = _fox_bwd(qkv, do_bf, cum_t3, cum, lse_t3, delta_t3)
    dcum_q = jnp.pad(dcum_q.reshape(FOX_HEADS, s).T, ((0, 0), (0, LANES - FOX_HEADS)))
    dff, gbf8 = _cum_bwd(dcum_k, dcum_q, ffp, bfp)
    dq_swa, dk_swa, dv_swa, grb, gsk8 = _swa_bwd(qkv, do_bf, delta, lse_swa, bias, sink, bucket)

    dq_fox = dqt_fox.T.astype(BF16)
    d_misc = jnp.concatenate([dk_swa, dv_swa, dff], axis=1)
    pieces = [dq_fox, dk_fox, dv_fox, dq_swa, d_misc, dz]
    grad_x = _grad_x_matmul(pieces, w_pad, dh, tm=512, tn=D_MODEL, name="grad_x")
    blocks = [(p, 0) for p in pieces[:-1]] + [(dz, 0), (dz, 1)]
    grad_w_pad = _grad_w_matmul(xt, blocks, tk=1024, name="grad_w_in")
    grad_w_in_full = _from_padded_cols(grad_w_pad)

    g_in4 = jnp.stack([jnp.pad(grad_w_in_full[:, j * shard_cols:(j + 1) * shard_cols], col_pad)
                       for j in range(N_CHIPS)])
    g_o4 = grad_w_o_full.reshape(N_CHIPS, D_MODEL // N_CHIPS, D_MODEL)
    owns, gots = _swap_halves([g_in4, g_o4])
    parts = [_add2(owns[0], gots[0], name="pair_sum_w_in"), _add2(owns[1], gots[1], name="pair_sum_w_o")]
    slabs = _scatter_to_owners(parts)
    halves = [_sum4(slabs[0], name="chip_sum_w_in"), _sum4(slabs[1], name="chip_sum_w_o")]
    g_w_in, g_w_o = _join_halves(halves)
    g_w_in = g_w_in[:, :shard_cols]

    cols_first = lambda a: jnp.transpose(a, (2, 0, 1))
    rows_first = lambda a: jnp.transpose(a, (1, 2, 0))
    d_w_in, nm_w_in, nv_w_in = [rows_first(a) for a in _adamw_cols(
        cols_first(w_in), cols_first(g_w_in[None]), cols_first(m_w_in), cols_first(v_w_in), name="adamw_w_in")]
    d_w_o, nm_w_o, nv_w_o = _adamw(w_o2, g_w_o, m_w_o[0], v_w_o[0], name="adamw_w_o")

    g_small = _pack_small(gbf8[0:1, :FOX_HEADS], grb[:, :SWA_HEADS], gsk8[0:1, :SWA_HEADS], gg8[0:1], gb8[0:1])
    g_small = g_small.at[LOSS_ROW, 0].set(loss8[0, 0])
    w_small = _pack_small(b_f, rel_bias, sink, ln_g, ln_b)
    m_small = _pack_small(m_b_f, m_rel_bias, m_sink, m_ln_g, m_ln_b)
    v_small = _pack_small(v_b_f, v_rel_bias, v_sink, v_ln_g, v_ln_b)
    gs, ds, ms, vs = _small_allreduce_adamw(g_small, w_small, m_small, v_small)
    loss = gs[LOSS_ROW, 0]
    g_bf, g_rb, g_sk, g_lg, g_lb = _unpack_small(gs)
    d_bf, d_rb, d_sk, d_lg, d_lb = _unpack_small(ds)
    m_bf, m_rb, m_sk, m_lg, m_lb = _unpack_small(ms)
    v_bf, v_rb, v_sk, v_lg, v_lb = _unpack_small(vs)

    e = lambda a: a[None]
    return (loss, e(grad_x),
            e(g_w_in), g_bf, g_rb, g_sk, e(g_w_o), g_lg, g_lb,
            d_w_in, d_bf, d_rb, d_sk, e(d_w_o), d_lg, d_lb,
            nm_w_in, m_bf, m_rb, m_sk, e(nm_w_o), m_lg, m_lb,
            nv_w_in, v_bf, v_rb, v_sk, e(nv_w_o), v_lg, v_lb)
```

```python
import functools
import math

import numpy as np
import jax
import jax.numpy as jnp
from jax import lax
from jax.experimental import pallas as pl
from jax.experimental.pallas import tpu as pltpu

F32 = jnp.float32
BF16 = jnp.bfloat16

D_MODEL = 1024
HEAD_DIM = 64
FOX_HEADS = 8
SWA_HEADS = 8
SWA_KV_HEADS = 2
SWA_GROUP = 4
FOX_W = 512
SWA_W = 512
SWA_KV_W = 128
BLOCK = 128
NUM_BUCKETS = 32
MAX_DISTANCE = 128
LN_EPS = 1e-5
NEG = -1e30
ALPHA = 2.0 ** 0.25
QK_SCALE = 0.125

ADAM_LR = 0.001
ADAM_B1 = 0.9
ADAM_B2 = 0.999
ADAM_EPS = 1e-08
ADAM_WD = 0.01
ADAM_STEP = 10

D_IN = 3336
SHARD_PAD = 896
N_A = 2304
N_C = 256
N_B = 1024
OFF_C = N_A
OFF_B = N_A + N_C
N_PAD = N_A + N_C + N_B
COL_FK, COL_FV, COL_SQ, COL_SK, COL_SV = 512, 1024, 1536, 2048, 2176

LANES = 128
FOX_T = 256
FOX_REF = 512
SUM_ROWS = 16
VMEM_LIMIT = 56 * 1024 * 1024

MESH = pl.DeviceIdType.MESH
N_CHIPS = 4
N_DEV = 8
SMALL_ROWS = 24
LOSS_ROW = 20
COPY_PIECES = 4


def _cparams(sem=None):
    return pltpu.CompilerParams(dimension_semantics=sem, vmem_limit_bytes=VMEM_LIMIT)


def _split3(x):
    hi = x.astype(BF16)
    r = x - hi.astype(F32)
    mid = r.astype(BF16)
    lo = (r - mid.astype(F32)).astype(BF16)
    return hi, mid, lo


def _dot(a, b):
    return jnp.dot(a, b, preferred_element_type=F32)


def _dot_nt(a, b):
    return lax.dot_general(a, b, (((1,), (1,)), ((), ())), preferred_element_type=F32)


def _project(x, w_pad):
    s, k = x.shape
    tm = 512
    chunk = 512

    def kern(x_ref, w_ref, qkv_ref, ff_ref, z_ref, xt_ref, vt_ref):
        xf = x_ref[...]
        xb = xf.astype(BF16)
        xt_ref[...] = xf.T.astype(BF16)
        for c0 in range(0, N_A, chunk):
            width = min(chunk, N_A - c0)
            res = _dot(xb, w_ref[:, c0:c0 + width])
            qkv_ref[:, c0:c0 + width] = res.astype(BF16)
            if c0 == COL_FV:
                vt_ref[...] = res.T.astype(BF16)
        ff_ref[...] = _dot(xb, w_ref[:, OFF_C:OFF_C + N_C])
        for c0 in range(0, N_B, 512):
            z_ref[:, c0:c0 + 512] = _dot(xb, w_ref[:, OFF_B + c0:OFF_B + c0 + 512])

    row = lambda i: (i, 0)
    return pl.pallas_call(
        kern, name="project",
        grid=(s // tm,),
        in_specs=[pl.BlockSpec((tm, k), row),
                  _resident((k, N_PAD), lambda i: (0, 0))],
        out_specs=[pl.BlockSpec((tm, N_A), row),
                   pl.BlockSpec((tm, N_C), row),
                   pl.BlockSpec((tm, N_B), row),
                   pl.BlockSpec((k, tm), lambda i: (0, i)),
                   pl.BlockSpec((FOX_W, tm), lambda i: (0, i))],
        out_shape=[jax.ShapeDtypeStruct((s, N_A), BF16),
                   jax.ShapeDtypeStruct((s, N_C), F32),
                   jax.ShapeDtypeStruct((s, N_B), F32),
                   jax.ShapeDtypeStruct((k, s), BF16),
                   jax.ShapeDtypeStruct((FOX_W, s), BF16)],
        compiler_params=_cparams(("parallel",)),
    )(x, w_pad)


def _grad_x_matmul(pieces, w_pad, dh, *, tm, tn, name):
    m = dh.shape[0]
    n, k = w_pad.shape
    widths = [p.shape[1] for p in pieces]
    offs = [sum(widths[:i]) for i in range(len(pieces))]
    assert sum(widths) == k

    def kern(*refs):
        p_refs, (b_ref, dh_ref, o_ref) = refs[:len(pieces)], refs[len(pieces):]
        acc = ALPHA * dh_ref[...]
        for p_ref, off, width in zip(p_refs, offs, widths):
            acc = acc + _dot_nt(p_ref[...], b_ref[:, off:off + width])
        o_ref[...] = acc

    assert tn == n
    return pl.pallas_call(
        kern, name=name,
        grid=(m // tm,),
        in_specs=[pl.BlockSpec((tm, w), lambda i: (i, 0)) for w in widths]
        + [_resident((n, k), lambda i: (0, 0)),
           pl.BlockSpec((tm, n), lambda i: (i, 0))],
        out_specs=pl.BlockSpec((tm, n), lambda i: (i, 0)),
        out_shape=jax.ShapeDtypeStruct((m, n), F32),
        compiler_params=_cparams(("parallel",)),
    )(*pieces, w_pad, dh)


def _grad_w_matmul(xt, blocks, *, tk, name):
    m, s = xt.shape
    tn = 512
    nb = len(blocks)

    def kern(a_ref, *refs):
        b_refs, o_ref = refs[:nb], refs[nb]

        @pl.when(pl.program_id(0) == 0)
        def _():
            o_ref[...] = jnp.zeros_like(o_ref)
        a = a_ref[...]
        for blk in range(nb):
            o_ref[:, blk * tn:(blk + 1) * tn] += _dot(a, b_refs[blk][...])

    return pl.pallas_call(
        kern, name=name,
        grid=(s // tk,),
        in_specs=[pl.BlockSpec((m, tk), lambda k: (0, k))]
        + [pl.BlockSpec((tk, tn), functools.partial(lambda k, col: (k, col), col=col)) for _, col in blocks],
        out_specs=_resident((m, nb * tn), lambda k: (0, 0)),
        out_shape=jax.ShapeDtypeStruct((m, nb * tn), F32),
        compiler_params=_cparams(("arbitrary",)),
    )(xt, *[arr for arr, _ in blocks])


def _tri(n, lower):
    r = lax.broadcasted_iota(jnp.int32, (n, n), 0)
    c = lax.broadcasted_iota(jnp.int32, (n, n), 1)
    keep = (c <= r) if lower else (c >= r)
    return jnp.where(keep, 1.0, 0.0).astype(BF16)


def _exact_dot(mat_bf16, x_f32, left):
    out = None
    for piece in _split3(x_f32):
        t = _dot(mat_bf16, piece) if left else _dot(piece, mat_bf16)
        out = t if out is None else out + t
    return out


def _log_sigmoid(z):
    return jnp.minimum(z, 0.0) - jnp.log(1.0 + jnp.exp(-jnp.abs(z)))


def _cum_fwd(ffp, bfp):
    s = ffp.shape[0]
    t = min(512, s)

    def kern(ff_ref, b_ref, cum_ref, carry_ref):
        @pl.when(pl.program_id(0) == 0)
        def _():
            carry_ref[...] = jnp.zeros_like(carry_ref)
        lane = lax.broadcasted_iota(jnp.int32, (1, LANES), 1)
        lf = _log_sigmoid(ff_ref[...] + b_ref[...])
        lf = jnp.where(lane < FOX_HEADS, lf, 0.0)
        cum = _exact_dot(_tri(t, True), lf, True) + carry_ref[0:1, :]
        cum_ref[...] = cum
        carry_ref[...] = jnp.broadcast_to(cum[t - 1:t, :], carry_ref.shape)

    return pl.pallas_call(
        kern, name="cum_fwd",
        grid=(s // t,),
        in_specs=[pl.BlockSpec((t, LANES), lambda i: (i, 0)),
                  pl.BlockSpec((1, LANES), lambda i: (0, 0))],
        out_specs=pl.BlockSpec((t, LANES), lambda i: (i, 0)),
        out_shape=jax.ShapeDtypeStruct((s, LANES), F32),
        scratch_shapes=[pltpu.VMEM((8, LANES), F32)],
        compiler_params=_cparams(("arbitrary",)),
    )(ffp, bfp)


def _cum_bwd(dcum_k, dcum_q, ffp, bfp):
    s = dcum_k.shape[0]
    t = min(512, s)
    nb = s // t

    def kern(dck_ref, dcq_ref, ff_ref, b_ref, dff_ref, gb_ref, carry_ref):
        @pl.when(pl.program_id(0) == 0)
        def _():
            carry_ref[...] = jnp.zeros_like(carry_ref)
            gb_ref[...] = jnp.zeros_like(gb_ref)
        lane = lax.broadcasted_iota(jnp.int32, (1, LANES), 1)
        dlf = _exact_dot(_tri(t, False), dck_ref[...] + dcq_ref[...], True) + carry_ref[0:1, :]
        carry_ref[...] = jnp.broadcast_to(dlf[0:1, :], carry_ref.shape)
        z = ff_ref[...] + b_ref[...]
        dff = jnp.where(lane < FOX_HEADS, dlf / (1.0 + jnp.exp(z)), 0.0)
        gb_ref[...] += jnp.broadcast_to(jnp.sum(dff, axis=0, keepdims=True), gb_ref.shape)
        dff_ref[...] = jnp.concatenate([dff, jnp.zeros_like(dff)], axis=1).astype(BF16)

    return pl.pallas_call(
        kern, name="cum_bwd",
        grid=(nb,),
        in_specs=[pl.BlockSpec((t, LANES), lambda i: (nb - 1 - i, 0)),
                  pl.BlockSpec((t, LANES), lambda i: (nb - 1 - i, 0)),
                  pl.BlockSpec((t, LANES), lambda i: (nb - 1 - i, 0)),
                  pl.BlockSpec((1, LANES), lambda i: (0, 0))],
        out_specs=[pl.BlockSpec((t, N_C), lambda i: (nb - 1 - i, 0)),
                   pl.BlockSpec((8, LANES), lambda i: (0, 0))],
        out_shape=[jax.ShapeDtypeStruct((s, N_C), BF16),
                   jax.ShapeDtypeStruct((8, LANES), F32)],
        scratch_shapes=[pltpu.VMEM((8, LANES), F32)],
        compiler_params=_cparams(("arbitrary",)),
    )(dcum_k, dcum_q, ffp, bfp)


def _resident(shape, index_map):
    return pl.BlockSpec(shape, index_map, pipeline_mode=pl.Buffered(1))


def _fox_fwd(qkv, vt, cum_t3, cum):
    s = qkv.shape[0]
    tk = min(FOX_T, s)
    tq = FOX_REF
    nq = s // tq
    nh = FOX_HEADS
    diag_tiles = tq // tk

    def kern(q_ref, k_ref, vt_ref, ct_ref, c_ref, o_ref, lse_ref, m_ref, acc_ref, u_ref):
        i = pl.program_id(0)
        lane = lax.broadcasted_iota(jnp.int32, (1, LANES), 1)
        krow = lax.broadcasted_iota(jnp.int32, (tk, tq), 0)
        qcol = lax.broadcasted_iota(jnp.int32, (tk, tq), 1)
        q0 = pl.multiple_of(i * tq, tq)
        qts, crefs = [], []
        for h in range(nh):
            p, a = divmod(h, 2)
            q2 = q_ref[:, p * LANES:(p + 1) * LANES] * jnp.asarray(QK_SCALE, BF16)
            sel = (lane < HEAD_DIM) if a == 0 else (lane >= HEAD_DIM)
            qts.append(jnp.where(sel, q2, jnp.zeros_like(q2)).astype(F32).T.astype(BF16))
            crefs.append(ct_ref[h, :, pl.ds(q0, LANES)][:, 0:1])
        m_ref[...] = jnp.full(m_ref.shape, NEG, F32)
        acc_ref[...] = jnp.zeros_like(acc_ref)
        ones = jnp.ones((SUM_ROWS, tk), BF16)

        def tile(j, diag):
            k0 = pl.multiple_of(j * tk, tk)
            cb = c_ref[pl.ds(k0, tk), :]
            sts = [_dot(k_ref[pl.ds(k0, tk), (h // 2) * LANES:(h // 2 + 1) * LANES], qts[h]) for h in range(nh)]
            tile_max = []
            for h in range(nh):
                u = sts[h] - (cb[:, h:h + 1] - crefs[h])
                if diag is not None:
                    u = jnp.where(krow + diag * tk <= qcol, u, NEG)
                u_ref[h] = u
                tile_max.append(jnp.max(u, axis=0, keepdims=True))
            pts, scales = [], []
            for h in range(nh):
                m_old = m_ref[h]
                m_new = jnp.maximum(m_old, tile_max[h])
                scales.append(jnp.exp(m_old - m_new))
                pts.append(jnp.exp(u_ref[h] - m_new).astype(BF16))
                m_ref[h] = m_new
            for h in range(nh):
                vth = jnp.concatenate([vt_ref[h * HEAD_DIM:(h + 1) * HEAD_DIM, pl.ds(k0, tk)], ones], axis=0)
                acc_ref[h] = scales[h] * acc_ref[h] + _dot(vth, pts[h])

        def body(j, c):
            tile(j, None)
            return c
        lax.fori_loop(0, i * diag_tiles, body, 0)
        for d in range(diag_tiles):
            tile(i * diag_tiles + d, d)

        ls = [acc_ref[h][HEAD_DIM:HEAD_DIM + 1] for h in range(nh)]
        for p in range(nh // 2):
            ot = jnp.concatenate([acc_ref[2 * p + a][:HEAD_DIM] * (1.0 / ls[2 * p + a]) for a in range(2)], axis=0)
            o_ref[:, p * LANES:(p + 1) * LANES] = ot.T
        for h in range(nh):
            lse_ref[h, :, pl.ds(q0, tq)] = m_ref[h] + jnp.log(ls[h])

    return pl.pallas_call(
        kern, name="fox_fwd",
        grid=(nq,),
        in_specs=[pl.BlockSpec((tq, FOX_W), lambda i: (i, 0)),
                  _resident((s, FOX_W), lambda i: (0, COL_FK // FOX_W)),
                  _resident((FOX_W, s), lambda i: (0, 0)),
                  _resident((nh, 1, s), lambda i: (0, 0, 0)),
                  _resident((s, LANES), lambda i: (0, 0))],
        out_specs=[pl.BlockSpec((tq, FOX_W), lambda i: (i, 0)),
                   pl.BlockSpec((nh, 1, s), lambda i: (0, 0, 0))],
        out_shape=[jax.ShapeDtypeStruct((s, FOX_W), F32),
                   jax.ShapeDtypeStruct((nh, 1, s), F32)],
        scratch_shapes=[pltpu.VMEM((nh, 1, tq), F32),
                        pltpu.VMEM((nh, HEAD_DIM + SUM_ROWS, tq), F32),
                        pltpu.VMEM((nh, tk, tq), F32)],
        compiler_params=_cparams(("arbitrary",)),
    )(qkv, qkv, vt, cum_t3, cum)


def _fox_bwd(qkv, do_bf, cum_t3, cum, lse_t3, delta_t3):
    s = qkv.shape[0]
    t = min(FOX_T, s)
    nq = s // t
    nh = FOX_HEADS
    npair = nh // 2

    def kern(q_ref, do_ref, k_ref, v_ref, ct_ref, c_ref, lse_ref, dl_ref,
             dqt_ref, dk_ref, dv_ref, dc_ref, dcq_ref, accv_ref, acck_ref, accd_ref):
        kj = pl.program_id(0)
        lane = lax.broadcasted_iota(jnp.int32, (1, LANES), 1)
        krow = lax.broadcasted_iota(jnp.int32, (t, t), 0)
        qcol = lax.broadcasted_iota(jnp.int32, (t, t), 1)
        causal = krow <= qcol
        sels = [lane < HEAD_DIM, lane >= HEAD_DIM]

        @pl.when(kj == 0)
        def _():
            dqt_ref[...] = jnp.zeros_like(dqt_ref)
            dcq_ref[...] = jnp.zeros_like(dcq_ref)

        accv_ref[...] = jnp.zeros_like(accv_ref)
        acck_ref[...] = jnp.zeros_like(acck_ref)
        accd_ref[...] = jnp.zeros_like(accd_ref)
        cb = c_ref[...]
        k2s, v2s, kts = [], [], []
        for p in range(npair):
            k2 = k_ref[:, p * LANES:(p + 1) * LANES]
            k2s.append(k2)
            v2s.append(v_ref[:, p * LANES:(p + 1) * LANES])
            kt = k2.astype(F32).T * QK_SCALE
            kts.append(kt[:HEAD_DIM].astype(BF16))
            kts.append(kt[HEAD_DIM:].astype(BF16))
        css = [cb[:, h:h + 1] for h in range(nh)]

        def tile(i, masked):
            q0 = pl.multiple_of(i * t, t)
            r0 = pl.multiple_of((i // (FOX_REF // t)) * FOX_REF, FOX_REF)
            sts, dpts, qms, doms = [], [], [], []
            for h in range(nh):
                p, a = divmod(h, 2)
                qi = q_ref[pl.ds(q0, t), p * LANES:(p + 1) * LANES] * jnp.asarray(QK_SCALE, BF16)
                doi = do_ref[pl.ds(q0, t), p * LANES:(p + 1) * LANES]
                qm = jnp.where(sels[a], qi, jnp.zeros_like(qi))
                dom = jnp.where(sels[a], doi, jnp.zeros_like(doi))
                qms.append(qm)
                doms.append(dom)
                sts.append(_dot_nt(k2s[p], qm))
                dpts.append(_dot_nt(v2s[p], dom))
            pts, dsts = [], []
            for h in range(nh):
                cref = ct_ref[h, :, pl.ds(r0, LANES)][:, 0:1]
                pt = jnp.exp(sts[h] - (css[h] - cref) - lse_ref[h, :, pl.ds(q0, t)])
                if masked:
                    pt = jnp.where(causal, pt, 0.0)
                ds32 = pt * (dpts[h] - dl_ref[h, :, pl.ds(q0, t)])
                part = ds32[:, 0:LANES]
                for c in range(1, t // LANES):
                    part = part + ds32[:, c * LANES:(c + 1) * LANES]
                accd_ref[h] += part
                dcq_ref[h, :, pl.ds(q0, t)] += jnp.sum(ds32, axis=0, keepdims=True)
                pts.append(pt.astype(BF16))
                dsts.append(ds32.astype(BF16))
            for p in range(npair):
                ha, hb = 2 * p, 2 * p + 1
                accv_ref[p] += _dot(pts[ha], doms[ha]) + _dot(pts[hb], doms[hb])
                acck_ref[p] += _dot(dsts[ha], qms[ha]) + _dot(dsts[hb], qms[hb])
            for h in range(nh):
                dqt_ref[h * HEAD_DIM:(h + 1) * HEAD_DIM, pl.ds(q0, t)] += _dot(kts[h], dsts[h])

        tile(kj, True)

        def body(i, c):
            tile(i, False)
            return c
        lax.fori_loop(kj + 1, nq, body, 0)

        dc = jnp.zeros((t, LANES), F32)
        for h in range(nh):
            dc = jnp.where(lane == h, -jnp.sum(accd_ref[h], axis=1, keepdims=True), dc)
        dc_ref[...] = dc
        for p in range(npair):
            dv_ref[:, p * LANES:(p + 1) * LANES] = accv_ref[p].astype(BF16)
            dk_ref[:, p * LANES:(p + 1) * LANES] = acck_ref[p].astype(BF16)

    whole = lambda kj: (0, 0, 0)
    return pl.pallas_call(
        kern, name="fox_bwd",
        grid=(nq,),
        in_specs=[_resident((s, FOX_W), lambda kj: (0, 0)),
                  _resident((s, FOX_W), lambda kj: (0, 0)),
                  pl.BlockSpec((t, FOX_W), lambda kj: (kj, COL_FK // FOX_W)),
                  pl.BlockSpec((t, FOX_W), lambda kj: (kj, COL_FV // FOX_W)),
                  _resident((nh, 1, s), whole),
                  pl.BlockSpec((t, LANES), lambda kj: (kj, 0)),
                  _resident((nh, 1, s), whole),
                  _resident((nh, 1, s), whole)],
        out_specs=[_resident((FOX_W, s), lambda kj: (0, 0)),
                   pl.BlockSpec((t, FOX_W), lambda kj: (kj, 0)),
                   pl.BlockSpec((t, FOX_W), lambda kj: (kj, 0)),
                   pl.BlockSpec((t, LANES), lambda kj: (kj, 0)),
                   _resident((nh, 1, s), whole)],
        out_shape=[jax.ShapeDtypeStruct((FOX_W, s), F32),
                   jax.ShapeDtypeStruct((s, FOX_W), BF16),
                   jax.ShapeDtypeStruct((s, FOX_W), BF16),
                   jax.ShapeDtypeStruct((s, LANES), F32),
                   jax.ShapeDtypeStruct((nh, 1, s), F32)],
        scratch_shapes=[pltpu.VMEM((npair, t, LANES), F32),
                        pltpu.VMEM((npair, t, LANES), F32),
                        pltpu.VMEM((nh, t, LANES), F32)],
        compiler_params=_cparams(("arbitrary",)),
    )(qkv, do_bf, qkv, qkv, cum_t3, cum, lse_t3, delta_t3)


def _bucket_table():
    qi = np.arange(BLOCK)[:, None]
    kj = np.arange(2 * BLOCK)[None, :]
    rel = np.maximum(qi + BLOCK - kj, 0).astype(np.int32)
    max_exact = NUM_BUCKETS // 2
    relf = np.maximum(rel, 1).astype(np.float32)
    large = max_exact + (np.log(relf / np.float32(max_exact)) / np.float32(math.log(MAX_DISTANCE / max_exact))
                         * np.float32(NUM_BUCKETS - max_exact)).astype(np.int32)
    large = np.minimum(large, NUM_BUCKETS - 1)
    return np.where(rel < max_exact, rel, large).astype(np.int32)


SWA_LANES = SWA_GROUP * BLOCK


def _swa_bias(rel_bias, bucket_t):
    def kern(rb_ref, bk_ref, o_ref):
        bk = bk_ref[...]
        kj = lax.broadcasted_iota(jnp.int32, (2 * BLOCK, BLOCK), 0)
        qi = lax.broadcasted_iota(jnp.int32, (2 * BLOCK, BLOCK), 1)
        rel = qi + BLOCK - kj
        band = (rel >= 0) & (rel < BLOCK)
        masks = [band & (kj >= BLOCK), band]
        for h in range(SWA_HEADS):
            g, hh = divmod(h, SWA_GROUP)
            acc = jnp.zeros((2 * BLOCK, BLOCK), F32)
            for b in range(NUM_BUCKETS):
                acc = jnp.where(bk == b, rb_ref[b, h], acc)
            for first in range(2):
                o_ref[first, g, :, hh * BLOCK:(hh + 1) * BLOCK] = jnp.where(masks[first], acc, NEG)

    return pl.pallas_call(
        kern, name="swa_bias",
        in_specs=[pl.BlockSpec(memory_space=pltpu.SMEM),
                  pl.BlockSpec(memory_space=pltpu.VMEM)],
        out_specs=pl.BlockSpec(memory_space=pltpu.VMEM),
        out_shape=jax.ShapeDtypeStruct((2, SWA_KV_HEADS, 2 * BLOCK, SWA_LANES), F32),
        compiler_params=_cparams(),
    )(rel_bias, bucket_t)


def _swa_operands(q_ref, kp_ref, kc_ref, vp_ref, vc_ref):
    k = jnp.concatenate([kp_ref[...], kc_ref[...]], axis=0)
    v = jnp.concatenate([vp_ref[...], vc_ref[...]], axis=0)
    qt = (q_ref[...] * jnp.asarray(QK_SCALE, BF16)).astype(F32).T.astype(BF16)
    return k, v, _group_rows(qt)


def _group_rows(xt):
    zeros = jnp.zeros((HEAD_DIM, SWA_LANES), BF16)
    out = []
    for g in range(SWA_KV_HEADS):
        heads = [xt[(SWA_GROUP * g + hh) * HEAD_DIM:(SWA_GROUP * g + hh + 1) * HEAD_DIM, :] for hh in range(SWA_GROUP)]
        rows = jnp.concatenate(heads, axis=1)
        padded = jnp.concatenate([rows, zeros] if g == 0 else [zeros, rows], axis=0)
        out.append((rows, padded))
    return out


def _pairs_to_rows(cols_t):
    out = []
    for p in range(SWA_HEADS // 2):
        g, hh = divmod(2 * p, SWA_GROUP)
        pair = jnp.concatenate([cols_t[g][:, hh * BLOCK:(hh + 1) * BLOCK],
                                cols_t[g][:, (hh + 1) * BLOCK:(hh + 2) * BLOCK]], axis=0)
        out.append(pair.T)
    return jnp.concatenate(out, axis=1)


def _swa_fwd(qkv, bias_t, sink_rows):
    s = qkv.shape[0]
    nb = s // BLOCK

    def kern(q_ref, kp_ref, kc_ref, vp_ref, vc_ref, bias_ref, sink_ref, o_ref, lse_ref):
        n = pl.program_id(0)
        table = jnp.minimum(n, 1)
        k, v, qts = _swa_operands(q_ref, kp_ref, kc_ref, vp_ref, vc_ref)
        vt = v.astype(F32).T.astype(BF16)
        us = [_dot(k, qts[g][1]) + bias_ref[table, g] for g in range(SWA_KV_HEADS)]
        outs = []
        for g in range(SWA_KV_HEADS):
            sk = sink_ref[g]
            m = jnp.maximum(jnp.max(us[g], axis=0, keepdims=True), sk)
            p = jnp.exp(us[g] - m)
            l = jnp.sum(p, axis=0, keepdims=True) + jnp.exp(sk - m)
            lse_ref[0, g] = m + jnp.log(l)
            outs.append(_dot(vt[g * HEAD_DIM:(g + 1) * HEAD_DIM, :], (p * (1.0 / l)).astype(BF16)))
        o_ref[...] = _pairs_to_rows(outs)

    cq, ck, cv = COL_SQ // SWA_W, COL_SK // LANES, COL_SV // LANES
    prev = lambda n: jnp.maximum(n - 1, 0)
    return pl.pallas_call(
        kern, name="swa_fwd",
        grid=(nb,),
        in_specs=[pl.BlockSpec((BLOCK, SWA_W), lambda n: (n, cq)),
                  pl.BlockSpec((BLOCK, LANES), lambda n: (prev(n), ck)),
                  pl.BlockSpec((BLOCK, LANES), lambda n: (n, ck)),
                  pl.BlockSpec((BLOCK, LANES), lambda n: (prev(n), cv)),
                  pl.BlockSpec((BLOCK, LANES), lambda n: (n, cv)),
                  _resident((2, SWA_KV_HEADS, 2 * BLOCK, SWA_LANES), lambda n: (0, 0, 0, 0)),
                  _resident((SWA_KV_HEADS, 1, SWA_LANES), lambda n: (0, 0, 0))],
        out_specs=[pl.BlockSpec((BLOCK, SWA_W), lambda n: (n, 0)),
                   pl.BlockSpec((1, SWA_KV_HEADS, 1, SWA_LANES), lambda n: (n, 0, 0, 0))],
        out_shape=[jax.ShapeDtypeStruct((s, SWA_W), F32),
                   jax.ShapeDtypeStruct((nb, SWA_KV_HEADS, 1, SWA_LANES), F32)],
        compiler_params=_cparams(("parallel",)),
    )(qkv, qkv, qkv, qkv, qkv, bias_t, sink_rows)


def _swa_bwd(qkv, do_bf, delta_rows, lse, bias_t, sink_rows, bucket_t):
    s = qkv.shape[0]
    nb = s // BLOCK

    def kern(q_ref, kp_ref, kc_ref, vp_ref, vc_ref, do_ref, dl_ref, lse_ref, bias_ref, sink_ref, bk_ref,
             dq_ref, dk_ref, dv_ref, grb_ref, gsk_ref, dbias_ref, ck_ref, cv_ref, sk_ref):
        n = pl.program_id(0)

        @pl.when(n == 0)
        def _():
            dbias_ref[...] = jnp.zeros_like(dbias_ref)
            ck_ref[...] = jnp.zeros_like(ck_ref)
            cv_ref[...] = jnp.zeros_like(cv_ref)
            sk_ref[...] = jnp.zeros_like(sk_ref)

        @pl.when(n < nb)
        def _():
            table = jnp.minimum(n, 1)
            k, v, qts = _swa_operands(q_ref, kp_ref, kc_ref, vp_ref, vc_ref)
            dots = _group_rows(do_ref[...].astype(F32).T.astype(BF16))
            kt = (k.astype(F32).T * QK_SCALE).astype(BF16)
            groups = range(SWA_KV_HEADS)
            sts = [_dot(k, qts[g][1]) for g in groups]
            dps = [_dot(v, dots[g][1]) for g in groups]
            ps, dss = [], []
            for g in groups:
                lse_g = lse_ref[0, g]
                dlt = dl_ref[0, g]
                p = jnp.exp(sts[g] + bias_ref[table, g] - lse_g)
                ds = p * (dps[g] - dlt)
                dbias_ref[g] += ds
                sk_ref[g] += -jnp.exp(sink_ref[g] - lse_g) * dlt
                ps.append(p.astype(BF16))
                dss.append(ds.astype(BF16))
            dvt = jnp.concatenate([_dot_nt(dots[g][0], ps[g]) for g in groups], axis=0)
            dkt = jnp.concatenate([_dot_nt(qts[g][0], dss[g]) for g in groups], axis=0)
            dqts = [_dot(kt[g * HEAD_DIM:(g + 1) * HEAD_DIM, :], dss[g]) for g in groups]
            dq_ref[...] = _pairs_to_rows(dqts).astype(BF16)
            dk2 = dkt.T
            dv2 = dvt.T
            dk_ref[...] = (ck_ref[...] + dk2[:BLOCK]).astype(BF16)
            dv_ref[...] = (cv_ref[...] + dv2[:BLOCK]).astype(BF16)
            ck_ref[...] = dk2[BLOCK:]
            cv_ref[...] = dv2[BLOCK:]

        @pl.when(n == nb)
        def _():
            dk_ref[...] = ck_ref[...].astype(BF16)
            dv_ref[...] = cv_ref[...].astype(BF16)
            bk = bk_ref[...]
            lane = lax.broadcasted_iota(jnp.int32, (8, LANES), 1)
            rowi = lax.broadcasted_iota(jnp.int32, (NUM_BUCKETS, LANES), 0)
            lanei = lax.broadcasted_iota(jnp.int32, (NUM_BUCKETS, LANES), 1)
            out = jnp.zeros((NUM_BUCKETS, LANES), F32)
            gsk = jnp.zeros((8, LANES), F32)
            for h in range(SWA_HEADS):
                g, hh = divmod(h, SWA_GROUP)
                cols = slice(hh * BLOCK, (hh + 1) * BLOCK)
                gsk = jnp.where(lane == h, jnp.sum(sk_ref[g][:, cols]), gsk)
                db = dbias_ref[g][:, cols]
                for b in range(NUM_BUCKETS):
                    val = jnp.sum(jnp.where(bk == b, db, 0.0))
                    out = jnp.where((rowi == b) & (lanei == h), val, out)
            grb_ref[...] = out
            gsk_ref[...] = gsk

    cq, ck, cv = COL_SQ // SWA_W, COL_SK // LANES, COL_SV // LANES
    cur = lambda n: jnp.minimum(n, nb - 1)
    prev = lambda n: jnp.maximum(jnp.minimum(n, nb - 1) - 1, 0)
    kout = lambda n: jnp.maximum(n - 1, 0)
    return pl.pallas_call(
        kern, name="swa_bwd",
        grid=(nb + 1,),
        in_specs=[pl.BlockSpec((BLOCK, SWA_W), lambda n: (cur(n), cq)),
                  pl.BlockSpec((BLOCK, LANES), lambda n: (prev(n), ck)),
                  pl.BlockSpec((BLOCK, LANES), lambda n: (cur(n), ck)),
                  pl.BlockSpec((BLOCK, LANES), lambda n: (prev(n), cv)),
                  pl.BlockSpec((BLOCK, LANES), lambda n: (cur(n), cv)),
                  pl.BlockSpec((BLOCK, SWA_W), lambda n: (cur(n), 1)),
                  pl.BlockSpec((1, SWA_KV_HEADS, 1, SWA_LANES), lambda n: (cur(n), 0, 0, 0)),
                  pl.BlockSpec((1, SWA_KV_HEADS, 1, SWA_LANES), lambda n: (cur(n), 0, 0, 0)),
                  _resident((2, SWA_KV_HEADS, 2 * BLOCK, SWA_LANES), lambda n: (0, 0, 0, 0)),
                  _resident((SWA_KV_HEADS, 1, SWA_LANES), lambda n: (0, 0, 0)),
                  _resident((2 * BLOCK, BLOCK), lambda n: (0, 0))],
        out_specs=[pl.BlockSpec((BLOCK, SWA_W), lambda n: (cur(n), 0)),
                   pl.BlockSpec((BLOCK, LANES), lambda n: (kout(n), 0)),
                   pl.BlockSpec((BLOCK, LANES), lambda n: (kout(n), 0)),
                   pl.BlockSpec((NUM_BUCKETS, LANES), lambda n: (0, 0)),
                   pl.BlockSpec((8, LANES), lambda n: (0, 0))],
        out_shape=[jax.ShapeDtypeStruct((s, SWA_W), BF16),
                   jax.ShapeDtypeStruct((s, LANES), BF16),
                   jax.ShapeDtypeStruct((s, LANES), BF16),
                   jax.ShapeDtypeStruct((NUM_BUCKETS, LANES), F32),
                   jax.ShapeDtypeStruct((8, LANES), F32)],
        scratch_shapes=[pltpu.VMEM((SWA_KV_HEADS, 2 * BLOCK, SWA_LANES), F32),
                        pltpu.VMEM((BLOCK, LANES), F32),
                        pltpu.VMEM((BLOCK, LANES), F32),
                        pltpu.VMEM((SWA_KV_HEADS, 1, SWA_LANES), F32)],
        compiler_params=_cparams(("arbitrary",)),
    )(qkv, qkv, qkv, qkv, qkv, do_bf, delta_rows, lse, bias_t, sink_rows, bucket_t)


def _post(x, target, o_fox, o_swa, z, w_o, ln_g, ln_b):
    s = x.shape[0]
    tm = min(256, s)
    nt = s // tm

    def kern(x_ref, t_ref, of_ref, os_ref, z_ref, w_ref, g_ref, b_ref,
             loss_ref, dh_ref, gwo_ref, do_ref, dz_ref, dl_ref, gg_ref, gb_ref, lacc_ref):
        step = pl.program_id(0)

        @pl.when(step == 0)
        def _():
            lacc_ref[...] = jnp.zeros_like(lacc_ref)
            gg_ref[...] = jnp.zeros_like(gg_ref)
            gwo_ref[...] = jnp.zeros_like(gwo_ref)
            gb_ref[...] = jnp.zeros_like(gb_ref)

        o = jnp.concatenate([of_ref[...], os_ref[...]], axis=1)
        zz = z_ref[...]
        sig = 1.0 / (1.0 + jnp.exp(-zz))
        silu = zz * sig
        mixed32 = o * silu
        mixed = mixed32.astype(BF16)
        w = w_ref[...]
        h = ALPHA * x_ref[...] + _dot(mixed, w)
        mu = jnp.mean(h, axis=1, keepdims=True)
        hc = h - mu
        var = jnp.mean(hc * hc, axis=1, keepdims=True)
        rstd = lax.rsqrt(var + LN_EPS)
        xhat = hc * rstd
        g = g_ref[...]
        err = xhat * g + b_ref[...] - t_ref[...]
        lacc_ref[...] += jnp.broadcast_to(jnp.sum(err * err, axis=0, keepdims=True), lacc_ref.shape)
        dout = err * (1.0 / D_MODEL)
        gg_ref[...] += jnp.broadcast_to(jnp.sum(dout * xhat, axis=0, keepdims=True), gg_ref.shape)
        gb_ref[...] += jnp.broadcast_to(jnp.sum(dout, axis=0, keepdims=True), gb_ref.shape)
        dxh = dout * g
        m1 = jnp.mean(dxh, axis=1, keepdims=True)
        m2 = jnp.mean(dxh * xhat, axis=1, keepdims=True)
        dh = rstd * (dxh - m1 - xhat * m2)
        dh_ref[...] = dh
        dy = dh.astype(BF16)
        gwo_ref[...] += _dot(mixed32.T.astype(BF16), dy)
        dmix = _dot_nt(dy, w)
        do = dmix * silu
        do_ref[...] = do.astype(BF16)
        dz_ref[...] = (dmix * o * (sig * (1.0 + zz * (1.0 - sig)))).astype(BF16)
        r = lax.broadcasted_iota(jnp.int32, (D_MODEL, LANES), 0) // HEAD_DIM
        c = lax.broadcasted_iota(jnp.int32, (D_MODEL, LANES), 1)
        pick = jnp.where(r == c, 1.0, 0.0).astype(BF16)
        dl_ref[...] = _exact_dot(pick, do * o, False)

        @pl.when(step == nt - 1)
        def _():
            tot = jnp.sum(lacc_ref[0:1, :]) * (0.5 / D_MODEL)
            loss_ref[...] = jnp.broadcast_to(tot, loss_ref.shape)

    row = lambda i: (i, 0)
    fixed = lambda i: (0, 0)
    wide = pl.BlockSpec((tm, D_MODEL), row)
    half = pl.BlockSpec((tm, FOX_W), row)
    return pl.pallas_call(
        kern, name="post",
        grid=(nt,),
        in_specs=[wide, wide, half, half, wide,
                  pl.BlockSpec((D_MODEL, D_MODEL), fixed),
                  pl.BlockSpec((1, D_MODEL), fixed),
                  pl.BlockSpec((1, D_MODEL), fixed)],
        out_specs=[pl.BlockSpec((8, LANES), fixed), wide,
                   _resident((D_MODEL, D_MODEL), fixed), wide, wide,
                   pl.BlockSpec((tm, LANES), row),
                   pl.BlockSpec((8, D_MODEL), fixed), pl.BlockSpec((8, D_MODEL), fixed)],
        out_shape=[jax.ShapeDtypeStruct((8, LANES), F32),
                   jax.ShapeDtypeStruct((s, D_MODEL), F32),
                   jax.ShapeDtypeStruct((D_MODEL, D_MODEL), F32),
                   jax.ShapeDtypeStruct((s, D_MODEL), BF16),
                   jax.ShapeDtypeStruct((s, D_MODEL), BF16),
                   jax.ShapeDtypeStruct((s, LANES), F32),
                   jax.ShapeDtypeStruct((8, D_MODEL), F32),
                   jax.ShapeDtypeStruct((8, D_MODEL), F32)],
        scratch_shapes=[pltpu.VMEM((8, D_MODEL), F32)],
        compiler_params=_cparams(("arbitrary",)),
    )(x, target, o_fox, o_swa, z, w_o, ln_g, ln_b)


def _adamw_math(w, g, m, v):
    m = ADAM_B1 * m + (1.0 - ADAM_B1) * g
    v = ADAM_B2 * v + (1.0 - ADAM_B2) * (g * g)
    m_hat = m / (1.0 - ADAM_B1 ** ADAM_STEP)
    v_hat = v / (1.0 - ADAM_B2 ** ADAM_STEP)
    delta = -ADAM_LR * (m_hat / (jnp.sqrt(v_hat) + ADAM_EPS) + ADAM_WD * w)
    return delta, m, v


def _adamw(w, g, m, v, *, name):
    r, c = w.shape
    tr = min(256, r)

    def kern(w_ref, g_ref, m_ref, v_ref, d_ref, mo_ref, vo_ref):
        d, mn, vn = _adamw_math(w_ref[...], g_ref[...], m_ref[...], v_ref[...])
        d_ref[...] = d
        mo_ref[...] = mn
        vo_ref[...] = vn

    blk = pl.BlockSpec((tr, c), lambda i: (i, 0))
    sds = jax.ShapeDtypeStruct((r, c), F32)
    return pl.pallas_call(
        kern, name=name,
        grid=(r // tr,),
        in_specs=[blk, blk, blk, blk],
        out_specs=[blk, blk, blk],
        out_shape=[sds, sds, sds],
        compiler_params=_cparams(("parallel",)),
    )(w, g, m, v)


def _adamw_cols(w, g, m, v, *, name):
    c, _, r = w.shape
    tc = 139
    assert c % tc == 0

    def kern(w_ref, g_ref, m_ref, v_ref, d_ref, mo_ref, vo_ref):
        d, mn, vn = _adamw_math(w_ref[...], g_ref[...], m_ref[...], v_ref[...])
        d_ref[...] = d
        mo_ref[...] = mn
        vo_ref[...] = vn

    blk = pl.BlockSpec((tc, 1, r), lambda i: (i, 0, 0))
    sds = jax.ShapeDtypeStruct((c, 1, r), F32)
    return pl.pallas_call(
        kern, name=name,
        grid=(c // tc,),
        in_specs=[blk, blk, blk, blk],
        out_specs=[blk, blk, blk],
        out_shape=[sds, sds, sds],
        compiler_params=_cparams(("parallel",)),
    )(w, g, m, v)


def _position():
    x, y, c = lax.axis_index("x"), lax.axis_index("y"), lax.axis_index("c")
    chips = [(1 - x, y), (x, 1 - y), (1 - x, 1 - y)]
    return x, y, c, chips


def _chip_index(cx, cy):
    return 2 * cx + cy


def _gather_weights(w_in_bf, w_o_bf):
    shards = (w_in_bf, w_o_bf)
    n_arr = len(shards)

    def kern(*refs):
        ins, outs = refs[:n_arr], refs[n_arr:2 * n_arr]
        send_sems, recv_sems, local_sems = refs[2 * n_arr:]
        x, y, c, chips = _position()
        me = _chip_index(x, y)
        sibling = (x, y, 1 - c)

        local = [pltpu.make_async_copy(ins[a], outs[a].at[me], local_sems.at[a]) for a in range(n_arr)]
        for cp in local:
            cp.start()

        def half(ref, a):
            rows = shards[a].shape[0] // 2
            return ref.at[pl.ds(c * rows, rows), :]

        def copy(a, k, src, slot, to):
            return pltpu.make_async_remote_copy(
                src_ref=src, dst_ref=half(outs[a].at[slot], a),
                send_sem=send_sems.at[a * 6 + k], recv_sem=recv_sems.at[a * 6 + k],
                device_id=to, device_id_type=MESH)

        first = [copy(a, j, half(ins[a], a), me, (*chip, c)) for a in range(n_arr) for j, chip in enumerate(chips)]
        for cp in first:
            cp.start()
        passed = []
        for a in range(n_arr):
            for j, chip in enumerate(chips):
                slot = _chip_index(*chip)
                copy(a, j, half(ins[a], a), slot, (*chip, c)).wait_recv()
                fwd = copy(a, 3 + j, half(outs[a].at[slot], a), slot, sibling)
                fwd.start()
                passed.append(fwd)
        for a in range(n_arr):
            for j, chip in enumerate(chips):
                slot = _chip_index(*chip)
                rows = shards[a].shape[0] // 2
                dst = outs[a].at[slot].at[pl.ds((1 - c) * rows, rows), :]
                pltpu.make_async_remote_copy(
                    src_ref=dst, dst_ref=dst, send_sem=send_sems.at[a * 6 + 3 + j],
                    recv_sem=recv_sems.at[a * 6 + 3 + j], device_id=sibling, device_id_type=MESH).wait_recv()
        for cp in first + passed:
            cp.wait_send()
        for cp in local:
            cp.wait()

    vmem = pl.BlockSpec(memory_space=pltpu.VMEM)
    return pl.pallas_call(
        kern, name="gather_weights",
        in_specs=[vmem] * n_arr,
        out_specs=[vmem] * n_arr,
        out_shape=[jax.ShapeDtypeStruct((N_CHIPS,) + w.shape, w.dtype) for w in shards],
        scratch_shapes=[pltpu.SemaphoreType.DMA((6 * n_arr,)),
                        pltpu.SemaphoreType.DMA((6 * n_arr,)),
                        pltpu.SemaphoreType.DMA((n_arr,))],
        compiler_params=_cparams(),
    )(*shards)


def _swap_halves(grads):
    n_arr = len(grads)

    def kern(*refs):
        ins = refs[:n_arr]
        owns = refs[n_arr:2 * n_arr]
        gots = refs[2 * n_arr:3 * n_arr]
        send_sems, recv_sems, local_sems = refs[3 * n_arr:]
        x, y, c, _ = _position()
        sibling = (x, y, 1 - c)
        local, remote = [], []
        for a in range(n_arr):
            rows = grads[a].shape[1] // 2
            piece = rows // COPY_PIECES
            for j in range(N_CHIPS):
                for r in range(COPY_PIECES):
                    k = (a * N_CHIPS + j) * COPY_PIECES + r
                    dst_rows = pl.ds(r * piece, piece)
                    local.append(pltpu.make_async_copy(
                        ins[a].at[j, pl.ds(c * rows + r * piece, piece), :],
                        owns[a].at[j, dst_rows, :], local_sems.at[k]))
                    remote.append(pltpu.make_async_remote_copy(
                        src_ref=ins[a].at[j, pl.ds((1 - c) * rows + r * piece, piece), :],
                        dst_ref=gots[a].at[j, dst_rows, :], send_sem=send_sems.at[k], recv_sem=recv_sems.at[k],
                        device_id=sibling, device_id_type=MESH))
        for cp in local + remote:
            cp.start()
        for cp in remote:
            cp.wait()
        for cp in local:
            cp.wait()

    hbm = pl.BlockSpec(memory_space=pltpu.VMEM)
    half = [jax.ShapeDtypeStruct((N_CHIPS, g.shape[1] // 2, g.shape[2]), F32) for g in grads]
    outs = pl.pallas_call(
        kern, name="swap_halves",
        in_specs=[hbm] * n_arr,
        out_specs=[hbm] * (2 * n_arr),
        out_shape=half + half,
        scratch_shapes=[pltpu.SemaphoreType.DMA((n_arr * N_CHIPS * COPY_PIECES,)),
                        pltpu.SemaphoreType.DMA((n_arr * N_CHIPS * COPY_PIECES,)),
                        pltpu.SemaphoreType.DMA((n_arr * N_CHIPS * COPY_PIECES,))],
        compiler_params=_cparams(),
    )(*grads)
    return outs[:n_arr], outs[n_arr:]


def _scatter_to_owners(parts):
    n_arr = len(parts)

    def kern(*refs):
        ins = refs[:n_arr]
        outs = refs[n_arr:2 * n_arr]
        send_sems, recv_sems, local_sems = refs[2 * n_arr:]
        x, y, c, chips = _position()
        me = _chip_index(x, y)
        local = [pltpu.make_async_copy(ins[a].at[me], outs[a].at[me], local_sems.at[a]) for a in range(n_arr)]
        for cp in local:
            cp.start()
        sends = []
        for a in range(n_arr):
            for j, chip in enumerate(chips):
                sends.append(pltpu.make_async_remote_copy(
                    src_ref=ins[a].at[_chip_index(*chip)], dst_ref=outs[a].at[me],
                    send_sem=send_sems.at[a * 3 + j], recv_sem=recv_sems.at[a * 3 + j],
                    device_id=(*chip, c), device_id_type=MESH))
        for cp in sends:
            cp.start()
        for a in range(n_arr):
            for j, chip in enumerate(chips):
                slot = outs[a].at[_chip_index(*chip)]
                pltpu.make_async_remote_copy(
                    src_ref=slot, dst_ref=slot, send_sem=send_sems.at[a * 3 + j],
                    recv_sem=recv_sems.at[a * 3 + j], device_id=(*chip, c), device_id_type=MESH).wait_recv()
        for cp in sends:
            cp.wait_send()
        for cp in local:
            cp.wait()

    hbm = pl.BlockSpec(memory_space=pltpu.VMEM)
    return pl.pallas_call(
        kern, name="scatter_to_owners",
        in_specs=[hbm] * n_arr,
        out_specs=[hbm] * n_arr,
        out_shape=[jax.ShapeDtypeStruct(p.shape, p.dtype) for p in parts],
        scratch_shapes=[pltpu.SemaphoreType.DMA((3 * n_arr,)),
                        pltpu.SemaphoreType.DMA((3 * n_arr,)),
                        pltpu.SemaphoreType.DMA((n_arr,))],
        compiler_params=_cparams(),
    )(*parts)


def _join_halves(halves):
    n_arr = len(halves)

    def kern(*refs):
        ins = refs[:n_arr]
        outs = refs[n_arr:2 * n_arr]
        send_sems, recv_sems, local_sems = refs[2 * n_arr:]
        x, y, c, _ = _position()
        sibling = (x, y, 1 - c)
        local, remote = [], []
        for a in range(n_arr):
            rows = halves[a].shape[0]
            piece = rows // COPY_PIECES
            for r in range(COPY_PIECES):
                k = a * COPY_PIECES + r
                src = ins[a].at[pl.ds(r * piece, piece), :]
                dst = outs[a].at[pl.ds(c * rows + r * piece, piece), :]
                local.append(pltpu.make_async_copy(src, dst, local_sems.at[k]))
                remote.append(pltpu.make_async_remote_copy(
                    src_ref=src, dst_ref=dst, send_sem=send_sems.at[k], recv_sem=recv_sems.at[k],
                    device_id=sibling, device_id_type=MESH))
        for cp in local + remote:
            cp.start()
        for a in range(n_arr):
            rows = halves[a].shape[0]
            piece = rows // COPY_PIECES
            for r in range(COPY_PIECES):
                k = a * COPY_PIECES + r
                theirs = outs[a].at[pl.ds((1 - c) * rows + r * piece, piece), :]
                pltpu.make_async_remote_copy(
                    src_ref=theirs, dst_ref=theirs, send_sem=send_sems.at[k], recv_sem=recv_sems.at[k],
                    device_id=sibling, device_id_type=MESH).wait_recv()
        for cp in remote:
            cp.wait_send()
        for cp in local:
            cp.wait()

    hbm = pl.BlockSpec(memory_space=pltpu.VMEM)
    return pl.pallas_call(
        kern, name="join_halves",
        in_specs=[hbm] * n_arr,
        out_specs=[hbm] * n_arr,
        out_shape=[jax.ShapeDtypeStruct((2 * h.shape[0], h.shape[1]), F32) for h in halves],
        scratch_shapes=[pltpu.SemaphoreType.DMA((n_arr * COPY_PIECES,)),
                        pltpu.SemaphoreType.DMA((n_arr * COPY_PIECES,)),
                        pltpu.SemaphoreType.DMA((n_arr * COPY_PIECES,))],
        compiler_params=_cparams(),
    )(*halves)


def _add2(a, b, *, name):
    n, r, c = a.shape
    tr = min(256, r)

    def kern(a_ref, b_ref, o_ref):
        o_ref[...] = (a_ref[...] + b_ref[...]).astype(BF16)

    blk = pl.BlockSpec((1, tr, c), lambda j, i: (j, i, 0))
    return pl.pallas_call(
        kern, name=name,
        grid=(n, r // tr),
        in_specs=[blk, blk],
        out_specs=blk,
        out_shape=jax.ShapeDtypeStruct(a.shape, BF16),
        compiler_params=_cparams(("parallel", "parallel")),
    )(a, b)


def _sum4(a, *, name):
    n, r, c = a.shape
    tr = min(256, r)

    def kern(a_ref, o_ref):
        f = lambda j: a_ref[j].astype(F32)
        o_ref[...] = ((f(0) + f(1)) + f(2)) + f(3)

    return pl.pallas_call(
        kern, name=name,
        grid=(r // tr,),
        in_specs=[pl.BlockSpec((n, tr, c), lambda i: (0, i, 0))],
        out_specs=pl.BlockSpec((tr, c), lambda i: (i, 0)),
        out_shape=jax.ShapeDtypeStruct((r, c), F32),
        compiler_params=_cparams(("parallel",)),
    )(a)


def _small_allreduce_adamw(g, w, m, v):
    def kern(g_ref, w_ref, m_ref, v_ref, gs_ref, d_ref, mo_ref, vo_ref, buf_ref, send_sems, recv_sems):
        x, y, c, _ = _position()
        me = 4 * x + 2 * y + c
        buf_ref[me] = g_ref[...]
        peers = [(x, y, 1 - c)] + [(px, py, pc) for px, py in _position()[3] for pc in (c, 1 - c)]
        sends = []
        for k, peer in enumerate(peers):
            sends.append(pltpu.make_async_remote_copy(
                src_ref=g_ref, dst_ref=buf_ref.at[me], send_sem=send_sems.at[k], recv_sem=recv_sems.at[k],
                device_id=peer, device_id_type=MESH))
        for cp in sends:
            cp.start()
        for k, (px, py, pc) in enumerate(peers):
            slot = buf_ref.at[4 * px + 2 * py + pc]
            pltpu.make_async_remote_copy(
                src_ref=slot, dst_ref=slot, send_sem=send_sems.at[k], recv_sem=recv_sems.at[k],
                device_id=(px, py, pc), device_id_type=MESH).wait_recv()
        for cp in sends:
            cp.wait_send()
        tot = buf_ref[0]
        for d in range(1, N_DEV):
            tot = tot + buf_ref[d]
        gs_ref[...] = tot
        delta, mn, vn = _adamw_math(w_ref[...], tot, m_ref[...], v_ref[...])
        d_ref[...] = delta
        mo_ref[...] = mn
        vo_ref[...] = vn

    vm = pl.BlockSpec(memory_space=pltpu.VMEM)
    sds = jax.ShapeDtypeStruct((SMALL_ROWS, LANES), F32)
    return pl.pallas_call(
        kern, name="small_allreduce_adamw",
        in_specs=[vm] * 4,
        out_specs=[vm] * 4,
        out_shape=[sds] * 4,
        scratch_shapes=[pltpu.VMEM((N_DEV, SMALL_ROWS, LANES), F32),
                        pltpu.SemaphoreType.DMA((N_DEV - 1,)),
                        pltpu.SemaphoreType.DMA((N_DEV - 1,))],
    )(g, w, m, v)


def _to_padded_cols(w):
    pad = jnp.zeros((w.shape[0], N_C - FOX_HEADS), w.dtype)
    return jnp.concatenate([w[:, 0:1536], w[:, 2056:2824], w[:, 1536:1544], pad,
                            w[:, 1544:2056], w[:, 2824:3336]], axis=1)


def _from_padded_cols(g):
    return jnp.concatenate([g[:, 0:1536], g[:, OFF_C:OFF_C + FOX_HEADS], g[:, OFF_B:OFF_B + FOX_W],
                            g[:, 1536:N_A], g[:, OFF_B + FOX_W:N_PAD]], axis=1)


def _pack_small(b_f, rel_bias, sink, ln_g, ln_b):
    row = lambda v: jnp.pad(v.reshape(1, -1), ((0, 0), (0, LANES - v.size)))
    return jnp.concatenate([ln_g.reshape(8, LANES), ln_b.reshape(8, LANES), rel_bias.reshape(2, LANES),
                            row(b_f), row(sink), jnp.zeros((4, LANES), F32)], axis=0)


def _unpack_small(p):
    ln_g = p[0:8].reshape(1, D_MODEL)
    ln_b = p[8:16].reshape(1, D_MODEL)
    rel_bias = p[16:18].reshape(NUM_BUCKETS, SWA_HEADS)
    b_f = p[18:19, :FOX_HEADS]
    sink = p[19:20, :SWA_HEADS]
    return b_f, rel_bias, sink, ln_g, ln_b


def _fox_rows(a):
    return a[:, :FOX_HEADS].T.reshape(FOX_HEADS, 1, a.shape[0])


def kernel(x, w_in, b_f, rel_bias, sink, w_o, ln_g, ln_b, loss_target, m_w_in, m_b_f, m_rel_bias, m_sink, m_w_o, m_ln_g, m_ln_b, v_w_in, v_b_f, v_rel_bias, v_sink, v_w_o, v_ln_g, v_ln_b):
    x2 = x[0]
    tgt = loss_target[0]
    s = x2.shape[0]
    w_in2, w_o2 = w_in[0], w_o[0]

    shard_cols = D_IN // N_CHIPS
    col_pad = ((0, 0), (0, SHARD_PAD - shard_cols))
    w_in_all, w_o_all = _gather_weights(jnp.pad(w_in2.astype(BF16), col_pad), w_o2.astype(BF16))
    w_full = jnp.concatenate([w_in_all[j, :, :shard_cols] for j in range(N_CHIPS)], axis=1)
    w_pad = _to_padded_cols(w_full)
    w_o_full = w_o_all.reshape(D_MODEL, D_MODEL)

    qkv, ffp, z, xt, vt = _project(x2, w_pad)
    bfp = jnp.pad(b_f, ((0, 0), (0, LANES - FOX_HEADS)))
    cum = _cum_fwd(ffp, bfp)
    cum_t3 = _fox_rows(cum)
    o_fox, lse_t3 = _fox_fwd(qkv, vt, cum_t3, cum)
    bucket_t = jnp.asarray(_bucket_table().T)
    bias_t = _swa_bias(rel_bias, bucket_t)
    sink_rows = jnp.repeat(sink.reshape(SWA_KV_HEADS, SWA_GROUP, 1), BLOCK, axis=2).reshape(SWA_KV_HEADS, 1, SWA_LANES)
    o_swa, lse_swa = _swa_fwd(qkv, bias_t, sink_rows)

    loss8, dh, grad_w_o_full, do_bf, dz, delta, gg8, gb8 = _post(
        x2, tgt, o_fox, o_swa, z, w_o_full, ln_g, ln_b)

    delta_t3 = _fox_rows(delta)
    dqt_fox, dk_fox, dv_fox, dcum_k, dcum_q = _fox_bwd(qkv, do_bf, cum_t3, cum, lse_t3, delta_t3)
    dcum_q = jnp.pad(dcum_q.reshape(FOX_HEADS, s).T, ((0, 0), (0, LANES - FOX_HEADS)))
    dff, gbf8 = _cum_bwd(dcum_k, dcum_q, ffp, bfp)
    delta_rows = (delta[:, FOX_HEADS:FOX_HEADS + SWA_HEADS].reshape(s // BLOCK, BLOCK, SWA_KV_HEADS, SWA_GROUP)
                  .transpose(0, 2, 3, 1).reshape(s // BLOCK, SWA_KV_HEADS, 1, SWA_LANES))
    dq_swa, dk_swa, dv_swa, grb, gsk8 = _swa_bwd(qkv, do_bf, delta_rows, lse_swa, bias_t, sink_rows, bucket_t)

    dq_fox = dqt_fox.T.astype(BF16)
    d_misc = jnp.concatenate([dk_swa, dv_swa, dff], axis=1)
    pieces = [dq_fox, dk_fox, dv_fox, dq_swa, d_misc, dz]
    grad_x = _grad_x_matmul(pieces, w_pad, dh, tm=512, tn=D_MODEL, name="grad_x")
    blocks = [(p, 0) for p in pieces[:-1]] + [(dz, 0), (dz, 1)]
    grad_w_pad = _grad_w_matmul(xt, blocks, tk=1024, name="grad_w_in")
    grad_w_in_full = _from_padded_cols(grad_w_pad)

    g_in4 = jnp.stack([jnp.pad(grad_w_in_full[:, j * shard_cols:(j + 1) * shard_cols], col_pad)
                       for j in range(N_CHIPS)])
    g_o4 = grad_w_o_full.reshape(N_CHIPS, D_MODEL // N_CHIPS, D_MODEL)
    owns, gots = _swap_halves([g_in4, g_o4])
    parts = [_add2(owns[0], gots[0], name="pair_sum_w_in"), _add2(owns[1], gots[1], name="pair_sum_w_o")]
    slabs = _scatter_to_owners(parts)
    halves = [_sum4(slabs[0], name="chip_sum_w_in"), _sum4(slabs[1], name="chip_sum_w_o")]
    g_w_in, g_w_o = _join_halves(halves)
    g_w_in = g_w_in[:, :shard_cols]

    cols_first = lambda a: jnp.transpose(a, (2, 0, 1))
    rows_first = lambda a: jnp.transpose(a, (1, 2, 0))
    d_w_in, nm_w_in, nv_w_in = [rows_first(a) for a in _adamw_cols(
        cols_first(w_in), cols_first(g_w_in[None]), cols_first(m_w_in), cols_first(v_w_in), name="adamw_w_in")]
    d_w_o, nm_w_o, nv_w_o = _adamw(w_o2, g_w_o, m_w_o[0], v_w_o[0], name="adamw_w_o")

    g_small = _pack_small(gbf8[0:1, :FOX_HEADS], grb[:, :SWA_HEADS], gsk8[0:1, :SWA_HEADS], gg8[0:1], gb8[0:1])
    g_small = g_small.at[LOSS_ROW, 0].set(loss8[0, 0])
    w_small = _pack_small(b_f, rel_bias, sink, ln_g, ln_b)
    m_small = _pack_small(m_b_f, m_rel_bias, m_sink, m_ln_g, m_ln_b)
    v_small = _pack_small(v_b_f, v_rel_bias, v_sink, v_ln_g, v_ln_b)
    gs, ds, ms, vs = _small_allreduce_adamw(g_small, w_small, m_small, v_small)
    loss = gs[LOSS_ROW, 0]
    g_bf, g_rb, g_sk, g_lg, g_lb = _unpack_small(gs)
    d_bf, d_rb, d_sk, d_lg, d_lb = _unpack_small(ds)
    m_bf, m_rb, m_sk, m_lg, m_lb = _unpack_small(ms)
    v_bf, v_rb, v_sk, v_lg, v_lb = _unpack_small(vs)

    e = lambda a: a[None]
    return (loss, e(grad_x),
            e(g_w_in), g_bf, g_rb, g_sk, e(g_w_o), g_lg, g_lb,
            d_w_in, d_bf, d_rb, d_sk, e(d_w_o), d_lg, d_lb,
            nm_w_in, m_bf, m_rb, m_sk, e(nm_w_o), m_lg, m_lb,
            nv_w_in, v_bf, v_rb, v_sk, e(nv_w_o), v_lg, v_lb)
```

```python
import functools
import math

import numpy as np
import jax
import jax.numpy as jnp
from jax import lax
from jax.experimental import pallas as pl
from jax.experimental.pallas import tpu as pltpu

F32 = jnp.float32
BF16 = jnp.bfloat16

D_MODEL = 1024
HEAD_DIM = 64
FOX_HEADS = 8
SWA_HEADS = 8
SWA_KV_HEADS = 2
SWA_GROUP = 4
FOX_W = 512
SWA_W = 512
SWA_KV_W = 128
BLOCK = 128
NUM_BUCKETS = 32
MAX_DISTANCE = 128
LN_EPS = 1e-5
NEG = -1e30
ALPHA = 2.0 ** 0.25
QK_SCALE = 0.125

ADAM_LR = 0.001
ADAM_B1 = 0.9
ADAM_B2 = 0.999
ADAM_EPS = 1e-08
ADAM_WD = 0.01
ADAM_STEP = 10

D_IN = 3336
SHARD_PAD = 896
N_A = 2304
N_C = 256
N_B = 1024
OFF_C = N_A
OFF_B = N_A + N_C
N_PAD = N_A + N_C + N_B
COL_FK, COL_FV, COL_SQ, COL_SK, COL_SV = 512, 1024, 1536, 2048, 2176

LANES = 128
FOX_T = 256
FOX_REF = 512
SUM_ROWS = 16
VMEM_LIMIT = 56 * 1024 * 1024

MESH = pl.DeviceIdType.MESH
N_CHIPS = 4
N_DEV = 8
SMALL_ROWS = 24
LOSS_ROW = 20


def _cparams(sem=None):
    return pltpu.CompilerParams(dimension_semantics=sem, vmem_limit_bytes=VMEM_LIMIT)


def _split3(x):
    hi = x.astype(BF16)
    r = x - hi.astype(F32)
    mid = r.astype(BF16)
    lo = (r - mid.astype(F32)).astype(BF16)
    return hi, mid, lo


def _dot(a, b):
    return jnp.dot(a, b, preferred_element_type=F32)


def _dot_nt(a, b):
    return lax.dot_general(a, b, (((1,), (1,)), ((), ())), preferred_element_type=F32)


def _project(x, w_pad):
    s, k = x.shape
    tm = 512
    chunk = 512

    def kern(x_ref, w_ref, qkv_ref, ff_ref, z_ref, xt_ref, vt_ref):
        xf = x_ref[...]
        xb = xf.astype(BF16)
        xt_ref[...] = xf.T.astype(BF16)
        for c0 in range(0, N_A, chunk):
            width = min(chunk, N_A - c0)
            res = _dot(xb, w_ref[:, c0:c0 + width])
            qkv_ref[:, c0:c0 + width] = res.astype(BF16)
            if c0 == COL_FV:
                vt_ref[...] = res.T.astype(BF16)
        ff_ref[...] = _dot(xb, w_ref[:, OFF_C:OFF_C + N_C])
        for c0 in range(0, N_B, 512):
            z_ref[:, c0:c0 + 512] = _dot(xb, w_ref[:, OFF_B + c0:OFF_B + c0 + 512])

    row = lambda i: (i, 0)
    return pl.pallas_call(
        kern, name="project",
        grid=(s // tm,),
        in_specs=[pl.BlockSpec((tm, k), row),
                  _resident((k, N_PAD), lambda i: (0, 0))],
        out_specs=[pl.BlockSpec((tm, N_A), row),
                   pl.BlockSpec((tm, N_C), row),
                   pl.BlockSpec((tm, N_B), row),
                   pl.BlockSpec((k, tm), lambda i: (0, i)),
                   pl.BlockSpec((FOX_W, tm), lambda i: (0, i))],
        out_shape=[jax.ShapeDtypeStruct((s, N_A), BF16),
                   jax.ShapeDtypeStruct((s, N_C), F32),
                   jax.ShapeDtypeStruct((s, N_B), F32),
                   jax.ShapeDtypeStruct((k, s), BF16),
                   jax.ShapeDtypeStruct((FOX_W, s), BF16)],
        compiler_params=_cparams(("parallel",)),
    )(x, w_pad)


def _grad_x_matmul(pieces, w_pad, dh, *, tm, tn, name):
    m = dh.shape[0]
    n, k = w_pad.shape
    widths = [p.shape[1] for p in pieces]
    offs = [sum(widths[:i]) for i in range(len(pieces))]
    assert sum(widths) == k

    def kern(*refs):
        p_refs, (b_ref, dh_ref, o_ref) = refs[:len(pieces)], refs[len(pieces):]
        acc = ALPHA * dh_ref[...]
        for p_ref, off, width in zip(p_refs, offs, widths):
            acc = acc + _dot_nt(p_ref[...], b_ref[:, off:off + width])
        o_ref[...] = acc

    assert tn == n
    return pl.pallas_call(
        kern, name=name,
        grid=(m // tm,),
        in_specs=[pl.BlockSpec((tm, w), lambda i: (i, 0)) for w in widths]
        + [_resident((n, k), lambda i: (0, 0)),
           pl.BlockSpec((tm, n), lambda i: (i, 0))],
        out_specs=pl.BlockSpec((tm, n), lambda i: (i, 0)),
        out_shape=jax.ShapeDtypeStruct((m, n), F32),
        compiler_params=_cparams(("parallel",)),
    )(*pieces, w_pad, dh)


def _grad_w_matmul(xt, blocks, *, tk, name):
    m, s = xt.shape
    tn = 512
    nb = len(blocks)

    def kern(a_ref, *refs):
        b_refs, o_ref = refs[:nb], refs[nb]

        @pl.when(pl.program_id(0) == 0)
        def _():
            o_ref[...] = jnp.zeros_like(o_ref)
        a = a_ref[...]
        for blk in range(nb):
            o_ref[:, blk * tn:(blk + 1) * tn] += _dot(a, b_refs[blk][...])

    return pl.pallas_call(
        kern, name=name,
        grid=(s // tk,),
        in_specs=[pl.BlockSpec((m, tk), lambda k: (0, k))]
        + [pl.BlockSpec((tk, tn), functools.partial(lambda k, col: (k, col), col=col)) for _, col in blocks],
        out_specs=_resident((m, nb * tn), lambda k: (0, 0)),
        out_shape=jax.ShapeDtypeStruct((m, nb * tn), F32),
        compiler_params=_cparams(("arbitrary",)),
    )(xt, *[arr for arr, _ in blocks])


def _tri(n, lower):
    r = lax.broadcasted_iota(jnp.int32, (n, n), 0)
    c = lax.broadcasted_iota(jnp.int32, (n, n), 1)
    keep = (c <= r) if lower else (c >= r)
    return jnp.where(keep, 1.0, 0.0).astype(BF16)


def _exact_dot(mat_bf16, x_f32, left):
    out = None
    for piece in _split3(x_f32):
        t = _dot(mat_bf16, piece) if left else _dot(piece, mat_bf16)
        out = t if out is None else out + t
    return out


def _log_sigmoid(z):
    return jnp.minimum(z, 0.0) - jnp.log(1.0 + jnp.exp(-jnp.abs(z)))


def _cum_fwd(ffp, bfp):
    s = ffp.shape[0]
    t = min(512, s)

    def kern(ff_ref, b_ref, cum_ref, carry_ref):
        @pl.when(pl.program_id(0) == 0)
        def _():
            carry_ref[...] = jnp.zeros_like(carry_ref)
        lane = lax.broadcasted_iota(jnp.int32, (1, LANES), 1)
        lf = _log_sigmoid(ff_ref[...] + b_ref[...])
        lf = jnp.where(lane < FOX_HEADS, lf, 0.0)
        cum = _exact_dot(_tri(t, True), lf, True) + carry_ref[0:1, :]
        cum_ref[...] = cum
        carry_ref[...] = jnp.broadcast_to(cum[t - 1:t, :], carry_ref.shape)

    return pl.pallas_call(
        kern, name="cum_fwd",
        grid=(s // t,),
        in_specs=[pl.BlockSpec((t, LANES), lambda i: (i, 0)),
                  pl.BlockSpec((1, LANES), lambda i: (0, 0))],
        out_specs=pl.BlockSpec((t, LANES), lambda i: (i, 0)),
        out_shape=jax.ShapeDtypeStruct((s, LANES), F32),
        scratch_shapes=[pltpu.VMEM((8, LANES), F32)],
        compiler_params=_cparams(("arbitrary",)),
    )(ffp, bfp)


def _cum_bwd(dcum_k, dcum_q, ffp, bfp):
    s = dcum_k.shape[0]
    t = min(512, s)
    nb = s // t

    def kern(dck_ref, dcq_ref, ff_ref, b_ref, dff_ref, gb_ref, carry_ref):
        @pl.when(pl.program_id(0) == 0)
        def _():
            carry_ref[...] = jnp.zeros_like(carry_ref)
            gb_ref[...] = jnp.zeros_like(gb_ref)
        lane = lax.broadcasted_iota(jnp.int32, (1, LANES), 1)
        dlf = _exact_dot(_tri(t, False), dck_ref[...] + dcq_ref[...], True) + carry_ref[0:1, :]
        carry_ref[...] = jnp.broadcast_to(dlf[0:1, :], carry_ref.shape)
        z = ff_ref[...] + b_ref[...]
        dff = jnp.where(lane < FOX_HEADS, dlf / (1.0 + jnp.exp(z)), 0.0)
        gb_ref[...] += jnp.broadcast_to(jnp.sum(dff, axis=0, keepdims=True), gb_ref.shape)
        dff_ref[...] = jnp.concatenate([dff, jnp.zeros_like(dff)], axis=1).astype(BF16)

    return pl.pallas_call(
        kern, name="cum_bwd",
        grid=(nb,),
        in_specs=[pl.BlockSpec((t, LANES), lambda i: (nb - 1 - i, 0)),
                  pl.BlockSpec((t, LANES), lambda i: (nb - 1 - i, 0)),
                  pl.BlockSpec((t, LANES), lambda i: (nb - 1 - i, 0)),
                  pl.BlockSpec((1, LANES), lambda i: (0, 0))],
        out_specs=[pl.BlockSpec((t, N_C), lambda i: (nb - 1 - i, 0)),
                   pl.BlockSpec((8, LANES), lambda i: (0, 0))],
        out_shape=[jax.ShapeDtypeStruct((s, N_C), BF16),
                   jax.ShapeDtypeStruct((8, LANES), F32)],
        scratch_shapes=[pltpu.VMEM((8, LANES), F32)],
        compiler_params=_cparams(("arbitrary",)),
    )(dcum_k, dcum_q, ffp, bfp)


def _resident(shape, index_map):
    return pl.BlockSpec(shape, index_map, pipeline_mode=pl.Buffered(1))


def _fox_fwd(qkv, vt, cum_t3, cum):
    s = qkv.shape[0]
    tk = tq = FOX_REF
    nq = s // tq
    nh = FOX_HEADS
    diag_tiles = tq // tk

    def kern(q_ref, k_ref, vt_ref, ct_ref, c_ref, o_ref, lse_ref, m_ref, acc_ref, u_ref):
        i = pl.program_id(0)
        lane = lax.broadcasted_iota(jnp.int32, (1, LANES), 1)
        krow = lax.broadcasted_iota(jnp.int32, (tk, tq), 0)
        qcol = lax.broadcasted_iota(jnp.int32, (tk, tq), 1)
        q0 = pl.multiple_of(i * tq, tq)
        qts, crefs = [], []
        for h in range(nh):
            p, a = divmod(h, 2)
            q2 = q_ref[:, p * LANES:(p + 1) * LANES] * jnp.asarray(QK_SCALE, BF16)
            sel = (lane < HEAD_DIM) if a == 0 else (lane >= HEAD_DIM)
            qts.append(jnp.where(sel, q2, jnp.zeros_like(q2)).astype(F32).T.astype(BF16))
            crefs.append(ct_ref[h, :, pl.ds(q0, LANES)][:, 0:1])
        m_ref[...] = jnp.full(m_ref.shape, NEG, F32)
        acc_ref[...] = jnp.zeros_like(acc_ref)
        ones = jnp.ones((SUM_ROWS, tk), BF16)

        def tile(j, diag):
            k0 = pl.multiple_of(j * tk, tk)
            cb = c_ref[pl.ds(k0, tk), :]
            sts = [_dot(k_ref[pl.ds(k0, tk), (h // 2) * LANES:(h // 2 + 1) * LANES], qts[h]) for h in range(nh)]
            tile_max = []
            for h in range(nh):
                u = sts[h] - (cb[:, h:h + 1] - crefs[h])
                if diag is not None:
                    u = jnp.where(krow + diag * tk <= qcol, u, NEG)
                u_ref[h] = u
                tile_max.append(jnp.max(u, axis=0, keepdims=True))
            pts, scales = [], []
            for h in range(nh):
                m_old = m_ref[h]
                m_new = jnp.maximum(m_old, tile_max[h])
                scales.append(jnp.exp(m_old - m_new))
                pts.append(jnp.exp(u_ref[h] - m_new).astype(BF16))
                m_ref[h] = m_new
            for h in range(nh):
                vth = jnp.concatenate([vt_ref[h * HEAD_DIM:(h + 1) * HEAD_DIM, pl.ds(k0, tk)], ones], axis=0)
                acc_ref[h] = scales[h] * acc_ref[h] + _dot(vth, pts[h])

        def body(j, c):
            tile(j, None)
            return c
        lax.fori_loop(0, i * diag_tiles, body, 0)
        for d in range(diag_tiles):
            tile(i * diag_tiles + d, d)

        ls = [acc_ref[h][HEAD_DIM:HEAD_DIM + 1] for h in range(nh)]
        for p in range(nh // 2):
            ot = jnp.concatenate([acc_ref[2 * p + a][:HEAD_DIM] * (1.0 / ls[2 * p + a]) for a in range(2)], axis=0)
            o_ref[:, p * LANES:(p + 1) * LANES] = ot.T
        for h in range(nh):
            lse_ref[h, :, pl.ds(q0, tq)] = m_ref[h] + jnp.log(ls[h])

    return pl.pallas_call(
        kern, name="fox_fwd",
        grid=(nq,),
        in_specs=[pl.BlockSpec((tq, FOX_W), lambda i: (i, 0)),
                  _resident((s, FOX_W), lambda i: (0, COL_FK // FOX_W)),
                  _resident((FOX_W, s), lambda i: (0, 0)),
                  _resident((nh, 1, s), lambda i: (0, 0, 0)),
                  _resident((s, LANES), lambda i: (0, 0))],
        out_specs=[pl.BlockSpec((tq, FOX_W), lambda i: (i, 0)),
                   pl.BlockSpec((nh, 1, s), lambda i: (0, 0, 0))],
        out_shape=[jax.ShapeDtypeStruct((s, FOX_W), F32),
                   jax.ShapeDtypeStruct((nh, 1, s), F32)],
        scratch_shapes=[pltpu.VMEM((nh, 1, tq), F32),
                        pltpu.VMEM((nh, HEAD_DIM + SUM_ROWS, tq), F32),
                        pltpu.VMEM((nh, tk, tq), F32)],
        compiler_params=_cparams(("arbitrary",)),
    )(qkv, qkv, vt, cum_t3, cum)


def _fox_bwd(qkv, do_bf, cum_t3, cum, lse_t3, delta_t3):
    s = qkv.shape[0]
    t = min(FOX_T, s)
    nq = s // t
    nh = FOX_HEADS
    npair = nh // 2

    def kern(q_ref, do_ref, k_ref, v_ref, ct_ref, c_ref, lse_ref, dl_ref,
             dqt_ref, dk_ref, dv_ref, dc_ref, dcq_ref, accv_ref, acck_ref, accd_ref):
        kj = pl.program_id(0)
        lane = lax.broadcasted_iota(jnp.int32, (1, LANES), 1)
        krow = lax.broadcasted_iota(jnp.int32, (t, t), 0)
        qcol = lax.broadcasted_iota(jnp.int32, (t, t), 1)
        causal = krow <= qcol
        sels = [lane < HEAD_DIM, lane >= HEAD_DIM]

        @pl.when(kj == 0)
        def _():
            dqt_ref[...] = jnp.zeros_like(dqt_ref)
            dcq_ref[...] = jnp.zeros_like(dcq_ref)

        accv_ref[...] = jnp.zeros_like(accv_ref)
        acck_ref[...] = jnp.zeros_like(acck_ref)
        accd_ref[...] = jnp.zeros_like(accd_ref)
        cb = c_ref[...]
        k2s, v2s, kts = [], [], []
        for p in range(npair):
            k2 = k_ref[:, p * LANES:(p + 1) * LANES]
            k2s.append(k2)
            v2s.append(v_ref[:, p * LANES:(p + 1) * LANES])
            kt = k2.astype(F32).T * QK_SCALE
            kts.append(kt[:HEAD_DIM].astype(BF16))
            kts.append(kt[HEAD_DIM:].astype(BF16))
        css = [cb[:, h:h + 1] for h in range(nh)]

        def tile(i, masked):
            q0 = pl.multiple_of(i * t, t)
            r0 = pl.multiple_of((i // (FOX_REF // t)) * FOX_REF, FOX_REF)
            sts, dpts, qms, doms = [], [], [], []
            for h in range(nh):
                p, a = divmod(h, 2)
                qi = q_ref[pl.ds(q0, t), p * LANES:(p + 1) * LANES] * jnp.asarray(QK_SCALE, BF16)
                doi = do_ref[pl.ds(q0, t), p * LANES:(p + 1) * LANES]
                qm = jnp.where(sels[a], qi, jnp.zeros_like(qi))
                dom = jnp.where(sels[a], doi, jnp.zeros_like(doi))
                qms.append(qm)
                doms.append(dom)
                sts.append(_dot_nt(k2s[p], qm))
                dpts.append(_dot_nt(v2s[p], dom))
            pts, dsts = [], []
            for h in range(nh):
                cref = ct_ref[h, :, pl.ds(r0, LANES)][:, 0:1]
                pt = jnp.exp(sts[h] - (css[h] - cref) - lse_ref[h, :, pl.ds(q0, t)])
                if masked:
                    pt = jnp.where(causal, pt, 0.0)
                ds32 = pt * (dpts[h] - dl_ref[h, :, pl.ds(q0, t)])
                part = ds32[:, 0:LANES]
                for c in range(1, t // LANES):
                    part = part + ds32[:, c * LANES:(c + 1) * LANES]
                accd_ref[h] += part
                dcq_ref[h, :, pl.ds(q0, t)] += jnp.sum(ds32, axis=0, keepdims=True)
                pts.append(pt.astype(BF16))
                dsts.append(ds32.astype(BF16))
            for p in range(npair):
                ha, hb = 2 * p, 2 * p + 1
                accv_ref[p] += _dot(pts[ha], doms[ha]) + _dot(pts[hb], doms[hb])
                acck_ref[p] += _dot(dsts[ha], qms[ha]) + _dot(dsts[hb], qms[hb])
            for h in range(nh):
                dqt_ref[h * HEAD_DIM:(h + 1) * HEAD_DIM, pl.ds(q0, t)] += _dot(kts[h], dsts[h])

        tile(kj, True)

        def body(i, c):
            tile(i, False)
            return c
        lax.fori_loop(kj + 1, nq, body, 0)

        dc = jnp.zeros((t, LANES), F32)
        for h in range(nh):
            dc = jnp.where(lane == h, -jnp.sum(accd_ref[h], axis=1, keepdims=True), dc)
        dc_ref[...] = dc
        for p in range(npair):
            dv_ref[:, p * LANES:(p + 1) * LANES] = accv_ref[p].astype(BF16)
            dk_ref[:, p * LANES:(p + 1) * LANES] = acck_ref[p].astype(BF16)

    whole = lambda kj: (0, 0, 0)
    return pl.pallas_call(
        kern, name="fox_bwd",
        grid=(nq,),
        in_specs=[_resident((s, FOX_W), lambda kj: (0, 0)),
                  _resident((s, FOX_W), lambda kj: (0, 0)),
                  pl.BlockSpec((t, FOX_W), lambda kj: (kj, COL_FK // FOX_W)),
                  pl.BlockSpec((t, FOX_W), lambda kj: (kj, COL_FV // FOX_W)),
                  _resident((nh, 1, s), whole),
                  pl.BlockSpec((t, LANES), lambda kj: (kj, 0)),
                  _resident((nh, 1, s), whole),
                  _resident((nh, 1, s), whole)],
        out_specs=[_resident((FOX_W, s), lambda kj: (0, 0)),
                   pl.BlockSpec((t, FOX_W), lambda kj: (kj, 0)),
                   pl.BlockSpec((t, FOX_W), lambda kj: (kj, 0)),
                   pl.BlockSpec((t, LANES), lambda kj: (kj, 0)),
                   _resident((nh, 1, s), whole)],
        out_shape=[jax.ShapeDtypeStruct((FOX_W, s), F32),
                   jax.ShapeDtypeStruct((s, FOX_W), BF16),
                   jax.ShapeDtypeStruct((s, FOX_W), BF16),
                   jax.ShapeDtypeStruct((s, LANES), F32),
                   jax.ShapeDtypeStruct((nh, 1, s), F32)],
        scratch_shapes=[pltpu.VMEM((npair, t, LANES), F32),
                        pltpu.VMEM((npair, t, LANES), F32),
                        pltpu.VMEM((nh, t, LANES), F32)],
        compiler_params=_cparams(("arbitrary",)),
    )(qkv, do_bf, qkv, qkv, cum_t3, cum, lse_t3, delta_t3)


def _bucket_table():
    qi = np.arange(BLOCK)[:, None]
    kj = np.arange(2 * BLOCK)[None, :]
    rel = np.maximum(qi + BLOCK - kj, 0).astype(np.int32)
    max_exact = NUM_BUCKETS // 2
    relf = np.maximum(rel, 1).astype(np.float32)
    large = max_exact + (np.log(relf / np.float32(max_exact)) / np.float32(math.log(MAX_DISTANCE / max_exact))
                         * np.float32(NUM_BUCKETS - max_exact)).astype(np.int32)
    large = np.minimum(large, NUM_BUCKETS - 1)
    return np.where(rel < max_exact, rel, large).astype(np.int32)


SWA_LANES = SWA_GROUP * BLOCK


def _swa_bias(rel_bias, bucket_t):
    def kern(rb_ref, bk_ref, o_ref):
        bk = bk_ref[...]
        kj = lax.broadcasted_iota(jnp.int32, (2 * BLOCK, BLOCK), 0)
        qi = lax.broadcasted_iota(jnp.int32, (2 * BLOCK, BLOCK), 1)
        rel = qi + BLOCK - kj
        band = (rel >= 0) & (rel < BLOCK)
        masks = [band & (kj >= BLOCK), band]
        for h in range(SWA_HEADS):
            g, hh = divmod(h, SWA_GROUP)
            acc = jnp.zeros((2 * BLOCK, BLOCK), F32)
            for b in range(NUM_BUCKETS):
                acc = jnp.where(bk == b, rb_ref[b, h], acc)
            for first in range(2):
                o_ref[first, g, :, hh * BLOCK:(hh + 1) * BLOCK] = jnp.where(masks[first], acc, NEG)

    return pl.pallas_call(
        kern, name="swa_bias",
        in_specs=[pl.BlockSpec(memory_space=pltpu.SMEM),
                  pl.BlockSpec(memory_space=pltpu.VMEM)],
        out_specs=pl.BlockSpec(memory_space=pltpu.VMEM),
        out_shape=jax.ShapeDtypeStruct((2, SWA_KV_HEADS, 2 * BLOCK, SWA_LANES), F32),
        compiler_params=_cparams(),
    )(rel_bias, bucket_t)


def _swa_operands(q_ref, kp_ref, kc_ref, vp_ref, vc_ref):
    k = jnp.concatenate([kp_ref[...], kc_ref[...]], axis=0)
    v = jnp.concatenate([vp_ref[...], vc_ref[...]], axis=0)
    qt = (q_ref[...] * jnp.asarray(QK_SCALE, BF16)).astype(F32).T.astype(BF16)
    return k, v, _group_rows(qt)


def _group_rows(xt):
    zeros = jnp.zeros((HEAD_DIM, SWA_LANES), BF16)
    out = []
    for g in range(SWA_KV_HEADS):
        heads = [xt[(SWA_GROUP * g + hh) * HEAD_DIM:(SWA_GROUP * g + hh + 1) * HEAD_DIM, :] for hh in range(SWA_GROUP)]
        rows = jnp.concatenate(heads, axis=1)
        padded = jnp.concatenate([rows, zeros] if g == 0 else [zeros, rows], axis=0)
        out.append((rows, padded))
    return out


def _pairs_to_rows(cols_t):
    out = []
    for p in range(SWA_HEADS // 2):
        g, hh = divmod(2 * p, SWA_GROUP)
        pair = jnp.concatenate([cols_t[g][:, hh * BLOCK:(hh + 1) * BLOCK],
                                cols_t[g][:, (hh + 1) * BLOCK:(hh + 2) * BLOCK]], axis=0)
        out.append(pair.T)
    return jnp.concatenate(out, axis=1)


def _swa_fwd(qkv, bias_t, sink_rows):
    s = qkv.shape[0]
    nb = s // BLOCK

    def kern(q_ref, kp_ref, kc_ref, vp_ref, vc_ref, bias_ref, sink_ref, o_ref, lse_ref):
        n = pl.program_id(0)
        table = jnp.minimum(n, 1)
        k, v, qts = _swa_operands(q_ref, kp_ref, kc_ref, vp_ref, vc_ref)
        vt = v.astype(F32).T.astype(BF16)
        us = [_dot(k, qts[g][1]) + bias_ref[table, g] for g in range(SWA_KV_HEADS)]
        outs = []
        for g in range(SWA_KV_HEADS):
            sk = sink_ref[g]
            m = jnp.maximum(jnp.max(us[g], axis=0, keepdims=True), sk)
            p = jnp.exp(us[g] - m)
            l = jnp.sum(p, axis=0, keepdims=True) + jnp.exp(sk - m)
            lse_ref[0, g] = m + jnp.log(l)
            outs.append(_dot(vt[g * HEAD_DIM:(g + 1) * HEAD_DIM, :], (p * (1.0 / l)).astype(BF16)))
        o_ref[...] = _pairs_to_rows(outs)

    cq, ck, cv = COL_SQ // SWA_W, COL_SK // LANES, COL_SV // LANES
    prev = lambda n: jnp.maximum(n - 1, 0)
    return pl.pallas_call(
        kern, name="swa_fwd",
        grid=(nb,),
        in_specs=[pl.BlockSpec((BLOCK, SWA_W), lambda n: (n, cq)),
                  pl.BlockSpec((BLOCK, LANES), lambda n: (prev(n), ck)),
                  pl.BlockSpec((BLOCK, LANES), lambda n: (n, ck)),
                  pl.BlockSpec((BLOCK, LANES), lambda n: (prev(n), cv)),
                  pl.BlockSpec((BLOCK, LANES), lambda n: (n, cv)),
                  _resident((2, SWA_KV_HEADS, 2 * BLOCK, SWA_LANES), lambda n: (0, 0, 0, 0)),
                  _resident((SWA_KV_HEADS, 1, SWA_LANES), lambda n: (0, 0, 0))],
        out_specs=[pl.BlockSpec((BLOCK, SWA_W), lambda n: (n, 0)),
                   pl.BlockSpec((1, SWA_KV_HEADS, 1, SWA_LANES), lambda n: (n, 0, 0, 0))],
        out_shape=[jax.ShapeDtypeStruct((s, SWA_W), F32),
                   jax.ShapeDtypeStruct((nb, SWA_KV_HEADS, 1, SWA_LANES), F32)],
        compiler_params=_cparams(("parallel",)),
    )(qkv, qkv, qkv, qkv, qkv, bias_t, sink_rows)


def _swa_bwd(qkv, do_bf, delta_rows, lse, bias_t, sink_rows, bucket_t):
    s = qkv.shape[0]
    nb = s // BLOCK

    def kern(q_ref, kp_ref, kc_ref, vp_ref, vc_ref, do_ref, dl_ref, lse_ref, bias_ref, sink_ref, bk_ref,
             dq_ref, dk_ref, dv_ref, grb_ref, gsk_ref, dbias_ref, ck_ref, cv_ref, sk_ref):
        n = pl.program_id(0)

        @pl.when(n == 0)
        def _():
            dbias_ref[...] = jnp.zeros_like(dbias_ref)
            ck_ref[...] = jnp.zeros_like(ck_ref)
            cv_ref[...] = jnp.zeros_like(cv_ref)
            sk_ref[...] = jnp.zeros_like(sk_ref)

        @pl.when(n < nb)
        def _():
            table = jnp.minimum(n, 1)
            k, v, qts = _swa_operands(q_ref, kp_ref, kc_ref, vp_ref, vc_ref)
            dots = _group_rows(do_ref[...].astype(F32).T.astype(BF16))
            kt = (k.astype(F32).T * QK_SCALE).astype(BF16)
            groups = range(SWA_KV_HEADS)
            sts = [_dot(k, qts[g][1]) for g in groups]
            dps = [_dot(v, dots[g][1]) for g in groups]
            ps, dss = [], []
            for g in groups:
                lse_g = lse_ref[0, g]
                dlt = dl_ref[0, g]
                p = jnp.exp(sts[g] + bias_ref[table, g] - lse_g)
                ds = p * (dps[g] - dlt)
                dbias_ref[g] += ds
                sk_ref[g] += -jnp.exp(sink_ref[g] - lse_g) * dlt
                ps.append(p.astype(BF16))
                dss.append(ds.astype(BF16))
            dvt = jnp.concatenate([_dot_nt(dots[g][0], ps[g]) for g in groups], axis=0)
            dkt = jnp.concatenate([_dot_nt(qts[g][0], dss[g]) for g in groups], axis=0)
            dqts = [_dot(kt[g * HEAD_DIM:(g + 1) * HEAD_DIM, :], dss[g]) for g in groups]
            dq_ref[...] = _pairs_to_rows(dqts).astype(BF16)
            dk2 = dkt.T
            dv2 = dvt.T
            dk_ref[...] = (ck_ref[...] + dk2[:BLOCK]).astype(BF16)
            dv_ref[...] = (cv_ref[...] + dv2[:BLOCK]).astype(BF16)
            ck_ref[...] = dk2[BLOCK:]
            cv_ref[...] = dv2[BLOCK:]

        @pl.when(n == nb)
        def _():
            dk_ref[...] = ck_ref[...].astype(BF16)
            dv_ref[...] = cv_ref[...].astype(BF16)
            bk = bk_ref[...]
            lane = lax.broadcasted_iota(jnp.int32, (8, LANES), 1)
            rowi = lax.broadcasted_iota(jnp.int32, (NUM_BUCKETS, LANES), 0)
            lanei = lax.broadcasted_iota(jnp.int32, (NUM_BUCKETS, LANES), 1)
            out = jnp.zeros((NUM_BUCKETS, LANES), F32)
            gsk = jnp.zeros((8, LANES), F32)
            for h in range(SWA_HEADS):
                g, hh = divmod(h, SWA_GROUP)
                cols = slice(hh * BLOCK, (hh + 1) * BLOCK)
                gsk = jnp.where(lane == h, jnp.sum(sk_ref[g][:, cols]), gsk)
                db = dbias_ref[g][:, cols]
                for b in range(NUM_BUCKETS):
                    val = jnp.sum(jnp.where(bk == b, db, 0.0))
                    out = jnp.where((rowi == b) & (lanei == h), val, out)
            grb_ref[...] = out
            gsk_ref[...] = gsk

    cq, ck, cv = COL_SQ // SWA_W, COL_SK // LANES, COL_SV // LANES
    cur = lambda n: jnp.minimum(n, nb - 1)
    prev = lambda n: jnp.maximum(jnp.minimum(n, nb - 1) - 1, 0)
    kout = lambda n: jnp.maximum(n - 1, 0)
    return pl.pallas_call(
        kern, name="swa_bwd",
        grid=(nb + 1,),
        in_specs=[pl.BlockSpec((BLOCK, SWA_W), lambda n: (cur(n), cq)),
                  pl.BlockSpec((BLOCK, LANES), lambda n: (prev(n), ck)),
                  pl.BlockSpec((BLOCK, LANES), lambda n: (cur(n), ck)),
                  pl.BlockSpec((BLOCK, LANES), lambda n: (prev(n), cv)),
                  pl.BlockSpec((BLOCK, LANES), lambda n: (cur(n), cv)),
                  pl.BlockSpec((BLOCK, SWA_W), lambda n: (cur(n), 1)),
                  pl.BlockSpec((1, SWA_KV_HEADS, 1, SWA_LANES), lambda n: (cur(n), 0, 0, 0)),
                  pl.BlockSpec((1, SWA_KV_HEADS, 1, SWA_LANES), lambda n: (cur(n), 0, 0, 0)),
                  _resident((2, SWA_KV_HEADS, 2 * BLOCK, SWA_LANES), lambda n: (0, 0, 0, 0)),
                  _resident((SWA_KV_HEADS, 1, SWA_LANES), lambda n: (0, 0, 0)),
                  _resident((2 * BLOCK, BLOCK), lambda n: (0, 0))],
        out_specs=[pl.BlockSpec((BLOCK, SWA_W), lambda n: (cur(n), 0)),
                   pl.BlockSpec((BLOCK, LANES), lambda n: (kout(n), 0)),
                   pl.BlockSpec((BLOCK, LANES), lambda n: (kout(n), 0)),
                   pl.BlockSpec((NUM_BUCKETS, LANES), lambda n: (0, 0)),
                   pl.BlockSpec((8, LANES), lambda n: (0, 0))],
        out_shape=[jax.ShapeDtypeStruct((s, SWA_W), BF16),
                   jax.ShapeDtypeStruct((s, LANES), BF16),
                   jax.ShapeDtypeStruct((s, LANES), BF16),
                   jax.ShapeDtypeStruct((NUM_BUCKETS, LANES), F32),
                   jax.ShapeDtypeStruct((8, LANES), F32)],
        scratch_shapes=[pltpu.VMEM((SWA_KV_HEADS, 2 * BLOCK, SWA_LANES), F32),
                        pltpu.VMEM((BLOCK, LANES), F32),
                        pltpu.VMEM((BLOCK, LANES), F32),
                        pltpu.VMEM((SWA_KV_HEADS, 1, SWA_LANES), F32)],
        compiler_params=_cparams(("arbitrary",)),
    )(qkv, qkv, qkv, qkv, qkv, do_bf, delta_rows, lse, bias_t, sink_rows, bucket_t)


def _post(x, target, o_fox, o_swa, z, w_o, ln_g, ln_b):
    s = x.shape[0]
    tm = min(256, s)
    nt = s // tm

    def kern(x_ref, t_ref, of_ref, os_ref, z_ref, w_ref, g_ref, b_ref,
             loss_ref, dh_ref, gwo_ref, do_ref, dz_ref, dl_ref, gg_ref, gb_ref, lacc_ref):
        step = pl.program_id(0)

        @pl.when(step == 0)
        def _():
            lacc_ref[...] = jnp.zeros_like(lacc_ref)
            gg_ref[...] = jnp.zeros_like(gg_ref)
            gwo_ref[...] = jnp.zeros_like(gwo_ref)
            gb_ref[...] = jnp.zeros_like(gb_ref)

        o = jnp.concatenate([of_ref[...], os_ref[...]], axis=1)
        zz = z_ref[...]
        sig = 1.0 / (1.0 + jnp.exp(-zz))
        silu = zz * sig
        mixed32 = o * silu
        mixed = mixed32.astype(BF16)
        w = w_ref[...]
        h = ALPHA * x_ref[...] + _dot(mixed, w)
        mu = jnp.mean(h, axis=1, keepdims=True)
        hc = h - mu
        var = jnp.mean(hc * hc, axis=1, keepdims=True)
        rstd = lax.rsqrt(var + LN_EPS)
        xhat = hc * rstd
        g = g_ref[...]
        err = xhat * g + b_ref[...] - t_ref[...]
        lacc_ref[...] += jnp.broadcast_to(jnp.sum(err * err, axis=0, keepdims=True), lacc_ref.shape)
        dout = err * (1.0 / D_MODEL)
        gg_ref[...] += jnp.broadcast_to(jnp.sum(dout * xhat, axis=0, keepdims=True), gg_ref.shape)
        gb_ref[...] += jnp.broadcast_to(jnp.sum(dout, axis=0, keepdims=True), gb_ref.shape)
        dxh = dout * g
        m1 = jnp.mean(dxh, axis=1, keepdims=True)
        m2 = jnp.mean(dxh * xhat, axis=1, keepdims=True)
        dh = rstd * (dxh - m1 - xhat * m2)
        dh_ref[...] = dh
        dy = dh.astype(BF16)
        gwo_ref[...] += _dot(mixed32.T.astype(BF16), dy)
        dmix = _dot_nt(dy, w)
        do = dmix * silu
        do_ref[...] = do.astype(BF16)
        dz_ref[...] = (dmix * o * (sig * (1.0 + zz * (1.0 - sig)))).astype(BF16)
        r = lax.broadcasted_iota(jnp.int32, (D_MODEL, LANES), 0) // HEAD_DIM
        c = lax.broadcasted_iota(jnp.int32, (D_MODEL, LANES), 1)
        pick = jnp.where(r == c, 1.0, 0.0).astype(BF16)
        dl_ref[...] = _exact_dot(pick, do * o, False)

        @pl.when(step == nt - 1)
        def _():
            tot = jnp.sum(lacc_ref[0:1, :]) * (0.5 / D_MODEL)
            loss_ref[...] = jnp.broadcast_to(tot, loss_ref.shape)

    row = lambda i: (i, 0)
    fixed = lambda i: (0, 0)
    wide = pl.BlockSpec((tm, D_MODEL), row)
    half = pl.BlockSpec((tm, FOX_W), row)
    return pl.pallas_call(
        kern, name="post",
        grid=(nt,),
        in_specs=[wide, wide, half, half, wide,
                  pl.BlockSpec((D_MODEL, D_MODEL), fixed),
                  pl.BlockSpec((1, D_MODEL), fixed),
                  pl.BlockSpec((1, D_MODEL), fixed)],
        out_specs=[pl.BlockSpec((8, LANES), fixed), wide,
                   _resident((D_MODEL, D_MODEL), fixed), wide, wide,
                   pl.BlockSpec((tm, LANES), row),
                   pl.BlockSpec((8, D_MODEL), fixed), pl.BlockSpec((8, D_MODEL), fixed)],
        out_shape=[jax.ShapeDtypeStruct((8, LANES), F32),
                   jax.ShapeDtypeStruct((s, D_MODEL), F32),
                   jax.ShapeDtypeStruct((D_MODEL, D_MODEL), F32),
                   jax.ShapeDtypeStruct((s, D_MODEL), BF16),
                   jax.ShapeDtypeStruct((s, D_MODEL), BF16),
                   jax.ShapeDtypeStruct((s, LANES), F32),
                   jax.ShapeDtypeStruct((8, D_MODEL), F32),
                   jax.ShapeDtypeStruct((8, D_MODEL), F32)],
        scratch_shapes=[pltpu.VMEM((8, D_MODEL), F32)],
        compiler_params=_cparams(("arbitrary",)),
    )(x, target, o_fox, o_swa, z, w_o, ln_g, ln_b)


def _adamw_math(w, g, m, v):
    m = ADAM_B1 * m + (1.0 - ADAM_B1) * g
    v = ADAM_B2 * v + (1.0 - ADAM_B2) * (g * g)
    m_hat = m / (1.0 - ADAM_B1 ** ADAM_STEP)
    v_hat = v / (1.0 - ADAM_B2 ** ADAM_STEP)
    delta = -ADAM_LR * (m_hat / (jnp.sqrt(v_hat) + ADAM_EPS) + ADAM_WD * w)
    return delta, m, v


def _adamw(w, g, m, v, *, name):
    r, c = w.shape
    tr = min(256, r)

    def kern(w_ref, g_ref, m_ref, v_ref, d_ref, mo_ref, vo_ref):
        d, mn, vn = _adamw_math(w_ref[...], g_ref[...], m_ref[...], v_ref[...])
        d_ref[...] = d
        mo_ref[...] = mn
        vo_ref[...] = vn

    blk = pl.BlockSpec((tr, c), lambda i: (i, 0))
    sds = jax.ShapeDtypeStruct((r, c), F32)
    return pl.pallas_call(
        kern, name=name,
        grid=(r // tr,),
        in_specs=[blk, blk, blk, blk],
        out_specs=[blk, blk, blk],
        out_shape=[sds, sds, sds],
        compiler_params=_cparams(("parallel",)),
    )(w, g, m, v)


def _adamw_cols(w, g, m, v, *, name):
    c, _, r = w.shape
    tc = 139
    assert c % tc == 0

    def kern(w_ref, g_ref, m_ref, v_ref, d_ref, mo_ref, vo_ref):
        d, mn, vn = _adamw_math(w_ref[...], g_ref[...], m_ref[...], v_ref[...])
        d_ref[...] = d
        mo_ref[...] = mn
        vo_ref[...] = vn

    blk = pl.BlockSpec((tc, 1, r), lambda i: (i, 0, 0))
    sds = jax.ShapeDtypeStruct((c, 1, r), F32)
    return pl.pallas_call(
        kern, name=name,
        grid=(c // tc,),
        in_specs=[blk, blk, blk, blk],
        out_specs=[blk, blk, blk],
        out_shape=[sds, sds, sds],
        compiler_params=_cparams(("parallel",)),
    )(w, g, m, v)


def _position():
    x, y, c = lax.axis_index("x"), lax.axis_index("y"), lax.axis_index("c")
    chips = [(1 - x, y), (x, 1 - y), (1 - x, 1 - y)]
    return x, y, c, chips


def _chip_index(cx, cy):
    return 2 * cx + cy


def _gather_weights(w_in_bf, w_o_bf):
    shards = (w_in_bf, w_o_bf)
    n_arr = len(shards)

    def kern(*refs):
        ins, outs = refs[:n_arr], refs[n_arr:2 * n_arr]
        send_sems, recv_sems, local_sems = refs[2 * n_arr:]
        x, y, c, chips = _position()
        me = _chip_index(x, y)
        sibling = (x, y, 1 - c)

        local = [pltpu.make_async_copy(ins[a], outs[a].at[me], local_sems.at[a]) for a in range(n_arr)]
        for cp in local:
            cp.start()

        def half(ref, a):
            rows = shards[a].shape[0] // 2
            return ref.at[pl.ds(c * rows, rows), :]

        def copy(a, k, src, slot, to):
            return pltpu.make_async_remote_copy(
                src_ref=src, dst_ref=half(outs[a].at[slot], a),
                send_sem=send_sems.at[a * 6 + k], recv_sem=recv_sems.at[a * 6 + k],
                device_id=to, device_id_type=MESH)

        first = [copy(a, j, half(ins[a], a), me, (*chip, c)) for a in range(n_arr) for j, chip in enumerate(chips)]
        for cp in first:
            cp.start()
        passed = []
        for a in range(n_arr):
            for j, chip in enumerate(chips):
                slot = _chip_index(*chip)
                copy(a, j, half(ins[a], a), slot, (*chip, c)).wait_recv()
                fwd = copy(a, 3 + j, half(outs[a].at[slot], a), slot, sibling)
                fwd.start()
                passed.append(fwd)
        for a in range(n_arr):
            for j, chip in enumerate(chips):
                slot = _chip_index(*chip)
                rows = shards[a].shape[0] // 2
                dst = outs[a].at[slot].at[pl.ds((1 - c) * rows, rows), :]
                pltpu.make_async_remote_copy(
                    src_ref=dst, dst_ref=dst, send_sem=send_sems.at[a * 6 + 3 + j],
                    recv_sem=recv_sems.at[a * 6 + 3 + j], device_id=sibling, device_id_type=MESH).wait_recv()
        for cp in first + passed:
            cp.wait_send()
        for cp in local:
            cp.wait()

    vmem = pl.BlockSpec(memory_space=pltpu.VMEM)
    return pl.pallas_call(
        kern, name="gather_weights",
        in_specs=[vmem] * n_arr,
        out_specs=[vmem] * n_arr,
        out_shape=[jax.ShapeDtypeStruct((N_CHIPS,) + w.shape, w.dtype) for w in shards],
        scratch_shapes=[pltpu.SemaphoreType.DMA((6 * n_arr,)),
                        pltpu.SemaphoreType.DMA((6 * n_arr,)),
                        pltpu.SemaphoreType.DMA((n_arr,))],
        compiler_params=_cparams(),
    )(*shards)


def _pair_reduce(grads):
    n_arr = len(grads)
    chunk = 128

    def kern(*refs):
        ins = refs[:n_arr]
        outs = refs[n_arr:2 * n_arr]
        gots = refs[2 * n_arr:3 * n_arr]
        send_sems, recv_sems = refs[3 * n_arr:]
        x, y, c, _ = _position()
        sibling = (x, y, 1 - c)
        copies = []
        for a in range(n_arr):
            rows = grads[a].shape[1] // 2
            copies.append(pltpu.make_async_remote_copy(
                src_ref=ins[a].at[:, pl.ds((1 - c) * rows, rows), :], dst_ref=gots[a],
                send_sem=send_sems.at[a], recv_sem=recv_sems.at[a], device_id=sibling, device_id_type=MESH))
        for cp in copies:
            cp.start()
        for a in range(n_arr):
            copies[a].wait()
            rows = grads[a].shape[1] // 2
            for j in range(N_CHIPS):
                for r0 in range(0, rows, chunk):
                    mine = ins[a][j, pl.ds(pl.multiple_of(c * rows + r0, chunk), chunk), :]
                    outs[a][j, r0:r0 + chunk, :] = (mine + gots[a][j, r0:r0 + chunk, :]).astype(BF16)

    vmem = pl.BlockSpec(memory_space=pltpu.VMEM)
    half = [(N_CHIPS, g.shape[1] // 2, g.shape[2]) for g in grads]
    return pl.pallas_call(
        kern, name="pair_reduce",
        in_specs=[vmem] * n_arr,
        out_specs=[vmem] * n_arr,
        out_shape=[jax.ShapeDtypeStruct(h, BF16) for h in half],
        scratch_shapes=[pltpu.VMEM(h, F32) for h in half]
        + [pltpu.SemaphoreType.DMA((n_arr,)), pltpu.SemaphoreType.DMA((n_arr,))],
        compiler_params=_cparams(),
    )(*grads)


def _chip_reduce(parts):
    n_arr = len(parts)
    chunk = 128

    def kern(*refs):
        ins = refs[:n_arr]
        outs = refs[n_arr:2 * n_arr]
        slabs = refs[2 * n_arr:3 * n_arr]
        send_sems, recv_sems, local_sems = refs[3 * n_arr:]
        x, y, c, chips = _position()
        me = _chip_index(x, y)
        local = [pltpu.make_async_copy(ins[a].at[me], slabs[a].at[me], local_sems.at[a]) for a in range(n_arr)]
        for cp in local:
            cp.start()
        sends = []
        for a in range(n_arr):
            for j, chip in enumerate(chips):
                sends.append(pltpu.make_async_remote_copy(
                    src_ref=ins[a].at[_chip_index(*chip)], dst_ref=slabs[a].at[me],
                    send_sem=send_sems.at[a * 3 + j], recv_sem=recv_sems.at[a * 3 + j],
                    device_id=(*chip, c), device_id_type=MESH))
        for cp in sends:
            cp.start()
        for a in range(n_arr):
            for j, chip in enumerate(chips):
                slot = slabs[a].at[_chip_index(*chip)]
                pltpu.make_async_remote_copy(
                    src_ref=slot, dst_ref=slot, send_sem=send_sems.at[a * 3 + j],
                    recv_sem=recv_sems.at[a * 3 + j], device_id=(*chip, c), device_id_type=MESH).wait_recv()
        for cp in sends:
            cp.wait_send()
        for cp in local:
            cp.wait()
        for a in range(n_arr):
            for r0 in range(0, parts[a].shape[1], chunk):
                f = lambda j: slabs[a][j, r0:r0 + chunk, :].astype(F32)
                outs[a][r0:r0 + chunk, :] = ((f(0) + f(1)) + f(2)) + f(3)

    vmem = pl.BlockSpec(memory_space=pltpu.VMEM)
    return pl.pallas_call(
        kern, name="chip_reduce",
        in_specs=[vmem] * n_arr,
        out_specs=[vmem] * n_arr,
        out_shape=[jax.ShapeDtypeStruct(p.shape[1:], F32) for p in parts],
        scratch_shapes=[pltpu.VMEM(p.shape, BF16) for p in parts]
        + [pltpu.SemaphoreType.DMA((3 * n_arr,)),
           pltpu.SemaphoreType.DMA((3 * n_arr,)),
           pltpu.SemaphoreType.DMA((n_arr,))],
        compiler_params=_cparams(),
    )(*parts)


def _join_halves(halves):
    n_arr = len(halves)

    def kern(*refs):
        ins = refs[:n_arr]
        outs = refs[n_arr:2 * n_arr]
        send_sems, recv_sems, local_sems = refs[2 * n_arr:]
        x, y, c, _ = _position()
        sibling = (x, y, 1 - c)
        local, remote = [], []
        for a in range(n_arr):
            rows = halves[a].shape[0]
            mine = outs[a].at[pl.ds(c * rows, rows), :]
            local.append(pltpu.make_async_copy(ins[a], mine, local_sems.at[a]))
            remote.append(pltpu.make_async_remote_copy(
                src_ref=ins[a], dst_ref=mine, send_sem=send_sems.at[a], recv_sem=recv_sems.at[a],
                device_id=sibling, device_id_type=MESH))
        for cp in local + remote:
            cp.start()
        for a in range(n_arr):
            rows = halves[a].shape[0]
            theirs = outs[a].at[pl.ds((1 - c) * rows, rows), :]
            pltpu.make_async_remote_copy(
                src_ref=theirs, dst_ref=theirs, send_sem=send_sems.at[a], recv_sem=recv_sems.at[a],
                device_id=sibling, device_id_type=MESH).wait_recv()
        for cp in remote:
            cp.wait_send()
        for cp in local:
            cp.wait()

    vmem = pl.BlockSpec(memory_space=pltpu.VMEM)
    return pl.pallas_call(
        kern, name="join_halves",
        in_specs=[vmem] * n_arr,
        out_specs=[vmem] * n_arr,
        out_shape=[jax.ShapeDtypeStruct((2 * h.shape[0], h.shape[1]), F32) for h in halves],
        scratch_shapes=[pltpu.SemaphoreType.DMA((n_arr,)),
                        pltpu.SemaphoreType.DMA((n_arr,)),
                        pltpu.SemaphoreType.DMA((n_arr,))],
        compiler_params=_cparams(),
    )(*halves)


def _small_allreduce_adamw(g, w, m, v):
    def kern(g_ref, w_ref, m_ref, v_ref, gs_ref, d_ref, mo_ref, vo_ref, buf_ref, send_sems, recv_sems):
        x, y, c, _ = _position()
        me = 4 * x + 2 * y + c
        buf_ref[me] = g_ref[...]
        peers = [(x, y, 1 - c)] + [(px, py, pc) for px, py in _position()[3] for pc in (c, 1 - c)]
        sends = []
        for k, peer in enumerate(peers):
            sends.append(pltpu.make_async_remote_copy(
                src_ref=g_ref, dst_ref=buf_ref.at[me], send_sem=send_sems.at[k], recv_sem=recv_sems.at[k],
                device_id=peer, device_id_type=MESH))
        for cp in sends:
            cp.start()
        for k, (px, py, pc) in enumerate(peers):
            slot = buf_ref.at[4 * px + 2 * py + pc]
            pltpu.make_async_remote_copy(
                src_ref=slot, dst_ref=slot, send_sem=send_sems.at[k], recv_sem=recv_sems.at[k],
                device_id=(px, py, pc), device_id_type=MESH).wait_recv()
        for cp in sends:
            cp.wait_send()
        tot = buf_ref[0]
        for d in range(1, N_DEV):
            tot = tot + buf_ref[d]
        gs_ref[...] = tot
        delta, mn, vn = _adamw_math(w_ref[...], tot, m_ref[...], v_ref[...])
        d_ref[...] = delta
        mo_ref[...] = mn
        vo_ref[...] = vn

    vm = pl.BlockSpec(memory_space=pltpu.VMEM)
    sds = jax.ShapeDtypeStruct((SMALL_ROWS, LANES), F32)
    return pl.pallas_call(
        kern, name="small_allreduce_adamw",
        in_specs=[vm] * 4,
        out_specs=[vm] * 4,
        out_shape=[sds] * 4,
        scratch_shapes=[pltpu.VMEM((N_DEV, SMALL_ROWS, LANES), F32),
                        pltpu.SemaphoreType.DMA((N_DEV - 1,)),
                        pltpu.SemaphoreType.DMA((N_DEV - 1,))],
    )(g, w, m, v)


def _to_padded_cols(w):
    pad = jnp.zeros((w.shape[0], N_C - FOX_HEADS), w.dtype)
    return jnp.concatenate([w[:, 0:1536], w[:, 2056:2824], w[:, 1536:1544], pad,
                            w[:, 1544:2056], w[:, 2824:3336]], axis=1)


def _from_padded_cols(g):
    return jnp.concatenate([g[:, 0:1536], g[:, OFF_C:OFF_C + FOX_HEADS], g[:, OFF_B:OFF_B + FOX_W],
                            g[:, 1536:N_A], g[:, OFF_B + FOX_W:N_PAD]], axis=1)


def _pack_small(b_f, rel_bias, sink, ln_g, ln_b):
    row = lambda v: jnp.pad(v.reshape(1, -1), ((0, 0), (0, LANES - v.size)))
    return jnp.concatenate([ln_g.reshape(8, LANES), ln_b.reshape(8, LANES), rel_bias.reshape(2, LANES),
                            row(b_f), row(sink), jnp.zeros((4, LANES), F32)], axis=0)


def _unpack_small(p):
    ln_g = p[0:8].reshape(1, D_MODEL)
    ln_b = p[8:16].reshape(1, D_MODEL)
    rel_bias = p[16:18].reshape(NUM_BUCKETS, SWA_HEADS)
    b_f = p[18:19, :FOX_HEADS]
    sink = p[19:20, :SWA_HEADS]
    return b_f, rel_bias, sink, ln_g, ln_b


def _fox_rows(a):
    return a[:, :FOX_HEADS].T.reshape(FOX_HEADS, 1, a.shape[0])


def kernel(x, w_in, b_f, rel_bias, sink, w_o, ln_g, ln_b, loss_target, m_w_in, m_b_f, m_rel_bias, m_sink, m_w_o, m_ln_g, m_ln_b, v_w_in, v_b_f, v_rel_bias, v_sink, v_w_o, v_ln_g, v_ln_b):
    x2 = x[0]
    tgt = loss_target[0]
    s = x2.shape[0]
    w_in2, w_o2 = w_in[0], w_o[0]

    shard_cols = D_IN // N_CHIPS
    col_pad = ((0, 0), (0, SHARD_PAD - shard_cols))
    w_in_all, w_o_all = _gather_weights(jnp.pad(w_in2.astype(BF16), col_pad), w_o2.astype(BF16))
    w_full = jnp.concatenate([w_in_all[j, :, :shard_cols] for j in range(N_CHIPS)], axis=1)
    w_pad = _to_padded_cols(w_full)
    w_o_full = w_o_all.reshape(D_MODEL, D_MODEL)

    qkv, ffp, z, xt, vt = _project(x2, w_pad)
    bfp = jnp.pad(b_f, ((0, 0), (0, LANES - FOX_HEADS)))
    cum = _cum_fwd(ffp, bfp)
    cum_t3 = _fox_rows(cum)
    o_fox, lse_t3 = _fox_fwd(qkv, vt, cum_t3, cum)
    bucket_t = jnp.asarray(_bucket_table().T)
    bias_t = _swa_bias(rel_bias, bucket_t)
    sink_rows = jnp.repeat(sink.reshape(SWA_KV_HEADS, SWA_GROUP, 1), BLOCK, axis=2).reshape(SWA_KV_HEADS, 1, SWA_LANES)
    o_swa, lse_swa = _swa_fwd(qkv, bias_t, sink_rows)

    loss8, dh, grad_w_o_full, do_bf, dz, delta, gg8, gb8 = _post(
        x2, tgt, o_fox, o_swa, z, w_o_full, ln_g, ln_b)

    delta_t3 = _fox_rows(delta)
    dqt_fox, dk_fox, dv_fox, dcum_k, dcum_q = _fox_bwd(qkv, do_bf, cum_t3, cum, lse_t3, delta_t3)
    dcum_q = jnp.pad(dcum_q.reshape(FOX_HEADS, s).T, ((0, 0), (0, LANES - FOX_HEADS)))
    dff, gbf8 = _cum_bwd(dcum_k, dcum_q, ffp, bfp)
    delta_rows = (delta[:, FOX_HEADS:FOX_HEADS + SWA_HEADS].reshape(s // BLOCK, BLOCK, SWA_KV_HEADS, SWA_GROUP)
                  .transpose(0, 2, 3, 1).reshape(s // BLOCK, SWA_KV_HEADS, 1, SWA_LANES))
    dq_swa, dk_swa, dv_swa, grb, gsk8 = _swa_bwd(qkv, do_bf, delta_rows, lse_swa, bias_t, sink_rows, bucket_t)

    dq_fox = dqt_fox.T.astype(BF16)
    d_misc = jnp.concatenate([dk_swa, dv_swa, dff], axis=1)
    pieces = [dq_fox, dk_fox, dv_fox, dq_swa, d_misc, dz]
    grad_x = _grad_x_matmul(pieces, w_pad, dh, tm=512, tn=D_MODEL, name="grad_x")
    blocks = [(p, 0) for p in pieces[:-1]] + [(dz, 0), (dz, 1)]
    grad_w_pad = _grad_w_matmul(xt, blocks, tk=1024, name="grad_w_in")
    grad_w_in_full = _from_padded_cols(grad_w_pad)

    g_in4 = jnp.stack([jnp.pad(grad_w_in_full[:, j * shard_cols:(j + 1) * shard_cols], col_pad)
                       for j in range(N_CHIPS)])
    g_o4 = grad_w_o_full.reshape(N_CHIPS, D_MODEL // N_CHIPS, D_MODEL)
    g_w_in, g_w_o = _join_halves(_chip_reduce(_pair_reduce([g_in4, g_o4])))
    g_w_in = g_w_in[:, :shard_cols]

    cols_first = lambda a: jnp.transpose(a, (2, 0, 1))
    rows_first = lambda a: jnp.transpose(a, (1, 2, 0))
    d_w_in, nm_w_in, nv_w_in = [rows_first(a) for a in _adamw_cols(
        cols_first(w_in), cols_first(g_w_in[None]), cols_first(m_w_in), cols_first(v_w_in), name="adamw_w_in")]
    d_w_o, nm_w_o, nv_w_o = _adamw(w_o2, g_w_o, m_w_o[0], v_w_o[0], name="adamw_w_o")

    g_small = _pack_small(gbf8[0:1, :FOX_HEADS], grb[:, :SWA_HEADS], gsk8[0:1, :SWA_HEADS], gg8[0:1], gb8[0:1])
    g_small = g_small.at[LOSS_ROW, 0].set(loss8[0, 0])
    w_small = _pack_small(b_f, rel_bias, sink, ln_g, ln_b)
    m_small = _pack_small(m_b_f, m_rel_bias, m_sink, m_ln_g, m_ln_b)
    v_small = _pack_small(v_b_f, v_rel_bias, v_sink, v_ln_g, v_ln_b)
    gs, ds, ms, vs = _small_allreduce_adamw(g_small, w_small, m_small, v_small)
    loss = gs[LOSS_ROW, 0]
    g_bf, g_rb, g_sk, g_lg, g_lb = _unpack_small(gs)
    d_bf, d_rb, d_sk, d_lg, d_lb = _unpack_small(ds)
    m_bf, m_rb, m_sk, m_lg, m_lb = _unpack_small(ms)
    v_bf, v_rb, v_sk, v_lg, v_lb = _unpack_small(vs)

    e = lambda a: a[None]
    return (loss, e(grad_x),
            e(g_w_in), g_bf, g_rb, g_sk, e(g_w_o), g_lg, g_lb,
            d_w_in, d_bf, d_rb, d_sk, e(d_w_o), d_lg, d_lb,
            nm_w_in, m_bf, m_rb, m_sk, e(nm_w_o), m_lg, m_lb,
            nv_w_in, v_bf, v_rb, v_sk, e(nv_w_o), v_lg, v_lb)
```

```python
import functools
import math

import numpy as np
import jax
import jax.numpy as jnp
from jax import lax
from jax.experimental import pallas as pl
from jax.experimental.pallas import tpu as pltpu

F32 = jnp.float32
BF16 = jnp.bfloat16

D_MODEL = 1024
HEAD_DIM = 64
FOX_HEADS = 8
SWA_HEADS = 8
SWA_KV_HEADS = 2
SWA_GROUP = 4
FOX_W = 512
SWA_W = 512
SWA_KV_W = 128
BLOCK = 128
NUM_BUCKETS = 32
MAX_DISTANCE = 128
LN_EPS = 1e-5
NEG = -1e30
ALPHA = 2.0 ** 0.25
QK_SCALE = 0.125

ADAM_LR = 0.001
ADAM_B1 = 0.9
ADAM_B2 = 0.999
ADAM_EPS = 1e-08
ADAM_WD = 0.01
ADAM_STEP = 10

D_IN = 3336
SHARD_PAD = 896
N_A = 2304
N_C = 256
N_B = 1024
OFF_C = N_A
OFF_B = N_A + N_C
N_PAD = N_A + N_C + N_B
COL_FK, COL_FV, COL_SQ, COL_SK, COL_SV = 512, 1024, 1536, 2048, 2176

LANES = 128
FOX_T = 256
FOX_REF = 512
SUM_ROWS = 16
VMEM_LIMIT = 56 * 1024 * 1024

MESH = pl.DeviceIdType.MESH
N_CHIPS = 4
N_DEV = 8
SMALL_ROWS = 56


def _cparams(sem=None):
    return pltpu.CompilerParams(dimension_semantics=sem, vmem_limit_bytes=VMEM_LIMIT)


def _split3(x):
    hi = x.astype(BF16)
    r = x - hi.astype(F32)
    mid = r.astype(BF16)
    lo = (r - mid.astype(F32)).astype(BF16)
    return hi, mid, lo


def _dot(a, b):
    return jnp.dot(a, b, preferred_element_type=F32)


def _dot_nt(a, b):
    return lax.dot_general(a, b, (((1,), (1,)), ((), ())), preferred_element_type=F32)


def _project(x, w_pad):
    s, k = x.shape
    tm = 512
    chunk = 512

    def kern(x_ref, w_ref, qkv_ref, ff_ref, z_ref, xt_ref, vt_ref):
        xf = x_ref[...]
        xb = xf.astype(BF16)
        xt_ref[...] = xf.T.astype(BF16)
        for c0 in range(0, N_A, chunk):
            width = min(chunk, N_A - c0)
            res = _dot(xb, w_ref[:, c0:c0 + width])
            qkv_ref[:, c0:c0 + width] = res.astype(BF16)
            if c0 == COL_FV:
                vt_ref[...] = res.T.astype(BF16)
        ff_ref[...] = _dot(xb, w_ref[:, OFF_C:OFF_C + N_C])
        for c0 in range(0, N_B, 512):
            z_ref[:, c0:c0 + 512] = _dot(xb, w_ref[:, OFF_B + c0:OFF_B + c0 + 512])

    row = lambda i: (i, 0)
    return pl.pallas_call(
        kern, name="project",
        grid=(s // tm,),
        in_specs=[pl.BlockSpec((tm, k), row),
                  _resident((k, N_PAD), lambda i: (0, 0))],
        out_specs=[pl.BlockSpec((tm, N_A), row),
                   pl.BlockSpec((tm, N_C), row),
                   pl.BlockSpec((tm, N_B), row),
                   pl.BlockSpec((k, tm), lambda i: (0, i)),
                   pl.BlockSpec((FOX_W, tm), lambda i: (0, i))],
        out_shape=[jax.ShapeDtypeStruct((s, N_A), BF16),
                   jax.ShapeDtypeStruct((s, N_C), F32),
                   jax.ShapeDtypeStruct((s, N_B), F32),
                   jax.ShapeDtypeStruct((k, s), BF16),
                   jax.ShapeDtypeStruct((FOX_W, s), BF16)],
        compiler_params=_cparams(("parallel",)),
    )(x, w_pad)


def _grad_x_matmul(pieces, w_pad, dh, *, tm, tn, name):
    m = dh.shape[0]
    n, k = w_pad.shape
    widths = [p.shape[1] for p in pieces]
    offs = [sum(widths[:i]) for i in range(len(pieces))]
    assert sum(widths) == k

    def kern(*refs):
        p_refs, (b_ref, dh_ref, o_ref) = refs[:len(pieces)], refs[len(pieces):]
        acc = ALPHA * dh_ref[...]
        for p_ref, off, width in zip(p_refs, offs, widths):
            acc = acc + _dot_nt(p_ref[...], b_ref[:, off:off + width])
        o_ref[...] = acc

    assert tn == n
    return pl.pallas_call(
        kern, name=name,
        grid=(m // tm,),
        in_specs=[pl.BlockSpec((tm, w), lambda i: (i, 0)) for w in widths]
        + [_resident((n, k), lambda i: (0, 0)),
           pl.BlockSpec((tm, n), lambda i: (i, 0))],
        out_specs=pl.BlockSpec((tm, n), lambda i: (i, 0)),
        out_shape=jax.ShapeDtypeStruct((m, n), F32),
        compiler_params=_cparams(("parallel",)),
    )(*pieces, w_pad, dh)


def _grad_w_matmul(xt, blocks, *, tk, name):
    m, s = xt.shape
    tn = 512
    nb = len(blocks)

    def kern(a_ref, *refs):
        b_refs, o_ref = refs[:nb], refs[nb]

        @pl.when(pl.program_id(0) == 0)
        def _():
            o_ref[...] = jnp.zeros_like(o_ref)
        a = a_ref[...]
        for blk in range(nb):
            o_ref[:, blk * tn:(blk + 1) * tn] += _dot(a, b_refs[blk][...])

    return pl.pallas_call(
        kern, name=name,
        grid=(s // tk,),
        in_specs=[pl.BlockSpec((m, tk), lambda k: (0, k))]
        + [pl.BlockSpec((tk, tn), functools.partial(lambda k, col: (k, col), col=col)) for _, col in blocks],
        out_specs=_resident((m, nb * tn), lambda k: (0, 0)),
        out_shape=jax.ShapeDtypeStruct((m, nb * tn), F32),
        compiler_params=_cparams(("arbitrary",)),
    )(xt, *[arr for arr, _ in blocks])


def _tri(n, lower):
    r = lax.broadcasted_iota(jnp.int32, (n, n), 0)
    c = lax.broadcasted_iota(jnp.int32, (n, n), 1)
    keep = (c <= r) if lower else (c >= r)
    return jnp.where(keep, 1.0, 0.0).astype(BF16)


def _exact_dot(mat_bf16, x_f32, left):
    out = None
    for piece in _split3(x_f32):
        t = _dot(mat_bf16, piece) if left else _dot(piece, mat_bf16)
        out = t if out is None else out + t
    return out


def _log_sigmoid(z):
    return jnp.minimum(z, 0.0) - jnp.log(1.0 + jnp.exp(-jnp.abs(z)))


def _cum_fwd(ffp, bfp):
    s = ffp.shape[0]
    t = min(512, s)

    def kern(ff_ref, b_ref, cum_ref, carry_ref):
        @pl.when(pl.program_id(0) == 0)
        def _():
            carry_ref[...] = jnp.zeros_like(carry_ref)
        lane = lax.broadcasted_iota(jnp.int32, (1, LANES), 1)
        lf = _log_sigmoid(ff_ref[...] + b_ref[...])
        lf = jnp.where(lane < FOX_HEADS, lf, 0.0)
        cum = _exact_dot(_tri(t, True), lf, True) + carry_ref[0:1, :]
        cum_ref[...] = cum
        carry_ref[...] = jnp.broadcast_to(cum[t - 1:t, :], carry_ref.shape)

    return pl.pallas_call(
        kern, name="cum_fwd",
        grid=(s // t,),
        in_specs=[pl.BlockSpec((t, LANES), lambda i: (i, 0)),
                  pl.BlockSpec((1, LANES), lambda i: (0, 0))],
        out_specs=pl.BlockSpec((t, LANES), lambda i: (i, 0)),
        out_shape=jax.ShapeDtypeStruct((s, LANES), F32),
        scratch_shapes=[pltpu.VMEM((8, LANES), F32)],
        compiler_params=_cparams(("arbitrary",)),
    )(ffp, bfp)


def _cum_bwd(dcum_k, dcum_q, ffp, bfp):
    s = dcum_k.shape[0]
    t = min(512, s)
    nb = s // t

    def kern(dck_ref, dcq_ref, ff_ref, b_ref, dff_ref, gb_ref, carry_ref):
        @pl.when(pl.program_id(0) == 0)
        def _():
            carry_ref[...] = jnp.zeros_like(carry_ref)
            gb_ref[...] = jnp.zeros_like(gb_ref)
        lane = lax.broadcasted_iota(jnp.int32, (1, LANES), 1)
        dlf = _exact_dot(_tri(t, False), dck_ref[...] + dcq_ref[...], True) + carry_ref[0:1, :]
        carry_ref[...] = jnp.broadcast_to(dlf[0:1, :], carry_ref.shape)
        z = ff_ref[...] + b_ref[...]
        dff = jnp.where(lane < FOX_HEADS, dlf / (1.0 + jnp.exp(z)), 0.0)
        gb_ref[...] += jnp.broadcast_to(jnp.sum(dff, axis=0, keepdims=True), gb_ref.shape)
        dff_ref[...] = jnp.concatenate([dff, jnp.zeros_like(dff)], axis=1).astype(BF16)

    return pl.pallas_call(
        kern, name="cum_bwd",
        grid=(nb,),
        in_specs=[pl.BlockSpec((t, LANES), lambda i: (nb - 1 - i, 0)),
                  pl.BlockSpec((t, LANES), lambda i: (nb - 1 - i, 0)),
                  pl.BlockSpec((t, LANES), lambda i: (nb - 1 - i, 0)),
                  pl.BlockSpec((1, LANES), lambda i: (0, 0))],
        out_specs=[pl.BlockSpec((t, N_C), lambda i: (nb - 1 - i, 0)),
                   pl.BlockSpec((8, LANES), lambda i: (0, 0))],
        out_shape=[jax.ShapeDtypeStruct((s, N_C), BF16),
                   jax.ShapeDtypeStruct((8, LANES), F32)],
        scratch_shapes=[pltpu.VMEM((8, LANES), F32)],
        compiler_params=_cparams(("arbitrary",)),
    )(dcum_k, dcum_q, ffp, bfp)


def _resident(shape, index_map):
    return pl.BlockSpec(shape, index_map, pipeline_mode=pl.Buffered(1))


def _fox_fwd(qkv, vt, cum_t3, cum):
    s = qkv.shape[0]
    tk = tq = FOX_REF
    nq = s // tq
    nh = FOX_HEADS
    diag_tiles = tq // tk

    def kern(q_ref, k_ref, vt_ref, ct_ref, c_ref, o_ref, lse_ref, m_ref, acc_ref, u_ref):
        i = pl.program_id(0)
        lane = lax.broadcasted_iota(jnp.int32, (1, LANES), 1)
        krow = lax.broadcasted_iota(jnp.int32, (tk, tq), 0)
        qcol = lax.broadcasted_iota(jnp.int32, (tk, tq), 1)
        q0 = pl.multiple_of(i * tq, tq)
        qts, crefs = [], []
        for h in range(nh):
            p, a = divmod(h, 2)
            q2 = q_ref[:, p * LANES:(p + 1) * LANES] * jnp.asarray(QK_SCALE, BF16)
            sel = (lane < HEAD_DIM) if a == 0 else (lane >= HEAD_DIM)
            qts.append(jnp.where(sel, q2, jnp.zeros_like(q2)).astype(F32).T.astype(BF16))
            crefs.append(ct_ref[h, :, pl.ds(q0, LANES)][:, 0:1])
        m_ref[...] = jnp.full(m_ref.shape, NEG, F32)
        acc_ref[...] = jnp.zeros_like(acc_ref)
        ones = jnp.ones((SUM_ROWS, tk), BF16)

        def tile(j, diag):
            k0 = pl.multiple_of(j * tk, tk)
            cb = c_ref[pl.ds(k0, tk), :]
            sts = [_dot(k_ref[pl.ds(k0, tk), (h // 2) * LANES:(h // 2 + 1) * LANES], qts[h]) for h in range(nh)]
            tile_max = []
            for h in range(nh):
                u = sts[h] - (cb[:, h:h + 1] - crefs[h])
                if diag is not None:
                    u = jnp.where(krow + diag * tk <= qcol, u, NEG)
                u_ref[h] = u
                tile_max.append(jnp.max(u, axis=0, keepdims=True))
            pts, scales = [], []
            for h in range(nh):
                m_old = m_ref[h]
                m_new = jnp.maximum(m_old, tile_max[h])
                scales.append(jnp.exp(m_old - m_new))
                pts.append(jnp.exp(u_ref[h] - m_new).astype(BF16))
                m_ref[h] = m_new
            for h in range(nh):
                vth = jnp.concatenate([vt_ref[h * HEAD_DIM:(h + 1) * HEAD_DIM, pl.ds(k0, tk)], ones], axis=0)
                acc_ref[h] = scales[h] * acc_ref[h] + _dot(vth, pts[h])

        def body(j, c):
            tile(j, None)
            return c
        lax.fori_loop(0, i * diag_tiles, body, 0)
        for d in range(diag_tiles):
            tile(i * diag_tiles + d, d)

        ls = [acc_ref[h][HEAD_DIM:HEAD_DIM + 1] for h in range(nh)]
        for p in range(nh // 2):
            ot = jnp.concatenate([acc_ref[2 * p + a][:HEAD_DIM] * (1.0 / ls[2 * p + a]) for a in range(2)], axis=0)
            o_ref[:, p * LANES:(p + 1) * LANES] = ot.T
        for h in range(nh):
            lse_ref[h, :, pl.ds(q0, tq)] = m_ref[h] + jnp.log(ls[h])

    return pl.pallas_call(
        kern, name="fox_fwd",
        grid=(nq,),
        in_specs=[pl.BlockSpec((tq, FOX_W), lambda i: (i, 0)),
                  _resident((s, FOX_W), lambda i: (0, COL_FK // FOX_W)),
                  _resident((FOX_W, s), lambda i: (0, 0)),
                  _resident((nh, 1, s), lambda i: (0, 0, 0)),
                  _resident((s, LANES), lambda i: (0, 0))],
        out_specs=[pl.BlockSpec((tq, FOX_W), lambda i: (i, 0)),
                   pl.BlockSpec((nh, 1, s), lambda i: (0, 0, 0))],
        out_shape=[jax.ShapeDtypeStruct((s, FOX_W), F32),
                   jax.ShapeDtypeStruct((nh, 1, s), F32)],
        scratch_shapes=[pltpu.VMEM((nh, 1, tq), F32),
                        pltpu.VMEM((nh, HEAD_DIM + SUM_ROWS, tq), F32),
                        pltpu.VMEM((nh, tk, tq), F32)],
        compiler_params=_cparams(("arbitrary",)),
    )(qkv, qkv, vt, cum_t3, cum)


def _fox_bwd(qkv, do_bf, cum_t3, cum, lse_t3, delta_t3):
    s = qkv.shape[0]
    t = min(FOX_T, s)
    nq = s // t
    nh = FOX_HEADS
    npair = nh // 2

    def kern(q_ref, do_ref, k_ref, v_ref, ct_ref, c_ref, lse_ref, dl_ref,
             dqt_ref, dk_ref, dv_ref, dc_ref, dcq_ref, accv_ref, acck_ref, accd_ref):
        kj = pl.program_id(0)
        lane = lax.broadcasted_iota(jnp.int32, (1, LANES), 1)
        krow = lax.broadcasted_iota(jnp.int32, (t, t), 0)
        qcol = lax.broadcasted_iota(jnp.int32, (t, t), 1)
        causal = krow <= qcol
        sels = [lane < HEAD_DIM, lane >= HEAD_DIM]

        @pl.when(kj == 0)
        def _():
            dqt_ref[...] = jnp.zeros_like(dqt_ref)
            dcq_ref[...] = jnp.zeros_like(dcq_ref)

        accv_ref[...] = jnp.zeros_like(accv_ref)
        acck_ref[...] = jnp.zeros_like(acck_ref)
        accd_ref[...] = jnp.zeros_like(accd_ref)
        cb = c_ref[...]
        k2s, v2s, kts = [], [], []
        for p in range(npair):
            k2 = k_ref[:, p * LANES:(p + 1) * LANES]
            k2s.append(k2)
            v2s.append(v_ref[:, p * LANES:(p + 1) * LANES])
            kt = k2.astype(F32).T * QK_SCALE
            kts.append(kt[:HEAD_DIM].astype(BF16))
            kts.append(kt[HEAD_DIM:].astype(BF16))
        css = [cb[:, h:h + 1] for h in range(nh)]

        def tile(i, masked):
            q0 = pl.multiple_of(i * t, t)
            r0 = pl.multiple_of((i // (FOX_REF // t)) * FOX_REF, FOX_REF)
            sts, dpts, qms, doms = [], [], [], []
            for h in range(nh):
                p, a = divmod(h, 2)
                qi = q_ref[pl.ds(q0, t), p * LANES:(p + 1) * LANES] * jnp.asarray(QK_SCALE, BF16)
                doi = do_ref[pl.ds(q0, t), p * LANES:(p + 1) * LANES]
                qm = jnp.where(sels[a], qi, jnp.zeros_like(qi))
                dom = jnp.where(sels[a], doi, jnp.zeros_like(doi))
                qms.append(qm)
                doms.append(dom)
                sts.append(_dot_nt(k2s[p], qm))
                dpts.append(_dot_nt(v2s[p], dom))
            pts, dsts = [], []
            for h in range(nh):
                cref = ct_ref[h, :, pl.ds(r0, LANES)][:, 0:1]
                pt = jnp.exp(sts[h] - (css[h] - cref) - lse_ref[h, :, pl.ds(q0, t)])
                if masked:
                    pt = jnp.where(causal, pt, 0.0)
                ds32 = pt * (dpts[h] - dl_ref[h, :, pl.ds(q0, t)])
                part = ds32[:, 0:LANES]
                for c in range(1, t // LANES):
                    part = part + ds32[:, c * LANES:(c + 1) * LANES]
                accd_ref[h] += part
                dcq_ref[h, :, pl.ds(q0, t)] += jnp.sum(ds32, axis=0, keepdims=True)
                pts.append(pt.astype(BF16))
                dsts.append(ds32.astype(BF16))
            for p in range(npair):
                ha, hb = 2 * p, 2 * p + 1
                accv_ref[p] += _dot(pts[ha], doms[ha]) + _dot(pts[hb], doms[hb])
                acck_ref[p] += _dot(dsts[ha], qms[ha]) + _dot(dsts[hb], qms[hb])
            for h in range(nh):
                dqt_ref[h * HEAD_DIM:(h + 1) * HEAD_DIM, pl.ds(q0, t)] += _dot(kts[h], dsts[h])

        tile(kj, True)

        def body(i, c):
            tile(i, False)
            return c
        lax.fori_loop(kj + 1, nq, body, 0)

        dc = jnp.zeros((t, LANES), F32)
        for h in range(nh):
            dc = jnp.where(lane == h, -jnp.sum(accd_ref[h], axis=1, keepdims=True), dc)
        dc_ref[...] = dc
        for p in range(npair):
            dv_ref[:, p * LANES:(p + 1) * LANES] = accv_ref[p].astype(BF16)
            dk_ref[:, p * LANES:(p + 1) * LANES] = acck_ref[p].astype(BF16)

    whole = lambda kj: (0, 0, 0)
    return pl.pallas_call(
        kern, name="fox_bwd",
        grid=(nq,),
        in_specs=[_resident((s, FOX_W), lambda kj: (0, 0)),
                  _resident((s, FOX_W), lambda kj: (0, 0)),
                  pl.BlockSpec((t, FOX_W), lambda kj: (kj, COL_FK // FOX_W)),
                  pl.BlockSpec((t, FOX_W), lambda kj: (kj, COL_FV // FOX_W)),
                  _resident((nh, 1, s), whole),
                  pl.BlockSpec((t, LANES), lambda kj: (kj, 0)),
                  _resident((nh, 1, s), whole),
                  _resident((nh, 1, s), whole)],
        out_specs=[_resident((FOX_W, s), lambda kj: (0, 0)),
                   pl.BlockSpec((t, FOX_W), lambda kj: (kj, 0)),
                   pl.BlockSpec((t, FOX_W), lambda kj: (kj, 0)),
                   pl.BlockSpec((t, LANES), lambda kj: (kj, 0)),
                   _resident((nh, 1, s), whole)],
        out_shape=[jax.ShapeDtypeStruct((FOX_W, s), F32),
                   jax.ShapeDtypeStruct((s, FOX_W), BF16),
                   jax.ShapeDtypeStruct((s, FOX_W), BF16),
                   jax.ShapeDtypeStruct((s, LANES), F32),
                   jax.ShapeDtypeStruct((nh, 1, s), F32)],
        scratch_shapes=[pltpu.VMEM((npair, t, LANES), F32),
                        pltpu.VMEM((npair, t, LANES), F32),
                        pltpu.VMEM((nh, t, LANES), F32)],
        compiler_params=_cparams(("arbitrary",)),
    )(qkv, do_bf, qkv, qkv, cum_t3, cum, lse_t3, delta_t3)


def _bucket_table():
    qi = np.arange(BLOCK)[:, None]
    kj = np.arange(2 * BLOCK)[None, :]
    rel = np.maximum(qi + BLOCK - kj, 0).astype(np.int32)
    max_exact = NUM_BUCKETS // 2
    relf = np.maximum(rel, 1).astype(np.float32)
    large = max_exact + (np.log(relf / np.float32(max_exact)) / np.float32(math.log(MAX_DISTANCE / max_exact))
                         * np.float32(NUM_BUCKETS - max_exact)).astype(np.int32)
    large = np.minimum(large, NUM_BUCKETS - 1)
    return np.where(rel < max_exact, rel, large).astype(np.int32)


SWA_LANES = SWA_GROUP * BLOCK


def _swa_bias(rel_bias, bucket_t):
    def kern(rb_ref, bk_ref, o_ref):
        bk = bk_ref[...]
        kj = lax.broadcasted_iota(jnp.int32, (2 * BLOCK, BLOCK), 0)
        qi = lax.broadcasted_iota(jnp.int32, (2 * BLOCK, BLOCK), 1)
        rel = qi + BLOCK - kj
        band = (rel >= 0) & (rel < BLOCK)
        masks = [band & (kj >= BLOCK), band]
        for h in range(SWA_HEADS):
            g, hh = divmod(h, SWA_GROUP)
            acc = jnp.zeros((2 * BLOCK, BLOCK), F32)
            for b in range(NUM_BUCKETS):
                acc = jnp.where(bk == b, rb_ref[b, h], acc)
            for first in range(2):
                o_ref[first, g, :, hh * BLOCK:(hh + 1) * BLOCK] = jnp.where(masks[first], acc, NEG)

    return pl.pallas_call(
        kern, name="swa_bias",
        in_specs=[pl.BlockSpec(memory_space=pltpu.SMEM),
                  pl.BlockSpec(memory_space=pltpu.VMEM)],
        out_specs=pl.BlockSpec(memory_space=pltpu.VMEM),
        out_shape=jax.ShapeDtypeStruct((2, SWA_KV_HEADS, 2 * BLOCK, SWA_LANES), F32),
        compiler_params=_cparams(),
    )(rel_bias, bucket_t)


def _swa_operands(q_ref, kp_ref, kc_ref, vp_ref, vc_ref):
    k = jnp.concatenate([kp_ref[...], kc_ref[...]], axis=0)
    v = jnp.concatenate([vp_ref[...], vc_ref[...]], axis=0)
    qt = (q_ref[...] * jnp.asarray(QK_SCALE, BF16)).astype(F32).T.astype(BF16)
    return k, v, _group_rows(qt)


def _group_rows(xt):
    zeros = jnp.zeros((HEAD_DIM, SWA_LANES), BF16)
    out = []
    for g in range(SWA_KV_HEADS):
        heads = [xt[(SWA_GROUP * g + hh) * HEAD_DIM:(SWA_GROUP * g + hh + 1) * HEAD_DIM, :] for hh in range(SWA_GROUP)]
        rows = jnp.concatenate(heads, axis=1)
        padded = jnp.concatenate([rows, zeros] if g == 0 else [zeros, rows], axis=0)
        out.append((rows, padded))
    return out


def _pairs_to_rows(cols_t):
    out = []
    for p in range(SWA_HEADS // 2):
        g, hh = divmod(2 * p, SWA_GROUP)
        pair = jnp.concatenate([cols_t[g][:, hh * BLOCK:(hh + 1) * BLOCK],
                                cols_t[g][:, (hh + 1) * BLOCK:(hh + 2) * BLOCK]], axis=0)
        out.append(pair.T)
    return jnp.concatenate(out, axis=1)


def _swa_fwd(qkv, bias_t, sink_rows):
    s = qkv.shape[0]
    nb = s // BLOCK

    def kern(q_ref, kp_ref, kc_ref, vp_ref, vc_ref, bias_ref, sink_ref, o_ref, lse_ref):
        n = pl.program_id(0)
        table = jnp.minimum(n, 1)
        k, v, qts = _swa_operands(q_ref, kp_ref, kc_ref, vp_ref, vc_ref)
        vt = v.astype(F32).T.astype(BF16)
        us = [_dot(k, qts[g][1]) + bias_ref[table, g] for g in range(SWA_KV_HEADS)]
        outs = []
        for g in range(SWA_KV_HEADS):
            sk = sink_ref[g]
            m = jnp.maximum(jnp.max(us[g], axis=0, keepdims=True), sk)
            p = jnp.exp(us[g] - m)
            l = jnp.sum(p, axis=0, keepdims=True) + jnp.exp(sk - m)
            lse_ref[0, g] = m + jnp.log(l)
            outs.append(_dot(vt[g * HEAD_DIM:(g + 1) * HEAD_DIM, :], (p * (1.0 / l)).astype(BF16)))
        o_ref[...] = _pairs_to_rows(outs)

    cq, ck, cv = COL_SQ // SWA_W, COL_SK // LANES, COL_SV // LANES
    prev = lambda n: jnp.maximum(n - 1, 0)
    return pl.pallas_call(
        kern, name="swa_fwd",
        grid=(nb,),
        in_specs=[pl.BlockSpec((BLOCK, SWA_W), lambda n: (n, cq)),
                  pl.BlockSpec((BLOCK, LANES), lambda n: (prev(n), ck)),
                  pl.BlockSpec((BLOCK, LANES), lambda n: (n, ck)),
                  pl.BlockSpec((BLOCK, LANES), lambda n: (prev(n), cv)),
                  pl.BlockSpec((BLOCK, LANES), lambda n: (n, cv)),
                  _resident((2, SWA_KV_HEADS, 2 * BLOCK, SWA_LANES), lambda n: (0, 0, 0, 0)),
                  _resident((SWA_KV_HEADS, 1, SWA_LANES), lambda n: (0, 0, 0))],
        out_specs=[pl.BlockSpec((BLOCK, SWA_W), lambda n: (n, 0)),
                   pl.BlockSpec((1, SWA_KV_HEADS, 1, SWA_LANES), lambda n: (n, 0, 0, 0))],
        out_shape=[jax.ShapeDtypeStruct((s, SWA_W), F32),
                   jax.ShapeDtypeStruct((nb, SWA_KV_HEADS, 1, SWA_LANES), F32)],
        compiler_params=_cparams(("parallel",)),
    )(qkv, qkv, qkv, qkv, qkv, bias_t, sink_rows)


def _swa_bwd(qkv, do_bf, delta_rows, lse, bias_t, sink_rows, bucket_t):
    s = qkv.shape[0]
    nb = s // BLOCK

    def kern(q_ref, kp_ref, kc_ref, vp_ref, vc_ref, do_ref, dl_ref, lse_ref, bias_ref, sink_ref, bk_ref,
             dq_ref, dk_ref, dv_ref, grb_ref, gsk_ref, dbias_ref, ck_ref, cv_ref, sk_ref):
        n = pl.program_id(0)

        @pl.when(n == 0)
        def _():
            dbias_ref[...] = jnp.zeros_like(dbias_ref)
            ck_ref[...] = jnp.zeros_like(ck_ref)
            cv_ref[...] = jnp.zeros_like(cv_ref)
            sk_ref[...] = jnp.zeros_like(sk_ref)

        @pl.when(n < nb)
        def _():
            table = jnp.minimum(n, 1)
            k, v, qts = _swa_operands(q_ref, kp_ref, kc_ref, vp_ref, vc_ref)
            dots = _group_rows(do_ref[...].astype(F32).T.astype(BF16))
            kt = (k.astype(F32).T * QK_SCALE).astype(BF16)
            groups = range(SWA_KV_HEADS)
            sts = [_dot(k, qts[g][1]) for g in groups]
            dps = [_dot(v, dots[g][1]) for g in groups]
            ps, dss = [], []
            for g in groups:
                lse_g = lse_ref[0, g]
                dlt = dl_ref[0, g]
                p = jnp.exp(sts[g] + bias_ref[table, g] - lse_g)
                ds = p * (dps[g] - dlt)
                dbias_ref[g] += ds
                sk_ref[g] += -jnp.exp(sink_ref[g] - lse_g) * dlt
                ps.append(p.astype(BF16))
                dss.append(ds.astype(BF16))
            dvt = jnp.concatenate([_dot_nt(dots[g][0], ps[g]) for g in groups], axis=0)
            dkt = jnp.concatenate([_dot_nt(qts[g][0], dss[g]) for g in groups], axis=0)
            dqts = [_dot(kt[g * HEAD_DIM:(g + 1) * HEAD_DIM, :], dss[g]) for g in groups]
            dq_ref[...] = _pairs_to_rows(dqts).astype(BF16)
            dk2 = dkt.T
            dv2 = dvt.T
            dk_ref[...] = (ck_ref[...] + dk2[:BLOCK]).astype(BF16)
            dv_ref[...] = (cv_ref[...] + dv2[:BLOCK]).astype(BF16)
            ck_ref[...] = dk2[BLOCK:]
            cv_ref[...] = dv2[BLOCK:]

        @pl.when(n == nb)
        def _():
            dk_ref[...] = ck_ref[...].astype(BF16)
            dv_ref[...] = cv_ref[...].astype(BF16)
            bk = bk_ref[...]
            lane = lax.broadcasted_iota(jnp.int32, (8, LANES), 1)
            rowi = lax.broadcasted_iota(jnp.int32, (NUM_BUCKETS, LANES), 0)
            lanei = lax.broadcasted_iota(jnp.int32, (NUM_BUCKETS, LANES), 1)
            out = jnp.zeros((NUM_BUCKETS, LANES), F32)
            gsk = jnp.zeros((8, LANES), F32)
            for h in range(SWA_HEADS):
                g, hh = divmod(h, SWA_GROUP)
                cols = slice(hh * BLOCK, (hh + 1) * BLOCK)
                gsk = jnp.where(lane == h, jnp.sum(sk_ref[g][:, cols]), gsk)
                db = dbias_ref[g][:, cols]
                for b in range(NUM_BUCKETS):
                    val = jnp.sum(jnp.where(bk == b, db, 0.0))
                    out = jnp.where((rowi == b) & (lanei == h), val, out)
            grb_ref[...] = out
            gsk_ref[...] = gsk

    cq, ck, cv = COL_SQ // SWA_W, COL_SK // LANES, COL_SV // LANES
    cur = lambda n: jnp.minimum(n, nb - 1)
    prev = lambda n: jnp.maximum(jnp.minimum(n, nb - 1) - 1, 0)
    kout = lambda n: jnp.maximum(n - 1, 0)
    return pl.pallas_call(
        kern, name="swa_bwd",
        grid=(nb + 1,),
        in_specs=[pl.BlockSpec((BLOCK, SWA_W), lambda n: (cur(n), cq)),
                  pl.BlockSpec((BLOCK, LANES), lambda n: (prev(n), ck)),
                  pl.BlockSpec((BLOCK, LANES), lambda n: (cur(n), ck)),
                  pl.BlockSpec((BLOCK, LANES), lambda n: (prev(n), cv)),
                  pl.BlockSpec((BLOCK, LANES), lambda n: (cur(n), cv)),
                  pl.BlockSpec((BLOCK, SWA_W), lambda n: (cur(n), 1)),
                  pl.BlockSpec((1, SWA_KV_HEADS, 1, SWA_LANES), lambda n: (cur(n), 0, 0, 0)),
                  pl.BlockSpec((1, SWA_KV_HEADS, 1, SWA_LANES), lambda n: (cur(n), 0, 0, 0)),
                  _resident((2, SWA_KV_HEADS, 2 * BLOCK, SWA_LANES), lambda n: (0, 0, 0, 0)),
                  _resident((SWA_KV_HEADS, 1, SWA_LANES), lambda n: (0, 0, 0)),
                  _resident((2 * BLOCK, BLOCK), lambda n: (0, 0))],
        out_specs=[pl.BlockSpec((BLOCK, SWA_W), lambda n: (cur(n), 0)),
                   pl.BlockSpec((BLOCK, LANES), lambda n: (kout(n), 0)),
                   pl.BlockSpec((BLOCK, LANES), lambda n: (kout(n), 0)),
                   pl.BlockSpec((NUM_BUCKETS, LANES), lambda n: (0, 0)),
                   pl.BlockSpec((8, LANES), lambda n: (0, 0))],
        out_shape=[jax.ShapeDtypeStruct((s, SWA_W), BF16),
                   jax.ShapeDtypeStruct((s, LANES), BF16),
                   jax.ShapeDtypeStruct((s, LANES), BF16),
                   jax.ShapeDtypeStruct((NUM_BUCKETS, LANES), F32),
                   jax.ShapeDtypeStruct((8, LANES), F32)],
        scratch_shapes=[pltpu.VMEM((SWA_KV_HEADS, 2 * BLOCK, SWA_LANES), F32),
                        pltpu.VMEM((BLOCK, LANES), F32),
                        pltpu.VMEM((BLOCK, LANES), F32),
                        pltpu.VMEM((SWA_KV_HEADS, 1, SWA_LANES), F32)],
        compiler_params=_cparams(("arbitrary",)),
    )(qkv, qkv, qkv, qkv, qkv, do_bf, delta_rows, lse, bias_t, sink_rows, bucket_t)


def _post(x, target, o_fox, o_swa, z, w_o, ln_g, ln_b):
    s = x.shape[0]
    tm = min(256, s)
    nt = s // tm

    def kern(x_ref, t_ref, of_ref, os_ref, z_ref, w_ref, g_ref, b_ref,
             loss_ref, dh_ref, gwo_ref, do_ref, dz_ref, dl_ref, gg_ref, gb_ref, lacc_ref):
        step = pl.program_id(0)

        @pl.when(step == 0)
        def _():
            lacc_ref[...] = jnp.zeros_like(lacc_ref)
            gg_ref[...] = jnp.zeros_like(gg_ref)
            gwo_ref[...] = jnp.zeros_like(gwo_ref)
            gb_ref[...] = jnp.zeros_like(gb_ref)

        o = jnp.concatenate([of_ref[...], os_ref[...]], axis=1)
        zz = z_ref[...]
        sig = 1.0 / (1.0 + jnp.exp(-zz))
        silu = zz * sig
        mixed32 = o * silu
        mixed = mixed32.astype(BF16)
        w = w_ref[...]
        h = ALPHA * x_ref[...] + _dot(mixed, w)
        mu = jnp.mean(h, axis=1, keepdims=True)
        hc = h - mu
        var = jnp.mean(hc * hc, axis=1, keepdims=True)
        rstd = lax.rsqrt(var + LN_EPS)
        xhat = hc * rstd
        g = g_ref[...]
        err = xhat * g + b_ref[...] - t_ref[...]
        lacc_ref[...] += jnp.broadcast_to(jnp.sum(err * err, axis=0, keepdims=True), lacc_ref.shape)
        dout = err * (1.0 / D_MODEL)
        gg_ref[...] += jnp.broadcast_to(jnp.sum(dout * xhat, axis=0, keepdims=True), gg_ref.shape)
        gb_ref[...] += jnp.broadcast_to(jnp.sum(dout, axis=0, keepdims=True), gb_ref.shape)
        dxh = dout * g
        m1 = jnp.mean(dxh, axis=1, keepdims=True)
        m2 = jnp.mean(dxh * xhat, axis=1, keepdims=True)
        dh = rstd * (dxh - m1 - xhat * m2)
        dh_ref[...] = dh
        dy = dh.astype(BF16)
        gwo_ref[...] += _dot(mixed32.T.astype(BF16), dy)
        dmix = _dot_nt(dy, w)
        do = dmix * silu
        do_ref[...] = do.astype(BF16)
        dz_ref[...] = (dmix * o * (sig * (1.0 + zz * (1.0 - sig)))).astype(BF16)
        r = lax.broadcasted_iota(jnp.int32, (D_MODEL, LANES), 0) // HEAD_DIM
        c = lax.broadcasted_iota(jnp.int32, (D_MODEL, LANES), 1)
        pick = jnp.where(r == c, 1.0, 0.0).astype(BF16)
        dl_ref[...] = _exact_dot(pick, do * o, False)

        @pl.when(step == nt - 1)
        def _():
            tot = jnp.sum(lacc_ref[0:1, :]) * (0.5 / D_MODEL)
            loss_ref[...] = jnp.broadcast_to(tot, loss_ref.shape)

    row = lambda i: (i, 0)
    fixed = lambda i: (0, 0)
    wide = pl.BlockSpec((tm, D_MODEL), row)
    half = pl.BlockSpec((tm, FOX_W), row)
    return pl.pallas_call(
        kern, name="post",
        grid=(nt,),
        in_specs=[wide, wide, half, half, wide,
                  pl.BlockSpec((D_MODEL, D_MODEL), fixed),
                  pl.BlockSpec((1, D_MODEL), fixed),
                  pl.BlockSpec((1, D_MODEL), fixed)],
        out_specs=[pl.BlockSpec((8, LANES), fixed), wide,
                   _resident((D_MODEL, D_MODEL), fixed), wide, wide,
                   pl.BlockSpec((tm, LANES), row),
                   pl.BlockSpec((8, D_MODEL), fixed), pl.BlockSpec((8, D_MODEL), fixed)],
        out_shape=[jax.ShapeDtypeStruct((8, LANES), F32),
                   jax.ShapeDtypeStruct((s, D_MODEL), F32),
                   jax.ShapeDtypeStruct((D_MODEL, D_MODEL), F32),
                   jax.ShapeDtypeStruct((s, D_MODEL), BF16),
                   jax.ShapeDtypeStruct((s, D_MODEL), BF16),
                   jax.ShapeDtypeStruct((s, LANES), F32),
                   jax.ShapeDtypeStruct((8, D_MODEL), F32),
                   jax.ShapeDtypeStruct((8, D_MODEL), F32)],
        scratch_shapes=[pltpu.VMEM((8, D_MODEL), F32)],
        compiler_params=_cparams(("arbitrary",)),
    )(x, target, o_fox, o_swa, z, w_o, ln_g, ln_b)


def _adamw_math(w, g, m, v):
    m = ADAM_B1 * m + (1.0 - ADAM_B1) * g
    v = ADAM_B2 * v + (1.0 - ADAM_B2) * (g * g)
    m_hat = m / (1.0 - ADAM_B1 ** ADAM_STEP)
    v_hat = v / (1.0 - ADAM_B2 ** ADAM_STEP)
    delta = -ADAM_LR * (m_hat / (jnp.sqrt(v_hat) + ADAM_EPS) + ADAM_WD * w)
    return delta, m, v


def _adamw(w, g, m, v, *, name):
    r, c = w.shape
    tr = min(256, r)

    def kern(w_ref, g_ref, m_ref, v_ref, d_ref, mo_ref, vo_ref):
        d, mn, vn = _adamw_math(w_ref[...], g_ref[...], m_ref[...], v_ref[...])
        d_ref[...] = d
        mo_ref[...] = mn
        vo_ref[...] = vn

    blk = pl.BlockSpec((tr, c), lambda i: (i, 0))
    sds = jax.ShapeDtypeStruct((r, c), F32)
    return pl.pallas_call(
        kern, name=name,
        grid=(r // tr,),
        in_specs=[blk, blk, blk, blk],
        out_specs=[blk, blk, blk],
        out_shape=[sds, sds, sds],
        compiler_params=_cparams(("parallel",)),
    )(w, g, m, v)


def _adamw_cols(w, g, m, v, *, name):
    c, _, r = w.shape
    tc = 139
    assert c % tc == 0

    def kern(w_ref, g_ref, m_ref, v_ref, d_ref, mo_ref, vo_ref):
        d, mn, vn = _adamw_math(w_ref[...], g_ref[...], m_ref[...], v_ref[...])
        d_ref[...] = d
        mo_ref[...] = mn
        vo_ref[...] = vn

    blk = pl.BlockSpec((tc, 1, r), lambda i: (i, 0, 0))
    sds = jax.ShapeDtypeStruct((c, 1, r), F32)
    return pl.pallas_call(
        kern, name=name,
        grid=(c // tc,),
        in_specs=[blk, blk, blk, blk],
        out_specs=[blk, blk, blk],
        out_shape=[sds, sds, sds],
        compiler_params=_cparams(("parallel",)),
    )(w, g, m, v)


def _position():
    x, y, c = lax.axis_index("x"), lax.axis_index("y"), lax.axis_index("c")
    chips = [(1 - x, y), (x, 1 - y), (1 - x, 1 - y)]
    return x, y, c, chips


def _chip_index(cx, cy):
    return 2 * cx + cy


def _gather_weights(w_in_bf, w_o_bf):
    shards = (w_in_bf, w_o_bf)
    n_arr = len(shards)

    def kern(*refs):
        ins, outs = refs[:n_arr], refs[n_arr:2 * n_arr]
        send_sems, recv_sems, local_sems = refs[2 * n_arr:]
        x, y, c, chips = _position()
        me = _chip_index(x, y)
        sibling = (x, y, 1 - c)

        local = [pltpu.make_async_copy(ins[a], outs[a].at[me], local_sems.at[a]) for a in range(n_arr)]
        for cp in local:
            cp.start()

        def half(ref, a):
            rows = shards[a].shape[0] // 2
            return ref.at[pl.ds(c * rows, rows), :]

        def copy(a, k, src, slot, to):
            return pltpu.make_async_remote_copy(
                src_ref=src, dst_ref=half(outs[a].at[slot], a),
                send_sem=send_sems.at[a * 6 + k], recv_sem=recv_sems.at[a * 6 + k],
                device_id=to, device_id_type=MESH)

        first = [copy(a, j, half(ins[a], a), me, (*chip, c)) for a in range(n_arr) for j, chip in enumerate(chips)]
        for cp in first:
            cp.start()
        passed = []
        for a in range(n_arr):
            for j, chip in enumerate(chips):
                slot = _chip_index(*chip)
                copy(a, j, half(ins[a], a), slot, (*chip, c)).wait_recv()
                fwd = copy(a, 3 + j, half(outs[a].at[slot], a), slot, sibling)
                fwd.start()
                passed.append(fwd)
        for a in range(n_arr):
            for j, chip in enumerate(chips):
                slot = _chip_index(*chip)
                rows = shards[a].shape[0] // 2
                dst = outs[a].at[slot].at[pl.ds((1 - c) * rows, rows), :]
                pltpu.make_async_remote_copy(
                    src_ref=dst, dst_ref=dst, send_sem=send_sems.at[a * 6 + 3 + j],
                    recv_sem=recv_sems.at[a * 6 + 3 + j], device_id=sibling, device_id_type=MESH).wait_recv()
        for cp in first + passed:
            cp.wait_send()
        for cp in local:
            cp.wait()

    vmem = pl.BlockSpec(memory_space=pltpu.VMEM)
    return pl.pallas_call(
        kern, name="gather_weights",
        in_specs=[vmem] * n_arr,
        out_specs=[vmem] * n_arr,
        out_shape=[jax.ShapeDtypeStruct((N_CHIPS,) + w.shape, w.dtype) for w in shards],
        scratch_shapes=[pltpu.SemaphoreType.DMA((6 * n_arr,)),
                        pltpu.SemaphoreType.DMA((6 * n_arr,)),
                        pltpu.SemaphoreType.DMA((n_arr,))],
        compiler_params=_cparams(),
    )(*shards)


def _pair_reduce(grads):
    n_arr = len(grads)
    chunk = 128

    def kern(*refs):
        ins = refs[:n_arr]
        outs = refs[n_arr:2 * n_arr]
        gots = refs[2 * n_arr:3 * n_arr]
        send_sems, recv_sems = refs[3 * n_arr:]
        x, y, c, _ = _position()
        sibling = (x, y, 1 - c)
        copies = []
        for a in range(n_arr):
            rows = grads[a].shape[1] // 2
            copies.append(pltpu.make_async_remote_copy(
                src_ref=ins[a].at[:, pl.ds((1 - c) * rows, rows), :], dst_ref=gots[a],
                send_sem=send_sems.at[a], recv_sem=recv_sems.at[a], device_id=sibling, device_id_type=MESH))
        for cp in copies:
            cp.start()
        for a in range(n_arr):
            copies[a].wait()
            rows = grads[a].shape[1] // 2
            for j in range(N_CHIPS):
                for r0 in range(0, rows, chunk):
                    mine = ins[a][j, pl.ds(pl.multiple_of(c * rows + r0, chunk), chunk), :]
                    outs[a][j, r0:r0 + chunk, :] = (mine + gots[a][j, r0:r0 + chunk, :]).astype(BF16)

    vmem = pl.BlockSpec(memory_space=pltpu.VMEM)
    half = [(N_CHIPS, g.shape[1] // 2, g.shape[2]) for g in grads]
    return pl.pallas_call(
        kern, name="pair_reduce",
        in_specs=[vmem] * n_arr,
        out_specs=[vmem] * n_arr,
        out_shape=[jax.ShapeDtypeStruct(h, BF16) for h in half],
        scratch_shapes=[pltpu.VMEM(h, F32) for h in half]
        + [pltpu.SemaphoreType.DMA((n_arr,)), pltpu.SemaphoreType.DMA((n_arr,))],
        compiler_params=_cparams(),
    )(*grads)


def _chip_reduce(parts):
    n_arr = len(parts)
    chunk = 128

    def kern(*refs):
        ins = refs[:n_arr]
        outs = refs[n_arr:2 * n_arr]
        slabs = refs[2 * n_arr:3 * n_arr]
        send_sems, recv_sems, local_sems = refs[3 * n_arr:]
        x, y, c, chips = _position()
        me = _chip_index(x, y)
        local = [pltpu.make_async_copy(ins[a].at[me], slabs[a].at[me], local_sems.at[a]) for a in range(n_arr)]
        for cp in local:
            cp.start()
        sends = []
        for a in range(n_arr):
            for j, chip in enumerate(chips):
                sends.append(pltpu.make_async_remote_copy(
                    src_ref=ins[a].at[_chip_index(*chip)], dst_ref=slabs[a].at[me],
                    send_sem=send_sems.at[a * 3 + j], recv_sem=recv_sems.at[a * 3 + j],
                    device_id=(*chip, c), device_id_type=MESH))
        for cp in sends:
            cp.start()
        for a in range(n_arr):
            for j, chip in enumerate(chips):
                slot = slabs[a].at[_chip_index(*chip)]
                pltpu.make_async_remote_copy(
                    src_ref=slot, dst_ref=slot, send_sem=send_sems.at[a * 3 + j],
                    recv_sem=recv_sems.at[a * 3 + j], device_id=(*chip, c), device_id_type=MESH).wait_recv()
        for cp in sends:
            cp.wait_send()
        for cp in local:
            cp.wait()
        for a in range(n_arr):
            for r0 in range(0, parts[a].shape[1], chunk):
                f = lambda j: slabs[a][j, r0:r0 + chunk, :].astype(F32)
                outs[a][r0:r0 + chunk, :] = ((f(0) + f(1)) + f(2)) + f(3)

    vmem = pl.BlockSpec(memory_space=pltpu.VMEM)
    return pl.pallas_call(
        kern, name="chip_reduce",
        in_specs=[vmem] * n_arr,
        out_specs=[vmem] * n_arr,
        out_shape=[jax.ShapeDtypeStruct(p.shape[1:], F32) for p in parts],
        scratch_shapes=[pltpu.VMEM(p.shape, BF16) for p in parts]
        + [pltpu.SemaphoreType.DMA((3 * n_arr,)),
           pltpu.SemaphoreType.DMA((3 * n_arr,)),
           pltpu.SemaphoreType.DMA((n_arr,))],
        compiler_params=_cparams(),
    )(*parts)


def _join_halves(halves):
    n_arr = len(halves)

    def kern(*refs):
        ins = refs[:n_arr]
        outs = refs[n_arr:2 * n_arr]
        send_sems, recv_sems, local_sems = refs[2 * n_arr:]
        x, y, c, _ = _position()
        sibling = (x, y, 1 - c)
        local, remote = [], []
        for a in range(n_arr):
            rows = halves[a].shape[0]
            mine = outs[a].at[pl.ds(c * rows, rows), :]
            local.append(pltpu.make_async_copy(ins[a], mine, local_sems.at[a]))
            remote.append(pltpu.make_async_remote_copy(
                src_ref=ins[a], dst_ref=mine, send_sem=send_sems.at[a], recv_sem=recv_sems.at[a],
                device_id=sibling, device_id_type=MESH))
        for cp in local + remote:
            cp.start()
        for a in range(n_arr):
            rows = halves[a].shape[0]
            theirs = outs[a].at[pl.ds((1 - c) * rows, rows), :]
            pltpu.make_async_remote_copy(
                src_ref=theirs, dst_ref=theirs, send_sem=send_sems.at[a], recv_sem=recv_sems.at[a],
                device_id=sibling, device_id_type=MESH).wait_recv()
        for cp in remote:
            cp.wait_send()
        for cp in local:
            cp.wait()

    vmem = pl.BlockSpec(memory_space=pltpu.VMEM)
    return pl.pallas_call(
        kern, name="join_halves",
        in_specs=[vmem] * n_arr,
        out_specs=[vmem] * n_arr,
        out_shape=[jax.ShapeDtypeStruct((2 * h.shape[0], h.shape[1]), F32) for h in halves],
        scratch_shapes=[pltpu.SemaphoreType.DMA((n_arr,)),
                        pltpu.SemaphoreType.DMA((n_arr,)),
                        pltpu.SemaphoreType.DMA((n_arr,))],
        compiler_params=_cparams(),
    )(*halves)


def _small_allreduce_adamw(partials, params, moms, vels):
    chunks = D_MODEL // LANES
    row_rb, row_bf, row_sk, row_loss = 2 * chunks, 2 * chunks + NUM_BUCKETS, 2 * chunks + NUM_BUCKETS + 1, SMALL_ROWS - 6

    def kern(gbf_ref, grb_ref, gsk_ref, gg_ref, gb_ref, loss_ref, *refs):
        p_refs, m_refs, v_refs = refs[0:5], refs[5:10], refs[10:15]
        lo_ref, g_outs, d_outs, mo_outs, vo_outs = refs[15], refs[16:21], refs[21:26], refs[26:31], refs[31:36]
        send_ref, buf_ref, send_sems, recv_sems = refs[36:]
        x, y, c, _ = _position()
        me = 4 * x + 2 * y + c
        send_ref[...] = jnp.zeros_like(send_ref)
        for r in range(chunks):
            send_ref[r:r + 1, :] = gg_ref[0:1, r * LANES:(r + 1) * LANES]
            send_ref[chunks + r:chunks + r + 1, :] = gb_ref[0:1, r * LANES:(r + 1) * LANES]
        send_ref[row_rb:row_rb + NUM_BUCKETS, :] = grb_ref[...]
        send_ref[row_bf:row_bf + 1, :] = gbf_ref[0:1, :]
        send_ref[row_sk:row_sk + 1, :] = gsk_ref[0:1, :]
        send_ref[row_loss:row_loss + 1, :] = loss_ref[0:1, :]
        buf_ref[me] = send_ref[...]
        peers = [(x, y, 1 - c)] + [(px, py, pc) for px, py in _position()[3] for pc in (c, 1 - c)]
        sends = []
        for k, peer in enumerate(peers):
            sends.append(pltpu.make_async_remote_copy(
                src_ref=send_ref, dst_ref=buf_ref.at[me], send_sem=send_sems.at[k], recv_sem=recv_sems.at[k],
                device_id=peer, device_id_type=MESH))
        for cp in sends:
            cp.start()
        for k, (px, py, pc) in enumerate(peers):
            slot = buf_ref.at[4 * px + 2 * py + pc]
            pltpu.make_async_remote_copy(
                src_ref=slot, dst_ref=slot, send_sem=send_sems.at[k], recv_sem=recv_sems.at[k],
                device_id=(px, py, pc), device_id_type=MESH).wait_recv()
        for cp in sends:
            cp.wait_send()
        tot = buf_ref[0]
        for d in range(1, N_DEV):
            tot = tot + buf_ref[d]
        lo_ref[...] = tot[row_loss:row_loss + 1, :]
        grads = [tot[row_bf:row_bf + 1, 0:FOX_HEADS],
                 tot[row_rb:row_rb + NUM_BUCKETS, 0:SWA_HEADS],
                 tot[row_sk:row_sk + 1, 0:SWA_HEADS],
                 jnp.concatenate([tot[r:r + 1, :] for r in range(chunks)], axis=1),
                 jnp.concatenate([tot[chunks + r:chunks + r + 1, :] for r in range(chunks)], axis=1)]
        for i, g in enumerate(grads):
            g_outs[i][...] = g
            delta, mn, vn = _adamw_math(p_refs[i][...], g, m_refs[i][...], v_refs[i][...])
            d_outs[i][...] = delta
            mo_outs[i][...] = mn
            vo_outs[i][...] = vn

    vm = pl.BlockSpec(memory_space=pltpu.VMEM)
    shapes = [jax.ShapeDtypeStruct(p.shape, F32) for p in params]
    outs = pl.pallas_call(
        kern, name="small_allreduce_adamw",
        in_specs=[vm] * 21,
        out_specs=[vm] * 21,
        out_shape=[jax.ShapeDtypeStruct((1, LANES), F32)] + shapes * 4,
        scratch_shapes=[pltpu.VMEM((SMALL_ROWS, LANES), F32),
                        pltpu.VMEM((N_DEV, SMALL_ROWS, LANES), F32),
                        pltpu.SemaphoreType.DMA((N_DEV - 1,)),
                        pltpu.SemaphoreType.DMA((N_DEV - 1,))],
    )(*partials, *params, *moms, *vels)
    return outs[0], outs[1:6], outs[6:11], outs[11:16], outs[16:21]


def _to_padded_cols(w):
    pad = jnp.zeros((w.shape[0], N_C - FOX_HEADS), w.dtype)
    return jnp.concatenate([w[:, 0:1536], w[:, 2056:2824], w[:, 1536:1544], pad,
                            w[:, 1544:2056], w[:, 2824:3336]], axis=1)


def _from_padded_cols(g):
    return jnp.concatenate([g[:, 0:1536], g[:, OFF_C:OFF_C + FOX_HEADS], g[:, OFF_B:OFF_B + FOX_W],
                            g[:, 1536:N_A], g[:, OFF_B + FOX_W:N_PAD]], axis=1)


def _fox_rows(a):
    return a[:, :FOX_HEADS].T.reshape(FOX_HEADS, 1, a.shape[0])


def kernel(x, w_in, b_f, rel_bias, sink, w_o, ln_g, ln_b, loss_target, m_w_in, m_b_f, m_rel_bias, m_sink, m_w_o, m_ln_g, m_ln_b, v_w_in, v_b_f, v_rel_bias, v_sink, v_w_o, v_ln_g, v_ln_b):
    x2 = x[0]
    tgt = loss_target[0]
    s = x2.shape[0]
    w_in2, w_o2 = w_in[0], w_o[0]

    shard_cols = D_IN // N_CHIPS
    col_pad = ((0, 0), (0, SHARD_PAD - shard_cols))
    w_in_all, w_o_all = _gather_weights(jnp.pad(w_in2.astype(BF16), col_pad), w_o2.astype(BF16))
    w_full = jnp.concatenate([w_in_all[j, :, :shard_cols] for j in range(N_CHIPS)], axis=1)
    w_pad = _to_padded_cols(w_full)
    w_o_full = w_o_all.reshape(D_MODEL, D_MODEL)

    qkv, ffp, z, xt, vt = _project(x2, w_pad)
    bfp = jnp.pad(b_f, ((0, 0), (0, LANES - FOX_HEADS)))
    cum = _cum_fwd(ffp, bfp)
    cum_t3 = _fox_rows(cum)
    o_fox, lse_t3 = _fox_fwd(qkv, vt, cum_t3, cum)
    bucket_t = jnp.asarray(_bucket_table().T)
    bias_t = _swa_bias(rel_bias, bucket_t)
    sink_rows = jnp.repeat(sink.reshape(SWA_KV_HEADS, SWA_GROUP, 1), BLOCK, axis=2).reshape(SWA_KV_HEADS, 1, SWA_LANES)
    o_swa, lse_swa = _swa_fwd(qkv, bias_t, sink_rows)

    loss8, dh, grad_w_o_full, do_bf, dz, delta, gg8, gb8 = _post(
        x2, tgt, o_fox, o_swa, z, w_o_full, ln_g, ln_b)

    delta_t3 = _fox_rows(delta)
    dqt_fox, dk_fox, dv_fox, dcum_k, dcum_q = _fox_bwd(qkv, do_bf, cum_t3, cum, lse_t3, delta_t3)
    dcum_q = jnp.pad(dcum_q.reshape(FOX_HEADS, s).T, ((0, 0), (0, LANES - FOX_HEADS)))
    dff, gbf8 = _cum_bwd(dcum_k, dcum_q, ffp, bfp)
    delta_rows = (delta[:, FOX_HEADS:FOX_HEADS + SWA_HEADS].reshape(s // BLOCK, BLOCK, SWA_KV_HEADS, SWA_GROUP)
                  .transpose(0, 2, 3, 1).reshape(s // BLOCK, SWA_KV_HEADS, 1, SWA_LANES))
    dq_swa, dk_swa, dv_swa, grb, gsk8 = _swa_bwd(qkv, do_bf, delta_rows, lse_swa, bias_t, sink_rows, bucket_t)

    dq_fox = dqt_fox.T.astype(BF16)
    d_misc = jnp.concatenate([dk_swa, dv_swa, dff], axis=1)
    pieces = [dq_fox, dk_fox, dv_fox, dq_swa, d_misc, dz]
    grad_x = _grad_x_matmul(pieces, w_pad, dh, tm=512, tn=D_MODEL, name="grad_x")
    blocks = [(p, 0) for p in pieces[:-1]] + [(dz, 0), (dz, 1)]
    grad_w_pad = _grad_w_matmul(xt, blocks, tk=1024, name="grad_w_in")
    grad_w_in_full = _from_padded_cols(grad_w_pad)

    g_in4 = jnp.stack([jnp.pad(grad_w_in_full[:, j * shard_cols:(j + 1) * shard_cols], col_pad)
                       for j in range(N_CHIPS)])
    g_o4 = grad_w_o_full.reshape(N_CHIPS, D_MODEL // N_CHIPS, D_MODEL)
    g_w_in, g_w_o = _join_halves(_chip_reduce(_pair_reduce([g_in4, g_o4])))
    g_w_in = g_w_in[:, :shard_cols]

    cols_first = lambda a: jnp.transpose(a, (2, 0, 1))
    rows_first = lambda a: jnp.transpose(a, (1, 2, 0))
    d_w_in, nm_w_in, nv_w_in = [rows_first(a) for a in _adamw_cols(
        cols_first(w_in), cols_first(g_w_in[None]), cols_first(m_w_in), cols_first(v_w_in), name="adamw_w_in")]
    d_w_o, nm_w_o, nv_w_o = _adamw(w_o2, g_w_o, m_w_o[0], v_w_o[0], name="adamw_w_o")

    loss_row, gs, ds, ms, vs = _small_allreduce_adamw(
        [gbf8, grb, gsk8, gg8, gb8, loss8],
        [b_f, rel_bias, sink, ln_g, ln_b],
        [m_b_f, m_rel_bias, m_sink, m_ln_g, m_ln_b],
        [v_b_f, v_rel_bias, v_sink, v_ln_g, v_ln_b])
    loss = loss_row[0, 0]
    g_bf, g_rb, g_sk, g_lg, g_lb = gs
    d_bf, d_rb, d_sk, d_lg, d_lb = ds
    m_bf, m_rb, m_sk, m_lg, m_lb = ms
    v_bf, v_rb, v_sk, v_lg, v_lb = vs

    e = lambda a: a[None]
    return (loss, e(grad_x),
            e(g_w_in), g_bf, g_rb, g_sk, e(g_w_o), g_lg, g_lb,
            d_w_in, d_bf, d_rb, d_sk, e(d_w_o), d_lg, d_lb,
            nm_w_in, m_bf, m_rb, m_sk, e(nm_w_o), m_lg, m_lb,
            nv_w_in, v_bf, v_rb, v_sk, e(nv_w_o), v_lg, v_lb)
```

```python
import functools
import math

import numpy as np
import jax
import jax.numpy as jnp
from jax import lax
from jax.experimental import pallas as pl
from jax.experimental.pallas import tpu as pltpu

F32 = jnp.float32
BF16 = jnp.bfloat16

D_MODEL = 1024
HEAD_DIM = 64
FOX_HEADS = 8
SWA_HEADS = 8
SWA_KV_HEADS = 2
SWA_GROUP = 4
FOX_W = 512
SWA_W = 512
SWA_KV_W = 128
BLOCK = 128
NUM_BUCKETS = 32
MAX_DISTANCE = 128
LN_EPS = 1e-5
NEG = -1e30
ALPHA = 2.0 ** 0.25
QK_SCALE = 0.125

ADAM_LR = 0.001
ADAM_B1 = 0.9
ADAM_B2 = 0.999
ADAM_EPS = 1e-08
ADAM_WD = 0.01
ADAM_STEP = 10

D_IN = 3336
SHARD_PAD = 896
N_A = 2304
N_C = 256
N_B = 1024
OFF_C = N_A
OFF_B = N_A + N_C
N_PAD = N_A + N_C + N_B
COL_FK, COL_FV, COL_SQ, COL_SK, COL_SV = 512, 1024, 1536, 2048, 2176

LANES = 128
FOX_T = 256
FOX_REF = 512
SUM_ROWS = 16
VMEM_LIMIT = 56 * 1024 * 1024

MESH = pl.DeviceIdType.MESH
N_CHIPS = 4
N_DEV = 8
SMALL_ROWS = 56


def _cparams(sem=None):
    return pltpu.CompilerParams(dimension_semantics=sem, vmem_limit_bytes=VMEM_LIMIT)


def _split3(x):
    hi = x.astype(BF16)
    r = x - hi.astype(F32)
    mid = r.astype(BF16)
    lo = (r - mid.astype(F32)).astype(BF16)
    return hi, mid, lo


def _dot(a, b):
    return jnp.dot(a, b, preferred_element_type=F32)


def _dot_nt(a, b):
    return lax.dot_general(a, b, (((1,), (1,)), ((), ())), preferred_element_type=F32)


def _project(x, w_pad):
    s, k = x.shape
    tm = 512
    chunk = 512

    def kern(x_ref, w_ref, qkv_ref, ff_ref, z_ref, xt_ref, vt_ref):
        xf = x_ref[...]
        xb = xf.astype(BF16)
        xt_ref[...] = xf.T.astype(BF16)
        for c0 in range(0, N_A, chunk):
            width = min(chunk, N_A - c0)
            res = _dot(xb, w_ref[:, c0:c0 + width])
            qkv_ref[:, c0:c0 + width] = res.astype(BF16)
            if c0 == COL_FV:
                vt_ref[...] = res.T.astype(BF16)
        ff_ref[...] = _dot(xb, w_ref[:, OFF_C:OFF_C + N_C])
        for c0 in range(0, N_B, 512):
            z_ref[:, c0:c0 + 512] = _dot(xb, w_ref[:, OFF_B + c0:OFF_B + c0 + 512])

    row = lambda i: (i, 0)
    return pl.pallas_call(
        kern, name="project",
        grid=(s // tm,),
        in_specs=[pl.BlockSpec((tm, k), row),
                  _resident((k, N_PAD), lambda i: (0, 0))],
        out_specs=[pl.BlockSpec((tm, N_A), row),
                   pl.BlockSpec((tm, N_C), row),
                   pl.BlockSpec((tm, N_B), row),
                   pl.BlockSpec((k, tm), lambda i: (0, i)),
                   pl.BlockSpec((FOX_W, tm), lambda i: (0, i))],
        out_shape=[jax.ShapeDtypeStruct((s, N_A), BF16),
                   jax.ShapeDtypeStruct((s, N_C), F32),
                   jax.ShapeDtypeStruct((s, N_B), F32),
                   jax.ShapeDtypeStruct((k, s), BF16),
                   jax.ShapeDtypeStruct((FOX_W, s), BF16)],
        compiler_params=_cparams(("parallel",)),
    )(x, w_pad)


def _grad_x_matmul(pieces, w_pad, dh, *, tm, tn, name):
    m = dh.shape[0]
    n, k = w_pad.shape
    widths = [p.shape[1] for p in pieces]
    offs = [sum(widths[:i]) for i in range(len(pieces))]
    assert sum(widths) == k

    def kern(*refs):
        p_refs, (b_ref, dh_ref, o_ref) = refs[:len(pieces)], refs[len(pieces):]
        acc = ALPHA * dh_ref[...]
        for p_ref, off, width in zip(p_refs, offs, widths):
            acc = acc + _dot_nt(p_ref[...], b_ref[:, off:off + width])
        o_ref[...] = acc

    assert tn == n
    return pl.pallas_call(
        kern, name=name,
        grid=(m // tm,),
        in_specs=[pl.BlockSpec((tm, w), lambda i: (i, 0)) for w in widths]
        + [_resident((n, k), lambda i: (0, 0)),
           pl.BlockSpec((tm, n), lambda i: (i, 0))],
        out_specs=pl.BlockSpec((tm, n), lambda i: (i, 0)),
        out_shape=jax.ShapeDtypeStruct((m, n), F32),
        compiler_params=_cparams(("parallel",)),
    )(*pieces, w_pad, dh)


def _grad_w_matmul(xt, blocks, *, tk, name):
    m, s = xt.shape
    tn = 512
    nb = len(blocks)

    def kern(a_ref, *refs):
        b_refs, o_ref = refs[:nb], refs[nb]

        @pl.when(pl.program_id(0) == 0)
        def _():
            o_ref[...] = jnp.zeros_like(o_ref)
        a = a_ref[...]
        for blk in range(nb):
            o_ref[:, blk * tn:(blk + 1) * tn] += _dot(a, b_refs[blk][...])

    return pl.pallas_call(
        kern, name=name,
        grid=(s // tk,),
        in_specs=[pl.BlockSpec((m, tk), lambda k: (0, k))]
        + [pl.BlockSpec((tk, tn), functools.partial(lambda k, col: (k, col), col=col)) for _, col in blocks],
        out_specs=_resident((m, nb * tn), lambda k: (0, 0)),
        out_shape=jax.ShapeDtypeStruct((m, nb * tn), F32),
        compiler_params=_cparams(("arbitrary",)),
    )(xt, *[arr for arr, _ in blocks])


def _tri(n, lower):
    r = lax.broadcasted_iota(jnp.int32, (n, n), 0)
    c = lax.broadcasted_iota(jnp.int32, (n, n), 1)
    keep = (c <= r) if lower else (c >= r)
    return jnp.where(keep, 1.0, 0.0).astype(BF16)


def _exact_dot(mat_bf16, x_f32, left):
    out = None
    for piece in _split3(x_f32):
        t = _dot(mat_bf16, piece) if left else _dot(piece, mat_bf16)
        out = t if out is None else out + t
    return out


def _log_sigmoid(z):
    return jnp.minimum(z, 0.0) - jnp.log(1.0 + jnp.exp(-jnp.abs(z)))


def _cum_fwd(ffp, bfp):
    s = ffp.shape[0]
    t = min(512, s)

    def kern(ff_ref, b_ref, cum_ref, carry_ref):
        @pl.when(pl.program_id(0) == 0)
        def _():
            carry_ref[...] = jnp.zeros_like(carry_ref)
        lane = lax.broadcasted_iota(jnp.int32, (1, LANES), 1)
        lf = _log_sigmoid(ff_ref[...] + b_ref[...])
        lf = jnp.where(lane < FOX_HEADS, lf, 0.0)
        cum = _exact_dot(_tri(t, True), lf, True) + carry_ref[0:1, :]
        cum_ref[...] = cum
        carry_ref[...] = jnp.broadcast_to(cum[t - 1:t, :], carry_ref.shape)

    return pl.pallas_call(
        kern, name="cum_fwd",
        grid=(s // t,),
        in_specs=[pl.BlockSpec((t, LANES), lambda i: (i, 0)),
                  pl.BlockSpec((1, LANES), lambda i: (0, 0))],
        out_specs=pl.BlockSpec((t, LANES), lambda i: (i, 0)),
        out_shape=jax.ShapeDtypeStruct((s, LANES), F32),
        scratch_shapes=[pltpu.VMEM((8, LANES), F32)],
        compiler_params=_cparams(("arbitrary",)),
    )(ffp, bfp)


def _cum_bwd(dcum_k, dcum_q, ffp, bfp):
    s = dcum_k.shape[0]
    t = min(512, s)
    nb = s // t

    def kern(dck_ref, dcq_ref, ff_ref, b_ref, dff_ref, gb_ref, carry_ref):
        @pl.when(pl.program_id(0) == 0)
        def _():
            carry_ref[...] = jnp.zeros_like(carry_ref)
            gb_ref[...] = jnp.zeros_like(gb_ref)
        lane = lax.broadcasted_iota(jnp.int32, (1, LANES), 1)
        dlf = _exact_dot(_tri(t, False), dck_ref[...] + dcq_ref[...], True) + carry_ref[0:1, :]
        carry_ref[...] = jnp.broadcast_to(dlf[0:1, :], carry_ref.shape)
        z = ff_ref[...] + b_ref[...]
        dff = jnp.where(lane < FOX_HEADS, dlf / (1.0 + jnp.exp(z)), 0.0)
        gb_ref[...] += jnp.broadcast_to(jnp.sum(dff, axis=0, keepdims=True), gb_ref.shape)
        dff_ref[...] = jnp.concatenate([dff, jnp.zeros_like(dff)], axis=1).astype(BF16)

    return pl.pallas_call(
        kern, name="cum_bwd",
        grid=(nb,),
        in_specs=[pl.BlockSpec((t, LANES), lambda i: (nb - 1 - i, 0)),
                  pl.BlockSpec((t, LANES), lambda i: (nb - 1 - i, 0)),
                  pl.BlockSpec((t, LANES), lambda i: (nb - 1 - i, 0)),
                  pl.BlockSpec((1, LANES), lambda i: (0, 0))],
        out_specs=[pl.BlockSpec((t, N_C), lambda i: (nb - 1 - i, 0)),
                   pl.BlockSpec((8, LANES), lambda i: (0, 0))],
        out_shape=[jax.ShapeDtypeStruct((s, N_C), BF16),
                   jax.ShapeDtypeStruct((8, LANES), F32)],
        scratch_shapes=[pltpu.VMEM((8, LANES), F32)],
        compiler_params=_cparams(("arbitrary",)),
    )(dcum_k, dcum_q, ffp, bfp)


def _resident(shape, index_map):
    return pl.BlockSpec(shape, index_map, pipeline_mode=pl.Buffered(1))


def _fox_fwd(qkv, vt, cum_t3, cum):
    s = qkv.shape[0]
    tk = tq = FOX_REF
    nq = s // tq
    nh = FOX_HEADS
    diag_tiles = tq // tk

    def kern(q_ref, k_ref, vt_ref, ct_ref, c_ref, o_ref, lse_ref, m_ref, acc_ref, u_ref):
        i = pl.program_id(0)
        lane = lax.broadcasted_iota(jnp.int32, (1, LANES), 1)
        krow = lax.broadcasted_iota(jnp.int32, (tk, tq), 0)
        qcol = lax.broadcasted_iota(jnp.int32, (tk, tq), 1)
        q0 = pl.multiple_of(i * tq, tq)
        qts, crefs = [], []
        for h in range(nh):
            p, a = divmod(h, 2)
            q2 = q_ref[:, p * LANES:(p + 1) * LANES] * jnp.asarray(QK_SCALE, BF16)
            sel = (lane < HEAD_DIM) if a == 0 else (lane >= HEAD_DIM)
            qts.append(jnp.where(sel, q2, jnp.zeros_like(q2)).astype(F32).T.astype(BF16))
            crefs.append(ct_ref[h, :, pl.ds(q0, LANES)][:, 0:1])
        m_ref[...] = jnp.full(m_ref.shape, NEG, F32)
        acc_ref[...] = jnp.zeros_like(acc_ref)
        ones = jnp.ones((SUM_ROWS, tk), BF16)

        def tile(j, diag):
            k0 = pl.multiple_of(j * tk, tk)
            cb = c_ref[pl.ds(k0, tk), :]
            sts = [_dot(k_ref[pl.ds(k0, tk), (h // 2) * LANES:(h // 2 + 1) * LANES], qts[h]) for h in range(nh)]
            tile_max = []
            for h in range(nh):
                u = sts[h] - (cb[:, h:h + 1] - crefs[h])
                if diag is not None:
                    u = jnp.where(krow + diag * tk <= qcol, u, NEG)
                u_ref[h] = u
                tile_max.append(jnp.max(u, axis=0, keepdims=True))
            pts, scales = [], []
            for h in range(nh):
                m_old = m_ref[h]
                m_new = jnp.maximum(m_old, tile_max[h])
                scales.append(jnp.exp(m_old - m_new))
                pts.append(jnp.exp(u_ref[h] - m_new).astype(BF16))
                m_ref[h] = m_new
            for h in range(nh):
                vth = jnp.concatenate([vt_ref[h * HEAD_DIM:(h + 1) * HEAD_DIM, pl.ds(k0, tk)], ones], axis=0)
                acc_ref[h] = scales[h] * acc_ref[h] + _dot(vth, pts[h])

        def body(j, c):
            tile(j, None)
            return c
        lax.fori_loop(0, i * diag_tiles, body, 0)
        for d in range(diag_tiles):
            tile(i * diag_tiles + d, d)

        ls = [acc_ref[h][HEAD_DIM:HEAD_DIM + 1] for h in range(nh)]
        for p in range(nh // 2):
            ot = jnp.concatenate([acc_ref[2 * p + a][:HEAD_DIM] * (1.0 / ls[2 * p + a]) for a in range(2)], axis=0)
            o_ref[:, p * LANES:(p + 1) * LANES] = ot.T
        for h in range(nh):
            lse_ref[h, :, pl.ds(q0, tq)] = m_ref[h] + jnp.log(ls[h])

    return pl.pallas_call(
        kern, name="fox_fwd",
        grid=(nq,),
        in_specs=[pl.BlockSpec((tq, FOX_W), lambda i: (i, 0)),
                  _resident((s, FOX_W), lambda i: (0, COL_FK // FOX_W)),
                  _resident((FOX_W, s), lambda i: (0, 0)),
                  _resident((nh, 1, s), lambda i: (0, 0, 0)),
                  _resident((s, LANES), lambda i: (0, 0))],
        out_specs=[pl.BlockSpec((tq, FOX_W), lambda i: (i, 0)),
                   pl.BlockSpec((nh, 1, s), lambda i: (0, 0, 0))],
        out_shape=[jax.ShapeDtypeStruct((s, FOX_W), F32),
                   jax.ShapeDtypeStruct((nh, 1, s), F32)],
        scratch_shapes=[pltpu.VMEM((nh, 1, tq), F32),
                        pltpu.VMEM((nh, HEAD_DIM + SUM_ROWS, tq), F32),
                        pltpu.VMEM((nh, tk, tq), F32)],
        compiler_params=_cparams(("arbitrary",)),
    )(qkv, qkv, vt, cum_t3, cum)


def _fox_bwd(qkv, do_bf, cum_t3, cum, lse_t3, delta_t3):
    s = qkv.shape[0]
    t = min(FOX_T, s)
    nq = s // t
    nh = FOX_HEADS
    npair = nh // 2

    def kern(q_ref, do_ref, k_ref, v_ref, ct_ref, c_ref, lse_ref, dl_ref,
             dqt_ref, dk_ref, dv_ref, dc_ref, dcq_ref, accv_ref, acck_ref, accd_ref):
        kj = pl.program_id(0)
        lane = lax.broadcasted_iota(jnp.int32, (1, LANES), 1)
        krow = lax.broadcasted_iota(jnp.int32, (t, t), 0)
        qcol = lax.broadcasted_iota(jnp.int32, (t, t), 1)
        causal = krow <= qcol
        sels = [lane < HEAD_DIM, lane >= HEAD_DIM]

        @pl.when(kj == 0)
        def _():
            dqt_ref[...] = jnp.zeros_like(dqt_ref)
            dcq_ref[...] = jnp.zeros_like(dcq_ref)

        accv_ref[...] = jnp.zeros_like(accv_ref)
        acck_ref[...] = jnp.zeros_like(acck_ref)
        accd_ref[...] = jnp.zeros_like(accd_ref)
        cb = c_ref[...]
        k2s, v2s, kts = [], [], []
        for p in range(npair):
            k2 = k_ref[:, p * LANES:(p + 1) * LANES]
            k2s.append(k2)
            v2s.append(v_ref[:, p * LANES:(p + 1) * LANES])
            kt = k2.astype(F32).T * QK_SCALE
            kts.append(kt[:HEAD_DIM].astype(BF16))
            kts.append(kt[HEAD_DIM:].astype(BF16))
        css = [cb[:, h:h + 1] for h in range(nh)]

        def tile(i, masked):
            q0 = pl.multiple_of(i * t, t)
            r0 = pl.multiple_of((i // (FOX_REF // t)) * FOX_REF, FOX_REF)
            sts, dpts, qms, doms = [], [], [], []
            for h in range(nh):
                p, a = divmod(h, 2)
                qi = q_ref[pl.ds(q0, t), p * LANES:(p + 1) * LANES] * jnp.asarray(QK_SCALE, BF16)
                doi = do_ref[pl.ds(q0, t), p * LANES:(p + 1) * LANES]
                qm = jnp.where(sels[a], qi, jnp.zeros_like(qi))
                dom = jnp.where(sels[a], doi, jnp.zeros_like(doi))
                qms.append(qm)
                doms.append(dom)
                sts.append(_dot_nt(k2s[p], qm))
                dpts.append(_dot_nt(v2s[p], dom))
            pts, dsts = [], []
            for h in range(nh):
                cref = ct_ref[h, :, pl.ds(r0, LANES)][:, 0:1]
                pt = jnp.exp(sts[h] - (css[h] - cref) - lse_ref[h, :, pl.ds(q0, t)])
                if masked:
                    pt = jnp.where(causal, pt, 0.0)
                ds32 = pt * (dpts[h] - dl_ref[h, :, pl.ds(q0, t)])
                part = ds32[:, 0:LANES]
                for c in range(1, t // LANES):
                    part = part + ds32[:, c * LANES:(c + 1) * LANES]
                accd_ref[h] += part
                dcq_ref[h, :, pl.ds(q0, t)] += jnp.sum(ds32, axis=0, keepdims=True)
                pts.append(pt.astype(BF16))
                dsts.append(ds32.astype(BF16))
            for p in range(npair):
                ha, hb = 2 * p, 2 * p + 1
                accv_ref[p] += _dot(pts[ha], doms[ha]) + _dot(pts[hb], doms[hb])
                acck_ref[p] += _dot(dsts[ha], qms[ha]) + _dot(dsts[hb], qms[hb])
            for h in range(nh):
                dqt_ref[h * HEAD_DIM:(h + 1) * HEAD_DIM, pl.ds(q0, t)] += _dot(kts[h], dsts[h])

        tile(kj, True)

        def body(i, c):
            tile(i, False)
            return c
        lax.fori_loop(kj + 1, nq, body, 0)

        dc = jnp.zeros((t, LANES), F32)
        for h in range(nh):
            dc = jnp.where(lane == h, -jnp.sum(accd_ref[h], axis=1, keepdims=True), dc)
        dc_ref[...] = dc
        for p in range(npair):
            dv_ref[:, p * LANES:(p + 1) * LANES] = accv_ref[p].astype(BF16)
            dk_ref[:, p * LANES:(p + 1) * LANES] = acck_ref[p].astype(BF16)

    whole = lambda kj: (0, 0, 0)
    return pl.pallas_call(
        kern, name="fox_bwd",
        grid=(nq,),
        in_specs=[_resident((s, FOX_W), lambda kj: (0, 0)),
                  _resident((s, FOX_W), lambda kj: (0, 0)),
                  pl.BlockSpec((t, FOX_W), lambda kj: (kj, COL_FK // FOX_W)),
                  pl.BlockSpec((t, FOX_W), lambda kj: (kj, COL_FV // FOX_W)),
                  _resident((nh, 1, s), whole),
                  pl.BlockSpec((t, LANES), lambda kj: (kj, 0)),
                  _resident((nh, 1, s), whole),
                  _resident((nh, 1, s), whole)],
        out_specs=[_resident((FOX_W, s), lambda kj: (0, 0)),
                   pl.BlockSpec((t, FOX_W), lambda kj: (kj, 0)),
                   pl.BlockSpec((t, FOX_W), lambda kj: (kj, 0)),
                   pl.BlockSpec((t, LANES), lambda kj: (kj, 0)),
                   _resident((nh, 1, s), whole)],
        out_shape=[jax.ShapeDtypeStruct((FOX_W, s), F32),
                   jax.ShapeDtypeStruct((s, FOX_W), BF16),
                   jax.ShapeDtypeStruct((s, FOX_W), BF16),
                   jax.ShapeDtypeStruct((s, LANES), F32),
                   jax.ShapeDtypeStruct((nh, 1, s), F32)],
        scratch_shapes=[pltpu.VMEM((npair, t, LANES), F32),
                        pltpu.VMEM((npair, t, LANES), F32),
                        pltpu.VMEM((nh, t, LANES), F32)],
        compiler_params=_cparams(("arbitrary",)),
    )(qkv, do_bf, qkv, qkv, cum_t3, cum, lse_t3, delta_t3)


def _bucket_table():
    qi = np.arange(BLOCK)[:, None]
    kj = np.arange(2 * BLOCK)[None, :]
    rel = np.maximum(qi + BLOCK - kj, 0).astype(np.int32)
    max_exact = NUM_BUCKETS // 2
    relf = np.maximum(rel, 1).astype(np.float32)
    large = max_exact + (np.log(relf / np.float32(max_exact)) / np.float32(math.log(MAX_DISTANCE / max_exact))
                         * np.float32(NUM_BUCKETS - max_exact)).astype(np.int32)
    large = np.minimum(large, NUM_BUCKETS - 1)
    return np.where(rel < max_exact, rel, large).astype(np.int32)


SWA_LANES = SWA_GROUP * BLOCK


def _swa_bias(rel_bias, bucket_t):
    def kern(rb_ref, bk_ref, o_ref):
        bk = bk_ref[...]
        kj = lax.broadcasted_iota(jnp.int32, (2 * BLOCK, BLOCK), 0)
        qi = lax.broadcasted_iota(jnp.int32, (2 * BLOCK, BLOCK), 1)
        rel = qi + BLOCK - kj
        band = (rel >= 0) & (rel < BLOCK)
        masks = [band & (kj >= BLOCK), band]
        for h in range(SWA_HEADS):
            g, hh = divmod(h, SWA_GROUP)
            acc = jnp.zeros((2 * BLOCK, BLOCK), F32)
            for b in range(NUM_BUCKETS):
                acc = jnp.where(bk == b, rb_ref[b, h], acc)
            for first in range(2):
                o_ref[first, g, :, hh * BLOCK:(hh + 1) * BLOCK] = jnp.where(masks[first], acc, NEG)

    return pl.pallas_call(
        kern, name="swa_bias",
        in_specs=[pl.BlockSpec(memory_space=pltpu.SMEM),
                  pl.BlockSpec(memory_space=pltpu.VMEM)],
        out_specs=pl.BlockSpec(memory_space=pltpu.VMEM),
        out_shape=jax.ShapeDtypeStruct((2, SWA_KV_HEADS, 2 * BLOCK, SWA_LANES), F32),
        compiler_params=_cparams(),
    )(rel_bias, bucket_t)


SWA_STEP = 2


def _swa_keys(prev_ref, cur_ref):
    return jnp.concatenate([prev_ref[...], cur_ref[...]], axis=0)


def _swa_queries(x_ref, scale):
    x = x_ref[...]
    if scale:
        x = x * jnp.asarray(QK_SCALE, BF16)
    xt = x.astype(F32).T.astype(BF16)
    return [_group_rows(xt[:, b * BLOCK:(b + 1) * BLOCK]) for b in range(SWA_STEP)]


def _group_rows(xt):
    zeros = jnp.zeros((HEAD_DIM, SWA_LANES), BF16)
    out = []
    for g in range(SWA_KV_HEADS):
        heads = [xt[(SWA_GROUP * g + hh) * HEAD_DIM:(SWA_GROUP * g + hh + 1) * HEAD_DIM, :] for hh in range(SWA_GROUP)]
        rows = jnp.concatenate(heads, axis=1)
        padded = jnp.concatenate([rows, zeros] if g == 0 else [zeros, rows], axis=0)
        out.append((rows, padded))
    return out


def _pairs_to_rows(cols_t):
    out = []
    for p in range(SWA_HEADS // 2):
        g, hh = divmod(2 * p, SWA_GROUP)
        pair = jnp.concatenate([cols_t[g][:, hh * BLOCK:(hh + 1) * BLOCK],
                                cols_t[g][:, (hh + 1) * BLOCK:(hh + 2) * BLOCK]], axis=0)
        out.append(pair.T)
    return jnp.concatenate(out, axis=1)


def _swa_fwd(qkv, bias_t, sink_rows):
    s = qkv.shape[0]
    nb = s // BLOCK
    rows = SWA_STEP * BLOCK
    units = [(b, g) for b in range(SWA_STEP) for g in range(SWA_KV_HEADS)]

    def kern(q_ref, kp_ref, kc_ref, vp_ref, vc_ref, bias_ref, sink_ref, o_ref, lse_ref):
        n = pl.program_id(0)
        tables = [jnp.minimum(n, 1)] + [1] * (SWA_STEP - 1)
        k3 = _swa_keys(kp_ref, kc_ref)
        vt3 = _swa_keys(vp_ref, vc_ref).astype(F32).T.astype(BF16)
        qts = _swa_queries(q_ref, True)
        us = [_dot(k3[b * BLOCK:(b + 2) * BLOCK], qts[b][g][1]) + bias_ref[tables[b], g] for b, g in units]
        outs = []
        for (b, g), u in zip(units, us):
            sk = sink_ref[g]
            m = jnp.maximum(jnp.max(u, axis=0, keepdims=True), sk)
            p = jnp.exp(u - m)
            l = jnp.sum(p, axis=0, keepdims=True) + jnp.exp(sk - m)
            lse_ref[b, g] = m + jnp.log(l)
            vt = vt3[g * HEAD_DIM:(g + 1) * HEAD_DIM, b * BLOCK:(b + 2) * BLOCK]
            outs.append(_dot(vt, (p * (1.0 / l)).astype(BF16)))
        for b in range(SWA_STEP):
            o_ref[b * BLOCK:(b + 1) * BLOCK, :] = _pairs_to_rows(outs[b * SWA_KV_HEADS:(b + 1) * SWA_KV_HEADS])

    cq, ck, cv = COL_SQ // SWA_W, COL_SK // LANES, COL_SV // LANES
    prev = lambda n: jnp.maximum(SWA_STEP * n - 1, 0)
    return pl.pallas_call(
        kern, name="swa_fwd",
        grid=(nb // SWA_STEP,),
        in_specs=[pl.BlockSpec((rows, SWA_W), lambda n: (n, cq)),
                  pl.BlockSpec((BLOCK, LANES), lambda n: (prev(n), ck)),
                  pl.BlockSpec((rows, LANES), lambda n: (n, ck)),
                  pl.BlockSpec((BLOCK, LANES), lambda n: (prev(n), cv)),
                  pl.BlockSpec((rows, LANES), lambda n: (n, cv)),
                  _resident((2, SWA_KV_HEADS, 2 * BLOCK, SWA_LANES), lambda n: (0, 0, 0, 0)),
                  _resident((SWA_KV_HEADS, 1, SWA_LANES), lambda n: (0, 0, 0))],
        out_specs=[pl.BlockSpec((rows, SWA_W), lambda n: (n, 0)),
                   pl.BlockSpec((SWA_STEP, SWA_KV_HEADS, 1, SWA_LANES), lambda n: (n, 0, 0, 0))],
        out_shape=[jax.ShapeDtypeStruct((s, SWA_W), F32),
                   jax.ShapeDtypeStruct((nb, SWA_KV_HEADS, 1, SWA_LANES), F32)],
        compiler_params=_cparams(("parallel",)),
    )(qkv, qkv, qkv, qkv, qkv, bias_t, sink_rows)


def _swa_bwd(qkv, do_bf, delta_rows, lse, bias_t, sink_rows, bucket_t):
    s = qkv.shape[0]
    nb = s // BLOCK
    steps = nb // SWA_STEP
    rows = SWA_STEP * BLOCK
    units = [(b, g) for b in range(SWA_STEP) for g in range(SWA_KV_HEADS)]

    def kern(q_ref, kp_ref, kc_ref, vp_ref, vc_ref, do_ref, dl_ref, lse_ref, bias_ref, sink_ref, bk_ref,
             dq_ref, dk_ref, dv_ref, grb_ref, gsk_ref, dbias_ref, ck_ref, cv_ref, sk_ref):
        n = pl.program_id(0)

        @pl.when(n == 0)
        def _():
            dbias_ref[...] = jnp.zeros_like(dbias_ref)
            ck_ref[...] = jnp.zeros_like(ck_ref)
            cv_ref[...] = jnp.zeros_like(cv_ref)
            sk_ref[...] = jnp.zeros_like(sk_ref)

        @pl.when(n < steps)
        def _():
            tables = [jnp.minimum(n, 1)] + [1] * (SWA_STEP - 1)
            k3 = _swa_keys(kp_ref, kc_ref)
            v3 = _swa_keys(vp_ref, vc_ref)
            kt3 = (k3.astype(F32).T * QK_SCALE).astype(BF16)
            qts = _swa_queries(q_ref, True)
            dots = _swa_queries(do_ref, False)
            sts = [_dot(k3[b * BLOCK:(b + 2) * BLOCK], qts[b][g][1]) for b, g in units]
            dps = [_dot(v3[b * BLOCK:(b + 2) * BLOCK], dots[b][g][1]) for b, g in units]
            ps, dss = [], []
            for i, (b, g) in enumerate(units):
                lse_g = lse_ref[b, g]
                dlt = dl_ref[b, g]
                p = jnp.exp(sts[i] + bias_ref[tables[b], g] - lse_g)
                ds = p * (dps[i] - dlt)
                dbias_ref[g] += ds
                sk_ref[g] += -jnp.exp(sink_ref[g] - lse_g) * dlt
                ps.append(p.astype(BF16))
                dss.append(ds.astype(BF16))
            dk2, dv2 = [], []
            for b in range(SWA_STEP):
                at = lambda g: b * SWA_KV_HEADS + g
                groups = range(SWA_KV_HEADS)
                dv2.append(jnp.concatenate([_dot_nt(dots[b][g][0], ps[at(g)]) for g in groups], axis=0).T)
                dk2.append(jnp.concatenate([_dot_nt(qts[b][g][0], dss[at(g)]) for g in groups], axis=0).T)
                dqts = [_dot(kt3[g * HEAD_DIM:(g + 1) * HEAD_DIM, b * BLOCK:(b + 2) * BLOCK], dss[at(g)]) for g in groups]
                dq_ref[b * BLOCK:(b + 1) * BLOCK, :] = _pairs_to_rows(dqts).astype(BF16)
            for acc_ref, out_ref, parts in ((ck_ref, dk_ref, dk2), (cv_ref, dv_ref, dv2)):
                done = acc_ref[BLOCK:] + parts[0][:BLOCK]
                out_ref[...] = jnp.concatenate([acc_ref[:BLOCK], done], axis=0).astype(BF16)
                acc_ref[:BLOCK] = parts[0][BLOCK:] + parts[1][:BLOCK]
                acc_ref[BLOCK:] = parts[1][BLOCK:]

        @pl.when(n == steps)
        def _():
            dk_ref[...] = ck_ref[...].astype(BF16)
            dv_ref[...] = cv_ref[...].astype(BF16)
            bk = bk_ref[...]
            lane = lax.broadcasted_iota(jnp.int32, (8, LANES), 1)
            rowi = lax.broadcasted_iota(jnp.int32, (NUM_BUCKETS, LANES), 0)
            lanei = lax.broadcasted_iota(jnp.int32, (NUM_BUCKETS, LANES), 1)
            out = jnp.zeros((NUM_BUCKETS, LANES), F32)
            gsk = jnp.zeros((8, LANES), F32)
            for h in range(SWA_HEADS):
                g, hh = divmod(h, SWA_GROUP)
                cols = slice(hh * BLOCK, (hh + 1) * BLOCK)
                gsk = jnp.where(lane == h, jnp.sum(sk_ref[g][:, cols]), gsk)
                db = dbias_ref[g][:, cols]
                for b in range(NUM_BUCKETS):
                    val = jnp.sum(jnp.where(bk == b, db, 0.0))
                    out = jnp.where((rowi == b) & (lanei == h), val, out)
            grb_ref[...] = out
            gsk_ref[...] = gsk

    cq, ck, cv = COL_SQ // SWA_W, COL_SK // LANES, COL_SV // LANES
    cur = lambda n: jnp.minimum(n, steps - 1)
    prev = lambda n: jnp.maximum(SWA_STEP * cur(n) - 1, 0)
    kout = lambda n: jnp.maximum(n - 1, 0)
    stat = pl.BlockSpec((SWA_STEP, SWA_KV_HEADS, 1, SWA_LANES), lambda n: (cur(n), 0, 0, 0))
    return pl.pallas_call(
        kern, name="swa_bwd",
        grid=(steps + 1,),
        in_specs=[pl.BlockSpec((rows, SWA_W), lambda n: (cur(n), cq)),
                  pl.BlockSpec((BLOCK, LANES), lambda n: (prev(n), ck)),
                  pl.BlockSpec((rows, LANES), lambda n: (cur(n), ck)),
                  pl.BlockSpec((BLOCK, LANES), lambda n: (prev(n), cv)),
                  pl.BlockSpec((rows, LANES), lambda n: (cur(n), cv)),
                  pl.BlockSpec((rows, SWA_W), lambda n: (cur(n), 1)),
                  stat, stat,
                  _resident((2, SWA_KV_HEADS, 2 * BLOCK, SWA_LANES), lambda n: (0, 0, 0, 0)),
                  _resident((SWA_KV_HEADS, 1, SWA_LANES), lambda n: (0, 0, 0)),
                  _resident((2 * BLOCK, BLOCK), lambda n: (0, 0))],
        out_specs=[pl.BlockSpec((rows, SWA_W), lambda n: (cur(n), 0)),
                   pl.BlockSpec((rows, LANES), lambda n: (kout(n), 0)),
                   pl.BlockSpec((rows, LANES), lambda n: (kout(n), 0)),
                   pl.BlockSpec((NUM_BUCKETS, LANES), lambda n: (0, 0)),
                   pl.BlockSpec((8, LANES), lambda n: (0, 0))],
        out_shape=[jax.ShapeDtypeStruct((s, SWA_W), BF16),
                   jax.ShapeDtypeStruct((s, LANES), BF16),
                   jax.ShapeDtypeStruct((s, LANES), BF16),
                   jax.ShapeDtypeStruct((NUM_BUCKETS, LANES), F32),
                   jax.ShapeDtypeStruct((8, LANES), F32)],
        scratch_shapes=[pltpu.VMEM((SWA_KV_HEADS, 2 * BLOCK, SWA_LANES), F32),
                        pltpu.VMEM((rows, LANES), F32),
                        pltpu.VMEM((rows, LANES), F32),
                        pltpu.VMEM((SWA_KV_HEADS, 1, SWA_LANES), F32)],
        compiler_params=_cparams(("arbitrary",)),
    )(qkv, qkv, qkv, qkv, qkv, do_bf, delta_rows, lse, bias_t, sink_rows, bucket_t)


def _post(x, target, o_fox, o_swa, z, w_o, ln_g, ln_b):
    s = x.shape[0]
    tm = min(256, s)
    nt = s // tm

    def kern(x_ref, t_ref, of_ref, os_ref, z_ref, w_ref, g_ref, b_ref,
             loss_ref, dh_ref, gwo_ref, do_ref, dz_ref, dl_ref, gg_ref, gb_ref, lacc_ref):
        step = pl.program_id(0)

        @pl.when(step == 0)
        def _():
            lacc_ref[...] = jnp.zeros_like(lacc_ref)
            gg_ref[...] = jnp.zeros_like(gg_ref)
            gwo_ref[...] = jnp.zeros_like(gwo_ref)
            gb_ref[...] = jnp.zeros_like(gb_ref)

        o = jnp.concatenate([of_ref[...], os_ref[...]], axis=1)
        zz = z_ref[...]
        sig = 1.0 / (1.0 + jnp.exp(-zz))
        silu = zz * sig
        mixed32 = o * silu
        mixed = mixed32.astype(BF16)
        w = w_ref[...]
        h = ALPHA * x_ref[...] + _dot(mixed, w)
        mu = jnp.mean(h, axis=1, keepdims=True)
        hc = h - mu
        var = jnp.mean(hc * hc, axis=1, keepdims=True)
        rstd = lax.rsqrt(var + LN_EPS)
        xhat = hc * rstd
        g = g_ref[...]
        err = xhat * g + b_ref[...] - t_ref[...]
        lacc_ref[...] += jnp.broadcast_to(jnp.sum(err * err, axis=0, keepdims=True), lacc_ref.shape)
        dout = err * (1.0 / D_MODEL)
        gg_ref[...] += jnp.broadcast_to(jnp.sum(dout * xhat, axis=0, keepdims=True), gg_ref.shape)
        gb_ref[...] += jnp.broadcast_to(jnp.sum(dout, axis=0, keepdims=True), gb_ref.shape)
        dxh = dout * g
        m1 = jnp.mean(dxh, axis=1, keepdims=True)
        m2 = jnp.mean(dxh * xhat, axis=1, keepdims=True)
        dh = rstd * (dxh - m1 - xhat * m2)
        dh_ref[...] = dh
        dy = dh.astype(BF16)
        gwo_ref[...] += _dot(mixed32.T.astype(BF16), dy)
        dmix = _dot_nt(dy, w)
        do = dmix * silu
        do_ref[...] = do.astype(BF16)
        dz_ref[...] = (dmix * o * (sig * (1.0 + zz * (1.0 - sig)))).astype(BF16)
        r = lax.broadcasted_iota(jnp.int32, (D_MODEL, LANES), 0) // HEAD_DIM
        c = lax.broadcasted_iota(jnp.int32, (D_MODEL, LANES), 1)
        pick = jnp.where(r == c, 1.0, 0.0).astype(BF16)
        dl_ref[...] = _exact_dot(pick, do * o, False)

        @pl.when(step == nt - 1)
        def _():
            tot = jnp.sum(lacc_ref[0:1, :]) * (0.5 / D_MODEL)
            loss_ref[...] = jnp.broadcast_to(tot, loss_ref.shape)

    row = lambda i: (i, 0)
    fixed = lambda i: (0, 0)
    wide = pl.BlockSpec((tm, D_MODEL), row)
    half = pl.BlockSpec((tm, FOX_W), row)
    return pl.pallas_call(
        kern, name="post",
        grid=(nt,),
        in_specs=[wide, wide, half, half, wide,
                  pl.BlockSpec((D_MODEL, D_MODEL), fixed),
                  pl.BlockSpec((1, D_MODEL), fixed),
                  pl.BlockSpec((1, D_MODEL), fixed)],
        out_specs=[pl.BlockSpec((8, LANES), fixed), wide,
                   _resident((D_MODEL, D_MODEL), fixed), wide, wide,
                   pl.BlockSpec((tm, LANES), row),
                   pl.BlockSpec((8, D_MODEL), fixed), pl.BlockSpec((8, D_MODEL), fixed)],
        out_shape=[jax.ShapeDtypeStruct((8, LANES), F32),
                   jax.ShapeDtypeStruct((s, D_MODEL), F32),
                   jax.ShapeDtypeStruct((D_MODEL, D_MODEL), F32),
                   jax.ShapeDtypeStruct((s, D_MODEL), BF16),
                   jax.ShapeDtypeStruct((s, D_MODEL), BF16),
                   jax.ShapeDtypeStruct((s, LANES), F32),
                   jax.ShapeDtypeStruct((8, D_MODEL), F32),
                   jax.ShapeDtypeStruct((8, D_MODEL), F32)],
        scratch_shapes=[pltpu.VMEM((8, D_MODEL), F32)],
        compiler_params=_cparams(("arbitrary",)),
    )(x, target, o_fox, o_swa, z, w_o, ln_g, ln_b)


def _adamw_math(w, g, m, v):
    m = ADAM_B1 * m + (1.0 - ADAM_B1) * g
    v = ADAM_B2 * v + (1.0 - ADAM_B2) * (g * g)
    m_hat = m / (1.0 - ADAM_B1 ** ADAM_STEP)
    v_hat = v / (1.0 - ADAM_B2 ** ADAM_STEP)
    delta = -ADAM_LR * (m_hat / (jnp.sqrt(v_hat) + ADAM_EPS) + ADAM_WD * w)
    return delta, m, v


def _adamw(w, g, m, v, *, name):
    r, c = w.shape
    tr = min(256, r)

    def kern(w_ref, g_ref, m_ref, v_ref, d_ref, mo_ref, vo_ref):
        d, mn, vn = _adamw_math(w_ref[...], g_ref[...], m_ref[...], v_ref[...])
        d_ref[...] = d
        mo_ref[...] = mn
        vo_ref[...] = vn

    blk = pl.BlockSpec((tr, c), lambda i: (i, 0))
    sds = jax.ShapeDtypeStruct((r, c), F32)
    return pl.pallas_call(
        kern, name=name,
        grid=(r // tr,),
        in_specs=[blk, blk, blk, blk],
        out_specs=[blk, blk, blk],
        out_shape=[sds, sds, sds],
        compiler_params=_cparams(("parallel",)),
    )(w, g, m, v)


def _adamw_cols(w, g, m, v, *, name):
    c, _, r = w.shape
    tc = 139
    assert c % tc == 0

    def kern(w_ref, g_ref, m_ref, v_ref, d_ref, mo_ref, vo_ref):
        d, mn, vn = _adamw_math(w_ref[...], g_ref[...], m_ref[...], v_ref[...])
        d_ref[...] = d
        mo_ref[...] = mn
        vo_ref[...] = vn

    blk = pl.BlockSpec((tc, 1, r), lambda i: (i, 0, 0))
    sds = jax.ShapeDtypeStruct((c, 1, r), F32)
    return pl.pallas_call(
        kern, name=name,
        grid=(c // tc,),
        in_specs=[blk, blk, blk, blk],
        out_specs=[blk, blk, blk],
        out_shape=[sds, sds, sds],
        compiler_params=_cparams(("parallel",)),
    )(w, g, m, v)


def _position():
    x, y, c = lax.axis_index("x"), lax.axis_index("y"), lax.axis_index("c")
    chips = [(1 - x, y), (x, 1 - y), (1 - x, 1 - y)]
    return x, y, c, chips


def _chip_index(cx, cy):
    return 2 * cx + cy


def _gather_weights(w_in_bf, w_o_bf):
    shards = (w_in_bf, w_o_bf)
    n_arr = len(shards)

    def kern(*refs):
        ins, outs = refs[:n_arr], refs[n_arr:2 * n_arr]
        send_sems, recv_sems, local_sems = refs[2 * n_arr:]
        x, y, c, chips = _position()
        me = _chip_index(x, y)
        sibling = (x, y, 1 - c)

        local = [pltpu.make_async_copy(ins[a], outs[a].at[me], local_sems.at[a]) for a in range(n_arr)]
        for cp in local:
            cp.start()

        def half(ref, a):
            rows = shards[a].shape[0] // 2
            return ref.at[pl.ds(c * rows, rows), :]

        def copy(a, k, src, slot, to):
            return pltpu.make_async_remote_copy(
                src_ref=src, dst_ref=half(outs[a].at[slot], a),
                send_sem=send_sems.at[a * 6 + k], recv_sem=recv_sems.at[a * 6 + k],
                device_id=to, device_id_type=MESH)

        first = [copy(a, j, half(ins[a], a), me, (*chip, c)) for a in range(n_arr) for j, chip in enumerate(chips)]
        for cp in first:
            cp.start()
        passed = []
        for a in range(n_arr):
            for j, chip in enumerate(chips):
                slot = _chip_index(*chip)
                copy(a, j, half(ins[a], a), slot, (*chip, c)).wait_recv()
                fwd = copy(a, 3 + j, half(outs[a].at[slot], a), slot, sibling)
                fwd.start()
                passed.append(fwd)
        for a in range(n_arr):
            for j, chip in enumerate(chips):
                slot = _chip_index(*chip)
                rows = shards[a].shape[0] // 2
                dst = outs[a].at[slot].at[pl.ds((1 - c) * rows, rows), :]
                pltpu.make_async_remote_copy(
                    src_ref=dst, dst_ref=dst, send_sem=send_sems.at[a * 6 + 3 + j],
                    recv_sem=recv_sems.at[a * 6 + 3 + j], device_id=sibling, device_id_type=MESH).wait_recv()
        for cp in first + passed:
            cp.wait_send()
        for cp in local:
            cp.wait()

    vmem = pl.BlockSpec(memory_space=pltpu.VMEM)
    return pl.pallas_call(
        kern, name="gather_weights",
        in_specs=[vmem] * n_arr,
        out_specs=[vmem] * n_arr,
        out_shape=[jax.ShapeDtypeStruct((N_CHIPS,) + w.shape, w.dtype) for w in shards],
        scratch_shapes=[pltpu.SemaphoreType.DMA((6 * n_arr,)),
                        pltpu.SemaphoreType.DMA((6 * n_arr,)),
                        pltpu.SemaphoreType.DMA((n_arr,))],
        compiler_params=_cparams(),
    )(*shards)


def _pair_reduce(grads):
    n_arr = len(grads)
    chunk = 128

    def kern(*refs):
        ins = refs[:n_arr]
        outs = refs[n_arr:2 * n_arr]
        gots = refs[2 * n_arr:3 * n_arr]
        send_sems, recv_sems = refs[3 * n_arr:]
        x, y, c, _ = _position()
        sibling = (x, y, 1 - c)
        copies = []
        for a in range(n_arr):
            rows = grads[a].shape[1] // 2
            copies.append(pltpu.make_async_remote_copy(
                src_ref=ins[a].at[:, pl.ds((1 - c) * rows, rows), :], dst_ref=gots[a],
                send_sem=send_sems.at[a], recv_sem=recv_sems.at[a], device_id=sibling, device_id_type=MESH))
        for cp in copies:
            cp.start()
        for a in range(n_arr):
            copies[a].wait()
            rows = grads[a].shape[1] // 2
            for j in range(N_CHIPS):
                for r0 in range(0, rows, chunk):
                    mine = ins[a][j, pl.ds(pl.multiple_of(c * rows + r0, chunk), chunk), :]
                    outs[a][j, r0:r0 + chunk, :] = (mine + gots[a][j, r0:r0 + chunk, :]).astype(BF16)

    vmem = pl.BlockSpec(memory_space=pltpu.VMEM)
    half = [(N_CHIPS, g.shape[1] // 2, g.shape[2]) for g in grads]
    return pl.pallas_call(
        kern, name="pair_reduce",
        in_specs=[vmem] * n_arr,
        out_specs=[vmem] * n_arr,
        out_shape=[jax.ShapeDtypeStruct(h, BF16) for h in half],
        scratch_shapes=[pltpu.VMEM(h, F32) for h in half]
        + [pltpu.SemaphoreType.DMA((n_arr,)), pltpu.SemaphoreType.DMA((n_arr,))],
        compiler_params=_cparams(),
    )(*grads)


def _chip_reduce(parts):
    n_arr = len(parts)
    chunk = 128

    def kern(*refs):
        ins = refs[:n_arr]
        outs = refs[n_arr:2 * n_arr]
        slabs = refs[2 * n_arr:3 * n_arr]
        send_sems, recv_sems, local_sems = refs[3 * n_arr:]
        x, y, c, chips = _position()
        me = _chip_index(x, y)
        local = [pltpu.make_async_copy(ins[a].at[me], slabs[a].at[me], local_sems.at[a]) for a in range(n_arr)]
        for cp in local:
            cp.start()
        sends = []
        for a in range(n_arr):
            for j, chip in enumerate(chips):
                sends.append(pltpu.make_async_remote_copy(
                    src_ref=ins[a].at[_chip_index(*chip)], dst_ref=slabs[a].at[me],
                    send_sem=send_sems.at[a * 3 + j], recv_sem=recv_sems.at[a * 3 + j],
                    device_id=(*chip, c), device_id_type=MESH))
        for cp in sends:
            cp.start()
        for a in range(n_arr):
            for j, chip in enumerate(chips):
                slot = slabs[a].at[_chip_index(*chip)]
                pltpu.make_async_remote_copy(
                    src_ref=slot, dst_ref=slot, send_sem=send_sems.at[a * 3 + j],
                    recv_sem=recv_sems.at[a * 3 + j], device_id=(*chip, c), device_id_type=MESH).wait_recv()
        for cp in sends:
            cp.wait_send()
        for cp in local:
            cp.wait()
        for a in range(n_arr):
            for r0 in range(0, parts[a].shape[1], chunk):
                f = lambda j: slabs[a][j, r0:r0 + chunk, :].astype(F32)
                outs[a][r0:r0 + chunk, :] = ((f(0) + f(1)) + f(2)) + f(3)

    vmem = pl.BlockSpec(memory_space=pltpu.VMEM)
    return pl.pallas_call(
        kern, name="chip_reduce",
        in_specs=[vmem] * n_arr,
        out_specs=[vmem] * n_arr,
        out_shape=[jax.ShapeDtypeStruct(p.shape[1:], F32) for p in parts],
        scratch_shapes=[pltpu.VMEM(p.shape, BF16) for p in parts]
        + [pltpu.SemaphoreType.DMA((3 * n_arr,)),
           pltpu.SemaphoreType.DMA((3 * n_arr,)),
           pltpu.SemaphoreType.DMA((n_arr,))],
        compiler_params=_cparams(),
    )(*parts)


def _join_halves(halves):
    n_arr = len(halves)

    def kern(*refs):
        ins = refs[:n_arr]
        outs = refs[n_arr:2 * n_arr]
        send_sems, recv_sems, local_sems = refs[2 * n_arr:]
        x, y, c, _ = _position()
        sibling = (x, y, 1 - c)
        local, remote = [], []
        for a in range(n_arr):
            rows = halves[a].shape[0]
            mine = outs[a].at[pl.ds(c * rows, rows), :]
            local.append(pltpu.make_async_copy(ins[a], mine, local_sems.at[a]))
            remote.append(pltpu.make_async_remote_copy(
                src_ref=ins[a], dst_ref=mine, send_sem=send_sems.at[a], recv_sem=recv_sems.at[a],
                device_id=sibling, device_id_type=MESH))
        for cp in local + remote:
            cp.start()
        for a in range(n_arr):
            rows = halves[a].shape[0]
            theirs = outs[a].at[pl.ds((1 - c) * rows, rows), :]
            pltpu.make_async_remote_copy(
                src_ref=theirs, dst_ref=theirs, send_sem=send_sems.at[a], recv_sem=recv_sems.at[a],
                device_id=sibling, device_id_type=MESH).wait_recv()
        for cp in remote:
            cp.wait_send()
        for cp in local:
            cp.wait()

    vmem = pl.BlockSpec(memory_space=pltpu.VMEM)
    return pl.pallas_call(
        kern, name="join_halves",
        in_specs=[vmem] * n_arr,
        out_specs=[vmem] * n_arr,
        out_shape=[jax.ShapeDtypeStruct((2 * h.shape[0], h.shape[1]), F32) for h in halves],
        scratch_shapes=[pltpu.SemaphoreType.DMA((n_arr,)),
                        pltpu.SemaphoreType.DMA((n_arr,)),
                        pltpu.SemaphoreType.DMA((n_arr,))],
        compiler_params=_cparams(),
    )(*halves)


def _small_allreduce_adamw(partials, params, moms, vels):
    chunks = D_MODEL // LANES
    row_rb, row_bf, row_sk, row_loss = 2 * chunks, 2 * chunks + NUM_BUCKETS, 2 * chunks + NUM_BUCKETS + 1, SMALL_ROWS - 6

    def kern(gbf_ref, grb_ref, gsk_ref, gg_ref, gb_ref, loss_ref, *refs):
        p_refs, m_refs, v_refs = refs[0:5], refs[5:10], refs[10:15]
        lo_ref, g_outs, d_outs, mo_outs, vo_outs = refs[15], refs[16:21], refs[21:26], refs[26:31], refs[31:36]
        send_ref, buf_ref, send_sems, recv_sems = refs[36:]
        x, y, c, _ = _position()
        me = 4 * x + 2 * y + c
        send_ref[...] = jnp.zeros_like(send_ref)
        for r in range(chunks):
            send_ref[r:r + 1, :] = gg_ref[0:1, r * LANES:(r + 1) * LANES]
            send_ref[chunks + r:chunks + r + 1, :] = gb_ref[0:1, r * LANES:(r + 1) * LANES]
        send_ref[row_rb:row_rb + NUM_BUCKETS, :] = grb_ref[...]
        send_ref[row_bf:row_bf + 1, :] = gbf_ref[0:1, :]
        send_ref[row_sk:row_sk + 1, :] = gsk_ref[0:1, :]
        send_ref[row_loss:row_loss + 1, :] = loss_ref[0:1, :]
        buf_ref[me] = send_ref[...]
        peers = [(x, y, 1 - c)] + [(px, py, pc) for px, py in _position()[3] for pc in (c, 1 - c)]
        sends = []
        for k, peer in enumerate(peers):
            sends.append(pltpu.make_async_remote_copy(
                src_ref=send_ref, dst_ref=buf_ref.at[me], send_sem=send_sems.at[k], recv_sem=recv_sems.at[k],
                device_id=peer, device_id_type=MESH))
        for cp in sends:
            cp.start()
        for k, (px, py, pc) in enumerate(peers):
            slot = buf_ref.at[4 * px + 2 * py + pc]
            pltpu.make_async_remote_copy(
                src_ref=slot, dst_ref=slot, send_sem=send_sems.at[k], recv_sem=recv_sems.at[k],
                device_id=(px, py, pc), device_id_type=MESH).wait_recv()
        for cp in sends:
            cp.wait_send()
        tot = buf_ref[0]
        for d in range(1, N_DEV):
            tot = tot + buf_ref[d]
        lo_ref[...] = tot[row_loss:row_loss + 1, :]
        grads = [tot[row_bf:row_bf + 1, 0:FOX_HEADS],
                 tot[row_rb:row_rb + NUM_BUCKETS, 0:SWA_HEADS],
                 tot[row_sk:row_sk + 1, 0:SWA_HEADS],
                 jnp.concatenate([tot[r:r + 1, :] for r in range(chunks)], axis=1),
                 jnp.concatenate([tot[chunks + r:chunks + r + 1, :] for r in range(chunks)], axis=1)]
        for i, g in enumerate(grads):
            g_outs[i][...] = g
            delta, mn, vn = _adamw_math(p_refs[i][...], g, m_refs[i][...], v_refs[i][...])
            d_outs[i][...] = delta
            mo_outs[i][...] = mn
            vo_outs[i][...] = vn

    vm = pl.BlockSpec(memory_space=pltpu.VMEM)
    shapes = [jax.ShapeDtypeStruct(p.shape, F32) for p in params]
    outs = pl.pallas_call(
        kern, name="small_allreduce_adamw",
        in_specs=[vm] * 21,
        out_specs=[vm] * 21,
        out_shape=[jax.ShapeDtypeStruct((1, LANES), F32)] + shapes * 4,
        scratch_shapes=[pltpu.VMEM((SMALL_ROWS, LANES), F32),
                        pltpu.VMEM((N_DEV, SMALL_ROWS, LANES), F32),
                        pltpu.SemaphoreType.DMA((N_DEV - 1,)),
                        pltpu.SemaphoreType.DMA((N_DEV - 1,))],
    )(*partials, *params, *moms, *vels)
    return outs[0], outs[1:6], outs[6:11], outs[11:16], outs[16:21]


def _to_padded_cols(w):
    pad = jnp.zeros((w.shape[0], N_C - FOX_HEADS), w.dtype)
    return jnp.concatenate([w[:, 0:1536], w[:, 2056:2824], w[:, 1536:1544], pad,
                            w[:, 1544:2056], w[:, 2824:3336]], axis=1)


def _from_padded_cols(g):
    return jnp.concatenate([g[:, 0:1536], g[:, OFF_C:OFF_C + FOX_HEADS], g[:, OFF_B:OFF_B + FOX_W],
                            g[:, 1536:N_A], g[:, OFF_B + FOX_W:N_PAD]], axis=1)


def _fox_rows(a):
    return a[:, :FOX_HEADS].T.reshape(FOX_HEADS, 1, a.shape[0])


def kernel(x, w_in, b_f, rel_bias, sink, w_o, ln_g, ln_b, loss_target, m_w_in, m_b_f, m_rel_bias, m_sink, m_w_o, m_ln_g, m_ln_b, v_w_in, v_b_f, v_rel_bias, v_sink, v_w_o, v_ln_g, v_ln_b):
    x2 = x[0]
    tgt = loss_target[0]
    s = x2.shape[0]
    w_in2, w_o2 = w_in[0], w_o[0]

    shard_cols = D_IN // N_CHIPS
    col_pad = ((0, 0), (0, SHARD_PAD - shard_cols))
    w_in_all, w_o_all = _gather_weights(jnp.pad(w_in2.astype(BF16), col_pad), w_o2.astype(BF16))
    w_full = jnp.concatenate([w_in_all[j, :, :shard_cols] for j in range(N_CHIPS)], axis=1)
    w_pad = _to_padded_cols(w_full)
    w_o_full = w_o_all.reshape(D_MODEL, D_MODEL)

    qkv, ffp, z, xt, vt = _project(x2, w_pad)
    bfp = jnp.pad(b_f, ((0, 0), (0, LANES - FOX_HEADS)))
    cum = _cum_fwd(ffp, bfp)
    cum_t3 = _fox_rows(cum)
    o_fox, lse_t3 = _fox_fwd(qkv, vt, cum_t3, cum)
    bucket_t = jnp.asarray(_bucket_table().T)
    bias_t = _swa_bias(rel_bias, bucket_t)
    sink_rows = jnp.repeat(sink.reshape(SWA_KV_HEADS, SWA_GROUP, 1), BLOCK, axis=2).reshape(SWA_KV_HEADS, 1, SWA_LANES)
    o_swa, lse_swa = _swa_fwd(qkv, bias_t, sink_rows)

    loss8, dh, grad_w_o_full, do_bf, dz, delta, gg8, gb8 = _post(
        x2, tgt, o_fox, o_swa, z, w_o_full, ln_g, ln_b)

    delta_t3 = _fox_rows(delta)
    dqt_fox, dk_fox, dv_fox, dcum_k, dcum_q = _fox_bwd(qkv, do_bf, cum_t3, cum, lse_t3, delta_t3)
    dcum_q = jnp.pad(dcum_q.reshape(FOX_HEADS, s).T, ((0, 0), (0, LANES - FOX_HEADS)))
    dff, gbf8 = _cum_bwd(dcum_k, dcum_q, ffp, bfp)
    delta_rows = (delta[:, FOX_HEADS:FOX_HEADS + SWA_HEADS].reshape(s // BLOCK, BLOCK, SWA_KV_HEADS, SWA_GROUP)
                  .transpose(0, 2, 3, 1).reshape(s // BLOCK, SWA_KV_HEADS, 1, SWA_LANES))
    dq_swa, dk_swa, dv_swa, grb, gsk8 = _swa_bwd(qkv, do_bf, delta_rows, lse_swa, bias_t, sink_rows, bucket_t)

    dq_fox = dqt_fox.T.astype(BF16)
    d_misc = jnp.concatenate([dk_swa, dv_swa, dff], axis=1)
    pieces = [dq_fox, dk_fox, dv_fox, dq_swa, d_misc, dz]
    grad_x = _grad_x_matmul(pieces, w_pad, dh, tm=512, tn=D_MODEL, name="grad_x")
    blocks = [(p, 0) for p in pieces[:-1]] + [(dz, 0), (dz, 1)]
    grad_w_pad = _grad_w_matmul(xt, blocks, tk=1024, name="grad_w_in")
    grad_w_in_full = _from_padded_cols(grad_w_pad)

    g_in4 = jnp.stack([jnp.pad(grad_w_in_full[:, j * shard_cols:(j + 1) * shard_cols], col_pad)
                       for j in range(N_CHIPS)])
    g_o4 = grad_w_o_full.reshape(N_CHIPS, D_MODEL // N_CHIPS, D_MODEL)
    g_w_in, g_w_o = _join_halves(_chip_reduce(_pair_reduce([g_in4, g_o4])))
    g_w_in = g_w_in[:, :shard_cols]

    cols_first = lambda a: jnp.transpose(a, (2, 0, 1))
    rows_first = lambda a: jnp.transpose(a, (1, 2, 0))
    g_cols = cols_first(g_w_in[None])
    d_w_in, nm_w_in, nv_w_in = [rows_first(a) for a in _adamw_cols(
        cols_first(w_in), g_cols, cols_first(m_w_in), cols_first(v_w_in), name="adamw_w_in")]
    g_w_in = rows_first(g_cols)
    d_w_o, nm_w_o, nv_w_o = _adamw(w_o2, g_w_o, m_w_o[0], v_w_o[0], name="adamw_w_o")

    loss_row, gs, ds, ms, vs = _small_allreduce_adamw(
        [gbf8, grb, gsk8, gg8, gb8, loss8],
        [b_f, rel_bias, sink, ln_g, ln_b],
        [m_b_f, m_rel_bias, m_sink, m_ln_g, m_ln_b],
        [v_b_f, v_rel_bias, v_sink, v_ln_g, v_ln_b])
    loss = loss_row[0, 0]
    g_bf, g_rb, g_sk, g_lg, g_lb = gs
    d_bf, d_rb, d_sk, d_lg, d_lb = ds
    m_bf, m_rb, m_sk, m_lg, m_lb = ms
    v_bf, v_rb, v_sk, v_lg, v_lb = vs

    e = lambda a: a[None]
    return (loss, e(grad_x),
            g_w_in, g_bf, g_rb, g_sk, e(g_w_o), g_lg, g_lb,
            d_w_in, d_bf, d_rb, d_sk, e(d_w_o), d_lg, d_lb,
            nm_w_in, m_bf, m_rb, m_sk, e(nm_w_o), m_lg, m_lb,
            nv_w_in, v_bf, v_rb, v_sk, e(nv_w_o), v_lg, v_lb)
```

```python
import functools
import math

import numpy as np
import jax
import jax.numpy as jnp
from jax import lax
from jax.experimental import pallas as pl
from jax.experimental.pallas import tpu as pltpu

F32 = jnp.float32
BF16 = jnp.bfloat16

D_MODEL = 1024
HEAD_DIM = 64
FOX_HEADS = 8
SWA_HEADS = 8
SWA_KV_HEADS = 2
SWA_GROUP = 4
FOX_W = 512
SWA_W = 512
SWA_KV_W = 128
BLOCK = 128
NUM_BUCKETS = 32
MAX_DISTANCE = 128
LN_EPS = 1e-5
NEG = -1e30
ALPHA = 2.0 ** 0.25
QK_SCALE = 0.125

ADAM_LR = 0.001
ADAM_B1 = 0.9
ADAM_B2 = 0.999
ADAM_EPS = 1e-08
ADAM_WD = 0.01
ADAM_STEP = 10

D_IN = 3336
SHARD_PAD = 896
N_A = 2304
N_C = 256
N_B = 1024
OFF_C = N_A
OFF_B = N_A + N_C
N_PAD = N_A + N_C + N_B
COL_FK, COL_FV, COL_SQ, COL_SK, COL_SV = 512, 1024, 1536, 2048, 2176

LANES = 128
FOX_T = 256
FOX_REF = 512
SUM_ROWS = 16
VMEM_LIMIT = 56 * 1024 * 1024

MESH = pl.DeviceIdType.MESH
N_CHIPS = 4
N_DEV = 8
SMALL_ROWS = 56


def _cparams(sem=None):
    return pltpu.CompilerParams(dimension_semantics=sem, vmem_limit_bytes=VMEM_LIMIT)


def _split3(x):
    hi = x.astype(BF16)
    r = x - hi.astype(F32)
    mid = r.astype(BF16)
    lo = (r - mid.astype(F32)).astype(BF16)
    return hi, mid, lo


def _dot(a, b):
    return jnp.dot(a, b, preferred_element_type=F32)


def _dot_nt(a, b):
    return lax.dot_general(a, b, (((1,), (1,)), ((), ())), preferred_element_type=F32)


def _project(x, w_pad):
    s, k = x.shape
    tm = 512
    chunk = 512

    def kern(x_ref, w_ref, qkv_ref, ff_ref, z_ref, xt_ref, vt_ref):
        xf = x_ref[...]
        xb = xf.astype(BF16)
        xt_ref[...] = xf.T.astype(BF16)
        for c0 in range(0, N_A, chunk):
            width = min(chunk, N_A - c0)
            res = _dot(xb, w_ref[:, c0:c0 + width])
            qkv_ref[:, c0:c0 + width] = res.astype(BF16)
            if c0 == COL_FV:
                vt_ref[...] = res.T.astype(BF16)
        ff_ref[...] = _dot(xb, w_ref[:, OFF_C:OFF_C + N_C])
        for c0 in range(0, N_B, 512):
            z_ref[:, c0:c0 + 512] = _dot(xb, w_ref[:, OFF_B + c0:OFF_B + c0 + 512])

    row = lambda i: (i, 0)
    return pl.pallas_call(
        kern, name="project",
        grid=(s // tm,),
        in_specs=[pl.BlockSpec((tm, k), row),
                  _resident((k, N_PAD), lambda i: (0, 0))],
        out_specs=[pl.BlockSpec((tm, N_A), row),
                   pl.BlockSpec((tm, N_C), row),
                   pl.BlockSpec((tm, N_B), row),
                   pl.BlockSpec((k, tm), lambda i: (0, i)),
                   pl.BlockSpec((FOX_W, tm), lambda i: (0, i))],
        out_shape=[jax.ShapeDtypeStruct((s, N_A), BF16),
                   jax.ShapeDtypeStruct((s, N_C), F32),
                   jax.ShapeDtypeStruct((s, N_B), F32),
                   jax.ShapeDtypeStruct((k, s), BF16),
                   jax.ShapeDtypeStruct((FOX_W, s), BF16)],
        compiler_params=_cparams(("parallel",)),
    )(x, w_pad)


def _grad_x_matmul(pieces, w_pad, dh, *, tm, tn, name):
    m = dh.shape[0]
    n, k = w_pad.shape
    widths = [p.shape[1] for p in pieces]
    offs = [sum(widths[:i]) for i in range(len(pieces))]
    assert sum(widths) == k

    def kern(*refs):
        p_refs, (b_ref, dh_ref, o_ref) = refs[:len(pieces)], refs[len(pieces):]
        acc = ALPHA * dh_ref[...]
        for p_ref, off, width in zip(p_refs, offs, widths):
            acc = acc + _dot_nt(p_ref[...], b_ref[:, off:off + width])
        o_ref[...] = acc

    assert tn == n
    return pl.pallas_call(
        kern, name=name,
        grid=(m // tm,),
        in_specs=[pl.BlockSpec((tm, w), lambda i: (i, 0)) for w in widths]
        + [_resident((n, k), lambda i: (0, 0)),
           pl.BlockSpec((tm, n), lambda i: (i, 0))],
        out_specs=pl.BlockSpec((tm, n), lambda i: (i, 0)),
        out_shape=jax.ShapeDtypeStruct((m, n), F32),
        compiler_params=_cparams(("parallel",)),
    )(*pieces, w_pad, dh)


def _grad_w_matmul(xt, blocks, *, tk, name):
    m, s = xt.shape
    tn = 512
    nb = len(blocks)

    def kern(a_ref, *refs):
        b_refs, o_ref = refs[:nb], refs[nb]

        @pl.when(pl.program_id(0) == 0)
        def _():
            o_ref[...] = jnp.zeros_like(o_ref)
        a = a_ref[...]
        for blk in range(nb):
            o_ref[:, blk * tn:(blk + 1) * tn] += _dot(a, b_refs[blk][...])

    return pl.pallas_call(
        kern, name=name,
        grid=(s // tk,),
        in_specs=[pl.BlockSpec((m, tk), lambda k: (0, k))]
        + [pl.BlockSpec((tk, tn), functools.partial(lambda k, col: (k, col), col=col)) for _, col in blocks],
        out_specs=_resident((m, nb * tn), lambda k: (0, 0)),
        out_shape=jax.ShapeDtypeStruct((m, nb * tn), F32),
        compiler_params=_cparams(("arbitrary",)),
    )(xt, *[arr for arr, _ in blocks])


def _tri(n, lower):
    r = lax.broadcasted_iota(jnp.int32, (n, n), 0)
    c = lax.broadcasted_iota(jnp.int32, (n, n), 1)
    keep = (c <= r) if lower else (c >= r)
    return jnp.where(keep, 1.0, 0.0).astype(BF16)


def _exact_dot(mat_bf16, x_f32, left):
    out = None
    for piece in _split3(x_f32):
        t = _dot(mat_bf16, piece) if left else _dot(piece, mat_bf16)
        out = t if out is None else out + t
    return out


def _log_sigmoid(z):
    return jnp.minimum(z, 0.0) - jnp.log(1.0 + jnp.exp(-jnp.abs(z)))


def _cum_fwd(ffp, bfp):
    s = ffp.shape[0]
    t = min(1024, s)

    def kern(ff_ref, b_ref, cum_ref, carry_ref):
        @pl.when(pl.program_id(0) == 0)
        def _():
            carry_ref[...] = jnp.zeros_like(carry_ref)
        lane = lax.broadcasted_iota(jnp.int32, (1, LANES), 1)
        lf = _log_sigmoid(ff_ref[...] + b_ref[...])
        lf = jnp.where(lane < FOX_HEADS, lf, 0.0)
        cum = _exact_dot(_tri(t, True), lf, True) + carry_ref[0:1, :]
        cum_ref[...] = cum
        carry_ref[...] = jnp.broadcast_to(cum[t - 1:t, :], carry_ref.shape)

    return pl.pallas_call(
        kern, name="cum_fwd",
        grid=(s // t,),
        in_specs=[pl.BlockSpec((t, LANES), lambda i: (i, 0)),
                  pl.BlockSpec((1, LANES), lambda i: (0, 0))],
        out_specs=pl.BlockSpec((t, LANES), lambda i: (i, 0)),
        out_shape=jax.ShapeDtypeStruct((s, LANES), F32),
        scratch_shapes=[pltpu.VMEM((8, LANES), F32)],
        compiler_params=_cparams(("arbitrary",)),
    )(ffp, bfp)


def _cum_bwd(dcum_k, dcum_q, ffp, bfp):
    s = dcum_k.shape[0]
    t = min(1024, s)
    nb = s // t

    def kern(dck_ref, dcq_ref, ff_ref, b_ref, dff_ref, gb_ref, carry_ref):
        @pl.when(pl.program_id(0) == 0)
        def _():
            carry_ref[...] = jnp.zeros_like(carry_ref)
            gb_ref[...] = jnp.zeros_like(gb_ref)
        lane = lax.broadcasted_iota(jnp.int32, (1, LANES), 1)
        dlf = _exact_dot(_tri(t, False), dck_ref[...] + dcq_ref[...], True) + carry_ref[0:1, :]
        carry_ref[...] = jnp.broadcast_to(dlf[0:1, :], carry_ref.shape)
        z = ff_ref[...] + b_ref[...]
        dff = jnp.where(lane < FOX_HEADS, dlf / (1.0 + jnp.exp(z)), 0.0)
        gb_ref[...] += jnp.broadcast_to(jnp.sum(dff, axis=0, keepdims=True), gb_ref.shape)
        dff_ref[...] = jnp.concatenate([dff, jnp.zeros_like(dff)], axis=1).astype(BF16)

    return pl.pallas_call(
        kern, name="cum_bwd",
        grid=(nb,),
        in_specs=[pl.BlockSpec((t, LANES), lambda i: (nb - 1 - i, 0)),
                  pl.BlockSpec((t, LANES), lambda i: (nb - 1 - i, 0)),
                  pl.BlockSpec((t, LANES), lambda i: (nb - 1 - i, 0)),
                  pl.BlockSpec((1, LANES), lambda i: (0, 0))],
        out_specs=[pl.BlockSpec((t, N_C), lambda i: (nb - 1 - i, 0)),
                   pl.BlockSpec((8, LANES), lambda i: (0, 0))],
        out_shape=[jax.ShapeDtypeStruct((s, N_C), BF16),
                   jax.ShapeDtypeStruct((8, LANES), F32)],
        scratch_shapes=[pltpu.VMEM((8, LANES), F32)],
        compiler_params=_cparams(("arbitrary",)),
    )(dcum_k, dcum_q, ffp, bfp)


def _resident(shape, index_map):
    return pl.BlockSpec(shape, index_map, pipeline_mode=pl.Buffered(1))


def _fox_fwd(qkv, vt, cum_t3, cum):
    s = qkv.shape[0]
    tk = tq = FOX_REF
    nq = s // tq
    nh = FOX_HEADS
    diag_tiles = tq // tk

    def kern(q_ref, k_ref, vt_ref, ct_ref, c_ref, o_ref, lse_ref, m_ref, acc_ref, u_ref):
        i = pl.program_id(0)
        lane = lax.broadcasted_iota(jnp.int32, (1, LANES), 1)
        krow = lax.broadcasted_iota(jnp.int32, (tk, tq), 0)
        qcol = lax.broadcasted_iota(jnp.int32, (tk, tq), 1)
        q0 = pl.multiple_of(i * tq, tq)
        qts, crefs = [], []
        for h in range(nh):
            p, a = divmod(h, 2)
            q2 = q_ref[:, p * LANES:(p + 1) * LANES] * jnp.asarray(QK_SCALE, BF16)
            sel = (lane < HEAD_DIM) if a == 0 else (lane >= HEAD_DIM)
            qts.append(jnp.where(sel, q2, jnp.zeros_like(q2)).astype(F32).T.astype(BF16))
            crefs.append(ct_ref[h, :, pl.ds(q0, LANES)][:, 0:1])
        m_ref[...] = jnp.full(m_ref.shape, NEG, F32)
        acc_ref[...] = jnp.zeros_like(acc_ref)
        ones = jnp.ones((SUM_ROWS, tk), BF16)

        def tile(j, diag):
            k0 = pl.multiple_of(j * tk, tk)
            cb = c_ref[pl.ds(k0, tk), :]
            sts = [_dot(k_ref[pl.ds(k0, tk), (h // 2) * LANES:(h // 2 + 1) * LANES], qts[h]) for h in range(nh)]
            tile_max = []
            for h in range(nh):
                u = sts[h] - (cb[:, h:h + 1] - crefs[h])
                if diag is not None:
                    u = jnp.where(krow + diag * tk <= qcol, u, NEG)
                u_ref[h] = u
                tile_max.append(jnp.max(u, axis=0, keepdims=True))
            pts, scales = [], []
            for h in range(nh):
                m_old = m_ref[h]
                m_new = jnp.maximum(m_old, tile_max[h])
                scales.append(jnp.exp(m_old - m_new))
                pts.append(jnp.exp(u_ref[h] - m_new).astype(BF16))
                m_ref[h] = m_new
            for h in range(nh):
                vth = jnp.concatenate([vt_ref[h * HEAD_DIM:(h + 1) * HEAD_DIM, pl.ds(k0, tk)], ones], axis=0)
                acc_ref[h] = scales[h] * acc_ref[h] + _dot(vth, pts[h])

        def body(j, c):
            tile(j, None)
            return c
        lax.fori_loop(0, i * diag_tiles, body, 0)
        for d in range(diag_tiles):
            tile(i * diag_tiles + d, d)

        ls = [acc_ref[h][HEAD_DIM:HEAD_DIM + 1] for h in range(nh)]
        for p in range(nh // 2):
            ot = jnp.concatenate([acc_ref[2 * p + a][:HEAD_DIM] * (1.0 / ls[2 * p + a]) for a in range(2)], axis=0)
            o_ref[:, p * LANES:(p + 1) * LANES] = ot.T
        for h in range(nh):
            lse_ref[h, :, pl.ds(q0, tq)] = m_ref[h] + jnp.log(ls[h])

    return pl.pallas_call(
        kern, name="fox_fwd",
        grid=(nq,),
        in_specs=[pl.BlockSpec((tq, FOX_W), lambda i: (i, 0)),
                  _resident((s, FOX_W), lambda i: (0, COL_FK // FOX_W)),
                  _resident((FOX_W, s), lambda i: (0, 0)),
                  _resident((nh, 1, s), lambda i: (0, 0, 0)),
                  _resident((s, LANES), lambda i: (0, 0))],
        out_specs=[pl.BlockSpec((tq, FOX_W), lambda i: (i, 0)),
                   pl.BlockSpec((nh, 1, s), lambda i: (0, 0, 0))],
        out_shape=[jax.ShapeDtypeStruct((s, FOX_W), F32),
                   jax.ShapeDtypeStruct((nh, 1, s), F32)],
        scratch_shapes=[pltpu.VMEM((nh, 1, tq), F32),
                        pltpu.VMEM((nh, HEAD_DIM + SUM_ROWS, tq), F32),
                        pltpu.VMEM((nh, tk, tq), F32)],
        compiler_params=_cparams(("arbitrary",)),
    )(qkv, qkv, vt, cum_t3, cum)


def _fox_bwd(qkv, do_bf, cum_t3, cum, lse_t3, delta_t3):
    s = qkv.shape[0]
    t = min(FOX_T, s)
    nq = s // t
    nh = FOX_HEADS
    npair = nh // 2

    def kern(q_ref, do_ref, k_ref, v_ref, ct_ref, c_ref, lse_ref, dl_ref,
             dqt_ref, dk_ref, dv_ref, dc_ref, dcq_ref, accv_ref, acck_ref, accd_ref):
        kj = pl.program_id(0)
        lane = lax.broadcasted_iota(jnp.int32, (1, LANES), 1)
        krow = lax.broadcasted_iota(jnp.int32, (t, t), 0)
        qcol = lax.broadcasted_iota(jnp.int32, (t, t), 1)
        causal = krow <= qcol
        sels = [lane < HEAD_DIM, lane >= HEAD_DIM]

        @pl.when(kj == 0)
        def _():
            dqt_ref[...] = jnp.zeros_like(dqt_ref)
            dcq_ref[...] = jnp.zeros_like(dcq_ref)

        accv_ref[...] = jnp.zeros_like(accv_ref)
        acck_ref[...] = jnp.zeros_like(acck_ref)
        accd_ref[...] = jnp.zeros_like(accd_ref)
        cb = c_ref[...]
        k2s, v2s, kts = [], [], []
        for p in range(npair):
            k2 = k_ref[:, p * LANES:(p + 1) * LANES]
            k2s.append(k2)
            v2s.append(v_ref[:, p * LANES:(p + 1) * LANES])
            kt = k2.astype(F32).T * QK_SCALE
            kts.append(kt[:HEAD_DIM].astype(BF16))
            kts.append(kt[HEAD_DIM:].astype(BF16))
        css = [cb[:, h:h + 1] for h in range(nh)]

        def tile(i, masked):
            q0 = pl.multiple_of(i * t, t)
            r0 = pl.multiple_of((i // (FOX_REF // t)) * FOX_REF, FOX_REF)
            sts, dpts, qms, doms = [], [], [], []
            for h in range(nh):
                p, a = divmod(h, 2)
                qi = q_ref[pl.ds(q0, t), p * LANES:(p + 1) * LANES] * jnp.asarray(QK_SCALE, BF16)
                doi = do_ref[pl.ds(q0, t), p * LANES:(p + 1) * LANES]
                qm = jnp.where(sels[a], qi, jnp.zeros_like(qi))
                dom = jnp.where(sels[a], doi, jnp.zeros_like(doi))
                qms.append(qm)
                doms.append(dom)
                sts.append(_dot_nt(k2s[p], qm))
                dpts.append(_dot_nt(v2s[p], dom))
            pts, dsts = [], []
            for h in range(nh):
                cref = ct_ref[h, :, pl.ds(r0, LANES)][:, 0:1]
                pt = jnp.exp(sts[h] - (css[h] - cref) - lse_ref[h, :, pl.ds(q0, t)])
                if masked:
                    pt = jnp.where(causal, pt, 0.0)
                ds32 = pt * (dpts[h] - dl_ref[h, :, pl.ds(q0, t)])
                part = ds32[:, 0:LANES]
                for c in range(1, t // LANES):
                    part = part + ds32[:, c * LANES:(c + 1) * LANES]
                accd_ref[h] += part
                dcq_ref[h, :, pl.ds(q0, t)] += jnp.sum(ds32, axis=0, keepdims=True)
                pts.append(pt.astype(BF16))
                dsts.append(ds32.astype(BF16))
            for p in range(npair):
                ha, hb = 2 * p, 2 * p + 1
                accv_ref[p] += _dot(pts[ha], doms[ha]) + _dot(pts[hb], doms[hb])
                acck_ref[p] += _dot(dsts[ha], qms[ha]) + _dot(dsts[hb], qms[hb])
            for h in range(nh):
                dqt_ref[h * HEAD_DIM:(h + 1) * HEAD_DIM, pl.ds(q0, t)] += _dot(kts[h], dsts[h])

        tile(kj, True)

        def body(i, c):
            tile(i, False)
            return c
        lax.fori_loop(kj + 1, nq, body, 0)

        dc = jnp.zeros((t, LANES), F32)
        for h in range(nh):
            dc = jnp.where(lane == h, -jnp.sum(accd_ref[h], axis=1, keepdims=True), dc)
        dc_ref[...] = dc
        for p in range(npair):
            dv_ref[:, p * LANES:(p + 1) * LANES] = accv_ref[p].astype(BF16)
            dk_ref[:, p * LANES:(p + 1) * LANES] = acck_ref[p].astype(BF16)

    whole = lambda kj: (0, 0, 0)
    return pl.pallas_call(
        kern, name="fox_bwd",
        grid=(nq,),
        in_specs=[_resident((s, FOX_W), lambda kj: (0, 0)),
                  _resident((s, FOX_W), lambda kj: (0, 0)),
                  pl.BlockSpec((t, FOX_W), lambda kj: (kj, COL_FK // FOX_W)),
                  pl.BlockSpec((t, FOX_W), lambda kj: (kj, COL_FV // FOX_W)),
                  _resident((nh, 1, s), whole),
                  pl.BlockSpec((t, LANES), lambda kj: (kj, 0)),
                  _resident((nh, 1, s), whole),
                  _resident((nh, 1, s), whole)],
        out_specs=[_resident((FOX_W, s), lambda kj: (0, 0)),
                   pl.BlockSpec((t, FOX_W), lambda kj: (kj, 0)),
                   pl.BlockSpec((t, FOX_W), lambda kj: (kj, 0)),
                   pl.BlockSpec((t, LANES), lambda kj: (kj, 0)),
                   _resident((nh, 1, s), whole)],
        out_shape=[jax.ShapeDtypeStruct((FOX_W, s), F32),
                   jax.ShapeDtypeStruct((s, FOX_W), BF16),
                   jax.ShapeDtypeStruct((s, FOX_W), BF16),
                   jax.ShapeDtypeStruct((s, LANES), F32),
                   jax.ShapeDtypeStruct((nh, 1, s), F32)],
        scratch_shapes=[pltpu.VMEM((npair, t, LANES), F32),
                        pltpu.VMEM((npair, t, LANES), F32),
                        pltpu.VMEM((nh, t, LANES), F32)],
        compiler_params=_cparams(("arbitrary",)),
    )(qkv, do_bf, qkv, qkv, cum_t3, cum, lse_t3, delta_t3)


def _bucket_table():
    qi = np.arange(BLOCK)[:, None]
    kj = np.arange(2 * BLOCK)[None, :]
    rel = np.maximum(qi + BLOCK - kj, 0).astype(np.int32)
    max_exact = NUM_BUCKETS // 2
    relf = np.maximum(rel, 1).astype(np.float32)
    large = max_exact + (np.log(relf / np.float32(max_exact)) / np.float32(math.log(MAX_DISTANCE / max_exact))
                         * np.float32(NUM_BUCKETS - max_exact)).astype(np.int32)
    large = np.minimum(large, NUM_BUCKETS - 1)
    return np.where(rel < max_exact, rel, large).astype(np.int32)


SWA_LANES = SWA_GROUP * BLOCK


def _swa_bias(rel_bias, bucket_t):
    def kern(rb_ref, bk_ref, o_ref):
        bk = bk_ref[...]
        kj = lax.broadcasted_iota(jnp.int32, (2 * BLOCK, BLOCK), 0)
        qi = lax.broadcasted_iota(jnp.int32, (2 * BLOCK, BLOCK), 1)
        rel = qi + BLOCK - kj
        band = (rel >= 0) & (rel < BLOCK)
        masks = [band & (kj >= BLOCK), band]
        for h in range(SWA_HEADS):
            g, hh = divmod(h, SWA_GROUP)
            acc = jnp.zeros((2 * BLOCK, BLOCK), F32)
            for b in range(NUM_BUCKETS):
                acc = jnp.where(bk == b, rb_ref[b, h], acc)
            for first in range(2):
                o_ref[first, g, :, hh * BLOCK:(hh + 1) * BLOCK] = jnp.where(masks[first], acc, NEG)

    return pl.pallas_call(
        kern, name="swa_bias",
        in_specs=[pl.BlockSpec(memory_space=pltpu.SMEM),
                  pl.BlockSpec(memory_space=pltpu.VMEM)],
        out_specs=pl.BlockSpec(memory_space=pltpu.VMEM),
        out_shape=jax.ShapeDtypeStruct((2, SWA_KV_HEADS, 2 * BLOCK, SWA_LANES), F32),
        compiler_params=_cparams(),
    )(rel_bias, bucket_t)


SWA_STEP = 4


def _swa_keys(prev_ref, cur_ref):
    return jnp.concatenate([prev_ref[...], cur_ref[...]], axis=0)


def _swa_queries(x_ref, scale):
    x = x_ref[...]
    if scale:
        x = x * jnp.asarray(QK_SCALE, BF16)
    xt = x.astype(F32).T.astype(BF16)
    return [_group_rows(xt[:, b * BLOCK:(b + 1) * BLOCK]) for b in range(SWA_STEP)]


def _group_rows(xt):
    zeros = jnp.zeros((HEAD_DIM, SWA_LANES), BF16)
    out = []
    for g in range(SWA_KV_HEADS):
        heads = [xt[(SWA_GROUP * g + hh) * HEAD_DIM:(SWA_GROUP * g + hh + 1) * HEAD_DIM, :] for hh in range(SWA_GROUP)]
        rows = jnp.concatenate(heads, axis=1)
        padded = jnp.concatenate([rows, zeros] if g == 0 else [zeros, rows], axis=0)
        out.append((rows, padded))
    return out


def _pairs_to_rows(cols_t):
    out = []
    for p in range(SWA_HEADS // 2):
        g, hh = divmod(2 * p, SWA_GROUP)
        pair = jnp.concatenate([cols_t[g][:, hh * BLOCK:(hh + 1) * BLOCK],
                                cols_t[g][:, (hh + 1) * BLOCK:(hh + 2) * BLOCK]], axis=0)
        out.append(pair.T)
    return jnp.concatenate(out, axis=1)


def _swa_fwd(qkv, bias_t, sink_rows):
    s = qkv.shape[0]
    nb = s // BLOCK
    rows = SWA_STEP * BLOCK
    units = [(b, g) for b in range(SWA_STEP) for g in range(SWA_KV_HEADS)]

    def kern(q_ref, kp_ref, kc_ref, vp_ref, vc_ref, bias_ref, sink_ref, o_ref, lse_ref):
        n = pl.program_id(0)
        tables = [jnp.minimum(n, 1)] + [1] * (SWA_STEP - 1)
        k3 = _swa_keys(kp_ref, kc_ref)
        vt3 = _swa_keys(vp_ref, vc_ref).astype(F32).T.astype(BF16)
        qts = _swa_queries(q_ref, True)
        us = [_dot(k3[b * BLOCK:(b + 2) * BLOCK], qts[b][g][1]) + bias_ref[tables[b], g] for b, g in units]
        outs = []
        for (b, g), u in zip(units, us):
            sk = sink_ref[g]
            m = jnp.maximum(jnp.max(u, axis=0, keepdims=True), sk)
            p = jnp.exp(u - m)
            l = jnp.sum(p, axis=0, keepdims=True) + jnp.exp(sk - m)
            lse_ref[b, g] = m + jnp.log(l)
            vt = vt3[g * HEAD_DIM:(g + 1) * HEAD_DIM, b * BLOCK:(b + 2) * BLOCK]
            outs.append(_dot(vt, (p * (1.0 / l)).astype(BF16)))
        for b in range(SWA_STEP):
            o_ref[b * BLOCK:(b + 1) * BLOCK, :] = _pairs_to_rows(outs[b * SWA_KV_HEADS:(b + 1) * SWA_KV_HEADS])

    cq, ck, cv = COL_SQ // SWA_W, COL_SK // LANES, COL_SV // LANES
    prev = lambda n: jnp.maximum(SWA_STEP * n - 1, 0)
    return pl.pallas_call(
        kern, name="swa_fwd",
        grid=(nb // SWA_STEP,),
        in_specs=[pl.BlockSpec((rows, SWA_W), lambda n: (n, cq)),
                  pl.BlockSpec((BLOCK, LANES), lambda n: (prev(n), ck)),
                  pl.BlockSpec((rows, LANES), lambda n: (n, ck)),
                  pl.BlockSpec((BLOCK, LANES), lambda n: (prev(n), cv)),
                  pl.BlockSpec((rows, LANES), lambda n: (n, cv)),
                  _resident((2, SWA_KV_HEADS, 2 * BLOCK, SWA_LANES), lambda n: (0, 0, 0, 0)),
                  _resident((SWA_KV_HEADS, 1, SWA_LANES), lambda n: (0, 0, 0))],
        out_specs=[pl.BlockSpec((rows, SWA_W), lambda n: (n, 0)),
                   pl.BlockSpec((SWA_STEP, SWA_KV_HEADS, 1, SWA_LANES), lambda n: (n, 0, 0, 0))],
        out_shape=[jax.ShapeDtypeStruct((s, SWA_W), F32),
                   jax.ShapeDtypeStruct((nb, SWA_KV_HEADS, 1, SWA_LANES), F32)],
        compiler_params=_cparams(("parallel",)),
    )(qkv, qkv, qkv, qkv, qkv, bias_t, sink_rows)


def _swa_bwd(qkv, do_bf, delta_rows, lse, bias_t, sink_rows, bucket_t):
    s = qkv.shape[0]
    nb = s // BLOCK
    steps = nb // SWA_STEP
    rows = SWA_STEP * BLOCK
    units = [(b, g) for b in range(SWA_STEP) for g in range(SWA_KV_HEADS)]

    def kern(q_ref, kp_ref, kc_ref, vp_ref, vc_ref, do_ref, dl_ref, lse_ref, bias_ref, sink_ref, bk_ref,
             dq_ref, dk_ref, dv_ref, grb_ref, gsk_ref, dbias_ref, ck_ref, cv_ref, sk_ref):
        n = pl.program_id(0)

        @pl.when(n == 0)
        def _():
            dbias_ref[...] = jnp.zeros_like(dbias_ref)
            ck_ref[...] = jnp.zeros_like(ck_ref)
            cv_ref[...] = jnp.zeros_like(cv_ref)
            sk_ref[...] = jnp.zeros_like(sk_ref)

        @pl.when(n < steps)
        def _():
            tables = [jnp.minimum(n, 1)] + [1] * (SWA_STEP - 1)
            k3 = _swa_keys(kp_ref, kc_ref)
            v3 = _swa_keys(vp_ref, vc_ref)
            kt3 = (k3.astype(F32).T * QK_SCALE).astype(BF16)
            qts = _swa_queries(q_ref, True)
            dots = _swa_queries(do_ref, False)
            sts = [_dot(k3[b * BLOCK:(b + 2) * BLOCK], qts[b][g][1]) for b, g in units]
            dps = [_dot(v3[b * BLOCK:(b + 2) * BLOCK], dots[b][g][1]) for b, g in units]
            ps, dss = [], []
            for i, (b, g) in enumerate(units):
                lse_g = lse_ref[b, g]
                dlt = dl_ref[b, g]
                p = jnp.exp(sts[i] + bias_ref[tables[b], g] - lse_g)
                ds = p * (dps[i] - dlt)
                dbias_ref[g] += ds
                sk_ref[g] += -jnp.exp(sink_ref[g] - lse_g) * dlt
                ps.append(p.astype(BF16))
                dss.append(ds.astype(BF16))
            dk2, dv2 = [], []
            for b in range(SWA_STEP):
                at = lambda g: b * SWA_KV_HEADS + g
                groups = range(SWA_KV_HEADS)
                dv2.append(jnp.concatenate([_dot_nt(dots[b][g][0], ps[at(g)]) for g in groups], axis=0).T)
                dk2.append(jnp.concatenate([_dot_nt(qts[b][g][0], dss[at(g)]) for g in groups], axis=0).T)
                dqts = [_dot(kt3[g * HEAD_DIM:(g + 1) * HEAD_DIM, b * BLOCK:(b + 2) * BLOCK], dss[at(g)]) for g in groups]
                dq_ref[b * BLOCK:(b + 1) * BLOCK, :] = _pairs_to_rows(dqts).astype(BF16)
            last = (SWA_STEP - 1) * BLOCK
            for acc_ref, out_ref, parts in ((ck_ref, dk_ref, dk2), (cv_ref, dv_ref, dv2)):
                done = acc_ref[last:] + parts[0][:BLOCK]
                out_ref[...] = jnp.concatenate([acc_ref[:last], done], axis=0).astype(BF16)
                for b in range(SWA_STEP - 1):
                    acc_ref[b * BLOCK:(b + 1) * BLOCK] = parts[b][BLOCK:] + parts[b + 1][:BLOCK]
                acc_ref[last:] = parts[SWA_STEP - 1][BLOCK:]

        @pl.when(n == steps)
        def _():
            dk_ref[...] = ck_ref[...].astype(BF16)
            dv_ref[...] = cv_ref[...].astype(BF16)
            bk = bk_ref[...]
            lane = lax.broadcasted_iota(jnp.int32, (8, LANES), 1)
            rowi = lax.broadcasted_iota(jnp.int32, (NUM_BUCKETS, LANES), 0)
            lanei = lax.broadcasted_iota(jnp.int32, (NUM_BUCKETS, LANES), 1)
            out = jnp.zeros((NUM_BUCKETS, LANES), F32)
            gsk = jnp.zeros((8, LANES), F32)
            for h in range(SWA_HEADS):
                g, hh = divmod(h, SWA_GROUP)
                cols = slice(hh * BLOCK, (hh + 1) * BLOCK)
                gsk = jnp.where(lane == h, jnp.sum(sk_ref[g][:, cols]), gsk)
                db = dbias_ref[g][:, cols]
                for b in range(NUM_BUCKETS):
                    val = jnp.sum(jnp.where(bk == b, db, 0.0))
                    out = jnp.where((rowi == b) & (lanei == h), val, out)
            grb_ref[...] = out
            gsk_ref[...] = gsk

    cq, ck, cv = COL_SQ // SWA_W, COL_SK // LANES, COL_SV // LANES
    cur = lambda n: jnp.minimum(n, steps - 1)
    prev = lambda n: jnp.maximum(SWA_STEP * cur(n) - 1, 0)
    kout = lambda n: jnp.maximum(n - 1, 0)
    stat = pl.BlockSpec((SWA_STEP, SWA_KV_HEADS, 1, SWA_LANES), lambda n: (cur(n), 0, 0, 0))
    return pl.pallas_call(
        kern, name="swa_bwd",
        grid=(steps + 1,),
        in_specs=[pl.BlockSpec((rows, SWA_W), lambda n: (cur(n), cq)),
                  pl.BlockSpec((BLOCK, LANES), lambda n: (prev(n), ck)),
                  pl.BlockSpec((rows, LANES), lambda n: (cur(n), ck)),
                  pl.BlockSpec((BLOCK, LANES), lambda n: (prev(n), cv)),
                  pl.BlockSpec((rows, LANES), lambda n: (cur(n), cv)),
                  pl.BlockSpec((rows, SWA_W), lambda n: (cur(n), 1)),
                  stat, stat,
                  _resident((2, SWA_KV_HEADS, 2 * BLOCK, SWA_LANES), lambda n: (0, 0, 0, 0)),
                  _resident((SWA_KV_HEADS, 1, SWA_LANES), lambda n: (0, 0, 0)),
                  _resident((2 * BLOCK, BLOCK), lambda n: (0, 0))],
        out_specs=[pl.BlockSpec((rows, SWA_W), lambda n: (cur(n), 0)),
                   pl.BlockSpec((rows, LANES), lambda n: (kout(n), 0)),
                   pl.BlockSpec((rows, LANES), lambda n: (kout(n), 0)),
                   pl.BlockSpec((NUM_BUCKETS, LANES), lambda n: (0, 0)),
                   pl.BlockSpec((8, LANES), lambda n: (0, 0))],
        out_shape=[jax.ShapeDtypeStruct((s, SWA_W), BF16),
                   jax.ShapeDtypeStruct((s, LANES), BF16),
                   jax.ShapeDtypeStruct((s, LANES), BF16),
                   jax.ShapeDtypeStruct((NUM_BUCKETS, LANES), F32),
                   jax.ShapeDtypeStruct((8, LANES), F32)],
        scratch_shapes=[pltpu.VMEM((SWA_KV_HEADS, 2 * BLOCK, SWA_LANES), F32),
                        pltpu.VMEM((rows, LANES), F32),
                        pltpu.VMEM((rows, LANES), F32),
                        pltpu.VMEM((SWA_KV_HEADS, 1, SWA_LANES), F32)],
        compiler_params=_cparams(("arbitrary",)),
    )(qkv, qkv, qkv, qkv, qkv, do_bf, delta_rows, lse, bias_t, sink_rows, bucket_t)


def _post(x, target, o_fox, o_swa, z, w_o, ln_g, ln_b):
    s = x.shape[0]
    tm = min(256, s)
    nt = s // tm

    def kern(x_ref, t_ref, of_ref, os_ref, z_ref, w_ref, g_ref, b_ref,
             loss_ref, dh_ref, gwo_ref, do_ref, dz_ref, dl_ref, gg_ref, gb_ref, lacc_ref):
        step = pl.program_id(0)

        @pl.when(step == 0)
        def _():
            lacc_ref[...] = jnp.zeros_like(lacc_ref)
            gg_ref[...] = jnp.zeros_like(gg_ref)
            gwo_ref[...] = jnp.zeros_like(gwo_ref)
            gb_ref[...] = jnp.zeros_like(gb_ref)

        o = jnp.concatenate([of_ref[...], os_ref[...]], axis=1)
        zz = z_ref[...]
        sig = 1.0 / (1.0 + jnp.exp(-zz))
        silu = zz * sig
        mixed32 = o * silu
        mixed = mixed32.astype(BF16)
        w = w_ref[...]
        h = ALPHA * x_ref[...] + _dot(mixed, w)
        mu = jnp.mean(h, axis=1, keepdims=True)
        hc = h - mu
        var = jnp.mean(hc * hc, axis=1, keepdims=True)
        rstd = lax.rsqrt(var + LN_EPS)
        xhat = hc * rstd
        g = g_ref[...]
        err = xhat * g + b_ref[...] - t_ref[...]
        lacc_ref[...] += jnp.broadcast_to(jnp.sum(err * err, axis=0, keepdims=True), lacc_ref.shape)
        dout = err * (1.0 / D_MODEL)
        gg_ref[...] += jnp.broadcast_to(jnp.sum(dout * xhat, axis=0, keepdims=True), gg_ref.shape)
        gb_ref[...] += jnp.broadcast_to(jnp.sum(dout, axis=0, keepdims=True), gb_ref.shape)
        dxh = dout * g
        m1 = jnp.mean(dxh, axis=1, keepdims=True)
        m2 = jnp.mean(dxh * xhat, axis=1, keepdims=True)
        dh = rstd * (dxh - m1 - xhat * m2)
        dh_ref[...] = dh
        dy = dh.astype(BF16)
        gwo_ref[...] += _dot(mixed32.T.astype(BF16), dy)
        dmix = _dot_nt(dy, w)
        do = dmix * silu
        do_ref[...] = do.astype(BF16)
        dz_ref[...] = (dmix * o * (sig * (1.0 + zz * (1.0 - sig)))).astype(BF16)
        r = lax.broadcasted_iota(jnp.int32, (D_MODEL, LANES), 0) // HEAD_DIM
        c = lax.broadcasted_iota(jnp.int32, (D_MODEL, LANES), 1)
        pick = jnp.where(r == c, 1.0, 0.0).astype(BF16)
        dl_ref[...] = _exact_dot(pick, do * o, False)

        @pl.when(step == nt - 1)
        def _():
            tot = jnp.sum(lacc_ref[0:1, :]) * (0.5 / D_MODEL)
            loss_ref[...] = jnp.broadcast_to(tot, loss_ref.shape)

    row = lambda i: (i, 0)
    fixed = lambda i: (0, 0)
    wide = pl.BlockSpec((tm, D_MODEL), row)
    half = pl.BlockSpec((tm, FOX_W), row)
    return pl.pallas_call(
        kern, name="post",
        grid=(nt,),
        in_specs=[wide, wide, half, half, wide,
                  pl.BlockSpec((D_MODEL, D_MODEL), fixed),
                  pl.BlockSpec((1, D_MODEL), fixed),
                  pl.BlockSpec((1, D_MODEL), fixed)],
        out_specs=[pl.BlockSpec((8, LANES), fixed), wide,
                   _resident((D_MODEL, D_MODEL), fixed), wide, wide,
                   pl.BlockSpec((tm, LANES), row),
                   pl.BlockSpec((8, D_MODEL), fixed), pl.BlockSpec((8, D_MODEL), fixed)],
        out_shape=[jax.ShapeDtypeStruct((8, LANES), F32),
                   jax.ShapeDtypeStruct((s, D_MODEL), F32),
                   jax.ShapeDtypeStruct((D_MODEL, D_MODEL), F32),
                   jax.ShapeDtypeStruct((s, D_MODEL), BF16),
                   jax.ShapeDtypeStruct((s, D_MODEL), BF16),
                   jax.ShapeDtypeStruct((s, LANES), F32),
                   jax.ShapeDtypeStruct((8, D_MODEL), F32),
                   jax.ShapeDtypeStruct((8, D_MODEL), F32)],
        scratch_shapes=[pltpu.VMEM((8, D_MODEL), F32)],
        compiler_params=_cparams(("arbitrary",)),
    )(x, target, o_fox, o_swa, z, w_o, ln_g, ln_b)


def _adamw_math(w, g, m, v):
    m = ADAM_B1 * m + (1.0 - ADAM_B1) * g
    v = ADAM_B2 * v + (1.0 - ADAM_B2) * (g * g)
    m_hat = m / (1.0 - ADAM_B1 ** ADAM_STEP)
    v_hat = v / (1.0 - ADAM_B2 ** ADAM_STEP)
    delta = -ADAM_LR * (m_hat / (jnp.sqrt(v_hat) + ADAM_EPS) + ADAM_WD * w)
    return delta, m, v


def _adamw(w, g, m, v, *, name):
    r, c = w.shape
    tr = min(256, r)

    def kern(w_ref, g_ref, m_ref, v_ref, d_ref, mo_ref, vo_ref):
        d, mn, vn = _adamw_math(w_ref[...], g_ref[...], m_ref[...], v_ref[...])
        d_ref[...] = d
        mo_ref[...] = mn
        vo_ref[...] = vn

    blk = pl.BlockSpec((tr, c), lambda i: (i, 0))
    sds = jax.ShapeDtypeStruct((r, c), F32)
    return pl.pallas_call(
        kern, name=name,
        grid=(r // tr,),
        in_specs=[blk, blk, blk, blk],
        out_specs=[blk, blk, blk],
        out_shape=[sds, sds, sds],
        compiler_params=_cparams(("parallel",)),
    )(w, g, m, v)


def _adamw_cols(w, g, m, v, *, name):
    c, _, r = w.shape
    tc = 139
    assert c % tc == 0

    def kern(w_ref, g_ref, m_ref, v_ref, d_ref, mo_ref, vo_ref):
        d, mn, vn = _adamw_math(w_ref[...], g_ref[...], m_ref[...], v_ref[...])
        d_ref[...] = d
        mo_ref[...] = mn
        vo_ref[...] = vn

    blk = pl.BlockSpec((tc, 1, r), lambda i: (i, 0, 0))
    sds = jax.ShapeDtypeStruct((c, 1, r), F32)
    return pl.pallas_call(
        kern, name=name,
        grid=(c // tc,),
        in_specs=[blk, blk, blk, blk],
        out_specs=[blk, blk, blk],
        out_shape=[sds, sds, sds],
        compiler_params=_cparams(("parallel",)),
    )(w, g, m, v)


def _position():
    x, y, c = lax.axis_index("x"), lax.axis_index("y"), lax.axis_index("c")
    chips = [(1 - x, y), (x, 1 - y), (1 - x, 1 - y)]
    return x, y, c, chips


def _chip_index(cx, cy):
    return 2 * cx + cy


def _gather_weights(w_in_bf, w_o_bf):
    shards = (w_in_bf, w_o_bf)
    n_arr = len(shards)

    def kern(*refs):
        ins, outs = refs[:n_arr], refs[n_arr:2 * n_arr]
        send_sems, recv_sems, local_sems = refs[2 * n_arr:]
        x, y, c, chips = _position()
        me = _chip_index(x, y)
        sibling = (x, y, 1 - c)

        local = [pltpu.make_async_copy(ins[a], outs[a].at[me], local_sems.at[a]) for a in range(n_arr)]
        for cp in local:
            cp.start()

        def half(ref, a):
            rows = shards[a].shape[0] // 2
            return ref.at[pl.ds(c * rows, rows), :]

        def copy(a, k, src, slot, to):
            return pltpu.make_async_remote_copy(
                src_ref=src, dst_ref=half(outs[a].at[slot], a),
                send_sem=send_sems.at[a * 6 + k], recv_sem=recv_sems.at[a * 6 + k],
                device_id=to, device_id_type=MESH)

        first = [copy(a, j, half(ins[a], a), me, (*chip, c)) for a in range(n_arr) for j, chip in enumerate(chips)]
        for cp in first:
            cp.start()
        passed = []
        for a in range(n_arr):
            for j, chip in enumerate(chips):
                slot = _chip_index(*chip)
                copy(a, j, half(ins[a], a), slot, (*chip, c)).wait_recv()
                fwd = copy(a, 3 + j, half(outs[a].at[slot], a), slot, sibling)
                fwd.start()
                passed.append(fwd)
        for a in range(n_arr):
            for j, chip in enumerate(chips):
                slot = _chip_index(*chip)
                rows = shards[a].shape[0] // 2
                dst = outs[a].at[slot].at[pl.ds((1 - c) * rows, rows), :]
                pltpu.make_async_remote_copy(
                    src_ref=dst, dst_ref=dst, send_sem=send_sems.at[a * 6 + 3 + j],
                    recv_sem=recv_sems.at[a * 6 + 3 + j], device_id=sibling, device_id_type=MESH).wait_recv()
        for cp in first + passed:
            cp.wait_send()
        for cp in local:
            cp.wait()

    vmem = pl.BlockSpec(memory_space=pltpu.VMEM)
    return pl.pallas_call(
        kern, name="gather_weights",
        in_specs=[vmem] * n_arr,
        out_specs=[vmem] * n_arr,
        out_shape=[jax.ShapeDtypeStruct((N_CHIPS,) + w.shape, w.dtype) for w in shards],
        scratch_shapes=[pltpu.SemaphoreType.DMA((6 * n_arr,)),
                        pltpu.SemaphoreType.DMA((6 * n_arr,)),
                        pltpu.SemaphoreType.DMA((n_arr,))],
        compiler_params=_cparams(),
    )(*shards)


def _pair_reduce(grads):
    n_arr = len(grads)
    chunk = 128

    def kern(*refs):
        ins = refs[:n_arr]
        outs = refs[n_arr:2 * n_arr]
        gots = refs[2 * n_arr:3 * n_arr]
        send_sems, recv_sems = refs[3 * n_arr:]
        x, y, c, _ = _position()
        sibling = (x, y, 1 - c)
        copies = []
        for a in range(n_arr):
            rows = grads[a].shape[1] // 2
            copies.append(pltpu.make_async_remote_copy(
                src_ref=ins[a].at[:, pl.ds((1 - c) * rows, rows), :], dst_ref=gots[a],
                send_sem=send_sems.at[a], recv_sem=recv_sems.at[a], device_id=sibling, device_id_type=MESH))
        for cp in copies:
            cp.start()
        for a in range(n_arr):
            copies[a].wait()
            rows = grads[a].shape[1] // 2
            for j in range(N_CHIPS):
                for r0 in range(0, rows, chunk):
                    mine = ins[a][j, pl.ds(pl.multiple_of(c * rows + r0, chunk), chunk), :]
                    outs[a][j, r0:r0 + chunk, :] = (mine + gots[a][j, r0:r0 + chunk, :]).astype(BF16)

    vmem = pl.BlockSpec(memory_space=pltpu.VMEM)
    half = [(N_CHIPS, g.shape[1] // 2, g.shape[2]) for g in grads]
    return pl.pallas_call(
        kern, name="pair_reduce",
        in_specs=[vmem] * n_arr,
        out_specs=[vmem] * n_arr,
        out_shape=[jax.ShapeDtypeStruct(h, BF16) for h in half],
        scratch_shapes=[pltpu.VMEM(h, F32) for h in half]
        + [pltpu.SemaphoreType.DMA((n_arr,)), pltpu.SemaphoreType.DMA((n_arr,))],
        compiler_params=_cparams(),
    )(*grads)


def _chip_reduce(parts):
    n_arr = len(parts)
    chunk = 128

    def kern(*refs):
        ins = refs[:n_arr]
        outs = refs[n_arr:2 * n_arr]
        slabs = refs[2 * n_arr:3 * n_arr]
        send_sems, recv_sems, local_sems = refs[3 * n_arr:]
        x, y, c, chips = _position()
        me = _chip_index(x, y)
        local = [pltpu.make_async_copy(ins[a].at[me], slabs[a].at[me], local_sems.at[a]) for a in range(n_arr)]
        for cp in local:
            cp.start()
        sends = []
        for a in range(n_arr):
            for j, chip in enumerate(chips):
                sends.append(pltpu.make_async_remote_copy(
                    src_ref=ins[a].at[_chip_index(*chip)], dst_ref=slabs[a].at[me],
                    send_sem=send_sems.at[a * 3 + j], recv_sem=recv_sems.at[a * 3 + j],
                    device_id=(*chip, c), device_id_type=MESH))
        for cp in sends:
            cp.start()
        for a in range(n_arr):
            for j, chip in enumerate(chips):
                slot = slabs[a].at[_chip_index(*chip)]
                pltpu.make_async_remote_copy(
                    src_ref=slot, dst_ref=slot, send_sem=send_sems.at[a * 3 + j],
                    recv_sem=recv_sems.at[a * 3 + j], device_id=(*chip, c), device_id_type=MESH).wait_recv()
        for cp in sends:
            cp.wait_send()
        for cp in local:
            cp.wait()
        for a in range(n_arr):
            for r0 in range(0, parts[a].shape[1], chunk):
                f = lambda j: slabs[a][j, r0:r0 + chunk, :].astype(F32)
                outs[a][r0:r0 + chunk, :] = ((f(0) + f(1)) + f(2)) + f(3)

    vmem = pl.BlockSpec(memory_space=pltpu.VMEM)
    return pl.pallas_call(
        kern, name="chip_reduce",
        in_specs=[vmem] * n_arr,
        out_specs=[vmem] * n_arr,
        out_shape=[jax.ShapeDtypeStruct(p.shape[1:], F32) for p in parts],
        scratch_shapes=[pltpu.VMEM(p.shape, BF16) for p in parts]
        + [pltpu.SemaphoreType.DMA((3 * n_arr,)),
           pltpu.SemaphoreType.DMA((3 * n_arr,)),
           pltpu.SemaphoreType.DMA((n_arr,))],
        compiler_params=_cparams(),
    )(*parts)


def _join_halves(halves):
    n_arr = len(halves)

    def kern(*refs):
        ins = refs[:n_arr]
        outs = refs[n_arr:2 * n_arr]
        send_sems, recv_sems, local_sems = refs[2 * n_arr:]
        x, y, c, _ = _position()
        sibling = (x, y, 1 - c)
        local, remote = [], []
        for a in range(n_arr):
            rows = halves[a].shape[0]
            mine = outs[a].at[pl.ds(c * rows, rows), :]
            local.append(pltpu.make_async_copy(ins[a], mine, local_sems.at[a]))
            remote.append(pltpu.make_async_remote_copy(
                src_ref=ins[a], dst_ref=mine, send_sem=send_sems.at[a], recv_sem=recv_sems.at[a],
                device_id=sibling, device_id_type=MESH))
        for cp in local + remote:
            cp.start()
        for a in range(n_arr):
            rows = halves[a].shape[0]
            theirs = outs[a].at[pl.ds((1 - c) * rows, rows), :]
            pltpu.make_async_remote_copy(
                src_ref=theirs, dst_ref=theirs, send_sem=send_sems.at[a], recv_sem=recv_sems.at[a],
                device_id=sibling, device_id_type=MESH).wait_recv()
        for cp in remote:
            cp.wait_send()
        for cp in local:
            cp.wait()

    vmem = pl.BlockSpec(memory_space=pltpu.VMEM)
    return pl.pallas_call(
        kern, name="join_halves",
        in_specs=[vmem] * n_arr,
        out_specs=[vmem] * n_arr,
        out_shape=[jax.ShapeDtypeStruct((2 * h.shape[0], h.shape[1]), F32) for h in halves],
        scratch_shapes=[pltpu.SemaphoreType.DMA((n_arr,)),
                        pltpu.SemaphoreType.DMA((n_arr,)),
                        pltpu.SemaphoreType.DMA((n_arr,))],
        compiler_params=_cparams(),
    )(*halves)


def _small_allreduce_adamw(partials, params, moms, vels):
    chunks = D_MODEL // LANES
    row_rb, row_bf, row_sk, row_loss = 2 * chunks, 2 * chunks + NUM_BUCKETS, 2 * chunks + NUM_BUCKETS + 1, SMALL_ROWS - 6

    def kern(gbf_ref, grb_ref, gsk_ref, gg_ref, gb_ref, loss_ref, *refs):
        p_refs, m_refs, v_refs = refs[0:5], refs[5:10], refs[10:15]
        lo_ref, g_outs, d_outs, mo_outs, vo_outs = refs[15], refs[16:21], refs[21:26], refs[26:31], refs[31:36]
        send_ref, buf_ref, send_sems, recv_sems = refs[36:]
        x, y, c, _ = _position()
        me = 4 * x + 2 * y + c
        send_ref[...] = jnp.zeros_like(send_ref)
        for r in range(chunks):
            send_ref[r:r + 1, :] = gg_ref[0:1, r * LANES:(r + 1) * LANES]
            send_ref[chunks + r:chunks + r + 1, :] = gb_ref[0:1, r * LANES:(r + 1) * LANES]
        send_ref[row_rb:row_rb + NUM_BUCKETS, :] = grb_ref[...]
        send_ref[row_bf:row_bf + 1, :] = gbf_ref[0:1, :]
        send_ref[row_sk:row_sk + 1, :] = gsk_ref[0:1, :]
        send_ref[row_loss:row_loss + 1, :] = loss_ref[0:1, :]
        buf_ref[me] = send_ref[...]
        peers = [(x, y, 1 - c)] + [(px, py, pc) for px, py in _position()[3] for pc in (c, 1 - c)]
        sends = []
        for k, peer in enumerate(peers):
            sends.append(pltpu.make_async_remote_copy(
                src_ref=send_ref, dst_ref=buf_ref.at[me], send_sem=send_sems.at[k], recv_sem=recv_sems.at[k],
                device_id=peer, device_id_type=MESH))
        for cp in sends:
            cp.start()
        for k, (px, py, pc) in enumerate(peers):
            slot = buf_ref.at[4 * px + 2 * py + pc]
            pltpu.make_async_remote_copy(
                src_ref=slot, dst_ref=slot, send_sem=send_sems.at[k], recv_sem=recv_sems.at[k],
                device_id=(px, py, pc), device_id_type=MESH).wait_recv()
        for cp in sends:
            cp.wait_send()
        tot = buf_ref[0]
        for d in range(1, N_DEV):
            tot = tot + buf_ref[d]
        lo_ref[...] = tot[row_loss:row_loss + 1, :]
        grads = [tot[row_bf:row_bf + 1, 0:FOX_HEADS],
                 tot[row_rb:row_rb + NUM_BUCKETS, 0:SWA_HEADS],
                 tot[row_sk:row_sk + 1, 0:SWA_HEADS],
                 jnp.concatenate([tot[r:r + 1, :] for r in range(chunks)], axis=1),
                 jnp.concatenate([tot[chunks + r:chunks + r + 1, :] for r in range(chunks)], axis=1)]
        for i, g in enumerate(grads):
            g_outs[i][...] = g
            delta, mn, vn = _adamw_math(p_refs[i][...], g, m_refs[i][...], v_refs[i][...])
            d_outs[i][...] = delta
            mo_outs[i][...] = mn
            vo_outs[i][...] = vn

    vm = pl.BlockSpec(memory_space=pltpu.VMEM)
    shapes = [jax.ShapeDtypeStruct(p.shape, F32) for p in params]
    outs = pl.pallas_call(
        kern, name="small_allreduce_adamw",
        in_specs=[vm] * 21,
        out_specs=[vm] * 21,
        out_shape=[jax.ShapeDtypeStruct((1, LANES), F32)] + shapes * 4,
        scratch_shapes=[pltpu.VMEM((SMALL_ROWS, LANES), F32),
                        pltpu.VMEM((N_DEV, SMALL_ROWS, LANES), F32),
                        pltpu.SemaphoreType.DMA((N_DEV - 1,)),
                        pltpu.SemaphoreType.DMA((N_DEV - 1,))],
    )(*partials, *params, *moms, *vels)
    return outs[0], outs[1:6], outs[6:11], outs[11:16], outs[16:21]


def _to_padded_cols(w):
    pad = jnp.zeros((w.shape[0], N_C - FOX_HEADS), w.dtype)
    return jnp.concatenate([w[:, 0:1536], w[:, 2056:2824], w[:, 1536:1544], pad,
                            w[:, 1544:2056], w[:, 2824:3336]], axis=1)


def _from_padded_cols(g):
    return jnp.concatenate([g[:, 0:1536], g[:, OFF_C:OFF_C + FOX_HEADS], g[:, OFF_B:OFF_B + FOX_W],
                            g[:, 1536:N_A], g[:, OFF_B + FOX_W:N_PAD]], axis=1)


def _fox_rows(a):
    return a[:, :FOX_HEADS].T.reshape(FOX_HEADS, 1, a.shape[0])


def kernel(x, w_in, b_f, rel_bias, sink, w_o, ln_g, ln_b, loss_target, m_w_in, m_b_f, m_rel_bias, m_sink, m_w_o, m_ln_g, m_ln_b, v_w_in, v_b_f, v_rel_bias, v_sink, v_w_o, v_ln_g, v_ln_b):
    x2 = x[0]
    tgt = loss_target[0]
    s = x2.shape[0]
    w_in2, w_o2 = w_in[0], w_o[0]

    shard_cols = D_IN // N_CHIPS
    col_pad = ((0, 0), (0, SHARD_PAD - shard_cols))
    w_in_all, w_o_all = _gather_weights(jnp.pad(w_in2.astype(BF16), col_pad), w_o2.astype(BF16))
    w_full = jnp.concatenate([w_in_all[j, :, :shard_cols] for j in range(N_CHIPS)], axis=1)
    w_pad = _to_padded_cols(w_full)
    w_o_full = w_o_all.reshape(D_MODEL, D_MODEL)

    qkv, ffp, z, xt, vt = _project(x2, w_pad)
    bfp = jnp.pad(b_f, ((0, 0), (0, LANES - FOX_HEADS)))
    cum = _cum_fwd(ffp, bfp)
    cum_t3 = _fox_rows(cum)
    o_fox, lse_t3 = _fox_fwd(qkv, vt, cum_t3, cum)
    bucket_t = jnp.asarray(_bucket_table().T)
    bias_t = _swa_bias(rel_bias, bucket_t)
    sink_rows = jnp.repeat(sink.reshape(SWA_KV_HEADS, SWA_GROUP, 1), BLOCK, axis=2).reshape(SWA_KV_HEADS, 1, SWA_LANES)
    o_swa, lse_swa = _swa_fwd(qkv, bias_t, sink_rows)

    loss8, dh, grad_w_o_full, do_bf, dz, delta, gg8, gb8 = _post(
        x2, tgt, o_fox, o_swa, z, w_o_full, ln_g, ln_b)

    delta_t3 = _fox_rows(delta)
    dqt_fox, dk_fox, dv_fox, dcum_k, dcum_q = _fox_bwd(qkv, do_bf, cum_t3, cum, lse_t3, delta_t3)
    dcum_q = jnp.pad(dcum_q.reshape(FOX_HEADS, s).T, ((0, 0), (0, LANES - FOX_HEADS)))
    dff, gbf8 = _cum_bwd(dcum_k, dcum_q, ffp, bfp)
    delta_rows = (delta[:, FOX_HEADS:FOX_HEADS + SWA_HEADS].reshape(s // BLOCK, BLOCK, SWA_KV_HEADS, SWA_GROUP)
                  .transpose(0, 2, 3, 1).reshape(s // BLOCK, SWA_KV_HEADS, 1, SWA_LANES))
    dq_swa, dk_swa, dv_swa, grb, gsk8 = _swa_bwd(qkv, do_bf, delta_rows, lse_swa, bias_t, sink_rows, bucket_t)

    dq_fox = dqt_fox.T.astype(BF16)
    d_misc = jnp.concatenate([dk_swa, dv_swa, dff], axis=1)
    pieces = [dq_fox, dk_fox, dv_fox, dq_swa, d_misc, dz]
    grad_x = _grad_x_matmul(pieces, w_pad, dh, tm=512, tn=D_MODEL, name="grad_x")
    blocks = [(p, 0) for p in pieces[:-1]] + [(dz, 0), (dz, 1)]
    grad_w_pad = _grad_w_matmul(xt, blocks, tk=1024, name="grad_w_in")
    grad_w_in_full = _from_padded_cols(grad_w_pad)

    g_in4 = jnp.stack([jnp.pad(grad_w_in_full[:, j * shard_cols:(j + 1) * shard_cols], col_pad)
                       for j in range(N_CHIPS)])
    g_o4 = grad_w_o_full.reshape(N_CHIPS, D_MODEL // N_CHIPS, D_MODEL)
    g_w_in, g_w_o = _join_halves(_chip_reduce(_pair_reduce([g_in4, g_o4])))
    g_w_in = g_w_in[:, :shard_cols]

    cols_first = lambda a: jnp.transpose(a, (2, 0, 1))
    rows_first = lambda a: jnp.transpose(a, (1, 2, 0))
    g_cols = cols_first(g_w_in[None])
    d_w_in, nm_w_in, nv_w_in = [rows_first(a) for a in _adamw_cols(
        cols_first(w_in), g_cols, cols_first(m_w_in), cols_first(v_w_in), name="adamw_w_in")]
    g_w_in = rows_first(g_cols)
    d_w_o, nm_w_o, nv_w_o = _adamw(w_o2, g_w_o, m_w_o[0], v_w_o[0], name="adamw_w_o")

    loss_row, gs, ds, ms, vs = _small_allreduce_adamw(
        [gbf8, grb, gsk8, gg8, gb8, loss8],
        [b_f, rel_bias, sink, ln_g, ln_b],
        [m_b_f, m_rel_bias, m_sink, m_ln_g, m_ln_b],
        [v_b_f, v_rel_bias, v_sink, v_ln_g, v_ln_b])
    loss = loss_row[0, 0]
    g_bf, g_rb, g_sk, g_lg, g_lb = gs
    d_bf, d_rb, d_sk, d_lg, d_lb = ds
    m_bf, m_rb, m_sk, m_lg, m_lb = ms
    v_bf, v_rb, v_sk, v_lg, v_lb = vs

    e = lambda a: a[None]
    return (loss, e(grad_x),
            g_w_in, g_bf, g_rb, g_sk, e(g_w_o), g_lg, g_lb,
            d_w_in, d_bf, d_rb, d_sk, e(d_w_o), d_lg, d_lb,
            nm_w_in, m_bf, m_rb, m_sk, e(nm_w_o), m_lg, m_lb,
            nv_w_in, v_bf, v_rb, v_sk, e(nv_w_o), v_lg, v_lb)
```

```python
import functools
import math

import numpy as np
import jax
import jax.numpy as jnp
from jax import lax
from jax.experimental import pallas as pl
from jax.experimental.pallas import tpu as pltpu

F32 = jnp.float32
BF16 = jnp.bfloat16

D_MODEL = 1024
HEAD_DIM = 64
FOX_HEADS = 8
SWA_HEADS = 8
SWA_KV_HEADS = 2
SWA_GROUP = 4
FOX_W = 512
SWA_W = 512
SWA_KV_W = 128
BLOCK = 128
NUM_BUCKETS = 32
MAX_DISTANCE = 128
LN_EPS = 1e-5
NEG = -1e30
ALPHA = 2.0 ** 0.25
QK_SCALE = 0.125

ADAM_LR = 0.001
ADAM_B1 = 0.9
ADAM_B2 = 0.999
ADAM_EPS = 1e-08
ADAM_WD = 0.01
ADAM_STEP = 10

D_IN = 3336
SHARD_PAD = 896
N_A = 2304
N_C = 256
N_B = 1024
OFF_C = N_A
OFF_B = N_A + N_C
N_PAD = N_A + N_C + N_B
COL_FK, COL_FV, COL_SQ, COL_SK, COL_SV = 512, 1024, 1536, 2048, 2176

LANES = 128
FOX_T = 256
FOX_REF = 512
SUM_ROWS = 16
VMEM_LIMIT = 56 * 1024 * 1024

MESH = pl.DeviceIdType.MESH
N_CHIPS = 4
N_DEV = 8
SMALL_ROWS = 56


def _cparams(sem=None):
    return pltpu.CompilerParams(dimension_semantics=sem, vmem_limit_bytes=VMEM_LIMIT)


def _split3(x):
    hi = x.astype(BF16)
    r = x - hi.astype(F32)
    mid = r.astype(BF16)
    lo = (r - mid.astype(F32)).astype(BF16)
    return hi, mid, lo


def _dot(a, b):
    return jnp.dot(a, b, preferred_element_type=F32)


def _dot_nt(a, b):
    return lax.dot_general(a, b, (((1,), (1,)), ((), ())), preferred_element_type=F32)


def _project(x, w_pad):
    s, k = x.shape
    tm = 512
    chunk = 512

    def kern(x_ref, w_ref, qkv_ref, ff_ref, z_ref, xt_ref, vt_ref):
        xf = x_ref[...]
        xb = xf.astype(BF16)
        xt_ref[...] = xf.T.astype(BF16)
        for c0 in range(0, N_A, chunk):
            width = min(chunk, N_A - c0)
            res = _dot(xb, w_ref[:, c0:c0 + width])
            qkv_ref[:, c0:c0 + width] = res.astype(BF16)
            if c0 == COL_FV:
                vt_ref[...] = res.T.astype(BF16)
        ff_ref[...] = _dot(xb, w_ref[:, OFF_C:OFF_C + N_C])
        for c0 in range(0, N_B, 512):
            z_ref[:, c0:c0 + 512] = _dot(xb, w_ref[:, OFF_B + c0:OFF_B + c0 + 512])

    row = lambda i: (i, 0)
    return pl.pallas_call(
        kern, name="project",
        grid=(s // tm,),
        in_specs=[pl.BlockSpec((tm, k), row),
                  _resident((k, N_PAD), lambda i: (0, 0))],
        out_specs=[pl.BlockSpec((tm, N_A), row),
                   pl.BlockSpec((tm, N_C), row),
                   pl.BlockSpec((tm, N_B), row),
                   pl.BlockSpec((k, tm), lambda i: (0, i)),
                   pl.BlockSpec((FOX_W, tm), lambda i: (0, i))],
        out_shape=[jax.ShapeDtypeStruct((s, N_A), BF16),
                   jax.ShapeDtypeStruct((s, N_C), F32),
                   jax.ShapeDtypeStruct((s, N_B), F32),
                   jax.ShapeDtypeStruct((k, s), BF16),
                   jax.ShapeDtypeStruct((FOX_W, s), BF16)],
        compiler_params=_cparams(("parallel",)),
    )(x, w_pad)


def _grad_x_matmul(pieces, w_pad, dh, *, tm, tn, name):
    m = dh.shape[0]
    n, k = w_pad.shape
    widths = [p.shape[1] for p in pieces]
    offs = [sum(widths[:i]) for i in range(len(pieces))]
    assert sum(widths) == k

    def kern(*refs):
        p_refs, (b_ref, dh_ref, o_ref) = refs[:len(pieces)], refs[len(pieces):]
        acc = ALPHA * dh_ref[...]
        for p_ref, off, width in zip(p_refs, offs, widths):
            acc = acc + _dot_nt(p_ref[...], b_ref[:, off:off + width])
        o_ref[...] = acc

    assert tn == n
    return pl.pallas_call(
        kern, name=name,
        grid=(m // tm,),
        in_specs=[pl.BlockSpec((tm, w), lambda i: (i, 0)) for w in widths]
        + [_resident((n, k), lambda i: (0, 0)),
           pl.BlockSpec((tm, n), lambda i: (i, 0))],
        out_specs=pl.BlockSpec((tm, n), lambda i: (i, 0)),
        out_shape=jax.ShapeDtypeStruct((m, n), F32),
        compiler_params=_cparams(("parallel",)),
    )(*pieces, w_pad, dh)


def _grad_w_matmul(xt, blocks, *, tk, name):
    m, s = xt.shape
    tn = 512
    nb = len(blocks)

    def kern(a_ref, *refs):
        b_refs, o_ref = refs[:nb], refs[nb]

        @pl.when(pl.program_id(0) == 0)
        def _():
            o_ref[...] = jnp.zeros_like(o_ref)
        a = a_ref[...]
        for blk in range(nb):
            o_ref[:, blk * tn:(blk + 1) * tn] += _dot(a, b_refs[blk][...])

    return pl.pallas_call(
        kern, name=name,
        grid=(s // tk,),
        in_specs=[pl.BlockSpec((m, tk), lambda k: (0, k))]
        + [pl.BlockSpec((tk, tn), functools.partial(lambda k, col: (k, col), col=col)) for _, col in blocks],
        out_specs=_resident((m, nb * tn), lambda k: (0, 0)),
        out_shape=jax.ShapeDtypeStruct((m, nb * tn), F32),
        compiler_params=_cparams(("arbitrary",)),
    )(xt, *[arr for arr, _ in blocks])


def _tri(n, lower):
    r = lax.broadcasted_iota(jnp.int32, (n, n), 0)
    c = lax.broadcasted_iota(jnp.int32, (n, n), 1)
    keep = (c <= r) if lower else (c >= r)
    return jnp.where(keep, 1.0, 0.0).astype(BF16)


def _exact_dot(mat_bf16, x_f32, left):
    out = None
    for piece in _split3(x_f32):
        t = _dot(mat_bf16, piece) if left else _dot(piece, mat_bf16)
        out = t if out is None else out + t
    return out


def _log_sigmoid(z):
    return jnp.minimum(z, 0.0) - jnp.log(1.0 + jnp.exp(-jnp.abs(z)))


def _cum_fwd(ffp, bfp):
    s = ffp.shape[0]
    t = min(1024, s)

    def kern(ff_ref, b_ref, cum_ref, carry_ref):
        @pl.when(pl.program_id(0) == 0)
        def _():
            carry_ref[...] = jnp.zeros_like(carry_ref)
        lane = lax.broadcasted_iota(jnp.int32, (1, LANES), 1)
        lf = _log_sigmoid(ff_ref[...] + b_ref[...])
        lf = jnp.where(lane < FOX_HEADS, lf, 0.0)
        cum = _exact_dot(_tri(t, True), lf, True) + carry_ref[0:1, :]
        cum_ref[...] = cum
        carry_ref[...] = jnp.broadcast_to(cum[t - 1:t, :], carry_ref.shape)

    return pl.pallas_call(
        kern, name="cum_fwd",
        grid=(s // t,),
        in_specs=[pl.BlockSpec((t, LANES), lambda i: (i, 0)),
                  pl.BlockSpec((1, LANES), lambda i: (0, 0))],
        out_specs=pl.BlockSpec((t, LANES), lambda i: (i, 0)),
        out_shape=jax.ShapeDtypeStruct((s, LANES), F32),
        scratch_shapes=[pltpu.VMEM((8, LANES), F32)],
        compiler_params=_cparams(("arbitrary",)),
    )(ffp, bfp)


def _cum_bwd(dcum_k, dcum_q, ffp, bfp):
    s = dcum_k.shape[0]
    t = min(1024, s)
    nb = s // t

    def kern(dck_ref, dcq_ref, ff_ref, b_ref, dff_ref, gb_ref, carry_ref):
        @pl.when(pl.program_id(0) == 0)
        def _():
            carry_ref[...] = jnp.zeros_like(carry_ref)
            gb_ref[...] = jnp.zeros_like(gb_ref)
        lane = lax.broadcasted_iota(jnp.int32, (1, LANES), 1)
        dlf = _exact_dot(_tri(t, False), dck_ref[...] + dcq_ref[...], True) + carry_ref[0:1, :]
        carry_ref[...] = jnp.broadcast_to(dlf[0:1, :], carry_ref.shape)
        z = ff_ref[...] + b_ref[...]
        dff = jnp.where(lane < FOX_HEADS, dlf / (1.0 + jnp.exp(z)), 0.0)
        gb_ref[...] += jnp.broadcast_to(jnp.sum(dff, axis=0, keepdims=True), gb_ref.shape)
        dff_ref[...] = jnp.concatenate([dff, jnp.zeros_like(dff)], axis=1).astype(BF16)

    return pl.pallas_call(
        kern, name="cum_bwd",
        grid=(nb,),
        in_specs=[pl.BlockSpec((t, LANES), lambda i: (nb - 1 - i, 0)),
                  pl.BlockSpec((t, LANES), lambda i: (nb - 1 - i, 0)),
                  pl.BlockSpec((t, LANES), lambda i: (nb - 1 - i, 0)),
                  pl.BlockSpec((1, LANES), lambda i: (0, 0))],
        out_specs=[pl.BlockSpec((t, N_C), lambda i: (nb - 1 - i, 0)),
                   pl.BlockSpec((8, LANES), lambda i: (0, 0))],
        out_shape=[jax.ShapeDtypeStruct((s, N_C), BF16),
                   jax.ShapeDtypeStruct((8, LANES), F32)],
        scratch_shapes=[pltpu.VMEM((8, LANES), F32)],
        compiler_params=_cparams(("arbitrary",)),
    )(dcum_k, dcum_q, ffp, bfp)


def _resident(shape, index_map):
    return pl.BlockSpec(shape, index_map, pipeline_mode=pl.Buffered(1))


def _fox_fwd(qkv, vt, cum_t3, cum):
    s = qkv.shape[0]
    tk = tq = FOX_REF
    nq = s // tq
    nh = FOX_HEADS
    diag_tiles = tq // tk

    def kern(q_ref, k_ref, vt_ref, ct_ref, c_ref, o_ref, lse_ref, m_ref, acc_ref, u_ref):
        i = pl.program_id(0)
        lane = lax.broadcasted_iota(jnp.int32, (1, LANES), 1)
        krow = lax.broadcasted_iota(jnp.int32, (tk, tq), 0)
        qcol = lax.broadcasted_iota(jnp.int32, (tk, tq), 1)
        q0 = pl.multiple_of(i * tq, tq)
        qts, crefs = [], []
        for h in range(nh):
            p, a = divmod(h, 2)
            q2 = q_ref[:, p * LANES:(p + 1) * LANES] * jnp.asarray(QK_SCALE, BF16)
            sel = (lane < HEAD_DIM) if a == 0 else (lane >= HEAD_DIM)
            qts.append(jnp.where(sel, q2, jnp.zeros_like(q2)).astype(F32).T.astype(BF16))
            crefs.append(ct_ref[h, :, pl.ds(q0, LANES)][:, 0:1])
        m_ref[...] = jnp.full(m_ref.shape, NEG, F32)
        acc_ref[...] = jnp.zeros_like(acc_ref)
        ones = jnp.ones((SUM_ROWS, tk), BF16)

        def tile(j, diag):
            k0 = pl.multiple_of(j * tk, tk)
            cb = c_ref[pl.ds(k0, tk), :]
            sts = [_dot(k_ref[pl.ds(k0, tk), (h // 2) * LANES:(h // 2 + 1) * LANES], qts[h]) for h in range(nh)]
            tile_max = []
            for h in range(nh):
                u = sts[h] - (cb[:, h:h + 1] - crefs[h])
                if diag is not None:
                    u = jnp.where(krow + diag * tk <= qcol, u, NEG)
                u_ref[h] = u
                tile_max.append(jnp.max(u, axis=0, keepdims=True))
            pts, scales = [], []
            for h in range(nh):
                m_old = m_ref[h]
                m_new = jnp.maximum(m_old, tile_max[h])
                scales.append(jnp.exp(m_old - m_new))
                pts.append(jnp.exp(u_ref[h] - m_new).astype(BF16))
                m_ref[h] = m_new
            for h in range(nh):
                vth = jnp.concatenate([vt_ref[h * HEAD_DIM:(h + 1) * HEAD_DIM, pl.ds(k0, tk)], ones], axis=0)
                acc_ref[h] = scales[h] * acc_ref[h] + _dot(vth, pts[h])

        def body(j, c):
            tile(j, None)
            return c
        lax.fori_loop(0, i * diag_tiles, body, 0)
        for d in range(diag_tiles):
            tile(i * diag_tiles + d, d)

        ls = [acc_ref[h][HEAD_DIM:HEAD_DIM + 1] for h in range(nh)]
        for p in range(nh // 2):
            ot = jnp.concatenate([acc_ref[2 * p + a][:HEAD_DIM] * (1.0 / ls[2 * p + a]) for a in range(2)], axis=0)
            o_ref[:, p * LANES:(p + 1) * LANES] = ot.T
        for h in range(nh):
            lse_ref[h, :, pl.ds(q0, tq)] = m_ref[h] + jnp.log(ls[h])

    return pl.pallas_call(
        kern, name="fox_fwd",
        grid=(nq,),
        in_specs=[pl.BlockSpec((tq, FOX_W), lambda i: (i, 0)),
                  _resident((s, FOX_W), lambda i: (0, COL_FK // FOX_W)),
                  _resident((FOX_W, s), lambda i: (0, 0)),
                  _resident((nh, 1, s), lambda i: (0, 0, 0)),
                  _resident((s, LANES), lambda i: (0, 0))],
        out_specs=[pl.BlockSpec((tq, FOX_W), lambda i: (i, 0)),
                   pl.BlockSpec((nh, 1, s), lambda i: (0, 0, 0))],
        out_shape=[jax.ShapeDtypeStruct((s, FOX_W), F32),
                   jax.ShapeDtypeStruct((nh, 1, s), F32)],
        scratch_shapes=[pltpu.VMEM((nh, 1, tq), F32),
                        pltpu.VMEM((nh, HEAD_DIM + SUM_ROWS, tq), F32),
                        pltpu.VMEM((nh, tk, tq), F32)],
        compiler_params=_cparams(("arbitrary",)),
    )(qkv, qkv, vt, cum_t3, cum)


def _fox_bwd(qkv, do_bf, cum_t3, cum, lse_t3, delta_t3):
    s = qkv.shape[0]
    t = min(FOX_T, s)
    nq = s // t
    nh = FOX_HEADS
    npair = nh // 2

    def kern(q_ref, do_ref, k_ref, v_ref, ct_ref, c_ref, lse_ref, dl_ref,
             dqt_ref, dk_ref, dv_ref, dc_ref, dcq_ref, accv_ref, acck_ref, accd_ref):
        kj = pl.program_id(0)
        lane = lax.broadcasted_iota(jnp.int32, (1, LANES), 1)
        krow = lax.broadcasted_iota(jnp.int32, (t, t), 0)
        qcol = lax.broadcasted_iota(jnp.int32, (t, t), 1)
        causal = krow <= qcol
        sels = [lane < HEAD_DIM, lane >= HEAD_DIM]

        @pl.when(kj == 0)
        def _():
            dqt_ref[...] = jnp.zeros_like(dqt_ref)
            dcq_ref[...] = jnp.zeros_like(dcq_ref)

        cb = c_ref[...]
        k2s, v2s, kts = [], [], []
        for p in range(npair):
            k2 = k_ref[:, p * LANES:(p + 1) * LANES]
            k2s.append(k2)
            v2s.append(v_ref[:, p * LANES:(p + 1) * LANES])
            kt = k2.astype(F32).T * QK_SCALE
            kts.append(kt[:HEAD_DIM].astype(BF16))
            kts.append(kt[HEAD_DIM:].astype(BF16))
        css = [cb[:, h:h + 1] for h in range(nh)]

        def tile(i, masked):
            q0 = pl.multiple_of(i * t, t)
            r0 = pl.multiple_of((i // (FOX_REF // t)) * FOX_REF, FOX_REF)
            sts, dpts, qms, doms = [], [], [], []
            for h in range(nh):
                p, a = divmod(h, 2)
                qi = q_ref[pl.ds(q0, t), p * LANES:(p + 1) * LANES] * jnp.asarray(QK_SCALE, BF16)
                doi = do_ref[pl.ds(q0, t), p * LANES:(p + 1) * LANES]
                qm = jnp.where(sels[a], qi, jnp.zeros_like(qi))
                dom = jnp.where(sels[a], doi, jnp.zeros_like(doi))
                qms.append(qm)
                doms.append(dom)
                sts.append(_dot_nt(k2s[p], qm))
                dpts.append(_dot_nt(v2s[p], dom))
            pts, dsts = [], []
            for h in range(nh):
                cref = ct_ref[h, :, pl.ds(r0, LANES)][:, 0:1]
                pt = jnp.exp(sts[h] - (css[h] - cref) - lse_ref[h, :, pl.ds(q0, t)])
                if masked:
                    pt = jnp.where(causal, pt, 0.0)
                ds32 = pt * (dpts[h] - dl_ref[h, :, pl.ds(q0, t)])
                part = ds32[:, 0:LANES]
                for c in range(1, t // LANES):
                    part = part + ds32[:, c * LANES:(c + 1) * LANES]
                accd_ref[h] = part if masked else accd_ref[h] + part
                dcq_ref[h, :, pl.ds(q0, t)] += jnp.sum(ds32, axis=0, keepdims=True)
                pts.append(pt.astype(BF16))
                dsts.append(ds32.astype(BF16))
            for p in range(npair):
                ha, hb = 2 * p, 2 * p + 1
                dv_p = _dot(pts[ha], doms[ha]) + _dot(pts[hb], doms[hb])
                dk_p = _dot(dsts[ha], qms[ha]) + _dot(dsts[hb], qms[hb])
                accv_ref[p] = dv_p if masked else accv_ref[p] + dv_p
                acck_ref[p] = dk_p if masked else acck_ref[p] + dk_p
            for h in range(nh):
                dqt_ref[h * HEAD_DIM:(h + 1) * HEAD_DIM, pl.ds(q0, t)] += _dot(kts[h], dsts[h])

        tile(kj, True)

        def body(i, c):
            tile(i, False)
            return c
        lax.fori_loop(kj + 1, nq, body, 0)

        dc = jnp.zeros((t, LANES), F32)
        for h in range(nh):
            dc = jnp.where(lane == h, -jnp.sum(accd_ref[h], axis=1, keepdims=True), dc)
        dc_ref[...] = dc
        for p in range(npair):
            dv_ref[:, p * LANES:(p + 1) * LANES] = accv_ref[p].astype(BF16)
            dk_ref[:, p * LANES:(p + 1) * LANES] = acck_ref[p].astype(BF16)

    whole = lambda kj: (0, 0, 0)
    return pl.pallas_call(
        kern, name="fox_bwd",
        grid=(nq,),
        in_specs=[_resident((s, FOX_W), lambda kj: (0, 0)),
                  _resident((s, FOX_W), lambda kj: (0, 0)),
                  pl.BlockSpec((t, FOX_W), lambda kj: (kj, COL_FK // FOX_W)),
                  pl.BlockSpec((t, FOX_W), lambda kj: (kj, COL_FV // FOX_W)),
                  _resident((nh, 1, s), whole),
                  pl.BlockSpec((t, LANES), lambda kj: (kj, 0)),
                  _resident((nh, 1, s), whole),
                  _resident((nh, 1, s), whole)],
        out_specs=[_resident((FOX_W, s), lambda kj: (0, 0)),
                   pl.BlockSpec((t, FOX_W), lambda kj: (kj, 0)),
                   pl.BlockSpec((t, FOX_W), lambda kj: (kj, 0)),
                   pl.BlockSpec((t, LANES), lambda kj: (kj, 0)),
                   _resident((nh, 1, s), whole)],
        out_shape=[jax.ShapeDtypeStruct((FOX_W, s), F32),
                   jax.ShapeDtypeStruct((s, FOX_W), BF16),
                   jax.ShapeDtypeStruct((s, FOX_W), BF16),
                   jax.ShapeDtypeStruct((s, LANES), F32),
                   jax.ShapeDtypeStruct((nh, 1, s), F32)],
        scratch_shapes=[pltpu.VMEM((npair, t, LANES), F32),
                        pltpu.VMEM((npair, t, LANES), F32),
                        pltpu.VMEM((nh, t, LANES), F32)],
        compiler_params=_cparams(("arbitrary",)),
    )(qkv, do_bf, qkv, qkv, cum_t3, cum, lse_t3, delta_t3)


def _bucket_table():
    qi = np.arange(BLOCK)[:, None]
    kj = np.arange(2 * BLOCK)[None, :]
    rel = np.maximum(qi + BLOCK - kj, 0).astype(np.int32)
    max_exact = NUM_BUCKETS // 2
    relf = np.maximum(rel, 1).astype(np.float32)
    large = max_exact + (np.log(relf / np.float32(max_exact)) / np.float32(math.log(MAX_DISTANCE / max_exact))
                         * np.float32(NUM_BUCKETS - max_exact)).astype(np.int32)
    large = np.minimum(large, NUM_BUCKETS - 1)
    return np.where(rel < max_exact, rel, large).astype(np.int32)


SWA_LANES = SWA_GROUP * BLOCK


def _swa_bias(rel_bias, bucket_t):
    def kern(rb_ref, bk_ref, o_ref):
        bk = bk_ref[...]
        kj = lax.broadcasted_iota(jnp.int32, (2 * BLOCK, BLOCK), 0)
        qi = lax.broadcasted_iota(jnp.int32, (2 * BLOCK, BLOCK), 1)
        rel = qi + BLOCK - kj
        band = (rel >= 0) & (rel < BLOCK)
        masks = [band & (kj >= BLOCK), band]
        for h in range(SWA_HEADS):
            g, hh = divmod(h, SWA_GROUP)
            acc = jnp.zeros((2 * BLOCK, BLOCK), F32)
            for b in range(NUM_BUCKETS):
                acc = jnp.where(bk == b, rb_ref[b, h], acc)
            for first in range(2):
                o_ref[first, g, :, hh * BLOCK:(hh + 1) * BLOCK] = jnp.where(masks[first], acc, NEG)

    return pl.pallas_call(
        kern, name="swa_bias",
        in_specs=[pl.BlockSpec(memory_space=pltpu.SMEM),
                  pl.BlockSpec(memory_space=pltpu.VMEM)],
        out_specs=pl.BlockSpec(memory_space=pltpu.VMEM),
        out_shape=jax.ShapeDtypeStruct((2, SWA_KV_HEADS, 2 * BLOCK, SWA_LANES), F32),
        compiler_params=_cparams(),
    )(rel_bias, bucket_t)


SWA_STEP = 4


def _swa_keys(prev_ref, cur_ref):
    return jnp.concatenate([prev_ref[...], cur_ref[...]], axis=0)


def _swa_queries(x_ref, scale):
    x = x_ref[...]
    if scale:
        x = x * jnp.asarray(QK_SCALE, BF16)
    xt = x.astype(F32).T.astype(BF16)
    return [_group_rows(xt[:, b * BLOCK:(b + 1) * BLOCK]) for b in range(SWA_STEP)]


def _group_rows(xt):
    zeros = jnp.zeros((HEAD_DIM, SWA_LANES), BF16)
    out = []
    for g in range(SWA_KV_HEADS):
        heads = [xt[(SWA_GROUP * g + hh) * HEAD_DIM:(SWA_GROUP * g + hh + 1) * HEAD_DIM, :] for hh in range(SWA_GROUP)]
        rows = jnp.concatenate(heads, axis=1)
        padded = jnp.concatenate([rows, zeros] if g == 0 else [zeros, rows], axis=0)
        out.append((rows, padded))
    return out


def _pairs_to_rows(cols_t):
    out = []
    for p in range(SWA_HEADS // 2):
        g, hh = divmod(2 * p, SWA_GROUP)
        pair = jnp.concatenate([cols_t[g][:, hh * BLOCK:(hh + 1) * BLOCK],
                                cols_t[g][:, (hh + 1) * BLOCK:(hh + 2) * BLOCK]], axis=0)
        out.append(pair.T)
    return jnp.concatenate(out, axis=1)


def _swa_fwd(qkv, bias_t, sink_rows):
    s = qkv.shape[0]
    nb = s // BLOCK
    rows = SWA_STEP * BLOCK
    units = [(b, g) for b in range(SWA_STEP) for g in range(SWA_KV_HEADS)]

    def kern(q_ref, kp_ref, kc_ref, vp_ref, vc_ref, bias_ref, sink_ref, o_ref, lse_ref):
        n = pl.program_id(0)
        tables = [jnp.minimum(n, 1)] + [1] * (SWA_STEP - 1)
        k3 = _swa_keys(kp_ref, kc_ref)
        vt3 = _swa_keys(vp_ref, vc_ref).astype(F32).T.astype(BF16)
        qts = _swa_queries(q_ref, True)
        us = [_dot(k3[b * BLOCK:(b + 2) * BLOCK], qts[b][g][1]) + bias_ref[tables[b], g] for b, g in units]
        outs = []
        for (b, g), u in zip(units, us):
            sk = sink_ref[g]
            m = jnp.maximum(jnp.max(u, axis=0, keepdims=True), sk)
            p = jnp.exp(u - m)
            l = jnp.sum(p, axis=0, keepdims=True) + jnp.exp(sk - m)
            lse_ref[b, g] = m + jnp.log(l)
            vt = vt3[g * HEAD_DIM:(g + 1) * HEAD_DIM, b * BLOCK:(b + 2) * BLOCK]
            outs.append(_dot(vt, (p * (1.0 / l)).astype(BF16)))
        for b in range(SWA_STEP):
            o_ref[b * BLOCK:(b + 1) * BLOCK, :] = _pairs_to_rows(outs[b * SWA_KV_HEADS:(b + 1) * SWA_KV_HEADS])

    cq, ck, cv = COL_SQ // SWA_W, COL_SK // LANES, COL_SV // LANES
    prev = lambda n: jnp.maximum(SWA_STEP * n - 1, 0)
    return pl.pallas_call(
        kern, name="swa_fwd",
        grid=(nb // SWA_STEP,),
        in_specs=[pl.BlockSpec((rows, SWA_W), lambda n: (n, cq)),
                  pl.BlockSpec((BLOCK, LANES), lambda n: (prev(n), ck)),
                  pl.BlockSpec((rows, LANES), lambda n: (n, ck)),
                  pl.BlockSpec((BLOCK, LANES), lambda n: (prev(n), cv)),
                  pl.BlockSpec((rows, LANES), lambda n: (n, cv)),
                  _resident((2, SWA_KV_HEADS, 2 * BLOCK, SWA_LANES), lambda n: (0, 0, 0, 0)),
                  _resident((SWA_KV_HEADS, 1, SWA_LANES), lambda n: (0, 0, 0))],
        out_specs=[pl.BlockSpec((rows, SWA_W), lambda n: (n, 0)),
                   pl.BlockSpec((SWA_STEP, SWA_KV_HEADS, 1, SWA_LANES), lambda n: (n, 0, 0, 0))],
        out_shape=[jax.ShapeDtypeStruct((s, SWA_W), F32),
                   jax.ShapeDtypeStruct((nb, SWA_KV_HEADS, 1, SWA_LANES), F32)],
        compiler_params=_cparams(("parallel",)),
    )(qkv, qkv, qkv, qkv, qkv, bias_t, sink_rows)


def _swa_bwd(qkv, do_bf, delta_rows, lse, bias_t, sink_rows, bucket_t):
    s = qkv.shape[0]
    nb = s // BLOCK
    steps = nb // SWA_STEP
    rows = SWA_STEP * BLOCK
    units = [(b, g) for b in range(SWA_STEP) for g in range(SWA_KV_HEADS)]

    def kern(q_ref, kp_ref, kc_ref, vp_ref, vc_ref, do_ref, dl_ref, lse_ref, bias_ref, sink_ref, bk_ref,
             dq_ref, dk_ref, dv_ref, grb_ref, gsk_ref, dbias_ref, ck_ref, cv_ref, sk_ref):
        n = pl.program_id(0)

        @pl.when(n == 0)
        def _():
            dbias_ref[...] = jnp.zeros_like(dbias_ref)
            ck_ref[...] = jnp.zeros_like(ck_ref)
            cv_ref[...] = jnp.zeros_like(cv_ref)
            sk_ref[...] = jnp.zeros_like(sk_ref)

        @pl.when(n < steps)
        def _():
            tables = [jnp.minimum(n, 1)] + [1] * (SWA_STEP - 1)
            k3 = _swa_keys(kp_ref, kc_ref)
            v3 = _swa_keys(vp_ref, vc_ref)
            kt3 = (k3.astype(F32).T * QK_SCALE).astype(BF16)
            qts = _swa_queries(q_ref, True)
            dots = _swa_queries(do_ref, False)
            sts = [_dot(k3[b * BLOCK:(b + 2) * BLOCK], qts[b][g][1]) for b, g in units]
            dps = [_dot(v3[b * BLOCK:(b + 2) * BLOCK], dots[b][g][1]) for b, g in units]
            ps, dss = [], []
            for i, (b, g) in enumerate(units):
                lse_g = lse_ref[b, g]
                dlt = dl_ref[b, g]
                p = jnp.exp(sts[i] + bias_ref[tables[b], g] - lse_g)
                ds = p * (dps[i] - dlt)
                dbias_ref[g] += ds
                sk_ref[g] += -jnp.exp(sink_ref[g] - lse_g) * dlt
                ps.append(p.astype(BF16))
                dss.append(ds.astype(BF16))
            dk2, dv2 = [], []
            for b in range(SWA_STEP):
                at = lambda g: b * SWA_KV_HEADS + g
                groups = range(SWA_KV_HEADS)
                dv2.append(jnp.concatenate([_dot_nt(dots[b][g][0], ps[at(g)]) for g in groups], axis=0).T)
                dk2.append(jnp.concatenate([_dot_nt(qts[b][g][0], dss[at(g)]) for g in groups], axis=0).T)
                dqts = [_dot(kt3[g * HEAD_DIM:(g + 1) * HEAD_DIM, b * BLOCK:(b + 2) * BLOCK], dss[at(g)]) for g in groups]
                dq_ref[b * BLOCK:(b + 1) * BLOCK, :] = _pairs_to_rows(dqts).astype(BF16)
            last = (SWA_STEP - 1) * BLOCK
            for acc_ref, out_ref, parts in ((ck_ref, dk_ref, dk2), (cv_ref, dv_ref, dv2)):
                done = acc_ref[last:] + parts[0][:BLOCK]
                out_ref[...] = jnp.concatenate([acc_ref[:last], done], axis=0).astype(BF16)
                for b in range(SWA_STEP - 1):
                    acc_ref[b * BLOCK:(b + 1) * BLOCK] = parts[b][BLOCK:] + parts[b + 1][:BLOCK]
                acc_ref[last:] = parts[SWA_STEP - 1][BLOCK:]

        @pl.when(n == steps)
        def _():
            dk_ref[...] = ck_ref[...].astype(BF16)
            dv_ref[...] = cv_ref[...].astype(BF16)
            bk = bk_ref[...]
            lane = lax.broadcasted_iota(jnp.int32, (8, LANES), 1)
            rowi = lax.broadcasted_iota(jnp.int32, (NUM_BUCKETS, LANES), 0)
            lanei = lax.broadcasted_iota(jnp.int32, (NUM_BUCKETS, LANES), 1)
            out = jnp.zeros((NUM_BUCKETS, LANES), F32)
            gsk = jnp.zeros((8, LANES), F32)
            for h in range(SWA_HEADS):
                g, hh = divmod(h, SWA_GROUP)
                cols = slice(hh * BLOCK, (hh + 1) * BLOCK)
                gsk = jnp.where(lane == h, jnp.sum(sk_ref[g][:, cols]), gsk)
                db = dbias_ref[g][:, cols]
                for b in range(NUM_BUCKETS):
                    val = jnp.sum(jnp.where(bk == b, db, 0.0))
                    out = jnp.where((rowi == b) & (lanei == h), val, out)
            grb_ref[...] = out
            gsk_ref[...] = gsk

    cq, ck, cv = COL_SQ // SWA_W, COL_SK // LANES, COL_SV // LANES
    cur = lambda n: jnp.minimum(n, steps - 1)
    prev = lambda n: jnp.maximum(SWA_STEP * cur(n) - 1, 0)
    kout = lambda n: jnp.maximum(n - 1, 0)
    stat = pl.BlockSpec((SWA_STEP, SWA_KV_HEADS, 1, SWA_LANES), lambda n: (cur(n), 0, 0, 0))
    return pl.pallas_call(
        kern, name="swa_bwd",
        grid=(steps + 1,),
        in_specs=[pl.BlockSpec((rows, SWA_W), lambda n: (cur(n), cq)),
                  pl.BlockSpec((BLOCK, LANES), lambda n: (prev(n), ck)),
                  pl.BlockSpec((rows, LANES), lambda n: (cur(n), ck)),
                  pl.BlockSpec((BLOCK, LANES), lambda n: (prev(n), cv)),
                  pl.BlockSpec((rows, LANES), lambda n: (cur(n), cv)),
                  pl.BlockSpec((rows, SWA_W), lambda n: (cur(n), 1)),
                  stat, stat,
                  _resident((2, SWA_KV_HEADS, 2 * BLOCK, SWA_LANES), lambda n: (0, 0, 0, 0)),
                  _resident((SWA_KV_HEADS, 1, SWA_LANES), lambda n: (0, 0, 0)),
                  _resident((2 * BLOCK, BLOCK), lambda n: (0, 0))],
        out_specs=[pl.BlockSpec((rows, SWA_W), lambda n: (cur(n), 0)),
                   pl.BlockSpec((rows, LANES), lambda n: (kout(n), 0)),
                   pl.BlockSpec((rows, LANES), lambda n: (kout(n), 0)),
                   pl.BlockSpec((NUM_BUCKETS, LANES), lambda n: (0, 0)),
                   pl.BlockSpec((8, LANES), lambda n: (0, 0))],
        out_shape=[jax.ShapeDtypeStruct((s, SWA_W), BF16),
                   jax.ShapeDtypeStruct((s, LANES), BF16),
                   jax.ShapeDtypeStruct((s, LANES), BF16),
                   jax.ShapeDtypeStruct((NUM_BUCKETS, LANES), F32),
                   jax.ShapeDtypeStruct((8, LANES), F32)],
        scratch_shapes=[pltpu.VMEM((SWA_KV_HEADS, 2 * BLOCK, SWA_LANES), F32),
                        pltpu.VMEM((rows, LANES), F32),
                        pltpu.VMEM((rows, LANES), F32),
                        pltpu.VMEM((SWA_KV_HEADS, 1, SWA_LANES), F32)],
        compiler_params=_cparams(("arbitrary",)),
    )(qkv, qkv, qkv, qkv, qkv, do_bf, delta_rows, lse, bias_t, sink_rows, bucket_t)


def _post(x, target, o_fox, o_swa, z, w_o, ln_g, ln_b):
    s = x.shape[0]
    tm = min(256, s)
    nt = s // tm

    def kern(x_ref, t_ref, of_ref, os_ref, z_ref, w_ref, g_ref, b_ref,
             loss_ref, dh_ref, gwo_ref, do_ref, dz_ref, dl_ref, gg_ref, gb_ref, lacc_ref):
        step = pl.program_id(0)

        @pl.when(step == 0)
        def _():
            lacc_ref[...] = jnp.zeros_like(lacc_ref)
            gg_ref[...] = jnp.zeros_like(gg_ref)
            gwo_ref[...] = jnp.zeros_like(gwo_ref)
            gb_ref[...] = jnp.zeros_like(gb_ref)

        o = jnp.concatenate([of_ref[...], os_ref[...]], axis=1)
        zz = z_ref[...]
        sig = 1.0 / (1.0 + jnp.exp(-zz))
        silu = zz * sig
        mixed32 = o * silu
        mixed = mixed32.astype(BF16)
        w = w_ref[...]
        h = ALPHA * x_ref[...] + _dot(mixed, w)
        mu = jnp.mean(h, axis=1, keepdims=True)
        hc = h - mu
        var = jnp.mean(hc * hc, axis=1, keepdims=True)
        rstd = lax.rsqrt(var + LN_EPS)
        xhat = hc * rstd
        g = g_ref[...]
        err = xhat * g + b_ref[...] - t_ref[...]
        lacc_ref[...] += jnp.broadcast_to(jnp.sum(err * err, axis=0, keepdims=True), lacc_ref.shape)
        dout = err * (1.0 / D_MODEL)
        gg_ref[...] += jnp.broadcast_to(jnp.sum(dout * xhat, axis=0, keepdims=True), gg_ref.shape)
        gb_ref[...] += jnp.broadcast_to(jnp.sum(dout, axis=0, keepdims=True), gb_ref.shape)
        dxh = dout * g
        m1 = jnp.mean(dxh, axis=1, keepdims=True)
        m2 = jnp.mean(dxh * xhat, axis=1, keepdims=True)
        dh = rstd * (dxh - m1 - xhat * m2)
        dh_ref[...] = dh
        dy = dh.astype(BF16)
        gwo_ref[...] += _dot(mixed32.T.astype(BF16), dy)
        dmix = _dot_nt(dy, w)
        do = dmix * silu
        do_ref[...] = do.astype(BF16)
        dz_ref[...] = (dmix * o * (sig * (1.0 + zz * (1.0 - sig)))).astype(BF16)
        r = lax.broadcasted_iota(jnp.int32, (D_MODEL, LANES), 0) // HEAD_DIM
        c = lax.broadcasted_iota(jnp.int32, (D_MODEL, LANES), 1)
        pick = jnp.where(r == c, 1.0, 0.0).astype(BF16)
        dl_ref[...] = _exact_dot(pick, do * o, False)

        @pl.when(step == nt - 1)
        def _():
            tot = jnp.sum(lacc_ref[0:1, :]) * (0.5 / D_MODEL)
            loss_ref[...] = jnp.broadcast_to(tot, loss_ref.shape)

    row = lambda i: (i, 0)
    fixed = lambda i: (0, 0)
    wide = pl.BlockSpec((tm, D_MODEL), row)
    half = pl.BlockSpec((tm, FOX_W), row)
    return pl.pallas_call(
        kern, name="post",
        grid=(nt,),
        in_specs=[wide, wide, half, half, wide,
                  pl.BlockSpec((D_MODEL, D_MODEL), fixed),
                  pl.BlockSpec((1, D_MODEL), fixed),
                  pl.BlockSpec((1, D_MODEL), fixed)],
        out_specs=[pl.BlockSpec((8, LANES), fixed), wide,
                   _resident((D_MODEL, D_MODEL), fixed), wide, wide,
                   pl.BlockSpec((tm, LANES), row),
                   pl.BlockSpec((8, D_MODEL), fixed), pl.BlockSpec((8, D_MODEL), fixed)],
        out_shape=[jax.ShapeDtypeStruct((8, LANES), F32),
                   jax.ShapeDtypeStruct((s, D_MODEL), F32),
                   jax.ShapeDtypeStruct((D_MODEL, D_MODEL), F32),
                   jax.ShapeDtypeStruct((s, D_MODEL), BF16),
                   jax.ShapeDtypeStruct((s, D_MODEL), BF16),
                   jax.ShapeDtypeStruct((s, LANES), F32),
                   jax.ShapeDtypeStruct((8, D_MODEL), F32),
                   jax.ShapeDtypeStruct((8, D_MODEL), F32)],
        scratch_shapes=[pltpu.VMEM((8, D_MODEL), F32)],
        compiler_params=_cparams(("arbitrary",)),
    )(x, target, o_fox, o_swa, z, w_o, ln_g, ln_b)


def _adamw_math(w, g, m, v):
    m = ADAM_B1 * m + (1.0 - ADAM_B1) * g
    v = ADAM_B2 * v + (1.0 - ADAM_B2) * (g * g)
    m_hat = m / (1.0 - ADAM_B1 ** ADAM_STEP)
    v_hat = v / (1.0 - ADAM_B2 ** ADAM_STEP)
    delta = -ADAM_LR * (m_hat / (jnp.sqrt(v_hat) + ADAM_EPS) + ADAM_WD * w)
    return delta, m, v


def _adamw(w, g, m, v, *, name):
    r, c = w.shape
    tr = min(256, r)

    def kern(w_ref, g_ref, m_ref, v_ref, d_ref, mo_ref, vo_ref):
        d, mn, vn = _adamw_math(w_ref[...], g_ref[...], m_ref[...], v_ref[...])
        d_ref[...] = d
        mo_ref[...] = mn
        vo_ref[...] = vn

    blk = pl.BlockSpec((tr, c), lambda i: (i, 0))
    sds = jax.ShapeDtypeStruct((r, c), F32)
    return pl.pallas_call(
        kern, name=name,
        grid=(r // tr,),
        in_specs=[blk, blk, blk, blk],
        out_specs=[blk, blk, blk],
        out_shape=[sds, sds, sds],
        compiler_params=_cparams(("parallel",)),
    )(w, g, m, v)


def _adamw_cols(w, g, m, v, *, name):
    c, _, r = w.shape
    tc = 139
    assert c % tc == 0

    def kern(w_ref, g_ref, m_ref, v_ref, d_ref, mo_ref, vo_ref):
        d, mn, vn = _adamw_math(w_ref[...], g_ref[...], m_ref[...], v_ref[...])
        d_ref[...] = d
        mo_ref[...] = mn
        vo_ref[...] = vn

    blk = pl.BlockSpec((tc, 1, r), lambda i: (i, 0, 0))
    sds = jax.ShapeDtypeStruct((c, 1, r), F32)
    return pl.pallas_call(
        kern, name=name,
        grid=(c // tc,),
        in_specs=[blk, blk, blk, blk],
        out_specs=[blk, blk, blk],
        out_shape=[sds, sds, sds],
        compiler_params=_cparams(("parallel",)),
    )(w, g, m, v)


def _position():
    x, y, c = lax.axis_index("x"), lax.axis_index("y"), lax.axis_index("c")
    chips = [(1 - x, y), (x, 1 - y), (1 - x, 1 - y)]
    return x, y, c, chips


def _chip_index(cx, cy):
    return 2 * cx + cy


def _gather_weights(w_in_bf, w_o_bf):
    shards = (w_in_bf, w_o_bf)
    n_arr = len(shards)

    def kern(*refs):
        ins, outs = refs[:n_arr], refs[n_arr:2 * n_arr]
        send_sems, recv_sems, local_sems = refs[2 * n_arr:]
        x, y, c, chips = _position()
        me = _chip_index(x, y)
        sibling = (x, y, 1 - c)

        local = [pltpu.make_async_copy(ins[a], outs[a].at[me], local_sems.at[a]) for a in range(n_arr)]
        for cp in local:
            cp.start()

        def half(ref, a):
            rows = shards[a].shape[0] // 2
            return ref.at[pl.ds(c * rows, rows), :]

        def copy(a, k, src, slot, to):
            return pltpu.make_async_remote_copy(
                src_ref=src, dst_ref=half(outs[a].at[slot], a),
                send_sem=send_sems.at[a * 6 + k], recv_sem=recv_sems.at[a * 6 + k],
                device_id=to, device_id_type=MESH)

        first = [copy(a, j, half(ins[a], a), me, (*chip, c)) for a in range(n_arr) for j, chip in enumerate(chips)]
        for cp in first:
            cp.start()
        passed = []
        for a in range(n_arr):
            for j, chip in enumerate(chips):
                slot = _chip_index(*chip)
                copy(a, j, half(ins[a], a), slot, (*chip, c)).wait_recv()
                fwd = copy(a, 3 + j, half(outs[a].at[slot], a), slot, sibling)
                fwd.start()
                passed.append(fwd)
        for a in range(n_arr):
            for j, chip in enumerate(chips):
                slot = _chip_index(*chip)
                rows = shards[a].shape[0] // 2
                dst = outs[a].at[slot].at[pl.ds((1 - c) * rows, rows), :]
                pltpu.make_async_remote_copy(
                    src_ref=dst, dst_ref=dst, send_sem=send_sems.at[a * 6 + 3 + j],
                    recv_sem=recv_sems.at[a * 6 + 3 + j], device_id=sibling, device_id_type=MESH).wait_recv()
        for cp in first + passed:
            cp.wait_send()
        for cp in local:
            cp.wait()

    vmem = pl.BlockSpec(memory_space=pltpu.VMEM)
    return pl.pallas_call(
        kern, name="gather_weights",
        in_specs=[vmem] * n_arr,
        out_specs=[vmem] * n_arr,
        out_shape=[jax.ShapeDtypeStruct((N_CHIPS,) + w.shape, w.dtype) for w in shards],
        scratch_shapes=[pltpu.SemaphoreType.DMA((6 * n_arr,)),
                        pltpu.SemaphoreType.DMA((6 * n_arr,)),
                        pltpu.SemaphoreType.DMA((n_arr,))],
        compiler_params=_cparams(),
    )(*shards)


def _pair_reduce(grads):
    n_arr = len(grads)
    chunk = 128

    def kern(*refs):
        ins = refs[:n_arr]
        outs = refs[n_arr:2 * n_arr]
        gots = refs[2 * n_arr:3 * n_arr]
        send_sems, recv_sems = refs[3 * n_arr:]
        x, y, c, _ = _position()
        sibling = (x, y, 1 - c)
        copies = []
        for a in range(n_arr):
            rows = grads[a].shape[1] // 2
            copies.append(pltpu.make_async_remote_copy(
                src_ref=ins[a].at[:, pl.ds((1 - c) * rows, rows), :], dst_ref=gots[a],
                send_sem=send_sems.at[a], recv_sem=recv_sems.at[a], device_id=sibling, device_id_type=MESH))
        for cp in copies:
            cp.start()
        for a in range(n_arr):
            copies[a].wait()
            rows = grads[a].shape[1] // 2
            for j in range(N_CHIPS):
                for r0 in range(0, rows, chunk):
                    mine = ins[a][j, pl.ds(pl.multiple_of(c * rows + r0, chunk), chunk), :]
                    outs[a][j, r0:r0 + chunk, :] = (mine + gots[a][j, r0:r0 + chunk, :]).astype(BF16)

    vmem = pl.BlockSpec(memory_space=pltpu.VMEM)
    half = [(N_CHIPS, g.shape[1] // 2, g.shape[2]) for g in grads]
    return pl.pallas_call(
        kern, name="pair_reduce",
        in_specs=[vmem] * n_arr,
        out_specs=[vmem] * n_arr,
        out_shape=[jax.ShapeDtypeStruct(h, BF16) for h in half],
        scratch_shapes=[pltpu.VMEM(h, F32) for h in half]
        + [pltpu.SemaphoreType.DMA((n_arr,)), pltpu.SemaphoreType.DMA((n_arr,))],
        compiler_params=_cparams(),
    )(*grads)


def _chip_reduce(parts):
    n_arr = len(parts)
    chunk = 128

    def kern(*refs):
        ins = refs[:n_arr]
        outs = refs[n_arr:2 * n_arr]
        slabs = refs[2 * n_arr:3 * n_arr]
        send_sems, recv_sems, local_sems = refs[3 * n_arr:]
        x, y, c, chips = _position()
        me = _chip_index(x, y)
        local = [pltpu.make_async_copy(ins[a].at[me], slabs[a].at[me], local_sems.at[a]) for a in range(n_arr)]
        for cp in local:
            cp.start()
        sends = []
        for a in range(n_arr):
            for j, chip in enumerate(chips):
                sends.append(pltpu.make_async_remote_copy(
                    src_ref=ins[a].at[_chip_index(*chip)], dst_ref=slabs[a].at[me],
                    send_sem=send_sems.at[a * 3 + j], recv_sem=recv_sems.at[a * 3 + j],
                    device_id=(*chip, c), device_id_type=MESH))
        for cp in sends:
            cp.start()
        for a in range(n_arr):
            for j, chip in enumerate(chips):
                slot = slabs[a].at[_chip_index(*chip)]
                pltpu.make_async_remote_copy(
                    src_ref=slot, dst_ref=slot, send_sem=send_sems.at[a * 3 + j],
                    recv_sem=recv_sems.at[a * 3 + j], device_id=(*chip, c), device_id_type=MESH).wait_recv()
        for cp in sends:
            cp.wait_send()
        for cp in local:
            cp.wait()
        for a in range(n_arr):
            for r0 in range(0, parts[a].shape[1], chunk):
                f = lambda j: slabs[a][j, r0:r0 + chunk, :].astype(F32)
                outs[a][r0:r0 + chunk, :] = ((f(0) + f(1)) + f(2)) + f(3)

    vmem = pl.BlockSpec(memory_space=pltpu.VMEM)
    return pl.pallas_call(
        kern, name="chip_reduce",
        in_specs=[vmem] * n_arr,
        out_specs=[vmem] * n_arr,
        out_shape=[jax.ShapeDtypeStruct(p.shape[1:], F32) for p in parts],
        scratch_shapes=[pltpu.VMEM(p.shape, BF16) for p in parts]
        + [pltpu.SemaphoreType.DMA((3 * n_arr,)),
           pltpu.SemaphoreType.DMA((3 * n_arr,)),
           pltpu.SemaphoreType.DMA((n_arr,))],
        compiler_params=_cparams(),
    )(*parts)


def _join_halves(halves):
    n_arr = len(halves)

    def kern(*refs):
        ins = refs[:n_arr]
        outs = refs[n_arr:2 * n_arr]
        send_sems, recv_sems, local_sems = refs[2 * n_arr:]
        x, y, c, _ = _position()
        sibling = (x, y, 1 - c)
        local, remote = [], []
        for a in range(n_arr):
            rows = halves[a].shape[0]
            mine = outs[a].at[pl.ds(c * rows, rows), :]
            local.append(pltpu.make_async_copy(ins[a], mine, local_sems.at[a]))
            remote.append(pltpu.make_async_remote_copy(
                src_ref=ins[a], dst_ref=mine, send_sem=send_sems.at[a], recv_sem=recv_sems.at[a],
                device_id=sibling, device_id_type=MESH))
        for cp in local + remote:
            cp.start()
        for a in range(n_arr):
            rows = halves[a].shape[0]
            theirs = outs[a].at[pl.ds((1 - c) * rows, rows), :]
            pltpu.make_async_remote_copy(
                src_ref=theirs, dst_ref=theirs, send_sem=send_sems.at[a], recv_sem=recv_sems.at[a],
                device_id=sibling, device_id_type=MESH).wait_recv()
        for cp in remote:
            cp.wait_send()
        for cp in local:
            cp.wait()

    vmem = pl.BlockSpec(memory_space=pltpu.VMEM)
    return pl.pallas_call(
        kern, name="join_halves",
        in_specs=[vmem] * n_arr,
        out_specs=[vmem] * n_arr,
        out_shape=[jax.ShapeDtypeStruct((2 * h.shape[0], h.shape[1]), F32) for h in halves],
        scratch_shapes=[pltpu.SemaphoreType.DMA((n_arr,)),
                        pltpu.SemaphoreType.DMA((n_arr,)),
                        pltpu.SemaphoreType.DMA((n_arr,))],
        compiler_params=_cparams(),
    )(*halves)


def _small_allreduce_adamw(partials, params, moms, vels):
    chunks = D_MODEL // LANES
    row_rb, row_bf, row_sk, row_loss = 2 * chunks, 2 * chunks + NUM_BUCKETS, 2 * chunks + NUM_BUCKETS + 1, SMALL_ROWS - 6

    def kern(gbf_ref, grb_ref, gsk_ref, gg_ref, gb_ref, loss_ref, *refs):
        p_refs, m_refs, v_refs = refs[0:5], refs[5:10], refs[10:15]
        lo_ref, g_outs, d_outs, mo_outs, vo_outs = refs[15], refs[16:21], refs[21:26], refs[26:31], refs[31:36]
        send_ref, buf_ref, send_sems, recv_sems = refs[36:]
        x, y, c, _ = _position()
        me = 4 * x + 2 * y + c
        send_ref[...] = jnp.zeros_like(send_ref)
        for r in range(chunks):
            send_ref[r:r + 1, :] = gg_ref[0:1, r * LANES:(r + 1) * LANES]
            send_ref[chunks + r:chunks + r + 1, :] = gb_ref[0:1, r * LANES:(r + 1) * LANES]
        send_ref[row_rb:row_rb + NUM_BUCKETS, :] = grb_ref[...]
        send_ref[row_bf:row_bf + 1, :] = gbf_ref[0:1, :]
        send_ref[row_sk:row_sk + 1, :] = gsk_ref[0:1, :]
        send_ref[row_loss:row_loss + 1, :] = loss_ref[0:1, :]
        buf_ref[me] = send_ref[...]
        peers = [(x, y, 1 - c)] + [(px, py, pc) for px, py in _position()[3] for pc in (c, 1 - c)]
        sends = []
        for k, peer in enumerate(peers):
            sends.append(pltpu.make_async_remote_copy(
                src_ref=send_ref, dst_ref=buf_ref.at[me], send_sem=send_sems.at[k], recv_sem=recv_sems.at[k],
                device_id=peer, device_id_type=MESH))
        for cp in sends:
            cp.start()
        for k, (px, py, pc) in enumerate(peers):
            slot = buf_ref.at[4 * px + 2 * py + pc]
            pltpu.make_async_remote_copy(
                src_ref=slot, dst_ref=slot, send_sem=send_sems.at[k], recv_sem=recv_sems.at[k],
                device_id=(px, py, pc), device_id_type=MESH).wait_recv()
        for cp in sends:
            cp.wait_send()
        tot = buf_ref[0]
        for d in range(1, N_DEV):
            tot = tot + buf_ref[d]
        lo_ref[...] = tot[row_loss:row_loss + 1, :]
        grads = [tot[row_bf:row_bf + 1, 0:FOX_HEADS],
                 tot[row_rb:row_rb + NUM_BUCKETS, 0:SWA_HEADS],
                 tot[row_sk:row_sk + 1, 0:SWA_HEADS],
                 jnp.concatenate([tot[r:r + 1, :] for r in range(chunks)], axis=1),
                 jnp.concatenate([tot[chunks + r:chunks + r + 1, :] for r in range(chunks)], axis=1)]
        for i, g in enumerate(grads):
            g_outs[i][...] = g
            delta, mn, vn = _adamw_math(p_refs[i][...], g, m_refs[i][...], v_refs[i][...])
            d_outs[i][...] = delta
            mo_outs[i][...] = mn
            vo_outs[i][...] = vn

    vm = pl.BlockSpec(memory_space=pltpu.VMEM)
    shapes = [jax.ShapeDtypeStruct(p.shape, F32) for p in params]
    outs = pl.pallas_call(
        kern, name="small_allreduce_adamw",
        in_specs=[vm] * 21,
        out_specs=[vm] * 21,
        out_shape=[jax.ShapeDtypeStruct((1, LANES), F32)] + shapes * 4,
        scratch_shapes=[pltpu.VMEM((SMALL_ROWS, LANES), F32),
                        pltpu.VMEM((N_DEV, SMALL_ROWS, LANES), F32),
                        pltpu.SemaphoreType.DMA((N_DEV - 1,)),
                        pltpu.SemaphoreType.DMA((N_DEV - 1,))],
    )(*partials, *params, *moms, *vels)
    return outs[0], outs[1:6], outs[6:11], outs[11:16], outs[16:21]


def _to_padded_cols(w):
    pad = jnp.zeros((w.shape[0], N_C - FOX_HEADS), w.dtype)
    return jnp.concatenate([w[:, 0:1536], w[:, 2056:2824], w[:, 1536:1544], pad,
                            w[:, 1544:2056], w[:, 2824:3336]], axis=1)


def _from_padded_cols(g):
    return jnp.concatenate([g[:, 0:1536], g[:, OFF_C:OFF_C + FOX_HEADS], g[:, OFF_B:OFF_B + FOX_W],
                            g[:, 1536:N_A], g[:, OFF_B + FOX_W:N_PAD]], axis=1)


def _fox_rows(a):
    return a[:, :FOX_HEADS].T.reshape(FOX_HEADS, 1, a.shape[0])


def kernel(x, w_in, b_f, rel_bias, sink, w_o, ln_g, ln_b, loss_target, m_w_in, m_b_f, m_rel_bias, m_sink, m_w_o, m_ln_g, m_ln_b, v_w_in, v_b_f, v_rel_bias, v_sink, v_w_o, v_ln_g, v_ln_b):
    x2 = x[0]
    tgt = loss_target[0]
    s = x2.shape[0]
    w_in2, w_o2 = w_in[0], w_o[0]

    shard_cols = D_IN // N_CHIPS
    col_pad = ((0, 0), (0, SHARD_PAD - shard_cols))
    w_in_all, w_o_all = _gather_weights(jnp.pad(w_in2.astype(BF16), col_pad), w_o2.astype(BF16))
    w_full = jnp.concatenate([w_in_all[j, :, :shard_cols] for j in range(N_CHIPS)], axis=1)
    w_pad = _to_padded_cols(w_full)
    w_o_full = w_o_all.reshape(D_MODEL, D_MODEL)

    qkv, ffp, z, xt, vt = _project(x2, w_pad)
    bfp = jnp.pad(b_f, ((0, 0), (0, LANES - FOX_HEADS)))
    cum = _cum_fwd(ffp, bfp)
    cum_t3 = _fox_rows(cum)
    o_fox, lse_t3 = _fox_fwd(qkv, vt, cum_t3, cum)
    bucket_t = jnp.asarray(_bucket_table().T)
    bias_t = _swa_bias(rel_bias, bucket_t)
    sink_rows = jnp.repeat(sink.reshape(SWA_KV_HEADS, SWA_GROUP, 1), BLOCK, axis=2).reshape(SWA_KV_HEADS, 1, SWA_LANES)
    o_swa, lse_swa = _swa_fwd(qkv, bias_t, sink_rows)

    loss8, dh, grad_w_o_full, do_bf, dz, delta, gg8, gb8 = _post(
        x2, tgt, o_fox, o_swa, z, w_o_full, ln_g, ln_b)

    delta_t3 = _fox_rows(delta)
    dqt_fox, dk_fox, dv_fox, dcum_k, dcum_q = _fox_bwd(qkv, do_bf, cum_t3, cum, lse_t3, delta_t3)
    dcum_q = jnp.pad(dcum_q.reshape(FOX_HEADS, s).T, ((0, 0), (0, LANES - FOX_HEADS)))
    dff, gbf8 = _cum_bwd(dcum_k, dcum_q, ffp, bfp)
    delta_rows = (delta[:, FOX_HEADS:FOX_HEADS + SWA_HEADS].reshape(s // BLOCK, BLOCK, SWA_KV_HEADS, SWA_GROUP)
                  .transpose(0, 2, 3, 1).reshape(s // BLOCK, SWA_KV_HEADS, 1, SWA_LANES))
    dq_swa, dk_swa, dv_swa, grb, gsk8 = _swa_bwd(qkv, do_bf, delta_rows, lse_swa, bias_t, sink_rows, bucket_t)

    dq_fox = dqt_fox.T.astype(BF16)
    d_misc = jnp.concatenate([dk_swa, dv_swa, dff], axis=1)
    pieces = [dq_fox, dk_fox, dv_fox, dq_swa, d_misc, dz]
    grad_x = _grad_x_matmul(pieces, w_pad, dh, tm=512, tn=D_MODEL, name="grad_x")
    blocks = [(p, 0) for p in pieces[:-1]] + [(dz, 0), (dz, 1)]
    grad_w_pad = _grad_w_matmul(xt, blocks, tk=1024, name="grad_w_in")
    grad_w_in_full = _from_padded_cols(grad_w_pad)

    g_in4 = jnp.stack([jnp.pad(grad_w_in_full[:, j * shard_cols:(j + 1) * shard_cols], col_pad)
                       for j in range(N_CHIPS)])
    g_o4 = grad_w_o_full.reshape(N_CHIPS, D_MODEL // N_CHIPS, D_MODEL)
    g_w_in, g_w_o = _join_halves(_chip_reduce(_pair_reduce([g_in4, g_o4])))
    g_w_in = g_w_in[:, :shard_cols]

    cols_first = lambda a: jnp.transpose(a, (2, 0, 1))
    rows_first = lambda a: jnp.transpose(a, (1, 2, 0))
    g_cols = cols_first(g_w_in[None])
    d_w_in, nm_w_in, nv_w_in = [rows_first(a) for a in _adamw_cols(
        cols_first(w_in), g_cols, cols_first(m_w_in), cols_first(v_w_in), name="adamw_w_in")]
    g_w_in = rows_first(g_cols)
    d_w_o, nm_w_o, nv_w_o = _adamw(w_o2, g_w_o, m_w_o[0], v_w_o[0], name="adamw_w_o")

    loss_row, gs, ds, ms, vs = _small_allreduce_adamw(
        [gbf8, grb, gsk8, gg8, gb8, loss8],
        [b_f, rel_bias, sink, ln_g, ln_b],
        [m_b_f, m_rel_bias, m_sink, m_ln_g, m_ln_b],
        [v_b_f, v_rel_bias, v_sink, v_ln_g, v_ln_b])
    loss = loss_row[0, 0]
    g_bf, g_rb, g_sk, g_lg, g_lb = gs
    d_bf, d_rb, d_sk, d_lg, d_lb = ds
    m_bf, m_rb, m_sk, m_lg, m_lb = ms
    v_bf, v_rb, v_sk, v_lg, v_lb = vs

    e = lambda a: a[None]
    return (loss, e(grad_x),
            g_w_in, g_bf, g_rb, g_sk, e(g_w_o), g_lg, g_lb,
            d_w_in, d_bf, d_rb, d_sk, e(d_w_o), d_lg, d_lb,
            nm_w_in, m_bf, m_rb, m_sk, e(nm_w_o), m_lg, m_lb,
            nv_w_in, v_bf, v_rb, v_sk, e(nv_w_o), v_lg, v_lb)
```

```python
import functools
import math

import numpy as np
import jax
import jax.numpy as jnp
from jax import lax
from jax.experimental import pallas as pl
from jax.experimental.pallas import tpu as pltpu

F32 = jnp.float32
BF16 = jnp.bfloat16

D_MODEL = 1024
HEAD_DIM = 64
FOX_HEADS = 8
SWA_HEADS = 8
SWA_KV_HEADS = 2
SWA_GROUP = 4
FOX_W = 512
SWA_W = 512
SWA_KV_W = 128
BLOCK = 128
NUM_BUCKETS = 32
MAX_DISTANCE = 128
LN_EPS = 1e-5
NEG = -1e30
ALPHA = 2.0 ** 0.25
QK_SCALE = 0.125

ADAM_LR = 0.001
ADAM_B1 = 0.9
ADAM_B2 = 0.999
ADAM_EPS = 1e-08
ADAM_WD = 0.01
ADAM_STEP = 10

D_IN = 3336
SHARD_PAD = 896
N_A = 2304
N_C = 256
N_B = 1024
OFF_C = N_A
OFF_B = N_A + N_C
N_PAD = N_A + N_C + N_B
COL_FK, COL_FV, COL_SQ, COL_SK, COL_SV = 512, 1024, 1536, 2048, 2176

LANES = 128
FOX_T = 256
FOX_REF = 512
SUM_ROWS = 16
VMEM_LIMIT = 56 * 1024 * 1024

MESH = pl.DeviceIdType.MESH
N_CHIPS = 4
N_DEV = 8
SMALL_ROWS = 56


def _cparams(sem=None):
    return pltpu.CompilerParams(dimension_semantics=sem, vmem_limit_bytes=VMEM_LIMIT)


def _split3(x):
    hi = x.astype(BF16)
    r = x - hi.astype(F32)
    mid = r.astype(BF16)
    lo = (r - mid.astype(F32)).astype(BF16)
    return hi, mid, lo


def _dot(a, b):
    return jnp.dot(a, b, preferred_element_type=F32)


def _dot_nt(a, b):
    return lax.dot_general(a, b, (((1,), (1,)), ((), ())), preferred_element_type=F32)


def _project(x, w_pad):
    s, k = x.shape
    tm = 512
    chunk = 512

    def kern(x_ref, w_ref, qkv_ref, ff_ref, z_ref, xt_ref, vt_ref):
        xf = x_ref[...]
        xb = xf.astype(BF16)
        xt_ref[...] = xf.T.astype(BF16)
        for c0 in range(0, N_A, chunk):
            width = min(chunk, N_A - c0)
            res = _dot(xb, w_ref[:, c0:c0 + width])
            qkv_ref[:, c0:c0 + width] = res.astype(BF16)
            if c0 == COL_FV:
                vt_ref[...] = res.T.astype(BF16)
        ff_ref[...] = _dot(xb, w_ref[:, OFF_C:OFF_C + N_C])
        for c0 in range(0, N_B, 512):
            z_ref[:, c0:c0 + 512] = _dot(xb, w_ref[:, OFF_B + c0:OFF_B + c0 + 512])

    row = lambda i: (i, 0)
    return pl.pallas_call(
        kern, name="project",
        grid=(s // tm,),
        in_specs=[pl.BlockSpec((tm, k), row),
                  _resident((k, N_PAD), lambda i: (0, 0))],
        out_specs=[pl.BlockSpec((tm, N_A), row),
                   pl.BlockSpec((tm, N_C), row),
                   pl.BlockSpec((tm, N_B), row),
                   pl.BlockSpec((k, tm), lambda i: (0, i)),
                   pl.BlockSpec((FOX_W, tm), lambda i: (0, i))],
        out_shape=[jax.ShapeDtypeStruct((s, N_A), BF16),
                   jax.ShapeDtypeStruct((s, N_C), F32),
                   jax.ShapeDtypeStruct((s, N_B), F32),
                   jax.ShapeDtypeStruct((k, s), BF16),
                   jax.ShapeDtypeStruct((FOX_W, s), BF16)],
        compiler_params=_cparams(("parallel",)),
    )(x, w_pad)


def _grad_x_matmul(pieces, w_pad, dh, *, tm, tn, name):
    m = dh.shape[0]
    n, k = w_pad.shape
    widths = [p.shape[1] for p in pieces]
    offs = [sum(widths[:i]) for i in range(len(pieces))]
    assert sum(widths) == k

    def kern(*refs):
        p_refs, (b_ref, dh_ref, o_ref) = refs[:len(pieces)], refs[len(pieces):]
        acc = ALPHA * dh_ref[...]
        for p_ref, off, width in zip(p_refs, offs, widths):
            acc = acc + _dot_nt(p_ref[...], b_ref[:, off:off + width])
        o_ref[...] = acc

    assert tn == n
    return pl.pallas_call(
        kern, name=name,
        grid=(m // tm,),
        in_specs=[pl.BlockSpec((tm, w), lambda i: (i, 0)) for w in widths]
        + [_resident((n, k), lambda i: (0, 0)),
           pl.BlockSpec((tm, n), lambda i: (i, 0))],
        out_specs=pl.BlockSpec((tm, n), lambda i: (i, 0)),
        out_shape=jax.ShapeDtypeStruct((m, n), F32),
        compiler_params=_cparams(("parallel",)),
    )(*pieces, w_pad, dh)


def _grad_w_matmul(xt, blocks, *, tk, name):
    m, s = xt.shape
    tn = 512
    nb = len(blocks)

    def kern(a_ref, *refs):
        b_refs, o_ref = refs[:nb], refs[nb]

        @pl.when(pl.program_id(0) == 0)
        def _():
            o_ref[...] = jnp.zeros_like(o_ref)
        a = a_ref[...]
        for blk in range(nb):
            o_ref[:, blk * tn:(blk + 1) * tn] += _dot(a, b_refs[blk][...])

    return pl.pallas_call(
        kern, name=name,
        grid=(s // tk,),
        in_specs=[pl.BlockSpec((m, tk), lambda k: (0, k))]
        + [pl.BlockSpec((tk, tn), functools.partial(lambda k, col: (k, col), col=col)) for _, col in blocks],
        out_specs=_resident((m, nb * tn), lambda k: (0, 0)),
        out_shape=jax.ShapeDtypeStruct((m, nb * tn), F32),
        compiler_params=_cparams(("arbitrary",)),
    )(xt, *[arr for arr, _ in blocks])


def _tri(n, lower):
    r = lax.broadcasted_iota(jnp.int32, (n, n), 0)
    c = lax.broadcasted_iota(jnp.int32, (n, n), 1)
    keep = (c <= r) if lower else (c >= r)
    return jnp.where(keep, 1.0, 0.0).astype(BF16)


def _exact_dot(mat_bf16, x_f32, left):
    out = None
    for piece in _split3(x_f32):
        t = _dot(mat_bf16, piece) if left else _dot(piece, mat_bf16)
        out = t if out is None else out + t
    return out


def _log_sigmoid(z):
    return jnp.minimum(z, 0.0) - jnp.log(1.0 + jnp.exp(-jnp.abs(z)))


def _cum_fwd(ffp, bfp):
    s = ffp.shape[0]
    t = min(1024, s)

    def kern(ff_ref, b_ref, cum_ref, carry_ref):
        @pl.when(pl.program_id(0) == 0)
        def _():
            carry_ref[...] = jnp.zeros_like(carry_ref)
        lane = lax.broadcasted_iota(jnp.int32, (1, LANES), 1)
        lf = _log_sigmoid(ff_ref[...] + b_ref[...])
        lf = jnp.where(lane < FOX_HEADS, lf, 0.0)
        cum = _exact_dot(_tri(t, True), lf, True) + carry_ref[0:1, :]
        cum_ref[...] = cum
        carry_ref[...] = jnp.broadcast_to(cum[t - 1:t, :], carry_ref.shape)

    return pl.pallas_call(
        kern, name="cum_fwd",
        grid=(s // t,),
        in_specs=[pl.BlockSpec((t, LANES), lambda i: (i, 0)),
                  pl.BlockSpec((1, LANES), lambda i: (0, 0))],
        out_specs=pl.BlockSpec((t, LANES), lambda i: (i, 0)),
        out_shape=jax.ShapeDtypeStruct((s, LANES), F32),
        scratch_shapes=[pltpu.VMEM((8, LANES), F32)],
        compiler_params=_cparams(("arbitrary",)),
    )(ffp, bfp)


def _cum_bwd(dcum_k, dcum_q, ffp, bfp):
    s = dcum_k.shape[0]
    t = min(1024, s)
    nb = s // t

    def kern(dck_ref, dcq_ref, ff_ref, b_ref, dff_ref, gb_ref, carry_ref):
        @pl.when(pl.program_id(0) == 0)
        def _():
            carry_ref[...] = jnp.zeros_like(carry_ref)
            gb_ref[...] = jnp.zeros_like(gb_ref)
        lane = lax.broadcasted_iota(jnp.int32, (1, LANES), 1)
        dlf = _exact_dot(_tri(t, False), dck_ref[...] + dcq_ref[...], True) + carry_ref[0:1, :]
        carry_ref[...] = jnp.broadcast_to(dlf[0:1, :], carry_ref.shape)
        z = ff_ref[...] + b_ref[...]
        dff = jnp.where(lane < FOX_HEADS, dlf / (1.0 + jnp.exp(z)), 0.0)
        gb_ref[...] += jnp.broadcast_to(jnp.sum(dff, axis=0, keepdims=True), gb_ref.shape)
        dff_ref[...] = jnp.concatenate([dff, jnp.zeros_like(dff)], axis=1).astype(BF16)

    return pl.pallas_call(
        kern, name="cum_bwd",
        grid=(nb,),
        in_specs=[pl.BlockSpec((t, LANES), lambda i: (nb - 1 - i, 0)),
                  pl.BlockSpec((t, LANES), lambda i: (nb - 1 - i, 0)),
                  pl.BlockSpec((t, LANES), lambda i: (nb - 1 - i, 0)),
                  pl.BlockSpec((1, LANES), lambda i: (0, 0))],
        out_specs=[pl.BlockSpec((t, N_C), lambda i: (nb - 1 - i, 0)),
                   pl.BlockSpec((8, LANES), lambda i: (0, 0))],
        out_shape=[jax.ShapeDtypeStruct((s, N_C), BF16),
                   jax.ShapeDtypeStruct((8, LANES), F32)],
        scratch_shapes=[pltpu.VMEM((8, LANES), F32)],
        compiler_params=_cparams(("arbitrary",)),
    )(dcum_k, dcum_q, ffp, bfp)


def _resident(shape, index_map):
    return pl.BlockSpec(shape, index_map, pipeline_mode=pl.Buffered(1))


def _fox_fwd(qkv, vt, cum_t3, cum):
    s = qkv.shape[0]
    tk = tq = FOX_REF
    nq = s // tq
    nh = FOX_HEADS
    diag_tiles = tq // tk

    def kern(q_ref, k_ref, vt_ref, ct_ref, c_ref, o_ref, lse_ref, m_ref, acc_ref, u_ref):
        i = pl.program_id(0)
        lane = lax.broadcasted_iota(jnp.int32, (1, LANES), 1)
        krow = lax.broadcasted_iota(jnp.int32, (tk, tq), 0)
        qcol = lax.broadcasted_iota(jnp.int32, (tk, tq), 1)
        q0 = pl.multiple_of(i * tq, tq)
        qts, crefs = [], []
        for h in range(nh):
            p, a = divmod(h, 2)
            q2 = q_ref[:, p * LANES:(p + 1) * LANES] * jnp.asarray(QK_SCALE, BF16)
            sel = (lane < HEAD_DIM) if a == 0 else (lane >= HEAD_DIM)
            qts.append(jnp.where(sel, q2, jnp.zeros_like(q2)).astype(F32).T.astype(BF16))
            crefs.append(ct_ref[h, :, pl.ds(q0, LANES)][:, 0:1])
        m_ref[...] = jnp.full(m_ref.shape, NEG, F32)
        acc_ref[...] = jnp.zeros_like(acc_ref)
        ones = jnp.ones((SUM_ROWS, tk), BF16)

        def tile(j, diag):
            k0 = pl.multiple_of(j * tk, tk)
            cb = c_ref[pl.ds(k0, tk), :]
            sts = [_dot(k_ref[pl.ds(k0, tk), (h // 2) * LANES:(h // 2 + 1) * LANES], qts[h]) for h in range(nh)]
            tile_max = []
            for h in range(nh):
                u = sts[h] - (cb[:, h:h + 1] - crefs[h])
                if diag is not None:
                    u = jnp.where(krow + diag * tk <= qcol, u, NEG)
                u_ref[h] = u
                tile_max.append(jnp.max(u, axis=0, keepdims=True))
            pts, scales = [], []
            for h in range(nh):
                m_old = m_ref[h]
                m_new = jnp.maximum(m_old, tile_max[h])
                scales.append(jnp.exp(m_old - m_new))
                pts.append(jnp.exp(u_ref[h] - m_new).astype(BF16))
                m_ref[h] = m_new
            for h in range(nh):
                vth = jnp.concatenate([vt_ref[h * HEAD_DIM:(h + 1) * HEAD_DIM, pl.ds(k0, tk)], ones], axis=0)
                acc_ref[h] = scales[h] * acc_ref[h] + _dot(vth, pts[h])

        def body(j, c):
            tile(j, None)
            return c
        lax.fori_loop(0, i * diag_tiles, body, 0)
        for d in range(diag_tiles):
            tile(i * diag_tiles + d, d)

        ls = [acc_ref[h][HEAD_DIM:HEAD_DIM + 1] for h in range(nh)]
        for p in range(nh // 2):
            ot = jnp.concatenate([acc_ref[2 * p + a][:HEAD_DIM] * (1.0 / ls[2 * p + a]) for a in range(2)], axis=0)
            o_ref[:, p * LANES:(p + 1) * LANES] = ot.T
        for h in range(nh):
            lse_ref[h, :, pl.ds(q0, tq)] = m_ref[h] + jnp.log(ls[h])

    return pl.pallas_call(
        kern, name="fox_fwd",
        grid=(nq,),
        in_specs=[pl.BlockSpec((tq, FOX_W), lambda i: (i, 0)),
                  _resident((s, FOX_W), lambda i: (0, COL_FK // FOX_W)),
                  _resident((FOX_W, s), lambda i: (0, 0)),
                  _resident((nh, 1, s), lambda i: (0, 0, 0)),
                  _resident((s, LANES), lambda i: (0, 0))],
        out_specs=[pl.BlockSpec((tq, FOX_W), lambda i: (i, 0)),
                   pl.BlockSpec((nh, 1, s), lambda i: (0, 0, 0))],
        out_shape=[jax.ShapeDtypeStruct((s, FOX_W), F32),
                   jax.ShapeDtypeStruct((nh, 1, s), F32)],
        scratch_shapes=[pltpu.VMEM((nh, 1, tq), F32),
                        pltpu.VMEM((nh, HEAD_DIM + SUM_ROWS, tq), F32),
                        pltpu.VMEM((nh, tk, tq), F32)],
        compiler_params=_cparams(("arbitrary",)),
    )(qkv, qkv, vt, cum_t3, cum)


def _fox_bwd(qkv, do_bf, cum_t3, cum, lse_t3, delta_t3):
    s = qkv.shape[0]
    t = min(FOX_T, s)
    nq = s // t
    nh = FOX_HEADS
    npair = nh // 2

    def kern(q_ref, do_ref, k_ref, v_ref, ct_ref, c_ref, lse_ref, dl_ref,
             dq_ref, dk_ref, dv_ref, dc_ref, dcq_ref, dqt_ref, accv_ref, acck_ref, accd_ref):
        kj = pl.program_id(0)
        lane = lax.broadcasted_iota(jnp.int32, (1, LANES), 1)
        krow = lax.broadcasted_iota(jnp.int32, (t, t), 0)
        qcol = lax.broadcasted_iota(jnp.int32, (t, t), 1)
        causal = krow <= qcol
        sels = [lane < HEAD_DIM, lane >= HEAD_DIM]

        @pl.when(kj == 0)
        def _():
            dqt_ref[...] = jnp.zeros_like(dqt_ref)
            dcq_ref[...] = jnp.zeros_like(dcq_ref)

        cb = c_ref[...]
        k2s, v2s, kts = [], [], []
        for p in range(npair):
            k2 = k_ref[:, p * LANES:(p + 1) * LANES]
            k2s.append(k2)
            v2s.append(v_ref[:, p * LANES:(p + 1) * LANES])
            kt = k2.astype(F32).T * QK_SCALE
            kts.append(kt[:HEAD_DIM].astype(BF16))
            kts.append(kt[HEAD_DIM:].astype(BF16))
        css = [cb[:, h:h + 1] for h in range(nh)]

        def tile(i, masked):
            q0 = pl.multiple_of(i * t, t)
            r0 = pl.multiple_of((i // (FOX_REF // t)) * FOX_REF, FOX_REF)
            sts, dpts, qms, doms = [], [], [], []
            for h in range(nh):
                p, a = divmod(h, 2)
                qi = q_ref[pl.ds(q0, t), p * LANES:(p + 1) * LANES] * jnp.asarray(QK_SCALE, BF16)
                doi = do_ref[pl.ds(q0, t), p * LANES:(p + 1) * LANES]
                qm = jnp.where(sels[a], qi, jnp.zeros_like(qi))
                dom = jnp.where(sels[a], doi, jnp.zeros_like(doi))
                qms.append(qm)
                doms.append(dom)
                sts.append(_dot_nt(k2s[p], qm))
                dpts.append(_dot_nt(v2s[p], dom))
            pts, dsts = [], []
            for h in range(nh):
                cref = ct_ref[h, :, pl.ds(r0, LANES)][:, 0:1]
                pt = jnp.exp(sts[h] - (css[h] - cref) - lse_ref[h, :, pl.ds(q0, t)])
                if masked:
                    pt = jnp.where(causal, pt, 0.0)
                ds32 = pt * (dpts[h] - dl_ref[h, :, pl.ds(q0, t)])
                part = ds32[:, 0:LANES]
                for c in range(1, t // LANES):
                    part = part + ds32[:, c * LANES:(c + 1) * LANES]
                accd_ref[h] = part if masked else accd_ref[h] + part
                dcq_ref[h, :, pl.ds(q0, t)] += jnp.sum(ds32, axis=0, keepdims=True)
                pts.append(pt.astype(BF16))
                dsts.append(ds32.astype(BF16))
            for p in range(npair):
                ha, hb = 2 * p, 2 * p + 1
                dv_p = _dot(pts[ha], doms[ha]) + _dot(pts[hb], doms[hb])
                dk_p = _dot(dsts[ha], qms[ha]) + _dot(dsts[hb], qms[hb])
                accv_ref[p] = dv_p if masked else accv_ref[p] + dv_p
                acck_ref[p] = dk_p if masked else acck_ref[p] + dk_p
            for h in range(nh):
                dqt_ref[h * HEAD_DIM:(h + 1) * HEAD_DIM, pl.ds(q0, t)] += _dot(kts[h], dsts[h])

        tile(kj, True)

        def body(i, c):
            tile(i, False)
            return c
        lax.fori_loop(kj + 1, nq, body, 0)

        dc = jnp.zeros((t, LANES), F32)
        for h in range(nh):
            dc = jnp.where(lane == h, -jnp.sum(accd_ref[h], axis=1, keepdims=True), dc)
        dc_ref[...] = dc
        for p in range(npair):
            dv_ref[:, p * LANES:(p + 1) * LANES] = accv_ref[p].astype(BF16)
            dk_ref[:, p * LANES:(p + 1) * LANES] = acck_ref[p].astype(BF16)

        @pl.when(kj == nq - 1)
        def _():
            for c0 in range(0, s, t):
                dq_ref[c0:c0 + t, :] = dqt_ref[:, c0:c0 + t].T.astype(BF16)

    whole = lambda kj: (0, 0, 0)
    return pl.pallas_call(
        kern, name="fox_bwd",
        grid=(nq,),
        in_specs=[_resident((s, FOX_W), lambda kj: (0, 0)),
                  _resident((s, FOX_W), lambda kj: (0, 0)),
                  pl.BlockSpec((t, FOX_W), lambda kj: (kj, COL_FK // FOX_W)),
                  pl.BlockSpec((t, FOX_W), lambda kj: (kj, COL_FV // FOX_W)),
                  _resident((nh, 1, s), whole),
                  pl.BlockSpec((t, LANES), lambda kj: (kj, 0)),
                  _resident((nh, 1, s), whole),
                  _resident((nh, 1, s), whole)],
        out_specs=[_resident((s, FOX_W), lambda kj: (0, 0)),
                   pl.BlockSpec((t, FOX_W), lambda kj: (kj, 0)),
                   pl.BlockSpec((t, FOX_W), lambda kj: (kj, 0)),
                   pl.BlockSpec((t, LANES), lambda kj: (kj, 0)),
                   _resident((nh, 1, s), whole)],
        out_shape=[jax.ShapeDtypeStruct((s, FOX_W), BF16),
                   jax.ShapeDtypeStruct((s, FOX_W), BF16),
                   jax.ShapeDtypeStruct((s, FOX_W), BF16),
                   jax.ShapeDtypeStruct((s, LANES), F32),
                   jax.ShapeDtypeStruct((nh, 1, s), F32)],
        scratch_shapes=[pltpu.VMEM((FOX_W, s), F32),
                        pltpu.VMEM((npair, t, LANES), F32),
                        pltpu.VMEM((npair, t, LANES), F32),
                        pltpu.VMEM((nh, t, LANES), F32)],
        compiler_params=_cparams(("arbitrary",)),
    )(qkv, do_bf, qkv, qkv, cum_t3, cum, lse_t3, delta_t3)


def _bucket_table():
    qi = np.arange(BLOCK)[:, None]
    kj = np.arange(2 * BLOCK)[None, :]
    rel = np.maximum(qi + BLOCK - kj, 0).astype(np.int32)
    max_exact = NUM_BUCKETS // 2
    relf = np.maximum(rel, 1).astype(np.float32)
    large = max_exact + (np.log(relf / np.float32(max_exact)) / np.float32(math.log(MAX_DISTANCE / max_exact))
                         * np.float32(NUM_BUCKETS - max_exact)).astype(np.int32)
    large = np.minimum(large, NUM_BUCKETS - 1)
    return np.where(rel < max_exact, rel, large).astype(np.int32)


SWA_LANES = SWA_GROUP * BLOCK


def _swa_bias(rel_bias, bucket_t):
    def kern(rb_ref, bk_ref, o_ref):
        bk = bk_ref[...]
        kj = lax.broadcasted_iota(jnp.int32, (2 * BLOCK, BLOCK), 0)
        qi = lax.broadcasted_iota(jnp.int32, (2 * BLOCK, BLOCK), 1)
        rel = qi + BLOCK - kj
        band = (rel >= 0) & (rel < BLOCK)
        masks = [band & (kj >= BLOCK), band]
        for h in range(SWA_HEADS):
            g, hh = divmod(h, SWA_GROUP)
            acc = jnp.zeros((2 * BLOCK, BLOCK), F32)
            for b in range(NUM_BUCKETS):
                acc = jnp.where(bk == b, rb_ref[b, h], acc)
            for first in range(2):
                o_ref[first, g, :, hh * BLOCK:(hh + 1) * BLOCK] = jnp.where(masks[first], acc, NEG)

    return pl.pallas_call(
        kern, name="swa_bias",
        in_specs=[pl.BlockSpec(memory_space=pltpu.SMEM),
                  pl.BlockSpec(memory_space=pltpu.VMEM)],
        out_specs=pl.BlockSpec(memory_space=pltpu.VMEM),
        out_shape=jax.ShapeDtypeStruct((2, SWA_KV_HEADS, 2 * BLOCK, SWA_LANES), F32),
        compiler_params=_cparams(),
    )(rel_bias, bucket_t)


SWA_STEP = 4


def _swa_keys(prev_ref, cur_ref):
    return jnp.concatenate([prev_ref[...], cur_ref[...]], axis=0)


def _swa_queries(x_ref, scale):
    x = x_ref[...]
    if scale:
        x = x * jnp.asarray(QK_SCALE, BF16)
    xt = x.astype(F32).T.astype(BF16)
    return [_group_rows(xt[:, b * BLOCK:(b + 1) * BLOCK]) for b in range(SWA_STEP)]


def _group_rows(xt):
    zeros = jnp.zeros((HEAD_DIM, SWA_LANES), BF16)
    out = []
    for g in range(SWA_KV_HEADS):
        heads = [xt[(SWA_GROUP * g + hh) * HEAD_DIM:(SWA_GROUP * g + hh + 1) * HEAD_DIM, :] for hh in range(SWA_GROUP)]
        rows = jnp.concatenate(heads, axis=1)
        padded = jnp.concatenate([rows, zeros] if g == 0 else [zeros, rows], axis=0)
        out.append((rows, padded))
    return out


def _pairs_to_rows(cols_t):
    out = []
    for p in range(SWA_HEADS // 2):
        g, hh = divmod(2 * p, SWA_GROUP)
        pair = jnp.concatenate([cols_t[g][:, hh * BLOCK:(hh + 1) * BLOCK],
                                cols_t[g][:, (hh + 1) * BLOCK:(hh + 2) * BLOCK]], axis=0)
        out.append(pair.T)
    return jnp.concatenate(out, axis=1)


def _swa_fwd(qkv, bias_t, sink_rows):
    s = qkv.shape[0]
    nb = s // BLOCK
    rows = SWA_STEP * BLOCK
    units = [(b, g) for b in range(SWA_STEP) for g in range(SWA_KV_HEADS)]

    def kern(q_ref, kp_ref, kc_ref, vp_ref, vc_ref, bias_ref, sink_ref, o_ref, lse_ref):
        n = pl.program_id(0)
        tables = [jnp.minimum(n, 1)] + [1] * (SWA_STEP - 1)
        k3 = _swa_keys(kp_ref, kc_ref)
        vt3 = _swa_keys(vp_ref, vc_ref).astype(F32).T.astype(BF16)
        qts = _swa_queries(q_ref, True)
        us = [_dot(k3[b * BLOCK:(b + 2) * BLOCK], qts[b][g][1]) + bias_ref[tables[b], g] for b, g in units]
        outs = []
        for (b, g), u in zip(units, us):
            sk = sink_ref[g]
            m = jnp.maximum(jnp.max(u, axis=0, keepdims=True), sk)
            p = jnp.exp(u - m)
            l = jnp.sum(p, axis=0, keepdims=True) + jnp.exp(sk - m)
            lse_ref[b, g] = m + jnp.log(l)
            vt = vt3[g * HEAD_DIM:(g + 1) * HEAD_DIM, b * BLOCK:(b + 2) * BLOCK]
            outs.append(_dot(vt, (p * (1.0 / l)).astype(BF16)))
        for b in range(SWA_STEP):
            o_ref[b * BLOCK:(b + 1) * BLOCK, :] = _pairs_to_rows(outs[b * SWA_KV_HEADS:(b + 1) * SWA_KV_HEADS])

    cq, ck, cv = COL_SQ // SWA_W, COL_SK // LANES, COL_SV // LANES
    prev = lambda n: jnp.maximum(SWA_STEP * n - 1, 0)
    return pl.pallas_call(
        kern, name="swa_fwd",
        grid=(nb // SWA_STEP,),
        in_specs=[pl.BlockSpec((rows, SWA_W), lambda n: (n, cq)),
                  pl.BlockSpec((BLOCK, LANES), lambda n: (prev(n), ck)),
                  pl.BlockSpec((rows, LANES), lambda n: (n, ck)),
                  pl.BlockSpec((BLOCK, LANES), lambda n: (prev(n), cv)),
                  pl.BlockSpec((rows, LANES), lambda n: (n, cv)),
                  _resident((2, SWA_KV_HEADS, 2 * BLOCK, SWA_LANES), lambda n: (0, 0, 0, 0)),
                  _resident((SWA_KV_HEADS, 1, SWA_LANES), lambda n: (0, 0, 0))],
        out_specs=[pl.BlockSpec((rows, SWA_W), lambda n: (n, 0)),
                   pl.BlockSpec((SWA_STEP, SWA_KV_HEADS, 1, SWA_LANES), lambda n: (n, 0, 0, 0))],
        out_shape=[jax.ShapeDtypeStruct((s, SWA_W), F32),
                   jax.ShapeDtypeStruct((nb, SWA_KV_HEADS, 1, SWA_LANES), F32)],
        compiler_params=_cparams(("parallel",)),
    )(qkv, qkv, qkv, qkv, qkv, bias_t, sink_rows)


def _swa_bwd(qkv, do_bf, delta_rows, lse, bias_t, sink_rows, bucket_t):
    s = qkv.shape[0]
    nb = s // BLOCK
    steps = nb // SWA_STEP
    rows = SWA_STEP * BLOCK
    units = [(b, g) for b in range(SWA_STEP) for g in range(SWA_KV_HEADS)]

    def kern(q_ref, kp_ref, kc_ref, vp_ref, vc_ref, do_ref, dl_ref, lse_ref, bias_ref, sink_ref, bk_ref,
             dq_ref, dk_ref, dv_ref, grb_ref, gsk_ref, dbias_ref, ck_ref, cv_ref, sk_ref):
        n = pl.program_id(0)

        @pl.when(n == 0)
        def _():
            dbias_ref[...] = jnp.zeros_like(dbias_ref)
            ck_ref[...] = jnp.zeros_like(ck_ref)
            cv_ref[...] = jnp.zeros_like(cv_ref)
            sk_ref[...] = jnp.zeros_like(sk_ref)

        @pl.when(n < steps)
        def _():
            tables = [jnp.minimum(n, 1)] + [1] * (SWA_STEP - 1)
            k3 = _swa_keys(kp_ref, kc_ref)
            v3 = _swa_keys(vp_ref, vc_ref)
            kt3 = (k3.astype(F32).T * QK_SCALE).astype(BF16)
            qts = _swa_queries(q_ref, True)
            dots = _swa_queries(do_ref, False)
            sts = [_dot(k3[b * BLOCK:(b + 2) * BLOCK], qts[b][g][1]) for b, g in units]
            dps = [_dot(v3[b * BLOCK:(b + 2) * BLOCK], dots[b][g][1]) for b, g in units]
            ps, dss = [], []
            for i, (b, g) in enumerate(units):
                lse_g = lse_ref[b, g]
                dlt = dl_ref[b, g]
                p = jnp.exp(sts[i] + bias_ref[tables[b], g] - lse_g)
                ds = p * (dps[i] - dlt)
                dbias_ref[g] += ds
                sk_ref[g] += -jnp.exp(sink_ref[g] - lse_g) * dlt
                ps.append(p.astype(BF16))
                dss.append(ds.astype(BF16))
            dk2, dv2 = [], []
            for b in range(SWA_STEP):
                at = lambda g: b * SWA_KV_HEADS + g
                groups = range(SWA_KV_HEADS)
                dv2.append(jnp.concatenate([_dot_nt(dots[b][g][0], ps[at(g)]) for g in groups], axis=0).T)
                dk2.append(jnp.concatenate([_dot_nt(qts[b][g][0], dss[at(g)]) for g in groups], axis=0).T)
                dqts = [_dot(kt3[g * HEAD_DIM:(g + 1) * HEAD_DIM, b * BLOCK:(b + 2) * BLOCK], dss[at(g)]) for g in groups]
                dq_ref[b * BLOCK:(b + 1) * BLOCK, :] = _pairs_to_rows(dqts).astype(BF16)
            last = (SWA_STEP - 1) * BLOCK
            for acc_ref, out_ref, parts in ((ck_ref, dk_ref, dk2), (cv_ref, dv_ref, dv2)):
                done = acc_ref[last:] + parts[0][:BLOCK]
                out_ref[...] = jnp.concatenate([acc_ref[:last], done], axis=0).astype(BF16)
                for b in range(SWA_STEP - 1):
                    acc_ref[b * BLOCK:(b + 1) * BLOCK] = parts[b][BLOCK:] + parts[b + 1][:BLOCK]
                acc_ref[last:] = parts[SWA_STEP - 1][BLOCK:]

        @pl.when(n == steps)
        def _():
            dk_ref[...] = ck_ref[...].astype(BF16)
            dv_ref[...] = cv_ref[...].astype(BF16)
            bk = bk_ref[...]
            lane = lax.broadcasted_iota(jnp.int32, (8, LANES), 1)
            rowi = lax.broadcasted_iota(jnp.int32, (NUM_BUCKETS, LANES), 0)
            lanei = lax.broadcasted_iota(jnp.int32, (NUM_BUCKETS, LANES), 1)
            out = jnp.zeros((NUM_BUCKETS, LANES), F32)
            gsk = jnp.zeros((8, LANES), F32)
            for h in range(SWA_HEADS):
                g, hh = divmod(h, SWA_GROUP)
                cols = slice(hh * BLOCK, (hh + 1) * BLOCK)
                gsk = jnp.where(lane == h, jnp.sum(sk_ref[g][:, cols]), gsk)
                db = dbias_ref[g][:, cols]
                for b in range(NUM_BUCKETS):
                    val = jnp.sum(jnp.where(bk == b, db, 0.0))
                    out = jnp.where((rowi == b) & (lanei == h), val, out)
            grb_ref[...] = out
            gsk_ref[...] = gsk

    cq, ck, cv = COL_SQ // SWA_W, COL_SK // LANES, COL_SV // LANES
    cur = lambda n: jnp.minimum(n, steps - 1)
    prev = lambda n: jnp.maximum(SWA_STEP * cur(n) - 1, 0)
    kout = lambda n: jnp.maximum(n - 1, 0)
    stat = pl.BlockSpec((SWA_STEP, SWA_KV_HEADS, 1, SWA_LANES), lambda n: (cur(n), 0, 0, 0))
    return pl.pallas_call(
        kern, name="swa_bwd",
        grid=(steps + 1,),
        in_specs=[pl.BlockSpec((rows, SWA_W), lambda n: (cur(n), cq)),
                  pl.BlockSpec((BLOCK, LANES), lambda n: (prev(n), ck)),
                  pl.BlockSpec((rows, LANES), lambda n: (cur(n), ck)),
                  pl.BlockSpec((BLOCK, LANES), lambda n: (prev(n), cv)),
                  pl.BlockSpec((rows, LANES), lambda n: (cur(n), cv)),
                  pl.BlockSpec((rows, SWA_W), lambda n: (cur(n), 1)),
                  stat, stat,
                  _resident((2, SWA_KV_HEADS, 2 * BLOCK, SWA_LANES), lambda n: (0, 0, 0, 0)),
                  _resident((SWA_KV_HEADS, 1, SWA_LANES), lambda n: (0, 0, 0)),
                  _resident((2 * BLOCK, BLOCK), lambda n: (0, 0))],
        out_specs=[pl.BlockSpec((rows, SWA_W), lambda n: (cur(n), 0)),
                   pl.BlockSpec((rows, LANES), lambda n: (kout(n), 0)),
                   pl.BlockSpec((rows, LANES), lambda n: (kout(n), 0)),
                   pl.BlockSpec((NUM_BUCKETS, LANES), lambda n: (0, 0)),
                   pl.BlockSpec((8, LANES), lambda n: (0, 0))],
        out_shape=[jax.ShapeDtypeStruct((s, SWA_W), BF16),
                   jax.ShapeDtypeStruct((s, LANES), BF16),
                   jax.ShapeDtypeStruct((s, LANES), BF16),
                   jax.ShapeDtypeStruct((NUM_BUCKETS, LANES), F32),
                   jax.ShapeDtypeStruct((8, LANES), F32)],
        scratch_shapes=[pltpu.VMEM((SWA_KV_HEADS, 2 * BLOCK, SWA_LANES), F32),
                        pltpu.VMEM((rows, LANES), F32),
                        pltpu.VMEM((rows, LANES), F32),
                        pltpu.VMEM((SWA_KV_HEADS, 1, SWA_LANES), F32)],
        compiler_params=_cparams(("arbitrary",)),
    )(qkv, qkv, qkv, qkv, qkv, do_bf, delta_rows, lse, bias_t, sink_rows, bucket_t)


def _post(x, target, o_fox, o_swa, z, w_o, ln_g, ln_b):
    s = x.shape[0]
    tm = min(256, s)
    nt = s // tm

    def kern(x_ref, t_ref, of_ref, os_ref, z_ref, w_ref, g_ref, b_ref,
             loss_ref, dh_ref, gwo_ref, do_ref, dz_ref, dl_ref, gg_ref, gb_ref, lacc_ref):
        step = pl.program_id(0)

        @pl.when(step == 0)
        def _():
            lacc_ref[...] = jnp.zeros_like(lacc_ref)
            gg_ref[...] = jnp.zeros_like(gg_ref)
            gwo_ref[...] = jnp.zeros_like(gwo_ref)
            gb_ref[...] = jnp.zeros_like(gb_ref)

        o = jnp.concatenate([of_ref[...], os_ref[...]], axis=1)
        zz = z_ref[...]
        sig = 1.0 / (1.0 + jnp.exp(-zz))
        silu = zz * sig
        mixed32 = o * silu
        mixed = mixed32.astype(BF16)
        w = w_ref[...]
        h = ALPHA * x_ref[...] + _dot(mixed, w)
        mu = jnp.mean(h, axis=1, keepdims=True)
        hc = h - mu
        var = jnp.mean(hc * hc, axis=1, keepdims=True)
        rstd = lax.rsqrt(var + LN_EPS)
        xhat = hc * rstd
        g = g_ref[...]
        err = xhat * g + b_ref[...] - t_ref[...]
        lacc_ref[...] += jnp.broadcast_to(jnp.sum(err * err, axis=0, keepdims=True), lacc_ref.shape)
        dout = err * (1.0 / D_MODEL)
        gg_ref[...] += jnp.broadcast_to(jnp.sum(dout * xhat, axis=0, keepdims=True), gg_ref.shape)
        gb_ref[...] += jnp.broadcast_to(jnp.sum(dout, axis=0, keepdims=True), gb_ref.shape)
        dxh = dout * g
        m1 = jnp.mean(dxh, axis=1, keepdims=True)
        m2 = jnp.mean(dxh * xhat, axis=1, keepdims=True)
        dh = rstd * (dxh - m1 - xhat * m2)
        dh_ref[...] = dh
        dy = dh.astype(BF16)
        gwo_ref[...] += _dot(mixed32.T.astype(BF16), dy)
        dmix = _dot_nt(dy, w)
        do = dmix * silu
        do_ref[...] = do.astype(BF16)
        dz_ref[...] = (dmix * o * (sig * (1.0 + zz * (1.0 - sig)))).astype(BF16)
        r = lax.broadcasted_iota(jnp.int32, (D_MODEL, LANES), 0) // HEAD_DIM
        c = lax.broadcasted_iota(jnp.int32, (D_MODEL, LANES), 1)
        pick = jnp.where(r == c, 1.0, 0.0).astype(BF16)
        dl_ref[...] = _exact_dot(pick, do * o, False)

        @pl.when(step == nt - 1)
        def _():
            tot = jnp.sum(lacc_ref[0:1, :]) * (0.5 / D_MODEL)
            loss_ref[...] = jnp.broadcast_to(tot, loss_ref.shape)

    row = lambda i: (i, 0)
    fixed = lambda i: (0, 0)
    wide = pl.BlockSpec((tm, D_MODEL), row)
    half = pl.BlockSpec((tm, FOX_W), row)
    return pl.pallas_call(
        kern, name="post",
        grid=(nt,),
        in_specs=[wide, wide, half, half, wide,
                  pl.BlockSpec((D_MODEL, D_MODEL), fixed),
                  pl.BlockSpec((1, D_MODEL), fixed),
                  pl.BlockSpec((1, D_MODEL), fixed)],
        out_specs=[pl.BlockSpec((8, LANES), fixed), wide,
                   _resident((D_MODEL, D_MODEL), fixed), wide, wide,
                   pl.BlockSpec((tm, LANES), row),
                   pl.BlockSpec((8, D_MODEL), fixed), pl.BlockSpec((8, D_MODEL), fixed)],
        out_shape=[jax.ShapeDtypeStruct((8, LANES), F32),
                   jax.ShapeDtypeStruct((s, D_MODEL), F32),
                   jax.ShapeDtypeStruct((D_MODEL, D_MODEL), F32),
                   jax.ShapeDtypeStruct((s, D_MODEL), BF16),
                   jax.ShapeDtypeStruct((s, D_MODEL), BF16),
                   jax.ShapeDtypeStruct((s, LANES), F32),
                   jax.ShapeDtypeStruct((8, D_MODEL), F32),
                   jax.ShapeDtypeStruct((8, D_MODEL), F32)],
        scratch_shapes=[pltpu.VMEM((8, D_MODEL), F32)],
        compiler_params=_cparams(("arbitrary",)),
    )(x, target, o_fox, o_swa, z, w_o, ln_g, ln_b)


def _adamw_math(w, g, m, v):
    m = ADAM_B1 * m + (1.0 - ADAM_B1) * g
    v = ADAM_B2 * v + (1.0 - ADAM_B2) * (g * g)
    m_hat = m / (1.0 - ADAM_B1 ** ADAM_STEP)
    v_hat = v / (1.0 - ADAM_B2 ** ADAM_STEP)
    delta = -ADAM_LR * (m_hat / (jnp.sqrt(v_hat) + ADAM_EPS) + ADAM_WD * w)
    return delta, m, v


def _adamw(w, g, m, v, *, name):
    r, c = w.shape
    tr = min(256, r)

    def kern(w_ref, g_ref, m_ref, v_ref, d_ref, mo_ref, vo_ref):
        d, mn, vn = _adamw_math(w_ref[...], g_ref[...], m_ref[...], v_ref[...])
        d_ref[...] = d
        mo_ref[...] = mn
        vo_ref[...] = vn

    blk = pl.BlockSpec((tr, c), lambda i: (i, 0))
    sds = jax.ShapeDtypeStruct((r, c), F32)
    return pl.pallas_call(
        kern, name=name,
        grid=(r // tr,),
        in_specs=[blk, blk, blk, blk],
        out_specs=[blk, blk, blk],
        out_shape=[sds, sds, sds],
        compiler_params=_cparams(("parallel",)),
    )(w, g, m, v)


def _adamw_cols(w, g, m, v, *, name):
    c, _, r = w.shape
    tc = 139
    assert c % tc == 0

    def kern(w_ref, g_ref, m_ref, v_ref, go_ref, d_ref, mo_ref, vo_ref):
        g = g_ref[...]
        d, mn, vn = _adamw_math(w_ref[...], g, m_ref[...], v_ref[...])
        go_ref[...] = g
        d_ref[...] = d
        mo_ref[...] = mn
        vo_ref[...] = vn

    blk = pl.BlockSpec((tc, 1, r), lambda i: (i, 0, 0))
    sds = jax.ShapeDtypeStruct((c, 1, r), F32)
    return pl.pallas_call(
        kern, name=name,
        grid=(c // tc,),
        in_specs=[blk, blk, blk, blk],
        out_specs=[blk, blk, blk, blk],
        out_shape=[sds, sds, sds, sds],
        compiler_params=_cparams(("parallel",)),
    )(w, g, m, v)


def _position():
    x, y, c = lax.axis_index("x"), lax.axis_index("y"), lax.axis_index("c")
    chips = [(1 - x, y), (x, 1 - y), (1 - x, 1 - y)]
    return x, y, c, chips


def _chip_index(cx, cy):
    return 2 * cx + cy


def _gather_weights(w_in_bf, w_o_bf):
    shards = (w_in_bf, w_o_bf)
    n_arr = len(shards)

    def kern(*refs):
        ins, outs = refs[:n_arr], refs[n_arr:2 * n_arr]
        send_sems, recv_sems, local_sems = refs[2 * n_arr:]
        x, y, c, chips = _position()
        me = _chip_index(x, y)
        sibling = (x, y, 1 - c)

        local = [pltpu.make_async_copy(ins[a], outs[a].at[me], local_sems.at[a]) for a in range(n_arr)]
        for cp in local:
            cp.start()

        def half(ref, a):
            rows = shards[a].shape[0] // 2
            return ref.at[pl.ds(c * rows, rows), :]

        def copy(a, k, src, slot, to):
            return pltpu.make_async_remote_copy(
                src_ref=src, dst_ref=half(outs[a].at[slot], a),
                send_sem=send_sems.at[a * 6 + k], recv_sem=recv_sems.at[a * 6 + k],
                device_id=to, device_id_type=MESH)

        first = [copy(a, j, half(ins[a], a), me, (*chip, c)) for a in range(n_arr) for j, chip in enumerate(chips)]
        for cp in first:
            cp.start()
        passed = []
        for a in range(n_arr):
            for j, chip in enumerate(chips):
                slot = _chip_index(*chip)
                copy(a, j, half(ins[a], a), slot, (*chip, c)).wait_recv()
                fwd = copy(a, 3 + j, half(outs[a].at[slot], a), slot, sibling)
                fwd.start()
                passed.append(fwd)
        for a in range(n_arr):
            for j, chip in enumerate(chips):
                slot = _chip_index(*chip)
                rows = shards[a].shape[0] // 2
                dst = outs[a].at[slot].at[pl.ds((1 - c) * rows, rows), :]
                pltpu.make_async_remote_copy(
                    src_ref=dst, dst_ref=dst, send_sem=send_sems.at[a * 6 + 3 + j],
                    recv_sem=recv_sems.at[a * 6 + 3 + j], device_id=sibling, device_id_type=MESH).wait_recv()
        for cp in first + passed:
            cp.wait_send()
        for cp in local:
            cp.wait()

    vmem = pl.BlockSpec(memory_space=pltpu.VMEM)
    return pl.pallas_call(
        kern, name="gather_weights",
        in_specs=[vmem] * n_arr,
        out_specs=[vmem] * n_arr,
        out_shape=[jax.ShapeDtypeStruct((N_CHIPS,) + w.shape, w.dtype) for w in shards],
        scratch_shapes=[pltpu.SemaphoreType.DMA((6 * n_arr,)),
                        pltpu.SemaphoreType.DMA((6 * n_arr,)),
                        pltpu.SemaphoreType.DMA((n_arr,))],
        compiler_params=_cparams(),
    )(*shards)


def _pair_reduce(grads):
    n_arr = len(grads)
    chunk = 128

    def kern(*refs):
        ins = refs[:n_arr]
        outs = refs[n_arr:2 * n_arr]
        gots = refs[2 * n_arr:3 * n_arr]
        send_sems, recv_sems = refs[3 * n_arr:]
        x, y, c, _ = _position()
        sibling = (x, y, 1 - c)
        copies = []
        for a in range(n_arr):
            rows = grads[a].shape[1] // 2
            copies.append(pltpu.make_async_remote_copy(
                src_ref=ins[a].at[:, pl.ds((1 - c) * rows, rows), :], dst_ref=gots[a],
                send_sem=send_sems.at[a], recv_sem=recv_sems.at[a], device_id=sibling, device_id_type=MESH))
        for cp in copies:
            cp.start()
        for a in range(n_arr):
            copies[a].wait()
            rows = grads[a].shape[1] // 2
            for j in range(N_CHIPS):
                for r0 in range(0, rows, chunk):
                    mine = ins[a][j, pl.ds(pl.multiple_of(c * rows + r0, chunk), chunk), :]
                    outs[a][j, r0:r0 + chunk, :] = (mine + gots[a][j, r0:r0 + chunk, :]).astype(BF16)

    vmem = pl.BlockSpec(memory_space=pltpu.VMEM)
    half = [(N_CHIPS, g.shape[1] // 2, g.shape[2]) for g in grads]
    return pl.pallas_call(
        kern, name="pair_reduce",
        in_specs=[vmem] * n_arr,
        out_specs=[vmem] * n_arr,
        out_shape=[jax.ShapeDtypeStruct(h, BF16) for h in half],
        scratch_shapes=[pltpu.VMEM(h, F32) for h in half]
        + [pltpu.SemaphoreType.DMA((n_arr,)), pltpu.SemaphoreType.DMA((n_arr,))],
        compiler_params=_cparams(),
    )(*grads)


def _chip_reduce(parts):
    n_arr = len(parts)
    chunk = 128

    def kern(*refs):
        ins = refs[:n_arr]
        outs = refs[n_arr:2 * n_arr]
        slabs = refs[2 * n_arr:3 * n_arr]
        send_sems, recv_sems, local_sems = refs[3 * n_arr:]
        x, y, c, chips = _position()
        me = _chip_index(x, y)
        local = [pltpu.make_async_copy(ins[a].at[me], slabs[a].at[me], local_sems.at[a]) for a in range(n_arr)]
        for cp in local:
            cp.start()
        sends = []
        for a in range(n_arr):
            for j, chip in enumerate(chips):
                sends.append(pltpu.make_async_remote_copy(
                    src_ref=ins[a].at[_chip_index(*chip)], dst_ref=slabs[a].at[me],
                    send_sem=send_sems.at[a * 3 + j], recv_sem=recv_sems.at[a * 3 + j],
                    device_id=(*chip, c), device_id_type=MESH))
        for cp in sends:
            cp.start()
        for a in range(n_arr):
            for j, chip in enumerate(chips):
                slot = slabs[a].at[_chip_index(*chip)]
                pltpu.make_async_remote_copy(
                    src_ref=slot, dst_ref=slot, send_sem=send_sems.at[a * 3 + j],
                    recv_sem=recv_sems.at[a * 3 + j], device_id=(*chip, c), device_id_type=MESH).wait_recv()
        for cp in sends:
            cp.wait_send()
        for cp in local:
            cp.wait()
        for a in range(n_arr):
            for r0 in range(0, parts[a].shape[1], chunk):
                f = lambda j: slabs[a][j, r0:r0 + chunk, :].astype(F32)
                outs[a][r0:r0 + chunk, :] = ((f(0) + f(1)) + f(2)) + f(3)

    vmem = pl.BlockSpec(memory_space=pltpu.VMEM)
    return pl.pallas_call(
        kern, name="chip_reduce",
        in_specs=[vmem] * n_arr,
        out_specs=[vmem] * n_arr,
        out_shape=[jax.ShapeDtypeStruct(p.shape[1:], F32) for p in parts],
        scratch_shapes=[pltpu.VMEM(p.shape, BF16) for p in parts]
        + [pltpu.SemaphoreType.DMA((3 * n_arr,)),
           pltpu.SemaphoreType.DMA((3 * n_arr,)),
           pltpu.SemaphoreType.DMA((n_arr,))],
        compiler_params=_cparams(),
    )(*parts)


def _join_halves(halves):
    n_arr = len(halves)

    def kern(*refs):
        ins = refs[:n_arr]
        outs = refs[n_arr:2 * n_arr]
        send_sems, recv_sems, local_sems = refs[2 * n_arr:]
        x, y, c, _ = _position()
        sibling = (x, y, 1 - c)
        local, remote = [], []
        for a in range(n_arr):
            rows = halves[a].shape[0]
            mine = outs[a].at[pl.ds(c * rows, rows), :]
            local.append(pltpu.make_async_copy(ins[a], mine, local_sems.at[a]))
            remote.append(pltpu.make_async_remote_copy(
                src_ref=ins[a], dst_ref=mine, send_sem=send_sems.at[a], recv_sem=recv_sems.at[a],
                device_id=sibling, device_id_type=MESH))
        for cp in local + remote:
            cp.start()
        for a in range(n_arr):
            rows = halves[a].shape[0]
            theirs = outs[a].at[pl.ds((1 - c) * rows, rows), :]
            pltpu.make_async_remote_copy(
                src_ref=theirs, dst_ref=theirs, send_sem=send_sems.at[a], recv_sem=recv_sems.at[a],
                device_id=sibling, device_id_type=MESH).wait_recv()
        for cp in remote:
            cp.wait_send()
        for cp in local:
            cp.wait()

    vmem = pl.BlockSpec(memory_space=pltpu.VMEM)
    return pl.pallas_call(
        kern, name="join_halves",
        in_specs=[vmem] * n_arr,
        out_specs=[vmem] * n_arr,
        out_shape=[jax.ShapeDtypeStruct((2 * h.shape[0], h.shape[1]), F32) for h in halves],
        scratch_shapes=[pltpu.SemaphoreType.DMA((n_arr,)),
                        pltpu.SemaphoreType.DMA((n_arr,)),
                        pltpu.SemaphoreType.DMA((n_arr,))],
        compiler_params=_cparams(),
    )(*halves)


def _small_allreduce_adamw(partials, params, moms, vels):
    chunks = D_MODEL // LANES
    row_rb, row_bf, row_sk, row_loss = 2 * chunks, 2 * chunks + NUM_BUCKETS, 2 * chunks + NUM_BUCKETS + 1, SMALL_ROWS - 6

    def kern(gbf_ref, grb_ref, gsk_ref, gg_ref, gb_ref, loss_ref, *refs):
        p_refs, m_refs, v_refs = refs[0:5], refs[5:10], refs[10:15]
        lo_ref, g_outs, d_outs, mo_outs, vo_outs = refs[15], refs[16:21], refs[21:26], refs[26:31], refs[31:36]
        send_ref, buf_ref, send_sems, recv_sems = refs[36:]
        x, y, c, _ = _position()
        me = 4 * x + 2 * y + c
        send_ref[...] = jnp.zeros_like(send_ref)
        for r in range(chunks):
            send_ref[r:r + 1, :] = gg_ref[0:1, r * LANES:(r + 1) * LANES]
            send_ref[chunks + r:chunks + r + 1, :] = gb_ref[0:1, r * LANES:(r + 1) * LANES]
        send_ref[row_rb:row_rb + NUM_BUCKETS, :] = grb_ref[...]
        send_ref[row_bf:row_bf + 1, :] = gbf_ref[0:1, :]
        send_ref[row_sk:row_sk + 1, :] = gsk_ref[0:1, :]
        send_ref[row_loss:row_loss + 1, :] = loss_ref[0:1, :]
        buf_ref[me] = send_ref[...]
        peers = [(x, y, 1 - c)] + [(px, py, pc) for px, py in _position()[3] for pc in (c, 1 - c)]
        sends = []
        for k, peer in enumerate(peers):
            sends.append(pltpu.make_async_remote_copy(
                src_ref=send_ref, dst_ref=buf_ref.at[me], send_sem=send_sems.at[k], recv_sem=recv_sems.at[k],
                device_id=peer, device_id_type=MESH))
        for cp in sends:
            cp.start()
        for k, (px, py, pc) in enumerate(peers):
            slot = buf_ref.at[4 * px + 2 * py + pc]
            pltpu.make_async_remote_copy(
                src_ref=slot, dst_ref=slot, send_sem=send_sems.at[k], recv_sem=recv_sems.at[k],
                device_id=(px, py, pc), device_id_type=MESH).wait_recv()
        for cp in sends:
            cp.wait_send()
        tot = buf_ref[0]
        for d in range(1, N_DEV):
            tot = tot + buf_ref[d]
        lo_ref[...] = tot[row_loss:row_loss + 1, :]
        grads = [tot[row_bf:row_bf + 1, 0:FOX_HEADS],
                 tot[row_rb:row_rb + NUM_BUCKETS, 0:SWA_HEADS],
                 tot[row_sk:row_sk + 1, 0:SWA_HEADS],
                 jnp.concatenate([tot[r:r + 1, :] for r in range(chunks)], axis=1),
                 jnp.concatenate([tot[chunks + r:chunks + r + 1, :] for r in range(chunks)], axis=1)]
        for i, g in enumerate(grads):
            g_outs[i][...] = g
            delta, mn, vn = _adamw_math(p_refs[i][...], g, m_refs[i][...], v_refs[i][...])
            d_outs[i][...] = delta
            mo_outs[i][...] = mn
            vo_outs[i][...] = vn

    vm = pl.BlockSpec(memory_space=pltpu.VMEM)
    shapes = [jax.ShapeDtypeStruct(p.shape, F32) for p in params]
    outs = pl.pallas_call(
        kern, name="small_allreduce_adamw",
        in_specs=[vm] * 21,
        out_specs=[vm] * 21,
        out_shape=[jax.ShapeDtypeStruct((1, LANES), F32)] + shapes * 4,
        scratch_shapes=[pltpu.VMEM((SMALL_ROWS, LANES), F32),
                        pltpu.VMEM((N_DEV, SMALL_ROWS, LANES), F32),
                        pltpu.SemaphoreType.DMA((N_DEV - 1,)),
                        pltpu.SemaphoreType.DMA((N_DEV - 1,))],
    )(*partials, *params, *moms, *vels)
    return outs[0], outs[1:6], outs[6:11], outs[11:16], outs[16:21]


def _to_padded_cols(w):
    pad = jnp.zeros((w.shape[0], N_C - FOX_HEADS), w.dtype)
    return jnp.concatenate([w[:, 0:1536], w[:, 2056:2824], w[:, 1536:1544], pad,
                            w[:, 1544:2056], w[:, 2824:3336]], axis=1)


def _from_padded_cols(g):
    return jnp.concatenate([g[:, 0:1536], g[:, OFF_C:OFF_C + FOX_HEADS], g[:, OFF_B:OFF_B + FOX_W],
                            g[:, 1536:N_A], g[:, OFF_B + FOX_W:N_PAD]], axis=1)


def _fox_rows(a):
    return a[:, :FOX_HEADS].T.reshape(FOX_HEADS, 1, a.shape[0])


def kernel(x, w_in, b_f, rel_bias, sink, w_o, ln_g, ln_b, loss_target, m_w_in, m_b_f, m_rel_bias, m_sink, m_w_o, m_ln_g, m_ln_b, v_w_in, v_b_f, v_rel_bias, v_sink, v_w_o, v_ln_g, v_ln_b):
    x2 = x[0]
    tgt = loss_target[0]
    s = x2.shape[0]
    w_in2, w_o2 = w_in[0], w_o[0]

    shard_cols = D_IN // N_CHIPS
    col_pad = ((0, 0), (0, SHARD_PAD - shard_cols))
    w_in_all, w_o_all = _gather_weights(jnp.pad(w_in2.astype(BF16), col_pad), w_o2.astype(BF16))
    w_full = jnp.concatenate([w_in_all[j, :, :shard_cols] for j in range(N_CHIPS)], axis=1)
    w_pad = _to_padded_cols(w_full)
    w_o_full = w_o_all.reshape(D_MODEL, D_MODEL)

    qkv, ffp, z, xt, vt = _project(x2, w_pad)
    bfp = jnp.pad(b_f, ((0, 0), (0, LANES - FOX_HEADS)))
    cum = _cum_fwd(ffp, bfp)
    cum_t3 = _fox_rows(cum)
    o_fox, lse_t3 = _fox_fwd(qkv, vt, cum_t3, cum)
    bucket_t = jnp.asarray(_bucket_table().T)
    bias_t = _swa_bias(rel_bias, bucket_t)
    sink_rows = jnp.repeat(sink.reshape(SWA_KV_HEADS, SWA_GROUP, 1), BLOCK, axis=2).reshape(SWA_KV_HEADS, 1, SWA_LANES)
    o_swa, lse_swa = _swa_fwd(qkv, bias_t, sink_rows)

    loss8, dh, grad_w_o_full, do_bf, dz, delta, gg8, gb8 = _post(
        x2, tgt, o_fox, o_swa, z, w_o_full, ln_g, ln_b)

    delta_t3 = _fox_rows(delta)
    dq_fox, dk_fox, dv_fox, dcum_k, dcum_q = _fox_bwd(qkv, do_bf, cum_t3, cum, lse_t3, delta_t3)
    dcum_q = jnp.pad(dcum_q.reshape(FOX_HEADS, s).T, ((0, 0), (0, LANES - FOX_HEADS)))
    dff, gbf8 = _cum_bwd(dcum_k, dcum_q, ffp, bfp)
    delta_rows = (delta[:, FOX_HEADS:FOX_HEADS + SWA_HEADS].reshape(s // BLOCK, BLOCK, SWA_KV_HEADS, SWA_GROUP)
                  .transpose(0, 2, 3, 1).reshape(s // BLOCK, SWA_KV_HEADS, 1, SWA_LANES))
    dq_swa, dk_swa, dv_swa, grb, gsk8 = _swa_bwd(qkv, do_bf, delta_rows, lse_swa, bias_t, sink_rows, bucket_t)

    d_misc = jnp.concatenate([dk_swa, dv_swa, dff], axis=1)
    pieces = [dq_fox, dk_fox, dv_fox, dq_swa, d_misc, dz]
    grad_x = _grad_x_matmul(pieces, w_pad, dh, tm=512, tn=D_MODEL, name="grad_x")
    blocks = [(p, 0) for p in pieces[:-1]] + [(dz, 0), (dz, 1)]
    grad_w_pad = _grad_w_matmul(xt, blocks, tk=1024, name="grad_w_in")
    grad_w_in_full = _from_padded_cols(grad_w_pad)

    g_in4 = jnp.stack([jnp.pad(grad_w_in_full[:, j * shard_cols:(j + 1) * shard_cols], col_pad)
                       for j in range(N_CHIPS)])
    g_o4 = grad_w_o_full.reshape(N_CHIPS, D_MODEL // N_CHIPS, D_MODEL)
    g_w_in, g_w_o = _join_halves(_chip_reduce(_pair_reduce([g_in4, g_o4])))
    g_w_in = g_w_in[:, :shard_cols]

    cols_first = lambda a: jnp.transpose(a, (2, 0, 1))
    rows_first = lambda a: jnp.transpose(a, (1, 2, 0))
    g_w_in, d_w_in, nm_w_in, nv_w_in = [rows_first(a) for a in _adamw_cols(
        cols_first(w_in), cols_first(g_w_in[None]), cols_first(m_w_in), cols_first(v_w_in), name="adamw_w_in")]
    d_w_o, nm_w_o, nv_w_o = _adamw(w_o2, g_w_o, m_w_o[0], v_w_o[0], name="adamw_w_o")

    loss_row, gs, ds, ms, vs = _small_allreduce_adamw(
        [gbf8, grb, gsk8, gg8, gb8, loss8],
        [b_f, rel_bias, sink, ln_g, ln_b],
        [m_b_f, m_rel_bias, m_sink, m_ln_g, m_ln_b],
        [v_b_f, v_rel_bias, v_sink, v_ln_g, v_ln_b])
    loss = loss_row[0, 0]
    g_bf, g_rb, g_sk, g_lg, g_lb = gs
    d_bf, d_rb, d_sk, d_lg, d_lb = ds
    m_bf, m_rb, m_sk, m_lg, m_lb = ms
    v_bf, v_rb, v_sk, v_lg, v_lb = vs

    e = lambda a: a[None]
    return (loss, e(grad_x),
            g_w_in, g_bf, g_rb, g_sk, e(g_w_o), g_lg, g_lb,
            d_w_in, d_bf, d_rb, d_sk, e(d_w_o), d_lg, d_lb,
            nm_w_in, m_bf, m_rb, m_sk, e(nm_w_o), m_lg, m_lb,
            nv_w_in, v_bf, v_rb, v_sk, e(nv_w_o), v_lg, v_lb)
```

```python
import functools
import math

import numpy as np
import jax
import jax.numpy as jnp
from jax import lax
from jax.experimental import pallas as pl
from jax.experimental.pallas import tpu as pltpu

F32 = jnp.float32
BF16 = jnp.bfloat16

D_MODEL = 1024
HEAD_DIM = 64
FOX_HEADS = 8
SWA_HEADS = 8
SWA_KV_HEADS = 2
SWA_GROUP = 4
FOX_W = 512
SWA_W = 512
BLOCK = 128
NUM_BUCKETS = 32
MAX_DISTANCE = 128
LN_EPS = 1e-5
NEG = -1e30
ALPHA = 2.0 ** 0.25
QK_SCALE = 0.125

ADAM_LR = 0.001
ADAM_B1 = 0.9
ADAM_B2 = 0.999
ADAM_EPS = 1e-08
ADAM_WD = 0.01
ADAM_STEP = 10

D_IN = 3336
SHARD_PAD = 896
N_A = 2304
N_C = 256
N_B = 1024
OFF_C = N_A
OFF_B = N_A + N_C
N_PAD = N_A + N_C + N_B
COL_FK, COL_FV, COL_SQ, COL_SK, COL_SV = 512, 1024, 1536, 2048, 2176

LANES = 128
FOX_T = 256
FOX_REF = 512
SUM_ROWS = 16
VMEM_LIMIT = 56 * 1024 * 1024

MESH = pl.DeviceIdType.MESH
N_CHIPS = 4
N_DEV = 8
SMALL_ROWS = 56


def _cparams(sem=None):
    return pltpu.CompilerParams(dimension_semantics=sem, vmem_limit_bytes=VMEM_LIMIT)


def _split3(x):
    hi = x.astype(BF16)
    r = x - hi.astype(F32)
    mid = r.astype(BF16)
    lo = (r - mid.astype(F32)).astype(BF16)
    return hi, mid, lo


def _dot(a, b):
    return jnp.dot(a, b, preferred_element_type=F32)


def _dot_nt(a, b):
    return lax.dot_general(a, b, (((1,), (1,)), ((), ())), preferred_element_type=F32)


def _project(x, w_pad):
    s, k = x.shape
    tm = 512
    chunk = 512

    def kern(x_ref, w_ref, qkv_ref, ff_ref, z_ref, xt_ref, vt_ref):
        xf = x_ref[...]
        xb = xf.astype(BF16)
        xt_ref[...] = xf.T.astype(BF16)
        for c0 in range(0, N_A, chunk):
            width = min(chunk, N_A - c0)
            res = _dot(xb, w_ref[:, c0:c0 + width])
            qkv_ref[:, c0:c0 + width] = res.astype(BF16)
            if c0 == COL_FV:
                vt_ref[...] = res.T.astype(BF16)
        ff_ref[...] = _dot(xb, w_ref[:, OFF_C:OFF_C + N_C])
        for c0 in range(0, N_B, 512):
            z_ref[:, c0:c0 + 512] = _dot(xb, w_ref[:, OFF_B + c0:OFF_B + c0 + 512])

    row = lambda i: (i, 0)
    return pl.pallas_call(
        kern, name="project",
        grid=(s // tm,),
        in_specs=[pl.BlockSpec((tm, k), row),
                  _resident((k, N_PAD), lambda i: (0, 0))],
        out_specs=[pl.BlockSpec((tm, N_A), row),
                   pl.BlockSpec((tm, N_C), row),
                   pl.BlockSpec((tm, N_B), row),
                   pl.BlockSpec((k, tm), lambda i: (0, i)),
                   pl.BlockSpec((FOX_W, tm), lambda i: (0, i))],
        out_shape=[jax.ShapeDtypeStruct((s, N_A), BF16),
                   jax.ShapeDtypeStruct((s, N_C), F32),
                   jax.ShapeDtypeStruct((s, N_B), F32),
                   jax.ShapeDtypeStruct((k, s), BF16),
                   jax.ShapeDtypeStruct((FOX_W, s), BF16)],
        compiler_params=_cparams(("parallel",)),
    )(x, w_pad)


def _grad_x_matmul(pieces, w_pad, dh, *, tm, tn, name):
    m = dh.shape[0]
    n, k = w_pad.shape
    widths = [p.shape[1] for p in pieces]
    offs = [sum(widths[:i]) for i in range(len(pieces))]
    assert sum(widths) == k

    def kern(*refs):
        p_refs, (b_ref, dh_ref, o_ref) = refs[:len(pieces)], refs[len(pieces):]
        acc = ALPHA * dh_ref[...]
        for p_ref, off, width in zip(p_refs, offs, widths):
            acc = acc + _dot_nt(p_ref[...], b_ref[:, off:off + width])
        o_ref[...] = acc

    assert tn == n
    return pl.pallas_call(
        kern, name=name,
        grid=(m // tm,),
        in_specs=[pl.BlockSpec((tm, w), lambda i: (i, 0)) for w in widths]
        + [_resident((n, k), lambda i: (0, 0)),
           pl.BlockSpec((tm, n), lambda i: (i, 0))],
        out_specs=pl.BlockSpec((tm, n), lambda i: (i, 0)),
        out_shape=jax.ShapeDtypeStruct((m, n), F32),
        compiler_params=_cparams(("parallel",)),
    )(*pieces, w_pad, dh)


def _grad_w_matmul(xt, blocks, *, tk, name):
    m, s = xt.shape
    tn = 512
    nb = len(blocks)

    def kern(a_ref, *refs):
        b_refs, o_ref = refs[:nb], refs[nb]

        @pl.when(pl.program_id(0) == 0)
        def _():
            o_ref[...] = jnp.zeros_like(o_ref)
        a = a_ref[...]
        for blk in range(nb):
            o_ref[:, blk * tn:(blk + 1) * tn] += _dot(a, b_refs[blk][...])

    return pl.pallas_call(
        kern, name=name,
        grid=(s // tk,),
        in_specs=[pl.BlockSpec((m, tk), lambda k: (0, k))]
        + [pl.BlockSpec((tk, tn), functools.partial(lambda k, col: (k, col), col=col)) for _, col in blocks],
        out_specs=_resident((m, nb * tn), lambda k: (0, 0)),
        out_shape=jax.ShapeDtypeStruct((m, nb * tn), F32),
        compiler_params=_cparams(("arbitrary",)),
    )(xt, *[arr for arr, _ in blocks])


def _tri(n, lower):
    r = lax.broadcasted_iota(jnp.int32, (n, n), 0)
    c = lax.broadcasted_iota(jnp.int32, (n, n), 1)
    keep = (c <= r) if lower else (c >= r)
    return jnp.where(keep, 1.0, 0.0).astype(BF16)


def _exact_dot(mat_bf16, x_f32, left):
    out = None
    for piece in _split3(x_f32):
        t = _dot(mat_bf16, piece) if left else _dot(piece, mat_bf16)
        out = t if out is None else out + t
    return out


def _log_sigmoid(z):
    return jnp.minimum(z, 0.0) - jnp.log(1.0 + jnp.exp(-jnp.abs(z)))


def _cum_fwd(ffp, bfp):
    s = ffp.shape[0]
    t = min(1024, s)

    def kern(ff_ref, b_ref, cum_ref, carry_ref):
        @pl.when(pl.program_id(0) == 0)
        def _():
            carry_ref[...] = jnp.zeros_like(carry_ref)
        lane = lax.broadcasted_iota(jnp.int32, (1, LANES), 1)
        lf = _log_sigmoid(ff_ref[...] + b_ref[...])
        lf = jnp.where(lane < FOX_HEADS, lf, 0.0)
        cum = _exact_dot(_tri(t, True), lf, True) + carry_ref[0:1, :]
        cum_ref[...] = cum
        carry_ref[...] = jnp.broadcast_to(cum[t - 1:t, :], carry_ref.shape)

    return pl.pallas_call(
        kern, name="cum_fwd",
        grid=(s // t,),
        in_specs=[pl.BlockSpec((t, LANES), lambda i: (i, 0)),
                  pl.BlockSpec((1, LANES), lambda i: (0, 0))],
        out_specs=pl.BlockSpec((t, LANES), lambda i: (i, 0)),
        out_shape=jax.ShapeDtypeStruct((s, LANES), F32),
        scratch_shapes=[pltpu.VMEM((8, LANES), F32)],
        compiler_params=_cparams(("arbitrary",)),
    )(ffp, bfp)


def _cum_bwd(dcum_k, dcum_q, ffp, bfp):
    s = dcum_k.shape[0]
    t = min(1024, s)
    nb = s // t

    def kern(dck_ref, dcq_ref, ff_ref, b_ref, dff_ref, gb_ref, carry_ref):
        @pl.when(pl.program_id(0) == 0)
        def _():
            carry_ref[...] = jnp.zeros_like(carry_ref)
            gb_ref[...] = jnp.zeros_like(gb_ref)
        lane = lax.broadcasted_iota(jnp.int32, (1, LANES), 1)
        dlf = _exact_dot(_tri(t, False), dck_ref[...] + dcq_ref[...], True) + carry_ref[0:1, :]
        carry_ref[...] = jnp.broadcast_to(dlf[0:1, :], carry_ref.shape)
        z = ff_ref[...] + b_ref[...]
        dff = jnp.where(lane < FOX_HEADS, dlf / (1.0 + jnp.exp(z)), 0.0)
        gb_ref[...] += jnp.broadcast_to(jnp.sum(dff, axis=0, keepdims=True), gb_ref.shape)
        dff_ref[...] = jnp.concatenate([dff, jnp.zeros_like(dff)], axis=1).astype(BF16)

    return pl.pallas_call(
        kern, name="cum_bwd",
        grid=(nb,),
        in_specs=[pl.BlockSpec((t, LANES), lambda i: (nb - 1 - i, 0)),
                  pl.BlockSpec((t, LANES), lambda i: (nb - 1 - i, 0)),
                  pl.BlockSpec((t, LANES), lambda i: (nb - 1 - i, 0)),
                  pl.BlockSpec((1, LANES), lambda i: (0, 0))],
        out_specs=[pl.BlockSpec((t, N_C), lambda i: (nb - 1 - i, 0)),
                   pl.BlockSpec((8, LANES), lambda i: (0, 0))],
        out_shape=[jax.ShapeDtypeStruct((s, N_C), BF16),
                   jax.ShapeDtypeStruct((8, LANES), F32)],
        scratch_shapes=[pltpu.VMEM((8, LANES), F32)],
        compiler_params=_cparams(("arbitrary",)),
    )(dcum_k, dcum_q, ffp, bfp)


def _resident(shape, index_map):
    return pl.BlockSpec(shape, index_map, pipeline_mode=pl.Buffered(1))


def _fox_fwd(qkv, vt, cum_t3, cum):
    s = qkv.shape[0]
    tk = tq = FOX_REF
    nq = s // tq
    nh = FOX_HEADS
    diag_tiles = tq // tk

    def kern(q_ref, k_ref, vt_ref, ct_ref, c_ref, o_ref, lse_ref, m_ref, acc_ref, u_ref):
        i = pl.program_id(0)
        lane = lax.broadcasted_iota(jnp.int32, (1, LANES), 1)
        krow = lax.broadcasted_iota(jnp.int32, (tk, tq), 0)
        qcol = lax.broadcasted_iota(jnp.int32, (tk, tq), 1)
        q0 = pl.multiple_of(i * tq, tq)
        qts, crefs = [], []
        for h in range(nh):
            p, a = divmod(h, 2)
            q2 = q_ref[:, p * LANES:(p + 1) * LANES] * jnp.asarray(QK_SCALE, BF16)
            sel = (lane < HEAD_DIM) if a == 0 else (lane >= HEAD_DIM)
            qts.append(jnp.where(sel, q2, jnp.zeros_like(q2)).astype(F32).T.astype(BF16))
            crefs.append(ct_ref[h, :, pl.ds(q0, LANES)][:, 0:1])
        m_ref[...] = jnp.full(m_ref.shape, NEG, F32)
        acc_ref[...] = jnp.zeros_like(acc_ref)
        ones = jnp.ones((SUM_ROWS, tk), BF16)

        def tile(j, diag):
            k0 = pl.multiple_of(j * tk, tk)
            cb = c_ref[pl.ds(k0, tk), :]
            sts = [_dot(k_ref[pl.ds(k0, tk), (h // 2) * LANES:(h // 2 + 1) * LANES], qts[h]) for h in range(nh)]
            tile_max = []
            for h in range(nh):
                u = sts[h] - (cb[:, h:h + 1] - crefs[h])
                if diag is not None:
                    u = jnp.where(krow + diag * tk <= qcol, u, NEG)
                u_ref[h] = u
                tile_max.append(jnp.max(u, axis=0, keepdims=True))
            pts, scales = [], []
            for h in range(nh):
                m_old = m_ref[h]
                m_new = jnp.maximum(m_old, tile_max[h])
                scales.append(jnp.exp(m_old - m_new))
                pts.append(jnp.exp(u_ref[h] - m_new).astype(BF16))
                m_ref[h] = m_new
            for h in range(nh):
                vth = jnp.concatenate([vt_ref[h * HEAD_DIM:(h + 1) * HEAD_DIM, pl.ds(k0, tk)], ones], axis=0)
                acc_ref[h] = scales[h] * acc_ref[h] + _dot(vth, pts[h])

        def body(j, c):
            tile(j, None)
            return c
        lax.fori_loop(0, i * diag_tiles, body, 0)
        for d in range(diag_tiles):
            tile(i * diag_tiles + d, d)

        ls = [acc_ref[h][HEAD_DIM:HEAD_DIM + 1] for h in range(nh)]
        for p in range(nh // 2):
            ot = jnp.concatenate([acc_ref[2 * p + a][:HEAD_DIM] * (1.0 / ls[2 * p + a]) for a in range(2)], axis=0)
            o_ref[:, p * LANES:(p + 1) * LANES] = ot.T
        for h in range(nh):
            lse_ref[h, :, pl.ds(q0, tq)] = m_ref[h] + jnp.log(ls[h])

    return pl.pallas_call(
        kern, name="fox_fwd",
        grid=(nq,),
        in_specs=[pl.BlockSpec((tq, FOX_W), lambda i: (i, 0)),
                  _resident((s, FOX_W), lambda i: (0, COL_FK // FOX_W)),
                  _resident((FOX_W, s), lambda i: (0, 0)),
                  _resident((nh, 1, s), lambda i: (0, 0, 0)),
                  _resident((s, LANES), lambda i: (0, 0))],
        out_specs=[pl.BlockSpec((tq, FOX_W), lambda i: (i, 0)),
                   pl.BlockSpec((nh, 1, s), lambda i: (0, 0, 0))],
        out_shape=[jax.ShapeDtypeStruct((s, FOX_W), F32),
                   jax.ShapeDtypeStruct((nh, 1, s), F32)],
        scratch_shapes=[pltpu.VMEM((nh, 1, tq), F32),
                        pltpu.VMEM((nh, HEAD_DIM + SUM_ROWS, tq), F32),
                        pltpu.VMEM((nh, tk, tq), F32)],
        compiler_params=_cparams(("arbitrary",)),
    )(qkv, qkv, vt, cum_t3, cum)


def _fox_bwd(qkv, do_bf, cum_t3, cum, lse_t3, delta_t3):
    s = qkv.shape[0]
    t = min(FOX_T, s)
    nq = s // t
    nh = FOX_HEADS
    npair = nh // 2

    def kern(q_ref, do_ref, k_ref, v_ref, ct_ref, c_ref, lse_ref, dl_ref,
             dq_ref, dk_ref, dv_ref, dc_ref, dcq_ref, dqt_ref, accv_ref, acck_ref, accd_ref):
        kj = pl.program_id(0)
        lane = lax.broadcasted_iota(jnp.int32, (1, LANES), 1)
        krow = lax.broadcasted_iota(jnp.int32, (t, t), 0)
        qcol = lax.broadcasted_iota(jnp.int32, (t, t), 1)
        causal = krow <= qcol
        sels = [lane < HEAD_DIM, lane >= HEAD_DIM]

        @pl.when(kj == 0)
        def _():
            dqt_ref[...] = jnp.zeros_like(dqt_ref)
            dcq_ref[...] = jnp.zeros_like(dcq_ref)

        cb = c_ref[...]
        k2s, v2s, kts = [], [], []
        for p in range(npair):
            k2 = k_ref[:, p * LANES:(p + 1) * LANES]
            k2s.append(k2)
            v2s.append(v_ref[:, p * LANES:(p + 1) * LANES])
            kt = k2.astype(F32).T * QK_SCALE
            kts.append(kt[:HEAD_DIM].astype(BF16))
            kts.append(kt[HEAD_DIM:].astype(BF16))
        css = [cb[:, h:h + 1] for h in range(nh)]

        def tile(i, masked):
            q0 = pl.multiple_of(i * t, t)
            r0 = pl.multiple_of((i // (FOX_REF // t)) * FOX_REF, FOX_REF)
            sts, dpts, qms, doms = [], [], [], []
            for h in range(nh):
                p, a = divmod(h, 2)
                qi = q_ref[pl.ds(q0, t), p * LANES:(p + 1) * LANES] * jnp.asarray(QK_SCALE, BF16)
                doi = do_ref[pl.ds(q0, t), p * LANES:(p + 1) * LANES]
                qm = jnp.where(sels[a], qi, jnp.zeros_like(qi))
                dom = jnp.where(sels[a], doi, jnp.zeros_like(doi))
                qms.append(qm)
                doms.append(dom)
                sts.append(_dot_nt(k2s[p], qm))
                dpts.append(_dot_nt(v2s[p], dom))
            pts, dsts = [], []
            for h in range(nh):
                cref = ct_ref[h, :, pl.ds(r0, LANES)][:, 0:1]
                pt = jnp.exp(sts[h] - (css[h] - cref) - lse_ref[h, :, pl.ds(q0, t)])
                if masked:
                    pt = jnp.where(causal, pt, 0.0)
                ds32 = pt * (dpts[h] - dl_ref[h, :, pl.ds(q0, t)])
                part = ds32[:, 0:LANES]
                for c in range(1, t // LANES):
                    part = part + ds32[:, c * LANES:(c + 1) * LANES]
                accd_ref[h] = part if masked else accd_ref[h] + part
                dcq_ref[h, :, pl.ds(q0, t)] += jnp.sum(ds32, axis=0, keepdims=True)
                pts.append(pt.astype(BF16))
                dsts.append(ds32.astype(BF16))
            for p in range(npair):
                ha, hb = 2 * p, 2 * p + 1
                dv_p = _dot(pts[ha], doms[ha]) + _dot(pts[hb], doms[hb])
                dk_p = _dot(dsts[ha], qms[ha]) + _dot(dsts[hb], qms[hb])
                accv_ref[p] = dv_p if masked else accv_ref[p] + dv_p
                acck_ref[p] = dk_p if masked else acck_ref[p] + dk_p
            for h in range(nh):
                dqt_ref[h * HEAD_DIM:(h + 1) * HEAD_DIM, pl.ds(q0, t)] += _dot(kts[h], dsts[h])

        tile(kj, True)

        def body(i, c):
            tile(i, False)
            return c
        lax.fori_loop(kj + 1, nq, body, 0)

        dc = jnp.zeros((t, LANES), F32)
        for h in range(nh):
            dc = jnp.where(lane == h, -jnp.sum(accd_ref[h], axis=1, keepdims=True), dc)
        dc_ref[...] = dc
        for p in range(npair):
            dv_ref[:, p * LANES:(p + 1) * LANES] = accv_ref[p].astype(BF16)
            dk_ref[:, p * LANES:(p + 1) * LANES] = acck_ref[p].astype(BF16)

        @pl.when(kj == nq - 1)
        def _():
            for c0 in range(0, s, t):
                dq_ref[c0:c0 + t, :] = dqt_ref[:, c0:c0 + t].T.astype(BF16)

    whole = lambda kj: (0, 0, 0)
    return pl.pallas_call(
        kern, name="fox_bwd",
        grid=(nq,),
        in_specs=[_resident((s, FOX_W), lambda kj: (0, 0)),
                  _resident((s, FOX_W), lambda kj: (0, 0)),
                  pl.BlockSpec((t, FOX_W), lambda kj: (kj, COL_FK // FOX_W)),
                  pl.BlockSpec((t, FOX_W), lambda kj: (kj, COL_FV // FOX_W)),
                  _resident((nh, 1, s), whole),
                  pl.BlockSpec((t, LANES), lambda kj: (kj, 0)),
                  _resident((nh, 1, s), whole),
                  _resident((nh, 1, s), whole)],
        out_specs=[_resident((s, FOX_W), lambda kj: (0, 0)),
                   pl.BlockSpec((t, FOX_W), lambda kj: (kj, 0)),
                   pl.BlockSpec((t, FOX_W), lambda kj: (kj, 0)),
                   pl.BlockSpec((t, LANES), lambda kj: (kj, 0)),
                   _resident((nh, 1, s), whole)],
        out_shape=[jax.ShapeDtypeStruct((s, FOX_W), BF16),
                   jax.ShapeDtypeStruct((s, FOX_W), BF16),
                   jax.ShapeDtypeStruct((s, FOX_W), BF16),
                   jax.ShapeDtypeStruct((s, LANES), F32),
                   jax.ShapeDtypeStruct((nh, 1, s), F32)],
        scratch_shapes=[pltpu.VMEM((FOX_W, s), F32),
                        pltpu.VMEM((npair, t, LANES), F32),
                        pltpu.VMEM((npair, t, LANES), F32),
                        pltpu.VMEM((nh, t, LANES), F32)],
        compiler_params=_cparams(("arbitrary",)),
    )(qkv, do_bf, qkv, qkv, cum_t3, cum, lse_t3, delta_t3)


def _bucket_table():
    qi = np.arange(BLOCK)[:, None]
    kj = np.arange(2 * BLOCK)[None, :]
    rel = np.maximum(qi + BLOCK - kj, 0).astype(np.int32)
    max_exact = NUM_BUCKETS // 2
    relf = np.maximum(rel, 1).astype(np.float32)
    large = max_exact + (np.log(relf / np.float32(max_exact)) / np.float32(math.log(MAX_DISTANCE / max_exact))
                         * np.float32(NUM_BUCKETS - max_exact)).astype(np.int32)
    large = np.minimum(large, NUM_BUCKETS - 1)
    return np.where(rel < max_exact, rel, large).astype(np.int32)


SWA_LANES = SWA_GROUP * BLOCK


def _swa_bias(rel_bias, bucket_t):
    def kern(rb_ref, bk_ref, o_ref):
        bk = bk_ref[...]
        kj = lax.broadcasted_iota(jnp.int32, (2 * BLOCK, BLOCK), 0)
        qi = lax.broadcasted_iota(jnp.int32, (2 * BLOCK, BLOCK), 1)
        rel = qi + BLOCK - kj
        band = (rel >= 0) & (rel < BLOCK)
        masks = [band & (kj >= BLOCK), band]
        for h in range(SWA_HEADS):
            g, hh = divmod(h, SWA_GROUP)
            acc = jnp.zeros((2 * BLOCK, BLOCK), F32)
            for b in range(NUM_BUCKETS):
                acc = jnp.where(bk == b, rb_ref[b, h], acc)
            for first in range(2):
                o_ref[first, g, :, hh * BLOCK:(hh + 1) * BLOCK] = jnp.where(masks[first], acc, NEG)

    return pl.pallas_call(
        kern, name="swa_bias",
        in_specs=[pl.BlockSpec(memory_space=pltpu.SMEM),
                  pl.BlockSpec(memory_space=pltpu.VMEM)],
        out_specs=pl.BlockSpec(memory_space=pltpu.VMEM),
        out_shape=jax.ShapeDtypeStruct((2, SWA_KV_HEADS, 2 * BLOCK, SWA_LANES), F32),
        compiler_params=_cparams(),
    )(rel_bias, bucket_t)


SWA_STEP = 4


def _swa_keys(prev_ref, cur_ref):
    return jnp.concatenate([prev_ref[...], cur_ref[...]], axis=0)


def _swa_queries(x_ref, scale):
    x = x_ref[...]
    if scale:
        x = x * jnp.asarray(QK_SCALE, BF16)
    xt = x.astype(F32).T.astype(BF16)
    return [_group_rows(xt[:, b * BLOCK:(b + 1) * BLOCK]) for b in range(SWA_STEP)]


def _group_rows(xt):
    zeros = jnp.zeros((HEAD_DIM, SWA_LANES), BF16)
    out = []
    for g in range(SWA_KV_HEADS):
        heads = [xt[(SWA_GROUP * g + hh) * HEAD_DIM:(SWA_GROUP * g + hh + 1) * HEAD_DIM, :] for hh in range(SWA_GROUP)]
        rows = jnp.concatenate(heads, axis=1)
        padded = jnp.concatenate([rows, zeros] if g == 0 else [zeros, rows], axis=0)
        out.append((rows, padded))
    return out


def _pairs_to_rows(cols_t):
    out = []
    for p in range(SWA_HEADS // 2):
        g, hh = divmod(2 * p, SWA_GROUP)
        pair = jnp.concatenate([cols_t[g][:, hh * BLOCK:(hh + 1) * BLOCK],
                                cols_t[g][:, (hh + 1) * BLOCK:(hh + 2) * BLOCK]], axis=0)
        out.append(pair.T)
    return jnp.concatenate(out, axis=1)


def _swa_fwd(qkv, bias_t, sink_rows):
    s = qkv.shape[0]
    nb = s // BLOCK
    rows = SWA_STEP * BLOCK
    units = [(b, g) for b in range(SWA_STEP) for g in range(SWA_KV_HEADS)]

    def kern(q_ref, kp_ref, kc_ref, vp_ref, vc_ref, bias_ref, sink_ref, o_ref, lse_ref):
        n = pl.program_id(0)
        tables = [jnp.minimum(n, 1)] + [1] * (SWA_STEP - 1)
        k3 = _swa_keys(kp_ref, kc_ref)
        vt3 = _swa_keys(vp_ref, vc_ref).astype(F32).T.astype(BF16)
        qts = _swa_queries(q_ref, True)
        us = [_dot(k3[b * BLOCK:(b + 2) * BLOCK], qts[b][g][1]) + bias_ref[tables[b], g] for b, g in units]
        outs = []
        for (b, g), u in zip(units, us):
            sk = sink_ref[g]
            m = jnp.maximum(jnp.max(u, axis=0, keepdims=True), sk)
            p = jnp.exp(u - m)
            l = jnp.sum(p, axis=0, keepdims=True) + jnp.exp(sk - m)
            lse_ref[b, g] = m + jnp.log(l)
            vt = vt3[g * HEAD_DIM:(g + 1) * HEAD_DIM, b * BLOCK:(b + 2) * BLOCK]
            outs.append(_dot(vt, (p * (1.0 / l)).astype(BF16)))
        for b in range(SWA_STEP):
            o_ref[b * BLOCK:(b + 1) * BLOCK, :] = _pairs_to_rows(outs[b * SWA_KV_HEADS:(b + 1) * SWA_KV_HEADS])

    cq, ck, cv = COL_SQ // SWA_W, COL_SK // LANES, COL_SV // LANES
    prev = lambda n: jnp.maximum(SWA_STEP * n - 1, 0)
    return pl.pallas_call(
        kern, name="swa_fwd",
        grid=(nb // SWA_STEP,),
        in_specs=[pl.BlockSpec((rows, SWA_W), lambda n: (n, cq)),
                  pl.BlockSpec((BLOCK, LANES), lambda n: (prev(n), ck)),
                  pl.BlockSpec((rows, LANES), lambda n: (n, ck)),
                  pl.BlockSpec((BLOCK, LANES), lambda n: (prev(n), cv)),
                  pl.BlockSpec((rows, LANES), lambda n: (n, cv)),
                  _resident((2, SWA_KV_HEADS, 2 * BLOCK, SWA_LANES), lambda n: (0, 0, 0, 0)),
                  _resident((SWA_KV_HEADS, 1, SWA_LANES), lambda n: (0, 0, 0))],
        out_specs=[pl.BlockSpec((rows, SWA_W), lambda n: (n, 0)),
                   pl.BlockSpec((SWA_STEP, SWA_KV_HEADS, 1, SWA_LANES), lambda n: (n, 0, 0, 0))],
        out_shape=[jax.ShapeDtypeStruct((s, SWA_W), F32),
                   jax.ShapeDtypeStruct((nb, SWA_KV_HEADS, 1, SWA_LANES), F32)],
        compiler_params=_cparams(("parallel",)),
    )(qkv, qkv, qkv, qkv, qkv, bias_t, sink_rows)


def _swa_bwd(qkv, do_bf, delta_rows, lse, bias_t, sink_rows, bucket_t):
    s = qkv.shape[0]
    nb = s // BLOCK
    steps = nb // SWA_STEP
    rows = SWA_STEP * BLOCK
    units = [(b, g) for b in range(SWA_STEP) for g in range(SWA_KV_HEADS)]

    def kern(q_ref, kp_ref, kc_ref, vp_ref, vc_ref, do_ref, dl_ref, lse_ref, bias_ref, sink_ref, bk_ref,
             dq_ref, dk_ref, dv_ref, grb_ref, gsk_ref, dbias_ref, ck_ref, cv_ref, sk_ref):
        n = pl.program_id(0)

        @pl.when(n == 0)
        def _():
            dbias_ref[...] = jnp.zeros_like(dbias_ref)
            ck_ref[...] = jnp.zeros_like(ck_ref)
            cv_ref[...] = jnp.zeros_like(cv_ref)
            sk_ref[...] = jnp.zeros_like(sk_ref)

        @pl.when(n < steps)
        def _():
            tables = [jnp.minimum(n, 1)] + [1] * (SWA_STEP - 1)
            k3 = _swa_keys(kp_ref, kc_ref)
            v3 = _swa_keys(vp_ref, vc_ref)
            kt3 = (k3.astype(F32).T * QK_SCALE).astype(BF16)
            qts = _swa_queries(q_ref, True)
            dots = _swa_queries(do_ref, False)
            sts = [_dot(k3[b * BLOCK:(b + 2) * BLOCK], qts[b][g][1]) for b, g in units]
            dps = [_dot(v3[b * BLOCK:(b + 2) * BLOCK], dots[b][g][1]) for b, g in units]
            ps, dss = [], []
            for i, (b, g) in enumerate(units):
                lse_g = lse_ref[b, g]
                dlt = dl_ref[b, g]
                p = jnp.exp(sts[i] + bias_ref[tables[b], g] - lse_g)
                ds = p * (dps[i] - dlt)
                dbias_ref[g] += ds
                sk_ref[g] += -jnp.exp(sink_ref[g] - lse_g) * dlt
                ps.append(p.astype(BF16))
                dss.append(ds.astype(BF16))
            dk2, dv2 = [], []
            for b in range(SWA_STEP):
                at = lambda g: b * SWA_KV_HEADS + g
                groups = range(SWA_KV_HEADS)
                dv2.append(jnp.concatenate([_dot_nt(dots[b][g][0], ps[at(g)]) for g in groups], axis=0).T)
                dk2.append(jnp.concatenate([_dot_nt(qts[b][g][0], dss[at(g)]) for g in groups], axis=0).T)
                dqts = [_dot(kt3[g * HEAD_DIM:(g + 1) * HEAD_DIM, b * BLOCK:(b + 2) * BLOCK], dss[at(g)]) for g in groups]
                dq_ref[b * BLOCK:(b + 1) * BLOCK, :] = _pairs_to_rows(dqts).astype(BF16)
            last = (SWA_STEP - 1) * BLOCK
            for acc_ref, out_ref, parts in ((ck_ref, dk_ref, dk2), (cv_ref, dv_ref, dv2)):
                done = acc_ref[last:] + parts[0][:BLOCK]
                out_ref[...] = jnp.concatenate([acc_ref[:last], done], axis=0).astype(BF16)
                for b in range(SWA_STEP - 1):
                    acc_ref[b * BLOCK:(b + 1) * BLOCK] = parts[b][BLOCK:] + parts[b + 1][:BLOCK]
                acc_ref[last:] = parts[SWA_STEP - 1][BLOCK:]

        @pl.when(n == steps)
        def _():
            dk_ref[...] = ck_ref[...].astype(BF16)
            dv_ref[...] = cv_ref[...].astype(BF16)
            bk = bk_ref[...]
            lane = lax.broadcasted_iota(jnp.int32, (8, LANES), 1)
            rowi = lax.broadcasted_iota(jnp.int32, (NUM_BUCKETS, LANES), 0)
            lanei = lax.broadcasted_iota(jnp.int32, (NUM_BUCKETS, LANES), 1)
            out = jnp.zeros((NUM_BUCKETS, LANES), F32)
            gsk = jnp.zeros((8, LANES), F32)
            for h in range(SWA_HEADS):
                g, hh = divmod(h, SWA_GROUP)
                cols = slice(hh * BLOCK, (hh + 1) * BLOCK)
                gsk = jnp.where(lane == h, jnp.sum(sk_ref[g][:, cols]), gsk)
                db = dbias_ref[g][:, cols]
                for b in range(NUM_BUCKETS):
                    val = jnp.sum(jnp.where(bk == b, db, 0.0))
                    out = jnp.where((rowi == b) & (lanei == h), val, out)
            grb_ref[...] = out
            gsk_ref[...] = gsk

    cq, ck, cv = COL_SQ // SWA_W, COL_SK // LANES, COL_SV // LANES
    cur = lambda n: jnp.minimum(n, steps - 1)
    prev = lambda n: jnp.maximum(SWA_STEP * cur(n) - 1, 0)
    kout = lambda n: jnp.maximum(n - 1, 0)
    stat = pl.BlockSpec((SWA_STEP, SWA_KV_HEADS, 1, SWA_LANES), lambda n: (cur(n), 0, 0, 0))
    return pl.pallas_call(
        kern, name="swa_bwd",
        grid=(steps + 1,),
        in_specs=[pl.BlockSpec((rows, SWA_W), lambda n: (cur(n), cq)),
                  pl.BlockSpec((BLOCK, LANES), lambda n: (prev(n), ck)),
                  pl.BlockSpec((rows, LANES), lambda n: (cur(n), ck)),
                  pl.BlockSpec((BLOCK, LANES), lambda n: (prev(n), cv)),
                  pl.BlockSpec((rows, LANES), lambda n: (cur(n), cv)),
                  pl.BlockSpec((rows, SWA_W), lambda n: (cur(n), 1)),
                  stat, stat,
                  _resident((2, SWA_KV_HEADS, 2 * BLOCK, SWA_LANES), lambda n: (0, 0, 0, 0)),
                  _resident((SWA_KV_HEADS, 1, SWA_LANES), lambda n: (0, 0, 0)),
                  _resident((2 * BLOCK, BLOCK), lambda n: (0, 0))],
        out_specs=[pl.BlockSpec((rows, SWA_W), lambda n: (cur(n), 0)),
                   pl.BlockSpec((rows, LANES), lambda n: (kout(n), 0)),
                   pl.BlockSpec((rows, LANES), lambda n: (kout(n), 0)),
                   pl.BlockSpec((NUM_BUCKETS, LANES), lambda n: (0, 0)),
                   pl.BlockSpec((8, LANES), lambda n: (0, 0))],
        out_shape=[jax.ShapeDtypeStruct((s, SWA_W), BF16),
                   jax.ShapeDtypeStruct((s, LANES), BF16),
                   jax.ShapeDtypeStruct((s, LANES), BF16),
                   jax.ShapeDtypeStruct((NUM_BUCKETS, LANES), F32),
                   jax.ShapeDtypeStruct((8, LANES), F32)],
        scratch_shapes=[pltpu.VMEM((SWA_KV_HEADS, 2 * BLOCK, SWA_LANES), F32),
                        pltpu.VMEM((rows, LANES), F32),
                        pltpu.VMEM((rows, LANES), F32),
                        pltpu.VMEM((SWA_KV_HEADS, 1, SWA_LANES), F32)],
        compiler_params=_cparams(("arbitrary",)),
    )(qkv, qkv, qkv, qkv, qkv, do_bf, delta_rows, lse, bias_t, sink_rows, bucket_t)


def _post(x, target, o_fox, o_swa, z, w_o, ln_g, ln_b):
    s = x.shape[0]
    tm = min(256, s)
    nt = s // tm

    def kern(x_ref, t_ref, of_ref, os_ref, z_ref, w_ref, g_ref, b_ref,
             loss_ref, dh_ref, gwo_ref, do_ref, dz_ref, dl_ref, gg_ref, gb_ref, lacc_ref):
        step = pl.program_id(0)

        @pl.when(step == 0)
        def _():
            lacc_ref[...] = jnp.zeros_like(lacc_ref)
            gg_ref[...] = jnp.zeros_like(gg_ref)
            gwo_ref[...] = jnp.zeros_like(gwo_ref)
            gb_ref[...] = jnp.zeros_like(gb_ref)

        o = jnp.concatenate([of_ref[...], os_ref[...]], axis=1)
        zz = z_ref[...]
        sig = 1.0 / (1.0 + jnp.exp(-zz))
        silu = zz * sig
        mixed32 = o * silu
        mixed = mixed32.astype(BF16)
        w = w_ref[...]
        h = ALPHA * x_ref[...] + _dot(mixed, w)
        mu = jnp.mean(h, axis=1, keepdims=True)
        hc = h - mu
        var = jnp.mean(hc * hc, axis=1, keepdims=True)
        rstd = lax.rsqrt(var + LN_EPS)
        xhat = hc * rstd
        g = g_ref[...]
        err = xhat * g + b_ref[...] - t_ref[...]
        lacc_ref[...] += jnp.broadcast_to(jnp.sum(err * err, axis=0, keepdims=True), lacc_ref.shape)
        dout = err * (1.0 / D_MODEL)
        gg_ref[...] += jnp.broadcast_to(jnp.sum(dout * xhat, axis=0, keepdims=True), gg_ref.shape)
        gb_ref[...] += jnp.broadcast_to(jnp.sum(dout, axis=0, keepdims=True), gb_ref.shape)
        dxh = dout * g
        m1 = jnp.mean(dxh, axis=1, keepdims=True)
        m2 = jnp.mean(dxh * xhat, axis=1, keepdims=True)
        dh = rstd * (dxh - m1 - xhat * m2)
        dh_ref[...] = dh
        dy = dh.astype(BF16)
        gwo_ref[...] += _dot(mixed32.T.astype(BF16), dy)
        dmix = _dot_nt(dy, w)
        do = dmix * silu
        do_ref[...] = do.astype(BF16)
        dz_ref[...] = (dmix * o * (sig * (1.0 + zz * (1.0 - sig)))).astype(BF16)
        r = lax.broadcasted_iota(jnp.int32, (D_MODEL, LANES), 0) // HEAD_DIM
        c = lax.broadcasted_iota(jnp.int32, (D_MODEL, LANES), 1)
        pick = jnp.where(r == c, 1.0, 0.0).astype(BF16)
        dl_ref[...] = _exact_dot(pick, do * o, False)

        @pl.when(step == nt - 1)
        def _():
            tot = jnp.sum(lacc_ref[0:1, :]) * (0.5 / D_MODEL)
            loss_ref[...] = jnp.broadcast_to(tot, loss_ref.shape)

    row = lambda i: (i, 0)
    fixed = lambda i: (0, 0)
    wide = pl.BlockSpec((tm, D_MODEL), row)
    half = pl.BlockSpec((tm, FOX_W), row)
    return pl.pallas_call(
        kern, name="post",
        grid=(nt,),
        in_specs=[wide, wide, half, half, wide,
                  pl.BlockSpec((D_MODEL, D_MODEL), fixed),
                  pl.BlockSpec((1, D_MODEL), fixed),
                  pl.BlockSpec((1, D_MODEL), fixed)],
        out_specs=[pl.BlockSpec((8, LANES), fixed), wide,
                   _resident((D_MODEL, D_MODEL), fixed), wide, wide,
                   pl.BlockSpec((tm, LANES), row),
                   pl.BlockSpec((8, D_MODEL), fixed), pl.BlockSpec((8, D_MODEL), fixed)],
        out_shape=[jax.ShapeDtypeStruct((8, LANES), F32),
                   jax.ShapeDtypeStruct((s, D_MODEL), F32),
                   jax.ShapeDtypeStruct((D_MODEL, D_MODEL), F32),
                   jax.ShapeDtypeStruct((s, D_MODEL), BF16),
                   jax.ShapeDtypeStruct((s, D_MODEL), BF16),
                   jax.ShapeDtypeStruct((s, LANES), F32),
                   jax.ShapeDtypeStruct((8, D_MODEL), F32),
                   jax.ShapeDtypeStruct((8, D_MODEL), F32)],
        scratch_shapes=[pltpu.VMEM((8, D_MODEL), F32)],
        compiler_params=_cparams(("arbitrary",)),
    )(x, target, o_fox, o_swa, z, w_o, ln_g, ln_b)


def _adamw_math(w, g, m, v):
    m = ADAM_B1 * m + (1.0 - ADAM_B1) * g
    v = ADAM_B2 * v + (1.0 - ADAM_B2) * (g * g)
    m_hat = m / (1.0 - ADAM_B1 ** ADAM_STEP)
    v_hat = v / (1.0 - ADAM_B2 ** ADAM_STEP)
    delta = -ADAM_LR * (m_hat / (jnp.sqrt(v_hat) + ADAM_EPS) + ADAM_WD * w)
    return delta, m, v


def _adamw(w, g, m, v, *, name):
    r, c = w.shape
    tr = min(256, r)

    def kern(w_ref, g_ref, m_ref, v_ref, d_ref, mo_ref, vo_ref):
        d, mn, vn = _adamw_math(w_ref[...], g_ref[...], m_ref[...], v_ref[...])
        d_ref[...] = d
        mo_ref[...] = mn
        vo_ref[...] = vn

    blk = pl.BlockSpec((tr, c), lambda i: (i, 0))
    sds = jax.ShapeDtypeStruct((r, c), F32)
    return pl.pallas_call(
        kern, name=name,
        grid=(r // tr,),
        in_specs=[blk, blk, blk, blk],
        out_specs=[blk, blk, blk],
        out_shape=[sds, sds, sds],
        compiler_params=_cparams(("parallel",)),
    )(w, g, m, v)


def _adamw_cols(w, g, m, v, *, name):
    c, _, r = w.shape
    tc = 139
    assert c % tc == 0

    def kern(w_ref, g_ref, m_ref, v_ref, go_ref, d_ref, mo_ref, vo_ref):
        g = g_ref[...]
        d, mn, vn = _adamw_math(w_ref[...], g, m_ref[...], v_ref[...])
        go_ref[...] = g
        d_ref[...] = d
        mo_ref[...] = mn
        vo_ref[...] = vn

    blk = pl.BlockSpec((tc, 1, r), lambda i: (i, 0, 0))
    sds = jax.ShapeDtypeStruct((c, 1, r), F32)
    return pl.pallas_call(
        kern, name=name,
        grid=(c // tc,),
        in_specs=[blk, blk, blk, blk],
        out_specs=[blk, blk, blk, blk],
        out_shape=[sds, sds, sds, sds],
        compiler_params=_cparams(("parallel",)),
    )(w, g, m, v)


def _position():
    x, y, c = lax.axis_index("x"), lax.axis_index("y"), lax.axis_index("c")
    chips = [(1 - x, y), (x, 1 - y), (1 - x, 1 - y)]
    return x, y, c, chips


def _chip_index(cx, cy):
    return 2 * cx + cy


def _gather_weights(w_in_bf, w_o_bf):
    shards = (w_in_bf, w_o_bf)
    n_arr = len(shards)

    def kern(*refs):
        ins, outs = refs[:n_arr], refs[n_arr:2 * n_arr]
        send_sems, recv_sems, local_sems = refs[2 * n_arr:]
        x, y, c, chips = _position()
        me = _chip_index(x, y)
        sibling = (x, y, 1 - c)

        local = [pltpu.make_async_copy(ins[a], outs[a].at[me], local_sems.at[a]) for a in range(n_arr)]
        for cp in local:
            cp.start()

        def half(ref, a):
            rows = shards[a].shape[0] // 2
            return ref.at[pl.ds(c * rows, rows), :]

        def copy(a, k, src, slot, to):
            return pltpu.make_async_remote_copy(
                src_ref=src, dst_ref=half(outs[a].at[slot], a),
                send_sem=send_sems.at[a * 6 + k], recv_sem=recv_sems.at[a * 6 + k],
                device_id=to, device_id_type=MESH)

        first = [copy(a, j, half(ins[a], a), me, (*chip, c)) for a in range(n_arr) for j, chip in enumerate(chips)]
        for cp in first:
            cp.start()
        passed = []
        for a in range(n_arr):
            for j, chip in enumerate(chips):
                slot = _chip_index(*chip)
                copy(a, j, half(ins[a], a), slot, (*chip, c)).wait_recv()
                fwd = copy(a, 3 + j, half(outs[a].at[slot], a), slot, sibling)
                fwd.start()
                passed.append(fwd)
        for a in range(n_arr):
            for j, chip in enumerate(chips):
                slot = _chip_index(*chip)
                rows = shards[a].shape[0] // 2
                dst = outs[a].at[slot].at[pl.ds((1 - c) * rows, rows), :]
                pltpu.make_async_remote_copy(
                    src_ref=dst, dst_ref=dst, send_sem=send_sems.at[a * 6 + 3 + j],
                    recv_sem=recv_sems.at[a * 6 + 3 + j], device_id=sibling, device_id_type=MESH).wait_recv()
        for cp in first + passed:
            cp.wait_send()
        for cp in local:
            cp.wait()

    vmem = pl.BlockSpec(memory_space=pltpu.VMEM)
    return pl.pallas_call(
        kern, name="gather_weights",
        in_specs=[vmem] * n_arr,
        out_specs=[vmem] * n_arr,
        out_shape=[jax.ShapeDtypeStruct((N_CHIPS,) + w.shape, w.dtype) for w in shards],
        scratch_shapes=[pltpu.SemaphoreType.DMA((6 * n_arr,)),
                        pltpu.SemaphoreType.DMA((6 * n_arr,)),
                        pltpu.SemaphoreType.DMA((n_arr,))],
        compiler_params=_cparams(),
    )(*shards)


def _pair_reduce(grads):
    n_arr = len(grads)
    chunk = 128

    def kern(*refs):
        ins = refs[:n_arr]
        outs = refs[n_arr:2 * n_arr]
        gots = refs[2 * n_arr:3 * n_arr]
        send_sems, recv_sems = refs[3 * n_arr:]
        x, y, c, _ = _position()
        sibling = (x, y, 1 - c)
        copies = []
        for a in range(n_arr):
            rows = grads[a].shape[1] // 2
            copies.append(pltpu.make_async_remote_copy(
                src_ref=ins[a].at[:, pl.ds((1 - c) * rows, rows), :], dst_ref=gots[a],
                send_sem=send_sems.at[a], recv_sem=recv_sems.at[a], device_id=sibling, device_id_type=MESH))
        for cp in copies:
            cp.start()
        for a in range(n_arr):
            copies[a].wait()
            rows = grads[a].shape[1] // 2
            for j in range(N_CHIPS):
                for r0 in range(0, rows, chunk):
                    mine = ins[a][j, pl.ds(pl.multiple_of(c * rows + r0, chunk), chunk), :]
                    outs[a][j, r0:r0 + chunk, :] = (mine + gots[a][j, r0:r0 + chunk, :]).astype(BF16)

    vmem = pl.BlockSpec(memory_space=pltpu.VMEM)
    half = [(N_CHIPS, g.shape[1] // 2, g.shape[2]) for g in grads]
    return pl.pallas_call(
        kern, name="pair_reduce",
        in_specs=[vmem] * n_arr,
        out_specs=[vmem] * n_arr,
        out_shape=[jax.ShapeDtypeStruct(h, BF16) for h in half],
        scratch_shapes=[pltpu.VMEM(h, F32) for h in half]
        + [pltpu.SemaphoreType.DMA((n_arr,)), pltpu.SemaphoreType.DMA((n_arr,))],
        compiler_params=_cparams(),
    )(*grads)


def _chip_reduce(parts):
    n_arr = len(parts)
    chunk = 128

    def kern(*refs):
        ins = refs[:n_arr]
        outs = refs[n_arr:2 * n_arr]
        slabs = refs[2 * n_arr:3 * n_arr]
        sums = refs[3 * n_arr:4 * n_arr]
        send_sems, recv_sems, local_sems, swap_send, swap_recv, swap_local = refs[4 * n_arr:]
        x, y, c, chips = _position()
        me = _chip_index(x, y)
        sibling = (x, y, 1 - c)
        local = [pltpu.make_async_copy(ins[a].at[me], slabs[a].at[me], local_sems.at[a]) for a in range(n_arr)]
        for cp in local:
            cp.start()
        sends = []
        for a in range(n_arr):
            for j, chip in enumerate(chips):
                sends.append(pltpu.make_async_remote_copy(
                    src_ref=ins[a].at[_chip_index(*chip)], dst_ref=slabs[a].at[me],
                    send_sem=send_sems.at[a * 3 + j], recv_sem=recv_sems.at[a * 3 + j],
                    device_id=(*chip, c), device_id_type=MESH))
        for cp in sends:
            cp.start()
        for a in range(n_arr):
            for j, chip in enumerate(chips):
                slot = slabs[a].at[_chip_index(*chip)]
                pltpu.make_async_remote_copy(
                    src_ref=slot, dst_ref=slot, send_sem=send_sems.at[a * 3 + j],
                    recv_sem=recv_sems.at[a * 3 + j], device_id=(*chip, c), device_id_type=MESH).wait_recv()
        for cp in sends:
            cp.wait_send()
        for cp in local:
            cp.wait()
        for a in range(n_arr):
            for r0 in range(0, parts[a].shape[1], chunk):
                f = lambda j: slabs[a][j, r0:r0 + chunk, :].astype(F32)
                sums[a][r0:r0 + chunk, :] = ((f(0) + f(1)) + f(2)) + f(3)
        swap_l, swap_r = [], []
        for a in range(n_arr):
            rows = parts[a].shape[1]
            mine = outs[a].at[pl.ds(c * rows, rows), :]
            swap_l.append(pltpu.make_async_copy(sums[a], mine, swap_local.at[a]))
            swap_r.append(pltpu.make_async_remote_copy(
                src_ref=sums[a], dst_ref=mine, send_sem=swap_send.at[a], recv_sem=swap_recv.at[a],
                device_id=sibling, device_id_type=MESH))
        for cp in swap_l + swap_r:
            cp.start()
        for a in range(n_arr):
            rows = parts[a].shape[1]
            theirs = outs[a].at[pl.ds((1 - c) * rows, rows), :]
            pltpu.make_async_remote_copy(
                src_ref=theirs, dst_ref=theirs, send_sem=swap_send.at[a], recv_sem=swap_recv.at[a],
                device_id=sibling, device_id_type=MESH).wait_recv()
        for cp in swap_r:
            cp.wait_send()
        for cp in swap_l:
            cp.wait()

    vmem = pl.BlockSpec(memory_space=pltpu.VMEM)
    return pl.pallas_call(
        kern, name="chip_reduce",
        in_specs=[vmem] * n_arr,
        out_specs=[vmem] * n_arr,
        out_shape=[jax.ShapeDtypeStruct((2 * p.shape[1], p.shape[2]), F32) for p in parts],
        scratch_shapes=[pltpu.VMEM(p.shape, BF16) for p in parts]
        + [pltpu.VMEM(p.shape[1:], F32) for p in parts]
        + [pltpu.SemaphoreType.DMA((3 * n_arr,)),
           pltpu.SemaphoreType.DMA((3 * n_arr,)),
           pltpu.SemaphoreType.DMA((n_arr,)),
           pltpu.SemaphoreType.DMA((n_arr,)),
           pltpu.SemaphoreType.DMA((n_arr,)),
           pltpu.SemaphoreType.DMA((n_arr,))],
        compiler_params=_cparams(),
    )(*parts)


def _small_allreduce_adamw(partials, params, moms, vels):
    chunks = D_MODEL // LANES
    row_rb, row_bf, row_sk, row_loss = 2 * chunks, 2 * chunks + NUM_BUCKETS, 2 * chunks + NUM_BUCKETS + 1, SMALL_ROWS - 6

    def kern(gbf_ref, grb_ref, gsk_ref, gg_ref, gb_ref, loss_ref, *refs):
        p_refs, m_refs, v_refs = refs[0:5], refs[5:10], refs[10:15]
        lo_ref, g_outs, d_outs, mo_outs, vo_outs = refs[15], refs[16:21], refs[21:26], refs[26:31], refs[31:36]
        send_ref, buf_ref, send_sems, recv_sems = refs[36:]
        x, y, c, _ = _position()
        me = 4 * x + 2 * y + c
        send_ref[...] = jnp.zeros_like(send_ref)
        for r in range(chunks):
            send_ref[r:r + 1, :] = gg_ref[0:1, r * LANES:(r + 1) * LANES]
            send_ref[chunks + r:chunks + r + 1, :] = gb_ref[0:1, r * LANES:(r + 1) * LANES]
        send_ref[row_rb:row_rb + NUM_BUCKETS, :] = grb_ref[...]
        send_ref[row_bf:row_bf + 1, :] = gbf_ref[0:1, :]
        send_ref[row_sk:row_sk + 1, :] = gsk_ref[0:1, :]
        send_ref[row_loss:row_loss + 1, :] = loss_ref[0:1, :]
        buf_ref[me] = send_ref[...]
        peers = [(x, y, 1 - c)] + [(px, py, pc) for px, py in _position()[3] for pc in (c, 1 - c)]
        sends = []
        for k, peer in enumerate(peers):
            sends.append(pltpu.make_async_remote_copy(
                src_ref=send_ref, dst_ref=buf_ref.at[me], send_sem=send_sems.at[k], recv_sem=recv_sems.at[k],
                device_id=peer, device_id_type=MESH))
        for cp in sends:
            cp.start()
        for k, (px, py, pc) in enumerate(peers):
            slot = buf_ref.at[4 * px + 2 * py + pc]
            pltpu.make_async_remote_copy(
                src_ref=slot, dst_ref=slot, send_sem=send_sems.at[k], recv_sem=recv_sems.at[k],
                device_id=(px, py, pc), device_id_type=MESH).wait_recv()
        for cp in sends:
            cp.wait_send()
        tot = buf_ref[0]
        for d in range(1, N_DEV):
            tot = tot + buf_ref[d]
        lo_ref[...] = tot[row_loss:row_loss + 1, :]
        grads = [tot[row_bf:row_bf + 1, 0:FOX_HEADS],
                 tot[row_rb:row_rb + NUM_BUCKETS, 0:SWA_HEADS],
                 tot[row_sk:row_sk + 1, 0:SWA_HEADS],
                 jnp.concatenate([tot[r:r + 1, :] for r in range(chunks)], axis=1),
                 jnp.concatenate([tot[chunks + r:chunks + r + 1, :] for r in range(chunks)], axis=1)]
        for i, g in enumerate(grads):
            g_outs[i][...] = g
            delta, mn, vn = _adamw_math(p_refs[i][...], g, m_refs[i][...], v_refs[i][...])
            d_outs[i][...] = delta
            mo_outs[i][...] = mn
            vo_outs[i][...] = vn

    vm = pl.BlockSpec(memory_space=pltpu.VMEM)
    shapes = [jax.ShapeDtypeStruct(p.shape, F32) for p in params]
    outs = pl.pallas_call(
        kern, name="small_allreduce_adamw",
        in_specs=[vm] * 21,
        out_specs=[vm] * 21,
        out_shape=[jax.ShapeDtypeStruct((1, LANES), F32)] + shapes * 4,
        scratch_shapes=[pltpu.VMEM((SMALL_ROWS, LANES), F32),
                        pltpu.VMEM((N_DEV, SMALL_ROWS, LANES), F32),
                        pltpu.SemaphoreType.DMA((N_DEV - 1,)),
                        pltpu.SemaphoreType.DMA((N_DEV - 1,))],
    )(*partials, *params, *moms, *vels)
    return outs[0], outs[1:6], outs[6:11], outs[11:16], outs[16:21]


def _to_padded_cols(w):
    pad = jnp.zeros((w.shape[0], N_C - FOX_HEADS), w.dtype)
    return jnp.concatenate([w[:, 0:1536], w[:, 2056:2824], w[:, 1536:1544], pad,
                            w[:, 1544:2056], w[:, 2824:3336]], axis=1)


def _from_padded_cols(g):
    return jnp.concatenate([g[:, 0:1536], g[:, OFF_C:OFF_C + FOX_HEADS], g[:, OFF_B:OFF_B + FOX_W],
                            g[:, 1536:N_A], g[:, OFF_B + FOX_W:N_PAD]], axis=1)


def _fox_rows(a):
    return a[:, :FOX_HEADS].T.reshape(FOX_HEADS, 1, a.shape[0])


def kernel(x, w_in, b_f, rel_bias, sink, w_o, ln_g, ln_b, loss_target, m_w_in, m_b_f, m_rel_bias, m_sink, m_w_o, m_ln_g, m_ln_b, v_w_in, v_b_f, v_rel_bias, v_sink, v_w_o, v_ln_g, v_ln_b):
    x2 = x[0]
    tgt = loss_target[0]
    s = x2.shape[0]
    w_in2, w_o2 = w_in[0], w_o[0]

    shard_cols = D_IN // N_CHIPS
    col_pad = ((0, 0), (0, SHARD_PAD - shard_cols))
    w_in_all, w_o_all = _gather_weights(jnp.pad(w_in2.astype(BF16), col_pad), w_o2.astype(BF16))
    w_full = jnp.concatenate([w_in_all[j, :, :shard_cols] for j in range(N_CHIPS)], axis=1)
    w_pad = _to_padded_cols(w_full)
    w_o_full = w_o_all.reshape(D_MODEL, D_MODEL)

    qkv, ffp, z, xt, vt = _project(x2, w_pad)
    bfp = jnp.pad(b_f, ((0, 0), (0, LANES - FOX_HEADS)))
    cum = _cum_fwd(ffp, bfp)
    cum_t3 = _fox_rows(cum)
    o_fox, lse_t3 = _fox_fwd(qkv, vt, cum_t3, cum)
    bucket_t = jnp.asarray(_bucket_table().T)
    bias_t = _swa_bias(rel_bias, bucket_t)
    sink_rows = jnp.repeat(sink.reshape(SWA_KV_HEADS, SWA_GROUP, 1), BLOCK, axis=2).reshape(SWA_KV_HEADS, 1, SWA_LANES)
    o_swa, lse_swa = _swa_fwd(qkv, bias_t, sink_rows)

    loss8, dh, grad_w_o_full, do_bf, dz, delta, gg8, gb8 = _post(
        x2, tgt, o_fox, o_swa, z, w_o_full, ln_g, ln_b)

    delta_t3 = _fox_rows(delta)
    dq_fox, dk_fox, dv_fox, dcum_k, dcum_q = _fox_bwd(qkv, do_bf, cum_t3, cum, lse_t3, delta_t3)
    dcum_q = jnp.pad(dcum_q.reshape(FOX_HEADS, s).T, ((0, 0), (0, LANES - FOX_HEADS)))
    dff, gbf8 = _cum_bwd(dcum_k, dcum_q, ffp, bfp)
    delta_rows = (delta[:, FOX_HEADS:FOX_HEADS + SWA_HEADS].reshape(s // BLOCK, BLOCK, SWA_KV_HEADS, SWA_GROUP)
                  .transpose(0, 2, 3, 1).reshape(s // BLOCK, SWA_KV_HEADS, 1, SWA_LANES))
    dq_swa, dk_swa, dv_swa, grb, gsk8 = _swa_bwd(qkv, do_bf, delta_rows, lse_swa, bias_t, sink_rows, bucket_t)

    d_misc = jnp.concatenate([dk_swa, dv_swa, dff], axis=1)
    pieces = [dq_fox, dk_fox, dv_fox, dq_swa, d_misc, dz]
    grad_x = _grad_x_matmul(pieces, w_pad, dh, tm=512, tn=D_MODEL, name="grad_x")
    blocks = [(p, 0) for p in pieces[:-1]] + [(dz, 0), (dz, 1)]
    grad_w_pad = _grad_w_matmul(xt, blocks, tk=1024, name="grad_w_in")
    grad_w_in_full = _from_padded_cols(grad_w_pad)

    g_in4 = jnp.stack([jnp.pad(grad_w_in_full[:, j * shard_cols:(j + 1) * shard_cols], col_pad)
                       for j in range(N_CHIPS)])
    g_o4 = grad_w_o_full.reshape(N_CHIPS, D_MODEL // N_CHIPS, D_MODEL)
    g_w_in, g_w_o = _chip_reduce(_pair_reduce([g_in4, g_o4]))
    g_w_in = g_w_in[:, :shard_cols]

    cols_first = lambda a: jnp.transpose(a, (2, 0, 1))
    rows_first = lambda a: jnp.transpose(a, (1, 2, 0))
    g_w_in, d_w_in, nm_w_in, nv_w_in = [rows_first(a) for a in _adamw_cols(
        cols_first(w_in), cols_first(g_w_in[None]), cols_first(m_w_in), cols_first(v_w_in), name="adamw_w_in")]
    d_w_o, nm_w_o, nv_w_o = _adamw(w_o2, g_w_o, m_w_o[0], v_w_o[0], name="adamw_w_o")

    loss_row, gs, ds, ms, vs = _small_allreduce_adamw(
        [gbf8, grb, gsk8, gg8, gb8, loss8],
        [b_f, rel_bias, sink, ln_g, ln_b],
        [m_b_f, m_rel_bias, m_sink, m_ln_g, m_ln_b],
        [v_b_f, v_rel_bias, v_sink, v_ln_g, v_ln_b])
    loss = loss_row[0, 0]
    g_bf, g_rb, g_sk, g_lg, g_lb = gs
    d_bf, d_rb, d_sk, d_lg, d_lb = ds
    m_bf, m_rb, m_sk, m_lg, m_lb = ms
    v_bf, v_rb, v_sk, v_lg, v_lb = vs

    e = lambda a: a[None]
    return (loss, e(grad_x),
            g_w_in, g_bf, g_rb, g_sk, e(g_w_o), g_lg, g_lb,
            d_w_in, d_bf, d_rb, d_sk, e(d_w_o), d_lg, d_lb,
            nm_w_in, m_bf, m_rb, m_sk, e(nm_w_o), m_lg, m_lb,
            nv_w_in, v_bf, v_rb, v_sk, e(nv_w_o), v_lg, v_lb)
```

```python
import functools
import math

import numpy as np
import jax
import jax.numpy as jnp
from jax import lax
from jax.experimental import pallas as pl
from jax.experimental.pallas import tpu as pltpu

F32 = jnp.float32
BF16 = jnp.bfloat16

D_MODEL = 1024
HEAD_DIM = 64
FOX_HEADS = 8
SWA_HEADS = 8
SWA_KV_HEADS = 2
SWA_GROUP = 4
FOX_W = 512
SWA_W = 512
BLOCK = 128
NUM_BUCKETS = 32
MAX_DISTANCE = 128
LN_EPS = 1e-5
NEG = -1e30
ALPHA = 2.0 ** 0.25
QK_SCALE = 0.125

ADAM_LR = 0.001
ADAM_B1 = 0.9
ADAM_B2 = 0.999
ADAM_EPS = 1e-08
ADAM_WD = 0.01
ADAM_STEP = 10

D_IN = 3336
SHARD_PAD = 896
N_A = 2304
N_C = 256
N_B = 1024
OFF_C = N_A
OFF_B = N_A + N_C
N_PAD = N_A + N_C + N_B
COL_FK, COL_FV, COL_SQ, COL_SK, COL_SV = 512, 1024, 1536, 2048, 2176

LANES = 128
FOX_T = 256
FOX_REF = 512
SUM_ROWS = 16
VMEM_LIMIT = 56 * 1024 * 1024

MESH = pl.DeviceIdType.MESH
N_CHIPS = 4
N_DEV = 8
SMALL_ROWS = 56


def _cparams(sem=None):
    return pltpu.CompilerParams(dimension_semantics=sem, vmem_limit_bytes=VMEM_LIMIT)


def _split3(x):
    hi = x.astype(BF16)
    r = x - hi.astype(F32)
    mid = r.astype(BF16)
    lo = (r - mid.astype(F32)).astype(BF16)
    return hi, mid, lo


def _dot(a, b):
    return jnp.dot(a, b, preferred_element_type=F32)


def _dot_nt(a, b):
    return lax.dot_general(a, b, (((1,), (1,)), ((), ())), preferred_element_type=F32)


def _project(x, w_pad):
    s, k = x.shape
    tm = 512
    chunk = 512

    def kern(x_ref, w_ref, qkv_ref, ff_ref, z_ref, xt_ref, vt_ref):
        xf = x_ref[...]
        xb = xf.astype(BF16)
        xt_ref[...] = xf.T.astype(BF16)
        for c0 in range(0, N_A, chunk):
            width = min(chunk, N_A - c0)
            res = _dot(xb, w_ref[:, c0:c0 + width])
            qkv_ref[:, c0:c0 + width] = res.astype(BF16)
            if c0 == COL_FV:
                vt_ref[...] = res.T.astype(BF16)
        ff_ref[...] = _dot(xb, w_ref[:, OFF_C:OFF_C + N_C])
        for c0 in range(0, N_B, 512):
            z_ref[:, c0:c0 + 512] = _dot(xb, w_ref[:, OFF_B + c0:OFF_B + c0 + 512])

    row = lambda i: (i, 0)
    return pl.pallas_call(
        kern, name="project",
        grid=(s // tm,),
        in_specs=[pl.BlockSpec((tm, k), row),
                  _resident((k, N_PAD), lambda i: (0, 0))],
        out_specs=[pl.BlockSpec((tm, N_A), row),
                   pl.BlockSpec((tm, N_C), row),
                   pl.BlockSpec((tm, N_B), row),
                   pl.BlockSpec((k, tm), lambda i: (0, i)),
                   pl.BlockSpec((FOX_W, tm), lambda i: (0, i))],
        out_shape=[jax.ShapeDtypeStruct((s, N_A), BF16),
                   jax.ShapeDtypeStruct((s, N_C), F32),
                   jax.ShapeDtypeStruct((s, N_B), F32),
                   jax.ShapeDtypeStruct((k, s), BF16),
                   jax.ShapeDtypeStruct((FOX_W, s), BF16)],
        compiler_params=_cparams(("parallel",)),
    )(x, w_pad)


def _grad_x_matmul(pieces, w_pad, dh, token, *, tm, tn, name):
    m = dh.shape[0]
    n, k = w_pad.shape
    widths = [p.shape[1] for p in pieces]
    offs = [sum(widths[:i]) for i in range(len(pieces))]
    assert sum(widths) == k

    def kern(*refs):
        p_refs, (b_ref, dh_ref, _, o_ref) = refs[:len(pieces)], refs[len(pieces):]
        acc = ALPHA * dh_ref[...]
        for p_ref, off, width in zip(p_refs, offs, widths):
            acc = acc + _dot_nt(p_ref[...], b_ref[:, off:off + width])
        o_ref[...] = acc

    assert tn == n
    return pl.pallas_call(
        kern, name=name,
        grid=(m // tm,),
        in_specs=[pl.BlockSpec((tm, w), lambda i: (i, 0)) for w in widths]
        + [_resident((n, k), lambda i: (0, 0)),
           pl.BlockSpec((tm, n), lambda i: (i, 0)),
           _resident(token.shape, lambda i: (0, 0))],
        out_specs=pl.BlockSpec((tm, n), lambda i: (i, 0)),
        out_shape=jax.ShapeDtypeStruct((m, n), F32),
        compiler_params=_cparams(("parallel",)),
    )(*pieces, w_pad, dh, token)


def _grad_w_matmul(xt, blocks, *, tk, name):
    m, s = xt.shape
    tn = 512
    nb = len(blocks)

    def kern(a_ref, *refs):
        b_refs, o_ref = refs[:nb], refs[nb]

        @pl.when(pl.program_id(0) == 0)
        def _():
            o_ref[...] = jnp.zeros_like(o_ref)
        a = a_ref[...]
        for blk in range(nb):
            o_ref[:, blk * tn:(blk + 1) * tn] += _dot(a, b_refs[blk][...])

    return pl.pallas_call(
        kern, name=name,
        grid=(s // tk,),
        in_specs=[pl.BlockSpec((m, tk), lambda k: (0, k))]
        + [pl.BlockSpec((tk, tn), functools.partial(lambda k, col: (k, col), col=col)) for _, col in blocks],
        out_specs=_resident((m, nb * tn), lambda k: (0, 0)),
        out_shape=jax.ShapeDtypeStruct((m, nb * tn), F32),
        compiler_params=_cparams(("arbitrary",)),
    )(xt, *[arr for arr, _ in blocks])


def _tri(n, lower):
    r = lax.broadcasted_iota(jnp.int32, (n, n), 0)
    c = lax.broadcasted_iota(jnp.int32, (n, n), 1)
    keep = (c <= r) if lower else (c >= r)
    return jnp.where(keep, 1.0, 0.0).astype(BF16)


def _exact_dot(mat_bf16, x_f32, left):
    out = None
    for piece in _split3(x_f32):
        t = _dot(mat_bf16, piece) if left else _dot(piece, mat_bf16)
        out = t if out is None else out + t
    return out


def _log_sigmoid(z):
    return jnp.minimum(z, 0.0) - jnp.log(1.0 + jnp.exp(-jnp.abs(z)))


def _cum_fwd(ffp, bfp):
    s = ffp.shape[0]
    t = min(1024, s)

    def kern(ff_ref, b_ref, cum_ref, carry_ref):
        @pl.when(pl.program_id(0) == 0)
        def _():
            carry_ref[...] = jnp.zeros_like(carry_ref)
        lane = lax.broadcasted_iota(jnp.int32, (1, LANES), 1)
        lf = _log_sigmoid(ff_ref[...] + b_ref[...])
        lf = jnp.where(lane < FOX_HEADS, lf, 0.0)
        cum = _exact_dot(_tri(t, True), lf, True) + carry_ref[0:1, :]
        cum_ref[...] = cum
        carry_ref[...] = jnp.broadcast_to(cum[t - 1:t, :], carry_ref.shape)

    return pl.pallas_call(
        kern, name="cum_fwd",
        grid=(s // t,),
        in_specs=[pl.BlockSpec((t, LANES), lambda i: (i, 0)),
                  pl.BlockSpec((1, LANES), lambda i: (0, 0))],
        out_specs=pl.BlockSpec((t, LANES), lambda i: (i, 0)),
        out_shape=jax.ShapeDtypeStruct((s, LANES), F32),
        scratch_shapes=[pltpu.VMEM((8, LANES), F32)],
        compiler_params=_cparams(("arbitrary",)),
    )(ffp, bfp)


def _cum_bwd(dcum_k, dcum_q, ffp, bfp):
    s = dcum_k.shape[0]
    t = min(1024, s)
    nb = s // t

    def kern(dck_ref, dcq_ref, ff_ref, b_ref, dff_ref, gb_ref, carry_ref):
        @pl.when(pl.program_id(0) == 0)
        def _():
            carry_ref[...] = jnp.zeros_like(carry_ref)
            gb_ref[...] = jnp.zeros_like(gb_ref)
        lane = lax.broadcasted_iota(jnp.int32, (1, LANES), 1)
        dlf = _exact_dot(_tri(t, False), dck_ref[...] + dcq_ref[...], True) + carry_ref[0:1, :]
        carry_ref[...] = jnp.broadcast_to(dlf[0:1, :], carry_ref.shape)
        z = ff_ref[...] + b_ref[...]
        dff = jnp.where(lane < FOX_HEADS, dlf / (1.0 + jnp.exp(z)), 0.0)
        gb_ref[...] += jnp.broadcast_to(jnp.sum(dff, axis=0, keepdims=True), gb_ref.shape)
        dff_ref[...] = jnp.concatenate([dff, jnp.zeros_like(dff)], axis=1).astype(BF16)

    return pl.pallas_call(
        kern, name="cum_bwd",
        grid=(nb,),
        in_specs=[pl.BlockSpec((t, LANES), lambda i: (nb - 1 - i, 0)),
                  pl.BlockSpec((t, LANES), lambda i: (nb - 1 - i, 0)),
                  pl.BlockSpec((t, LANES), lambda i: (nb - 1 - i, 0)),
                  pl.BlockSpec((1, LANES), lambda i: (0, 0))],
        out_specs=[pl.BlockSpec((t, N_C), lambda i: (nb - 1 - i, 0)),
                   pl.BlockSpec((8, LANES), lambda i: (0, 0))],
        out_shape=[jax.ShapeDtypeStruct((s, N_C), BF16),
                   jax.ShapeDtypeStruct((8, LANES), F32)],
        scratch_shapes=[pltpu.VMEM((8, LANES), F32)],
        compiler_params=_cparams(("arbitrary",)),
    )(dcum_k, dcum_q, ffp, bfp)


def _resident(shape, index_map):
    return pl.BlockSpec(shape, index_map, pipeline_mode=pl.Buffered(1))


def _fox_fwd(qkv, vt, cum_t3, cum):
    s = qkv.shape[0]
    tk = tq = FOX_REF
    nq = s // tq
    nh = FOX_HEADS
    diag_tiles = tq // tk

    def kern(q_ref, k_ref, vt_ref, ct_ref, c_ref, o_ref, lse_ref, m_ref, acc_ref, u_ref):
        i = pl.program_id(0)
        lane = lax.broadcasted_iota(jnp.int32, (1, LANES), 1)
        krow = lax.broadcasted_iota(jnp.int32, (tk, tq), 0)
        qcol = lax.broadcasted_iota(jnp.int32, (tk, tq), 1)
        q0 = pl.multiple_of(i * tq, tq)
        qts, crefs = [], []
        for h in range(nh):
            p, a = divmod(h, 2)
            q2 = q_ref[:, p * LANES:(p + 1) * LANES] * jnp.asarray(QK_SCALE, BF16)
            sel = (lane < HEAD_DIM) if a == 0 else (lane >= HEAD_DIM)
            qts.append(jnp.where(sel, q2, jnp.zeros_like(q2)).astype(F32).T.astype(BF16))
            crefs.append(ct_ref[h, :, pl.ds(q0, LANES)][:, 0:1])
        m_ref[...] = jnp.full(m_ref.shape, NEG, F32)
        acc_ref[...] = jnp.zeros_like(acc_ref)
        ones = jnp.ones((SUM_ROWS, tk), BF16)

        def tile(j, diag):
            k0 = pl.multiple_of(j * tk, tk)
            cb = c_ref[pl.ds(k0, tk), :]
            sts = [_dot(k_ref[pl.ds(k0, tk), (h // 2) * LANES:(h // 2 + 1) * LANES], qts[h]) for h in range(nh)]
            tile_max = []
            for h in range(nh):
                u = sts[h] - (cb[:, h:h + 1] - crefs[h])
                if diag is not None:
                    u = jnp.where(krow + diag * tk <= qcol, u, NEG)
                u_ref[h] = u
                tile_max.append(jnp.max(u, axis=0, keepdims=True))
            pts, scales = [], []
            for h in range(nh):
                m_old = m_ref[h]
                m_new = jnp.maximum(m_old, tile_max[h])
                scales.append(jnp.exp(m_old - m_new))
                pts.append(jnp.exp(u_ref[h] - m_new).astype(BF16))
                m_ref[h] = m_new
            for h in range(nh):
                vth = jnp.concatenate([vt_ref[h * HEAD_DIM:(h + 1) * HEAD_DIM, pl.ds(k0, tk)], ones], axis=0)
                acc_ref[h] = scales[h] * acc_ref[h] + _dot(vth, pts[h])

        def body(j, c):
            tile(j, None)
            return c
        lax.fori_loop(0, i * diag_tiles, body, 0)
        for d in range(diag_tiles):
            tile(i * diag_tiles + d, d)

        ls = [acc_ref[h][HEAD_DIM:HEAD_DIM + 1] for h in range(nh)]
        for p in range(nh // 2):
            ot = jnp.concatenate([acc_ref[2 * p + a][:HEAD_DIM] * (1.0 / ls[2 * p + a]) for a in range(2)], axis=0)
            o_ref[:, p * LANES:(p + 1) * LANES] = ot.T
        for h in range(nh):
            lse_ref[h, :, pl.ds(q0, tq)] = m_ref[h] + jnp.log(ls[h])

    return pl.pallas_call(
        kern, name="fox_fwd",
        grid=(nq,),
        in_specs=[pl.BlockSpec((tq, FOX_W), lambda i: (i, 0)),
                  _resident((s, FOX_W), lambda i: (0, COL_FK // FOX_W)),
                  _resident((FOX_W, s), lambda i: (0, 0)),
                  _resident((nh, 1, s), lambda i: (0, 0, 0)),
                  _resident((s, LANES), lambda i: (0, 0))],
        out_specs=[pl.BlockSpec((tq, FOX_W), lambda i: (i, 0)),
                   pl.BlockSpec((nh, 1, s), lambda i: (0, 0, 0))],
        out_shape=[jax.ShapeDtypeStruct((s, FOX_W), F32),
                   jax.ShapeDtypeStruct((nh, 1, s), F32)],
        scratch_shapes=[pltpu.VMEM((nh, 1, tq), F32),
                        pltpu.VMEM((nh, HEAD_DIM + SUM_ROWS, tq), F32),
                        pltpu.VMEM((nh, tk, tq), F32)],
        compiler_params=_cparams(("arbitrary",)),
    )(qkv, qkv, vt, cum_t3, cum)


def _fox_bwd(qkv, do_bf, cum_t3, cum, lse_t3, delta_t3):
    s = qkv.shape[0]
    t = min(FOX_T, s)
    nq = s // t
    nh = FOX_HEADS
    npair = nh // 2

    def kern(q_ref, do_ref, k_ref, v_ref, ct_ref, c_ref, lse_ref, dl_ref,
             dq_ref, dk_ref, dv_ref, dc_ref, dcq_ref, dqt_ref, accv_ref, acck_ref, accd_ref):
        kj = pl.program_id(0)
        lane = lax.broadcasted_iota(jnp.int32, (1, LANES), 1)
        krow = lax.broadcasted_iota(jnp.int32, (t, t), 0)
        qcol = lax.broadcasted_iota(jnp.int32, (t, t), 1)
        causal = krow <= qcol
        sels = [lane < HEAD_DIM, lane >= HEAD_DIM]

        @pl.when(kj == 0)
        def _():
            dqt_ref[...] = jnp.zeros_like(dqt_ref)
            dcq_ref[...] = jnp.zeros_like(dcq_ref)

        cb = c_ref[...]
        k2s, v2s, kts = [], [], []
        for p in range(npair):
            k2 = k_ref[:, p * LANES:(p + 1) * LANES]
            k2s.append(k2)
            v2s.append(v_ref[:, p * LANES:(p + 1) * LANES])
            kt = k2.astype(F32).T * QK_SCALE
            kts.append(kt[:HEAD_DIM].astype(BF16))
            kts.append(kt[HEAD_DIM:].astype(BF16))
        css = [cb[:, h:h + 1] for h in range(nh)]

        def tile(i, masked):
            q0 = pl.multiple_of(i * t, t)
            r0 = pl.multiple_of((i // (FOX_REF // t)) * FOX_REF, FOX_REF)
            sts, dpts, qms, doms = [], [], [], []
            for h in range(nh):
                p, a = divmod(h, 2)
                qi = q_ref[pl.ds(q0, t), p * LANES:(p + 1) * LANES] * jnp.asarray(QK_SCALE, BF16)
                doi = do_ref[pl.ds(q0, t), p * LANES:(p + 1) * LANES]
                qm = jnp.where(sels[a], qi, jnp.zeros_like(qi))
                dom = jnp.where(sels[a], doi, jnp.zeros_like(doi))
                qms.append(qm)
                doms.append(dom)
                sts.append(_dot_nt(k2s[p], qm))
                dpts.append(_dot_nt(v2s[p], dom))
            pts, dsts = [], []
            for h in range(nh):
                cref = ct_ref[h, :, pl.ds(r0, LANES)][:, 0:1]
                pt = jnp.exp(sts[h] - (css[h] - cref) - lse_ref[h, :, pl.ds(q0, t)])
                if masked:
                    pt = jnp.where(causal, pt, 0.0)
                ds32 = pt * (dpts[h] - dl_ref[h, :, pl.ds(q0, t)])
                part = ds32[:, 0:LANES]
                for c in range(1, t // LANES):
                    part = part + ds32[:, c * LANES:(c + 1) * LANES]
                accd_ref[h] = part if masked else accd_ref[h] + part
                dcq_ref[h, :, pl.ds(q0, t)] += jnp.sum(ds32, axis=0, keepdims=True)
                pts.append(pt.astype(BF16))
                dsts.append(ds32.astype(BF16))
            for p in range(npair):
                ha, hb = 2 * p, 2 * p + 1
                dv_p = _dot(pts[ha], doms[ha]) + _dot(pts[hb], doms[hb])
                dk_p = _dot(dsts[ha], qms[ha]) + _dot(dsts[hb], qms[hb])
                accv_ref[p] = dv_p if masked else accv_ref[p] + dv_p
                acck_ref[p] = dk_p if masked else acck_ref[p] + dk_p
            for h in range(nh):
                dqt_ref[h * HEAD_DIM:(h + 1) * HEAD_DIM, pl.ds(q0, t)] += _dot(kts[h], dsts[h])

        tile(kj, True)

        def body(i, c):
            tile(i, False)
            return c
        lax.fori_loop(kj + 1, nq, body, 0)

        dc = jnp.zeros((t, LANES), F32)
        for h in range(nh):
            dc = jnp.where(lane == h, -jnp.sum(accd_ref[h], axis=1, keepdims=True), dc)
        dc_ref[...] = dc
        for p in range(npair):
            dv_ref[:, p * LANES:(p + 1) * LANES] = accv_ref[p].astype(BF16)
            dk_ref[:, p * LANES:(p + 1) * LANES] = acck_ref[p].astype(BF16)

        @pl.when(kj == nq - 1)
        def _():
            for c0 in range(0, s, t):
                dq_ref[c0:c0 + t, :] = dqt_ref[:, c0:c0 + t].T.astype(BF16)

    whole = lambda kj: (0, 0, 0)
    return pl.pallas_call(
        kern, name="fox_bwd",
        grid=(nq,),
        in_specs=[_resident((s, FOX_W), lambda kj: (0, 0)),
                  _resident((s, FOX_W), lambda kj: (0, 0)),
                  pl.BlockSpec((t, FOX_W), lambda kj: (kj, COL_FK // FOX_W)),
                  pl.BlockSpec((t, FOX_W), lambda kj: (kj, COL_FV // FOX_W)),
                  _resident((nh, 1, s), whole),
                  pl.BlockSpec((t, LANES), lambda kj: (kj, 0)),
                  _resident((nh, 1, s), whole),
                  _resident((nh, 1, s), whole)],
        out_specs=[_resident((s, FOX_W), lambda kj: (0, 0)),
                   pl.BlockSpec((t, FOX_W), lambda kj: (kj, 0)),
                   pl.BlockSpec((t, FOX_W), lambda kj: (kj, 0)),
                   pl.BlockSpec((t, LANES), lambda kj: (kj, 0)),
                   _resident((nh, 1, s), whole)],
        out_shape=[jax.ShapeDtypeStruct((s, FOX_W), BF16),
                   jax.ShapeDtypeStruct((s, FOX_W), BF16),
                   jax.ShapeDtypeStruct((s, FOX_W), BF16),
                   jax.ShapeDtypeStruct((s, LANES), F32),
                   jax.ShapeDtypeStruct((nh, 1, s), F32)],
        scratch_shapes=[pltpu.VMEM((FOX_W, s), F32),
                        pltpu.VMEM((npair, t, LANES), F32),
                        pltpu.VMEM((npair, t, LANES), F32),
                        pltpu.VMEM((nh, t, LANES), F32)],
        compiler_params=_cparams(("arbitrary",)),
    )(qkv, do_bf, qkv, qkv, cum_t3, cum, lse_t3, delta_t3)


def _bucket_table():
    qi = np.arange(BLOCK)[:, None]
    kj = np.arange(2 * BLOCK)[None, :]
    rel = np.maximum(qi + BLOCK - kj, 0).astype(np.int32)
    max_exact = NUM_BUCKETS // 2
    relf = np.maximum(rel, 1).astype(np.float32)
    large = max_exact + (np.log(relf / np.float32(max_exact)) / np.float32(math.log(MAX_DISTANCE / max_exact))
                         * np.float32(NUM_BUCKETS - max_exact)).astype(np.int32)
    large = np.minimum(large, NUM_BUCKETS - 1)
    return np.where(rel < max_exact, rel, large).astype(np.int32)


SWA_LANES = SWA_GROUP * BLOCK


def _swa_bias(rel_bias, bucket_t):
    def kern(rb_ref, bk_ref, o_ref):
        bk = bk_ref[...]
        kj = lax.broadcasted_iota(jnp.int32, (2 * BLOCK, BLOCK), 0)
        qi = lax.broadcasted_iota(jnp.int32, (2 * BLOCK, BLOCK), 1)
        rel = qi + BLOCK - kj
        band = (rel >= 0) & (rel < BLOCK)
        masks = [band & (kj >= BLOCK), band]
        for h in range(SWA_HEADS):
            g, hh = divmod(h, SWA_GROUP)
            acc = jnp.zeros((2 * BLOCK, BLOCK), F32)
            for b in range(NUM_BUCKETS):
                acc = jnp.where(bk == b, rb_ref[b, h], acc)
            for first in range(2):
                o_ref[first, g, :, hh * BLOCK:(hh + 1) * BLOCK] = jnp.where(masks[first], acc, NEG)

    return pl.pallas_call(
        kern, name="swa_bias",
        in_specs=[pl.BlockSpec(memory_space=pltpu.SMEM),
                  pl.BlockSpec(memory_space=pltpu.VMEM)],
        out_specs=pl.BlockSpec(memory_space=pltpu.VMEM),
        out_shape=jax.ShapeDtypeStruct((2, SWA_KV_HEADS, 2 * BLOCK, SWA_LANES), F32),
        compiler_params=_cparams(),
    )(rel_bias, bucket_t)


SWA_STEP = 4


def _swa_keys(prev_ref, cur_ref):
    return jnp.concatenate([prev_ref[...], cur_ref[...]], axis=0)


def _swa_queries(x_ref, scale):
    x = x_ref[...]
    if scale:
        x = x * jnp.asarray(QK_SCALE, BF16)
    xt = x.astype(F32).T.astype(BF16)
    return [_group_rows(xt[:, b * BLOCK:(b + 1) * BLOCK]) for b in range(SWA_STEP)]


def _group_rows(xt):
    zeros = jnp.zeros((HEAD_DIM, SWA_LANES), BF16)
    out = []
    for g in range(SWA_KV_HEADS):
        heads = [xt[(SWA_GROUP * g + hh) * HEAD_DIM:(SWA_GROUP * g + hh + 1) * HEAD_DIM, :] for hh in range(SWA_GROUP)]
        rows = jnp.concatenate(heads, axis=1)
        padded = jnp.concatenate([rows, zeros] if g == 0 else [zeros, rows], axis=0)
        out.append((rows, padded))
    return out


def _pairs_to_rows(cols_t):
    out = []
    for p in range(SWA_HEADS // 2):
        g, hh = divmod(2 * p, SWA_GROUP)
        pair = jnp.concatenate([cols_t[g][:, hh * BLOCK:(hh + 1) * BLOCK],
                                cols_t[g][:, (hh + 1) * BLOCK:(hh + 2) * BLOCK]], axis=0)
        out.append(pair.T)
    return jnp.concatenate(out, axis=1)


def _swa_fwd(qkv, bias_t, sink_rows):
    s = qkv.shape[0]
    nb = s // BLOCK
    rows = SWA_STEP * BLOCK
    units = [(b, g) for b in range(SWA_STEP) for g in range(SWA_KV_HEADS)]

    def kern(q_ref, kp_ref, kc_ref, vp_ref, vc_ref, bias_ref, sink_ref, o_ref, lse_ref):
        n = pl.program_id(0)
        tables = [jnp.minimum(n, 1)] + [1] * (SWA_STEP - 1)
        k3 = _swa_keys(kp_ref, kc_ref)
        vt3 = _swa_keys(vp_ref, vc_ref).astype(F32).T.astype(BF16)
        qts = _swa_queries(q_ref, True)
        us = [_dot(k3[b * BLOCK:(b + 2) * BLOCK], qts[b][g][1]) + bias_ref[tables[b], g] for b, g in units]
        outs = []
        for (b, g), u in zip(units, us):
            sk = sink_ref[g]
            m = jnp.maximum(jnp.max(u, axis=0, keepdims=True), sk)
            p = jnp.exp(u - m)
            l = jnp.sum(p, axis=0, keepdims=True) + jnp.exp(sk - m)
            lse_ref[b, g] = m + jnp.log(l)
            vt = vt3[g * HEAD_DIM:(g + 1) * HEAD_DIM, b * BLOCK:(b + 2) * BLOCK]
            outs.append(_dot(vt, (p * (1.0 / l)).astype(BF16)))
        for b in range(SWA_STEP):
            o_ref[b * BLOCK:(b + 1) * BLOCK, :] = _pairs_to_rows(outs[b * SWA_KV_HEADS:(b + 1) * SWA_KV_HEADS])

    cq, ck, cv = COL_SQ // SWA_W, COL_SK // LANES, COL_SV // LANES
    prev = lambda n: jnp.maximum(SWA_STEP * n - 1, 0)
    return pl.pallas_call(
        kern, name="swa_fwd",
        grid=(nb // SWA_STEP,),
        in_specs=[pl.BlockSpec((rows, SWA_W), lambda n: (n, cq)),
                  pl.BlockSpec((BLOCK, LANES), lambda n: (prev(n), ck)),
                  pl.BlockSpec((rows, LANES), lambda n: (n, ck)),
                  pl.BlockSpec((BLOCK, LANES), lambda n: (prev(n), cv)),
                  pl.BlockSpec((rows, LANES), lambda n: (n, cv)),
                  _resident((2, SWA_KV_HEADS, 2 * BLOCK, SWA_LANES), lambda n: (0, 0, 0, 0)),
                  _resident((SWA_KV_HEADS, 1, SWA_LANES), lambda n: (0, 0, 0))],
        out_specs=[pl.BlockSpec((rows, SWA_W), lambda n: (n, 0)),
                   pl.BlockSpec((SWA_STEP, SWA_KV_HEADS, 1, SWA_LANES), lambda n: (n, 0, 0, 0))],
        out_shape=[jax.ShapeDtypeStruct((s, SWA_W), F32),
                   jax.ShapeDtypeStruct((nb, SWA_KV_HEADS, 1, SWA_LANES), F32)],
        compiler_params=_cparams(("parallel",)),
    )(qkv, qkv, qkv, qkv, qkv, bias_t, sink_rows)


def _swa_bwd(qkv, do_bf, delta_rows, lse, bias_t, sink_rows, bucket_t):
    s = qkv.shape[0]
    nb = s // BLOCK
    steps = nb // SWA_STEP
    rows = SWA_STEP * BLOCK
    units = [(b, g) for b in range(SWA_STEP) for g in range(SWA_KV_HEADS)]

    def kern(q_ref, kp_ref, kc_ref, vp_ref, vc_ref, do_ref, dl_ref, lse_ref, bias_ref, sink_ref, bk_ref,
             dq_ref, dk_ref, dv_ref, grb_ref, gsk_ref, dbias_ref, ck_ref, cv_ref, sk_ref):
        n = pl.program_id(0)

        @pl.when(n == 0)
        def _():
            dbias_ref[...] = jnp.zeros_like(dbias_ref)
            ck_ref[...] = jnp.zeros_like(ck_ref)
            cv_ref[...] = jnp.zeros_like(cv_ref)
            sk_ref[...] = jnp.zeros_like(sk_ref)

        @pl.when(n < steps)
        def _():
            tables = [jnp.minimum(n, 1)] + [1] * (SWA_STEP - 1)
            k3 = _swa_keys(kp_ref, kc_ref)
            v3 = _swa_keys(vp_ref, vc_ref)
            kt3 = (k3.astype(F32).T * QK_SCALE).astype(BF16)
            qts = _swa_queries(q_ref, True)
            dots = _swa_queries(do_ref, False)
            sts = [_dot(k3[b * BLOCK:(b + 2) * BLOCK], qts[b][g][1]) for b, g in units]
            dps = [_dot(v3[b * BLOCK:(b + 2) * BLOCK], dots[b][g][1]) for b, g in units]
            ps, dss = [], []
            for i, (b, g) in enumerate(units):
                lse_g = lse_ref[b, g]
                dlt = dl_ref[b, g]
                p = jnp.exp(sts[i] + bias_ref[tables[b], g] - lse_g)
                ds = p * (dps[i] - dlt)
                dbias_ref[g] += ds
                sk_ref[g] += -jnp.exp(sink_ref[g] - lse_g) * dlt
                ps.append(p.astype(BF16))
                dss.append(ds.astype(BF16))
            dk2, dv2 = [], []
            for b in range(SWA_STEP):
                at = lambda g: b * SWA_KV_HEADS + g
                groups = range(SWA_KV_HEADS)
                dv2.append(jnp.concatenate([_dot_nt(dots[b][g][0], ps[at(g)]) for g in groups], axis=0).T)
                dk2.append(jnp.concatenate([_dot_nt(qts[b][g][0], dss[at(g)]) for g in groups], axis=0).T)
                dqts = [_dot(kt3[g * HEAD_DIM:(g + 1) * HEAD_DIM, b * BLOCK:(b + 2) * BLOCK], dss[at(g)]) for g in groups]
                dq_ref[b * BLOCK:(b + 1) * BLOCK, :] = _pairs_to_rows(dqts).astype(BF16)
            last = (SWA_STEP - 1) * BLOCK
            for acc_ref, out_ref, parts in ((ck_ref, dk_ref, dk2), (cv_ref, dv_ref, dv2)):
                done = acc_ref[last:] + parts[0][:BLOCK]
                out_ref[...] = jnp.concatenate([acc_ref[:last], done], axis=0).astype(BF16)
                for b in range(SWA_STEP - 1):
                    acc_ref[b * BLOCK:(b + 1) * BLOCK] = parts[b][BLOCK:] + parts[b + 1][:BLOCK]
                acc_ref[last:] = parts[SWA_STEP - 1][BLOCK:]

        @pl.when(n == steps)
        def _():
            dk_ref[...] = ck_ref[...].astype(BF16)
            dv_ref[...] = cv_ref[...].astype(BF16)
            bk = bk_ref[...]
            lane = lax.broadcasted_iota(jnp.int32, (8, LANES), 1)
            rowi = lax.broadcasted_iota(jnp.int32, (NUM_BUCKETS, LANES), 0)
            lanei = lax.broadcasted_iota(jnp.int32, (NUM_BUCKETS, LANES), 1)
            out = jnp.zeros((NUM_BUCKETS, LANES), F32)
            gsk = jnp.zeros((8, LANES), F32)
            for h in range(SWA_HEADS):
                g, hh = divmod(h, SWA_GROUP)
                cols = slice(hh * BLOCK, (hh + 1) * BLOCK)
                gsk = jnp.where(lane == h, jnp.sum(sk_ref[g][:, cols]), gsk)
                db = dbias_ref[g][:, cols]
                for b in range(NUM_BUCKETS):
                    val = jnp.sum(jnp.where(bk == b, db, 0.0))
                    out = jnp.where((rowi == b) & (lanei == h), val, out)
            grb_ref[...] = out
            gsk_ref[...] = gsk

    cq, ck, cv = COL_SQ // SWA_W, COL_SK // LANES, COL_SV // LANES
    cur = lambda n: jnp.minimum(n, steps - 1)
    prev = lambda n: jnp.maximum(SWA_STEP * cur(n) - 1, 0)
    kout = lambda n: jnp.maximum(n - 1, 0)
    stat = pl.BlockSpec((SWA_STEP, SWA_KV_HEADS, 1, SWA_LANES), lambda n: (cur(n), 0, 0, 0))
    return pl.pallas_call(
        kern, name="swa_bwd",
        grid=(steps + 1,),
        in_specs=[pl.BlockSpec((rows, SWA_W), lambda n: (cur(n), cq)),
                  pl.BlockSpec((BLOCK, LANES), lambda n: (prev(n), ck)),
                  pl.BlockSpec((rows, LANES), lambda n: (cur(n), ck)),
                  pl.BlockSpec((BLOCK, LANES), lambda n: (prev(n), cv)),
                  pl.BlockSpec((rows, LANES), lambda n: (cur(n), cv)),
                  pl.BlockSpec((rows, SWA_W), lambda n: (cur(n), 1)),
                  stat, stat,
                  _resident((2, SWA_KV_HEADS, 2 * BLOCK, SWA_LANES), lambda n: (0, 0, 0, 0)),
                  _resident((SWA_KV_HEADS, 1, SWA_LANES), lambda n: (0, 0, 0)),
                  _resident((2 * BLOCK, BLOCK), lambda n: (0, 0))],
        out_specs=[pl.BlockSpec((rows, SWA_W), lambda n: (cur(n), 0)),
                   pl.BlockSpec((rows, LANES), lambda n: (kout(n), 0)),
                   pl.BlockSpec((rows, LANES), lambda n: (kout(n), 0)),
                   pl.BlockSpec((NUM_BUCKETS, LANES), lambda n: (0, 0)),
                   pl.BlockSpec((8, LANES), lambda n: (0, 0))],
        out_shape=[jax.ShapeDtypeStruct((s, SWA_W), BF16),
                   jax.ShapeDtypeStruct((s, LANES), BF16),
                   jax.ShapeDtypeStruct((s, LANES), BF16),
                   jax.ShapeDtypeStruct((NUM_BUCKETS, LANES), F32),
                   jax.ShapeDtypeStruct((8, LANES), F32)],
        scratch_shapes=[pltpu.VMEM((SWA_KV_HEADS, 2 * BLOCK, SWA_LANES), F32),
                        pltpu.VMEM((rows, LANES), F32),
                        pltpu.VMEM((rows, LANES), F32),
                        pltpu.VMEM((SWA_KV_HEADS, 1, SWA_LANES), F32)],
        compiler_params=_cparams(("arbitrary",)),
    )(qkv, qkv, qkv, qkv, qkv, do_bf, delta_rows, lse, bias_t, sink_rows, bucket_t)


def _post(x, target, o_fox, o_swa, z, w_o, ln_g, ln_b):
    s = x.shape[0]
    tm = min(256, s)
    nt = s // tm

    def kern(x_ref, t_ref, of_ref, os_ref, z_ref, w_ref, g_ref, b_ref,
             loss_ref, dh_ref, gwo_ref, do_ref, dz_ref, dl_ref, gg_ref, gb_ref, lacc_ref):
        step = pl.program_id(0)

        @pl.when(step == 0)
        def _():
            lacc_ref[...] = jnp.zeros_like(lacc_ref)
            gg_ref[...] = jnp.zeros_like(gg_ref)
            gwo_ref[...] = jnp.zeros_like(gwo_ref)
            gb_ref[...] = jnp.zeros_like(gb_ref)

        o = jnp.concatenate([of_ref[...], os_ref[...]], axis=1)
        zz = z_ref[...]
        sig = 1.0 / (1.0 + jnp.exp(-zz))
        silu = zz * sig
        mixed32 = o * silu
        mixed = mixed32.astype(BF16)
        w = w_ref[...]
        h = ALPHA * x_ref[...] + _dot(mixed, w)
        mu = jnp.mean(h, axis=1, keepdims=True)
        hc = h - mu
        var = jnp.mean(hc * hc, axis=1, keepdims=True)
        rstd = lax.rsqrt(var + LN_EPS)
        xhat = hc * rstd
        g = g_ref[...]
        err = xhat * g + b_ref[...] - t_ref[...]
        lacc_ref[...] += jnp.broadcast_to(jnp.sum(err * err, axis=0, keepdims=True), lacc_ref.shape)
        dout = err * (1.0 / D_MODEL)
        gg_ref[...] += jnp.broadcast_to(jnp.sum(dout * xhat, axis=0, keepdims=True), gg_ref.shape)
        gb_ref[...] += jnp.broadcast_to(jnp.sum(dout, axis=0, keepdims=True), gb_ref.shape)
        dxh = dout * g
        m1 = jnp.mean(dxh, axis=1, keepdims=True)
        m2 = jnp.mean(dxh * xhat, axis=1, keepdims=True)
        dh = rstd * (dxh - m1 - xhat * m2)
        dh_ref[...] = dh
        dy = dh.astype(BF16)
        gwo_ref[...] += _dot(mixed32.T.astype(BF16), dy)
        dmix = _dot_nt(dy, w)
        do = dmix * silu
        do_ref[...] = do.astype(BF16)
        dz_ref[...] = (dmix * o * (sig * (1.0 + zz * (1.0 - sig)))).astype(BF16)
        r = lax.broadcasted_iota(jnp.int32, (D_MODEL, LANES), 0) // HEAD_DIM
        c = lax.broadcasted_iota(jnp.int32, (D_MODEL, LANES), 1)
        pick = jnp.where(r == c, 1.0, 0.0).astype(BF16)
        dl_ref[...] = _exact_dot(pick, do * o, False)

        @pl.when(step == nt - 1)
        def _():
            tot = jnp.sum(lacc_ref[0:1, :]) * (0.5 / D_MODEL)
            loss_ref[...] = jnp.broadcast_to(tot, loss_ref.shape)

    row = lambda i: (i, 0)
    fixed = lambda i: (0, 0)
    wide = pl.BlockSpec((tm, D_MODEL), row)
    half = pl.BlockSpec((tm, FOX_W), row)
    return pl.pallas_call(
        kern, name="post",
        grid=(nt,),
        in_specs=[wide, wide, half, half, wide,
                  pl.BlockSpec((D_MODEL, D_MODEL), fixed),
                  pl.BlockSpec((1, D_MODEL), fixed),
                  pl.BlockSpec((1, D_MODEL), fixed)],
        out_specs=[pl.BlockSpec((8, LANES), fixed), wide,
                   _resident((D_MODEL, D_MODEL), fixed), wide, wide,
                   pl.BlockSpec((tm, LANES), row),
                   pl.BlockSpec((8, D_MODEL), fixed), pl.BlockSpec((8, D_MODEL), fixed)],
        out_shape=[jax.ShapeDtypeStruct((8, LANES), F32),
                   jax.ShapeDtypeStruct((s, D_MODEL), F32),
                   jax.ShapeDtypeStruct((D_MODEL, D_MODEL), F32),
                   jax.ShapeDtypeStruct((s, D_MODEL), BF16),
                   jax.ShapeDtypeStruct((s, D_MODEL), BF16),
                   jax.ShapeDtypeStruct((s, LANES), F32),
                   jax.ShapeDtypeStruct((8, D_MODEL), F32),
                   jax.ShapeDtypeStruct((8, D_MODEL), F32)],
        scratch_shapes=[pltpu.VMEM((8, D_MODEL), F32)],
        compiler_params=_cparams(("arbitrary",)),
    )(x, target, o_fox, o_swa, z, w_o, ln_g, ln_b)


def _adamw_math(w, g, m, v):
    m = ADAM_B1 * m + (1.0 - ADAM_B1) * g
    v = ADAM_B2 * v + (1.0 - ADAM_B2) * (g * g)
    m_hat = m / (1.0 - ADAM_B1 ** ADAM_STEP)
    v_hat = v / (1.0 - ADAM_B2 ** ADAM_STEP)
    delta = -ADAM_LR * (m_hat / (jnp.sqrt(v_hat) + ADAM_EPS) + ADAM_WD * w)
    return delta, m, v


def _adamw(w, g, m, v, *, name):
    r, c = w.shape
    tr = min(256, r)

    def kern(w_ref, g_ref, m_ref, v_ref, d_ref, mo_ref, vo_ref):
        d, mn, vn = _adamw_math(w_ref[...], g_ref[...], m_ref[...], v_ref[...])
        d_ref[...] = d
        mo_ref[...] = mn
        vo_ref[...] = vn

    blk = pl.BlockSpec((tr, c), lambda i: (i, 0))
    sds = jax.ShapeDtypeStruct((r, c), F32)
    return pl.pallas_call(
        kern, name=name,
        grid=(r // tr,),
        in_specs=[blk, blk, blk, blk],
        out_specs=[blk, blk, blk],
        out_shape=[sds, sds, sds],
        compiler_params=_cparams(("parallel",)),
    )(w, g, m, v)


def _adamw_cols(w, g, m, v, *, name):
    c, _, r = w.shape
    tc = 139
    assert c % tc == 0

    def kern(w_ref, g_ref, m_ref, v_ref, go_ref, d_ref, mo_ref, vo_ref):
        g = g_ref[...]
        d, mn, vn = _adamw_math(w_ref[...], g, m_ref[...], v_ref[...])
        go_ref[...] = g
        d_ref[...] = d
        mo_ref[...] = mn
        vo_ref[...] = vn

    blk = pl.BlockSpec((tc, 1, r), lambda i: (i, 0, 0))
    sds = jax.ShapeDtypeStruct((c, 1, r), F32)
    return pl.pallas_call(
        kern, name=name,
        grid=(c // tc,),
        in_specs=[blk, blk, blk, blk],
        out_specs=[blk, blk, blk, blk],
        out_shape=[sds, sds, sds, sds],
        compiler_params=_cparams(("parallel",)),
    )(w, g, m, v)


def _position():
    x, y, c = lax.axis_index("x"), lax.axis_index("y"), lax.axis_index("c")
    chips = [(1 - x, y), (x, 1 - y), (1 - x, 1 - y)]
    return x, y, c, chips


def _chip_index(cx, cy):
    return 2 * cx + cy


def _gather_weights(w_in_bf, w_o_bf):
    shards = (w_in_bf, w_o_bf)
    n_arr = len(shards)

    def kern(*refs):
        ins, outs = refs[:n_arr], refs[n_arr:2 * n_arr]
        send_sems, recv_sems, local_sems = refs[2 * n_arr:]
        x, y, c, chips = _position()
        me = _chip_index(x, y)
        sibling = (x, y, 1 - c)

        local = [pltpu.make_async_copy(ins[a], outs[a].at[me], local_sems.at[a]) for a in range(n_arr)]
        for cp in local:
            cp.start()

        def half(ref, a):
            rows = shards[a].shape[0] // 2
            return ref.at[pl.ds(c * rows, rows), :]

        def copy(a, k, src, slot, to):
            return pltpu.make_async_remote_copy(
                src_ref=src, dst_ref=half(outs[a].at[slot], a),
                send_sem=send_sems.at[a * 6 + k], recv_sem=recv_sems.at[a * 6 + k],
                device_id=to, device_id_type=MESH)

        first = [copy(a, j, half(ins[a], a), me, (*chip, c)) for a in range(n_arr) for j, chip in enumerate(chips)]
        for cp in first:
            cp.start()
        passed = []
        for a in range(n_arr):
            for j, chip in enumerate(chips):
                slot = _chip_index(*chip)
                copy(a, j, half(ins[a], a), slot, (*chip, c)).wait_recv()
                fwd = copy(a, 3 + j, half(outs[a].at[slot], a), slot, sibling)
                fwd.start()
                passed.append(fwd)
        for a in range(n_arr):
            for j, chip in enumerate(chips):
                slot = _chip_index(*chip)
                rows = shards[a].shape[0] // 2
                dst = outs[a].at[slot].at[pl.ds((1 - c) * rows, rows), :]
                pltpu.make_async_remote_copy(
                    src_ref=dst, dst_ref=dst, send_sem=send_sems.at[a * 6 + 3 + j],
                    recv_sem=recv_sems.at[a * 6 + 3 + j], device_id=sibling, device_id_type=MESH).wait_recv()
        for cp in first + passed:
            cp.wait_send()
        for cp in local:
            cp.wait()

    vmem = pl.BlockSpec(memory_space=pltpu.VMEM)
    return pl.pallas_call(
        kern, name="gather_weights",
        in_specs=[vmem] * n_arr,
        out_specs=[vmem] * n_arr,
        out_shape=[jax.ShapeDtypeStruct((N_CHIPS,) + w.shape, w.dtype) for w in shards],
        scratch_shapes=[pltpu.SemaphoreType.DMA((6 * n_arr,)),
                        pltpu.SemaphoreType.DMA((6 * n_arr,)),
                        pltpu.SemaphoreType.DMA((n_arr,))],
        compiler_params=_cparams(),
    )(*shards)


def _pair_reduce(grads):
    n_arr = len(grads)
    chunk = 128

    def kern(*refs):
        ins = refs[:n_arr]
        outs = refs[n_arr:2 * n_arr]
        gots = refs[2 * n_arr:3 * n_arr]
        send_sems, recv_sems = refs[3 * n_arr:]
        x, y, c, _ = _position()
        sibling = (x, y, 1 - c)
        copies = []
        for a in range(n_arr):
            rows = grads[a].shape[1] // 2
            copies.append(pltpu.make_async_remote_copy(
                src_ref=ins[a].at[:, pl.ds((1 - c) * rows, rows), :], dst_ref=gots[a],
                send_sem=send_sems.at[a], recv_sem=recv_sems.at[a], device_id=sibling, device_id_type=MESH))
        for cp in copies:
            cp.start()
        for a in range(n_arr):
            copies[a].wait()
            rows = grads[a].shape[1] // 2
            for j in range(N_CHIPS):
                for r0 in range(0, rows, chunk):
                    mine = ins[a][j, pl.ds(pl.multiple_of(c * rows + r0, chunk), chunk), :]
                    outs[a][j, r0:r0 + chunk, :] = (mine + gots[a][j, r0:r0 + chunk, :]).astype(BF16)

    vmem = pl.BlockSpec(memory_space=pltpu.VMEM)
    half = [(N_CHIPS, g.shape[1] // 2, g.shape[2]) for g in grads]
    return pl.pallas_call(
        kern, name="pair_reduce",
        in_specs=[vmem] * n_arr,
        out_specs=[vmem] * n_arr,
        out_shape=[jax.ShapeDtypeStruct(h, BF16) for h in half],
        scratch_shapes=[pltpu.VMEM(h, F32) for h in half]
        + [pltpu.SemaphoreType.DMA((n_arr,)), pltpu.SemaphoreType.DMA((n_arr,))],
        compiler_params=_cparams(),
    )(*grads)


def _chip_reduce(parts):
    n_arr = len(parts)
    chunk = 128

    def kern(*refs):
        ins = refs[:n_arr]
        outs = refs[n_arr:2 * n_arr]
        slabs = refs[2 * n_arr:3 * n_arr]
        sums = refs[3 * n_arr:4 * n_arr]
        send_sems, recv_sems, local_sems, swap_send, swap_recv, swap_local = refs[4 * n_arr:]
        x, y, c, chips = _position()
        me = _chip_index(x, y)
        sibling = (x, y, 1 - c)
        local = [pltpu.make_async_copy(ins[a].at[me], slabs[a].at[me], local_sems.at[a]) for a in range(n_arr)]
        for cp in local:
            cp.start()
        sends = []
        for a in range(n_arr):
            for j, chip in enumerate(chips):
                sends.append(pltpu.make_async_remote_copy(
                    src_ref=ins[a].at[_chip_index(*chip)], dst_ref=slabs[a].at[me],
                    send_sem=send_sems.at[a * 3 + j], recv_sem=recv_sems.at[a * 3 + j],
                    device_id=(*chip, c), device_id_type=MESH))
        for cp in sends:
            cp.start()
        for a in range(n_arr):
            for j, chip in enumerate(chips):
                slot = slabs[a].at[_chip_index(*chip)]
                pltpu.make_async_remote_copy(
                    src_ref=slot, dst_ref=slot, send_sem=send_sems.at[a * 3 + j],
                    recv_sem=recv_sems.at[a * 3 + j], device_id=(*chip, c), device_id_type=MESH).wait_recv()
        for cp in sends:
            cp.wait_send()
        for cp in local:
            cp.wait()
        for a in range(n_arr):
            for r0 in range(0, parts[a].shape[1], chunk):
                f = lambda j: slabs[a][j, r0:r0 + chunk, :].astype(F32)
                sums[a][r0:r0 + chunk, :] = ((f(0) + f(1)) + f(2)) + f(3)
        swap_l, swap_r = [], []
        for a in range(n_arr):
            rows = parts[a].shape[1]
            mine = outs[a].at[pl.ds(c * rows, rows), :]
            swap_l.append(pltpu.make_async_copy(sums[a], mine, swap_local.at[a]))
            swap_r.append(pltpu.make_async_remote_copy(
                src_ref=sums[a], dst_ref=mine, send_sem=swap_send.at[a], recv_sem=swap_recv.at[a],
                device_id=sibling, device_id_type=MESH))
        for cp in swap_l + swap_r:
            cp.start()
        for a in range(n_arr):
            rows = parts[a].shape[1]
            theirs = outs[a].at[pl.ds((1 - c) * rows, rows), :]
            pltpu.make_async_remote_copy(
                src_ref=theirs, dst_ref=theirs, send_sem=swap_send.at[a], recv_sem=swap_recv.at[a],
                device_id=sibling, device_id_type=MESH).wait_recv()
        for cp in swap_r:
            cp.wait_send()
        for cp in swap_l:
            cp.wait()

    vmem = pl.BlockSpec(memory_space=pltpu.VMEM)
    return pl.pallas_call(
        kern, name="chip_reduce",
        in_specs=[vmem] * n_arr,
        out_specs=[vmem] * n_arr,
        out_shape=[jax.ShapeDtypeStruct((2 * p.shape[1], p.shape[2]), F32) for p in parts],
        scratch_shapes=[pltpu.VMEM(p.shape, BF16) for p in parts]
        + [pltpu.VMEM(p.shape[1:], F32) for p in parts]
        + [pltpu.SemaphoreType.DMA((3 * n_arr,)),
           pltpu.SemaphoreType.DMA((3 * n_arr,)),
           pltpu.SemaphoreType.DMA((n_arr,)),
           pltpu.SemaphoreType.DMA((n_arr,)),
           pltpu.SemaphoreType.DMA((n_arr,)),
           pltpu.SemaphoreType.DMA((n_arr,))],
        compiler_params=_cparams(),
    )(*parts)


def _scatter_start(parts):
    n_arr = len(parts)
    hbm = pl.BlockSpec(memory_space=pltpu.HBM)
    sem = pl.BlockSpec(memory_space=pltpu.SEMAPHORE)

    def kern(*refs):
        ins, lands = refs[:n_arr], refs[n_arr:2 * n_arr]
        send_sems, recv_sems, token = refs[2 * n_arr], refs[2 * n_arr + 1], refs[-1]
        x, y, c, chips = _position()
        me = _chip_index(x, y)
        for a in range(n_arr):
            for j, chip in enumerate(chips):
                pltpu.make_async_remote_copy(
                    src_ref=ins[a].at[_chip_index(*chip)], dst_ref=lands[a].at[me],
                    send_sem=send_sems.at[a * 3 + j], recv_sem=recv_sems.at[a * 3 + j],
                    device_id=(*chip, c), device_id_type=MESH).start()
        token[...] = jnp.zeros_like(token)

    slab = [pltpu.HBM(p.shape, p.dtype) for p in parts]
    outs = pl.pallas_call(
        kern, name="scatter_start",
        in_specs=[hbm] * (2 * n_arr),
        out_specs=(sem, sem, *[hbm] * (2 * n_arr), pl.BlockSpec(memory_space=pltpu.VMEM)),
        out_shape=(pltpu.SemaphoreType.DMA((3 * n_arr,)), pltpu.SemaphoreType.DMA((3 * n_arr,)),
                   *slab, *slab, jax.ShapeDtypeStruct((8, LANES), F32)),
        input_output_aliases={i: 2 + i for i in range(2 * n_arr)},
        compiler_params=pltpu.CompilerParams(has_side_effects=pltpu.SideEffectType.DATAFLOW_SIDE_EFFECTING),
    )(*[pltpu.with_memory_space_constraint(p, pltpu.HBM) for p in parts],
      *[pltpu.with_memory_space_constraint(lax.empty(p.shape, p.dtype), pltpu.HBM) for p in parts])
    return outs[0], outs[1], outs[2:2 + n_arr], outs[2 + n_arr:2 + 2 * n_arr], outs[-1]


def _scatter_wait(send_sems, recv_sems, parts_thru, lands_thru, after):
    n_arr = len(parts_thru)
    hbm = pl.BlockSpec(memory_space=pltpu.HBM)
    sem = pl.BlockSpec(memory_space=pltpu.SEMAPHORE)

    def kern(*refs):
        ins, lands = refs[:n_arr], refs[n_arr:2 * n_arr]
        send_ref, recv_ref = refs[2 * n_arr], refs[2 * n_arr + 1]
        x, y, c, chips = _position()
        for a in range(n_arr):
            for j, chip in enumerate(chips):
                slot = _chip_index(*chip)
                copy = pltpu.make_async_remote_copy(
                    src_ref=ins[a].at[slot], dst_ref=lands[a].at[slot],
                    send_sem=send_ref.at[a * 3 + j], recv_sem=recv_ref.at[a * 3 + j],
                    device_id=(*chip, c), device_id_type=MESH)
                copy.wait_send()
                copy.wait_recv()

    slab = [pltpu.HBM(p.shape, p.dtype) for p in parts_thru]
    outs = pl.pallas_call(
        kern, name="scatter_wait",
        in_specs=[hbm] * (2 * n_arr) + [sem, sem, pl.BlockSpec(memory_space=pl.ANY)],
        out_specs=[hbm] * (2 * n_arr),
        out_shape=slab + slab,
        input_output_aliases={i: i for i in range(2 * n_arr)},
        compiler_params=pltpu.CompilerParams(has_side_effects=pltpu.SideEffectType.DATAFLOW_SIDE_EFFECTING),
    )(*parts_thru, *lands_thru, send_sems, recv_sems, after)
    return outs[:n_arr], outs[n_arr:]


def _chip_sum_swap(parts, lands):
    n_arr = len(parts)
    chunk = 128

    def kern(*refs):
        own, got = refs[:n_arr], refs[n_arr:2 * n_arr]
        outs = refs[2 * n_arr:3 * n_arr]
        sums = refs[3 * n_arr:4 * n_arr]
        swap_send, swap_recv, swap_local = refs[4 * n_arr:]
        x, y, c, _ = _position()
        me = _chip_index(x, y)
        sibling = (x, y, 1 - c)
        for a in range(n_arr):
            for r0 in range(0, parts[a].shape[1], chunk):
                mine = own[a][me, r0:r0 + chunk, :].astype(F32)

                def term(i):
                    other = got[a][jnp.where(i == me, (i + 1) % N_CHIPS, i), r0:r0 + chunk, :].astype(F32)
                    return jnp.where(i == me, mine, other)
                sums[a][r0:r0 + chunk, :] = ((term(0) + term(1)) + term(2)) + term(3)
        swap_l, swap_r = [], []
        for a in range(n_arr):
            rows = parts[a].shape[1]
            mine = outs[a].at[pl.ds(c * rows, rows), :]
            swap_l.append(pltpu.make_async_copy(sums[a], mine, swap_local.at[a]))
            swap_r.append(pltpu.make_async_remote_copy(
                src_ref=sums[a], dst_ref=mine, send_sem=swap_send.at[a], recv_sem=swap_recv.at[a],
                device_id=sibling, device_id_type=MESH))
        for cp in swap_l + swap_r:
            cp.start()
        for a in range(n_arr):
            rows = parts[a].shape[1]
            theirs = outs[a].at[pl.ds((1 - c) * rows, rows), :]
            pltpu.make_async_remote_copy(
                src_ref=theirs, dst_ref=theirs, send_sem=swap_send.at[a], recv_sem=swap_recv.at[a],
                device_id=sibling, device_id_type=MESH).wait_recv()
        for cp in swap_r:
            cp.wait_send()
        for cp in swap_l:
            cp.wait()

    vmem = pl.BlockSpec(memory_space=pltpu.VMEM)
    return pl.pallas_call(
        kern, name="chip_sum_swap",
        in_specs=[vmem] * (2 * n_arr),
        out_specs=[vmem] * n_arr,
        out_shape=[jax.ShapeDtypeStruct((2 * p.shape[1], p.shape[2]), F32) for p in parts],
        scratch_shapes=[pltpu.VMEM(p.shape[1:], F32) for p in parts]
        + [pltpu.SemaphoreType.DMA((n_arr,)),
           pltpu.SemaphoreType.DMA((n_arr,)),
           pltpu.SemaphoreType.DMA((n_arr,))],
        compiler_params=_cparams(),
    )(*parts, *lands)


def _small_allreduce_adamw(partials, params, moms, vels):
    chunks = D_MODEL // LANES
    row_rb, row_bf, row_sk, row_loss = 2 * chunks, 2 * chunks + NUM_BUCKETS, 2 * chunks + NUM_BUCKETS + 1, SMALL_ROWS - 6

    def kern(gbf_ref, grb_ref, gsk_ref, gg_ref, gb_ref, loss_ref, *refs):
        p_refs, m_refs, v_refs = refs[0:5], refs[5:10], refs[10:15]
        lo_ref, g_outs, d_outs, mo_outs, vo_outs = refs[15], refs[16:21], refs[21:26], refs[26:31], refs[31:36]
        send_ref, buf_ref, send_sems, recv_sems = refs[36:]
        x, y, c, _ = _position()
        me = 4 * x + 2 * y + c
        send_ref[...] = jnp.zeros_like(send_ref)
        for r in range(chunks):
            send_ref[r:r + 1, :] = gg_ref[0:1, r * LANES:(r + 1) * LANES]
            send_ref[chunks + r:chunks + r + 1, :] = gb_ref[0:1, r * LANES:(r + 1) * LANES]
        send_ref[row_rb:row_rb + NUM_BUCKETS, :] = grb_ref[...]
        send_ref[row_bf:row_bf + 1, :] = gbf_ref[0:1, :]
        send_ref[row_sk:row_sk + 1, :] = gsk_ref[0:1, :]
        send_ref[row_loss:row_loss + 1, :] = loss_ref[0:1, :]
        buf_ref[me] = send_ref[...]
        peers = [(x, y, 1 - c)] + [(px, py, pc) for px, py in _position()[3] for pc in (c, 1 - c)]
        sends = []
        for k, peer in enumerate(peers):
            sends.append(pltpu.make_async_remote_copy(
                src_ref=send_ref, dst_ref=buf_ref.at[me], send_sem=send_sems.at[k], recv_sem=recv_sems.at[k],
                device_id=peer, device_id_type=MESH))
        for cp in sends:
            cp.start()
        for k, (px, py, pc) in enumerate(peers):
            slot = buf_ref.at[4 * px + 2 * py + pc]
            pltpu.make_async_remote_copy(
                src_ref=slot, dst_ref=slot, send_sem=send_sems.at[k], recv_sem=recv_sems.at[k],
                device_id=(px, py, pc), device_id_type=MESH).wait_recv()
        for cp in sends:
            cp.wait_send()
        tot = buf_ref[0]
        for d in range(1, N_DEV):
            tot = tot + buf_ref[d]
        lo_ref[...] = tot[row_loss:row_loss + 1, :]
        grads = [tot[row_bf:row_bf + 1, 0:FOX_HEADS],
                 tot[row_rb:row_rb + NUM_BUCKETS, 0:SWA_HEADS],
                 tot[row_sk:row_sk + 1, 0:SWA_HEADS],
                 jnp.concatenate([tot[r:r + 1, :] for r in range(chunks)], axis=1),
                 jnp.concatenate([tot[chunks + r:chunks + r + 1, :] for r in range(chunks)], axis=1)]
        for i, g in enumerate(grads):
            g_outs[i][...] = g
            delta, mn, vn = _adamw_math(p_refs[i][...], g, m_refs[i][...], v_refs[i][...])
            d_outs[i][...] = delta
            mo_outs[i][...] = mn
            vo_outs[i][...] = vn

    vm = pl.BlockSpec(memory_space=pltpu.VMEM)
    shapes = [jax.ShapeDtypeStruct(p.shape, F32) for p in params]
    outs = pl.pallas_call(
        kern, name="small_allreduce_adamw",
        in_specs=[vm] * 21,
        out_specs=[vm] * 21,
        out_shape=[jax.ShapeDtypeStruct((1, LANES), F32)] + shapes * 4,
        scratch_shapes=[pltpu.VMEM((SMALL_ROWS, LANES), F32),
                        pltpu.VMEM((N_DEV, SMALL_ROWS, LANES), F32),
                        pltpu.SemaphoreType.DMA((N_DEV - 1,)),
                        pltpu.SemaphoreType.DMA((N_DEV - 1,))],
    )(*partials, *params, *moms, *vels)
    return outs[0], outs[1:6], outs[6:11], outs[11:16], outs[16:21]


def _to_padded_cols(w):
    pad = jnp.zeros((w.shape[0], N_C - FOX_HEADS), w.dtype)
    return jnp.concatenate([w[:, 0:1536], w[:, 2056:2824], w[:, 1536:1544], pad,
                            w[:, 1544:2056], w[:, 2824:3336]], axis=1)


def _from_padded_cols(g):
    return jnp.concatenate([g[:, 0:1536], g[:, OFF_C:OFF_C + FOX_HEADS], g[:, OFF_B:OFF_B + FOX_W],
                            g[:, 1536:N_A], g[:, OFF_B + FOX_W:N_PAD]], axis=1)


def _fox_rows(a):
    return a[:, :FOX_HEADS].T.reshape(FOX_HEADS, 1, a.shape[0])


def kernel(x, w_in, b_f, rel_bias, sink, w_o, ln_g, ln_b, loss_target, m_w_in, m_b_f, m_rel_bias, m_sink, m_w_o, m_ln_g, m_ln_b, v_w_in, v_b_f, v_rel_bias, v_sink, v_w_o, v_ln_g, v_ln_b):
    x2 = x[0]
    tgt = loss_target[0]
    s = x2.shape[0]
    w_in2, w_o2 = w_in[0], w_o[0]

    shard_cols = D_IN // N_CHIPS
    col_pad = ((0, 0), (0, SHARD_PAD - shard_cols))
    w_in_all, w_o_all = _gather_weights(jnp.pad(w_in2.astype(BF16), col_pad), w_o2.astype(BF16))
    w_full = jnp.concatenate([w_in_all[j, :, :shard_cols] for j in range(N_CHIPS)], axis=1)
    w_pad = _to_padded_cols(w_full)
    w_o_full = w_o_all.reshape(D_MODEL, D_MODEL)

    qkv, ffp, z, xt, vt = _project(x2, w_pad)
    bfp = jnp.pad(b_f, ((0, 0), (0, LANES - FOX_HEADS)))
    cum = _cum_fwd(ffp, bfp)
    cum_t3 = _fox_rows(cum)
    o_fox, lse_t3 = _fox_fwd(qkv, vt, cum_t3, cum)
    bucket_t = jnp.asarray(_bucket_table().T)
    bias_t = _swa_bias(rel_bias, bucket_t)
    sink_rows = jnp.repeat(sink.reshape(SWA_KV_HEADS, SWA_GROUP, 1), BLOCK, axis=2).reshape(SWA_KV_HEADS, 1, SWA_LANES)
    o_swa, lse_swa = _swa_fwd(qkv, bias_t, sink_rows)

    loss8, dh, grad_w_o_full, do_bf, dz, delta, gg8, gb8 = _post(
        x2, tgt, o_fox, o_swa, z, w_o_full, ln_g, ln_b)

    delta_t3 = _fox_rows(delta)
    dq_fox, dk_fox, dv_fox, dcum_k, dcum_q = _fox_bwd(qkv, do_bf, cum_t3, cum, lse_t3, delta_t3)
    dcum_q = jnp.pad(dcum_q.reshape(FOX_HEADS, s).T, ((0, 0), (0, LANES - FOX_HEADS)))
    dff, gbf8 = _cum_bwd(dcum_k, dcum_q, ffp, bfp)
    delta_rows = (delta[:, FOX_HEADS:FOX_HEADS + SWA_HEADS].reshape(s // BLOCK, BLOCK, SWA_KV_HEADS, SWA_GROUP)
                  .transpose(0, 2, 3, 1).reshape(s // BLOCK, SWA_KV_HEADS, 1, SWA_LANES))
    dq_swa, dk_swa, dv_swa, grb, gsk8 = _swa_bwd(qkv, do_bf, delta_rows, lse_swa, bias_t, sink_rows, bucket_t)

    d_misc = jnp.concatenate([dk_swa, dv_swa, dff], axis=1)
    pieces = [dq_fox, dk_fox, dv_fox, dq_swa, d_misc, dz]
    blocks = [(p, 0) for p in pieces[:-1]] + [(dz, 0), (dz, 1)]
    grad_w_pad = _grad_w_matmul(xt, blocks, tk=1024, name="grad_w_in")
    grad_w_in_full = _from_padded_cols(grad_w_pad)

    g_in4 = jnp.stack([jnp.pad(grad_w_in_full[:, j * shard_cols:(j + 1) * shard_cols], col_pad)
                       for j in range(N_CHIPS)])
    g_o4 = grad_w_o_full.reshape(N_CHIPS, D_MODEL // N_CHIPS, D_MODEL)
    parts = _pair_reduce([g_in4, g_o4])
    send_sems, recv_sems, parts_thru, lands_thru, token = _scatter_start(parts)
    grad_x = _grad_x_matmul(pieces, w_pad, dh, token, tm=512, tn=D_MODEL, name="grad_x")
    parts, lands = _scatter_wait(send_sems, recv_sems, parts_thru, lands_thru, grad_x)
    g_w_in, g_w_o = _chip_sum_swap(parts, lands)
    g_w_in = g_w_in[:, :shard_cols]

    cols_first = lambda a: jnp.transpose(a, (2, 0, 1))
    rows_first = lambda a: jnp.transpose(a, (1, 2, 0))
    g_w_in, d_w_in, nm_w_in, nv_w_in = [rows_first(a) for a in _adamw_cols(
        cols_first(w_in), cols_first(g_w_in[None]), cols_first(m_w_in), cols_first(v_w_in), name="adamw_w_in")]
    d_w_o, nm_w_o, nv_w_o = _adamw(w_o2, g_w_o, m_w_o[0], v_w_o[0], name="adamw_w_o")

    loss_row, gs, ds, ms, vs = _small_allreduce_adamw(
        [gbf8, grb, gsk8, gg8, gb8, loss8],
        [b_f, rel_bias, sink, ln_g, ln_b],
        [m_b_f, m_rel_bias, m_sink, m_ln_g, m_ln_b],
        [v_b_f, v_rel_bias, v_sink, v_ln_g, v_ln_b])
    loss = loss_row[0, 0]
    g_bf, g_rb, g_sk, g_lg, g_lb = gs
    d_bf, d_rb, d_sk, d_lg, d_lb = ds
    m_bf, m_rb, m_sk, m_lg, m_lb = ms
    v_bf, v_rb, v_sk, v_lg, v_lb = vs

    e = lambda a: a[None]
    return (loss, e(grad_x),
            g_w_in, g_bf, g_rb, g_sk, e(g_w_o), g_lg, g_lb,
            d_w_in, d_bf, d_rb, d_sk, e(d_w_o), d_lg, d_lb,
            nm_w_in, m_bf, m_rb, m_sk, e(nm_w_o), m_lg, m_lb,
            nv_w_in, v_bf, v_rb, v_sk, e(nv_w_o), v_lg, v_lb)
```

```python
import functools
import math

import numpy as np
import jax
import jax.numpy as jnp
from jax import lax
from jax.experimental import pallas as pl
from jax.experimental.pallas import tpu as pltpu

F32 = jnp.float32
BF16 = jnp.bfloat16

D_MODEL = 1024
HEAD_DIM = 64
FOX_HEADS = 8
SWA_HEADS = 8
SWA_KV_HEADS = 2
SWA_GROUP = 4
FOX_W = 512
SWA_W = 512
BLOCK = 128
NUM_BUCKETS = 32
MAX_DISTANCE = 128
LN_EPS = 1e-5
NEG = -1e30
ALPHA = 2.0 ** 0.25
QK_SCALE = 0.125

ADAM_LR = 0.001
ADAM_B1 = 0.9
ADAM_B2 = 0.999
ADAM_EPS = 1e-08
ADAM_WD = 0.01
ADAM_STEP = 10

D_IN = 3336
SHARD_PAD = 896
N_A = 2304
N_C = 256
N_B = 1024
OFF_C = N_A
OFF_B = N_A + N_C
N_PAD = N_A + N_C + N_B
COL_FK, COL_FV, COL_SQ, COL_SK, COL_SV = 512, 1024, 1536, 2048, 2176

LANES = 128
FOX_T = 256
FOX_REF = 512
SUM_ROWS = 16
VMEM_LIMIT = 56 * 1024 * 1024

MESH = pl.DeviceIdType.MESH
N_CHIPS = 4
N_DEV = 8
SMALL_ROWS = 56


def _cparams(sem=None):
    return pltpu.CompilerParams(dimension_semantics=sem, vmem_limit_bytes=VMEM_LIMIT)


def _split3(x):
    hi = x.astype(BF16)
    r = x - hi.astype(F32)
    mid = r.astype(BF16)
    lo = (r - mid.astype(F32)).astype(BF16)
    return hi, mid, lo


def _dot(a, b):
    return jnp.dot(a, b, preferred_element_type=F32)


def _dot_nt(a, b):
    return lax.dot_general(a, b, (((1,), (1,)), ((), ())), preferred_element_type=F32)


def _project(x, w_pad, token):
    s, k = x.shape
    tm = 512
    chunk = 512

    def kern(x_ref, w_ref, _, qkv_ref, ff_ref, z_ref, xt_ref, vt_ref):
        xf = x_ref[...]
        xb = xf.astype(BF16)
        xt_ref[...] = xf.T.astype(BF16)
        for c0 in range(0, N_A, chunk):
            width = min(chunk, N_A - c0)
            res = _dot(xb, w_ref[:, c0:c0 + width])
            qkv_ref[:, c0:c0 + width] = res.astype(BF16)
            if c0 == COL_FV:
                vt_ref[...] = res.T.astype(BF16)
        ff_ref[...] = _dot(xb, w_ref[:, OFF_C:OFF_C + N_C])
        for c0 in range(0, N_B, 512):
            z_ref[:, c0:c0 + 512] = _dot(xb, w_ref[:, OFF_B + c0:OFF_B + c0 + 512])

    row = lambda i: (i, 0)
    return pl.pallas_call(
        kern, name="project",
        grid=(s // tm,),
        in_specs=[pl.BlockSpec((tm, k), row),
                  _resident((k, N_PAD), lambda i: (0, 0)),
                  _resident(token.shape, lambda i: (0, 0))],
        out_specs=[pl.BlockSpec((tm, N_A), row),
                   pl.BlockSpec((tm, N_C), row),
                   pl.BlockSpec((tm, N_B), row),
                   pl.BlockSpec((k, tm), lambda i: (0, i)),
                   pl.BlockSpec((FOX_W, tm), lambda i: (0, i))],
        out_shape=[jax.ShapeDtypeStruct((s, N_A), BF16),
                   jax.ShapeDtypeStruct((s, N_C), F32),
                   jax.ShapeDtypeStruct((s, N_B), F32),
                   jax.ShapeDtypeStruct((k, s), BF16),
                   jax.ShapeDtypeStruct((FOX_W, s), BF16)],
        compiler_params=_cparams(("parallel",)),
    )(x, w_pad, token)


def _grad_x_matmul(pieces, w_pad, dh, token, *, tm, tn, name):
    m = dh.shape[0]
    n, k = w_pad.shape
    widths = [p.shape[1] for p in pieces]
    offs = [sum(widths[:i]) for i in range(len(pieces))]
    assert sum(widths) == k

    def kern(*refs):
        p_refs, (b_ref, dh_ref, _, o_ref) = refs[:len(pieces)], refs[len(pieces):]
        acc = ALPHA * dh_ref[...]
        for p_ref, off, width in zip(p_refs, offs, widths):
            acc = acc + _dot_nt(p_ref[...], b_ref[:, off:off + width])
        o_ref[...] = acc

    assert tn == n
    return pl.pallas_call(
        kern, name=name,
        grid=(m // tm,),
        in_specs=[pl.BlockSpec((tm, w), lambda i: (i, 0)) for w in widths]
        + [_resident((n, k), lambda i: (0, 0)),
           pl.BlockSpec((tm, n), lambda i: (i, 0)),
           _resident(token.shape, lambda i: (0, 0))],
        out_specs=pl.BlockSpec((tm, n), lambda i: (i, 0)),
        out_shape=jax.ShapeDtypeStruct((m, n), F32),
        compiler_params=_cparams(("parallel",)),
    )(*pieces, w_pad, dh, token)


def _grad_w_matmul(xt, blocks, *, tk, name):
    m, s = xt.shape
    tn = 512
    nb = len(blocks)

    def kern(a_ref, *refs):
        b_refs, o_ref = refs[:nb], refs[nb]

        @pl.when(pl.program_id(0) == 0)
        def _():
            o_ref[...] = jnp.zeros_like(o_ref)
        a = a_ref[...]
        for blk in range(nb):
            o_ref[:, blk * tn:(blk + 1) * tn] += _dot(a, b_refs[blk][...])

    return pl.pallas_call(
        kern, name=name,
        grid=(s // tk,),
        in_specs=[pl.BlockSpec((m, tk), lambda k: (0, k))]
        + [pl.BlockSpec((tk, tn), functools.partial(lambda k, col: (k, col), col=col)) for _, col in blocks],
        out_specs=_resident((m, nb * tn), lambda k: (0, 0)),
        out_shape=jax.ShapeDtypeStruct((m, nb * tn), F32),
        compiler_params=_cparams(("arbitrary",)),
    )(xt, *[arr for arr, _ in blocks])


def _tri(n, lower):
    r = lax.broadcasted_iota(jnp.int32, (n, n), 0)
    c = lax.broadcasted_iota(jnp.int32, (n, n), 1)
    keep = (c <= r) if lower else (c >= r)
    return jnp.where(keep, 1.0, 0.0).astype(BF16)


def _exact_dot(mat_bf16, x_f32, left):
    out = None
    for piece in _split3(x_f32):
        t = _dot(mat_bf16, piece) if left else _dot(piece, mat_bf16)
        out = t if out is None else out + t
    return out


def _log_sigmoid(z):
    return jnp.minimum(z, 0.0) - jnp.log(1.0 + jnp.exp(-jnp.abs(z)))


def _cum_fwd(ffp, bfp):
    s = ffp.shape[0]
    t = min(1024, s)

    def kern(ff_ref, b_ref, cum_ref, carry_ref):
        @pl.when(pl.program_id(0) == 0)
        def _():
            carry_ref[...] = jnp.zeros_like(carry_ref)
        lane = lax.broadcasted_iota(jnp.int32, (1, LANES), 1)
        lf = _log_sigmoid(ff_ref[...] + b_ref[...])
        lf = jnp.where(lane < FOX_HEADS, lf, 0.0)
        cum = _exact_dot(_tri(t, True), lf, True) + carry_ref[0:1, :]
        cum_ref[...] = cum
        carry_ref[...] = jnp.broadcast_to(cum[t - 1:t, :], carry_ref.shape)

    return pl.pallas_call(
        kern, name="cum_fwd",
        grid=(s // t,),
        in_specs=[pl.BlockSpec((t, LANES), lambda i: (i, 0)),
                  pl.BlockSpec((1, LANES), lambda i: (0, 0))],
        out_specs=pl.BlockSpec((t, LANES), lambda i: (i, 0)),
        out_shape=jax.ShapeDtypeStruct((s, LANES), F32),
        scratch_shapes=[pltpu.VMEM((8, LANES), F32)],
        compiler_params=_cparams(("arbitrary",)),
    )(ffp, bfp)


def _cum_bwd(dcum_k, dcum_q, ffp, bfp):
    s = dcum_k.shape[0]
    t = min(1024, s)
    nb = s // t

    def kern(dck_ref, dcq_ref, ff_ref, b_ref, dff_ref, gb_ref, carry_ref):
        @pl.when(pl.program_id(0) == 0)
        def _():
            carry_ref[...] = jnp.zeros_like(carry_ref)
            gb_ref[...] = jnp.zeros_like(gb_ref)
        lane = lax.broadcasted_iota(jnp.int32, (1, LANES), 1)
        dlf = _exact_dot(_tri(t, False), dck_ref[...] + dcq_ref[...], True) + carry_ref[0:1, :]
        carry_ref[...] = jnp.broadcast_to(dlf[0:1, :], carry_ref.shape)
        z = ff_ref[...] + b_ref[...]
        dff = jnp.where(lane < FOX_HEADS, dlf / (1.0 + jnp.exp(z)), 0.0)
        gb_ref[...] += jnp.broadcast_to(jnp.sum(dff, axis=0, keepdims=True), gb_ref.shape)
        dff_ref[...] = jnp.concatenate([dff, jnp.zeros_like(dff)], axis=1).astype(BF16)

    return pl.pallas_call(
        kern, name="cum_bwd",
        grid=(nb,),
        in_specs=[pl.BlockSpec((t, LANES), lambda i: (nb - 1 - i, 0)),
                  pl.BlockSpec((t, LANES), lambda i: (nb - 1 - i, 0)),
                  pl.BlockSpec((t, LANES), lambda i: (nb - 1 - i, 0)),
                  pl.BlockSpec((1, LANES), lambda i: (0, 0))],
        out_specs=[pl.BlockSpec((t, N_C), lambda i: (nb - 1 - i, 0)),
                   pl.BlockSpec((8, LANES), lambda i: (0, 0))],
        out_shape=[jax.ShapeDtypeStruct((s, N_C), BF16),
                   jax.ShapeDtypeStruct((8, LANES), F32)],
        scratch_shapes=[pltpu.VMEM((8, LANES), F32)],
        compiler_params=_cparams(("arbitrary",)),
    )(dcum_k, dcum_q, ffp, bfp)


def _resident(shape, index_map):
    return pl.BlockSpec(shape, index_map, pipeline_mode=pl.Buffered(1))


def _fox_fwd(qkv, vt, cum_t3, cum):
    s = qkv.shape[0]
    tk = tq = FOX_REF
    nq = s // tq
    nh = FOX_HEADS
    diag_tiles = tq // tk

    def kern(q_ref, k_ref, vt_ref, ct_ref, c_ref, o_ref, lse_ref, m_ref, acc_ref, u_ref):
        i = pl.program_id(0)
        lane = lax.broadcasted_iota(jnp.int32, (1, LANES), 1)
        krow = lax.broadcasted_iota(jnp.int32, (tk, tq), 0)
        qcol = lax.broadcasted_iota(jnp.int32, (tk, tq), 1)
        q0 = pl.multiple_of(i * tq, tq)
        qts, crefs = [], []
        for h in range(nh):
            p, a = divmod(h, 2)
            q2 = q_ref[:, p * LANES:(p + 1) * LANES] * jnp.asarray(QK_SCALE, BF16)
            sel = (lane < HEAD_DIM) if a == 0 else (lane >= HEAD_DIM)
            qts.append(jnp.where(sel, q2, jnp.zeros_like(q2)).astype(F32).T.astype(BF16))
            crefs.append(ct_ref[h, :, pl.ds(q0, LANES)][:, 0:1])
        m_ref[...] = jnp.full(m_ref.shape, NEG, F32)
        acc_ref[...] = jnp.zeros_like(acc_ref)
        ones = jnp.ones((SUM_ROWS, tk), BF16)

        def tile(j, diag):
            k0 = pl.multiple_of(j * tk, tk)
            cb = c_ref[pl.ds(k0, tk), :]
            sts = [_dot(k_ref[pl.ds(k0, tk), (h // 2) * LANES:(h // 2 + 1) * LANES], qts[h]) for h in range(nh)]
            tile_max = []
            for h in range(nh):
                u = sts[h] - (cb[:, h:h + 1] - crefs[h])
                if diag is not None:
                    u = jnp.where(krow + diag * tk <= qcol, u, NEG)
                u_ref[h] = u
                tile_max.append(jnp.max(u, axis=0, keepdims=True))
            pts, scales = [], []
            for h in range(nh):
                m_old = m_ref[h]
                m_new = jnp.maximum(m_old, tile_max[h])
                scales.append(jnp.exp(m_old - m_new))
                pts.append(jnp.exp(u_ref[h] - m_new).astype(BF16))
                m_ref[h] = m_new
            for h in range(nh):
                vth = jnp.concatenate([vt_ref[h * HEAD_DIM:(h + 1) * HEAD_DIM, pl.ds(k0, tk)], ones], axis=0)
                acc_ref[h] = scales[h] * acc_ref[h] + _dot(vth, pts[h])

        def body(j, c):
            tile(j, None)
            return c
        lax.fori_loop(0, i * diag_tiles, body, 0)
        for d in range(diag_tiles):
            tile(i * diag_tiles + d, d)

        ls = [acc_ref[h][HEAD_DIM:HEAD_DIM + 1] for h in range(nh)]
        for p in range(nh // 2):
            ot = jnp.concatenate([acc_ref[2 * p + a][:HEAD_DIM] * (1.0 / ls[2 * p + a]) for a in range(2)], axis=0)
            o_ref[:, p * LANES:(p + 1) * LANES] = ot.T
        for h in range(nh):
            lse_ref[h, :, pl.ds(q0, tq)] = m_ref[h] + jnp.log(ls[h])

    return pl.pallas_call(
        kern, name="fox_fwd",
        grid=(nq,),
        in_specs=[pl.BlockSpec((tq, FOX_W), lambda i: (i, 0)),
                  _resident((s, FOX_W), lambda i: (0, COL_FK // FOX_W)),
                  _resident((FOX_W, s), lambda i: (0, 0)),
                  _resident((nh, 1, s), lambda i: (0, 0, 0)),
                  _resident((s, LANES), lambda i: (0, 0))],
        out_specs=[pl.BlockSpec((tq, FOX_W), lambda i: (i, 0)),
                   pl.BlockSpec((nh, 1, s), lambda i: (0, 0, 0))],
        out_shape=[jax.ShapeDtypeStruct((s, FOX_W), F32),
                   jax.ShapeDtypeStruct((nh, 1, s), F32)],
        scratch_shapes=[pltpu.VMEM((nh, 1, tq), F32),
                        pltpu.VMEM((nh, HEAD_DIM + SUM_ROWS, tq), F32),
                        pltpu.VMEM((nh, tk, tq), F32)],
        compiler_params=_cparams(("arbitrary",)),
    )(qkv, qkv, vt, cum_t3, cum)


def _fox_bwd(qkv, do_bf, cum_t3, cum, lse_t3, delta_t3):
    s = qkv.shape[0]
    t = min(FOX_T, s)
    nq = s // t
    nh = FOX_HEADS
    npair = nh // 2

    def kern(q_ref, do_ref, k_ref, v_ref, ct_ref, c_ref, lse_ref, dl_ref,
             dq_ref, dk_ref, dv_ref, dc_ref, dcq_ref, dqt_ref, accv_ref, acck_ref, accd_ref):
        kj = pl.program_id(0)
        lane = lax.broadcasted_iota(jnp.int32, (1, LANES), 1)
        krow = lax.broadcasted_iota(jnp.int32, (t, t), 0)
        qcol = lax.broadcasted_iota(jnp.int32, (t, t), 1)
        causal = krow <= qcol
        sels = [lane < HEAD_DIM, lane >= HEAD_DIM]

        @pl.when(kj == 0)
        def _():
            dqt_ref[...] = jnp.zeros_like(dqt_ref)
            dcq_ref[...] = jnp.zeros_like(dcq_ref)

        cb = c_ref[...]
        k2s, v2s, kts = [], [], []
        for p in range(npair):
            k2 = k_ref[:, p * LANES:(p + 1) * LANES]
            k2s.append(k2)
            v2s.append(v_ref[:, p * LANES:(p + 1) * LANES])
            kt = k2.astype(F32).T * QK_SCALE
            kts.append(kt[:HEAD_DIM].astype(BF16))
            kts.append(kt[HEAD_DIM:].astype(BF16))
        css = [cb[:, h:h + 1] for h in range(nh)]

        def tile(i, masked):
            q0 = pl.multiple_of(i * t, t)
            r0 = pl.multiple_of((i // (FOX_REF // t)) * FOX_REF, FOX_REF)
            sts, dpts, qms, doms = [], [], [], []
            for h in range(nh):
                p, a = divmod(h, 2)
                qi = q_ref[pl.ds(q0, t), p * LANES:(p + 1) * LANES] * jnp.asarray(QK_SCALE, BF16)
                doi = do_ref[pl.ds(q0, t), p * LANES:(p + 1) * LANES]
                qm = jnp.where(sels[a], qi, jnp.zeros_like(qi))
                dom = jnp.where(sels[a], doi, jnp.zeros_like(doi))
                qms.append(qm)
                doms.append(dom)
                sts.append(_dot_nt(k2s[p], qm))
                dpts.append(_dot_nt(v2s[p], dom))
            pts, dsts = [], []
            for h in range(nh):
                cref = ct_ref[h, :, pl.ds(r0, LANES)][:, 0:1]
                pt = jnp.exp(sts[h] - (css[h] - cref) - lse_ref[h, :, pl.ds(q0, t)])
                if masked:
                    pt = jnp.where(causal, pt, 0.0)
                ds32 = pt * (dpts[h] - dl_ref[h, :, pl.ds(q0, t)])
                part = ds32[:, 0:LANES]
                for c in range(1, t // LANES):
                    part = part + ds32[:, c * LANES:(c + 1) * LANES]
                accd_ref[h] = part if masked else accd_ref[h] + part
                dcq_ref[h, :, pl.ds(q0, t)] += jnp.sum(ds32, axis=0, keepdims=True)
                pts.append(pt.astype(BF16))
                dsts.append(ds32.astype(BF16))
            for p in range(npair):
                ha, hb = 2 * p, 2 * p + 1
                dv_p = _dot(pts[ha], doms[ha]) + _dot(pts[hb], doms[hb])
                dk_p = _dot(dsts[ha], qms[ha]) + _dot(dsts[hb], qms[hb])
                accv_ref[p] = dv_p if masked else accv_ref[p] + dv_p
                acck_ref[p] = dk_p if masked else acck_ref[p] + dk_p
            for h in range(nh):
                dqt_ref[h * HEAD_DIM:(h + 1) * HEAD_DIM, pl.ds(q0, t)] += _dot(kts[h], dsts[h])

        tile(kj, True)

        def body(i, c):
            tile(i, False)
            return c
        lax.fori_loop(kj + 1, nq, body, 0)

        dc = jnp.zeros((t, LANES), F32)
        for h in range(nh):
            dc = jnp.where(lane == h, -jnp.sum(accd_ref[h], axis=1, keepdims=True), dc)
        dc_ref[...] = dc
        for p in range(npair):
            dv_ref[:, p * LANES:(p + 1) * LANES] = accv_ref[p].astype(BF16)
            dk_ref[:, p * LANES:(p + 1) * LANES] = acck_ref[p].astype(BF16)

        @pl.when(kj == nq - 1)
        def _():
            for c0 in range(0, s, t):
                dq_ref[c0:c0 + t, :] = dqt_ref[:, c0:c0 + t].T.astype(BF16)

    whole = lambda kj: (0, 0, 0)
    return pl.pallas_call(
        kern, name="fox_bwd",
        grid=(nq,),
        in_specs=[_resident((s, FOX_W), lambda kj: (0, 0)),
                  _resident((s, FOX_W), lambda kj: (0, 0)),
                  pl.BlockSpec((t, FOX_W), lambda kj: (kj, COL_FK // FOX_W)),
                  pl.BlockSpec((t, FOX_W), lambda kj: (kj, COL_FV // FOX_W)),
                  _resident((nh, 1, s), whole),
                  pl.BlockSpec((t, LANES), lambda kj: (kj, 0)),
                  _resident((nh, 1, s), whole),
                  _resident((nh, 1, s), whole)],
        out_specs=[_resident((s, FOX_W), lambda kj: (0, 0)),
                   pl.BlockSpec((t, FOX_W), lambda kj: (kj, 0)),
                   pl.BlockSpec((t, FOX_W), lambda kj: (kj, 0)),
                   pl.BlockSpec((t, LANES), lambda kj: (kj, 0)),
                   _resident((nh, 1, s), whole)],
        out_shape=[jax.ShapeDtypeStruct((s, FOX_W), BF16),
                   jax.ShapeDtypeStruct((s, FOX_W), BF16),
                   jax.ShapeDtypeStruct((s, FOX_W), BF16),
                   jax.ShapeDtypeStruct((s, LANES), F32),
                   jax.ShapeDtypeStruct((nh, 1, s), F32)],
        scratch_shapes=[pltpu.VMEM((FOX_W, s), F32),
                        pltpu.VMEM((npair, t, LANES), F32),
                        pltpu.VMEM((npair, t, LANES), F32),
                        pltpu.VMEM((nh, t, LANES), F32)],
        compiler_params=_cparams(("arbitrary",)),
    )(qkv, do_bf, qkv, qkv, cum_t3, cum, lse_t3, delta_t3)


def _bucket_table():
    qi = np.arange(BLOCK)[:, None]
    kj = np.arange(2 * BLOCK)[None, :]
    rel = np.maximum(qi + BLOCK - kj, 0).astype(np.int32)
    max_exact = NUM_BUCKETS // 2
    relf = np.maximum(rel, 1).astype(np.float32)
    large = max_exact + (np.log(relf / np.float32(max_exact)) / np.float32(math.log(MAX_DISTANCE / max_exact))
                         * np.float32(NUM_BUCKETS - max_exact)).astype(np.int32)
    large = np.minimum(large, NUM_BUCKETS - 1)
    return np.where(rel < max_exact, rel, large).astype(np.int32)


SWA_LANES = SWA_GROUP * BLOCK


def _swa_bias(rel_bias, bucket_t):
    def kern(rb_ref, bk_ref, o_ref):
        bk = bk_ref[...]
        kj = lax.broadcasted_iota(jnp.int32, (2 * BLOCK, BLOCK), 0)
        qi = lax.broadcasted_iota(jnp.int32, (2 * BLOCK, BLOCK), 1)
        rel = qi + BLOCK - kj
        band = (rel >= 0) & (rel < BLOCK)
        masks = [band & (kj >= BLOCK), band]
        for h in range(SWA_HEADS):
            g, hh = divmod(h, SWA_GROUP)
            acc = jnp.zeros((2 * BLOCK, BLOCK), F32)
            for b in range(NUM_BUCKETS):
                acc = jnp.where(bk == b, rb_ref[b, h], acc)
            for first in range(2):
                o_ref[first, g, :, hh * BLOCK:(hh + 1) * BLOCK] = jnp.where(masks[first], acc, NEG)

    return pl.pallas_call(
        kern, name="swa_bias",
        in_specs=[pl.BlockSpec(memory_space=pltpu.SMEM),
                  pl.BlockSpec(memory_space=pltpu.VMEM)],
        out_specs=pl.BlockSpec(memory_space=pltpu.VMEM),
        out_shape=jax.ShapeDtypeStruct((2, SWA_KV_HEADS, 2 * BLOCK, SWA_LANES), F32),
        compiler_params=_cparams(),
    )(rel_bias, bucket_t)


SWA_STEP = 4


def _swa_keys(prev_ref, cur_ref):
    return jnp.concatenate([prev_ref[...], cur_ref[...]], axis=0)


def _swa_queries(x_ref, scale):
    x = x_ref[...]
    if scale:
        x = x * jnp.asarray(QK_SCALE, BF16)
    xt = x.astype(F32).T.astype(BF16)
    return [_group_rows(xt[:, b * BLOCK:(b + 1) * BLOCK]) for b in range(SWA_STEP)]


def _group_rows(xt):
    zeros = jnp.zeros((HEAD_DIM, SWA_LANES), BF16)
    out = []
    for g in range(SWA_KV_HEADS):
        heads = [xt[(SWA_GROUP * g + hh) * HEAD_DIM:(SWA_GROUP * g + hh + 1) * HEAD_DIM, :] for hh in range(SWA_GROUP)]
        rows = jnp.concatenate(heads, axis=1)
        padded = jnp.concatenate([rows, zeros] if g == 0 else [zeros, rows], axis=0)
        out.append((rows, padded))
    return out


def _pairs_to_rows(cols_t):
    out = []
    for p in range(SWA_HEADS // 2):
        g, hh = divmod(2 * p, SWA_GROUP)
        pair = jnp.concatenate([cols_t[g][:, hh * BLOCK:(hh + 1) * BLOCK],
                                cols_t[g][:, (hh + 1) * BLOCK:(hh + 2) * BLOCK]], axis=0)
        out.append(pair.T)
    return jnp.concatenate(out, axis=1)


def _swa_fwd(qkv, bias_t, sink_rows):
    s = qkv.shape[0]
    nb = s // BLOCK
    rows = SWA_STEP * BLOCK
    units = [(b, g) for b in range(SWA_STEP) for g in range(SWA_KV_HEADS)]

    def kern(q_ref, kp_ref, kc_ref, vp_ref, vc_ref, bias_ref, sink_ref, o_ref, lse_ref):
        n = pl.program_id(0)
        tables = [jnp.minimum(n, 1)] + [1] * (SWA_STEP - 1)
        k3 = _swa_keys(kp_ref, kc_ref)
        vt3 = _swa_keys(vp_ref, vc_ref).astype(F32).T.astype(BF16)
        qts = _swa_queries(q_ref, True)
        us = [_dot(k3[b * BLOCK:(b + 2) * BLOCK], qts[b][g][1]) + bias_ref[tables[b], g] for b, g in units]
        outs = []
        for (b, g), u in zip(units, us):
            sk = sink_ref[g]
            m = jnp.maximum(jnp.max(u, axis=0, keepdims=True), sk)
            p = jnp.exp(u - m)
            l = jnp.sum(p, axis=0, keepdims=True) + jnp.exp(sk - m)
            lse_ref[b, g] = m + jnp.log(l)
            vt = vt3[g * HEAD_DIM:(g + 1) * HEAD_DIM, b * BLOCK:(b + 2) * BLOCK]
            outs.append(_dot(vt, (p * (1.0 / l)).astype(BF16)))
        for b in range(SWA_STEP):
            o_ref[b * BLOCK:(b + 1) * BLOCK, :] = _pairs_to_rows(outs[b * SWA_KV_HEADS:(b + 1) * SWA_KV_HEADS])

    cq, ck, cv = COL_SQ // SWA_W, COL_SK // LANES, COL_SV // LANES
    prev = lambda n: jnp.maximum(SWA_STEP * n - 1, 0)
    return pl.pallas_call(
        kern, name="swa_fwd",
        grid=(nb // SWA_STEP,),
        in_specs=[pl.BlockSpec((rows, SWA_W), lambda n: (n, cq)),
                  pl.BlockSpec((BLOCK, LANES), lambda n: (prev(n), ck)),
                  pl.BlockSpec((rows, LANES), lambda n: (n, ck)),
                  pl.BlockSpec((BLOCK, LANES), lambda n: (prev(n), cv)),
                  pl.BlockSpec((rows, LANES), lambda n: (n, cv)),
                  _resident((2, SWA_KV_HEADS, 2 * BLOCK, SWA_LANES), lambda n: (0, 0, 0, 0)),
                  _resident((SWA_KV_HEADS, 1, SWA_LANES), lambda n: (0, 0, 0))],
        out_specs=[pl.BlockSpec((rows, SWA_W), lambda n: (n, 0)),
                   pl.BlockSpec((SWA_STEP, SWA_KV_HEADS, 1, SWA_LANES), lambda n: (n, 0, 0, 0))],
        out_shape=[jax.ShapeDtypeStruct((s, SWA_W), F32),
                   jax.ShapeDtypeStruct((nb, SWA_KV_HEADS, 1, SWA_LANES), F32)],
        compiler_params=_cparams(("parallel",)),
    )(qkv, qkv, qkv, qkv, qkv, bias_t, sink_rows)


def _swa_bwd(qkv, do_bf, delta_rows, lse, bias_t, sink_rows, bucket_t):
    s = qkv.shape[0]
    nb = s // BLOCK
    steps = nb // SWA_STEP
    rows = SWA_STEP * BLOCK
    units = [(b, g) for b in range(SWA_STEP) for g in range(SWA_KV_HEADS)]

    def kern(q_ref, kp_ref, kc_ref, vp_ref, vc_ref, do_ref, dl_ref, lse_ref, bias_ref, sink_ref, bk_ref,
             dq_ref, dk_ref, dv_ref, grb_ref, gsk_ref, dbias_ref, ck_ref, cv_ref, sk_ref):
        n = pl.program_id(0)

        @pl.when(n == 0)
        def _():
            dbias_ref[...] = jnp.zeros_like(dbias_ref)
            ck_ref[...] = jnp.zeros_like(ck_ref)
            cv_ref[...] = jnp.zeros_like(cv_ref)
            sk_ref[...] = jnp.zeros_like(sk_ref)

        @pl.when(n < steps)
        def _():
            tables = [jnp.minimum(n, 1)] + [1] * (SWA_STEP - 1)
            k3 = _swa_keys(kp_ref, kc_ref)
            v3 = _swa_keys(vp_ref, vc_ref)
            kt3 = (k3.astype(F32).T * QK_SCALE).astype(BF16)
            qts = _swa_queries(q_ref, True)
            dots = _swa_queries(do_ref, False)
            sts = [_dot(k3[b * BLOCK:(b + 2) * BLOCK], qts[b][g][1]) for b, g in units]
            dps = [_dot(v3[b * BLOCK:(b + 2) * BLOCK], dots[b][g][1]) for b, g in units]
            ps, dss = [], []
            for i, (b, g) in enumerate(units):
                lse_g = lse_ref[b, g]
                dlt = dl_ref[b, g]
                p = jnp.exp(sts[i] + bias_ref[tables[b], g] - lse_g)
                ds = p * (dps[i] - dlt)
                dbias_ref[g] += ds
                sk_ref[g] += -jnp.exp(sink_ref[g] - lse_g) * dlt
                ps.append(p.astype(BF16))
                dss.append(ds.astype(BF16))
            dk2, dv2 = [], []
            for b in range(SWA_STEP):
                at = lambda g: b * SWA_KV_HEADS + g
                groups = range(SWA_KV_HEADS)
                dv2.append(jnp.concatenate([_dot_nt(dots[b][g][0], ps[at(g)]) for g in groups], axis=0).T)
                dk2.append(jnp.concatenate([_dot_nt(qts[b][g][0], dss[at(g)]) for g in groups], axis=0).T)
                dqts = [_dot(kt3[g * HEAD_DIM:(g + 1) * HEAD_DIM, b * BLOCK:(b + 2) * BLOCK], dss[at(g)]) for g in groups]
                dq_ref[b * BLOCK:(b + 1) * BLOCK, :] = _pairs_to_rows(dqts).astype(BF16)
            last = (SWA_STEP - 1) * BLOCK
            for acc_ref, out_ref, parts in ((ck_ref, dk_ref, dk2), (cv_ref, dv_ref, dv2)):
                done = acc_ref[last:] + parts[0][:BLOCK]
                out_ref[...] = jnp.concatenate([acc_ref[:last], done], axis=0).astype(BF16)
                for b in range(SWA_STEP - 1):
                    acc_ref[b * BLOCK:(b + 1) * BLOCK] = parts[b][BLOCK:] + parts[b + 1][:BLOCK]
                acc_ref[last:] = parts[SWA_STEP - 1][BLOCK:]

        @pl.when(n == steps)
        def _():
            dk_ref[...] = ck_ref[...].astype(BF16)
            dv_ref[...] = cv_ref[...].astype(BF16)
            bk = bk_ref[...]
            lane = lax.broadcasted_iota(jnp.int32, (8, LANES), 1)
            rowi = lax.broadcasted_iota(jnp.int32, (NUM_BUCKETS, LANES), 0)
            lanei = lax.broadcasted_iota(jnp.int32, (NUM_BUCKETS, LANES), 1)
            out = jnp.zeros((NUM_BUCKETS, LANES), F32)
            gsk = jnp.zeros((8, LANES), F32)
            for h in range(SWA_HEADS):
                g, hh = divmod(h, SWA_GROUP)
                cols = slice(hh * BLOCK, (hh + 1) * BLOCK)
                gsk = jnp.where(lane == h, jnp.sum(sk_ref[g][:, cols]), gsk)
                db = dbias_ref[g][:, cols]
                for b in range(NUM_BUCKETS):
                    val = jnp.sum(jnp.where(bk == b, db, 0.0))
                    out = jnp.where((rowi == b) & (lanei == h), val, out)
            grb_ref[...] = out
            gsk_ref[...] = gsk

    cq, ck, cv = COL_SQ // SWA_W, COL_SK // LANES, COL_SV // LANES
    cur = lambda n: jnp.minimum(n, steps - 1)
    prev = lambda n: jnp.maximum(SWA_STEP * cur(n) - 1, 0)
    kout = lambda n: jnp.maximum(n - 1, 0)
    stat = pl.BlockSpec((SWA_STEP, SWA_KV_HEADS, 1, SWA_LANES), lambda n: (cur(n), 0, 0, 0))
    return pl.pallas_call(
        kern, name="swa_bwd",
        grid=(steps + 1,),
        in_specs=[pl.BlockSpec((rows, SWA_W), lambda n: (cur(n), cq)),
                  pl.BlockSpec((BLOCK, LANES), lambda n: (prev(n), ck)),
                  pl.BlockSpec((rows, LANES), lambda n: (cur(n), ck)),
                  pl.BlockSpec((BLOCK, LANES), lambda n: (prev(n), cv)),
                  pl.BlockSpec((rows, LANES), lambda n: (cur(n), cv)),
                  pl.BlockSpec((rows, SWA_W), lambda n: (cur(n), 1)),
                  stat, stat,
                  _resident((2, SWA_KV_HEADS, 2 * BLOCK, SWA_LANES), lambda n: (0, 0, 0, 0)),
                  _resident((SWA_KV_HEADS, 1, SWA_LANES), lambda n: (0, 0, 0)),
                  _resident((2 * BLOCK, BLOCK), lambda n: (0, 0))],
        out_specs=[pl.BlockSpec((rows, SWA_W), lambda n: (cur(n), 0)),
                   pl.BlockSpec((rows, LANES), lambda n: (kout(n), 0)),
                   pl.BlockSpec((rows, LANES), lambda n: (kout(n), 0)),
                   pl.BlockSpec((NUM_BUCKETS, LANES), lambda n: (0, 0)),
                   pl.BlockSpec((8, LANES), lambda n: (0, 0))],
        out_shape=[jax.ShapeDtypeStruct((s, SWA_W), BF16),
                   jax.ShapeDtypeStruct((s, LANES), BF16),
                   jax.ShapeDtypeStruct((s, LANES), BF16),
                   jax.ShapeDtypeStruct((NUM_BUCKETS, LANES), F32),
                   jax.ShapeDtypeStruct((8, LANES), F32)],
        scratch_shapes=[pltpu.VMEM((SWA_KV_HEADS, 2 * BLOCK, SWA_LANES), F32),
                        pltpu.VMEM((rows, LANES), F32),
                        pltpu.VMEM((rows, LANES), F32),
                        pltpu.VMEM((SWA_KV_HEADS, 1, SWA_LANES), F32)],
        compiler_params=_cparams(("arbitrary",)),
    )(qkv, qkv, qkv, qkv, qkv, do_bf, delta_rows, lse, bias_t, sink_rows, bucket_t)


def _post(x, target, o_fox, o_swa, z, w_o, ln_g, ln_b):
    s = x.shape[0]
    tm = min(256, s)
    nt = s // tm

    def kern(x_ref, t_ref, of_ref, os_ref, z_ref, w_ref, g_ref, b_ref,
             loss_ref, dh_ref, gwo_ref, do_ref, dz_ref, dl_ref, gg_ref, gb_ref, lacc_ref):
        step = pl.program_id(0)

        @pl.when(step == 0)
        def _():
            lacc_ref[...] = jnp.zeros_like(lacc_ref)
            gg_ref[...] = jnp.zeros_like(gg_ref)
            gwo_ref[...] = jnp.zeros_like(gwo_ref)
            gb_ref[...] = jnp.zeros_like(gb_ref)

        o = jnp.concatenate([of_ref[...], os_ref[...]], axis=1)
        zz = z_ref[...]
        sig = 1.0 / (1.0 + jnp.exp(-zz))
        silu = zz * sig
        mixed32 = o * silu
        mixed = mixed32.astype(BF16)
        w = w_ref[...]
        h = ALPHA * x_ref[...] + _dot(mixed, w)
        mu = jnp.mean(h, axis=1, keepdims=True)
        hc = h - mu
        var = jnp.mean(hc * hc, axis=1, keepdims=True)
        rstd = lax.rsqrt(var + LN_EPS)
        xhat = hc * rstd
        g = g_ref[...]
        err = xhat * g + b_ref[...] - t_ref[...]
        lacc_ref[...] += jnp.broadcast_to(jnp.sum(err * err, axis=0, keepdims=True), lacc_ref.shape)
        dout = err * (1.0 / D_MODEL)
        gg_ref[...] += jnp.broadcast_to(jnp.sum(dout * xhat, axis=0, keepdims=True), gg_ref.shape)
        gb_ref[...] += jnp.broadcast_to(jnp.sum(dout, axis=0, keepdims=True), gb_ref.shape)
        dxh = dout * g
        m1 = jnp.mean(dxh, axis=1, keepdims=True)
        m2 = jnp.mean(dxh * xhat, axis=1, keepdims=True)
        dh = rstd * (dxh - m1 - xhat * m2)
        dh_ref[...] = dh
        dy = dh.astype(BF16)
        gwo_ref[...] += _dot(mixed32.T.astype(BF16), dy)
        dmix = _dot_nt(dy, w)
        do = dmix * silu
        do_ref[...] = do.astype(BF16)
        dz_ref[...] = (dmix * o * (sig * (1.0 + zz * (1.0 - sig)))).astype(BF16)
        r = lax.broadcasted_iota(jnp.int32, (D_MODEL, LANES), 0) // HEAD_DIM
        c = lax.broadcasted_iota(jnp.int32, (D_MODEL, LANES), 1)
        pick = jnp.where(r == c, 1.0, 0.0).astype(BF16)
        dl_ref[...] = _exact_dot(pick, do * o, False)

        @pl.when(step == nt - 1)
        def _():
            tot = jnp.sum(lacc_ref[0:1, :]) * (0.5 / D_MODEL)
            loss_ref[...] = jnp.broadcast_to(tot, loss_ref.shape)

    row = lambda i: (i, 0)
    fixed = lambda i: (0, 0)
    wide = pl.BlockSpec((tm, D_MODEL), row)
    half = pl.BlockSpec((tm, FOX_W), row)
    return pl.pallas_call(
        kern, name="post",
        grid=(nt,),
        in_specs=[wide, wide, half, half, wide,
                  pl.BlockSpec((D_MODEL, D_MODEL), fixed),
                  pl.BlockSpec((1, D_MODEL), fixed),
                  pl.BlockSpec((1, D_MODEL), fixed)],
        out_specs=[pl.BlockSpec((8, LANES), fixed), wide,
                   _resident((D_MODEL, D_MODEL), fixed), wide, wide,
                   pl.BlockSpec((tm, LANES), row),
                   pl.BlockSpec((8, D_MODEL), fixed), pl.BlockSpec((8, D_MODEL), fixed)],
        out_shape=[jax.ShapeDtypeStruct((8, LANES), F32),
                   jax.ShapeDtypeStruct((s, D_MODEL), F32),
                   jax.ShapeDtypeStruct((D_MODEL, D_MODEL), F32),
                   jax.ShapeDtypeStruct((s, D_MODEL), BF16),
                   jax.ShapeDtypeStruct((s, D_MODEL), BF16),
                   jax.ShapeDtypeStruct((s, LANES), F32),
                   jax.ShapeDtypeStruct((8, D_MODEL), F32),
                   jax.ShapeDtypeStruct((8, D_MODEL), F32)],
        scratch_shapes=[pltpu.VMEM((8, D_MODEL), F32)],
        compiler_params=_cparams(("arbitrary",)),
    )(x, target, o_fox, o_swa, z, w_o, ln_g, ln_b)


def _adamw_math(w, g, m, v):
    m = ADAM_B1 * m + (1.0 - ADAM_B1) * g
    v = ADAM_B2 * v + (1.0 - ADAM_B2) * (g * g)
    m_hat = m / (1.0 - ADAM_B1 ** ADAM_STEP)
    v_hat = v / (1.0 - ADAM_B2 ** ADAM_STEP)
    delta = -ADAM_LR * (m_hat / (jnp.sqrt(v_hat) + ADAM_EPS) + ADAM_WD * w)
    return delta, m, v


def _adamw(w, g, m, v, *, name):
    r, c = w.shape
    tr = min(256, r)

    def kern(w_ref, g_ref, m_ref, v_ref, d_ref, mo_ref, vo_ref):
        d, mn, vn = _adamw_math(w_ref[...], g_ref[...], m_ref[...], v_ref[...])
        d_ref[...] = d
        mo_ref[...] = mn
        vo_ref[...] = vn

    blk = pl.BlockSpec((tr, c), lambda i: (i, 0))
    sds = jax.ShapeDtypeStruct((r, c), F32)
    return pl.pallas_call(
        kern, name=name,
        grid=(r // tr,),
        in_specs=[blk, blk, blk, blk],
        out_specs=[blk, blk, blk],
        out_shape=[sds, sds, sds],
        compiler_params=_cparams(("parallel",)),
    )(w, g, m, v)


def _adamw_cols(w, g, m, v, *, name):
    c, _, r = w.shape
    tc = 139
    assert c % tc == 0

    def kern(w_ref, g_ref, m_ref, v_ref, go_ref, d_ref, mo_ref, vo_ref):
        g = g_ref[...]
        d, mn, vn = _adamw_math(w_ref[...], g, m_ref[...], v_ref[...])
        go_ref[...] = g
        d_ref[...] = d
        mo_ref[...] = mn
        vo_ref[...] = vn

    blk = pl.BlockSpec((tc, 1, r), lambda i: (i, 0, 0))
    sds = jax.ShapeDtypeStruct((c, 1, r), F32)
    return pl.pallas_call(
        kern, name=name,
        grid=(c // tc,),
        in_specs=[blk, blk, blk, blk],
        out_specs=[blk, blk, blk, blk],
        out_shape=[sds, sds, sds, sds],
        compiler_params=_cparams(("parallel",)),
    )(w, g, m, v)


def _position():
    x, y, c = lax.axis_index("x"), lax.axis_index("y"), lax.axis_index("c")
    chips = [(1 - x, y), (x, 1 - y), (1 - x, 1 - y)]
    return x, y, c, chips


def _chip_index(cx, cy):
    return 2 * cx + cy


def _gather_weights(*shards):
    n_arr = len(shards)

    def kern(*refs):
        ins, outs = refs[:n_arr], refs[n_arr:2 * n_arr]
        send_sems, recv_sems, local_sems = refs[2 * n_arr:]
        x, y, c, chips = _position()
        me = _chip_index(x, y)
        sibling = (x, y, 1 - c)

        local = [pltpu.make_async_copy(ins[a], outs[a].at[me], local_sems.at[a]) for a in range(n_arr)]
        for cp in local:
            cp.start()

        def half(ref, a):
            rows = shards[a].shape[0] // 2
            return ref.at[pl.ds(c * rows, rows), :]

        def copy(a, k, src, slot, to):
            return pltpu.make_async_remote_copy(
                src_ref=src, dst_ref=half(outs[a].at[slot], a),
                send_sem=send_sems.at[a * 6 + k], recv_sem=recv_sems.at[a * 6 + k],
                device_id=to, device_id_type=MESH)

        first = [copy(a, j, half(ins[a], a), me, (*chip, c)) for a in range(n_arr) for j, chip in enumerate(chips)]
        for cp in first:
            cp.start()
        passed = []
        for a in range(n_arr):
            for j, chip in enumerate(chips):
                slot = _chip_index(*chip)
                copy(a, j, half(ins[a], a), slot, (*chip, c)).wait_recv()
                fwd = copy(a, 3 + j, half(outs[a].at[slot], a), slot, sibling)
                fwd.start()
                passed.append(fwd)
        for a in range(n_arr):
            for j, chip in enumerate(chips):
                slot = _chip_index(*chip)
                rows = shards[a].shape[0] // 2
                dst = outs[a].at[slot].at[pl.ds((1 - c) * rows, rows), :]
                pltpu.make_async_remote_copy(
                    src_ref=dst, dst_ref=dst, send_sem=send_sems.at[a * 6 + 3 + j],
                    recv_sem=recv_sems.at[a * 6 + 3 + j], device_id=sibling, device_id_type=MESH).wait_recv()
        for cp in first + passed:
            cp.wait_send()
        for cp in local:
            cp.wait()

    vmem = pl.BlockSpec(memory_space=pltpu.VMEM)
    return pl.pallas_call(
        kern, name="gather_weights",
        in_specs=[vmem] * n_arr,
        out_specs=[vmem] * n_arr,
        out_shape=[jax.ShapeDtypeStruct((N_CHIPS,) + w.shape, w.dtype) for w in shards],
        scratch_shapes=[pltpu.SemaphoreType.DMA((6 * n_arr,)),
                        pltpu.SemaphoreType.DMA((6 * n_arr,)),
                        pltpu.SemaphoreType.DMA((n_arr,))],
        compiler_params=_cparams(),
    )(*shards)


def _pair_reduce(grads):
    n_arr = len(grads)
    chunk = 128

    def kern(*refs):
        ins = refs[:n_arr]
        outs = refs[n_arr:2 * n_arr]
        gots = refs[2 * n_arr:3 * n_arr]
        send_sems, recv_sems = refs[3 * n_arr:]
        x, y, c, _ = _position()
        sibling = (x, y, 1 - c)
        copies = []
        for a in range(n_arr):
            rows = grads[a].shape[1] // 2
            copies.append(pltpu.make_async_remote_copy(
                src_ref=ins[a].at[:, pl.ds((1 - c) * rows, rows), :], dst_ref=gots[a],
                send_sem=send_sems.at[a], recv_sem=recv_sems.at[a], device_id=sibling, device_id_type=MESH))
        for cp in copies:
            cp.start()
        for a in range(n_arr):
            copies[a].wait()
            rows = grads[a].shape[1] // 2
            for j in range(N_CHIPS):
                for r0 in range(0, rows, chunk):
                    mine = ins[a][j, pl.ds(pl.multiple_of(c * rows + r0, chunk), chunk), :]
                    outs[a][j, r0:r0 + chunk, :] = (mine + gots[a][j, r0:r0 + chunk, :]).astype(BF16)

    vmem = pl.BlockSpec(memory_space=pltpu.VMEM)
    half = [(N_CHIPS, g.shape[1] // 2, g.shape[2]) for g in grads]
    return pl.pallas_call(
        kern, name="pair_reduce",
        in_specs=[vmem] * n_arr,
        out_specs=[vmem] * n_arr,
        out_shape=[jax.ShapeDtypeStruct(h, BF16) for h in half],
        scratch_shapes=[pltpu.VMEM(h, F32) for h in half]
        + [pltpu.SemaphoreType.DMA((n_arr,)), pltpu.SemaphoreType.DMA((n_arr,))],
        compiler_params=_cparams(),
    )(*grads)


def _chip_reduce(parts):
    n_arr = len(parts)
    chunk = 128

    def kern(*refs):
        ins = refs[:n_arr]
        outs = refs[n_arr:2 * n_arr]
        slabs = refs[2 * n_arr:3 * n_arr]
        sums = refs[3 * n_arr:4 * n_arr]
        send_sems, recv_sems, local_sems, swap_send, swap_recv, swap_local = refs[4 * n_arr:]
        x, y, c, chips = _position()
        me = _chip_index(x, y)
        sibling = (x, y, 1 - c)
        local = [pltpu.make_async_copy(ins[a].at[me], slabs[a].at[me], local_sems.at[a]) for a in range(n_arr)]
        for cp in local:
            cp.start()
        sends = []
        for a in range(n_arr):
            for j, chip in enumerate(chips):
                sends.append(pltpu.make_async_remote_copy(
                    src_ref=ins[a].at[_chip_index(*chip)], dst_ref=slabs[a].at[me],
                    send_sem=send_sems.at[a * 3 + j], recv_sem=recv_sems.at[a * 3 + j],
                    device_id=(*chip, c), device_id_type=MESH))
        for cp in sends:
            cp.start()
        for a in range(n_arr):
            for j, chip in enumerate(chips):
                slot = slabs[a].at[_chip_index(*chip)]
                pltpu.make_async_remote_copy(
                    src_ref=slot, dst_ref=slot, send_sem=send_sems.at[a * 3 + j],
                    recv_sem=recv_sems.at[a * 3 + j], device_id=(*chip, c), device_id_type=MESH).wait_recv()
        for cp in sends:
            cp.wait_send()
        for cp in local:
            cp.wait()
        for a in range(n_arr):
            for r0 in range(0, parts[a].shape[1], chunk):
                f = lambda j: slabs[a][j, r0:r0 + chunk, :].astype(F32)
                sums[a][r0:r0 + chunk, :] = ((f(0) + f(1)) + f(2)) + f(3)
        swap_l, swap_r = [], []
        for a in range(n_arr):
            rows = parts[a].shape[1]
            mine = outs[a].at[pl.ds(c * rows, rows), :]
            swap_l.append(pltpu.make_async_copy(sums[a], mine, swap_local.at[a]))
            swap_r.append(pltpu.make_async_remote_copy(
                src_ref=sums[a], dst_ref=mine, send_sem=swap_send.at[a], recv_sem=swap_recv.at[a],
                device_id=sibling, device_id_type=MESH))
        for cp in swap_l + swap_r:
            cp.start()
        for a in range(n_arr):
            rows = parts[a].shape[1]
            theirs = outs[a].at[pl.ds((1 - c) * rows, rows), :]
            pltpu.make_async_remote_copy(
                src_ref=theirs, dst_ref=theirs, send_sem=swap_send.at[a], recv_sem=swap_recv.at[a],
                device_id=sibling, device_id_type=MESH).wait_recv()
        for cp in swap_r:
            cp.wait_send()
        for cp in swap_l:
            cp.wait()

    vmem = pl.BlockSpec(memory_space=pltpu.VMEM)
    return pl.pallas_call(
        kern, name="chip_reduce",
        in_specs=[vmem] * n_arr,
        out_specs=[vmem] * n_arr,
        out_shape=[jax.ShapeDtypeStruct((2 * p.shape[1], p.shape[2]), F32) for p in parts],
        scratch_shapes=[pltpu.VMEM(p.shape, BF16) for p in parts]
        + [pltpu.VMEM(p.shape[1:], F32) for p in parts]
        + [pltpu.SemaphoreType.DMA((3 * n_arr,)),
           pltpu.SemaphoreType.DMA((3 * n_arr,)),
           pltpu.SemaphoreType.DMA((n_arr,)),
           pltpu.SemaphoreType.DMA((n_arr,)),
           pltpu.SemaphoreType.DMA((n_arr,)),
           pltpu.SemaphoreType.DMA((n_arr,))],
        compiler_params=_cparams(),
    )(*parts)


def _wo_gather_start(shard):
    hbm = pl.BlockSpec(memory_space=pltpu.HBM)
    sem = pl.BlockSpec(memory_space=pltpu.SEMAPHORE)
    land_shape = (N_CHIPS,) + shard.shape

    def kern(src_ref, land_ref, send_sems, recv_sems, src_thru, land_thru, token):
        x, y, c, chips = _position()
        me = _chip_index(x, y)
        for j, chip in enumerate(chips):
            pltpu.make_async_remote_copy(
                src_ref=src_ref, dst_ref=land_ref.at[me], send_sem=send_sems.at[j], recv_sem=recv_sems.at[j],
                device_id=(*chip, c), device_id_type=MESH).start()
        token[...] = jnp.zeros_like(token)

    return pl.pallas_call(
        kern, name="wo_gather_start",
        in_specs=[hbm, hbm],
        out_specs=(sem, sem, hbm, hbm, pl.BlockSpec(memory_space=pltpu.VMEM)),
        out_shape=(pltpu.SemaphoreType.DMA((3,)), pltpu.SemaphoreType.DMA((3,)),
                   pltpu.HBM(shard.shape, shard.dtype), pltpu.HBM(land_shape, shard.dtype),
                   jax.ShapeDtypeStruct((8, LANES), F32)),
        input_output_aliases={0: 2, 1: 3},
        compiler_params=pltpu.CompilerParams(has_side_effects=pltpu.SideEffectType.DATAFLOW_SIDE_EFFECTING),
    )(pltpu.with_memory_space_constraint(shard, pltpu.HBM),
      pltpu.with_memory_space_constraint(lax.empty(land_shape, shard.dtype), pltpu.HBM))


def _wo_gather_wait(send_sems, recv_sems, src_thru, land_thru, after):
    hbm = pl.BlockSpec(memory_space=pltpu.HBM)
    sem = pl.BlockSpec(memory_space=pltpu.SEMAPHORE)

    def kern(src_ref, land_ref, send_sems, recv_sems, after_ref, src_out, land_out):
        x, y, c, chips = _position()
        for j, chip in enumerate(chips):
            copy = pltpu.make_async_remote_copy(
                src_ref=src_ref, dst_ref=land_ref.at[_chip_index(*chip)], send_sem=send_sems.at[j],
                recv_sem=recv_sems.at[j], device_id=(*chip, c), device_id_type=MESH)
            copy.wait_send()
            copy.wait_recv()

    return pl.pallas_call(
        kern, name="wo_gather_wait",
        in_specs=[hbm, hbm, sem, sem, pl.BlockSpec(memory_space=pl.ANY)],
        out_specs=[hbm, hbm],
        out_shape=[pltpu.HBM(src_thru.shape, src_thru.dtype), pltpu.HBM(land_thru.shape, land_thru.dtype)],
        input_output_aliases={0: 0, 1: 1},
        compiler_params=pltpu.CompilerParams(has_side_effects=pltpu.SideEffectType.DATAFLOW_SIDE_EFFECTING),
    )(src_thru, land_thru, send_sems, recv_sems, after)[1]


def _scatter_start(parts):
    n_arr = len(parts)
    hbm = pl.BlockSpec(memory_space=pltpu.HBM)
    sem = pl.BlockSpec(memory_space=pltpu.SEMAPHORE)

    def kern(*refs):
        ins, lands = refs[:n_arr], refs[n_arr:2 * n_arr]
        send_sems, recv_sems, token = refs[2 * n_arr], refs[2 * n_arr + 1], refs[-1]
        x, y, c, chips = _position()
        me = _chip_index(x, y)
        for a in range(n_arr):
            for j, chip in enumerate(chips):
                pltpu.make_async_remote_copy(
                    src_ref=ins[a].at[_chip_index(*chip)], dst_ref=lands[a].at[me],
                    send_sem=send_sems.at[a * 3 + j], recv_sem=recv_sems.at[a * 3 + j],
                    device_id=(*chip, c), device_id_type=MESH).start()
        token[...] = jnp.zeros_like(token)

    slab = [pltpu.HBM(p.shape, p.dtype) for p in parts]
    outs = pl.pallas_call(
        kern, name="scatter_start",
        in_specs=[hbm] * (2 * n_arr),
        out_specs=(sem, sem, *[hbm] * (2 * n_arr), pl.BlockSpec(memory_space=pltpu.VMEM)),
        out_shape=(pltpu.SemaphoreType.DMA((3 * n_arr,)), pltpu.SemaphoreType.DMA((3 * n_arr,)),
                   *slab, *slab, jax.ShapeDtypeStruct((8, LANES), F32)),
        input_output_aliases={i: 2 + i for i in range(2 * n_arr)},
        compiler_params=pltpu.CompilerParams(has_side_effects=pltpu.SideEffectType.DATAFLOW_SIDE_EFFECTING),
    )(*[pltpu.with_memory_space_constraint(p, pltpu.HBM) for p in parts],
      *[pltpu.with_memory_space_constraint(lax.empty(p.shape, p.dtype), pltpu.HBM) for p in parts])
    return outs[0], outs[1], outs[2:2 + n_arr], outs[2 + n_arr:2 + 2 * n_arr], outs[-1]


def _scatter_wait(send_sems, recv_sems, parts_thru, lands_thru, after):
    n_arr = len(parts_thru)
    hbm = pl.BlockSpec(memory_space=pltpu.HBM)
    sem = pl.BlockSpec(memory_space=pltpu.SEMAPHORE)

    def kern(*refs):
        ins, lands = refs[:n_arr], refs[n_arr:2 * n_arr]
        send_ref, recv_ref = refs[2 * n_arr], refs[2 * n_arr + 1]
        x, y, c, chips = _position()
        for a in range(n_arr):
            for j, chip in enumerate(chips):
                slot = _chip_index(*chip)
                copy = pltpu.make_async_remote_copy(
                    src_ref=ins[a].at[slot], dst_ref=lands[a].at[slot],
                    send_sem=send_ref.at[a * 3 + j], recv_sem=recv_ref.at[a * 3 + j],
                    device_id=(*chip, c), device_id_type=MESH)
                copy.wait_send()
                copy.wait_recv()

    slab = [pltpu.HBM(p.shape, p.dtype) for p in parts_thru]
    outs = pl.pallas_call(
        kern, name="scatter_wait",
        in_specs=[hbm] * (2 * n_arr) + [sem, sem, pl.BlockSpec(memory_space=pl.ANY)],
        out_specs=[hbm] * (2 * n_arr),
        out_shape=slab + slab,
        input_output_aliases={i: i for i in range(2 * n_arr)},
        compiler_params=pltpu.CompilerParams(has_side_effects=pltpu.SideEffectType.DATAFLOW_SIDE_EFFECTING),
    )(*parts_thru, *lands_thru, send_sems, recv_sems, after)
    return outs[:n_arr], outs[n_arr:]


def _chip_sum_swap(parts, lands):
    n_arr = len(parts)
    chunk = 128

    def kern(*refs):
        own, got = refs[:n_arr], refs[n_arr:2 * n_arr]
        outs = refs[2 * n_arr:3 * n_arr]
        sums = refs[3 * n_arr:4 * n_arr]
        swap_send, swap_recv, swap_local = refs[4 * n_arr:]
        x, y, c, _ = _position()
        me = _chip_index(x, y)
        sibling = (x, y, 1 - c)
        for a in range(n_arr):
            for r0 in range(0, parts[a].shape[1], chunk):
                mine = own[a][me, r0:r0 + chunk, :].astype(F32)

                def term(i):
                    other = got[a][jnp.where(i == me, (i + 1) % N_CHIPS, i), r0:r0 + chunk, :].astype(F32)
                    return jnp.where(i == me, mine, other)
                sums[a][r0:r0 + chunk, :] = ((term(0) + term(1)) + term(2)) + term(3)
        swap_l, swap_r = [], []
        for a in range(n_arr):
            rows = parts[a].shape[1]
            mine = outs[a].at[pl.ds(c * rows, rows), :]
            swap_l.append(pltpu.make_async_copy(sums[a], mine, swap_local.at[a]))
            swap_r.append(pltpu.make_async_remote_copy(
                src_ref=sums[a], dst_ref=mine, send_sem=swap_send.at[a], recv_sem=swap_recv.at[a],
                device_id=sibling, device_id_type=MESH))
        for cp in swap_l + swap_r:
            cp.start()
        for a in range(n_arr):
            rows = parts[a].shape[1]
            theirs = outs[a].at[pl.ds((1 - c) * rows, rows), :]
            pltpu.make_async_remote_copy(
                src_ref=theirs, dst_ref=theirs, send_sem=swap_send.at[a], recv_sem=swap_recv.at[a],
                device_id=sibling, device_id_type=MESH).wait_recv()
        for cp in swap_r:
            cp.wait_send()
        for cp in swap_l:
            cp.wait()

    vmem = pl.BlockSpec(memory_space=pltpu.VMEM)
    return pl.pallas_call(
        kern, name="chip_sum_swap",
        in_specs=[vmem] * (2 * n_arr),
        out_specs=[vmem] * n_arr,
        out_shape=[jax.ShapeDtypeStruct((2 * p.shape[1], p.shape[2]), F32) for p in parts],
        scratch_shapes=[pltpu.VMEM(p.shape[1:], F32) for p in parts]
        + [pltpu.SemaphoreType.DMA((n_arr,)),
           pltpu.SemaphoreType.DMA((n_arr,)),
           pltpu.SemaphoreType.DMA((n_arr,))],
        compiler_params=_cparams(),
    )(*parts, *lands)


def _small_allreduce_adamw(partials, params, moms, vels):
    chunks = D_MODEL // LANES
    row_rb, row_bf, row_sk, row_loss = 2 * chunks, 2 * chunks + NUM_BUCKETS, 2 * chunks + NUM_BUCKETS + 1, SMALL_ROWS - 6

    def kern(gbf_ref, grb_ref, gsk_ref, gg_ref, gb_ref, loss_ref, *refs):
        p_refs, m_refs, v_refs = refs[0:5], refs[5:10], refs[10:15]
        lo_ref, g_outs, d_outs, mo_outs, vo_outs = refs[15], refs[16:21], refs[21:26], refs[26:31], refs[31:36]
        send_ref, buf_ref, send_sems, recv_sems = refs[36:]
        x, y, c, _ = _position()
        me = 4 * x + 2 * y + c
        send_ref[...] = jnp.zeros_like(send_ref)
        for r in range(chunks):
            send_ref[r:r + 1, :] = gg_ref[0:1, r * LANES:(r + 1) * LANES]
            send_ref[chunks + r:chunks + r + 1, :] = gb_ref[0:1, r * LANES:(r + 1) * LANES]
        send_ref[row_rb:row_rb + NUM_BUCKETS, :] = grb_ref[...]
        send_ref[row_bf:row_bf + 1, :] = gbf_ref[0:1, :]
        send_ref[row_sk:row_sk + 1, :] = gsk_ref[0:1, :]
        send_ref[row_loss:row_loss + 1, :] = loss_ref[0:1, :]
        buf_ref[me] = send_ref[...]
        peers = [(x, y, 1 - c)] + [(px, py, pc) for px, py in _position()[3] for pc in (c, 1 - c)]
        sends = []
        for k, peer in enumerate(peers):
            sends.append(pltpu.make_async_remote_copy(
                src_ref=send_ref, dst_ref=buf_ref.at[me], send_sem=send_sems.at[k], recv_sem=recv_sems.at[k],
                device_id=peer, device_id_type=MESH))
        for cp in sends:
            cp.start()
        for k, (px, py, pc) in enumerate(peers):
            slot = buf_ref.at[4 * px + 2 * py + pc]
            pltpu.make_async_remote_copy(
                src_ref=slot, dst_ref=slot, send_sem=send_sems.at[k], recv_sem=recv_sems.at[k],
                device_id=(px, py, pc), device_id_type=MESH).wait_recv()
        for cp in sends:
            cp.wait_send()
        tot = buf_ref[0]
        for d in range(1, N_DEV):
            tot = tot + buf_ref[d]
        lo_ref[...] = tot[row_loss:row_loss + 1, :]
        grads = [tot[row_bf:row_bf + 1, 0:FOX_HEADS],
                 tot[row_rb:row_rb + NUM_BUCKETS, 0:SWA_HEADS],
                 tot[row_sk:row_sk + 1, 0:SWA_HEADS],
                 jnp.concatenate([tot[r:r + 1, :] for r in range(chunks)], axis=1),
                 jnp.concatenate([tot[chunks + r:chunks + r + 1, :] for r in range(chunks)], axis=1)]
        for i, g in enumerate(grads):
            g_outs[i][...] = g
            delta, mn, vn = _adamw_math(p_refs[i][...], g, m_refs[i][...], v_refs[i][...])
            d_outs[i][...] = delta
            mo_outs[i][...] = mn
            vo_outs[i][...] = vn

    vm = pl.BlockSpec(memory_space=pltpu.VMEM)
    shapes = [jax.ShapeDtypeStruct(p.shape, F32) for p in params]
    outs = pl.pallas_call(
        kern, name="small_allreduce_adamw",
        in_specs=[vm] * 21,
        out_specs=[vm] * 21,
        out_shape=[jax.ShapeDtypeStruct((1, LANES), F32)] + shapes * 4,
        scratch_shapes=[pltpu.VMEM((SMALL_ROWS, LANES), F32),
                        pltpu.VMEM((N_DEV, SMALL_ROWS, LANES), F32),
                        pltpu.SemaphoreType.DMA((N_DEV - 1,)),
                        pltpu.SemaphoreType.DMA((N_DEV - 1,))],
    )(*partials, *params, *moms, *vels)
    return outs[0], outs[1:6], outs[6:11], outs[11:16], outs[16:21]


def _to_padded_cols(w):
    pad = jnp.zeros((w.shape[0], N_C - FOX_HEADS), w.dtype)
    return jnp.concatenate([w[:, 0:1536], w[:, 2056:2824], w[:, 1536:1544], pad,
                            w[:, 1544:2056], w[:, 2824:3336]], axis=1)


def _from_padded_cols(g):
    return jnp.concatenate([g[:, 0:1536], g[:, OFF_C:OFF_C + FOX_HEADS], g[:, OFF_B:OFF_B + FOX_W],
                            g[:, 1536:N_A], g[:, OFF_B + FOX_W:N_PAD]], axis=1)


def _fox_rows(a):
    return a[:, :FOX_HEADS].T.reshape(FOX_HEADS, 1, a.shape[0])


def kernel(x, w_in, b_f, rel_bias, sink, w_o, ln_g, ln_b, loss_target, m_w_in, m_b_f, m_rel_bias, m_sink, m_w_o, m_ln_g, m_ln_b, v_w_in, v_b_f, v_rel_bias, v_sink, v_w_o, v_ln_g, v_ln_b):
    x2 = x[0]
    tgt = loss_target[0]
    s = x2.shape[0]
    w_in2, w_o2 = w_in[0], w_o[0]

    shard_cols = D_IN // N_CHIPS
    col_pad = ((0, 0), (0, SHARD_PAD - shard_cols))
    (w_in_all,) = _gather_weights(jnp.pad(w_in2.astype(BF16), col_pad))
    w_full = jnp.concatenate([w_in_all[j, :, :shard_cols] for j in range(N_CHIPS)], axis=1)
    w_pad = _to_padded_cols(w_full)
    w_o_bf, _ = lax.optimization_barrier((w_o2.astype(BF16), w_in_all))
    wo_send, wo_recv, wo_src, wo_land, wo_token = _wo_gather_start(w_o_bf)

    qkv, ffp, z, xt, vt = _project(x2, w_pad, wo_token)
    bfp = jnp.pad(b_f, ((0, 0), (0, LANES - FOX_HEADS)))
    cum = _cum_fwd(ffp, bfp)
    cum_t3 = _fox_rows(cum)
    o_fox, lse_t3 = _fox_fwd(qkv, vt, cum_t3, cum)
    bucket_t = jnp.asarray(_bucket_table().T)
    bias_t = _swa_bias(rel_bias, bucket_t)
    sink_rows = jnp.repeat(sink.reshape(SWA_KV_HEADS, SWA_GROUP, 1), BLOCK, axis=2).reshape(SWA_KV_HEADS, 1, SWA_LANES)
    o_swa, lse_swa = _swa_fwd(qkv, bias_t, sink_rows)

    wo_land = _wo_gather_wait(wo_send, wo_recv, wo_src, wo_land, o_swa)
    my_chip = _chip_index(lax.axis_index("x"), lax.axis_index("y"))
    w_o_full = lax.dynamic_update_slice(wo_land, w_o_bf[None], (my_chip, 0, 0)).reshape(D_MODEL, D_MODEL)
    loss8, dh, grad_w_o_full, do_bf, dz, delta, gg8, gb8 = _post(
        x2, tgt, o_fox, o_swa, z, w_o_full, ln_g, ln_b)

    delta_t3 = _fox_rows(delta)
    dq_fox, dk_fox, dv_fox, dcum_k, dcum_q = _fox_bwd(qkv, do_bf, cum_t3, cum, lse_t3, delta_t3)
    dcum_q = jnp.pad(dcum_q.reshape(FOX_HEADS, s).T, ((0, 0), (0, LANES - FOX_HEADS)))
    dff, gbf8 = _cum_bwd(dcum_k, dcum_q, ffp, bfp)
    delta_rows = (delta[:, FOX_HEADS:FOX_HEADS + SWA_HEADS].reshape(s // BLOCK, BLOCK, SWA_KV_HEADS, SWA_GROUP)
                  .transpose(0, 2, 3, 1).reshape(s // BLOCK, SWA_KV_HEADS, 1, SWA_LANES))
    dq_swa, dk_swa, dv_swa, grb, gsk8 = _swa_bwd(qkv, do_bf, delta_rows, lse_swa, bias_t, sink_rows, bucket_t)

    d_misc = jnp.concatenate([dk_swa, dv_swa, dff], axis=1)
    pieces = [dq_fox, dk_fox, dv_fox, dq_swa, d_misc, dz]
    blocks = [(p, 0) for p in pieces[:-1]] + [(dz, 0), (dz, 1)]
    grad_w_pad = _grad_w_matmul(xt, blocks, tk=1024, name="grad_w_in")
    grad_w_in_full = _from_padded_cols(grad_w_pad)

    g_in4 = jnp.stack([jnp.pad(grad_w_in_full[:, j * shard_cols:(j + 1) * shard_cols], col_pad)
                       for j in range(N_CHIPS)])
    g_o4 = grad_w_o_full.reshape(N_CHIPS, D_MODEL // N_CHIPS, D_MODEL)
    parts = _pair_reduce([g_in4, g_o4])
    send_sems, recv_sems, parts_thru, lands_thru, token = _scatter_start(parts)
    grad_x = _grad_x_matmul(pieces, w_pad, dh, token, tm=512, tn=D_MODEL, name="grad_x")
    parts, lands = _scatter_wait(send_sems, recv_sems, parts_thru, lands_thru, grad_x)
    g_w_in, g_w_o = _chip_sum_swap(parts, lands)
    g_w_in = g_w_in[:, :shard_cols]

    cols_first = lambda a: jnp.transpose(a, (2, 0, 1))
    rows_first = lambda a: jnp.transpose(a, (1, 2, 0))
    g_w_in, d_w_in, nm_w_in, nv_w_in = [rows_first(a) for a in _adamw_cols(
        cols_first(w_in), cols_first(g_w_in[None]), cols_first(m_w_in), cols_first(v_w_in), name="adamw_w_in")]
    d_w_o, nm_w_o, nv_w_o = _adamw(w_o2, g_w_o, m_w_o[0], v_w_o[0], name="adamw_w_o")

    loss_row, gs, ds, ms, vs = _small_allreduce_adamw(
        [gbf8, grb, gsk8, gg8, gb8, loss8],
        [b_f, rel_bias, sink, ln_g, ln_b],
        [m_b_f, m_rel_bias, m_sink, m_ln_g, m_ln_b],
        [v_b_f, v_rel_bias, v_sink, v_ln_g, v_ln_b])
    loss = loss_row[0, 0]
    g_bf, g_rb, g_sk, g_lg, g_lb = gs
    d_bf, d_rb, d_sk, d_lg, d_lb = ds
    m_bf, m_rb, m_sk, m_lg, m_lb = ms
    v_bf, v_rb, v_sk, v_lg, v_lb = vs

    e = lambda a: a[None]
    return (loss, e(grad_x),
            g_w_in, g_bf, g_rb, g_sk, e(g_w_o), g_lg, g_lb,
            d_w_in, d_bf, d_rb, d_sk, e(d_w_o), d_lg, d_lb,
            nm_w_in, m_bf, m_rb, m_sk, e(nm_w_o), m_lg, m_lb,
            nv_w_in, v_bf, v_rb, v_sk, e(nv_w_o), v_lg, v_lb)
```

```python
import functools
import math

import numpy as np
import jax
import jax.numpy as jnp
from jax import lax
from jax.experimental import pallas as pl
from jax.experimental.pallas import tpu as pltpu

F32 = jnp.float32
BF16 = jnp.bfloat16

D_MODEL = 1024
HEAD_DIM = 64
FOX_HEADS = 8
SWA_HEADS = 8
SWA_KV_HEADS = 2
SWA_GROUP = 4
FOX_W = 512
SWA_W = 512
BLOCK = 128
NUM_BUCKETS = 32
MAX_DISTANCE = 128
LN_EPS = 1e-5
NEG = -1e30
ALPHA = 2.0 ** 0.25
QK_SCALE = 0.125

ADAM_LR = 0.001
ADAM_B1 = 0.9
ADAM_B2 = 0.999
ADAM_EPS = 1e-08
ADAM_WD = 0.01
ADAM_STEP = 10

D_IN = 3336
SHARD_PAD = 896
N_A = 2304
N_C = 256
N_B = 1024
OFF_C = N_A
OFF_B = N_A + N_C
N_PAD = N_A + N_C + N_B
COL_FK, COL_FV, COL_SQ, COL_SK, COL_SV = 512, 1024, 1536, 2048, 2176

LANES = 128
FOX_T = 256
FOX_REF = 512
SUM_ROWS = 16
VMEM_LIMIT = 56 * 1024 * 1024

MESH = pl.DeviceIdType.MESH
N_CHIPS = 4
N_DEV = 8
SMALL_ROWS = 56


def _cparams(sem=None):
    return pltpu.CompilerParams(dimension_semantics=sem, vmem_limit_bytes=VMEM_LIMIT)


def _split3(x):
    hi = x.astype(BF16)
    r = x - hi.astype(F32)
    mid = r.astype(BF16)
    lo = (r - mid.astype(F32)).astype(BF16)
    return hi, mid, lo


def _dot(a, b):
    return jnp.dot(a, b, preferred_element_type=F32)


def _dot_nt(a, b):
    return lax.dot_general(a, b, (((1,), (1,)), ((), ())), preferred_element_type=F32)


def _project(x, w_pad, token):
    s, k = x.shape
    tm = 512
    chunk = 512

    def kern(x_ref, w_ref, _, qkv_ref, ff_ref, z_ref, xt_ref, vt_ref):
        xf = x_ref[...]
        xb = xf.astype(BF16)
        xt_ref[...] = xf.T.astype(BF16)
        for c0 in range(0, N_A, chunk):
            width = min(chunk, N_A - c0)
            res = _dot(xb, w_ref[:, c0:c0 + width])
            qkv_ref[:, c0:c0 + width] = res.astype(BF16)
            if c0 == COL_FV:
                vt_ref[...] = res.T.astype(BF16)
        ff_ref[...] = _dot(xb, w_ref[:, OFF_C:OFF_C + N_C])
        for c0 in range(0, N_B, 512):
            z_ref[:, c0:c0 + 512] = _dot(xb, w_ref[:, OFF_B + c0:OFF_B + c0 + 512])

    row = lambda i: (i, 0)
    return pl.pallas_call(
        kern, name="project",
        grid=(s // tm,),
        in_specs=[pl.BlockSpec((tm, k), row),
                  _resident((k, N_PAD), lambda i: (0, 0)),
                  _resident(token.shape, lambda i: (0, 0))],
        out_specs=[pl.BlockSpec((tm, N_A), row),
                   pl.BlockSpec((tm, N_C), row),
                   pl.BlockSpec((tm, N_B), row),
                   pl.BlockSpec((k, tm), lambda i: (0, i)),
                   pl.BlockSpec((FOX_W, tm), lambda i: (0, i))],
        out_shape=[jax.ShapeDtypeStruct((s, N_A), BF16),
                   jax.ShapeDtypeStruct((s, N_C), F32),
                   jax.ShapeDtypeStruct((s, N_B), F32),
                   jax.ShapeDtypeStruct((k, s), BF16),
                   jax.ShapeDtypeStruct((FOX_W, s), BF16)],
        compiler_params=_cparams(("parallel",)),
    )(x, w_pad, token)


def _grad_x_matmul(pieces, w_pad, dh, token, *, tm, tn, name):
    m = dh.shape[0]
    n, k = w_pad.shape
    widths = [p.shape[1] for p in pieces]
    offs = [sum(widths[:i]) for i in range(len(pieces))]
    assert sum(widths) == k

    def kern(*refs):
        p_refs, (b_ref, dh_ref, _, o_ref) = refs[:len(pieces)], refs[len(pieces):]
        acc = ALPHA * dh_ref[...]
        for p_ref, off, width in zip(p_refs, offs, widths):
            acc = acc + _dot_nt(p_ref[...], b_ref[:, off:off + width])
        o_ref[...] = acc

    assert tn == n
    return pl.pallas_call(
        kern, name=name,
        grid=(m // tm,),
        in_specs=[pl.BlockSpec((tm, w), lambda i: (i, 0)) for w in widths]
        + [_resident((n, k), lambda i: (0, 0)),
           pl.BlockSpec((tm, n), lambda i: (i, 0)),
           _resident(token.shape, lambda i: (0, 0))],
        out_specs=pl.BlockSpec((tm, n), lambda i: (i, 0)),
        out_shape=jax.ShapeDtypeStruct((m, n), F32),
        compiler_params=_cparams(("parallel",)),
    )(*pieces, w_pad, dh, token)


def _grad_w_matmul(xt, blocks, *, tk, name):
    m, s = xt.shape
    tn = 512
    nb = len(blocks)

    def kern(a_ref, *refs):
        b_refs, o_ref = refs[:nb], refs[nb]

        @pl.when(pl.program_id(0) == 0)
        def _():
            o_ref[...] = jnp.zeros_like(o_ref)
        a = a_ref[...]
        for blk in range(nb):
            o_ref[:, blk * tn:(blk + 1) * tn] += _dot(a, b_refs[blk][...])

    return pl.pallas_call(
        kern, name=name,
        grid=(s // tk,),
        in_specs=[pl.BlockSpec((m, tk), lambda k: (0, k))]
        + [pl.BlockSpec((tk, tn), functools.partial(lambda k, col: (k, col), col=col)) for _, col in blocks],
        out_specs=_resident((m, nb * tn), lambda k: (0, 0)),
        out_shape=jax.ShapeDtypeStruct((m, nb * tn), F32),
        compiler_params=_cparams(("arbitrary",)),
    )(xt, *[arr for arr, _ in blocks])


def _tri(n, lower):
    r = lax.broadcasted_iota(jnp.int32, (n, n), 0)
    c = lax.broadcasted_iota(jnp.int32, (n, n), 1)
    keep = (c <= r) if lower else (c >= r)
    return jnp.where(keep, 1.0, 0.0).astype(BF16)


def _exact_dot(mat_bf16, x_f32, left):
    out = None
    for piece in _split3(x_f32):
        t = _dot(mat_bf16, piece) if left else _dot(piece, mat_bf16)
        out = t if out is None else out + t
    return out


def _log_sigmoid(z):
    return jnp.minimum(z, 0.0) - jnp.log(1.0 + jnp.exp(-jnp.abs(z)))


def _cum_fwd(ffp, bfp):
    s = ffp.shape[0]
    t = min(1024, s)

    def kern(ff_ref, b_ref, cum_ref, carry_ref):
        @pl.when(pl.program_id(0) == 0)
        def _():
            carry_ref[...] = jnp.zeros_like(carry_ref)
        lane = lax.broadcasted_iota(jnp.int32, (1, LANES), 1)
        lf = _log_sigmoid(ff_ref[...] + b_ref[...])
        lf = jnp.where(lane < FOX_HEADS, lf, 0.0)
        cum = _exact_dot(_tri(t, True), lf, True) + carry_ref[0:1, :]
        cum_ref[...] = cum
        carry_ref[...] = jnp.broadcast_to(cum[t - 1:t, :], carry_ref.shape)

    return pl.pallas_call(
        kern, name="cum_fwd",
        grid=(s // t,),
        in_specs=[pl.BlockSpec((t, LANES), lambda i: (i, 0)),
                  pl.BlockSpec((1, LANES), lambda i: (0, 0))],
        out_specs=pl.BlockSpec((t, LANES), lambda i: (i, 0)),
        out_shape=jax.ShapeDtypeStruct((s, LANES), F32),
        scratch_shapes=[pltpu.VMEM((8, LANES), F32)],
        compiler_params=_cparams(("arbitrary",)),
    )(ffp, bfp)


def _cum_bwd(dcum_k, dcum_q, ffp, bfp):
    s = dcum_k.shape[0]
    t = min(1024, s)
    nb = s // t

    def kern(dck_ref, dcq_ref, ff_ref, b_ref, dff_ref, gb_ref, carry_ref):
        @pl.when(pl.program_id(0) == 0)
        def _():
            carry_ref[...] = jnp.zeros_like(carry_ref)
            gb_ref[...] = jnp.zeros_like(gb_ref)
        lane = lax.broadcasted_iota(jnp.int32, (1, LANES), 1)
        dlf = _exact_dot(_tri(t, False), dck_ref[...] + dcq_ref[...], True) + carry_ref[0:1, :]
        carry_ref[...] = jnp.broadcast_to(dlf[0:1, :], carry_ref.shape)
        z = ff_ref[...] + b_ref[...]
        dff = jnp.where(lane < FOX_HEADS, dlf / (1.0 + jnp.exp(z)), 0.0)
        gb_ref[...] += jnp.broadcast_to(jnp.sum(dff, axis=0, keepdims=True), gb_ref.shape)
        dff_ref[...] = jnp.concatenate([dff, jnp.zeros_like(dff)], axis=1).astype(BF16)

    return pl.pallas_call(
        kern, name="cum_bwd",
        grid=(nb,),
        in_specs=[pl.BlockSpec((t, LANES), lambda i: (nb - 1 - i, 0)),
                  pl.BlockSpec((t, LANES), lambda i: (nb - 1 - i, 0)),
                  pl.BlockSpec((t, LANES), lambda i: (nb - 1 - i, 0)),
                  pl.BlockSpec((1, LANES), lambda i: (0, 0))],
        out_specs=[pl.BlockSpec((t, N_C), lambda i: (nb - 1 - i, 0)),
                   pl.BlockSpec((8, LANES), lambda i: (0, 0))],
        out_shape=[jax.ShapeDtypeStruct((s, N_C), BF16),
                   jax.ShapeDtypeStruct((8, LANES), F32)],
        scratch_shapes=[pltpu.VMEM((8, LANES), F32)],
        compiler_params=_cparams(("arbitrary",)),
    )(dcum_k, dcum_q, ffp, bfp)


def _resident(shape, index_map):
    return pl.BlockSpec(shape, index_map, pipeline_mode=pl.Buffered(1))


def _fox_fwd(qkv, vt, cum_t3, cum):
    s = qkv.shape[0]
    tk = tq = FOX_REF
    nq = s // tq
    nh = FOX_HEADS
    diag_tiles = tq // tk

    def kern(q_ref, k_ref, vt_ref, ct_ref, c_ref, o_ref, lse_ref, m_ref, acc_ref, u_ref):
        i = pl.program_id(0)
        lane = lax.broadcasted_iota(jnp.int32, (1, LANES), 1)
        krow = lax.broadcasted_iota(jnp.int32, (tk, tq), 0)
        qcol = lax.broadcasted_iota(jnp.int32, (tk, tq), 1)
        q0 = pl.multiple_of(i * tq, tq)
        qts, crefs = [], []
        for h in range(nh):
            p, a = divmod(h, 2)
            q2 = q_ref[:, p * LANES:(p + 1) * LANES] * jnp.asarray(QK_SCALE, BF16)
            sel = (lane < HEAD_DIM) if a == 0 else (lane >= HEAD_DIM)
            qts.append(jnp.where(sel, q2, jnp.zeros_like(q2)).astype(F32).T.astype(BF16))
            crefs.append(ct_ref[h, :, pl.ds(q0, LANES)][:, 0:1])
        m_ref[...] = jnp.full(m_ref.shape, NEG, F32)
        acc_ref[...] = jnp.zeros_like(acc_ref)
        ones = jnp.ones((SUM_ROWS, tk), BF16)

        def tile(j, diag):
            k0 = pl.multiple_of(j * tk, tk)
            cb = c_ref[pl.ds(k0, tk), :]
            sts = [_dot(k_ref[pl.ds(k0, tk), (h // 2) * LANES:(h // 2 + 1) * LANES], qts[h]) for h in range(nh)]
            tile_max = []
            for h in range(nh):
                u = sts[h] - (cb[:, h:h + 1] - crefs[h])
                if diag is not None:
                    u = jnp.where(krow + diag * tk <= qcol, u, NEG)
                u_ref[h] = u
                tile_max.append(jnp.max(u, axis=0, keepdims=True))
            pts, scales = [], []
            for h in range(nh):
                m_old = m_ref[h]
                m_new = jnp.maximum(m_old, tile_max[h])
                scales.append(jnp.exp(m_old - m_new))
                pts.append(jnp.exp(u_ref[h] - m_new).astype(BF16))
                m_ref[h] = m_new
            for h in range(nh):
                vth = jnp.concatenate([vt_ref[h * HEAD_DIM:(h + 1) * HEAD_DIM, pl.ds(k0, tk)], ones], axis=0)
                acc_ref[h] = scales[h] * acc_ref[h] + _dot(vth, pts[h])

        def body(j, c):
            tile(j, None)
            return c
        lax.fori_loop(0, i * diag_tiles, body, 0)
        for d in range(diag_tiles):
            tile(i * diag_tiles + d, d)

        ls = [acc_ref[h][HEAD_DIM:HEAD_DIM + 1] for h in range(nh)]
        for p in range(nh // 2):
            ot = jnp.concatenate([acc_ref[2 * p + a][:HEAD_DIM] * (1.0 / ls[2 * p + a]) for a in range(2)], axis=0)
            o_ref[:, p * LANES:(p + 1) * LANES] = ot.T
        for h in range(nh):
            lse_ref[h, :, pl.ds(q0, tq)] = m_ref[h] + jnp.log(ls[h])

    return pl.pallas_call(
        kern, name="fox_fwd",
        grid=(nq,),
        in_specs=[pl.BlockSpec((tq, FOX_W), lambda i: (i, 0)),
                  _resident((s, FOX_W), lambda i: (0, COL_FK // FOX_W)),
                  _resident((FOX_W, s), lambda i: (0, 0)),
                  _resident((nh, 1, s), lambda i: (0, 0, 0)),
                  _resident((s, LANES), lambda i: (0, 0))],
        out_specs=[pl.BlockSpec((tq, FOX_W), lambda i: (i, 0)),
                   pl.BlockSpec((nh, 1, s), lambda i: (0, 0, 0))],
        out_shape=[jax.ShapeDtypeStruct((s, FOX_W), F32),
                   jax.ShapeDtypeStruct((nh, 1, s), F32)],
        scratch_shapes=[pltpu.VMEM((nh, 1, tq), F32),
                        pltpu.VMEM((nh, HEAD_DIM + SUM_ROWS, tq), F32),
                        pltpu.VMEM((nh, tk, tq), F32)],
        compiler_params=_cparams(("arbitrary",)),
    )(qkv, qkv, vt, cum_t3, cum)


def _fox_bwd(qkv, do_bf, cum_t3, cum, lse_t3, delta_t3):
    s = qkv.shape[0]
    t = min(FOX_T, s)
    nq = s // t
    nh = FOX_HEADS
    npair = nh // 2

    def kern(q_ref, do_ref, k_ref, v_ref, ct_ref, c_ref, lse_ref, dl_ref,
             dq_ref, dk_ref, dv_ref, dc_ref, dcq_ref, dqt_ref, accv_ref, acck_ref, accd_ref):
        kj = pl.program_id(0)
        lane = lax.broadcasted_iota(jnp.int32, (1, LANES), 1)
        krow = lax.broadcasted_iota(jnp.int32, (t, t), 0)
        qcol = lax.broadcasted_iota(jnp.int32, (t, t), 1)
        causal = krow <= qcol
        sels = [lane < HEAD_DIM, lane >= HEAD_DIM]

        @pl.when(kj == 0)
        def _():
            dqt_ref[...] = jnp.zeros_like(dqt_ref)
            dcq_ref[...] = jnp.zeros_like(dcq_ref)

        cb = c_ref[...]
        k2s, v2s, kts = [], [], []
        for p in range(npair):
            k2 = k_ref[:, p * LANES:(p + 1) * LANES]
            k2s.append(k2)
            v2s.append(v_ref[:, p * LANES:(p + 1) * LANES])
            kt = k2.astype(F32).T * QK_SCALE
            kts.append(kt[:HEAD_DIM].astype(BF16))
            kts.append(kt[HEAD_DIM:].astype(BF16))
        css = [cb[:, h:h + 1] for h in range(nh)]

        def tile(i, masked):
            q0 = pl.multiple_of(i * t, t)
            r0 = pl.multiple_of((i // (FOX_REF // t)) * FOX_REF, FOX_REF)
            sts, dpts, qms, doms = [], [], [], []
            for h in range(nh):
                p, a = divmod(h, 2)
                qi = q_ref[pl.ds(q0, t), p * LANES:(p + 1) * LANES] * jnp.asarray(QK_SCALE, BF16)
                doi = do_ref[pl.ds(q0, t), p * LANES:(p + 1) * LANES]
                qm = jnp.where(sels[a], qi, jnp.zeros_like(qi))
                dom = jnp.where(sels[a], doi, jnp.zeros_like(doi))
                qms.append(qm)
                doms.append(dom)
                sts.append(_dot_nt(k2s[p], qm))
                dpts.append(_dot_nt(v2s[p], dom))
            pts, dsts = [], []
            for h in range(nh):
                cref = ct_ref[h, :, pl.ds(r0, LANES)][:, 0:1]
                pt = jnp.exp(sts[h] - (css[h] - cref) - lse_ref[h, :, pl.ds(q0, t)])
                if masked:
                    pt = jnp.where(causal, pt, 0.0)
                ds32 = pt * (dpts[h] - dl_ref[h, :, pl.ds(q0, t)])
                part = ds32[:, 0:LANES]
                for c in range(1, t // LANES):
                    part = part + ds32[:, c * LANES:(c + 1) * LANES]
                accd_ref[h] = part if masked else accd_ref[h] + part
                dcq_ref[h, :, pl.ds(q0, t)] += jnp.sum(ds32, axis=0, keepdims=True)
                pts.append(pt.astype(BF16))
                dsts.append(ds32.astype(BF16))
            for p in range(npair):
                ha, hb = 2 * p, 2 * p + 1
                dv_p = _dot(pts[ha], doms[ha]) + _dot(pts[hb], doms[hb])
                dk_p = _dot(dsts[ha], qms[ha]) + _dot(dsts[hb], qms[hb])
                accv_ref[p] = dv_p if masked else accv_ref[p] + dv_p
                acck_ref[p] = dk_p if masked else acck_ref[p] + dk_p
            for h in range(nh):
                dqt_ref[h * HEAD_DIM:(h + 1) * HEAD_DIM, pl.ds(q0, t)] += _dot(kts[h], dsts[h])

        tile(kj, True)

        def body(i, c):
            tile(i, False)
            return c
        lax.fori_loop(kj + 1, nq, body, 0)

        dc = jnp.zeros((t, LANES), F32)
        for h in range(nh):
            dc = jnp.where(lane == h, -jnp.sum(accd_ref[h], axis=1, keepdims=True), dc)
        dc_ref[...] = dc
        for p in range(npair):
            dv_ref[:, p * LANES:(p + 1) * LANES] = accv_ref[p].astype(BF16)
            dk_ref[:, p * LANES:(p + 1) * LANES] = acck_ref[p].astype(BF16)

        @pl.when(kj == nq - 1)
        def _():
            for c0 in range(0, s, t):
                dq_ref[c0:c0 + t, :] = dqt_ref[:, c0:c0 + t].T.astype(BF16)

    whole = lambda kj: (0, 0, 0)
    return pl.pallas_call(
        kern, name="fox_bwd",
        grid=(nq,),
        in_specs=[_resident((s, FOX_W), lambda kj: (0, 0)),
                  _resident((s, FOX_W), lambda kj: (0, 0)),
                  pl.BlockSpec((t, FOX_W), lambda kj: (kj, COL_FK // FOX_W)),
                  pl.BlockSpec((t, FOX_W), lambda kj: (kj, COL_FV // FOX_W)),
                  _resident((nh, 1, s), whole),
                  pl.BlockSpec((t, LANES), lambda kj: (kj, 0)),
                  _resident((nh, 1, s), whole),
                  _resident((nh, 1, s), whole)],
        out_specs=[_resident((s, FOX_W), lambda kj: (0, 0)),
                   pl.BlockSpec((t, FOX_W), lambda kj: (kj, 0)),
                   pl.BlockSpec((t, FOX_W), lambda kj: (kj, 0)),
                   pl.BlockSpec((t, LANES), lambda kj: (kj, 0)),
                   _resident((nh, 1, s), whole)],
        out_shape=[jax.ShapeDtypeStruct((s, FOX_W), BF16),
                   jax.ShapeDtypeStruct((s, FOX_W), BF16),
                   jax.ShapeDtypeStruct((s, FOX_W), BF16),
                   jax.ShapeDtypeStruct((s, LANES), F32),
                   jax.ShapeDtypeStruct((nh, 1, s), F32)],
        scratch_shapes=[pltpu.VMEM((FOX_W, s), F32),
                        pltpu.VMEM((npair, t, LANES), F32),
                        pltpu.VMEM((npair, t, LANES), F32),
                        pltpu.VMEM((nh, t, LANES), F32)],
        compiler_params=_cparams(("arbitrary",)),
    )(qkv, do_bf, qkv, qkv, cum_t3, cum, lse_t3, delta_t3)


def _bucket_table():
    qi = np.arange(BLOCK)[:, None]
    kj = np.arange(2 * BLOCK)[None, :]
    rel = np.maximum(qi + BLOCK - kj, 0).astype(np.int32)
    max_exact = NUM_BUCKETS // 2
    relf = np.maximum(rel, 1).astype(np.float32)
    large = max_exact + (np.log(relf / np.float32(max_exact)) / np.float32(math.log(MAX_DISTANCE / max_exact))
                         * np.float32(NUM_BUCKETS - max_exact)).astype(np.int32)
    large = np.minimum(large, NUM_BUCKETS - 1)
    return np.where(rel < max_exact, rel, large).astype(np.int32)


SWA_LANES = SWA_GROUP * BLOCK


def _swa_bias(rel_bias, bucket_t):
    def kern(rb_ref, bk_ref, o_ref):
        bk = bk_ref[...]
        kj = lax.broadcasted_iota(jnp.int32, (2 * BLOCK, BLOCK), 0)
        qi = lax.broadcasted_iota(jnp.int32, (2 * BLOCK, BLOCK), 1)
        rel = qi + BLOCK - kj
        band = (rel >= 0) & (rel < BLOCK)
        masks = [band & (kj >= BLOCK), band]
        for h in range(SWA_HEADS):
            g, hh = divmod(h, SWA_GROUP)
            acc = jnp.zeros((2 * BLOCK, BLOCK), F32)
            for b in range(NUM_BUCKETS):
                acc = jnp.where(bk == b, rb_ref[b, h], acc)
            for first in range(2):
                o_ref[first, g, :, hh * BLOCK:(hh + 1) * BLOCK] = jnp.where(masks[first], acc, NEG)

    return pl.pallas_call(
        kern, name="swa_bias",
        in_specs=[pl.BlockSpec(memory_space=pltpu.SMEM),
                  pl.BlockSpec(memory_space=pltpu.VMEM)],
        out_specs=pl.BlockSpec(memory_space=pltpu.VMEM),
        out_shape=jax.ShapeDtypeStruct((2, SWA_KV_HEADS, 2 * BLOCK, SWA_LANES), F32),
        compiler_params=_cparams(),
    )(rel_bias, bucket_t)


SWA_STEP = 8


def _swa_keys(prev_ref, cur_ref):
    return jnp.concatenate([prev_ref[...], cur_ref[...]], axis=0)


def _swa_queries(x_ref, scale):
    x = x_ref[...]
    if scale:
        x = x * jnp.asarray(QK_SCALE, BF16)
    xt = x.astype(F32).T.astype(BF16)
    return [_group_rows(xt[:, b * BLOCK:(b + 1) * BLOCK]) for b in range(SWA_STEP)]


def _group_rows(xt):
    zeros = jnp.zeros((HEAD_DIM, SWA_LANES), BF16)
    out = []
    for g in range(SWA_KV_HEADS):
        heads = [xt[(SWA_GROUP * g + hh) * HEAD_DIM:(SWA_GROUP * g + hh + 1) * HEAD_DIM, :] for hh in range(SWA_GROUP)]
        rows = jnp.concatenate(heads, axis=1)
        padded = jnp.concatenate([rows, zeros] if g == 0 else [zeros, rows], axis=0)
        out.append((rows, padded))
    return out


def _pairs_to_rows(cols_t):
    out = []
    for p in range(SWA_HEADS // 2):
        g, hh = divmod(2 * p, SWA_GROUP)
        pair = jnp.concatenate([cols_t[g][:, hh * BLOCK:(hh + 1) * BLOCK],
                                cols_t[g][:, (hh + 1) * BLOCK:(hh + 2) * BLOCK]], axis=0)
        out.append(pair.T)
    return jnp.concatenate(out, axis=1)


def _swa_fwd(qkv, bias_t, sink_rows):
    s = qkv.shape[0]
    nb = s // BLOCK
    rows = SWA_STEP * BLOCK
    units = [(b, g) for b in range(SWA_STEP) for g in range(SWA_KV_HEADS)]

    def kern(q_ref, kp_ref, kc_ref, vp_ref, vc_ref, bias_ref, sink_ref, o_ref, lse_ref):
        n = pl.program_id(0)
        tables = [jnp.minimum(n, 1)] + [1] * (SWA_STEP - 1)
        k3 = _swa_keys(kp_ref, kc_ref)
        vt3 = _swa_keys(vp_ref, vc_ref).astype(F32).T.astype(BF16)
        qts = _swa_queries(q_ref, True)
        us = [_dot(k3[b * BLOCK:(b + 2) * BLOCK], qts[b][g][1]) + bias_ref[tables[b], g] for b, g in units]
        outs = []
        for (b, g), u in zip(units, us):
            sk = sink_ref[g]
            m = jnp.maximum(jnp.max(u, axis=0, keepdims=True), sk)
            p = jnp.exp(u - m)
            l = jnp.sum(p, axis=0, keepdims=True) + jnp.exp(sk - m)
            lse_ref[b, g] = m + jnp.log(l)
            vt = vt3[g * HEAD_DIM:(g + 1) * HEAD_DIM, b * BLOCK:(b + 2) * BLOCK]
            outs.append(_dot(vt, (p * (1.0 / l)).astype(BF16)))
        for b in range(SWA_STEP):
            o_ref[b * BLOCK:(b + 1) * BLOCK, :] = _pairs_to_rows(outs[b * SWA_KV_HEADS:(b + 1) * SWA_KV_HEADS])

    cq, ck, cv = COL_SQ // SWA_W, COL_SK // LANES, COL_SV // LANES
    prev = lambda n: jnp.maximum(SWA_STEP * n - 1, 0)
    return pl.pallas_call(
        kern, name="swa_fwd",
        grid=(nb // SWA_STEP,),
        in_specs=[pl.BlockSpec((rows, SWA_W), lambda n: (n, cq)),
                  pl.BlockSpec((BLOCK, LANES), lambda n: (prev(n), ck)),
                  pl.BlockSpec((rows, LANES), lambda n: (n, ck)),
                  pl.BlockSpec((BLOCK, LANES), lambda n: (prev(n), cv)),
                  pl.BlockSpec((rows, LANES), lambda n: (n, cv)),
                  _resident((2, SWA_KV_HEADS, 2 * BLOCK, SWA_LANES), lambda n: (0, 0, 0, 0)),
                  _resident((SWA_KV_HEADS, 1, SWA_LANES), lambda n: (0, 0, 0))],
        out_specs=[pl.BlockSpec((rows, SWA_W), lambda n: (n, 0)),
                   pl.BlockSpec((SWA_STEP, SWA_KV_HEADS, 1, SWA_LANES), lambda n: (n, 0, 0, 0))],
        out_shape=[jax.ShapeDtypeStruct((s, SWA_W), F32),
                   jax.ShapeDtypeStruct((nb, SWA_KV_HEADS, 1, SWA_LANES), F32)],
        compiler_params=_cparams(("parallel",)),
    )(qkv, qkv, qkv, qkv, qkv, bias_t, sink_rows)


def _swa_bwd(qkv, do_bf, delta_rows, lse, bias_t, sink_rows, bucket_t):
    s = qkv.shape[0]
    nb = s // BLOCK
    steps = nb // SWA_STEP
    rows = SWA_STEP * BLOCK
    units = [(b, g) for b in range(SWA_STEP) for g in range(SWA_KV_HEADS)]

    def kern(q_ref, kp_ref, kc_ref, vp_ref, vc_ref, do_ref, dl_ref, lse_ref, bias_ref, sink_ref, bk_ref,
             dq_ref, dk_ref, dv_ref, grb_ref, gsk_ref, dbias_ref, ck_ref, cv_ref, sk_ref):
        n = pl.program_id(0)

        @pl.when(n == 0)
        def _():
            dbias_ref[...] = jnp.zeros_like(dbias_ref)
            ck_ref[...] = jnp.zeros_like(ck_ref)
            cv_ref[...] = jnp.zeros_like(cv_ref)
            sk_ref[...] = jnp.zeros_like(sk_ref)

        @pl.when(n < steps)
        def _():
            tables = [jnp.minimum(n, 1)] + [1] * (SWA_STEP - 1)
            k3 = _swa_keys(kp_ref, kc_ref)
            v3 = _swa_keys(vp_ref, vc_ref)
            kt3 = (k3.astype(F32).T * QK_SCALE).astype(BF16)
            qts = _swa_queries(q_ref, True)
            dots = _swa_queries(do_ref, False)
            sts = [_dot(k3[b * BLOCK:(b + 2) * BLOCK], qts[b][g][1]) for b, g in units]
            dps = [_dot(v3[b * BLOCK:(b + 2) * BLOCK], dots[b][g][1]) for b, g in units]
            ps, dss = [], []
            for i, (b, g) in enumerate(units):
                lse_g = lse_ref[b, g]
                dlt = dl_ref[b, g]
                p = jnp.exp(sts[i] + bias_ref[tables[b], g] - lse_g)
                ds = p * (dps[i] - dlt)
                dbias_ref[g] += ds
                sk_ref[g] += -jnp.exp(sink_ref[g] - lse_g) * dlt
                ps.append(p.astype(BF16))
                dss.append(ds.astype(BF16))
            dk2, dv2 = [], []
            for b in range(SWA_STEP):
                at = lambda g: b * SWA_KV_HEADS + g
                groups = range(SWA_KV_HEADS)
                dv2.append(jnp.concatenate([_dot_nt(dots[b][g][0], ps[at(g)]) for g in groups], axis=0).T)
                dk2.append(jnp.concatenate([_dot_nt(qts[b][g][0], dss[at(g)]) for g in groups], axis=0).T)
                dqts = [_dot(kt3[g * HEAD_DIM:(g + 1) * HEAD_DIM, b * BLOCK:(b + 2) * BLOCK], dss[at(g)]) for g in groups]
                dq_ref[b * BLOCK:(b + 1) * BLOCK, :] = _pairs_to_rows(dqts).astype(BF16)
            last = (SWA_STEP - 1) * BLOCK
            for acc_ref, out_ref, parts in ((ck_ref, dk_ref, dk2), (cv_ref, dv_ref, dv2)):
                done = acc_ref[last:] + parts[0][:BLOCK]
                out_ref[...] = jnp.concatenate([acc_ref[:last], done], axis=0).astype(BF16)
                for b in range(SWA_STEP - 1):
                    acc_ref[b * BLOCK:(b + 1) * BLOCK] = parts[b][BLOCK:] + parts[b + 1][:BLOCK]
                acc_ref[last:] = parts[SWA_STEP - 1][BLOCK:]

        @pl.when(n == steps)
        def _():
            dk_ref[...] = ck_ref[...].astype(BF16)
            dv_ref[...] = cv_ref[...].astype(BF16)
            bk = bk_ref[...]
            lane = lax.broadcasted_iota(jnp.int32, (8, LANES), 1)
            rowi = lax.broadcasted_iota(jnp.int32, (NUM_BUCKETS, LANES), 0)
            lanei = lax.broadcasted_iota(jnp.int32, (NUM_BUCKETS, LANES), 1)
            out = jnp.zeros((NUM_BUCKETS, LANES), F32)
            gsk = jnp.zeros((8, LANES), F32)
            for h in range(SWA_HEADS):
                g, hh = divmod(h, SWA_GROUP)
                cols = slice(hh * BLOCK, (hh + 1) * BLOCK)
                gsk = jnp.where(lane == h, jnp.sum(sk_ref[g][:, cols]), gsk)
                db = dbias_ref[g][:, cols]
                for b in range(NUM_BUCKETS):
                    val = jnp.sum(jnp.where(bk == b, db, 0.0))
                    out = jnp.where((rowi == b) & (lanei == h), val, out)
            grb_ref[...] = out
            gsk_ref[...] = gsk

    cq, ck, cv = COL_SQ // SWA_W, COL_SK // LANES, COL_SV // LANES
    cur = lambda n: jnp.minimum(n, steps - 1)
    prev = lambda n: jnp.maximum(SWA_STEP * cur(n) - 1, 0)
    kout = lambda n: jnp.maximum(n - 1, 0)
    stat = pl.BlockSpec((SWA_STEP, SWA_KV_HEADS, 1, SWA_LANES), lambda n: (cur(n), 0, 0, 0))
    return pl.pallas_call(
        kern, name="swa_bwd",
        grid=(steps + 1,),
        in_specs=[pl.BlockSpec((rows, SWA_W), lambda n: (cur(n), cq)),
                  pl.BlockSpec((BLOCK, LANES), lambda n: (prev(n), ck)),
                  pl.BlockSpec((rows, LANES), lambda n: (cur(n), ck)),
                  pl.BlockSpec((BLOCK, LANES), lambda n: (prev(n), cv)),
                  pl.BlockSpec((rows, LANES), lambda n: (cur(n), cv)),
                  pl.BlockSpec((rows, SWA_W), lambda n: (cur(n), 1)),
                  stat, stat,
                  _resident((2, SWA_KV_HEADS, 2 * BLOCK, SWA_LANES), lambda n: (0, 0, 0, 0)),
                  _resident((SWA_KV_HEADS, 1, SWA_LANES), lambda n: (0, 0, 0)),
                  _resident((2 * BLOCK, BLOCK), lambda n: (0, 0))],
        out_specs=[pl.BlockSpec((rows, SWA_W), lambda n: (cur(n), 0)),
                   pl.BlockSpec((rows, LANES), lambda n: (kout(n), 0)),
                   pl.BlockSpec((rows, LANES), lambda n: (kout(n), 0)),
                   pl.BlockSpec((NUM_BUCKETS, LANES), lambda n: (0, 0)),
                   pl.BlockSpec((8, LANES), lambda n: (0, 0))],
        out_shape=[jax.ShapeDtypeStruct((s, SWA_W), BF16),
                   jax.ShapeDtypeStruct((s, LANES), BF16),
                   jax.ShapeDtypeStruct((s, LANES), BF16),
                   jax.ShapeDtypeStruct((NUM_BUCKETS, LANES), F32),
                   jax.ShapeDtypeStruct((8, LANES), F32)],
        scratch_shapes=[pltpu.VMEM((SWA_KV_HEADS, 2 * BLOCK, SWA_LANES), F32),
                        pltpu.VMEM((rows, LANES), F32),
                        pltpu.VMEM((rows, LANES), F32),
                        pltpu.VMEM((SWA_KV_HEADS, 1, SWA_LANES), F32)],
        compiler_params=_cparams(("arbitrary",)),
    )(qkv, qkv, qkv, qkv, qkv, do_bf, delta_rows, lse, bias_t, sink_rows, bucket_t)


def _post(x, target, o_fox, o_swa, z, w_o, ln_g, ln_b):
    s = x.shape[0]
    tm = min(256, s)
    nt = s // tm

    def kern(x_ref, t_ref, of_ref, os_ref, z_ref, w_ref, g_ref, b_ref,
             loss_ref, dh_ref, gwo_ref, do_ref, dz_ref, dl_ref, gg_ref, gb_ref, lacc_ref):
        step = pl.program_id(0)

        @pl.when(step == 0)
        def _():
            lacc_ref[...] = jnp.zeros_like(lacc_ref)
            gg_ref[...] = jnp.zeros_like(gg_ref)
            gwo_ref[...] = jnp.zeros_like(gwo_ref)
            gb_ref[...] = jnp.zeros_like(gb_ref)

        o = jnp.concatenate([of_ref[...], os_ref[...]], axis=1)
        zz = z_ref[...]
        sig = 1.0 / (1.0 + jnp.exp(-zz))
        silu = zz * sig
        mixed32 = o * silu
        mixed = mixed32.astype(BF16)
        w = w_ref[...]
        h = ALPHA * x_ref[...] + _dot(mixed, w)
        mu = jnp.mean(h, axis=1, keepdims=True)
        hc = h - mu
        var = jnp.mean(hc * hc, axis=1, keepdims=True)
        rstd = lax.rsqrt(var + LN_EPS)
        xhat = hc * rstd
        g = g_ref[...]
        err = xhat * g + b_ref[...] - t_ref[...]
        lacc_ref[...] += jnp.broadcast_to(jnp.sum(err * err, axis=0, keepdims=True), lacc_ref.shape)
        dout = err * (1.0 / D_MODEL)
        gg_ref[...] += jnp.broadcast_to(jnp.sum(dout * xhat, axis=0, keepdims=True), gg_ref.shape)
        gb_ref[...] += jnp.broadcast_to(jnp.sum(dout, axis=0, keepdims=True), gb_ref.shape)
        dxh = dout * g
        m1 = jnp.mean(dxh, axis=1, keepdims=True)
        m2 = jnp.mean(dxh * xhat, axis=1, keepdims=True)
        dh = rstd * (dxh - m1 - xhat * m2)
        dh_ref[...] = dh
        dy = dh.astype(BF16)
        gwo_ref[...] += _dot(mixed32.T.astype(BF16), dy)
        dmix = _dot_nt(dy, w)
        do = dmix * silu
        do_ref[...] = do.astype(BF16)
        dz_ref[...] = (dmix * o * (sig * (1.0 + zz * (1.0 - sig)))).astype(BF16)
        r = lax.broadcasted_iota(jnp.int32, (D_MODEL, LANES), 0) // HEAD_DIM
        c = lax.broadcasted_iota(jnp.int32, (D_MODEL, LANES), 1)
        pick = jnp.where(r == c, 1.0, 0.0).astype(BF16)
        dl_ref[...] = _exact_dot(pick, do * o, False)

        @pl.when(step == nt - 1)
        def _():
            tot = jnp.sum(lacc_ref[0:1, :]) * (0.5 / D_MODEL)
            loss_ref[...] = jnp.broadcast_to(tot, loss_ref.shape)

    row = lambda i: (i, 0)
    fixed = lambda i: (0, 0)
    wide = pl.BlockSpec((tm, D_MODEL), row)
    half = pl.BlockSpec((tm, FOX_W), row)
    return pl.pallas_call(
        kern, name="post",
        grid=(nt,),
        in_specs=[wide, wide, half, half, wide,
                  pl.BlockSpec((D_MODEL, D_MODEL), fixed),
                  pl.BlockSpec((1, D_MODEL), fixed),
                  pl.BlockSpec((1, D_MODEL), fixed)],
        out_specs=[pl.BlockSpec((8, LANES), fixed), wide,
                   _resident((D_MODEL, D_MODEL), fixed), wide, wide,
                   pl.BlockSpec((tm, LANES), row),
                   pl.BlockSpec((8, D_MODEL), fixed), pl.BlockSpec((8, D_MODEL), fixed)],
        out_shape=[jax.ShapeDtypeStruct((8, LANES), F32),
                   jax.ShapeDtypeStruct((s, D_MODEL), F32),
                   jax.ShapeDtypeStruct((D_MODEL, D_MODEL), F32),
                   jax.ShapeDtypeStruct((s, D_MODEL), BF16),
                   jax.ShapeDtypeStruct((s, D_MODEL), BF16),
                   jax.ShapeDtypeStruct((s, LANES), F32),
                   jax.ShapeDtypeStruct((8, D_MODEL), F32),
                   jax.ShapeDtypeStruct((8, D_MODEL), F32)],
        scratch_shapes=[pltpu.VMEM((8, D_MODEL), F32)],
        compiler_params=_cparams(("arbitrary",)),
    )(x, target, o_fox, o_swa, z, w_o, ln_g, ln_b)


def _adamw_math(w, g, m, v):
    m = ADAM_B1 * m + (1.0 - ADAM_B1) * g
    v = ADAM_B2 * v + (1.0 - ADAM_B2) * (g * g)
    m_hat = m / (1.0 - ADAM_B1 ** ADAM_STEP)
    v_hat = v / (1.0 - ADAM_B2 ** ADAM_STEP)
    delta = -ADAM_LR * (m_hat / (jnp.sqrt(v_hat) + ADAM_EPS) + ADAM_WD * w)
    return delta, m, v


def _adamw(w, g, m, v, *, name):
    r, c = w.shape
    tr = min(256, r)

    def kern(w_ref, g_ref, m_ref, v_ref, d_ref, mo_ref, vo_ref):
        d, mn, vn = _adamw_math(w_ref[...], g_ref[...], m_ref[...], v_ref[...])
        d_ref[...] = d
        mo_ref[...] = mn
        vo_ref[...] = vn

    blk = pl.BlockSpec((tr, c), lambda i: (i, 0))
    sds = jax.ShapeDtypeStruct((r, c), F32)
    return pl.pallas_call(
        kern, name=name,
        grid=(r // tr,),
        in_specs=[blk, blk, blk, blk],
        out_specs=[blk, blk, blk],
        out_shape=[sds, sds, sds],
        compiler_params=_cparams(("parallel",)),
    )(w, g, m, v)


def _adamw_cols(w, g, m, v, *, name):
    c, _, r = w.shape
    tc = 139
    assert c % tc == 0

    def kern(w_ref, g_ref, m_ref, v_ref, go_ref, d_ref, mo_ref, vo_ref):
        g = g_ref[...]
        d, mn, vn = _adamw_math(w_ref[...], g, m_ref[...], v_ref[...])
        go_ref[...] = g
        d_ref[...] = d
        mo_ref[...] = mn
        vo_ref[...] = vn

    blk = pl.BlockSpec((tc, 1, r), lambda i: (i, 0, 0))
    sds = jax.ShapeDtypeStruct((c, 1, r), F32)
    return pl.pallas_call(
        kern, name=name,
        grid=(c // tc,),
        in_specs=[blk, blk, blk, blk],
        out_specs=[blk, blk, blk, blk],
        out_shape=[sds, sds, sds, sds],
        compiler_params=_cparams(("parallel",)),
    )(w, g, m, v)


def _position():
    x, y, c = lax.axis_index("x"), lax.axis_index("y"), lax.axis_index("c")
    chips = [(1 - x, y), (x, 1 - y), (1 - x, 1 - y)]
    return x, y, c, chips


def _chip_index(cx, cy):
    return 2 * cx + cy


def _gather_weights(*shards):
    n_arr = len(shards)

    def kern(*refs):
        ins, outs = refs[:n_arr], refs[n_arr:2 * n_arr]
        send_sems, recv_sems, local_sems = refs[2 * n_arr:]
        x, y, c, chips = _position()
        me = _chip_index(x, y)
        sibling = (x, y, 1 - c)

        local = [pltpu.make_async_copy(ins[a], outs[a].at[me], local_sems.at[a]) for a in range(n_arr)]
        for cp in local:
            cp.start()

        def half(ref, a):
            rows = shards[a].shape[0] // 2
            return ref.at[pl.ds(c * rows, rows), :]

        def copy(a, k, src, slot, to):
            return pltpu.make_async_remote_copy(
                src_ref=src, dst_ref=half(outs[a].at[slot], a),
                send_sem=send_sems.at[a * 6 + k], recv_sem=recv_sems.at[a * 6 + k],
                device_id=to, device_id_type=MESH)

        first = [copy(a, j, half(ins[a], a), me, (*chip, c)) for a in range(n_arr) for j, chip in enumerate(chips)]
        for cp in first:
            cp.start()
        passed = []
        for a in range(n_arr):
            for j, chip in enumerate(chips):
                slot = _chip_index(*chip)
                copy(a, j, half(ins[a], a), slot, (*chip, c)).wait_recv()
                fwd = copy(a, 3 + j, half(outs[a].at[slot], a), slot, sibling)
                fwd.start()
                passed.append(fwd)
        for a in range(n_arr):
            for j, chip in enumerate(chips):
                slot = _chip_index(*chip)
                rows = shards[a].shape[0] // 2
                dst = outs[a].at[slot].at[pl.ds((1 - c) * rows, rows), :]
                pltpu.make_async_remote_copy(
                    src_ref=dst, dst_ref=dst, send_sem=send_sems.at[a * 6 + 3 + j],
                    recv_sem=recv_sems.at[a * 6 + 3 + j], device_id=sibling, device_id_type=MESH).wait_recv()
        for cp in first + passed:
            cp.wait_send()
        for cp in local:
            cp.wait()

    vmem = pl.BlockSpec(memory_space=pltpu.VMEM)
    return pl.pallas_call(
        kern, name="gather_weights",
        in_specs=[vmem] * n_arr,
        out_specs=[vmem] * n_arr,
        out_shape=[jax.ShapeDtypeStruct((N_CHIPS,) + w.shape, w.dtype) for w in shards],
        scratch_shapes=[pltpu.SemaphoreType.DMA((6 * n_arr,)),
                        pltpu.SemaphoreType.DMA((6 * n_arr,)),
                        pltpu.SemaphoreType.DMA((n_arr,))],
        compiler_params=_cparams(),
    )(*shards)


def _pair_reduce(grads):
    n_arr = len(grads)
    chunk = 128

    def kern(*refs):
        ins = refs[:n_arr]
        outs = refs[n_arr:2 * n_arr]
        gots = refs[2 * n_arr:3 * n_arr]
        send_sems, recv_sems = refs[3 * n_arr:]
        x, y, c, _ = _position()
        sibling = (x, y, 1 - c)
        copies = []
        for a in range(n_arr):
            rows = grads[a].shape[1] // 2
            copies.append(pltpu.make_async_remote_copy(
                src_ref=ins[a].at[:, pl.ds((1 - c) * rows, rows), :], dst_ref=gots[a],
                send_sem=send_sems.at[a], recv_sem=recv_sems.at[a], device_id=sibling, device_id_type=MESH))
        for cp in copies:
            cp.start()
        for a in range(n_arr):
            copies[a].wait()
            rows = grads[a].shape[1] // 2
            for j in range(N_CHIPS):
                for r0 in range(0, rows, chunk):
                    mine = ins[a][j, pl.ds(pl.multiple_of(c * rows + r0, chunk), chunk), :]
                    outs[a][j, r0:r0 + chunk, :] = (mine + gots[a][j, r0:r0 + chunk, :]).astype(BF16)

    vmem = pl.BlockSpec(memory_space=pltpu.VMEM)
    half = [(N_CHIPS, g.shape[1] // 2, g.shape[2]) for g in grads]
    return pl.pallas_call(
        kern, name="pair_reduce",
        in_specs=[vmem] * n_arr,
        out_specs=[vmem] * n_arr,
        out_shape=[jax.ShapeDtypeStruct(h, BF16) for h in half],
        scratch_shapes=[pltpu.VMEM(h, F32) for h in half]
        + [pltpu.SemaphoreType.DMA((n_arr,)), pltpu.SemaphoreType.DMA((n_arr,))],
        compiler_params=_cparams(),
    )(*grads)


def _wo_gather_start(shard):
    hbm = pl.BlockSpec(memory_space=pltpu.HBM)
    sem = pl.BlockSpec(memory_space=pltpu.SEMAPHORE)
    land_shape = (N_CHIPS,) + shard.shape

    def kern(src_ref, land_ref, send_sems, recv_sems, src_thru, land_thru, token):
        x, y, c, chips = _position()
        me = _chip_index(x, y)
        for j, chip in enumerate(chips):
            pltpu.make_async_remote_copy(
                src_ref=src_ref, dst_ref=land_ref.at[me], send_sem=send_sems.at[j], recv_sem=recv_sems.at[j],
                device_id=(*chip, c), device_id_type=MESH).start()
        token[...] = jnp.zeros_like(token)

    return pl.pallas_call(
        kern, name="wo_gather_start",
        in_specs=[hbm, hbm],
        out_specs=(sem, sem, hbm, hbm, pl.BlockSpec(memory_space=pltpu.VMEM)),
        out_shape=(pltpu.SemaphoreType.DMA((3,)), pltpu.SemaphoreType.DMA((3,)),
                   pltpu.HBM(shard.shape, shard.dtype), pltpu.HBM(land_shape, shard.dtype),
                   jax.ShapeDtypeStruct((8, LANES), F32)),
        input_output_aliases={0: 2, 1: 3},
        compiler_params=pltpu.CompilerParams(has_side_effects=pltpu.SideEffectType.DATAFLOW_SIDE_EFFECTING),
    )(pltpu.with_memory_space_constraint(shard, pltpu.HBM),
      pltpu.with_memory_space_constraint(lax.empty(land_shape, shard.dtype), pltpu.HBM))


def _wo_gather_wait(send_sems, recv_sems, src_thru, land_thru, after):
    hbm = pl.BlockSpec(memory_space=pltpu.HBM)
    sem = pl.BlockSpec(memory_space=pltpu.SEMAPHORE)

    def kern(src_ref, land_ref, send_sems, recv_sems, after_ref, src_out, land_out):
        x, y, c, chips = _position()
        for j, chip in enumerate(chips):
            copy = pltpu.make_async_remote_copy(
                src_ref=src_ref, dst_ref=land_ref.at[_chip_index(*chip)], send_sem=send_sems.at[j],
                recv_sem=recv_sems.at[j], device_id=(*chip, c), device_id_type=MESH)
            copy.wait_send()
            copy.wait_recv()

    return pl.pallas_call(
        kern, name="wo_gather_wait",
        in_specs=[hbm, hbm, sem, sem, pl.BlockSpec(memory_space=pl.ANY)],
        out_specs=[hbm, hbm],
        out_shape=[pltpu.HBM(src_thru.shape, src_thru.dtype), pltpu.HBM(land_thru.shape, land_thru.dtype)],
        input_output_aliases={0: 0, 1: 1},
        compiler_params=pltpu.CompilerParams(has_side_effects=pltpu.SideEffectType.DATAFLOW_SIDE_EFFECTING),
    )(src_thru, land_thru, send_sems, recv_sems, after)[1]


def _scatter_start(parts):
    n_arr = len(parts)
    hbm = pl.BlockSpec(memory_space=pltpu.HBM)
    sem = pl.BlockSpec(memory_space=pltpu.SEMAPHORE)

    def kern(*refs):
        ins, lands = refs[:n_arr], refs[n_arr:2 * n_arr]
        send_sems, recv_sems, token = refs[2 * n_arr], refs[2 * n_arr + 1], refs[-1]
        x, y, c, chips = _position()
        me = _chip_index(x, y)
        for a in range(n_arr):
            for j, chip in enumerate(chips):
                pltpu.make_async_remote_copy(
                    src_ref=ins[a].at[_chip_index(*chip)], dst_ref=lands[a].at[me],
                    send_sem=send_sems.at[a * 3 + j], recv_sem=recv_sems.at[a * 3 + j],
                    device_id=(*chip, c), device_id_type=MESH).start()
        token[...] = jnp.zeros_like(token)

    slab = [pltpu.HBM(p.shape, p.dtype) for p in parts]
    outs = pl.pallas_call(
        kern, name="scatter_start",
        in_specs=[hbm] * (2 * n_arr),
        out_specs=(sem, sem, *[hbm] * (2 * n_arr), pl.BlockSpec(memory_space=pltpu.VMEM)),
        out_shape=(pltpu.SemaphoreType.DMA((3 * n_arr,)), pltpu.SemaphoreType.DMA((3 * n_arr,)),
                   *slab, *slab, jax.ShapeDtypeStruct((8, LANES), F32)),
        input_output_aliases={i: 2 + i for i in range(2 * n_arr)},
        compiler_params=pltpu.CompilerParams(has_side_effects=pltpu.SideEffectType.DATAFLOW_SIDE_EFFECTING),
    )(*[pltpu.with_memory_space_constraint(p, pltpu.HBM) for p in parts],
      *[pltpu.with_memory_space_constraint(lax.empty(p.shape, p.dtype), pltpu.HBM) for p in parts])
    return outs[0], outs[1], outs[2:2 + n_arr], outs[2 + n_arr:2 + 2 * n_arr], outs[-1]


def _scatter_wait(send_sems, recv_sems, parts_thru, lands_thru, after):
    n_arr = len(parts_thru)
    hbm = pl.BlockSpec(memory_space=pltpu.HBM)
    sem = pl.BlockSpec(memory_space=pltpu.SEMAPHORE)

    def kern(*refs):
        ins, lands = refs[:n_arr], refs[n_arr:2 * n_arr]
        send_ref, recv_ref = refs[2 * n_arr], refs[2 * n_arr + 1]
        x, y, c, chips = _position()
        for a in range(n_arr):
            for j, chip in enumerate(chips):
                slot = _chip_index(*chip)
                copy = pltpu.make_async_remote_copy(
                    src_ref=ins[a].at[slot], dst_ref=lands[a].at[slot],
                    send_sem=send_ref.at[a * 3 + j], recv_sem=recv_ref.at[a * 3 + j],
                    device_id=(*chip, c), device_id_type=MESH)
                copy.wait_send()
                copy.wait_recv()

    slab = [pltpu.HBM(p.shape, p.dtype) for p in parts_thru]
    outs = pl.pallas_call(
        kern, name="scatter_wait",
        in_specs=[hbm] * (2 * n_arr) + [sem, sem, pl.BlockSpec(memory_space=pl.ANY)],
        out_specs=[hbm] * (2 * n_arr),
        out_shape=slab + slab,
        input_output_aliases={i: i for i in range(2 * n_arr)},
        compiler_params=pltpu.CompilerParams(has_side_effects=pltpu.SideEffectType.DATAFLOW_SIDE_EFFECTING),
    )(*parts_thru, *lands_thru, send_sems, recv_sems, after)
    return outs[:n_arr], outs[n_arr:]


def _chip_sum_swap(parts, lands):
    n_arr = len(parts)
    chunk = 128

    def kern(*refs):
        own, got = refs[:n_arr], refs[n_arr:2 * n_arr]
        outs = refs[2 * n_arr:3 * n_arr]
        sums = refs[3 * n_arr:4 * n_arr]
        swap_send, swap_recv, swap_local = refs[4 * n_arr:]
        x, y, c, _ = _position()
        me = _chip_index(x, y)
        sibling = (x, y, 1 - c)
        for a in range(n_arr):
            for r0 in range(0, parts[a].shape[1], chunk):
                mine = own[a][me, r0:r0 + chunk, :].astype(F32)

                def term(i):
                    other = got[a][jnp.where(i == me, (i + 1) % N_CHIPS, i), r0:r0 + chunk, :].astype(F32)
                    return jnp.where(i == me, mine, other)
                sums[a][r0:r0 + chunk, :] = ((term(0) + term(1)) + term(2)) + term(3)
        swap_l, swap_r = [], []
        for a in range(n_arr):
            rows = parts[a].shape[1]
            mine = outs[a].at[pl.ds(c * rows, rows), :]
            swap_l.append(pltpu.make_async_copy(sums[a], mine, swap_local.at[a]))
            swap_r.append(pltpu.make_async_remote_copy(
                src_ref=sums[a], dst_ref=mine, send_sem=swap_send.at[a], recv_sem=swap_recv.at[a],
                device_id=sibling, device_id_type=MESH))
        for cp in swap_l + swap_r:
            cp.start()
        for a in range(n_arr):
            rows = parts[a].shape[1]
            theirs = outs[a].at[pl.ds((1 - c) * rows, rows), :]
            pltpu.make_async_remote_copy(
                src_ref=theirs, dst_ref=theirs, send_sem=swap_send.at[a], recv_sem=swap_recv.at[a],
                device_id=sibling, device_id_type=MESH).wait_recv()
        for cp in swap_r:
            cp.wait_send()
        for cp in swap_l:
            cp.wait()

    vmem = pl.BlockSpec(memory_space=pltpu.VMEM)
    return pl.pallas_call(
        kern, name="chip_sum_swap",
        in_specs=[vmem] * (2 * n_arr),
        out_specs=[vmem] * n_arr,
        out_shape=[jax.ShapeDtypeStruct((2 * p.shape[1], p.shape[2]), F32) for p in parts],
        scratch_shapes=[pltpu.VMEM(p.shape[1:], F32) for p in parts]
        + [pltpu.SemaphoreType.DMA((n_arr,)),
           pltpu.SemaphoreType.DMA((n_arr,)),
           pltpu.SemaphoreType.DMA((n_arr,))],
        compiler_params=_cparams(),
    )(*parts, *lands)


def _small_allreduce_adamw(partials, params, moms, vels):
    chunks = D_MODEL // LANES
    row_rb, row_bf, row_sk, row_loss = 2 * chunks, 2 * chunks + NUM_BUCKETS, 2 * chunks + NUM_BUCKETS + 1, SMALL_ROWS - 6

    def kern(gbf_ref, grb_ref, gsk_ref, gg_ref, gb_ref, loss_ref, *refs):
        p_refs, m_refs, v_refs = refs[0:5], refs[5:10], refs[10:15]
        lo_ref, g_outs, d_outs, mo_outs, vo_outs = refs[15], refs[16:21], refs[21:26], refs[26:31], refs[31:36]
        send_ref, buf_ref, send_sems, recv_sems = refs[36:]
        x, y, c, _ = _position()
        me = 4 * x + 2 * y + c
        send_ref[...] = jnp.zeros_like(send_ref)
        for r in range(chunks):
            send_ref[r:r + 1, :] = gg_ref[0:1, r * LANES:(r + 1) * LANES]
            send_ref[chunks + r:chunks + r + 1, :] = gb_ref[0:1, r * LANES:(r + 1) * LANES]
        send_ref[row_rb:row_rb + NUM_BUCKETS, :] = grb_ref[...]
        send_ref[row_bf:row_bf + 1, :] = gbf_ref[0:1, :]
        send_ref[row_sk:row_sk + 1, :] = gsk_ref[0:1, :]
        send_ref[row_loss:row_loss + 1, :] = loss_ref[0:1, :]
        buf_ref[me] = send_ref[...]
        peers = [(x, y, 1 - c)] + [(px, py, pc) for px, py in _position()[3] for pc in (c, 1 - c)]
        sends = []
        for k, peer in enumerate(peers):
            sends.append(pltpu.make_async_remote_copy(
                src_ref=send_ref, dst_ref=buf_ref.at[me], send_sem=send_sems.at[k], recv_sem=recv_sems.at[k],
                device_id=peer, device_id_type=MESH))
        for cp in sends:
            cp.start()
        for k, (px, py, pc) in enumerate(peers):
            slot = buf_ref.at[4 * px + 2 * py + pc]
            pltpu.make_async_remote_copy(
                src_ref=slot, dst_ref=slot, send_sem=send_sems.at[k], recv_sem=recv_sems.at[k],
                device_id=(px, py, pc), device_id_type=MESH).wait_recv()
        for cp in sends:
            cp.wait_send()
        tot = buf_ref[0]
        for d in range(1, N_DEV):
            tot = tot + buf_ref[d]
        lo_ref[...] = tot[row_loss:row_loss + 1, :]
        grads = [tot[row_bf:row_bf + 1, 0:FOX_HEADS],
                 tot[row_rb:row_rb + NUM_BUCKETS, 0:SWA_HEADS],
                 tot[row_sk:row_sk + 1, 0:SWA_HEADS],
                 jnp.concatenate([tot[r:r + 1, :] for r in range(chunks)], axis=1),
                 jnp.concatenate([tot[chunks + r:chunks + r + 1, :] for r in range(chunks)], axis=1)]
        for i, g in enumerate(grads):
            g_outs[i][...] = g
            delta, mn, vn = _adamw_math(p_refs[i][...], g, m_refs[i][...], v_refs[i][...])
            d_outs[i][...] = delta
            mo_outs[i][...] = mn
            vo_outs[i][...] = vn

    vm = pl.BlockSpec(memory_space=pltpu.VMEM)
    shapes = [jax.ShapeDtypeStruct(p.shape, F32) for p in params]
    outs = pl.pallas_call(
        kern, name="small_allreduce_adamw",
        in_specs=[vm] * 21,
        out_specs=[vm] * 21,
        out_shape=[jax.ShapeDtypeStruct((1, LANES), F32)] + shapes * 4,
        scratch_shapes=[pltpu.VMEM((SMALL_ROWS, LANES), F32),
                        pltpu.VMEM((N_DEV, SMALL_ROWS, LANES), F32),
                        pltpu.SemaphoreType.DMA((N_DEV - 1,)),
                        pltpu.SemaphoreType.DMA((N_DEV - 1,))],
    )(*partials, *params, *moms, *vels)
    return outs[0], outs[1:6], outs[6:11], outs[11:16], outs[16:21]


def _to_padded_cols(w):
    pad = jnp.zeros((w.shape[0], N_C - FOX_HEADS), w.dtype)
    return jnp.concatenate([w[:, 0:1536], w[:, 2056:2824], w[:, 1536:1544], pad,
                            w[:, 1544:2056], w[:, 2824:3336]], axis=1)


def _from_padded_cols(g):
    return jnp.concatenate([g[:, 0:1536], g[:, OFF_C:OFF_C + FOX_HEADS], g[:, OFF_B:OFF_B + FOX_W],
                            g[:, 1536:N_A], g[:, OFF_B + FOX_W:N_PAD]], axis=1)


def _fox_rows(a):
    return a[:, :FOX_HEADS].T.reshape(FOX_HEADS, 1, a.shape[0])


def kernel(x, w_in, b_f, rel_bias, sink, w_o, ln_g, ln_b, loss_target, m_w_in, m_b_f, m_rel_bias, m_sink, m_w_o, m_ln_g, m_ln_b, v_w_in, v_b_f, v_rel_bias, v_sink, v_w_o, v_ln_g, v_ln_b):
    x2 = x[0]
    tgt = loss_target[0]
    s = x2.shape[0]
    w_in2, w_o2 = w_in[0], w_o[0]

    shard_cols = D_IN // N_CHIPS
    col_pad = ((0, 0), (0, SHARD_PAD - shard_cols))
    (w_in_all,) = _gather_weights(jnp.pad(w_in2.astype(BF16), col_pad))
    w_full = jnp.concatenate([w_in_all[j, :, :shard_cols] for j in range(N_CHIPS)], axis=1)
    w_pad = _to_padded_cols(w_full)
    w_o_bf, _ = lax.optimization_barrier((w_o2.astype(BF16), w_in_all))
    wo_send, wo_recv, wo_src, wo_land, wo_token = _wo_gather_start(w_o_bf)

    qkv, ffp, z, xt, vt = _project(x2, w_pad, wo_token)
    bfp = jnp.pad(b_f, ((0, 0), (0, LANES - FOX_HEADS)))
    cum = _cum_fwd(ffp, bfp)
    cum_t3 = _fox_rows(cum)
    o_fox, lse_t3 = _fox_fwd(qkv, vt, cum_t3, cum)
    bucket_t = jnp.asarray(_bucket_table().T)
    bias_t = _swa_bias(rel_bias, bucket_t)
    sink_rows = jnp.repeat(sink.reshape(SWA_KV_HEADS, SWA_GROUP, 1), BLOCK, axis=2).reshape(SWA_KV_HEADS, 1, SWA_LANES)
    o_swa, lse_swa = _swa_fwd(qkv, bias_t, sink_rows)

    wo_land = _wo_gather_wait(wo_send, wo_recv, wo_src, wo_land, o_swa)
    my_chip = _chip_index(lax.axis_index("x"), lax.axis_index("y"))
    w_o_full = lax.dynamic_update_slice(wo_land, w_o_bf[None], (my_chip, 0, 0)).reshape(D_MODEL, D_MODEL)
    loss8, dh, grad_w_o_full, do_bf, dz, delta, gg8, gb8 = _post(
        x2, tgt, o_fox, o_swa, z, w_o_full, ln_g, ln_b)

    delta_t3 = _fox_rows(delta)
    dq_fox, dk_fox, dv_fox, dcum_k, dcum_q = _fox_bwd(qkv, do_bf, cum_t3, cum, lse_t3, delta_t3)
    dcum_q = jnp.pad(dcum_q.reshape(FOX_HEADS, s).T, ((0, 0), (0, LANES - FOX_HEADS)))
    dff, gbf8 = _cum_bwd(dcum_k, dcum_q, ffp, bfp)
    delta_rows = (delta[:, FOX_HEADS:FOX_HEADS + SWA_HEADS].reshape(s // BLOCK, BLOCK, SWA_KV_HEADS, SWA_GROUP)
                  .transpose(0, 2, 3, 1).reshape(s // BLOCK, SWA_KV_HEADS, 1, SWA_LANES))
    dq_swa, dk_swa, dv_swa, grb, gsk8 = _swa_bwd(qkv, do_bf, delta_rows, lse_swa, bias_t, sink_rows, bucket_t)

    d_misc = jnp.concatenate([dk_swa, dv_swa, dff], axis=1)
    pieces = [dq_fox, dk_fox, dv_fox, dq_swa, d_misc, dz]
    blocks = [(p, 0) for p in pieces[:-1]] + [(dz, 0), (dz, 1)]
    grad_w_pad = _grad_w_matmul(xt, blocks, tk=1024, name="grad_w_in")
    grad_w_in_full = _from_padded_cols(grad_w_pad)

    g_in4 = jnp.stack([jnp.pad(grad_w_in_full[:, j * shard_cols:(j + 1) * shard_cols], col_pad)
                       for j in range(N_CHIPS)])
    g_o4 = grad_w_o_full.reshape(N_CHIPS, D_MODEL // N_CHIPS, D_MODEL)
    parts = _pair_reduce([g_in4, g_o4])
    send_sems, recv_sems, parts_thru, lands_thru, token = _scatter_start(parts)
    grad_x = _grad_x_matmul(pieces, w_pad, dh, token, tm=512, tn=D_MODEL, name="grad_x")
    parts, lands = _scatter_wait(send_sems, recv_sems, parts_thru, lands_thru, grad_x)
    g_w_in, g_w_o = _chip_sum_swap(parts, lands)
    g_w_in = g_w_in[:, :shard_cols]

    cols_first = lambda a: jnp.transpose(a, (2, 0, 1))
    rows_first = lambda a: jnp.transpose(a, (1, 2, 0))
    g_w_in, d_w_in, nm_w_in, nv_w_in = [rows_first(a) for a in _adamw_cols(
        cols_first(w_in), cols_first(g_w_in[None]), cols_first(m_w_in), cols_first(v_w_in), name="adamw_w_in")]
    d_w_o, nm_w_o, nv_w_o = _adamw(w_o2, g_w_o, m_w_o[0], v_w_o[0], name="adamw_w_o")

    loss_row, gs, ds, ms, vs = _small_allreduce_adamw(
        [gbf8, grb, gsk8, gg8, gb8, loss8],
        [b_f, rel_bias, sink, ln_g, ln_b],
        [m_b_f, m_rel_bias, m_sink, m_ln_g, m_ln_b],
        [v_b_f, v_rel_bias, v_sink, v_ln_g, v_ln_b])
    loss = loss_row[0, 0]
    g_bf, g_rb, g_sk, g_lg, g_lb = gs
    d_bf, d_rb, d_sk, d_lg, d_lb = ds
    m_bf, m_rb, m_sk, m_lg, m_lb = ms
    v_bf, v_rb, v_sk, v_lg, v_lb = vs

    e = lambda a: a[None]
    return (loss, e(grad_x),
            g_w_in, g_bf, g_rb, g_sk, e(g_w_o), g_lg, g_lb,
            d_w_in, d_bf, d_rb, d_sk, e(d_w_o), d_lg, d_lb,
            nm_w_in, m_bf, m_rb, m_sk, e(nm_w_o), m_lg, m_lb,
            nv_w_in, v_bf, v_rb, v_sk, e(nv_w_o), v_lg, v_lb)
```

```python
import functools
import math

import numpy as np
import jax
import jax.numpy as jnp
from jax import lax
from jax.experimental import pallas as pl
from jax.experimental.pallas import tpu as pltpu

F32 = jnp.float32
BF16 = jnp.bfloat16

D_MODEL = 1024
HEAD_DIM = 64
FOX_HEADS = 8
SWA_HEADS = 8
SWA_KV_HEADS = 2
SWA_GROUP = 4
FOX_W = 512
SWA_W = 512
BLOCK = 128
NUM_BUCKETS = 32
MAX_DISTANCE = 128
LN_EPS = 1e-5
NEG = -1e30
ALPHA = 2.0 ** 0.25
QK_SCALE = 0.125

ADAM_LR = 0.001
ADAM_B1 = 0.9
ADAM_B2 = 0.999
ADAM_EPS = 1e-08
ADAM_WD = 0.01
ADAM_STEP = 10

D_IN = 3336
SHARD_PAD = 896
N_A = 2304
N_C = 256
N_B = 1024
OFF_C = N_A
OFF_B = N_A + N_C
N_PAD = N_A + N_C + N_B
COL_FK, COL_FV, COL_SQ, COL_SK, COL_SV = 512, 1024, 1536, 2048, 2176

LANES = 128
FOX_T = 256
FOX_REF = 512
SUM_ROWS = 16
VMEM_LIMIT = 56 * 1024 * 1024

MESH = pl.DeviceIdType.MESH
N_CHIPS = 4
N_DEV = 8
SMALL_ROWS = 56


def _cparams(sem=None):
    return pltpu.CompilerParams(dimension_semantics=sem, vmem_limit_bytes=VMEM_LIMIT)


def _split3(x):
    hi = x.astype(BF16)
    r = x - hi.astype(F32)
    mid = r.astype(BF16)
    lo = (r - mid.astype(F32)).astype(BF16)
    return hi, mid, lo


def _dot(a, b):
    return jnp.dot(a, b, preferred_element_type=F32)


def _dot_nt(a, b):
    return lax.dot_general(a, b, (((1,), (1,)), ((), ())), preferred_element_type=F32)


def _project(x, w_pad, token):
    s, k = x.shape
    tm = 512
    chunk = 512

    def kern(x_ref, w_ref, _, qkv_ref, ff_ref, z_ref, xt_ref, vt_ref):
        xf = x_ref[...]
        xb = xf.astype(BF16)
        xt_ref[...] = xf.T.astype(BF16)
        for c0 in range(0, N_A, chunk):
            width = min(chunk, N_A - c0)
            res = _dot(xb, w_ref[:, c0:c0 + width])
            qkv_ref[:, c0:c0 + width] = res.astype(BF16)
            if c0 == COL_FV:
                vt_ref[...] = res.T.astype(BF16)
        ff_ref[...] = _dot(xb, w_ref[:, OFF_C:OFF_C + N_C])
        for c0 in range(0, N_B, 512):
            z_ref[:, c0:c0 + 512] = _dot(xb, w_ref[:, OFF_B + c0:OFF_B + c0 + 512])

    row = lambda i: (i, 0)
    return pl.pallas_call(
        kern, name="project",
        grid=(s // tm,),
        in_specs=[pl.BlockSpec((tm, k), row),
                  _resident((k, N_PAD), lambda i: (0, 0)),
                  _resident(token.shape, lambda i: (0, 0))],
        out_specs=[pl.BlockSpec((tm, N_A), row),
                   pl.BlockSpec((tm, N_C), row),
                   pl.BlockSpec((tm, N_B), row),
                   pl.BlockSpec((k, tm), lambda i: (0, i)),
                   pl.BlockSpec((FOX_W, tm), lambda i: (0, i))],
        out_shape=[jax.ShapeDtypeStruct((s, N_A), BF16),
                   jax.ShapeDtypeStruct((s, N_C), F32),
                   jax.ShapeDtypeStruct((s, N_B), F32),
                   jax.ShapeDtypeStruct((k, s), BF16),
                   jax.ShapeDtypeStruct((FOX_W, s), BF16)],
        compiler_params=_cparams(("parallel",)),
    )(x, w_pad, token)


def _grad_x_matmul(pieces, w_pad, dh, token, *, tm, tn, name):
    m = dh.shape[0]
    n, k = w_pad.shape
    widths = [p.shape[1] for p in pieces]
    offs = [sum(widths[:i]) for i in range(len(pieces))]
    assert sum(widths) == k

    def kern(*refs):
        p_refs, (b_ref, dh_ref, _, o_ref) = refs[:len(pieces)], refs[len(pieces):]
        acc = ALPHA * dh_ref[...]
        for p_ref, off, width in zip(p_refs, offs, widths):
            acc = acc + _dot_nt(p_ref[...], b_ref[:, off:off + width])
        o_ref[...] = acc

    assert tn == n
    return pl.pallas_call(
        kern, name=name,
        grid=(m // tm,),
        in_specs=[pl.BlockSpec((tm, w), lambda i: (i, 0)) for w in widths]
        + [_resident((n, k), lambda i: (0, 0)),
           pl.BlockSpec((tm, n), lambda i: (i, 0)),
           _resident(token.shape, lambda i: (0, 0))],
        out_specs=pl.BlockSpec((tm, n), lambda i: (i, 0)),
        out_shape=jax.ShapeDtypeStruct((m, n), F32),
        compiler_params=_cparams(("parallel",)),
    )(*pieces, w_pad, dh, token)


def _grad_w_matmul(xt, blocks, *, tk, name):
    m, s = xt.shape
    tn = 512
    nb = len(blocks)

    def kern(a_ref, *refs):
        b_refs, o_ref = refs[:nb], refs[nb]

        @pl.when(pl.program_id(0) == 0)
        def _():
            o_ref[...] = jnp.zeros_like(o_ref)
        a = a_ref[...]
        for blk in range(nb):
            o_ref[:, blk * tn:(blk + 1) * tn] += _dot(a, b_refs[blk][...])

    return pl.pallas_call(
        kern, name=name,
        grid=(s // tk,),
        in_specs=[pl.BlockSpec((m, tk), lambda k: (0, k))]
        + [pl.BlockSpec((tk, tn), functools.partial(lambda k, col: (k, col), col=col)) for _, col in blocks],
        out_specs=_resident((m, nb * tn), lambda k: (0, 0)),
        out_shape=jax.ShapeDtypeStruct((m, nb * tn), F32),
        compiler_params=_cparams(("arbitrary",)),
    )(xt, *[arr for arr, _ in blocks])


def _tri(n, lower):
    r = lax.broadcasted_iota(jnp.int32, (n, n), 0)
    c = lax.broadcasted_iota(jnp.int32, (n, n), 1)
    keep = (c <= r) if lower else (c >= r)
    return jnp.where(keep, 1.0, 0.0).astype(BF16)


def _exact_dot(mat_bf16, x_f32, left):
    out = None
    for piece in _split3(x_f32):
        t = _dot(mat_bf16, piece) if left else _dot(piece, mat_bf16)
        out = t if out is None else out + t
    return out


def _log_sigmoid(z):
    return jnp.minimum(z, 0.0) - jnp.log(1.0 + jnp.exp(-jnp.abs(z)))


def _cum_fwd(ffp, bfp):
    s = ffp.shape[0]
    t = min(1024, s)

    def kern(ff_ref, b_ref, cum_ref, carry_ref):
        @pl.when(pl.program_id(0) == 0)
        def _():
            carry_ref[...] = jnp.zeros_like(carry_ref)
        lane = lax.broadcasted_iota(jnp.int32, (1, LANES), 1)
        lf = _log_sigmoid(ff_ref[...] + b_ref[...])
        lf = jnp.where(lane < FOX_HEADS, lf, 0.0)
        cum = _exact_dot(_tri(t, True), lf, True) + carry_ref[0:1, :]
        cum_ref[...] = cum
        carry_ref[...] = jnp.broadcast_to(cum[t - 1:t, :], carry_ref.shape)

    return pl.pallas_call(
        kern, name="cum_fwd",
        grid=(s // t,),
        in_specs=[pl.BlockSpec((t, LANES), lambda i: (i, 0)),
                  pl.BlockSpec((1, LANES), lambda i: (0, 0))],
        out_specs=pl.BlockSpec((t, LANES), lambda i: (i, 0)),
        out_shape=jax.ShapeDtypeStruct((s, LANES), F32),
        scratch_shapes=[pltpu.VMEM((8, LANES), F32)],
        compiler_params=_cparams(("arbitrary",)),
    )(ffp, bfp)


def _cum_bwd(dcum_k, dcum_q, ffp, bfp):
    s = dcum_k.shape[0]
    t = min(1024, s)
    nb = s // t

    def kern(dck_ref, dcq_ref, ff_ref, b_ref, dff_ref, gb_ref, carry_ref):
        @pl.when(pl.program_id(0) == 0)
        def _():
            carry_ref[...] = jnp.zeros_like(carry_ref)
            gb_ref[...] = jnp.zeros_like(gb_ref)
        lane = lax.broadcasted_iota(jnp.int32, (1, LANES), 1)
        dlf = _exact_dot(_tri(t, False), dck_ref[...] + dcq_ref[...], True) + carry_ref[0:1, :]
        carry_ref[...] = jnp.broadcast_to(dlf[0:1, :], carry_ref.shape)
        z = ff_ref[...] + b_ref[...]
        dff = jnp.where(lane < FOX_HEADS, dlf / (1.0 + jnp.exp(z)), 0.0)
        gb_ref[...] += jnp.broadcast_to(jnp.sum(dff, axis=0, keepdims=True), gb_ref.shape)
        dff_ref[...] = jnp.concatenate([dff, jnp.zeros_like(dff)], axis=1).astype(BF16)

    return pl.pallas_call(
        kern, name="cum_bwd",
        grid=(nb,),
        in_specs=[pl.BlockSpec((t, LANES), lambda i: (nb - 1 - i, 0)),
                  pl.BlockSpec((t, LANES), lambda i: (nb - 1 - i, 0)),
                  pl.BlockSpec((t, LANES), lambda i: (nb - 1 - i, 0)),
                  pl.BlockSpec((1, LANES), lambda i: (0, 0))],
        out_specs=[pl.BlockSpec((t, N_C), lambda i: (nb - 1 - i, 0)),
                   pl.BlockSpec((8, LANES), lambda i: (0, 0))],
        out_shape=[jax.ShapeDtypeStruct((s, N_C), BF16),
                   jax.ShapeDtypeStruct((8, LANES), F32)],
        scratch_shapes=[pltpu.VMEM((8, LANES), F32)],
        compiler_params=_cparams(("arbitrary",)),
    )(dcum_k, dcum_q, ffp, bfp)


def _resident(shape, index_map):
    return pl.BlockSpec(shape, index_map, pipeline_mode=pl.Buffered(1))


def _fox_fwd(qkv, vt, cum_t3, cum):
    s = qkv.shape[0]
    tk = tq = FOX_REF
    nq = s // tq
    nh = FOX_HEADS
    diag_tiles = tq // tk

    def kern(q_ref, k_ref, vt_ref, ct_ref, c_ref, o_ref, lse_ref, m_ref, acc_ref, u_ref):
        i = pl.program_id(0)
        lane = lax.broadcasted_iota(jnp.int32, (1, LANES), 1)
        krow = lax.broadcasted_iota(jnp.int32, (tk, tq), 0)
        qcol = lax.broadcasted_iota(jnp.int32, (tk, tq), 1)
        q0 = pl.multiple_of(i * tq, tq)
        qts, crefs = [], []
        for h in range(nh):
            p, a = divmod(h, 2)
            q2 = q_ref[:, p * LANES:(p + 1) * LANES] * jnp.asarray(QK_SCALE, BF16)
            sel = (lane < HEAD_DIM) if a == 0 else (lane >= HEAD_DIM)
            qts.append(jnp.where(sel, q2, jnp.zeros_like(q2)).astype(F32).T.astype(BF16))
            crefs.append(ct_ref[h, :, pl.ds(q0, LANES)][:, 0:1])
        m_ref[...] = jnp.full(m_ref.shape, NEG, F32)
        acc_ref[...] = jnp.zeros_like(acc_ref)
        ones = jnp.ones((SUM_ROWS, tk), BF16)

        def tile(j, diag):
            k0 = pl.multiple_of(j * tk, tk)
            cb = c_ref[pl.ds(k0, tk), :]
            sts = [_dot(k_ref[pl.ds(k0, tk), (h // 2) * LANES:(h // 2 + 1) * LANES], qts[h]) for h in range(nh)]
            tile_max = []
            for h in range(nh):
                u = sts[h] - (cb[:, h:h + 1] - crefs[h])
                if diag is not None:
                    u = jnp.where(krow + diag * tk <= qcol, u, NEG)
                u_ref[h] = u
                tile_max.append(jnp.max(u, axis=0, keepdims=True))
            pts, scales = [], []
            for h in range(nh):
                m_old = m_ref[h]
                m_new = jnp.maximum(m_old, tile_max[h])
                scales.append(jnp.exp(m_old - m_new))
                pts.append(jnp.exp(u_ref[h] - m_new).astype(BF16))
                m_ref[h] = m_new
            for h in range(nh):
                vth = jnp.concatenate([vt_ref[h * HEAD_DIM:(h + 1) * HEAD_DIM, pl.ds(k0, tk)], ones], axis=0)
                acc_ref[h] = scales[h] * acc_ref[h] + _dot(vth, pts[h])

        def body(j, c):
            tile(j, None)
            return c
        lax.fori_loop(0, i * diag_tiles, body, 0)
        for d in range(diag_tiles):
            tile(i * diag_tiles + d, d)

        ls = [acc_ref[h][HEAD_DIM:HEAD_DIM + 1] for h in range(nh)]
        for p in range(nh // 2):
            ot = jnp.concatenate([acc_ref[2 * p + a][:HEAD_DIM] * (1.0 / ls[2 * p + a]) for a in range(2)], axis=0)
            o_ref[:, p * LANES:(p + 1) * LANES] = ot.T
        for h in range(nh):
            lse_ref[h, :, pl.ds(q0, tq)] = m_ref[h] + jnp.log(ls[h])

    return pl.pallas_call(
        kern, name="fox_fwd",
        grid=(nq,),
        in_specs=[pl.BlockSpec((tq, FOX_W), lambda i: (i, 0)),
                  _resident((s, FOX_W), lambda i: (0, COL_FK // FOX_W)),
                  _resident((FOX_W, s), lambda i: (0, 0)),
                  _resident((nh, 1, s), lambda i: (0, 0, 0)),
                  _resident((s, LANES), lambda i: (0, 0))],
        out_specs=[pl.BlockSpec((tq, FOX_W), lambda i: (i, 0)),
                   pl.BlockSpec((nh, 1, s), lambda i: (0, 0, 0))],
        out_shape=[jax.ShapeDtypeStruct((s, FOX_W), F32),
                   jax.ShapeDtypeStruct((nh, 1, s), F32)],
        scratch_shapes=[pltpu.VMEM((nh, 1, tq), F32),
                        pltpu.VMEM((nh, HEAD_DIM + SUM_ROWS, tq), F32),
                        pltpu.VMEM((nh, tk, tq), F32)],
        compiler_params=_cparams(("arbitrary",)),
    )(qkv, qkv, vt, cum_t3, cum)


def _fox_bwd(qkv, do_bf, cum_t3, cum, lse_t3, delta_t3):
    s = qkv.shape[0]
    t = min(FOX_T, s)
    nq = s // t
    nh = FOX_HEADS
    npair = nh // 2

    def kern(q_ref, do_ref, k_ref, v_ref, ct_ref, c_ref, lse_ref, dl_ref,
             dq_ref, dk_ref, dv_ref, dc_ref, dcq_ref, dqt_ref, accv_ref, acck_ref, accd_ref):
        kj = pl.program_id(0)
        lane = lax.broadcasted_iota(jnp.int32, (1, LANES), 1)
        krow = lax.broadcasted_iota(jnp.int32, (t, t), 0)
        qcol = lax.broadcasted_iota(jnp.int32, (t, t), 1)
        causal = krow <= qcol
        sels = [lane < HEAD_DIM, lane >= HEAD_DIM]

        @pl.when(kj == 0)
        def _():
            dqt_ref[...] = jnp.zeros_like(dqt_ref)
            dcq_ref[...] = jnp.zeros_like(dcq_ref)

        cb = c_ref[...]
        k2s, v2s, kts = [], [], []
        for p in range(npair):
            k2 = k_ref[:, p * LANES:(p + 1) * LANES]
            k2s.append(k2)
            v2s.append(v_ref[:, p * LANES:(p + 1) * LANES])
            kt = k2.astype(F32).T * QK_SCALE
            kts.append(kt[:HEAD_DIM].astype(BF16))
            kts.append(kt[HEAD_DIM:].astype(BF16))
        css = [cb[:, h:h + 1] for h in range(nh)]

        def tile(i, masked):
            q0 = pl.multiple_of(i * t, t)
            r0 = pl.multiple_of((i // (FOX_REF // t)) * FOX_REF, FOX_REF)
            sts, dpts, qms, doms = [], [], [], []
            for h in range(nh):
                p, a = divmod(h, 2)
                qi = q_ref[pl.ds(q0, t), p * LANES:(p + 1) * LANES] * jnp.asarray(QK_SCALE, BF16)
                doi = do_ref[pl.ds(q0, t), p * LANES:(p + 1) * LANES]
                qm = jnp.where(sels[a], qi, jnp.zeros_like(qi))
                dom = jnp.where(sels[a], doi, jnp.zeros_like(doi))
                qms.append(qm)
                doms.append(dom)
                sts.append(_dot_nt(k2s[p], qm))
                dpts.append(_dot_nt(v2s[p], dom))
            pts, dsts = [], []
            for h in range(nh):
                cref = ct_ref[h, :, pl.ds(r0, LANES)][:, 0:1]
                pt = jnp.exp(sts[h] - (css[h] - cref) - lse_ref[h, :, pl.ds(q0, t)])
                if masked:
                    pt = jnp.where(causal, pt, 0.0)
                ds32 = pt * (dpts[h] - dl_ref[h, :, pl.ds(q0, t)])
                part = ds32[:, 0:LANES]
                for c in range(1, t // LANES):
                    part = part + ds32[:, c * LANES:(c + 1) * LANES]
                accd_ref[h] = part if masked else accd_ref[h] + part
                dcq_ref[h, :, pl.ds(q0, t)] += jnp.sum(ds32, axis=0, keepdims=True)
                pts.append(pt.astype(BF16))
                dsts.append(ds32.astype(BF16))
            for p in range(npair):
                ha, hb = 2 * p, 2 * p + 1
                dv_p = _dot(pts[ha], doms[ha]) + _dot(pts[hb], doms[hb])
                dk_p = _dot(dsts[ha], qms[ha]) + _dot(dsts[hb], qms[hb])
                accv_ref[p] = dv_p if masked else accv_ref[p] + dv_p
                acck_ref[p] = dk_p if masked else acck_ref[p] + dk_p
            for h in range(nh):
                dqt_ref[h * HEAD_DIM:(h + 1) * HEAD_DIM, pl.ds(q0, t)] += _dot(kts[h], dsts[h])

        tile(kj, True)

        def body(i, c):
            tile(i, False)
            return c
        lax.fori_loop(kj + 1, nq, body, 0)

        dc = jnp.zeros((t, LANES), F32)
        for h in range(nh):
            dc = jnp.where(lane == h, -jnp.sum(accd_ref[h], axis=1, keepdims=True), dc)
        dc_ref[...] = dc
        for p in range(npair):
            dv_ref[:, p * LANES:(p + 1) * LANES] = accv_ref[p].astype(BF16)
            dk_ref[:, p * LANES:(p + 1) * LANES] = acck_ref[p].astype(BF16)

        @pl.when(kj == nq - 1)
        def _():
            for c0 in range(0, s, t):
                dq_ref[c0:c0 + t, :] = dqt_ref[:, c0:c0 + t].T.astype(BF16)

    whole = lambda kj: (0, 0, 0)
    return pl.pallas_call(
        kern, name="fox_bwd",
        grid=(nq,),
        in_specs=[_resident((s, FOX_W), lambda kj: (0, 0)),
                  _resident((s, FOX_W), lambda kj: (0, 0)),
                  pl.BlockSpec((t, FOX_W), lambda kj: (kj, COL_FK // FOX_W)),
                  pl.BlockSpec((t, FOX_W), lambda kj: (kj, COL_FV // FOX_W)),
                  _resident((nh, 1, s), whole),
                  pl.BlockSpec((t, LANES), lambda kj: (kj, 0)),
                  _resident((nh, 1, s), whole),
                  _resident((nh, 1, s), whole)],
        out_specs=[_resident((s, FOX_W), lambda kj: (0, 0)),
                   pl.BlockSpec((t, FOX_W), lambda kj: (kj, 0)),
                   pl.BlockSpec((t, FOX_W), lambda kj: (kj, 0)),
                   pl.BlockSpec((t, LANES), lambda kj: (kj, 0)),
                   _resident((nh, 1, s), whole)],
        out_shape=[jax.ShapeDtypeStruct((s, FOX_W), BF16),
                   jax.ShapeDtypeStruct((s, FOX_W), BF16),
                   jax.ShapeDtypeStruct((s, FOX_W), BF16),
                   jax.ShapeDtypeStruct((s, LANES), F32),
                   jax.ShapeDtypeStruct((nh, 1, s), F32)],
        scratch_shapes=[pltpu.VMEM((FOX_W, s), F32),
                        pltpu.VMEM((npair, t, LANES), F32),
                        pltpu.VMEM((npair, t, LANES), F32),
                        pltpu.VMEM((nh, t, LANES), F32)],
        compiler_params=_cparams(("arbitrary",)),
    )(qkv, do_bf, qkv, qkv, cum_t3, cum, lse_t3, delta_t3)


def _bucket_table():
    qi = np.arange(BLOCK)[:, None]
    kj = np.arange(2 * BLOCK)[None, :]
    rel = np.maximum(qi + BLOCK - kj, 0).astype(np.int32)
    max_exact = NUM_BUCKETS // 2
    relf = np.maximum(rel, 1).astype(np.float32)
    large = max_exact + (np.log(relf / np.float32(max_exact)) / np.float32(math.log(MAX_DISTANCE / max_exact))
                         * np.float32(NUM_BUCKETS - max_exact)).astype(np.int32)
    large = np.minimum(large, NUM_BUCKETS - 1)
    return np.where(rel < max_exact, rel, large).astype(np.int32)


SWA_LANES = SWA_GROUP * BLOCK


def _swa_bias(rel_bias, bucket_t):
    def kern(rb_ref, bk_ref, o_ref):
        bk = bk_ref[...]
        kj = lax.broadcasted_iota(jnp.int32, (2 * BLOCK, BLOCK), 0)
        qi = lax.broadcasted_iota(jnp.int32, (2 * BLOCK, BLOCK), 1)
        rel = qi + BLOCK - kj
        band = (rel >= 0) & (rel < BLOCK)
        masks = [band & (kj >= BLOCK), band]
        for h in range(SWA_HEADS):
            g, hh = divmod(h, SWA_GROUP)
            acc = jnp.zeros((2 * BLOCK, BLOCK), F32)
            for b in range(NUM_BUCKETS):
                acc = jnp.where(bk == b, rb_ref[b, h], acc)
            for first in range(2):
                o_ref[first, g, :, hh * BLOCK:(hh + 1) * BLOCK] = jnp.where(masks[first], acc, NEG)

    return pl.pallas_call(
        kern, name="swa_bias",
        in_specs=[pl.BlockSpec(memory_space=pltpu.SMEM),
                  pl.BlockSpec(memory_space=pltpu.VMEM)],
        out_specs=pl.BlockSpec(memory_space=pltpu.VMEM),
        out_shape=jax.ShapeDtypeStruct((2, SWA_KV_HEADS, 2 * BLOCK, SWA_LANES), F32),
        compiler_params=_cparams(),
    )(rel_bias, bucket_t)


SWA_STEP = 8


def _swa_keys(prev_ref, cur_ref):
    return jnp.concatenate([prev_ref[...], cur_ref[...]], axis=0)


def _swa_queries(x_ref, scale):
    x = x_ref[...]
    if scale:
        x = x * jnp.asarray(QK_SCALE, BF16)
    xt = x.astype(F32).T.astype(BF16)
    return [_group_rows(xt[:, b * BLOCK:(b + 1) * BLOCK]) for b in range(SWA_STEP)]


def _group_rows(xt):
    zeros = jnp.zeros((HEAD_DIM, SWA_LANES), BF16)
    out = []
    for g in range(SWA_KV_HEADS):
        heads = [xt[(SWA_GROUP * g + hh) * HEAD_DIM:(SWA_GROUP * g + hh + 1) * HEAD_DIM, :] for hh in range(SWA_GROUP)]
        rows = jnp.concatenate(heads, axis=1)
        padded = jnp.concatenate([rows, zeros] if g == 0 else [zeros, rows], axis=0)
        out.append((rows, padded))
    return out


def _pairs_to_rows(cols_t):
    out = []
    for p in range(SWA_HEADS // 2):
        g, hh = divmod(2 * p, SWA_GROUP)
        pair = jnp.concatenate([cols_t[g][:, hh * BLOCK:(hh + 1) * BLOCK],
                                cols_t[g][:, (hh + 1) * BLOCK:(hh + 2) * BLOCK]], axis=0)
        out.append(pair.T)
    return jnp.concatenate(out, axis=1)


def _swa_fwd(qkv, bias_t, sink_rows):
    s = qkv.shape[0]
    nb = s // BLOCK
    rows = SWA_STEP * BLOCK
    units = [(b, g) for b in range(SWA_STEP) for g in range(SWA_KV_HEADS)]

    def kern(q_ref, kp_ref, kc_ref, vp_ref, vc_ref, bias_ref, sink_ref, o_ref, lse_ref):
        n = pl.program_id(0)
        tables = [jnp.minimum(n, 1)] + [1] * (SWA_STEP - 1)
        k3 = _swa_keys(kp_ref, kc_ref)
        vt3 = _swa_keys(vp_ref, vc_ref).astype(F32).T.astype(BF16)
        qts = _swa_queries(q_ref, True)
        us = [_dot(k3[b * BLOCK:(b + 2) * BLOCK], qts[b][g][1]) + bias_ref[tables[b], g] for b, g in units]
        outs = []
        for (b, g), u in zip(units, us):
            sk = sink_ref[g]
            m = jnp.maximum(jnp.max(u, axis=0, keepdims=True), sk)
            p = jnp.exp(u - m)
            l = jnp.sum(p, axis=0, keepdims=True) + jnp.exp(sk - m)
            lse_ref[b, g] = m + jnp.log(l)
            vt = vt3[g * HEAD_DIM:(g + 1) * HEAD_DIM, b * BLOCK:(b + 2) * BLOCK]
            outs.append(_dot(vt, (p * (1.0 / l)).astype(BF16)))
        for b in range(SWA_STEP):
            o_ref[b * BLOCK:(b + 1) * BLOCK, :] = _pairs_to_rows(outs[b * SWA_KV_HEADS:(b + 1) * SWA_KV_HEADS])

    cq, ck, cv = COL_SQ // SWA_W, COL_SK // LANES, COL_SV // LANES
    prev = lambda n: jnp.maximum(SWA_STEP * n - 1, 0)
    return pl.pallas_call(
        kern, name="swa_fwd",
        grid=(nb // SWA_STEP,),
        in_specs=[pl.BlockSpec((rows, SWA_W), lambda n: (n, cq)),
                  pl.BlockSpec((BLOCK, LANES), lambda n: (prev(n), ck)),
                  pl.BlockSpec((rows, LANES), lambda n: (n, ck)),
                  pl.BlockSpec((BLOCK, LANES), lambda n: (prev(n), cv)),
                  pl.BlockSpec((rows, LANES), lambda n: (n, cv)),
                  _resident((2, SWA_KV_HEADS, 2 * BLOCK, SWA_LANES), lambda n: (0, 0, 0, 0)),
                  _resident((SWA_KV_HEADS, 1, SWA_LANES), lambda n: (0, 0, 0))],
        out_specs=[pl.BlockSpec((rows, SWA_W), lambda n: (n, 0)),
                   pl.BlockSpec((SWA_STEP, SWA_KV_HEADS, 1, SWA_LANES), lambda n: (n, 0, 0, 0))],
        out_shape=[jax.ShapeDtypeStruct((s, SWA_W), F32),
                   jax.ShapeDtypeStruct((nb, SWA_KV_HEADS, 1, SWA_LANES), F32)],
        compiler_params=_cparams(("parallel",)),
    )(qkv, qkv, qkv, qkv, qkv, bias_t, sink_rows)


def _swa_bwd(qkv, do_bf, delta_rows, lse, bias_t, sink_rows, bucket_t):
    s = qkv.shape[0]
    nb = s // BLOCK
    steps = nb // SWA_STEP
    rows = SWA_STEP * BLOCK
    units = [(b, g) for b in range(SWA_STEP) for g in range(SWA_KV_HEADS)]

    def kern(q_ref, kp_ref, kc_ref, vp_ref, vc_ref, do_ref, dl_ref, lse_ref, bias_ref, sink_ref, bk_ref,
             dq_ref, dk_ref, dv_ref, grb_ref, gsk_ref, dbias_ref, ck_ref, cv_ref, sk_ref):
        n = pl.program_id(0)

        @pl.when(n == 0)
        def _():
            dbias_ref[...] = jnp.zeros_like(dbias_ref)
            ck_ref[...] = jnp.zeros_like(ck_ref)
            cv_ref[...] = jnp.zeros_like(cv_ref)
            sk_ref[...] = jnp.zeros_like(sk_ref)

        @pl.when(n < steps)
        def _():
            tables = [jnp.minimum(n, 1)] + [1] * (SWA_STEP - 1)
            k3 = _swa_keys(kp_ref, kc_ref)
            v3 = _swa_keys(vp_ref, vc_ref)
            kt3 = (k3.astype(F32).T * QK_SCALE).astype(BF16)
            qts = _swa_queries(q_ref, True)
            dots = _swa_queries(do_ref, False)
            sts = [_dot(k3[b * BLOCK:(b + 2) * BLOCK], qts[b][g][1]) for b, g in units]
            dps = [_dot(v3[b * BLOCK:(b + 2) * BLOCK], dots[b][g][1]) for b, g in units]
            ps, dss = [], []
            for i, (b, g) in enumerate(units):
                lse_g = lse_ref[b, g]
                dlt = dl_ref[b, g]
                p = jnp.exp(sts[i] + bias_ref[tables[b], g] - lse_g)
                ds = p * (dps[i] - dlt)
                dbias_ref[g] += ds
                sk_ref[g] += -jnp.exp(sink_ref[g] - lse_g) * dlt
                ps.append(p.astype(BF16))
                dss.append(ds.astype(BF16))
            dk2, dv2 = [], []
            for b in range(SWA_STEP):
                at = lambda g: b * SWA_KV_HEADS + g
                groups = range(SWA_KV_HEADS)
                dv2.append(jnp.concatenate([_dot_nt(dots[b][g][0], ps[at(g)]) for g in groups], axis=0).T)
                dk2.append(jnp.concatenate([_dot_nt(qts[b][g][0], dss[at(g)]) for g in groups], axis=0).T)
                dqts = [_dot(kt3[g * HEAD_DIM:(g + 1) * HEAD_DIM, b * BLOCK:(b + 2) * BLOCK], dss[at(g)]) for g in groups]
                dq_ref[b * BLOCK:(b + 1) * BLOCK, :] = _pairs_to_rows(dqts).astype(BF16)
            last = (SWA_STEP - 1) * BLOCK
            for acc_ref, out_ref, parts in ((ck_ref, dk_ref, dk2), (cv_ref, dv_ref, dv2)):
                done = acc_ref[last:] + parts[0][:BLOCK]
                out_ref[...] = jnp.concatenate([acc_ref[:last], done], axis=0).astype(BF16)
                for b in range(SWA_STEP - 1):
                    acc_ref[b * BLOCK:(b + 1) * BLOCK] = parts[b][BLOCK:] + parts[b + 1][:BLOCK]
                acc_ref[last:] = parts[SWA_STEP - 1][BLOCK:]

        @pl.when(n == steps)
        def _():
            dk_ref[...] = ck_ref[...].astype(BF16)
            dv_ref[...] = cv_ref[...].astype(BF16)
            bk = bk_ref[...]
            lane = lax.broadcasted_iota(jnp.int32, (8, LANES), 1)
            rowi = lax.broadcasted_iota(jnp.int32, (NUM_BUCKETS, LANES), 0)
            lanei = lax.broadcasted_iota(jnp.int32, (NUM_BUCKETS, LANES), 1)
            out = jnp.zeros((NUM_BUCKETS, LANES), F32)
            gsk = jnp.zeros((8, LANES), F32)
            for h in range(SWA_HEADS):
                g, hh = divmod(h, SWA_GROUP)
                cols = slice(hh * BLOCK, (hh + 1) * BLOCK)
                gsk = jnp.where(lane == h, jnp.sum(sk_ref[g][:, cols]), gsk)
                db = dbias_ref[g][:, cols]
                for b in range(NUM_BUCKETS):
                    val = jnp.sum(jnp.where(bk == b, db, 0.0))
                    out = jnp.where((rowi == b) & (lanei == h), val, out)
            grb_ref[...] = out
            gsk_ref[...] = gsk

    cq, ck, cv = COL_SQ // SWA_W, COL_SK // LANES, COL_SV // LANES
    cur = lambda n: jnp.minimum(n, steps - 1)
    prev = lambda n: jnp.maximum(SWA_STEP * cur(n) - 1, 0)
    kout = lambda n: jnp.maximum(n - 1, 0)
    stat = pl.BlockSpec((SWA_STEP, SWA_KV_HEADS, 1, SWA_LANES), lambda n: (cur(n), 0, 0, 0))
    return pl.pallas_call(
        kern, name="swa_bwd",
        grid=(steps + 1,),
        in_specs=[pl.BlockSpec((rows, SWA_W), lambda n: (cur(n), cq)),
                  pl.BlockSpec((BLOCK, LANES), lambda n: (prev(n), ck)),
                  pl.BlockSpec((rows, LANES), lambda n: (cur(n), ck)),
                  pl.BlockSpec((BLOCK, LANES), lambda n: (prev(n), cv)),
                  pl.BlockSpec((rows, LANES), lambda n: (cur(n), cv)),
                  pl.BlockSpec((rows, SWA_W), lambda n: (cur(n), 1)),
                  stat, stat,
                  _resident((2, SWA_KV_HEADS, 2 * BLOCK, SWA_LANES), lambda n: (0, 0, 0, 0)),
                  _resident((SWA_KV_HEADS, 1, SWA_LANES), lambda n: (0, 0, 0)),
                  _resident((2 * BLOCK, BLOCK), lambda n: (0, 0))],
        out_specs=[pl.BlockSpec((rows, SWA_W), lambda n: (cur(n), 0)),
                   pl.BlockSpec((rows, LANES), lambda n: (kout(n), 0)),
                   pl.BlockSpec((rows, LANES), lambda n: (kout(n), 0)),
                   pl.BlockSpec((NUM_BUCKETS, LANES), lambda n: (0, 0)),
                   pl.BlockSpec((8, LANES), lambda n: (0, 0))],
        out_shape=[jax.ShapeDtypeStruct((s, SWA_W), BF16),
                   jax.ShapeDtypeStruct((s, LANES), BF16),
                   jax.ShapeDtypeStruct((s, LANES), BF16),
                   jax.ShapeDtypeStruct((NUM_BUCKETS, LANES), F32),
                   jax.ShapeDtypeStruct((8, LANES), F32)],
        scratch_shapes=[pltpu.VMEM((SWA_KV_HEADS, 2 * BLOCK, SWA_LANES), F32),
                        pltpu.VMEM((rows, LANES), F32),
                        pltpu.VMEM((rows, LANES), F32),
                        pltpu.VMEM((SWA_KV_HEADS, 1, SWA_LANES), F32)],
        compiler_params=_cparams(("arbitrary",)),
    )(qkv, qkv, qkv, qkv, qkv, do_bf, delta_rows, lse, bias_t, sink_rows, bucket_t)


def _post(x, target, o_fox, o_swa, z, w_o, ln_g, ln_b):
    s = x.shape[0]
    tm = min(256, s)
    nt = s // tm

    def kern(x_ref, t_ref, of_ref, os_ref, z_ref, w_ref, g_ref, b_ref,
             loss_ref, dh_ref, gwo_ref, do_ref, dz_ref, dl_ref, gg_ref, gb_ref, lacc_ref):
        step = pl.program_id(0)

        @pl.when(step == 0)
        def _():
            lacc_ref[...] = jnp.zeros_like(lacc_ref)
            gg_ref[...] = jnp.zeros_like(gg_ref)
            gwo_ref[...] = jnp.zeros_like(gwo_ref)
            gb_ref[...] = jnp.zeros_like(gb_ref)

        o = jnp.concatenate([of_ref[...], os_ref[...]], axis=1)
        zz = z_ref[...]
        sig = 1.0 / (1.0 + jnp.exp(-zz))
        silu = zz * sig
        mixed32 = o * silu
        mixed = mixed32.astype(BF16)
        w = w_ref[...]
        h = ALPHA * x_ref[...] + _dot(mixed, w)
        mu = jnp.mean(h, axis=1, keepdims=True)
        hc = h - mu
        var = jnp.mean(hc * hc, axis=1, keepdims=True)
        rstd = lax.rsqrt(var + LN_EPS)
        xhat = hc * rstd
        g = g_ref[...]
        err = xhat * g + b_ref[...] - t_ref[...]
        lacc_ref[...] += jnp.broadcast_to(jnp.sum(err * err, axis=0, keepdims=True), lacc_ref.shape)
        dout = err * (1.0 / D_MODEL)
        gg_ref[...] += jnp.broadcast_to(jnp.sum(dout * xhat, axis=0, keepdims=True), gg_ref.shape)
        gb_ref[...] += jnp.broadcast_to(jnp.sum(dout, axis=0, keepdims=True), gb_ref.shape)
        dxh = dout * g
        m1 = jnp.mean(dxh, axis=1, keepdims=True)
        m2 = jnp.mean(dxh * xhat, axis=1, keepdims=True)
        dh = rstd * (dxh - m1 - xhat * m2)
        dh_ref[...] = dh
        dy = dh.astype(BF16)
        gwo_ref[...] += _dot(mixed32.T.astype(BF16), dy)
        dmix = _dot_nt(dy, w)
        do = dmix * silu
        do_ref[...] = do.astype(BF16)
        dz_ref[...] = (dmix * o * (sig * (1.0 + zz * (1.0 - sig)))).astype(BF16)
        r = lax.broadcasted_iota(jnp.int32, (D_MODEL, LANES), 0) // HEAD_DIM
        c = lax.broadcasted_iota(jnp.int32, (D_MODEL, LANES), 1)
        pick = jnp.where(r == c, 1.0, 0.0).astype(BF16)
        dl_ref[...] = _exact_dot(pick, do * o, False)

        @pl.when(step == nt - 1)
        def _():
            tot = jnp.sum(lacc_ref[0:1, :]) * (0.5 / D_MODEL)
            loss_ref[...] = jnp.broadcast_to(tot, loss_ref.shape)

    row = lambda i: (i, 0)
    fixed = lambda i: (0, 0)
    wide = pl.BlockSpec((tm, D_MODEL), row)
    half = pl.BlockSpec((tm, FOX_W), row)
    return pl.pallas_call(
        kern, name="post",
        grid=(nt,),
        in_specs=[wide, wide, half, half, wide,
                  pl.BlockSpec((D_MODEL, D_MODEL), fixed),
                  pl.BlockSpec((1, D_MODEL), fixed),
                  pl.BlockSpec((1, D_MODEL), fixed)],
        out_specs=[pl.BlockSpec((8, LANES), fixed), wide,
                   _resident((D_MODEL, D_MODEL), fixed), wide, wide,
                   pl.BlockSpec((tm, LANES), row),
                   pl.BlockSpec((8, D_MODEL), fixed), pl.BlockSpec((8, D_MODEL), fixed)],
        out_shape=[jax.ShapeDtypeStruct((8, LANES), F32),
                   jax.ShapeDtypeStruct((s, D_MODEL), F32),
                   jax.ShapeDtypeStruct((D_MODEL, D_MODEL), F32),
                   jax.ShapeDtypeStruct((s, D_MODEL), BF16),
                   jax.ShapeDtypeStruct((s, D_MODEL), BF16),
                   jax.ShapeDtypeStruct((s, LANES), F32),
                   jax.ShapeDtypeStruct((8, D_MODEL), F32),
                   jax.ShapeDtypeStruct((8, D_MODEL), F32)],
        scratch_shapes=[pltpu.VMEM((8, D_MODEL), F32)],
        compiler_params=_cparams(("arbitrary",)),
    )(x, target, o_fox, o_swa, z, w_o, ln_g, ln_b)


def _adamw_math(w, g, m, v):
    m = ADAM_B1 * m + (1.0 - ADAM_B1) * g
    v = ADAM_B2 * v + (1.0 - ADAM_B2) * (g * g)
    m_hat = m / (1.0 - ADAM_B1 ** ADAM_STEP)
    v_hat = v / (1.0 - ADAM_B2 ** ADAM_STEP)
    delta = -ADAM_LR * (m_hat / (jnp.sqrt(v_hat) + ADAM_EPS) + ADAM_WD * w)
    return delta, m, v


def _adamw(w, g, m, v, *, name):
    r, c = w.shape
    tr = min(256, r)

    def kern(w_ref, g_ref, m_ref, v_ref, d_ref, mo_ref, vo_ref):
        d, mn, vn = _adamw_math(w_ref[...], g_ref[...], m_ref[...], v_ref[...])
        d_ref[...] = d
        mo_ref[...] = mn
        vo_ref[...] = vn

    blk = pl.BlockSpec((tr, c), lambda i: (i, 0))
    sds = jax.ShapeDtypeStruct((r, c), F32)
    return pl.pallas_call(
        kern, name=name,
        grid=(r // tr,),
        in_specs=[blk, blk, blk, blk],
        out_specs=[blk, blk, blk],
        out_shape=[sds, sds, sds],
        compiler_params=_cparams(("parallel",)),
    )(w, g, m, v)


def _adamw_cols(w, g, m, v, *, name):
    c, _, r = w.shape
    tc = 139
    assert c % tc == 0

    def kern(w_ref, g_ref, m_ref, v_ref, go_ref, d_ref, mo_ref, vo_ref):
        g = g_ref[...]
        d, mn, vn = _adamw_math(w_ref[...], g, m_ref[...], v_ref[...])
        go_ref[...] = g
        d_ref[...] = d
        mo_ref[...] = mn
        vo_ref[...] = vn

    blk = pl.BlockSpec((tc, 1, r), lambda i: (i, 0, 0))
    sds = jax.ShapeDtypeStruct((c, 1, r), F32)
    return pl.pallas_call(
        kern, name=name,
        grid=(c // tc,),
        in_specs=[blk, blk, blk, blk],
        out_specs=[blk, blk, blk, blk],
        out_shape=[sds, sds, sds, sds],
        compiler_params=_cparams(("parallel",)),
    )(w, g, m, v)


def _position():
    x, y, c = lax.axis_index("x"), lax.axis_index("y"), lax.axis_index("c")
    chips = [(1 - x, y), (x, 1 - y), (1 - x, 1 - y)]
    return x, y, c, chips


def _chip_index(cx, cy):
    return 2 * cx + cy


def _gather_weights(*shards):
    n_arr = len(shards)

    def kern(*refs):
        ins, outs = refs[:n_arr], refs[n_arr:2 * n_arr]
        send_sems, recv_sems, local_sems = refs[2 * n_arr:]
        x, y, c, chips = _position()
        me = _chip_index(x, y)
        sibling = (x, y, 1 - c)

        local = [pltpu.make_async_copy(ins[a], outs[a].at[me], local_sems.at[a]) for a in range(n_arr)]
        for cp in local:
            cp.start()

        def half(ref, a):
            rows = shards[a].shape[0] // 2
            return ref.at[pl.ds(c * rows, rows), :]

        def copy(a, k, src, slot, to):
            return pltpu.make_async_remote_copy(
                src_ref=src, dst_ref=half(outs[a].at[slot], a),
                send_sem=send_sems.at[a * 6 + k], recv_sem=recv_sems.at[a * 6 + k],
                device_id=to, device_id_type=MESH)

        first = [copy(a, j, half(ins[a], a), me, (*chip, c)) for a in range(n_arr) for j, chip in enumerate(chips)]
        for cp in first:
            cp.start()
        passed = []
        for a in range(n_arr):
            for j, chip in enumerate(chips):
                slot = _chip_index(*chip)
                copy(a, j, half(ins[a], a), slot, (*chip, c)).wait_recv()
                fwd = copy(a, 3 + j, half(outs[a].at[slot], a), slot, sibling)
                fwd.start()
                passed.append(fwd)
        for a in range(n_arr):
            for j, chip in enumerate(chips):
                slot = _chip_index(*chip)
                rows = shards[a].shape[0] // 2
                dst = outs[a].at[slot].at[pl.ds((1 - c) * rows, rows), :]
                pltpu.make_async_remote_copy(
                    src_ref=dst, dst_ref=dst, send_sem=send_sems.at[a * 6 + 3 + j],
                    recv_sem=recv_sems.at[a * 6 + 3 + j], device_id=sibling, device_id_type=MESH).wait_recv()
        for cp in first + passed:
            cp.wait_send()
        for cp in local:
            cp.wait()

    vmem = pl.BlockSpec(memory_space=pltpu.VMEM)
    return pl.pallas_call(
        kern, name="gather_weights",
        in_specs=[vmem] * n_arr,
        out_specs=[vmem] * n_arr,
        out_shape=[jax.ShapeDtypeStruct((N_CHIPS,) + w.shape, w.dtype) for w in shards],
        scratch_shapes=[pltpu.SemaphoreType.DMA((6 * n_arr,)),
                        pltpu.SemaphoreType.DMA((6 * n_arr,)),
                        pltpu.SemaphoreType.DMA((n_arr,))],
        compiler_params=_cparams(),
    )(*shards)


def _pair_reduce(grads):
    n_arr = len(grads)
    chunk = 128

    def kern(*refs):
        ins = refs[:n_arr]
        outs = refs[n_arr:2 * n_arr]
        gots = refs[2 * n_arr:3 * n_arr]
        send_sems, recv_sems = refs[3 * n_arr:]
        x, y, c, _ = _position()
        sibling = (x, y, 1 - c)
        copies = []
        for a in range(n_arr):
            rows = grads[a].shape[1] // 2
            copies.append(pltpu.make_async_remote_copy(
                src_ref=ins[a].at[:, pl.ds((1 - c) * rows, rows), :], dst_ref=gots[a],
                send_sem=send_sems.at[a], recv_sem=recv_sems.at[a], device_id=sibling, device_id_type=MESH))
        for cp in copies:
            cp.start()
        for a in range(n_arr):
            copies[a].wait()
            rows = grads[a].shape[1] // 2
            for j in range(N_CHIPS):
                for r0 in range(0, rows, chunk):
                    mine = ins[a][j, pl.ds(pl.multiple_of(c * rows + r0, chunk), chunk), :]
                    outs[a][j, r0:r0 + chunk, :] = (mine + gots[a][j, r0:r0 + chunk, :]).astype(BF16)

    vmem = pl.BlockSpec(memory_space=pltpu.VMEM)
    half = [(N_CHIPS, g.shape[1] // 2, g.shape[2]) for g in grads]
    return pl.pallas_call(
        kern, name="pair_reduce",
        in_specs=[vmem] * n_arr,
        out_specs=[vmem] * n_arr,
        out_shape=[jax.ShapeDtypeStruct(h, BF16) for h in half],
        scratch_shapes=[pltpu.VMEM(h, F32) for h in half]
        + [pltpu.SemaphoreType.DMA((n_arr,)), pltpu.SemaphoreType.DMA((n_arr,))],
        compiler_params=_cparams(),
    )(*grads)


def _wo_gather_start(shard):
    hbm = pl.BlockSpec(memory_space=pltpu.HBM)
    sem = pl.BlockSpec(memory_space=pltpu.SEMAPHORE)
    land_shape = (N_CHIPS,) + shard.shape

    def kern(src_ref, land_ref, send_sems, recv_sems, src_thru, land_thru, token):
        x, y, c, chips = _position()
        me = _chip_index(x, y)
        for j, chip in enumerate(chips):
            pltpu.make_async_remote_copy(
                src_ref=src_ref, dst_ref=land_ref.at[me], send_sem=send_sems.at[j], recv_sem=recv_sems.at[j],
                device_id=(*chip, c), device_id_type=MESH).start()
        token[...] = jnp.zeros_like(token)

    return pl.pallas_call(
        kern, name="wo_gather_start",
        in_specs=[hbm, hbm],
        out_specs=(sem, sem, hbm, hbm, pl.BlockSpec(memory_space=pltpu.VMEM)),
        out_shape=(pltpu.SemaphoreType.DMA((3,)), pltpu.SemaphoreType.DMA((3,)),
                   pltpu.HBM(shard.shape, shard.dtype), pltpu.HBM(land_shape, shard.dtype),
                   jax.ShapeDtypeStruct((8, LANES), F32)),
        input_output_aliases={0: 2, 1: 3},
        compiler_params=pltpu.CompilerParams(has_side_effects=pltpu.SideEffectType.DATAFLOW_SIDE_EFFECTING),
    )(pltpu.with_memory_space_constraint(shard, pltpu.HBM),
      pltpu.with_memory_space_constraint(lax.empty(land_shape, shard.dtype), pltpu.HBM))


def _wo_gather_wait(send_sems, recv_sems, src_thru, land_thru, after):
    hbm = pl.BlockSpec(memory_space=pltpu.HBM)
    sem = pl.BlockSpec(memory_space=pltpu.SEMAPHORE)

    def kern(src_ref, land_ref, send_sems, recv_sems, after_ref, src_out, land_out):
        x, y, c, chips = _position()
        for j, chip in enumerate(chips):
            copy = pltpu.make_async_remote_copy(
                src_ref=src_ref, dst_ref=land_ref.at[_chip_index(*chip)], send_sem=send_sems.at[j],
                recv_sem=recv_sems.at[j], device_id=(*chip, c), device_id_type=MESH)
            copy.wait_send()
            copy.wait_recv()

    return pl.pallas_call(
        kern, name="wo_gather_wait",
        in_specs=[hbm, hbm, sem, sem, pl.BlockSpec(memory_space=pl.ANY)],
        out_specs=[hbm, hbm],
        out_shape=[pltpu.HBM(src_thru.shape, src_thru.dtype), pltpu.HBM(land_thru.shape, land_thru.dtype)],
        input_output_aliases={0: 0, 1: 1},
        compiler_params=pltpu.CompilerParams(has_side_effects=pltpu.SideEffectType.DATAFLOW_SIDE_EFFECTING),
    )(src_thru, land_thru, send_sems, recv_sems, after)[1]


def _scatter_start(parts):
    n_arr = len(parts)
    hbm = pl.BlockSpec(memory_space=pltpu.HBM)
    sem = pl.BlockSpec(memory_space=pltpu.SEMAPHORE)

    def kern(*refs):
        ins, lands = refs[:n_arr], refs[n_arr:2 * n_arr]
        send_sems, recv_sems, token = refs[2 * n_arr], refs[2 * n_arr + 1], refs[-1]
        x, y, c, chips = _position()
        me = _chip_index(x, y)
        for a in range(n_arr):
            for j, chip in enumerate(chips):
                pltpu.make_async_remote_copy(
                    src_ref=ins[a].at[_chip_index(*chip)], dst_ref=lands[a].at[me],
                    send_sem=send_sems.at[a * 3 + j], recv_sem=recv_sems.at[a * 3 + j],
                    device_id=(*chip, c), device_id_type=MESH).start()
        token[...] = jnp.zeros_like(token)

    slab = [pltpu.HBM(p.shape, p.dtype) for p in parts]
    outs = pl.pallas_call(
        kern, name="scatter_start",
        in_specs=[hbm] * (2 * n_arr),
        out_specs=(sem, sem, *[hbm] * (2 * n_arr), pl.BlockSpec(memory_space=pltpu.VMEM)),
        out_shape=(pltpu.SemaphoreType.DMA((3 * n_arr,)), pltpu.SemaphoreType.DMA((3 * n_arr,)),
                   *slab, *slab, jax.ShapeDtypeStruct((8, LANES), F32)),
        input_output_aliases={i: 2 + i for i in range(2 * n_arr)},
        compiler_params=pltpu.CompilerParams(has_side_effects=pltpu.SideEffectType.DATAFLOW_SIDE_EFFECTING),
    )(*[pltpu.with_memory_space_constraint(p, pltpu.HBM) for p in parts],
      *[pltpu.with_memory_space_constraint(lax.empty(p.shape, p.dtype), pltpu.HBM) for p in parts])
    return outs[0], outs[1], outs[2:2 + n_arr], outs[2 + n_arr:2 + 2 * n_arr], outs[-1]


def _scatter_wait(send_sems, recv_sems, parts_thru, lands_thru, after):
    n_arr = len(parts_thru)
    hbm = pl.BlockSpec(memory_space=pltpu.HBM)
    sem = pl.BlockSpec(memory_space=pltpu.SEMAPHORE)

    def kern(*refs):
        ins, lands = refs[:n_arr], refs[n_arr:2 * n_arr]
        send_ref, recv_ref = refs[2 * n_arr], refs[2 * n_arr + 1]
        x, y, c, chips = _position()
        for a in range(n_arr):
            for j, chip in enumerate(chips):
                slot = _chip_index(*chip)
                copy = pltpu.make_async_remote_copy(
                    src_ref=ins[a].at[slot], dst_ref=lands[a].at[slot],
                    send_sem=send_ref.at[a * 3 + j], recv_sem=recv_ref.at[a * 3 + j],
                    device_id=(*chip, c), device_id_type=MESH)
                copy.wait_send()
                copy.wait_recv()

    slab = [pltpu.HBM(p.shape, p.dtype) for p in parts_thru]
    outs = pl.pallas_call(
        kern, name="scatter_wait",
        in_specs=[hbm] * (2 * n_arr) + [sem, sem, pl.BlockSpec(memory_space=pl.ANY)],
        out_specs=[hbm] * (2 * n_arr),
        out_shape=slab + slab,
        input_output_aliases={i: i for i in range(2 * n_arr)},
        compiler_params=pltpu.CompilerParams(has_side_effects=pltpu.SideEffectType.DATAFLOW_SIDE_EFFECTING),
    )(*parts_thru, *lands_thru, send_sems, recv_sems, after)
    return outs[:n_arr], outs[n_arr:]


def _chip_sum_swap(parts, lands):
    n_arr = len(parts)
    chunk = 128

    def kern(*refs):
        own, got = refs[:n_arr], refs[n_arr:2 * n_arr]
        outs = refs[2 * n_arr:3 * n_arr]
        sums = refs[3 * n_arr:4 * n_arr]
        swap_send, swap_recv, swap_local = refs[4 * n_arr:]
        x, y, c, _ = _position()
        me = _chip_index(x, y)
        sibling = (x, y, 1 - c)
        for a in range(n_arr):
            for r0 in range(0, parts[a].shape[1], chunk):
                mine = own[a][me, r0:r0 + chunk, :].astype(F32)

                def term(i):
                    other = got[a][jnp.where(i == me, (i + 1) % N_CHIPS, i), r0:r0 + chunk, :].astype(F32)
                    return jnp.where(i == me, mine, other)
                sums[a][r0:r0 + chunk, :] = ((term(0) + term(1)) + term(2)) + term(3)
        swap_l, swap_r = [], []
        for a in range(n_arr):
            rows = parts[a].shape[1]
            mine = outs[a].at[pl.ds(c * rows, rows), :]
            swap_l.append(pltpu.make_async_copy(sums[a], mine, swap_local.at[a]))
            swap_r.append(pltpu.make_async_remote_copy(
                src_ref=sums[a], dst_ref=mine, send_sem=swap_send.at[a], recv_sem=swap_recv.at[a],
                device_id=sibling, device_id_type=MESH))
        for cp in swap_l + swap_r:
            cp.start()
        for a in range(n_arr):
            rows = parts[a].shape[1]
            theirs = outs[a].at[pl.ds((1 - c) * rows, rows), :]
            pltpu.make_async_remote_copy(
                src_ref=theirs, dst_ref=theirs, send_sem=swap_send.at[a], recv_sem=swap_recv.at[a],
                device_id=sibling, device_id_type=MESH).wait_recv()
        for cp in swap_r:
            cp.wait_send()
        for cp in swap_l:
            cp.wait()

    vmem = pl.BlockSpec(memory_space=pltpu.VMEM)
    return pl.pallas_call(
        kern, name="chip_sum_swap",
        in_specs=[vmem] * (2 * n_arr),
        out_specs=[vmem] * n_arr,
        out_shape=[jax.ShapeDtypeStruct((2 * p.shape[1], p.shape[2]), F32) for p in parts],
        scratch_shapes=[pltpu.VMEM(p.shape[1:], F32) for p in parts]
        + [pltpu.SemaphoreType.DMA((n_arr,)),
           pltpu.SemaphoreType.DMA((n_arr,)),
           pltpu.SemaphoreType.DMA((n_arr,))],
        compiler_params=_cparams(),
    )(*parts, *lands)


def _small_allreduce_adamw(partials, params, moms, vels):
    chunks = D_MODEL // LANES
    row_rb, row_bf, row_sk, row_loss = 2 * chunks, 2 * chunks + NUM_BUCKETS, 2 * chunks + NUM_BUCKETS + 1, SMALL_ROWS - 6

    def kern(gbf_ref, grb_ref, gsk_ref, gg_ref, gb_ref, loss_ref, *refs):
        p_refs, m_refs, v_refs = refs[0:5], refs[5:10], refs[10:15]
        lo_ref, g_outs, d_outs, mo_outs, vo_outs = refs[15], refs[16:21], refs[21:26], refs[26:31], refs[31:36]
        send_ref, buf_ref, send_sems, recv_sems = refs[36:]
        x, y, c, _ = _position()
        me = 4 * x + 2 * y + c
        send_ref[...] = jnp.zeros_like(send_ref)
        for r in range(chunks):
            send_ref[r:r + 1, :] = gg_ref[0:1, r * LANES:(r + 1) * LANES]
            send_ref[chunks + r:chunks + r + 1, :] = gb_ref[0:1, r * LANES:(r + 1) * LANES]
        send_ref[row_rb:row_rb + NUM_BUCKETS, :] = grb_ref[...]
        send_ref[row_bf:row_bf + 1, :] = gbf_ref[0:1, :]
        send_ref[row_sk:row_sk + 1, :] = gsk_ref[0:1, :]
        send_ref[row_loss:row_loss + 1, :] = loss_ref[0:1, :]
        buf_ref[me] = send_ref[...]
        peers = [(x, y, 1 - c)] + [(px, py, pc) for px, py in _position()[3] for pc in (c, 1 - c)]
        sends = []
        for k, peer in enumerate(peers):
            sends.append(pltpu.make_async_remote_copy(
                src_ref=send_ref, dst_ref=buf_ref.at[me], send_sem=send_sems.at[k], recv_sem=recv_sems.at[k],
                device_id=peer, device_id_type=MESH))
        for cp in sends:
            cp.start()
        for k, (px, py, pc) in enumerate(peers):
            slot = buf_ref.at[4 * px + 2 * py + pc]
            pltpu.make_async_remote_copy(
                src_ref=slot, dst_ref=slot, send_sem=send_sems.at[k], recv_sem=recv_sems.at[k],
                device_id=(px, py, pc), device_id_type=MESH).wait_recv()
        for cp in sends:
            cp.wait_send()
        tot = buf_ref[0]
        for d in range(1, N_DEV):
            tot = tot + buf_ref[d]
        lo_ref[...] = tot[row_loss:row_loss + 1, :]
        grads = [tot[row_bf:row_bf + 1, 0:FOX_HEADS],
                 tot[row_rb:row_rb + NUM_BUCKETS, 0:SWA_HEADS],
                 tot[row_sk:row_sk + 1, 0:SWA_HEADS],
                 jnp.concatenate([tot[r:r + 1, :] for r in range(chunks)], axis=1),
                 jnp.concatenate([tot[chunks + r:chunks + r + 1, :] for r in range(chunks)], axis=1)]
        for i, g in enumerate(grads):
            g_outs[i][...] = g
            delta, mn, vn = _adamw_math(p_refs[i][...], g, m_refs[i][...], v_refs[i][...])
            d_outs[i][...] = delta
            mo_outs[i][...] = mn
            vo_outs[i][...] = vn

    vm = pl.BlockSpec(memory_space=pltpu.VMEM)
    shapes = [jax.ShapeDtypeStruct(p.shape, F32) for p in params]
    outs = pl.pallas_call(
        kern, name="small_allreduce_adamw",
        in_specs=[vm] * 21,
        out_specs=[vm] * 21,
        out_shape=[jax.ShapeDtypeStruct((1, LANES), F32)] + shapes * 4,
        scratch_shapes=[pltpu.VMEM((SMALL_ROWS, LANES), F32),
                        pltpu.VMEM((N_DEV, SMALL_ROWS, LANES), F32),
                        pltpu.SemaphoreType.DMA((N_DEV - 1,)),
                        pltpu.SemaphoreType.DMA((N_DEV - 1,))],
    )(*partials, *params, *moms, *vels)
    return outs[0], outs[1:6], outs[6:11], outs[11:16], outs[16:21]


def _to_padded_cols(w):
    pad = jnp.zeros((w.shape[0], N_C - FOX_HEADS), w.dtype)
    return jnp.concatenate([w[:, 0:1536], w[:, 2056:2824], w[:, 1536:1544], pad,
                            w[:, 1544:2056], w[:, 2824:3336]], axis=1)


_COLUMN_RUNS = ((0, 0, 1536), (1536, OFF_C, FOX_HEADS), (1544, OFF_B, FOX_W), (2056, 1536, N_A - 1536),
                (2824, OFF_B + FOX_W, SWA_W))


def _shards_from_padded(g):
    shard_cols = D_IN // N_CHIPS
    shards = []
    for j in range(N_CHIPS):
        lo, hi = j * shard_cols, (j + 1) * shard_cols
        pieces = []
        for ref0, pad0, width in _COLUMN_RUNS:
            a, b = max(lo, ref0), min(hi, ref0 + width)
            if a < b:
                pieces.append(g[:, pad0 + a - ref0:pad0 + b - ref0])
        pieces.append(jnp.zeros((g.shape[0], SHARD_PAD - shard_cols), g.dtype))
        shards.append(jnp.concatenate(pieces, axis=1))
    return jnp.stack(shards)


def _fox_rows(a):
    return a[:, :FOX_HEADS].T.reshape(FOX_HEADS, 1, a.shape[0])


def kernel(x, w_in, b_f, rel_bias, sink, w_o, ln_g, ln_b, loss_target, m_w_in, m_b_f, m_rel_bias, m_sink, m_w_o, m_ln_g, m_ln_b, v_w_in, v_b_f, v_rel_bias, v_sink, v_w_o, v_ln_g, v_ln_b):
    x2 = x[0]
    tgt = loss_target[0]
    s = x2.shape[0]
    w_in2, w_o2 = w_in[0], w_o[0]

    shard_cols = D_IN // N_CHIPS
    col_pad = ((0, 0), (0, SHARD_PAD - shard_cols))
    (w_in_all,) = _gather_weights(jnp.pad(w_in2.astype(BF16), col_pad))
    w_full = jnp.concatenate([w_in_all[j, :, :shard_cols] for j in range(N_CHIPS)], axis=1)
    w_pad = _to_padded_cols(w_full)
    w_o_bf, _ = lax.optimization_barrier((w_o2.astype(BF16), w_in_all))
    wo_send, wo_recv, wo_src, wo_land, wo_token = _wo_gather_start(w_o_bf)

    qkv, ffp, z, xt, vt = _project(x2, w_pad, wo_token)
    bfp = jnp.pad(b_f, ((0, 0), (0, LANES - FOX_HEADS)))
    cum = _cum_fwd(ffp, bfp)
    cum_t3 = _fox_rows(cum)
    o_fox, lse_t3 = _fox_fwd(qkv, vt, cum_t3, cum)
    bucket_t = jnp.asarray(_bucket_table().T)
    bias_t = _swa_bias(rel_bias, bucket_t)
    sink_rows = jnp.repeat(sink.reshape(SWA_KV_HEADS, SWA_GROUP, 1), BLOCK, axis=2).reshape(SWA_KV_HEADS, 1, SWA_LANES)
    o_swa, lse_swa = _swa_fwd(qkv, bias_t, sink_rows)

    wo_land = _wo_gather_wait(wo_send, wo_recv, wo_src, wo_land, o_swa)
    my_chip = _chip_index(lax.axis_index("x"), lax.axis_index("y"))
    w_o_full = lax.dynamic_update_slice(wo_land, w_o_bf[None], (my_chip, 0, 0)).reshape(D_MODEL, D_MODEL)
    loss8, dh, grad_w_o_full, do_bf, dz, delta, gg8, gb8 = _post(
        x2, tgt, o_fox, o_swa, z, w_o_full, ln_g, ln_b)

    delta_t3 = _fox_rows(delta)
    dq_fox, dk_fox, dv_fox, dcum_k, dcum_q = _fox_bwd(qkv, do_bf, cum_t3, cum, lse_t3, delta_t3)
    dcum_q = jnp.pad(dcum_q.reshape(FOX_HEADS, s).T, ((0, 0), (0, LANES - FOX_HEADS)))
    dff, gbf8 = _cum_bwd(dcum_k, dcum_q, ffp, bfp)
    delta_rows = (delta[:, FOX_HEADS:FOX_HEADS + SWA_HEADS].reshape(s // BLOCK, BLOCK, SWA_KV_HEADS, SWA_GROUP)
                  .transpose(0, 2, 3, 1).reshape(s // BLOCK, SWA_KV_HEADS, 1, SWA_LANES))
    dq_swa, dk_swa, dv_swa, grb, gsk8 = _swa_bwd(qkv, do_bf, delta_rows, lse_swa, bias_t, sink_rows, bucket_t)

    d_misc = jnp.concatenate([dk_swa, dv_swa, dff], axis=1)
    pieces = [dq_fox, dk_fox, dv_fox, dq_swa, d_misc, dz]
    blocks = [(p, 0) for p in pieces[:-1]] + [(dz, 0), (dz, 1)]
    grad_w_pad = _grad_w_matmul(xt, blocks, tk=1024, name="grad_w_in")

    g_in4 = _shards_from_padded(grad_w_pad)
    g_o4 = grad_w_o_full.reshape(N_CHIPS, D_MODEL // N_CHIPS, D_MODEL)
    parts = _pair_reduce([g_in4, g_o4])
    send_sems, recv_sems, parts_thru, lands_thru, token = _scatter_start(parts)
    grad_x = _grad_x_matmul(pieces, w_pad, dh, token, tm=512, tn=D_MODEL, name="grad_x")
    parts, lands = _scatter_wait(send_sems, recv_sems, parts_thru, lands_thru, grad_x)
    g_w_in, g_w_o = _chip_sum_swap(parts, lands)
    g_w_in = g_w_in[:, :shard_cols]

    cols_first = lambda a: jnp.transpose(a, (2, 0, 1))
    rows_first = lambda a: jnp.transpose(a, (1, 2, 0))
    g_w_in, d_w_in, nm_w_in, nv_w_in = [rows_first(a) for a in _adamw_cols(
        cols_first(w_in), cols_first(g_w_in[None]), cols_first(m_w_in), cols_first(v_w_in), name="adamw_w_in")]
    d_w_o, nm_w_o, nv_w_o = _adamw(w_o2, g_w_o, m_w_o[0], v_w_o[0], name="adamw_w_o")

    loss_row, gs, ds, ms, vs = _small_allreduce_adamw(
        [gbf8, grb, gsk8, gg8, gb8, loss8],
        [b_f, rel_bias, sink, ln_g, ln_b],
        [m_b_f, m_rel_bias, m_sink, m_ln_g, m_ln_b],
        [v_b_f, v_rel_bias, v_sink, v_ln_g, v_ln_b])
    loss = loss_row[0, 0]
    g_bf, g_rb, g_sk, g_lg, g_lb = gs
    d_bf, d_rb, d_sk, d_lg, d_lb = ds
    m_bf, m_rb, m_sk, m_lg, m_lb = ms
    v_bf, v_rb, v_sk, v_lg, v_lb = vs

    e = lambda a: a[None]
    return (loss, e(grad_x),
            g_w_in, g_bf, g_rb, g_sk, e(g_w_o), g_lg, g_lb,
            d_w_in, d_bf, d_rb, d_sk, e(d_w_o), d_lg, d_lb,
            nm_w_in, m_bf, m_rb, m_sk, e(nm_w_o), m_lg, m_lb,
            nv_w_in, v_bf, v_rb, v_sk, e(nv_w_o), v_lg, v_lb)
```

```python
import functools
import math

import numpy as np
import jax
import jax.numpy as jnp
from jax import lax
from jax.experimental import pallas as pl
from jax.experimental.pallas import tpu as pltpu

F32 = jnp.float32
BF16 = jnp.bfloat16

D_MODEL = 1024
HEAD_DIM = 64
FOX_HEADS = 8
SWA_HEADS = 8
SWA_KV_HEADS = 2
SWA_GROUP = 4
FOX_W = 512
SWA_W = 512
BLOCK = 128
NUM_BUCKETS = 32
MAX_DISTANCE = 128
LN_EPS = 1e-5
NEG = -1e30
ALPHA = 2.0 ** 0.25
QK_SCALE = 0.125

ADAM_LR = 0.001
ADAM_B1 = 0.9
ADAM_B2 = 0.999
ADAM_EPS = 1e-08
ADAM_WD = 0.01
ADAM_STEP = 10

D_IN = 3336
SHARD_PAD = 896
N_A = 2304
N_C = 256
N_B = 1024
OFF_C = N_A
OFF_B = N_A + N_C
N_PAD = N_A + N_C + N_B
COL_FK, COL_FV, COL_SQ, COL_SK, COL_SV = 512, 1024, 1536, 2048, 2176

LANES = 128
FOX_T = 256
FOX_REF = 512
SUM_ROWS = 16
VMEM_LIMIT = 56 * 1024 * 1024

MESH = pl.DeviceIdType.MESH
N_CHIPS = 4
N_DEV = 8
SMALL_ROWS = 56


def _cparams(sem=None):
    return pltpu.CompilerParams(dimension_semantics=sem, vmem_limit_bytes=VMEM_LIMIT)


def _split3(x):
    hi = x.astype(BF16)
    r = x - hi.astype(F32)
    mid = r.astype(BF16)
    lo = (r - mid.astype(F32)).astype(BF16)
    return hi, mid, lo


def _dot(a, b):
    return jnp.dot(a, b, preferred_element_type=F32)


def _dot_nt(a, b):
    return lax.dot_general(a, b, (((1,), (1,)), ((), ())), preferred_element_type=F32)


def _project(x, w_pad, token):
    s, k = x.shape
    tm = 512
    chunk = 512

    def kern(x_ref, w_ref, _, qkv_ref, ff_ref, z_ref, xt_ref, vt_ref):
        xf = x_ref[...]
        xb = xf.astype(BF16)
        xt_ref[...] = xf.T.astype(BF16)
        for c0 in range(0, N_A, chunk):
            width = min(chunk, N_A - c0)
            res = _dot(xb, w_ref[:, c0:c0 + width])
            qkv_ref[:, c0:c0 + width] = res.astype(BF16)
            if c0 == COL_FV:
                vt_ref[...] = res.T.astype(BF16)
        ff_ref[...] = _dot(xb, w_ref[:, OFF_C:OFF_C + N_C])
        for c0 in range(0, N_B, 512):
            z_ref[:, c0:c0 + 512] = _dot(xb, w_ref[:, OFF_B + c0:OFF_B + c0 + 512])

    row = lambda i: (i, 0)
    return pl.pallas_call(
        kern, name="project",
        grid=(s // tm,),
        in_specs=[pl.BlockSpec((tm, k), row),
                  _resident((k, N_PAD), lambda i: (0, 0)),
                  _resident(token.shape, lambda i: (0, 0))],
        out_specs=[pl.BlockSpec((tm, N_A), row),
                   pl.BlockSpec((tm, N_C), row),
                   pl.BlockSpec((tm, N_B), row),
                   pl.BlockSpec((k, tm), lambda i: (0, i)),
                   pl.BlockSpec((FOX_W, tm), lambda i: (0, i))],
        out_shape=[jax.ShapeDtypeStruct((s, N_A), BF16),
                   jax.ShapeDtypeStruct((s, N_C), F32),
                   jax.ShapeDtypeStruct((s, N_B), F32),
                   jax.ShapeDtypeStruct((k, s), BF16),
                   jax.ShapeDtypeStruct((FOX_W, s), BF16)],
        compiler_params=_cparams(("parallel",)),
    )(x, w_pad, token)


def _grad_x_matmul(pieces, w_pad, dh, token, *, tm, tn, name):
    m = dh.shape[0]
    n, k = w_pad.shape
    widths = [p.shape[1] for p in pieces]
    offs = [sum(widths[:i]) for i in range(len(pieces))]
    assert sum(widths) == k

    def kern(*refs):
        p_refs, (b_ref, dh_ref, _, o_ref) = refs[:len(pieces)], refs[len(pieces):]
        acc = ALPHA * dh_ref[...]
        for p_ref, off, width in zip(p_refs, offs, widths):
            acc = acc + _dot_nt(p_ref[...], b_ref[:, off:off + width])
        o_ref[...] = acc

    assert tn == n
    return pl.pallas_call(
        kern, name=name,
        grid=(m // tm,),
        in_specs=[pl.BlockSpec((tm, w), lambda i: (i, 0)) for w in widths]
        + [_resident((n, k), lambda i: (0, 0)),
           pl.BlockSpec((tm, n), lambda i: (i, 0)),
           _resident(token.shape, lambda i: (0, 0))],
        out_specs=pl.BlockSpec((tm, n), lambda i: (i, 0)),
        out_shape=jax.ShapeDtypeStruct((m, n), F32),
        compiler_params=_cparams(("parallel",)),
    )(*pieces, w_pad, dh, token)


def _grad_w_matmul(xt, blocks, *, tk, name):
    m, s = xt.shape
    tn = 512
    nb = len(blocks)

    def kern(a_ref, *refs):
        b_refs, o_ref = refs[:nb], refs[nb]

        @pl.when(pl.program_id(0) == 0)
        def _():
            o_ref[...] = jnp.zeros_like(o_ref)
        a = a_ref[...]
        for blk in range(nb):
            o_ref[:, blk * tn:(blk + 1) * tn] += _dot(a, b_refs[blk][...])

    return pl.pallas_call(
        kern, name=name,
        grid=(s // tk,),
        in_specs=[pl.BlockSpec((m, tk), lambda k: (0, k))]
        + [pl.BlockSpec((tk, tn), functools.partial(lambda k, col: (k, col), col=col)) for _, col in blocks],
        out_specs=_resident((m, nb * tn), lambda k: (0, 0)),
        out_shape=jax.ShapeDtypeStruct((m, nb * tn), F32),
        compiler_params=_cparams(("arbitrary",)),
    )(xt, *[arr for arr, _ in blocks])


def _tri(n, lower):
    r = lax.broadcasted_iota(jnp.int32, (n, n), 0)
    c = lax.broadcasted_iota(jnp.int32, (n, n), 1)
    keep = (c <= r) if lower else (c >= r)
    return jnp.where(keep, 1.0, 0.0).astype(BF16)


def _exact_dot(mat_bf16, x_f32, left):
    out = None
    for piece in _split3(x_f32):
        t = _dot(mat_bf16, piece) if left else _dot(piece, mat_bf16)
        out = t if out is None else out + t
    return out


def _log_sigmoid(z):
    return jnp.minimum(z, 0.0) - jnp.log(1.0 + jnp.exp(-jnp.abs(z)))


def _cum_fwd(ffp, bfp):
    s = ffp.shape[0]
    t = min(1024, s)

    def kern(ff_ref, b_ref, cum_ref, carry_ref):
        @pl.when(pl.program_id(0) == 0)
        def _():
            carry_ref[...] = jnp.zeros_like(carry_ref)
        lane = lax.broadcasted_iota(jnp.int32, (1, LANES), 1)
        lf = _log_sigmoid(ff_ref[...] + b_ref[...])
        lf = jnp.where(lane < FOX_HEADS, lf, 0.0)
        cum = _exact_dot(_tri(t, True), lf, True) + carry_ref[0:1, :]
        cum_ref[...] = cum
        carry_ref[...] = jnp.broadcast_to(cum[t - 1:t, :], carry_ref.shape)

    return pl.pallas_call(
        kern, name="cum_fwd",
        grid=(s // t,),
        in_specs=[pl.BlockSpec((t, LANES), lambda i: (i, 0)),
                  pl.BlockSpec((1, LANES), lambda i: (0, 0))],
        out_specs=pl.BlockSpec((t, LANES), lambda i: (i, 0)),
        out_shape=jax.ShapeDtypeStruct((s, LANES), F32),
        scratch_shapes=[pltpu.VMEM((8, LANES), F32)],
        compiler_params=_cparams(("arbitrary",)),
    )(ffp, bfp)


def _cum_bwd(dcum_k, dcum_q, ffp, bfp):
    s = dcum_k.shape[0]
    t = min(1024, s)
    nb = s // t

    def kern(dck_ref, dcq_ref, ff_ref, b_ref, dff_ref, gb_ref, carry_ref):
        @pl.when(pl.program_id(0) == 0)
        def _():
            carry_ref[...] = jnp.zeros_like(carry_ref)
            gb_ref[...] = jnp.zeros_like(gb_ref)
        lane = lax.broadcasted_iota(jnp.int32, (1, LANES), 1)
        dlf = _exact_dot(_tri(t, False), dck_ref[...] + dcq_ref[...], True) + carry_ref[0:1, :]
        carry_ref[...] = jnp.broadcast_to(dlf[0:1, :], carry_ref.shape)
        z = ff_ref[...] + b_ref[...]
        dff = jnp.where(lane < FOX_HEADS, dlf / (1.0 + jnp.exp(z)), 0.0)
        gb_ref[...] += jnp.broadcast_to(jnp.sum(dff, axis=0, keepdims=True), gb_ref.shape)
        dff_ref[...] = jnp.concatenate([dff, jnp.zeros_like(dff)], axis=1).astype(BF16)

    return pl.pallas_call(
        kern, name="cum_bwd",
        grid=(nb,),
        in_specs=[pl.BlockSpec((t, LANES), lambda i: (nb - 1 - i, 0)),
                  pl.BlockSpec((t, LANES), lambda i: (nb - 1 - i, 0)),
                  pl.BlockSpec((t, LANES), lambda i: (nb - 1 - i, 0)),
                  pl.BlockSpec((1, LANES), lambda i: (0, 0))],
        out_specs=[pl.BlockSpec((t, N_C), lambda i: (nb - 1 - i, 0)),
                   pl.BlockSpec((8, LANES), lambda i: (0, 0))],
        out_shape=[jax.ShapeDtypeStruct((s, N_C), BF16),
                   jax.ShapeDtypeStruct((8, LANES), F32)],
        scratch_shapes=[pltpu.VMEM((8, LANES), F32)],
        compiler_params=_cparams(("arbitrary",)),
    )(dcum_k, dcum_q, ffp, bfp)


def _resident(shape, index_map):
    return pl.BlockSpec(shape, index_map, pipeline_mode=pl.Buffered(1))


def _fox_fwd(qkv, vt, cum_t3, cum):
    s = qkv.shape[0]
    tk = tq = FOX_REF
    nq = s // tq
    nh = FOX_HEADS
    diag_tiles = tq // tk

    def kern(q_ref, k_ref, vt_ref, ct_ref, c_ref, o_ref, lse_ref, m_ref, acc_ref, u_ref):
        i = pl.program_id(0)
        lane = lax.broadcasted_iota(jnp.int32, (1, LANES), 1)
        krow = lax.broadcasted_iota(jnp.int32, (tk, tq), 0)
        qcol = lax.broadcasted_iota(jnp.int32, (tk, tq), 1)
        q0 = pl.multiple_of(i * tq, tq)
        qts, crefs = [], []
        for h in range(nh):
            p, a = divmod(h, 2)
            q2 = q_ref[:, p * LANES:(p + 1) * LANES] * jnp.asarray(QK_SCALE, BF16)
            sel = (lane < HEAD_DIM) if a == 0 else (lane >= HEAD_DIM)
            qts.append(jnp.where(sel, q2, jnp.zeros_like(q2)).astype(F32).T.astype(BF16))
            crefs.append(ct_ref[h, :, pl.ds(q0, LANES)][:, 0:1])
        m_ref[...] = jnp.full(m_ref.shape, NEG, F32)
        acc_ref[...] = jnp.zeros_like(acc_ref)
        ones = jnp.ones((SUM_ROWS, tk), BF16)

        def tile(j, diag):
            k0 = pl.multiple_of(j * tk, tk)
            cb = c_ref[pl.ds(k0, tk), :]
            sts = [_dot(k_ref[pl.ds(k0, tk), (h // 2) * LANES:(h // 2 + 1) * LANES], qts[h]) for h in range(nh)]
            tile_max = []
            for h in range(nh):
                u = sts[h] - (cb[:, h:h + 1] - crefs[h])
                if diag is not None:
                    u = jnp.where(krow + diag * tk <= qcol, u, NEG)
                u_ref[h] = u
                tile_max.append(jnp.max(u, axis=0, keepdims=True))
            pts, scales = [], []
            for h in range(nh):
                m_old = m_ref[h]
                m_new = jnp.maximum(m_old, tile_max[h])
                scales.append(jnp.exp(m_old - m_new))
                pts.append(jnp.exp(u_ref[h] - m_new).astype(BF16))
                m_ref[h] = m_new
            for h in range(nh):
                vth = jnp.concatenate([vt_ref[h * HEAD_DIM:(h + 1) * HEAD_DIM, pl.ds(k0, tk)], ones], axis=0)
                acc_ref[h] = scales[h] * acc_ref[h] + _dot(vth, pts[h])

        def body(j, c):
            tile(j, None)
            return c
        lax.fori_loop(0, i * diag_tiles, body, 0)
        for d in range(diag_tiles):
            tile(i * diag_tiles + d, d)

        ls = [acc_ref[h][HEAD_DIM:HEAD_DIM + 1] for h in range(nh)]
        for p in range(nh // 2):
            ot = jnp.concatenate([acc_ref[2 * p + a][:HEAD_DIM] * (1.0 / ls[2 * p + a]) for a in range(2)], axis=0)
            o_ref[:, p * LANES:(p + 1) * LANES] = ot.T
        for h in range(nh):
            lse_ref[h, :, pl.ds(q0, tq)] = m_ref[h] + jnp.log(ls[h])

    return pl.pallas_call(
        kern, name="fox_fwd",
        grid=(nq,),
        in_specs=[pl.BlockSpec((tq, FOX_W), lambda i: (i, 0)),
                  _resident((s, FOX_W), lambda i: (0, COL_FK // FOX_W)),
                  _resident((FOX_W, s), lambda i: (0, 0)),
                  _resident((nh, 1, s), lambda i: (0, 0, 0)),
                  _resident((s, LANES), lambda i: (0, 0))],
        out_specs=[pl.BlockSpec((tq, FOX_W), lambda i: (i, 0)),
                   pl.BlockSpec((nh, 1, s), lambda i: (0, 0, 0))],
        out_shape=[jax.ShapeDtypeStruct((s, FOX_W), F32),
                   jax.ShapeDtypeStruct((nh, 1, s), F32)],
        scratch_shapes=[pltpu.VMEM((nh, 1, tq), F32),
                        pltpu.VMEM((nh, HEAD_DIM + SUM_ROWS, tq), F32),
                        pltpu.VMEM((nh, tk, tq), F32)],
        compiler_params=_cparams(("arbitrary",)),
    )(qkv, qkv, vt, cum_t3, cum)


def _fox_bwd(qkv, do_bf, cum_t3, cum, lse_t3, delta_t3):
    s = qkv.shape[0]
    t = min(FOX_T, s)
    nq = s // t
    nh = FOX_HEADS
    npair = nh // 2

    def kern(q_ref, do_ref, k_ref, v_ref, ct_ref, c_ref, lse_ref, dl_ref,
             dq_ref, dk_ref, dv_ref, dc_ref, dcq_ref, dqt_ref, accv_ref, acck_ref, accd_ref):
        kj = pl.program_id(0)
        lane = lax.broadcasted_iota(jnp.int32, (1, LANES), 1)
        krow = lax.broadcasted_iota(jnp.int32, (t, t), 0)
        qcol = lax.broadcasted_iota(jnp.int32, (t, t), 1)
        causal = krow <= qcol
        sels = [lane < HEAD_DIM, lane >= HEAD_DIM]

        @pl.when(kj == 0)
        def _():
            dqt_ref[...] = jnp.zeros_like(dqt_ref)
            dcq_ref[...] = jnp.zeros_like(dcq_ref)

        cb = c_ref[...]
        k2s, v2s, kts = [], [], []
        for p in range(npair):
            k2 = k_ref[:, p * LANES:(p + 1) * LANES]
            k2s.append(k2)
            v2s.append(v_ref[:, p * LANES:(p + 1) * LANES])
            kt = k2.astype(F32).T * QK_SCALE
            kts.append(kt[:HEAD_DIM].astype(BF16))
            kts.append(kt[HEAD_DIM:].astype(BF16))
        css = [cb[:, h:h + 1] for h in range(nh)]

        def tile(i, masked):
            q0 = pl.multiple_of(i * t, t)
            r0 = pl.multiple_of((i // (FOX_REF // t)) * FOX_REF, FOX_REF)
            sts, dpts, qms, doms = [], [], [], []
            for h in range(nh):
                p, a = divmod(h, 2)
                qi = q_ref[pl.ds(q0, t), p * LANES:(p + 1) * LANES] * jnp.asarray(QK_SCALE, BF16)
                doi = do_ref[pl.ds(q0, t), p * LANES:(p + 1) * LANES]
                qm = jnp.where(sels[a], qi, jnp.zeros_like(qi))
                dom = jnp.where(sels[a], doi, jnp.zeros_like(doi))
                qms.append(qm)
                doms.append(dom)
                sts.append(_dot_nt(k2s[p], qm))
                dpts.append(_dot_nt(v2s[p], dom))
            pts, dsts = [], []
            for h in range(nh):
                cref = ct_ref[h, :, pl.ds(r0, LANES)][:, 0:1]
                pt = jnp.exp(sts[h] - (css[h] - cref) - lse_ref[h, :, pl.ds(q0, t)])
                if masked:
                    pt = jnp.where(causal, pt, 0.0)
                ds32 = pt * (dpts[h] - dl_ref[h, :, pl.ds(q0, t)])
                part = ds32[:, 0:LANES]
                for c in range(1, t // LANES):
                    part = part + ds32[:, c * LANES:(c + 1) * LANES]
                accd_ref[h] = part if masked else accd_ref[h] + part
                dcq_ref[h, :, pl.ds(q0, t)] += jnp.sum(ds32, axis=0, keepdims=True)
                pts.append(pt.astype(BF16))
                dsts.append(ds32.astype(BF16))
            for p in range(npair):
                ha, hb = 2 * p, 2 * p + 1
                dv_p = _dot(pts[ha], doms[ha]) + _dot(pts[hb], doms[hb])
                dk_p = _dot(dsts[ha], qms[ha]) + _dot(dsts[hb], qms[hb])
                accv_ref[p] = dv_p if masked else accv_ref[p] + dv_p
                acck_ref[p] = dk_p if masked else acck_ref[p] + dk_p
            for h in range(nh):
                dqt_ref[h * HEAD_DIM:(h + 1) * HEAD_DIM, pl.ds(q0, t)] += _dot(kts[h], dsts[h])

        tile(kj, True)

        def body(i, c):
            tile(i, False)
            return c
        lax.fori_loop(kj + 1, nq, body, 0)

        dc = jnp.zeros((t, LANES), F32)
        for h in range(nh):
            dc = jnp.where(lane == h, -jnp.sum(accd_ref[h], axis=1, keepdims=True), dc)
        dc_ref[...] = dc
        for p in range(npair):
            dv_ref[:, p * LANES:(p + 1) * LANES] = accv_ref[p].astype(BF16)
            dk_ref[:, p * LANES:(p + 1) * LANES] = acck_ref[p].astype(BF16)

        @pl.when(kj == nq - 1)
        def _():
            for c0 in range(0, s, t):
                dq_ref[c0:c0 + t, :] = dqt_ref[:, c0:c0 + t].T.astype(BF16)

    whole = lambda kj: (0, 0, 0)
    return pl.pallas_call(
        kern, name="fox_bwd",
        grid=(nq,),
        in_specs=[_resident((s, FOX_W), lambda kj: (0, 0)),
                  _resident((s, FOX_W), lambda kj: (0, 0)),
                  pl.BlockSpec((t, FOX_W), lambda kj: (kj, COL_FK // FOX_W)),
                  pl.BlockSpec((t, FOX_W), lambda kj: (kj, COL_FV // FOX_W)),
                  _resident((nh, 1, s), whole),
                  pl.BlockSpec((t, LANES), lambda kj: (kj, 0)),
                  _resident((nh, 1, s), whole),
                  _resident((nh, 1, s), whole)],
        out_specs=[_resident((s, FOX_W), lambda kj: (0, 0)),
                   pl.BlockSpec((t, FOX_W), lambda kj: (kj, 0)),
                   pl.BlockSpec((t, FOX_W), lambda kj: (kj, 0)),
                   pl.BlockSpec((t, LANES), lambda kj: (kj, 0)),
                   _resident((nh, 1, s), whole)],
        out_shape=[jax.ShapeDtypeStruct((s, FOX_W), BF16),
                   jax.ShapeDtypeStruct((s, FOX_W), BF16),
                   jax.ShapeDtypeStruct((s, FOX_W), BF16),
                   jax.ShapeDtypeStruct((s, LANES), F32),
                   jax.ShapeDtypeStruct((nh, 1, s), F32)],
        scratch_shapes=[pltpu.VMEM((FOX_W, s), F32),
                        pltpu.VMEM((npair, t, LANES), F32),
                        pltpu.VMEM((npair, t, LANES), F32),
                        pltpu.VMEM((nh, t, LANES), F32)],
        compiler_params=_cparams(("arbitrary",)),
    )(qkv, do_bf, qkv, qkv, cum_t3, cum, lse_t3, delta_t3)


def _bucket_table():
    qi = np.arange(BLOCK)[:, None]
    kj = np.arange(2 * BLOCK)[None, :]
    rel = np.maximum(qi + BLOCK - kj, 0).astype(np.int32)
    max_exact = NUM_BUCKETS // 2
    relf = np.maximum(rel, 1).astype(np.float32)
    large = max_exact + (np.log(relf / np.float32(max_exact)) / np.float32(math.log(MAX_DISTANCE / max_exact))
                         * np.float32(NUM_BUCKETS - max_exact)).astype(np.int32)
    large = np.minimum(large, NUM_BUCKETS - 1)
    return np.where(rel < max_exact, rel, large).astype(np.int32)


SWA_LANES = SWA_GROUP * BLOCK


def _swa_bias(rel_bias, bucket_t):
    def kern(rb_ref, bk_ref, o_ref):
        bk = bk_ref[...]
        kj = lax.broadcasted_iota(jnp.int32, (2 * BLOCK, BLOCK), 0)
        qi = lax.broadcasted_iota(jnp.int32, (2 * BLOCK, BLOCK), 1)
        rel = qi + BLOCK - kj
        band = (rel >= 0) & (rel < BLOCK)
        masks = [band & (kj >= BLOCK), band]
        for h in range(SWA_HEADS):
            g, hh = divmod(h, SWA_GROUP)
            acc = jnp.zeros((2 * BLOCK, BLOCK), F32)
            for b in range(NUM_BUCKETS):
                acc = jnp.where(bk == b, rb_ref[b, h], acc)
            for first in range(2):
                o_ref[first, g, :, hh * BLOCK:(hh + 1) * BLOCK] = jnp.where(masks[first], acc, NEG)

    return pl.pallas_call(
        kern, name="swa_bias",
        in_specs=[pl.BlockSpec(memory_space=pltpu.SMEM),
                  pl.BlockSpec(memory_space=pltpu.VMEM)],
        out_specs=pl.BlockSpec(memory_space=pltpu.VMEM),
        out_shape=jax.ShapeDtypeStruct((2, SWA_KV_HEADS, 2 * BLOCK, SWA_LANES), F32),
        compiler_params=_cparams(),
    )(rel_bias, bucket_t)


SWA_STEP = 8


def _swa_keys(prev_ref, cur_ref):
    return jnp.concatenate([prev_ref[...], cur_ref[...]], axis=0)


def _swa_queries(x_ref, scale):
    x = x_ref[...]
    if scale:
        x = x * jnp.asarray(QK_SCALE, BF16)
    xt = x.astype(F32).T.astype(BF16)
    return [_group_rows(xt[:, b * BLOCK:(b + 1) * BLOCK]) for b in range(SWA_STEP)]


def _group_rows(xt):
    zeros = jnp.zeros((HEAD_DIM, SWA_LANES), BF16)
    out = []
    for g in range(SWA_KV_HEADS):
        heads = [xt[(SWA_GROUP * g + hh) * HEAD_DIM:(SWA_GROUP * g + hh + 1) * HEAD_DIM, :] for hh in range(SWA_GROUP)]
        rows = jnp.concatenate(heads, axis=1)
        padded = jnp.concatenate([rows, zeros] if g == 0 else [zeros, rows], axis=0)
        out.append((rows, padded))
    return out


def _pairs_to_rows(cols_t):
    out = []
    for p in range(SWA_HEADS // 2):
        g, hh = divmod(2 * p, SWA_GROUP)
        pair = jnp.concatenate([cols_t[g][:, hh * BLOCK:(hh + 1) * BLOCK],
                                cols_t[g][:, (hh + 1) * BLOCK:(hh + 2) * BLOCK]], axis=0)
        out.append(pair.T)
    return jnp.concatenate(out, axis=1)


def _swa_fwd(qkv, bias_t, sink_rows):
    s = qkv.shape[0]
    nb = s // BLOCK
    rows = SWA_STEP * BLOCK
    units = [(b, g) for b in range(SWA_STEP) for g in range(SWA_KV_HEADS)]

    def kern(q_ref, kp_ref, kc_ref, vp_ref, vc_ref, bias_ref, sink_ref, o_ref, lse_ref):
        n = pl.program_id(0)
        tables = [jnp.minimum(n, 1)] + [1] * (SWA_STEP - 1)
        k3 = _swa_keys(kp_ref, kc_ref)
        vt3 = _swa_keys(vp_ref, vc_ref).astype(F32).T.astype(BF16)
        qts = _swa_queries(q_ref, True)
        us = [_dot(k3[b * BLOCK:(b + 2) * BLOCK], qts[b][g][1]) + bias_ref[tables[b], g] for b, g in units]
        outs = []
        for (b, g), u in zip(units, us):
            sk = sink_ref[g]
            m = jnp.maximum(jnp.max(u, axis=0, keepdims=True), sk)
            p = jnp.exp(u - m)
            l = jnp.sum(p, axis=0, keepdims=True) + jnp.exp(sk - m)
            lse_ref[b, g] = m + jnp.log(l)
            vt = vt3[g * HEAD_DIM:(g + 1) * HEAD_DIM, b * BLOCK:(b + 2) * BLOCK]
            outs.append(_dot(vt, (p * (1.0 / l)).astype(BF16)))
        for b in range(SWA_STEP):
            o_ref[b * BLOCK:(b + 1) * BLOCK, :] = _pairs_to_rows(outs[b * SWA_KV_HEADS:(b + 1) * SWA_KV_HEADS])

    cq, ck, cv = COL_SQ // SWA_W, COL_SK // LANES, COL_SV // LANES
    prev = lambda n: jnp.maximum(SWA_STEP * n - 1, 0)
    return pl.pallas_call(
        kern, name="swa_fwd",
        grid=(nb // SWA_STEP,),
        in_specs=[pl.BlockSpec((rows, SWA_W), lambda n: (n, cq)),
                  pl.BlockSpec((BLOCK, LANES), lambda n: (prev(n), ck)),
                  pl.BlockSpec((rows, LANES), lambda n: (n, ck)),
                  pl.BlockSpec((BLOCK, LANES), lambda n: (prev(n), cv)),
                  pl.BlockSpec((rows, LANES), lambda n: (n, cv)),
                  _resident((2, SWA_KV_HEADS, 2 * BLOCK, SWA_LANES), lambda n: (0, 0, 0, 0)),
                  _resident((SWA_KV_HEADS, 1, SWA_LANES), lambda n: (0, 0, 0))],
        out_specs=[pl.BlockSpec((rows, SWA_W), lambda n: (n, 0)),
                   pl.BlockSpec((SWA_STEP, SWA_KV_HEADS, 1, SWA_LANES), lambda n: (n, 0, 0, 0))],
        out_shape=[jax.ShapeDtypeStruct((s, SWA_W), F32),
                   jax.ShapeDtypeStruct((nb, SWA_KV_HEADS, 1, SWA_LANES), F32)],
        compiler_params=_cparams(("parallel",)),
    )(qkv, qkv, qkv, qkv, qkv, bias_t, sink_rows)


def _swa_bwd(qkv, do_bf, delta_rows, lse, bias_t, sink_rows, bucket_t):
    s = qkv.shape[0]
    nb = s // BLOCK
    steps = nb // SWA_STEP
    rows = SWA_STEP * BLOCK
    units = [(b, g) for b in range(SWA_STEP) for g in range(SWA_KV_HEADS)]

    def kern(q_ref, kp_ref, kc_ref, vp_ref, vc_ref, do_ref, dl_ref, lse_ref, bias_ref, sink_ref, bk_ref,
             dq_ref, dk_ref, dv_ref, grb_ref, gsk_ref, dbias_ref, ck_ref, cv_ref, sk_ref):
        n = pl.program_id(0)

        @pl.when(n == 0)
        def _():
            dbias_ref[...] = jnp.zeros_like(dbias_ref)
            ck_ref[...] = jnp.zeros_like(ck_ref)
            cv_ref[...] = jnp.zeros_like(cv_ref)
            sk_ref[...] = jnp.zeros_like(sk_ref)

        @pl.when(n < steps)
        def _():
            tables = [jnp.minimum(n, 1)] + [1] * (SWA_STEP - 1)
            k3 = _swa_keys(kp_ref, kc_ref)
            v3 = _swa_keys(vp_ref, vc_ref)
            kt3 = (k3.astype(F32).T * QK_SCALE).astype(BF16)
            qts = _swa_queries(q_ref, True)
            dots = _swa_queries(do_ref, False)
            sts = [_dot(k3[b * BLOCK:(b + 2) * BLOCK], qts[b][g][1]) for b, g in units]
            dps = [_dot(v3[b * BLOCK:(b + 2) * BLOCK], dots[b][g][1]) for b, g in units]
            ps, dss = [], []
            for i, (b, g) in enumerate(units):
                lse_g = lse_ref[b, g]
                dlt = dl_ref[b, g]
                p = jnp.exp(sts[i] + bias_ref[tables[b], g] - lse_g)
                ds = p * (dps[i] - dlt)
                dbias_ref[g] += ds
                sk_ref[g] += -jnp.exp(sink_ref[g] - lse_g) * dlt
                ps.append(p.astype(BF16))
                dss.append(ds.astype(BF16))
            dk2, dv2 = [], []
            for b in range(SWA_STEP):
                at = lambda g: b * SWA_KV_HEADS + g
                groups = range(SWA_KV_HEADS)
                dv2.append(jnp.concatenate([_dot_nt(dots[b][g][0], ps[at(g)]) for g in groups], axis=0).T)
                dk2.append(jnp.concatenate([_dot_nt(qts[b][g][0], dss[at(g)]) for g in groups], axis=0).T)
                dqts = [_dot(kt3[g * HEAD_DIM:(g + 1) * HEAD_DIM, b * BLOCK:(b + 2) * BLOCK], dss[at(g)]) for g in groups]
                dq_ref[b * BLOCK:(b + 1) * BLOCK, :] = _pairs_to_rows(dqts).astype(BF16)
            last = (SWA_STEP - 1) * BLOCK
            for acc_ref, out_ref, parts in ((ck_ref, dk_ref, dk2), (cv_ref, dv_ref, dv2)):
                done = acc_ref[last:] + parts[0][:BLOCK]
                out_ref[...] = jnp.concatenate([acc_ref[:last], done], axis=0).astype(BF16)
                for b in range(SWA_STEP - 1):
                    acc_ref[b * BLOCK:(b + 1) * BLOCK] = parts[b][BLOCK:] + parts[b + 1][:BLOCK]
                acc_ref[last:] = parts[SWA_STEP - 1][BLOCK:]

        @pl.when(n == steps)
        def _():
            dk_ref[...] = ck_ref[...].astype(BF16)
            dv_ref[...] = cv_ref[...].astype(BF16)
            bk = bk_ref[...]
            lane = lax.broadcasted_iota(jnp.int32, (8, LANES), 1)
            rowi = lax.broadcasted_iota(jnp.int32, (NUM_BUCKETS, LANES), 0)
            lanei = lax.broadcasted_iota(jnp.int32, (NUM_BUCKETS, LANES), 1)
            out = jnp.zeros((NUM_BUCKETS, LANES), F32)
            gsk = jnp.zeros((8, LANES), F32)
            for h in range(SWA_HEADS):
                g, hh = divmod(h, SWA_GROUP)
                cols = slice(hh * BLOCK, (hh + 1) * BLOCK)
                gsk = jnp.where(lane == h, jnp.sum(sk_ref[g][:, cols]), gsk)
                db = dbias_ref[g][:, cols]
                for b in range(NUM_BUCKETS):
                    val = jnp.sum(jnp.where(bk == b, db, 0.0))
                    out = jnp.where((rowi == b) & (lanei == h), val, out)
            grb_ref[...] = out
            gsk_ref[...] = gsk

    cq, ck, cv = COL_SQ // SWA_W, COL_SK // LANES, COL_SV // LANES
    cur = lambda n: jnp.minimum(n, steps - 1)
    prev = lambda n: jnp.maximum(SWA_STEP * cur(n) - 1, 0)
    kout = lambda n: jnp.maximum(n - 1, 0)
    stat = pl.BlockSpec((SWA_STEP, SWA_KV_HEADS, 1, SWA_LANES), lambda n: (cur(n), 0, 0, 0))
    return pl.pallas_call(
        kern, name="swa_bwd",
        grid=(steps + 1,),
        in_specs=[pl.BlockSpec((rows, SWA_W), lambda n: (cur(n), cq)),
                  pl.BlockSpec((BLOCK, LANES), lambda n: (prev(n), ck)),
                  pl.BlockSpec((rows, LANES), lambda n: (cur(n), ck)),
                  pl.BlockSpec((BLOCK, LANES), lambda n: (prev(n), cv)),
                  pl.BlockSpec((rows, LANES), lambda n: (cur(n), cv)),
                  pl.BlockSpec((rows, SWA_W), lambda n: (cur(n), 1)),
                  stat, stat,
                  _resident((2, SWA_KV_HEADS, 2 * BLOCK, SWA_LANES), lambda n: (0, 0, 0, 0)),
                  _resident((SWA_KV_HEADS, 1, SWA_LANES), lambda n: (0, 0, 0)),
                  _resident((2 * BLOCK, BLOCK), lambda n: (0, 0))],
        out_specs=[pl.BlockSpec((rows, SWA_W), lambda n: (cur(n), 0)),
                   pl.BlockSpec((rows, LANES), lambda n: (kout(n), 0)),
                   pl.BlockSpec((rows, LANES), lambda n: (kout(n), 0)),
                   pl.BlockSpec((NUM_BUCKETS, LANES), lambda n: (0, 0)),
                   pl.BlockSpec((8, LANES), lambda n: (0, 0))],
        out_shape=[jax.ShapeDtypeStruct((s, SWA_W), BF16),
                   jax.ShapeDtypeStruct((s, LANES), BF16),
                   jax.ShapeDtypeStruct((s, LANES), BF16),
                   jax.ShapeDtypeStruct((NUM_BUCKETS, LANES), F32),
                   jax.ShapeDtypeStruct((8, LANES), F32)],
        scratch_shapes=[pltpu.VMEM((SWA_KV_HEADS, 2 * BLOCK, SWA_LANES), F32),
                        pltpu.VMEM((rows, LANES), F32),
                        pltpu.VMEM((rows, LANES), F32),
                        pltpu.VMEM((SWA_KV_HEADS, 1, SWA_LANES), F32)],
        compiler_params=_cparams(("arbitrary",)),
    )(qkv, qkv, qkv, qkv, qkv, do_bf, delta_rows, lse, bias_t, sink_rows, bucket_t)


def _post(x, target, o_fox, o_swa, z, w_o, ln_g, ln_b):
    s = x.shape[0]
    tm = min(256, s)
    nt = s // tm

    def kern(x_ref, t_ref, of_ref, os_ref, z_ref, w_ref, g_ref, b_ref,
             loss_ref, dh_ref, gwo_ref, do_ref, dz_ref, dl_ref, gg_ref, gb_ref, lacc_ref):
        step = pl.program_id(0)

        @pl.when(step == 0)
        def _():
            lacc_ref[...] = jnp.zeros_like(lacc_ref)
            gg_ref[...] = jnp.zeros_like(gg_ref)
            gwo_ref[...] = jnp.zeros_like(gwo_ref)
            gb_ref[...] = jnp.zeros_like(gb_ref)

        o = jnp.concatenate([of_ref[...], os_ref[...]], axis=1)
        zz = z_ref[...]
        sig = 1.0 / (1.0 + jnp.exp(-zz))
        silu = zz * sig
        mixed32 = o * silu
        mixed = mixed32.astype(BF16)
        w = w_ref[...]
        h = ALPHA * x_ref[...] + _dot(mixed, w)
        mu = jnp.mean(h, axis=1, keepdims=True)
        hc = h - mu
        var = jnp.mean(hc * hc, axis=1, keepdims=True)
        rstd = lax.rsqrt(var + LN_EPS)
        xhat = hc * rstd
        g = g_ref[...]
        err = xhat * g + b_ref[...] - t_ref[...]
        lacc_ref[...] += jnp.broadcast_to(jnp.sum(err * err, axis=0, keepdims=True), lacc_ref.shape)
        dout = err * (1.0 / D_MODEL)
        gg_ref[...] += jnp.broadcast_to(jnp.sum(dout * xhat, axis=0, keepdims=True), gg_ref.shape)
        gb_ref[...] += jnp.broadcast_to(jnp.sum(dout, axis=0, keepdims=True), gb_ref.shape)
        dxh = dout * g
        m1 = jnp.mean(dxh, axis=1, keepdims=True)
        m2 = jnp.mean(dxh * xhat, axis=1, keepdims=True)
        dh = rstd * (dxh - m1 - xhat * m2)
        dh_ref[...] = dh
        dy = dh.astype(BF16)
        gwo_ref[...] += _dot(mixed32.T.astype(BF16), dy)
        dmix = _dot_nt(dy, w)
        do = dmix * silu
        do_ref[...] = do.astype(BF16)
        dz_ref[...] = (dmix * o * (sig * (1.0 + zz * (1.0 - sig)))).astype(BF16)
        r = lax.broadcasted_iota(jnp.int32, (D_MODEL, LANES), 0) // HEAD_DIM
        c = lax.broadcasted_iota(jnp.int32, (D_MODEL, LANES), 1)
        pick = jnp.where(r == c, 1.0, 0.0).astype(BF16)
        dl_ref[...] = _exact_dot(pick, do * o, False)

        @pl.when(step == nt - 1)
        def _():
            tot = jnp.sum(lacc_ref[0:1, :]) * (0.5 / D_MODEL)
            loss_ref[...] = jnp.broadcast_to(tot, loss_ref.shape)

    row = lambda i: (i, 0)
    fixed = lambda i: (0, 0)
    wide = pl.BlockSpec((tm, D_MODEL), row)
    half = pl.BlockSpec((tm, FOX_W), row)
    return pl.pallas_call(
        kern, name="post",
        grid=(nt,),
        in_specs=[wide, wide, half, half, wide,
                  pl.BlockSpec((D_MODEL, D_MODEL), fixed),
                  pl.BlockSpec((1, D_MODEL), fixed),
                  pl.BlockSpec((1, D_MODEL), fixed)],
        out_specs=[pl.BlockSpec((8, LANES), fixed), wide,
                   _resident((D_MODEL, D_MODEL), fixed), wide, wide,
                   pl.BlockSpec((tm, LANES), row),
                   pl.BlockSpec((8, D_MODEL), fixed), pl.BlockSpec((8, D_MODEL), fixed)],
        out_shape=[jax.ShapeDtypeStruct((8, LANES), F32),
                   jax.ShapeDtypeStruct((s, D_MODEL), F32),
                   jax.ShapeDtypeStruct((D_MODEL, D_MODEL), F32),
                   jax.ShapeDtypeStruct((s, D_MODEL), BF16),
                   jax.ShapeDtypeStruct((s, D_MODEL), BF16),
                   jax.ShapeDtypeStruct((s, LANES), F32),
                   jax.ShapeDtypeStruct((8, D_MODEL), F32),
                   jax.ShapeDtypeStruct((8, D_MODEL), F32)],
        scratch_shapes=[pltpu.VMEM((8, D_MODEL), F32)],
        compiler_params=_cparams(("arbitrary",)),
    )(x, target, o_fox, o_swa, z, w_o, ln_g, ln_b)


def _adamw_math(w, g, m, v):
    m = ADAM_B1 * m + (1.0 - ADAM_B1) * g
    v = ADAM_B2 * v + (1.0 - ADAM_B2) * (g * g)
    m_hat = m / (1.0 - ADAM_B1 ** ADAM_STEP)
    v_hat = v / (1.0 - ADAM_B2 ** ADAM_STEP)
    delta = -ADAM_LR * (m_hat / (jnp.sqrt(v_hat) + ADAM_EPS) + ADAM_WD * w)
    return delta, m, v


def _adamw(w, g, m, v, *, name):
    r, c = w.shape
    tr = min(256, r)

    def kern(w_ref, g_ref, m_ref, v_ref, d_ref, mo_ref, vo_ref):
        d, mn, vn = _adamw_math(w_ref[...], g_ref[...], m_ref[...], v_ref[...])
        d_ref[...] = d
        mo_ref[...] = mn
        vo_ref[...] = vn

    blk = pl.BlockSpec((tr, c), lambda i: (i, 0))
    sds = jax.ShapeDtypeStruct((r, c), F32)
    return pl.pallas_call(
        kern, name=name,
        grid=(r // tr,),
        in_specs=[blk, blk, blk, blk],
        out_specs=[blk, blk, blk],
        out_shape=[sds, sds, sds],
        compiler_params=_cparams(("parallel",)),
    )(w, g, m, v)


def _adamw_cols(w, g, m, v, *, name):
    c, _, r = w.shape
    tc = 139
    assert c % tc == 0

    def kern(w_ref, g_ref, m_ref, v_ref, go_ref, d_ref, mo_ref, vo_ref):
        g = g_ref[...]
        d, mn, vn = _adamw_math(w_ref[...], g, m_ref[...], v_ref[...])
        go_ref[...] = g
        d_ref[...] = d
        mo_ref[...] = mn
        vo_ref[...] = vn

    blk = pl.BlockSpec((tc, 1, r), lambda i: (i, 0, 0))
    sds = jax.ShapeDtypeStruct((c, 1, r), F32)
    return pl.pallas_call(
        kern, name=name,
        grid=(c // tc,),
        in_specs=[blk, blk, blk, blk],
        out_specs=[blk, blk, blk, blk],
        out_shape=[sds, sds, sds, sds],
        compiler_params=_cparams(("parallel",)),
    )(w, g, m, v)


def _position():
    x, y, c = lax.axis_index("x"), lax.axis_index("y"), lax.axis_index("c")
    chips = [(1 - x, y), (x, 1 - y), (1 - x, 1 - y)]
    return x, y, c, chips


def _chip_index(cx, cy):
    return 2 * cx + cy


def _gather_weights(*shards):
    n_arr = len(shards)

    def kern(*refs):
        ins, outs = refs[:n_arr], refs[n_arr:2 * n_arr]
        send_sems, recv_sems, local_sems = refs[2 * n_arr:]
        x, y, c, chips = _position()
        me = _chip_index(x, y)
        sibling = (x, y, 1 - c)

        local = [pltpu.make_async_copy(ins[a], outs[a].at[me], local_sems.at[a]) for a in range(n_arr)]
        for cp in local:
            cp.start()

        def half(ref, a):
            rows = shards[a].shape[0] // 2
            return ref.at[pl.ds(c * rows, rows), :]

        def copy(a, k, src, slot, to):
            return pltpu.make_async_remote_copy(
                src_ref=src, dst_ref=half(outs[a].at[slot], a),
                send_sem=send_sems.at[a * 6 + k], recv_sem=recv_sems.at[a * 6 + k],
                device_id=to, device_id_type=MESH)

        first = [copy(a, j, half(ins[a], a), me, (*chip, c)) for a in range(n_arr) for j, chip in enumerate(chips)]
        for cp in first:
            cp.start()
        passed = []
        for a in range(n_arr):
            for j, chip in enumerate(chips):
                slot = _chip_index(*chip)
                copy(a, j, half(ins[a], a), slot, (*chip, c)).wait_recv()
                fwd = copy(a, 3 + j, half(outs[a].at[slot], a), slot, sibling)
                fwd.start()
                passed.append(fwd)
        for a in range(n_arr):
            for j, chip in enumerate(chips):
                slot = _chip_index(*chip)
                rows = shards[a].shape[0] // 2
                dst = outs[a].at[slot].at[pl.ds((1 - c) * rows, rows), :]
                pltpu.make_async_remote_copy(
                    src_ref=dst, dst_ref=dst, send_sem=send_sems.at[a * 6 + 3 + j],
                    recv_sem=recv_sems.at[a * 6 + 3 + j], device_id=sibling, device_id_type=MESH).wait_recv()
        for cp in first + passed:
            cp.wait_send()
        for cp in local:
            cp.wait()

    vmem = pl.BlockSpec(memory_space=pltpu.VMEM)
    return pl.pallas_call(
        kern, name="gather_weights",
        in_specs=[vmem] * n_arr,
        out_specs=[vmem] * n_arr,
        out_shape=[jax.ShapeDtypeStruct((N_CHIPS,) + w.shape, w.dtype) for w in shards],
        scratch_shapes=[pltpu.SemaphoreType.DMA((6 * n_arr,)),
                        pltpu.SemaphoreType.DMA((6 * n_arr,)),
                        pltpu.SemaphoreType.DMA((n_arr,))],
        compiler_params=_cparams(),
    )(*shards)


def _pair_reduce(grads):
    n_arr = len(grads)
    chunk = 128

    def kern(*refs):
        ins = refs[:n_arr]
        outs = refs[n_arr:2 * n_arr]
        gots = refs[2 * n_arr:3 * n_arr]
        send_sems, recv_sems = refs[3 * n_arr:]
        x, y, c, _ = _position()
        sibling = (x, y, 1 - c)
        copies = []
        for a in range(n_arr):
            rows = grads[a].shape[1] // 2
            copies.append(pltpu.make_async_remote_copy(
                src_ref=ins[a].at[:, pl.ds((1 - c) * rows, rows), :], dst_ref=gots[a],
                send_sem=send_sems.at[a], recv_sem=recv_sems.at[a], device_id=sibling, device_id_type=MESH))
        for cp in copies:
            cp.start()
        for a in range(n_arr):
            copies[a].wait()
            rows = grads[a].shape[1] // 2
            for j in range(N_CHIPS):
                for r0 in range(0, rows, chunk):
                    mine = ins[a][j, pl.ds(pl.multiple_of(c * rows + r0, chunk), chunk), :]
                    outs[a][j, r0:r0 + chunk, :] = (mine + gots[a][j, r0:r0 + chunk, :]).astype(BF16)

    vmem = pl.BlockSpec(memory_space=pltpu.VMEM)
    half = [(N_CHIPS, g.shape[1] // 2, g.shape[2]) for g in grads]
    return pl.pallas_call(
        kern, name="pair_reduce",
        in_specs=[vmem] * n_arr,
        out_specs=[vmem] * n_arr,
        out_shape=[jax.ShapeDtypeStruct(h, BF16) for h in half],
        scratch_shapes=[pltpu.VMEM(h, F32) for h in half]
        + [pltpu.SemaphoreType.DMA((n_arr,)), pltpu.SemaphoreType.DMA((n_arr,))],
        compiler_params=_cparams(),
    )(*grads)


def _wo_gather_start(shard):
    hbm = pl.BlockSpec(memory_space=pltpu.HBM)
    sem = pl.BlockSpec(memory_space=pltpu.SEMAPHORE)
    land_shape = (N_CHIPS,) + shard.shape

    def kern(src_ref, land_ref, send_sems, recv_sems, src_thru, land_thru, token):
        x, y, c, chips = _position()
        me = _chip_index(x, y)
        for j, chip in enumerate(chips):
            pltpu.make_async_remote_copy(
                src_ref=src_ref, dst_ref=land_ref.at[me], send_sem=send_sems.at[j], recv_sem=recv_sems.at[j],
                device_id=(*chip, c), device_id_type=MESH).start()
        token[...] = jnp.zeros_like(token)

    return pl.pallas_call(
        kern, name="wo_gather_start",
        in_specs=[hbm, hbm],
        out_specs=(sem, sem, hbm, hbm, pl.BlockSpec(memory_space=pltpu.VMEM)),
        out_shape=(pltpu.SemaphoreType.DMA((3,)), pltpu.SemaphoreType.DMA((3,)),
                   pltpu.HBM(shard.shape, shard.dtype), pltpu.HBM(land_shape, shard.dtype),
                   jax.ShapeDtypeStruct((8, LANES), F32)),
        input_output_aliases={0: 2, 1: 3},
        compiler_params=pltpu.CompilerParams(has_side_effects=pltpu.SideEffectType.DATAFLOW_SIDE_EFFECTING),
    )(pltpu.with_memory_space_constraint(shard, pltpu.HBM),
      pltpu.with_memory_space_constraint(lax.empty(land_shape, shard.dtype), pltpu.HBM))


def _wo_gather_wait(send_sems, recv_sems, src_thru, land_thru, after):
    hbm = pl.BlockSpec(memory_space=pltpu.HBM)
    sem = pl.BlockSpec(memory_space=pltpu.SEMAPHORE)

    def kern(src_ref, land_ref, send_sems, recv_sems, after_ref, src_out, land_out):
        x, y, c, chips = _position()
        for j, chip in enumerate(chips):
            copy = pltpu.make_async_remote_copy(
                src_ref=src_ref, dst_ref=land_ref.at[_chip_index(*chip)], send_sem=send_sems.at[j],
                recv_sem=recv_sems.at[j], device_id=(*chip, c), device_id_type=MESH)
            copy.wait_send()
            copy.wait_recv()

    return pl.pallas_call(
        kern, name="wo_gather_wait",
        in_specs=[hbm, hbm, sem, sem, pl.BlockSpec(memory_space=pl.ANY)],
        out_specs=[hbm, hbm],
        out_shape=[pltpu.HBM(src_thru.shape, src_thru.dtype), pltpu.HBM(land_thru.shape, land_thru.dtype)],
        input_output_aliases={0: 0, 1: 1},
        compiler_params=pltpu.CompilerParams(has_side_effects=pltpu.SideEffectType.DATAFLOW_SIDE_EFFECTING),
    )(src_thru, land_thru, send_sems, recv_sems, after)[1]


def _scatter_start(parts):
    n_arr = len(parts)
    hbm = pl.BlockSpec(memory_space=pltpu.HBM)
    sem = pl.BlockSpec(memory_space=pltpu.SEMAPHORE)

    def kern(*refs):
        ins, lands = refs[:n_arr], refs[n_arr:2 * n_arr]
        send_sems, recv_sems, token = refs[2 * n_arr], refs[2 * n_arr + 1], refs[-1]
        x, y, c, chips = _position()
        me = _chip_index(x, y)
        for a in range(n_arr):
            for j, chip in enumerate(chips):
                pltpu.make_async_remote_copy(
                    src_ref=ins[a].at[_chip_index(*chip)], dst_ref=lands[a].at[me],
                    send_sem=send_sems.at[a * 3 + j], recv_sem=recv_sems.at[a * 3 + j],
                    device_id=(*chip, c), device_id_type=MESH).start()
        token[...] = jnp.zeros_like(token)

    slab = [pltpu.HBM(p.shape, p.dtype) for p in parts]
    outs = pl.pallas_call(
        kern, name="scatter_start",
        in_specs=[hbm] * (2 * n_arr),
        out_specs=(sem, sem, *[hbm] * (2 * n_arr), pl.BlockSpec(memory_space=pltpu.VMEM)),
        out_shape=(pltpu.SemaphoreType.DMA((3 * n_arr,)), pltpu.SemaphoreType.DMA((3 * n_arr,)),
                   *slab, *slab, jax.ShapeDtypeStruct((8, LANES), F32)),
        input_output_aliases={i: 2 + i for i in range(2 * n_arr)},
        compiler_params=pltpu.CompilerParams(has_side_effects=pltpu.SideEffectType.DATAFLOW_SIDE_EFFECTING),
    )(*[pltpu.with_memory_space_constraint(p, pltpu.HBM) for p in parts],
      *[pltpu.with_memory_space_constraint(lax.empty(p.shape, p.dtype), pltpu.HBM) for p in parts])
    return outs[0], outs[1], outs[2:2 + n_arr], outs[2 + n_arr:2 + 2 * n_arr], outs[-1]


def _scatter_wait(send_sems, recv_sems, parts_thru, lands_thru, after):
    n_arr = len(parts_thru)
    hbm = pl.BlockSpec(memory_space=pltpu.HBM)
    sem = pl.BlockSpec(memory_space=pltpu.SEMAPHORE)

    def kern(*refs):
        ins, lands = refs[:n_arr], refs[n_arr:2 * n_arr]
        send_ref, recv_ref = refs[2 * n_arr], refs[2 * n_arr + 1]
        x, y, c, chips = _position()
        for a in range(n_arr):
            for j, chip in enumerate(chips):
                slot = _chip_index(*chip)
                copy = pltpu.make_async_remote_copy(
                    src_ref=ins[a].at[slot], dst_ref=lands[a].at[slot],
                    send_sem=send_ref.at[a * 3 + j], recv_sem=recv_ref.at[a * 3 + j],
                    device_id=(*chip, c), device_id_type=MESH)
                copy.wait_send()
                copy.wait_recv()

    slab = [pltpu.HBM(p.shape, p.dtype) for p in parts_thru]
    outs = pl.pallas_call(
        kern, name="scatter_wait",
        in_specs=[hbm] * (2 * n_arr) + [sem, sem, pl.BlockSpec(memory_space=pl.ANY)],
        out_specs=[hbm] * (2 * n_arr),
        out_shape=slab + slab,
        input_output_aliases={i: i for i in range(2 * n_arr)},
        compiler_params=pltpu.CompilerParams(has_side_effects=pltpu.SideEffectType.DATAFLOW_SIDE_EFFECTING),
    )(*parts_thru, *lands_thru, send_sems, recv_sems, after)
    return outs[:n_arr], outs[n_arr:]


def _chip_sum_swap(parts, lands):
    n_arr = len(parts)
    chunk = 128

    def kern(*refs):
        own, got = refs[:n_arr], refs[n_arr:2 * n_arr]
        outs = refs[2 * n_arr:3 * n_arr]
        sums = refs[3 * n_arr:4 * n_arr]
        swap_send, swap_recv, swap_local = refs[4 * n_arr:]
        x, y, c, _ = _position()
        me = _chip_index(x, y)
        sibling = (x, y, 1 - c)
        for a in range(n_arr):
            for r0 in range(0, parts[a].shape[1], chunk):
                mine = own[a][me, r0:r0 + chunk, :].astype(F32)

                def term(i):
                    other = got[a][jnp.where(i == me, (i + 1) % N_CHIPS, i), r0:r0 + chunk, :].astype(F32)
                    return jnp.where(i == me, mine, other)
                sums[a][r0:r0 + chunk, :] = ((term(0) + term(1)) + term(2)) + term(3)
        swap_l, swap_r = [], []
        for a in range(n_arr):
            rows = parts[a].shape[1]
            mine = outs[a].at[pl.ds(c * rows, rows), :]
            swap_l.append(pltpu.make_async_copy(sums[a], mine, swap_local.at[a]))
            swap_r.append(pltpu.make_async_remote_copy(
                src_ref=sums[a], dst_ref=mine, send_sem=swap_send.at[a], recv_sem=swap_recv.at[a],
                device_id=sibling, device_id_type=MESH))
        for cp in swap_l + swap_r:
            cp.start()
        for a in range(n_arr):
            rows = parts[a].shape[1]
            theirs = outs[a].at[pl.ds((1 - c) * rows, rows), :]
            pltpu.make_async_remote_copy(
                src_ref=theirs, dst_ref=theirs, send_sem=swap_send.at[a], recv_sem=swap_recv.at[a],
                device_id=sibling, device_id_type=MESH).wait_recv()
        for cp in swap_r:
            cp.wait_send()
        for cp in swap_l:
            cp.wait()

    vmem = pl.BlockSpec(memory_space=pltpu.VMEM)
    return pl.pallas_call(
        kern, name="chip_sum_swap",
        in_specs=[vmem] * (2 * n_arr),
        out_specs=[vmem] * n_arr,
        out_shape=[jax.ShapeDtypeStruct((2 * p.shape[1], p.shape[2]), F32) for p in parts],
        scratch_shapes=[pltpu.VMEM(p.shape[1:], F32) for p in parts]
        + [pltpu.SemaphoreType.DMA((n_arr,)),
           pltpu.SemaphoreType.DMA((n_arr,)),
           pltpu.SemaphoreType.DMA((n_arr,))],
        compiler_params=_cparams(),
    )(*parts, *lands)


def _small_allreduce_adamw(partials, params, moms, vels):
    chunks = D_MODEL // LANES
    row_rb, row_bf, row_sk, row_loss = 2 * chunks, 2 * chunks + NUM_BUCKETS, 2 * chunks + NUM_BUCKETS + 1, SMALL_ROWS - 6

    def kern(gbf_ref, grb_ref, gsk_ref, gg_ref, gb_ref, loss_ref, *refs):
        p_refs, m_refs, v_refs = refs[0:5], refs[5:10], refs[10:15]
        lo_ref, g_outs, d_outs, mo_outs, vo_outs = refs[15], refs[16:21], refs[21:26], refs[26:31], refs[31:36]
        send_ref, buf_ref, send_sems, recv_sems = refs[36:]
        x, y, c, _ = _position()
        me = 4 * x + 2 * y + c
        send_ref[...] = jnp.zeros_like(send_ref)
        for r in range(chunks):
            send_ref[r:r + 1, :] = gg_ref[0:1, r * LANES:(r + 1) * LANES]
            send_ref[chunks + r:chunks + r + 1, :] = gb_ref[0:1, r * LANES:(r + 1) * LANES]
        send_ref[row_rb:row_rb + NUM_BUCKETS, :] = grb_ref[...]
        send_ref[row_bf:row_bf + 1, :] = gbf_ref[0:1, :]
        send_ref[row_sk:row_sk + 1, :] = gsk_ref[0:1, :]
        send_ref[row_loss:row_loss + 1, :] = loss_ref[0:1, :]
        buf_ref[me] = send_ref[...]
        peers = [(x, y, 1 - c)] + [(px, py, pc) for px, py in _position()[3] for pc in (c, 1 - c)]
        sends = []
        for k, peer in enumerate(peers):
            sends.append(pltpu.make_async_remote_copy(
                src_ref=send_ref, dst_ref=buf_ref.at[me], send_sem=send_sems.at[k], recv_sem=recv_sems.at[k],
                device_id=peer, device_id_type=MESH))
        for cp in sends:
            cp.start()
        for k, (px, py, pc) in enumerate(peers):
            slot = buf_ref.at[4 * px + 2 * py + pc]
            pltpu.make_async_remote_copy(
                src_ref=slot, dst_ref=slot, send_sem=send_sems.at[k], recv_sem=recv_sems.at[k],
                device_id=(px, py, pc), device_id_type=MESH).wait_recv()
        for cp in sends:
            cp.wait_send()
        tot = buf_ref[0]
        for d in range(1, N_DEV):
            tot = tot + buf_ref[d]
        lo_ref[...] = tot[row_loss:row_loss + 1, :]
        grads = [tot[row_bf:row_bf + 1, 0:FOX_HEADS],
                 tot[row_rb:row_rb + NUM_BUCKETS, 0:SWA_HEADS],
                 tot[row_sk:row_sk + 1, 0:SWA_HEADS],
                 jnp.concatenate([tot[r:r + 1, :] for r in range(chunks)], axis=1),
                 jnp.concatenate([tot[chunks + r:chunks + r + 1, :] for r in range(chunks)], axis=1)]
        for i, g in enumerate(grads):
            g_outs[i][...] = g
            delta, mn, vn = _adamw_math(p_refs[i][...], g, m_refs[i][...], v_refs[i][...])
            d_outs[i][...] = delta
            mo_outs[i][...] = mn
            vo_outs[i][...] = vn

    vm = pl.BlockSpec(memory_space=pltpu.VMEM)
    shapes = [jax.ShapeDtypeStruct(p.shape, F32) for p in params]
    outs = pl.pallas_call(
        kern, name="small_allreduce_adamw",
        in_specs=[vm] * 21,
        out_specs=[vm] * 21,
        out_shape=[jax.ShapeDtypeStruct((1, LANES), F32)] + shapes * 4,
        scratch_shapes=[pltpu.VMEM((SMALL_ROWS, LANES), F32),
                        pltpu.VMEM((N_DEV, SMALL_ROWS, LANES), F32),
                        pltpu.SemaphoreType.DMA((N_DEV - 1,)),
                        pltpu.SemaphoreType.DMA((N_DEV - 1,))],
    )(*partials, *params, *moms, *vels)
    return outs[0], outs[1:6], outs[6:11], outs[11:16], outs[16:21]


def _padded_from_shards(shards):
    shard_cols = D_IN // N_CHIPS
    pieces, at = [], 0
    for ref0, pad0, width in sorted(_COLUMN_RUNS, key=lambda run: run[1]):
        if pad0 > at:
            pieces.append(jnp.zeros((shards.shape[1], pad0 - at), shards.dtype))
        for j in range(N_CHIPS):
            a, b = max(ref0, j * shard_cols), min(ref0 + width, (j + 1) * shard_cols)
            if a < b:
                pieces.append(shards[j, :, a - j * shard_cols:b - j * shard_cols])
        at = pad0 + width
    return jnp.concatenate(pieces, axis=1)


_COLUMN_RUNS = ((0, 0, 1536), (1536, OFF_C, FOX_HEADS), (1544, OFF_B, FOX_W), (2056, 1536, N_A - 1536),
                (2824, OFF_B + FOX_W, SWA_W))


def _shards_from_padded(g):
    shard_cols = D_IN // N_CHIPS
    shards = []
    for j in range(N_CHIPS):
        lo, hi = j * shard_cols, (j + 1) * shard_cols
        pieces = []
        for ref0, pad0, width in _COLUMN_RUNS:
            a, b = max(lo, ref0), min(hi, ref0 + width)
            if a < b:
                pieces.append(g[:, pad0 + a - ref0:pad0 + b - ref0])
        pieces.append(jnp.zeros((g.shape[0], SHARD_PAD - shard_cols), g.dtype))
        shards.append(jnp.concatenate(pieces, axis=1))
    return jnp.stack(shards)


def _fox_rows(a):
    return a[:, :FOX_HEADS].T.reshape(FOX_HEADS, 1, a.shape[0])


def kernel(x, w_in, b_f, rel_bias, sink, w_o, ln_g, ln_b, loss_target, m_w_in, m_b_f, m_rel_bias, m_sink, m_w_o, m_ln_g, m_ln_b, v_w_in, v_b_f, v_rel_bias, v_sink, v_w_o, v_ln_g, v_ln_b):
    x2 = x[0]
    tgt = loss_target[0]
    s = x2.shape[0]
    w_in2, w_o2 = w_in[0], w_o[0]

    shard_cols = D_IN // N_CHIPS
    col_pad = ((0, 0), (0, SHARD_PAD - shard_cols))
    (w_in_all,) = _gather_weights(jnp.pad(w_in2.astype(BF16), col_pad))
    w_pad = _padded_from_shards(w_in_all)
    w_o_bf, _ = lax.optimization_barrier((w_o2.astype(BF16), w_in_all))
    wo_send, wo_recv, wo_src, wo_land, wo_token = _wo_gather_start(w_o_bf)

    qkv, ffp, z, xt, vt = _project(x2, w_pad, wo_token)
    bfp = jnp.pad(b_f, ((0, 0), (0, LANES - FOX_HEADS)))
    cum = _cum_fwd(ffp, bfp)
    cum_t3 = _fox_rows(cum)
    o_fox, lse_t3 = _fox_fwd(qkv, vt, cum_t3, cum)
    bucket_t = jnp.asarray(_bucket_table().T)
    bias_t = _swa_bias(rel_bias, bucket_t)
    sink_rows = jnp.repeat(sink.reshape(SWA_KV_HEADS, SWA_GROUP, 1), BLOCK, axis=2).reshape(SWA_KV_HEADS, 1, SWA_LANES)
    o_swa, lse_swa = _swa_fwd(qkv, bias_t, sink_rows)

    wo_land = _wo_gather_wait(wo_send, wo_recv, wo_src, wo_land, o_swa)
    my_chip = _chip_index(lax.axis_index("x"), lax.axis_index("y"))
    w_o_full = lax.dynamic_update_slice(wo_land, w_o_bf[None], (my_chip, 0, 0)).reshape(D_MODEL, D_MODEL)
    loss8, dh, grad_w_o_full, do_bf, dz, delta, gg8, gb8 = _post(
        x2, tgt, o_fox, o_swa, z, w_o_full, ln_g, ln_b)

    delta_t3 = _fox_rows(delta)
    dq_fox, dk_fox, dv_fox, dcum_k, dcum_q = _fox_bwd(qkv, do_bf, cum_t3, cum, lse_t3, delta_t3)
    dcum_q = jnp.pad(dcum_q.reshape(FOX_HEADS, s).T, ((0, 0), (0, LANES - FOX_HEADS)))
    dff, gbf8 = _cum_bwd(dcum_k, dcum_q, ffp, bfp)
    delta_rows = (delta[:, FOX_HEADS:FOX_HEADS + SWA_HEADS].reshape(s // BLOCK, BLOCK, SWA_KV_HEADS, SWA_GROUP)
                  .transpose(0, 2, 3, 1).reshape(s // BLOCK, SWA_KV_HEADS, 1, SWA_LANES))
    dq_swa, dk_swa, dv_swa, grb, gsk8 = _swa_bwd(qkv, do_bf, delta_rows, lse_swa, bias_t, sink_rows, bucket_t)

    d_misc = jnp.concatenate([dk_swa, dv_swa, dff], axis=1)
    pieces = [dq_fox, dk_fox, dv_fox, dq_swa, d_misc, dz]
    blocks = [(p, 0) for p in pieces[:-1]] + [(dz, 0), (dz, 1)]
    grad_w_pad = _grad_w_matmul(xt, blocks, tk=1024, name="grad_w_in")

    g_in4 = _shards_from_padded(grad_w_pad)
    g_o4 = grad_w_o_full.reshape(N_CHIPS, D_MODEL // N_CHIPS, D_MODEL)
    parts = _pair_reduce([g_in4, g_o4])
    send_sems, recv_sems, parts_thru, lands_thru, token = _scatter_start(parts)
    grad_x = _grad_x_matmul(pieces, w_pad, dh, token, tm=512, tn=D_MODEL, name="grad_x")
    parts, lands = _scatter_wait(send_sems, recv_sems, parts_thru, lands_thru, grad_x)
    g_w_in, g_w_o = _chip_sum_swap(parts, lands)
    g_w_in = g_w_in[:, :shard_cols]

    cols_first = lambda a: jnp.transpose(a, (2, 0, 1))
    rows_first = lambda a: jnp.transpose(a, (1, 2, 0))
    g_w_in, d_w_in, nm_w_in, nv_w_in = [rows_first(a) for a in _adamw_cols(
        cols_first(w_in), cols_first(g_w_in[None]), cols_first(m_w_in), cols_first(v_w_in), name="adamw_w_in")]
    d_w_o, nm_w_o, nv_w_o = _adamw(w_o2, g_w_o, m_w_o[0], v_w_o[0], name="adamw_w_o")

    loss_row, gs, ds, ms, vs = _small_allreduce_adamw(
        [gbf8, grb, gsk8, gg8, gb8, loss8],
        [b_f, rel_bias, sink, ln_g, ln_b],
        [m_b_f, m_rel_bias, m_sink, m_ln_g, m_ln_b],
        [v_b_f, v_rel_bias, v_sink, v_ln_g, v_ln_b])
    loss = loss_row[0, 0]
    g_bf, g_rb, g_sk, g_lg, g_lb = gs
    d_bf, d_rb, d_sk, d_lg, d_lb = ds
    m_bf, m_rb, m_sk, m_lg, m_lb = ms
    v_bf, v_rb, v_sk, v_lg, v_lb = vs

    e = lambda a: a[None]
    return (loss, e(grad_x),
            g_w_in, g_bf, g_rb, g_sk, e(g_w_o), g_lg, g_lb,
            d_w_in, d_bf, d_rb, d_sk, e(d_w_o), d_lg, d_lb,
            nm_w_in, m_bf, m_rb, m_sk, e(nm_w_o), m_lg, m_lb,
            nv_w_in, v_bf, v_rb, v_sk, e(nv_w_o), v_lg, v_lb)
```

```python
import functools
import math

import numpy as np
import jax
import jax.numpy as jnp
from jax import lax
from jax.experimental import pallas as pl
from jax.experimental.pallas import tpu as pltpu

F32 = jnp.float32
BF16 = jnp.bfloat16

D_MODEL = 1024
HEAD_DIM = 64
FOX_HEADS = 8
SWA_HEADS = 8
SWA_KV_HEADS = 2
SWA_GROUP = 4
FOX_W = 512
SWA_W = 512
BLOCK = 128
NUM_BUCKETS = 32
MAX_DISTANCE = 128
LN_EPS = 1e-5
NEG = -1e30
ALPHA = 2.0 ** 0.25
QK_SCALE = 0.125

ADAM_LR = 0.001
ADAM_B1 = 0.9
ADAM_B2 = 0.999
ADAM_EPS = 1e-08
ADAM_WD = 0.01
ADAM_STEP = 10

D_IN = 3336
SHARD_PAD = 896
N_A = 2304
N_C = 256
N_B = 1024
OFF_C = N_A
OFF_B = N_A + N_C
N_PAD = N_A + N_C + N_B
COL_FK, COL_FV, COL_SQ, COL_SK, COL_SV = 512, 1024, 1536, 2048, 2176

LANES = 128
FOX_T = 256
FOX_REF = 512
SUM_ROWS = 16
VMEM_LIMIT = 56 * 1024 * 1024

MESH = pl.DeviceIdType.MESH
N_CHIPS = 4
N_DEV = 8
SMALL_ROWS = 56


def _cparams(sem=None):
    return pltpu.CompilerParams(dimension_semantics=sem, vmem_limit_bytes=VMEM_LIMIT)


def _split3(x):
    hi = x.astype(BF16)
    r = x - hi.astype(F32)
    mid = r.astype(BF16)
    lo = (r - mid.astype(F32)).astype(BF16)
    return hi, mid, lo


def _dot(a, b):
    return jnp.dot(a, b, preferred_element_type=F32)


def _dot_nt(a, b):
    return lax.dot_general(a, b, (((1,), (1,)), ((), ())), preferred_element_type=F32)


def _project(x, w_pad, token):
    s, k = x.shape
    tm = 512
    chunk = 512

    def kern(x_ref, w_ref, _, qkv_ref, ff_ref, z_ref, xt_ref, vt_ref):
        xf = x_ref[...]
        xb = xf.astype(BF16)
        xt_ref[...] = xf.T.astype(BF16)
        for c0 in range(0, N_A, chunk):
            width = min(chunk, N_A - c0)
            res = _dot(xb, w_ref[:, c0:c0 + width])
            qkv_ref[:, c0:c0 + width] = res.astype(BF16)
            if c0 == COL_FV:
                vt_ref[...] = res.T.astype(BF16)
        ff_ref[...] = _dot(xb, w_ref[:, OFF_C:OFF_C + N_C])
        for c0 in range(0, N_B, 512):
            z_ref[:, c0:c0 + 512] = _dot(xb, w_ref[:, OFF_B + c0:OFF_B + c0 + 512])

    row = lambda i: (i, 0)
    return pl.pallas_call(
        kern, name="project",
        grid=(s // tm,),
        in_specs=[pl.BlockSpec((tm, k), row),
                  _resident((k, N_PAD), lambda i: (0, 0)),
                  _resident(token.shape, lambda i: (0, 0))],
        out_specs=[pl.BlockSpec((tm, N_A), row),
                   pl.BlockSpec((tm, N_C), row),
                   pl.BlockSpec((tm, N_B), row),
                   pl.BlockSpec((k, tm), lambda i: (0, i)),
                   pl.BlockSpec((FOX_W, tm), lambda i: (0, i))],
        out_shape=[jax.ShapeDtypeStruct((s, N_A), BF16),
                   jax.ShapeDtypeStruct((s, N_C), F32),
                   jax.ShapeDtypeStruct((s, N_B), F32),
                   jax.ShapeDtypeStruct((k, s), BF16),
                   jax.ShapeDtypeStruct((FOX_W, s), BF16)],
        compiler_params=_cparams(("parallel",)),
    )(x, w_pad, token)


def _grad_x_matmul(pieces, w_pad, dh, token, *, tm, tn, name):
    m = dh.shape[0]
    n, k = w_pad.shape
    widths = [p.shape[1] for p in pieces]
    offs = [sum(widths[:i]) for i in range(len(pieces))]
    assert sum(widths) == k

    def kern(*refs):
        p_refs, (b_ref, dh_ref, _, o_ref) = refs[:len(pieces)], refs[len(pieces):]
        acc = ALPHA * dh_ref[...]
        for p_ref, off, width in zip(p_refs, offs, widths):
            acc = acc + _dot_nt(p_ref[...], b_ref[:, off:off + width])
        o_ref[...] = acc

    assert tn == n
    return pl.pallas_call(
        kern, name=name,
        grid=(m // tm,),
        in_specs=[pl.BlockSpec((tm, w), lambda i: (i, 0)) for w in widths]
        + [_resident((n, k), lambda i: (0, 0)),
           pl.BlockSpec((tm, n), lambda i: (i, 0)),
           _resident(token.shape, lambda i: (0, 0))],
        out_specs=pl.BlockSpec((tm, n), lambda i: (i, 0)),
        out_shape=jax.ShapeDtypeStruct((m, n), F32),
        compiler_params=_cparams(("parallel",)),
    )(*pieces, w_pad, dh, token)


def _grad_w_matmul(xt, blocks, *, tk, name):
    m, s = xt.shape
    tn = 512
    nb = len(blocks)

    def kern(a_ref, *refs):
        b_refs, o_ref = refs[:nb], refs[nb]

        @pl.when(pl.program_id(0) == 0)
        def _():
            o_ref[...] = jnp.zeros_like(o_ref)
        a = a_ref[...]
        for blk in range(nb):
            o_ref[:, blk * tn:(blk + 1) * tn] += _dot(a, b_refs[blk][...])

    return pl.pallas_call(
        kern, name=name,
        grid=(s // tk,),
        in_specs=[pl.BlockSpec((m, tk), lambda k: (0, k))]
        + [pl.BlockSpec((tk, tn), functools.partial(lambda k, col: (k, col), col=col)) for _, col in blocks],
        out_specs=_resident((m, nb * tn), lambda k: (0, 0)),
        out_shape=jax.ShapeDtypeStruct((m, nb * tn), F32),
        compiler_params=_cparams(("arbitrary",)),
    )(xt, *[arr for arr, _ in blocks])


def _tri(n, lower):
    r = lax.broadcasted_iota(jnp.int32, (n, n), 0)
    c = lax.broadcasted_iota(jnp.int32, (n, n), 1)
    keep = (c <= r) if lower else (c >= r)
    return jnp.where(keep, 1.0, 0.0).astype(BF16)


def _exact_dot(mat_bf16, x_f32, left):
    out = None
    for piece in _split3(x_f32):
        t = _dot(mat_bf16, piece) if left else _dot(piece, mat_bf16)
        out = t if out is None else out + t
    return out


def _log_sigmoid(z):
    return jnp.minimum(z, 0.0) - jnp.log(1.0 + jnp.exp(-jnp.abs(z)))


def _cum_fwd(ffp, bfp):
    s = ffp.shape[0]
    t = min(1024, s)

    def kern(ff_ref, b_ref, cum_ref, carry_ref):
        @pl.when(pl.program_id(0) == 0)
        def _():
            carry_ref[...] = jnp.zeros_like(carry_ref)
        lane = lax.broadcasted_iota(jnp.int32, (1, LANES), 1)
        lf = _log_sigmoid(ff_ref[...] + b_ref[...])
        lf = jnp.where(lane < FOX_HEADS, lf, 0.0)
        cum = _exact_dot(_tri(t, True), lf, True) + carry_ref[0:1, :]
        cum_ref[...] = cum
        carry_ref[...] = jnp.broadcast_to(cum[t - 1:t, :], carry_ref.shape)

    return pl.pallas_call(
        kern, name="cum_fwd",
        grid=(s // t,),
        in_specs=[pl.BlockSpec((t, LANES), lambda i: (i, 0)),
                  pl.BlockSpec((1, LANES), lambda i: (0, 0))],
        out_specs=pl.BlockSpec((t, LANES), lambda i: (i, 0)),
        out_shape=jax.ShapeDtypeStruct((s, LANES), F32),
        scratch_shapes=[pltpu.VMEM((8, LANES), F32)],
        compiler_params=_cparams(("arbitrary",)),
    )(ffp, bfp)


def _cum_bwd(dcum_k, dcum_q, ffp, bfp):
    s = dcum_k.shape[0]
    t = min(1024, s)
    nb = s // t

    def kern(dck_ref, dcq_ref, ff_ref, b_ref, dff_ref, gb_ref, carry_ref):
        @pl.when(pl.program_id(0) == 0)
        def _():
            carry_ref[...] = jnp.zeros_like(carry_ref)
            gb_ref[...] = jnp.zeros_like(gb_ref)
        lane = lax.broadcasted_iota(jnp.int32, (1, LANES), 1)
        dlf = _exact_dot(_tri(t, False), dck_ref[...] + dcq_ref[...], True) + carry_ref[0:1, :]
        carry_ref[...] = jnp.broadcast_to(dlf[0:1, :], carry_ref.shape)
        z = ff_ref[...] + b_ref[...]
        dff = jnp.where(lane < FOX_HEADS, dlf / (1.0 + jnp.exp(z)), 0.0)
        gb_ref[...] += jnp.broadcast_to(jnp.sum(dff, axis=0, keepdims=True), gb_ref.shape)
        dff_ref[...] = jnp.concatenate([dff, jnp.zeros_like(dff)], axis=1).astype(BF16)

    return pl.pallas_call(
        kern, name="cum_bwd",
        grid=(nb,),
        in_specs=[pl.BlockSpec((t, LANES), lambda i: (nb - 1 - i, 0)),
                  pl.BlockSpec((t, LANES), lambda i: (nb - 1 - i, 0)),
                  pl.BlockSpec((t, LANES), lambda i: (nb - 1 - i, 0)),
                  pl.BlockSpec((1, LANES), lambda i: (0, 0))],
        out_specs=[pl.BlockSpec((t, N_C), lambda i: (nb - 1 - i, 0)),
                   pl.BlockSpec((8, LANES), lambda i: (0, 0))],
        out_shape=[jax.ShapeDtypeStruct((s, N_C), BF16),
                   jax.ShapeDtypeStruct((8, LANES), F32)],
        scratch_shapes=[pltpu.VMEM((8, LANES), F32)],
        compiler_params=_cparams(("arbitrary",)),
    )(dcum_k, dcum_q, ffp, bfp)


def _resident(shape, index_map):
    return pl.BlockSpec(shape, index_map, pipeline_mode=pl.Buffered(1))


def _fox_fwd(qkv, vt, cum_t3, cum):
    s = qkv.shape[0]
    tk = tq = FOX_REF
    nq = s // tq
    nh = FOX_HEADS
    diag_tiles = tq // tk

    def kern(q_ref, k_ref, vt_ref, ct_ref, c_ref, o_ref, lse_ref, m_ref, acc_ref, u_ref):
        i = pl.program_id(0)
        lane = lax.broadcasted_iota(jnp.int32, (1, LANES), 1)
        krow = lax.broadcasted_iota(jnp.int32, (tk, tq), 0)
        qcol = lax.broadcasted_iota(jnp.int32, (tk, tq), 1)
        q0 = pl.multiple_of(i * tq, tq)
        qts, crefs = [], []
        for h in range(nh):
            p, a = divmod(h, 2)
            q2 = q_ref[:, p * LANES:(p + 1) * LANES] * jnp.asarray(QK_SCALE, BF16)
            sel = (lane < HEAD_DIM) if a == 0 else (lane >= HEAD_DIM)
            qts.append(jnp.where(sel, q2, jnp.zeros_like(q2)).astype(F32).T.astype(BF16))
            crefs.append(ct_ref[h, :, pl.ds(q0, LANES)][:, 0:1])
        m_ref[...] = jnp.full(m_ref.shape, NEG, F32)
        acc_ref[...] = jnp.zeros_like(acc_ref)
        ones = jnp.ones((SUM_ROWS, tk), BF16)

        def tile(j, diag):
            k0 = pl.multiple_of(j * tk, tk)
            cb = c_ref[pl.ds(k0, tk), :]
            sts = [_dot(k_ref[pl.ds(k0, tk), (h // 2) * LANES:(h // 2 + 1) * LANES], qts[h]) for h in range(nh)]
            tile_max = []
            for h in range(nh):
                u = sts[h] - (cb[:, h:h + 1] - crefs[h])
                if diag is not None:
                    u = jnp.where(krow + diag * tk <= qcol, u, NEG)
                u_ref[h] = u
                tile_max.append(jnp.max(u, axis=0, keepdims=True))
            pts, scales = [], []
            for h in range(nh):
                m_old = m_ref[h]
                m_new = jnp.maximum(m_old, tile_max[h])
                scales.append(jnp.exp(m_old - m_new))
                pts.append(jnp.exp(u_ref[h] - m_new).astype(BF16))
                m_ref[h] = m_new
            for h in range(nh):
                vth = jnp.concatenate([vt_ref[h * HEAD_DIM:(h + 1) * HEAD_DIM, pl.ds(k0, tk)], ones], axis=0)
                acc_ref[h] = scales[h] * acc_ref[h] + _dot(vth, pts[h])

        def body(j, c):
            tile(j, None)
            return c
        lax.fori_loop(0, i * diag_tiles, body, 0)
        for d in range(diag_tiles):
            tile(i * diag_tiles + d, d)

        ls = [acc_ref[h][HEAD_DIM:HEAD_DIM + 1] for h in range(nh)]
        for p in range(nh // 2):
            ot = jnp.concatenate([acc_ref[2 * p + a][:HEAD_DIM] * (1.0 / ls[2 * p + a]) for a in range(2)], axis=0)
            o_ref[:, p * LANES:(p + 1) * LANES] = ot.T
        for h in range(nh):
            lse_ref[h, :, pl.ds(q0, tq)] = m_ref[h] + jnp.log(ls[h])

    return pl.pallas_call(
        kern, name="fox_fwd",
        grid=(nq,),
        in_specs=[pl.BlockSpec((tq, FOX_W), lambda i: (i, 0)),
                  _resident((s, FOX_W), lambda i: (0, COL_FK // FOX_W)),
                  _resident((FOX_W, s), lambda i: (0, 0)),
                  _resident((nh, 1, s), lambda i: (0, 0, 0)),
                  _resident((s, LANES), lambda i: (0, 0))],
        out_specs=[pl.BlockSpec((tq, FOX_W), lambda i: (i, 0)),
                   pl.BlockSpec((nh, 1, s), lambda i: (0, 0, 0))],
        out_shape=[jax.ShapeDtypeStruct((s, FOX_W), F32),
                   jax.ShapeDtypeStruct((nh, 1, s), F32)],
        scratch_shapes=[pltpu.VMEM((nh, 1, tq), F32),
                        pltpu.VMEM((nh, HEAD_DIM + SUM_ROWS, tq), F32),
                        pltpu.VMEM((nh, tk, tq), F32)],
        compiler_params=_cparams(("arbitrary",)),
    )(qkv, qkv, vt, cum_t3, cum)


def _fox_bwd(qkv, do_bf, cum_t3, cum, lse_t3, delta_t3):
    s = qkv.shape[0]
    t = min(FOX_T, s)
    nq = s // t
    nh = FOX_HEADS
    npair = nh // 2

    def kern(q_ref, do_ref, k_ref, v_ref, ct_ref, c_ref, lse_ref, dl_ref,
             dq_ref, dk_ref, dv_ref, dc_ref, dcq_ref, dqt_ref, accv_ref, acck_ref, accd_ref):
        kj = pl.program_id(0)
        lane = lax.broadcasted_iota(jnp.int32, (1, LANES), 1)
        krow = lax.broadcasted_iota(jnp.int32, (t, t), 0)
        qcol = lax.broadcasted_iota(jnp.int32, (t, t), 1)
        causal = krow <= qcol
        sels = [lane < HEAD_DIM, lane >= HEAD_DIM]

        @pl.when(kj == 0)
        def _():
            dqt_ref[...] = jnp.zeros_like(dqt_ref)
            dcq_ref[...] = jnp.zeros_like(dcq_ref)

        cb = c_ref[...]
        k2s, v2s, kts = [], [], []
        for p in range(npair):
            k2 = k_ref[:, p * LANES:(p + 1) * LANES]
            k2s.append(k2)
            v2s.append(v_ref[:, p * LANES:(p + 1) * LANES])
            kt = k2.astype(F32).T * QK_SCALE
            kts.append(kt[:HEAD_DIM].astype(BF16))
            kts.append(kt[HEAD_DIM:].astype(BF16))
        css = [cb[:, h:h + 1] for h in range(nh)]

        def tile(i, masked):
            q0 = pl.multiple_of(i * t, t)
            r0 = pl.multiple_of((i // (FOX_REF // t)) * FOX_REF, FOX_REF)
            sts, dpts, qms, doms = [], [], [], []
            for h in range(nh):
                p, a = divmod(h, 2)
                qi = q_ref[pl.ds(q0, t), p * LANES:(p + 1) * LANES] * jnp.asarray(QK_SCALE, BF16)
                doi = do_ref[pl.ds(q0, t), p * LANES:(p + 1) * LANES]
                qm = jnp.where(sels[a], qi, jnp.zeros_like(qi))
                dom = jnp.where(sels[a], doi, jnp.zeros_like(doi))
                qms.append(qm)
                doms.append(dom)
                sts.append(_dot_nt(k2s[p], qm))
                dpts.append(_dot_nt(v2s[p], dom))
            pts, dsts = [], []
            for h in range(nh):
                cref = ct_ref[h, :, pl.ds(r0, LANES)][:, 0:1]
                pt = jnp.exp(sts[h] - (css[h] - cref) - lse_ref[h, :, pl.ds(q0, t)])
                if masked:
                    pt = jnp.where(causal, pt, 0.0)
                ds32 = pt * (dpts[h] - dl_ref[h, :, pl.ds(q0, t)])
                part = ds32[:, 0:LANES]
                for c in range(1, t // LANES):
                    part = part + ds32[:, c * LANES:(c + 1) * LANES]
                accd_ref[h] = part if masked else accd_ref[h] + part
                dcq_ref[h, :, pl.ds(q0, t)] += jnp.sum(ds32, axis=0, keepdims=True)
                pts.append(pt.astype(BF16))
                dsts.append(ds32.astype(BF16))
            for p in range(npair):
                ha, hb = 2 * p, 2 * p + 1
                dv_p = _dot(pts[ha], doms[ha]) + _dot(pts[hb], doms[hb])
                dk_p = _dot(dsts[ha], qms[ha]) + _dot(dsts[hb], qms[hb])
                accv_ref[p] = dv_p if masked else accv_ref[p] + dv_p
                acck_ref[p] = dk_p if masked else acck_ref[p] + dk_p
            for h in range(nh):
                dqt_ref[h * HEAD_DIM:(h + 1) * HEAD_DIM, pl.ds(q0, t)] += _dot(kts[h], dsts[h])

        tile(kj, True)

        def body(i, c):
            tile(i, False)
            return c
        lax.fori_loop(kj + 1, nq, body, 0)

        dc = jnp.zeros((t, LANES), F32)
        for h in range(nh):
            dc = jnp.where(lane == h, -jnp.sum(accd_ref[h], axis=1, keepdims=True), dc)
        dc_ref[...] = dc
        for p in range(npair):
            dv_ref[:, p * LANES:(p + 1) * LANES] = accv_ref[p].astype(BF16)
            dk_ref[:, p * LANES:(p + 1) * LANES] = acck_ref[p].astype(BF16)

        dq_ref[...] = dqt_ref[:, pl.ds(pl.multiple_of(kj * t, t), t)].T.astype(BF16)

    whole = lambda kj: (0, 0, 0)
    return pl.pallas_call(
        kern, name="fox_bwd",
        grid=(nq,),
        in_specs=[_resident((s, FOX_W), lambda kj: (0, 0)),
                  _resident((s, FOX_W), lambda kj: (0, 0)),
                  pl.BlockSpec((t, FOX_W), lambda kj: (kj, COL_FK // FOX_W)),
                  pl.BlockSpec((t, FOX_W), lambda kj: (kj, COL_FV // FOX_W)),
                  _resident((nh, 1, s), whole),
                  pl.BlockSpec((t, LANES), lambda kj: (kj, 0)),
                  _resident((nh, 1, s), whole),
                  _resident((nh, 1, s), whole)],
        out_specs=[pl.BlockSpec((t, FOX_W), lambda kj: (kj, 0)),
                   pl.BlockSpec((t, FOX_W), lambda kj: (kj, 0)),
                   pl.BlockSpec((t, FOX_W), lambda kj: (kj, 0)),
                   pl.BlockSpec((t, LANES), lambda kj: (kj, 0)),
                   _resident((nh, 1, s), whole)],
        out_shape=[jax.ShapeDtypeStruct((s, FOX_W), BF16),
                   jax.ShapeDtypeStruct((s, FOX_W), BF16),
                   jax.ShapeDtypeStruct((s, FOX_W), BF16),
                   jax.ShapeDtypeStruct((s, LANES), F32),
                   jax.ShapeDtypeStruct((nh, 1, s), F32)],
        scratch_shapes=[pltpu.VMEM((FOX_W, s), F32),
                        pltpu.VMEM((npair, t, LANES), F32),
                        pltpu.VMEM((npair, t, LANES), F32),
                        pltpu.VMEM((nh, t, LANES), F32)],
        compiler_params=_cparams(("arbitrary",)),
    )(qkv, do_bf, qkv, qkv, cum_t3, cum, lse_t3, delta_t3)


def _bucket_table():
    qi = np.arange(BLOCK)[:, None]
    kj = np.arange(2 * BLOCK)[None, :]
    rel = np.maximum(qi + BLOCK - kj, 0).astype(np.int32)
    max_exact = NUM_BUCKETS // 2
    relf = np.maximum(rel, 1).astype(np.float32)
    large = max_exact + (np.log(relf / np.float32(max_exact)) / np.float32(math.log(MAX_DISTANCE / max_exact))
                         * np.float32(NUM_BUCKETS - max_exact)).astype(np.int32)
    large = np.minimum(large, NUM_BUCKETS - 1)
    return np.where(rel < max_exact, rel, large).astype(np.int32)


SWA_LANES = SWA_GROUP * BLOCK


def _swa_bias(rel_bias, bucket_t):
    def kern(rb_ref, bk_ref, o_ref):
        bk = bk_ref[...]
        kj = lax.broadcasted_iota(jnp.int32, (2 * BLOCK, BLOCK), 0)
        qi = lax.broadcasted_iota(jnp.int32, (2 * BLOCK, BLOCK), 1)
        rel = qi + BLOCK - kj
        band = (rel >= 0) & (rel < BLOCK)
        masks = [band & (kj >= BLOCK), band]
        for h in range(SWA_HEADS):
            g, hh = divmod(h, SWA_GROUP)
            acc = jnp.zeros((2 * BLOCK, BLOCK), F32)
            for b in range(NUM_BUCKETS):
                acc = jnp.where(bk == b, rb_ref[b, h], acc)
            for first in range(2):
                o_ref[first, g, :, hh * BLOCK:(hh + 1) * BLOCK] = jnp.where(masks[first], acc, NEG)

    return pl.pallas_call(
        kern, name="swa_bias",
        in_specs=[pl.BlockSpec(memory_space=pltpu.SMEM),
                  pl.BlockSpec(memory_space=pltpu.VMEM)],
        out_specs=pl.BlockSpec(memory_space=pltpu.VMEM),
        out_shape=jax.ShapeDtypeStruct((2, SWA_KV_HEADS, 2 * BLOCK, SWA_LANES), F32),
        compiler_params=_cparams(),
    )(rel_bias, bucket_t)


SWA_STEP = 8


def _swa_keys(prev_ref, cur_ref):
    return jnp.concatenate([prev_ref[...], cur_ref[...]], axis=0)


def _swa_queries(x_ref, scale):
    x = x_ref[...]
    if scale:
        x = x * jnp.asarray(QK_SCALE, BF16)
    xt = x.astype(F32).T.astype(BF16)
    return [_group_rows(xt[:, b * BLOCK:(b + 1) * BLOCK]) for b in range(SWA_STEP)]


def _group_rows(xt):
    zeros = jnp.zeros((HEAD_DIM, SWA_LANES), BF16)
    out = []
    for g in range(SWA_KV_HEADS):
        heads = [xt[(SWA_GROUP * g + hh) * HEAD_DIM:(SWA_GROUP * g + hh + 1) * HEAD_DIM, :] for hh in range(SWA_GROUP)]
        rows = jnp.concatenate(heads, axis=1)
        padded = jnp.concatenate([rows, zeros] if g == 0 else [zeros, rows], axis=0)
        out.append((rows, padded))
    return out


def _pairs_to_rows(cols_t):
    out = []
    for p in range(SWA_HEADS // 2):
        g, hh = divmod(2 * p, SWA_GROUP)
        pair = jnp.concatenate([cols_t[g][:, hh * BLOCK:(hh + 1) * BLOCK],
                                cols_t[g][:, (hh + 1) * BLOCK:(hh + 2) * BLOCK]], axis=0)
        out.append(pair.T)
    return jnp.concatenate(out, axis=1)


def _swa_fwd(qkv, bias_t, sink_rows):
    s = qkv.shape[0]
    nb = s // BLOCK
    rows = SWA_STEP * BLOCK
    units = [(b, g) for b in range(SWA_STEP) for g in range(SWA_KV_HEADS)]

    def kern(q_ref, kp_ref, kc_ref, vp_ref, vc_ref, bias_ref, sink_ref, o_ref, lse_ref):
        n = pl.program_id(0)
        tables = [jnp.minimum(n, 1)] + [1] * (SWA_STEP - 1)
        k3 = _swa_keys(kp_ref, kc_ref)
        vt3 = _swa_keys(vp_ref, vc_ref).astype(F32).T.astype(BF16)
        qts = _swa_queries(q_ref, True)
        us = [_dot(k3[b * BLOCK:(b + 2) * BLOCK], qts[b][g][1]) + bias_ref[tables[b], g] for b, g in units]
        outs = []
        for (b, g), u in zip(units, us):
            sk = sink_ref[g]
            m = jnp.maximum(jnp.max(u, axis=0, keepdims=True), sk)
            p = jnp.exp(u - m)
            l = jnp.sum(p, axis=0, keepdims=True) + jnp.exp(sk - m)
            lse_ref[b, g] = m + jnp.log(l)
            vt = vt3[g * HEAD_DIM:(g + 1) * HEAD_DIM, b * BLOCK:(b + 2) * BLOCK]
            outs.append(_dot(vt, (p * (1.0 / l)).astype(BF16)))
        for b in range(SWA_STEP):
            o_ref[b * BLOCK:(b + 1) * BLOCK, :] = _pairs_to_rows(outs[b * SWA_KV_HEADS:(b + 1) * SWA_KV_HEADS])

    cq, ck, cv = COL_SQ // SWA_W, COL_SK // LANES, COL_SV // LANES
    prev = lambda n: jnp.maximum(SWA_STEP * n - 1, 0)
    return pl.pallas_call(
        kern, name="swa_fwd",
        grid=(nb // SWA_STEP,),
        in_specs=[pl.BlockSpec((rows, SWA_W), lambda n: (n, cq)),
                  pl.BlockSpec((BLOCK, LANES), lambda n: (prev(n), ck)),
                  pl.BlockSpec((rows, LANES), lambda n: (n, ck)),
                  pl.BlockSpec((BLOCK, LANES), lambda n: (prev(n), cv)),
                  pl.BlockSpec((rows, LANES), lambda n: (n, cv)),
                  _resident((2, SWA_KV_HEADS, 2 * BLOCK, SWA_LANES), lambda n: (0, 0, 0, 0)),
                  _resident((SWA_KV_HEADS, 1, SWA_LANES), lambda n: (0, 0, 0))],
        out_specs=[pl.BlockSpec((rows, SWA_W), lambda n: (n, 0)),
                   pl.BlockSpec((SWA_STEP, SWA_KV_HEADS, 1, SWA_LANES), lambda n: (n, 0, 0, 0))],
        out_shape=[jax.ShapeDtypeStruct((s, SWA_W), F32),
                   jax.ShapeDtypeStruct((nb, SWA_KV_HEADS, 1, SWA_LANES), F32)],
        compiler_params=_cparams(("parallel",)),
    )(qkv, qkv, qkv, qkv, qkv, bias_t, sink_rows)


def _swa_bwd(qkv, do_bf, delta_rows, lse, bias_t, sink_rows, bucket_t):
    s = qkv.shape[0]
    nb = s // BLOCK
    steps = nb // SWA_STEP
    rows = SWA_STEP * BLOCK
    units = [(b, g) for b in range(SWA_STEP) for g in range(SWA_KV_HEADS)]

    def kern(q_ref, kp_ref, kc_ref, vp_ref, vc_ref, do_ref, dl_ref, lse_ref, bias_ref, sink_ref, bk_ref,
             dq_ref, dk_ref, dv_ref, grb_ref, gsk_ref, dbias_ref, ck_ref, cv_ref, sk_ref):
        n = pl.program_id(0)

        @pl.when(n == 0)
        def _():
            dbias_ref[...] = jnp.zeros_like(dbias_ref)
            ck_ref[...] = jnp.zeros_like(ck_ref)
            cv_ref[...] = jnp.zeros_like(cv_ref)
            sk_ref[...] = jnp.zeros_like(sk_ref)

        @pl.when(n < steps)
        def _():
            tables = [jnp.minimum(n, 1)] + [1] * (SWA_STEP - 1)
            k3 = _swa_keys(kp_ref, kc_ref)
            v3 = _swa_keys(vp_ref, vc_ref)
            kt3 = (k3.astype(F32).T * QK_SCALE).astype(BF16)
            qts = _swa_queries(q_ref, True)
            dots = _swa_queries(do_ref, False)
            sts = [_dot(k3[b * BLOCK:(b + 2) * BLOCK], qts[b][g][1]) for b, g in units]
            dps = [_dot(v3[b * BLOCK:(b + 2) * BLOCK], dots[b][g][1]) for b, g in units]
            ps, dss = [], []
            for i, (b, g) in enumerate(units):
                lse_g = lse_ref[b, g]
                dlt = dl_ref[b, g]
                p = jnp.exp(sts[i] + bias_ref[tables[b], g] - lse_g)
                ds = p * (dps[i] - dlt)
                dbias_ref[g] += ds
                sk_ref[g] += -jnp.exp(sink_ref[g] - lse_g) * dlt
                ps.append(p.astype(BF16))
                dss.append(ds.astype(BF16))
            dk2, dv2 = [], []
            for b in range(SWA_STEP):
                at = lambda g: b * SWA_KV_HEADS + g
                groups = range(SWA_KV_HEADS)
                dv2.append(jnp.concatenate([_dot_nt(dots[b][g][0], ps[at(g)]) for g in groups], axis=0).T)
                dk2.append(jnp.concatenate([_dot_nt(qts[b][g][0], dss[at(g)]) for g in groups], axis=0).T)
                dqts = [_dot(kt3[g * HEAD_DIM:(g + 1) * HEAD_DIM, b * BLOCK:(b + 2) * BLOCK], dss[at(g)]) for g in groups]
                dq_ref[b * BLOCK:(b + 1) * BLOCK, :] = _pairs_to_rows(dqts).astype(BF16)
            last = (SWA_STEP - 1) * BLOCK
            for acc_ref, out_ref, parts in ((ck_ref, dk_ref, dk2), (cv_ref, dv_ref, dv2)):
                done = acc_ref[last:] + parts[0][:BLOCK]
                out_ref[...] = jnp.concatenate([acc_ref[:last], done], axis=0).astype(BF16)
                for b in range(SWA_STEP - 1):
                    acc_ref[b * BLOCK:(b + 1) * BLOCK] = parts[b][BLOCK:] + parts[b + 1][:BLOCK]
                acc_ref[last:] = parts[SWA_STEP - 1][BLOCK:]

        @pl.when(n == steps)
        def _():
            dk_ref[...] = ck_ref[...].astype(BF16)
            dv_ref[...] = cv_ref[...].astype(BF16)
            bk = bk_ref[...]
            lane = lax.broadcasted_iota(jnp.int32, (8, LANES), 1)
            rowi = lax.broadcasted_iota(jnp.int32, (NUM_BUCKETS, LANES), 0)
            lanei = lax.broadcasted_iota(jnp.int32, (NUM_BUCKETS, LANES), 1)
            out = jnp.zeros((NUM_BUCKETS, LANES), F32)
            gsk = jnp.zeros((8, LANES), F32)
            for h in range(SWA_HEADS):
                g, hh = divmod(h, SWA_GROUP)
                cols = slice(hh * BLOCK, (hh + 1) * BLOCK)
                gsk = jnp.where(lane == h, jnp.sum(sk_ref[g][:, cols]), gsk)
                db = dbias_ref[g][:, cols]
                for b in range(NUM_BUCKETS):
                    val = jnp.sum(jnp.where(bk == b, db, 0.0))
                    out = jnp.where((rowi == b) & (lanei == h), val, out)
            grb_ref[...] = out
            gsk_ref[...] = gsk

    cq, ck, cv = COL_SQ // SWA_W, COL_SK // LANES, COL_SV // LANES
    cur = lambda n: jnp.minimum(n, steps - 1)
    prev = lambda n: jnp.maximum(SWA_STEP * cur(n) - 1, 0)
    kout = lambda n: jnp.maximum(n - 1, 0)
    stat = pl.BlockSpec((SWA_STEP, SWA_KV_HEADS, 1, SWA_LANES), lambda n: (cur(n), 0, 0, 0))
    return pl.pallas_call(
        kern, name="swa_bwd",
        grid=(steps + 1,),
        in_specs=[pl.BlockSpec((rows, SWA_W), lambda n: (cur(n), cq)),
                  pl.BlockSpec((BLOCK, LANES), lambda n: (prev(n), ck)),
                  pl.BlockSpec((rows, LANES), lambda n: (cur(n), ck)),
                  pl.BlockSpec((BLOCK, LANES), lambda n: (prev(n), cv)),
                  pl.BlockSpec((rows, LANES), lambda n: (cur(n), cv)),
                  pl.BlockSpec((rows, SWA_W), lambda n: (cur(n), 1)),
                  stat, stat,
                  _resident((2, SWA_KV_HEADS, 2 * BLOCK, SWA_LANES), lambda n: (0, 0, 0, 0)),
                  _resident((SWA_KV_HEADS, 1, SWA_LANES), lambda n: (0, 0, 0)),
                  _resident((2 * BLOCK, BLOCK), lambda n: (0, 0))],
        out_specs=[pl.BlockSpec((rows, SWA_W), lambda n: (cur(n), 0)),
                   pl.BlockSpec((rows, LANES), lambda n: (kout(n), 0)),
                   pl.BlockSpec((rows, LANES), lambda n: (kout(n), 0)),
                   pl.BlockSpec((NUM_BUCKETS, LANES), lambda n: (0, 0)),
                   pl.BlockSpec((8, LANES), lambda n: (0, 0))],
        out_shape=[jax.ShapeDtypeStruct((s, SWA_W), BF16),
                   jax.ShapeDtypeStruct((s, LANES), BF16),
                   jax.ShapeDtypeStruct((s, LANES), BF16),
                   jax.ShapeDtypeStruct((NUM_BUCKETS, LANES), F32),
                   jax.ShapeDtypeStruct((8, LANES), F32)],
        scratch_shapes=[pltpu.VMEM((SWA_KV_HEADS, 2 * BLOCK, SWA_LANES), F32),
                        pltpu.VMEM((rows, LANES), F32),
                        pltpu.VMEM((rows, LANES), F32),
                        pltpu.VMEM((SWA_KV_HEADS, 1, SWA_LANES), F32)],
        compiler_params=_cparams(("arbitrary",)),
    )(qkv, qkv, qkv, qkv, qkv, do_bf, delta_rows, lse, bias_t, sink_rows, bucket_t)


def _post(x, target, o_fox, o_swa, z, w_o, ln_g, ln_b):
    s = x.shape[0]
    tm = min(256, s)
    nt = s // tm

    def kern(x_ref, t_ref, of_ref, os_ref, z_ref, w_ref, g_ref, b_ref,
             loss_ref, dh_ref, gwo_ref, do_ref, dz_ref, dl_ref, gg_ref, gb_ref, lacc_ref):
        step = pl.program_id(0)

        @pl.when(step == 0)
        def _():
            lacc_ref[...] = jnp.zeros_like(lacc_ref)
            gg_ref[...] = jnp.zeros_like(gg_ref)
            gwo_ref[...] = jnp.zeros_like(gwo_ref)
            gb_ref[...] = jnp.zeros_like(gb_ref)

        o = jnp.concatenate([of_ref[...], os_ref[...]], axis=1)
        zz = z_ref[...]
        sig = 1.0 / (1.0 + jnp.exp(-zz))
        silu = zz * sig
        mixed32 = o * silu
        mixed = mixed32.astype(BF16)
        w = w_ref[...]
        h = ALPHA * x_ref[...] + _dot(mixed, w)
        mu = jnp.mean(h, axis=1, keepdims=True)
        hc = h - mu
        var = jnp.mean(hc * hc, axis=1, keepdims=True)
        rstd = lax.rsqrt(var + LN_EPS)
        xhat = hc * rstd
        g = g_ref[...]
        err = xhat * g + b_ref[...] - t_ref[...]
        lacc_ref[...] += jnp.broadcast_to(jnp.sum(err * err, axis=0, keepdims=True), lacc_ref.shape)
        dout = err * (1.0 / D_MODEL)
        gg_ref[...] += jnp.broadcast_to(jnp.sum(dout * xhat, axis=0, keepdims=True), gg_ref.shape)
        gb_ref[...] += jnp.broadcast_to(jnp.sum(dout, axis=0, keepdims=True), gb_ref.shape)
        dxh = dout * g
        m1 = jnp.mean(dxh, axis=1, keepdims=True)
        m2 = jnp.mean(dxh * xhat, axis=1, keepdims=True)
        dh = rstd * (dxh - m1 - xhat * m2)
        dh_ref[...] = dh
        dy = dh.astype(BF16)
        gwo_ref[...] += _dot(mixed32.T.astype(BF16), dy)
        dmix = _dot_nt(dy, w)
        do = dmix * silu
        do_ref[...] = do.astype(BF16)
        dz_ref[...] = (dmix * o * (sig * (1.0 + zz * (1.0 - sig)))).astype(BF16)
        r = lax.broadcasted_iota(jnp.int32, (D_MODEL, LANES), 0) // HEAD_DIM
        c = lax.broadcasted_iota(jnp.int32, (D_MODEL, LANES), 1)
        pick = jnp.where(r == c, 1.0, 0.0).astype(BF16)
        dl_ref[...] = _exact_dot(pick, do * o, False)

        @pl.when(step == nt - 1)
        def _():
            tot = jnp.sum(lacc_ref[0:1, :]) * (0.5 / D_MODEL)
            loss_ref[...] = jnp.broadcast_to(tot, loss_ref.shape)

    row = lambda i: (i, 0)
    fixed = lambda i: (0, 0)
    wide = pl.BlockSpec((tm, D_MODEL), row)
    half = pl.BlockSpec((tm, FOX_W), row)
    return pl.pallas_call(
        kern, name="post",
        grid=(nt,),
        in_specs=[wide, wide, half, half, wide,
                  pl.BlockSpec((D_MODEL, D_MODEL), fixed),
                  pl.BlockSpec((1, D_MODEL), fixed),
                  pl.BlockSpec((1, D_MODEL), fixed)],
        out_specs=[pl.BlockSpec((8, LANES), fixed), wide,
                   _resident((D_MODEL, D_MODEL), fixed), wide, wide,
                   pl.BlockSpec((tm, LANES), row),
                   pl.BlockSpec((8, D_MODEL), fixed), pl.BlockSpec((8, D_MODEL), fixed)],
        out_shape=[jax.ShapeDtypeStruct((8, LANES), F32),
                   jax.ShapeDtypeStruct((s, D_MODEL), F32),
                   jax.ShapeDtypeStruct((D_MODEL, D_MODEL), F32),
                   jax.ShapeDtypeStruct((s, D_MODEL), BF16),
                   jax.ShapeDtypeStruct((s, D_MODEL), BF16),
                   jax.ShapeDtypeStruct((s, LANES), F32),
                   jax.ShapeDtypeStruct((8, D_MODEL), F32),
                   jax.ShapeDtypeStruct((8, D_MODEL), F32)],
        scratch_shapes=[pltpu.VMEM((8, D_MODEL), F32)],
        compiler_params=_cparams(("arbitrary",)),
    )(x, target, o_fox, o_swa, z, w_o, ln_g, ln_b)


def _adamw_math(w, g, m, v):
    m = ADAM_B1 * m + (1.0 - ADAM_B1) * g
    v = ADAM_B2 * v + (1.0 - ADAM_B2) * (g * g)
    m_hat = m / (1.0 - ADAM_B1 ** ADAM_STEP)
    v_hat = v / (1.0 - ADAM_B2 ** ADAM_STEP)
    delta = -ADAM_LR * (m_hat / (jnp.sqrt(v_hat) + ADAM_EPS) + ADAM_WD * w)
    return delta, m, v


def _adamw(w, g, m, v, *, name):
    r, c = w.shape
    tr = min(256, r)

    def kern(w_ref, g_ref, m_ref, v_ref, d_ref, mo_ref, vo_ref):
        d, mn, vn = _adamw_math(w_ref[...], g_ref[...], m_ref[...], v_ref[...])
        d_ref[...] = d
        mo_ref[...] = mn
        vo_ref[...] = vn

    blk = pl.BlockSpec((tr, c), lambda i: (i, 0))
    sds = jax.ShapeDtypeStruct((r, c), F32)
    return pl.pallas_call(
        kern, name=name,
        grid=(r // tr,),
        in_specs=[blk, blk, blk, blk],
        out_specs=[blk, blk, blk],
        out_shape=[sds, sds, sds],
        compiler_params=_cparams(("parallel",)),
    )(w, g, m, v)


def _adamw_cols(w, g, m, v, *, name):
    c, _, r = w.shape
    tc = 139
    assert c % tc == 0

    def kern(w_ref, g_ref, m_ref, v_ref, go_ref, d_ref, mo_ref, vo_ref):
        g = g_ref[...]
        d, mn, vn = _adamw_math(w_ref[...], g, m_ref[...], v_ref[...])
        go_ref[...] = g
        d_ref[...] = d
        mo_ref[...] = mn
        vo_ref[...] = vn

    blk = pl.BlockSpec((tc, 1, r), lambda i: (i, 0, 0))
    sds = jax.ShapeDtypeStruct((c, 1, r), F32)
    return pl.pallas_call(
        kern, name=name,
        grid=(c // tc,),
        in_specs=[blk, blk, blk, blk],
        out_specs=[blk, blk, blk, blk],
        out_shape=[sds, sds, sds, sds],
        compiler_params=_cparams(("parallel",)),
    )(w, g, m, v)


def _position():
    x, y, c = lax.axis_index("x"), lax.axis_index("y"), lax.axis_index("c")
    chips = [(1 - x, y), (x, 1 - y), (1 - x, 1 - y)]
    return x, y, c, chips


def _chip_index(cx, cy):
    return 2 * cx + cy


def _gather_weights(*shards):
    n_arr = len(shards)

    def kern(*refs):
        ins, outs = refs[:n_arr], refs[n_arr:2 * n_arr]
        send_sems, recv_sems, local_sems = refs[2 * n_arr:]
        x, y, c, chips = _position()
        me = _chip_index(x, y)
        sibling = (x, y, 1 - c)

        local = [pltpu.make_async_copy(ins[a], outs[a].at[me], local_sems.at[a]) for a in range(n_arr)]
        for cp in local:
            cp.start()

        def half(ref, a):
            rows = shards[a].shape[0] // 2
            return ref.at[pl.ds(c * rows, rows), :]

        def copy(a, k, src, slot, to):
            return pltpu.make_async_remote_copy(
                src_ref=src, dst_ref=half(outs[a].at[slot], a),
                send_sem=send_sems.at[a * 6 + k], recv_sem=recv_sems.at[a * 6 + k],
                device_id=to, device_id_type=MESH)

        first = [copy(a, j, half(ins[a], a), me, (*chip, c)) for a in range(n_arr) for j, chip in enumerate(chips)]
        for cp in first:
            cp.start()
        passed = []
        for a in range(n_arr):
            for j, chip in enumerate(chips):
                slot = _chip_index(*chip)
                copy(a, j, half(ins[a], a), slot, (*chip, c)).wait_recv()
                fwd = copy(a, 3 + j, half(outs[a].at[slot], a), slot, sibling)
                fwd.start()
                passed.append(fwd)
        for a in range(n_arr):
            for j, chip in enumerate(chips):
                slot = _chip_index(*chip)
                rows = shards[a].shape[0] // 2
                dst = outs[a].at[slot].at[pl.ds((1 - c) * rows, rows), :]
                pltpu.make_async_remote_copy(
                    src_ref=dst, dst_ref=dst, send_sem=send_sems.at[a * 6 + 3 + j],
                    recv_sem=recv_sems.at[a * 6 + 3 + j], device_id=sibling, device_id_type=MESH).wait_recv()
        for cp in first + passed:
            cp.wait_send()
        for cp in local:
            cp.wait()

    vmem = pl.BlockSpec(memory_space=pltpu.VMEM)
    return pl.pallas_call(
        kern, name="gather_weights",
        in_specs=[vmem] * n_arr,
        out_specs=[vmem] * n_arr,
        out_shape=[jax.ShapeDtypeStruct((N_CHIPS,) + w.shape, w.dtype) for w in shards],
        scratch_shapes=[pltpu.SemaphoreType.DMA((6 * n_arr,)),
                        pltpu.SemaphoreType.DMA((6 * n_arr,)),
                        pltpu.SemaphoreType.DMA((n_arr,))],
        compiler_params=_cparams(),
    )(*shards)


def _pair_reduce(grads):
    n_arr = len(grads)
    chunk = 128

    def kern(*refs):
        ins = refs[:n_arr]
        outs = refs[n_arr:2 * n_arr]
        gots = refs[2 * n_arr:3 * n_arr]
        send_sems, recv_sems = refs[3 * n_arr:]
        x, y, c, _ = _position()
        sibling = (x, y, 1 - c)
        copies = []
        for a in range(n_arr):
            rows = grads[a].shape[1] // 2
            copies.append(pltpu.make_async_remote_copy(
                src_ref=ins[a].at[:, pl.ds((1 - c) * rows, rows), :], dst_ref=gots[a],
                send_sem=send_sems.at[a], recv_sem=recv_sems.at[a], device_id=sibling, device_id_type=MESH))
        for cp in copies:
            cp.start()
        for a in range(n_arr):
            copies[a].wait()
            rows = grads[a].shape[1] // 2
            for j in range(N_CHIPS):
                for r0 in range(0, rows, chunk):
                    mine = ins[a][j, pl.ds(pl.multiple_of(c * rows + r0, chunk), chunk), :]
                    outs[a][j, r0:r0 + chunk, :] = (mine + gots[a][j, r0:r0 + chunk, :]).astype(BF16)

    vmem = pl.BlockSpec(memory_space=pltpu.VMEM)
    half = [(N_CHIPS, g.shape[1] // 2, g.shape[2]) for g in grads]
    return pl.pallas_call(
        kern, name="pair_reduce",
        in_specs=[vmem] * n_arr,
        out_specs=[vmem] * n_arr,
        out_shape=[jax.ShapeDtypeStruct(h, BF16) for h in half],
        scratch_shapes=[pltpu.VMEM(h, F32) for h in half]
        + [pltpu.SemaphoreType.DMA((n_arr,)), pltpu.SemaphoreType.DMA((n_arr,))],
        compiler_params=_cparams(),
    )(*grads)


def _wo_gather_start(shard):
    hbm = pl.BlockSpec(memory_space=pltpu.HBM)
    sem = pl.BlockSpec(memory_space=pltpu.SEMAPHORE)
    land_shape = (N_CHIPS,) + shard.shape

    def kern(src_ref, land_ref, send_sems, recv_sems, src_thru, land_thru, token):
        x, y, c, chips = _position()
        me = _chip_index(x, y)
        for j, chip in enumerate(chips):
            pltpu.make_async_remote_copy(
                src_ref=src_ref, dst_ref=land_ref.at[me], send_sem=send_sems.at[j], recv_sem=recv_sems.at[j],
                device_id=(*chip, c), device_id_type=MESH).start()
        token[...] = jnp.zeros_like(token)

    return pl.pallas_call(
        kern, name="wo_gather_start",
        in_specs=[hbm, hbm],
        out_specs=(sem, sem, hbm, hbm, pl.BlockSpec(memory_space=pltpu.VMEM)),
        out_shape=(pltpu.SemaphoreType.DMA((3,)), pltpu.SemaphoreType.DMA((3,)),
                   pltpu.HBM(shard.shape, shard.dtype), pltpu.HBM(land_shape, shard.dtype),
                   jax.ShapeDtypeStruct((8, LANES), F32)),
        input_output_aliases={0: 2, 1: 3},
        compiler_params=pltpu.CompilerParams(has_side_effects=pltpu.SideEffectType.DATAFLOW_SIDE_EFFECTING),
    )(pltpu.with_memory_space_constraint(shard, pltpu.HBM),
      pltpu.with_memory_space_constraint(lax.empty(land_shape, shard.dtype), pltpu.HBM))


def _wo_gather_wait(send_sems, recv_sems, src_thru, land_thru, after):
    hbm = pl.BlockSpec(memory_space=pltpu.HBM)
    sem = pl.BlockSpec(memory_space=pltpu.SEMAPHORE)

    def kern(src_ref, land_ref, send_sems, recv_sems, after_ref, src_out, land_out):
        x, y, c, chips = _position()
        for j, chip in enumerate(chips):
            copy = pltpu.make_async_remote_copy(
                src_ref=src_ref, dst_ref=land_ref.at[_chip_index(*chip)], send_sem=send_sems.at[j],
                recv_sem=recv_sems.at[j], device_id=(*chip, c), device_id_type=MESH)
            copy.wait_send()
            copy.wait_recv()

    return pl.pallas_call(
        kern, name="wo_gather_wait",
        in_specs=[hbm, hbm, sem, sem, pl.BlockSpec(memory_space=pl.ANY)],
        out_specs=[hbm, hbm],
        out_shape=[pltpu.HBM(src_thru.shape, src_thru.dtype), pltpu.HBM(land_thru.shape, land_thru.dtype)],
        input_output_aliases={0: 0, 1: 1},
        compiler_params=pltpu.CompilerParams(has_side_effects=pltpu.SideEffectType.DATAFLOW_SIDE_EFFECTING),
    )(src_thru, land_thru, send_sems, recv_sems, after)[1]


def _scatter_start(parts):
    n_arr = len(parts)
    hbm = pl.BlockSpec(memory_space=pltpu.HBM)
    sem = pl.BlockSpec(memory_space=pltpu.SEMAPHORE)

    def kern(*refs):
        ins, lands = refs[:n_arr], refs[n_arr:2 * n_arr]
        send_sems, recv_sems, token = refs[2 * n_arr], refs[2 * n_arr + 1], refs[-1]
        x, y, c, chips = _position()
        me = _chip_index(x, y)
        for a in range(n_arr):
            for j, chip in enumerate(chips):
                pltpu.make_async_remote_copy(
                    src_ref=ins[a].at[_chip_index(*chip)], dst_ref=lands[a].at[me],
                    send_sem=send_sems.at[a * 3 + j], recv_sem=recv_sems.at[a * 3 + j],
                    device_id=(*chip, c), device_id_type=MESH).start()
        token[...] = jnp.zeros_like(token)

    slab = [pltpu.HBM(p.shape, p.dtype) for p in parts]
    outs = pl.pallas_call(
        kern, name="scatter_start",
        in_specs=[hbm] * (2 * n_arr),
        out_specs=(sem, sem, *[hbm] * (2 * n_arr), pl.BlockSpec(memory_space=pltpu.VMEM)),
        out_shape=(pltpu.SemaphoreType.DMA((3 * n_arr,)), pltpu.SemaphoreType.DMA((3 * n_arr,)),
                   *slab, *slab, jax.ShapeDtypeStruct((8, LANES), F32)),
        input_output_aliases={i: 2 + i for i in range(2 * n_arr)},
        compiler_params=pltpu.CompilerParams(has_side_effects=pltpu.SideEffectType.DATAFLOW_SIDE_EFFECTING),
    )(*[pltpu.with_memory_space_constraint(p, pltpu.HBM) for p in parts],
      *[pltpu.with_memory_space_constraint(lax.empty(p.shape, p.dtype), pltpu.HBM) for p in parts])
    return outs[0], outs[1], outs[2:2 + n_arr], outs[2 + n_arr:2 + 2 * n_arr], outs[-1]


def _scatter_wait(send_sems, recv_sems, parts_thru, lands_thru, after):
    n_arr = len(parts_thru)
    hbm = pl.BlockSpec(memory_space=pltpu.HBM)
    sem = pl.BlockSpec(memory_space=pltpu.SEMAPHORE)

    def kern(*refs):
        ins, lands = refs[:n_arr], refs[n_arr:2 * n_arr]
        send_ref, recv_ref = refs[2 * n_arr], refs[2 * n_arr + 1]
        x, y, c, chips = _position()
        for a in range(n_arr):
            for j, chip in enumerate(chips):
                slot = _chip_index(*chip)
                copy = pltpu.make_async_remote_copy(
                    src_ref=ins[a].at[slot], dst_ref=lands[a].at[slot],
                    send_sem=send_ref.at[a * 3 + j], recv_sem=recv_ref.at[a * 3 + j],
                    device_id=(*chip, c), device_id_type=MESH)
                copy.wait_send()
                copy.wait_recv()

    slab = [pltpu.HBM(p.shape, p.dtype) for p in parts_thru]
    outs = pl.pallas_call(
        kern, name="scatter_wait",
        in_specs=[hbm] * (2 * n_arr) + [sem, sem, pl.BlockSpec(memory_space=pl.ANY)],
        out_specs=[hbm] * (2 * n_arr),
        out_shape=slab + slab,
        input_output_aliases={i: i for i in range(2 * n_arr)},
        compiler_params=pltpu.CompilerParams(has_side_effects=pltpu.SideEffectType.DATAFLOW_SIDE_EFFECTING),
    )(*parts_thru, *lands_thru, send_sems, recv_sems, after)
    return outs[:n_arr], outs[n_arr:]


def _chip_sum_swap(parts, lands):
    n_arr = len(parts)
    chunk = 128

    def kern(*refs):
        own, got = refs[:n_arr], refs[n_arr:2 * n_arr]
        outs = refs[2 * n_arr:3 * n_arr]
        sums = refs[3 * n_arr:4 * n_arr]
        swap_send, swap_recv, swap_local = refs[4 * n_arr:]
        x, y, c, _ = _position()
        me = _chip_index(x, y)
        sibling = (x, y, 1 - c)
        for a in range(n_arr):
            for r0 in range(0, parts[a].shape[1], chunk):
                mine = own[a][me, r0:r0 + chunk, :].astype(F32)

                def term(i):
                    other = got[a][jnp.where(i == me, (i + 1) % N_CHIPS, i), r0:r0 + chunk, :].astype(F32)
                    return jnp.where(i == me, mine, other)
                sums[a][r0:r0 + chunk, :] = ((term(0) + term(1)) + term(2)) + term(3)
        swap_l, swap_r = [], []
        for a in range(n_arr):
            rows = parts[a].shape[1]
            mine = outs[a].at[pl.ds(c * rows, rows), :]
            swap_l.append(pltpu.make_async_copy(sums[a], mine, swap_local.at[a]))
            swap_r.append(pltpu.make_async_remote_copy(
                src_ref=sums[a], dst_ref=mine, send_sem=swap_send.at[a], recv_sem=swap_recv.at[a],
                device_id=sibling, device_id_type=MESH))
        for cp in swap_l + swap_r:
            cp.start()
        for a in range(n_arr):
            rows = parts[a].shape[1]
            theirs = outs[a].at[pl.ds((1 - c) * rows, rows), :]
            pltpu.make_async_remote_copy(
                src_ref=theirs, dst_ref=theirs, send_sem=swap_send.at[a], recv_sem=swap_recv.at[a],
                device_id=sibling, device_id_type=MESH).wait_recv()
        for cp in swap_r:
            cp.wait_send()
        for cp in swap_l:
            cp.wait()

    vmem = pl.BlockSpec(memory_space=pltpu.VMEM)
    return pl.pallas_call(
        kern, name="chip_sum_swap",
        in_specs=[vmem] * (2 * n_arr),
        out_specs=[vmem] * n_arr,
        out_shape=[jax.ShapeDtypeStruct((2 * p.shape[1], p.shape[2]), F32) for p in parts],
        scratch_shapes=[pltpu.VMEM(p.shape[1:], F32) for p in parts]
        + [pltpu.SemaphoreType.DMA((n_arr,)),
           pltpu.SemaphoreType.DMA((n_arr,)),
           pltpu.SemaphoreType.DMA((n_arr,))],
        compiler_params=_cparams(),
    )(*parts, *lands)


def _small_allreduce_adamw(partials, params, moms, vels):
    chunks = D_MODEL // LANES
    row_rb, row_bf, row_sk, row_loss = 2 * chunks, 2 * chunks + NUM_BUCKETS, 2 * chunks + NUM_BUCKETS + 1, SMALL_ROWS - 6

    def kern(gbf_ref, grb_ref, gsk_ref, gg_ref, gb_ref, loss_ref, *refs):
        p_refs, m_refs, v_refs = refs[0:5], refs[5:10], refs[10:15]
        lo_ref, g_outs, d_outs, mo_outs, vo_outs = refs[15], refs[16:21], refs[21:26], refs[26:31], refs[31:36]
        send_ref, buf_ref, send_sems, recv_sems = refs[36:]
        x, y, c, _ = _position()
        me = 4 * x + 2 * y + c
        send_ref[...] = jnp.zeros_like(send_ref)
        for r in range(chunks):
            send_ref[r:r + 1, :] = gg_ref[0:1, r * LANES:(r + 1) * LANES]
            send_ref[chunks + r:chunks + r + 1, :] = gb_ref[0:1, r * LANES:(r + 1) * LANES]
        send_ref[row_rb:row_rb + NUM_BUCKETS, :] = grb_ref[...]
        send_ref[row_bf:row_bf + 1, :] = gbf_ref[0:1, :]
        send_ref[row_sk:row_sk + 1, :] = gsk_ref[0:1, :]
        send_ref[row_loss:row_loss + 1, :] = loss_ref[0:1, :]
        buf_ref[me] = send_ref[...]
        peers = [(x, y, 1 - c)] + [(px, py, pc) for px, py in _position()[3] for pc in (c, 1 - c)]
        sends = []
        for k, peer in enumerate(peers):
            sends.append(pltpu.make_async_remote_copy(
                src_ref=send_ref, dst_ref=buf_ref.at[me], send_sem=send_sems.at[k], recv_sem=recv_sems.at[k],
                device_id=peer, device_id_type=MESH))
        for cp in sends:
            cp.start()
        for k, (px, py, pc) in enumerate(peers):
            slot = buf_ref.at[4 * px + 2 * py + pc]
            pltpu.make_async_remote_copy(
                src_ref=slot, dst_ref=slot, send_sem=send_sems.at[k], recv_sem=recv_sems.at[k],
                device_id=(px, py, pc), device_id_type=MESH).wait_recv()
        for cp in sends:
            cp.wait_send()
        tot = buf_ref[0]
        for d in range(1, N_DEV):
            tot = tot + buf_ref[d]
        lo_ref[...] = tot[row_loss:row_loss + 1, :]
        grads = [tot[row_bf:row_bf + 1, 0:FOX_HEADS],
                 tot[row_rb:row_rb + NUM_BUCKETS, 0:SWA_HEADS],
                 tot[row_sk:row_sk + 1, 0:SWA_HEADS],
                 jnp.concatenate([tot[r:r + 1, :] for r in range(chunks)], axis=1),
                 jnp.concatenate([tot[chunks + r:chunks + r + 1, :] for r in range(chunks)], axis=1)]
        for i, g in enumerate(grads):
            g_outs[i][...] = g
            delta, mn, vn = _adamw_math(p_refs[i][...], g, m_refs[i][...], v_refs[i][...])
            d_outs[i][...] = delta
            mo_outs[i][...] = mn
            vo_outs[i][...] = vn

    vm = pl.BlockSpec(memory_space=pltpu.VMEM)
    shapes = [jax.ShapeDtypeStruct(p.shape, F32) for p in params]
    outs = pl.pallas_call(
        kern, name="small_allreduce_adamw",
        in_specs=[vm] * 21,
        out_specs=[vm] * 21,
        out_shape=[jax.ShapeDtypeStruct((1, LANES), F32)] + shapes * 4,
        scratch_shapes=[pltpu.VMEM((SMALL_ROWS, LANES), F32),
                        pltpu.VMEM((N_DEV, SMALL_ROWS, LANES), F32),
                        pltpu.SemaphoreType.DMA((N_DEV - 1,)),
                        pltpu.SemaphoreType.DMA((N_DEV - 1,))],
    )(*partials, *params, *moms, *vels)
    return outs[0], outs[1:6], outs[6:11], outs[11:16], outs[16:21]


def _to_padded_cols(w):
    pad = jnp.zeros((w.shape[0], N_C - FOX_HEADS), w.dtype)
    return jnp.concatenate([w[:, 0:1536], w[:, 2056:2824], w[:, 1536:1544], pad,
                            w[:, 1544:2056], w[:, 2824:3336]], axis=1)


_COLUMN_RUNS = ((0, 0, 1536), (1536, OFF_C, FOX_HEADS), (1544, OFF_B, FOX_W), (2056, 1536, N_A - 1536),
                (2824, OFF_B + FOX_W, SWA_W))


def _shards_from_padded(g):
    shard_cols = D_IN // N_CHIPS
    shards = []
    for j in range(N_CHIPS):
        lo, hi = j * shard_cols, (j + 1) * shard_cols
        pieces = []
        for ref0, pad0, width in _COLUMN_RUNS:
            a, b = max(lo, ref0), min(hi, ref0 + width)
            if a < b:
                pieces.append(g[:, pad0 + a - ref0:pad0 + b - ref0])
        pieces.append(jnp.zeros((g.shape[0], SHARD_PAD - shard_cols), g.dtype))
        shards.append(jnp.concatenate(pieces, axis=1))
    return jnp.stack(shards)


def _fox_rows(a):
    return a[:, :FOX_HEADS].T.reshape(FOX_HEADS, 1, a.shape[0])


def kernel(x, w_in, b_f, rel_bias, sink, w_o, ln_g, ln_b, loss_target, m_w_in, m_b_f, m_rel_bias, m_sink, m_w_o, m_ln_g, m_ln_b, v_w_in, v_b_f, v_rel_bias, v_sink, v_w_o, v_ln_g, v_ln_b):
    x2 = x[0]
    tgt = loss_target[0]
    s = x2.shape[0]
    w_in2, w_o2 = w_in[0], w_o[0]

    shard_cols = D_IN // N_CHIPS
    col_pad = ((0, 0), (0, SHARD_PAD - shard_cols))
    (w_in_all,) = _gather_weights(jnp.pad(w_in2.astype(BF16), col_pad))
    w_full = jnp.concatenate([w_in_all[j, :, :shard_cols] for j in range(N_CHIPS)], axis=1)
    w_pad = _to_padded_cols(w_full)
    w_o_bf, _ = lax.optimization_barrier((w_o2.astype(BF16), w_in_all))
    wo_send, wo_recv, wo_src, wo_land, wo_token = _wo_gather_start(w_o_bf)

    qkv, ffp, z, xt, vt = _project(x2, w_pad, wo_token)
    bfp = jnp.pad(b_f, ((0, 0), (0, LANES - FOX_HEADS)))
    cum = _cum_fwd(ffp, bfp)
    cum_t3 = _fox_rows(cum)
    o_fox, lse_t3 = _fox_fwd(qkv, vt, cum_t3, cum)
    bucket_t = jnp.asarray(_bucket_table().T)
    bias_t = _swa_bias(rel_bias, bucket_t)
    sink_rows = jnp.repeat(sink.reshape(SWA_KV_HEADS, SWA_GROUP, 1), BLOCK, axis=2).reshape(SWA_KV_HEADS, 1, SWA_LANES)
    o_swa, lse_swa = _swa_fwd(qkv, bias_t, sink_rows)

    wo_land = _wo_gather_wait(wo_send, wo_recv, wo_src, wo_land, o_swa)
    my_chip = _chip_index(lax.axis_index("x"), lax.axis_index("y"))
    w_o_full = lax.dynamic_update_slice(wo_land, w_o_bf[None], (my_chip, 0, 0)).reshape(D_MODEL, D_MODEL)
    loss8, dh, grad_w_o_full, do_bf, dz, delta, gg8, gb8 = _post(
        x2, tgt, o_fox, o_swa, z, w_o_full, ln_g, ln_b)

    delta_t3 = _fox_rows(delta)
    dq_fox, dk_fox, dv_fox, dcum_k, dcum_q = _fox_bwd(qkv, do_bf, cum_t3, cum, lse_t3, delta_t3)
    dcum_q = jnp.pad(dcum_q.reshape(FOX_HEADS, s).T, ((0, 0), (0, LANES - FOX_HEADS)))
    dff, gbf8 = _cum_bwd(dcum_k, dcum_q, ffp, bfp)
    delta_rows = (delta[:, FOX_HEADS:FOX_HEADS + SWA_HEADS].reshape(s // BLOCK, BLOCK, SWA_KV_HEADS, SWA_GROUP)
                  .transpose(0, 2, 3, 1).reshape(s // BLOCK, SWA_KV_HEADS, 1, SWA_LANES))
    dq_swa, dk_swa, dv_swa, grb, gsk8 = _swa_bwd(qkv, do_bf, delta_rows, lse_swa, bias_t, sink_rows, bucket_t)

    d_misc = jnp.concatenate([dk_swa, dv_swa, dff], axis=1)
    pieces = [dq_fox, dk_fox, dv_fox, dq_swa, d_misc, dz]
    blocks = [(p, 0) for p in pieces[:-1]] + [(dz, 0), (dz, 1)]
    grad_w_pad = _grad_w_matmul(xt, blocks, tk=1024, name="grad_w_in")

    g_in4 = _shards_from_padded(grad_w_pad)
    g_o4 = grad_w_o_full.reshape(N_CHIPS, D_MODEL // N_CHIPS, D_MODEL)
    parts = _pair_reduce([g_in4, g_o4])
    send_sems, recv_sems, parts_thru, lands_thru, token = _scatter_start(parts)
    grad_x = _grad_x_matmul(pieces, w_pad, dh, token, tm=512, tn=D_MODEL, name="grad_x")
    parts, lands = _scatter_wait(send_sems, recv_sems, parts_thru, lands_thru, grad_x)
    g_w_in, g_w_o = _chip_sum_swap(parts, lands)
    g_w_in = g_w_in[:, :shard_cols]

    cols_first = lambda a: jnp.transpose(a, (2, 0, 1))
    rows_first = lambda a: jnp.transpose(a, (1, 2, 0))
    g_w_in, d_w_in, nm_w_in, nv_w_in = [rows_first(a) for a in _adamw_cols(
        cols_first(w_in), cols_first(g_w_in[None]), cols_first(m_w_in), cols_first(v_w_in), name="adamw_w_in")]
    d_w_o, nm_w_o, nv_w_o = _adamw(w_o2, g_w_o, m_w_o[0], v_w_o[0], name="adamw_w_o")

    loss_row, gs, ds, ms, vs = _small_allreduce_adamw(
        [gbf8, grb, gsk8, gg8, gb8, loss8],
        [b_f, rel_bias, sink, ln_g, ln_b],
        [m_b_f, m_rel_bias, m_sink, m_ln_g, m_ln_b],
        [v_b_f, v_rel_bias, v_sink, v_ln_g, v_ln_b])
    loss = loss_row[0, 0]
    g_bf, g_rb, g_sk, g_lg, g_lb = gs
    d_bf, d_rb, d_sk, d_lg, d_lb = ds
    m_bf, m_rb, m_sk, m_lg, m_lb = ms
    v_bf, v_rb, v_sk, v_lg, v_lb = vs

    e = lambda a: a[None]
    return (loss, e(grad_x),
            g_w_in, g_bf, g_rb, g_sk, e(g_w_o), g_lg, g_lb,
            d_w_in, d_bf, d_rb, d_sk, e(d_w_o), d_lg, d_lb,
            nm_w_in, m_bf, m_rb, m_sk, e(nm_w_o), m_lg, m_lb,
            nv_w_in, v_bf, v_rb, v_sk, e(nv_w_o), v_lg, v_lb)
```

```python
import functools
import math

import numpy as np
import jax
import jax.numpy as jnp
from jax import lax
from jax.experimental import pallas as pl
from jax.experimental.pallas import tpu as pltpu

F32 = jnp.float32
BF16 = jnp.bfloat16

D_MODEL = 1024
HEAD_DIM = 64
FOX_HEADS = 8
SWA_HEADS = 8
SWA_KV_HEADS = 2
SWA_GROUP = 4
FOX_W = 512
SWA_W = 512
BLOCK = 128
NUM_BUCKETS = 32
MAX_DISTANCE = 128
LN_EPS = 1e-5
NEG = -1e30
ALPHA = 2.0 ** 0.25
QK_SCALE = 0.125

ADAM_LR = 0.001
ADAM_B1 = 0.9
ADAM_B2 = 0.999
ADAM_EPS = 1e-08
ADAM_WD = 0.01
ADAM_STEP = 10

D_IN = 3336
SHARD_PAD = 896
N_A = 2304
N_C = 256
N_B = 1024
OFF_C = N_A
OFF_B = N_A + N_C
N_PAD = N_A + N_C + N_B
COL_FK, COL_FV, COL_SQ, COL_SK, COL_SV = 512, 1024, 1536, 2048, 2176

LANES = 128
FOX_T = 256
FOX_REF = 512
SUM_ROWS = 16
VMEM_LIMIT = 56 * 1024 * 1024

MESH = pl.DeviceIdType.MESH
N_CHIPS = 4
N_DEV = 8
SMALL_ROWS = 56


def _cparams(sem=None):
    return pltpu.CompilerParams(dimension_semantics=sem, vmem_limit_bytes=VMEM_LIMIT)


def _split3(x):
    hi = x.astype(BF16)
    r = x - hi.astype(F32)
    mid = r.astype(BF16)
    lo = (r - mid.astype(F32)).astype(BF16)
    return hi, mid, lo


def _dot(a, b):
    return jnp.dot(a, b, preferred_element_type=F32)


def _dot_nt(a, b):
    return lax.dot_general(a, b, (((1,), (1,)), ((), ())), preferred_element_type=F32)


def _project(x, w_pad, bfp, token):
    s, k = x.shape
    tm = min(512, s)
    chunk = 512

    def kern(x_ref, w_ref, b_ref, _, qkv_ref, ff_ref, z_ref, xt_ref, vt_ref, cum_ref, carry_ref):
        @pl.when(pl.program_id(0) == 0)
        def _():
            carry_ref[...] = jnp.zeros_like(carry_ref)
        xf = x_ref[...]
        xb = xf.astype(BF16)
        xt_ref[...] = xf.T.astype(BF16)
        for c0 in range(0, N_A, chunk):
            width = min(chunk, N_A - c0)
            res = _dot(xb, w_ref[:, c0:c0 + width])
            qkv_ref[:, c0:c0 + width] = res.astype(BF16)
            if c0 == COL_FV:
                vt_ref[...] = res.T.astype(BF16)
        ff = _dot(xb, w_ref[:, OFF_C:OFF_C + N_C])
        ff_ref[...] = ff
        lane = lax.broadcasted_iota(jnp.int32, (1, LANES), 1)
        lf = jnp.where(lane < FOX_HEADS, _log_sigmoid(ff[:, :LANES] + b_ref[...]), 0.0)
        cum = _exact_dot(_tri(tm, True), lf, True) + carry_ref[0:1, :]
        cum_ref[...] = cum
        carry_ref[...] = jnp.broadcast_to(cum[tm - 1:tm, :], carry_ref.shape)
        for c0 in range(0, N_B, 512):
            z_ref[:, c0:c0 + 512] = _dot(xb, w_ref[:, OFF_B + c0:OFF_B + c0 + 512])

    row = lambda i: (i, 0)
    return pl.pallas_call(
        kern, name="project",
        grid=(s // tm,),
        in_specs=[pl.BlockSpec((tm, k), row),
                  _resident((k, N_PAD), lambda i: (0, 0)),
                  _resident((1, LANES), lambda i: (0, 0)),
                  _resident(token.shape, lambda i: (0, 0))],
        out_specs=[pl.BlockSpec((tm, N_A), row),
                   pl.BlockSpec((tm, N_C), row),
                   pl.BlockSpec((tm, N_B), row),
                   pl.BlockSpec((k, tm), lambda i: (0, i)),
                   pl.BlockSpec((FOX_W, tm), lambda i: (0, i)),
                   pl.BlockSpec((tm, LANES), row)],
        out_shape=[jax.ShapeDtypeStruct((s, N_A), BF16),
                   jax.ShapeDtypeStruct((s, N_C), F32),
                   jax.ShapeDtypeStruct((s, N_B), F32),
                   jax.ShapeDtypeStruct((k, s), BF16),
                   jax.ShapeDtypeStruct((FOX_W, s), BF16),
                   jax.ShapeDtypeStruct((s, LANES), F32)],
        scratch_shapes=[pltpu.VMEM((8, LANES), F32)],
        compiler_params=_cparams(("arbitrary",)),
    )(x, w_pad, bfp, token)


def _grad_x_matmul(pieces, w_pad, dh, token, *, tm, tn, name):
    m = dh.shape[0]
    n, k = w_pad.shape
    widths = [p.shape[1] for p in pieces]
    offs = [sum(widths[:i]) for i in range(len(pieces))]
    assert sum(widths) == k

    def kern(*refs):
        p_refs, (b_ref, dh_ref, _, o_ref) = refs[:len(pieces)], refs[len(pieces):]
        acc = ALPHA * dh_ref[...]
        for p_ref, off, width in zip(p_refs, offs, widths):
            acc = acc + _dot_nt(p_ref[...], b_ref[:, off:off + width])
        o_ref[...] = acc

    assert tn == n
    return pl.pallas_call(
        kern, name=name,
        grid=(m // tm,),
        in_specs=[pl.BlockSpec((tm, w), lambda i: (i, 0)) for w in widths]
        + [_resident((n, k), lambda i: (0, 0)),
           pl.BlockSpec((tm, n), lambda i: (i, 0)),
           _resident(token.shape, lambda i: (0, 0))],
        out_specs=pl.BlockSpec((tm, n), lambda i: (i, 0)),
        out_shape=jax.ShapeDtypeStruct((m, n), F32),
        compiler_params=_cparams(("parallel",)),
    )(*pieces, w_pad, dh, token)


def _grad_w_matmul(xt, blocks, *, tk, name):
    m, s = xt.shape
    tn = 512
    nb = len(blocks)

    def kern(a_ref, *refs):
        b_refs, o_ref = refs[:nb], refs[nb]

        @pl.when(pl.program_id(0) == 0)
        def _():
            o_ref[...] = jnp.zeros_like(o_ref)
        a = a_ref[...]
        for blk in range(nb):
            o_ref[:, blk * tn:(blk + 1) * tn] += _dot(a, b_refs[blk][...])

    return pl.pallas_call(
        kern, name=name,
        grid=(s // tk,),
        in_specs=[pl.BlockSpec((m, tk), lambda k: (0, k))]
        + [pl.BlockSpec((tk, tn), functools.partial(lambda k, col: (k, col), col=col)) for _, col in blocks],
        out_specs=_resident((m, nb * tn), lambda k: (0, 0)),
        out_shape=jax.ShapeDtypeStruct((m, nb * tn), F32),
        compiler_params=_cparams(("arbitrary",)),
    )(xt, *[arr for arr, _ in blocks])


def _tri(n, lower):
    r = lax.broadcasted_iota(jnp.int32, (n, n), 0)
    c = lax.broadcasted_iota(jnp.int32, (n, n), 1)
    keep = (c <= r) if lower else (c >= r)
    return jnp.where(keep, 1.0, 0.0).astype(BF16)


def _exact_dot(mat_bf16, x_f32, left):
    out = None
    for piece in _split3(x_f32):
        t = _dot(mat_bf16, piece) if left else _dot(piece, mat_bf16)
        out = t if out is None else out + t
    return out


def _log_sigmoid(z):
    return jnp.minimum(z, 0.0) - jnp.log(1.0 + jnp.exp(-jnp.abs(z)))


def _cum_bwd(dcum_k, dcum_q, ffp, bfp):
    s = dcum_k.shape[0]
    t = min(1024, s)
    nb = s // t

    def kern(dck_ref, dcq_ref, ff_ref, b_ref, dff_ref, gb_ref, carry_ref):
        @pl.when(pl.program_id(0) == 0)
        def _():
            carry_ref[...] = jnp.zeros_like(carry_ref)
            gb_ref[...] = jnp.zeros_like(gb_ref)
        lane = lax.broadcasted_iota(jnp.int32, (1, LANES), 1)
        dlf = _exact_dot(_tri(t, False), dck_ref[...] + dcq_ref[...], True) + carry_ref[0:1, :]
        carry_ref[...] = jnp.broadcast_to(dlf[0:1, :], carry_ref.shape)
        z = ff_ref[...] + b_ref[...]
        dff = jnp.where(lane < FOX_HEADS, dlf / (1.0 + jnp.exp(z)), 0.0)
        gb_ref[...] += jnp.broadcast_to(jnp.sum(dff, axis=0, keepdims=True), gb_ref.shape)
        dff_ref[...] = jnp.concatenate([dff, jnp.zeros_like(dff)], axis=1).astype(BF16)

    return pl.pallas_call(
        kern, name="cum_bwd",
        grid=(nb,),
        in_specs=[pl.BlockSpec((t, LANES), lambda i: (nb - 1 - i, 0)),
                  pl.BlockSpec((t, LANES), lambda i: (nb - 1 - i, 0)),
                  pl.BlockSpec((t, LANES), lambda i: (nb - 1 - i, 0)),
                  pl.BlockSpec((1, LANES), lambda i: (0, 0))],
        out_specs=[pl.BlockSpec((t, N_C), lambda i: (nb - 1 - i, 0)),
                   pl.BlockSpec((8, LANES), lambda i: (0, 0))],
        out_shape=[jax.ShapeDtypeStruct((s, N_C), BF16),
                   jax.ShapeDtypeStruct((8, LANES), F32)],
        scratch_shapes=[pltpu.VMEM((8, LANES), F32)],
        compiler_params=_cparams(("arbitrary",)),
    )(dcum_k, dcum_q, ffp, bfp)


def _resident(shape, index_map):
    return pl.BlockSpec(shape, index_map, pipeline_mode=pl.Buffered(1))


def _fox_fwd(qkv, vt, cum_t3, cum):
    s = qkv.shape[0]
    tk = tq = FOX_REF
    nq = s // tq
    nh = FOX_HEADS
    diag_tiles = tq // tk

    def kern(q_ref, k_ref, vt_ref, ct_ref, c_ref, o_ref, lse_ref, m_ref, acc_ref, u_ref):
        i = pl.program_id(0)
        lane = lax.broadcasted_iota(jnp.int32, (1, LANES), 1)
        krow = lax.broadcasted_iota(jnp.int32, (tk, tq), 0)
        qcol = lax.broadcasted_iota(jnp.int32, (tk, tq), 1)
        q0 = pl.multiple_of(i * tq, tq)
        qts, crefs = [], []
        for h in range(nh):
            p, a = divmod(h, 2)
            q2 = q_ref[:, p * LANES:(p + 1) * LANES] * jnp.asarray(QK_SCALE, BF16)
            sel = (lane < HEAD_DIM) if a == 0 else (lane >= HEAD_DIM)
            qts.append(jnp.where(sel, q2, jnp.zeros_like(q2)).astype(F32).T.astype(BF16))
            crefs.append(ct_ref[h, :, pl.ds(q0, LANES)][:, 0:1])
        m_ref[...] = jnp.full(m_ref.shape, NEG, F32)
        acc_ref[...] = jnp.zeros_like(acc_ref)
        ones = jnp.ones((SUM_ROWS, tk), BF16)

        def tile(j, diag):
            k0 = pl.multiple_of(j * tk, tk)
            cb = c_ref[pl.ds(k0, tk), :]
            sts = [_dot(k_ref[pl.ds(k0, tk), (h // 2) * LANES:(h // 2 + 1) * LANES], qts[h]) for h in range(nh)]
            tile_max = []
            for h in range(nh):
                u = sts[h] - (cb[:, h:h + 1] - crefs[h])
                if diag is not None:
                    u = jnp.where(krow + diag * tk <= qcol, u, NEG)
                u_ref[h] = u
                tile_max.append(jnp.max(u, axis=0, keepdims=True))
            pts, scales = [], []
            for h in range(nh):
                m_old = m_ref[h]
                m_new = jnp.maximum(m_old, tile_max[h])
                scales.append(jnp.exp(m_old - m_new))
                pts.append(jnp.exp(u_ref[h] - m_new).astype(BF16))
                m_ref[h] = m_new
            for h in range(nh):
                vth = jnp.concatenate([vt_ref[h * HEAD_DIM:(h + 1) * HEAD_DIM, pl.ds(k0, tk)], ones], axis=0)
                acc_ref[h] = scales[h] * acc_ref[h] + _dot(vth, pts[h])

        def body(j, c):
            tile(j, None)
            return c
        lax.fori_loop(0, i * diag_tiles, body, 0)
        for d in range(diag_tiles):
            tile(i * diag_tiles + d, d)

        ls = [acc_ref[h][HEAD_DIM:HEAD_DIM + 1] for h in range(nh)]
        for p in range(nh // 2):
            ot = jnp.concatenate([acc_ref[2 * p + a][:HEAD_DIM] * (1.0 / ls[2 * p + a]) for a in range(2)], axis=0)
            o_ref[:, p * LANES:(p + 1) * LANES] = ot.T
        for h in range(nh):
            lse_ref[h, :, pl.ds(q0, tq)] = m_ref[h] + jnp.log(ls[h])

    return pl.pallas_call(
        kern, name="fox_fwd",
        grid=(nq,),
        in_specs=[pl.BlockSpec((tq, FOX_W), lambda i: (i, 0)),
                  _resident((s, FOX_W), lambda i: (0, COL_FK // FOX_W)),
                  _resident((FOX_W, s), lambda i: (0, 0)),
                  _resident((nh, 1, s), lambda i: (0, 0, 0)),
                  _resident((s, LANES), lambda i: (0, 0))],
        out_specs=[pl.BlockSpec((tq, FOX_W), lambda i: (i, 0)),
                   pl.BlockSpec((nh, 1, s), lambda i: (0, 0, 0))],
        out_shape=[jax.ShapeDtypeStruct((s, FOX_W), F32),
                   jax.ShapeDtypeStruct((nh, 1, s), F32)],
        scratch_shapes=[pltpu.VMEM((nh, 1, tq), F32),
                        pltpu.VMEM((nh, HEAD_DIM + SUM_ROWS, tq), F32),
                        pltpu.VMEM((nh, tk, tq), F32)],
        compiler_params=_cparams(("arbitrary",)),
    )(qkv, qkv, vt, cum_t3, cum)


def _fox_bwd(qkv, do_bf, cum_t3, cum, lse_t3, delta_t3):
    s = qkv.shape[0]
    t = min(FOX_T, s)
    nq = s // t
    nh = FOX_HEADS
    npair = nh // 2

    def kern(q_ref, do_ref, k_ref, v_ref, ct_ref, c_ref, lse_ref, dl_ref,
             dq_ref, dk_ref, dv_ref, dc_ref, dcq_ref, dqt_ref, accv_ref, acck_ref, accd_ref):
        kj = pl.program_id(0)
        lane = lax.broadcasted_iota(jnp.int32, (1, LANES), 1)
        krow = lax.broadcasted_iota(jnp.int32, (t, t), 0)
        qcol = lax.broadcasted_iota(jnp.int32, (t, t), 1)
        causal = krow <= qcol
        sels = [lane < HEAD_DIM, lane >= HEAD_DIM]

        @pl.when(kj == 0)
        def _():
            dqt_ref[...] = jnp.zeros_like(dqt_ref)
            dcq_ref[...] = jnp.zeros_like(dcq_ref)

        cb = c_ref[...]
        k2s, v2s, kts = [], [], []
        for p in range(npair):
            k2 = k_ref[:, p * LANES:(p + 1) * LANES]
            k2s.append(k2)
            v2s.append(v_ref[:, p * LANES:(p + 1) * LANES])
            kt = k2.astype(F32).T * QK_SCALE
            kts.append(kt[:HEAD_DIM].astype(BF16))
            kts.append(kt[HEAD_DIM:].astype(BF16))
        css = [cb[:, h:h + 1] for h in range(nh)]

        def tile(i, masked):
            q0 = pl.multiple_of(i * t, t)
            r0 = pl.multiple_of((i // (FOX_REF // t)) * FOX_REF, FOX_REF)
            sts, dpts, qms, doms = [], [], [], []
            for h in range(nh):
                p, a = divmod(h, 2)
                qi = q_ref[pl.ds(q0, t), p * LANES:(p + 1) * LANES] * jnp.asarray(QK_SCALE, BF16)
                doi = do_ref[pl.ds(q0, t), p * LANES:(p + 1) * LANES]
                qm = jnp.where(sels[a], qi, jnp.zeros_like(qi))
                dom = jnp.where(sels[a], doi, jnp.zeros_like(doi))
                qms.append(qm)
                doms.append(dom)
                sts.append(_dot_nt(k2s[p], qm))
                dpts.append(_dot_nt(v2s[p], dom))
            pts, dsts = [], []
            for h in range(nh):
                cref = ct_ref[h, :, pl.ds(r0, LANES)][:, 0:1]
                pt = jnp.exp(sts[h] - (css[h] - cref) - lse_ref[h, :, pl.ds(q0, t)])
                if masked:
                    pt = jnp.where(causal, pt, 0.0)
                ds32 = pt * (dpts[h] - dl_ref[h, :, pl.ds(q0, t)])
                part = ds32[:, 0:LANES]
                for c in range(1, t // LANES):
                    part = part + ds32[:, c * LANES:(c + 1) * LANES]
                accd_ref[h] = part if masked else accd_ref[h] + part
                dcq_ref[h, :, pl.ds(q0, t)] += jnp.sum(ds32, axis=0, keepdims=True)
                pts.append(pt.astype(BF16))
                dsts.append(ds32.astype(BF16))
            for p in range(npair):
                ha, hb = 2 * p, 2 * p + 1
                dv_p = _dot(pts[ha], doms[ha]) + _dot(pts[hb], doms[hb])
                dk_p = _dot(dsts[ha], qms[ha]) + _dot(dsts[hb], qms[hb])
                accv_ref[p] = dv_p if masked else accv_ref[p] + dv_p
                acck_ref[p] = dk_p if masked else acck_ref[p] + dk_p
            for h in range(nh):
                dqt_ref[h * HEAD_DIM:(h + 1) * HEAD_DIM, pl.ds(q0, t)] += _dot(kts[h], dsts[h])

        tile(kj, True)

        def body(i, c):
            tile(i, False)
            return c
        lax.fori_loop(kj + 1, nq, body, 0)

        dc = jnp.zeros((t, LANES), F32)
        for h in range(nh):
            dc = jnp.where(lane == h, -jnp.sum(accd_ref[h], axis=1, keepdims=True), dc)
        dc_ref[...] = dc
        for p in range(npair):
            dv_ref[:, p * LANES:(p + 1) * LANES] = accv_ref[p].astype(BF16)
            dk_ref[:, p * LANES:(p + 1) * LANES] = acck_ref[p].astype(BF16)

        dq_ref[...] = dqt_ref[:, pl.ds(pl.multiple_of(kj * t, t), t)].T.astype(BF16)

    whole = lambda kj: (0, 0, 0)
    return pl.pallas_call(
        kern, name="fox_bwd",
        grid=(nq,),
        in_specs=[_resident((s, FOX_W), lambda kj: (0, 0)),
                  _resident((s, FOX_W), lambda kj: (0, 0)),
                  pl.BlockSpec((t, FOX_W), lambda kj: (kj, COL_FK // FOX_W)),
                  pl.BlockSpec((t, FOX_W), lambda kj: (kj, COL_FV // FOX_W)),
                  _resident((nh, 1, s), whole),
                  pl.BlockSpec((t, LANES), lambda kj: (kj, 0)),
                  _resident((nh, 1, s), whole),
                  _resident((nh, 1, s), whole)],
        out_specs=[pl.BlockSpec((t, FOX_W), lambda kj: (kj, 0)),
                   pl.BlockSpec((t, FOX_W), lambda kj: (kj, 0)),
                   pl.BlockSpec((t, FOX_W), lambda kj: (kj, 0)),
                   pl.BlockSpec((t, LANES), lambda kj: (kj, 0)),
                   _resident((nh, 1, s), whole)],
        out_shape=[jax.ShapeDtypeStruct((s, FOX_W), BF16),
                   jax.ShapeDtypeStruct((s, FOX_W), BF16),
                   jax.ShapeDtypeStruct((s, FOX_W), BF16),
                   jax.ShapeDtypeStruct((s, LANES), F32),
                   jax.ShapeDtypeStruct((nh, 1, s), F32)],
        scratch_shapes=[pltpu.VMEM((FOX_W, s), F32),
                        pltpu.VMEM((npair, t, LANES), F32),
                        pltpu.VMEM((npair, t, LANES), F32),
                        pltpu.VMEM((nh, t, LANES), F32)],
        compiler_params=_cparams(("arbitrary",)),
    )(qkv, do_bf, qkv, qkv, cum_t3, cum, lse_t3, delta_t3)


def _bucket_table():
    qi = np.arange(BLOCK)[:, None]
    kj = np.arange(2 * BLOCK)[None, :]
    rel = np.maximum(qi + BLOCK - kj, 0).astype(np.int32)
    max_exact = NUM_BUCKETS // 2
    relf = np.maximum(rel, 1).astype(np.float32)
    large = max_exact + (np.log(relf / np.float32(max_exact)) / np.float32(math.log(MAX_DISTANCE / max_exact))
                         * np.float32(NUM_BUCKETS - max_exact)).astype(np.int32)
    large = np.minimum(large, NUM_BUCKETS - 1)
    return np.where(rel < max_exact, rel, large).astype(np.int32)


SWA_LANES = SWA_GROUP * BLOCK


def _swa_bias(rel_bias, bucket_t):
    def kern(rb_ref, bk_ref, o_ref):
        bk = bk_ref[...]
        kj = lax.broadcasted_iota(jnp.int32, (2 * BLOCK, BLOCK), 0)
        qi = lax.broadcasted_iota(jnp.int32, (2 * BLOCK, BLOCK), 1)
        rel = qi + BLOCK - kj
        band = (rel >= 0) & (rel < BLOCK)
        masks = [band & (kj >= BLOCK), band]
        for h in range(SWA_HEADS):
            g, hh = divmod(h, SWA_GROUP)
            acc = jnp.zeros((2 * BLOCK, BLOCK), F32)
            for b in range(NUM_BUCKETS):
                acc = jnp.where(bk == b, rb_ref[b, h], acc)
            for first in range(2):
                o_ref[first, g, :, hh * BLOCK:(hh + 1) * BLOCK] = jnp.where(masks[first], acc, NEG)

    return pl.pallas_call(
        kern, name="swa_bias",
        in_specs=[pl.BlockSpec(memory_space=pltpu.SMEM),
                  pl.BlockSpec(memory_space=pltpu.VMEM)],
        out_specs=pl.BlockSpec(memory_space=pltpu.VMEM),
        out_shape=jax.ShapeDtypeStruct((2, SWA_KV_HEADS, 2 * BLOCK, SWA_LANES), F32),
        compiler_params=_cparams(),
    )(rel_bias, bucket_t)


SWA_STEP = 8


def _swa_keys(prev_ref, cur_ref):
    return jnp.concatenate([prev_ref[...], cur_ref[...]], axis=0)


def _swa_queries(x_ref, scale):
    x = x_ref[...]
    if scale:
        x = x * jnp.asarray(QK_SCALE, BF16)
    xt = x.astype(F32).T.astype(BF16)
    return [_group_rows(xt[:, b * BLOCK:(b + 1) * BLOCK]) for b in range(SWA_STEP)]


def _group_rows(xt):
    zeros = jnp.zeros((HEAD_DIM, SWA_LANES), BF16)
    out = []
    for g in range(SWA_KV_HEADS):
        heads = [xt[(SWA_GROUP * g + hh) * HEAD_DIM:(SWA_GROUP * g + hh + 1) * HEAD_DIM, :] for hh in range(SWA_GROUP)]
        rows = jnp.concatenate(heads, axis=1)
        padded = jnp.concatenate([rows, zeros] if g == 0 else [zeros, rows], axis=0)
        out.append((rows, padded))
    return out


def _pairs_to_rows(cols_t):
    out = []
    for p in range(SWA_HEADS // 2):
        g, hh = divmod(2 * p, SWA_GROUP)
        pair = jnp.concatenate([cols_t[g][:, hh * BLOCK:(hh + 1) * BLOCK],
                                cols_t[g][:, (hh + 1) * BLOCK:(hh + 2) * BLOCK]], axis=0)
        out.append(pair.T)
    return jnp.concatenate(out, axis=1)


def _swa_fwd(qkv, bias_t, sink_rows):
    s = qkv.shape[0]
    nb = s // BLOCK
    rows = SWA_STEP * BLOCK
    units = [(b, g) for b in range(SWA_STEP) for g in range(SWA_KV_HEADS)]

    def kern(q_ref, kp_ref, kc_ref, vp_ref, vc_ref, bias_ref, sink_ref, o_ref, lse_ref):
        n = pl.program_id(0)
        tables = [jnp.minimum(n, 1)] + [1] * (SWA_STEP - 1)
        k3 = _swa_keys(kp_ref, kc_ref)
        vt3 = _swa_keys(vp_ref, vc_ref).astype(F32).T.astype(BF16)
        qts = _swa_queries(q_ref, True)
        us = [_dot(k3[b * BLOCK:(b + 2) * BLOCK], qts[b][g][1]) + bias_ref[tables[b], g] for b, g in units]
        outs = []
        for (b, g), u in zip(units, us):
            sk = sink_ref[g]
            m = jnp.maximum(jnp.max(u, axis=0, keepdims=True), sk)
            p = jnp.exp(u - m)
            l = jnp.sum(p, axis=0, keepdims=True) + jnp.exp(sk - m)
            lse_ref[b, g] = m + jnp.log(l)
            vt = vt3[g * HEAD_DIM:(g + 1) * HEAD_DIM, b * BLOCK:(b + 2) * BLOCK]
            outs.append(_dot(vt, (p * (1.0 / l)).astype(BF16)))
        for b in range(SWA_STEP):
            o_ref[b * BLOCK:(b + 1) * BLOCK, :] = _pairs_to_rows(outs[b * SWA_KV_HEADS:(b + 1) * SWA_KV_HEADS])

    cq, ck, cv = COL_SQ // SWA_W, COL_SK // LANES, COL_SV // LANES
    prev = lambda n: jnp.maximum(SWA_STEP * n - 1, 0)
    return pl.pallas_call(
        kern, name="swa_fwd",
        grid=(nb // SWA_STEP,),
        in_specs=[pl.BlockSpec((rows, SWA_W), lambda n: (n, cq)),
                  pl.BlockSpec((BLOCK, LANES), lambda n: (prev(n), ck)),
                  pl.BlockSpec((rows, LANES), lambda n: (n, ck)),
                  pl.BlockSpec((BLOCK, LANES), lambda n: (prev(n), cv)),
                  pl.BlockSpec((rows, LANES), lambda n: (n, cv)),
                  _resident((2, SWA_KV_HEADS, 2 * BLOCK, SWA_LANES), lambda n: (0, 0, 0, 0)),
                  _resident((SWA_KV_HEADS, 1, SWA_LANES), lambda n: (0, 0, 0))],
        out_specs=[pl.BlockSpec((rows, SWA_W), lambda n: (n, 0)),
                   pl.BlockSpec((SWA_STEP, SWA_KV_HEADS, 1, SWA_LANES), lambda n: (n, 0, 0, 0))],
        out_shape=[jax.ShapeDtypeStruct((s, SWA_W), F32),
                   jax.ShapeDtypeStruct((nb, SWA_KV_HEADS, 1, SWA_LANES), F32)],
        compiler_params=_cparams(("parallel",)),
    )(qkv, qkv, qkv, qkv, qkv, bias_t, sink_rows)


def _swa_bwd(qkv, do_bf, delta_rows, lse, bias_t, sink_rows, bucket_t):
    s = qkv.shape[0]
    nb = s // BLOCK
    steps = nb // SWA_STEP
    rows = SWA_STEP * BLOCK
    units = [(b, g) for b in range(SWA_STEP) for g in range(SWA_KV_HEADS)]

    def kern(q_ref, kp_ref, kc_ref, vp_ref, vc_ref, do_ref, dl_ref, lse_ref, bias_ref, sink_ref, bk_ref,
             dq_ref, dk_ref, dv_ref, grb_ref, gsk_ref, dbias_ref, ck_ref, cv_ref, sk_ref):
        n = pl.program_id(0)

        @pl.when(n == 0)
        def _():
            dbias_ref[...] = jnp.zeros_like(dbias_ref)
            ck_ref[...] = jnp.zeros_like(ck_ref)
            cv_ref[...] = jnp.zeros_like(cv_ref)
            sk_ref[...] = jnp.zeros_like(sk_ref)

        @pl.when(n < steps)
        def _():
            tables = [jnp.minimum(n, 1)] + [1] * (SWA_STEP - 1)
            k3 = _swa_keys(kp_ref, kc_ref)
            v3 = _swa_keys(vp_ref, vc_ref)
            kt3 = (k3.astype(F32).T * QK_SCALE).astype(BF16)
            qts = _swa_queries(q_ref, True)
            dots = _swa_queries(do_ref, False)
            sts = [_dot(k3[b * BLOCK:(b + 2) * BLOCK], qts[b][g][1]) for b, g in units]
            dps = [_dot(v3[b * BLOCK:(b + 2) * BLOCK], dots[b][g][1]) for b, g in units]
            ps, dss = [], []
            for i, (b, g) in enumerate(units):
                lse_g = lse_ref[b, g]
                dlt = dl_ref[b, g]
                p = jnp.exp(sts[i] + bias_ref[tables[b], g] - lse_g)
                ds = p * (dps[i] - dlt)
                dbias_ref[g] += ds
                sk_ref[g] += -jnp.exp(sink_ref[g] - lse_g) * dlt
                ps.append(p.astype(BF16))
                dss.append(ds.astype(BF16))
            dk2, dv2 = [], []
            for b in range(SWA_STEP):
                at = lambda g: b * SWA_KV_HEADS + g
                groups = range(SWA_KV_HEADS)
                dv2.append(jnp.concatenate([_dot_nt(dots[b][g][0], ps[at(g)]) for g in groups], axis=0).T)
                dk2.append(jnp.concatenate([_dot_nt(qts[b][g][0], dss[at(g)]) for g in groups], axis=0).T)
                dqts = [_dot(kt3[g * HEAD_DIM:(g + 1) * HEAD_DIM, b * BLOCK:(b + 2) * BLOCK], dss[at(g)]) for g in groups]
                dq_ref[b * BLOCK:(b + 1) * BLOCK, :] = _pairs_to_rows(dqts).astype(BF16)
            last = (SWA_STEP - 1) * BLOCK
            for acc_ref, out_ref, parts in ((ck_ref, dk_ref, dk2), (cv_ref, dv_ref, dv2)):
                done = acc_ref[last:] + parts[0][:BLOCK]
                out_ref[...] = jnp.concatenate([acc_ref[:last], done], axis=0).astype(BF16)
                for b in range(SWA_STEP - 1):
                    acc_ref[b * BLOCK:(b + 1) * BLOCK] = parts[b][BLOCK:] + parts[b + 1][:BLOCK]
                acc_ref[last:] = parts[SWA_STEP - 1][BLOCK:]

        @pl.when(n == steps)
        def _():
            dk_ref[...] = ck_ref[...].astype(BF16)
            dv_ref[...] = cv_ref[...].astype(BF16)
            bk = bk_ref[...]
            lane = lax.broadcasted_iota(jnp.int32, (8, LANES), 1)
            rowi = lax.broadcasted_iota(jnp.int32, (NUM_BUCKETS, LANES), 0)
            lanei = lax.broadcasted_iota(jnp.int32, (NUM_BUCKETS, LANES), 1)
            out = jnp.zeros((NUM_BUCKETS, LANES), F32)
            gsk = jnp.zeros((8, LANES), F32)
            for h in range(SWA_HEADS):
                g, hh = divmod(h, SWA_GROUP)
                cols = slice(hh * BLOCK, (hh + 1) * BLOCK)
                gsk = jnp.where(lane == h, jnp.sum(sk_ref[g][:, cols]), gsk)
                db = dbias_ref[g][:, cols]
                for b in range(NUM_BUCKETS):
                    val = jnp.sum(jnp.where(bk == b, db, 0.0))
                    out = jnp.where((rowi == b) & (lanei == h), val, out)
            grb_ref[...] = out
            gsk_ref[...] = gsk

    cq, ck, cv = COL_SQ // SWA_W, COL_SK // LANES, COL_SV // LANES
    cur = lambda n: jnp.minimum(n, steps - 1)
    prev = lambda n: jnp.maximum(SWA_STEP * cur(n) - 1, 0)
    kout = lambda n: jnp.maximum(n - 1, 0)
    stat = pl.BlockSpec((SWA_STEP, SWA_KV_HEADS, 1, SWA_LANES), lambda n: (cur(n), 0, 0, 0))
    return pl.pallas_call(
        kern, name="swa_bwd",
        grid=(steps + 1,),
        in_specs=[pl.BlockSpec((rows, SWA_W), lambda n: (cur(n), cq)),
                  pl.BlockSpec((BLOCK, LANES), lambda n: (prev(n), ck)),
                  pl.BlockSpec((rows, LANES), lambda n: (cur(n), ck)),
                  pl.BlockSpec((BLOCK, LANES), lambda n: (prev(n), cv)),
                  pl.BlockSpec((rows, LANES), lambda n: (cur(n), cv)),
                  pl.BlockSpec((rows, SWA_W), lambda n: (cur(n), 1)),
                  stat, stat,
                  _resident((2, SWA_KV_HEADS, 2 * BLOCK, SWA_LANES), lambda n: (0, 0, 0, 0)),
                  _resident((SWA_KV_HEADS, 1, SWA_LANES), lambda n: (0, 0, 0)),
                  _resident((2 * BLOCK, BLOCK), lambda n: (0, 0))],
        out_specs=[pl.BlockSpec((rows, SWA_W), lambda n: (cur(n), 0)),
                   pl.BlockSpec((rows, LANES), lambda n: (kout(n), 0)),
                   pl.BlockSpec((rows, LANES), lambda n: (kout(n), 0)),
                   pl.BlockSpec((NUM_BUCKETS, LANES), lambda n: (0, 0)),
                   pl.BlockSpec((8, LANES), lambda n: (0, 0))],
        out_shape=[jax.ShapeDtypeStruct((s, SWA_W), BF16),
                   jax.ShapeDtypeStruct((s, LANES), BF16),
                   jax.ShapeDtypeStruct((s, LANES), BF16),
                   jax.ShapeDtypeStruct((NUM_BUCKETS, LANES), F32),
                   jax.ShapeDtypeStruct((8, LANES), F32)],
        scratch_shapes=[pltpu.VMEM((SWA_KV_HEADS, 2 * BLOCK, SWA_LANES), F32),
                        pltpu.VMEM((rows, LANES), F32),
                        pltpu.VMEM((rows, LANES), F32),
                        pltpu.VMEM((SWA_KV_HEADS, 1, SWA_LANES), F32)],
        compiler_params=_cparams(("arbitrary",)),
    )(qkv, qkv, qkv, qkv, qkv, do_bf, delta_rows, lse, bias_t, sink_rows, bucket_t)


def _post(x, target, o_fox, o_swa, z, w_o, ln_g, ln_b):
    s = x.shape[0]
    tm = min(256, s)
    nt = s // tm

    def kern(x_ref, t_ref, of_ref, os_ref, z_ref, w_ref, g_ref, b_ref,
             loss_ref, dh_ref, gwo_ref, do_ref, dz_ref, dl_ref, gg_ref, gb_ref, lacc_ref):
        step = pl.program_id(0)

        @pl.when(step == 0)
        def _():
            lacc_ref[...] = jnp.zeros_like(lacc_ref)
            gg_ref[...] = jnp.zeros_like(gg_ref)
            gwo_ref[...] = jnp.zeros_like(gwo_ref)
            gb_ref[...] = jnp.zeros_like(gb_ref)

        o = jnp.concatenate([of_ref[...], os_ref[...]], axis=1)
        zz = z_ref[...]
        sig = 1.0 / (1.0 + jnp.exp(-zz))
        silu = zz * sig
        mixed32 = o * silu
        mixed = mixed32.astype(BF16)
        w = w_ref[...]
        h = ALPHA * x_ref[...] + _dot(mixed, w)
        mu = jnp.mean(h, axis=1, keepdims=True)
        hc = h - mu
        var = jnp.mean(hc * hc, axis=1, keepdims=True)
        rstd = lax.rsqrt(var + LN_EPS)
        xhat = hc * rstd
        g = g_ref[...]
        err = xhat * g + b_ref[...] - t_ref[...]
        lacc_ref[...] += jnp.broadcast_to(jnp.sum(err * err, axis=0, keepdims=True), lacc_ref.shape)
        dout = err * (1.0 / D_MODEL)
        gg_ref[...] += jnp.broadcast_to(jnp.sum(dout * xhat, axis=0, keepdims=True), gg_ref.shape)
        gb_ref[...] += jnp.broadcast_to(jnp.sum(dout, axis=0, keepdims=True), gb_ref.shape)
        dxh = dout * g
        m1 = jnp.mean(dxh, axis=1, keepdims=True)
        m2 = jnp.mean(dxh * xhat, axis=1, keepdims=True)
        dh = rstd * (dxh - m1 - xhat * m2)
        dh_ref[...] = dh
        dy = dh.astype(BF16)
        gwo_ref[...] += _dot(mixed32.T.astype(BF16), dy)
        dmix = _dot_nt(dy, w)
        do = dmix * silu
        do_ref[...] = do.astype(BF16)
        dz_ref[...] = (dmix * o * (sig * (1.0 + zz * (1.0 - sig)))).astype(BF16)
        r = lax.broadcasted_iota(jnp.int32, (D_MODEL, LANES), 0) // HEAD_DIM
        c = lax.broadcasted_iota(jnp.int32, (D_MODEL, LANES), 1)
        pick = jnp.where(r == c, 1.0, 0.0).astype(BF16)
        dl_ref[...] = _exact_dot(pick, do * o, False)

        @pl.when(step == nt - 1)
        def _():
            tot = jnp.sum(lacc_ref[0:1, :]) * (0.5 / D_MODEL)
            loss_ref[...] = jnp.broadcast_to(tot, loss_ref.shape)

    row = lambda i: (i, 0)
    fixed = lambda i: (0, 0)
    wide = pl.BlockSpec((tm, D_MODEL), row)
    half = pl.BlockSpec((tm, FOX_W), row)
    return pl.pallas_call(
        kern, name="post",
        grid=(nt,),
        in_specs=[wide, wide, half, half, wide,
                  pl.BlockSpec((D_MODEL, D_MODEL), fixed),
                  pl.BlockSpec((1, D_MODEL), fixed),
                  pl.BlockSpec((1, D_MODEL), fixed)],
        out_specs=[pl.BlockSpec((8, LANES), fixed), wide,
                   _resident((D_MODEL, D_MODEL), fixed), wide, wide,
                   pl.BlockSpec((tm, LANES), row),
                   pl.BlockSpec((8, D_MODEL), fixed), pl.BlockSpec((8, D_MODEL), fixed)],
        out_shape=[jax.ShapeDtypeStruct((8, LANES), F32),
                   jax.ShapeDtypeStruct((s, D_MODEL), F32),
                   jax.ShapeDtypeStruct((D_MODEL, D_MODEL), F32),
                   jax.ShapeDtypeStruct((s, D_MODEL), BF16),
                   jax.ShapeDtypeStruct((s, D_MODEL), BF16),
                   jax.ShapeDtypeStruct((s, LANES), F32),
                   jax.ShapeDtypeStruct((8, D_MODEL), F32),
                   jax.ShapeDtypeStruct((8, D_MODEL), F32)],
        scratch_shapes=[pltpu.VMEM((8, D_MODEL), F32)],
        compiler_params=_cparams(("arbitrary",)),
    )(x, target, o_fox, o_swa, z, w_o, ln_g, ln_b)


def _adamw_math(w, g, m, v):
    m = ADAM_B1 * m + (1.0 - ADAM_B1) * g
    v = ADAM_B2 * v + (1.0 - ADAM_B2) * (g * g)
    m_hat = m / (1.0 - ADAM_B1 ** ADAM_STEP)
    v_hat = v / (1.0 - ADAM_B2 ** ADAM_STEP)
    delta = -ADAM_LR * (m_hat / (jnp.sqrt(v_hat) + ADAM_EPS) + ADAM_WD * w)
    return delta, m, v


def _adamw(w, g, m, v, *, name):
    r, c = w.shape
    tr = min(256, r)

    def kern(w_ref, g_ref, m_ref, v_ref, d_ref, mo_ref, vo_ref):
        d, mn, vn = _adamw_math(w_ref[...], g_ref[...], m_ref[...], v_ref[...])
        d_ref[...] = d
        mo_ref[...] = mn
        vo_ref[...] = vn

    blk = pl.BlockSpec((tr, c), lambda i: (i, 0))
    sds = jax.ShapeDtypeStruct((r, c), F32)
    return pl.pallas_call(
        kern, name=name,
        grid=(r // tr,),
        in_specs=[blk, blk, blk, blk],
        out_specs=[blk, blk, blk],
        out_shape=[sds, sds, sds],
        compiler_params=_cparams(("parallel",)),
    )(w, g, m, v)


def _adamw_cols(w, g, m, v, *, name):
    c, _, r = w.shape
    tc = 139
    assert c % tc == 0

    def kern(w_ref, g_ref, m_ref, v_ref, go_ref, d_ref, mo_ref, vo_ref):
        g = g_ref[...]
        d, mn, vn = _adamw_math(w_ref[...], g, m_ref[...], v_ref[...])
        go_ref[...] = g
        d_ref[...] = d
        mo_ref[...] = mn
        vo_ref[...] = vn

    blk = pl.BlockSpec((tc, 1, r), lambda i: (i, 0, 0))
    sds = jax.ShapeDtypeStruct((c, 1, r), F32)
    return pl.pallas_call(
        kern, name=name,
        grid=(c // tc,),
        in_specs=[blk, blk, blk, blk],
        out_specs=[blk, blk, blk, blk],
        out_shape=[sds, sds, sds, sds],
        compiler_params=_cparams(("parallel",)),
    )(w, g, m, v)


def _position():
    x, y, c = lax.axis_index("x"), lax.axis_index("y"), lax.axis_index("c")
    chips = [(1 - x, y), (x, 1 - y), (1 - x, 1 - y)]
    return x, y, c, chips


def _chip_index(cx, cy):
    return 2 * cx + cy


def _gather_weights(*shards):
    n_arr = len(shards)

    def kern(*refs):
        ins, outs = refs[:n_arr], refs[n_arr:2 * n_arr]
        send_sems, recv_sems, local_sems = refs[2 * n_arr:]
        x, y, c, chips = _position()
        me = _chip_index(x, y)
        sibling = (x, y, 1 - c)

        local = [pltpu.make_async_copy(ins[a], outs[a].at[me], local_sems.at[a]) for a in range(n_arr)]
        for cp in local:
            cp.start()

        def half(ref, a):
            rows = shards[a].shape[0] // 2
            return ref.at[pl.ds(c * rows, rows), :]

        def copy(a, k, src, slot, to):
            return pltpu.make_async_remote_copy(
                src_ref=src, dst_ref=half(outs[a].at[slot], a),
                send_sem=send_sems.at[a * 6 + k], recv_sem=recv_sems.at[a * 6 + k],
                device_id=to, device_id_type=MESH)

        first = [copy(a, j, half(ins[a], a), me, (*chip, c)) for a in range(n_arr) for j, chip in enumerate(chips)]
        for cp in first:
            cp.start()
        passed = []
        for a in range(n_arr):
            for j, chip in enumerate(chips):
                slot = _chip_index(*chip)
                copy(a, j, half(ins[a], a), slot, (*chip, c)).wait_recv()
                fwd = copy(a, 3 + j, half(outs[a].at[slot], a), slot, sibling)
                fwd.start()
                passed.append(fwd)
        for a in range(n_arr):
            for j, chip in enumerate(chips):
                slot = _chip_index(*chip)
                rows = shards[a].shape[0] // 2
                dst = outs[a].at[slot].at[pl.ds((1 - c) * rows, rows), :]
                pltpu.make_async_remote_copy(
                    src_ref=dst, dst_ref=dst, send_sem=send_sems.at[a * 6 + 3 + j],
                    recv_sem=recv_sems.at[a * 6 + 3 + j], device_id=sibling, device_id_type=MESH).wait_recv()
        for cp in first + passed:
            cp.wait_send()
        for cp in local:
            cp.wait()

    vmem = pl.BlockSpec(memory_space=pltpu.VMEM)
    return pl.pallas_call(
        kern, name="gather_weights",
        in_specs=[vmem] * n_arr,
        out_specs=[vmem] * n_arr,
        out_shape=[jax.ShapeDtypeStruct((N_CHIPS,) + w.shape, w.dtype) for w in shards],
        scratch_shapes=[pltpu.SemaphoreType.DMA((6 * n_arr,)),
                        pltpu.SemaphoreType.DMA((6 * n_arr,)),
                        pltpu.SemaphoreType.DMA((n_arr,))],
        compiler_params=_cparams(),
    )(*shards)


def _pair_reduce(grads):
    n_arr = len(grads)
    chunk = 128

    def kern(*refs):
        ins = refs[:n_arr]
        outs = refs[n_arr:2 * n_arr]
        gots = refs[2 * n_arr:3 * n_arr]
        send_sems, recv_sems = refs[3 * n_arr:]
        x, y, c, _ = _position()
        sibling = (x, y, 1 - c)
        copies = []
        for a in range(n_arr):
            rows = grads[a].shape[1] // 2
            copies.append(pltpu.make_async_remote_copy(
                src_ref=ins[a].at[:, pl.ds((1 - c) * rows, rows), :], dst_ref=gots[a],
                send_sem=send_sems.at[a], recv_sem=recv_sems.at[a], device_id=sibling, device_id_type=MESH))
        for cp in copies:
            cp.start()
        for a in range(n_arr):
            copies[a].wait()
            rows = grads[a].shape[1] // 2
            for j in range(N_CHIPS):
                for r0 in range(0, rows, chunk):
                    mine = ins[a][j, pl.ds(pl.multiple_of(c * rows + r0, chunk), chunk), :]
                    outs[a][j, r0:r0 + chunk, :] = (mine + gots[a][j, r0:r0 + chunk, :]).astype(BF16)

    vmem = pl.BlockSpec(memory_space=pltpu.VMEM)
    half = [(N_CHIPS, g.shape[1] // 2, g.shape[2]) for g in grads]
    return pl.pallas_call(
        kern, name="pair_reduce",
        in_specs=[vmem] * n_arr,
        out_specs=[vmem] * n_arr,
        out_shape=[jax.ShapeDtypeStruct(h, BF16) for h in half],
        scratch_shapes=[pltpu.VMEM(h, F32) for h in half]
        + [pltpu.SemaphoreType.DMA((n_arr,)), pltpu.SemaphoreType.DMA((n_arr,))],
        compiler_params=_cparams(),
    )(*grads)


def _wo_gather_start(shard):
    hbm = pl.BlockSpec(memory_space=pltpu.HBM)
    sem = pl.BlockSpec(memory_space=pltpu.SEMAPHORE)
    land_shape = (N_CHIPS,) + shard.shape

    def kern(src_ref, land_ref, send_sems, recv_sems, src_thru, land_thru, token):
        x, y, c, chips = _position()
        me = _chip_index(x, y)
        for j, chip in enumerate(chips):
            pltpu.make_async_remote_copy(
                src_ref=src_ref, dst_ref=land_ref.at[me], send_sem=send_sems.at[j], recv_sem=recv_sems.at[j],
                device_id=(*chip, c), device_id_type=MESH).start()
        token[...] = jnp.zeros_like(token)

    return pl.pallas_call(
        kern, name="wo_gather_start",
        in_specs=[hbm, hbm],
        out_specs=(sem, sem, hbm, hbm, pl.BlockSpec(memory_space=pltpu.VMEM)),
        out_shape=(pltpu.SemaphoreType.DMA((3,)), pltpu.SemaphoreType.DMA((3,)),
                   pltpu.HBM(shard.shape, shard.dtype), pltpu.HBM(land_shape, shard.dtype),
                   jax.ShapeDtypeStruct((8, LANES), F32)),
        input_output_aliases={0: 2, 1: 3},
        compiler_params=pltpu.CompilerParams(has_side_effects=pltpu.SideEffectType.DATAFLOW_SIDE_EFFECTING),
    )(pltpu.with_memory_space_constraint(shard, pltpu.HBM),
      pltpu.with_memory_space_constraint(lax.empty(land_shape, shard.dtype), pltpu.HBM))


def _wo_gather_wait(send_sems, recv_sems, src_thru, land_thru, after):
    hbm = pl.BlockSpec(memory_space=pltpu.HBM)
    sem = pl.BlockSpec(memory_space=pltpu.SEMAPHORE)

    def kern(src_ref, land_ref, send_sems, recv_sems, after_ref, src_out, land_out):
        x, y, c, chips = _position()
        for j, chip in enumerate(chips):
            copy = pltpu.make_async_remote_copy(
                src_ref=src_ref, dst_ref=land_ref.at[_chip_index(*chip)], send_sem=send_sems.at[j],
                recv_sem=recv_sems.at[j], device_id=(*chip, c), device_id_type=MESH)
            copy.wait_send()
            copy.wait_recv()

    return pl.pallas_call(
        kern, name="wo_gather_wait",
        in_specs=[hbm, hbm, sem, sem, pl.BlockSpec(memory_space=pl.ANY)],
        out_specs=[hbm, hbm],
        out_shape=[pltpu.HBM(src_thru.shape, src_thru.dtype), pltpu.HBM(land_thru.shape, land_thru.dtype)],
        input_output_aliases={0: 0, 1: 1},
        compiler_params=pltpu.CompilerParams(has_side_effects=pltpu.SideEffectType.DATAFLOW_SIDE_EFFECTING),
    )(src_thru, land_thru, send_sems, recv_sems, after)[1]


def _scatter_start(parts):
    n_arr = len(parts)
    hbm = pl.BlockSpec(memory_space=pltpu.HBM)
    sem = pl.BlockSpec(memory_space=pltpu.SEMAPHORE)

    def kern(*refs):
        ins, lands = refs[:n_arr], refs[n_arr:2 * n_arr]
        send_sems, recv_sems, token = refs[2 * n_arr], refs[2 * n_arr + 1], refs[-1]
        x, y, c, chips = _position()
        me = _chip_index(x, y)
        for a in range(n_arr):
            for j, chip in enumerate(chips):
                pltpu.make_async_remote_copy(
                    src_ref=ins[a].at[_chip_index(*chip)], dst_ref=lands[a].at[me],
                    send_sem=send_sems.at[a * 3 + j], recv_sem=recv_sems.at[a * 3 + j],
                    device_id=(*chip, c), device_id_type=MESH).start()
        token[...] = jnp.zeros_like(token)

    slab = [pltpu.HBM(p.shape, p.dtype) for p in parts]
    outs = pl.pallas_call(
        kern, name="scatter_start",
        in_specs=[hbm] * (2 * n_arr),
        out_specs=(sem, sem, *[hbm] * (2 * n_arr), pl.BlockSpec(memory_space=pltpu.VMEM)),
        out_shape=(pltpu.SemaphoreType.DMA((3 * n_arr,)), pltpu.SemaphoreType.DMA((3 * n_arr,)),
                   *slab, *slab, jax.ShapeDtypeStruct((8, LANES), F32)),
        input_output_aliases={i: 2 + i for i in range(2 * n_arr)},
        compiler_params=pltpu.CompilerParams(has_side_effects=pltpu.SideEffectType.DATAFLOW_SIDE_EFFECTING),
    )(*[pltpu.with_memory_space_constraint(p, pltpu.HBM) for p in parts],
      *[pltpu.with_memory_space_constraint(lax.empty(p.shape, p.dtype), pltpu.HBM) for p in parts])
    return outs[0], outs[1], outs[2:2 + n_arr], outs[2 + n_arr:2 + 2 * n_arr], outs[-1]


def _scatter_wait(send_sems, recv_sems, parts_thru, lands_thru, after):
    n_arr = len(parts_thru)
    hbm = pl.BlockSpec(memory_space=pltpu.HBM)
    sem = pl.BlockSpec(memory_space=pltpu.SEMAPHORE)

    def kern(*refs):
        ins, lands = refs[:n_arr], refs[n_arr:2 * n_arr]
        send_ref, recv_ref = refs[2 * n_arr], refs[2 * n_arr + 1]
        x, y, c, chips = _position()
        for a in range(n_arr):
            for j, chip in enumerate(chips):
                slot = _chip_index(*chip)
                copy = pltpu.make_async_remote_copy(
                    src_ref=ins[a].at[slot], dst_ref=lands[a].at[slot],
                    send_sem=send_ref.at[a * 3 + j], recv_sem=recv_ref.at[a * 3 + j],
                    device_id=(*chip, c), device_id_type=MESH)
                copy.wait_send()
                copy.wait_recv()

    slab = [pltpu.HBM(p.shape, p.dtype) for p in parts_thru]
    outs = pl.pallas_call(
        kern, name="scatter_wait",
        in_specs=[hbm] * (2 * n_arr) + [sem, sem, pl.BlockSpec(memory_space=pl.ANY)],
        out_specs=[hbm] * (2 * n_arr),
        out_shape=slab + slab,
        input_output_aliases={i: i for i in range(2 * n_arr)},
        compiler_params=pltpu.CompilerParams(has_side_effects=pltpu.SideEffectType.DATAFLOW_SIDE_EFFECTING),
    )(*parts_thru, *lands_thru, send_sems, recv_sems, after)
    return outs[:n_arr], outs[n_arr:]


def _chip_sum_swap(parts, lands):
    n_arr = len(parts)
    chunk = 128

    def kern(*refs):
        own, got = refs[:n_arr], refs[n_arr:2 * n_arr]
        outs = refs[2 * n_arr:3 * n_arr]
        sums = refs[3 * n_arr:4 * n_arr]
        swap_send, swap_recv, swap_local = refs[4 * n_arr:]
        x, y, c, _ = _position()
        me = _chip_index(x, y)
        sibling = (x, y, 1 - c)
        for a in range(n_arr):
            for r0 in range(0, parts[a].shape[1], chunk):
                mine = own[a][me, r0:r0 + chunk, :].astype(F32)

                def term(i):
                    other = got[a][jnp.where(i == me, (i + 1) % N_CHIPS, i), r0:r0 + chunk, :].astype(F32)
                    return jnp.where(i == me, mine, other)
                sums[a][r0:r0 + chunk, :] = ((term(0) + term(1)) + term(2)) + term(3)
        swap_l, swap_r = [], []
        for a in range(n_arr):
            rows = parts[a].shape[1]
            mine = outs[a].at[pl.ds(c * rows, rows), :]
            swap_l.append(pltpu.make_async_copy(sums[a], mine, swap_local.at[a]))
            swap_r.append(pltpu.make_async_remote_copy(
                src_ref=sums[a], dst_ref=mine, send_sem=swap_send.at[a], recv_sem=swap_recv.at[a],
                device_id=sibling, device_id_type=MESH))
        for cp in swap_l + swap_r:
            cp.start()
        for a in range(n_arr):
            rows = parts[a].shape[1]
            theirs = outs[a].at[pl.ds((1 - c) * rows, rows), :]
            pltpu.make_async_remote_copy(
                src_ref=theirs, dst_ref=theirs, send_sem=swap_send.at[a], recv_sem=swap_recv.at[a],
                device_id=sibling, device_id_type=MESH).wait_recv()
        for cp in swap_r:
            cp.wait_send()
        for cp in swap_l:
            cp.wait()

    vmem = pl.BlockSpec(memory_space=pltpu.VMEM)
    return pl.pallas_call(
        kern, name="chip_sum_swap",
        in_specs=[vmem] * (2 * n_arr),
        out_specs=[vmem] * n_arr,
        out_shape=[jax.ShapeDtypeStruct((2 * p.shape[1], p.shape[2]), F32) for p in parts],
        scratch_shapes=[pltpu.VMEM(p.shape[1:], F32) for p in parts]
        + [pltpu.SemaphoreType.DMA((n_arr,)),
           pltpu.SemaphoreType.DMA((n_arr,)),
           pltpu.SemaphoreType.DMA((n_arr,))],
        compiler_params=_cparams(),
    )(*parts, *lands)


def _small_allreduce_adamw(partials, params, moms, vels):
    chunks = D_MODEL // LANES
    row_rb, row_bf, row_sk, row_loss = 2 * chunks, 2 * chunks + NUM_BUCKETS, 2 * chunks + NUM_BUCKETS + 1, SMALL_ROWS - 6

    def kern(gbf_ref, grb_ref, gsk_ref, gg_ref, gb_ref, loss_ref, *refs):
        p_refs, m_refs, v_refs = refs[0:5], refs[5:10], refs[10:15]
        lo_ref, g_outs, d_outs, mo_outs, vo_outs = refs[15], refs[16:21], refs[21:26], refs[26:31], refs[31:36]
        send_ref, buf_ref, send_sems, recv_sems = refs[36:]
        x, y, c, _ = _position()
        me = 4 * x + 2 * y + c
        send_ref[...] = jnp.zeros_like(send_ref)
        for r in range(chunks):
            send_ref[r:r + 1, :] = gg_ref[0:1, r * LANES:(r + 1) * LANES]
            send_ref[chunks + r:chunks + r + 1, :] = gb_ref[0:1, r * LANES:(r + 1) * LANES]
        send_ref[row_rb:row_rb + NUM_BUCKETS, :] = grb_ref[...]
        send_ref[row_bf:row_bf + 1, :] = gbf_ref[0:1, :]
        send_ref[row_sk:row_sk + 1, :] = gsk_ref[0:1, :]
        send_ref[row_loss:row_loss + 1, :] = loss_ref[0:1, :]
        buf_ref[me] = send_ref[...]
        peers = [(x, y, 1 - c)] + [(px, py, pc) for px, py in _position()[3] for pc in (c, 1 - c)]
        sends = []
        for k, peer in enumerate(peers):
            sends.append(pltpu.make_async_remote_copy(
                src_ref=send_ref, dst_ref=buf_ref.at[me], send_sem=send_sems.at[k], recv_sem=recv_sems.at[k],
                device_id=peer, device_id_type=MESH))
        for cp in sends:
            cp.start()
        for k, (px, py, pc) in enumerate(peers):
            slot = buf_ref.at[4 * px + 2 * py + pc]
            pltpu.make_async_remote_copy(
                src_ref=slot, dst_ref=slot, send_sem=send_sems.at[k], recv_sem=recv_sems.at[k],
                device_id=(px, py, pc), device_id_type=MESH).wait_recv()
        for cp in sends:
            cp.wait_send()
        tot = buf_ref[0]
        for d in range(1, N_DEV):
            tot = tot + buf_ref[d]
        lo_ref[...] = tot[row_loss:row_loss + 1, :]
        grads = [tot[row_bf:row_bf + 1, 0:FOX_HEADS],
                 tot[row_rb:row_rb + NUM_BUCKETS, 0:SWA_HEADS],
                 tot[row_sk:row_sk + 1, 0:SWA_HEADS],
                 jnp.concatenate([tot[r:r + 1, :] for r in range(chunks)], axis=1),
                 jnp.concatenate([tot[chunks + r:chunks + r + 1, :] for r in range(chunks)], axis=1)]
        for i, g in enumerate(grads):
            g_outs[i][...] = g
            delta, mn, vn = _adamw_math(p_refs[i][...], g, m_refs[i][...], v_refs[i][...])
            d_outs[i][...] = delta
            mo_outs[i][...] = mn
            vo_outs[i][...] = vn

    vm = pl.BlockSpec(memory_space=pltpu.VMEM)
    shapes = [jax.ShapeDtypeStruct(p.shape, F32) for p in params]
    outs = pl.pallas_call(
        kern, name="small_allreduce_adamw",
        in_specs=[vm] * 21,
        out_specs=[vm] * 21,
        out_shape=[jax.ShapeDtypeStruct((1, LANES), F32)] + shapes * 4,
        scratch_shapes=[pltpu.VMEM((SMALL_ROWS, LANES), F32),
                        pltpu.VMEM((N_DEV, SMALL_ROWS, LANES), F32),
                        pltpu.SemaphoreType.DMA((N_DEV - 1,)),
                        pltpu.SemaphoreType.DMA((N_DEV - 1,))],
    )(*partials, *params, *moms, *vels)
    return outs[0], outs[1:6], outs[6:11], outs[11:16], outs[16:21]


def _to_padded_cols(w):
    pad = jnp.zeros((w.shape[0], N_C - FOX_HEADS), w.dtype)
    return jnp.concatenate([w[:, 0:1536], w[:, 2056:2824], w[:, 1536:1544], pad,
                            w[:, 1544:2056], w[:, 2824:3336]], axis=1)


_COLUMN_RUNS = ((0, 0, 1536), (1536, OFF_C, FOX_HEADS), (1544, OFF_B, FOX_W), (2056, 1536, N_A - 1536),
                (2824, OFF_B + FOX_W, SWA_W))


def _shards_from_padded(g):
    shard_cols = D_IN // N_CHIPS
    shards = []
    for j in range(N_CHIPS):
        lo, hi = j * shard_cols, (j + 1) * shard_cols
        pieces = []
        for ref0, pad0, width in _COLUMN_RUNS:
            a, b = max(lo, ref0), min(hi, ref0 + width)
            if a < b:
                pieces.append(g[:, pad0 + a - ref0:pad0 + b - ref0])
        pieces.append(jnp.zeros((g.shape[0], SHARD_PAD - shard_cols), g.dtype))
        shards.append(jnp.concatenate(pieces, axis=1))
    return jnp.stack(shards)


def _fox_rows(a):
    return a[:, :FOX_HEADS].T.reshape(FOX_HEADS, 1, a.shape[0])


def kernel(x, w_in, b_f, rel_bias, sink, w_o, ln_g, ln_b, loss_target, m_w_in, m_b_f, m_rel_bias, m_sink, m_w_o, m_ln_g, m_ln_b, v_w_in, v_b_f, v_rel_bias, v_sink, v_w_o, v_ln_g, v_ln_b):
    x2 = x[0]
    tgt = loss_target[0]
    s = x2.shape[0]
    w_in2, w_o2 = w_in[0], w_o[0]

    shard_cols = D_IN // N_CHIPS
    col_pad = ((0, 0), (0, SHARD_PAD - shard_cols))
    (w_in_all,) = _gather_weights(jnp.pad(w_in2.astype(BF16), col_pad))
    w_full = jnp.concatenate([w_in_all[j, :, :shard_cols] for j in range(N_CHIPS)], axis=1)
    w_pad = _to_padded_cols(w_full)
    w_o_bf, _ = lax.optimization_barrier((w_o2.astype(BF16), w_in_all))
    wo_send, wo_recv, wo_src, wo_land, wo_token = _wo_gather_start(w_o_bf)

    bfp = jnp.pad(b_f, ((0, 0), (0, LANES - FOX_HEADS)))
    qkv, ffp, z, xt, vt, cum = _project(x2, w_pad, bfp, wo_token)
    cum_t3 = _fox_rows(cum)
    o_fox, lse_t3 = _fox_fwd(qkv, vt, cum_t3, cum)
    bucket_t = jnp.asarray(_bucket_table().T)
    bias_t = _swa_bias(rel_bias, bucket_t)
    sink_rows = jnp.repeat(sink.reshape(SWA_KV_HEADS, SWA_GROUP, 1), BLOCK, axis=2).reshape(SWA_KV_HEADS, 1, SWA_LANES)
    o_swa, lse_swa = _swa_fwd(qkv, bias_t, sink_rows)

    wo_land = _wo_gather_wait(wo_send, wo_recv, wo_src, wo_land, o_swa)
    my_chip = _chip_index(lax.axis_index("x"), lax.axis_index("y"))
    w_o_full = lax.dynamic_update_slice(wo_land, w_o_bf[None], (my_chip, 0, 0)).reshape(D_MODEL, D_MODEL)
    loss8, dh, grad_w_o_full, do_bf, dz, delta, gg8, gb8 = _post(
        x2, tgt, o_fox, o_swa, z, w_o_full, ln_g, ln_b)

    delta_t3 = _fox_rows(delta)
    dq_fox, dk_fox, dv_fox, dcum_k, dcum_q = _fox_bwd(qkv, do_bf, cum_t3, cum, lse_t3, delta_t3)
    dcum_q = jnp.pad(dcum_q.reshape(FOX_HEADS, s).T, ((0, 0), (0, LANES - FOX_HEADS)))
    dff, gbf8 = _cum_bwd(dcum_k, dcum_q, ffp, bfp)
    delta_rows = (delta[:, FOX_HEADS:FOX_HEADS + SWA_HEADS].reshape(s // BLOCK, BLOCK, SWA_KV_HEADS, SWA_GROUP)
                  .transpose(0, 2, 3, 1).reshape(s // BLOCK, SWA_KV_HEADS, 1, SWA_LANES))
    dq_swa, dk_swa, dv_swa, grb, gsk8 = _swa_bwd(qkv, do_bf, delta_rows, lse_swa, bias_t, sink_rows, bucket_t)

    d_misc = jnp.concatenate([dk_swa, dv_swa, dff], axis=1)
    pieces = [dq_fox, dk_fox, dv_fox, dq_swa, d_misc, dz]
    blocks = [(p, 0) for p in pieces[:-1]] + [(dz, 0), (dz, 1)]
    grad_w_pad = _grad_w_matmul(xt, blocks, tk=1024, name="grad_w_in")

    g_in4 = _shards_from_padded(grad_w_pad)
    g_o4 = grad_w_o_full.reshape(N_CHIPS, D_MODEL // N_CHIPS, D_MODEL)
    parts = _pair_reduce([g_in4, g_o4])
    send_sems, recv_sems, parts_thru, lands_thru, token = _scatter_start(parts)
    grad_x = _grad_x_matmul(pieces, w_pad, dh, token, tm=512, tn=D_MODEL, name="grad_x")
    parts, lands = _scatter_wait(send_sems, recv_sems, parts_thru, lands_thru, grad_x)
    g_w_in, g_w_o = _chip_sum_swap(parts, lands)
    g_w_in = g_w_in[:, :shard_cols]

    cols_first = lambda a: jnp.transpose(a, (2, 0, 1))
    rows_first = lambda a: jnp.transpose(a, (1, 2, 0))
    g_w_in, d_w_in, nm_w_in, nv_w_in = [rows_first(a) for a in _adamw_cols(
        cols_first(w_in), cols_first(g_w_in[None]), cols_first(m_w_in), cols_first(v_w_in), name="adamw_w_in")]
    d_w_o, nm_w_o, nv_w_o = _adamw(w_o2, g_w_o, m_w_o[0], v_w_o[0], name="adamw_w_o")

    loss_row, gs, ds, ms, vs = _small_allreduce_adamw(
        [gbf8, grb, gsk8, gg8, gb8, loss8],
        [b_f, rel_bias, sink, ln_g, ln_b],
        [m_b_f, m_rel_bias, m_sink, m_ln_g, m_ln_b],
        [v_b_f, v_rel_bias, v_sink, v_ln_g, v_ln_b])
    loss = loss_row[0, 0]
    g_bf, g_rb, g_sk, g_lg, g_lb = gs
    d_bf, d_rb, d_sk, d_lg, d_lb = ds
    m_bf, m_rb, m_sk, m_lg, m_lb = ms
    v_bf, v_rb, v_sk, v_lg, v_lb = vs

    e = lambda a: a[None]
    return (loss, e(grad_x),
            g_w_in, g_bf, g_rb, g_sk, e(g_w_o), g_lg, g_lb,
            d_w_in, d_bf, d_rb, d_sk, e(d_w_o), d_lg, d_lb,
            nm_w_in, m_bf, m_rb, m_sk, e(nm_w_o), m_lg, m_lb,
            nv_w_in, v_bf, v_rb, v_sk, e(nv_w_o), v_lg, v_lb)
```

```python
import functools
import math

import numpy as np
import jax
import jax.numpy as jnp
from jax import lax
from jax.experimental import pallas as pl
from jax.experimental.pallas import tpu as pltpu

F32 = jnp.float32
BF16 = jnp.bfloat16

D_MODEL = 1024
HEAD_DIM = 64
FOX_HEADS = 8
SWA_HEADS = 8
SWA_KV_HEADS = 2
SWA_GROUP = 4
FOX_W = 512
SWA_W = 512
BLOCK = 128
NUM_BUCKETS = 32
MAX_DISTANCE = 128
LN_EPS = 1e-5
NEG = -1e30
ALPHA = 2.0 ** 0.25
QK_SCALE = 0.125

ADAM_LR = 0.001
ADAM_B1 = 0.9
ADAM_B2 = 0.999
ADAM_EPS = 1e-08
ADAM_WD = 0.01
ADAM_STEP = 10

D_IN = 3336
SHARD_PAD = 896
N_A = 2304
N_C = 256
N_B = 1024
OFF_C = N_A
OFF_B = N_A + N_C
N_PAD = N_A + N_C + N_B
COL_FK, COL_FV, COL_SQ, COL_SK, COL_SV = 512, 1024, 1536, 2048, 2176

LANES = 128
FOX_T = 256
FOX_REF = 512
SUM_ROWS = 16
VMEM_LIMIT = 56 * 1024 * 1024

MESH = pl.DeviceIdType.MESH
N_CHIPS = 4
N_DEV = 8
SMALL_ROWS = 56


def _cparams(sem=None):
    return pltpu.CompilerParams(dimension_semantics=sem, vmem_limit_bytes=VMEM_LIMIT)


def _split3(x):
    hi = x.astype(BF16)
    r = x - hi.astype(F32)
    mid = r.astype(BF16)
    lo = (r - mid.astype(F32)).astype(BF16)
    return hi, mid, lo


def _dot(a, b):
    return jnp.dot(a, b, preferred_element_type=F32)


def _dot_nt(a, b):
    return lax.dot_general(a, b, (((1,), (1,)), ((), ())), preferred_element_type=F32)


def _project(x, w_pad, bfp, token):
    s, k = x.shape
    tm = min(512, s)
    chunk = 512

    def kern(x_ref, w_ref, b_ref, _, qkv_ref, ff_ref, z_ref, xt_ref, vt_ref, cum_ref, carry_ref):
        @pl.when(pl.program_id(0) == 0)
        def _():
            carry_ref[...] = jnp.zeros_like(carry_ref)
        xf = x_ref[...]
        xb = xf.astype(BF16)
        xt_ref[...] = xf.T.astype(BF16)
        for c0 in range(0, N_A, chunk):
            width = min(chunk, N_A - c0)
            res = _dot(xb, w_ref[:, c0:c0 + width])
            qkv_ref[:, c0:c0 + width] = res.astype(BF16)
            if c0 == COL_FV:
                vt_ref[...] = res.T.astype(BF16)
        ff = _dot(xb, w_ref[:, OFF_C:OFF_C + N_C])
        ff_ref[...] = ff
        lane = lax.broadcasted_iota(jnp.int32, (1, LANES), 1)
        lf = jnp.where(lane < FOX_HEADS, _log_sigmoid(ff[:, :LANES] + b_ref[...]), 0.0)
        cum, carry = _block_scan(lf, carry_ref[0:1, :], False)
        cum_ref[...] = cum
        carry_ref[...] = jnp.broadcast_to(carry, carry_ref.shape)
        for c0 in range(0, N_B, 512):
            z_ref[:, c0:c0 + 512] = _dot(xb, w_ref[:, OFF_B + c0:OFF_B + c0 + 512])

    row = lambda i: (i, 0)
    return pl.pallas_call(
        kern, name="project",
        grid=(s // tm,),
        in_specs=[pl.BlockSpec((tm, k), row),
                  _resident((k, N_PAD), lambda i: (0, 0)),
                  _resident((1, LANES), lambda i: (0, 0)),
                  _resident(token.shape, lambda i: (0, 0))],
        out_specs=[pl.BlockSpec((tm, N_A), row),
                   pl.BlockSpec((tm, N_C), row),
                   pl.BlockSpec((tm, N_B), row),
                   pl.BlockSpec((k, tm), lambda i: (0, i)),
                   pl.BlockSpec((FOX_W, tm), lambda i: (0, i)),
                   pl.BlockSpec((tm, LANES), row)],
        out_shape=[jax.ShapeDtypeStruct((s, N_A), BF16),
                   jax.ShapeDtypeStruct((s, N_C), F32),
                   jax.ShapeDtypeStruct((s, N_B), F32),
                   jax.ShapeDtypeStruct((k, s), BF16),
                   jax.ShapeDtypeStruct((FOX_W, s), BF16),
                   jax.ShapeDtypeStruct((s, LANES), F32)],
        scratch_shapes=[pltpu.VMEM((8, LANES), F32)],
        compiler_params=_cparams(("arbitrary",)),
    )(x, w_pad, bfp, token)


def _grad_x_matmul(pieces, w_pad, dh, token, *, tm, tn, name):
    m = dh.shape[0]
    n, k = w_pad.shape
    widths = [p.shape[1] for p in pieces]
    offs = [sum(widths[:i]) for i in range(len(pieces))]
    assert sum(widths) == k

    def kern(*refs):
        p_refs, (b_ref, dh_ref, _, o_ref) = refs[:len(pieces)], refs[len(pieces):]
        acc = ALPHA * dh_ref[...]
        for p_ref, off, width in zip(p_refs, offs, widths):
            acc = acc + _dot_nt(p_ref[...], b_ref[:, off:off + width])
        o_ref[...] = acc

    assert tn == n
    return pl.pallas_call(
        kern, name=name,
        grid=(m // tm,),
        in_specs=[pl.BlockSpec((tm, w), lambda i: (i, 0)) for w in widths]
        + [_resident((n, k), lambda i: (0, 0)),
           pl.BlockSpec((tm, n), lambda i: (i, 0)),
           _resident(token.shape, lambda i: (0, 0))],
        out_specs=pl.BlockSpec((tm, n), lambda i: (i, 0)),
        out_shape=jax.ShapeDtypeStruct((m, n), F32),
        compiler_params=_cparams(("parallel",)),
    )(*pieces, w_pad, dh, token)


def _grad_w_matmul(xt, blocks, *, tk, name):
    m, s = xt.shape
    tn = 512
    nb = len(blocks)

    def kern(a_ref, *refs):
        b_refs, o_ref = refs[:nb], refs[nb]

        @pl.when(pl.program_id(0) == 0)
        def _():
            o_ref[...] = jnp.zeros_like(o_ref)
        a = a_ref[...]
        for blk in range(nb):
            o_ref[:, blk * tn:(blk + 1) * tn] += _dot(a, b_refs[blk][...])

    return pl.pallas_call(
        kern, name=name,
        grid=(s // tk,),
        in_specs=[pl.BlockSpec((m, tk), lambda k: (0, k))]
        + [pl.BlockSpec((tk, tn), functools.partial(lambda k, col: (k, col), col=col)) for _, col in blocks],
        out_specs=_resident((m, nb * tn), lambda k: (0, 0)),
        out_shape=jax.ShapeDtypeStruct((m, nb * tn), F32),
        compiler_params=_cparams(("arbitrary",)),
    )(xt, *[arr for arr, _ in blocks])


def _tri(n, lower):
    r = lax.broadcasted_iota(jnp.int32, (n, n), 0)
    c = lax.broadcasted_iota(jnp.int32, (n, n), 1)
    keep = (c <= r) if lower else (c >= r)
    return jnp.where(keep, 1.0, 0.0).astype(BF16)


def _exact_dot(mat_bf16, x_f32, left):
    out = None
    for piece in _split3(x_f32):
        t = _dot(mat_bf16, piece) if left else _dot(piece, mat_bf16)
        out = t if out is None else out + t
    return out


def _block_scan(x, carry, reverse):
    n = x.shape[0] // LANES
    tri = _tri(LANES, not reverse)
    out = [None] * n
    for b in (reversed(range(n)) if reverse else range(n)):
        blk = _exact_dot(tri, x[b * LANES:(b + 1) * LANES], True) + carry
        carry = blk[0:1, :] if reverse else blk[LANES - 1:LANES, :]
        out[b] = blk
    return jnp.concatenate(out, axis=0), carry


def _log_sigmoid(z):
    return jnp.minimum(z, 0.0) - jnp.log(1.0 + jnp.exp(-jnp.abs(z)))


def _cum_bwd(dcum_k, dcum_q, ffp, bfp):
    s = dcum_k.shape[0]
    t = min(1024, s)
    nb = s // t

    def kern(dck_ref, dcq_ref, ff_ref, b_ref, dff_ref, gb_ref, carry_ref):
        @pl.when(pl.program_id(0) == 0)
        def _():
            carry_ref[...] = jnp.zeros_like(carry_ref)
            gb_ref[...] = jnp.zeros_like(gb_ref)
        lane = lax.broadcasted_iota(jnp.int32, (1, LANES), 1)
        dlf, carry = _block_scan(dck_ref[...] + dcq_ref[...], carry_ref[0:1, :], True)
        carry_ref[...] = jnp.broadcast_to(carry, carry_ref.shape)
        z = ff_ref[...] + b_ref[...]
        dff = jnp.where(lane < FOX_HEADS, dlf / (1.0 + jnp.exp(z)), 0.0)
        gb_ref[...] += jnp.broadcast_to(jnp.sum(dff, axis=0, keepdims=True), gb_ref.shape)
        dff_ref[...] = jnp.concatenate([dff, jnp.zeros_like(dff)], axis=1).astype(BF16)

    return pl.pallas_call(
        kern, name="cum_bwd",
        grid=(nb,),
        in_specs=[pl.BlockSpec((t, LANES), lambda i: (nb - 1 - i, 0)),
                  pl.BlockSpec((t, LANES), lambda i: (nb - 1 - i, 0)),
                  pl.BlockSpec((t, LANES), lambda i: (nb - 1 - i, 0)),
                  pl.BlockSpec((1, LANES), lambda i: (0, 0))],
        out_specs=[pl.BlockSpec((t, N_C), lambda i: (nb - 1 - i, 0)),
                   pl.BlockSpec((8, LANES), lambda i: (0, 0))],
        out_shape=[jax.ShapeDtypeStruct((s, N_C), BF16),
                   jax.ShapeDtypeStruct((8, LANES), F32)],
        scratch_shapes=[pltpu.VMEM((8, LANES), F32)],
        compiler_params=_cparams(("arbitrary",)),
    )(dcum_k, dcum_q, ffp, bfp)


def _resident(shape, index_map):
    return pl.BlockSpec(shape, index_map, pipeline_mode=pl.Buffered(1))


def _fox_fwd(qkv, vt, cum_t3, cum):
    s = qkv.shape[0]
    tk = tq = FOX_REF
    nq = s // tq
    nh = FOX_HEADS
    diag_tiles = tq // tk

    def kern(q_ref, k_ref, vt_ref, ct_ref, c_ref, o_ref, lse_ref, m_ref, acc_ref, u_ref):
        i = pl.program_id(0)
        lane = lax.broadcasted_iota(jnp.int32, (1, LANES), 1)
        krow = lax.broadcasted_iota(jnp.int32, (tk, tq), 0)
        qcol = lax.broadcasted_iota(jnp.int32, (tk, tq), 1)
        q0 = pl.multiple_of(i * tq, tq)
        qts, crefs = [], []
        for h in range(nh):
            p, a = divmod(h, 2)
            q2 = q_ref[:, p * LANES:(p + 1) * LANES] * jnp.asarray(QK_SCALE, BF16)
            sel = (lane < HEAD_DIM) if a == 0 else (lane >= HEAD_DIM)
            qts.append(jnp.where(sel, q2, jnp.zeros_like(q2)).astype(F32).T.astype(BF16))
            crefs.append(ct_ref[h, :, pl.ds(q0, LANES)][:, 0:1])
        m_ref[...] = jnp.full(m_ref.shape, NEG, F32)
        acc_ref[...] = jnp.zeros_like(acc_ref)
        ones = jnp.ones((SUM_ROWS, tk), BF16)

        def tile(j, diag):
            k0 = pl.multiple_of(j * tk, tk)
            cb = c_ref[pl.ds(k0, tk), :]
            sts = [_dot(k_ref[pl.ds(k0, tk), (h // 2) * LANES:(h // 2 + 1) * LANES], qts[h]) for h in range(nh)]
            tile_max = []
            for h in range(nh):
                u = sts[h] - (cb[:, h:h + 1] - crefs[h])
                if diag is not None:
                    u = jnp.where(krow + diag * tk <= qcol, u, NEG)
                u_ref[h] = u
                tile_max.append(jnp.max(u, axis=0, keepdims=True))
            pts, scales = [], []
            for h in range(nh):
                m_old = m_ref[h]
                m_new = jnp.maximum(m_old, tile_max[h])
                scales.append(jnp.exp(m_old - m_new))
                pts.append(jnp.exp(u_ref[h] - m_new).astype(BF16))
                m_ref[h] = m_new
            for h in range(nh):
                vth = jnp.concatenate([vt_ref[h * HEAD_DIM:(h + 1) * HEAD_DIM, pl.ds(k0, tk)], ones], axis=0)
                acc_ref[h] = scales[h] * acc_ref[h] + _dot(vth, pts[h])

        def body(j, c):
            tile(j, None)
            return c
        lax.fori_loop(0, i * diag_tiles, body, 0)
        for d in range(diag_tiles):
            tile(i * diag_tiles + d, d)

        ls = [acc_ref[h][HEAD_DIM:HEAD_DIM + 1] for h in range(nh)]
        for p in range(nh // 2):
            ot = jnp.concatenate([acc_ref[2 * p + a][:HEAD_DIM] * (1.0 / ls[2 * p + a]) for a in range(2)], axis=0)
            o_ref[:, p * LANES:(p + 1) * LANES] = ot.T
        for h in range(nh):
            lse_ref[h, :, pl.ds(q0, tq)] = m_ref[h] + jnp.log(ls[h])

    return pl.pallas_call(
        kern, name="fox_fwd",
        grid=(nq,),
        in_specs=[pl.BlockSpec((tq, FOX_W), lambda i: (i, 0)),
                  _resident((s, FOX_W), lambda i: (0, COL_FK // FOX_W)),
                  _resident((FOX_W, s), lambda i: (0, 0)),
                  _resident((nh, 1, s), lambda i: (0, 0, 0)),
                  _resident((s, LANES), lambda i: (0, 0))],
        out_specs=[pl.BlockSpec((tq, FOX_W), lambda i: (i, 0)),
                   pl.BlockSpec((nh, 1, s), lambda i: (0, 0, 0))],
        out_shape=[jax.ShapeDtypeStruct((s, FOX_W), F32),
                   jax.ShapeDtypeStruct((nh, 1, s), F32)],
        scratch_shapes=[pltpu.VMEM((nh, 1, tq), F32),
                        pltpu.VMEM((nh, HEAD_DIM + SUM_ROWS, tq), F32),
                        pltpu.VMEM((nh, tk, tq), F32)],
        compiler_params=_cparams(("arbitrary",)),
    )(qkv, qkv, vt, cum_t3, cum)


def _fox_bwd(qkv, do_bf, cum_t3, cum, lse_t3, delta_t3):
    s = qkv.shape[0]
    t = min(FOX_T, s)
    nq = s // t
    nh = FOX_HEADS
    npair = nh // 2

    def kern(q_ref, do_ref, k_ref, v_ref, ct_ref, c_ref, lse_ref, dl_ref,
             dq_ref, dk_ref, dv_ref, dc_ref, dcq_ref, dqt_ref, accv_ref, acck_ref, accd_ref):
        kj = pl.program_id(0)
        lane = lax.broadcasted_iota(jnp.int32, (1, LANES), 1)
        krow = lax.broadcasted_iota(jnp.int32, (t, t), 0)
        qcol = lax.broadcasted_iota(jnp.int32, (t, t), 1)
        causal = krow <= qcol
        sels = [lane < HEAD_DIM, lane >= HEAD_DIM]

        @pl.when(kj == 0)
        def _():
            dqt_ref[...] = jnp.zeros_like(dqt_ref)
            dcq_ref[...] = jnp.zeros_like(dcq_ref)

        cb = c_ref[...]
        k2s, v2s, kts = [], [], []
        for p in range(npair):
            k2 = k_ref[:, p * LANES:(p + 1) * LANES]
            k2s.append(k2)
            v2s.append(v_ref[:, p * LANES:(p + 1) * LANES])
            kt = k2.astype(F32).T * QK_SCALE
            kts.append(kt[:HEAD_DIM].astype(BF16))
            kts.append(kt[HEAD_DIM:].astype(BF16))
        css = [cb[:, h:h + 1] for h in range(nh)]

        def tile(i, masked):
            q0 = pl.multiple_of(i * t, t)
            r0 = pl.multiple_of((i // (FOX_REF // t)) * FOX_REF, FOX_REF)
            sts, dpts, qms, doms = [], [], [], []
            for h in range(nh):
                p, a = divmod(h, 2)
                qi = q_ref[pl.ds(q0, t), p * LANES:(p + 1) * LANES] * jnp.asarray(QK_SCALE, BF16)
                doi = do_ref[pl.ds(q0, t), p * LANES:(p + 1) * LANES]
                qm = jnp.where(sels[a], qi, jnp.zeros_like(qi))
                dom = jnp.where(sels[a], doi, jnp.zeros_like(doi))
                qms.append(qm)
                doms.append(dom)
                sts.append(_dot_nt(k2s[p], qm))
                dpts.append(_dot_nt(v2s[p], dom))
            pts, dsts = [], []
            for h in range(nh):
                cref = ct_ref[h, :, pl.ds(r0, LANES)][:, 0:1]
                pt = jnp.exp(sts[h] - (css[h] - cref) - lse_ref[h, :, pl.ds(q0, t)])
                if masked:
                    pt = jnp.where(causal, pt, 0.0)
                ds32 = pt * (dpts[h] - dl_ref[h, :, pl.ds(q0, t)])
                part = ds32[:, 0:LANES]
                for c in range(1, t // LANES):
                    part = part + ds32[:, c * LANES:(c + 1) * LANES]
                accd_ref[h] = part if masked else accd_ref[h] + part
                dcq_ref[h, :, pl.ds(q0, t)] += jnp.sum(ds32, axis=0, keepdims=True)
                pts.append(pt.astype(BF16))
                dsts.append(ds32.astype(BF16))
            for p in range(npair):
                ha, hb = 2 * p, 2 * p + 1
                dv_p = _dot(pts[ha], doms[ha]) + _dot(pts[hb], doms[hb])
                dk_p = _dot(dsts[ha], qms[ha]) + _dot(dsts[hb], qms[hb])
                accv_ref[p] = dv_p if masked else accv_ref[p] + dv_p
                acck_ref[p] = dk_p if masked else acck_ref[p] + dk_p
            for h in range(nh):
                dqt_ref[h * HEAD_DIM:(h + 1) * HEAD_DIM, pl.ds(q0, t)] += _dot(kts[h], dsts[h])

        tile(kj, True)

        def body(i, c):
            tile(i, False)
            return c
        lax.fori_loop(kj + 1, nq, body, 0)

        dc = jnp.zeros((t, LANES), F32)
        for h in range(nh):
            dc = jnp.where(lane == h, -jnp.sum(accd_ref[h], axis=1, keepdims=True), dc)
        dc_ref[...] = dc
        for p in range(npair):
            dv_ref[:, p * LANES:(p + 1) * LANES] = accv_ref[p].astype(BF16)
            dk_ref[:, p * LANES:(p + 1) * LANES] = acck_ref[p].astype(BF16)

        dq_ref[...] = dqt_ref[:, pl.ds(pl.multiple_of(kj * t, t), t)].T.astype(BF16)

    whole = lambda kj: (0, 0, 0)
    return pl.pallas_call(
        kern, name="fox_bwd",
        grid=(nq,),
        in_specs=[_resident((s, FOX_W), lambda kj: (0, 0)),
                  _resident((s, FOX_W), lambda kj: (0, 0)),
                  pl.BlockSpec((t, FOX_W), lambda kj: (kj, COL_FK // FOX_W)),
                  pl.BlockSpec((t, FOX_W), lambda kj: (kj, COL_FV // FOX_W)),
                  _resident((nh, 1, s), whole),
                  pl.BlockSpec((t, LANES), lambda kj: (kj, 0)),
                  _resident((nh, 1, s), whole),
                  _resident((nh, 1, s), whole)],
        out_specs=[pl.BlockSpec((t, FOX_W), lambda kj: (kj, 0)),
                   pl.BlockSpec((t, FOX_W), lambda kj: (kj, 0)),
                   pl.BlockSpec((t, FOX_W), lambda kj: (kj, 0)),
                   pl.BlockSpec((t, LANES), lambda kj: (kj, 0)),
                   _resident((nh, 1, s), whole)],
        out_shape=[jax.ShapeDtypeStruct((s, FOX_W), BF16),
                   jax.ShapeDtypeStruct((s, FOX_W), BF16),
                   jax.ShapeDtypeStruct((s, FOX_W), BF16),
                   jax.ShapeDtypeStruct((s, LANES), F32),
                   jax.ShapeDtypeStruct((nh, 1, s), F32)],
        scratch_shapes=[pltpu.VMEM((FOX_W, s), F32),
                        pltpu.VMEM((npair, t, LANES), F32),
                        pltpu.VMEM((npair, t, LANES), F32),
                        pltpu.VMEM((nh, t, LANES), F32)],
        compiler_params=_cparams(("arbitrary",)),
    )(qkv, do_bf, qkv, qkv, cum_t3, cum, lse_t3, delta_t3)


def _bucket_table():
    qi = np.arange(BLOCK)[:, None]
    kj = np.arange(2 * BLOCK)[None, :]
    rel = np.maximum(qi + BLOCK - kj, 0).astype(np.int32)
    max_exact = NUM_BUCKETS // 2
    relf = np.maximum(rel, 1).astype(np.float32)
    large = max_exact + (np.log(relf / np.float32(max_exact)) / np.float32(math.log(MAX_DISTANCE / max_exact))
                         * np.float32(NUM_BUCKETS - max_exact)).astype(np.int32)
    large = np.minimum(large, NUM_BUCKETS - 1)
    return np.where(rel < max_exact, rel, large).astype(np.int32)


SWA_LANES = SWA_GROUP * BLOCK


def _swa_bias(rel_bias, bucket_t):
    def kern(rb_ref, bk_ref, o_ref):
        bk = bk_ref[...]
        kj = lax.broadcasted_iota(jnp.int32, (2 * BLOCK, BLOCK), 0)
        qi = lax.broadcasted_iota(jnp.int32, (2 * BLOCK, BLOCK), 1)
        rel = qi + BLOCK - kj
        band = (rel >= 0) & (rel < BLOCK)
        masks = [band & (kj >= BLOCK), band]
        for h in range(SWA_HEADS):
            g, hh = divmod(h, SWA_GROUP)
            acc = jnp.zeros((2 * BLOCK, BLOCK), F32)
            for b in range(NUM_BUCKETS):
                acc = jnp.where(bk == b, rb_ref[b, h], acc)
            for first in range(2):
                o_ref[first, g, :, hh * BLOCK:(hh + 1) * BLOCK] = jnp.where(masks[first], acc, NEG)

    return pl.pallas_call(
        kern, name="swa_bias",
        in_specs=[pl.BlockSpec(memory_space=pltpu.SMEM),
                  pl.BlockSpec(memory_space=pltpu.VMEM)],
        out_specs=pl.BlockSpec(memory_space=pltpu.VMEM),
        out_shape=jax.ShapeDtypeStruct((2, SWA_KV_HEADS, 2 * BLOCK, SWA_LANES), F32),
        compiler_params=_cparams(),
    )(rel_bias, bucket_t)


SWA_STEP = 8


def _swa_keys(prev_ref, cur_ref):
    return jnp.concatenate([prev_ref[...], cur_ref[...]], axis=0)


def _swa_queries(x_ref, scale):
    x = x_ref[...]
    if scale:
        x = x * jnp.asarray(QK_SCALE, BF16)
    xt = x.astype(F32).T.astype(BF16)
    return [_group_rows(xt[:, b * BLOCK:(b + 1) * BLOCK]) for b in range(SWA_STEP)]


def _group_rows(xt):
    zeros = jnp.zeros((HEAD_DIM, SWA_LANES), BF16)
    out = []
    for g in range(SWA_KV_HEADS):
        heads = [xt[(SWA_GROUP * g + hh) * HEAD_DIM:(SWA_GROUP * g + hh + 1) * HEAD_DIM, :] for hh in range(SWA_GROUP)]
        rows = jnp.concatenate(heads, axis=1)
        padded = jnp.concatenate([rows, zeros] if g == 0 else [zeros, rows], axis=0)
        out.append((rows, padded))
    return out


def _pairs_to_rows(cols_t):
    out = []
    for p in range(SWA_HEADS // 2):
        g, hh = divmod(2 * p, SWA_GROUP)
        pair = jnp.concatenate([cols_t[g][:, hh * BLOCK:(hh + 1) * BLOCK],
                                cols_t[g][:, (hh + 1) * BLOCK:(hh + 2) * BLOCK]], axis=0)
        out.append(pair.T)
    return jnp.concatenate(out, axis=1)


def _swa_fwd(qkv, bias_t, sink_rows):
    s = qkv.shape[0]
    nb = s // BLOCK
    rows = SWA_STEP * BLOCK
    units = [(b, g) for b in range(SWA_STEP) for g in range(SWA_KV_HEADS)]

    def kern(q_ref, kp_ref, kc_ref, vp_ref, vc_ref, bias_ref, sink_ref, o_ref, lse_ref):
        n = pl.program_id(0)
        tables = [jnp.minimum(n, 1)] + [1] * (SWA_STEP - 1)
        k3 = _swa_keys(kp_ref, kc_ref)
        vt3 = _swa_keys(vp_ref, vc_ref).astype(F32).T.astype(BF16)
        qts = _swa_queries(q_ref, True)
        us = [_dot(k3[b * BLOCK:(b + 2) * BLOCK], qts[b][g][1]) + bias_ref[tables[b], g] for b, g in units]
        outs = []
        for (b, g), u in zip(units, us):
            sk = sink_ref[g]
            m = jnp.maximum(jnp.max(u, axis=0, keepdims=True), sk)
            p = jnp.exp(u - m)
            l = jnp.sum(p, axis=0, keepdims=True) + jnp.exp(sk - m)
            lse_ref[b, g] = m + jnp.log(l)
            vt = vt3[g * HEAD_DIM:(g + 1) * HEAD_DIM, b * BLOCK:(b + 2) * BLOCK]
            outs.append(_dot(vt, (p * (1.0 / l)).astype(BF16)))
        for b in range(SWA_STEP):
            o_ref[b * BLOCK:(b + 1) * BLOCK, :] = _pairs_to_rows(outs[b * SWA_KV_HEADS:(b + 1) * SWA_KV_HEADS])

    cq, ck, cv = COL_SQ // SWA_W, COL_SK // LANES, COL_SV // LANES
    prev = lambda n: jnp.maximum(SWA_STEP * n - 1, 0)
    return pl.pallas_call(
        kern, name="swa_fwd",
        grid=(nb // SWA_STEP,),
        in_specs=[pl.BlockSpec((rows, SWA_W), lambda n: (n, cq)),
                  pl.BlockSpec((BLOCK, LANES), lambda n: (prev(n), ck)),
                  pl.BlockSpec((rows, LANES), lambda n: (n, ck)),
                  pl.BlockSpec((BLOCK, LANES), lambda n: (prev(n), cv)),
                  pl.BlockSpec((rows, LANES), lambda n: (n, cv)),
                  _resident((2, SWA_KV_HEADS, 2 * BLOCK, SWA_LANES), lambda n: (0, 0, 0, 0)),
                  _resident((SWA_KV_HEADS, 1, SWA_LANES), lambda n: (0, 0, 0))],
        out_specs=[pl.BlockSpec((rows, SWA_W), lambda n: (n, 0)),
                   pl.BlockSpec((SWA_STEP, SWA_KV_HEADS, 1, SWA_LANES), lambda n: (n, 0, 0, 0))],
        out_shape=[jax.ShapeDtypeStruct((s, SWA_W), F32),
                   jax.ShapeDtypeStruct((nb, SWA_KV_HEADS, 1, SWA_LANES), F32)],
        compiler_params=_cparams(("parallel",)),
    )(qkv, qkv, qkv, qkv, qkv, bias_t, sink_rows)


def _swa_bwd(qkv, do_bf, delta_rows, lse, bias_t, sink_rows, bucket_t):
    s = qkv.shape[0]
    nb = s // BLOCK
    steps = nb // SWA_STEP
    rows = SWA_STEP * BLOCK
    units = [(b, g) for b in range(SWA_STEP) for g in range(SWA_KV_HEADS)]

    def kern(q_ref, kp_ref, kc_ref, vp_ref, vc_ref, do_ref, dl_ref, lse_ref, bias_ref, sink_ref, bk_ref,
             dq_ref, dk_ref, dv_ref, grb_ref, gsk_ref, dbias_ref, ck_ref, cv_ref, sk_ref):
        n = pl.program_id(0)

        @pl.when(n == 0)
        def _():
            dbias_ref[...] = jnp.zeros_like(dbias_ref)
            ck_ref[...] = jnp.zeros_like(ck_ref)
            cv_ref[...] = jnp.zeros_like(cv_ref)
            sk_ref[...] = jnp.zeros_like(sk_ref)

        @pl.when(n < steps)
        def _():
            tables = [jnp.minimum(n, 1)] + [1] * (SWA_STEP - 1)
            k3 = _swa_keys(kp_ref, kc_ref)
            v3 = _swa_keys(vp_ref, vc_ref)
            kt3 = (k3.astype(F32).T * QK_SCALE).astype(BF16)
            qts = _swa_queries(q_ref, True)
            dots = _swa_queries(do_ref, False)
            sts = [_dot(k3[b * BLOCK:(b + 2) * BLOCK], qts[b][g][1]) for b, g in units]
            dps = [_dot(v3[b * BLOCK:(b + 2) * BLOCK], dots[b][g][1]) for b, g in units]
            ps, dss = [], []
            for i, (b, g) in enumerate(units):
                lse_g = lse_ref[b, g]
                dlt = dl_ref[b, g]
                p = jnp.exp(sts[i] + bias_ref[tables[b], g] - lse_g)
                ds = p * (dps[i] - dlt)
                dbias_ref[g] += ds
                sk_ref[g] += -jnp.exp(sink_ref[g] - lse_g) * dlt
                ps.append(p.astype(BF16))
                dss.append(ds.astype(BF16))
            dk2, dv2 = [], []
            for b in range(SWA_STEP):
                at = lambda g: b * SWA_KV_HEADS + g
                groups = range(SWA_KV_HEADS)
                dv2.append(jnp.concatenate([_dot_nt(dots[b][g][0], ps[at(g)]) for g in groups], axis=0).T)
                dk2.append(jnp.concatenate([_dot_nt(qts[b][g][0], dss[at(g)]) for g in groups], axis=0).T)
                dqts = [_dot(kt3[g * HEAD_DIM:(g + 1) * HEAD_DIM, b * BLOCK:(b + 2) * BLOCK], dss[at(g)]) for g in groups]
                dq_ref[b * BLOCK:(b + 1) * BLOCK, :] = _pairs_to_rows(dqts).astype(BF16)
            last = (SWA_STEP - 1) * BLOCK
            for acc_ref, out_ref, parts in ((ck_ref, dk_ref, dk2), (cv_ref, dv_ref, dv2)):
                done = acc_ref[last:] + parts[0][:BLOCK]
                out_ref[...] = jnp.concatenate([acc_ref[:last], done], axis=0).astype(BF16)
                for b in range(SWA_STEP - 1):
                    acc_ref[b * BLOCK:(b + 1) * BLOCK] = parts[b][BLOCK:] + parts[b + 1][:BLOCK]
                acc_ref[last:] = parts[SWA_STEP - 1][BLOCK:]

        @pl.when(n == steps)
        def _():
            dk_ref[...] = ck_ref[...].astype(BF16)
            dv_ref[...] = cv_ref[...].astype(BF16)
            bk = bk_ref[...]
            lane = lax.broadcasted_iota(jnp.int32, (8, LANES), 1)
            rowi = lax.broadcasted_iota(jnp.int32, (NUM_BUCKETS, LANES), 0)
            lanei = lax.broadcasted_iota(jnp.int32, (NUM_BUCKETS, LANES), 1)
            out = jnp.zeros((NUM_BUCKETS, LANES), F32)
            gsk = jnp.zeros((8, LANES), F32)
            for h in range(SWA_HEADS):
                g, hh = divmod(h, SWA_GROUP)
                cols = slice(hh * BLOCK, (hh + 1) * BLOCK)
                gsk = jnp.where(lane == h, jnp.sum(sk_ref[g][:, cols]), gsk)
                db = dbias_ref[g][:, cols]
                for b in range(NUM_BUCKETS):
                    val = jnp.sum(jnp.where(bk == b, db, 0.0))
                    out = jnp.where((rowi == b) & (lanei == h), val, out)
            grb_ref[...] = out
            gsk_ref[...] = gsk

    cq, ck, cv = COL_SQ // SWA_W, COL_SK // LANES, COL_SV // LANES
    cur = lambda n: jnp.minimum(n, steps - 1)
    prev = lambda n: jnp.maximum(SWA_STEP * cur(n) - 1, 0)
    kout = lambda n: jnp.maximum(n - 1, 0)
    stat = pl.BlockSpec((SWA_STEP, SWA_KV_HEADS, 1, SWA_LANES), lambda n: (cur(n), 0, 0, 0))
    return pl.pallas_call(
        kern, name="swa_bwd",
        grid=(steps + 1,),
        in_specs=[pl.BlockSpec((rows, SWA_W), lambda n: (cur(n), cq)),
                  pl.BlockSpec((BLOCK, LANES), lambda n: (prev(n), ck)),
                  pl.BlockSpec((rows, LANES), lambda n: (cur(n), ck)),
                  pl.BlockSpec((BLOCK, LANES), lambda n: (prev(n), cv)),
                  pl.BlockSpec((rows, LANES), lambda n: (cur(n), cv)),
                  pl.BlockSpec((rows, SWA_W), lambda n: (cur(n), 1)),
                  stat, stat,
                  _resident((2, SWA_KV_HEADS, 2 * BLOCK, SWA_LANES), lambda n: (0, 0, 0, 0)),
                  _resident((SWA_KV_HEADS, 1, SWA_LANES), lambda n: (0, 0, 0)),
                  _resident((2 * BLOCK, BLOCK), lambda n: (0, 0))],
        out_specs=[pl.BlockSpec((rows, SWA_W), lambda n: (cur(n), 0)),
                   pl.BlockSpec((rows, LANES), lambda n: (kout(n), 0)),
                   pl.BlockSpec((rows, LANES), lambda n: (kout(n), 0)),
                   pl.BlockSpec((NUM_BUCKETS, LANES), lambda n: (0, 0)),
                   pl.BlockSpec((8, LANES), lambda n: (0, 0))],
        out_shape=[jax.ShapeDtypeStruct((s, SWA_W), BF16),
                   jax.ShapeDtypeStruct((s, LANES), BF16),
                   jax.ShapeDtypeStruct((s, LANES), BF16),
                   jax.ShapeDtypeStruct((NUM_BUCKETS, LANES), F32),
                   jax.ShapeDtypeStruct((8, LANES), F32)],
        scratch_shapes=[pltpu.VMEM((SWA_KV_HEADS, 2 * BLOCK, SWA_LANES), F32),
                        pltpu.VMEM((rows, LANES), F32),
                        pltpu.VMEM((rows, LANES), F32),
                        pltpu.VMEM((SWA_KV_HEADS, 1, SWA_LANES), F32)],
        compiler_params=_cparams(("arbitrary",)),
    )(qkv, qkv, qkv, qkv, qkv, do_bf, delta_rows, lse, bias_t, sink_rows, bucket_t)


def _post(x, target, o_fox, o_swa, z, w_o, ln_g, ln_b):
    s = x.shape[0]
    tm = min(256, s)
    nt = s // tm

    def kern(x_ref, t_ref, of_ref, os_ref, z_ref, w_ref, g_ref, b_ref,
             loss_ref, dh_ref, gwo_ref, do_ref, dz_ref, dl_ref, gg_ref, gb_ref, lacc_ref):
        step = pl.program_id(0)

        @pl.when(step == 0)
        def _():
            lacc_ref[...] = jnp.zeros_like(lacc_ref)
            gg_ref[...] = jnp.zeros_like(gg_ref)
            gwo_ref[...] = jnp.zeros_like(gwo_ref)
            gb_ref[...] = jnp.zeros_like(gb_ref)

        o = jnp.concatenate([of_ref[...], os_ref[...]], axis=1)
        zz = z_ref[...]
        sig = 1.0 / (1.0 + jnp.exp(-zz))
        silu = zz * sig
        mixed32 = o * silu
        mixed = mixed32.astype(BF16)
        w = w_ref[...]
        h = ALPHA * x_ref[...] + _dot(mixed, w)
        mu = jnp.mean(h, axis=1, keepdims=True)
        hc = h - mu
        var = jnp.mean(hc * hc, axis=1, keepdims=True)
        rstd = lax.rsqrt(var + LN_EPS)
        xhat = hc * rstd
        g = g_ref[...]
        err = xhat * g + b_ref[...] - t_ref[...]
        lacc_ref[...] += jnp.broadcast_to(jnp.sum(err * err, axis=0, keepdims=True), lacc_ref.shape)
        dout = err * (1.0 / D_MODEL)
        gg_ref[...] += jnp.broadcast_to(jnp.sum(dout * xhat, axis=0, keepdims=True), gg_ref.shape)
        gb_ref[...] += jnp.broadcast_to(jnp.sum(dout, axis=0, keepdims=True), gb_ref.shape)
        dxh = dout * g
        m1 = jnp.mean(dxh, axis=1, keepdims=True)
        m2 = jnp.mean(dxh * xhat, axis=1, keepdims=True)
        dh = rstd * (dxh - m1 - xhat * m2)
        dh_ref[...] = dh
        dy = dh.astype(BF16)
        gwo_ref[...] += _dot(mixed32.T.astype(BF16), dy)
        dmix = _dot_nt(dy, w)
        do = dmix * silu
        do_ref[...] = do.astype(BF16)
        dz_ref[...] = (dmix * o * (sig * (1.0 + zz * (1.0 - sig)))).astype(BF16)
        r = lax.broadcasted_iota(jnp.int32, (D_MODEL, LANES), 0) // HEAD_DIM
        c = lax.broadcasted_iota(jnp.int32, (D_MODEL, LANES), 1)
        pick = jnp.where(r == c, 1.0, 0.0).astype(BF16)
        dl_ref[...] = _exact_dot(pick, do * o, False)

        @pl.when(step == nt - 1)
        def _():
            tot = jnp.sum(lacc_ref[0:1, :]) * (0.5 / D_MODEL)
            loss_ref[...] = jnp.broadcast_to(tot, loss_ref.shape)

    row = lambda i: (i, 0)
    fixed = lambda i: (0, 0)
    wide = pl.BlockSpec((tm, D_MODEL), row)
    half = pl.BlockSpec((tm, FOX_W), row)
    return pl.pallas_call(
        kern, name="post",
        grid=(nt,),
        in_specs=[wide, wide, half, half, wide,
                  pl.BlockSpec((D_MODEL, D_MODEL), fixed),
                  pl.BlockSpec((1, D_MODEL), fixed),
                  pl.BlockSpec((1, D_MODEL), fixed)],
        out_specs=[pl.BlockSpec((8, LANES), fixed), wide,
                   _resident((D_MODEL, D_MODEL), fixed), wide, wide,
                   pl.BlockSpec((tm, LANES), row),
                   pl.BlockSpec((8, D_MODEL), fixed), pl.BlockSpec((8, D_MODEL), fixed)],
        out_shape=[jax.ShapeDtypeStruct((8, LANES), F32),
                   jax.ShapeDtypeStruct((s, D_MODEL), F32),
                   jax.ShapeDtypeStruct((D_MODEL, D_MODEL), F32),
                   jax.ShapeDtypeStruct((s, D_MODEL), BF16),
                   jax.ShapeDtypeStruct((s, D_MODEL), BF16),
                   jax.ShapeDtypeStruct((s, LANES), F32),
                   jax.ShapeDtypeStruct((8, D_MODEL), F32),
                   jax.ShapeDtypeStruct((8, D_MODEL), F32)],
        scratch_shapes=[pltpu.VMEM((8, D_MODEL), F32)],
        compiler_params=_cparams(("arbitrary",)),
    )(x, target, o_fox, o_swa, z, w_o, ln_g, ln_b)


def _adamw_math(w, g, m, v):
    m = ADAM_B1 * m + (1.0 - ADAM_B1) * g
    v = ADAM_B2 * v + (1.0 - ADAM_B2) * (g * g)
    m_hat = m / (1.0 - ADAM_B1 ** ADAM_STEP)
    v_hat = v / (1.0 - ADAM_B2 ** ADAM_STEP)
    delta = -ADAM_LR * (m_hat / (jnp.sqrt(v_hat) + ADAM_EPS) + ADAM_WD * w)
    return delta, m, v


def _adamw(w, g, m, v, *, name):
    r, c = w.shape
    tr = min(256, r)

    def kern(w_ref, g_ref, m_ref, v_ref, d_ref, mo_ref, vo_ref):
        d, mn, vn = _adamw_math(w_ref[...], g_ref[...], m_ref[...], v_ref[...])
        d_ref[...] = d
        mo_ref[...] = mn
        vo_ref[...] = vn

    blk = pl.BlockSpec((tr, c), lambda i: (i, 0))
    sds = jax.ShapeDtypeStruct((r, c), F32)
    return pl.pallas_call(
        kern, name=name,
        grid=(r // tr,),
        in_specs=[blk, blk, blk, blk],
        out_specs=[blk, blk, blk],
        out_shape=[sds, sds, sds],
        compiler_params=_cparams(("parallel",)),
    )(w, g, m, v)


def _adamw_cols(w, g, m, v, *, name):
    c, _, r = w.shape
    tc = 139
    assert c % tc == 0

    def kern(w_ref, g_ref, m_ref, v_ref, go_ref, d_ref, mo_ref, vo_ref):
        g = g_ref[...]
        d, mn, vn = _adamw_math(w_ref[...], g, m_ref[...], v_ref[...])
        go_ref[...] = g
        d_ref[...] = d
        mo_ref[...] = mn
        vo_ref[...] = vn

    blk = pl.BlockSpec((tc, 1, r), lambda i: (i, 0, 0))
    sds = jax.ShapeDtypeStruct((c, 1, r), F32)
    return pl.pallas_call(
        kern, name=name,
        grid=(c // tc,),
        in_specs=[blk, blk, blk, blk],
        out_specs=[blk, blk, blk, blk],
        out_shape=[sds, sds, sds, sds],
        compiler_params=_cparams(("parallel",)),
    )(w, g, m, v)


def _position():
    x, y, c = lax.axis_index("x"), lax.axis_index("y"), lax.axis_index("c")
    chips = [(1 - x, y), (x, 1 - y), (1 - x, 1 - y)]
    return x, y, c, chips


def _chip_index(cx, cy):
    return 2 * cx + cy


def _gather_weights(*shards):
    n_arr = len(shards)

    def kern(*refs):
        ins, outs = refs[:n_arr], refs[n_arr:2 * n_arr]
        send_sems, recv_sems, local_sems = refs[2 * n_arr:]
        x, y, c, chips = _position()
        me = _chip_index(x, y)
        sibling = (x, y, 1 - c)

        local = [pltpu.make_async_copy(ins[a], outs[a].at[me], local_sems.at[a]) for a in range(n_arr)]
        for cp in local:
            cp.start()

        def half(ref, a):
            rows = shards[a].shape[0] // 2
            return ref.at[pl.ds(c * rows, rows), :]

        def copy(a, k, src, slot, to):
            return pltpu.make_async_remote_copy(
                src_ref=src, dst_ref=half(outs[a].at[slot], a),
                send_sem=send_sems.at[a * 6 + k], recv_sem=recv_sems.at[a * 6 + k],
                device_id=to, device_id_type=MESH)

        first = [copy(a, j, half(ins[a], a), me, (*chip, c)) for a in range(n_arr) for j, chip in enumerate(chips)]
        for cp in first:
            cp.start()
        passed = []
        for a in range(n_arr):
            for j, chip in enumerate(chips):
                slot = _chip_index(*chip)
                copy(a, j, half(ins[a], a), slot, (*chip, c)).wait_recv()
                fwd = copy(a, 3 + j, half(outs[a].at[slot], a), slot, sibling)
                fwd.start()
                passed.append(fwd)
        for a in range(n_arr):
            for j, chip in enumerate(chips):
                slot = _chip_index(*chip)
                rows = shards[a].shape[0] // 2
                dst = outs[a].at[slot].at[pl.ds((1 - c) * rows, rows), :]
                pltpu.make_async_remote_copy(
                    src_ref=dst, dst_ref=dst, send_sem=send_sems.at[a * 6 + 3 + j],
                    recv_sem=recv_sems.at[a * 6 + 3 + j], device_id=sibling, device_id_type=MESH).wait_recv()
        for cp in first + passed:
            cp.wait_send()
        for cp in local:
            cp.wait()

    vmem = pl.BlockSpec(memory_space=pltpu.VMEM)
    return pl.pallas_call(
        kern, name="gather_weights",
        in_specs=[vmem] * n_arr,
        out_specs=[vmem] * n_arr,
        out_shape=[jax.ShapeDtypeStruct((N_CHIPS,) + w.shape, w.dtype) for w in shards],
        scratch_shapes=[pltpu.SemaphoreType.DMA((6 * n_arr,)),
                        pltpu.SemaphoreType.DMA((6 * n_arr,)),
                        pltpu.SemaphoreType.DMA((n_arr,))],
        compiler_params=_cparams(),
    )(*shards)


def _pair_reduce(grads):
    n_arr = len(grads)
    chunk = 128

    def kern(*refs):
        ins = refs[:n_arr]
        outs = refs[n_arr:2 * n_arr]
        gots = refs[2 * n_arr:3 * n_arr]
        send_sems, recv_sems = refs[3 * n_arr:]
        x, y, c, _ = _position()
        sibling = (x, y, 1 - c)
        copies = []
        for a in range(n_arr):
            rows = grads[a].shape[1] // 2
            copies.append(pltpu.make_async_remote_copy(
                src_ref=ins[a].at[:, pl.ds((1 - c) * rows, rows), :], dst_ref=gots[a],
                send_sem=send_sems.at[a], recv_sem=recv_sems.at[a], device_id=sibling, device_id_type=MESH))
        for cp in copies:
            cp.start()
        for a in range(n_arr):
            copies[a].wait()
            rows = grads[a].shape[1] // 2
            for j in range(N_CHIPS):
                for r0 in range(0, rows, chunk):
                    mine = ins[a][j, pl.ds(pl.multiple_of(c * rows + r0, chunk), chunk), :]
                    outs[a][j, r0:r0 + chunk, :] = (mine + gots[a][j, r0:r0 + chunk, :]).astype(BF16)

    vmem = pl.BlockSpec(memory_space=pltpu.VMEM)
    half = [(N_CHIPS, g.shape[1] // 2, g.shape[2]) for g in grads]
    return pl.pallas_call(
        kern, name="pair_reduce",
        in_specs=[vmem] * n_arr,
        out_specs=[vmem] * n_arr,
        out_shape=[jax.ShapeDtypeStruct(h, BF16) for h in half],
        scratch_shapes=[pltpu.VMEM(h, F32) for h in half]
        + [pltpu.SemaphoreType.DMA((n_arr,)), pltpu.SemaphoreType.DMA((n_arr,))],
        compiler_params=_cparams(),
    )(*grads)


def _wo_gather_start(shard):
    hbm = pl.BlockSpec(memory_space=pltpu.HBM)
    sem = pl.BlockSpec(memory_space=pltpu.SEMAPHORE)
    land_shape = (N_CHIPS,) + shard.shape

    def kern(src_ref, land_ref, send_sems, recv_sems, src_thru, land_thru, token):
        x, y, c, chips = _position()
        me = _chip_index(x, y)
        for j, chip in enumerate(chips):
            pltpu.make_async_remote_copy(
                src_ref=src_ref, dst_ref=land_ref.at[me], send_sem=send_sems.at[j], recv_sem=recv_sems.at[j],
                device_id=(*chip, c), device_id_type=MESH).start()
        token[...] = jnp.zeros_like(token)

    return pl.pallas_call(
        kern, name="wo_gather_start",
        in_specs=[hbm, hbm],
        out_specs=(sem, sem, hbm, hbm, pl.BlockSpec(memory_space=pltpu.VMEM)),
        out_shape=(pltpu.SemaphoreType.DMA((3,)), pltpu.SemaphoreType.DMA((3,)),
                   pltpu.HBM(shard.shape, shard.dtype), pltpu.HBM(land_shape, shard.dtype),
                   jax.ShapeDtypeStruct((8, LANES), F32)),
        input_output_aliases={0: 2, 1: 3},
        compiler_params=pltpu.CompilerParams(has_side_effects=pltpu.SideEffectType.DATAFLOW_SIDE_EFFECTING),
    )(pltpu.with_memory_space_constraint(shard, pltpu.HBM),
      pltpu.with_memory_space_constraint(lax.empty(land_shape, shard.dtype), pltpu.HBM))


def _wo_gather_wait(send_sems, recv_sems, src_thru, land_thru, after):
    hbm = pl.BlockSpec(memory_space=pltpu.HBM)
    sem = pl.BlockSpec(memory_space=pltpu.SEMAPHORE)

    def kern(src_ref, land_ref, send_sems, recv_sems, after_ref, src_out, land_out):
        x, y, c, chips = _position()
        for j, chip in enumerate(chips):
            copy = pltpu.make_async_remote_copy(
                src_ref=src_ref, dst_ref=land_ref.at[_chip_index(*chip)], send_sem=send_sems.at[j],
                recv_sem=recv_sems.at[j], device_id=(*chip, c), device_id_type=MESH)
            copy.wait_send()
            copy.wait_recv()

    return pl.pallas_call(
        kern, name="wo_gather_wait",
        in_specs=[hbm, hbm, sem, sem, pl.BlockSpec(memory_space=pl.ANY)],
        out_specs=[hbm, hbm],
        out_shape=[pltpu.HBM(src_thru.shape, src_thru.dtype), pltpu.HBM(land_thru.shape, land_thru.dtype)],
        input_output_aliases={0: 0, 1: 1},
        compiler_params=pltpu.CompilerParams(has_side_effects=pltpu.SideEffectType.DATAFLOW_SIDE_EFFECTING),
    )(src_thru, land_thru, send_sems, recv_sems, after)[1]


def _scatter_start(parts):
    n_arr = len(parts)
    hbm = pl.BlockSpec(memory_space=pltpu.HBM)
    sem = pl.BlockSpec(memory_space=pltpu.SEMAPHORE)

    def kern(*refs):
        ins, lands = refs[:n_arr], refs[n_arr:2 * n_arr]
        send_sems, recv_sems, token = refs[2 * n_arr], refs[2 * n_arr + 1], refs[-1]
        x, y, c, chips = _position()
        me = _chip_index(x, y)
        for a in range(n_arr):
            for j, chip in enumerate(chips):
                pltpu.make_async_remote_copy(
                    src_ref=ins[a].at[_chip_index(*chip)], dst_ref=lands[a].at[me],
                    send_sem=send_sems.at[a * 3 + j], recv_sem=recv_sems.at[a * 3 + j],
                    device_id=(*chip, c), device_id_type=MESH).start()
        token[...] = jnp.zeros_like(token)

    slab = [pltpu.HBM(p.shape, p.dtype) for p in parts]
    outs = pl.pallas_call(
        kern, name="scatter_start",
        in_specs=[hbm] * (2 * n_arr),
        out_specs=(sem, sem, *[hbm] * (2 * n_arr), pl.BlockSpec(memory_space=pltpu.VMEM)),
        out_shape=(pltpu.SemaphoreType.DMA((3 * n_arr,)), pltpu.SemaphoreType.DMA((3 * n_arr,)),
                   *slab, *slab, jax.ShapeDtypeStruct((8, LANES), F32)),
        input_output_aliases={i: 2 + i for i in range(2 * n_arr)},
        compiler_params=pltpu.CompilerParams(has_side_effects=pltpu.SideEffectType.DATAFLOW_SIDE_EFFECTING),
    )(*[pltpu.with_memory_space_constraint(p, pltpu.HBM) for p in parts],
      *[pltpu.with_memory_space_constraint(lax.empty(p.shape, p.dtype), pltpu.HBM) for p in parts])
    return outs[0], outs[1], outs[2:2 + n_arr], outs[2 + n_arr:2 + 2 * n_arr], outs[-1]


def _scatter_wait(send_sems, recv_sems, parts_thru, lands_thru, after):
    n_arr = len(parts_thru)
    hbm = pl.BlockSpec(memory_space=pltpu.HBM)
    sem = pl.BlockSpec(memory_space=pltpu.SEMAPHORE)

    def kern(*refs):
        ins, lands = refs[:n_arr], refs[n_arr:2 * n_arr]
        send_ref, recv_ref = refs[2 * n_arr], refs[2 * n_arr + 1]
        x, y, c, chips = _position()
        for a in range(n_arr):
            for j, chip in enumerate(chips):
                slot = _chip_index(*chip)
                copy = pltpu.make_async_remote_copy(
                    src_ref=ins[a].at[slot], dst_ref=lands[a].at[slot],
                    send_sem=send_ref.at[a * 3 + j], recv_sem=recv_ref.at[a * 3 + j],
                    device_id=(*chip, c), device_id_type=MESH)
                copy.wait_send()
                copy.wait_recv()

    slab = [pltpu.HBM(p.shape, p.dtype) for p in parts_thru]
    outs = pl.pallas_call(
        kern, name="scatter_wait",
        in_specs=[hbm] * (2 * n_arr) + [sem, sem, pl.BlockSpec(memory_space=pl.ANY)],
        out_specs=[hbm] * (2 * n_arr),
        out_shape=slab + slab,
        input_output_aliases={i: i for i in range(2 * n_arr)},
        compiler_params=pltpu.CompilerParams(has_side_effects=pltpu.SideEffectType.DATAFLOW_SIDE_EFFECTING),
    )(*parts_thru, *lands_thru, send_sems, recv_sems, after)
    return outs[:n_arr], outs[n_arr:]


def _chip_sum_swap(parts, lands):
    n_arr = len(parts)
    chunk = 128

    def kern(*refs):
        own, got = refs[:n_arr], refs[n_arr:2 * n_arr]
        outs = refs[2 * n_arr:3 * n_arr]
        sums = refs[3 * n_arr:4 * n_arr]
        swap_send, swap_recv, swap_local = refs[4 * n_arr:]
        x, y, c, _ = _position()
        me = _chip_index(x, y)
        sibling = (x, y, 1 - c)
        for a in range(n_arr):
            for r0 in range(0, parts[a].shape[1], chunk):
                mine = own[a][me, r0:r0 + chunk, :].astype(F32)

                def term(i):
                    other = got[a][jnp.where(i == me, (i + 1) % N_CHIPS, i), r0:r0 + chunk, :].astype(F32)
                    return jnp.where(i == me, mine, other)
                sums[a][r0:r0 + chunk, :] = ((term(0) + term(1)) + term(2)) + term(3)
        swap_l, swap_r = [], []
        for a in range(n_arr):
            rows = parts[a].shape[1]
            mine = outs[a].at[pl.ds(c * rows, rows), :]
            swap_l.append(pltpu.make_async_copy(sums[a], mine, swap_local.at[a]))
            swap_r.append(pltpu.make_async_remote_copy(
                src_ref=sums[a], dst_ref=mine, send_sem=swap_send.at[a], recv_sem=swap_recv.at[a],
                device_id=sibling, device_id_type=MESH))
        for cp in swap_l + swap_r:
            cp.start()
        for a in range(n_arr):
            rows = parts[a].shape[1]
            theirs = outs[a].at[pl.ds((1 - c) * rows, rows), :]
            pltpu.make_async_remote_copy(
                src_ref=theirs, dst_ref=theirs, send_sem=swap_send.at[a], recv_sem=swap_recv.at[a],
                device_id=sibling, device_id_type=MESH).wait_recv()
        for cp in swap_r:
            cp.wait_send()
        for cp in swap_l:
            cp.wait()

    vmem = pl.BlockSpec(memory_space=pltpu.VMEM)
    return pl.pallas_call(
        kern, name="chip_sum_swap",
        in_specs=[vmem] * (2 * n_arr),
        out_specs=[vmem] * n_arr,
        out_shape=[jax.ShapeDtypeStruct((2 * p.shape[1], p.shape[2]), F32) for p in parts],
        scratch_shapes=[pltpu.VMEM(p.shape[1:], F32) for p in parts]
        + [pltpu.SemaphoreType.DMA((n_arr,)),
           pltpu.SemaphoreType.DMA((n_arr,)),
           pltpu.SemaphoreType.DMA((n_arr,))],
        compiler_params=_cparams(),
    )(*parts, *lands)


def _small_allreduce_adamw(partials, params, moms, vels):
    chunks = D_MODEL // LANES
    row_rb, row_bf, row_sk, row_loss = 2 * chunks, 2 * chunks + NUM_BUCKETS, 2 * chunks + NUM_BUCKETS + 1, SMALL_ROWS - 6

    def kern(gbf_ref, grb_ref, gsk_ref, gg_ref, gb_ref, loss_ref, *refs):
        p_refs, m_refs, v_refs = refs[0:5], refs[5:10], refs[10:15]
        lo_ref, g_outs, d_outs, mo_outs, vo_outs = refs[15], refs[16:21], refs[21:26], refs[26:31], refs[31:36]
        send_ref, buf_ref, send_sems, recv_sems = refs[36:]
        x, y, c, _ = _position()
        me = 4 * x + 2 * y + c
        send_ref[...] = jnp.zeros_like(send_ref)
        for r in range(chunks):
            send_ref[r:r + 1, :] = gg_ref[0:1, r * LANES:(r + 1) * LANES]
            send_ref[chunks + r:chunks + r + 1, :] = gb_ref[0:1, r * LANES:(r + 1) * LANES]
        send_ref[row_rb:row_rb + NUM_BUCKETS, :] = grb_ref[...]
        send_ref[row_bf:row_bf + 1, :] = gbf_ref[0:1, :]
        send_ref[row_sk:row_sk + 1, :] = gsk_ref[0:1, :]
        send_ref[row_loss:row_loss + 1, :] = loss_ref[0:1, :]
        buf_ref[me] = send_ref[...]
        peers = [(x, y, 1 - c)] + [(px, py, pc) for px, py in _position()[3] for pc in (c, 1 - c)]
        sends = []
        for k, peer in enumerate(peers):
            sends.append(pltpu.make_async_remote_copy(
                src_ref=send_ref, dst_ref=buf_ref.at[me], send_sem=send_sems.at[k], recv_sem=recv_sems.at[k],
                device_id=peer, device_id_type=MESH))
        for cp in sends:
            cp.start()
        for k, (px, py, pc) in enumerate(peers):
            slot = buf_ref.at[4 * px + 2 * py + pc]
            pltpu.make_async_remote_copy(
                src_ref=slot, dst_ref=slot, send_sem=send_sems.at[k], recv_sem=recv_sems.at[k],
                device_id=(px, py, pc), device_id_type=MESH).wait_recv()
        for cp in sends:
            cp.wait_send()
        tot = buf_ref[0]
        for d in range(1, N_DEV):
            tot = tot + buf_ref[d]
        lo_ref[...] = tot[row_loss:row_loss + 1, :]
        grads = [tot[row_bf:row_bf + 1, 0:FOX_HEADS],
                 tot[row_rb:row_rb + NUM_BUCKETS, 0:SWA_HEADS],
                 tot[row_sk:row_sk + 1, 0:SWA_HEADS],
                 jnp.concatenate([tot[r:r + 1, :] for r in range(chunks)], axis=1),
                 jnp.concatenate([tot[chunks + r:chunks + r + 1, :] for r in range(chunks)], axis=1)]
        for i, g in enumerate(grads):
            g_outs[i][...] = g
            delta, mn, vn = _adamw_math(p_refs[i][...], g, m_refs[i][...], v_refs[i][...])
            d_outs[i][...] = delta
            mo_outs[i][...] = mn
            vo_outs[i][...] = vn

    vm = pl.BlockSpec(memory_space=pltpu.VMEM)
    shapes = [jax.ShapeDtypeStruct(p.shape, F32) for p in params]
    outs = pl.pallas_call(
        kern, name="small_allreduce_adamw",
        in_specs=[vm] * 21,
        out_specs=[vm] * 21,
        out_shape=[jax.ShapeDtypeStruct((1, LANES), F32)] + shapes * 4,
        scratch_shapes=[pltpu.VMEM((SMALL_ROWS, LANES), F32),
                        pltpu.VMEM((N_DEV, SMALL_ROWS, LANES), F32),
                        pltpu.SemaphoreType.DMA((N_DEV - 1,)),
                        pltpu.SemaphoreType.DMA((N_DEV - 1,))],
    )(*partials, *params, *moms, *vels)
    return outs[0], outs[1:6], outs[6:11], outs[11:16], outs[16:21]


def _to_padded_cols(w):
    pad = jnp.zeros((w.shape[0], N_C - FOX_HEADS), w.dtype)
    return jnp.concatenate([w[:, 0:1536], w[:, 2056:2824], w[:, 1536:1544], pad,
                            w[:, 1544:2056], w[:, 2824:3336]], axis=1)


_COLUMN_RUNS = ((0, 0, 1536), (1536, OFF_C, FOX_HEADS), (1544, OFF_B, FOX_W), (2056, 1536, N_A - 1536),
                (2824, OFF_B + FOX_W, SWA_W))


def _shards_from_padded(g):
    shard_cols = D_IN // N_CHIPS
    shards = []
    for j in range(N_CHIPS):
        lo, hi = j * shard_cols, (j + 1) * shard_cols
        pieces = []
        for ref0, pad0, width in _COLUMN_RUNS:
            a, b = max(lo, ref0), min(hi, ref0 + width)
            if a < b:
                pieces.append(g[:, pad0 + a - ref0:pad0 + b - ref0])
        pieces.append(jnp.zeros((g.shape[0], SHARD_PAD - shard_cols), g.dtype))
        shards.append(jnp.concatenate(pieces, axis=1))
    return jnp.stack(shards)


def _fox_rows(a):
    return a[:, :FOX_HEADS].T.reshape(FOX_HEADS, 1, a.shape[0])


def kernel(x, w_in, b_f, rel_bias, sink, w_o, ln_g, ln_b, loss_target, m_w_in, m_b_f, m_rel_bias, m_sink, m_w_o, m_ln_g, m_ln_b, v_w_in, v_b_f, v_rel_bias, v_sink, v_w_o, v_ln_g, v_ln_b):
    x2 = x[0]
    tgt = loss_target[0]
    s = x2.shape[0]
    w_in2, w_o2 = w_in[0], w_o[0]

    shard_cols = D_IN // N_CHIPS
    col_pad = ((0, 0), (0, SHARD_PAD - shard_cols))
    (w_in_all,) = _gather_weights(jnp.pad(w_in2.astype(BF16), col_pad))
    w_full = jnp.concatenate([w_in_all[j, :, :shard_cols] for j in range(N_CHIPS)], axis=1)
    w_pad = _to_padded_cols(w_full)
    w_o_bf, _ = lax.optimization_barrier((w_o2.astype(BF16), w_in_all))
    wo_send, wo_recv, wo_src, wo_land, wo_token = _wo_gather_start(w_o_bf)

    bfp = jnp.pad(b_f, ((0, 0), (0, LANES - FOX_HEADS)))
    qkv, ffp, z, xt, vt, cum = _project(x2, w_pad, bfp, wo_token)
    cum_t3 = _fox_rows(cum)
    o_fox, lse_t3 = _fox_fwd(qkv, vt, cum_t3, cum)
    bucket_t = jnp.asarray(_bucket_table().T)
    bias_t = _swa_bias(rel_bias, bucket_t)
    sink_rows = jnp.repeat(sink.reshape(SWA_KV_HEADS, SWA_GROUP, 1), BLOCK, axis=2).reshape(SWA_KV_HEADS, 1, SWA_LANES)
    o_swa, lse_swa = _swa_fwd(qkv, bias_t, sink_rows)

    wo_land = _wo_gather_wait(wo_send, wo_recv, wo_src, wo_land, o_swa)
    my_chip = _chip_index(lax.axis_index("x"), lax.axis_index("y"))
    w_o_full = lax.dynamic_update_slice(wo_land, w_o_bf[None], (my_chip, 0, 0)).reshape(D_MODEL, D_MODEL)
    loss8, dh, grad_w_o_full, do_bf, dz, delta, gg8, gb8 = _post(
        x2, tgt, o_fox, o_swa, z, w_o_full, ln_g, ln_b)

    delta_t3 = _fox_rows(delta)
    dq_fox, dk_fox, dv_fox, dcum_k, dcum_q = _fox_bwd(qkv, do_bf, cum_t3, cum, lse_t3, delta_t3)
    dcum_q = jnp.pad(dcum_q.reshape(FOX_HEADS, s).T, ((0, 0), (0, LANES - FOX_HEADS)))
    dff, gbf8 = _cum_bwd(dcum_k, dcum_q, ffp, bfp)
    delta_rows = (delta[:, FOX_HEADS:FOX_HEADS + SWA_HEADS].reshape(s // BLOCK, BLOCK, SWA_KV_HEADS, SWA_GROUP)
                  .transpose(0, 2, 3, 1).reshape(s // BLOCK, SWA_KV_HEADS, 1, SWA_LANES))
    dq_swa, dk_swa, dv_swa, grb, gsk8 = _swa_bwd(qkv, do_bf, delta_rows, lse_swa, bias_t, sink_rows, bucket_t)

    d_misc = jnp.concatenate([dk_swa, dv_swa, dff], axis=1)
    pieces = [dq_fox, dk_fox, dv_fox, dq_swa, d_misc, dz]
    blocks = [(p, 0) for p in pieces[:-1]] + [(dz, 0), (dz, 1)]
    grad_w_pad = _grad_w_matmul(xt, blocks, tk=1024, name="grad_w_in")

    g_in4 = _shards_from_padded(grad_w_pad)
    g_o4 = grad_w_o_full.reshape(N_CHIPS, D_MODEL // N_CHIPS, D_MODEL)
    parts = _pair_reduce([g_in4, g_o4])
    send_sems, recv_sems, parts_thru, lands_thru, token = _scatter_start(parts)
    grad_x = _grad_x_matmul(pieces, w_pad, dh, token, tm=512, tn=D_MODEL, name="grad_x")
    parts, lands = _scatter_wait(send_sems, recv_sems, parts_thru, lands_thru, grad_x)
    g_w_in, g_w_o = _chip_sum_swap(parts, lands)
    g_w_in = g_w_in[:, :shard_cols]

    cols_first = lambda a: jnp.transpose(a, (2, 0, 1))
    rows_first = lambda a: jnp.transpose(a, (1, 2, 0))
    g_w_in, d_w_in, nm_w_in, nv_w_in = [rows_first(a) for a in _adamw_cols(
        cols_first(w_in), cols_first(g_w_in[None]), cols_first(m_w_in), cols_first(v_w_in), name="adamw_w_in")]
    d_w_o, nm_w_o, nv_w_o = _adamw(w_o2, g_w_o, m_w_o[0], v_w_o[0], name="adamw_w_o")

    loss_row, gs, ds, ms, vs = _small_allreduce_adamw(
        [gbf8, grb, gsk8, gg8, gb8, loss8],
        [b_f, rel_bias, sink, ln_g, ln_b],
        [m_b_f, m_rel_bias, m_sink, m_ln_g, m_ln_b],
        [v_b_f, v_rel_bias, v_sink, v_ln_g, v_ln_b])
    loss = loss_row[0, 0]
    g_bf, g_rb, g_sk, g_lg, g_lb = gs
    d_bf, d_rb, d_sk, d_lg, d_lb = ds
    m_bf, m_rb, m_sk, m_lg, m_lb = ms
    v_bf, v_rb, v_sk, v_lg, v_lb = vs

    e = lambda a: a[None]
    return (loss, e(grad_x),
            g_w_in, g_bf, g_rb, g_sk, e(g_w_o), g_lg, g_lb,
            d_w_in, d_bf, d_rb, d_sk, e(d_w_o), d_lg, d_lb,
            nm_w_in, m_bf, m_rb, m_sk, e(nm_w_o), m_lg, m_lb,
            nv_w_in, v_bf, v_rb, v_sk, e(nv_w_o), v_lg, v_lb)
```
